```python
import jax, jax.numpy as jnp
from jax import lax
import numpy as np

D_MODEL = 1024
BATCH = 8
SEQ = 4096
DEPTH = 1

HEAD_DIM = 64
A_HEADS = D_MODEL // (2 * HEAD_DIM)
A_KV_HEADS = max(1, A_HEADS // 4)
A_WINDOW = 128
B_HEADS = D_MODEL // (2 * HEAD_DIM)
B_PATTERNS = ((128, 1), (512, 4), (2048, 16))
BLOCK = 128
ROPE_THETA = 10000.0
EPS = 1e-6
NEG = -1e30

A_WIDTH = A_HEADS * HEAD_DIM
A_KV_WIDTH = A_KV_HEADS * HEAD_DIM
B_WIDTH = B_HEADS * HEAD_DIM
MIX_WIDTH = A_WIDTH + B_WIDTH
IN_SPLITS = (A_WIDTH, A_KV_WIDTH, A_KV_WIDTH, A_WIDTH, B_WIDTH, B_WIDTH, B_WIDTH, B_WIDTH)
IN_WIDTH = sum(IN_SPLITS)

kernel_name = "hybrid_swa_sink_dilated_gated"


def rmsnorm(t, gain):
    tf = t.astype(jnp.float32)
    tf = tf * lax.rsqrt(jnp.mean(tf * tf, axis=-1, keepdims=True) + EPS)
    return (tf * gain.astype(jnp.float32)).astype(t.dtype)


def rope(t, pos):
    half = HEAD_DIM // 2
    inv = ROPE_THETA ** (-jnp.arange(half, dtype=jnp.float32) / half)
    ang = pos.astype(jnp.float32)[:, None] * inv[None, :]
    cos = jnp.cos(ang)[:, None, :]
    sin = jnp.sin(ang)[:, None, :]
    tf = t.astype(jnp.float32)
    t1, t2 = tf[..., :half], tf[..., half:]
    return jnp.concatenate([t1 * cos - t2 * sin, t2 * cos + t1 * sin], axis=-1).astype(t.dtype)


def banded_attention(q, k, v, max_dist, sinks=None):
    n, L, h, d = q.shape
    hkv = k.shape[2]
    g = h // hkv
    nb = -(-L // BLOCK)
    lp = nb * BLOCK
    pad = lp - L
    q = jnp.pad(q, ((0, 0), (0, pad), (0, 0), (0, 0)))
    k = jnp.pad(k, ((0, 0), (BLOCK, pad), (0, 0), (0, 0)))
    v = jnp.pad(v, ((0, 0), (BLOCK, pad), (0, 0), (0, 0)))
    qb = q.reshape(n, nb, BLOCK, hkv, g, d)
    kb = k.reshape(n, nb + 1, BLOCK, hkv, d)
    vb = v.reshape(n, nb + 1, BLOCK, hkv, d)
    kw = jnp.concatenate([kb[:, :-1], kb[:, 1:]], axis=2)
    vw = jnp.concatenate([vb[:, :-1], vb[:, 1:]], axis=2)
    s = jnp.einsum("nbqkgd,nbskd->nbkgqs", qb, kw,
                   preferred_element_type=jnp.float32) * (d ** -0.5)
    qi = jnp.arange(BLOCK)[:, None]
    sj = jnp.arange(2 * BLOCK)[None, :]
    dist = qi - sj + BLOCK
    key_pos = jnp.arange(nb)[:, None] * BLOCK - BLOCK + sj
    valid = ((dist >= 0) & (dist <= max_dist))[None] & (key_pos >= 0)[:, None, :]
    s = jnp.where(valid[None, :, None, None], s, NEG)
    m = s.max(axis=-1)
    if sinks is not None:
        sk = sinks.astype(jnp.float32).reshape(hkv, g)[None, None, :, :, None]
        m = jnp.maximum(m, sk)
    p = jnp.exp(s - m[..., None])
    l = p.sum(axis=-1)
    if sinks is not None:
        l = l + jnp.exp(sk - m)
    o = jnp.einsum("nbkgqs,nbskd->nbqkgd", p, vw.astype(jnp.float32))
    m = m.transpose(0, 1, 4, 2, 3)
    l = l.transpose(0, 1, 4, 2, 3)
    o = o / l[..., None]
    o = o.reshape(n, lp, h, d)[:, :L]
    m = m.reshape(n, lp, h)[:, :L]
    l = l.reshape(n, lp, h)[:, :L]
    return o, m, l


def dilated_attention(q, k, v):
    b, S, h, d = q.shape
    outs, ms, ls = [], [], []
    for window, dil in B_PATTERNS:
        L = S // dil

        def fold(t):
            return t.reshape(b, L, dil, h, d).transpose(0, 2, 1, 3, 4).reshape(b * dil, L, h, d)

        o, m, l = banded_attention(fold(q), fold(k), fold(v), window // dil)
        outs.append(o.reshape(b, dil, L, h, d).transpose(0, 2, 1, 3, 4).reshape(b, S, h, d))
        ms.append(m.reshape(b, dil, L, h).transpose(0, 2, 1, 3).reshape(b, S, h))
        ls.append(l.reshape(b, dil, L, h).transpose(0, 2, 1, 3).reshape(b, S, h))
    o = jnp.stack(outs)
    m = jnp.stack(ms)
    l = jnp.stack(ls)
    w = l * jnp.exp(m - m.max(axis=0, keepdims=True))
    return (w[..., None] * o).sum(axis=0) / w.sum(axis=0)[..., None]


def _fwd_setup_inputs(seed: int = 0) -> dict:
    key = jax.random.key(seed)
    ks = jax.random.split(key, 10)
    f32 = jnp.float32
    x = jax.random.normal(ks[0], (BATCH, SEQ, D_MODEL), f32)
    norm_gain = 1.0 + 0.1 * jax.random.normal(ks[1], (DEPTH, D_MODEL), f32)
    w_in = jax.random.normal(ks[2], (DEPTH, D_MODEL, IN_WIDTH), f32) * D_MODEL ** -0.5
    q_norm_a = 1.0 + 0.1 * jax.random.normal(ks[3], (DEPTH, HEAD_DIM), f32)
    k_norm_a = 1.0 + 0.1 * jax.random.normal(ks[4], (DEPTH, HEAD_DIM), f32)
    sinks_a = 0.5 * jax.random.normal(ks[5], (DEPTH, A_HEADS), f32)
    q_norm_b = 1.0 + 0.1 * jax.random.normal(ks[6], (DEPTH, HEAD_DIM), f32)
    k_norm_b = 1.0 + 0.1 * jax.random.normal(ks[7], (DEPTH, HEAD_DIM), f32)
    w_out = jax.random.normal(ks[8], (DEPTH, MIX_WIDTH, D_MODEL), f32) * MIX_WIDTH ** -0.5
    return {"x": x, "norm_gain": norm_gain, "w_in": w_in,
            "q_norm_a": q_norm_a, "k_norm_a": k_norm_a, "sinks_a": sinks_a,
            "q_norm_b": q_norm_b, "k_norm_b": k_norm_b, "w_out": w_out}


def _fwd_reference(x, norm_gain, w_in, q_norm_a, k_norm_a, sinks_a, q_norm_b, k_norm_b, w_out):
    b, S, _ = x.shape
    pos = jnp.arange(S)
    split_at = [int(c) for c in np.cumsum(IN_SPLITS)[:-1]]
    for i in range(DEPTH):
        hdn = rmsnorm(x, norm_gain[i])
        proj = jnp.einsum("bsd,de->bse", hdn, w_in[i])
        q_a, k_a, v_a, g_a, q_b, k_b, v_b, g_b = jnp.split(proj, split_at, axis=-1)

        q_a = rope(rmsnorm(q_a.reshape(b, S, A_HEADS, HEAD_DIM), q_norm_a[i]), pos)
        k_a = rope(rmsnorm(k_a.reshape(b, S, A_KV_HEADS, HEAD_DIM), k_norm_a[i]), pos)
        v_a = v_a.reshape(b, S, A_KV_HEADS, HEAD_DIM)
        o_a, _, _ = banded_attention(q_a, k_a, v_a, A_WINDOW - 1, sinks=sinks_a[i])
        o_a = o_a.reshape(b, S, A_WIDTH).astype(x.dtype) * jax.nn.silu(g_a)

        q_b = rope(rmsnorm(q_b.reshape(b, S, B_HEADS, HEAD_DIM), q_norm_b[i]), pos)
        k_b = rope(rmsnorm(k_b.reshape(b, S, B_HEADS, HEAD_DIM), k_norm_b[i]), pos)
        v_b = v_b.reshape(b, S, B_HEADS, HEAD_DIM)
        o_b = dilated_attention(q_b, k_b, v_b)
        o_b = o_b.reshape(b, S, B_WIDTH).astype(x.dtype) * jax.nn.silu(g_b)

        mixed = jnp.concatenate([o_a, o_b], axis=-1)
        x = x + jnp.einsum("bse,ed->bsd", mixed, w_out[i])
    return x


import jax as _jax
import jax.numpy as _jnp

TWIN_FORMAT = 'train_step'
FWD_PARAMS = ['x', 'norm_gain', 'w_in', 'q_norm_a', 'k_norm_a', 'sinks_a', 'q_norm_b', 'k_norm_b', 'w_out']
TWIN_WEIGHTS = ['norm_gain', 'w_in', 'q_norm_a', 'k_norm_a', 'sinks_a', 'q_norm_b', 'k_norm_b', 'w_out']
TWIN_DIFF_INPUT = 'x'
TWIN_INPUTS = ['x', 'norm_gain', 'w_in', 'q_norm_a', 'k_norm_a', 'sinks_a', 'q_norm_b', 'k_norm_b', 'w_out', 'loss_target', 'm_norm_gain', 'm_w_in', 'm_q_norm_a', 'm_k_norm_a', 'm_sinks_a', 'm_q_norm_b', 'm_k_norm_b', 'm_w_out', 'v_norm_gain', 'v_w_in', 'v_q_norm_a', 'v_k_norm_a', 'v_sinks_a', 'v_q_norm_b', 'v_k_norm_b', 'v_w_out']
TWIN_OUTPUTS = ['loss', 'grad_x', 'grad_norm_gain', 'grad_w_in', 'grad_q_norm_a', 'grad_k_norm_a', 'grad_sinks_a', 'grad_q_norm_b', 'grad_k_norm_b', 'grad_w_out', 'delta_norm_gain', 'delta_w_in', 'delta_q_norm_a', 'delta_k_norm_a', 'delta_sinks_a', 'delta_q_norm_b', 'delta_k_norm_b', 'delta_w_out', 'new_m_norm_gain', 'new_m_w_in', 'new_m_q_norm_a', 'new_m_k_norm_a', 'new_m_sinks_a', 'new_m_q_norm_b', 'new_m_k_norm_b', 'new_m_w_out', 'new_v_norm_gain', 'new_v_w_in', 'new_v_q_norm_a', 'new_v_k_norm_a', 'new_v_sinks_a', 'new_v_q_norm_b', 'new_v_k_norm_b', 'new_v_w_out']
TWIN_LEAF_KINDS = {'loss': 'loss', 'grad_x': 'grad_x', 'grad_norm_gain': 'grad_w', 'grad_w_in': 'grad_w', 'grad_q_norm_a': 'grad_w', 'grad_k_norm_a': 'grad_w', 'grad_sinks_a': 'grad_w', 'grad_q_norm_b': 'grad_w', 'grad_k_norm_b': 'grad_w', 'grad_w_out': 'grad_w', 'delta_norm_gain': 'delta_w', 'delta_w_in': 'delta_w', 'delta_q_norm_a': 'delta_w', 'delta_k_norm_a': 'delta_w', 'delta_sinks_a': 'delta_w', 'delta_q_norm_b': 'delta_w', 'delta_k_norm_b': 'delta_w', 'delta_w_out': 'delta_w', 'new_m_norm_gain': 'new_m', 'new_m_w_in': 'new_m', 'new_m_q_norm_a': 'new_m', 'new_m_k_norm_a': 'new_m', 'new_m_sinks_a': 'new_m', 'new_m_q_norm_b': 'new_m', 'new_m_k_norm_b': 'new_m', 'new_m_w_out': 'new_m', 'new_v_norm_gain': 'new_v', 'new_v_w_in': 'new_v', 'new_v_q_norm_a': 'new_v', 'new_v_k_norm_a': 'new_v', 'new_v_sinks_a': 'new_v', 'new_v_q_norm_b': 'new_v', 'new_v_k_norm_b': 'new_v', 'new_v_w_out': 'new_v'}


def _forward(args):
    return _fwd_reference(*[args[k] for k in FWD_PARAMS])


def _output_shape():
    out = _jax.eval_shape(lambda: _forward(_fwd_setup_inputs(0)))
    return out.shape, out.dtype

N_MICROBATCH = 1
ADAM_LR = 0.001
ADAM_B1 = 0.9
ADAM_B2 = 0.999
ADAM_EPS = 1e-08
ADAM_WD = 0.01
ADAM_STEP = 10
PER_EXAMPLE_BATCH_AXIS = {'x': 0, 'loss_target': 0}
SHARED_INPUTS = []
_WEIGHT_DTYPES = {'norm_gain': _jnp.float32, 'w_in': _jnp.float32, 'q_norm_a': _jnp.float32, 'k_norm_a': _jnp.float32, 'sinks_a': _jnp.float32, 'q_norm_b': _jnp.float32, 'k_norm_b': _jnp.float32, 'w_out': _jnp.float32}
MOMENT_SCALE = {'norm_gain': 2.217925e-01, 'w_in': 3.963448e-02, 'q_norm_a': 1.010157e+00, 'k_norm_a': 1.018051e+00, 'sinks_a': 2.888526e-01, 'q_norm_b': 6.414703e-01, 'k_norm_b': 6.463596e-01, 'w_out': 3.335834e-02}


def _to_microbatches(a, axis):
    t = _jnp.moveaxis(a, axis, 0)
    t = t.reshape((N_MICROBATCH, t.shape[0] // N_MICROBATCH) + t.shape[1:])
    return _jnp.moveaxis(t, 1, axis + 1)


def setup_inputs(seed: int = 0) -> dict:
    inp = _fwd_setup_inputs(seed)
    key = _jax.random.fold_in(_jax.random.key(seed), 7919)
    shape, _ = _output_shape()
    out = dict(inp)
    out["loss_target"] = _jax.random.normal(_jax.random.fold_in(key, 0), shape, _jnp.float32)
    for i, name in enumerate(TWIN_WEIGHTS):
        w = inp[name].astype(_jnp.float32)
        if MOMENT_SCALE is None:
            s = _jnp.sqrt(_jnp.mean(_jnp.square(w)) + 1e-30)
        else:
            s = MOMENT_SCALE[name]
        km, kv = _jax.random.split(_jax.random.fold_in(key, i + 1))
        out[name] = w
        out["m_" + name] = s * _jax.random.normal(km, w.shape, _jnp.float32)
        out["v_" + name] = (s * s) * _jax.random.uniform(kv, w.shape, _jnp.float32, 0.5, 1.5)
    if N_MICROBATCH > 1:
        for name, axis in PER_EXAMPLE_BATCH_AXIS.items():
            out[name] = _to_microbatches(out[name], axis)
    return {'x': out['x'], 'norm_gain': out['norm_gain'], 'w_in': out['w_in'], 'q_norm_a': out['q_norm_a'], 'k_norm_a': out['k_norm_a'], 'sinks_a': out['sinks_a'], 'q_norm_b': out['q_norm_b'], 'k_norm_b': out['k_norm_b'], 'w_out': out['w_out'], 'loss_target': out['loss_target'], 'm_norm_gain': out['m_norm_gain'], 'm_w_in': out['m_w_in'], 'm_q_norm_a': out['m_q_norm_a'], 'm_k_norm_a': out['m_k_norm_a'], 'm_sinks_a': out['m_sinks_a'], 'm_q_norm_b': out['m_q_norm_b'], 'm_k_norm_b': out['m_k_norm_b'], 'm_w_out': out['m_w_out'], 'v_norm_gain': out['v_norm_gain'], 'v_w_in': out['v_w_in'], 'v_q_norm_a': out['v_q_norm_a'], 'v_k_norm_a': out['v_k_norm_a'], 'v_sinks_a': out['v_sinks_a'], 'v_q_norm_b': out['v_q_norm_b'], 'v_k_norm_b': out['v_k_norm_b'], 'v_w_out': out['v_w_out']}


def _loss(weights, diff, rest, loss_target):
    with _jax.named_scope("forward"):
        args = {**rest, TWIN_DIFF_INPUT: diff, **{k: w.astype(_WEIGHT_DTYPES[k]) for k, w in weights.items()}}
        y = _forward(args)
    with _jax.named_scope("loss_head"):
        err = _jnp.square(y.astype(_jnp.float32) - loss_target)
        return 0.5 * _jnp.sum(_jnp.mean(err, axis=-1)) if err.ndim else 0.5 * err


def _adamw(w, g, m, v):
    m = ADAM_B1 * m + (1.0 - ADAM_B1) * g
    v = ADAM_B2 * v + (1.0 - ADAM_B2) * _jnp.square(g)
    m_hat = m / (1.0 - ADAM_B1 ** ADAM_STEP)
    v_hat = v / (1.0 - ADAM_B2 ** ADAM_STEP)
    delta = -ADAM_LR * (m_hat / (_jnp.sqrt(v_hat) + ADAM_EPS) + ADAM_WD * w)
    return delta, m, v


def reference(x, norm_gain, w_in, q_norm_a, k_norm_a, sinks_a, q_norm_b, k_norm_b, w_out, loss_target, m_norm_gain, m_w_in, m_q_norm_a, m_k_norm_a, m_sinks_a, m_q_norm_b, m_k_norm_b, m_w_out, v_norm_gain, v_w_in, v_q_norm_a, v_k_norm_a, v_sinks_a, v_q_norm_b, v_k_norm_b, v_w_out):
    given = dict(x=x, norm_gain=norm_gain, w_in=w_in, q_norm_a=q_norm_a, k_norm_a=k_norm_a, sinks_a=sinks_a, q_norm_b=q_norm_b, k_norm_b=k_norm_b, w_out=w_out, loss_target=loss_target, m_norm_gain=m_norm_gain, m_w_in=m_w_in, m_q_norm_a=m_q_norm_a, m_k_norm_a=m_k_norm_a, m_sinks_a=m_sinks_a, m_q_norm_b=m_q_norm_b, m_k_norm_b=m_k_norm_b, m_w_out=m_w_out, v_norm_gain=v_norm_gain, v_w_in=v_w_in, v_q_norm_a=v_q_norm_a, v_k_norm_a=v_k_norm_a, v_sinks_a=v_sinks_a, v_q_norm_b=v_q_norm_b, v_k_norm_b=v_k_norm_b, v_w_out=v_w_out)
    weights = {n: given[n] for n in TWIN_WEIGHTS}
    shared = {n: given[n] for n in SHARED_INPUTS}
    per_example = {n: given[n] for n in ['x']}
    grad_fn = _jax.value_and_grad(_loss, argnums=(0, 1))

    def one_microbatch(ex, loss_target):
        ex = dict(ex)
        diff = ex.pop(TWIN_DIFF_INPUT)
        return grad_fn(weights, diff, {**shared, **ex}, loss_target)

    if N_MICROBATCH == 1:
        loss, (grad_w, grad_x) = one_microbatch(per_example, given["loss_target"])
    else:
        def body(carry, xs):
            loss_sum, grad_sum = carry
            l_k, (gw_k, gx_k) = one_microbatch(xs[0], xs[1])
            with _jax.named_scope("update"):
                return (loss_sum + l_k, _jax.tree.map(_jnp.add, grad_sum, gw_k)), gx_k

        init = (_jnp.zeros((), _jnp.float32), _jax.tree.map(_jnp.zeros_like, weights))
        (loss, grad_w), grad_x = _jax.lax.scan(body, init, (per_example, given["loss_target"]))
    with _jax.named_scope("update"):
        delta_w, new_m, new_v = {}, {}, {}
        for n in TWIN_WEIGHTS:
            delta_w[n], new_m[n], new_v[n] = _adamw(weights[n], grad_w[n], given["m_" + n], given["v_" + n])
    return (loss, grad_x, *[grad_w[n] for n in TWIN_WEIGHTS], *[delta_w[n] for n in TWIN_WEIGHTS],
            *[new_m[n] for n in TWIN_WEIGHTS], *[new_v[n] for n in TWIN_WEIGHTS])
```

```python
import functools

import jax
import jax.numpy as jnp
from jax import lax
from jax.experimental import pallas as pl
from jax.experimental.pallas import tpu as pltpu

F32 = jnp.float32
BF16 = jnp.bfloat16

D_MODEL = 1024
HEAD_DIM = 64
HEADS = 8
WIDTH = HEADS * HEAD_DIM
A_KV_WIDTH = 2 * HEAD_DIM
BLOCK = 128
LANES = 128
B_DILATIONS = (1, 4, 16)
A_MAX_DIST = 127
B_MAX_DIST = 128
ROPE_THETA = 10000.0
EPS = 1e-6
NEG = -1e30
SCALE = HEAD_DIM ** -0.5

IN_WIDTH = 3328
C_QA, C_KA, C_VA, C_GA, C_QB, C_KB, C_VB, C_GB, C_END = 0, 512, 640, 768, 1280, 1792, 2304, 2816, 3328

N_DEV = 8
N_CHIP = 4
MESH = pl.DeviceIdType.MESH

ADAM_LR = 0.001
ADAM_B1 = 0.9
ADAM_B2 = 0.999
ADAM_EPS = 1e-08
ADAM_WD = 0.01
ADAM_STEP = 10

ROW_TILE = 256
BWD_ROW_TILE = 128
ACC_COLS = 256
VMEM_LIMIT = 56 * 1024 * 1024


def _dot(a, b):
    return jnp.dot(a, b, preferred_element_type=F32)


def _dot_nt(a, b):
    return lax.dot_general(a, b, (((1,), (1,)), ((), ())), preferred_element_type=F32)


def _dot_tn(a, b):
    return lax.dot_general(a, b, (((0,), (0,)), ((), ())), preferred_element_type=F32)


def _head_sum(z, bd):
    hi = z.astype(BF16)
    lo = (z - hi.astype(F32)).astype(BF16)
    return _dot(hi, bd) + _dot(lo, bd)


def _swap_halves(t):
    w = t.shape[1]
    lane = lax.broadcasted_iota(jnp.int32, t.shape, 1)
    return jnp.where(lane % HEAD_DIM < HEAD_DIM // 2, pltpu.roll(t, w - 32, 1), pltpu.roll(t, 32, 1))


def _qknorm_rope(t, g, cos, sin_s, bd):
    r = lax.rsqrt(_head_sum(t * t, bd) * (1.0 / HEAD_DIM) + EPS)
    n = (t * r) * g
    return n * cos + _swap_halves(n) * sin_s


def _qknorm_rope_bwd(dout, t, g, cos, sin_s, bd):
    dn = dout * cos + _swap_halves(dout * sin_s)
    r = lax.rsqrt(_head_sum(t * t, bd) * (1.0 / HEAD_DIM) + EPS)
    tr = t * r
    u = dn * g
    dt = r * (u - tr * (_head_sum(u * tr, bd) * (1.0 / HEAD_DIM)))
    return dt, dn * tr


def _sigmoid(g):
    return 1.0 / (1.0 + jnp.exp(-g))


def _expand_heads(st):
    t = st.shape[0]
    lane = lax.broadcasted_iota(jnp.int32, (t, LANES), 1)
    chunks = []
    for c in range(WIDTH // LANES):
        chunks.append(jnp.where(lane < HEAD_DIM, st[:, 2 * c:2 * c + 1], st[:, 2 * c + 1:2 * c + 2]))
    return jnp.concatenate(chunks, axis=1)


def _reduce_heads(z):
    t = z.shape[0]
    lane = lax.broadcasted_iota(jnp.int32, (t, LANES), 1)
    out = jnp.zeros((t, LANES), F32)
    for c in range(WIDTH // LANES):
        zc = z[:, c * LANES:(c + 1) * LANES]
        for ph in range(2):
            s = jnp.sum(jnp.where((lane // HEAD_DIM) == ph, zc, 0.0), axis=-1, keepdims=True)
            out = jnp.where(lane == 2 * c + ph, s, out)
    return out


def _inproj(x2, gain, w_bf, cos, sin_s, gqa, gka, gqb, gkb, bd512, bd128):
    s = x2.shape[0]
    tm = ROW_TILE

    def body(x_ref, gain_ref, w_hbm, cos_ref, sin_ref, gqa_ref, gka_ref, gqb_ref, gkb_ref, bd512_ref, bd128_ref,
             qa_ref, ka_ref, va_ref, qb_ref, kb_ref, vb_ref,
             qa_raw_ref, ka_raw_ref, ga_ref, qb_raw_ref, kb_raw_ref, gb_ref, w_vmem):
        @pl.when(pl.program_id(0) == 0)
        def _():
            pltpu.sync_copy(w_hbm, w_vmem)

        xt = x_ref[...]
        r = lax.rsqrt(jnp.mean(xt * xt, axis=-1, keepdims=True) + EPS)
        h = ((xt * r) * gain_ref[...]).astype(BF16)
        cos1 = cos_ref[...]
        sin1 = sin_ref[...]
        cos4 = jnp.tile(cos1, (1, 4))
        sin4 = jnp.tile(sin1, (1, 4))

        def seg(a, b):
            return _dot(h, w_vmem[:, a:b])

        t = seg(C_QA, C_KA)
        qa_raw_ref[...] = t
        qa_ref[...] = (_qknorm_rope(t, gqa_ref[...], cos4, sin4, bd512_ref[...]) * SCALE).astype(BF16)
        t = seg(C_KA, C_VA)
        ka_raw_ref[...] = t
        ka_ref[...] = _qknorm_rope(t, gka_ref[...], cos1, sin1, bd128_ref[...]).astype(BF16)
        va_ref[...] = seg(C_VA, C_GA).astype(BF16)
        ga_ref[...] = seg(C_GA, C_QB)
        t = seg(C_QB, C_KB)
        qb_raw_ref[...] = t
        qb_ref[...] = (_qknorm_rope(t, gqb_ref[...], cos4, sin4, bd512_ref[...]) * SCALE).astype(BF16)
        t = seg(C_KB, C_VB)
        kb_raw_ref[...] = t
        kb_ref[...] = _qknorm_rope(t, gkb_ref[...], cos4, sin4, bd512_ref[...]).astype(BF16)
        vb_ref[...] = seg(C_VB, C_GB).astype(BF16)
        gb_ref[...] = seg(C_GB, C_END)

    def rows(w):
        return pl.BlockSpec((tm, w), lambda i: (i, 0))

    def whole(a):
        return pl.BlockSpec(a.shape, lambda i: (0, 0))

    sds = jax.ShapeDtypeStruct
    out_shape = (sds((s, WIDTH), BF16), sds((s, A_KV_WIDTH), BF16), sds((s, A_KV_WIDTH), BF16),
                 sds((s, WIDTH), BF16), sds((s, WIDTH), BF16), sds((s, WIDTH), BF16),
                 sds((s, WIDTH), F32), sds((s, A_KV_WIDTH), F32), sds((s, WIDTH), F32),
                 sds((s, WIDTH), F32), sds((s, WIDTH), F32), sds((s, WIDTH), F32))
    out_specs = (rows(WIDTH), rows(A_KV_WIDTH), rows(A_KV_WIDTH), rows(WIDTH), rows(WIDTH), rows(WIDTH),
                 rows(WIDTH), rows(A_KV_WIDTH), rows(WIDTH), rows(WIDTH), rows(WIDTH), rows(WIDTH))
    return pl.pallas_call(
        body, name="inproj_fwd", grid=(s // tm,),
        in_specs=[rows(D_MODEL), whole(gain), pl.BlockSpec(memory_space=pl.ANY), rows(LANES), rows(LANES),
                  whole(gqa), whole(gka), whole(gqb), whole(gkb), whole(bd512), whole(bd128)],
        out_specs=out_specs, out_shape=out_shape,
        scratch_shapes=[pltpu.VMEM((D_MODEL, IN_WIDTH), BF16)],
        compiler_params=pltpu.CompilerParams(dimension_semantics=("arbitrary",), vmem_limit_bytes=VMEM_LIMIT),
    )(x2, gain, w_bf, cos, sin_s, gqa, gka, gqb, gkb, bd512, bd128)


def _band_mask(i, max_dist):
    qi = lax.broadcasted_iota(jnp.int32, (BLOCK, 2 * BLOCK), 0)
    sj = lax.broadcasted_iota(jnp.int32, (BLOCK, 2 * BLOCK), 1)
    dist = qi - sj + BLOCK
    return (dist >= 0) & (dist <= max_dist) & ((sj >= BLOCK) | (i > 0))


def _kv_window(prev_ref, cur_ref, kc):
    sl = slice(kc * LANES, (kc + 1) * LANES)
    return jnp.concatenate([prev_ref[:, sl], cur_ref[:, sl]], axis=0)


def _swap_heads(a):
    return pltpu.roll(a.astype(F32), HEAD_DIM, 1).astype(a.dtype)


def _attn_fwd(q, k, v, sinks, *, max_dist, name):
    dil, ln, _ = q.shape
    kw = k.shape[2]
    gqa = kw == A_KV_WIDTH
    nb = ln // BLOCK
    with_sinks = sinks is not None

    def body(*refs):
        if with_sinks:
            q_ref, kp_ref, kc_ref, vp_ref, vc_ref, sink_ref, o_ref, lse_ref = refs
        else:
            q_ref, kp_ref, kc_ref, vp_ref, vc_ref, o_ref, lse_ref = refs
        valid = _band_mask(pl.program_id(1), max_dist)
        lane = lax.broadcasted_iota(jnp.int32, (1, LANES), 1)
        lse_ref[...] = jnp.zeros((BLOCK, LANES), F32)
        if gqa:
            k_plain, v_plain = _kv_window(kp_ref, kc_ref, 0), _kv_window(vp_ref, vc_ref, 0)
            k_swap, v_swap = _swap_heads(k_plain), _swap_heads(v_plain)
        for c in range(WIDTH // LANES):
            q2 = q_ref[:, c * LANES:(c + 1) * LANES]
            if not gqa:
                k2, v2 = _kv_window(kp_ref, kc_ref, c), _kv_window(vp_ref, vc_ref, c)
            o_pair = jnp.zeros((BLOCK, LANES), F32)
            for ph in range(2):
                h = 2 * c + ph
                if gqa:
                    kk, vv = (k_plain, v_plain) if (c // 2) == ph else (k_swap, v_swap)
                else:
                    kk, vv = k2, v2
                half = (lane // HEAD_DIM) == ph
                qm = jnp.where(half, q2, jnp.zeros_like(q2))
                sc = jnp.where(valid, _dot_nt(qm, kk), NEG)
                m = jnp.max(sc, axis=-1, keepdims=True)
                if with_sinks:
                    sk = sink_ref[0, h]
                    m = jnp.maximum(m, sk)
                p = jnp.exp(sc - m)
                l = jnp.sum(p, axis=-1, keepdims=True)
                if with_sinks:
                    l = l + jnp.exp(sk - m)
                o2 = _dot(p.astype(BF16), vv) / l
                o_pair = jnp.where(half, o2, o_pair)
                lse_ref[:, h:h + 1] = m + jnp.log(l)
            o_ref[:, c * LANES:(c + 1) * LANES] = o_pair

    def cur(w):
        return pl.BlockSpec((None, BLOCK, w), lambda r, i: (r, i, 0))

    def prev(w):
        return pl.BlockSpec((None, BLOCK, w), lambda r, i: (r, jnp.maximum(i - 1, 0), 0))

    in_specs = [cur(WIDTH), prev(kw), cur(kw), prev(kw), cur(kw)]
    args = [q, k, k, v, v]
    if with_sinks:
        in_specs.append(pl.BlockSpec(memory_space=pltpu.SMEM))
        args.append(sinks)
    return pl.pallas_call(
        body, name=name, grid=(dil, nb), in_specs=in_specs,
        out_specs=(cur(WIDTH), cur(LANES)),
        out_shape=(jax.ShapeDtypeStruct((dil, ln, WIDTH), F32), jax.ShapeDtypeStruct((dil, ln, LANES), F32)),
        compiler_params=pltpu.CompilerParams(dimension_semantics=("arbitrary", "arbitrary")),
    )(*args)


def _attn_bwd(q, k, v, do, lse, delta, *, max_dist, name):
    dil, ln, _ = q.shape
    kw = k.shape[2]
    gqa = kw == A_KV_WIDTH
    nb = ln // BLOCK
    n_kc = kw // LANES

    def body(q_ref, kp_ref, kc_ref, vp_ref, vc_ref, do_ref, lse_ref, dl_ref, dq_ref, dk_ref, dv_ref, ck_ref, cv_ref):
        i = pl.program_id(1)

        @pl.when(i == 0)
        def _():
            ck_ref[...] = jnp.zeros_like(ck_ref)
            cv_ref[...] = jnp.zeros_like(cv_ref)

        @pl.when(i < nb)
        def _():
            valid = _band_mask(i, max_dist)
            lane = lax.broadcasted_iota(jnp.int32, (1, LANES), 1)
            dk_acc = [jnp.zeros((2 * BLOCK, LANES), F32) for _ in range(n_kc)]
            dv_acc = [jnp.zeros((2 * BLOCK, LANES), F32) for _ in range(n_kc)]
            if gqa:
                k_plain, v_plain = _kv_window(kp_ref, kc_ref, 0), _kv_window(vp_ref, vc_ref, 0)
                k_swap, v_swap = _swap_heads(k_plain), _swap_heads(v_plain)
            for c in range(WIDTH // LANES):
                q2 = q_ref[:, c * LANES:(c + 1) * LANES]
                do2 = do_ref[:, c * LANES:(c + 1) * LANES]
                kc = 0 if gqa else c
                if not gqa:
                    k2, v2 = _kv_window(kp_ref, kc_ref, c), _kv_window(vp_ref, vc_ref, c)
                dq_pair = jnp.zeros((BLOCK, LANES), F32)
                for ph in range(2):
                    h = 2 * c + ph
                    swapped = gqa and (c // 2) != ph
                    if gqa:
                        kk, vv = (k_swap, v_swap) if swapped else (k_plain, v_plain)
                    else:
                        kk, vv = k2, v2
                    half = (lane // HEAD_DIM) == ph
                    qm = jnp.where(half, q2, jnp.zeros_like(q2))
                    dom = jnp.where(half, do2, jnp.zeros_like(do2))
                    sc = jnp.where(valid, _dot_nt(qm, kk), NEG)
                    p = jnp.exp(sc - lse_ref[:, h:h + 1])
                    dp = _dot_nt(dom, vv)
                    ds = (p * (dp - dl_ref[:, h:h + 1])).astype(BF16)
                    dq_pair = jnp.where(half, _dot(ds, kk), dq_pair)
                    dkc = _dot_tn(ds, qm)
                    dvc = _dot_tn(p.astype(BF16), dom)
                    if swapped:
                        dkc, dvc = _swap_heads(dkc), _swap_heads(dvc)
                    dk_acc[kc] = dk_acc[kc] + dkc
                    dv_acc[kc] = dv_acc[kc] + dvc
                dq_ref[:, c * LANES:(c + 1) * LANES] = dq_pair * SCALE
            for kc in range(n_kc):
                sl = slice(kc * LANES, (kc + 1) * LANES)
                dk_ref[:, sl] = ck_ref[:, sl] + dk_acc[kc][:BLOCK]
                dv_ref[:, sl] = cv_ref[:, sl] + dv_acc[kc][:BLOCK]
                ck_ref[:, sl] = dk_acc[kc][BLOCK:]
                cv_ref[:, sl] = dv_acc[kc][BLOCK:]

        @pl.when(i == nb)
        def _():
            dk_ref[...] = ck_ref[...]
            dv_ref[...] = cv_ref[...]

    def cur(w):
        return pl.BlockSpec((None, BLOCK, w), lambda r, i: (r, jnp.minimum(i, nb - 1), 0))

    def prev(w):
        return pl.BlockSpec((None, BLOCK, w), lambda r, i: (r, jnp.clip(i - 1, 0, nb - 1), 0))

    def lag(w):
        return pl.BlockSpec((None, BLOCK, w), lambda r, i: (r, jnp.maximum(i - 1, 0), 0))

    sds = jax.ShapeDtypeStruct
    return pl.pallas_call(
        body, name=name, grid=(dil, nb + 1),
        in_specs=[cur(WIDTH), prev(kw), cur(kw), prev(kw), cur(kw), cur(WIDTH), cur(LANES), cur(LANES)],
        out_specs=(cur(WIDTH), lag(kw), lag(kw)),
        out_shape=(sds((dil, ln, WIDTH), F32), sds((dil, ln, kw), F32), sds((dil, ln, kw), F32)),
        scratch_shapes=[pltpu.VMEM((BLOCK, kw), F32), pltpu.VMEM((BLOCK, kw), F32)],
        compiler_params=pltpu.CompilerParams(dimension_semantics=("arbitrary", "arbitrary")),
    )(q, k, k, v, v, do, lse, delta)


def _outproj(o_a, lse_a, o_b, lse_b, g_a, g_b, x2, tgt2, w_out_bf, sink_row):
    s = x2.shape[0]
    tm = ROW_TILE

    def body(oa_ref, lsea_ref, ob1_ref, ob2_ref, ob3_ref, ls1_ref, ls2_ref, ls3_ref, ga_ref, gb_ref, x_ref, t_ref,
             w_ref, sink_ref,
             dy_ref, doa_ref, dob_ref, dga_ref, dgb_ref, lseb_ref, dla_ref, dlb_ref, gw_ref, loss_ref, dsink_ref):
        i = pl.program_id(0)

        @pl.when(i == 0)
        def _():
            gw_ref[...] = jnp.zeros_like(gw_ref)
            loss_ref[...] = jnp.zeros_like(loss_ref)
            dsink_ref[...] = jnp.zeros_like(dsink_ref)

        ls = [ls1_ref[...], ls2_ref[...], ls3_ref[...]]
        mx = jnp.maximum(jnp.maximum(ls[0], ls[1]), ls[2])
        lse_b = mx + jnp.log(jnp.exp(ls[0] - mx) + jnp.exp(ls[1] - mx) + jnp.exp(ls[2] - mx))
        lane = lax.broadcasted_iota(jnp.int32, (tm, LANES), 1)
        lse_b = jnp.where(lane < HEADS, lse_b, 0.0)
        lseb_ref[...] = lse_b
        o_b = (_expand_heads(jnp.exp(ls[0] - lse_b)) * ob1_ref[...]
               + _expand_heads(jnp.exp(ls[1] - lse_b)) * ob2_ref[...]
               + _expand_heads(jnp.exp(ls[2] - lse_b)) * ob3_ref[...])
        o_a = oa_ref[...]
        g_a = ga_ref[...]
        g_b = gb_ref[...]
        sg_a = _sigmoid(g_a)
        sg_b = _sigmoid(g_b)
        silu_a = g_a * sg_a
        silu_b = g_b * sg_b
        mixed = jnp.concatenate([o_a * silu_a, o_b * silu_b], axis=1).astype(BF16)
        w = w_ref[...]
        y = x_ref[...] + _dot(mixed, w)
        diff = y - t_ref[...]
        loss_ref[...] += (0.5 / D_MODEL) * jnp.sum(diff * diff)
        dy = diff * (1.0 / D_MODEL)
        dy_ref[...] = dy
        dyb = dy.astype(BF16)
        gw_ref[...] += _dot_tn(mixed, dyb)
        dmixed = _dot_nt(dyb, w)
        dm_a = dmixed[:, :WIDTH]
        dm_b = dmixed[:, WIDTH:]
        do_a = dm_a * silu_a
        do_b = dm_b * silu_b
        doa_ref[...] = do_a.astype(BF16)
        dob_ref[...] = do_b.astype(BF16)
        dga_ref[...] = (dm_a * o_a * (sg_a * (1.0 + g_a * (1.0 - sg_a)))).astype(BF16)
        dgb_ref[...] = (dm_b * o_b * (sg_b * (1.0 + g_b * (1.0 - sg_b)))).astype(BF16)
        dl_a = _reduce_heads(do_a * o_a)
        dla_ref[...] = dl_a
        dlb_ref[...] = _reduce_heads(do_b * o_b)
        dsink_ref[...] -= jnp.sum(jnp.exp(sink_ref[...] - lsea_ref[...]) * dl_a, axis=0, keepdims=True)

    def rows(w):
        return pl.BlockSpec((tm, w), lambda i: (i, 0))

    def whole(shape):
        return pl.BlockSpec(shape, lambda i: (0, 0))

    sds = jax.ShapeDtypeStruct
    return pl.pallas_call(
        body, name="outproj_fwd_bwd", grid=(s // tm,),
        in_specs=[rows(WIDTH), rows(LANES), rows(WIDTH), rows(WIDTH), rows(WIDTH), rows(LANES), rows(LANES),
                  rows(LANES), rows(WIDTH), rows(WIDTH), rows(D_MODEL), rows(D_MODEL),
                  whole((D_MODEL, D_MODEL)), whole((1, LANES))],
        out_specs=(rows(D_MODEL), rows(WIDTH), rows(WIDTH), rows(WIDTH), rows(WIDTH), rows(LANES), rows(LANES),
                   rows(LANES), whole((D_MODEL, D_MODEL)), whole((1, LANES)), whole((1, LANES))),
        out_shape=(sds((s, D_MODEL), F32), sds((s, WIDTH), BF16), sds((s, WIDTH), BF16), sds((s, WIDTH), BF16),
                   sds((s, WIDTH), BF16), sds((s, LANES), F32), sds((s, LANES), F32), sds((s, LANES), F32),
                   sds((D_MODEL, D_MODEL), F32), sds((1, LANES), F32), sds((1, LANES), F32)),
        compiler_params=pltpu.CompilerParams(dimension_semantics=("arbitrary",), vmem_limit_bytes=VMEM_LIMIT),
    )(o_a, lse_a, o_b[0], o_b[1], o_b[2], lse_b[0], lse_b[1], lse_b[2], g_a, g_b, x2, tgt2, w_out_bf, sink_row)


def _inproj_bwd(x2, dy, gain, w_bf, cos, sin_s, gqa, gka, gqb, gkb, bd512, bd128,
                qa_raw, ka_raw, qb_raw, kb_raw, dq_a, dk_a, dv_a, dq_b, dk_b, dv_b, dg_a, dg_b):
    s = x2.shape[0]
    tm = BWD_ROW_TILE
    n_steps = s // tm

    def body(x_ref, dy_ref, gain_ref, w_hbm, cos_ref, sin_ref, gqa_ref, gka_ref, gqb_ref, gkb_ref, bd512_ref,
             bd128_ref, qa_raw_ref, ka_raw_ref, qb_raw_ref, kb_raw_ref, dqa_ref, dka_ref, dva_ref,
             dqb1_ref, dqb2_ref, dqb3_ref, dkb1_ref, dkb2_ref, dkb3_ref, dvb1_ref, dvb2_ref, dvb3_ref,
             dga_ref, dgb_ref,
             gx_ref, gw_hbm, dgain_ref, dgqa_ref, dgka_ref, dgqb_ref, dgkb_ref,
             w_vmem, acc_ref, dproj_ref):
        i = pl.program_id(0)

        @pl.when(i == 0)
        def _():
            pltpu.sync_copy(w_hbm, w_vmem)
            acc_ref[...] = jnp.zeros_like(acc_ref)
            dgain_ref[...] = jnp.zeros_like(dgain_ref)
            dgqa_ref[...] = jnp.zeros_like(dgqa_ref)
            dgka_ref[...] = jnp.zeros_like(dgka_ref)
            dgqb_ref[...] = jnp.zeros_like(dgqb_ref)
            dgkb_ref[...] = jnp.zeros_like(dgkb_ref)

        cos1 = cos_ref[...]
        sin1 = sin_ref[...]
        cos4 = jnp.tile(cos1, (1, 4))
        sin4 = jnp.tile(sin1, (1, 4))

        dt, dg = _qknorm_rope_bwd(dqa_ref[...], qa_raw_ref[...], gqa_ref[...], cos4, sin4, bd512_ref[...])
        dproj_ref[:, C_QA:C_KA] = dt.astype(BF16)
        dgqa_ref[...] += jnp.sum(dg, axis=0, keepdims=True)
        dt, dg = _qknorm_rope_bwd(dka_ref[...], ka_raw_ref[...], gka_ref[...], cos1, sin1, bd128_ref[...])
        dproj_ref[:, C_KA:C_VA] = dt.astype(BF16)
        dgka_ref[...] += jnp.sum(dg, axis=0, keepdims=True)
        dproj_ref[:, C_VA:C_GA] = dva_ref[...].astype(BF16)
        dproj_ref[:, C_GA:C_QB] = dga_ref[...]
        dq = (dqb1_ref[...] + dqb2_ref[...]) + dqb3_ref[...]
        dt, dg = _qknorm_rope_bwd(dq, qb_raw_ref[...], gqb_ref[...], cos4, sin4, bd512_ref[...])
        dproj_ref[:, C_QB:C_KB] = dt.astype(BF16)
        dgqb_ref[...] += jnp.sum(dg, axis=0, keepdims=True)
        dk = (dkb1_ref[...] + dkb2_ref[...]) + dkb3_ref[...]
        dt, dg = _qknorm_rope_bwd(dk, kb_raw_ref[...], gkb_ref[...], cos4, sin4, bd512_ref[...])
        dproj_ref[:, C_KB:C_VB] = dt.astype(BF16)
        dgkb_ref[...] += jnp.sum(dg, axis=0, keepdims=True)
        dproj_ref[:, C_VB:C_GB] = ((dvb1_ref[...] + dvb2_ref[...]) + dvb3_ref[...]).astype(BF16)
        dproj_ref[:, C_GB:C_END] = dgb_ref[...]

        xt = x_ref[...]
        gain_row = gain_ref[...]
        r = lax.rsqrt(jnp.mean(xt * xt, axis=-1, keepdims=True) + EPS)
        xr = xt * r
        h = (xr * gain_row).astype(BF16)
        for a in range(0, IN_WIDTH, ACC_COLS):
            acc_ref[:, a:a + ACC_COLS] += _dot_tn(h, dproj_ref[:, a:a + ACC_COLS])
        dh = _dot_nt(dproj_ref[...], w_vmem[...])
        dgain_ref[...] += jnp.sum(dh * xr, axis=0, keepdims=True)
        u = dh * gain_row
        gx_ref[...] = dy_ref[...] + r * (u - xr * jnp.mean(u * xr, axis=-1, keepdims=True))

        @pl.when(i == n_steps - 1)
        def _():
            pltpu.sync_copy(acc_ref, gw_hbm)

    def rows(w):
        return pl.BlockSpec((tm, w), lambda i: (i, 0))

    def whole(a):
        return pl.BlockSpec(a.shape, lambda i: (0, 0))

    def acc_row(w):
        return pl.BlockSpec((1, w), lambda i: (0, 0))

    sds = jax.ShapeDtypeStruct
    any_spec = pl.BlockSpec(memory_space=pl.ANY)
    return pl.pallas_call(
        body, name="inproj_bwd", grid=(n_steps,),
        in_specs=[rows(D_MODEL), rows(D_MODEL), whole(gain), any_spec, rows(LANES), rows(LANES),
                  whole(gqa), whole(gka), whole(gqb), whole(gkb), whole(bd512), whole(bd128),
                  rows(WIDTH), rows(A_KV_WIDTH), rows(WIDTH), rows(WIDTH),
                  rows(WIDTH), rows(A_KV_WIDTH), rows(A_KV_WIDTH)] + [rows(WIDTH)] * 9 + [rows(WIDTH), rows(WIDTH)],
        out_specs=(rows(D_MODEL), any_spec, acc_row(D_MODEL), acc_row(WIDTH), acc_row(A_KV_WIDTH), acc_row(WIDTH),
                   acc_row(WIDTH)),
        out_shape=(sds((s, D_MODEL), F32), sds((D_MODEL, IN_WIDTH), F32), sds((1, D_MODEL), F32),
                   sds((1, WIDTH), F32), sds((1, A_KV_WIDTH), F32), sds((1, WIDTH), F32), sds((1, WIDTH), F32)),
        scratch_shapes=[pltpu.VMEM((D_MODEL, IN_WIDTH), BF16), pltpu.VMEM((D_MODEL, IN_WIDTH), F32),
                        pltpu.VMEM((tm, IN_WIDTH), BF16)],
        compiler_params=pltpu.CompilerParams(dimension_semantics=("arbitrary",), vmem_limit_bytes=VMEM_LIMIT),
    )(x2, dy, gain, w_bf, cos, sin_s, gqa, gka, gqb, gkb, bd512, bd128, qa_raw, ka_raw, qb_raw, kb_raw,
      dq_a, dk_a, dv_a, dq_b[0], dq_b[1], dq_b[2], dk_b[0], dk_b[1], dk_b[2], dv_b[0], dv_b[1], dv_b[2],
      dg_a, dg_b)


def _fold(a, dil):
    s, w = a.shape
    if dil == 1:
        return a[None]
    return a.reshape(s // dil, dil, w).transpose(1, 0, 2)


def _unfold(a):
    dil, ln, w = a.shape
    if dil == 1:
        return a[0]
    return a.transpose(1, 0, 2).reshape(dil * ln, w)


def _rope_tables(s):
    half = HEAD_DIM // 2
    inv = ROPE_THETA ** (-jnp.arange(half, dtype=F32) / half)
    ang = jnp.arange(s).astype(F32)[:, None] * inv[None, :]
    cos = jnp.cos(ang)
    sin = jnp.sin(ang)
    return jnp.tile(cos, (1, 4)), jnp.concatenate([-sin, sin, -sin, sin], axis=1)


def _block_diag_ones(w):
    idx = jnp.arange(w) // HEAD_DIM
    return (idx[:, None] == idx[None, :]).astype(BF16)


def _local_step(x2, tgt2, norm_gain, w_in_bf, q_norm_a, k_norm_a, sinks_a, q_norm_b, k_norm_b, w_out_bf):
    s = x2.shape[0]
    cos, sin_s = _rope_tables(s)
    bd512, bd128 = _block_diag_ones(WIDTH), _block_diag_ones(A_KV_WIDTH)
    gqa = jnp.tile(q_norm_a, (1, HEADS))
    gka = jnp.tile(k_norm_a, (1, 2))
    gqb = jnp.tile(q_norm_b, (1, HEADS))
    gkb = jnp.tile(k_norm_b, (1, HEADS))
    sink_row = jnp.pad(sinks_a, ((0, 0), (0, LANES - HEADS)))

    (qa, ka, va, qb, kb, vb, qa_raw, ka_raw, g_a, qb_raw, kb_raw, g_b) = _inproj(
        x2, norm_gain, w_in_bf, cos, sin_s, gqa, gka, gqb, gkb, bd512, bd128)

    o_a, lse_a = _attn_fwd(qa[None], ka[None], va[None], sinks_a, max_dist=A_MAX_DIST, name="attn_a_fwd")
    o_a, lse_a = o_a[0], lse_a[0]
    qb_f = [_fold(qb, d) for d in B_DILATIONS]
    kb_f = [_fold(kb, d) for d in B_DILATIONS]
    vb_f = [_fold(vb, d) for d in B_DILATIONS]
    o_b, lse_p = [], []
    for n, d in enumerate(B_DILATIONS):
        o, l = _attn_fwd(qb_f[n], kb_f[n], vb_f[n], None, max_dist=B_MAX_DIST, name="attn_b%d_fwd" % d)
        o_b.append(_unfold(o))
        lse_p.append(_unfold(l))

    (dy, do_a, do_b, dg_a, dg_b, lse_b, dl_a, dl_b, gw_out, loss_part, dsink) = _outproj(
        o_a, lse_a, o_b, lse_p, g_a, g_b, x2, tgt2, w_out_bf, sink_row)

    dq_a, dk_a, dv_a = _attn_bwd(qa[None], ka[None], va[None], do_a[None], lse_a[None], dl_a[None],
                                 max_dist=A_MAX_DIST, name="attn_a_bwd")
    dq_b, dk_b, dv_b = [], [], []
    for n, d in enumerate(B_DILATIONS):
        dq, dk, dv = _attn_bwd(qb_f[n], kb_f[n], vb_f[n], _fold(do_b, d), _fold(lse_b, d), _fold(dl_b, d),
                               max_dist=B_MAX_DIST, name="attn_b%d_bwd" % d)
        dq_b.append(_unfold(dq))
        dk_b.append(_unfold(dk))
        dv_b.append(_unfold(dv))

    gx, gw_in, dgain, dgqa, dgka, dgqb, dgkb = _inproj_bwd(
        x2, dy, norm_gain, w_in_bf, cos, sin_s, gqa, gka, gqb, gkb, bd512, bd128,
        qa_raw, ka_raw, qb_raw, kb_raw, dq_a[0], dk_a[0], dv_a[0], dq_b, dk_b, dv_b, dg_a, dg_b)
    return loss_part, gx, gw_in, gw_out, (dgain, dgqa, dgka, dsink, dgqb, dgkb)


def _position():
    return lax.axis_index("x"), lax.axis_index("y"), lax.axis_index("c")


def _all_to_all(srcs, name):
    n = len(srcs)
    bcast = [a.ndim == 2 for a in srcs]

    def body(*refs):
        src_refs, dst_refs = refs[:n], refs[n:2 * n]
        send_sems, recv_sems, local_sems = refs[2 * n:]
        x, y, c = _position()
        me = 4 * x + 2 * y + c
        copies = []
        for k in range(n):
            own = src_refs[k] if bcast[k] else src_refs[k].at[me]
            local = pltpu.make_async_copy(own, dst_refs[k].at[me], local_sems.at[k])
            local.start()
            copies.append(local)
        sends, arrivals = [], []
        for d in range(1, N_DEV):
            px, py, pc = x ^ (d >> 2), y ^ ((d >> 1) & 1), c ^ (d & 1)
            peer = 4 * px + 2 * py + pc
            for k in range(n):
                sem = k * N_DEV + d
                send = pltpu.make_async_remote_copy(
                    src_ref=src_refs[k] if bcast[k] else src_refs[k].at[peer], dst_ref=dst_refs[k].at[me],
                    send_sem=send_sems.at[sem], recv_sem=recv_sems.at[sem],
                    device_id=(px, py, pc), device_id_type=MESH)
                send.start()
                sends.append(send)
                arrivals.append(pltpu.make_async_remote_copy(
                    src_ref=src_refs[k] if bcast[k] else src_refs[k].at[me], dst_ref=dst_refs[k].at[peer],
                    send_sem=send_sems.at[sem], recv_sem=recv_sems.at[sem],
                    device_id=(px, py, pc), device_id_type=MESH))
        for arrival in arrivals:
            arrival.wait_recv()
        for send in sends:
            send.wait_send()
        for local in copies:
            local.wait()

    any_spec = pl.BlockSpec(memory_space=pl.ANY)
    out_shape = tuple(jax.ShapeDtypeStruct((N_DEV,) + a.shape[-2:], a.dtype) for a in srcs)
    return pl.pallas_call(
        body, name=name, in_specs=[any_spec] * n, out_specs=tuple([any_spec] * n), out_shape=out_shape,
        scratch_shapes=[pltpu.SemaphoreType.DMA((n * N_DEV,)), pltpu.SemaphoreType.DMA((n * N_DEV,)),
                        pltpu.SemaphoreType.DMA((n,))],
    )(*srcs)


def _pair_exchange(srcs, name):
    n = len(srcs)

    def body(*refs):
        src_refs, dst_refs = refs[:n], refs[n:2 * n]
        send_sems, recv_sems, local_sems = refs[2 * n:]
        x, y, c = _position()
        copies, sends, arrivals = [], [], []
        for k in range(n):
            local = pltpu.make_async_copy(src_refs[k], dst_refs[k].at[c], local_sems.at[k])
            local.start()
            copies.append(local)
            send = pltpu.make_async_remote_copy(
                src_ref=src_refs[k], dst_ref=dst_refs[k].at[c], send_sem=send_sems.at[k], recv_sem=recv_sems.at[k],
                device_id=(x, y, 1 - c), device_id_type=MESH)
            send.start()
            sends.append(send)
            arrivals.append(pltpu.make_async_remote_copy(
                src_ref=src_refs[k], dst_ref=dst_refs[k].at[1 - c], send_sem=send_sems.at[k],
                recv_sem=recv_sems.at[k], device_id=(x, y, 1 - c), device_id_type=MESH))
        for arrival in arrivals:
            arrival.wait_recv()
        for send in sends:
            send.wait_send()
        for local in copies:
            local.wait()

    any_spec = pl.BlockSpec(memory_space=pl.ANY)
    out_shape = tuple(jax.ShapeDtypeStruct((2,) + a.shape, a.dtype) for a in srcs)
    return pl.pallas_call(
        body, name=name, in_specs=[any_spec] * n, out_specs=tuple([any_spec] * n), out_shape=out_shape,
        scratch_shapes=[pltpu.SemaphoreType.DMA((n,)), pltpu.SemaphoreType.DMA((n,)), pltpu.SemaphoreType.DMA((n,))],
    )(*srcs)


SUM_ROWS = 128


def _sum_parts(parts, name):
    _, r, c = parts.shape

    def body(p_ref, o_ref):
        acc = p_ref[0]
        for j in range(1, N_DEV):
            acc = acc + p_ref[j]
        o_ref[...] = acc

    return pl.pallas_call(
        body, name=name, grid=(r // SUM_ROWS,),
        in_specs=[pl.BlockSpec((N_DEV, SUM_ROWS, c), lambda i: (0, i, 0))],
        out_specs=pl.BlockSpec((SUM_ROWS, c), lambda i: (i, 0)),
        out_shape=jax.ShapeDtypeStruct((r, c), F32),
    )(parts)


def _adamw_math(w, g, m, v):
    m = ADAM_B1 * m + (1.0 - ADAM_B1) * g
    v = ADAM_B2 * v + (1.0 - ADAM_B2) * (g * g)
    m_hat = m / (1.0 - ADAM_B1 ** ADAM_STEP)
    v_hat = v / (1.0 - ADAM_B2 ** ADAM_STEP)
    delta = -ADAM_LR * (m_hat / (jnp.sqrt(v_hat) + ADAM_EPS) + ADAM_WD * w)
    return delta, m, v


def _adamw(w, g, m, v, name):
    r, c = w.shape

    def body(w_ref, g_ref, m_ref, v_ref, d_ref, nm_ref, nv_ref):
        delta, nm, nv = _adamw_math(w_ref[...], g_ref[...], m_ref[...], v_ref[...])
        d_ref[...] = delta
        nm_ref[...] = nm
        nv_ref[...] = nv

    spec = pl.BlockSpec((SUM_ROWS, c), lambda i: (i, 0))
    shape = jax.ShapeDtypeStruct((r, c), F32)
    return pl.pallas_call(
        body, name=name, grid=(r // SUM_ROWS,), in_specs=[spec] * 4, out_specs=(spec,) * 3,
        out_shape=(shape,) * 3,
    )(w, g, m, v)


PACK_ROWS = 8


def _fold_heads(v):
    y = v[:, 0:LANES]
    for j in range(1, v.shape[1] // LANES):
        y = y + v[:, j * LANES:(j + 1) * LANES]
    return y + pltpu.roll(y, HEAD_DIM, 1)


def _small_adamw(recv, w_p, m_p, v_p):
    def body(r_ref, w_ref, m_ref, v_ref, g_ref, d_ref, nm_ref, nv_ref):
        tot = r_ref[0]
        for j in range(1, N_DEV):
            tot = tot + r_ref[j]
        row1 = tot[1:2, :]
        row2 = tot[2:3, :]
        pieces = [_fold_heads(row1[:, 0:WIDTH]), _fold_heads(row2[:, WIDTH:WIDTH + A_KV_WIDTH]),
                  _fold_heads(row1[:, WIDTH:2 * WIDTH]), _fold_heads(row2[:, 0:WIDTH]),
                  row2[:, WIDTH + A_KV_WIDTH:WIDTH + 2 * A_KV_WIDTH], jnp.zeros((1, 3 * LANES), F32)]
        g = jnp.concatenate([tot[0:1, :], jnp.concatenate(pieces, axis=1), jnp.zeros((PACK_ROWS - 2, D_MODEL), F32)],
                            axis=0)
        g_ref[...] = g
        delta, nm, nv = _adamw_math(w_ref[...], g, m_ref[...], v_ref[...])
        d_ref[...] = delta
        nm_ref[...] = nm
        nv_ref[...] = nv

    shape = jax.ShapeDtypeStruct((PACK_ROWS, D_MODEL), F32)
    return pl.pallas_call(body, name="small_adamw", out_shape=(shape,) * 4)(recv, w_p, m_p, v_p)


def _pack_small(norm_gain, q_a, k_a, q_b, k_b, sinks):
    def lane_pad(a):
        return jnp.pad(a, ((0, 0), (0, LANES - a.shape[1])))
    row1 = jnp.concatenate([lane_pad(q_a), lane_pad(k_a), lane_pad(q_b), lane_pad(k_b), lane_pad(sinks),
                            jnp.zeros((1, 3 * LANES), F32)], axis=1)
    return jnp.concatenate([norm_gain, row1, jnp.zeros((PACK_ROWS - 2, D_MODEL), F32)], axis=0)


def _unpack_small(p):
    return (p[0:1, :], p[1:2, 0:HEAD_DIM], p[1:2, LANES:LANES + HEAD_DIM], p[1:2, 2 * LANES:2 * LANES + HEAD_DIM],
            p[1:2, 3 * LANES:3 * LANES + HEAD_DIM], p[1:2, 4 * LANES:4 * LANES + HEADS])


def kernel(x, norm_gain, w_in, q_norm_a, k_norm_a, sinks_a, q_norm_b, k_norm_b, w_out, loss_target, m_norm_gain, m_w_in, m_q_norm_a, m_k_norm_a, m_sinks_a, m_q_norm_b, m_k_norm_b, m_w_out, v_norm_gain, v_w_in, v_q_norm_a, v_k_norm_a, v_sinks_a, v_q_norm_b, v_k_norm_b, v_w_out):
    c = lax.axis_index("c")
    in_cols = IN_WIDTH // N_CHIP
    out_rows = D_MODEL // N_CHIP
    in_half = D_MODEL // 2
    out_half = out_rows // 2

    w_in_half = lax.dynamic_slice_in_dim(w_in[0], c * in_half, in_half, axis=0).astype(BF16)
    w_out_half = lax.dynamic_slice_in_dim(w_out[0], c * out_half, out_half, axis=0).astype(BF16)
    w_in_all, w_out_all = _all_to_all([w_in_half, w_out_half], "gather_weights")
    w_in_bf = w_in_all.reshape(N_CHIP, D_MODEL, in_cols).transpose(1, 0, 2).reshape(D_MODEL, IN_WIDTH)
    w_out_bf = w_out_all.reshape(D_MODEL, D_MODEL)

    loss_part, gx, gw_in, gw_out, (dgain, dgqa, dgka, dsink, dgqb, dgkb) = _local_step(
        x[0], loss_target[0], norm_gain, w_in_bf, q_norm_a, k_norm_a, sinks_a, q_norm_b, k_norm_b, w_out_bf)

    gw_in_parts = gw_in.reshape(2, in_half, N_CHIP, in_cols).transpose(2, 0, 1, 3).reshape(N_DEV, in_half, in_cols)
    gw_out_parts = gw_out.reshape(N_DEV, out_half, D_MODEL)
    small = jnp.concatenate([
        dgain, jnp.concatenate([dgqa, dgqb], axis=1),
        jnp.concatenate([dgkb, dgka, dsink, jnp.zeros((1, D_MODEL - WIDTH - 2 * A_KV_WIDTH), F32)], axis=1),
        jnp.zeros((PACK_ROWS - 3, D_MODEL), F32)], axis=0)
    in_recv, out_recv, small_recv = _all_to_all([gw_in_parts, gw_out_parts, small], "scatter_grads")
    in_sum = _sum_parts(in_recv, "sum_grad_w_in")
    out_sum = _sum_parts(out_recv, "sum_grad_w_out")
    in_both, out_both = _pair_exchange([in_sum, out_sum], "pair_exchange")
    g_w_in = in_both.reshape(D_MODEL, in_cols)
    g_w_out = out_both.reshape(out_rows, D_MODEL)

    d_w_in, nm_w_in, nv_w_in = _adamw(w_in[0], g_w_in, m_w_in[0], v_w_in[0], "adamw_w_in")
    d_w_out, nm_w_out, nv_w_out = _adamw(w_out[0], g_w_out, m_w_out[0], v_w_out[0], "adamw_w_out")
    g_s, d_s, nm_s, nv_s = _small_adamw(
        small_recv,
        _pack_small(norm_gain, q_norm_a, k_norm_a, q_norm_b, k_norm_b, sinks_a),
        _pack_small(m_norm_gain, m_q_norm_a, m_k_norm_a, m_q_norm_b, m_k_norm_b, m_sinks_a),
        _pack_small(v_norm_gain, v_q_norm_a, v_k_norm_a, v_q_norm_b, v_k_norm_b, v_sinks_a))

    loss = lax.psum(loss_part[0, 0], ("x", "y", "c"))

    def leaves(small_packed, big_in, big_out):
        gain, qa, ka, qb, kb, sk = _unpack_small(small_packed)
        return (gain, big_in[None], qa, ka, sk, qb, kb, big_out[None])

    return ((loss, gx[None]) + leaves(g_s, g_w_in, g_w_out) + leaves(d_s, d_w_in, d_w_out)
            + leaves(nm_s, nm_w_in, nm_w_out) + leaves(nv_s, nv_w_in, nv_w_out))
```

```python
import functools

import jax
import jax.numpy as jnp
from jax import lax
from jax.experimental import pallas as pl
from jax.experimental.pallas import tpu as pltpu

F32 = jnp.float32
BF16 = jnp.bfloat16

D_MODEL = 1024
HEAD_DIM = 64
HEADS = 8
WIDTH = HEADS * HEAD_DIM
A_KV_WIDTH = 2 * HEAD_DIM
BLOCK = 128
LANES = 128
B_DILATIONS = (1, 4, 16)
A_MAX_DIST = 127
B_MAX_DIST = 128
ROPE_THETA = 10000.0
EPS = 1e-6
NEG = -1e30
SCALE = HEAD_DIM ** -0.5

IN_WIDTH = 3328
C_QA, C_KA, C_VA, C_GA, C_QB, C_KB, C_VB, C_GB, C_END = 0, 512, 640, 768, 1280, 1792, 2304, 2816, 3328

N_DEV = 8
N_CHIP = 4
MESH = pl.DeviceIdType.MESH

ADAM_LR = 0.001
ADAM_B1 = 0.9
ADAM_B2 = 0.999
ADAM_EPS = 1e-08
ADAM_WD = 0.01
ADAM_STEP = 10

ROW_TILE = 256
BWD_ROW_TILE = 128
ACC_COLS = 256
VMEM_LIMIT = 56 * 1024 * 1024


def _dot(a, b):
    return jnp.dot(a, b, preferred_element_type=F32)


def _dot_nt(a, b):
    return lax.dot_general(a, b, (((1,), (1,)), ((), ())), preferred_element_type=F32)


def _dot_tn(a, b):
    return lax.dot_general(a, b, (((0,), (0,)), ((), ())), preferred_element_type=F32)


def _head_sum(z, bd):
    hi = z.astype(BF16)
    lo = (z - hi.astype(F32)).astype(BF16)
    return _dot(hi, bd) + _dot(lo, bd)


def _swap_halves(t):
    w = t.shape[1]
    lane = lax.broadcasted_iota(jnp.int32, t.shape, 1)
    return jnp.where(lane % HEAD_DIM < HEAD_DIM // 2, pltpu.roll(t, w - 32, 1), pltpu.roll(t, 32, 1))


def _qknorm_rope(t, g, cos, sin_s, bd):
    r = lax.rsqrt(_head_sum(t * t, bd) * (1.0 / HEAD_DIM) + EPS)
    n = (t * r) * g
    return n * cos + _swap_halves(n) * sin_s


def _qknorm_rope_bwd(dout, t, g, cos, sin_s, bd):
    dn = dout * cos + _swap_halves(dout * sin_s)
    r = lax.rsqrt(_head_sum(t * t, bd) * (1.0 / HEAD_DIM) + EPS)
    tr = t * r
    u = dn * g
    dt = r * (u - tr * (_head_sum(u * tr, bd) * (1.0 / HEAD_DIM)))
    return dt, dn * tr


def _sigmoid(g):
    return 1.0 / (1.0 + jnp.exp(-g))


def _expand_heads(st):
    t = st.shape[0]
    lane = lax.broadcasted_iota(jnp.int32, (t, LANES), 1)
    chunks = []
    for c in range(WIDTH // LANES):
        chunks.append(jnp.where(lane < HEAD_DIM, st[:, 2 * c:2 * c + 1], st[:, 2 * c + 1:2 * c + 2]))
    return jnp.concatenate(chunks, axis=1)


def _reduce_heads(z):
    t = z.shape[0]
    lane = lax.broadcasted_iota(jnp.int32, (t, LANES), 1)
    out = jnp.zeros((t, LANES), F32)
    for c in range(WIDTH // LANES):
        zc = z[:, c * LANES:(c + 1) * LANES]
        for ph in range(2):
            s = jnp.sum(jnp.where((lane // HEAD_DIM) == ph, zc, 0.0), axis=-1, keepdims=True)
            out = jnp.where(lane == 2 * c + ph, s, out)
    return out


def _inproj(x2, gain, w_bf, cos, sin_s, gqa, gka, gqb, gkb, bd512, bd128):
    s = x2.shape[0]
    tm = ROW_TILE

    def body(x_ref, gain_ref, w_hbm, cos_ref, sin_ref, gqa_ref, gka_ref, gqb_ref, gkb_ref, bd512_ref, bd128_ref,
             qa_ref, ka_ref, va_ref, qb_ref, kb_ref, vb_ref,
             qa_raw_ref, ka_raw_ref, ga_ref, qb_raw_ref, kb_raw_ref, gb_ref, w_vmem):
        @pl.when(pl.program_id(0) == 0)
        def _():
            pltpu.sync_copy(w_hbm, w_vmem)

        xt = x_ref[...]
        r = lax.rsqrt(jnp.mean(xt * xt, axis=-1, keepdims=True) + EPS)
        h = ((xt * r) * gain_ref[...]).astype(BF16)
        cos1 = cos_ref[...]
        sin1 = sin_ref[...]
        cos4 = jnp.tile(cos1, (1, 4))
        sin4 = jnp.tile(sin1, (1, 4))

        def seg(a, b):
            return _dot(h, w_vmem[:, a:b])

        t = seg(C_QA, C_KA)
        qa_raw_ref[...] = t
        qa_ref[...] = (_qknorm_rope(t, gqa_ref[...], cos4, sin4, bd512_ref[...]) * SCALE).astype(BF16)
        t = seg(C_KA, C_VA)
        ka_raw_ref[...] = t
        ka_ref[...] = _qknorm_rope(t, gka_ref[...], cos1, sin1, bd128_ref[...]).astype(BF16)
        va_ref[...] = seg(C_VA, C_GA).astype(BF16)
        ga_ref[...] = seg(C_GA, C_QB)
        t = seg(C_QB, C_KB)
        qb_raw_ref[...] = t
        qb_ref[...] = (_qknorm_rope(t, gqb_ref[...], cos4, sin4, bd512_ref[...]) * SCALE).astype(BF16)
        t = seg(C_KB, C_VB)
        kb_raw_ref[...] = t
        kb_ref[...] = _qknorm_rope(t, gkb_ref[...], cos4, sin4, bd512_ref[...]).astype(BF16)
        vb_ref[...] = seg(C_VB, C_GB).astype(BF16)
        gb_ref[...] = seg(C_GB, C_END)

    def rows(w):
        return pl.BlockSpec((tm, w), lambda i: (i, 0))

    def whole(a):
        return pl.BlockSpec(a.shape, lambda i: (0, 0))

    sds = jax.ShapeDtypeStruct
    out_shape = (sds((s, WIDTH), BF16), sds((s, A_KV_WIDTH), BF16), sds((s, A_KV_WIDTH), BF16),
                 sds((s, WIDTH), BF16), sds((s, WIDTH), BF16), sds((s, WIDTH), BF16),
                 sds((s, WIDTH), F32), sds((s, A_KV_WIDTH), F32), sds((s, WIDTH), F32),
                 sds((s, WIDTH), F32), sds((s, WIDTH), F32), sds((s, WIDTH), F32))
    out_specs = (rows(WIDTH), rows(A_KV_WIDTH), rows(A_KV_WIDTH), rows(WIDTH), rows(WIDTH), rows(WIDTH),
                 rows(WIDTH), rows(A_KV_WIDTH), rows(WIDTH), rows(WIDTH), rows(WIDTH), rows(WIDTH))
    return pl.pallas_call(
        body, name="inproj_fwd", grid=(s // tm,),
        in_specs=[rows(D_MODEL), whole(gain), pl.BlockSpec(memory_space=pl.ANY), rows(LANES), rows(LANES),
                  whole(gqa), whole(gka), whole(gqb), whole(gkb), whole(bd512), whole(bd128)],
        out_specs=out_specs, out_shape=out_shape,
        scratch_shapes=[pltpu.VMEM((D_MODEL, IN_WIDTH), BF16)],
        compiler_params=pltpu.CompilerParams(dimension_semantics=("arbitrary",), vmem_limit_bytes=VMEM_LIMIT),
    )(x2, gain, w_bf, cos, sin_s, gqa, gka, gqb, gkb, bd512, bd128)


def _band_mask(i, max_dist):
    qi = lax.broadcasted_iota(jnp.int32, (BLOCK, 2 * BLOCK), 0)
    sj = lax.broadcasted_iota(jnp.int32, (BLOCK, 2 * BLOCK), 1)
    dist = qi - sj + BLOCK
    return (dist >= 0) & (dist <= max_dist) & ((sj >= BLOCK) | (i > 0))


def _kv_window(prev_ref, cur_ref, kc):
    sl = slice(kc * LANES, (kc + 1) * LANES)
    return jnp.concatenate([prev_ref[:, sl], cur_ref[:, sl]], axis=0)


def _swap_heads(a):
    return pltpu.roll(a.astype(F32), HEAD_DIM, 1).astype(a.dtype)


def _attn_fwd(q, k, v, sinks, *, max_dist, name):
    dil, ln, _ = q.shape
    kw = k.shape[2]
    gqa = kw == A_KV_WIDTH
    nb = ln // BLOCK
    with_sinks = sinks is not None

    def body(*refs):
        if with_sinks:
            q_ref, kp_ref, kc_ref, vp_ref, vc_ref, sink_ref, o_ref, lse_ref = refs
        else:
            q_ref, kp_ref, kc_ref, vp_ref, vc_ref, o_ref, lse_ref = refs
        valid = _band_mask(pl.program_id(1), max_dist)
        lane = lax.broadcasted_iota(jnp.int32, (1, LANES), 1)
        lse_ref[...] = jnp.zeros((BLOCK, LANES), F32)
        if gqa:
            k_plain, v_plain = _kv_window(kp_ref, kc_ref, 0), _kv_window(vp_ref, vc_ref, 0)
            k_swap, v_swap = _swap_heads(k_plain), _swap_heads(v_plain)
        for c in range(WIDTH // LANES):
            q2 = q_ref[:, c * LANES:(c + 1) * LANES]
            if not gqa:
                k2, v2 = _kv_window(kp_ref, kc_ref, c), _kv_window(vp_ref, vc_ref, c)
            o_pair = jnp.zeros((BLOCK, LANES), F32)
            for ph in range(2):
                h = 2 * c + ph
                if gqa:
                    kk, vv = (k_plain, v_plain) if (c // 2) == ph else (k_swap, v_swap)
                else:
                    kk, vv = k2, v2
                half = (lane // HEAD_DIM) == ph
                qm = jnp.where(half, q2, jnp.zeros_like(q2))
                sc = jnp.where(valid, _dot_nt(qm, kk), NEG)
                m = jnp.max(sc, axis=-1, keepdims=True)
                if with_sinks:
                    sk = sink_ref[0, h]
                    m = jnp.maximum(m, sk)
                p = jnp.exp(sc - m)
                l = jnp.sum(p, axis=-1, keepdims=True)
                if with_sinks:
                    l = l + jnp.exp(sk - m)
                o2 = _dot(p.astype(BF16), vv) / l
                o_pair = jnp.where(half, o2, o_pair)
                lse_ref[:, h:h + 1] = m + jnp.log(l)
            o_ref[:, c * LANES:(c + 1) * LANES] = o_pair

    def cur(w):
        return pl.BlockSpec((None, BLOCK, w), lambda r, i: (r, i, 0))

    def prev(w):
        return pl.BlockSpec((None, BLOCK, w), lambda r, i: (r, jnp.maximum(i - 1, 0), 0))

    in_specs = [cur(WIDTH), prev(kw), cur(kw), prev(kw), cur(kw)]
    args = [q, k, k, v, v]
    if with_sinks:
        in_specs.append(pl.BlockSpec(memory_space=pltpu.SMEM))
        args.append(sinks)
    return pl.pallas_call(
        body, name=name, grid=(dil, nb), in_specs=in_specs,
        out_specs=(cur(WIDTH), cur(LANES)),
        out_shape=(jax.ShapeDtypeStruct((dil, ln, WIDTH), F32), jax.ShapeDtypeStruct((dil, ln, LANES), F32)),
        compiler_params=pltpu.CompilerParams(dimension_semantics=("arbitrary", "arbitrary")),
    )(*args)


def _attn_bwd(q, k, v, do, lse, delta, *, max_dist, name):
    dil, ln, _ = q.shape
    kw = k.shape[2]
    gqa = kw == A_KV_WIDTH
    nb = ln // BLOCK
    n_kc = kw // LANES

    def body(q_ref, kp_ref, kc_ref, vp_ref, vc_ref, do_ref, lse_ref, dl_ref, dq_ref, dk_ref, dv_ref, ck_ref, cv_ref):
        i = pl.program_id(1)

        @pl.when(i == 0)
        def _():
            ck_ref[...] = jnp.zeros_like(ck_ref)
            cv_ref[...] = jnp.zeros_like(cv_ref)

        @pl.when(i < nb)
        def _():
            valid = _band_mask(i, max_dist)
            lane = lax.broadcasted_iota(jnp.int32, (1, LANES), 1)
            dk_acc = [jnp.zeros((2 * BLOCK, LANES), F32) for _ in range(n_kc)]
            dv_acc = [jnp.zeros((2 * BLOCK, LANES), F32) for _ in range(n_kc)]
            if gqa:
                k_plain, v_plain = _kv_window(kp_ref, kc_ref, 0), _kv_window(vp_ref, vc_ref, 0)
                k_swap, v_swap = _swap_heads(k_plain), _swap_heads(v_plain)
            for c in range(WIDTH // LANES):
                q2 = q_ref[:, c * LANES:(c + 1) * LANES]
                do2 = do_ref[:, c * LANES:(c + 1) * LANES]
                kc = 0 if gqa else c
                if not gqa:
                    k2, v2 = _kv_window(kp_ref, kc_ref, c), _kv_window(vp_ref, vc_ref, c)
                dq_pair = jnp.zeros((BLOCK, LANES), F32)
                for ph in range(2):
                    h = 2 * c + ph
                    swapped = gqa and (c // 2) != ph
                    if gqa:
                        kk, vv = (k_swap, v_swap) if swapped else (k_plain, v_plain)
                    else:
                        kk, vv = k2, v2
                    half = (lane // HEAD_DIM) == ph
                    qm = jnp.where(half, q2, jnp.zeros_like(q2))
                    dom = jnp.where(half, do2, jnp.zeros_like(do2))
                    sc = jnp.where(valid, _dot_nt(qm, kk), NEG)
                    p = jnp.exp(sc - lse_ref[:, h:h + 1])
                    dp = _dot_nt(dom, vv)
                    ds = (p * (dp - dl_ref[:, h:h + 1])).astype(BF16)
                    dq_pair = jnp.where(half, _dot(ds, kk), dq_pair)
                    dkc = _dot_tn(ds, qm)
                    dvc = _dot_tn(p.astype(BF16), dom)
                    if swapped:
                        dkc, dvc = _swap_heads(dkc), _swap_heads(dvc)
                    dk_acc[kc] = dk_acc[kc] + dkc
                    dv_acc[kc] = dv_acc[kc] + dvc
                dq_ref[:, c * LANES:(c + 1) * LANES] = dq_pair * SCALE
            for kc in range(n_kc):
                sl = slice(kc * LANES, (kc + 1) * LANES)
                dk_ref[:, sl] = ck_ref[:, sl] + dk_acc[kc][:BLOCK]
                dv_ref[:, sl] = cv_ref[:, sl] + dv_acc[kc][:BLOCK]
                ck_ref[:, sl] = dk_acc[kc][BLOCK:]
                cv_ref[:, sl] = dv_acc[kc][BLOCK:]

        @pl.when(i == nb)
        def _():
            dk_ref[...] = ck_ref[...]
            dv_ref[...] = cv_ref[...]

    def cur(w):
        return pl.BlockSpec((None, BLOCK, w), lambda r, i: (r, jnp.minimum(i, nb - 1), 0))

    def prev(w):
        return pl.BlockSpec((None, BLOCK, w), lambda r, i: (r, jnp.clip(i - 1, 0, nb - 1), 0))

    def lag(w):
        return pl.BlockSpec((None, BLOCK, w), lambda r, i: (r, jnp.maximum(i - 1, 0), 0))

    sds = jax.ShapeDtypeStruct
    return pl.pallas_call(
        body, name=name, grid=(dil, nb + 1),
        in_specs=[cur(WIDTH), prev(kw), cur(kw), prev(kw), cur(kw), cur(WIDTH), cur(LANES), cur(LANES)],
        out_specs=(cur(WIDTH), lag(kw), lag(kw)),
        out_shape=(sds((dil, ln, WIDTH), F32), sds((dil, ln, kw), F32), sds((dil, ln, kw), F32)),
        scratch_shapes=[pltpu.VMEM((BLOCK, kw), F32), pltpu.VMEM((BLOCK, kw), F32)],
        compiler_params=pltpu.CompilerParams(dimension_semantics=("arbitrary", "arbitrary")),
    )(q, k, k, v, v, do, lse, delta)


def _outproj(o_a, lse_a, o_b, lse_b, g_a, g_b, x2, tgt2, w_out_bf, sink_row):
    s = x2.shape[0]
    tm = ROW_TILE

    def body(oa_ref, lsea_ref, ob1_ref, ob2_ref, ob3_ref, ls1_ref, ls2_ref, ls3_ref, ga_ref, gb_ref, x_ref, t_ref,
             w_ref, sink_ref,
             dy_ref, doa_ref, dob_ref, dga_ref, dgb_ref, lseb_ref, dla_ref, dlb_ref, gw_ref, loss_ref, dsink_ref):
        i = pl.program_id(0)

        @pl.when(i == 0)
        def _():
            gw_ref[...] = jnp.zeros_like(gw_ref)
            loss_ref[...] = jnp.zeros_like(loss_ref)
            dsink_ref[...] = jnp.zeros_like(dsink_ref)

        ls = [ls1_ref[...], ls2_ref[...], ls3_ref[...]]
        mx = jnp.maximum(jnp.maximum(ls[0], ls[1]), ls[2])
        lse_b = mx + jnp.log(jnp.exp(ls[0] - mx) + jnp.exp(ls[1] - mx) + jnp.exp(ls[2] - mx))
        lane = lax.broadcasted_iota(jnp.int32, (tm, LANES), 1)
        lse_b = jnp.where(lane < HEADS, lse_b, 0.0)
        lseb_ref[...] = lse_b
        o_b = (_expand_heads(jnp.exp(ls[0] - lse_b)) * ob1_ref[...]
               + _expand_heads(jnp.exp(ls[1] - lse_b)) * ob2_ref[...]
               + _expand_heads(jnp.exp(ls[2] - lse_b)) * ob3_ref[...])
        o_a = oa_ref[...]
        g_a = ga_ref[...]
        g_b = gb_ref[...]
        sg_a = _sigmoid(g_a)
        sg_b = _sigmoid(g_b)
        silu_a = g_a * sg_a
        silu_b = g_b * sg_b
        mixed = jnp.concatenate([o_a * silu_a, o_b * silu_b], axis=1).astype(BF16)
        w = w_ref[...]
        y = x_ref[...] + _dot(mixed, w)
        diff = y - t_ref[...]
        loss_ref[...] += (0.5 / D_MODEL) * jnp.sum(diff * diff)
        dy = diff * (1.0 / D_MODEL)
        dy_ref[...] = dy
        dyb = dy.astype(BF16)
        gw_ref[...] += _dot_tn(mixed, dyb)
        dmixed = _dot_nt(dyb, w)
        dm_a = dmixed[:, :WIDTH]
        dm_b = dmixed[:, WIDTH:]
        do_a = dm_a * silu_a
        do_b = dm_b * silu_b
        doa_ref[...] = do_a.astype(BF16)
        dob_ref[...] = do_b.astype(BF16)
        dga_ref[...] = (dm_a * o_a * (sg_a * (1.0 + g_a * (1.0 - sg_a)))).astype(BF16)
        dgb_ref[...] = (dm_b * o_b * (sg_b * (1.0 + g_b * (1.0 - sg_b)))).astype(BF16)
        dl_a = _reduce_heads(do_a * o_a)
        dla_ref[...] = dl_a
        dlb_ref[...] = _reduce_heads(do_b * o_b)
        dsink_ref[...] -= jnp.sum(jnp.exp(sink_ref[...] - lsea_ref[...]) * dl_a, axis=0, keepdims=True)

    def rows(w):
        return pl.BlockSpec((tm, w), lambda i: (i, 0))

    def whole(shape):
        return pl.BlockSpec(shape, lambda i: (0, 0))

    sds = jax.ShapeDtypeStruct
    return pl.pallas_call(
        body, name="outproj_fwd_bwd", grid=(s // tm,),
        in_specs=[rows(WIDTH), rows(LANES), rows(WIDTH), rows(WIDTH), rows(WIDTH), rows(LANES), rows(LANES),
                  rows(LANES), rows(WIDTH), rows(WIDTH), rows(D_MODEL), rows(D_MODEL),
                  whole((D_MODEL, D_MODEL)), whole((1, LANES))],
        out_specs=(rows(D_MODEL), rows(WIDTH), rows(WIDTH), rows(WIDTH), rows(WIDTH), rows(LANES), rows(LANES),
                   rows(LANES), whole((D_MODEL, D_MODEL)), whole((1, LANES)), whole((1, LANES))),
        out_shape=(sds((s, D_MODEL), F32), sds((s, WIDTH), BF16), sds((s, WIDTH), BF16), sds((s, WIDTH), BF16),
                   sds((s, WIDTH), BF16), sds((s, LANES), F32), sds((s, LANES), F32), sds((s, LANES), F32),
                   sds((D_MODEL, D_MODEL), F32), sds((1, LANES), F32), sds((1, LANES), F32)),
        compiler_params=pltpu.CompilerParams(dimension_semantics=("arbitrary",), vmem_limit_bytes=VMEM_LIMIT),
    )(o_a, lse_a, o_b[0], o_b[1], o_b[2], lse_b[0], lse_b[1], lse_b[2], g_a, g_b, x2, tgt2, w_out_bf, sink_row)


def _inproj_bwd(x2, dy, gain, w_bf, cos, sin_s, gqa, gka, gqb, gkb, bd512, bd128,
                qa_raw, ka_raw, qb_raw, kb_raw, dq_a, dk_a, dv_a, dq_b, dk_b, dv_b, dg_a, dg_b):
    s = x2.shape[0]
    tm = BWD_ROW_TILE
    n_steps = s // tm

    def body(x_ref, dy_ref, gain_ref, w_hbm, cos_ref, sin_ref, gqa_ref, gka_ref, gqb_ref, gkb_ref, bd512_ref,
             bd128_ref, qa_raw_ref, ka_raw_ref, qb_raw_ref, kb_raw_ref, dqa_ref, dka_ref, dva_ref,
             dqb1_ref, dqb2_ref, dqb3_ref, dkb1_ref, dkb2_ref, dkb3_ref, dvb1_ref, dvb2_ref, dvb3_ref,
             dga_ref, dgb_ref,
             gx_ref, gw_hbm, dgain_ref, dgqa_ref, dgka_ref, dgqb_ref, dgkb_ref,
             w_vmem, acc_ref, dproj_ref):
        i = pl.program_id(0)

        @pl.when(i == 0)
        def _():
            pltpu.sync_copy(w_hbm, w_vmem)
            acc_ref[...] = jnp.zeros_like(acc_ref)
            dgain_ref[...] = jnp.zeros_like(dgain_ref)
            dgqa_ref[...] = jnp.zeros_like(dgqa_ref)
            dgka_ref[...] = jnp.zeros_like(dgka_ref)
            dgqb_ref[...] = jnp.zeros_like(dgqb_ref)
            dgkb_ref[...] = jnp.zeros_like(dgkb_ref)

        cos1 = cos_ref[...]
        sin1 = sin_ref[...]
        cos4 = jnp.tile(cos1, (1, 4))
        sin4 = jnp.tile(sin1, (1, 4))

        dt, dg = _qknorm_rope_bwd(dqa_ref[...], qa_raw_ref[...], gqa_ref[...], cos4, sin4, bd512_ref[...])
        dproj_ref[:, C_QA:C_KA] = dt.astype(BF16)
        dgqa_ref[...] += jnp.sum(dg, axis=0, keepdims=True)
        dt, dg = _qknorm_rope_bwd(dka_ref[...], ka_raw_ref[...], gka_ref[...], cos1, sin1, bd128_ref[...])
        dproj_ref[:, C_KA:C_VA] = dt.astype(BF16)
        dgka_ref[...] += jnp.sum(dg, axis=0, keepdims=True)
        dproj_ref[:, C_VA:C_GA] = dva_ref[...].astype(BF16)
        dproj_ref[:, C_GA:C_QB] = dga_ref[...]
        dq = (dqb1_ref[...] + dqb2_ref[...]) + dqb3_ref[...]
        dt, dg = _qknorm_rope_bwd(dq, qb_raw_ref[...], gqb_ref[...], cos4, sin4, bd512_ref[...])
        dproj_ref[:, C_QB:C_KB] = dt.astype(BF16)
        dgqb_ref[...] += jnp.sum(dg, axis=0, keepdims=True)
        dk = (dkb1_ref[...] + dkb2_ref[...]) + dkb3_ref[...]
        dt, dg = _qknorm_rope_bwd(dk, kb_raw_ref[...], gkb_ref[...], cos4, sin4, bd512_ref[...])
        dproj_ref[:, C_KB:C_VB] = dt.astype(BF16)
        dgkb_ref[...] += jnp.sum(dg, axis=0, keepdims=True)
        dproj_ref[:, C_VB:C_GB] = ((dvb1_ref[...] + dvb2_ref[...]) + dvb3_ref[...]).astype(BF16)
        dproj_ref[:, C_GB:C_END] = dgb_ref[...]

        xt = x_ref[...]
        gain_row = gain_ref[...]
        r = lax.rsqrt(jnp.mean(xt * xt, axis=-1, keepdims=True) + EPS)
        xr = xt * r
        h = (xr * gain_row).astype(BF16)
        for a in range(0, IN_WIDTH, ACC_COLS):
            acc_ref[:, a:a + ACC_COLS] += _dot_tn(h, dproj_ref[:, a:a + ACC_COLS])
        dh = _dot_nt(dproj_ref[...], w_vmem[...])
        dgain_ref[...] += jnp.sum(dh * xr, axis=0, keepdims=True)
        u = dh * gain_row
        gx_ref[...] = dy_ref[...] + r * (u - xr * jnp.mean(u * xr, axis=-1, keepdims=True))

        @pl.when(i == n_steps - 1)
        def _():
            pltpu.sync_copy(acc_ref, gw_hbm)

    def rows(w):
        return pl.BlockSpec((tm, w), lambda i: (i, 0))

    def whole(a):
        return pl.BlockSpec(a.shape, lambda i: (0, 0))

    def acc_row(w):
        return pl.BlockSpec((1, w), lambda i: (0, 0))

    sds = jax.ShapeDtypeStruct
    any_spec = pl.BlockSpec(memory_space=pl.ANY)
    return pl.pallas_call(
        body, name="inproj_bwd", grid=(n_steps,),
        in_specs=[rows(D_MODEL), rows(D_MODEL), whole(gain), any_spec, rows(LANES), rows(LANES),
                  whole(gqa), whole(gka), whole(gqb), whole(gkb), whole(bd512), whole(bd128),
                  rows(WIDTH), rows(A_KV_WIDTH), rows(WIDTH), rows(WIDTH),
                  rows(WIDTH), rows(A_KV_WIDTH), rows(A_KV_WIDTH)] + [rows(WIDTH)] * 9 + [rows(WIDTH), rows(WIDTH)],
        out_specs=(rows(D_MODEL), any_spec, acc_row(D_MODEL), acc_row(WIDTH), acc_row(A_KV_WIDTH), acc_row(WIDTH),
                   acc_row(WIDTH)),
        out_shape=(sds((s, D_MODEL), F32), sds((D_MODEL, IN_WIDTH), F32), sds((1, D_MODEL), F32),
                   sds((1, WIDTH), F32), sds((1, A_KV_WIDTH), F32), sds((1, WIDTH), F32), sds((1, WIDTH), F32)),
        scratch_shapes=[pltpu.VMEM((D_MODEL, IN_WIDTH), BF16), pltpu.VMEM((D_MODEL, IN_WIDTH), F32),
                        pltpu.VMEM((tm, IN_WIDTH), BF16)],
        compiler_params=pltpu.CompilerParams(dimension_semantics=("arbitrary",), vmem_limit_bytes=VMEM_LIMIT),
    )(x2, dy, gain, w_bf, cos, sin_s, gqa, gka, gqb, gkb, bd512, bd128, qa_raw, ka_raw, qb_raw, kb_raw,
      dq_a, dk_a, dv_a, dq_b[0], dq_b[1], dq_b[2], dk_b[0], dk_b[1], dk_b[2], dv_b[0], dv_b[1], dv_b[2],
      dg_a, dg_b)


def _fold(a, dil):
    s, w = a.shape
    if dil == 1:
        return a[None]
    return a.reshape(s // dil, dil, w).transpose(1, 0, 2)


def _unfold(a):
    dil, ln, w = a.shape
    if dil == 1:
        return a[0]
    return a.transpose(1, 0, 2).reshape(dil * ln, w)


def _rope_tables(s):
    half = HEAD_DIM // 2
    inv = ROPE_THETA ** (-jnp.arange(half, dtype=F32) / half)
    ang = jnp.arange(s).astype(F32)[:, None] * inv[None, :]
    cos = jnp.cos(ang)
    sin = jnp.sin(ang)
    return jnp.tile(cos, (1, 4)), jnp.concatenate([-sin, sin, -sin, sin], axis=1)


def _block_diag_ones(w):
    idx = jnp.arange(w) // HEAD_DIM
    return (idx[:, None] == idx[None, :]).astype(BF16)


def _local_step(x2, tgt2, norm_gain, w_in_bf, q_norm_a, k_norm_a, sinks_a, q_norm_b, k_norm_b, w_out_bf):
    s = x2.shape[0]
    cos, sin_s = _rope_tables(s)
    bd512, bd128 = _block_diag_ones(WIDTH), _block_diag_ones(A_KV_WIDTH)
    gqa = jnp.tile(q_norm_a, (1, HEADS))
    gka = jnp.tile(k_norm_a, (1, 2))
    gqb = jnp.tile(q_norm_b, (1, HEADS))
    gkb = jnp.tile(k_norm_b, (1, HEADS))
    sink_row = jnp.pad(sinks_a, ((0, 0), (0, LANES - HEADS)))

    (qa, ka, va, qb, kb, vb, qa_raw, ka_raw, g_a, qb_raw, kb_raw, g_b) = _inproj(
        x2, norm_gain, w_in_bf, cos, sin_s, gqa, gka, gqb, gkb, bd512, bd128)

    o_a, lse_a = _attn_fwd(qa[None], ka[None], va[None], sinks_a, max_dist=A_MAX_DIST, name="attn_a_fwd")
    o_a, lse_a = o_a[0], lse_a[0]
    qb_f = [_fold(qb, d) for d in B_DILATIONS]
    kb_f = [_fold(kb, d) for d in B_DILATIONS]
    vb_f = [_fold(vb, d) for d in B_DILATIONS]
    o_b, lse_p = [], []
    for n, d in enumerate(B_DILATIONS):
        o, l = _attn_fwd(qb_f[n], kb_f[n], vb_f[n], None, max_dist=B_MAX_DIST, name="attn_b%d_fwd" % d)
        o_b.append(_unfold(o))
        lse_p.append(_unfold(l))

    (dy, do_a, do_b, dg_a, dg_b, lse_b, dl_a, dl_b, gw_out, loss_part, dsink) = _outproj(
        o_a, lse_a, o_b, lse_p, g_a, g_b, x2, tgt2, w_out_bf, sink_row)

    dq_a, dk_a, dv_a = _attn_bwd(qa[None], ka[None], va[None], do_a[None], lse_a[None], dl_a[None],
                                 max_dist=A_MAX_DIST, name="attn_a_bwd")
    dq_b, dk_b, dv_b = [], [], []
    for n, d in enumerate(B_DILATIONS):
        dq, dk, dv = _attn_bwd(qb_f[n], kb_f[n], vb_f[n], _fold(do_b, d), _fold(lse_b, d), _fold(dl_b, d),
                               max_dist=B_MAX_DIST, name="attn_b%d_bwd" % d)
        dq_b.append(_unfold(dq))
        dk_b.append(_unfold(dk))
        dv_b.append(_unfold(dv))

    gx, gw_in, dgain, dgqa, dgka, dgqb, dgkb = _inproj_bwd(
        x2, dy, norm_gain, w_in_bf, cos, sin_s, gqa, gka, gqb, gkb, bd512, bd128,
        qa_raw, ka_raw, qb_raw, kb_raw, dq_a[0], dk_a[0], dv_a[0], dq_b, dk_b, dv_b, dg_a, dg_b)
    return loss_part, gx, gw_in, gw_out, (dgain, dgqa, dgka, dsink, dgqb, dgkb)


def _position():
    return lax.axis_index("x"), lax.axis_index("y"), lax.axis_index("c")


def _all_to_all(srcs, name):
    n = len(srcs)
    bcast = [a.ndim == 2 for a in srcs]

    def body(*refs):
        src_refs, dst_refs = refs[:n], refs[n:2 * n]
        send_sems, recv_sems, local_sems = refs[2 * n:]
        x, y, c = _position()
        me = 4 * x + 2 * y + c
        copies = []
        for k in range(n):
            own = src_refs[k] if bcast[k] else src_refs[k].at[me]
            local = pltpu.make_async_copy(own, dst_refs[k].at[me], local_sems.at[k])
            local.start()
            copies.append(local)
        sends, arrivals = [], []
        for d in range(1, N_DEV):
            px, py, pc = x ^ (d >> 2), y ^ ((d >> 1) & 1), c ^ (d & 1)
            peer = 4 * px + 2 * py + pc
            for k in range(n):
                sem = k * N_DEV + d
                send = pltpu.make_async_remote_copy(
                    src_ref=src_refs[k] if bcast[k] else src_refs[k].at[peer], dst_ref=dst_refs[k].at[me],
                    send_sem=send_sems.at[sem], recv_sem=recv_sems.at[sem],
                    device_id=(px, py, pc), device_id_type=MESH)
                send.start()
                sends.append(send)
                arrivals.append(pltpu.make_async_remote_copy(
                    src_ref=src_refs[k] if bcast[k] else src_refs[k].at[me], dst_ref=dst_refs[k].at[peer],
                    send_sem=send_sems.at[sem], recv_sem=recv_sems.at[sem],
                    device_id=(px, py, pc), device_id_type=MESH))
        for arrival in arrivals:
            arrival.wait_recv()
        for send in sends:
            send.wait_send()
        for local in copies:
            local.wait()

    any_spec = pl.BlockSpec(memory_space=pl.ANY)
    out_shape = tuple(jax.ShapeDtypeStruct((N_DEV,) + a.shape[-2:], a.dtype) for a in srcs)
    return pl.pallas_call(
        body, name=name, in_specs=[any_spec] * n, out_specs=tuple([any_spec] * n), out_shape=out_shape,
        scratch_shapes=[pltpu.SemaphoreType.DMA((n * N_DEV,)), pltpu.SemaphoreType.DMA((n * N_DEV,)),
                        pltpu.SemaphoreType.DMA((n,))],
    )(*srcs)


PAIR_CHUNKS = 4
GATHER_CHUNKS = 2


def _pair_exchange(srcs, name):
    n = len(srcs)
    ch = PAIR_CHUNKS

    def body(*refs):
        src_refs, dst_refs = refs[:n], refs[n:2 * n]
        send_sems, recv_sems, local_sems = refs[2 * n:]
        x, y, c = _position()
        copies, sends, arrivals = [], [], []
        for k in range(n):
            local = pltpu.make_async_copy(src_refs[k], dst_refs[k].at[c], local_sems.at[k])
            local.start()
            copies.append(local)
            rc = srcs[k].shape[0] // ch
            for j in range(ch):
                rows = pl.ds(j * rc, rc)
                send = pltpu.make_async_remote_copy(
                    src_ref=src_refs[k].at[rows], dst_ref=dst_refs[k].at[c, rows],
                    send_sem=send_sems.at[k * ch + j], recv_sem=recv_sems.at[k * ch + j],
                    device_id=(x, y, 1 - c), device_id_type=MESH)
                send.start()
                sends.append(send)
                arrivals.append(pltpu.make_async_remote_copy(
                    src_ref=src_refs[k].at[rows], dst_ref=dst_refs[k].at[1 - c, rows],
                    send_sem=send_sems.at[k * ch + j], recv_sem=recv_sems.at[k * ch + j],
                    device_id=(x, y, 1 - c), device_id_type=MESH))
        for arrival in arrivals:
            arrival.wait_recv()
        for send in sends:
            send.wait_send()
        for local in copies:
            local.wait()

    any_spec = pl.BlockSpec(memory_space=pl.ANY)
    out_shape = tuple(jax.ShapeDtypeStruct((2,) + a.shape, a.dtype) for a in srcs)
    return pl.pallas_call(
        body, name=name, in_specs=[any_spec] * n, out_specs=tuple([any_spec] * n), out_shape=out_shape,
        scratch_shapes=[pltpu.SemaphoreType.DMA((n * ch,)), pltpu.SemaphoreType.DMA((n * ch,)),
                        pltpu.SemaphoreType.DMA((n,))],
    )(*srcs)


def _gather_weights(blocks, name):
    n = len(blocks)
    ch = GATHER_CHUNKS
    n_sems = n * (N_CHIP - 1) * ch

    def body(*refs):
        src_refs, dst_refs = refs[:n], refs[n:2 * n]
        ici_send, ici_recv, d2d_send, d2d_recv, local_sems = refs[2 * n:]
        x, y, c = _position()
        b = 2 * x + y
        copies = []
        for k in range(n):
            local = pltpu.make_async_copy(src_refs[k], dst_refs[k].at[b], local_sems.at[k])
            local.start()
            copies.append(local)

        def rows(k, core, j):
            half = blocks[k].shape[0] // 2
            return pl.ds(core * half + j * (half // ch), half // ch)

        plan = []
        for d in range(1, N_CHIP):
            px, py = x ^ (d >> 1), y ^ (d & 1)
            for j in range(ch):
                for k in range(n):
                    plan.append((px, py, 2 * px + py, k, j, ((d - 1) * ch + j) * n + k))
        sends = []
        for px, py, pb, k, j, sem in plan:
            send = pltpu.make_async_remote_copy(
                src_ref=src_refs[k].at[rows(k, c, j)], dst_ref=dst_refs[k].at[b, rows(k, c, j)],
                send_sem=ici_send.at[sem], recv_sem=ici_recv.at[sem], device_id=(px, py, c), device_id_type=MESH)
            send.start()
            sends.append(send)
        for px, py, pb, k, j, sem in plan:
            landed = dst_refs[k].at[pb, rows(k, c, j)]
            pltpu.make_async_remote_copy(
                src_ref=landed, dst_ref=landed, send_sem=ici_send.at[sem], recv_sem=ici_recv.at[sem],
                device_id=(px, py, c), device_id_type=MESH).wait_recv()
            forward = pltpu.make_async_remote_copy(
                src_ref=landed, dst_ref=landed, send_sem=d2d_send.at[sem], recv_sem=d2d_recv.at[sem],
                device_id=(x, y, 1 - c), device_id_type=MESH)
            forward.start()
            sends.append(forward)
        for px, py, pb, k, j, sem in plan:
            passed = dst_refs[k].at[pb, rows(k, 1 - c, j)]
            pltpu.make_async_remote_copy(
                src_ref=passed, dst_ref=passed, send_sem=d2d_send.at[sem], recv_sem=d2d_recv.at[sem],
                device_id=(x, y, 1 - c), device_id_type=MESH).wait_recv()
        for send in sends:
            send.wait_send()
        for local in copies:
            local.wait()

    any_spec = pl.BlockSpec(memory_space=pl.ANY)
    out_shape = tuple(jax.ShapeDtypeStruct((N_CHIP,) + a.shape, a.dtype) for a in blocks)
    return pl.pallas_call(
        body, name=name, in_specs=[any_spec] * n, out_specs=tuple([any_spec] * n), out_shape=out_shape,
        scratch_shapes=[pltpu.SemaphoreType.DMA((n_sems,)) for _ in range(4)] + [pltpu.SemaphoreType.DMA((n,))],
    )(*blocks)


SUM_ROWS = 128


def _sum_parts(parts, name):
    _, r, c = parts.shape

    def body(p_ref, o_ref):
        acc = p_ref[0].astype(F32)
        for j in range(1, N_DEV):
            acc = acc + p_ref[j].astype(F32)
        o_ref[...] = acc

    return pl.pallas_call(
        body, name=name, grid=(r // SUM_ROWS,),
        in_specs=[pl.BlockSpec((N_DEV, SUM_ROWS, c), lambda i: (0, i, 0))],
        out_specs=pl.BlockSpec((SUM_ROWS, c), lambda i: (i, 0)),
        out_shape=jax.ShapeDtypeStruct((r, c), F32),
    )(parts)


def _adamw_math(w, g, m, v):
    m = ADAM_B1 * m + (1.0 - ADAM_B1) * g
    v = ADAM_B2 * v + (1.0 - ADAM_B2) * (g * g)
    m_hat = m / (1.0 - ADAM_B1 ** ADAM_STEP)
    v_hat = v / (1.0 - ADAM_B2 ** ADAM_STEP)
    delta = -ADAM_LR * (m_hat / (jnp.sqrt(v_hat) + ADAM_EPS) + ADAM_WD * w)
    return delta, m, v


def _adamw(w, g, m, v, name):
    r, c = w.shape

    def body(w_ref, g_ref, m_ref, v_ref, d_ref, nm_ref, nv_ref):
        delta, nm, nv = _adamw_math(w_ref[...], g_ref[...], m_ref[...], v_ref[...])
        d_ref[...] = delta
        nm_ref[...] = nm
        nv_ref[...] = nv

    spec = pl.BlockSpec((SUM_ROWS, c), lambda i: (i, 0))
    shape = jax.ShapeDtypeStruct((r, c), F32)
    return pl.pallas_call(
        body, name=name, grid=(r // SUM_ROWS,), in_specs=[spec] * 4, out_specs=(spec,) * 3,
        out_shape=(shape,) * 3,
    )(w, g, m, v)


PACK_ROWS = 8


def _fold_heads(v):
    y = v[:, 0:LANES]
    for j in range(1, v.shape[1] // LANES):
        y = y + v[:, j * LANES:(j + 1) * LANES]
    return y + pltpu.roll(y, HEAD_DIM, 1)


def _small_adamw(recv, w_p, m_p, v_p):
    def body(r_ref, w_ref, m_ref, v_ref, g_ref, d_ref, nm_ref, nv_ref):
        tot = r_ref[0]
        for j in range(1, N_DEV):
            tot = tot + r_ref[j]
        row1 = tot[1:2, :]
        row2 = tot[2:3, :]
        pieces = [_fold_heads(row1[:, 0:WIDTH]), _fold_heads(row2[:, WIDTH:WIDTH + A_KV_WIDTH]),
                  _fold_heads(row1[:, WIDTH:2 * WIDTH]), _fold_heads(row2[:, 0:WIDTH]),
                  row2[:, WIDTH + A_KV_WIDTH:WIDTH + 2 * A_KV_WIDTH], jnp.zeros((1, 3 * LANES), F32)]
        g = jnp.concatenate([tot[0:1, :], jnp.concatenate(pieces, axis=1), jnp.zeros((PACK_ROWS - 2, D_MODEL), F32)],
                            axis=0)
        g_ref[...] = g
        delta, nm, nv = _adamw_math(w_ref[...], g, m_ref[...], v_ref[...])
        d_ref[...] = delta
        nm_ref[...] = nm
        nv_ref[...] = nv

    shape = jax.ShapeDtypeStruct((PACK_ROWS, D_MODEL), F32)
    return pl.pallas_call(body, name="small_adamw", out_shape=(shape,) * 4)(recv, w_p, m_p, v_p)


def _pack_small(norm_gain, q_a, k_a, q_b, k_b, sinks):
    def lane_pad(a):
        return jnp.pad(a, ((0, 0), (0, LANES - a.shape[1])))
    row1 = jnp.concatenate([lane_pad(q_a), lane_pad(k_a), lane_pad(q_b), lane_pad(k_b), lane_pad(sinks),
                            jnp.zeros((1, 3 * LANES), F32)], axis=1)
    return jnp.concatenate([norm_gain, row1, jnp.zeros((PACK_ROWS - 2, D_MODEL), F32)], axis=0)


def _unpack_small(p):
    return (p[0:1, :], p[1:2, 0:HEAD_DIM], p[1:2, LANES:LANES + HEAD_DIM], p[1:2, 2 * LANES:2 * LANES + HEAD_DIM],
            p[1:2, 3 * LANES:3 * LANES + HEAD_DIM], p[1:2, 4 * LANES:4 * LANES + HEADS])


def kernel(x, norm_gain, w_in, q_norm_a, k_norm_a, sinks_a, q_norm_b, k_norm_b, w_out, loss_target, m_norm_gain, m_w_in, m_q_norm_a, m_k_norm_a, m_sinks_a, m_q_norm_b, m_k_norm_b, m_w_out, v_norm_gain, v_w_in, v_q_norm_a, v_k_norm_a, v_sinks_a, v_q_norm_b, v_k_norm_b, v_w_out):
    in_cols = IN_WIDTH // N_CHIP
    out_rows = D_MODEL // N_CHIP
    in_half = D_MODEL // 2
    out_half = out_rows // 2

    w_in_all, w_out_all = _gather_weights([w_in[0].astype(BF16), w_out[0].astype(BF16)], "gather_weights")
    w_in_bf = w_in_all.transpose(1, 0, 2).reshape(D_MODEL, IN_WIDTH)
    w_out_bf = w_out_all.reshape(D_MODEL, D_MODEL)

    loss_part, gx, gw_in, gw_out, (dgain, dgqa, dgka, dsink, dgqb, dgkb) = _local_step(
        x[0], loss_target[0], norm_gain, w_in_bf, q_norm_a, k_norm_a, sinks_a, q_norm_b, k_norm_b, w_out_bf)

    gw_in_parts = gw_in.reshape(2, in_half, N_CHIP, in_cols).transpose(2, 0, 1, 3).reshape(
        N_DEV, in_half, in_cols).astype(BF16)
    gw_out_parts = gw_out.reshape(N_DEV, out_half, D_MODEL).astype(BF16)
    small = jnp.concatenate([
        dgain, jnp.concatenate([dgqa, dgqb], axis=1),
        jnp.concatenate([dgkb, dgka, dsink, jnp.zeros((1, D_MODEL - WIDTH - 2 * A_KV_WIDTH), F32)], axis=1),
        jnp.zeros((PACK_ROWS - 3, D_MODEL), F32)], axis=0)
    in_recv, out_recv, small_recv = _all_to_all([gw_in_parts, gw_out_parts, small], "scatter_grads")
    in_sum = _sum_parts(in_recv, "sum_grad_w_in")
    out_sum = _sum_parts(out_recv, "sum_grad_w_out")
    in_both, out_both = _pair_exchange([in_sum, out_sum], "pair_exchange")
    g_w_in = in_both.reshape(D_MODEL, in_cols)
    g_w_out = out_both.reshape(out_rows, D_MODEL)

    d_w_in, nm_w_in, nv_w_in = _adamw(w_in[0], g_w_in, m_w_in[0], v_w_in[0], "adamw_w_in")
    d_w_out, nm_w_out, nv_w_out = _adamw(w_out[0], g_w_out, m_w_out[0], v_w_out[0], "adamw_w_out")
    g_s, d_s, nm_s, nv_s = _small_adamw(
        small_recv,
        _pack_small(norm_gain, q_norm_a, k_norm_a, q_norm_b, k_norm_b, sinks_a),
        _pack_small(m_norm_gain, m_q_norm_a, m_k_norm_a, m_q_norm_b, m_k_norm_b, m_sinks_a),
        _pack_small(v_norm_gain, v_q_norm_a, v_k_norm_a, v_q_norm_b, v_k_norm_b, v_sinks_a))

    loss = lax.psum(loss_part[0, 0], ("x", "y", "c"))

    def leaves(small_packed, big_in, big_out):
        gain, qa, ka, qb, kb, sk = _unpack_small(small_packed)
        return (gain, big_in[None], qa, ka, sk, qb, kb, big_out[None])

    return ((loss, gx[None]) + leaves(g_s, g_w_in, g_w_out) + leaves(d_s, d_w_in, d_w_out)
            + leaves(nm_s, nm_w_in, nm_w_out) + leaves(nv_s, nv_w_in, nv_w_out))
```

```python
import functools

import jax
import jax.numpy as jnp
from jax import lax
from jax.experimental import pallas as pl
from jax.experimental.pallas import tpu as pltpu

F32 = jnp.float32
BF16 = jnp.bfloat16

D_MODEL = 1024
HEAD_DIM = 64
HEADS = 8
WIDTH = HEADS * HEAD_DIM
A_KV_WIDTH = 2 * HEAD_DIM
BLOCK = 128
LANES = 128
B_DILATIONS = (1, 4, 16)
A_MAX_DIST = 127
B_MAX_DIST = 128
ROPE_THETA = 10000.0
EPS = 1e-6
NEG = -1e30
SCALE = HEAD_DIM ** -0.5

IN_WIDTH = 3328
C_QA, C_KA, C_VA, C_GA, C_QB, C_KB, C_VB, C_GB, C_END = 0, 512, 640, 768, 1280, 1792, 2304, 2816, 3328

N_DEV = 8
N_CHIP = 4
MESH = pl.DeviceIdType.MESH

ADAM_LR = 0.001
ADAM_B1 = 0.9
ADAM_B2 = 0.999
ADAM_EPS = 1e-08
ADAM_WD = 0.01
ADAM_STEP = 10

ROW_TILE = 256
BWD_ROW_TILE = 128
ACC_COLS = 256
VMEM_LIMIT = 56 * 1024 * 1024


def _dot(a, b):
    return jnp.dot(a, b, preferred_element_type=F32)


def _dot_nt(a, b):
    return lax.dot_general(a, b, (((1,), (1,)), ((), ())), preferred_element_type=F32)


def _dot_tn(a, b):
    return lax.dot_general(a, b, (((0,), (0,)), ((), ())), preferred_element_type=F32)


def _head_sum(z, bd):
    hi = z.astype(BF16)
    lo = (z - hi.astype(F32)).astype(BF16)
    return _dot(hi, bd) + _dot(lo, bd)


def _swap_halves(t):
    w = t.shape[1]
    lane = lax.broadcasted_iota(jnp.int32, t.shape, 1)
    return jnp.where(lane % HEAD_DIM < HEAD_DIM // 2, pltpu.roll(t, w - 32, 1), pltpu.roll(t, 32, 1))


def _qknorm_rope(t, g, cos, sin_s, bd):
    r = lax.rsqrt(_head_sum(t * t, bd) * (1.0 / HEAD_DIM) + EPS)
    n = (t * r) * g
    return n * cos + _swap_halves(n) * sin_s


def _qknorm_rope_bwd(dout, t, g, cos, sin_s, bd):
    dn = dout * cos + _swap_halves(dout * sin_s)
    r = lax.rsqrt(_head_sum(t * t, bd) * (1.0 / HEAD_DIM) + EPS)
    tr = t * r
    u = dn * g
    dt = r * (u - tr * (_head_sum(u * tr, bd) * (1.0 / HEAD_DIM)))
    return dt, dn * tr


def _sigmoid(g):
    return 1.0 / (1.0 + jnp.exp(-g))


def _expand_heads(st):
    t = st.shape[0]
    lane = lax.broadcasted_iota(jnp.int32, (t, LANES), 1)
    chunks = []
    for c in range(WIDTH // LANES):
        chunks.append(jnp.where(lane < HEAD_DIM, st[:, 2 * c:2 * c + 1], st[:, 2 * c + 1:2 * c + 2]))
    return jnp.concatenate(chunks, axis=1)


def _reduce_heads(z):
    t = z.shape[0]
    lane = lax.broadcasted_iota(jnp.int32, (t, LANES), 1)
    out = jnp.zeros((t, LANES), F32)
    for c in range(WIDTH // LANES):
        zc = z[:, c * LANES:(c + 1) * LANES]
        for ph in range(2):
            s = jnp.sum(jnp.where((lane // HEAD_DIM) == ph, zc, 0.0), axis=-1, keepdims=True)
            out = jnp.where(lane == 2 * c + ph, s, out)
    return out


def _inproj(x2, gain, w_bf, cos, sin_s, gqa, gka, gqb, gkb, bd512, bd128):
    s = x2.shape[0]
    tm = ROW_TILE

    def body(x_ref, gain_ref, w_hbm, cos_ref, sin_ref, gqa_ref, gka_ref, gqb_ref, gkb_ref, bd512_ref, bd128_ref,
             qa_ref, ka_ref, va_ref, qb_ref, kb_ref, vb_ref,
             qa_raw_ref, ka_raw_ref, ga_ref, qb_raw_ref, kb_raw_ref, gb_ref, w_vmem):
        @pl.when(pl.program_id(0) == 0)
        def _():
            pltpu.sync_copy(w_hbm, w_vmem)

        xt = x_ref[...]
        r = lax.rsqrt(jnp.mean(xt * xt, axis=-1, keepdims=True) + EPS)
        h = ((xt * r) * gain_ref[...]).astype(BF16)
        cos1 = cos_ref[...]
        sin1 = sin_ref[...]
        cos4 = jnp.tile(cos1, (1, 4))
        sin4 = jnp.tile(sin1, (1, 4))

        def seg(a, b):
            return _dot(h, w_vmem[:, a:b])

        t = seg(C_QA, C_KA)
        qa_raw_ref[...] = t
        qa_ref[...] = (_qknorm_rope(t, gqa_ref[...], cos4, sin4, bd512_ref[...]) * SCALE).astype(BF16)
        t = seg(C_KA, C_VA)
        ka_raw_ref[...] = t
        ka_ref[...] = _qknorm_rope(t, gka_ref[...], cos1, sin1, bd128_ref[...]).astype(BF16)
        va_ref[...] = seg(C_VA, C_GA).astype(BF16)
        ga_ref[...] = seg(C_GA, C_QB)
        t = seg(C_QB, C_KB)
        qb_raw_ref[...] = t
        qb_ref[...] = (_qknorm_rope(t, gqb_ref[...], cos4, sin4, bd512_ref[...]) * SCALE).astype(BF16)
        t = seg(C_KB, C_VB)
        kb_raw_ref[...] = t
        kb_ref[...] = _qknorm_rope(t, gkb_ref[...], cos4, sin4, bd512_ref[...]).astype(BF16)
        vb_ref[...] = seg(C_VB, C_GB).astype(BF16)
        gb_ref[...] = seg(C_GB, C_END)

    def rows(w):
        return pl.BlockSpec((tm, w), lambda i: (i, 0))

    def whole(a):
        return pl.BlockSpec(a.shape, lambda i: (0, 0))

    sds = jax.ShapeDtypeStruct
    out_shape = (sds((s, WIDTH), BF16), sds((s, A_KV_WIDTH), BF16), sds((s, A_KV_WIDTH), BF16),
                 sds((s, WIDTH), BF16), sds((s, WIDTH), BF16), sds((s, WIDTH), BF16),
                 sds((s, WIDTH), F32), sds((s, A_KV_WIDTH), F32), sds((s, WIDTH), F32),
                 sds((s, WIDTH), F32), sds((s, WIDTH), F32), sds((s, WIDTH), F32))
    out_specs = (rows(WIDTH), rows(A_KV_WIDTH), rows(A_KV_WIDTH), rows(WIDTH), rows(WIDTH), rows(WIDTH),
                 rows(WIDTH), rows(A_KV_WIDTH), rows(WIDTH), rows(WIDTH), rows(WIDTH), rows(WIDTH))
    return pl.pallas_call(
        body, name="inproj_fwd", grid=(s // tm,),
        in_specs=[rows(D_MODEL), whole(gain), pl.BlockSpec(memory_space=pl.ANY), rows(LANES), rows(LANES),
                  whole(gqa), whole(gka), whole(gqb), whole(gkb), whole(bd512), whole(bd128)],
        out_specs=out_specs, out_shape=out_shape,
        scratch_shapes=[pltpu.VMEM((D_MODEL, IN_WIDTH), BF16)],
        compiler_params=pltpu.CompilerParams(dimension_semantics=("arbitrary",), vmem_limit_bytes=VMEM_LIMIT),
    )(x2, gain, w_bf, cos, sin_s, gqa, gka, gqb, gkb, bd512, bd128)


def _band_mask(i, max_dist):
    qi = lax.broadcasted_iota(jnp.int32, (BLOCK, 2 * BLOCK), 0)
    sj = lax.broadcasted_iota(jnp.int32, (BLOCK, 2 * BLOCK), 1)
    dist = qi - sj + BLOCK
    return (dist >= 0) & (dist <= max_dist) & ((sj >= BLOCK) | (i > 0))


def _kv_window(prev_ref, cur_ref, kc):
    sl = slice(kc * LANES, (kc + 1) * LANES)
    return jnp.concatenate([prev_ref[:, sl], cur_ref[:, sl]], axis=0)


def _swap_heads(a):
    return pltpu.roll(a.astype(F32), HEAD_DIM, 1).astype(a.dtype)


def _attn_fwd(q, k, v, sinks, *, max_dist, name):
    dil, ln, _ = q.shape
    kw = k.shape[2]
    gqa = kw == A_KV_WIDTH
    nb = ln // BLOCK
    with_sinks = sinks is not None

    def body(*refs):
        if with_sinks:
            q_ref, kp_ref, kc_ref, vp_ref, vc_ref, sink_ref, o_ref, lse_ref = refs
        else:
            q_ref, kp_ref, kc_ref, vp_ref, vc_ref, o_ref, lse_ref = refs
        valid = _band_mask(pl.program_id(1), max_dist)
        lane = lax.broadcasted_iota(jnp.int32, (1, LANES), 1)
        lse_ref[...] = jnp.zeros((BLOCK, LANES), F32)
        if gqa:
            k_plain, v_plain = _kv_window(kp_ref, kc_ref, 0), _kv_window(vp_ref, vc_ref, 0)
            k_swap, v_swap = _swap_heads(k_plain), _swap_heads(v_plain)
        for c in range(WIDTH // LANES):
            q2 = q_ref[:, c * LANES:(c + 1) * LANES]
            if not gqa:
                k2, v2 = _kv_window(kp_ref, kc_ref, c), _kv_window(vp_ref, vc_ref, c)
            o_pair = jnp.zeros((BLOCK, LANES), F32)
            for ph in range(2):
                h = 2 * c + ph
                if gqa:
                    kk, vv = (k_plain, v_plain) if (c // 2) == ph else (k_swap, v_swap)
                else:
                    kk, vv = k2, v2
                half = (lane // HEAD_DIM) == ph
                qm = jnp.where(half, q2, jnp.zeros_like(q2))
                sc = jnp.where(valid, _dot_nt(qm, kk), NEG)
                m = jnp.max(sc, axis=-1, keepdims=True)
                if with_sinks:
                    sk = sink_ref[0, h]
                    m = jnp.maximum(m, sk)
                p = jnp.exp(sc - m)
                l = jnp.sum(p, axis=-1, keepdims=True)
                if with_sinks:
                    l = l + jnp.exp(sk - m)
                o2 = _dot(p.astype(BF16), vv) / l
                o_pair = jnp.where(half, o2, o_pair)
                lse_ref[:, h:h + 1] = m + jnp.log(l)
            o_ref[:, c * LANES:(c + 1) * LANES] = o_pair

    def cur(w):
        return pl.BlockSpec((None, BLOCK, w), lambda r, i: (r, i, 0))

    def prev(w):
        return pl.BlockSpec((None, BLOCK, w), lambda r, i: (r, jnp.maximum(i - 1, 0), 0))

    in_specs = [cur(WIDTH), prev(kw), cur(kw), prev(kw), cur(kw)]
    args = [q, k, k, v, v]
    if with_sinks:
        in_specs.append(pl.BlockSpec(memory_space=pltpu.SMEM))
        args.append(sinks)
    return pl.pallas_call(
        body, name=name, grid=(dil, nb), in_specs=in_specs,
        out_specs=(cur(WIDTH), cur(LANES)),
        out_shape=(jax.ShapeDtypeStruct((dil, ln, WIDTH), F32), jax.ShapeDtypeStruct((dil, ln, LANES), F32)),
        compiler_params=pltpu.CompilerParams(dimension_semantics=("arbitrary", "arbitrary")),
    )(*args)


def _attn_bwd(q, k, v, do, lse, delta, *, max_dist, name):
    dil, ln, _ = q.shape
    kw = k.shape[2]
    gqa = kw == A_KV_WIDTH
    nb = ln // BLOCK
    n_kc = kw // LANES

    def body(q_ref, kp_ref, kc_ref, vp_ref, vc_ref, do_ref, lse_ref, dl_ref, dq_ref, dk_ref, dv_ref, ck_ref, cv_ref):
        i = pl.program_id(1)

        @pl.when(i == 0)
        def _():
            ck_ref[...] = jnp.zeros_like(ck_ref)
            cv_ref[...] = jnp.zeros_like(cv_ref)

        @pl.when(i < nb)
        def _():
            valid = _band_mask(i, max_dist)
            lane = lax.broadcasted_iota(jnp.int32, (1, LANES), 1)
            dk_acc = [jnp.zeros((2 * BLOCK, LANES), F32) for _ in range(n_kc)]
            dv_acc = [jnp.zeros((2 * BLOCK, LANES), F32) for _ in range(n_kc)]
            if gqa:
                k_plain, v_plain = _kv_window(kp_ref, kc_ref, 0), _kv_window(vp_ref, vc_ref, 0)
                k_swap, v_swap = _swap_heads(k_plain), _swap_heads(v_plain)
            for c in range(WIDTH // LANES):
                q2 = q_ref[:, c * LANES:(c + 1) * LANES]
                do2 = do_ref[:, c * LANES:(c + 1) * LANES]
                kc = 0 if gqa else c
                if not gqa:
                    k2, v2 = _kv_window(kp_ref, kc_ref, c), _kv_window(vp_ref, vc_ref, c)
                dq_pair = jnp.zeros((BLOCK, LANES), F32)
                for ph in range(2):
                    h = 2 * c + ph
                    swapped = gqa and (c // 2) != ph
                    if gqa:
                        kk, vv = (k_swap, v_swap) if swapped else (k_plain, v_plain)
                    else:
                        kk, vv = k2, v2
                    half = (lane // HEAD_DIM) == ph
                    qm = jnp.where(half, q2, jnp.zeros_like(q2))
                    dom = jnp.where(half, do2, jnp.zeros_like(do2))
                    sc = jnp.where(valid, _dot_nt(qm, kk), NEG)
                    p = jnp.exp(sc - lse_ref[:, h:h + 1])
                    dp = _dot_nt(dom, vv)
                    ds = (p * (dp - dl_ref[:, h:h + 1])).astype(BF16)
                    dq_pair = jnp.where(half, _dot(ds, kk), dq_pair)
                    dkc = _dot_tn(ds, qm)
                    dvc = _dot_tn(p.astype(BF16), dom)
                    if swapped:
                        dkc, dvc = _swap_heads(dkc), _swap_heads(dvc)
                    dk_acc[kc] = dk_acc[kc] + dkc
                    dv_acc[kc] = dv_acc[kc] + dvc
                dq_ref[:, c * LANES:(c + 1) * LANES] = dq_pair * SCALE
            for kc in range(n_kc):
                sl = slice(kc * LANES, (kc + 1) * LANES)
                dk_ref[:, sl] = ck_ref[:, sl] + dk_acc[kc][:BLOCK]
                dv_ref[:, sl] = cv_ref[:, sl] + dv_acc[kc][:BLOCK]
                ck_ref[:, sl] = dk_acc[kc][BLOCK:]
                cv_ref[:, sl] = dv_acc[kc][BLOCK:]

        @pl.when(i == nb)
        def _():
            dk_ref[...] = ck_ref[...]
            dv_ref[...] = cv_ref[...]

    def cur(w):
        return pl.BlockSpec((None, BLOCK, w), lambda r, i: (r, jnp.minimum(i, nb - 1), 0))

    def prev(w):
        return pl.BlockSpec((None, BLOCK, w), lambda r, i: (r, jnp.clip(i - 1, 0, nb - 1), 0))

    def lag(w):
        return pl.BlockSpec((None, BLOCK, w), lambda r, i: (r, jnp.maximum(i - 1, 0), 0))

    sds = jax.ShapeDtypeStruct
    return pl.pallas_call(
        body, name=name, grid=(dil, nb + 1),
        in_specs=[cur(WIDTH), prev(kw), cur(kw), prev(kw), cur(kw), cur(WIDTH), cur(LANES), cur(LANES)],
        out_specs=(cur(WIDTH), lag(kw), lag(kw)),
        out_shape=(sds((dil, ln, WIDTH), F32), sds((dil, ln, kw), F32), sds((dil, ln, kw), F32)),
        scratch_shapes=[pltpu.VMEM((BLOCK, kw), F32), pltpu.VMEM((BLOCK, kw), F32)],
        compiler_params=pltpu.CompilerParams(dimension_semantics=("arbitrary", "arbitrary")),
    )(q, k, k, v, v, do, lse, delta)


def _outproj(o_a, lse_a, o_b, lse_b, g_a, g_b, x2, tgt2, w_out_bf, sink_row):
    s = x2.shape[0]
    tm = ROW_TILE

    def body(oa_ref, lsea_ref, ob1_ref, ob2_ref, ob3_ref, ls1_ref, ls2_ref, ls3_ref, ga_ref, gb_ref, x_ref, t_ref,
             w_ref, sink_ref,
             dy_ref, doa_ref, dob_ref, dga_ref, dgb_ref, lseb_ref, dla_ref, dlb_ref, gw_ref, loss_ref, dsink_ref):
        i = pl.program_id(0)

        @pl.when(i == 0)
        def _():
            gw_ref[...] = jnp.zeros_like(gw_ref)
            loss_ref[...] = jnp.zeros_like(loss_ref)
            dsink_ref[...] = jnp.zeros_like(dsink_ref)

        ls = [ls1_ref[...], ls2_ref[...], ls3_ref[...]]
        mx = jnp.maximum(jnp.maximum(ls[0], ls[1]), ls[2])
        lse_b = mx + jnp.log(jnp.exp(ls[0] - mx) + jnp.exp(ls[1] - mx) + jnp.exp(ls[2] - mx))
        lane = lax.broadcasted_iota(jnp.int32, (tm, LANES), 1)
        lse_b = jnp.where(lane < HEADS, lse_b, 0.0)
        lseb_ref[...] = lse_b
        o_b = (_expand_heads(jnp.exp(ls[0] - lse_b)) * ob1_ref[...]
               + _expand_heads(jnp.exp(ls[1] - lse_b)) * ob2_ref[...]
               + _expand_heads(jnp.exp(ls[2] - lse_b)) * ob3_ref[...])
        o_a = oa_ref[...]
        g_a = ga_ref[...]
        g_b = gb_ref[...]
        sg_a = _sigmoid(g_a)
        sg_b = _sigmoid(g_b)
        silu_a = g_a * sg_a
        silu_b = g_b * sg_b
        mixed = jnp.concatenate([o_a * silu_a, o_b * silu_b], axis=1).astype(BF16)
        w = w_ref[...]
        y = x_ref[...] + _dot(mixed, w)
        diff = y - t_ref[...]
        loss_ref[...] += (0.5 / D_MODEL) * jnp.sum(diff * diff)
        dy = diff * (1.0 / D_MODEL)
        dy_ref[...] = dy
        dyb = dy.astype(BF16)
        gw_ref[...] += _dot_tn(mixed, dyb)
        dmixed = _dot_nt(dyb, w)
        dm_a = dmixed[:, :WIDTH]
        dm_b = dmixed[:, WIDTH:]
        do_a = dm_a * silu_a
        do_b = dm_b * silu_b
        doa_ref[...] = do_a.astype(BF16)
        dob_ref[...] = do_b.astype(BF16)
        dga_ref[...] = (dm_a * o_a * (sg_a * (1.0 + g_a * (1.0 - sg_a)))).astype(BF16)
        dgb_ref[...] = (dm_b * o_b * (sg_b * (1.0 + g_b * (1.0 - sg_b)))).astype(BF16)
        dl_a = _reduce_heads(do_a * o_a)
        dla_ref[...] = dl_a
        dlb_ref[...] = _reduce_heads(do_b * o_b)
        dsink_ref[...] -= jnp.sum(jnp.exp(sink_ref[...] - lsea_ref[...]) * dl_a, axis=0, keepdims=True)

    def rows(w):
        return pl.BlockSpec((tm, w), lambda i: (i, 0))

    def whole(shape):
        return pl.BlockSpec(shape, lambda i: (0, 0))

    sds = jax.ShapeDtypeStruct
    return pl.pallas_call(
        body, name="outproj_fwd_bwd", grid=(s // tm,),
        in_specs=[rows(WIDTH), rows(LANES), rows(WIDTH), rows(WIDTH), rows(WIDTH), rows(LANES), rows(LANES),
                  rows(LANES), rows(WIDTH), rows(WIDTH), rows(D_MODEL), rows(D_MODEL),
                  whole((D_MODEL, D_MODEL)), whole((1, LANES))],
        out_specs=(rows(D_MODEL), rows(WIDTH), rows(WIDTH), rows(WIDTH), rows(WIDTH), rows(LANES), rows(LANES),
                   rows(LANES), whole((D_MODEL, D_MODEL)), whole((1, LANES)), whole((1, LANES))),
        out_shape=(sds((s, D_MODEL), F32), sds((s, WIDTH), BF16), sds((s, WIDTH), BF16), sds((s, WIDTH), BF16),
                   sds((s, WIDTH), BF16), sds((s, LANES), F32), sds((s, LANES), F32), sds((s, LANES), F32),
                   sds((D_MODEL, D_MODEL), F32), sds((1, LANES), F32), sds((1, LANES), F32)),
        compiler_params=pltpu.CompilerParams(dimension_semantics=("arbitrary",), vmem_limit_bytes=VMEM_LIMIT),
    )(o_a, lse_a, o_b[0], o_b[1], o_b[2], lse_b[0], lse_b[1], lse_b[2], g_a, g_b, x2, tgt2, w_out_bf, sink_row)


def _inproj_bwd(x2, dy, gain, w_bf, cos, sin_s, gqa, gka, gqb, gkb, bd512, bd128,
                qa_raw, ka_raw, qb_raw, kb_raw, dq_a, dk_a, dv_a, dq_b, dk_b, dv_b, dg_a, dg_b):
    s = x2.shape[0]
    tm = BWD_ROW_TILE
    n_steps = s // tm

    def body(x_ref, dy_ref, gain_ref, w_hbm, cos_ref, sin_ref, gqa_ref, gka_ref, gqb_ref, gkb_ref, bd512_ref,
             bd128_ref, qa_raw_ref, ka_raw_ref, qb_raw_ref, kb_raw_ref, dqa_ref, dka_ref, dva_ref,
             dqb1_ref, dqb2_ref, dqb3_ref, dkb1_ref, dkb2_ref, dkb3_ref, dvb1_ref, dvb2_ref, dvb3_ref,
             dga_ref, dgb_ref,
             gx_ref, gw_hbm, dgain_ref, dgqa_ref, dgka_ref, dgqb_ref, dgkb_ref,
             w_vmem, acc_ref, dproj_ref):
        i = pl.program_id(0)

        @pl.when(i == 0)
        def _():
            pltpu.sync_copy(w_hbm, w_vmem)
            acc_ref[...] = jnp.zeros_like(acc_ref)
            dgain_ref[...] = jnp.zeros_like(dgain_ref)
            dgqa_ref[...] = jnp.zeros_like(dgqa_ref)
            dgka_ref[...] = jnp.zeros_like(dgka_ref)
            dgqb_ref[...] = jnp.zeros_like(dgqb_ref)
            dgkb_ref[...] = jnp.zeros_like(dgkb_ref)

        cos1 = cos_ref[...]
        sin1 = sin_ref[...]
        cos4 = jnp.tile(cos1, (1, 4))
        sin4 = jnp.tile(sin1, (1, 4))

        dt, dg = _qknorm_rope_bwd(dqa_ref[...], qa_raw_ref[...], gqa_ref[...], cos4, sin4, bd512_ref[...])
        dproj_ref[:, C_QA:C_KA] = dt.astype(BF16)
        dgqa_ref[...] += jnp.sum(dg, axis=0, keepdims=True)
        dt, dg = _qknorm_rope_bwd(dka_ref[...], ka_raw_ref[...], gka_ref[...], cos1, sin1, bd128_ref[...])
        dproj_ref[:, C_KA:C_VA] = dt.astype(BF16)
        dgka_ref[...] += jnp.sum(dg, axis=0, keepdims=True)
        dproj_ref[:, C_VA:C_GA] = dva_ref[...].astype(BF16)
        dproj_ref[:, C_GA:C_QB] = dga_ref[...]
        dq = (dqb1_ref[...] + dqb2_ref[...]) + dqb3_ref[...]
        dt, dg = _qknorm_rope_bwd(dq, qb_raw_ref[...], gqb_ref[...], cos4, sin4, bd512_ref[...])
        dproj_ref[:, C_QB:C_KB] = dt.astype(BF16)
        dgqb_ref[...] += jnp.sum(dg, axis=0, keepdims=True)
        dk = (dkb1_ref[...] + dkb2_ref[...]) + dkb3_ref[...]
        dt, dg = _qknorm_rope_bwd(dk, kb_raw_ref[...], gkb_ref[...], cos4, sin4, bd512_ref[...])
        dproj_ref[:, C_KB:C_VB] = dt.astype(BF16)
        dgkb_ref[...] += jnp.sum(dg, axis=0, keepdims=True)
        dproj_ref[:, C_VB:C_GB] = ((dvb1_ref[...] + dvb2_ref[...]) + dvb3_ref[...]).astype(BF16)
        dproj_ref[:, C_GB:C_END] = dgb_ref[...]

        xt = x_ref[...]
        gain_row = gain_ref[...]
        r = lax.rsqrt(jnp.mean(xt * xt, axis=-1, keepdims=True) + EPS)
        xr = xt * r
        h = (xr * gain_row).astype(BF16)
        for a in range(0, IN_WIDTH, ACC_COLS):
            acc_ref[:, a:a + ACC_COLS] += _dot_tn(h, dproj_ref[:, a:a + ACC_COLS])
        dh = _dot_nt(dproj_ref[...], w_vmem[...])
        dgain_ref[...] += jnp.sum(dh * xr, axis=0, keepdims=True)
        u = dh * gain_row
        gx_ref[...] = dy_ref[...] + r * (u - xr * jnp.mean(u * xr, axis=-1, keepdims=True))

        @pl.when(i == n_steps - 1)
        def _():
            pltpu.sync_copy(acc_ref, gw_hbm)

    def rows(w):
        return pl.BlockSpec((tm, w), lambda i: (i, 0))

    def whole(a):
        return pl.BlockSpec(a.shape, lambda i: (0, 0))

    def acc_row(w):
        return pl.BlockSpec((1, w), lambda i: (0, 0))

    sds = jax.ShapeDtypeStruct
    any_spec = pl.BlockSpec(memory_space=pl.ANY)
    return pl.pallas_call(
        body, name="inproj_bwd", grid=(n_steps,),
        in_specs=[rows(D_MODEL), rows(D_MODEL), whole(gain), any_spec, rows(LANES), rows(LANES),
                  whole(gqa), whole(gka), whole(gqb), whole(gkb), whole(bd512), whole(bd128),
                  rows(WIDTH), rows(A_KV_WIDTH), rows(WIDTH), rows(WIDTH),
                  rows(WIDTH), rows(A_KV_WIDTH), rows(A_KV_WIDTH)] + [rows(WIDTH)] * 9 + [rows(WIDTH), rows(WIDTH)],
        out_specs=(rows(D_MODEL), any_spec, acc_row(D_MODEL), acc_row(WIDTH), acc_row(A_KV_WIDTH), acc_row(WIDTH),
                   acc_row(WIDTH)),
        out_shape=(sds((s, D_MODEL), F32), sds((D_MODEL, IN_WIDTH), F32), sds((1, D_MODEL), F32),
                   sds((1, WIDTH), F32), sds((1, A_KV_WIDTH), F32), sds((1, WIDTH), F32), sds((1, WIDTH), F32)),
        scratch_shapes=[pltpu.VMEM((D_MODEL, IN_WIDTH), BF16), pltpu.VMEM((D_MODEL, IN_WIDTH), F32),
                        pltpu.VMEM((tm, IN_WIDTH), BF16)],
        compiler_params=pltpu.CompilerParams(dimension_semantics=("arbitrary",), vmem_limit_bytes=VMEM_LIMIT),
    )(x2, dy, gain, w_bf, cos, sin_s, gqa, gka, gqb, gkb, bd512, bd128, qa_raw, ka_raw, qb_raw, kb_raw,
      dq_a, dk_a, dv_a, dq_b[0], dq_b[1], dq_b[2], dk_b[0], dk_b[1], dk_b[2], dv_b[0], dv_b[1], dv_b[2],
      dg_a, dg_b)


def _fold(a, dil):
    s, w = a.shape
    if dil == 1:
        return a[None]
    return a.reshape(s // dil, dil, w).transpose(1, 0, 2)


def _unfold(a):
    dil, ln, w = a.shape
    if dil == 1:
        return a[0]
    return a.transpose(1, 0, 2).reshape(dil * ln, w)


def _rope_tables(s):
    half = HEAD_DIM // 2
    inv = ROPE_THETA ** (-jnp.arange(half, dtype=F32) / half)
    ang = jnp.arange(s).astype(F32)[:, None] * inv[None, :]
    cos = jnp.cos(ang)
    sin = jnp.sin(ang)
    return jnp.tile(cos, (1, 4)), jnp.concatenate([-sin, sin, -sin, sin], axis=1)


def _block_diag_ones(w):
    idx = jnp.arange(w) // HEAD_DIM
    return (idx[:, None] == idx[None, :]).astype(BF16)


def _local_step(x2, tgt2, norm_gain, w_in_bf, q_norm_a, k_norm_a, sinks_a, q_norm_b, k_norm_b, w_out_bf):
    s = x2.shape[0]
    cos, sin_s = _rope_tables(s)
    bd512, bd128 = _block_diag_ones(WIDTH), _block_diag_ones(A_KV_WIDTH)
    gqa = jnp.tile(q_norm_a, (1, HEADS))
    gka = jnp.tile(k_norm_a, (1, 2))
    gqb = jnp.tile(q_norm_b, (1, HEADS))
    gkb = jnp.tile(k_norm_b, (1, HEADS))
    sink_row = jnp.pad(sinks_a, ((0, 0), (0, LANES - HEADS)))

    (qa, ka, va, qb, kb, vb, qa_raw, ka_raw, g_a, qb_raw, kb_raw, g_b) = _inproj(
        x2, norm_gain, w_in_bf, cos, sin_s, gqa, gka, gqb, gkb, bd512, bd128)

    o_a, lse_a = _attn_fwd(qa[None], ka[None], va[None], sinks_a, max_dist=A_MAX_DIST, name="attn_a_fwd")
    o_a, lse_a = o_a[0], lse_a[0]
    qb_f = [_fold(qb, d) for d in B_DILATIONS]
    kb_f = [_fold(kb, d) for d in B_DILATIONS]
    vb_f = [_fold(vb, d) for d in B_DILATIONS]
    o_b, lse_p = [], []
    for n, d in enumerate(B_DILATIONS):
        o, l = _attn_fwd(qb_f[n], kb_f[n], vb_f[n], None, max_dist=B_MAX_DIST, name="attn_b%d_fwd" % d)
        o_b.append(_unfold(o))
        lse_p.append(_unfold(l))

    (dy, do_a, do_b, dg_a, dg_b, lse_b, dl_a, dl_b, gw_out, loss_part, dsink) = _outproj(
        o_a, lse_a, o_b, lse_p, g_a, g_b, x2, tgt2, w_out_bf, sink_row)

    dq_a, dk_a, dv_a = _attn_bwd(qa[None], ka[None], va[None], do_a[None], lse_a[None], dl_a[None],
                                 max_dist=A_MAX_DIST, name="attn_a_bwd")
    dq_b, dk_b, dv_b = [], [], []
    for n, d in enumerate(B_DILATIONS):
        dq, dk, dv = _attn_bwd(qb_f[n], kb_f[n], vb_f[n], _fold(do_b, d), _fold(lse_b, d), _fold(dl_b, d),
                               max_dist=B_MAX_DIST, name="attn_b%d_bwd" % d)
        dq_b.append(_unfold(dq))
        dk_b.append(_unfold(dk))
        dv_b.append(_unfold(dv))

    gx, gw_in, dgain, dgqa, dgka, dgqb, dgkb = _inproj_bwd(
        x2, dy, norm_gain, w_in_bf, cos, sin_s, gqa, gka, gqb, gkb, bd512, bd128,
        qa_raw, ka_raw, qb_raw, kb_raw, dq_a[0], dk_a[0], dv_a[0], dq_b, dk_b, dv_b, dg_a, dg_b)
    return loss_part, gx, gw_in, gw_out, (dgain, dgqa, dgka, dsink, dgqb, dgkb)


def _position():
    return lax.axis_index("x"), lax.axis_index("y"), lax.axis_index("c")


def _all_to_all(srcs, name):
    n = len(srcs)
    bcast = [a.ndim == 2 for a in srcs]

    def body(*refs):
        src_refs, dst_refs = refs[:n], refs[n:2 * n]
        send_sems, recv_sems, local_sems = refs[2 * n:]
        x, y, c = _position()
        me = 4 * x + 2 * y + c
        copies = []
        for k in range(n):
            own = src_refs[k] if bcast[k] else src_refs[k].at[me]
            local = pltpu.make_async_copy(own, dst_refs[k].at[me], local_sems.at[k])
            local.start()
            copies.append(local)
        sends, arrivals = [], []
        for d in range(1, N_DEV):
            px, py, pc = x ^ (d >> 2), y ^ ((d >> 1) & 1), c ^ (d & 1)
            peer = 4 * px + 2 * py + pc
            for k in range(n):
                sem = k * N_DEV + d
                send = pltpu.make_async_remote_copy(
                    src_ref=src_refs[k] if bcast[k] else src_refs[k].at[peer], dst_ref=dst_refs[k].at[me],
                    send_sem=send_sems.at[sem], recv_sem=recv_sems.at[sem],
                    device_id=(px, py, pc), device_id_type=MESH)
                send.start()
                sends.append(send)
                arrivals.append(pltpu.make_async_remote_copy(
                    src_ref=src_refs[k] if bcast[k] else src_refs[k].at[me], dst_ref=dst_refs[k].at[peer],
                    send_sem=send_sems.at[sem], recv_sem=recv_sems.at[sem],
                    device_id=(px, py, pc), device_id_type=MESH))
        for arrival in arrivals:
            arrival.wait_recv()
        for send in sends:
            send.wait_send()
        for local in copies:
            local.wait()

    any_spec = pl.BlockSpec(memory_space=pl.ANY)
    out_shape = tuple(jax.ShapeDtypeStruct((N_DEV,) + a.shape[-2:], a.dtype) for a in srcs)
    return pl.pallas_call(
        body, name=name, in_specs=[any_spec] * n, out_specs=tuple([any_spec] * n), out_shape=out_shape,
        scratch_shapes=[pltpu.SemaphoreType.DMA((n * N_DEV,)), pltpu.SemaphoreType.DMA((n * N_DEV,)),
                        pltpu.SemaphoreType.DMA((n,))],
    )(*srcs)


PAIR_CHUNKS = 4
GATHER_CHUNKS = 2


def _pair_exchange(srcs, name):
    n = len(srcs)
    ch = PAIR_CHUNKS

    def body(*refs):
        src_refs, dst_refs = refs[:n], refs[n:2 * n]
        send_sems, recv_sems, local_sems = refs[2 * n:]
        x, y, c = _position()
        copies, sends, arrivals = [], [], []
        for k in range(n):
            local = pltpu.make_async_copy(src_refs[k], dst_refs[k].at[c], local_sems.at[k])
            local.start()
            copies.append(local)
            rc = srcs[k].shape[0] // ch
            for j in range(ch):
                rows = pl.ds(j * rc, rc)
                send = pltpu.make_async_remote_copy(
                    src_ref=src_refs[k].at[rows], dst_ref=dst_refs[k].at[c, rows],
                    send_sem=send_sems.at[k * ch + j], recv_sem=recv_sems.at[k * ch + j],
                    device_id=(x, y, 1 - c), device_id_type=MESH)
                send.start()
                sends.append(send)
                arrivals.append(pltpu.make_async_remote_copy(
                    src_ref=src_refs[k].at[rows], dst_ref=dst_refs[k].at[1 - c, rows],
                    send_sem=send_sems.at[k * ch + j], recv_sem=recv_sems.at[k * ch + j],
                    device_id=(x, y, 1 - c), device_id_type=MESH))
        for arrival in arrivals:
            arrival.wait_recv()
        for send in sends:
            send.wait_send()
        for local in copies:
            local.wait()

    vmem_spec = pl.BlockSpec(memory_space=pltpu.VMEM)
    out_shape = tuple(jax.ShapeDtypeStruct((2,) + a.shape, a.dtype) for a in srcs)
    return pl.pallas_call(
        body, name=name, in_specs=[vmem_spec] * n, out_specs=tuple([vmem_spec] * n), out_shape=out_shape,
        scratch_shapes=[pltpu.SemaphoreType.DMA((n * ch,)), pltpu.SemaphoreType.DMA((n * ch,)),
                        pltpu.SemaphoreType.DMA((n,))],
        compiler_params=pltpu.CompilerParams(vmem_limit_bytes=VMEM_LIMIT),
    )(*srcs)


def _gather_weights(blocks, name):
    n = len(blocks)
    ch = GATHER_CHUNKS
    n_sems = n * (N_CHIP - 1) * ch

    def body(*refs):
        src_refs, dst_refs = refs[:n], refs[n:2 * n]
        ici_send, ici_recv, d2d_send, d2d_recv, local_sems = refs[2 * n:]
        x, y, c = _position()
        b = 2 * x + y
        copies = []
        for k in range(n):
            local = pltpu.make_async_copy(src_refs[k], dst_refs[k].at[b], local_sems.at[k])
            local.start()
            copies.append(local)

        def rows(k, core, j):
            half = blocks[k].shape[0] // 2
            return pl.ds(core * half + j * (half // ch), half // ch)

        plan = []
        for d in range(1, N_CHIP):
            px, py = x ^ (d >> 1), y ^ (d & 1)
            for j in range(ch):
                for k in range(n):
                    plan.append((px, py, 2 * px + py, k, j, ((d - 1) * ch + j) * n + k))
        sends = []
        for px, py, pb, k, j, sem in plan:
            send = pltpu.make_async_remote_copy(
                src_ref=src_refs[k].at[rows(k, c, j)], dst_ref=dst_refs[k].at[b, rows(k, c, j)],
                send_sem=ici_send.at[sem], recv_sem=ici_recv.at[sem], device_id=(px, py, c), device_id_type=MESH)
            send.start()
            sends.append(send)
        for px, py, pb, k, j, sem in plan:
            landed = dst_refs[k].at[pb, rows(k, c, j)]
            pltpu.make_async_remote_copy(
                src_ref=landed, dst_ref=landed, send_sem=ici_send.at[sem], recv_sem=ici_recv.at[sem],
                device_id=(px, py, c), device_id_type=MESH).wait_recv()
            forward = pltpu.make_async_remote_copy(
                src_ref=landed, dst_ref=landed, send_sem=d2d_send.at[sem], recv_sem=d2d_recv.at[sem],
                device_id=(x, y, 1 - c), device_id_type=MESH)
            forward.start()
            sends.append(forward)
        for px, py, pb, k, j, sem in plan:
            passed = dst_refs[k].at[pb, rows(k, 1 - c, j)]
            pltpu.make_async_remote_copy(
                src_ref=passed, dst_ref=passed, send_sem=d2d_send.at[sem], recv_sem=d2d_recv.at[sem],
                device_id=(x, y, 1 - c), device_id_type=MESH).wait_recv()
        for send in sends:
            send.wait_send()
        for local in copies:
            local.wait()

    vmem_spec = pl.BlockSpec(memory_space=pltpu.VMEM)
    out_shape = tuple(jax.ShapeDtypeStruct((N_CHIP,) + a.shape, a.dtype) for a in blocks)
    return pl.pallas_call(
        body, name=name, in_specs=[vmem_spec] * n, out_specs=tuple([vmem_spec] * n), out_shape=out_shape,
        scratch_shapes=[pltpu.SemaphoreType.DMA((n_sems,)) for _ in range(4)] + [pltpu.SemaphoreType.DMA((n,))],
        compiler_params=pltpu.CompilerParams(vmem_limit_bytes=VMEM_LIMIT),
    )(*blocks)


SUM_ROWS = 128


def _sum_parts(parts, name):
    _, r, c = parts.shape

    def body(p_ref, o_ref):
        acc = p_ref[0].astype(F32)
        for j in range(1, N_DEV):
            acc = acc + p_ref[j].astype(F32)
        o_ref[...] = acc

    return pl.pallas_call(
        body, name=name, grid=(r // SUM_ROWS,),
        in_specs=[pl.BlockSpec((N_DEV, SUM_ROWS, c), lambda i: (0, i, 0))],
        out_specs=pl.BlockSpec((SUM_ROWS, c), lambda i: (i, 0)),
        out_shape=jax.ShapeDtypeStruct((r, c), F32),
    )(parts)


def _adamw_math(w, g, m, v):
    m = ADAM_B1 * m + (1.0 - ADAM_B1) * g
    v = ADAM_B2 * v + (1.0 - ADAM_B2) * (g * g)
    m_hat = m / (1.0 - ADAM_B1 ** ADAM_STEP)
    v_hat = v / (1.0 - ADAM_B2 ** ADAM_STEP)
    delta = -ADAM_LR * (m_hat / (jnp.sqrt(v_hat) + ADAM_EPS) + ADAM_WD * w)
    return delta, m, v


def _adamw(w, g, m, v, name):
    r, c = w.shape

    def body(w_ref, g_ref, m_ref, v_ref, d_ref, nm_ref, nv_ref):
        delta, nm, nv = _adamw_math(w_ref[...], g_ref[...], m_ref[...], v_ref[...])
        d_ref[...] = delta
        nm_ref[...] = nm
        nv_ref[...] = nv

    spec = pl.BlockSpec((SUM_ROWS, c), lambda i: (i, 0))
    shape = jax.ShapeDtypeStruct((r, c), F32)
    return pl.pallas_call(
        body, name=name, grid=(r // SUM_ROWS,), in_specs=[spec] * 4, out_specs=(spec,) * 3,
        out_shape=(shape,) * 3,
    )(w, g, m, v)


PACK_ROWS = 8


def _fold_heads(v):
    y = v[:, 0:LANES]
    for j in range(1, v.shape[1] // LANES):
        y = y + v[:, j * LANES:(j + 1) * LANES]
    return y + pltpu.roll(y, HEAD_DIM, 1)


def _small_adamw(recv, w_p, m_p, v_p):
    def body(r_ref, w_ref, m_ref, v_ref, g_ref, d_ref, nm_ref, nv_ref):
        tot = r_ref[0]
        for j in range(1, N_DEV):
            tot = tot + r_ref[j]
        row1 = tot[1:2, :]
        row2 = tot[2:3, :]
        pieces = [_fold_heads(row1[:, 0:WIDTH]), _fold_heads(row2[:, WIDTH:WIDTH + A_KV_WIDTH]),
                  _fold_heads(row1[:, WIDTH:2 * WIDTH]), _fold_heads(row2[:, 0:WIDTH]),
                  row2[:, WIDTH + A_KV_WIDTH:WIDTH + 2 * A_KV_WIDTH], jnp.zeros((1, 3 * LANES), F32)]
        g = jnp.concatenate([tot[0:1, :], jnp.concatenate(pieces, axis=1), jnp.zeros((PACK_ROWS - 2, D_MODEL), F32)],
                            axis=0)
        g_ref[...] = g
        delta, nm, nv = _adamw_math(w_ref[...], g, m_ref[...], v_ref[...])
        d_ref[...] = delta
        nm_ref[...] = nm
        nv_ref[...] = nv

    shape = jax.ShapeDtypeStruct((PACK_ROWS, D_MODEL), F32)
    return pl.pallas_call(body, name="small_adamw", out_shape=(shape,) * 4)(recv, w_p, m_p, v_p)


def _pack_small(norm_gain, q_a, k_a, q_b, k_b, sinks):
    def lane_pad(a):
        return jnp.pad(a, ((0, 0), (0, LANES - a.shape[1])))
    row1 = jnp.concatenate([lane_pad(q_a), lane_pad(k_a), lane_pad(q_b), lane_pad(k_b), lane_pad(sinks),
                            jnp.zeros((1, 3 * LANES), F32)], axis=1)
    return jnp.concatenate([norm_gain, row1, jnp.zeros((PACK_ROWS - 2, D_MODEL), F32)], axis=0)


def _unpack_small(p):
    return (p[0:1, :], p[1:2, 0:HEAD_DIM], p[1:2, LANES:LANES + HEAD_DIM], p[1:2, 2 * LANES:2 * LANES + HEAD_DIM],
            p[1:2, 3 * LANES:3 * LANES + HEAD_DIM], p[1:2, 4 * LANES:4 * LANES + HEADS])


def kernel(x, norm_gain, w_in, q_norm_a, k_norm_a, sinks_a, q_norm_b, k_norm_b, w_out, loss_target, m_norm_gain, m_w_in, m_q_norm_a, m_k_norm_a, m_sinks_a, m_q_norm_b, m_k_norm_b, m_w_out, v_norm_gain, v_w_in, v_q_norm_a, v_k_norm_a, v_sinks_a, v_q_norm_b, v_k_norm_b, v_w_out):
    in_cols = IN_WIDTH // N_CHIP
    out_rows = D_MODEL // N_CHIP
    in_half = D_MODEL // 2
    out_half = out_rows // 2

    w_in_all, w_out_all = _gather_weights([w_in[0].astype(BF16), w_out[0].astype(BF16)], "gather_weights")
    w_in_bf = w_in_all.transpose(1, 0, 2).reshape(D_MODEL, IN_WIDTH)
    w_out_bf = w_out_all.reshape(D_MODEL, D_MODEL)

    loss_part, gx, gw_in, gw_out, (dgain, dgqa, dgka, dsink, dgqb, dgkb) = _local_step(
        x[0], loss_target[0], norm_gain, w_in_bf, q_norm_a, k_norm_a, sinks_a, q_norm_b, k_norm_b, w_out_bf)

    gw_in_parts = gw_in.reshape(2, in_half, N_CHIP, in_cols).transpose(2, 0, 1, 3).reshape(
        N_DEV, in_half, in_cols).astype(BF16)
    gw_out_parts = gw_out.reshape(N_DEV, out_half, D_MODEL).astype(BF16)
    small = jnp.concatenate([
        dgain, jnp.concatenate([dgqa, dgqb], axis=1),
        jnp.concatenate([dgkb, dgka, dsink, jnp.zeros((1, D_MODEL - WIDTH - 2 * A_KV_WIDTH), F32)], axis=1),
        jnp.zeros((PACK_ROWS - 3, D_MODEL), F32)], axis=0)
    in_recv, out_recv, small_recv = _all_to_all([gw_in_parts, gw_out_parts, small], "scatter_grads")
    in_sum = _sum_parts(in_recv, "sum_grad_w_in")
    out_sum = _sum_parts(out_recv, "sum_grad_w_out")
    in_both, out_both = _pair_exchange([in_sum, out_sum], "pair_exchange")
    g_w_in = in_both.reshape(D_MODEL, in_cols)
    g_w_out = out_both.reshape(out_rows, D_MODEL)

    d_w_in, nm_w_in, nv_w_in = _adamw(w_in[0], g_w_in, m_w_in[0], v_w_in[0], "adamw_w_in")
    d_w_out, nm_w_out, nv_w_out = _adamw(w_out[0], g_w_out, m_w_out[0], v_w_out[0], "adamw_w_out")
    g_s, d_s, nm_s, nv_s = _small_adamw(
        small_recv,
        _pack_small(norm_gain, q_norm_a, k_norm_a, q_norm_b, k_norm_b, sinks_a),
        _pack_small(m_norm_gain, m_q_norm_a, m_k_norm_a, m_q_norm_b, m_k_norm_b, m_sinks_a),
        _pack_small(v_norm_gain, v_q_norm_a, v_k_norm_a, v_q_norm_b, v_k_norm_b, v_sinks_a))

    loss = lax.psum(loss_part[0, 0], ("x", "y", "c"))

    def leaves(small_packed, big_in, big_out):
        gain, qa, ka, qb, kb, sk = _unpack_small(small_packed)
        return (gain, big_in[None], qa, ka, sk, qb, kb, big_out[None])

    return ((loss, gx[None]) + leaves(g_s, g_w_in, g_w_out) + leaves(d_s, d_w_in, d_w_out)
            + leaves(nm_s, nm_w_in, nm_w_out) + leaves(nv_s, nv_w_in, nv_w_out))
```

```python
import jax
import jax.numpy as jnp
from jax import lax
from jax.experimental import pallas as pl
from jax.experimental.pallas import tpu as pltpu

F32 = jnp.float32
BF16 = jnp.bfloat16

D_MODEL = 1024
HEAD_DIM = 64
HEADS = 8
WIDTH = HEADS * HEAD_DIM
A_KV_WIDTH = 2 * HEAD_DIM
BLOCK = 128
LANES = 128
FOLD = 16
A_MAX_DIST = 127
B_MAX_DIST = 128
ROPE_THETA = 10000.0
EPS = 1e-6
NEG = -1e30
SCALE = HEAD_DIM ** -0.5

IN_WIDTH = 3328
C_QA, C_KA, C_VA, C_GA, C_QB, C_KB, C_VB, C_GB, C_END = 0, 512, 640, 768, 1280, 1792, 2304, 2816, 3328

N_DEV = 8
N_CHIP = 4
MESH = pl.DeviceIdType.MESH
IN_COLS = IN_WIDTH // N_CHIP
WIN = 896
WIN_START = (0, 768, 1664, 2432)
WIN_SHIFT = (0, 64, 0, 64)
OUT_ROWS = D_MODEL // N_CHIP
RELATIONS = (3, 1, 2, 0)

ADAM_LR = 0.001
ADAM_B1 = 0.9
ADAM_B2 = 0.999
ADAM_EPS = 1e-08
ADAM_WD = 0.01
ADAM_STEP = 10

ROW_TILE = 256
FOLD_ROWS = ROW_TILE // FOLD
GRAD_ROWS = 512
VMEM_LIMIT = 56 * 1024 * 1024


def _dot(a, b):
    return jnp.dot(a, b, preferred_element_type=F32)


def _dot_nt(a, b):
    return lax.dot_general(a, b, (((1,), (1,)), ((), ())), preferred_element_type=F32)


def _dot_tn(a, b):
    return lax.dot_general(a, b, (((0,), (0,)), ((), ())), preferred_element_type=F32)


def _head_sum(z, bd):
    hi = z.astype(BF16)
    lo = (z - hi.astype(F32)).astype(BF16)
    return _dot(hi, bd) + _dot(lo, bd)


def _swap_halves(t):
    w = t.shape[1]
    lane = lax.broadcasted_iota(jnp.int32, t.shape, 1)
    return jnp.where(lane % HEAD_DIM < HEAD_DIM // 2, pltpu.roll(t, w - 32, 1), pltpu.roll(t, 32, 1))


def _qknorm_rope(t, g, cos, sin_s, bd):
    r = lax.rsqrt(_head_sum(t * t, bd) * (1.0 / HEAD_DIM) + EPS)
    n = (t * r) * g
    return n * cos + _swap_halves(n) * sin_s


def _qknorm_rope_bwd(dout, t, g, cos, sin_s, bd):
    dn = dout * cos + _swap_halves(dout * sin_s)
    r = lax.rsqrt(_head_sum(t * t, bd) * (1.0 / HEAD_DIM) + EPS)
    tr = t * r
    u = dn * g
    dt = r * (u - tr * (_head_sum(u * tr, bd) * (1.0 / HEAD_DIM)))
    return dt, dn * tr


def _sigmoid(g):
    return 1.0 / (1.0 + jnp.exp(-g))


def _expand_heads(st):
    t = st.shape[0]
    lane = lax.broadcasted_iota(jnp.int32, (t, LANES), 1)
    chunks = []
    for c in range(WIDTH // LANES):
        chunks.append(jnp.where(lane < HEAD_DIM, st[:, 2 * c:2 * c + 1], st[:, 2 * c + 1:2 * c + 2]))
    return jnp.concatenate(chunks, axis=1)


def _reduce_heads(z):
    t = z.shape[0]
    lane = lax.broadcasted_iota(jnp.int32, (t, LANES), 1)
    out = jnp.zeros((t, LANES), F32)
    for c in range(WIDTH // LANES):
        zc = z[:, c * LANES:(c + 1) * LANES]
        for ph in range(2):
            s = jnp.sum(jnp.where((lane // HEAD_DIM) == ph, zc, 0.0), axis=-1, keepdims=True)
            out = jnp.where(lane == 2 * c + ph, s, out)
    return out


def _fold_scratch(w):
    return pltpu.VMEM((w // LANES, ROW_TILE, LANES), F32)


def _store_folded(out_ref, val, scr):
    n = val.shape[1] // LANES
    for c in range(n):
        scr[c] = val[:, c * LANES:(c + 1) * LANES]
    for r in range(FOLD):
        piece = [scr[c, pl.ds(r, FOLD_ROWS, stride=FOLD), :] for c in range(n)]
        out_ref[r] = (piece[0] if n == 1 else jnp.concatenate(piece, axis=1)).astype(out_ref.dtype)


def _load_folded(in_ref, scr):
    n = in_ref.shape[2] // LANES
    for r in range(FOLD):
        blk = in_ref[r].astype(F32)
        for c in range(n):
            scr[c, pl.ds(r, FOLD_ROWS, stride=FOLD), :] = blk[:, c * LANES:(c + 1) * LANES]
    return scr[0] if n == 1 else jnp.concatenate([scr[c] for c in range(n)], axis=1)


def _rows(w, tm=ROW_TILE):
    return pl.BlockSpec((tm, w), lambda i: (i, 0))


def _folded_rows(w):
    return pl.BlockSpec((FOLD, FOLD_ROWS, w), lambda i: (0, i, 0))


def _whole(shape):
    return pl.BlockSpec(shape, lambda i: (0,) * len(shape))


def _inproj(x2, gain, w_bf, cos, sin_s, gqa, gka, gqb, gkb, bd512, bd128):
    s = x2.shape[0]
    tm = ROW_TILE

    def body(x_ref, gain_ref, w_hbm, cos_ref, sin_ref, gqa_ref, gka_ref, gqb_ref, gkb_ref, bd512_ref, bd128_ref,
             qa_ref, ka_ref, va_ref, qb_ref, kb_ref, vb_ref, qbf_ref, kbf_ref, vbf_ref,
             qa_raw_ref, ka_raw_ref, ga_ref, qb_raw_ref, kb_raw_ref, gb_ref, w_vmem, scr):
        @pl.when(pl.program_id(0) == 0)
        def _():
            pltpu.sync_copy(w_hbm, w_vmem)

        xt = x_ref[...]
        r = lax.rsqrt(jnp.mean(xt * xt, axis=-1, keepdims=True) + EPS)
        h = ((xt * r) * gain_ref[...]).astype(BF16)
        cos1 = cos_ref[...]
        sin1 = sin_ref[...]
        cos4 = jnp.tile(cos1, (1, 4))
        sin4 = jnp.tile(sin1, (1, 4))

        def seg(a, b):
            return _dot(h, w_vmem[:, a:b])

        t = seg(C_QA, C_KA)
        qa_raw_ref[...] = t
        qa_ref[...] = (_qknorm_rope(t, gqa_ref[...], cos4, sin4, bd512_ref[...]) * SCALE).astype(BF16)
        t = seg(C_KA, C_VA)
        ka_raw_ref[...] = t
        ka_ref[...] = _qknorm_rope(t, gka_ref[...], cos1, sin1, bd128_ref[...]).astype(BF16)
        va_ref[...] = seg(C_VA, C_GA).astype(BF16)
        ga_ref[...] = seg(C_GA, C_QB)
        t = seg(C_QB, C_KB)
        qb_raw_ref[...] = t
        t = _qknorm_rope(t, gqb_ref[...], cos4, sin4, bd512_ref[...]) * SCALE
        qb_ref[...] = t.astype(BF16)
        _store_folded(qbf_ref, t, scr)
        t = seg(C_KB, C_VB)
        kb_raw_ref[...] = t
        t = _qknorm_rope(t, gkb_ref[...], cos4, sin4, bd512_ref[...])
        kb_ref[...] = t.astype(BF16)
        _store_folded(kbf_ref, t, scr)
        t = seg(C_VB, C_GB)
        vb_ref[...] = t.astype(BF16)
        _store_folded(vbf_ref, t, scr)
        gb_ref[...] = seg(C_GB, C_END)

    sds = jax.ShapeDtypeStruct
    folded = sds((FOLD, s // FOLD, WIDTH), BF16)
    out_shape = (sds((s, WIDTH), BF16), sds((s, A_KV_WIDTH), BF16), sds((s, A_KV_WIDTH), BF16),
                 sds((s, WIDTH), BF16), sds((s, WIDTH), BF16), sds((s, WIDTH), BF16), folded, folded, folded,
                 sds((s, WIDTH), F32), sds((s, A_KV_WIDTH), F32), sds((s, WIDTH), F32),
                 sds((s, WIDTH), F32), sds((s, WIDTH), F32), sds((s, WIDTH), F32))
    out_specs = (_rows(WIDTH), _rows(A_KV_WIDTH), _rows(A_KV_WIDTH), _rows(WIDTH), _rows(WIDTH), _rows(WIDTH),
                 _folded_rows(WIDTH), _folded_rows(WIDTH), _folded_rows(WIDTH),
                 _rows(WIDTH), _rows(A_KV_WIDTH), _rows(WIDTH), _rows(WIDTH), _rows(WIDTH), _rows(WIDTH))
    return pl.pallas_call(
        body, name="inproj_fwd", grid=(s // tm,),
        in_specs=[_rows(D_MODEL), _whole(gain.shape), pl.BlockSpec(memory_space=pl.ANY), _rows(LANES), _rows(LANES),
                  _whole(gqa.shape), _whole(gka.shape), _whole(gqb.shape), _whole(gkb.shape), _whole(bd512.shape),
                  _whole(bd128.shape)],
        out_specs=out_specs, out_shape=out_shape,
        scratch_shapes=[pltpu.VMEM((D_MODEL, IN_WIDTH), BF16), _fold_scratch(WIDTH)],
        compiler_params=pltpu.CompilerParams(dimension_semantics=("arbitrary",), vmem_limit_bytes=VMEM_LIMIT),
    )(x2, gain, w_bf, cos, sin_s, gqa, gka, gqb, gkb, bd512, bd128)


def _seq_pos(idx, dil):
    if dil == 4:
        return 4 * (idx % 32) + idx // 32
    return idx


def _band_mask(i, max_dist, dil):
    qi = lax.broadcasted_iota(jnp.int32, (BLOCK, 2 * BLOCK), 0)
    sj = lax.broadcasted_iota(jnp.int32, (BLOCK, 2 * BLOCK), 1)
    dist = _seq_pos(qi, dil) + BLOCK - (_seq_pos(sj % BLOCK, dil) + BLOCK * (sj // BLOCK))
    return (dist >= 0) & (dist <= max_dist) & ((sj >= BLOCK) | (i > 0))


def _attn_view(a, dil):
    if dil == 1:
        return a[None]
    if dil == 4:
        return a.reshape(4, 4, a.shape[1], a.shape[2])
    return a


def _attn_unview(a, dil):
    if dil == 1:
        return a[0]
    if dil == 4:
        return a.reshape(FOLD, a.shape[2], a.shape[3])
    return a


def _attn_specs(dil, nb):
    if dil == 4:
        def spec(fn):
            return lambda w: pl.BlockSpec((4, None, BLOCK // 4, w), lambda r, i: (0, r, fn(i), 0))
    else:
        def spec(fn):
            return lambda w: pl.BlockSpec((None, BLOCK, w), lambda r, i: (r, fn(i), 0))
    return spec


def _blk_load(ref, sl, dil):
    if dil == 4:
        return ref[:, :, sl].reshape(BLOCK, sl.stop - sl.start)
    return ref[:, sl]


def _blk_store(ref, sl, val, dil):
    if dil == 4:
        ref[:, :, sl] = val.reshape(4, BLOCK // 4, sl.stop - sl.start)
    else:
        ref[:, sl] = val


def _kv_window(prev_ref, cur_ref, kc, dil):
    sl = slice(kc * LANES, (kc + 1) * LANES)
    return jnp.concatenate([_blk_load(prev_ref, sl, dil), _blk_load(cur_ref, sl, dil)], axis=0)


def _swap_heads(a):
    return pltpu.roll(a.astype(F32), HEAD_DIM, 1).astype(a.dtype)


def _attn_fwd(q, k, v, sinks, *, dil, max_dist, name):
    q, k, v = _attn_view(q, dil), _attn_view(k, dil), _attn_view(v, dil)
    kw = k.shape[-1]
    gqa = kw == A_KV_WIDTH
    n_seq = dil
    nb = (q.shape[-2] * (4 if dil == 4 else 1)) // BLOCK
    with_sinks = sinks is not None
    all_lanes = slice(0, LANES)

    def body(*refs):
        if with_sinks:
            q_ref, kp_ref, kc_ref, vp_ref, vc_ref, sink_ref, o_ref, lse_ref = refs
        else:
            q_ref, kp_ref, kc_ref, vp_ref, vc_ref, o_ref, lse_ref = refs
        valid = _band_mask(pl.program_id(1), max_dist, dil)
        lane = lax.broadcasted_iota(jnp.int32, (1, LANES), 1)
        lse_blk = jnp.zeros((BLOCK, LANES), F32)
        if gqa:
            k_plain, v_plain = _kv_window(kp_ref, kc_ref, 0, dil), _kv_window(vp_ref, vc_ref, 0, dil)
            k_swap, v_swap = _swap_heads(k_plain), _swap_heads(v_plain)
        for c in range(WIDTH // LANES):
            sl = slice(c * LANES, (c + 1) * LANES)
            q2 = _blk_load(q_ref, sl, dil)
            if not gqa:
                k2, v2 = _kv_window(kp_ref, kc_ref, c, dil), _kv_window(vp_ref, vc_ref, c, dil)
            o_pair = jnp.zeros((BLOCK, LANES), F32)
            for ph in range(2):
                h = 2 * c + ph
                if gqa:
                    kk, vv = (k_plain, v_plain) if (c // 2) == ph else (k_swap, v_swap)
                else:
                    kk, vv = k2, v2
                half = (lane // HEAD_DIM) == ph
                qm = jnp.where(half, q2, jnp.zeros_like(q2))
                sc = jnp.where(valid, _dot_nt(qm, kk), NEG)
                m = jnp.max(sc, axis=-1, keepdims=True)
                if with_sinks:
                    sk = sink_ref[0, h]
                    m = jnp.maximum(m, sk)
                p = jnp.exp(sc - m)
                l = jnp.sum(p, axis=-1, keepdims=True)
                if with_sinks:
                    l = l + jnp.exp(sk - m)
                o2 = _dot(p.astype(BF16), vv) / l
                o_pair = jnp.where(half, o2, o_pair)
                lse_blk = jnp.where(lane == h, m + jnp.log(l), lse_blk)
            _blk_store(o_ref, sl, o_pair, dil)
        _blk_store(lse_ref, all_lanes, lse_blk, dil)

    spec = _attn_specs(dil, nb)
    cur = spec(lambda i: i)
    prev = spec(lambda i: jnp.maximum(i - 1, 0))
    in_specs = [cur(WIDTH), prev(kw), cur(kw), prev(kw), cur(kw)]
    args = [q, k, k, v, v]
    if with_sinks:
        in_specs.append(pl.BlockSpec(memory_space=pltpu.SMEM))
        args.append(sinks)
    o, lse = pl.pallas_call(
        body, name=name, grid=(n_seq, nb), in_specs=in_specs,
        out_specs=(cur(WIDTH), cur(LANES)),
        out_shape=(jax.ShapeDtypeStruct(q.shape, F32), jax.ShapeDtypeStruct(q.shape[:-1] + (LANES,), F32)),
        compiler_params=pltpu.CompilerParams(dimension_semantics=("arbitrary", "arbitrary")),
    )(*args)
    return _attn_unview(o, dil), _attn_unview(lse, dil)


def _attn_bwd(q, k, v, do, lse, delta, *, dil, max_dist, name):
    q, k, v, do, lse, delta = (_attn_view(a, dil) for a in (q, k, v, do, lse, delta))
    kw = k.shape[-1]
    gqa = kw == A_KV_WIDTH
    n_seq = dil
    nb = (q.shape[-2] * (4 if dil == 4 else 1)) // BLOCK
    n_kc = kw // LANES
    all_lanes = slice(0, LANES)

    def body(q_ref, kp_ref, kc_ref, vp_ref, vc_ref, do_ref, lse_ref, dl_ref, dq_ref, dk_ref, dv_ref, ck_ref, cv_ref):
        i = pl.program_id(1)

        @pl.when(i == 0)
        def _():
            ck_ref[...] = jnp.zeros_like(ck_ref)
            cv_ref[...] = jnp.zeros_like(cv_ref)

        @pl.when(i < nb)
        def _():
            valid = _band_mask(i, max_dist, dil)
            lane = lax.broadcasted_iota(jnp.int32, (1, LANES), 1)
            lse_blk = _blk_load(lse_ref, all_lanes, dil)
            dl_blk = _blk_load(dl_ref, all_lanes, dil)
            dk_acc = [jnp.zeros((2 * BLOCK, LANES), F32) for _ in range(n_kc)]
            dv_acc = [jnp.zeros((2 * BLOCK, LANES), F32) for _ in range(n_kc)]
            if gqa:
                k_plain, v_plain = _kv_window(kp_ref, kc_ref, 0, dil), _kv_window(vp_ref, vc_ref, 0, dil)
                k_swap, v_swap = _swap_heads(k_plain), _swap_heads(v_plain)
            for c in range(WIDTH // LANES):
                sl = slice(c * LANES, (c + 1) * LANES)
                q2 = _blk_load(q_ref, sl, dil)
                do2 = _blk_load(do_ref, sl, dil)
                kc = 0 if gqa else c
                if not gqa:
                    k2, v2 = _kv_window(kp_ref, kc_ref, c, dil), _kv_window(vp_ref, vc_ref, c, dil)
                dq_pair = jnp.zeros((BLOCK, LANES), F32)
                for ph in range(2):
                    h = 2 * c + ph
                    swapped = gqa and (c // 2) != ph
                    if gqa:
                        kk, vv = (k_swap, v_swap) if swapped else (k_plain, v_plain)
                    else:
                        kk, vv = k2, v2
                    half = (lane // HEAD_DIM) == ph
                    qm = jnp.where(half, q2, jnp.zeros_like(q2))
                    dom = jnp.where(half, do2, jnp.zeros_like(do2))
                    sc = jnp.where(valid, _dot_nt(qm, kk), NEG)
                    p = jnp.exp(sc - lse_blk[:, h:h + 1])
                    dp = _dot_nt(dom, vv)
                    ds = (p * (dp - dl_blk[:, h:h + 1])).astype(BF16)
                    dq_pair = jnp.where(half, _dot(ds, kk), dq_pair)
                    dkc = _dot_tn(ds, qm)
                    dvc = _dot_tn(p.astype(BF16), dom)
                    if swapped:
                        dkc, dvc = _swap_heads(dkc), _swap_heads(dvc)
                    dk_acc[kc] = dk_acc[kc] + dkc
                    dv_acc[kc] = dv_acc[kc] + dvc
                _blk_store(dq_ref, sl, dq_pair * SCALE, dil)
            for kc in range(n_kc):
                sl = slice(kc * LANES, (kc + 1) * LANES)
                _blk_store(dk_ref, sl, ck_ref[:, sl] + dk_acc[kc][:BLOCK], dil)
                _blk_store(dv_ref, sl, cv_ref[:, sl] + dv_acc[kc][:BLOCK], dil)
                ck_ref[:, sl] = dk_acc[kc][BLOCK:]
                cv_ref[:, sl] = dv_acc[kc][BLOCK:]

        @pl.when(i == nb)
        def _():
            for kc in range(n_kc):
                sl = slice(kc * LANES, (kc + 1) * LANES)
                _blk_store(dk_ref, sl, ck_ref[:, sl], dil)
                _blk_store(dv_ref, sl, cv_ref[:, sl], dil)

    spec = _attn_specs(dil, nb)
    cur = spec(lambda i: jnp.minimum(i, nb - 1))
    prev = spec(lambda i: jnp.clip(i - 1, 0, nb - 1))
    lag = spec(lambda i: jnp.maximum(i - 1, 0))
    sds = jax.ShapeDtypeStruct
    dq, dk, dv = pl.pallas_call(
        body, name=name, grid=(n_seq, nb + 1),
        in_specs=[cur(WIDTH), prev(kw), cur(kw), prev(kw), cur(kw), cur(WIDTH), cur(LANES), cur(LANES)],
        out_specs=(cur(WIDTH), lag(kw), lag(kw)),
        out_shape=(sds(q.shape, F32), sds(k.shape, F32), sds(k.shape, F32)),
        scratch_shapes=[pltpu.VMEM((BLOCK, kw), F32), pltpu.VMEM((BLOCK, kw), F32)],
        compiler_params=pltpu.CompilerParams(dimension_semantics=("arbitrary", "arbitrary")),
    )(q, k, k, v, v, do, lse, delta)
    return _attn_unview(dq, dil), _attn_unview(dk, dil), _attn_unview(dv, dil)


def _outproj(o_a, lse_a, o_b1, o_b4, o_b16, lse_b1, lse_b4, lse_b16, g_a, g_b, x2, tgt2, w_out_bf, sink_row):
    s = x2.shape[0]
    tm = ROW_TILE

    def body(oa_ref, lsea_ref, ob1_ref, ob4_ref, ob16_ref, ls1_ref, ls4_ref, ls16_ref, ga_ref, gb_ref, x_ref, t_ref,
             w_ref, sink_ref,
             dy_ref, doa_ref, dob_ref, dobf_ref, dga_ref, dgb_ref, lseb_ref, lsebf_ref, dla_ref, dlb_ref, dlbf_ref,
             gw_ref, loss_ref, dsink_ref, scr, scr_st):
        i = pl.program_id(0)

        @pl.when(i == 0)
        def _():
            gw_ref[...] = jnp.zeros_like(gw_ref)
            loss_ref[...] = jnp.zeros_like(loss_ref)
            dsink_ref[...] = jnp.zeros_like(dsink_ref)

        ls = [ls1_ref[...], _load_folded(ls4_ref, scr_st), _load_folded(ls16_ref, scr_st)]
        mx = jnp.maximum(jnp.maximum(ls[0], ls[1]), ls[2])
        lse_b = mx + jnp.log(jnp.exp(ls[0] - mx) + jnp.exp(ls[1] - mx) + jnp.exp(ls[2] - mx))
        lane = lax.broadcasted_iota(jnp.int32, (tm, LANES), 1)
        lse_b = jnp.where(lane < HEADS, lse_b, 0.0)
        lseb_ref[...] = lse_b
        _store_folded(lsebf_ref, lse_b, scr_st)
        o_b = _expand_heads(jnp.exp(ls[0] - lse_b)) * ob1_ref[...]
        o_b = o_b + _expand_heads(jnp.exp(ls[1] - lse_b)) * _load_folded(ob4_ref, scr)
        o_b = o_b + _expand_heads(jnp.exp(ls[2] - lse_b)) * _load_folded(ob16_ref, scr)
        o_a = oa_ref[...]
        g_a = ga_ref[...]
        g_b = gb_ref[...]
        sg_a = _sigmoid(g_a)
        sg_b = _sigmoid(g_b)
        silu_a = g_a * sg_a
        silu_b = g_b * sg_b
        mixed = jnp.concatenate([o_a * silu_a, o_b * silu_b], axis=1).astype(BF16)
        w = w_ref[...]
        y = x_ref[...] + _dot(mixed, w)
        diff = y - t_ref[...]
        loss_ref[...] += (0.5 / D_MODEL) * jnp.sum(diff * diff)
        dy = diff * (1.0 / D_MODEL)
        dy_ref[...] = dy
        dyb = dy.astype(BF16)
        gw_ref[...] += _dot_tn(mixed, dyb)
        dmixed = _dot_nt(dyb, w)
        dm_a = dmixed[:, :WIDTH]
        dm_b = dmixed[:, WIDTH:]
        do_a = dm_a * silu_a
        do_b = dm_b * silu_b
        doa_ref[...] = do_a.astype(BF16)
        dob_ref[...] = do_b.astype(BF16)
        _store_folded(dobf_ref, do_b, scr)
        dga_ref[...] = (dm_a * o_a * (sg_a * (1.0 + g_a * (1.0 - sg_a)))).astype(BF16)
        dgb_ref[...] = (dm_b * o_b * (sg_b * (1.0 + g_b * (1.0 - sg_b)))).astype(BF16)
        dl_a = _reduce_heads(do_a * o_a)
        dla_ref[...] = dl_a
        dl_b = _reduce_heads(do_b * o_b)
        dlb_ref[...] = dl_b
        _store_folded(dlbf_ref, dl_b, scr_st)
        dsink_ref[...] -= jnp.sum(jnp.exp(sink_ref[...] - lsea_ref[...]) * dl_a, axis=0, keepdims=True)

    sds = jax.ShapeDtypeStruct
    ln = s // FOLD
    return pl.pallas_call(
        body, name="outproj_fwd_bwd", grid=(s // tm,),
        in_specs=[_rows(WIDTH), _rows(LANES), _rows(WIDTH), _folded_rows(WIDTH), _folded_rows(WIDTH), _rows(LANES),
                  _folded_rows(LANES), _folded_rows(LANES), _rows(WIDTH), _rows(WIDTH), _rows(D_MODEL),
                  _rows(D_MODEL), _whole((D_MODEL, D_MODEL)), _whole((1, LANES))],
        out_specs=(_rows(D_MODEL), _rows(WIDTH), _rows(WIDTH), _folded_rows(WIDTH), _rows(WIDTH), _rows(WIDTH),
                   _rows(LANES), _folded_rows(LANES), _rows(LANES), _rows(LANES), _folded_rows(LANES),
                   _whole((D_MODEL, D_MODEL)), _whole((1, LANES)), _whole((1, LANES))),
        out_shape=(sds((s, D_MODEL), F32), sds((s, WIDTH), BF16), sds((s, WIDTH), BF16),
                   sds((FOLD, ln, WIDTH), BF16), sds((s, WIDTH), BF16), sds((s, WIDTH), BF16),
                   sds((s, LANES), F32), sds((FOLD, ln, LANES), F32), sds((s, LANES), F32), sds((s, LANES), F32),
                   sds((FOLD, ln, LANES), F32),
                   sds((D_MODEL, D_MODEL), F32), sds((1, LANES), F32), sds((1, LANES), F32)),
        scratch_shapes=[_fold_scratch(WIDTH), _fold_scratch(LANES)],
        compiler_params=pltpu.CompilerParams(dimension_semantics=("arbitrary",), vmem_limit_bytes=VMEM_LIMIT),
    )(o_a, lse_a, o_b1, o_b4, o_b16, lse_b1, lse_b4, lse_b16, g_a, g_b, x2, tgt2, w_out_bf, sink_row)


def _inproj_bwd(x2, dy, gain, w_bf, cos, sin_s, gqa, gka, gqb, gkb, bd512, bd128,
                qa_raw, ka_raw, qb_raw, kb_raw, dq_a, dk_a, dv_a, dqkv_b1, dqkv_b4, dqkv_b16, dg_a, dg_b):
    s = x2.shape[0]
    tm = ROW_TILE

    def body(x_ref, dy_ref, gain_ref, w_hbm, cos_ref, sin_ref, gqa_ref, gka_ref, gqb_ref, gkb_ref, bd512_ref,
             bd128_ref, qa_raw_ref, ka_raw_ref, qb_raw_ref, kb_raw_ref, dqa_ref, dka_ref, dva_ref,
             dq1_ref, dk1_ref, dv1_ref, dq4_ref, dk4_ref, dv4_ref, dq16_ref, dk16_ref, dv16_ref, dga_ref, dgb_ref,
             gx_ref, ht_ref, win_ref,
             dgain_ref, dgqa_ref, dgka_ref, dgqb_ref, dgkb_ref, w_vmem, dproj_ref, scr):
        i = pl.program_id(0)

        @pl.when(i == 0)
        def _():
            pltpu.sync_copy(w_hbm, w_vmem)
            dgain_ref[...] = jnp.zeros_like(dgain_ref)
            dgqa_ref[...] = jnp.zeros_like(dgqa_ref)
            dgka_ref[...] = jnp.zeros_like(dgka_ref)
            dgqb_ref[...] = jnp.zeros_like(dgqb_ref)
            dgkb_ref[...] = jnp.zeros_like(dgkb_ref)

        cos1 = cos_ref[...]
        sin1 = sin_ref[...]
        cos4 = jnp.tile(cos1, (1, 4))
        sin4 = jnp.tile(sin1, (1, 4))

        dt, dg = _qknorm_rope_bwd(dqa_ref[...], qa_raw_ref[...], gqa_ref[...], cos4, sin4, bd512_ref[...])
        dproj_ref[:, C_QA:C_KA] = dt.astype(BF16)
        dgqa_ref[...] += jnp.sum(dg, axis=0, keepdims=True)
        dt, dg = _qknorm_rope_bwd(dka_ref[...], ka_raw_ref[...], gka_ref[...], cos1, sin1, bd128_ref[...])
        dproj_ref[:, C_KA:C_VA] = dt.astype(BF16)
        dgka_ref[...] += jnp.sum(dg, axis=0, keepdims=True)
        dproj_ref[:, C_VA:C_GA] = dva_ref[...].astype(BF16)
        dproj_ref[:, C_GA:C_QB] = dga_ref[...]
        dq = (dq1_ref[...] + _load_folded(dq4_ref, scr)) + _load_folded(dq16_ref, scr)
        dt, dg = _qknorm_rope_bwd(dq, qb_raw_ref[...], gqb_ref[...], cos4, sin4, bd512_ref[...])
        dproj_ref[:, C_QB:C_KB] = dt.astype(BF16)
        dgqb_ref[...] += jnp.sum(dg, axis=0, keepdims=True)
        dk = (dk1_ref[...] + _load_folded(dk4_ref, scr)) + _load_folded(dk16_ref, scr)
        dt, dg = _qknorm_rope_bwd(dk, kb_raw_ref[...], gkb_ref[...], cos4, sin4, bd512_ref[...])
        dproj_ref[:, C_KB:C_VB] = dt.astype(BF16)
        dgkb_ref[...] += jnp.sum(dg, axis=0, keepdims=True)
        dv = (dv1_ref[...] + _load_folded(dv4_ref, scr)) + _load_folded(dv16_ref, scr)
        dproj_ref[:, C_VB:C_GB] = dv.astype(BF16)
        dproj_ref[:, C_GB:C_END] = dgb_ref[...]
        for k, start in enumerate(WIN_START):
            win_ref[k] = dproj_ref[:, start:start + WIN]

        xt = x_ref[...]
        gain_row = gain_ref[...]
        r = lax.rsqrt(jnp.mean(xt * xt, axis=-1, keepdims=True) + EPS)
        xr = xt * r
        ht_ref[...] = (xr * gain_row).T.astype(BF16)
        dh = _dot_nt(dproj_ref[...], w_vmem[...])
        dgain_ref[...] += jnp.sum(dh * xr, axis=0, keepdims=True)
        u = dh * gain_row
        gx_ref[...] = dy_ref[...] + r * (u - xr * jnp.mean(u * xr, axis=-1, keepdims=True))

    def acc_row(w):
        return pl.BlockSpec((1, w), lambda i: (0, 0))

    sds = jax.ShapeDtypeStruct
    any_spec = pl.BlockSpec(memory_space=pl.ANY)
    win_spec = pl.BlockSpec((N_CHIP, tm, WIN), lambda i: (0, i, 0))
    return pl.pallas_call(
        body, name="inproj_bwd", grid=(s // tm,),
        in_specs=[_rows(D_MODEL), _rows(D_MODEL), _whole(gain.shape), any_spec, _rows(LANES), _rows(LANES),
                  _whole(gqa.shape), _whole(gka.shape), _whole(gqb.shape), _whole(gkb.shape), _whole(bd512.shape),
                  _whole(bd128.shape),
                  _rows(WIDTH), _rows(A_KV_WIDTH), _rows(WIDTH), _rows(WIDTH),
                  _rows(WIDTH), _rows(A_KV_WIDTH), _rows(A_KV_WIDTH)]
                 + [_rows(WIDTH)] * 3 + [_folded_rows(WIDTH)] * 6 + [_rows(WIDTH), _rows(WIDTH)],
        out_specs=(_rows(D_MODEL), pl.BlockSpec((D_MODEL, tm), lambda i: (0, i)), win_spec, acc_row(D_MODEL), acc_row(WIDTH), acc_row(A_KV_WIDTH), acc_row(WIDTH), acc_row(WIDTH)),
        out_shape=(sds((s, D_MODEL), F32), sds((D_MODEL, s), BF16), sds((N_CHIP, s, WIN), BF16),
                   sds((1, D_MODEL), F32),
                   sds((1, WIDTH), F32), sds((1, A_KV_WIDTH), F32), sds((1, WIDTH), F32), sds((1, WIDTH), F32)),
        scratch_shapes=[pltpu.VMEM((D_MODEL, IN_WIDTH), BF16), pltpu.VMEM((tm, IN_WIDTH), BF16),
                        _fold_scratch(WIDTH)],
        compiler_params=pltpu.CompilerParams(dimension_semantics=("arbitrary",), vmem_limit_bytes=VMEM_LIMIT),
    )(x2, dy, gain, w_bf, cos, sin_s, gqa, gka, gqb, gkb, bd512, bd128, qa_raw, ka_raw, qb_raw, kb_raw,
      dq_a, dk_a, dv_a, *dqkv_b1, *dqkv_b4, *dqkv_b16, dg_a, dg_b)


def _rope_tables(s):
    half = HEAD_DIM // 2
    inv = ROPE_THETA ** (-jnp.arange(half, dtype=F32) / half)
    ang = jnp.arange(s).astype(F32)[:, None] * inv[None, :]
    cos = jnp.cos(ang)
    sin = jnp.sin(ang)
    return jnp.tile(cos, (1, 4)), jnp.concatenate([-sin, sin, -sin, sin], axis=1)


def _block_diag_ones(w):
    idx = jnp.arange(w) // HEAD_DIM
    return (idx[:, None] == idx[None, :]).astype(BF16)


def _local_step(x2, tgt2, norm_gain, w_in_bf, q_norm_a, k_norm_a, sinks_a, q_norm_b, k_norm_b, w_out_bf):
    s = x2.shape[0]
    cos, sin_s = _rope_tables(s)
    bd512, bd128 = _block_diag_ones(WIDTH), _block_diag_ones(A_KV_WIDTH)
    gqa = jnp.tile(q_norm_a, (1, HEADS))
    gka = jnp.tile(k_norm_a, (1, 2))
    gqb = jnp.tile(q_norm_b, (1, HEADS))
    gkb = jnp.tile(k_norm_b, (1, HEADS))
    sink_row = jnp.pad(sinks_a, ((0, 0), (0, LANES - HEADS)))

    (qa, ka, va, qb, kb, vb, qbf, kbf, vbf, qa_raw, ka_raw, g_a, qb_raw, kb_raw, g_b) = _inproj(
        x2, norm_gain, w_in_bf, cos, sin_s, gqa, gka, gqb, gkb, bd512, bd128)

    o_a, lse_a = _attn_fwd(qa, ka, va, sinks_a, dil=1, max_dist=A_MAX_DIST, name="attn_a_fwd")
    o_b1, lse_b1 = _attn_fwd(qb, kb, vb, None, dil=1, max_dist=B_MAX_DIST, name="attn_b1_fwd")
    o_b4, lse_b4 = _attn_fwd(qbf, kbf, vbf, None, dil=4, max_dist=B_MAX_DIST, name="attn_b4_fwd")
    o_b16, lse_b16 = _attn_fwd(qbf, kbf, vbf, None, dil=16, max_dist=B_MAX_DIST, name="attn_b16_fwd")

    (dy, do_a, do_b, do_bf, dg_a, dg_b, lse_b, lse_bf, dl_a, dl_b, dl_bf, gw_out, loss_part, dsink) = _outproj(
        o_a, lse_a, o_b1, o_b4, o_b16, lse_b1, lse_b4, lse_b16, g_a, g_b, x2, tgt2, w_out_bf, sink_row)

    dq_a, dk_a, dv_a = _attn_bwd(qa, ka, va, do_a, lse_a, dl_a, dil=1, max_dist=A_MAX_DIST, name="attn_a_bwd")
    d_b1 = _attn_bwd(qb, kb, vb, do_b, lse_b, dl_b, dil=1, max_dist=B_MAX_DIST, name="attn_b1_bwd")
    d_b4 = _attn_bwd(qbf, kbf, vbf, do_bf, lse_bf, dl_bf, dil=4, max_dist=B_MAX_DIST, name="attn_b4_bwd")
    d_b16 = _attn_bwd(qbf, kbf, vbf, do_bf, lse_bf, dl_bf, dil=16, max_dist=B_MAX_DIST, name="attn_b16_bwd")

    gx, h_t, wins, dgain, dgqa, dgka, dgqb, dgkb = _inproj_bwd(
        x2, dy, norm_gain, w_in_bf, cos, sin_s, gqa, gka, gqb, gkb, bd512, bd128,
        qa_raw, ka_raw, qb_raw, kb_raw, dq_a, dk_a, dv_a, d_b1, d_b4, d_b16, dg_a, dg_b)
    return loss_part, gx, h_t, wins, gw_out, (dgain, dgqa, dgka, dsink, dgqb, dgkb)


def _position():
    return lax.axis_index("x"), lax.axis_index("y"), lax.axis_index("c")


GATHER_CHUNKS = 2


def _gather_weights(blocks, name):
    n = len(blocks)
    ch = GATHER_CHUNKS
    n_sems = n * (N_CHIP - 1) * ch

    def body(*refs):
        src_refs, dst_refs = refs[:n], refs[n:2 * n]
        ici_send, ici_recv, d2d_send, d2d_recv, local_sems = refs[2 * n:]
        x, y, c = _position()
        b = 2 * x + y
        copies = []
        for k in range(n):
            local = pltpu.make_async_copy(src_refs[k], dst_refs[k].at[b], local_sems.at[k])
            local.start()
            copies.append(local)

        def rows(k, core, j):
            half = blocks[k].shape[0] // 2
            return pl.ds(core * half + j * (half // ch), half // ch)

        plan = []
        for d in range(1, N_CHIP):
            px, py = x ^ (d >> 1), y ^ (d & 1)
            for j in range(ch):
                for k in range(n):
                    plan.append((px, py, 2 * px + py, k, j, ((d - 1) * ch + j) * n + k))
        sends = []
        for px, py, pb, k, j, sem in plan:
            send = pltpu.make_async_remote_copy(
                src_ref=src_refs[k].at[rows(k, c, j)], dst_ref=dst_refs[k].at[b, rows(k, c, j)],
                send_sem=ici_send.at[sem], recv_sem=ici_recv.at[sem], device_id=(px, py, c), device_id_type=MESH)
            send.start()
            sends.append(send)
        for px, py, pb, k, j, sem in plan:
            landed = dst_refs[k].at[pb, rows(k, c, j)]
            pltpu.make_async_remote_copy(
                src_ref=landed, dst_ref=landed, send_sem=ici_send.at[sem], recv_sem=ici_recv.at[sem],
                device_id=(px, py, c), device_id_type=MESH).wait_recv()
            forward = pltpu.make_async_remote_copy(
                src_ref=landed, dst_ref=landed, send_sem=d2d_send.at[sem], recv_sem=d2d_recv.at[sem],
                device_id=(x, y, 1 - c), device_id_type=MESH)
            forward.start()
            sends.append(forward)
        for px, py, pb, k, j, sem in plan:
            passed = dst_refs[k].at[pb, rows(k, 1 - c, j)]
            pltpu.make_async_remote_copy(
                src_ref=passed, dst_ref=passed, send_sem=d2d_send.at[sem], recv_sem=d2d_recv.at[sem],
                device_id=(x, y, 1 - c), device_id_type=MESH).wait_recv()
        for send in sends:
            send.wait_send()
        for local in copies:
            local.wait()

    vmem_spec = pl.BlockSpec(memory_space=pltpu.VMEM)
    out_shape = tuple(jax.ShapeDtypeStruct((N_CHIP,) + a.shape, a.dtype) for a in blocks)
    return pl.pallas_call(
        body, name=name, in_specs=[vmem_spec] * n, out_specs=tuple([vmem_spec] * n), out_shape=out_shape,
        scratch_shapes=[pltpu.SemaphoreType.DMA((n_sems,)) for _ in range(4)] + [pltpu.SemaphoreType.DMA((n,))],
        compiler_params=pltpu.CompilerParams(vmem_limit_bytes=VMEM_LIMIT),
    )(*blocks)


def _grad_reduce(order, h_t, wins, gw_out, small):
    s = h_t.shape[1]
    tk = GRAD_ROWS
    n_i = s // tk
    half = D_MODEL // 2
    o_half = OUT_ROWS // 2
    n_rel = N_CHIP - 1

    def body(order_ref, ht_ref, win_ref, gwo_ref, small_ref,
             win_out, wout_out, small_out,
             acc, mine, s1, r1, s2, r2, so1, ro1, so2, ro2, pair_in, pair_o, small_land,
             s1_send, s1_recv, s2_send, s2_recv, o1_send, o1_recv, o2_send, o2_recv,
             pair_send, pair_recv, small_send, small_recv):
        j = pl.program_id(0)
        i = pl.program_id(1)
        x, y, c = _position()
        me = 4 * x + 2 * y + c
        sibling = (x, y, 1 - c)
        my_rows = pl.ds(pl.multiple_of(c * half, half), half)
        sib_rows = pl.ds(pl.multiple_of((1 - c) * half, half), half)

        def chip_of(rel):
            return x ^ (rel >> 1), y ^ (rel & 1)

        def level1(k):
            return pltpu.make_async_remote_copy(src_ref=s1.at[k], dst_ref=r1.at[k], send_sem=s1_send.at[k],
                                                recv_sem=s1_recv.at[k], device_id=sibling, device_id_type=MESH)

        def level2(k):
            px, py = chip_of(RELATIONS[k])
            return pltpu.make_async_remote_copy(src_ref=s2.at[k], dst_ref=r2.at[k], send_sem=s2_send.at[k],
                                                recv_sem=s2_recv.at[k], device_id=(px, py, c), device_id_type=MESH)

        def out_level1(bk):
            return pltpu.make_async_remote_copy(src_ref=so1.at[bk], dst_ref=ro1.at[bk], send_sem=o1_send.at[bk],
                                                recv_sem=o1_recv.at[bk], device_id=sibling, device_id_type=MESH)

        def out_level2(k):
            px, py = chip_of(RELATIONS[k])
            return pltpu.make_async_remote_copy(src_ref=so2.at[k], dst_ref=ro2.at[k], send_sem=o2_send.at[k],
                                                recv_sem=o2_recv.at[k], device_id=(px, py, c), device_id_type=MESH)

        def small_copy(d):
            px, py, pc = x ^ (d >> 2), y ^ ((d >> 1) & 1), c ^ (d & 1)
            return pltpu.make_async_remote_copy(src_ref=small_ref, dst_ref=small_land.at[me],
                                                send_sem=small_send.at[d], recv_sem=small_recv.at[d],
                                                device_id=(px, py, pc), device_id_type=MESH)

        def pair_copy(k, buf):
            return pltpu.make_async_remote_copy(src_ref=buf.at[0], dst_ref=buf.at[1], send_sem=pair_send.at[k],
                                                recv_sem=pair_recv.at[k], device_id=sibling, device_id_type=MESH)

        def out_rows(bk, core):
            return pl.ds(pl.multiple_of(bk * OUT_ROWS + core * o_half, o_half), o_half)

        @pl.when((j == 0) & (i == 0))
        def _():
            for d in range(1, N_DEV):
                small_copy(d).start()
            small_land[me] = small_ref[...]
            for bk in range(N_CHIP):
                so1[bk] = gwo_ref[out_rows(bk, 1 - c), :].astype(BF16)
                out_level1(bk).start()

        @pl.when((j == 0) & (i == 1))
        def _():
            b = 2 * x + y
            for bk in range(N_CHIP):
                out_level1(bk).wait_recv()
            for k in range(n_rel):
                px, py = chip_of(RELATIONS[k])
                bk = 2 * px + py
                so2[k] = (gwo_ref[out_rows(bk, c), :] + ro1[bk].astype(F32)).astype(BF16)
                out_level2(k).start()

        contrib = _dot(ht_ref[...], win_ref[...])

        @pl.when(i == 0)
        def _():
            acc[...] = contrib

        @pl.when(i > 0)
        def _():
            acc[...] += contrib

        for k in range(N_CHIP):
            @pl.when((j == k) & (i == n_i - 1))
            def _(k=k):
                s1[k] = acc[sib_rows, :].astype(BF16)
                level1(k).start()
                mine[...] = acc[my_rows, :]

            if k < n_rel:
                @pl.when((j == k + 1) & (i == 1))
                def _(k=k):
                    level1(k).wait_recv()
                    s2[k] = (mine[...] + r1[k].astype(F32)).astype(BF16)
                    level2(k).start()

        @pl.when((j == N_CHIP - 1) & (i == n_i - 1))
        def _():
            b = 2 * x + y
            level1(N_CHIP - 1).wait_recv()
            total = mine[...] + r1[N_CHIP - 1].astype(F32)
            for k in range(n_rel):
                level2(k).wait_recv()
                total = total + r2[k].astype(F32)
            pair_in[0] = total
            pair_copy(0, pair_in).start()
            total_o = gwo_ref[out_rows(b, c), :] + ro1[b].astype(F32)
            for k in range(n_rel):
                out_level2(k).wait_recv()
                total_o = total_o + ro2[k].astype(F32)
            pair_o[0] = total_o
            pair_copy(1, pair_o).start()
            win_out[c] = total
            wout_out[c] = total_o
            for d in range(1, N_DEV):
                small_copy(d).wait_recv()
            small_out[...] = small_land[...]
            pair_copy(0, pair_in).wait_recv()
            win_out[1 - c] = pair_in[1]
            pair_copy(1, pair_o).wait_recv()
            wout_out[1 - c] = pair_o[1]
            for d in range(1, N_DEV):
                small_copy(d).wait_send()
            for k in range(N_CHIP):
                level1(k).wait_send()
                out_level1(k).wait_send()
            for k in range(n_rel):
                level2(k).wait_send()
                out_level2(k).wait_send()
            pair_copy(0, pair_in).wait_send()
            pair_copy(1, pair_o).wait_send()

    vmem = pl.BlockSpec(memory_space=pltpu.VMEM)
    dma = pltpu.SemaphoreType.DMA
    sds = jax.ShapeDtypeStruct
    grid_spec = pltpu.PrefetchScalarGridSpec(
        num_scalar_prefetch=1, grid=(N_CHIP, n_i),
        in_specs=[pl.BlockSpec((D_MODEL, tk), lambda j, i, order: (0, i)),
                  pl.BlockSpec((None, tk, WIN), lambda j, i, order: (order[j], i, 0)), vmem, vmem],
        out_specs=(vmem, vmem, vmem),
        scratch_shapes=[
            pltpu.VMEM((D_MODEL, WIN), F32), pltpu.VMEM((half, WIN), F32),
            pltpu.VMEM((N_CHIP, half, WIN), BF16), pltpu.VMEM((N_CHIP, half, WIN), BF16),
            pltpu.VMEM((n_rel, half, WIN), BF16), pltpu.VMEM((n_rel, half, WIN), BF16),
            pltpu.VMEM((N_CHIP, o_half, D_MODEL), BF16), pltpu.VMEM((N_CHIP, o_half, D_MODEL), BF16),
            pltpu.VMEM((n_rel, o_half, D_MODEL), BF16), pltpu.VMEM((n_rel, o_half, D_MODEL), BF16),
            pltpu.VMEM((2, half, WIN), F32), pltpu.VMEM((2, o_half, D_MODEL), F32),
            pltpu.VMEM((N_DEV, PACK_ROWS, D_MODEL), F32),
            dma((N_CHIP,)), dma((N_CHIP,)), dma((n_rel,)), dma((n_rel,)),
            dma((N_CHIP,)), dma((N_CHIP,)), dma((n_rel,)), dma((n_rel,)),
            dma((2,)), dma((2,)), dma((N_DEV,)), dma((N_DEV,))])
    return pl.pallas_call(
        body, name="grad_w_in_reduce", grid_spec=grid_spec,
        out_shape=(sds((2, half, WIN), F32), sds((2, o_half, D_MODEL), F32), sds((N_DEV, PACK_ROWS, D_MODEL), F32)),
        compiler_params=pltpu.CompilerParams(dimension_semantics=("arbitrary", "arbitrary"),
                                             vmem_limit_bytes=VMEM_LIMIT),
    )(order, h_t, wins, gw_out, small)


ADAM_ROWS = 128


def _adamw_math(w, g, m, v):
    m = ADAM_B1 * m + (1.0 - ADAM_B1) * g
    v = ADAM_B2 * v + (1.0 - ADAM_B2) * (g * g)
    m_hat = m / (1.0 - ADAM_B1 ** ADAM_STEP)
    v_hat = v / (1.0 - ADAM_B2 ** ADAM_STEP)
    delta = -ADAM_LR * (m_hat / (jnp.sqrt(v_hat) + ADAM_EPS) + ADAM_WD * w)
    return delta, m, v


def _adamw(w, g, m, v, name):
    r, c = w.shape

    def body(w_ref, g_ref, m_ref, v_ref, d_ref, nm_ref, nv_ref):
        delta, nm, nv = _adamw_math(w_ref[...], g_ref[...], m_ref[...], v_ref[...])
        d_ref[...] = delta
        nm_ref[...] = nm
        nv_ref[...] = nv

    spec = pl.BlockSpec((ADAM_ROWS, c), lambda i: (i, 0))
    shape = jax.ShapeDtypeStruct((r, c), F32)
    return pl.pallas_call(
        body, name=name, grid=(r // ADAM_ROWS,), in_specs=[spec] * 4, out_specs=(spec,) * 3,
        out_shape=(shape,) * 3,
    )(w, g, m, v)


PACK_ROWS = 8


def _fold_heads(v):
    y = v[:, 0:LANES]
    for j in range(1, v.shape[1] // LANES):
        y = y + v[:, j * LANES:(j + 1) * LANES]
    return y + pltpu.roll(y, HEAD_DIM, 1)


def _small_adamw(recv, w_p, m_p, v_p):
    def body(r_ref, w_ref, m_ref, v_ref, g_ref, d_ref, nm_ref, nv_ref):
        tot = r_ref[0]
        for j in range(1, N_DEV):
            tot = tot + r_ref[j]
        row1 = tot[1:2, :]
        row2 = tot[2:3, :]
        pieces = [_fold_heads(row1[:, 0:WIDTH]), _fold_heads(row2[:, WIDTH:WIDTH + A_KV_WIDTH]),
                  _fold_heads(row1[:, WIDTH:2 * WIDTH]), _fold_heads(row2[:, 0:WIDTH]),
                  row2[:, WIDTH + A_KV_WIDTH:WIDTH + 2 * A_KV_WIDTH], jnp.zeros((1, 3 * LANES), F32)]
        g = jnp.concatenate([tot[0:1, :], jnp.concatenate(pieces, axis=1), jnp.zeros((PACK_ROWS - 2, D_MODEL), F32)],
                            axis=0)
        g_ref[...] = g
        delta, nm, nv = _adamw_math(w_ref[...], g, m_ref[...], v_ref[...])
        d_ref[...] = delta
        nm_ref[...] = nm
        nv_ref[...] = nv

    shape = jax.ShapeDtypeStruct((PACK_ROWS, D_MODEL), F32)
    return pl.pallas_call(body, name="small_adamw", out_shape=(shape,) * 4)(recv, w_p, m_p, v_p)


def _pack_small(norm_gain, q_a, k_a, q_b, k_b, sinks):
    def lane_pad(a):
        return jnp.pad(a, ((0, 0), (0, LANES - a.shape[1])))
    row1 = jnp.concatenate([lane_pad(q_a), lane_pad(k_a), lane_pad(q_b), lane_pad(k_b), lane_pad(sinks),
                            jnp.zeros((1, 3 * LANES), F32)], axis=1)
    return jnp.concatenate([norm_gain, row1, jnp.zeros((PACK_ROWS - 2, D_MODEL), F32)], axis=0)


def _unpack_small(p):
    return (p[0:1, :], p[1:2, 0:HEAD_DIM], p[1:2, LANES:LANES + HEAD_DIM], p[1:2, 2 * LANES:2 * LANES + HEAD_DIM],
            p[1:2, 3 * LANES:3 * LANES + HEAD_DIM], p[1:2, 4 * LANES:4 * LANES + HEADS])


def kernel(x, norm_gain, w_in, q_norm_a, k_norm_a, sinks_a, q_norm_b, k_norm_b, w_out, loss_target, m_norm_gain, m_w_in, m_q_norm_a, m_k_norm_a, m_sinks_a, m_q_norm_b, m_k_norm_b, m_w_out, v_norm_gain, v_w_in, v_q_norm_a, v_k_norm_a, v_sinks_a, v_q_norm_b, v_k_norm_b, v_w_out):
    chip = 2 * lax.axis_index("x") + lax.axis_index("y")

    w_in_all, w_out_all = _gather_weights([w_in[0].astype(BF16), w_out[0].astype(BF16)], "gather_weights")
    w_in_bf = w_in_all.transpose(1, 0, 2).reshape(D_MODEL, IN_WIDTH)
    w_out_bf = w_out_all.reshape(D_MODEL, D_MODEL)

    loss_part, gx, h_t, wins, gw_out, (dgain, dgqa, dgka, dsink, dgqb, dgkb) = _local_step(
        x[0], loss_target[0], norm_gain, w_in_bf, q_norm_a, k_norm_a, sinks_a, q_norm_b, k_norm_b, w_out_bf)

    small = jnp.concatenate([
        dgain, jnp.concatenate([dgqa, dgqb], axis=1),
        jnp.concatenate([dgkb, dgka, dsink, jnp.zeros((1, D_MODEL - WIDTH - 2 * A_KV_WIDTH), F32)], axis=1),
        jnp.zeros((PACK_ROWS - 3, D_MODEL), F32)], axis=0)
    order = (chip ^ jnp.array(RELATIONS, jnp.int32)).astype(jnp.int32)
    win_sum, wout_sum, small_recv = _grad_reduce(order, h_t, wins, gw_out, small)
    shift = jnp.array(WIN_SHIFT, jnp.int32)[chip]
    g_w_in = lax.dynamic_slice_in_dim(win_sum.reshape(D_MODEL, WIN), shift, IN_COLS, axis=1)
    g_w_out = wout_sum.reshape(OUT_ROWS, D_MODEL)

    d_w_in, nm_w_in, nv_w_in = _adamw(w_in[0], g_w_in, m_w_in[0], v_w_in[0], "adamw_w_in")
    d_w_out, nm_w_out, nv_w_out = _adamw(w_out[0], g_w_out, m_w_out[0], v_w_out[0], "adamw_w_out")
    g_s, d_s, nm_s, nv_s = _small_adamw(
        small_recv,
        _pack_small(norm_gain, q_norm_a, k_norm_a, q_norm_b, k_norm_b, sinks_a),
        _pack_small(m_norm_gain, m_q_norm_a, m_k_norm_a, m_q_norm_b, m_k_norm_b, m_sinks_a),
        _pack_small(v_norm_gain, v_q_norm_a, v_k_norm_a, v_q_norm_b, v_k_norm_b, v_sinks_a))

    loss = lax.psum(loss_part[0, 0], ("x", "y", "c"))

    def leaves(small_packed, big_in, big_out):
        gain, qa, ka, qb, kb, sk = _unpack_small(small_packed)
        return (gain, big_in[None], qa, ka, sk, qb, kb, big_out[None])

    return ((loss, gx[None]) + leaves(g_s, g_w_in, g_w_out) + leaves(d_s, d_w_in, d_w_out)
            + leaves(nm_s, nm_w_in, nm_w_out) + leaves(nv_s, nv_w_in, nv_w_out))
```

```python
import jax
import jax.numpy as jnp
from jax import lax
from jax.experimental import pallas as pl
from jax.experimental.pallas import tpu as pltpu

F32 = jnp.float32
BF16 = jnp.bfloat16

D_MODEL = 1024
HEAD_DIM = 64
HEADS = 8
WIDTH = HEADS * HEAD_DIM
A_KV_WIDTH = 2 * HEAD_DIM
BLOCK = 128
LANES = 128
FOLD = 16
A_MAX_DIST = 127
B_MAX_DIST = 128
ROPE_THETA = 10000.0
EPS = 1e-6
NEG = -1e30
SCALE = HEAD_DIM ** -0.5

IN_WIDTH = 3328
C_QA, C_KA, C_VA, C_GA, C_QB, C_KB, C_VB, C_GB, C_END = 0, 512, 640, 768, 1280, 1792, 2304, 2816, 3328

N_DEV = 8
N_CHIP = 4
MESH = pl.DeviceIdType.MESH
IN_COLS = IN_WIDTH // N_CHIP
WIN = 896
WIN_START = (0, 768, 1664, 2432)
WIN_SHIFT = (0, 64, 0, 64)
OUT_ROWS = D_MODEL // N_CHIP
RELATIONS = (3, 1, 2, 0)

ADAM_LR = 0.001
ADAM_B1 = 0.9
ADAM_B2 = 0.999
ADAM_EPS = 1e-08
ADAM_WD = 0.01
ADAM_STEP = 10

ROW_TILE = 256
FOLD_ROWS = ROW_TILE // FOLD
GRAD_ROWS = 512
VMEM_LIMIT = 56 * 1024 * 1024


def _dot(a, b):
    return jnp.dot(a, b, preferred_element_type=F32)


def _dot_nt(a, b):
    return lax.dot_general(a, b, (((1,), (1,)), ((), ())), preferred_element_type=F32)


def _dot_tn(a, b):
    return lax.dot_general(a, b, (((0,), (0,)), ((), ())), preferred_element_type=F32)


def _head_sum(z, bd):
    hi = z.astype(BF16)
    lo = (z - hi.astype(F32)).astype(BF16)
    return _dot(hi, bd) + _dot(lo, bd)


def _swap_halves(t):
    w = t.shape[1]
    lane = lax.broadcasted_iota(jnp.int32, t.shape, 1)
    return jnp.where(lane % HEAD_DIM < HEAD_DIM // 2, pltpu.roll(t, w - 32, 1), pltpu.roll(t, 32, 1))


def _qknorm_rope(t, g, cos, sin_s, bd):
    r = lax.rsqrt(_head_sum(t * t, bd) * (1.0 / HEAD_DIM) + EPS)
    n = (t * r) * g
    return n * cos + _swap_halves(n) * sin_s


def _qknorm_rope_bwd(dout, t, g, cos, sin_s, bd):
    dn = dout * cos + _swap_halves(dout * sin_s)
    r = lax.rsqrt(_head_sum(t * t, bd) * (1.0 / HEAD_DIM) + EPS)
    tr = t * r
    u = dn * g
    dt = r * (u - tr * (_head_sum(u * tr, bd) * (1.0 / HEAD_DIM)))
    return dt, dn * tr


def _sigmoid(g):
    return 1.0 / (1.0 + jnp.exp(-g))


def _expand_heads(st):
    t = st.shape[0]
    lane = lax.broadcasted_iota(jnp.int32, (t, LANES), 1)
    chunks = []
    for c in range(WIDTH // LANES):
        chunks.append(jnp.where(lane < HEAD_DIM, st[:, 2 * c:2 * c + 1], st[:, 2 * c + 1:2 * c + 2]))
    return jnp.concatenate(chunks, axis=1)


def _reduce_heads(z):
    t = z.shape[0]
    lane = lax.broadcasted_iota(jnp.int32, (t, LANES), 1)
    out = jnp.zeros((t, LANES), F32)
    for c in range(WIDTH // LANES):
        zc = z[:, c * LANES:(c + 1) * LANES]
        for ph in range(2):
            s = jnp.sum(jnp.where((lane // HEAD_DIM) == ph, zc, 0.0), axis=-1, keepdims=True)
            out = jnp.where(lane == 2 * c + ph, s, out)
    return out


def _fold_scratch(w):
    return pltpu.VMEM((w // LANES, ROW_TILE, LANES), F32)


def _store_folded(out_ref, val, scr):
    n = val.shape[1] // LANES
    for c in range(n):
        scr[c] = val[:, c * LANES:(c + 1) * LANES]
    for r in range(FOLD):
        piece = [scr[c, pl.ds(r, FOLD_ROWS, stride=FOLD), :] for c in range(n)]
        out_ref[r] = (piece[0] if n == 1 else jnp.concatenate(piece, axis=1)).astype(out_ref.dtype)


def _load_folded(in_ref, scr):
    n = in_ref.shape[2] // LANES
    for r in range(FOLD):
        blk = in_ref[r].astype(F32)
        for c in range(n):
            scr[c, pl.ds(r, FOLD_ROWS, stride=FOLD), :] = blk[:, c * LANES:(c + 1) * LANES]
    return scr[0] if n == 1 else jnp.concatenate([scr[c] for c in range(n)], axis=1)


def _rows(w, tm=ROW_TILE):
    return pl.BlockSpec((tm, w), lambda i: (i, 0))


def _folded_rows(w):
    return pl.BlockSpec((FOLD, FOLD_ROWS, w), lambda i: (0, i, 0))


def _whole(shape):
    return pl.BlockSpec(shape, lambda i: (0,) * len(shape))


def _inproj(x2, gain, w_bf, cos, sin_s, gqa, gka, gqb, gkb, bd512, bd128):
    s = x2.shape[0]
    tm = ROW_TILE

    def body(x_ref, gain_ref, w_hbm, cos_ref, sin_ref, gqa_ref, gka_ref, gqb_ref, gkb_ref, bd512_ref, bd128_ref,
             qa_ref, ka_ref, va_ref, qb_ref, kb_ref, vb_ref, qbf_ref, kbf_ref, vbf_ref,
             qa_raw_ref, ka_raw_ref, ga_ref, qb_raw_ref, kb_raw_ref, gb_ref, w_vmem, scr):
        @pl.when(pl.program_id(0) == 0)
        def _():
            pltpu.sync_copy(w_hbm, w_vmem)

        xt = x_ref[...]
        r = lax.rsqrt(jnp.mean(xt * xt, axis=-1, keepdims=True) + EPS)
        h = ((xt * r) * gain_ref[...]).astype(BF16)
        cos1 = cos_ref[...]
        sin1 = sin_ref[...]
        cos4 = jnp.tile(cos1, (1, 4))
        sin4 = jnp.tile(sin1, (1, 4))

        def seg(a, b):
            return _dot(h, w_vmem[:, a:b])

        t = seg(C_QA, C_KA)
        qa_raw_ref[...] = t
        qa_ref[...] = (_qknorm_rope(t, gqa_ref[...], cos4, sin4, bd512_ref[...]) * SCALE).astype(BF16)
        t = seg(C_KA, C_VA)
        ka_raw_ref[...] = t
        ka_ref[...] = _qknorm_rope(t, gka_ref[...], cos1, sin1, bd128_ref[...]).astype(BF16)
        va_ref[...] = seg(C_VA, C_GA).astype(BF16)
        ga_ref[...] = seg(C_GA, C_QB)
        t = seg(C_QB, C_KB)
        qb_raw_ref[...] = t
        t = _qknorm_rope(t, gqb_ref[...], cos4, sin4, bd512_ref[...]) * SCALE
        qb_ref[...] = t.astype(BF16)
        _store_folded(qbf_ref, t, scr)
        t = seg(C_KB, C_VB)
        kb_raw_ref[...] = t
        t = _qknorm_rope(t, gkb_ref[...], cos4, sin4, bd512_ref[...])
        kb_ref[...] = t.astype(BF16)
        _store_folded(kbf_ref, t, scr)
        t = seg(C_VB, C_GB)
        vb_ref[...] = t.astype(BF16)
        _store_folded(vbf_ref, t, scr)
        gb_ref[...] = seg(C_GB, C_END)

    sds = jax.ShapeDtypeStruct
    folded = sds((FOLD, s // FOLD, WIDTH), BF16)
    out_shape = (sds((s, WIDTH), BF16), sds((s, A_KV_WIDTH), BF16), sds((s, A_KV_WIDTH), BF16),
                 sds((s, WIDTH), BF16), sds((s, WIDTH), BF16), sds((s, WIDTH), BF16), folded, folded, folded,
                 sds((s, WIDTH), F32), sds((s, A_KV_WIDTH), F32), sds((s, WIDTH), F32),
                 sds((s, WIDTH), F32), sds((s, WIDTH), F32), sds((s, WIDTH), F32))
    out_specs = (_rows(WIDTH), _rows(A_KV_WIDTH), _rows(A_KV_WIDTH), _rows(WIDTH), _rows(WIDTH), _rows(WIDTH),
                 _folded_rows(WIDTH), _folded_rows(WIDTH), _folded_rows(WIDTH),
                 _rows(WIDTH), _rows(A_KV_WIDTH), _rows(WIDTH), _rows(WIDTH), _rows(WIDTH), _rows(WIDTH))
    return pl.pallas_call(
        body, name="inproj_fwd", grid=(s // tm,),
        in_specs=[_rows(D_MODEL), _whole(gain.shape), pl.BlockSpec(memory_space=pl.ANY), _rows(LANES), _rows(LANES),
                  _whole(gqa.shape), _whole(gka.shape), _whole(gqb.shape), _whole(gkb.shape), _whole(bd512.shape),
                  _whole(bd128.shape)],
        out_specs=out_specs, out_shape=out_shape,
        scratch_shapes=[pltpu.VMEM((D_MODEL, IN_WIDTH), BF16), _fold_scratch(WIDTH)],
        compiler_params=pltpu.CompilerParams(dimension_semantics=("arbitrary",), vmem_limit_bytes=VMEM_LIMIT),
    )(x2, gain, w_bf, cos, sin_s, gqa, gka, gqb, gkb, bd512, bd128)


def _seq_pos(idx, dil):
    if dil == 4:
        return 4 * (idx % 32) + idx // 32
    return idx


def _upper_mask(dil):
    qi = lax.broadcasted_iota(jnp.int32, (2 * BLOCK, BLOCK), 0) % BLOCK
    kj = lax.broadcasted_iota(jnp.int32, (2 * BLOCK, BLOCK), 1)
    return _seq_pos(kj, dil) > _seq_pos(qi, dil)


def _eye_mask():
    qi = lax.broadcasted_iota(jnp.int32, (2 * BLOCK, BLOCK), 0) % BLOCK
    kj = lax.broadcasted_iota(jnp.int32, (2 * BLOCK, BLOCK), 1)
    return qi == kj


def _stack_heads(a2, c, gqa):
    lane = lax.broadcasted_iota(jnp.int32, (1, LANES), 1) // HEAD_DIM
    zero = jnp.zeros_like(a2)
    if gqa:
        keep = lane == (c // 2)
        return jnp.concatenate([jnp.where(keep, a2, zero), jnp.where(keep, _swap_heads(a2), zero)], axis=0)
    return jnp.concatenate([jnp.where(lane == 0, a2, zero), jnp.where(lane == 1, a2, zero)], axis=0)


def _unstack_heads(a, c, gqa):
    lane = lax.broadcasted_iota(jnp.int32, (1, LANES), 1) // HEAD_DIM
    if gqa:
        return jnp.where(lane == (c // 2), a[:BLOCK], _swap_heads(a[BLOCK:]))
    return jnp.where(lane == 0, a[:BLOCK], a[BLOCK:])


def _stacked_head_ids(c, gqa):
    if gqa:
        return 2 * c + c // 2, 2 * c + 1 - c // 2
    return 2 * c, 2 * c + 1


def _per_head_rows(blk, heads):
    return jnp.concatenate([blk[:, heads[0]:heads[0] + 1], blk[:, heads[1]:heads[1] + 1]], axis=0)


def _attn_view(a, dil):
    if dil == 1:
        return a[None]
    if dil == 4:
        return a.reshape(4, 4, a.shape[1], a.shape[2])
    return a


def _attn_unview(a, dil):
    if dil == 1:
        return a[0]
    if dil == 4:
        return a.reshape(FOLD, a.shape[2], a.shape[3])
    return a


def _attn_specs(dil, nb):
    if dil == 4:
        def spec(fn):
            return lambda w: pl.BlockSpec((4, None, BLOCK // 4, w), lambda r, i: (0, r, fn(i), 0))
    else:
        def spec(fn):
            return lambda w: pl.BlockSpec((None, BLOCK, w), lambda r, i: (r, fn(i), 0))
    return spec


def _blk_load(ref, sl, dil):
    if dil == 4:
        return ref[:, :, sl].reshape(BLOCK, sl.stop - sl.start)
    return ref[:, sl]


def _blk_store(ref, sl, val, dil):
    if dil == 4:
        ref[:, :, sl] = val.reshape(4, BLOCK // 4, sl.stop - sl.start)
    else:
        ref[:, sl] = val


def _swap_heads(a):
    return pltpu.roll(a.astype(F32), HEAD_DIM, 1).astype(a.dtype)


def _attn_fwd(q, k, v, sinks, *, dil, max_dist, name):
    q, k, v = _attn_view(q, dil), _attn_view(k, dil), _attn_view(v, dil)
    kw = k.shape[-1]
    gqa = kw == A_KV_WIDTH
    n_seq = dil
    nb = (q.shape[-2] * (4 if dil == 4 else 1)) // BLOCK
    with_sinks = sinks is not None
    all_lanes = slice(0, LANES)
    assert max_dist in (BLOCK - 1, BLOCK)
    diag = max_dist == BLOCK

    def body(*refs):
        if with_sinks:
            q_ref, kp_ref, kc_ref, vp_ref, vc_ref, sink_ref, o_ref, m_ref, l_ref = refs
        else:
            q_ref, kp_ref, kc_ref, vp_ref, vc_ref, o_ref, m_ref, l_ref = refs

        def block(has_prev):
            upper, eye = _upper_mask(dil), _eye_mask()
            lane = lax.broadcasted_iota(jnp.int32, (1, LANES), 1)
            first_rows = lax.broadcasted_iota(jnp.int32, (2 * BLOCK, 1), 0) < BLOCK
            m_blk = jnp.zeros((BLOCK, LANES), F32)
            l_blk = jnp.ones((BLOCK, LANES), F32)
            for c in range(WIDTH // LANES):
                sl = slice(c * LANES, (c + 1) * LANES)
                ksl = slice(0, LANES) if gqa else sl
                kcur, vcur = _blk_load(kc_ref, ksl, dil), _blk_load(vc_ref, ksl, dil)
                heads = _stacked_head_ids(c, gqa)
                qs = _stack_heads(_blk_load(q_ref, sl, dil), c, gqa)
                if has_prev:
                    kp, vp = _blk_load(kp_ref, ksl, dil), _blk_load(vp_ref, ksl, dil)
                    s = _dot_nt(qs, jnp.concatenate([kp, kcur], axis=0))
                    s_p = s[:, :BLOCK]
                    sc = jnp.where(upper, s_p, s[:, BLOCK:])
                else:
                    sc = jnp.where(upper, NEG, _dot_nt(qs, kcur))
                with_diag = diag and has_prev
                if with_diag:
                    sd = jnp.where(eye, s_p, NEG)
                    m = jnp.max(jnp.maximum(sc, sd), axis=-1, keepdims=True)
                else:
                    m = jnp.max(sc, axis=-1, keepdims=True)
                if with_sinks:
                    sk = jnp.where(first_rows, sink_ref[0, heads[0]], sink_ref[0, heads[1]])
                    m = jnp.maximum(m, sk)
                p = jnp.exp(sc - m)
                zero = jnp.zeros_like(p)
                if with_diag:
                    pd = jnp.exp(sd - m)
                    l = jnp.sum(p + pd, axis=-1, keepdims=True)
                else:
                    pd = zero
                    l = jnp.sum(p, axis=-1, keepdims=True)
                if with_sinks:
                    l = l + jnp.exp(sk - m)
                p_cur = jnp.where(upper, zero, p)
                if has_prev:
                    pf = jnp.concatenate([jnp.where(upper, p, pd), p_cur], axis=1).astype(BF16)
                    o2 = _dot(pf, jnp.concatenate([vp, vcur], axis=0))
                else:
                    o2 = _dot(p_cur.astype(BF16), vcur)
                _blk_store(o_ref, sl, _unstack_heads(o2, c, gqa), dil)
                for n, h in enumerate(heads):
                    rows = slice(n * BLOCK, (n + 1) * BLOCK)
                    m_blk = jnp.where(lane == h, m[rows], m_blk)
                    l_blk = jnp.where(lane == h, l[rows], l_blk)
            _blk_store(m_ref, all_lanes, m_blk, dil)
            _blk_store(l_ref, all_lanes, l_blk, dil)

        @pl.when(pl.program_id(1) == 0)
        def _():
            block(False)

        @pl.when(pl.program_id(1) > 0)
        def _():
            block(True)

    spec = _attn_specs(dil, nb)
    cur = spec(lambda i: i)
    prev = spec(lambda i: jnp.maximum(i - 1, 0))
    in_specs = [cur(WIDTH), prev(kw), cur(kw), prev(kw), cur(kw)]
    args = [q, k, k, v, v]
    if with_sinks:
        in_specs.append(pl.BlockSpec(memory_space=pltpu.SMEM))
        args.append(sinks)
    stats = jax.ShapeDtypeStruct(q.shape[:-1] + (LANES,), F32)
    o, m, l = pl.pallas_call(
        body, name=name, grid=(n_seq, nb), in_specs=in_specs,
        out_specs=(cur(WIDTH), cur(LANES), cur(LANES)),
        out_shape=(jax.ShapeDtypeStruct(q.shape, F32), stats, stats),
        compiler_params=pltpu.CompilerParams(dimension_semantics=("arbitrary", "arbitrary")),
    )(*args)
    return _attn_unview(o, dil), _attn_unview(m, dil), _attn_unview(l, dil)


def _attn_bwd(q, k, v, do, lse, delta, *, dil, max_dist, name):
    q, k, v, do, lse, delta = (_attn_view(a, dil) for a in (q, k, v, do, lse, delta))
    kw = k.shape[-1]
    gqa = kw == A_KV_WIDTH
    n_seq = dil
    nb = (q.shape[-2] * (4 if dil == 4 else 1)) // BLOCK
    n_kc = kw // LANES
    all_lanes = slice(0, LANES)
    assert max_dist in (BLOCK - 1, BLOCK)
    diag = max_dist == BLOCK

    def body(q_ref, kp_ref, kc_ref, vp_ref, vc_ref, do_ref, lse_ref, dl_ref, dq_ref, dk_ref, dv_ref, ck_ref, cv_ref):
        i = pl.program_id(1)

        def block(has_prev):
            upper, eye = _upper_mask(dil), _eye_mask()
            lse_blk = _blk_load(lse_ref, all_lanes, dil)
            dl_blk = _blk_load(dl_ref, all_lanes, dil)
            dk_acc = [None] * n_kc
            dv_acc = [None] * n_kc
            for c in range(WIDTH // LANES):
                sl = slice(c * LANES, (c + 1) * LANES)
                kc = 0 if gqa else c
                ksl = slice(kc * LANES, (kc + 1) * LANES)
                kcur, vcur = _blk_load(kc_ref, ksl, dil), _blk_load(vc_ref, ksl, dil)
                heads = _stacked_head_ids(c, gqa)
                qs = _stack_heads(_blk_load(q_ref, sl, dil), c, gqa)
                dos = _stack_heads(_blk_load(do_ref, sl, dil), c, gqa)
                lse2 = _per_head_rows(lse_blk, heads)
                dl2 = _per_head_rows(dl_blk, heads)
                if has_prev:
                    k2 = jnp.concatenate([_blk_load(kp_ref, ksl, dil), kcur], axis=0)
                    v2 = jnp.concatenate([_blk_load(vp_ref, ksl, dil), vcur], axis=0)
                    s = _dot_nt(qs, k2)
                    dp = _dot_nt(dos, v2)
                    s_p, dp_p = s[:, :BLOCK], dp[:, :BLOCK]
                    sc = jnp.where(upper, s_p, s[:, BLOCK:])
                    dpc = jnp.where(upper, dp_p, dp[:, BLOCK:])
                else:
                    k2, v2 = kcur, vcur
                    sc = jnp.where(upper, NEG, _dot_nt(qs, kcur))
                    dpc = _dot_nt(dos, vcur)
                p = jnp.exp(sc - lse2)
                ds = p * (dpc - dl2)
                zero = jnp.zeros_like(p)
                pf = jnp.where(upper, zero, p)
                dsf = jnp.where(upper, zero, ds)
                if has_prev:
                    if diag:
                        pd = jnp.exp(jnp.where(eye, s_p, NEG) - lse2)
                        dsd = pd * (dp_p - dl2)
                    else:
                        pd = dsd = zero
                    pf = jnp.concatenate([jnp.where(upper, p, pd), pf], axis=1)
                    dsf = jnp.concatenate([jnp.where(upper, ds, dsd), dsf], axis=1)
                pf, dsf = pf.astype(BF16), dsf.astype(BF16)
                dq2 = _dot(dsf, k2)
                dk2 = _dot_tn(dsf, qs)
                dv2 = _dot_tn(pf, dos)
                _blk_store(dq_ref, sl, _unstack_heads(dq2, c, gqa) * SCALE, dil)
                dk_acc[kc] = dk2 if dk_acc[kc] is None else dk_acc[kc] + dk2
                dv_acc[kc] = dv2 if dv_acc[kc] is None else dv_acc[kc] + dv2
            for kc in range(n_kc):
                sl = slice(kc * LANES, (kc + 1) * LANES)
                if has_prev:
                    _blk_store(dk_ref, sl, ck_ref[:, sl] + dk_acc[kc][:BLOCK], dil)
                    _blk_store(dv_ref, sl, cv_ref[:, sl] + dv_acc[kc][:BLOCK], dil)
                    ck_ref[:, sl] = dk_acc[kc][BLOCK:]
                    cv_ref[:, sl] = dv_acc[kc][BLOCK:]
                else:
                    ck_ref[:, sl] = dk_acc[kc]
                    cv_ref[:, sl] = dv_acc[kc]

        @pl.when(i == 0)
        def _():
            block(False)

        @pl.when((i > 0) & (i < nb))
        def _():
            block(True)

        @pl.when(i == nb)
        def _():
            for kc in range(n_kc):
                sl = slice(kc * LANES, (kc + 1) * LANES)
                _blk_store(dk_ref, sl, ck_ref[:, sl], dil)
                _blk_store(dv_ref, sl, cv_ref[:, sl], dil)

    spec = _attn_specs(dil, nb)
    cur = spec(lambda i: jnp.minimum(i, nb - 1))
    prev = spec(lambda i: jnp.clip(i - 1, 0, nb - 1))
    lag = spec(lambda i: jnp.maximum(i - 1, 0))
    sds = jax.ShapeDtypeStruct
    dq, dk, dv = pl.pallas_call(
        body, name=name, grid=(n_seq, nb + 1),
        in_specs=[cur(WIDTH), prev(kw), cur(kw), prev(kw), cur(kw), cur(WIDTH), cur(LANES), cur(LANES)],
        out_specs=(cur(WIDTH), lag(kw), lag(kw)),
        out_shape=(sds(q.shape, F32), sds(k.shape, F32), sds(k.shape, F32)),
        scratch_shapes=[pltpu.VMEM((BLOCK, kw), F32), pltpu.VMEM((BLOCK, kw), F32)],
        compiler_params=pltpu.CompilerParams(dimension_semantics=("arbitrary", "arbitrary")),
    )(q, k, k, v, v, do, lse, delta)
    return _attn_unview(dq, dil), _attn_unview(dk, dil), _attn_unview(dv, dil)


def _outproj(att_a, att_b1, att_b4, att_b16, g_a, g_b, x2, tgt2, w_out_bf, sink_row):
    s = x2.shape[0]
    tm = ROW_TILE

    def body(oa_ref, ma_ref, la_ref, ob1_ref, m1_ref, l1_ref, ob4_ref, m4_ref, l4_ref, ob16_ref, m16_ref, l16_ref,
             ga_ref, gb_ref, x_ref, t_ref, w_ref, sink_ref,
             dy_ref, doa_ref, dob_ref, dobf_ref, dga_ref, dgb_ref, lsea_ref, lseb_ref, lsebf_ref, dla_ref, dlb_ref,
             dlbf_ref, gw_ref, loss_ref, dsink_ref, scr, scr_st):
        i = pl.program_id(0)

        @pl.when(i == 0)
        def _():
            gw_ref[...] = jnp.zeros_like(gw_ref)
            loss_ref[...] = jnp.zeros_like(loss_ref)
            dsink_ref[...] = jnp.zeros_like(dsink_ref)

        ms = [m1_ref[...], _load_folded(m4_ref, scr_st), _load_folded(m16_ref, scr_st)]
        ls = [l1_ref[...], _load_folded(l4_ref, scr_st), _load_folded(l16_ref, scr_st)]
        mx = jnp.maximum(jnp.maximum(ms[0], ms[1]), ms[2])
        scale = [jnp.exp(mp - mx) for mp in ms]
        den = (ls[0] * scale[0] + ls[1] * scale[1]) + ls[2] * scale[2]
        lane = lax.broadcasted_iota(jnp.int32, (tm, LANES), 1)
        lse_b = jnp.where(lane < HEADS, mx + jnp.log(den), 0.0)
        lseb_ref[...] = lse_b
        _store_folded(lsebf_ref, lse_b, scr_st)
        inv_den = 1.0 / den
        o_b = _expand_heads(scale[0] * inv_den) * ob1_ref[...]
        o_b = o_b + _expand_heads(scale[1] * inv_den) * _load_folded(ob4_ref, scr)
        o_b = o_b + _expand_heads(scale[2] * inv_den) * _load_folded(ob16_ref, scr)
        l_a = la_ref[...]
        lse_a = jnp.where(lane < HEADS, ma_ref[...] + jnp.log(l_a), 0.0)
        lsea_ref[...] = lse_a
        o_a = _expand_heads(1.0 / l_a) * oa_ref[...]
        g_a = ga_ref[...]
        g_b = gb_ref[...]
        sg_a = _sigmoid(g_a)
        sg_b = _sigmoid(g_b)
        silu_a = g_a * sg_a
        silu_b = g_b * sg_b
        mixed = jnp.concatenate([o_a * silu_a, o_b * silu_b], axis=1).astype(BF16)
        w = w_ref[...]
        y = x_ref[...] + _dot(mixed, w)
        diff = y - t_ref[...]
        loss_ref[...] += (0.5 / D_MODEL) * jnp.sum(diff * diff)
        dy = diff * (1.0 / D_MODEL)
        dy_ref[...] = dy
        dyb = dy.astype(BF16)
        gw_ref[...] += _dot_tn(mixed, dyb)
        dmixed = _dot_nt(dyb, w)
        dm_a = dmixed[:, :WIDTH]
        dm_b = dmixed[:, WIDTH:]
        do_a = dm_a * silu_a
        do_b = dm_b * silu_b
        doa_ref[...] = do_a.astype(BF16)
        dob_ref[...] = do_b.astype(BF16)
        _store_folded(dobf_ref, do_b, scr)
        dga_ref[...] = (dm_a * o_a * (sg_a * (1.0 + g_a * (1.0 - sg_a)))).astype(BF16)
        dgb_ref[...] = (dm_b * o_b * (sg_b * (1.0 + g_b * (1.0 - sg_b)))).astype(BF16)
        dl_a = _reduce_heads(do_a * o_a)
        dla_ref[...] = dl_a
        dl_b = _reduce_heads(do_b * o_b)
        dlb_ref[...] = dl_b
        _store_folded(dlbf_ref, dl_b, scr_st)
        dsink_ref[...] -= jnp.sum(jnp.exp(sink_ref[...] - lse_a) * dl_a, axis=0, keepdims=True)

    sds = jax.ShapeDtypeStruct
    ln = s // FOLD
    natural = [_rows(WIDTH), _rows(LANES), _rows(LANES)]
    folded = [_folded_rows(WIDTH), _folded_rows(LANES), _folded_rows(LANES)]
    return pl.pallas_call(
        body, name="outproj_fwd_bwd", grid=(s // tm,),
        in_specs=natural + natural + folded + folded
                 + [_rows(WIDTH), _rows(WIDTH), _rows(D_MODEL), _rows(D_MODEL), _whole((D_MODEL, D_MODEL)),
                    _whole((1, LANES))],
        out_specs=(_rows(D_MODEL), _rows(WIDTH), _rows(WIDTH), _folded_rows(WIDTH), _rows(WIDTH), _rows(WIDTH),
                   _rows(LANES), _rows(LANES), _folded_rows(LANES), _rows(LANES), _rows(LANES), _folded_rows(LANES),
                   _whole((D_MODEL, D_MODEL)), _whole((1, LANES)), _whole((1, LANES))),
        out_shape=(sds((s, D_MODEL), F32), sds((s, WIDTH), BF16), sds((s, WIDTH), BF16),
                   sds((FOLD, ln, WIDTH), BF16), sds((s, WIDTH), BF16), sds((s, WIDTH), BF16),
                   sds((s, LANES), F32), sds((s, LANES), F32), sds((FOLD, ln, LANES), F32), sds((s, LANES), F32),
                   sds((s, LANES), F32), sds((FOLD, ln, LANES), F32),
                   sds((D_MODEL, D_MODEL), F32), sds((1, LANES), F32), sds((1, LANES), F32)),
        scratch_shapes=[_fold_scratch(WIDTH), _fold_scratch(LANES)],
        compiler_params=pltpu.CompilerParams(dimension_semantics=("arbitrary",), vmem_limit_bytes=VMEM_LIMIT),
    )(*att_a, *att_b1, *att_b4, *att_b16, g_a, g_b, x2, tgt2, w_out_bf, sink_row)


def _inproj_bwd(x2, dy, gain, w_bf, cos, sin_s, gqa, gka, gqb, gkb, bd512, bd128,
                qa_raw, ka_raw, qb_raw, kb_raw, dq_a, dk_a, dv_a, dqkv_b1, dqkv_b4, dqkv_b16, dg_a, dg_b):
    s = x2.shape[0]
    tm = ROW_TILE

    def body(x_ref, dy_ref, gain_ref, w_hbm, cos_ref, sin_ref, gqa_ref, gka_ref, gqb_ref, gkb_ref, bd512_ref,
             bd128_ref, qa_raw_ref, ka_raw_ref, qb_raw_ref, kb_raw_ref, dqa_ref, dka_ref, dva_ref,
             dq1_ref, dk1_ref, dv1_ref, dq4_ref, dk4_ref, dv4_ref, dq16_ref, dk16_ref, dv16_ref, dga_ref, dgb_ref,
             gx_ref, ht_ref, win_ref,
             dgain_ref, dgqa_ref, dgka_ref, dgqb_ref, dgkb_ref, w_vmem, dproj_ref, scr):
        i = pl.program_id(0)

        @pl.when(i == 0)
        def _():
            pltpu.sync_copy(w_hbm, w_vmem)
            dgain_ref[...] = jnp.zeros_like(dgain_ref)
            dgqa_ref[...] = jnp.zeros_like(dgqa_ref)
            dgka_ref[...] = jnp.zeros_like(dgka_ref)
            dgqb_ref[...] = jnp.zeros_like(dgqb_ref)
            dgkb_ref[...] = jnp.zeros_like(dgkb_ref)

        cos1 = cos_ref[...]
        sin1 = sin_ref[...]
        cos4 = jnp.tile(cos1, (1, 4))
        sin4 = jnp.tile(sin1, (1, 4))

        dt, dg = _qknorm_rope_bwd(dqa_ref[...], qa_raw_ref[...], gqa_ref[...], cos4, sin4, bd512_ref[...])
        dproj_ref[:, C_QA:C_KA] = dt.astype(BF16)
        dgqa_ref[...] += jnp.sum(dg, axis=0, keepdims=True)
        dt, dg = _qknorm_rope_bwd(dka_ref[...], ka_raw_ref[...], gka_ref[...], cos1, sin1, bd128_ref[...])
        dproj_ref[:, C_KA:C_VA] = dt.astype(BF16)
        dgka_ref[...] += jnp.sum(dg, axis=0, keepdims=True)
        dproj_ref[:, C_VA:C_GA] = dva_ref[...].astype(BF16)
        dproj_ref[:, C_GA:C_QB] = dga_ref[...]
        dq = (dq1_ref[...] + _load_folded(dq4_ref, scr)) + _load_folded(dq16_ref, scr)
        dt, dg = _qknorm_rope_bwd(dq, qb_raw_ref[...], gqb_ref[...], cos4, sin4, bd512_ref[...])
        dproj_ref[:, C_QB:C_KB] = dt.astype(BF16)
        dgqb_ref[...] += jnp.sum(dg, axis=0, keepdims=True)
        dk = (dk1_ref[...] + _load_folded(dk4_ref, scr)) + _load_folded(dk16_ref, scr)
        dt, dg = _qknorm_rope_bwd(dk, kb_raw_ref[...], gkb_ref[...], cos4, sin4, bd512_ref[...])
        dproj_ref[:, C_KB:C_VB] = dt.astype(BF16)
        dgkb_ref[...] += jnp.sum(dg, axis=0, keepdims=True)
        dv = (dv1_ref[...] + _load_folded(dv4_ref, scr)) + _load_folded(dv16_ref, scr)
        dproj_ref[:, C_VB:C_GB] = dv.astype(BF16)
        dproj_ref[:, C_GB:C_END] = dgb_ref[...]
        for k, start in enumerate(WIN_START):
            win_ref[k] = dproj_ref[:, start:start + WIN]

        xt = x_ref[...]
        gain_row = gain_ref[...]
        r = lax.rsqrt(jnp.mean(xt * xt, axis=-1, keepdims=True) + EPS)
        xr = xt * r
        ht_ref[...] = (xr * gain_row).T.astype(BF16)
        dh = _dot_nt(dproj_ref[...], w_vmem[...])
        dgain_ref[...] += jnp.sum(dh * xr, axis=0, keepdims=True)
        u = dh * gain_row
        gx_ref[...] = dy_ref[...] + r * (u - xr * jnp.mean(u * xr, axis=-1, keepdims=True))

    def acc_row(w):
        return pl.BlockSpec((1, w), lambda i: (0, 0))

    sds = jax.ShapeDtypeStruct
    any_spec = pl.BlockSpec(memory_space=pl.ANY)
    win_spec = pl.BlockSpec((N_CHIP, tm, WIN), lambda i: (0, i, 0))
    return pl.pallas_call(
        body, name="inproj_bwd", grid=(s // tm,),
        in_specs=[_rows(D_MODEL), _rows(D_MODEL), _whole(gain.shape), any_spec, _rows(LANES), _rows(LANES),
                  _whole(gqa.shape), _whole(gka.shape), _whole(gqb.shape), _whole(gkb.shape), _whole(bd512.shape),
                  _whole(bd128.shape),
                  _rows(WIDTH), _rows(A_KV_WIDTH), _rows(WIDTH), _rows(WIDTH),
                  _rows(WIDTH), _rows(A_KV_WIDTH), _rows(A_KV_WIDTH)]
                 + [_rows(WIDTH)] * 3 + [_folded_rows(WIDTH)] * 6 + [_rows(WIDTH), _rows(WIDTH)],
        out_specs=(_rows(D_MODEL), pl.BlockSpec((D_MODEL, tm), lambda i: (0, i)), win_spec, acc_row(D_MODEL), acc_row(WIDTH), acc_row(A_KV_WIDTH), acc_row(WIDTH), acc_row(WIDTH)),
        out_shape=(sds((s, D_MODEL), F32), sds((D_MODEL, s), BF16), sds((N_CHIP, s, WIN), BF16),
                   sds((1, D_MODEL), F32),
                   sds((1, WIDTH), F32), sds((1, A_KV_WIDTH), F32), sds((1, WIDTH), F32), sds((1, WIDTH), F32)),
        scratch_shapes=[pltpu.VMEM((D_MODEL, IN_WIDTH), BF16), pltpu.VMEM((tm, IN_WIDTH), BF16),
                        _fold_scratch(WIDTH)],
        compiler_params=pltpu.CompilerParams(dimension_semantics=("arbitrary",), vmem_limit_bytes=VMEM_LIMIT),
    )(x2, dy, gain, w_bf, cos, sin_s, gqa, gka, gqb, gkb, bd512, bd128, qa_raw, ka_raw, qb_raw, kb_raw,
      dq_a, dk_a, dv_a, *dqkv_b1, *dqkv_b4, *dqkv_b16, dg_a, dg_b)


def _rope_tables(s):
    half = HEAD_DIM // 2
    inv = ROPE_THETA ** (-jnp.arange(half, dtype=F32) / half)
    ang = jnp.arange(s).astype(F32)[:, None] * inv[None, :]
    cos = jnp.cos(ang)
    sin = jnp.sin(ang)
    return jnp.tile(cos, (1, 4)), jnp.concatenate([-sin, sin, -sin, sin], axis=1)


def _block_diag_ones(w):
    idx = jnp.arange(w) // HEAD_DIM
    return (idx[:, None] == idx[None, :]).astype(BF16)


def _local_step(x2, tgt2, norm_gain, w_in_bf, q_norm_a, k_norm_a, sinks_a, q_norm_b, k_norm_b, w_out_bf):
    s = x2.shape[0]
    cos, sin_s = _rope_tables(s)
    bd512, bd128 = _block_diag_ones(WIDTH), _block_diag_ones(A_KV_WIDTH)
    gqa = jnp.tile(q_norm_a, (1, HEADS))
    gka = jnp.tile(k_norm_a, (1, 2))
    gqb = jnp.tile(q_norm_b, (1, HEADS))
    gkb = jnp.tile(k_norm_b, (1, HEADS))
    sink_row = jnp.pad(sinks_a, ((0, 0), (0, LANES - HEADS)))

    (qa, ka, va, qb, kb, vb, qbf, kbf, vbf, qa_raw, ka_raw, g_a, qb_raw, kb_raw, g_b) = _inproj(
        x2, norm_gain, w_in_bf, cos, sin_s, gqa, gka, gqb, gkb, bd512, bd128)

    att_a = _attn_fwd(qa, ka, va, sinks_a, dil=1, max_dist=A_MAX_DIST, name="attn_a_fwd")
    att_b1 = _attn_fwd(qb, kb, vb, None, dil=1, max_dist=B_MAX_DIST, name="attn_b1_fwd")
    att_b4 = _attn_fwd(qbf, kbf, vbf, None, dil=4, max_dist=B_MAX_DIST, name="attn_b4_fwd")
    att_b16 = _attn_fwd(qbf, kbf, vbf, None, dil=16, max_dist=B_MAX_DIST, name="attn_b16_fwd")

    (dy, do_a, do_b, do_bf, dg_a, dg_b, lse_a, lse_b, lse_bf, dl_a, dl_b, dl_bf, gw_out, loss_part,
     dsink) = _outproj(att_a, att_b1, att_b4, att_b16, g_a, g_b, x2, tgt2, w_out_bf, sink_row)

    dq_a, dk_a, dv_a = _attn_bwd(qa, ka, va, do_a, lse_a, dl_a, dil=1, max_dist=A_MAX_DIST, name="attn_a_bwd")
    d_b1 = _attn_bwd(qb, kb, vb, do_b, lse_b, dl_b, dil=1, max_dist=B_MAX_DIST, name="attn_b1_bwd")
    d_b4 = _attn_bwd(qbf, kbf, vbf, do_bf, lse_bf, dl_bf, dil=4, max_dist=B_MAX_DIST, name="attn_b4_bwd")
    d_b16 = _attn_bwd(qbf, kbf, vbf, do_bf, lse_bf, dl_bf, dil=16, max_dist=B_MAX_DIST, name="attn_b16_bwd")

    gx, h_t, wins, dgain, dgqa, dgka, dgqb, dgkb = _inproj_bwd(
        x2, dy, norm_gain, w_in_bf, cos, sin_s, gqa, gka, gqb, gkb, bd512, bd128,
        qa_raw, ka_raw, qb_raw, kb_raw, dq_a, dk_a, dv_a, d_b1, d_b4, d_b16, dg_a, dg_b)
    return loss_part, gx, h_t, wins, gw_out, (dgain, dgqa, dgka, dsink, dgqb, dgkb)


def _position():
    return lax.axis_index("x"), lax.axis_index("y"), lax.axis_index("c")


GATHER_CHUNKS = 2


def _gather_weights(blocks, name):
    n = len(blocks)
    ch = GATHER_CHUNKS
    n_sems = n * (N_CHIP - 1) * ch

    def body(*refs):
        src_refs, dst_refs = refs[:n], refs[n:2 * n]
        ici_send, ici_recv, d2d_send, d2d_recv, local_sems = refs[2 * n:]
        x, y, c = _position()
        b = 2 * x + y
        copies = []
        for k in range(n):
            local = pltpu.make_async_copy(src_refs[k], dst_refs[k].at[b], local_sems.at[k])
            local.start()
            copies.append(local)

        def rows(k, core, j):
            half = blocks[k].shape[0] // 2
            return pl.ds(core * half + j * (half // ch), half // ch)

        plan = []
        for d in range(1, N_CHIP):
            px, py = x ^ (d >> 1), y ^ (d & 1)
            for j in range(ch):
                for k in range(n):
                    plan.append((px, py, 2 * px + py, k, j, ((d - 1) * ch + j) * n + k))
        sends = []
        for px, py, pb, k, j, sem in plan:
            send = pltpu.make_async_remote_copy(
                src_ref=src_refs[k].at[rows(k, c, j)], dst_ref=dst_refs[k].at[b, rows(k, c, j)],
                send_sem=ici_send.at[sem], recv_sem=ici_recv.at[sem], device_id=(px, py, c), device_id_type=MESH)
            send.start()
            sends.append(send)
        for px, py, pb, k, j, sem in plan:
            landed = dst_refs[k].at[pb, rows(k, c, j)]
            pltpu.make_async_remote_copy(
                src_ref=landed, dst_ref=landed, send_sem=ici_send.at[sem], recv_sem=ici_recv.at[sem],
                device_id=(px, py, c), device_id_type=MESH).wait_recv()
            forward = pltpu.make_async_remote_copy(
                src_ref=landed, dst_ref=landed, send_sem=d2d_send.at[sem], recv_sem=d2d_recv.at[sem],
                device_id=(x, y, 1 - c), device_id_type=MESH)
            forward.start()
            sends.append(forward)
        for px, py, pb, k, j, sem in plan:
            passed = dst_refs[k].at[pb, rows(k, 1 - c, j)]
            pltpu.make_async_remote_copy(
                src_ref=passed, dst_ref=passed, send_sem=d2d_send.at[sem], recv_sem=d2d_recv.at[sem],
                device_id=(x, y, 1 - c), device_id_type=MESH).wait_recv()
        for send in sends:
            send.wait_send()
        for local in copies:
            local.wait()

    vmem_spec = pl.BlockSpec(memory_space=pltpu.VMEM)
    out_shape = tuple(jax.ShapeDtypeStruct((N_CHIP,) + a.shape, a.dtype) for a in blocks)
    return pl.pallas_call(
        body, name=name, in_specs=[vmem_spec] * n, out_specs=tuple([vmem_spec] * n), out_shape=out_shape,
        scratch_shapes=[pltpu.SemaphoreType.DMA((n_sems,)) for _ in range(4)] + [pltpu.SemaphoreType.DMA((n,))],
        compiler_params=pltpu.CompilerParams(vmem_limit_bytes=VMEM_LIMIT),
    )(*blocks)


def _grad_reduce(order, h_t, wins, gw_out, small):
    s = h_t.shape[1]
    tk = GRAD_ROWS
    n_i = s // tk
    half = D_MODEL // 2
    o_half = OUT_ROWS // 2
    n_rel = N_CHIP - 1

    def body(order_ref, ht_ref, win_ref, gwo_ref, small_ref,
             win_out, wout_out, small_out,
             acc, mine, s1, r1, s2, r2, so1, ro1, so2, ro2, pair_in, pair_o, small_land,
             s1_send, s1_recv, s2_send, s2_recv, o1_send, o1_recv, o2_send, o2_recv,
             pair_send, pair_recv, small_send, small_recv):
        j = pl.program_id(0)
        i = pl.program_id(1)
        x, y, c = _position()
        me = 4 * x + 2 * y + c
        sibling = (x, y, 1 - c)
        my_rows = pl.ds(pl.multiple_of(c * half, half), half)
        sib_rows = pl.ds(pl.multiple_of((1 - c) * half, half), half)

        def chip_of(rel):
            return x ^ (rel >> 1), y ^ (rel & 1)

        def level1(k):
            return pltpu.make_async_remote_copy(src_ref=s1.at[k], dst_ref=r1.at[k], send_sem=s1_send.at[k],
                                                recv_sem=s1_recv.at[k], device_id=sibling, device_id_type=MESH)

        def level2(k):
            px, py = chip_of(RELATIONS[k])
            return pltpu.make_async_remote_copy(src_ref=s2.at[k], dst_ref=r2.at[k], send_sem=s2_send.at[k],
                                                recv_sem=s2_recv.at[k], device_id=(px, py, c), device_id_type=MESH)

        def out_level1(bk):
            return pltpu.make_async_remote_copy(src_ref=so1.at[bk], dst_ref=ro1.at[bk], send_sem=o1_send.at[bk],
                                                recv_sem=o1_recv.at[bk], device_id=sibling, device_id_type=MESH)

        def out_level2(k):
            px, py = chip_of(RELATIONS[k])
            return pltpu.make_async_remote_copy(src_ref=so2.at[k], dst_ref=ro2.at[k], send_sem=o2_send.at[k],
                                                recv_sem=o2_recv.at[k], device_id=(px, py, c), device_id_type=MESH)

        def small_copy(d):
            px, py, pc = x ^ (d >> 2), y ^ ((d >> 1) & 1), c ^ (d & 1)
            return pltpu.make_async_remote_copy(src_ref=small_ref, dst_ref=small_land.at[me],
                                                send_sem=small_send.at[d], recv_sem=small_recv.at[d],
                                                device_id=(px, py, pc), device_id_type=MESH)

        def pair_copy(k, buf):
            return pltpu.make_async_remote_copy(src_ref=buf.at[0], dst_ref=buf.at[1], send_sem=pair_send.at[k],
                                                recv_sem=pair_recv.at[k], device_id=sibling, device_id_type=MESH)

        def out_rows(bk, core):
            return pl.ds(pl.multiple_of(bk * OUT_ROWS + core * o_half, o_half), o_half)

        @pl.when((j == 0) & (i == 0))
        def _():
            for d in range(1, N_DEV):
                small_copy(d).start()
            small_land[me] = small_ref[...]
            for bk in range(N_CHIP):
                so1[bk] = gwo_ref[out_rows(bk, 1 - c), :].astype(BF16)
                out_level1(bk).start()

        @pl.when((j == 0) & (i == 1))
        def _():
            b = 2 * x + y
            for bk in range(N_CHIP):
                out_level1(bk).wait_recv()
            for k in range(n_rel):
                px, py = chip_of(RELATIONS[k])
                bk = 2 * px + py
                so2[k] = (gwo_ref[out_rows(bk, c), :] + ro1[bk].astype(F32)).astype(BF16)
                out_level2(k).start()

        contrib = _dot(ht_ref[...], win_ref[...])

        @pl.when(i == 0)
        def _():
            acc[...] = contrib

        @pl.when(i > 0)
        def _():
            acc[...] += contrib

        for k in range(N_CHIP):
            @pl.when((j == k) & (i == n_i - 1))
            def _(k=k):
                s1[k] = acc[sib_rows, :].astype(BF16)
                level1(k).start()
                mine[...] = acc[my_rows, :]

            if k < n_rel:
                @pl.when((j == k + 1) & (i == 1))
                def _(k=k):
                    level1(k).wait_recv()
                    s2[k] = (mine[...] + r1[k].astype(F32)).astype(BF16)
                    level2(k).start()

        @pl.when((j == N_CHIP - 1) & (i == n_i - 1))
        def _():
            b = 2 * x + y
            level1(N_CHIP - 1).wait_recv()
            total = mine[...] + r1[N_CHIP - 1].astype(F32)
            for k in range(n_rel):
                level2(k).wait_recv()
                total = total + r2[k].astype(F32)
            pair_in[0] = total
            pair_copy(0, pair_in).start()
            total_o = gwo_ref[out_rows(b, c), :] + ro1[b].astype(F32)
            for k in range(n_rel):
                out_level2(k).wait_recv()
                total_o = total_o + ro2[k].astype(F32)
            pair_o[0] = total_o
            pair_copy(1, pair_o).start()
            win_out[c] = total
            wout_out[c] = total_o
            for d in range(1, N_DEV):
                small_copy(d).wait_recv()
            small_out[...] = small_land[...]
            pair_copy(0, pair_in).wait_recv()
            win_out[1 - c] = pair_in[1]
            pair_copy(1, pair_o).wait_recv()
            wout_out[1 - c] = pair_o[1]
            for d in range(1, N_DEV):
                small_copy(d).wait_send()
            for k in range(N_CHIP):
                level1(k).wait_send()
                out_level1(k).wait_send()
            for k in range(n_rel):
                level2(k).wait_send()
                out_level2(k).wait_send()
            pair_copy(0, pair_in).wait_send()
            pair_copy(1, pair_o).wait_send()

    vmem = pl.BlockSpec(memory_space=pltpu.VMEM)
    dma = pltpu.SemaphoreType.DMA
    sds = jax.ShapeDtypeStruct
    grid_spec = pltpu.PrefetchScalarGridSpec(
        num_scalar_prefetch=1, grid=(N_CHIP, n_i),
        in_specs=[pl.BlockSpec((D_MODEL, tk), lambda j, i, order: (0, i)),
                  pl.BlockSpec((None, tk, WIN), lambda j, i, order: (order[j], i, 0)), vmem, vmem],
        out_specs=(vmem, vmem, vmem),
        scratch_shapes=[
            pltpu.VMEM((D_MODEL, WIN), F32), pltpu.VMEM((half, WIN), F32),
            pltpu.VMEM((N_CHIP, half, WIN), BF16), pltpu.VMEM((N_CHIP, half, WIN), BF16),
            pltpu.VMEM((n_rel, half, WIN), BF16), pltpu.VMEM((n_rel, half, WIN), BF16),
            pltpu.VMEM((N_CHIP, o_half, D_MODEL), BF16), pltpu.VMEM((N_CHIP, o_half, D_MODEL), BF16),
            pltpu.VMEM((n_rel, o_half, D_MODEL), BF16), pltpu.VMEM((n_rel, o_half, D_MODEL), BF16),
            pltpu.VMEM((2, half, WIN), F32), pltpu.VMEM((2, o_half, D_MODEL), F32),
            pltpu.VMEM((N_DEV, PACK_ROWS, D_MODEL), F32),
            dma((N_CHIP,)), dma((N_CHIP,)), dma((n_rel,)), dma((n_rel,)),
            dma((N_CHIP,)), dma((N_CHIP,)), dma((n_rel,)), dma((n_rel,)),
            dma((2,)), dma((2,)), dma((N_DEV,)), dma((N_DEV,))])
    return pl.pallas_call(
        body, name="grad_w_in_reduce", grid_spec=grid_spec,
        out_shape=(sds((2, half, WIN), F32), sds((2, o_half, D_MODEL), F32), sds((N_DEV, PACK_ROWS, D_MODEL), F32)),
        compiler_params=pltpu.CompilerParams(dimension_semantics=("arbitrary", "arbitrary"),
                                             vmem_limit_bytes=VMEM_LIMIT),
    )(order, h_t, wins, gw_out, small)


ADAM_ROWS = 128


def _adamw_math(w, g, m, v):
    m = ADAM_B1 * m + (1.0 - ADAM_B1) * g
    v = ADAM_B2 * v + (1.0 - ADAM_B2) * (g * g)
    m_hat = m / (1.0 - ADAM_B1 ** ADAM_STEP)
    v_hat = v / (1.0 - ADAM_B2 ** ADAM_STEP)
    delta = -ADAM_LR * (m_hat / (jnp.sqrt(v_hat) + ADAM_EPS) + ADAM_WD * w)
    return delta, m, v


def _adamw(w, g, m, v, name):
    r, c = w.shape

    def body(w_ref, g_ref, m_ref, v_ref, d_ref, nm_ref, nv_ref):
        delta, nm, nv = _adamw_math(w_ref[...], g_ref[...], m_ref[...], v_ref[...])
        d_ref[...] = delta
        nm_ref[...] = nm
        nv_ref[...] = nv

    spec = pl.BlockSpec((ADAM_ROWS, c), lambda i: (i, 0))
    shape = jax.ShapeDtypeStruct((r, c), F32)
    return pl.pallas_call(
        body, name=name, grid=(r // ADAM_ROWS,), in_specs=[spec] * 4, out_specs=(spec,) * 3,
        out_shape=(shape,) * 3,
    )(w, g, m, v)


PACK_ROWS = 8


def _fold_heads(v):
    y = v[:, 0:LANES]
    for j in range(1, v.shape[1] // LANES):
        y = y + v[:, j * LANES:(j + 1) * LANES]
    return y + pltpu.roll(y, HEAD_DIM, 1)


def _small_adamw(recv, w_p, m_p, v_p):
    def body(r_ref, w_ref, m_ref, v_ref, g_ref, d_ref, nm_ref, nv_ref):
        tot = r_ref[0]
        for j in range(1, N_DEV):
            tot = tot + r_ref[j]
        row1 = tot[1:2, :]
        row2 = tot[2:3, :]
        pieces = [_fold_heads(row1[:, 0:WIDTH]), _fold_heads(row2[:, WIDTH:WIDTH + A_KV_WIDTH]),
                  _fold_heads(row1[:, WIDTH:2 * WIDTH]), _fold_heads(row2[:, 0:WIDTH]),
                  row2[:, WIDTH + A_KV_WIDTH:WIDTH + 2 * A_KV_WIDTH], jnp.zeros((1, 3 * LANES), F32)]
        g = jnp.concatenate([tot[0:1, :], jnp.concatenate(pieces, axis=1), jnp.zeros((PACK_ROWS - 2, D_MODEL), F32)],
                            axis=0)
        g_ref[...] = g
        delta, nm, nv = _adamw_math(w_ref[...], g, m_ref[...], v_ref[...])
        d_ref[...] = delta
        nm_ref[...] = nm
        nv_ref[...] = nv

    shape = jax.ShapeDtypeStruct((PACK_ROWS, D_MODEL), F32)
    return pl.pallas_call(body, name="small_adamw", out_shape=(shape,) * 4)(recv, w_p, m_p, v_p)


def _pack_small(norm_gain, q_a, k_a, q_b, k_b, sinks):
    def lane_pad(a):
        return jnp.pad(a, ((0, 0), (0, LANES - a.shape[1])))
    row1 = jnp.concatenate([lane_pad(q_a), lane_pad(k_a), lane_pad(q_b), lane_pad(k_b), lane_pad(sinks),
                            jnp.zeros((1, 3 * LANES), F32)], axis=1)
    return jnp.concatenate([norm_gain, row1, jnp.zeros((PACK_ROWS - 2, D_MODEL), F32)], axis=0)


def _unpack_small(p):
    return (p[0:1, :], p[1:2, 0:HEAD_DIM], p[1:2, LANES:LANES + HEAD_DIM], p[1:2, 2 * LANES:2 * LANES + HEAD_DIM],
            p[1:2, 3 * LANES:3 * LANES + HEAD_DIM], p[1:2, 4 * LANES:4 * LANES + HEADS])


def kernel(x, norm_gain, w_in, q_norm_a, k_norm_a, sinks_a, q_norm_b, k_norm_b, w_out, loss_target, m_norm_gain, m_w_in, m_q_norm_a, m_k_norm_a, m_sinks_a, m_q_norm_b, m_k_norm_b, m_w_out, v_norm_gain, v_w_in, v_q_norm_a, v_k_norm_a, v_sinks_a, v_q_norm_b, v_k_norm_b, v_w_out):
    chip = 2 * lax.axis_index("x") + lax.axis_index("y")

    w_in_all, w_out_all = _gather_weights([w_in[0].astype(BF16), w_out[0].astype(BF16)], "gather_weights")
    w_in_bf = w_in_all.transpose(1, 0, 2).reshape(D_MODEL, IN_WIDTH)
    w_out_bf = w_out_all.reshape(D_MODEL, D_MODEL)

    loss_part, gx, h_t, wins, gw_out, (dgain, dgqa, dgka, dsink, dgqb, dgkb) = _local_step(
        x[0], loss_target[0], norm_gain, w_in_bf, q_norm_a, k_norm_a, sinks_a, q_norm_b, k_norm_b, w_out_bf)

    small = jnp.concatenate([
        dgain, jnp.concatenate([dgqa, dgqb], axis=1),
        jnp.concatenate([dgkb, dgka, dsink, jnp.zeros((1, D_MODEL - WIDTH - 2 * A_KV_WIDTH), F32)], axis=1),
        jnp.zeros((PACK_ROWS - 3, D_MODEL), F32)], axis=0)
    order = (chip ^ jnp.array(RELATIONS, jnp.int32)).astype(jnp.int32)
    win_sum, wout_sum, small_recv = _grad_reduce(order, h_t, wins, gw_out, small)
    shift = jnp.array(WIN_SHIFT, jnp.int32)[chip]
    g_w_in = lax.dynamic_slice_in_dim(win_sum.reshape(D_MODEL, WIN), shift, IN_COLS, axis=1)
    g_w_out = wout_sum.reshape(OUT_ROWS, D_MODEL)

    d_w_in, nm_w_in, nv_w_in = _adamw(w_in[0], g_w_in, m_w_in[0], v_w_in[0], "adamw_w_in")
    d_w_out, nm_w_out, nv_w_out = _adamw(w_out[0], g_w_out, m_w_out[0], v_w_out[0], "adamw_w_out")
    g_s, d_s, nm_s, nv_s = _small_adamw(
        small_recv,
        _pack_small(norm_gain, q_norm_a, k_norm_a, q_norm_b, k_norm_b, sinks_a),
        _pack_small(m_norm_gain, m_q_norm_a, m_k_norm_a, m_q_norm_b, m_k_norm_b, m_sinks_a),
        _pack_small(v_norm_gain, v_q_norm_a, v_k_norm_a, v_q_norm_b, v_k_norm_b, v_sinks_a))

    loss = lax.psum(loss_part[0, 0], ("x", "y", "c"))

    def leaves(small_packed, big_in, big_out):
        gain, qa, ka, qb, kb, sk = _unpack_small(small_packed)
        return (gain, big_in[None], qa, ka, sk, qb, kb, big_out[None])

    return ((loss, gx[None]) + leaves(g_s, g_w_in, g_w_out) + leaves(d_s, d_w_in, d_w_out)
            + leaves(nm_s, nm_w_in, nm_w_out) + leaves(nv_s, nv_w_in, nv_w_out))
```

```python
import jax
import jax.numpy as jnp
from jax import lax
from jax.experimental import pallas as pl
from jax.experimental.pallas import tpu as pltpu

F32 = jnp.float32
BF16 = jnp.bfloat16

D_MODEL = 1024
HEAD_DIM = 64
HEADS = 8
WIDTH = HEADS * HEAD_DIM
A_KV_WIDTH = 2 * HEAD_DIM
BLOCK = 128
LANES = 128
FOLD = 16
A_MAX_DIST = 127
B_MAX_DIST = 128
ROPE_THETA = 10000.0
EPS = 1e-6
NEG = -1e30
SCALE = HEAD_DIM ** -0.5

IN_WIDTH = 3328
C_QA, C_KA, C_VA, C_GA, C_QB, C_KB, C_VB, C_GB, C_END = 0, 512, 640, 768, 1280, 1792, 2304, 2816, 3328

N_DEV = 8
N_CHIP = 4
MESH = pl.DeviceIdType.MESH
IN_COLS = IN_WIDTH // N_CHIP
WIN = 896
WIN_START = (0, 768, 1664, 2432)
WIN_SHIFT = (0, 64, 0, 64)
OUT_ROWS = D_MODEL // N_CHIP
RELATIONS = (3, 1, 2, 0)

ADAM_LR = 0.001
ADAM_B1 = 0.9
ADAM_B2 = 0.999
ADAM_EPS = 1e-08
ADAM_WD = 0.01
ADAM_STEP = 10

ROW_TILE = 256
FOLD_ROWS = ROW_TILE // FOLD
GRAD_ROWS = 512
ACC_COLS = 256
VMEM_LIMIT = 56 * 1024 * 1024


def _dot(a, b):
    return jnp.dot(a, b, preferred_element_type=F32)


def _dot_nt(a, b):
    return lax.dot_general(a, b, (((1,), (1,)), ((), ())), preferred_element_type=F32)


def _dot_tn(a, b):
    return lax.dot_general(a, b, (((0,), (0,)), ((), ())), preferred_element_type=F32)


def _head_sum(z, bd):
    hi = z.astype(BF16)
    lo = (z - hi.astype(F32)).astype(BF16)
    return _dot(hi, bd) + _dot(lo, bd)


def _swap_halves(t):
    w = t.shape[1]
    lane = lax.broadcasted_iota(jnp.int32, t.shape, 1)
    return jnp.where(lane % HEAD_DIM < HEAD_DIM // 2, pltpu.roll(t, w - 32, 1), pltpu.roll(t, 32, 1))


def _qknorm_rope(t, g, cos, sin_s, bd):
    r = lax.rsqrt(_head_sum(t * t, bd) * (1.0 / HEAD_DIM) + EPS)
    n = (t * r) * g
    return n * cos + _swap_halves(n) * sin_s


def _qknorm_rope_bwd(dout, t, g, cos, sin_s, bd):
    dn = dout * cos + _swap_halves(dout * sin_s)
    r = lax.rsqrt(_head_sum(t * t, bd) * (1.0 / HEAD_DIM) + EPS)
    tr = t * r
    u = dn * g
    dt = r * (u - tr * (_head_sum(u * tr, bd) * (1.0 / HEAD_DIM)))
    return dt, dn * tr


def _sigmoid(g):
    return 1.0 / (1.0 + jnp.exp(-g))


def _expand_heads(st):
    t = st.shape[0]
    lane = lax.broadcasted_iota(jnp.int32, (t, LANES), 1)
    chunks = []
    for c in range(WIDTH // LANES):
        chunks.append(jnp.where(lane < HEAD_DIM, st[:, 2 * c:2 * c + 1], st[:, 2 * c + 1:2 * c + 2]))
    return jnp.concatenate(chunks, axis=1)


def _reduce_heads(z):
    t = z.shape[0]
    lane = lax.broadcasted_iota(jnp.int32, (t, LANES), 1)
    out = jnp.zeros((t, LANES), F32)
    for c in range(WIDTH // LANES):
        zc = z[:, c * LANES:(c + 1) * LANES]
        for ph in range(2):
            s = jnp.sum(jnp.where((lane // HEAD_DIM) == ph, zc, 0.0), axis=-1, keepdims=True)
            out = jnp.where(lane == 2 * c + ph, s, out)
    return out


def _fold_scratch(w):
    return pltpu.VMEM((w // LANES, ROW_TILE, LANES), F32)


def _store_folded(out_ref, val, scr):
    n = val.shape[1] // LANES
    for c in range(n):
        scr[c] = val[:, c * LANES:(c + 1) * LANES]
    for r in range(FOLD):
        piece = [scr[c, pl.ds(r, FOLD_ROWS, stride=FOLD), :] for c in range(n)]
        out_ref[r] = (piece[0] if n == 1 else jnp.concatenate(piece, axis=1)).astype(out_ref.dtype)


def _load_folded(in_ref, scr):
    n = in_ref.shape[2] // LANES
    for r in range(FOLD):
        blk = in_ref[r].astype(F32)
        for c in range(n):
            scr[c, pl.ds(r, FOLD_ROWS, stride=FOLD), :] = blk[:, c * LANES:(c + 1) * LANES]
    return scr[0] if n == 1 else jnp.concatenate([scr[c] for c in range(n)], axis=1)


def _rows(w, tm=ROW_TILE):
    return pl.BlockSpec((tm, w), lambda i: (i, 0))


def _folded_rows(w):
    return pl.BlockSpec((FOLD, FOLD_ROWS, w), lambda i: (0, i, 0))


def _whole(shape):
    return pl.BlockSpec(shape, lambda i: (0,) * len(shape))


def _inproj(x2, gain, w_bf, cos, sin_s, gqa, gka, gqb, gkb, bd512, bd128):
    s = x2.shape[0]
    tm = ROW_TILE

    def body(x_ref, gain_ref, w_hbm, cos_ref, sin_ref, gqa_ref, gka_ref, gqb_ref, gkb_ref, bd512_ref, bd128_ref,
             qa_ref, ka_ref, va_ref, qb_ref, kb_ref, vb_ref, qbf_ref, kbf_ref, vbf_ref,
             qa_raw_ref, ka_raw_ref, ga_ref, qb_raw_ref, kb_raw_ref, gb_ref, w_vmem, scr):
        @pl.when(pl.program_id(0) == 0)
        def _():
            pltpu.sync_copy(w_hbm, w_vmem)

        xt = x_ref[...]
        r = lax.rsqrt(jnp.mean(xt * xt, axis=-1, keepdims=True) + EPS)
        h = ((xt * r) * gain_ref[...]).astype(BF16)
        cos1 = cos_ref[...]
        sin1 = sin_ref[...]
        cos4 = jnp.tile(cos1, (1, 4))
        sin4 = jnp.tile(sin1, (1, 4))

        def seg(a, b):
            return _dot(h, w_vmem[:, a:b])

        t = seg(C_QA, C_KA)
        qa_raw_ref[...] = t
        qa_ref[...] = (_qknorm_rope(t, gqa_ref[...], cos4, sin4, bd512_ref[...]) * SCALE).astype(BF16)
        t = seg(C_KA, C_VA)
        ka_raw_ref[...] = t
        ka_ref[...] = _qknorm_rope(t, gka_ref[...], cos1, sin1, bd128_ref[...]).astype(BF16)
        va_ref[...] = seg(C_VA, C_GA).astype(BF16)
        ga_ref[...] = seg(C_GA, C_QB)
        t = seg(C_QB, C_KB)
        qb_raw_ref[...] = t
        t = _qknorm_rope(t, gqb_ref[...], cos4, sin4, bd512_ref[...]) * SCALE
        qb_ref[...] = t.astype(BF16)
        _store_folded(qbf_ref, t, scr)
        t = seg(C_KB, C_VB)
        kb_raw_ref[...] = t
        t = _qknorm_rope(t, gkb_ref[...], cos4, sin4, bd512_ref[...])
        kb_ref[...] = t.astype(BF16)
        _store_folded(kbf_ref, t, scr)
        t = seg(C_VB, C_GB)
        vb_ref[...] = t.astype(BF16)
        _store_folded(vbf_ref, t, scr)
        gb_ref[...] = seg(C_GB, C_END)

    sds = jax.ShapeDtypeStruct
    folded = sds((FOLD, s // FOLD, WIDTH), BF16)
    out_shape = (sds((s, WIDTH), BF16), sds((s, A_KV_WIDTH), BF16), sds((s, A_KV_WIDTH), BF16),
                 sds((s, WIDTH), BF16), sds((s, WIDTH), BF16), sds((s, WIDTH), BF16), folded, folded, folded,
                 sds((s, WIDTH), F32), sds((s, A_KV_WIDTH), F32), sds((s, WIDTH), F32),
                 sds((s, WIDTH), F32), sds((s, WIDTH), F32), sds((s, WIDTH), F32))
    out_specs = (_rows(WIDTH), _rows(A_KV_WIDTH), _rows(A_KV_WIDTH), _rows(WIDTH), _rows(WIDTH), _rows(WIDTH),
                 _folded_rows(WIDTH), _folded_rows(WIDTH), _folded_rows(WIDTH),
                 _rows(WIDTH), _rows(A_KV_WIDTH), _rows(WIDTH), _rows(WIDTH), _rows(WIDTH), _rows(WIDTH))
    return pl.pallas_call(
        body, name="inproj_fwd", grid=(s // tm,),
        in_specs=[_rows(D_MODEL), _whole(gain.shape), pl.BlockSpec(memory_space=pl.ANY), _rows(LANES), _rows(LANES),
                  _whole(gqa.shape), _whole(gka.shape), _whole(gqb.shape), _whole(gkb.shape), _whole(bd512.shape),
                  _whole(bd128.shape)],
        out_specs=out_specs, out_shape=out_shape,
        scratch_shapes=[pltpu.VMEM((D_MODEL, IN_WIDTH), BF16), _fold_scratch(WIDTH)],
        compiler_params=pltpu.CompilerParams(dimension_semantics=("arbitrary",), vmem_limit_bytes=VMEM_LIMIT),
    )(x2, gain, w_bf, cos, sin_s, gqa, gka, gqb, gkb, bd512, bd128)


def _seq_pos(idx, dil):
    if dil == 4:
        return 4 * (idx % 32) + idx // 32
    return idx


def _upper_mask(dil):
    qi = lax.broadcasted_iota(jnp.int32, (2 * BLOCK, BLOCK), 0) % BLOCK
    kj = lax.broadcasted_iota(jnp.int32, (2 * BLOCK, BLOCK), 1)
    return _seq_pos(kj, dil) > _seq_pos(qi, dil)


def _eye_mask():
    qi = lax.broadcasted_iota(jnp.int32, (2 * BLOCK, BLOCK), 0) % BLOCK
    kj = lax.broadcasted_iota(jnp.int32, (2 * BLOCK, BLOCK), 1)
    return qi == kj


def _stack_heads(a2, c, gqa):
    lane = lax.broadcasted_iota(jnp.int32, (1, LANES), 1) // HEAD_DIM
    zero = jnp.zeros_like(a2)
    if gqa:
        keep = lane == (c // 2)
        return jnp.concatenate([jnp.where(keep, a2, zero), jnp.where(keep, _swap_heads(a2), zero)], axis=0)
    return jnp.concatenate([jnp.where(lane == 0, a2, zero), jnp.where(lane == 1, a2, zero)], axis=0)


def _unstack_heads(a, c, gqa):
    lane = lax.broadcasted_iota(jnp.int32, (1, LANES), 1) // HEAD_DIM
    if gqa:
        return jnp.where(lane == (c // 2), a[:BLOCK], _swap_heads(a[BLOCK:]))
    return jnp.where(lane == 0, a[:BLOCK], a[BLOCK:])


def _stacked_head_ids(c, gqa):
    if gqa:
        return 2 * c + c // 2, 2 * c + 1 - c // 2
    return 2 * c, 2 * c + 1


def _per_head_rows(blk, heads):
    return jnp.concatenate([blk[:, heads[0]:heads[0] + 1], blk[:, heads[1]:heads[1] + 1]], axis=0)


def _attn_view(a, dil):
    if dil == 1:
        return a[None]
    if dil == 4:
        return a.reshape(4, 4, a.shape[1], a.shape[2])
    return a


def _attn_unview(a, dil):
    if dil == 1:
        return a[0]
    if dil == 4:
        return a.reshape(FOLD, a.shape[2], a.shape[3])
    return a


def _attn_specs(dil, nb):
    if dil == 4:
        def spec(fn):
            return lambda w: pl.BlockSpec((4, None, BLOCK // 4, w), lambda r, i: (0, r, fn(i), 0))
    else:
        def spec(fn):
            return lambda w: pl.BlockSpec((None, BLOCK, w), lambda r, i: (r, fn(i), 0))
    return spec


def _blk_load(ref, sl, dil):
    if dil == 4:
        return ref[:, :, sl].reshape(BLOCK, sl.stop - sl.start)
    return ref[:, sl]


def _blk_store(ref, sl, val, dil):
    if dil == 4:
        ref[:, :, sl] = val.reshape(4, BLOCK // 4, sl.stop - sl.start)
    else:
        ref[:, sl] = val


def _swap_heads(a):
    return pltpu.roll(a.astype(F32), HEAD_DIM, 1).astype(a.dtype)


def _attn_fwd(q, k, v, sinks, *, dil, max_dist, name):
    q, k, v = _attn_view(q, dil), _attn_view(k, dil), _attn_view(v, dil)
    kw = k.shape[-1]
    gqa = kw == A_KV_WIDTH
    n_seq = dil
    nb = (q.shape[-2] * (4 if dil == 4 else 1)) // BLOCK
    with_sinks = sinks is not None
    all_lanes = slice(0, LANES)
    assert max_dist in (BLOCK - 1, BLOCK)
    diag = max_dist == BLOCK

    def body(*refs):
        if with_sinks:
            q_ref, kp_ref, kc_ref, vp_ref, vc_ref, sink_ref, o_ref, m_ref, l_ref = refs
        else:
            q_ref, kp_ref, kc_ref, vp_ref, vc_ref, o_ref, m_ref, l_ref = refs

        def block(has_prev):
            upper, eye = _upper_mask(dil), _eye_mask()
            lane = lax.broadcasted_iota(jnp.int32, (1, LANES), 1)
            first_rows = lax.broadcasted_iota(jnp.int32, (2 * BLOCK, 1), 0) < BLOCK
            m_blk = jnp.zeros((BLOCK, LANES), F32)
            l_blk = jnp.ones((BLOCK, LANES), F32)
            for c in range(WIDTH // LANES):
                sl = slice(c * LANES, (c + 1) * LANES)
                ksl = slice(0, LANES) if gqa else sl
                kcur, vcur = _blk_load(kc_ref, ksl, dil), _blk_load(vc_ref, ksl, dil)
                heads = _stacked_head_ids(c, gqa)
                qs = _stack_heads(_blk_load(q_ref, sl, dil), c, gqa)
                if has_prev:
                    kp, vp = _blk_load(kp_ref, ksl, dil), _blk_load(vp_ref, ksl, dil)
                    s = _dot_nt(qs, jnp.concatenate([kp, kcur], axis=0))
                    s_p = s[:, :BLOCK]
                    sc = jnp.where(upper, s_p, s[:, BLOCK:])
                else:
                    sc = jnp.where(upper, NEG, _dot_nt(qs, kcur))
                with_diag = diag and has_prev
                if with_diag:
                    sd = jnp.where(eye, s_p, NEG)
                    m = jnp.max(jnp.maximum(sc, sd), axis=-1, keepdims=True)
                else:
                    m = jnp.max(sc, axis=-1, keepdims=True)
                if with_sinks:
                    sk = jnp.where(first_rows, sink_ref[0, heads[0]], sink_ref[0, heads[1]])
                    m = jnp.maximum(m, sk)
                p = jnp.exp(sc - m)
                zero = jnp.zeros_like(p)
                if with_diag:
                    pd = jnp.exp(sd - m)
                    l = jnp.sum(p + pd, axis=-1, keepdims=True)
                else:
                    pd = zero
                    l = jnp.sum(p, axis=-1, keepdims=True)
                if with_sinks:
                    l = l + jnp.exp(sk - m)
                p_cur = jnp.where(upper, zero, p)
                if has_prev:
                    pf = jnp.concatenate([jnp.where(upper, p, pd), p_cur], axis=1).astype(BF16)
                    o2 = _dot(pf, jnp.concatenate([vp, vcur], axis=0))
                else:
                    o2 = _dot(p_cur.astype(BF16), vcur)
                _blk_store(o_ref, sl, _unstack_heads(o2, c, gqa), dil)
                for n, h in enumerate(heads):
                    rows = slice(n * BLOCK, (n + 1) * BLOCK)
                    m_blk = jnp.where(lane == h, m[rows], m_blk)
                    l_blk = jnp.where(lane == h, l[rows], l_blk)
            _blk_store(m_ref, all_lanes, m_blk, dil)
            _blk_store(l_ref, all_lanes, l_blk, dil)

        @pl.when(pl.program_id(1) == 0)
        def _():
            block(False)

        @pl.when(pl.program_id(1) > 0)
        def _():
            block(True)

    spec = _attn_specs(dil, nb)
    cur = spec(lambda i: i)
    prev = spec(lambda i: jnp.maximum(i - 1, 0))
    in_specs = [cur(WIDTH), prev(kw), cur(kw), prev(kw), cur(kw)]
    args = [q, k, k, v, v]
    if with_sinks:
        in_specs.append(pl.BlockSpec(memory_space=pltpu.SMEM))
        args.append(sinks)
    stats = jax.ShapeDtypeStruct(q.shape[:-1] + (LANES,), F32)
    o, m, l = pl.pallas_call(
        body, name=name, grid=(n_seq, nb), in_specs=in_specs,
        out_specs=(cur(WIDTH), cur(LANES), cur(LANES)),
        out_shape=(jax.ShapeDtypeStruct(q.shape, F32), stats, stats),
        compiler_params=pltpu.CompilerParams(dimension_semantics=("arbitrary", "arbitrary")),
    )(*args)
    return _attn_unview(o, dil), _attn_unview(m, dil), _attn_unview(l, dil)


def _attn_bwd(q, k, v, do, lse, delta, *, dil, max_dist, name):
    q, k, v, do, lse, delta = (_attn_view(a, dil) for a in (q, k, v, do, lse, delta))
    kw = k.shape[-1]
    gqa = kw == A_KV_WIDTH
    n_seq = dil
    nb = (q.shape[-2] * (4 if dil == 4 else 1)) // BLOCK
    n_kc = kw // LANES
    all_lanes = slice(0, LANES)
    assert max_dist in (BLOCK - 1, BLOCK)
    diag = max_dist == BLOCK

    def body(q_ref, kp_ref, kc_ref, vp_ref, vc_ref, do_ref, lse_ref, dl_ref, dq_ref, dk_ref, dv_ref, ck_ref, cv_ref):
        i = pl.program_id(1)

        def block(has_prev):
            upper, eye = _upper_mask(dil), _eye_mask()
            lse_blk = _blk_load(lse_ref, all_lanes, dil)
            dl_blk = _blk_load(dl_ref, all_lanes, dil)
            dk_acc = [None] * n_kc
            dv_acc = [None] * n_kc
            for c in range(WIDTH // LANES):
                sl = slice(c * LANES, (c + 1) * LANES)
                kc = 0 if gqa else c
                ksl = slice(kc * LANES, (kc + 1) * LANES)
                kcur, vcur = _blk_load(kc_ref, ksl, dil), _blk_load(vc_ref, ksl, dil)
                heads = _stacked_head_ids(c, gqa)
                qs = _stack_heads(_blk_load(q_ref, sl, dil), c, gqa)
                dos = _stack_heads(_blk_load(do_ref, sl, dil), c, gqa)
                lse2 = _per_head_rows(lse_blk, heads)
                dl2 = _per_head_rows(dl_blk, heads)
                if has_prev:
                    k2 = jnp.concatenate([_blk_load(kp_ref, ksl, dil), kcur], axis=0)
                    v2 = jnp.concatenate([_blk_load(vp_ref, ksl, dil), vcur], axis=0)
                    s = _dot_nt(qs, k2)
                    dp = _dot_nt(dos, v2)
                    s_p, dp_p = s[:, :BLOCK], dp[:, :BLOCK]
                    sc = jnp.where(upper, s_p, s[:, BLOCK:])
                    dpc = jnp.where(upper, dp_p, dp[:, BLOCK:])
                else:
                    k2, v2 = kcur, vcur
                    sc = jnp.where(upper, NEG, _dot_nt(qs, kcur))
                    dpc = _dot_nt(dos, vcur)
                p = jnp.exp(sc - lse2)
                ds = p * (dpc - dl2)
                zero = jnp.zeros_like(p)
                pf = jnp.where(upper, zero, p)
                dsf = jnp.where(upper, zero, ds)
                if has_prev:
                    if diag:
                        pd = jnp.exp(jnp.where(eye, s_p, NEG) - lse2)
                        dsd = pd * (dp_p - dl2)
                    else:
                        pd = dsd = zero
                    pf = jnp.concatenate([jnp.where(upper, p, pd), pf], axis=1)
                    dsf = jnp.concatenate([jnp.where(upper, ds, dsd), dsf], axis=1)
                pf, dsf = pf.astype(BF16), dsf.astype(BF16)
                dq2 = _dot(dsf, k2)
                dk2 = _dot_tn(dsf, qs)
                dv2 = _dot_tn(pf, dos)
                _blk_store(dq_ref, sl, _unstack_heads(dq2, c, gqa) * SCALE, dil)
                dk_acc[kc] = dk2 if dk_acc[kc] is None else dk_acc[kc] + dk2
                dv_acc[kc] = dv2 if dv_acc[kc] is None else dv_acc[kc] + dv2
            for kc in range(n_kc):
                sl = slice(kc * LANES, (kc + 1) * LANES)
                if has_prev:
                    _blk_store(dk_ref, sl, ck_ref[:, sl] + dk_acc[kc][:BLOCK], dil)
                    _blk_store(dv_ref, sl, cv_ref[:, sl] + dv_acc[kc][:BLOCK], dil)
                    ck_ref[:, sl] = dk_acc[kc][BLOCK:]
                    cv_ref[:, sl] = dv_acc[kc][BLOCK:]
                else:
                    ck_ref[:, sl] = dk_acc[kc]
                    cv_ref[:, sl] = dv_acc[kc]

        @pl.when(i == 0)
        def _():
            block(False)

        @pl.when((i > 0) & (i < nb))
        def _():
            block(True)

        @pl.when(i == nb)
        def _():
            for kc in range(n_kc):
                sl = slice(kc * LANES, (kc + 1) * LANES)
                _blk_store(dk_ref, sl, ck_ref[:, sl], dil)
                _blk_store(dv_ref, sl, cv_ref[:, sl], dil)

    spec = _attn_specs(dil, nb)
    cur = spec(lambda i: jnp.minimum(i, nb - 1))
    prev = spec(lambda i: jnp.clip(i - 1, 0, nb - 1))
    lag = spec(lambda i: jnp.maximum(i - 1, 0))
    sds = jax.ShapeDtypeStruct
    dq, dk, dv = pl.pallas_call(
        body, name=name, grid=(n_seq, nb + 1),
        in_specs=[cur(WIDTH), prev(kw), cur(kw), prev(kw), cur(kw), cur(WIDTH), cur(LANES), cur(LANES)],
        out_specs=(cur(WIDTH), lag(kw), lag(kw)),
        out_shape=(sds(q.shape, F32), sds(k.shape, F32), sds(k.shape, F32)),
        scratch_shapes=[pltpu.VMEM((BLOCK, kw), F32), pltpu.VMEM((BLOCK, kw), F32)],
        compiler_params=pltpu.CompilerParams(dimension_semantics=("arbitrary", "arbitrary")),
    )(q, k, k, v, v, do, lse, delta)
    return _attn_unview(dq, dil), _attn_unview(dk, dil), _attn_unview(dv, dil)


def _outproj(att_a, att_b1, att_b4, att_b16, g_a, g_b, x2, tgt2, w_out_bf, sink_row):
    s = x2.shape[0]
    tm = ROW_TILE

    def body(oa_ref, ma_ref, la_ref, ob1_ref, m1_ref, l1_ref, ob4_ref, m4_ref, l4_ref, ob16_ref, m16_ref, l16_ref,
             ga_ref, gb_ref, x_ref, t_ref, w_ref, sink_ref,
             dy_ref, doa_ref, dob_ref, dobf_ref, dga_ref, dgb_ref, lsea_ref, lseb_ref, lsebf_ref, dla_ref, dlb_ref,
             dlbf_ref, gw_ref, loss_ref, dsink_ref, scr, scr_st):
        i = pl.program_id(0)

        @pl.when(i == 0)
        def _():
            gw_ref[...] = jnp.zeros_like(gw_ref)
            loss_ref[...] = jnp.zeros_like(loss_ref)
            dsink_ref[...] = jnp.zeros_like(dsink_ref)

        ms = [m1_ref[...], _load_folded(m4_ref, scr_st), _load_folded(m16_ref, scr_st)]
        ls = [l1_ref[...], _load_folded(l4_ref, scr_st), _load_folded(l16_ref, scr_st)]
        mx = jnp.maximum(jnp.maximum(ms[0], ms[1]), ms[2])
        scale = [jnp.exp(mp - mx) for mp in ms]
        den = (ls[0] * scale[0] + ls[1] * scale[1]) + ls[2] * scale[2]
        lane = lax.broadcasted_iota(jnp.int32, (tm, LANES), 1)
        lse_b = jnp.where(lane < HEADS, mx + jnp.log(den), 0.0)
        lseb_ref[...] = lse_b
        _store_folded(lsebf_ref, lse_b, scr_st)
        inv_den = 1.0 / den
        o_b = _expand_heads(scale[0] * inv_den) * ob1_ref[...]
        o_b = o_b + _expand_heads(scale[1] * inv_den) * _load_folded(ob4_ref, scr)
        o_b = o_b + _expand_heads(scale[2] * inv_den) * _load_folded(ob16_ref, scr)
        l_a = la_ref[...]
        lse_a = jnp.where(lane < HEADS, ma_ref[...] + jnp.log(l_a), 0.0)
        lsea_ref[...] = lse_a
        o_a = _expand_heads(1.0 / l_a) * oa_ref[...]
        g_a = ga_ref[...]
        g_b = gb_ref[...]
        sg_a = _sigmoid(g_a)
        sg_b = _sigmoid(g_b)
        silu_a = g_a * sg_a
        silu_b = g_b * sg_b
        mixed = jnp.concatenate([o_a * silu_a, o_b * silu_b], axis=1).astype(BF16)
        w = w_ref[...]
        y = x_ref[...] + _dot(mixed, w)
        diff = y - t_ref[...]
        loss_ref[...] += (0.5 / D_MODEL) * jnp.sum(diff * diff)
        dy = diff * (1.0 / D_MODEL)
        dy_ref[...] = dy
        dyb = dy.astype(BF16)
        gw_ref[...] += _dot_tn(mixed, dyb)
        dmixed = _dot_nt(dyb, w)
        dm_a = dmixed[:, :WIDTH]
        dm_b = dmixed[:, WIDTH:]
        do_a = dm_a * silu_a
        do_b = dm_b * silu_b
        doa_ref[...] = do_a.astype(BF16)
        dob_ref[...] = do_b.astype(BF16)
        _store_folded(dobf_ref, do_b, scr)
        dga_ref[...] = (dm_a * o_a * (sg_a * (1.0 + g_a * (1.0 - sg_a)))).astype(BF16)
        dgb_ref[...] = (dm_b * o_b * (sg_b * (1.0 + g_b * (1.0 - sg_b)))).astype(BF16)
        dl_a = _reduce_heads(do_a * o_a)
        dla_ref[...] = dl_a
        dl_b = _reduce_heads(do_b * o_b)
        dlb_ref[...] = dl_b
        _store_folded(dlbf_ref, dl_b, scr_st)
        dsink_ref[...] -= jnp.sum(jnp.exp(sink_ref[...] - lse_a) * dl_a, axis=0, keepdims=True)

    sds = jax.ShapeDtypeStruct
    ln = s // FOLD
    natural = [_rows(WIDTH), _rows(LANES), _rows(LANES)]
    folded = [_folded_rows(WIDTH), _folded_rows(LANES), _folded_rows(LANES)]
    return pl.pallas_call(
        body, name="outproj_fwd_bwd", grid=(s // tm,),
        in_specs=natural + natural + folded + folded
                 + [_rows(WIDTH), _rows(WIDTH), _rows(D_MODEL), _rows(D_MODEL), _whole((D_MODEL, D_MODEL)),
                    _whole((1, LANES))],
        out_specs=(_rows(D_MODEL), _rows(WIDTH), _rows(WIDTH), _folded_rows(WIDTH), _rows(WIDTH), _rows(WIDTH),
                   _rows(LANES), _rows(LANES), _folded_rows(LANES), _rows(LANES), _rows(LANES), _folded_rows(LANES),
                   _whole((D_MODEL, D_MODEL)), _whole((1, LANES)), _whole((1, LANES))),
        out_shape=(sds((s, D_MODEL), F32), sds((s, WIDTH), BF16), sds((s, WIDTH), BF16),
                   sds((FOLD, ln, WIDTH), BF16), sds((s, WIDTH), BF16), sds((s, WIDTH), BF16),
                   sds((s, LANES), F32), sds((s, LANES), F32), sds((FOLD, ln, LANES), F32), sds((s, LANES), F32),
                   sds((s, LANES), F32), sds((FOLD, ln, LANES), F32),
                   sds((D_MODEL, D_MODEL), F32), sds((1, LANES), F32), sds((1, LANES), F32)),
        scratch_shapes=[_fold_scratch(WIDTH), _fold_scratch(LANES)],
        compiler_params=pltpu.CompilerParams(dimension_semantics=("arbitrary",), vmem_limit_bytes=VMEM_LIMIT),
    )(*att_a, *att_b1, *att_b4, *att_b16, g_a, g_b, x2, tgt2, w_out_bf, sink_row)


def _inproj_bwd(x2, dy, gain, w_bf, cos, sin_s, gqa, gka, gqb, gkb, bd512, bd128,
                qa_raw, ka_raw, qb_raw, kb_raw, dq_a, dk_a, dv_a, dqkv_b1, dqkv_b4, dqkv_b16, dg_a, dg_b):
    s = x2.shape[0]
    tm = ROW_TILE

    def body(x_ref, dy_ref, gain_ref, w_hbm, cos_ref, sin_ref, gqa_ref, gka_ref, gqb_ref, gkb_ref, bd512_ref,
             bd128_ref, qa_raw_ref, ka_raw_ref, qb_raw_ref, kb_raw_ref, dqa_ref, dka_ref, dva_ref,
             dq1_ref, dk1_ref, dv1_ref, dq4_ref, dk4_ref, dv4_ref, dq16_ref, dk16_ref, dv16_ref, dga_ref, dgb_ref,
             gx_ref, ht_ref, win_ref,
             dgain_ref, dgqa_ref, dgka_ref, dgqb_ref, dgkb_ref, w_vmem, dproj_ref, scr):
        i = pl.program_id(0)

        @pl.when(i == 0)
        def _():
            pltpu.sync_copy(w_hbm, w_vmem)
            dgain_ref[...] = jnp.zeros_like(dgain_ref)
            dgqa_ref[...] = jnp.zeros_like(dgqa_ref)
            dgka_ref[...] = jnp.zeros_like(dgka_ref)
            dgqb_ref[...] = jnp.zeros_like(dgqb_ref)
            dgkb_ref[...] = jnp.zeros_like(dgkb_ref)

        cos1 = cos_ref[...]
        sin1 = sin_ref[...]
        cos4 = jnp.tile(cos1, (1, 4))
        sin4 = jnp.tile(sin1, (1, 4))

        dt, dg = _qknorm_rope_bwd(dqa_ref[...], qa_raw_ref[...], gqa_ref[...], cos4, sin4, bd512_ref[...])
        dproj_ref[:, C_QA:C_KA] = dt.astype(BF16)
        dgqa_ref[...] += jnp.sum(dg, axis=0, keepdims=True)
        dt, dg = _qknorm_rope_bwd(dka_ref[...], ka_raw_ref[...], gka_ref[...], cos1, sin1, bd128_ref[...])
        dproj_ref[:, C_KA:C_VA] = dt.astype(BF16)
        dgka_ref[...] += jnp.sum(dg, axis=0, keepdims=True)
        dproj_ref[:, C_VA:C_GA] = dva_ref[...].astype(BF16)
        dproj_ref[:, C_GA:C_QB] = dga_ref[...]
        dq = (dq1_ref[...] + _load_folded(dq4_ref, scr)) + _load_folded(dq16_ref, scr)
        dt, dg = _qknorm_rope_bwd(dq, qb_raw_ref[...], gqb_ref[...], cos4, sin4, bd512_ref[...])
        dproj_ref[:, C_QB:C_KB] = dt.astype(BF16)
        dgqb_ref[...] += jnp.sum(dg, axis=0, keepdims=True)
        dk = (dk1_ref[...] + _load_folded(dk4_ref, scr)) + _load_folded(dk16_ref, scr)
        dt, dg = _qknorm_rope_bwd(dk, kb_raw_ref[...], gkb_ref[...], cos4, sin4, bd512_ref[...])
        dproj_ref[:, C_KB:C_VB] = dt.astype(BF16)
        dgkb_ref[...] += jnp.sum(dg, axis=0, keepdims=True)
        dv = (dv1_ref[...] + _load_folded(dv4_ref, scr)) + _load_folded(dv16_ref, scr)
        dproj_ref[:, C_VB:C_GB] = dv.astype(BF16)
        dproj_ref[:, C_GB:C_END] = dgb_ref[...]
        for k, start in enumerate(WIN_START):
            win_ref[k] = dproj_ref[:, start:start + WIN]

        xt = x_ref[...]
        gain_row = gain_ref[...]
        r = lax.rsqrt(jnp.mean(xt * xt, axis=-1, keepdims=True) + EPS)
        xr = xt * r
        ht_ref[...] = (xr * gain_row).T.astype(BF16)
        dh = _dot_nt(dproj_ref[...], w_vmem[...])
        dgain_ref[...] += jnp.sum(dh * xr, axis=0, keepdims=True)
        u = dh * gain_row
        gx_ref[...] = dy_ref[...] + r * (u - xr * jnp.mean(u * xr, axis=-1, keepdims=True))

    def acc_row(w):
        return pl.BlockSpec((1, w), lambda i: (0, 0))

    sds = jax.ShapeDtypeStruct
    any_spec = pl.BlockSpec(memory_space=pl.ANY)
    win_spec = pl.BlockSpec((N_CHIP, tm, WIN), lambda i: (0, i, 0))
    return pl.pallas_call(
        body, name="inproj_bwd", grid=(s // tm,),
        in_specs=[_rows(D_MODEL), _rows(D_MODEL), _whole(gain.shape), any_spec, _rows(LANES), _rows(LANES),
                  _whole(gqa.shape), _whole(gka.shape), _whole(gqb.shape), _whole(gkb.shape), _whole(bd512.shape),
                  _whole(bd128.shape),
                  _rows(WIDTH), _rows(A_KV_WIDTH), _rows(WIDTH), _rows(WIDTH),
                  _rows(WIDTH), _rows(A_KV_WIDTH), _rows(A_KV_WIDTH)]
                 + [_rows(WIDTH)] * 3 + [_folded_rows(WIDTH)] * 6 + [_rows(WIDTH), _rows(WIDTH)],
        out_specs=(_rows(D_MODEL), pl.BlockSpec((D_MODEL, tm), lambda i: (0, i)), win_spec, acc_row(D_MODEL), acc_row(WIDTH), acc_row(A_KV_WIDTH), acc_row(WIDTH), acc_row(WIDTH)),
        out_shape=(sds((s, D_MODEL), F32), sds((D_MODEL, s), BF16), sds((N_CHIP, s, WIN), BF16),
                   sds((1, D_MODEL), F32),
                   sds((1, WIDTH), F32), sds((1, A_KV_WIDTH), F32), sds((1, WIDTH), F32), sds((1, WIDTH), F32)),
        scratch_shapes=[pltpu.VMEM((D_MODEL, IN_WIDTH), BF16), pltpu.VMEM((tm, IN_WIDTH), BF16),
                        _fold_scratch(WIDTH)],
        compiler_params=pltpu.CompilerParams(dimension_semantics=("arbitrary",), vmem_limit_bytes=VMEM_LIMIT),
    )(x2, dy, gain, w_bf, cos, sin_s, gqa, gka, gqb, gkb, bd512, bd128, qa_raw, ka_raw, qb_raw, kb_raw,
      dq_a, dk_a, dv_a, *dqkv_b1, *dqkv_b4, *dqkv_b16, dg_a, dg_b)


def _rope_tables(s):
    half = HEAD_DIM // 2
    inv = ROPE_THETA ** (-jnp.arange(half, dtype=F32) / half)
    ang = jnp.arange(s).astype(F32)[:, None] * inv[None, :]
    cos = jnp.cos(ang)
    sin = jnp.sin(ang)
    return jnp.tile(cos, (1, 4)), jnp.concatenate([-sin, sin, -sin, sin], axis=1)


def _block_diag_ones(w):
    idx = jnp.arange(w) // HEAD_DIM
    return (idx[:, None] == idx[None, :]).astype(BF16)


def _local_step(x2, tgt2, norm_gain, w_in_bf, q_norm_a, k_norm_a, sinks_a, q_norm_b, k_norm_b, w_out_bf):
    s = x2.shape[0]
    cos, sin_s = _rope_tables(s)
    bd512, bd128 = _block_diag_ones(WIDTH), _block_diag_ones(A_KV_WIDTH)
    gqa = jnp.tile(q_norm_a, (1, HEADS))
    gka = jnp.tile(k_norm_a, (1, 2))
    gqb = jnp.tile(q_norm_b, (1, HEADS))
    gkb = jnp.tile(k_norm_b, (1, HEADS))
    sink_row = jnp.pad(sinks_a, ((0, 0), (0, LANES - HEADS)))

    (qa, ka, va, qb, kb, vb, qbf, kbf, vbf, qa_raw, ka_raw, g_a, qb_raw, kb_raw, g_b) = _inproj(
        x2, norm_gain, w_in_bf, cos, sin_s, gqa, gka, gqb, gkb, bd512, bd128)

    att_a = _attn_fwd(qa, ka, va, sinks_a, dil=1, max_dist=A_MAX_DIST, name="attn_a_fwd")
    att_b1 = _attn_fwd(qb, kb, vb, None, dil=1, max_dist=B_MAX_DIST, name="attn_b1_fwd")
    att_b4 = _attn_fwd(qbf, kbf, vbf, None, dil=4, max_dist=B_MAX_DIST, name="attn_b4_fwd")
    att_b16 = _attn_fwd(qbf, kbf, vbf, None, dil=16, max_dist=B_MAX_DIST, name="attn_b16_fwd")

    (dy, do_a, do_b, do_bf, dg_a, dg_b, lse_a, lse_b, lse_bf, dl_a, dl_b, dl_bf, gw_out, loss_part,
     dsink) = _outproj(att_a, att_b1, att_b4, att_b16, g_a, g_b, x2, tgt2, w_out_bf, sink_row)

    dq_a, dk_a, dv_a = _attn_bwd(qa, ka, va, do_a, lse_a, dl_a, dil=1, max_dist=A_MAX_DIST, name="attn_a_bwd")
    d_b1 = _attn_bwd(qb, kb, vb, do_b, lse_b, dl_b, dil=1, max_dist=B_MAX_DIST, name="attn_b1_bwd")
    d_b4 = _attn_bwd(qbf, kbf, vbf, do_bf, lse_bf, dl_bf, dil=4, max_dist=B_MAX_DIST, name="attn_b4_bwd")
    d_b16 = _attn_bwd(qbf, kbf, vbf, do_bf, lse_bf, dl_bf, dil=16, max_dist=B_MAX_DIST, name="attn_b16_bwd")

    gx, h_t, wins, dgain, dgqa, dgka, dgqb, dgkb = _inproj_bwd(
        x2, dy, norm_gain, w_in_bf, cos, sin_s, gqa, gka, gqb, gkb, bd512, bd128,
        qa_raw, ka_raw, qb_raw, kb_raw, dq_a, dk_a, dv_a, d_b1, d_b4, d_b16, dg_a, dg_b)
    return loss_part, gx, h_t, wins, gw_out, (dgain, dgqa, dgka, dsink, dgqb, dgkb)


def _position():
    return lax.axis_index("x"), lax.axis_index("y"), lax.axis_index("c")


GATHER_CHUNKS = 4


def _gather_weights(blocks, name):
    n = len(blocks)
    ch = GATHER_CHUNKS
    n_sems = n * (N_CHIP - 1) * ch

    def body(*refs):
        src_refs, dst_refs = refs[:n], refs[n:2 * n]
        ici_send, ici_recv, d2d_send, d2d_recv, local_sems = refs[2 * n:]
        x, y, c = _position()
        b = 2 * x + y
        copies = []
        for k in range(n):
            local = pltpu.make_async_copy(src_refs[k], dst_refs[k].at[b], local_sems.at[k])
            local.start()
            copies.append(local)

        def rows(k, core, j):
            half = blocks[k].shape[0] // 2
            return pl.ds(core * half + j * (half // ch), half // ch)

        plan = []
        for d in range(1, N_CHIP):
            px, py = x ^ (d >> 1), y ^ (d & 1)
            for j in range(ch):
                for k in range(n):
                    plan.append((px, py, 2 * px + py, k, j, ((d - 1) * ch + j) * n + k))
        sends = []
        for px, py, pb, k, j, sem in plan:
            send = pltpu.make_async_remote_copy(
                src_ref=src_refs[k].at[rows(k, c, j)], dst_ref=dst_refs[k].at[b, rows(k, c, j)],
                send_sem=ici_send.at[sem], recv_sem=ici_recv.at[sem], device_id=(px, py, c), device_id_type=MESH)
            send.start()
            sends.append(send)
        for px, py, pb, k, j, sem in plan:
            landed = dst_refs[k].at[pb, rows(k, c, j)]
            pltpu.make_async_remote_copy(
                src_ref=landed, dst_ref=landed, send_sem=ici_send.at[sem], recv_sem=ici_recv.at[sem],
                device_id=(px, py, c), device_id_type=MESH).wait_recv()
            forward = pltpu.make_async_remote_copy(
                src_ref=landed, dst_ref=landed, send_sem=d2d_send.at[sem], recv_sem=d2d_recv.at[sem],
                device_id=(x, y, 1 - c), device_id_type=MESH)
            forward.start()
            sends.append(forward)
        for px, py, pb, k, j, sem in plan:
            passed = dst_refs[k].at[pb, rows(k, 1 - c, j)]
            pltpu.make_async_remote_copy(
                src_ref=passed, dst_ref=passed, send_sem=d2d_send.at[sem], recv_sem=d2d_recv.at[sem],
                device_id=(x, y, 1 - c), device_id_type=MESH).wait_recv()
        for send in sends:
            send.wait_send()
        for local in copies:
            local.wait()

    vmem_spec = pl.BlockSpec(memory_space=pltpu.VMEM)
    out_shape = tuple(jax.ShapeDtypeStruct((N_CHIP,) + a.shape, a.dtype) for a in blocks)
    return pl.pallas_call(
        body, name=name, in_specs=[vmem_spec] * n, out_specs=tuple([vmem_spec] * n), out_shape=out_shape,
        scratch_shapes=[pltpu.SemaphoreType.DMA((n_sems,)) for _ in range(4)] + [pltpu.SemaphoreType.DMA((n,))],
        compiler_params=pltpu.CompilerParams(vmem_limit_bytes=VMEM_LIMIT),
    )(*blocks)


def _grad_reduce(order, h_t, wins, gw_out, small):
    s = h_t.shape[1]
    tk = GRAD_ROWS
    n_i = s // tk
    half = D_MODEL // 2
    o_half = OUT_ROWS // 2
    n_rel = N_CHIP - 1

    def body(order_ref, ht_ref, win_ref, gwo_ref, small_ref,
             win_out, wout_out, small_out,
             acc, mine, s1, r1, s2, r2, so1, ro1, so2, ro2, pair_in, pair_o, small_land,
             s1_send, s1_recv, s2_send, s2_recv, o1_send, o1_recv, o2_send, o2_recv,
             pair_send, pair_recv, small_send, small_recv):
        j = pl.program_id(0)
        i = pl.program_id(1)
        x, y, c = _position()
        me = 4 * x + 2 * y + c
        sibling = (x, y, 1 - c)
        my_rows = pl.ds(pl.multiple_of(c * half, half), half)
        sib_rows = pl.ds(pl.multiple_of((1 - c) * half, half), half)

        def chip_of(rel):
            return x ^ (rel >> 1), y ^ (rel & 1)

        def level1(k):
            return pltpu.make_async_remote_copy(src_ref=s1.at[k], dst_ref=r1.at[k], send_sem=s1_send.at[k],
                                                recv_sem=s1_recv.at[k], device_id=sibling, device_id_type=MESH)

        def level2(k):
            px, py = chip_of(RELATIONS[k])
            return pltpu.make_async_remote_copy(src_ref=s2.at[k], dst_ref=r2.at[k], send_sem=s2_send.at[k],
                                                recv_sem=s2_recv.at[k], device_id=(px, py, c), device_id_type=MESH)

        def out_level1(bk):
            return pltpu.make_async_remote_copy(src_ref=so1.at[bk], dst_ref=ro1.at[bk], send_sem=o1_send.at[bk],
                                                recv_sem=o1_recv.at[bk], device_id=sibling, device_id_type=MESH)

        def out_level2(k):
            px, py = chip_of(RELATIONS[k])
            return pltpu.make_async_remote_copy(src_ref=so2.at[k], dst_ref=ro2.at[k], send_sem=o2_send.at[k],
                                                recv_sem=o2_recv.at[k], device_id=(px, py, c), device_id_type=MESH)

        def small_copy(d):
            px, py, pc = x ^ (d >> 2), y ^ ((d >> 1) & 1), c ^ (d & 1)
            return pltpu.make_async_remote_copy(src_ref=small_ref, dst_ref=small_land.at[me],
                                                send_sem=small_send.at[d], recv_sem=small_recv.at[d],
                                                device_id=(px, py, pc), device_id_type=MESH)

        def pair_copy(k, buf):
            return pltpu.make_async_remote_copy(src_ref=buf.at[0], dst_ref=buf.at[1], send_sem=pair_send.at[k],
                                                recv_sem=pair_recv.at[k], device_id=sibling, device_id_type=MESH)

        def out_rows(bk, core):
            return pl.ds(pl.multiple_of(bk * OUT_ROWS + core * o_half, o_half), o_half)

        @pl.when((j == 0) & (i == 0))
        def _():
            for d in range(1, N_DEV):
                small_copy(d).start()
            small_land[me] = small_ref[...]
            for bk in range(N_CHIP):
                so1[bk] = gwo_ref[out_rows(bk, 1 - c), :].astype(BF16)
                out_level1(bk).start()

        @pl.when((j == 0) & (i == 1))
        def _():
            b = 2 * x + y
            for bk in range(N_CHIP):
                out_level1(bk).wait_recv()
            for k in range(n_rel):
                px, py = chip_of(RELATIONS[k])
                bk = 2 * px + py
                so2[k] = (gwo_ref[out_rows(bk, c), :] + ro1[bk].astype(F32)).astype(BF16)
                out_level2(k).start()

        @pl.when(i == 0)
        def _():
            acc[...] = jnp.zeros_like(acc)

        for n0 in range(0, WIN, ACC_COLS):
            n1 = min(n0 + ACC_COLS, WIN)
            acc[:, n0:n1] += _dot(ht_ref[...], win_ref[:, n0:n1])

        for k in range(N_CHIP):
            @pl.when((j == k) & (i == n_i - 1))
            def _(k=k):
                s1[k] = acc[sib_rows, :].astype(BF16)
                level1(k).start()
                mine[...] = acc[my_rows, :]

            if k < n_rel:
                @pl.when((j == k + 1) & (i == 1))
                def _(k=k):
                    level1(k).wait_recv()
                    s2[k] = (mine[...] + r1[k].astype(F32)).astype(BF16)
                    level2(k).start()

        @pl.when((j == N_CHIP - 1) & (i == n_i - 1))
        def _():
            b = 2 * x + y
            level1(N_CHIP - 1).wait_recv()
            total = mine[...] + r1[N_CHIP - 1].astype(F32)
            for k in range(n_rel):
                level2(k).wait_recv()
                total = total + r2[k].astype(F32)
            pair_in[0] = total
            pair_copy(0, pair_in).start()
            total_o = gwo_ref[out_rows(b, c), :] + ro1[b].astype(F32)
            for k in range(n_rel):
                out_level2(k).wait_recv()
                total_o = total_o + ro2[k].astype(F32)
            pair_o[0] = total_o
            pair_copy(1, pair_o).start()
            win_out[c] = total
            wout_out[c] = total_o
            for d in range(1, N_DEV):
                small_copy(d).wait_recv()
            small_out[...] = small_land[...]
            pair_copy(0, pair_in).wait_recv()
            win_out[1 - c] = pair_in[1]
            pair_copy(1, pair_o).wait_recv()
            wout_out[1 - c] = pair_o[1]
            for d in range(1, N_DEV):
                small_copy(d).wait_send()
            for k in range(N_CHIP):
                level1(k).wait_send()
                out_level1(k).wait_send()
            for k in range(n_rel):
                level2(k).wait_send()
                out_level2(k).wait_send()
            pair_copy(0, pair_in).wait_send()
            pair_copy(1, pair_o).wait_send()

    vmem = pl.BlockSpec(memory_space=pltpu.VMEM)
    dma = pltpu.SemaphoreType.DMA
    sds = jax.ShapeDtypeStruct
    grid_spec = pltpu.PrefetchScalarGridSpec(
        num_scalar_prefetch=1, grid=(N_CHIP, n_i),
        in_specs=[pl.BlockSpec((D_MODEL, tk), lambda j, i, order: (0, i)),
                  pl.BlockSpec((None, tk, WIN), lambda j, i, order: (order[j], i, 0)), vmem, vmem],
        out_specs=(vmem, vmem, vmem),
        scratch_shapes=[
            pltpu.VMEM((D_MODEL, WIN), F32), pltpu.VMEM((half, WIN), F32),
            pltpu.VMEM((N_CHIP, half, WIN), BF16), pltpu.VMEM((N_CHIP, half, WIN), BF16),
            pltpu.VMEM((n_rel, half, WIN), BF16), pltpu.VMEM((n_rel, half, WIN), BF16),
            pltpu.VMEM((N_CHIP, o_half, D_MODEL), BF16), pltpu.VMEM((N_CHIP, o_half, D_MODEL), BF16),
            pltpu.VMEM((n_rel, o_half, D_MODEL), BF16), pltpu.VMEM((n_rel, o_half, D_MODEL), BF16),
            pltpu.VMEM((2, half, WIN), F32), pltpu.VMEM((2, o_half, D_MODEL), F32),
            pltpu.VMEM((N_DEV, PACK_ROWS, D_MODEL), F32),
            dma((N_CHIP,)), dma((N_CHIP,)), dma((n_rel,)), dma((n_rel,)),
            dma((N_CHIP,)), dma((N_CHIP,)), dma((n_rel,)), dma((n_rel,)),
            dma((2,)), dma((2,)), dma((N_DEV,)), dma((N_DEV,))])
    return pl.pallas_call(
        body, name="grad_w_in_reduce", grid_spec=grid_spec,
        out_shape=(sds((2, half, WIN), F32), sds((2, o_half, D_MODEL), F32), sds((N_DEV, PACK_ROWS, D_MODEL), F32)),
        compiler_params=pltpu.CompilerParams(dimension_semantics=("arbitrary", "arbitrary"),
                                             vmem_limit_bytes=VMEM_LIMIT),
    )(order, h_t, wins, gw_out, small)


ADAM_ROWS = 128


def _adamw_math(w, g, m, v):
    m = ADAM_B1 * m + (1.0 - ADAM_B1) * g
    v = ADAM_B2 * v + (1.0 - ADAM_B2) * (g * g)
    m_hat = m / (1.0 - ADAM_B1 ** ADAM_STEP)
    v_hat = v / (1.0 - ADAM_B2 ** ADAM_STEP)
    delta = -ADAM_LR * (m_hat / (jnp.sqrt(v_hat) + ADAM_EPS) + ADAM_WD * w)
    return delta, m, v


def _adamw(w, g, m, v, name):
    r, c = w.shape

    def body(w_ref, g_ref, m_ref, v_ref, d_ref, nm_ref, nv_ref):
        delta, nm, nv = _adamw_math(w_ref[...], g_ref[...], m_ref[...], v_ref[...])
        d_ref[...] = delta
        nm_ref[...] = nm
        nv_ref[...] = nv

    spec = pl.BlockSpec((ADAM_ROWS, c), lambda i: (i, 0))
    shape = jax.ShapeDtypeStruct((r, c), F32)
    return pl.pallas_call(
        body, name=name, grid=(r // ADAM_ROWS,), in_specs=[spec] * 4, out_specs=(spec,) * 3,
        out_shape=(shape,) * 3,
    )(w, g, m, v)


PACK_ROWS = 8


def _fold_heads(v):
    y = v[:, 0:LANES]
    for j in range(1, v.shape[1] // LANES):
        y = y + v[:, j * LANES:(j + 1) * LANES]
    return y + pltpu.roll(y, HEAD_DIM, 1)


def _small_adamw(recv, w_p, m_p, v_p):
    def body(r_ref, w_ref, m_ref, v_ref, g_ref, d_ref, nm_ref, nv_ref, loss_ref):
        tot = r_ref[0]
        for j in range(1, N_DEV):
            tot = tot + r_ref[j]
        loss_ref[...] = tot[3:4, 0:LANES]
        row1 = tot[1:2, :]
        row2 = tot[2:3, :]
        pieces = [_fold_heads(row1[:, 0:WIDTH]), _fold_heads(row2[:, WIDTH:WIDTH + A_KV_WIDTH]),
                  _fold_heads(row1[:, WIDTH:2 * WIDTH]), _fold_heads(row2[:, 0:WIDTH]),
                  row2[:, WIDTH + A_KV_WIDTH:WIDTH + 2 * A_KV_WIDTH], jnp.zeros((1, 3 * LANES), F32)]
        g = jnp.concatenate([tot[0:1, :], jnp.concatenate(pieces, axis=1), jnp.zeros((PACK_ROWS - 2, D_MODEL), F32)],
                            axis=0)
        g_ref[...] = g
        delta, nm, nv = _adamw_math(w_ref[...], g, m_ref[...], v_ref[...])
        d_ref[...] = delta
        nm_ref[...] = nm
        nv_ref[...] = nv

    shape = jax.ShapeDtypeStruct((PACK_ROWS, D_MODEL), F32)
    return pl.pallas_call(body, name="small_adamw",
                          out_shape=(shape,) * 4 + (jax.ShapeDtypeStruct((1, LANES), F32),))(recv, w_p, m_p, v_p)


def _pack_small(norm_gain, q_a, k_a, q_b, k_b, sinks):
    def lane_pad(a):
        return jnp.pad(a, ((0, 0), (0, LANES - a.shape[1])))
    row1 = jnp.concatenate([lane_pad(q_a), lane_pad(k_a), lane_pad(q_b), lane_pad(k_b), lane_pad(sinks),
                            jnp.zeros((1, 3 * LANES), F32)], axis=1)
    return jnp.concatenate([norm_gain, row1, jnp.zeros((PACK_ROWS - 2, D_MODEL), F32)], axis=0)


def _unpack_small(p):
    return (p[0:1, :], p[1:2, 0:HEAD_DIM], p[1:2, LANES:LANES + HEAD_DIM], p[1:2, 2 * LANES:2 * LANES + HEAD_DIM],
            p[1:2, 3 * LANES:3 * LANES + HEAD_DIM], p[1:2, 4 * LANES:4 * LANES + HEADS])


def kernel(x, norm_gain, w_in, q_norm_a, k_norm_a, sinks_a, q_norm_b, k_norm_b, w_out, loss_target, m_norm_gain, m_w_in, m_q_norm_a, m_k_norm_a, m_sinks_a, m_q_norm_b, m_k_norm_b, m_w_out, v_norm_gain, v_w_in, v_q_norm_a, v_k_norm_a, v_sinks_a, v_q_norm_b, v_k_norm_b, v_w_out):
    chip = 2 * lax.axis_index("x") + lax.axis_index("y")

    w_in_all, w_out_all = _gather_weights([w_in[0].astype(BF16), w_out[0].astype(BF16)], "gather_weights")
    w_in_bf = w_in_all.transpose(1, 0, 2).reshape(D_MODEL, IN_WIDTH)
    w_out_bf = w_out_all.reshape(D_MODEL, D_MODEL)

    loss_part, gx, h_t, wins, gw_out, (dgain, dgqa, dgka, dsink, dgqb, dgkb) = _local_step(
        x[0], loss_target[0], norm_gain, w_in_bf, q_norm_a, k_norm_a, sinks_a, q_norm_b, k_norm_b, w_out_bf)

    small = jnp.concatenate([
        dgain, jnp.concatenate([dgqa, dgqb], axis=1),
        jnp.concatenate([dgkb, dgka, dsink, jnp.zeros((1, D_MODEL - WIDTH - 2 * A_KV_WIDTH), F32)], axis=1),
        jnp.pad(loss_part, ((0, 0), (0, D_MODEL - LANES))),
        jnp.zeros((PACK_ROWS - 4, D_MODEL), F32)], axis=0)
    order = (chip ^ jnp.array(RELATIONS, jnp.int32)).astype(jnp.int32)
    win_sum, wout_sum, small_recv = _grad_reduce(order, h_t, wins, gw_out, small)
    shift = jnp.array(WIN_SHIFT, jnp.int32)[chip]
    g_w_in = lax.dynamic_slice_in_dim(win_sum.reshape(D_MODEL, WIN), shift, IN_COLS, axis=1)
    g_w_out = wout_sum.reshape(OUT_ROWS, D_MODEL)

    d_w_in, nm_w_in, nv_w_in = _adamw(w_in[0], g_w_in, m_w_in[0], v_w_in[0], "adamw_w_in")
    d_w_out, nm_w_out, nv_w_out = _adamw(w_out[0], g_w_out, m_w_out[0], v_w_out[0], "adamw_w_out")
    g_s, d_s, nm_s, nv_s, loss_row = _small_adamw(
        small_recv,
        _pack_small(norm_gain, q_norm_a, k_norm_a, q_norm_b, k_norm_b, sinks_a),
        _pack_small(m_norm_gain, m_q_norm_a, m_k_norm_a, m_q_norm_b, m_k_norm_b, m_sinks_a),
        _pack_small(v_norm_gain, v_q_norm_a, v_k_norm_a, v_q_norm_b, v_k_norm_b, v_sinks_a))
    loss = loss_row[0, 0]

    def leaves(small_packed, big_in, big_out):
        gain, qa, ka, qb, kb, sk = _unpack_small(small_packed)
        return (gain, big_in[None], qa, ka, sk, qb, kb, big_out[None])

    return ((loss, gx[None]) + leaves(g_s, g_w_in, g_w_out) + leaves(d_s, d_w_in, d_w_out)
            + leaves(nm_s, nm_w_in, nm_w_out) + leaves(nv_s, nv_w_in, nv_w_out))
```

```python
import jax
import jax.numpy as jnp
from jax import lax
from jax.experimental import pallas as pl
from jax.experimental.pallas import tpu as pltpu

F32 = jnp.float32
BF16 = jnp.bfloat16

D_MODEL = 1024
HEAD_DIM = 64
HEADS = 8
WIDTH = HEADS * HEAD_DIM
A_KV_WIDTH = 2 * HEAD_DIM
BLOCK = 128
LANES = 128
FOLD = 16
A_MAX_DIST = 127
B_MAX_DIST = 128
ROPE_THETA = 10000.0
EPS = 1e-6
NEG = -1e30
SCALE = HEAD_DIM ** -0.5

IN_WIDTH = 3328
C_QA, C_KA, C_VA, C_GA, C_QB, C_KB, C_VB, C_GB, C_END = 0, 512, 640, 768, 1280, 1792, 2304, 2816, 3328

N_DEV = 8
N_CHIP = 4
MESH = pl.DeviceIdType.MESH
IN_COLS = IN_WIDTH // N_CHIP
WIN = 896
WIN_START = (0, 768, 1664, 2432)
WIN_SHIFT = (0, 64, 0, 64)
OUT_ROWS = D_MODEL // N_CHIP
RELATIONS = (3, 1, 2, 0)

ADAM_LR = 0.001
ADAM_B1 = 0.9
ADAM_B2 = 0.999
ADAM_EPS = 1e-08
ADAM_WD = 0.01
ADAM_STEP = 10

ROW_TILE = 256
FOLD_ROWS = ROW_TILE // FOLD
GRAD_ROWS = 1024
ACC_COLS = 256
VMEM_LIMIT = 56 * 1024 * 1024


def _dot(a, b):
    return jnp.dot(a, b, preferred_element_type=F32)


def _dot_nt(a, b):
    return lax.dot_general(a, b, (((1,), (1,)), ((), ())), preferred_element_type=F32)


def _dot_tn(a, b):
    return lax.dot_general(a, b, (((0,), (0,)), ((), ())), preferred_element_type=F32)


def _head_sum(z, bd):
    w = bd.shape[0]
    zb = z.astype(BF16)
    parts = [_dot(zb[:, a:a + w], bd) for a in range(0, z.shape[1], w)]
    return parts[0] if len(parts) == 1 else jnp.concatenate(parts, axis=1)


def _swap_halves(t):
    w = t.shape[1]
    lane = lax.broadcasted_iota(jnp.int32, t.shape, 1)
    return jnp.where(lane % HEAD_DIM < HEAD_DIM // 2, pltpu.roll(t, w - 32, 1), pltpu.roll(t, 32, 1))


def _qknorm_rope(t, g, cos, sin_s, bd):
    r = lax.rsqrt(_head_sum(t * t, bd) * (1.0 / HEAD_DIM) + EPS)
    n = (t * r) * g
    return n * cos + _swap_halves(n) * sin_s


def _qknorm_rope_bwd(dout, t, g, cos, sin_s, bd):
    dn = dout * cos + _swap_halves(dout * sin_s)
    r = lax.rsqrt(_head_sum(t * t, bd) * (1.0 / HEAD_DIM) + EPS)
    tr = t * r
    u = dn * g
    dt = r * (u - tr * (_head_sum(u * tr, bd) * (1.0 / HEAD_DIM)))
    return dt, dn * tr


def _sigmoid(g):
    return 1.0 / (1.0 + jnp.exp(-g))


def _expand_heads(st):
    t = st.shape[0]
    lane = lax.broadcasted_iota(jnp.int32, (t, LANES), 1)
    chunks = []
    for c in range(WIDTH // LANES):
        chunks.append(jnp.where(lane < HEAD_DIM, st[:, 2 * c:2 * c + 1], st[:, 2 * c + 1:2 * c + 2]))
    return jnp.concatenate(chunks, axis=1)


def _reduce_heads(z):
    t = z.shape[0]
    lane = lax.broadcasted_iota(jnp.int32, (t, LANES), 1)
    out = jnp.zeros((t, LANES), F32)
    for c in range(WIDTH // LANES):
        zc = z[:, c * LANES:(c + 1) * LANES]
        for ph in range(2):
            s = jnp.sum(jnp.where((lane // HEAD_DIM) == ph, zc, 0.0), axis=-1, keepdims=True)
            out = jnp.where(lane == 2 * c + ph, s, out)
    return out


def _fold_scratch(w):
    return pltpu.VMEM((w // LANES, ROW_TILE, LANES), F32)


def _store_folded(out_ref, val, scr):
    n = val.shape[1] // LANES
    for c in range(n):
        scr[c] = val[:, c * LANES:(c + 1) * LANES]
    for r in range(FOLD):
        piece = [scr[c, pl.ds(r, FOLD_ROWS, stride=FOLD), :] for c in range(n)]
        out_ref[r] = (piece[0] if n == 1 else jnp.concatenate(piece, axis=1)).astype(out_ref.dtype)


def _load_folded(in_ref, scr):
    n = in_ref.shape[2] // LANES
    for r in range(FOLD):
        blk = in_ref[r].astype(F32)
        for c in range(n):
            scr[c, pl.ds(r, FOLD_ROWS, stride=FOLD), :] = blk[:, c * LANES:(c + 1) * LANES]
    return scr[0] if n == 1 else jnp.concatenate([scr[c] for c in range(n)], axis=1)


def _rows(w, tm=ROW_TILE):
    return pl.BlockSpec((tm, w), lambda i: (i, 0))


def _folded_rows(w):
    return pl.BlockSpec((FOLD, FOLD_ROWS, w), lambda i: (0, i, 0))


def _whole(shape):
    return pl.BlockSpec(shape, lambda i: (0,) * len(shape))


def _inproj(x2, gain, w_bf, cos, sin_s, gqa, gka, gqb, gkb, bd256, bd128):
    s = x2.shape[0]
    tm = ROW_TILE

    def body(x_ref, gain_ref, w_hbm, cos_ref, sin_ref, gqa_ref, gka_ref, gqb_ref, gkb_ref, bd256_ref, bd128_ref,
             qa_ref, ka_ref, va_ref, qb_ref, kb_ref, vb_ref, qbf_ref, kbf_ref, vbf_ref,
             qa_raw_ref, ka_raw_ref, ga_ref, qb_raw_ref, kb_raw_ref, gb_ref, w_vmem, scr):
        @pl.when(pl.program_id(0) == 0)
        def _():
            pltpu.sync_copy(w_hbm, w_vmem)

        xt = x_ref[...]
        r = lax.rsqrt(jnp.mean(xt * xt, axis=-1, keepdims=True) + EPS)
        h = ((xt * r) * gain_ref[...]).astype(BF16)
        cos1 = cos_ref[...]
        sin1 = sin_ref[...]
        cos4 = jnp.tile(cos1, (1, 4))
        sin4 = jnp.tile(sin1, (1, 4))

        def seg(a, b):
            return _dot_nt(h, w_vmem[a:b, :])

        t = seg(C_QA, C_KA)
        qa_raw_ref[...] = t
        qa_ref[...] = (_qknorm_rope(t, gqa_ref[...], cos4, sin4, bd256_ref[...]) * SCALE).astype(BF16)
        t = seg(C_KA, C_VA)
        ka_raw_ref[...] = t
        ka_ref[...] = _qknorm_rope(t, gka_ref[...], cos1, sin1, bd128_ref[...]).astype(BF16)
        va_ref[...] = seg(C_VA, C_GA).astype(BF16)
        ga_ref[...] = seg(C_GA, C_QB)
        t = seg(C_QB, C_KB)
        qb_raw_ref[...] = t
        t = _qknorm_rope(t, gqb_ref[...], cos4, sin4, bd256_ref[...]) * SCALE
        qb_ref[...] = t.astype(BF16)
        _store_folded(qbf_ref, t, scr)
        t = seg(C_KB, C_VB)
        kb_raw_ref[...] = t
        t = _qknorm_rope(t, gkb_ref[...], cos4, sin4, bd256_ref[...])
        kb_ref[...] = t.astype(BF16)
        _store_folded(kbf_ref, t, scr)
        t = seg(C_VB, C_GB)
        vb_ref[...] = t.astype(BF16)
        _store_folded(vbf_ref, t, scr)
        gb_ref[...] = seg(C_GB, C_END)

    sds = jax.ShapeDtypeStruct
    folded = sds((FOLD, s // FOLD, WIDTH), BF16)
    out_shape = (sds((s, WIDTH), BF16), sds((s, A_KV_WIDTH), BF16), sds((s, A_KV_WIDTH), BF16),
                 sds((s, WIDTH), BF16), sds((s, WIDTH), BF16), sds((s, WIDTH), BF16), folded, folded, folded,
                 sds((s, WIDTH), F32), sds((s, A_KV_WIDTH), F32), sds((s, WIDTH), F32),
                 sds((s, WIDTH), F32), sds((s, WIDTH), F32), sds((s, WIDTH), F32))
    out_specs = (_rows(WIDTH), _rows(A_KV_WIDTH), _rows(A_KV_WIDTH), _rows(WIDTH), _rows(WIDTH), _rows(WIDTH),
                 _folded_rows(WIDTH), _folded_rows(WIDTH), _folded_rows(WIDTH),
                 _rows(WIDTH), _rows(A_KV_WIDTH), _rows(WIDTH), _rows(WIDTH), _rows(WIDTH), _rows(WIDTH))
    return pl.pallas_call(
        body, name="inproj_fwd", grid=(s // tm,),
        in_specs=[_rows(D_MODEL), _whole(gain.shape), pl.BlockSpec(memory_space=pl.ANY), _rows(LANES), _rows(LANES),
                  _whole(gqa.shape), _whole(gka.shape), _whole(gqb.shape), _whole(gkb.shape), _whole(bd256.shape),
                  _whole(bd128.shape)],
        out_specs=out_specs, out_shape=out_shape,
        scratch_shapes=[pltpu.VMEM((IN_WIDTH, D_MODEL), BF16), _fold_scratch(WIDTH)],
        compiler_params=pltpu.CompilerParams(dimension_semantics=("arbitrary",), vmem_limit_bytes=VMEM_LIMIT),
    )(x2, gain, w_bf, cos, sin_s, gqa, gka, gqb, gkb, bd256, bd128)


def _seq_pos(idx, dil):
    if dil == 4:
        return 4 * (idx % 32) + idx // 32
    return idx


def _upper_mask(dil):
    qi = lax.broadcasted_iota(jnp.int32, (2 * BLOCK, BLOCK), 0) % BLOCK
    kj = lax.broadcasted_iota(jnp.int32, (2 * BLOCK, BLOCK), 1)
    return _seq_pos(kj, dil) > _seq_pos(qi, dil)


def _eye_mask():
    qi = lax.broadcasted_iota(jnp.int32, (2 * BLOCK, BLOCK), 0) % BLOCK
    kj = lax.broadcasted_iota(jnp.int32, (2 * BLOCK, BLOCK), 1)
    return qi == kj


def _stack_heads(a2, c, gqa):
    lane = lax.broadcasted_iota(jnp.int32, (1, LANES), 1) // HEAD_DIM
    zero = jnp.zeros_like(a2)
    if gqa:
        keep = lane == (c // 2)
        return jnp.concatenate([jnp.where(keep, a2, zero), jnp.where(keep, _swap_heads(a2), zero)], axis=0)
    return jnp.concatenate([jnp.where(lane == 0, a2, zero), jnp.where(lane == 1, a2, zero)], axis=0)


def _unstack_heads(a, c, gqa):
    lane = lax.broadcasted_iota(jnp.int32, (1, LANES), 1) // HEAD_DIM
    if gqa:
        return jnp.where(lane == (c // 2), a[:BLOCK], _swap_heads(a[BLOCK:]))
    return jnp.where(lane == 0, a[:BLOCK], a[BLOCK:])


def _stacked_head_ids(c, gqa):
    if gqa:
        return 2 * c + c // 2, 2 * c + 1 - c // 2
    return 2 * c, 2 * c + 1


def _per_head_rows(blk, heads):
    return jnp.concatenate([blk[:, heads[0]:heads[0] + 1], blk[:, heads[1]:heads[1] + 1]], axis=0)


def _attn_view(a, dil):
    if dil == 1:
        return a[None]
    if dil == 4:
        return a.reshape(4, 4, a.shape[1], a.shape[2])
    return a


def _attn_unview(a, dil):
    if dil == 1:
        return a[0]
    if dil == 4:
        return a.reshape(FOLD, a.shape[2], a.shape[3])
    return a


def _attn_specs(dil, nb):
    if dil == 4:
        def spec(fn):
            return lambda w: pl.BlockSpec((4, None, BLOCK // 4, w), lambda r, i: (0, r, fn(i), 0))
    else:
        def spec(fn):
            return lambda w: pl.BlockSpec((None, BLOCK, w), lambda r, i: (r, fn(i), 0))
    return spec


def _blk_load(ref, sl, dil):
    if dil == 4:
        return ref[:, :, sl].reshape(BLOCK, sl.stop - sl.start)
    return ref[:, sl]


def _blk_store(ref, sl, val, dil):
    if dil == 4:
        ref[:, :, sl] = val.reshape(4, BLOCK // 4, sl.stop - sl.start)
    else:
        ref[:, sl] = val


def _swap_heads(a):
    return pltpu.roll(a.astype(F32), HEAD_DIM, 1).astype(a.dtype)


def _attn_fwd(q, k, v, sinks, *, dil, max_dist, name):
    q, k, v = _attn_view(q, dil), _attn_view(k, dil), _attn_view(v, dil)
    kw = k.shape[-1]
    gqa = kw == A_KV_WIDTH
    n_seq = dil
    nb = (q.shape[-2] * (4 if dil == 4 else 1)) // BLOCK
    with_sinks = sinks is not None
    all_lanes = slice(0, LANES)
    assert max_dist in (BLOCK - 1, BLOCK)
    diag = max_dist == BLOCK

    def body(*refs):
        if with_sinks:
            q_ref, kp_ref, kc_ref, vp_ref, vc_ref, sink_ref, o_ref, m_ref, l_ref = refs
        else:
            q_ref, kp_ref, kc_ref, vp_ref, vc_ref, o_ref, m_ref, l_ref = refs

        def block(has_prev):
            upper, eye = _upper_mask(dil), _eye_mask()
            lane = lax.broadcasted_iota(jnp.int32, (1, LANES), 1)
            first_rows = lax.broadcasted_iota(jnp.int32, (2 * BLOCK, 1), 0) < BLOCK
            m_blk = jnp.zeros((BLOCK, LANES), F32)
            l_blk = jnp.ones((BLOCK, LANES), F32)
            for c in range(WIDTH // LANES):
                sl = slice(c * LANES, (c + 1) * LANES)
                ksl = slice(0, LANES) if gqa else sl
                kcur, vcur = _blk_load(kc_ref, ksl, dil), _blk_load(vc_ref, ksl, dil)
                heads = _stacked_head_ids(c, gqa)
                qs = _stack_heads(_blk_load(q_ref, sl, dil), c, gqa)
                if has_prev:
                    kp, vp = _blk_load(kp_ref, ksl, dil), _blk_load(vp_ref, ksl, dil)
                    s = _dot_nt(qs, jnp.concatenate([kp, kcur], axis=0))
                    s_p = s[:, :BLOCK]
                    sc = jnp.where(upper, s_p, s[:, BLOCK:])
                else:
                    sc = jnp.where(upper, NEG, _dot_nt(qs, kcur))
                with_diag = diag and has_prev
                if with_diag:
                    sd = jnp.where(eye, s_p, NEG)
                    m = jnp.max(jnp.maximum(sc, sd), axis=-1, keepdims=True)
                else:
                    m = jnp.max(sc, axis=-1, keepdims=True)
                if with_sinks:
                    sk = jnp.where(first_rows, sink_ref[0, heads[0]], sink_ref[0, heads[1]])
                    m = jnp.maximum(m, sk)
                p = jnp.exp(sc - m)
                zero = jnp.zeros_like(p)
                if with_diag:
                    pd = jnp.exp(sd - m)
                    l = jnp.sum(p + pd, axis=-1, keepdims=True)
                else:
                    pd = zero
                    l = jnp.sum(p, axis=-1, keepdims=True)
                if with_sinks:
                    l = l + jnp.exp(sk - m)
                p_cur = jnp.where(upper, zero, p)
                if has_prev:
                    pf = jnp.concatenate([jnp.where(upper, p, pd), p_cur], axis=1).astype(BF16)
                    o2 = _dot(pf, jnp.concatenate([vp, vcur], axis=0))
                else:
                    o2 = _dot(p_cur.astype(BF16), vcur)
                _blk_store(o_ref, sl, _unstack_heads(o2, c, gqa), dil)
                for n, h in enumerate(heads):
                    rows = slice(n * BLOCK, (n + 1) * BLOCK)
                    m_blk = jnp.where(lane == h, m[rows], m_blk)
                    l_blk = jnp.where(lane == h, l[rows], l_blk)
            _blk_store(m_ref, all_lanes, m_blk, dil)
            _blk_store(l_ref, all_lanes, l_blk, dil)

        @pl.when(pl.program_id(1) == 0)
        def _():
            block(False)

        @pl.when(pl.program_id(1) > 0)
        def _():
            block(True)

    spec = _attn_specs(dil, nb)
    cur = spec(lambda i: i)
    prev = spec(lambda i: jnp.maximum(i - 1, 0))
    in_specs = [cur(WIDTH), prev(kw), cur(kw), prev(kw), cur(kw)]
    args = [q, k, k, v, v]
    if with_sinks:
        in_specs.append(pl.BlockSpec(memory_space=pltpu.SMEM))
        args.append(sinks)
    stats = jax.ShapeDtypeStruct(q.shape[:-1] + (LANES,), F32)
    o, m, l = pl.pallas_call(
        body, name=name, grid=(n_seq, nb), in_specs=in_specs,
        out_specs=(cur(WIDTH), cur(LANES), cur(LANES)),
        out_shape=(jax.ShapeDtypeStruct(q.shape, F32), stats, stats),
        compiler_params=pltpu.CompilerParams(dimension_semantics=("arbitrary", "arbitrary")),
    )(*args)
    return _attn_unview(o, dil), _attn_unview(m, dil), _attn_unview(l, dil)


def _attn_bwd(q, k, v, do, lse, delta, *, dil, max_dist, name):
    q, k, v, do, lse, delta = (_attn_view(a, dil) for a in (q, k, v, do, lse, delta))
    kw = k.shape[-1]
    gqa = kw == A_KV_WIDTH
    n_seq = dil
    nb = (q.shape[-2] * (4 if dil == 4 else 1)) // BLOCK
    n_kc = kw // LANES
    all_lanes = slice(0, LANES)
    assert max_dist in (BLOCK - 1, BLOCK)
    diag = max_dist == BLOCK

    def body(q_ref, kp_ref, kc_ref, vp_ref, vc_ref, do_ref, lse_ref, dl_ref, dq_ref, dk_ref, dv_ref, ck_ref, cv_ref):
        i = pl.program_id(1)

        def block(has_prev):
            upper, eye = _upper_mask(dil), _eye_mask()
            lse_blk = _blk_load(lse_ref, all_lanes, dil)
            dl_blk = _blk_load(dl_ref, all_lanes, dil)
            dk_acc = [None] * n_kc
            dv_acc = [None] * n_kc
            for c in range(WIDTH // LANES):
                sl = slice(c * LANES, (c + 1) * LANES)
                kc = 0 if gqa else c
                ksl = slice(kc * LANES, (kc + 1) * LANES)
                kcur, vcur = _blk_load(kc_ref, ksl, dil), _blk_load(vc_ref, ksl, dil)
                heads = _stacked_head_ids(c, gqa)
                qs = _stack_heads(_blk_load(q_ref, sl, dil), c, gqa)
                dos = _stack_heads(_blk_load(do_ref, sl, dil), c, gqa)
                lse2 = _per_head_rows(lse_blk, heads)
                dl2 = _per_head_rows(dl_blk, heads)
                if has_prev:
                    k2 = jnp.concatenate([_blk_load(kp_ref, ksl, dil), kcur], axis=0)
                    v2 = jnp.concatenate([_blk_load(vp_ref, ksl, dil), vcur], axis=0)
                    s = _dot_nt(qs, k2)
                    dp = _dot_nt(dos, v2)
                    s_p, dp_p = s[:, :BLOCK], dp[:, :BLOCK]
                    sc = jnp.where(upper, s_p, s[:, BLOCK:])
                    dpc = jnp.where(upper, dp_p, dp[:, BLOCK:])
                else:
                    k2, v2 = kcur, vcur
                    sc = jnp.where(upper, NEG, _dot_nt(qs, kcur))
                    dpc = _dot_nt(dos, vcur)
                p = jnp.exp(sc - lse2)
                ds = p * (dpc - dl2)
                zero = jnp.zeros_like(p)
                pf = jnp.where(upper, zero, p)
                dsf = jnp.where(upper, zero, ds)
                if has_prev:
                    if diag:
                        pd = jnp.exp(jnp.where(eye, s_p, NEG) - lse2)
                        dsd = pd * (dp_p - dl2)
                    else:
                        pd = dsd = zero
                    pf = jnp.concatenate([jnp.where(upper, p, pd), pf], axis=1)
                    dsf = jnp.concatenate([jnp.where(upper, ds, dsd), dsf], axis=1)
                pf, dsf = pf.astype(BF16), dsf.astype(BF16)
                dq2 = _dot(dsf, k2)
                dk2 = _dot_tn(dsf, qs)
                dv2 = _dot_tn(pf, dos)
                _blk_store(dq_ref, sl, _unstack_heads(dq2, c, gqa) * SCALE, dil)
                dk_acc[kc] = dk2 if dk_acc[kc] is None else dk_acc[kc] + dk2
                dv_acc[kc] = dv2 if dv_acc[kc] is None else dv_acc[kc] + dv2
            for kc in range(n_kc):
                sl = slice(kc * LANES, (kc + 1) * LANES)
                if has_prev:
                    _blk_store(dk_ref, sl, ck_ref[:, sl] + dk_acc[kc][:BLOCK], dil)
                    _blk_store(dv_ref, sl, cv_ref[:, sl] + dv_acc[kc][:BLOCK], dil)
                    ck_ref[:, sl] = dk_acc[kc][BLOCK:]
                    cv_ref[:, sl] = dv_acc[kc][BLOCK:]
                else:
                    ck_ref[:, sl] = dk_acc[kc]
                    cv_ref[:, sl] = dv_acc[kc]

        @pl.when(i == 0)
        def _():
            block(False)

        @pl.when((i > 0) & (i < nb))
        def _():
            block(True)

        @pl.when(i == nb)
        def _():
            for kc in range(n_kc):
                sl = slice(kc * LANES, (kc + 1) * LANES)
                _blk_store(dk_ref, sl, ck_ref[:, sl], dil)
                _blk_store(dv_ref, sl, cv_ref[:, sl], dil)

    spec = _attn_specs(dil, nb)
    cur = spec(lambda i: jnp.minimum(i, nb - 1))
    prev = spec(lambda i: jnp.clip(i - 1, 0, nb - 1))
    lag = spec(lambda i: jnp.maximum(i - 1, 0))
    sds = jax.ShapeDtypeStruct
    dq, dk, dv = pl.pallas_call(
        body, name=name, grid=(n_seq, nb + 1),
        in_specs=[cur(WIDTH), prev(kw), cur(kw), prev(kw), cur(kw), cur(WIDTH), cur(LANES), cur(LANES)],
        out_specs=(cur(WIDTH), lag(kw), lag(kw)),
        out_shape=(sds(q.shape, F32), sds(k.shape, F32), sds(k.shape, F32)),
        scratch_shapes=[pltpu.VMEM((BLOCK, kw), F32), pltpu.VMEM((BLOCK, kw), F32)],
        compiler_params=pltpu.CompilerParams(dimension_semantics=("arbitrary", "arbitrary")),
    )(q, k, k, v, v, do, lse, delta)
    return _attn_unview(dq, dil), _attn_unview(dk, dil), _attn_unview(dv, dil)


def _outproj(att_a, att_b1, att_b4, att_b16, g_a, g_b, x2, tgt2, w_out_bf, sink_row):
    s = x2.shape[0]
    tm = ROW_TILE

    def body(oa_ref, ma_ref, la_ref, ob1_ref, m1_ref, l1_ref, ob4_ref, m4_ref, l4_ref, ob16_ref, m16_ref, l16_ref,
             ga_ref, gb_ref, x_ref, t_ref, w_ref, sink_ref,
             dy_ref, doa_ref, dob_ref, dobf_ref, dga_ref, dgb_ref, lsea_ref, lseb_ref, lsebf_ref, dla_ref, dlb_ref,
             dlbf_ref, gw_ref, loss_ref, dsink_ref, scr, scr_st):
        i = pl.program_id(0)

        @pl.when(i == 0)
        def _():
            gw_ref[...] = jnp.zeros_like(gw_ref)
            loss_ref[...] = jnp.zeros_like(loss_ref)
            dsink_ref[...] = jnp.zeros_like(dsink_ref)

        ms = [m1_ref[...], _load_folded(m4_ref, scr_st), _load_folded(m16_ref, scr_st)]
        ls = [l1_ref[...], _load_folded(l4_ref, scr_st), _load_folded(l16_ref, scr_st)]
        mx = jnp.maximum(jnp.maximum(ms[0], ms[1]), ms[2])
        scale = [jnp.exp(mp - mx) for mp in ms]
        den = (ls[0] * scale[0] + ls[1] * scale[1]) + ls[2] * scale[2]
        lane = lax.broadcasted_iota(jnp.int32, (tm, LANES), 1)
        lse_b = jnp.where(lane < HEADS, mx + jnp.log(den), 0.0)
        lseb_ref[...] = lse_b
        _store_folded(lsebf_ref, lse_b, scr_st)
        inv_den = 1.0 / den
        o_b = _expand_heads(scale[0] * inv_den) * ob1_ref[...]
        o_b = o_b + _expand_heads(scale[1] * inv_den) * _load_folded(ob4_ref, scr)
        o_b = o_b + _expand_heads(scale[2] * inv_den) * _load_folded(ob16_ref, scr)
        l_a = la_ref[...]
        lse_a = jnp.where(lane < HEADS, ma_ref[...] + jnp.log(l_a), 0.0)
        lsea_ref[...] = lse_a
        o_a = _expand_heads(1.0 / l_a) * oa_ref[...]
        g_a = ga_ref[...]
        g_b = gb_ref[...]
        sg_a = _sigmoid(g_a)
        sg_b = _sigmoid(g_b)
        silu_a = g_a * sg_a
        silu_b = g_b * sg_b
        mixed = jnp.concatenate([o_a * silu_a, o_b * silu_b], axis=1).astype(BF16)
        w = w_ref[...]
        y = x_ref[...] + _dot(mixed, w)
        diff = y - t_ref[...]
        loss_ref[...] += (0.5 / D_MODEL) * jnp.sum(diff * diff)
        dy = diff * (1.0 / D_MODEL)
        dy_ref[...] = dy
        dyb = dy.astype(BF16)
        gw_ref[...] += _dot_tn(mixed, dyb)
        dmixed = _dot_nt(dyb, w)
        dm_a = dmixed[:, :WIDTH]
        dm_b = dmixed[:, WIDTH:]
        do_a = dm_a * silu_a
        do_b = dm_b * silu_b
        doa_ref[...] = do_a.astype(BF16)
        dob_ref[...] = do_b.astype(BF16)
        _store_folded(dobf_ref, do_b, scr)
        dga_ref[...] = (dm_a * o_a * (sg_a * (1.0 + g_a * (1.0 - sg_a)))).astype(BF16)
        dgb_ref[...] = (dm_b * o_b * (sg_b * (1.0 + g_b * (1.0 - sg_b)))).astype(BF16)
        dl_a = _reduce_heads(do_a * o_a)
        dla_ref[...] = dl_a
        dl_b = _reduce_heads(do_b * o_b)
        dlb_ref[...] = dl_b
        _store_folded(dlbf_ref, dl_b, scr_st)
        dsink_ref[...] -= jnp.sum(jnp.exp(sink_ref[...] - lse_a) * dl_a, axis=0, keepdims=True)

    sds = jax.ShapeDtypeStruct
    ln = s // FOLD
    natural = [_rows(WIDTH), _rows(LANES), _rows(LANES)]
    folded = [_folded_rows(WIDTH), _folded_rows(LANES), _folded_rows(LANES)]
    return pl.pallas_call(
        body, name="outproj_fwd_bwd", grid=(s // tm,),
        in_specs=natural + natural + folded + folded
                 + [_rows(WIDTH), _rows(WIDTH), _rows(D_MODEL), _rows(D_MODEL), _whole((D_MODEL, D_MODEL)),
                    _whole((1, LANES))],
        out_specs=(_rows(D_MODEL), _rows(WIDTH), _rows(WIDTH), _folded_rows(WIDTH), _rows(WIDTH), _rows(WIDTH),
                   _rows(LANES), _rows(LANES), _folded_rows(LANES), _rows(LANES), _rows(LANES), _folded_rows(LANES),
                   _whole((D_MODEL, D_MODEL)), _whole((1, LANES)), _whole((1, LANES))),
        out_shape=(sds((s, D_MODEL), F32), sds((s, WIDTH), BF16), sds((s, WIDTH), BF16),
                   sds((FOLD, ln, WIDTH), BF16), sds((s, WIDTH), BF16), sds((s, WIDTH), BF16),
                   sds((s, LANES), F32), sds((s, LANES), F32), sds((FOLD, ln, LANES), F32), sds((s, LANES), F32),
                   sds((s, LANES), F32), sds((FOLD, ln, LANES), F32),
                   sds((D_MODEL, D_MODEL), F32), sds((1, LANES), F32), sds((1, LANES), F32)),
        scratch_shapes=[_fold_scratch(WIDTH), _fold_scratch(LANES)],
        compiler_params=pltpu.CompilerParams(dimension_semantics=("arbitrary",), vmem_limit_bytes=VMEM_LIMIT),
    )(*att_a, *att_b1, *att_b4, *att_b16, g_a, g_b, x2, tgt2, w_out_bf, sink_row)


def _inproj_bwd(x2, dy, gain, w_bf, cos, sin_s, gqa, gka, gqb, gkb, bd256, bd128,
                qa_raw, ka_raw, qb_raw, kb_raw, dq_a, dk_a, dv_a, dqkv_b1, dqkv_b4, dqkv_b16, dg_a, dg_b):
    s = x2.shape[0]
    tm = ROW_TILE

    def body(x_ref, dy_ref, gain_ref, w_hbm, cos_ref, sin_ref, gqa_ref, gka_ref, gqb_ref, gkb_ref, bd256_ref,
             bd128_ref, qa_raw_ref, ka_raw_ref, qb_raw_ref, kb_raw_ref, dqa_ref, dka_ref, dva_ref,
             dq1_ref, dk1_ref, dv1_ref, dq4_ref, dk4_ref, dv4_ref, dq16_ref, dk16_ref, dv16_ref, dga_ref, dgb_ref,
             gx_ref, ht_ref, win_ref,
             dgain_ref, dgqa_ref, dgka_ref, dgqb_ref, dgkb_ref, w_vmem, dproj_ref, scr):
        i = pl.program_id(0)

        @pl.when(i == 0)
        def _():
            pltpu.sync_copy(w_hbm, w_vmem)
            dgain_ref[...] = jnp.zeros_like(dgain_ref)
            dgqa_ref[...] = jnp.zeros_like(dgqa_ref)
            dgka_ref[...] = jnp.zeros_like(dgka_ref)
            dgqb_ref[...] = jnp.zeros_like(dgqb_ref)
            dgkb_ref[...] = jnp.zeros_like(dgkb_ref)

        cos1 = cos_ref[...]
        sin1 = sin_ref[...]
        cos4 = jnp.tile(cos1, (1, 4))
        sin4 = jnp.tile(sin1, (1, 4))

        dt, dg = _qknorm_rope_bwd(dqa_ref[...], qa_raw_ref[...], gqa_ref[...], cos4, sin4, bd256_ref[...])
        dproj_ref[:, C_QA:C_KA] = dt.astype(BF16)
        dgqa_ref[...] += jnp.sum(dg, axis=0, keepdims=True)
        dt, dg = _qknorm_rope_bwd(dka_ref[...], ka_raw_ref[...], gka_ref[...], cos1, sin1, bd128_ref[...])
        dproj_ref[:, C_KA:C_VA] = dt.astype(BF16)
        dgka_ref[...] += jnp.sum(dg, axis=0, keepdims=True)
        dproj_ref[:, C_VA:C_GA] = dva_ref[...].astype(BF16)
        dproj_ref[:, C_GA:C_QB] = dga_ref[...]
        dq = (dq1_ref[...] + _load_folded(dq4_ref, scr)) + _load_folded(dq16_ref, scr)
        dt, dg = _qknorm_rope_bwd(dq, qb_raw_ref[...], gqb_ref[...], cos4, sin4, bd256_ref[...])
        dproj_ref[:, C_QB:C_KB] = dt.astype(BF16)
        dgqb_ref[...] += jnp.sum(dg, axis=0, keepdims=True)
        dk = (dk1_ref[...] + _load_folded(dk4_ref, scr)) + _load_folded(dk16_ref, scr)
        dt, dg = _qknorm_rope_bwd(dk, kb_raw_ref[...], gkb_ref[...], cos4, sin4, bd256_ref[...])
        dproj_ref[:, C_KB:C_VB] = dt.astype(BF16)
        dgkb_ref[...] += jnp.sum(dg, axis=0, keepdims=True)
        dv = (dv1_ref[...] + _load_folded(dv4_ref, scr)) + _load_folded(dv16_ref, scr)
        dproj_ref[:, C_VB:C_GB] = dv.astype(BF16)
        dproj_ref[:, C_GB:C_END] = dgb_ref[...]
        for k, start in enumerate(WIN_START):
            win_ref[k] = dproj_ref[:, start:start + WIN]

        xt = x_ref[...]
        gain_row = gain_ref[...]
        r = lax.rsqrt(jnp.mean(xt * xt, axis=-1, keepdims=True) + EPS)
        xr = xt * r
        ht_ref[...] = (xr * gain_row).T.astype(BF16)
        dh = _dot(dproj_ref[...], w_vmem[...])
        dgain_ref[...] += jnp.sum(dh * xr, axis=0, keepdims=True)
        u = dh * gain_row
        gx_ref[...] = dy_ref[...] + r * (u - xr * jnp.mean(u * xr, axis=-1, keepdims=True))

    def acc_row(w):
        return pl.BlockSpec((1, w), lambda i: (0, 0))

    sds = jax.ShapeDtypeStruct
    any_spec = pl.BlockSpec(memory_space=pl.ANY)
    win_spec = pl.BlockSpec((N_CHIP, tm, WIN), lambda i: (0, i, 0))
    return pl.pallas_call(
        body, name="inproj_bwd", grid=(s // tm,),
        in_specs=[_rows(D_MODEL), _rows(D_MODEL), _whole(gain.shape), any_spec, _rows(LANES), _rows(LANES),
                  _whole(gqa.shape), _whole(gka.shape), _whole(gqb.shape), _whole(gkb.shape), _whole(bd256.shape),
                  _whole(bd128.shape),
                  _rows(WIDTH), _rows(A_KV_WIDTH), _rows(WIDTH), _rows(WIDTH),
                  _rows(WIDTH), _rows(A_KV_WIDTH), _rows(A_KV_WIDTH)]
                 + [_rows(WIDTH)] * 3 + [_folded_rows(WIDTH)] * 6 + [_rows(WIDTH), _rows(WIDTH)],
        out_specs=(_rows(D_MODEL), pl.BlockSpec((D_MODEL, tm), lambda i: (0, i)), win_spec, acc_row(D_MODEL), acc_row(WIDTH), acc_row(A_KV_WIDTH), acc_row(WIDTH), acc_row(WIDTH)),
        out_shape=(sds((s, D_MODEL), F32), sds((D_MODEL, s), BF16), sds((N_CHIP, s, WIN), BF16),
                   sds((1, D_MODEL), F32),
                   sds((1, WIDTH), F32), sds((1, A_KV_WIDTH), F32), sds((1, WIDTH), F32), sds((1, WIDTH), F32)),
        scratch_shapes=[pltpu.VMEM((IN_WIDTH, D_MODEL), BF16), pltpu.VMEM((tm, IN_WIDTH), BF16),
                        _fold_scratch(WIDTH)],
        compiler_params=pltpu.CompilerParams(dimension_semantics=("arbitrary",), vmem_limit_bytes=VMEM_LIMIT),
    )(x2, dy, gain, w_bf, cos, sin_s, gqa, gka, gqb, gkb, bd256, bd128, qa_raw, ka_raw, qb_raw, kb_raw,
      dq_a, dk_a, dv_a, *dqkv_b1, *dqkv_b4, *dqkv_b16, dg_a, dg_b)


def _rope_tables(s):
    half = HEAD_DIM // 2
    inv = jnp.tile(ROPE_THETA ** (-jnp.arange(half, dtype=F32) / half), 4)
    sign = jnp.tile(jnp.concatenate([-jnp.ones((half,), F32), jnp.ones((half,), F32)]), 2)
    ang = jnp.arange(s).astype(F32)[:, None] * inv[None, :]
    return jnp.cos(ang), jnp.sin(ang) * sign[None, :]


def _block_diag_ones(w):
    idx = jnp.arange(w) // HEAD_DIM
    return (idx[:, None] == idx[None, :]).astype(BF16)


def _local_step(x2, tgt2, norm_gain, w_in_bf, q_norm_a, k_norm_a, sinks_a, q_norm_b, k_norm_b, w_out_bf):
    s = x2.shape[0]
    cos, sin_s = _rope_tables(s)
    bd256, bd128 = _block_diag_ones(2 * LANES), _block_diag_ones(A_KV_WIDTH)
    gqa = jnp.tile(q_norm_a, (1, HEADS))
    gka = jnp.tile(k_norm_a, (1, 2))
    gqb = jnp.tile(q_norm_b, (1, HEADS))
    gkb = jnp.tile(k_norm_b, (1, HEADS))
    sink_row = jnp.pad(sinks_a, ((0, 0), (0, LANES - HEADS)))

    (qa, ka, va, qb, kb, vb, qbf, kbf, vbf, qa_raw, ka_raw, g_a, qb_raw, kb_raw, g_b) = _inproj(
        x2, norm_gain, w_in_bf, cos, sin_s, gqa, gka, gqb, gkb, bd256, bd128)

    att_a = _attn_fwd(qa, ka, va, sinks_a, dil=1, max_dist=A_MAX_DIST, name="attn_a_fwd")
    att_b1 = _attn_fwd(qb, kb, vb, None, dil=1, max_dist=B_MAX_DIST, name="attn_b1_fwd")
    att_b4 = _attn_fwd(qbf, kbf, vbf, None, dil=4, max_dist=B_MAX_DIST, name="attn_b4_fwd")
    att_b16 = _attn_fwd(qbf, kbf, vbf, None, dil=16, max_dist=B_MAX_DIST, name="attn_b16_fwd")

    (dy, do_a, do_b, do_bf, dg_a, dg_b, lse_a, lse_b, lse_bf, dl_a, dl_b, dl_bf, gw_out, loss_part,
     dsink) = _outproj(att_a, att_b1, att_b4, att_b16, g_a, g_b, x2, tgt2, w_out_bf, sink_row)

    dq_a, dk_a, dv_a = _attn_bwd(qa, ka, va, do_a, lse_a, dl_a, dil=1, max_dist=A_MAX_DIST, name="attn_a_bwd")
    d_b1 = _attn_bwd(qb, kb, vb, do_b, lse_b, dl_b, dil=1, max_dist=B_MAX_DIST, name="attn_b1_bwd")
    d_b4 = _attn_bwd(qbf, kbf, vbf, do_bf, lse_bf, dl_bf, dil=4, max_dist=B_MAX_DIST, name="attn_b4_bwd")
    d_b16 = _attn_bwd(qbf, kbf, vbf, do_bf, lse_bf, dl_bf, dil=16, max_dist=B_MAX_DIST, name="attn_b16_bwd")

    gx, h_t, wins, dgain, dgqa, dgka, dgqb, dgkb = _inproj_bwd(
        x2, dy, norm_gain, w_in_bf, cos, sin_s, gqa, gka, gqb, gkb, bd256, bd128,
        qa_raw, ka_raw, qb_raw, kb_raw, dq_a, dk_a, dv_a, d_b1, d_b4, d_b16, dg_a, dg_b)
    return loss_part, gx, h_t, wins, gw_out, (dgain, dgqa, dgka, dsink, dgqb, dgkb)


def _position():
    return lax.axis_index("x"), lax.axis_index("y"), lax.axis_index("c")


GATHER_CHUNKS = 2


def _gather_weights(blocks, name):
    n = len(blocks)
    ch = GATHER_CHUNKS
    n_sems = n * (N_CHIP - 1) * ch

    def body(*refs):
        src_refs, dst_refs = refs[:n], refs[n:2 * n]
        ici_send, ici_recv, d2d_send, d2d_recv, local_sems = refs[2 * n:]
        x, y, c = _position()
        b = 2 * x + y
        copies = []
        for k in range(n):
            local = pltpu.make_async_copy(src_refs[k], dst_refs[k].at[b], local_sems.at[k])
            local.start()
            copies.append(local)

        def rows(k, core, j):
            half = blocks[k].shape[0] // 2
            return pl.ds(core * half + j * (half // ch), half // ch)

        plan = []
        for d in range(1, N_CHIP):
            px, py = x ^ (d >> 1), y ^ (d & 1)
            for j in range(ch):
                for k in range(n):
                    plan.append((px, py, 2 * px + py, k, j, ((d - 1) * ch + j) * n + k))
        sends = []
        for px, py, pb, k, j, sem in plan:
            send = pltpu.make_async_remote_copy(
                src_ref=src_refs[k].at[rows(k, c, j)], dst_ref=dst_refs[k].at[b, rows(k, c, j)],
                send_sem=ici_send.at[sem], recv_sem=ici_recv.at[sem], device_id=(px, py, c), device_id_type=MESH)
            send.start()
            sends.append(send)
        for px, py, pb, k, j, sem in plan:
            landed = dst_refs[k].at[pb, rows(k, c, j)]
            pltpu.make_async_remote_copy(
                src_ref=landed, dst_ref=landed, send_sem=ici_send.at[sem], recv_sem=ici_recv.at[sem],
                device_id=(px, py, c), device_id_type=MESH).wait_recv()
            forward = pltpu.make_async_remote_copy(
                src_ref=landed, dst_ref=landed, send_sem=d2d_send.at[sem], recv_sem=d2d_recv.at[sem],
                device_id=(x, y, 1 - c), device_id_type=MESH)
            forward.start()
            sends.append(forward)
        for px, py, pb, k, j, sem in plan:
            passed = dst_refs[k].at[pb, rows(k, 1 - c, j)]
            pltpu.make_async_remote_copy(
                src_ref=passed, dst_ref=passed, send_sem=d2d_send.at[sem], recv_sem=d2d_recv.at[sem],
                device_id=(x, y, 1 - c), device_id_type=MESH).wait_recv()
        for send in sends:
            send.wait_send()
        for local in copies:
            local.wait()

    vmem_spec = pl.BlockSpec(memory_space=pltpu.VMEM)
    out_shape = tuple(jax.ShapeDtypeStruct((N_CHIP,) + a.shape, a.dtype) for a in blocks)
    return pl.pallas_call(
        body, name=name, in_specs=[vmem_spec] * n, out_specs=tuple([vmem_spec] * n), out_shape=out_shape,
        scratch_shapes=[pltpu.SemaphoreType.DMA((n_sems,)) for _ in range(4)] + [pltpu.SemaphoreType.DMA((n,))],
        compiler_params=pltpu.CompilerParams(vmem_limit_bytes=VMEM_LIMIT),
    )(*blocks)


def _grad_reduce(order, h_t, wins, gw_out, small):
    s = h_t.shape[1]
    tk = GRAD_ROWS
    n_i = s // tk
    half = D_MODEL // 2
    o_half = OUT_ROWS // 2
    n_rel = N_CHIP - 1

    def body(order_ref, ht_ref, win_ref, gwo_ref, small_ref,
             win_out, wout_out, small_out,
             acc, mine, s1, r1, s2, r2, so1, ro1, so2, ro2, pair_in, pair_o, small_land,
             s1_send, s1_recv, s2_send, s2_recv, o1_send, o1_recv, o2_send, o2_recv,
             pair_send, pair_recv, small_send, small_recv):
        j = pl.program_id(0)
        i = pl.program_id(1)
        x, y, c = _position()
        me = 4 * x + 2 * y + c
        sibling = (x, y, 1 - c)
        my_rows = pl.ds(pl.multiple_of(c * half, half), half)
        sib_rows = pl.ds(pl.multiple_of((1 - c) * half, half), half)

        def chip_of(rel):
            return x ^ (rel >> 1), y ^ (rel & 1)

        def level1(k):
            return pltpu.make_async_remote_copy(src_ref=s1.at[k], dst_ref=r1.at[k], send_sem=s1_send.at[k],
                                                recv_sem=s1_recv.at[k], device_id=sibling, device_id_type=MESH)

        def level2(k):
            px, py = chip_of(RELATIONS[k])
            return pltpu.make_async_remote_copy(src_ref=s2.at[k], dst_ref=r2.at[k], send_sem=s2_send.at[k],
                                                recv_sem=s2_recv.at[k], device_id=(px, py, c), device_id_type=MESH)

        def out_level1(bk):
            return pltpu.make_async_remote_copy(src_ref=so1.at[bk], dst_ref=ro1.at[bk], send_sem=o1_send.at[bk],
                                                recv_sem=o1_recv.at[bk], device_id=sibling, device_id_type=MESH)

        def out_level2(k):
            px, py = chip_of(RELATIONS[k])
            return pltpu.make_async_remote_copy(src_ref=so2.at[k], dst_ref=ro2.at[k], send_sem=o2_send.at[k],
                                                recv_sem=o2_recv.at[k], device_id=(px, py, c), device_id_type=MESH)

        def small_copy(d):
            px, py, pc = x ^ (d >> 2), y ^ ((d >> 1) & 1), c ^ (d & 1)
            return pltpu.make_async_remote_copy(src_ref=small_ref, dst_ref=small_land.at[me],
                                                send_sem=small_send.at[d], recv_sem=small_recv.at[d],
                                                device_id=(px, py, pc), device_id_type=MESH)

        def pair_copy(k, buf):
            return pltpu.make_async_remote_copy(src_ref=buf.at[0], dst_ref=buf.at[1], send_sem=pair_send.at[k],
                                                recv_sem=pair_recv.at[k], device_id=sibling, device_id_type=MESH)

        def out_rows(bk, core):
            return pl.ds(pl.multiple_of(bk * OUT_ROWS + core * o_half, o_half), o_half)

        @pl.when((j == 0) & (i == 0))
        def _():
            for d in range(1, N_DEV):
                small_copy(d).start()
            small_land[me] = small_ref[...]
            for bk in range(N_CHIP):
                so1[bk] = gwo_ref[out_rows(bk, 1 - c), :].astype(BF16)
                out_level1(bk).start()

        @pl.when((j == 0) & (i == 1))
        def _():
            b = 2 * x + y
            for bk in range(N_CHIP):
                out_level1(bk).wait_recv()
            for k in range(n_rel):
                px, py = chip_of(RELATIONS[k])
                bk = 2 * px + py
                so2[k] = (gwo_ref[out_rows(bk, c), :] + ro1[bk].astype(F32)).astype(BF16)
                out_level2(k).start()

        @pl.when(i == 0)
        def _():
            acc[...] = jnp.zeros_like(acc)

        for n0 in range(0, WIN, ACC_COLS):
            n1 = min(n0 + ACC_COLS, WIN)
            acc[:, n0:n1] += _dot(ht_ref[...], win_ref[:, n0:n1])

        for k in range(N_CHIP):
            @pl.when((j == k) & (i == n_i - 1))
            def _(k=k):
                s1[k] = acc[sib_rows, :].astype(BF16)
                level1(k).start()
                mine[...] = acc[my_rows, :]

            if k < n_rel:
                @pl.when((j == k + 1) & (i == 1))
                def _(k=k):
                    level1(k).wait_recv()
                    s2[k] = (mine[...] + r1[k].astype(F32)).astype(BF16)
                    level2(k).start()

        @pl.when((j == N_CHIP - 1) & (i == n_i - 1))
        def _():
            b = 2 * x + y
            level1(N_CHIP - 1).wait_recv()
            total = mine[...] + r1[N_CHIP - 1].astype(F32)
            for k in range(n_rel):
                level2(k).wait_recv()
                total = total + r2[k].astype(F32)
            total = total.T
            pair_in[0] = total
            pair_copy(0, pair_in).start()
            total_o = gwo_ref[out_rows(b, c), :] + ro1[b].astype(F32)
            for k in range(n_rel):
                out_level2(k).wait_recv()
                total_o = total_o + ro2[k].astype(F32)
            pair_o[0] = total_o
            pair_copy(1, pair_o).start()
            win_out[c] = total
            wout_out[c] = total_o
            for d in range(1, N_DEV):
                small_copy(d).wait_recv()
            small_out[...] = small_land[...]
            pair_copy(0, pair_in).wait_recv()
            win_out[1 - c] = pair_in[1]
            pair_copy(1, pair_o).wait_recv()
            wout_out[1 - c] = pair_o[1]
            for d in range(1, N_DEV):
                small_copy(d).wait_send()
            for k in range(N_CHIP):
                level1(k).wait_send()
                out_level1(k).wait_send()
            for k in range(n_rel):
                level2(k).wait_send()
                out_level2(k).wait_send()
            pair_copy(0, pair_in).wait_send()
            pair_copy(1, pair_o).wait_send()

    vmem = pl.BlockSpec(memory_space=pltpu.VMEM)
    dma = pltpu.SemaphoreType.DMA
    sds = jax.ShapeDtypeStruct
    grid_spec = pltpu.PrefetchScalarGridSpec(
        num_scalar_prefetch=1, grid=(N_CHIP, n_i),
        in_specs=[pl.BlockSpec((D_MODEL, tk), lambda j, i, order: (0, i)),
                  pl.BlockSpec((None, tk, WIN), lambda j, i, order: (order[j], i, 0)), vmem, vmem],
        out_specs=(vmem, vmem, vmem),
        scratch_shapes=[
            pltpu.VMEM((D_MODEL, WIN), F32), pltpu.VMEM((half, WIN), F32),
            pltpu.VMEM((N_CHIP, half, WIN), BF16), pltpu.VMEM((N_CHIP, half, WIN), BF16),
            pltpu.VMEM((n_rel, half, WIN), BF16), pltpu.VMEM((n_rel, half, WIN), BF16),
            pltpu.VMEM((N_CHIP, o_half, D_MODEL), BF16), pltpu.VMEM((N_CHIP, o_half, D_MODEL), BF16),
            pltpu.VMEM((n_rel, o_half, D_MODEL), BF16), pltpu.VMEM((n_rel, o_half, D_MODEL), BF16),
            pltpu.VMEM((2, WIN, half), F32), pltpu.VMEM((2, o_half, D_MODEL), F32),
            pltpu.VMEM((N_DEV, PACK_ROWS, D_MODEL), F32),
            dma((N_CHIP,)), dma((N_CHIP,)), dma((n_rel,)), dma((n_rel,)),
            dma((N_CHIP,)), dma((N_CHIP,)), dma((n_rel,)), dma((n_rel,)),
            dma((2,)), dma((2,)), dma((N_DEV,)), dma((N_DEV,))])
    return pl.pallas_call(
        body, name="grad_w_in_reduce", grid_spec=grid_spec,
        out_shape=(sds((2, WIN, half), F32), sds((2, o_half, D_MODEL), F32), sds((N_DEV, PACK_ROWS, D_MODEL), F32)),
        compiler_params=pltpu.CompilerParams(dimension_semantics=("arbitrary", "arbitrary"),
                                             vmem_limit_bytes=VMEM_LIMIT),
    )(order, h_t, wins, gw_out, small)


ADAM_STEPS = 4


def _adamw_math(w, g, m, v):
    m = ADAM_B1 * m + (1.0 - ADAM_B1) * g
    v = ADAM_B2 * v + (1.0 - ADAM_B2) * (g * g)
    m_hat = m / (1.0 - ADAM_B1 ** ADAM_STEP)
    v_hat = v / (1.0 - ADAM_B2 ** ADAM_STEP)
    delta = -ADAM_LR * (m_hat / (jnp.sqrt(v_hat) + ADAM_EPS) + ADAM_WD * w)
    return delta, m, v


def _adamw(w, g, m, v, name):
    r, c = w.shape

    def body(w_ref, g_ref, m_ref, v_ref, d_ref, nm_ref, nv_ref):
        delta, nm, nv = _adamw_math(w_ref[...], g_ref[...], m_ref[...], v_ref[...])
        d_ref[...] = delta
        nm_ref[...] = nm
        nv_ref[...] = nv

    rows = r // ADAM_STEPS
    assert rows * ADAM_STEPS == r and rows % 8 == 0
    spec = pl.BlockSpec((rows, c), lambda i: (i, 0))
    shape = jax.ShapeDtypeStruct((r, c), F32)
    return pl.pallas_call(
        body, name=name, grid=(ADAM_STEPS,), in_specs=[spec] * 4, out_specs=(spec,) * 3,
        out_shape=(shape,) * 3, compiler_params=pltpu.CompilerParams(vmem_limit_bytes=VMEM_LIMIT),
    )(w, g, m, v)


PACK_ROWS = 8


def _fold_heads(v):
    y = v[:, 0:LANES]
    for j in range(1, v.shape[1] // LANES):
        y = y + v[:, j * LANES:(j + 1) * LANES]
    return y + pltpu.roll(y, HEAD_DIM, 1)


def _small_adamw(recv, w_p, m_p, v_p):
    def body(r_ref, w_ref, m_ref, v_ref, g_ref, d_ref, nm_ref, nv_ref, loss_ref):
        tot = r_ref[0]
        for j in range(1, N_DEV):
            tot = tot + r_ref[j]
        loss_ref[...] = tot[3:4, 0:LANES]
        row1 = tot[1:2, :]
        row2 = tot[2:3, :]
        pieces = [_fold_heads(row1[:, 0:WIDTH]), _fold_heads(row2[:, WIDTH:WIDTH + A_KV_WIDTH]),
                  _fold_heads(row1[:, WIDTH:2 * WIDTH]), _fold_heads(row2[:, 0:WIDTH]),
                  row2[:, WIDTH + A_KV_WIDTH:WIDTH + 2 * A_KV_WIDTH], jnp.zeros((1, 3 * LANES), F32)]
        g = jnp.concatenate([tot[0:1, :], jnp.concatenate(pieces, axis=1), jnp.zeros((PACK_ROWS - 2, D_MODEL), F32)],
                            axis=0)
        g_ref[...] = g
        delta, nm, nv = _adamw_math(w_ref[...], g, m_ref[...], v_ref[...])
        d_ref[...] = delta
        nm_ref[...] = nm
        nv_ref[...] = nv

    shape = jax.ShapeDtypeStruct((PACK_ROWS, D_MODEL), F32)
    return pl.pallas_call(body, name="small_adamw",
                          out_shape=(shape,) * 4 + (jax.ShapeDtypeStruct((1, LANES), F32),))(recv, w_p, m_p, v_p)


def _pack_small(norm_gain, q_a, k_a, q_b, k_b, sinks):
    def lane_pad(a):
        return jnp.pad(a, ((0, 0), (0, LANES - a.shape[1])))
    row1 = jnp.concatenate([lane_pad(q_a), lane_pad(k_a), lane_pad(q_b), lane_pad(k_b), lane_pad(sinks),
                            jnp.zeros((1, 3 * LANES), F32)], axis=1)
    return jnp.concatenate([norm_gain, row1, jnp.zeros((PACK_ROWS - 2, D_MODEL), F32)], axis=0)


def _unpack_small(p):
    return (p[0:1, :], p[1:2, 0:HEAD_DIM], p[1:2, LANES:LANES + HEAD_DIM], p[1:2, 2 * LANES:2 * LANES + HEAD_DIM],
            p[1:2, 3 * LANES:3 * LANES + HEAD_DIM], p[1:2, 4 * LANES:4 * LANES + HEADS])


def kernel(x, norm_gain, w_in, q_norm_a, k_norm_a, sinks_a, q_norm_b, k_norm_b, w_out, loss_target, m_norm_gain, m_w_in, m_q_norm_a, m_k_norm_a, m_sinks_a, m_q_norm_b, m_k_norm_b, m_w_out, v_norm_gain, v_w_in, v_q_norm_a, v_k_norm_a, v_sinks_a, v_q_norm_b, v_k_norm_b, v_w_out):
    chip = 2 * lax.axis_index("x") + lax.axis_index("y")

    w_in_t, m_w_in_t, v_w_in_t = w_in[0].T, m_w_in[0].T, v_w_in[0].T

    w_in_all, w_out_all = _gather_weights([w_in_t.astype(BF16), w_out[0].astype(BF16)], "gather_weights")
    w_in_bf = w_in_all.reshape(IN_WIDTH, D_MODEL)
    w_out_bf = w_out_all.reshape(D_MODEL, D_MODEL)

    loss_part, gx, h_t, wins, gw_out, (dgain, dgqa, dgka, dsink, dgqb, dgkb) = _local_step(
        x[0], loss_target[0], norm_gain, w_in_bf, q_norm_a, k_norm_a, sinks_a, q_norm_b, k_norm_b, w_out_bf)

    small = jnp.concatenate([
        dgain, jnp.concatenate([dgqa, dgqb], axis=1),
        jnp.concatenate([dgkb, dgka, dsink, jnp.zeros((1, D_MODEL - WIDTH - 2 * A_KV_WIDTH), F32)], axis=1),
        jnp.pad(loss_part, ((0, 0), (0, D_MODEL - LANES))),
        jnp.zeros((PACK_ROWS - 4, D_MODEL), F32)], axis=0)
    order = (chip ^ jnp.array(RELATIONS, jnp.int32)).astype(jnp.int32)
    win_sum, wout_sum, small_recv = _grad_reduce(order, h_t, wins, gw_out, small)
    shift = jnp.array(WIN_SHIFT, jnp.int32)[chip]
    g_w_in_t = lax.dynamic_slice_in_dim(win_sum.transpose(1, 0, 2).reshape(WIN, D_MODEL), shift, IN_COLS, axis=0)
    g_w_out = wout_sum.reshape(OUT_ROWS, D_MODEL)

    d_w_in, nm_w_in, nv_w_in = (a.T for a in _adamw(w_in_t, g_w_in_t, m_w_in_t, v_w_in_t, "adamw_w_in"))
    g_w_in = g_w_in_t.T
    d_w_out, nm_w_out, nv_w_out = _adamw(w_out[0], g_w_out, m_w_out[0], v_w_out[0], "adamw_w_out")
    g_s, d_s, nm_s, nv_s, loss_row = _small_adamw(
        small_recv,
        _pack_small(norm_gain, q_norm_a, k_norm_a, q_norm_b, k_norm_b, sinks_a),
        _pack_small(m_norm_gain, m_q_norm_a, m_k_norm_a, m_q_norm_b, m_k_norm_b, m_sinks_a),
        _pack_small(v_norm_gain, v_q_norm_a, v_k_norm_a, v_q_norm_b, v_k_norm_b, v_sinks_a))
    loss = loss_row[0, 0]

    def leaves(small_packed, big_in, big_out):
        gain, qa, ka, qb, kb, sk = _unpack_small(small_packed)
        return (gain, big_in[None], qa, ka, sk, qb, kb, big_out[None])

    return ((loss, gx[None]) + leaves(g_s, g_w_in, g_w_out) + leaves(d_s, d_w_in, d_w_out)
            + leaves(nm_s, nm_w_in, nm_w_out) + leaves(nv_s, nv_w_in, nv_w_out))
```

```python
import jax
import jax.numpy as jnp
from jax import lax
from jax.experimental import pallas as pl
from jax.experimental.pallas import tpu as pltpu

F32 = jnp.float32
BF16 = jnp.bfloat16

D_MODEL = 1024
HEAD_DIM = 64
HEADS = 8
WIDTH = HEADS * HEAD_DIM
A_KV_WIDTH = 2 * HEAD_DIM
BLOCK = 128
LANES = 128
FOLD = 16
A_MAX_DIST = 127
B_MAX_DIST = 128
ROPE_THETA = 10000.0
EPS = 1e-6
NEG = -1e30
SCALE = HEAD_DIM ** -0.5

IN_WIDTH = 3328
C_QA, C_KA, C_VA, C_GA, C_QB, C_KB, C_VB, C_GB, C_END = 0, 512, 640, 768, 1280, 1792, 2304, 2816, 3328

N_DEV = 8
N_CHIP = 4
MESH = pl.DeviceIdType.MESH
IN_COLS = IN_WIDTH // N_CHIP
WIN = 896
WIN_START = (0, 768, 1664, 2432)
WIN_SHIFT = (0, 64, 0, 64)
OUT_ROWS = D_MODEL // N_CHIP
RELATIONS = (3, 1, 2, 0)

ADAM_LR = 0.001
ADAM_B1 = 0.9
ADAM_B2 = 0.999
ADAM_EPS = 1e-08
ADAM_WD = 0.01
ADAM_STEP = 10

ROW_TILE = 256
FOLD_ROWS = ROW_TILE // FOLD
GRAD_ROWS = 1024
ACC_COLS = 256
VMEM_LIMIT = 56 * 1024 * 1024


def _dot(a, b):
    return jnp.dot(a, b, preferred_element_type=F32)


def _dot_nt(a, b):
    return lax.dot_general(a, b, (((1,), (1,)), ((), ())), preferred_element_type=F32)


def _dot_tn(a, b):
    return lax.dot_general(a, b, (((0,), (0,)), ((), ())), preferred_element_type=F32)


def _head_sum(z, bd):
    w = bd.shape[0]
    zb = z.astype(BF16)
    parts = [_dot(zb[:, a:a + w], bd) for a in range(0, z.shape[1], w)]
    return parts[0] if len(parts) == 1 else jnp.concatenate(parts, axis=1)


def _swap_halves(t):
    w = t.shape[1]
    lane = lax.broadcasted_iota(jnp.int32, t.shape, 1)
    return jnp.where(lane % HEAD_DIM < HEAD_DIM // 2, pltpu.roll(t, w - 32, 1), pltpu.roll(t, 32, 1))


def _qknorm_rope(t, g, cos, sin_s, bd):
    r = lax.rsqrt(_head_sum(t * t, bd) * (1.0 / HEAD_DIM) + EPS)
    n = (t * r) * g
    return n * cos + _swap_halves(n) * sin_s


def _qknorm_rope_bwd(dout, t, g, cos, sin_s, bd):
    dn = dout * cos + _swap_halves(dout * sin_s)
    r = lax.rsqrt(_head_sum(t * t, bd) * (1.0 / HEAD_DIM) + EPS)
    tr = t * r
    u = dn * g
    dt = r * (u - tr * (_head_sum(u * tr, bd) * (1.0 / HEAD_DIM)))
    return dt, dn * tr


def _sigmoid(g):
    return 1.0 / (1.0 + jnp.exp(-g))


def _expand_heads(st):
    t = st.shape[0]
    lane = lax.broadcasted_iota(jnp.int32, (t, LANES), 1)
    chunks = []
    for c in range(WIDTH // LANES):
        chunks.append(jnp.where(lane < HEAD_DIM, st[:, 2 * c:2 * c + 1], st[:, 2 * c + 1:2 * c + 2]))
    return jnp.concatenate(chunks, axis=1)


def _reduce_heads(z):
    t = z.shape[0]
    lane = lax.broadcasted_iota(jnp.int32, (t, LANES), 1)
    out = jnp.zeros((t, LANES), F32)
    for c in range(WIDTH // LANES):
        zc = z[:, c * LANES:(c + 1) * LANES]
        for ph in range(2):
            s = jnp.sum(jnp.where((lane // HEAD_DIM) == ph, zc, 0.0), axis=-1, keepdims=True)
            out = jnp.where(lane == 2 * c + ph, s, out)
    return out


def _fold_scratch(w):
    return pltpu.VMEM((w // LANES, ROW_TILE, LANES), F32)


def _store_folded(out_ref, val, scr):
    n = val.shape[1] // LANES
    for c in range(n):
        scr[c] = val[:, c * LANES:(c + 1) * LANES]
    for r in range(FOLD):
        piece = [scr[c, pl.ds(r, FOLD_ROWS, stride=FOLD), :] for c in range(n)]
        out_ref[r] = (piece[0] if n == 1 else jnp.concatenate(piece, axis=1)).astype(out_ref.dtype)


def _load_folded(in_ref, scr):
    n = in_ref.shape[2] // LANES
    for r in range(FOLD):
        blk = in_ref[r].astype(F32)
        for c in range(n):
            scr[c, pl.ds(r, FOLD_ROWS, stride=FOLD), :] = blk[:, c * LANES:(c + 1) * LANES]
    return scr[0] if n == 1 else jnp.concatenate([scr[c] for c in range(n)], axis=1)


def _rows(w, tm=ROW_TILE):
    return pl.BlockSpec((tm, w), lambda i: (i, 0))


def _folded_rows(w):
    return pl.BlockSpec((FOLD, FOLD_ROWS, w), lambda i: (0, i, 0))


def _whole(shape):
    return pl.BlockSpec(shape, lambda i: (0,) * len(shape))


def _inproj(x2, gain, w_bf, cos, sin_s, gqa, gka, gqb, gkb, bd256, bd128):
    s = x2.shape[0]
    tm = ROW_TILE

    def body(x_ref, gain_ref, w_hbm, cos_ref, sin_ref, gqa_ref, gka_ref, gqb_ref, gkb_ref, bd256_ref, bd128_ref,
             qa_ref, ka_ref, va_ref, qb_ref, kb_ref, vb_ref, qbf_ref, kbf_ref, vbf_ref,
             qa_raw_ref, ka_raw_ref, ga_ref, qb_raw_ref, kb_raw_ref, gb_ref, w_vmem, scr):
        @pl.when(pl.program_id(0) == 0)
        def _():
            pltpu.sync_copy(w_hbm, w_vmem)

        xt = x_ref[...]
        r = lax.rsqrt(jnp.mean(xt * xt, axis=-1, keepdims=True) + EPS)
        h = ((xt * r) * gain_ref[...]).astype(BF16)
        cos1 = cos_ref[...]
        sin1 = sin_ref[...]
        cos4 = jnp.tile(cos1, (1, 4))
        sin4 = jnp.tile(sin1, (1, 4))

        def seg(a, b):
            return _dot_nt(h, w_vmem[a:b, :])

        t = seg(C_QA, C_KA)
        qa_raw_ref[...] = t
        qa_ref[...] = (_qknorm_rope(t, gqa_ref[...], cos4, sin4, bd256_ref[...]) * SCALE).astype(BF16)
        t = seg(C_KA, C_VA)
        ka_raw_ref[...] = t
        ka_ref[...] = _qknorm_rope(t, gka_ref[...], cos1, sin1, bd128_ref[...]).astype(BF16)
        va_ref[...] = seg(C_VA, C_GA).astype(BF16)
        ga_ref[...] = seg(C_GA, C_QB)
        t = seg(C_QB, C_KB)
        qb_raw_ref[...] = t
        t = _qknorm_rope(t, gqb_ref[...], cos4, sin4, bd256_ref[...]) * SCALE
        qb_ref[...] = t.astype(BF16)
        _store_folded(qbf_ref, t, scr)
        t = seg(C_KB, C_VB)
        kb_raw_ref[...] = t
        t = _qknorm_rope(t, gkb_ref[...], cos4, sin4, bd256_ref[...])
        kb_ref[...] = t.astype(BF16)
        _store_folded(kbf_ref, t, scr)
        t = seg(C_VB, C_GB)
        vb_ref[...] = t.astype(BF16)
        _store_folded(vbf_ref, t, scr)
        gb_ref[...] = seg(C_GB, C_END)

    sds = jax.ShapeDtypeStruct
    folded = sds((FOLD, s // FOLD, WIDTH), BF16)
    out_shape = (sds((s, WIDTH), BF16), sds((s, A_KV_WIDTH), BF16), sds((s, A_KV_WIDTH), BF16),
                 sds((s, WIDTH), BF16), sds((s, WIDTH), BF16), sds((s, WIDTH), BF16), folded, folded, folded,
                 sds((s, WIDTH), F32), sds((s, A_KV_WIDTH), F32), sds((s, WIDTH), F32),
                 sds((s, WIDTH), F32), sds((s, WIDTH), F32), sds((s, WIDTH), F32))
    out_specs = (_rows(WIDTH), _rows(A_KV_WIDTH), _rows(A_KV_WIDTH), _rows(WIDTH), _rows(WIDTH), _rows(WIDTH),
                 _folded_rows(WIDTH), _folded_rows(WIDTH), _folded_rows(WIDTH),
                 _rows(WIDTH), _rows(A_KV_WIDTH), _rows(WIDTH), _rows(WIDTH), _rows(WIDTH), _rows(WIDTH))
    return pl.pallas_call(
        body, name="inproj_fwd", grid=(s // tm,),
        in_specs=[_rows(D_MODEL), _whole(gain.shape), pl.BlockSpec(memory_space=pl.ANY), _rows(LANES), _rows(LANES),
                  _whole(gqa.shape), _whole(gka.shape), _whole(gqb.shape), _whole(gkb.shape), _whole(bd256.shape),
                  _whole(bd128.shape)],
        out_specs=out_specs, out_shape=out_shape,
        scratch_shapes=[pltpu.VMEM((IN_WIDTH, D_MODEL), BF16), _fold_scratch(WIDTH)],
        compiler_params=pltpu.CompilerParams(dimension_semantics=("arbitrary",), vmem_limit_bytes=VMEM_LIMIT),
    )(x2, gain, w_bf, cos, sin_s, gqa, gka, gqb, gkb, bd256, bd128)


def _seq_pos(idx, dil):
    if dil == 4:
        return 4 * (idx % 32) + idx // 32
    return idx


SOFTMAX_ROWS = 64


def _upper_mask(dil, r0=0, rows=2 * BLOCK):
    qi = (lax.broadcasted_iota(jnp.int32, (rows, BLOCK), 0) + r0) % BLOCK
    kj = lax.broadcasted_iota(jnp.int32, (rows, BLOCK), 1)
    return _seq_pos(kj, dil) > _seq_pos(qi, dil)


def _eye_mask(r0=0, rows=2 * BLOCK):
    qi = (lax.broadcasted_iota(jnp.int32, (rows, BLOCK), 0) + r0) % BLOCK
    kj = lax.broadcasted_iota(jnp.int32, (rows, BLOCK), 1)
    return qi == kj


def _stack_heads(a2, c, gqa):
    lane = lax.broadcasted_iota(jnp.int32, (1, LANES), 1) // HEAD_DIM
    zero = jnp.zeros_like(a2)
    if gqa:
        keep = lane == (c // 2)
        return jnp.concatenate([jnp.where(keep, a2, zero), jnp.where(keep, _swap_heads(a2), zero)], axis=0)
    return jnp.concatenate([jnp.where(lane == 0, a2, zero), jnp.where(lane == 1, a2, zero)], axis=0)


def _unstack_heads(a, c, gqa):
    lane = lax.broadcasted_iota(jnp.int32, (1, LANES), 1) // HEAD_DIM
    if gqa:
        return jnp.where(lane == (c // 2), a[:BLOCK], _swap_heads(a[BLOCK:]))
    return jnp.where(lane == 0, a[:BLOCK], a[BLOCK:])


def _stacked_head_ids(c, gqa):
    if gqa:
        return 2 * c + c // 2, 2 * c + 1 - c // 2
    return 2 * c, 2 * c + 1


def _per_head_rows(blk, heads):
    return jnp.concatenate([blk[:, heads[0]:heads[0] + 1], blk[:, heads[1]:heads[1] + 1]], axis=0)


def _attn_view(a, dil):
    if dil == 1:
        return a[None]
    if dil == 4:
        return a.reshape(4, 4, a.shape[1], a.shape[2])
    return a


def _attn_unview(a, dil):
    if dil == 1:
        return a[0]
    if dil == 4:
        return a.reshape(FOLD, a.shape[2], a.shape[3])
    return a


def _attn_specs(dil, nb):
    if dil == 4:
        def spec(fn):
            return lambda w: pl.BlockSpec((4, None, BLOCK // 4, w), lambda r, i: (0, r, fn(i), 0))
    else:
        def spec(fn):
            return lambda w: pl.BlockSpec((None, BLOCK, w), lambda r, i: (r, fn(i), 0))
    return spec


def _blk_load(ref, sl, dil):
    if dil == 4:
        return ref[:, :, sl].reshape(BLOCK, sl.stop - sl.start)
    return ref[:, sl]


def _blk_store(ref, sl, val, dil):
    if dil == 4:
        ref[:, :, sl] = val.reshape(4, BLOCK // 4, sl.stop - sl.start)
    else:
        ref[:, sl] = val


def _swap_heads(a):
    return pltpu.roll(a.astype(F32), HEAD_DIM, 1).astype(a.dtype)


def _attn_fwd(q, k, v, sinks, *, dil, max_dist, name):
    q, k, v = _attn_view(q, dil), _attn_view(k, dil), _attn_view(v, dil)
    kw = k.shape[-1]
    gqa = kw == A_KV_WIDTH
    n_seq = dil
    nb = (q.shape[-2] * (4 if dil == 4 else 1)) // BLOCK
    with_sinks = sinks is not None
    all_lanes = slice(0, LANES)
    assert max_dist in (BLOCK - 1, BLOCK)
    diag = max_dist == BLOCK

    def body(*refs):
        if with_sinks:
            q_ref, kp_ref, kc_ref, vp_ref, vc_ref, sink_ref, o_ref, m_ref, l_ref = refs
        else:
            q_ref, kp_ref, kc_ref, vp_ref, vc_ref, o_ref, m_ref, l_ref = refs

        def block(has_prev):
            lane = lax.broadcasted_iota(jnp.int32, (1, LANES), 1)
            with_diag = diag and has_prev
            upper, eye = _upper_mask(dil), _eye_mask()
            first_rows = lax.broadcasted_iota(jnp.int32, (2 * BLOCK, 1), 0) < BLOCK
            m_blk = jnp.zeros((BLOCK, LANES), F32)
            l_blk = jnp.ones((BLOCK, LANES), F32)
            chunks = range(WIDTH // LANES)
            scores, values = [], []
            for c in chunks:
                sl = slice(c * LANES, (c + 1) * LANES)
                ksl = slice(0, LANES) if gqa else sl
                kcur, vcur = _blk_load(kc_ref, ksl, dil), _blk_load(vc_ref, ksl, dil)
                qs = _stack_heads(_blk_load(q_ref, sl, dil), c, gqa)
                if has_prev:
                    kcur = jnp.concatenate([_blk_load(kp_ref, ksl, dil), kcur], axis=0)
                    vcur = jnp.concatenate([_blk_load(vp_ref, ksl, dil), vcur], axis=0)
                scores.append(_dot_nt(qs, kcur))
                values.append(vcur)
            probs = []
            for c in chunks:
                heads = _stacked_head_ids(c, gqa)
                s = scores[c]
                if has_prev:
                    s_p = s[:, :BLOCK]
                    sc = jnp.where(upper, s_p, s[:, BLOCK:])
                else:
                    sc = jnp.where(upper, NEG, s)
                if with_diag:
                    sd = jnp.where(eye, s_p, NEG)
                    m = jnp.max(jnp.maximum(sc, sd), axis=-1, keepdims=True)
                else:
                    m = jnp.max(sc, axis=-1, keepdims=True)
                if with_sinks:
                    sk = jnp.where(first_rows, sink_ref[0, heads[0]], sink_ref[0, heads[1]])
                    m = jnp.maximum(m, sk)
                p = jnp.exp(sc - m)
                zero = jnp.zeros_like(p)
                if with_diag:
                    pd = jnp.exp(sd - m)
                    l = jnp.sum(p + pd, axis=-1, keepdims=True)
                else:
                    pd = zero
                    l = jnp.sum(p, axis=-1, keepdims=True)
                if with_sinks:
                    l = l + jnp.exp(sk - m)
                pf = jnp.where(upper, zero, p)
                if has_prev:
                    pf = jnp.concatenate([jnp.where(upper, p, pd), pf], axis=1)
                probs.append(pf.astype(BF16))
                for n, h in enumerate(heads):
                    rows = slice(n * BLOCK, (n + 1) * BLOCK)
                    m_blk = jnp.where(lane == h, m[rows], m_blk)
                    l_blk = jnp.where(lane == h, l[rows], l_blk)
            for c in chunks:
                sl = slice(c * LANES, (c + 1) * LANES)
                _blk_store(o_ref, sl, _unstack_heads(_dot(probs[c], values[c]), c, gqa), dil)
            _blk_store(m_ref, all_lanes, m_blk, dil)
            _blk_store(l_ref, all_lanes, l_blk, dil)

        @pl.when(pl.program_id(1) == 0)
        def _():
            block(False)

        @pl.when(pl.program_id(1) > 0)
        def _():
            block(True)

    spec = _attn_specs(dil, nb)
    cur = spec(lambda i: i)
    prev = spec(lambda i: jnp.maximum(i - 1, 0))
    in_specs = [cur(WIDTH), prev(kw), cur(kw), prev(kw), cur(kw)]
    args = [q, k, k, v, v]
    if with_sinks:
        in_specs.append(pl.BlockSpec(memory_space=pltpu.SMEM))
        args.append(sinks)
    stats = jax.ShapeDtypeStruct(q.shape[:-1] + (LANES,), F32)
    o, m, l = pl.pallas_call(
        body, name=name, grid=(n_seq, nb), in_specs=in_specs,
        out_specs=(cur(WIDTH), cur(LANES), cur(LANES)),
        out_shape=(jax.ShapeDtypeStruct(q.shape, F32), stats, stats),
        compiler_params=pltpu.CompilerParams(dimension_semantics=("arbitrary", "arbitrary")),
    )(*args)
    return _attn_unview(o, dil), _attn_unview(m, dil), _attn_unview(l, dil)


def _attn_bwd(q, k, v, do, lse, delta, *, dil, max_dist, name):
    q, k, v, do, lse, delta = (_attn_view(a, dil) for a in (q, k, v, do, lse, delta))
    kw = k.shape[-1]
    gqa = kw == A_KV_WIDTH
    n_seq = dil
    nb = (q.shape[-2] * (4 if dil == 4 else 1)) // BLOCK
    n_kc = kw // LANES
    all_lanes = slice(0, LANES)
    assert max_dist in (BLOCK - 1, BLOCK)
    diag = max_dist == BLOCK

    def body(q_ref, kp_ref, kc_ref, vp_ref, vc_ref, do_ref, lse_ref, dl_ref, dq_ref, dk_ref, dv_ref, ck_ref, cv_ref):
        i = pl.program_id(1)

        def block(has_prev):
            upper, eye = _upper_mask(dil), _eye_mask()
            lse_blk = _blk_load(lse_ref, all_lanes, dil)
            dl_blk = _blk_load(dl_ref, all_lanes, dil)
            dk_acc = [None] * n_kc
            dv_acc = [None] * n_kc
            chunks = range(WIDTH // LANES)
            operands, products = [], []
            for c in chunks:
                sl = slice(c * LANES, (c + 1) * LANES)
                kc = 0 if gqa else c
                ksl = slice(kc * LANES, (kc + 1) * LANES)
                k2, v2 = _blk_load(kc_ref, ksl, dil), _blk_load(vc_ref, ksl, dil)
                if has_prev:
                    k2 = jnp.concatenate([_blk_load(kp_ref, ksl, dil), k2], axis=0)
                    v2 = jnp.concatenate([_blk_load(vp_ref, ksl, dil), v2], axis=0)
                qs = _stack_heads(_blk_load(q_ref, sl, dil), c, gqa)
                dos = _stack_heads(_blk_load(do_ref, sl, dil), c, gqa)
                operands.append((qs, dos, k2))
                products.append((_dot_nt(qs, k2), _dot_nt(dos, v2)))
            weights = []
            for c in chunks:
                heads = _stacked_head_ids(c, gqa)
                lse2 = _per_head_rows(lse_blk, heads)
                dl2 = _per_head_rows(dl_blk, heads)
                s, dp = products[c]
                if has_prev:
                    s_p, dp_p = s[:, :BLOCK], dp[:, :BLOCK]
                    sc = jnp.where(upper, s_p, s[:, BLOCK:])
                    dpc = jnp.where(upper, dp_p, dp[:, BLOCK:])
                else:
                    sc = jnp.where(upper, NEG, s)
                    dpc = dp
                p = jnp.exp(sc - lse2)
                ds = p * (dpc - dl2)
                zero = jnp.zeros_like(p)
                pf = jnp.where(upper, zero, p)
                dsf = jnp.where(upper, zero, ds)
                if has_prev:
                    if diag:
                        pd = jnp.exp(jnp.where(eye, s_p, NEG) - lse2)
                        dsd = pd * (dp_p - dl2)
                    else:
                        pd = dsd = zero
                    pf = jnp.concatenate([jnp.where(upper, p, pd), pf], axis=1)
                    dsf = jnp.concatenate([jnp.where(upper, ds, dsd), dsf], axis=1)
                weights.append((pf.astype(BF16), dsf.astype(BF16)))
            for c in chunks:
                sl = slice(c * LANES, (c + 1) * LANES)
                kc = 0 if gqa else c
                qs, dos, k2 = operands[c]
                pf, dsf = weights[c]
                _blk_store(dq_ref, sl, _unstack_heads(_dot(dsf, k2), c, gqa) * SCALE, dil)
                dk2 = _dot_tn(dsf, qs)
                dv2 = _dot_tn(pf, dos)
                dk_acc[kc] = dk2 if dk_acc[kc] is None else dk_acc[kc] + dk2
                dv_acc[kc] = dv2 if dv_acc[kc] is None else dv_acc[kc] + dv2
            for kc in range(n_kc):
                sl = slice(kc * LANES, (kc + 1) * LANES)
                if has_prev:
                    _blk_store(dk_ref, sl, ck_ref[:, sl] + dk_acc[kc][:BLOCK], dil)
                    _blk_store(dv_ref, sl, cv_ref[:, sl] + dv_acc[kc][:BLOCK], dil)
                    ck_ref[:, sl] = dk_acc[kc][BLOCK:]
                    cv_ref[:, sl] = dv_acc[kc][BLOCK:]
                else:
                    ck_ref[:, sl] = dk_acc[kc]
                    cv_ref[:, sl] = dv_acc[kc]

        @pl.when(i == 0)
        def _():
            block(False)

        @pl.when((i > 0) & (i < nb))
        def _():
            block(True)

        @pl.when(i == nb)
        def _():
            for kc in range(n_kc):
                sl = slice(kc * LANES, (kc + 1) * LANES)
                _blk_store(dk_ref, sl, ck_ref[:, sl], dil)
                _blk_store(dv_ref, sl, cv_ref[:, sl], dil)

    spec = _attn_specs(dil, nb)
    cur = spec(lambda i: jnp.minimum(i, nb - 1))
    prev = spec(lambda i: jnp.clip(i - 1, 0, nb - 1))
    lag = spec(lambda i: jnp.maximum(i - 1, 0))
    sds = jax.ShapeDtypeStruct
    dq, dk, dv = pl.pallas_call(
        body, name=name, grid=(n_seq, nb + 1),
        in_specs=[cur(WIDTH), prev(kw), cur(kw), prev(kw), cur(kw), cur(WIDTH), cur(LANES), cur(LANES)],
        out_specs=(cur(WIDTH), lag(kw), lag(kw)),
        out_shape=(sds(q.shape, F32), sds(k.shape, F32), sds(k.shape, F32)),
        scratch_shapes=[pltpu.VMEM((BLOCK, kw), F32), pltpu.VMEM((BLOCK, kw), F32)],
        compiler_params=pltpu.CompilerParams(dimension_semantics=("arbitrary", "arbitrary")),
    )(q, k, k, v, v, do, lse, delta)
    return _attn_unview(dq, dil), _attn_unview(dk, dil), _attn_unview(dv, dil)


def _outproj(att_a, att_b1, att_b4, att_b16, g_a, g_b, x2, tgt2, w_out_bf, sink_row):
    s = x2.shape[0]
    tm = ROW_TILE

    def body(oa_ref, ma_ref, la_ref, ob1_ref, m1_ref, l1_ref, ob4_ref, m4_ref, l4_ref, ob16_ref, m16_ref, l16_ref,
             ga_ref, gb_ref, x_ref, t_ref, w_ref, sink_ref,
             dy_ref, doa_ref, dob_ref, dobf_ref, dga_ref, dgb_ref, lsea_ref, lseb_ref, lsebf_ref, dla_ref, dlb_ref,
             dlbf_ref, gw_ref, loss_ref, dsink_ref, scr, scr_st):
        i = pl.program_id(0)

        @pl.when(i == 0)
        def _():
            gw_ref[...] = jnp.zeros_like(gw_ref)
            loss_ref[...] = jnp.zeros_like(loss_ref)
            dsink_ref[...] = jnp.zeros_like(dsink_ref)

        ms = [m1_ref[...], _load_folded(m4_ref, scr_st), _load_folded(m16_ref, scr_st)]
        ls = [l1_ref[...], _load_folded(l4_ref, scr_st), _load_folded(l16_ref, scr_st)]
        mx = jnp.maximum(jnp.maximum(ms[0], ms[1]), ms[2])
        scale = [jnp.exp(mp - mx) for mp in ms]
        den = (ls[0] * scale[0] + ls[1] * scale[1]) + ls[2] * scale[2]
        lane = lax.broadcasted_iota(jnp.int32, (tm, LANES), 1)
        lse_b = jnp.where(lane < HEADS, mx + jnp.log(den), 0.0)
        lseb_ref[...] = lse_b
        _store_folded(lsebf_ref, lse_b, scr_st)
        inv_den = 1.0 / den
        o_b = _expand_heads(scale[0] * inv_den) * ob1_ref[...]
        o_b = o_b + _expand_heads(scale[1] * inv_den) * _load_folded(ob4_ref, scr)
        o_b = o_b + _expand_heads(scale[2] * inv_den) * _load_folded(ob16_ref, scr)
        l_a = la_ref[...]
        lse_a = jnp.where(lane < HEADS, ma_ref[...] + jnp.log(l_a), 0.0)
        lsea_ref[...] = lse_a
        o_a = _expand_heads(1.0 / l_a) * oa_ref[...]
        g_a = ga_ref[...]
        g_b = gb_ref[...]
        sg_a = _sigmoid(g_a)
        sg_b = _sigmoid(g_b)
        silu_a = g_a * sg_a
        silu_b = g_b * sg_b
        mixed = jnp.concatenate([o_a * silu_a, o_b * silu_b], axis=1).astype(BF16)
        w = w_ref[...]
        y = x_ref[...] + _dot(mixed, w)
        diff = y - t_ref[...]
        loss_ref[...] += (0.5 / D_MODEL) * jnp.sum(diff * diff)
        dy = diff * (1.0 / D_MODEL)
        dy_ref[...] = dy
        dyb = dy.astype(BF16)
        gw_ref[...] += _dot_tn(mixed, dyb)
        dmixed = _dot_nt(dyb, w)
        dm_a = dmixed[:, :WIDTH]
        dm_b = dmixed[:, WIDTH:]
        do_a = dm_a * silu_a
        do_b = dm_b * silu_b
        doa_ref[...] = do_a.astype(BF16)
        dob_ref[...] = do_b.astype(BF16)
        _store_folded(dobf_ref, do_b, scr)
        dga_ref[...] = (dm_a * o_a * (sg_a * (1.0 + g_a * (1.0 - sg_a)))).astype(BF16)
        dgb_ref[...] = (dm_b * o_b * (sg_b * (1.0 + g_b * (1.0 - sg_b)))).astype(BF16)
        dl_a = _reduce_heads(do_a * o_a)
        dla_ref[...] = dl_a
        dl_b = _reduce_heads(do_b * o_b)
        dlb_ref[...] = dl_b
        _store_folded(dlbf_ref, dl_b, scr_st)
        dsink_ref[...] -= jnp.sum(jnp.exp(sink_ref[...] - lse_a) * dl_a, axis=0, keepdims=True)

    sds = jax.ShapeDtypeStruct
    ln = s // FOLD
    natural = [_rows(WIDTH), _rows(LANES), _rows(LANES)]
    folded = [_folded_rows(WIDTH), _folded_rows(LANES), _folded_rows(LANES)]
    return pl.pallas_call(
        body, name="outproj_fwd_bwd", grid=(s // tm,),
        in_specs=natural + natural + folded + folded
                 + [_rows(WIDTH), _rows(WIDTH), _rows(D_MODEL), _rows(D_MODEL), _whole((D_MODEL, D_MODEL)),
                    _whole((1, LANES))],
        out_specs=(_rows(D_MODEL), _rows(WIDTH), _rows(WIDTH), _folded_rows(WIDTH), _rows(WIDTH), _rows(WIDTH),
                   _rows(LANES), _rows(LANES), _folded_rows(LANES), _rows(LANES), _rows(LANES), _folded_rows(LANES),
                   _whole((D_MODEL, D_MODEL)), _whole((1, LANES)), _whole((1, LANES))),
        out_shape=(sds((s, D_MODEL), F32), sds((s, WIDTH), BF16), sds((s, WIDTH), BF16),
                   sds((FOLD, ln, WIDTH), BF16), sds((s, WIDTH), BF16), sds((s, WIDTH), BF16),
                   sds((s, LANES), F32), sds((s, LANES), F32), sds((FOLD, ln, LANES), F32), sds((s, LANES), F32),
                   sds((s, LANES), F32), sds((FOLD, ln, LANES), F32),
                   sds((D_MODEL, D_MODEL), F32), sds((1, LANES), F32), sds((1, LANES), F32)),
        scratch_shapes=[_fold_scratch(WIDTH), _fold_scratch(LANES)],
        compiler_params=pltpu.CompilerParams(dimension_semantics=("arbitrary",), vmem_limit_bytes=VMEM_LIMIT),
    )(*att_a, *att_b1, *att_b4, *att_b16, g_a, g_b, x2, tgt2, w_out_bf, sink_row)


def _inproj_bwd(x2, dy, gain, w_bf, cos, sin_s, gqa, gka, gqb, gkb, bd256, bd128,
                qa_raw, ka_raw, qb_raw, kb_raw, dq_a, dk_a, dv_a, dqkv_b1, dqkv_b4, dqkv_b16, dg_a, dg_b):
    s = x2.shape[0]
    tm = ROW_TILE

    def body(x_ref, dy_ref, gain_ref, w_hbm, cos_ref, sin_ref, gqa_ref, gka_ref, gqb_ref, gkb_ref, bd256_ref,
             bd128_ref, qa_raw_ref, ka_raw_ref, qb_raw_ref, kb_raw_ref, dqa_ref, dka_ref, dva_ref,
             dq1_ref, dk1_ref, dv1_ref, dq4_ref, dk4_ref, dv4_ref, dq16_ref, dk16_ref, dv16_ref, dga_ref, dgb_ref,
             gx_ref, ht_ref, win_ref,
             dgain_ref, dgqa_ref, dgka_ref, dgqb_ref, dgkb_ref, w_vmem, dproj_ref, scr):
        i = pl.program_id(0)

        @pl.when(i == 0)
        def _():
            pltpu.sync_copy(w_hbm, w_vmem)
            dgain_ref[...] = jnp.zeros_like(dgain_ref)
            dgqa_ref[...] = jnp.zeros_like(dgqa_ref)
            dgka_ref[...] = jnp.zeros_like(dgka_ref)
            dgqb_ref[...] = jnp.zeros_like(dgqb_ref)
            dgkb_ref[...] = jnp.zeros_like(dgkb_ref)

        cos1 = cos_ref[...]
        sin1 = sin_ref[...]
        cos4 = jnp.tile(cos1, (1, 4))
        sin4 = jnp.tile(sin1, (1, 4))

        dt, dg = _qknorm_rope_bwd(dqa_ref[...], qa_raw_ref[...], gqa_ref[...], cos4, sin4, bd256_ref[...])
        dproj_ref[:, C_QA:C_KA] = dt.astype(BF16)
        dgqa_ref[...] += jnp.sum(dg, axis=0, keepdims=True)
        dt, dg = _qknorm_rope_bwd(dka_ref[...], ka_raw_ref[...], gka_ref[...], cos1, sin1, bd128_ref[...])
        dproj_ref[:, C_KA:C_VA] = dt.astype(BF16)
        dgka_ref[...] += jnp.sum(dg, axis=0, keepdims=True)
        dproj_ref[:, C_VA:C_GA] = dva_ref[...].astype(BF16)
        dproj_ref[:, C_GA:C_QB] = dga_ref[...]
        dq = (dq1_ref[...] + _load_folded(dq4_ref, scr)) + _load_folded(dq16_ref, scr)
        dt, dg = _qknorm_rope_bwd(dq, qb_raw_ref[...], gqb_ref[...], cos4, sin4, bd256_ref[...])
        dproj_ref[:, C_QB:C_KB] = dt.astype(BF16)
        dgqb_ref[...] += jnp.sum(dg, axis=0, keepdims=True)
        dk = (dk1_ref[...] + _load_folded(dk4_ref, scr)) + _load_folded(dk16_ref, scr)
        dt, dg = _qknorm_rope_bwd(dk, kb_raw_ref[...], gkb_ref[...], cos4, sin4, bd256_ref[...])
        dproj_ref[:, C_KB:C_VB] = dt.astype(BF16)
        dgkb_ref[...] += jnp.sum(dg, axis=0, keepdims=True)
        dv = (dv1_ref[...] + _load_folded(dv4_ref, scr)) + _load_folded(dv16_ref, scr)
        dproj_ref[:, C_VB:C_GB] = dv.astype(BF16)
        dproj_ref[:, C_GB:C_END] = dgb_ref[...]
        for k, start in enumerate(WIN_START):
            win_ref[k] = dproj_ref[:, start:start + WIN]

        xt = x_ref[...]
        gain_row = gain_ref[...]
        r = lax.rsqrt(jnp.mean(xt * xt, axis=-1, keepdims=True) + EPS)
        xr = xt * r
        ht_ref[...] = (xr * gain_row).T.astype(BF16)
        dh = _dot(dproj_ref[...], w_vmem[...])
        dgain_ref[...] += jnp.sum(dh * xr, axis=0, keepdims=True)
        u = dh * gain_row
        gx_ref[...] = dy_ref[...] + r * (u - xr * jnp.mean(u * xr, axis=-1, keepdims=True))

    def acc_row(w):
        return pl.BlockSpec((1, w), lambda i: (0, 0))

    sds = jax.ShapeDtypeStruct
    any_spec = pl.BlockSpec(memory_space=pl.ANY)
    win_spec = pl.BlockSpec((N_CHIP, tm, WIN), lambda i: (0, i, 0))
    return pl.pallas_call(
        body, name="inproj_bwd", grid=(s // tm,),
        in_specs=[_rows(D_MODEL), _rows(D_MODEL), _whole(gain.shape), any_spec, _rows(LANES), _rows(LANES),
                  _whole(gqa.shape), _whole(gka.shape), _whole(gqb.shape), _whole(gkb.shape), _whole(bd256.shape),
                  _whole(bd128.shape),
                  _rows(WIDTH), _rows(A_KV_WIDTH), _rows(WIDTH), _rows(WIDTH),
                  _rows(WIDTH), _rows(A_KV_WIDTH), _rows(A_KV_WIDTH)]
                 + [_rows(WIDTH)] * 3 + [_folded_rows(WIDTH)] * 6 + [_rows(WIDTH), _rows(WIDTH)],
        out_specs=(_rows(D_MODEL), pl.BlockSpec((D_MODEL, tm), lambda i: (0, i)), win_spec, acc_row(D_MODEL), acc_row(WIDTH), acc_row(A_KV_WIDTH), acc_row(WIDTH), acc_row(WIDTH)),
        out_shape=(sds((s, D_MODEL), F32), sds((D_MODEL, s), BF16), sds((N_CHIP, s, WIN), BF16),
                   sds((1, D_MODEL), F32),
                   sds((1, WIDTH), F32), sds((1, A_KV_WIDTH), F32), sds((1, WIDTH), F32), sds((1, WIDTH), F32)),
        scratch_shapes=[pltpu.VMEM((IN_WIDTH, D_MODEL), BF16), pltpu.VMEM((tm, IN_WIDTH), BF16),
                        _fold_scratch(WIDTH)],
        compiler_params=pltpu.CompilerParams(dimension_semantics=("arbitrary",), vmem_limit_bytes=VMEM_LIMIT),
    )(x2, dy, gain, w_bf, cos, sin_s, gqa, gka, gqb, gkb, bd256, bd128, qa_raw, ka_raw, qb_raw, kb_raw,
      dq_a, dk_a, dv_a, *dqkv_b1, *dqkv_b4, *dqkv_b16, dg_a, dg_b)


def _rope_tables(s):
    half = HEAD_DIM // 2
    inv = jnp.tile(ROPE_THETA ** (-jnp.arange(half, dtype=F32) / half), 4)
    sign = jnp.tile(jnp.concatenate([-jnp.ones((half,), F32), jnp.ones((half,), F32)]), 2)
    ang = jnp.arange(s).astype(F32)[:, None] * inv[None, :]
    return jnp.cos(ang), jnp.sin(ang) * sign[None, :]


def _block_diag_ones(w):
    idx = jnp.arange(w) // HEAD_DIM
    return (idx[:, None] == idx[None, :]).astype(BF16)


def _local_step(x2, tgt2, norm_gain, w_in_bf, q_norm_a, k_norm_a, sinks_a, q_norm_b, k_norm_b, w_out_bf):
    s = x2.shape[0]
    cos, sin_s = _rope_tables(s)
    bd256, bd128 = _block_diag_ones(2 * LANES), _block_diag_ones(A_KV_WIDTH)
    gqa = jnp.tile(q_norm_a, (1, HEADS))
    gka = jnp.tile(k_norm_a, (1, 2))
    gqb = jnp.tile(q_norm_b, (1, HEADS))
    gkb = jnp.tile(k_norm_b, (1, HEADS))
    sink_row = jnp.pad(sinks_a, ((0, 0), (0, LANES - HEADS)))

    (qa, ka, va, qb, kb, vb, qbf, kbf, vbf, qa_raw, ka_raw, g_a, qb_raw, kb_raw, g_b) = _inproj(
        x2, norm_gain, w_in_bf, cos, sin_s, gqa, gka, gqb, gkb, bd256, bd128)

    att_a = _attn_fwd(qa, ka, va, sinks_a, dil=1, max_dist=A_MAX_DIST, name="attn_a_fwd")
    att_b1 = _attn_fwd(qb, kb, vb, None, dil=1, max_dist=B_MAX_DIST, name="attn_b1_fwd")
    att_b4 = _attn_fwd(qbf, kbf, vbf, None, dil=4, max_dist=B_MAX_DIST, name="attn_b4_fwd")
    att_b16 = _attn_fwd(qbf, kbf, vbf, None, dil=16, max_dist=B_MAX_DIST, name="attn_b16_fwd")

    (dy, do_a, do_b, do_bf, dg_a, dg_b, lse_a, lse_b, lse_bf, dl_a, dl_b, dl_bf, gw_out, loss_part,
     dsink) = _outproj(att_a, att_b1, att_b4, att_b16, g_a, g_b, x2, tgt2, w_out_bf, sink_row)

    dq_a, dk_a, dv_a = _attn_bwd(qa, ka, va, do_a, lse_a, dl_a, dil=1, max_dist=A_MAX_DIST, name="attn_a_bwd")
    d_b1 = _attn_bwd(qb, kb, vb, do_b, lse_b, dl_b, dil=1, max_dist=B_MAX_DIST, name="attn_b1_bwd")
    d_b4 = _attn_bwd(qbf, kbf, vbf, do_bf, lse_bf, dl_bf, dil=4, max_dist=B_MAX_DIST, name="attn_b4_bwd")
    d_b16 = _attn_bwd(qbf, kbf, vbf, do_bf, lse_bf, dl_bf, dil=16, max_dist=B_MAX_DIST, name="attn_b16_bwd")

    gx, h_t, wins, dgain, dgqa, dgka, dgqb, dgkb = _inproj_bwd(
        x2, dy, norm_gain, w_in_bf, cos, sin_s, gqa, gka, gqb, gkb, bd256, bd128,
        qa_raw, ka_raw, qb_raw, kb_raw, dq_a, dk_a, dv_a, d_b1, d_b4, d_b16, dg_a, dg_b)
    return loss_part, gx, h_t, wins, gw_out, (dgain, dgqa, dgka, dsink, dgqb, dgkb)


def _position():
    return lax.axis_index("x"), lax.axis_index("y"), lax.axis_index("c")


GATHER_CHUNKS = 2


def _gather_weights(blocks, name):
    n = len(blocks)
    ch = GATHER_CHUNKS
    n_sems = n * (N_CHIP - 1) * ch

    def body(*refs):
        src_refs, dst_refs = refs[:n], refs[n:2 * n]
        ici_send, ici_recv, d2d_send, d2d_recv, local_sems = refs[2 * n:]
        x, y, c = _position()
        b = 2 * x + y
        copies = []
        for k in range(n):
            local = pltpu.make_async_copy(src_refs[k], dst_refs[k].at[b], local_sems.at[k])
            local.start()
            copies.append(local)

        def rows(k, core, j):
            half = blocks[k].shape[0] // 2
            return pl.ds(core * half + j * (half // ch), half // ch)

        plan = []
        for d in range(1, N_CHIP):
            px, py = x ^ (d >> 1), y ^ (d & 1)
            for j in range(ch):
                for k in range(n):
                    plan.append((px, py, 2 * px + py, k, j, ((d - 1) * ch + j) * n + k))
        sends = []
        for px, py, pb, k, j, sem in plan:
            send = pltpu.make_async_remote_copy(
                src_ref=src_refs[k].at[rows(k, c, j)], dst_ref=dst_refs[k].at[b, rows(k, c, j)],
                send_sem=ici_send.at[sem], recv_sem=ici_recv.at[sem], device_id=(px, py, c), device_id_type=MESH)
            send.start()
            sends.append(send)
        for px, py, pb, k, j, sem in plan:
            landed = dst_refs[k].at[pb, rows(k, c, j)]
            pltpu.make_async_remote_copy(
                src_ref=landed, dst_ref=landed, send_sem=ici_send.at[sem], recv_sem=ici_recv.at[sem],
                device_id=(px, py, c), device_id_type=MESH).wait_recv()
            forward = pltpu.make_async_remote_copy(
                src_ref=landed, dst_ref=landed, send_sem=d2d_send.at[sem], recv_sem=d2d_recv.at[sem],
                device_id=(x, y, 1 - c), device_id_type=MESH)
            forward.start()
            sends.append(forward)
        for px, py, pb, k, j, sem in plan:
            passed = dst_refs[k].at[pb, rows(k, 1 - c, j)]
            pltpu.make_async_remote_copy(
                src_ref=passed, dst_ref=passed, send_sem=d2d_send.at[sem], recv_sem=d2d_recv.at[sem],
                device_id=(x, y, 1 - c), device_id_type=MESH).wait_recv()
        for send in sends:
            send.wait_send()
        for local in copies:
            local.wait()

    vmem_spec = pl.BlockSpec(memory_space=pltpu.VMEM)
    out_shape = tuple(jax.ShapeDtypeStruct((N_CHIP,) + a.shape, a.dtype) for a in blocks)
    return pl.pallas_call(
        body, name=name, in_specs=[vmem_spec] * n, out_specs=tuple([vmem_spec] * n), out_shape=out_shape,
        scratch_shapes=[pltpu.SemaphoreType.DMA((n_sems,)) for _ in range(4)] + [pltpu.SemaphoreType.DMA((n,))],
        compiler_params=pltpu.CompilerParams(vmem_limit_bytes=VMEM_LIMIT),
    )(*blocks)


def _grad_reduce(order, h_t, wins, gw_out, small):
    s = h_t.shape[1]
    tk = GRAD_ROWS
    n_i = s // tk
    half = D_MODEL // 2
    o_half = OUT_ROWS // 2
    n_rel = N_CHIP - 1

    def body(order_ref, ht_ref, win_ref, gwo_ref, small_ref,
             win_out, wout_out, small_out,
             acc, mine, s1, r1, s2, r2, so1, ro1, so2, ro2, pair_in, pair_o, small_land,
             s1_send, s1_recv, s2_send, s2_recv, o1_send, o1_recv, o2_send, o2_recv,
             pair_send, pair_recv, small_send, small_recv):
        j = pl.program_id(0)
        i = pl.program_id(1)
        x, y, c = _position()
        me = 4 * x + 2 * y + c
        sibling = (x, y, 1 - c)
        my_rows = pl.ds(pl.multiple_of(c * half, half), half)
        sib_rows = pl.ds(pl.multiple_of((1 - c) * half, half), half)

        def chip_of(rel):
            return x ^ (rel >> 1), y ^ (rel & 1)

        def level1(k):
            return pltpu.make_async_remote_copy(src_ref=s1.at[k], dst_ref=r1.at[k], send_sem=s1_send.at[k],
                                                recv_sem=s1_recv.at[k], device_id=sibling, device_id_type=MESH)

        def level2(k):
            px, py = chip_of(RELATIONS[k])
            return pltpu.make_async_remote_copy(src_ref=s2.at[k], dst_ref=r2.at[k], send_sem=s2_send.at[k],
                                                recv_sem=s2_recv.at[k], device_id=(px, py, c), device_id_type=MESH)

        def out_level1(bk):
            return pltpu.make_async_remote_copy(src_ref=so1.at[bk], dst_ref=ro1.at[bk], send_sem=o1_send.at[bk],
                                                recv_sem=o1_recv.at[bk], device_id=sibling, device_id_type=MESH)

        def out_level2(k):
            px, py = chip_of(RELATIONS[k])
            return pltpu.make_async_remote_copy(src_ref=so2.at[k], dst_ref=ro2.at[k], send_sem=o2_send.at[k],
                                                recv_sem=o2_recv.at[k], device_id=(px, py, c), device_id_type=MESH)

        def small_copy(d):
            px, py, pc = x ^ (d >> 2), y ^ ((d >> 1) & 1), c ^ (d & 1)
            return pltpu.make_async_remote_copy(src_ref=small_ref, dst_ref=small_land.at[me],
                                                send_sem=small_send.at[d], recv_sem=small_recv.at[d],
                                                device_id=(px, py, pc), device_id_type=MESH)

        def pair_copy(k, buf):
            return pltpu.make_async_remote_copy(src_ref=buf.at[0], dst_ref=buf.at[1], send_sem=pair_send.at[k],
                                                recv_sem=pair_recv.at[k], device_id=sibling, device_id_type=MESH)

        def out_rows(bk, core):
            return pl.ds(pl.multiple_of(bk * OUT_ROWS + core * o_half, o_half), o_half)

        @pl.when((j == 0) & (i == 0))
        def _():
            for d in range(1, N_DEV):
                small_copy(d).start()
            small_land[me] = small_ref[...]
            for bk in range(N_CHIP):
                so1[bk] = gwo_ref[out_rows(bk, 1 - c), :].astype(BF16)
                out_level1(bk).start()

        @pl.when((j == 0) & (i == 1))
        def _():
            b = 2 * x + y
            for bk in range(N_CHIP):
                out_level1(bk).wait_recv()
            for k in range(n_rel):
                px, py = chip_of(RELATIONS[k])
                bk = 2 * px + py
                so2[k] = (gwo_ref[out_rows(bk, c), :] + ro1[bk].astype(F32)).astype(BF16)
                out_level2(k).start()

        @pl.when(i == 0)
        def _():
            acc[...] = jnp.zeros_like(acc)

        for n0 in range(0, WIN, ACC_COLS):
            n1 = min(n0 + ACC_COLS, WIN)
            acc[:, n0:n1] += _dot(ht_ref[...], win_ref[:, n0:n1])

        for k in range(N_CHIP):
            @pl.when((j == k) & (i == n_i - 1))
            def _(k=k):
                s1[k] = acc[sib_rows, :].astype(BF16)
                level1(k).start()
                mine[...] = acc[my_rows, :]

            if k < n_rel:
                @pl.when((j == k + 1) & (i == 1))
                def _(k=k):
                    level1(k).wait_recv()
                    s2[k] = (mine[...] + r1[k].astype(F32)).astype(BF16)
                    level2(k).start()

        @pl.when((j == N_CHIP - 1) & (i == n_i - 1))
        def _():
            b = 2 * x + y
            level1(N_CHIP - 1).wait_recv()
            total = mine[...] + r1[N_CHIP - 1].astype(F32)
            for k in range(n_rel):
                level2(k).wait_recv()
                total = total + r2[k].astype(F32)
            total = total.T
            pair_in[0] = total
            pair_copy(0, pair_in).start()
            total_o = gwo_ref[out_rows(b, c), :] + ro1[b].astype(F32)
            for k in range(n_rel):
                out_level2(k).wait_recv()
                total_o = total_o + ro2[k].astype(F32)
            pair_o[0] = total_o
            pair_copy(1, pair_o).start()
            win_out[c] = total
            wout_out[c] = total_o
            for d in range(1, N_DEV):
                small_copy(d).wait_recv()
            small_out[...] = small_land[...]
            pair_copy(0, pair_in).wait_recv()
            win_out[1 - c] = pair_in[1]
            pair_copy(1, pair_o).wait_recv()
            wout_out[1 - c] = pair_o[1]
            for d in range(1, N_DEV):
                small_copy(d).wait_send()
            for k in range(N_CHIP):
                level1(k).wait_send()
                out_level1(k).wait_send()
            for k in range(n_rel):
                level2(k).wait_send()
                out_level2(k).wait_send()
            pair_copy(0, pair_in).wait_send()
            pair_copy(1, pair_o).wait_send()

    vmem = pl.BlockSpec(memory_space=pltpu.VMEM)
    dma = pltpu.SemaphoreType.DMA
    sds = jax.ShapeDtypeStruct
    grid_spec = pltpu.PrefetchScalarGridSpec(
        num_scalar_prefetch=1, grid=(N_CHIP, n_i),
        in_specs=[pl.BlockSpec((D_MODEL, tk), lambda j, i, order: (0, i)),
                  pl.BlockSpec((None, tk, WIN), lambda j, i, order: (order[j], i, 0)), vmem, vmem],
        out_specs=(vmem, vmem, vmem),
        scratch_shapes=[
            pltpu.VMEM((D_MODEL, WIN), F32), pltpu.VMEM((half, WIN), F32),
            pltpu.VMEM((N_CHIP, half, WIN), BF16), pltpu.VMEM((N_CHIP, half, WIN), BF16),
            pltpu.VMEM((n_rel, half, WIN), BF16), pltpu.VMEM((n_rel, half, WIN), BF16),
            pltpu.VMEM((N_CHIP, o_half, D_MODEL), BF16), pltpu.VMEM((N_CHIP, o_half, D_MODEL), BF16),
            pltpu.VMEM((n_rel, o_half, D_MODEL), BF16), pltpu.VMEM((n_rel, o_half, D_MODEL), BF16),
            pltpu.VMEM((2, WIN, half), F32), pltpu.VMEM((2, o_half, D_MODEL), F32),
            pltpu.VMEM((N_DEV, PACK_ROWS, D_MODEL), F32),
            dma((N_CHIP,)), dma((N_CHIP,)), dma((n_rel,)), dma((n_rel,)),
            dma((N_CHIP,)), dma((N_CHIP,)), dma((n_rel,)), dma((n_rel,)),
            dma((2,)), dma((2,)), dma((N_DEV,)), dma((N_DEV,))])
    return pl.pallas_call(
        body, name="grad_w_in_reduce", grid_spec=grid_spec,
        out_shape=(sds((2, WIN, half), F32), sds((2, o_half, D_MODEL), F32), sds((N_DEV, PACK_ROWS, D_MODEL), F32)),
        compiler_params=pltpu.CompilerParams(dimension_semantics=("arbitrary", "arbitrary"),
                                             vmem_limit_bytes=VMEM_LIMIT),
    )(order, h_t, wins, gw_out, small)


ADAM_STEPS = 4


def _adamw_math(w, g, m, v):
    m = ADAM_B1 * m + (1.0 - ADAM_B1) * g
    v = ADAM_B2 * v + (1.0 - ADAM_B2) * (g * g)
    m_hat = m / (1.0 - ADAM_B1 ** ADAM_STEP)
    v_hat = v / (1.0 - ADAM_B2 ** ADAM_STEP)
    delta = -ADAM_LR * (m_hat / (jnp.sqrt(v_hat) + ADAM_EPS) + ADAM_WD * w)
    return delta, m, v


def _adamw(w, g, m, v, name):
    r, c = w.shape

    def body(w_ref, g_ref, m_ref, v_ref, d_ref, nm_ref, nv_ref):
        delta, nm, nv = _adamw_math(w_ref[...], g_ref[...], m_ref[...], v_ref[...])
        d_ref[...] = delta
        nm_ref[...] = nm
        nv_ref[...] = nv

    rows = r // ADAM_STEPS
    assert rows * ADAM_STEPS == r and rows % 8 == 0
    spec = pl.BlockSpec((rows, c), lambda i: (i, 0))
    shape = jax.ShapeDtypeStruct((r, c), F32)
    return pl.pallas_call(
        body, name=name, grid=(ADAM_STEPS,), in_specs=[spec] * 4, out_specs=(spec,) * 3,
        out_shape=(shape,) * 3, compiler_params=pltpu.CompilerParams(vmem_limit_bytes=VMEM_LIMIT),
    )(w, g, m, v)


PACK_ROWS = 8


def _fold_heads(v):
    y = v[:, 0:LANES]
    for j in range(1, v.shape[1] // LANES):
        y = y + v[:, j * LANES:(j + 1) * LANES]
    return y + pltpu.roll(y, HEAD_DIM, 1)


def _small_adamw(recv, w_p, m_p, v_p):
    def body(r_ref, w_ref, m_ref, v_ref, g_ref, d_ref, nm_ref, nv_ref, loss_ref):
        tot = r_ref[0]
        for j in range(1, N_DEV):
            tot = tot + r_ref[j]
        loss_ref[...] = tot[3:4, 0:LANES]
        row1 = tot[1:2, :]
        row2 = tot[2:3, :]
        pieces = [_fold_heads(row1[:, 0:WIDTH]), _fold_heads(row2[:, WIDTH:WIDTH + A_KV_WIDTH]),
                  _fold_heads(row1[:, WIDTH:2 * WIDTH]), _fold_heads(row2[:, 0:WIDTH]),
                  row2[:, WIDTH + A_KV_WIDTH:WIDTH + 2 * A_KV_WIDTH], jnp.zeros((1, 3 * LANES), F32)]
        g = jnp.concatenate([tot[0:1, :], jnp.concatenate(pieces, axis=1), jnp.zeros((PACK_ROWS - 2, D_MODEL), F32)],
                            axis=0)
        g_ref[...] = g
        delta, nm, nv = _adamw_math(w_ref[...], g, m_ref[...], v_ref[...])
        d_ref[...] = delta
        nm_ref[...] = nm
        nv_ref[...] = nv

    shape = jax.ShapeDtypeStruct((PACK_ROWS, D_MODEL), F32)
    return pl.pallas_call(body, name="small_adamw",
                          out_shape=(shape,) * 4 + (jax.ShapeDtypeStruct((1, LANES), F32),))(recv, w_p, m_p, v_p)


def _pack_small(norm_gain, q_a, k_a, q_b, k_b, sinks):
    def lane_pad(a):
        return jnp.pad(a, ((0, 0), (0, LANES - a.shape[1])))
    row1 = jnp.concatenate([lane_pad(q_a), lane_pad(k_a), lane_pad(q_b), lane_pad(k_b), lane_pad(sinks),
                            jnp.zeros((1, 3 * LANES), F32)], axis=1)
    return jnp.concatenate([norm_gain, row1, jnp.zeros((PACK_ROWS - 2, D_MODEL), F32)], axis=0)


def _unpack_small(p):
    return (p[0:1, :], p[1:2, 0:HEAD_DIM], p[1:2, LANES:LANES + HEAD_DIM], p[1:2, 2 * LANES:2 * LANES + HEAD_DIM],
            p[1:2, 3 * LANES:3 * LANES + HEAD_DIM], p[1:2, 4 * LANES:4 * LANES + HEADS])


def kernel(x, norm_gain, w_in, q_norm_a, k_norm_a, sinks_a, q_norm_b, k_norm_b, w_out, loss_target, m_norm_gain, m_w_in, m_q_norm_a, m_k_norm_a, m_sinks_a, m_q_norm_b, m_k_norm_b, m_w_out, v_norm_gain, v_w_in, v_q_norm_a, v_k_norm_a, v_sinks_a, v_q_norm_b, v_k_norm_b, v_w_out):
    chip = 2 * lax.axis_index("x") + lax.axis_index("y")

    w_in_t, m_w_in_t, v_w_in_t = w_in[0].T, m_w_in[0].T, v_w_in[0].T

    w_in_all, w_out_all = _gather_weights([w_in_t.astype(BF16), w_out[0].astype(BF16)], "gather_weights")
    w_in_bf = w_in_all.reshape(IN_WIDTH, D_MODEL)
    w_out_bf = w_out_all.reshape(D_MODEL, D_MODEL)

    loss_part, gx, h_t, wins, gw_out, (dgain, dgqa, dgka, dsink, dgqb, dgkb) = _local_step(
        x[0], loss_target[0], norm_gain, w_in_bf, q_norm_a, k_norm_a, sinks_a, q_norm_b, k_norm_b, w_out_bf)

    small = jnp.concatenate([
        dgain, jnp.concatenate([dgqa, dgqb], axis=1),
        jnp.concatenate([dgkb, dgka, dsink, jnp.zeros((1, D_MODEL - WIDTH - 2 * A_KV_WIDTH), F32)], axis=1),
        jnp.pad(loss_part, ((0, 0), (0, D_MODEL - LANES))),
        jnp.zeros((PACK_ROWS - 4, D_MODEL), F32)], axis=0)
    order = (chip ^ jnp.array(RELATIONS, jnp.int32)).astype(jnp.int32)
    win_sum, wout_sum, small_recv = _grad_reduce(order, h_t, wins, gw_out, small)
    shift = jnp.array(WIN_SHIFT, jnp.int32)[chip]
    g_w_in_t = lax.dynamic_slice_in_dim(win_sum.transpose(1, 0, 2).reshape(WIN, D_MODEL), shift, IN_COLS, axis=0)
    g_w_out = wout_sum.reshape(OUT_ROWS, D_MODEL)

    d_w_in, nm_w_in, nv_w_in = (a.T for a in _adamw(w_in_t, g_w_in_t, m_w_in_t, v_w_in_t, "adamw_w_in"))
    g_w_in = g_w_in_t.T
    d_w_out, nm_w_out, nv_w_out = _adamw(w_out[0], g_w_out, m_w_out[0], v_w_out[0], "adamw_w_out")
    g_s, d_s, nm_s, nv_s, loss_row = _small_adamw(
        small_recv,
        _pack_small(norm_gain, q_norm_a, k_norm_a, q_norm_b, k_norm_b, sinks_a),
        _pack_small(m_norm_gain, m_q_norm_a, m_k_norm_a, m_q_norm_b, m_k_norm_b, m_sinks_a),
        _pack_small(v_norm_gain, v_q_norm_a, v_k_norm_a, v_q_norm_b, v_k_norm_b, v_sinks_a))
    loss = loss_row[0, 0]

    def leaves(small_packed, big_in, big_out):
        gain, qa, ka, qb, kb, sk = _unpack_small(small_packed)
        return (gain, big_in[None], qa, ka, sk, qb, kb, big_out[None])

    return ((loss, gx[None]) + leaves(g_s, g_w_in, g_w_out) + leaves(d_s, d_w_in, d_w_out)
            + leaves(nm_s, nm_w_in, nm_w_out) + leaves(nv_s, nv_w_in, nv_w_out))
```

```python
import jax
import jax.numpy as jnp
from jax import lax
from jax.experimental import pallas as pl
from jax.experimental.pallas import tpu as pltpu

F32 = jnp.float32
BF16 = jnp.bfloat16

D_MODEL = 1024
HEAD_DIM = 64
HEADS = 8
WIDTH = HEADS * HEAD_DIM
A_KV_WIDTH = 2 * HEAD_DIM
BLOCK = 128
LANES = 128
FOLD = 16
A_MAX_DIST = 127
B_MAX_DIST = 128
ROPE_THETA = 10000.0
EPS = 1e-6
NEG = -1e30
SCALE = HEAD_DIM ** -0.5

IN_WIDTH = 3328
C_QA, C_KA, C_VA, C_GA, C_QB, C_KB, C_VB, C_GB, C_END = 0, 512, 640, 768, 1280, 1792, 2304, 2816, 3328

N_DEV = 8
N_CHIP = 4
MESH = pl.DeviceIdType.MESH
IN_COLS = IN_WIDTH // N_CHIP
WIN = 896
WIN_START = (0, 768, 1664, 2432)
WIN_SHIFT = (0, 64, 0, 64)
OUT_ROWS = D_MODEL // N_CHIP
RELATIONS = (3, 1, 2, 0)

ADAM_LR = 0.001
ADAM_B1 = 0.9
ADAM_B2 = 0.999
ADAM_EPS = 1e-08
ADAM_WD = 0.01
ADAM_STEP = 10

ROW_TILE = 256
FOLD_ROWS = ROW_TILE // FOLD
GRAD_ROWS = 1024
ACC_COLS = 256
VMEM_LIMIT = 56 * 1024 * 1024


def _dot(a, b):
    return jnp.dot(a, b, preferred_element_type=F32)


def _dot_nt(a, b):
    return lax.dot_general(a, b, (((1,), (1,)), ((), ())), preferred_element_type=F32)


def _dot_tn(a, b):
    return lax.dot_general(a, b, (((0,), (0,)), ((), ())), preferred_element_type=F32)


def _head_sum(z, bd):
    w = bd.shape[0]
    zb = z.astype(BF16)
    parts = [_dot(zb[:, a:a + w], bd) for a in range(0, z.shape[1], w)]
    return parts[0] if len(parts) == 1 else jnp.concatenate(parts, axis=1)


def _swap_halves(t):
    w = t.shape[1]
    lane = lax.broadcasted_iota(jnp.int32, t.shape, 1)
    return jnp.where(lane % HEAD_DIM < HEAD_DIM // 2, pltpu.roll(t, w - 32, 1), pltpu.roll(t, 32, 1))


def _qknorm_rope(t, g, cos, sin_s, bd):
    r = lax.rsqrt(_head_sum(t * t, bd) * (1.0 / HEAD_DIM) + EPS)
    n = (t * r) * g
    return n * cos + _swap_halves(n) * sin_s


def _qknorm_rope_bwd(dout, t, g, cos, sin_s, bd):
    dout, t = dout.astype(F32), t.astype(F32)
    dn = dout * cos + _swap_halves(dout * sin_s)
    r = lax.rsqrt(_head_sum(t * t, bd) * (1.0 / HEAD_DIM) + EPS)
    tr = t * r
    u = dn * g
    dt = r * (u - tr * (_head_sum(u * tr, bd) * (1.0 / HEAD_DIM)))
    return dt, dn * tr


def _sigmoid(g):
    return 1.0 / (1.0 + jnp.exp(-g))


def _expand_heads(st):
    t = st.shape[0]
    lane = lax.broadcasted_iota(jnp.int32, (t, LANES), 1)
    chunks = []
    for c in range(WIDTH // LANES):
        chunks.append(jnp.where(lane < HEAD_DIM, st[:, 2 * c:2 * c + 1], st[:, 2 * c + 1:2 * c + 2]))
    return jnp.concatenate(chunks, axis=1)


def _reduce_heads(z):
    t = z.shape[0]
    lane = lax.broadcasted_iota(jnp.int32, (t, LANES), 1)
    out = jnp.zeros((t, LANES), F32)
    for c in range(WIDTH // LANES):
        zc = z[:, c * LANES:(c + 1) * LANES]
        for ph in range(2):
            s = jnp.sum(jnp.where((lane // HEAD_DIM) == ph, zc, 0.0), axis=-1, keepdims=True)
            out = jnp.where(lane == 2 * c + ph, s, out)
    return out


def _fold_scratch(w):
    return pltpu.VMEM((w // LANES, ROW_TILE, LANES), F32)


def _store_folded(out_ref, val, scr):
    n = val.shape[1] // LANES
    for c in range(n):
        scr[c] = val[:, c * LANES:(c + 1) * LANES]
    for r in range(FOLD):
        piece = [scr[c, pl.ds(r, FOLD_ROWS, stride=FOLD), :] for c in range(n)]
        out_ref[r] = (piece[0] if n == 1 else jnp.concatenate(piece, axis=1)).astype(out_ref.dtype)


def _load_folded(in_ref, scr):
    n = in_ref.shape[2] // LANES
    for r in range(FOLD):
        blk = in_ref[r].astype(F32)
        for c in range(n):
            scr[c, pl.ds(r, FOLD_ROWS, stride=FOLD), :] = blk[:, c * LANES:(c + 1) * LANES]
    return scr[0] if n == 1 else jnp.concatenate([scr[c] for c in range(n)], axis=1)


def _rows(w, tm=ROW_TILE):
    return pl.BlockSpec((tm, w), lambda i: (i, 0))


def _folded_rows(w):
    return pl.BlockSpec((FOLD, FOLD_ROWS, w), lambda i: (0, i, 0))


def _whole(shape):
    return pl.BlockSpec(shape, lambda i: (0,) * len(shape))


def _inproj(x2, gain, w_bf, cos, sin_s, gqa, gka, gqb, gkb, bd256, bd128):
    s = x2.shape[0]
    tm = ROW_TILE

    def body(x_ref, gain_ref, w_hbm, cos_ref, sin_ref, gqa_ref, gka_ref, gqb_ref, gkb_ref, bd256_ref, bd128_ref,
             qa_ref, ka_ref, va_ref, qb_ref, kb_ref, vb_ref, qbf_ref, kbf_ref, vbf_ref,
             qa_raw_ref, ka_raw_ref, ga_ref, qb_raw_ref, kb_raw_ref, gb_ref, w_vmem, scr):
        @pl.when(pl.program_id(0) == 0)
        def _():
            pltpu.sync_copy(w_hbm, w_vmem)

        xt = x_ref[...]
        r = lax.rsqrt(jnp.mean(xt * xt, axis=-1, keepdims=True) + EPS)
        h = ((xt * r) * gain_ref[...]).astype(BF16)
        cos1 = cos_ref[...]
        sin1 = sin_ref[...]
        cos4 = jnp.tile(cos1, (1, 4))
        sin4 = jnp.tile(sin1, (1, 4))

        def seg(a, b):
            return _dot_nt(h, w_vmem[a:b, :])

        t = seg(C_QA, C_KA)
        qa_raw_ref[...] = t.astype(BF16)
        qa_ref[...] = (_qknorm_rope(t, gqa_ref[...], cos4, sin4, bd256_ref[...]) * SCALE).astype(BF16)
        t = seg(C_KA, C_VA)
        ka_raw_ref[...] = t.astype(BF16)
        ka_ref[...] = _qknorm_rope(t, gka_ref[...], cos1, sin1, bd128_ref[...]).astype(BF16)
        va_ref[...] = seg(C_VA, C_GA).astype(BF16)
        ga_ref[...] = seg(C_GA, C_QB).astype(BF16)
        t = seg(C_QB, C_KB)
        qb_raw_ref[...] = t.astype(BF16)
        t = _qknorm_rope(t, gqb_ref[...], cos4, sin4, bd256_ref[...]) * SCALE
        qb_ref[...] = t.astype(BF16)
        _store_folded(qbf_ref, t, scr)
        t = seg(C_KB, C_VB)
        kb_raw_ref[...] = t.astype(BF16)
        t = _qknorm_rope(t, gkb_ref[...], cos4, sin4, bd256_ref[...])
        kb_ref[...] = t.astype(BF16)
        _store_folded(kbf_ref, t, scr)
        t = seg(C_VB, C_GB)
        vb_ref[...] = t.astype(BF16)
        _store_folded(vbf_ref, t, scr)
        gb_ref[...] = seg(C_GB, C_END).astype(BF16)

    sds = jax.ShapeDtypeStruct
    folded = sds((FOLD, s // FOLD, WIDTH), BF16)
    out_shape = (sds((s, WIDTH), BF16), sds((s, A_KV_WIDTH), BF16), sds((s, A_KV_WIDTH), BF16),
                 sds((s, WIDTH), BF16), sds((s, WIDTH), BF16), sds((s, WIDTH), BF16), folded, folded, folded,
                 sds((s, WIDTH), BF16), sds((s, A_KV_WIDTH), BF16), sds((s, WIDTH), BF16),
                 sds((s, WIDTH), BF16), sds((s, WIDTH), BF16), sds((s, WIDTH), BF16))
    out_specs = (_rows(WIDTH), _rows(A_KV_WIDTH), _rows(A_KV_WIDTH), _rows(WIDTH), _rows(WIDTH), _rows(WIDTH),
                 _folded_rows(WIDTH), _folded_rows(WIDTH), _folded_rows(WIDTH),
                 _rows(WIDTH), _rows(A_KV_WIDTH), _rows(WIDTH), _rows(WIDTH), _rows(WIDTH), _rows(WIDTH))
    return pl.pallas_call(
        body, name="inproj_fwd", grid=(s // tm,),
        in_specs=[_rows(D_MODEL), _whole(gain.shape), pl.BlockSpec(memory_space=pl.ANY), _rows(LANES), _rows(LANES),
                  _whole(gqa.shape), _whole(gka.shape), _whole(gqb.shape), _whole(gkb.shape), _whole(bd256.shape),
                  _whole(bd128.shape)],
        out_specs=out_specs, out_shape=out_shape,
        scratch_shapes=[pltpu.VMEM((IN_WIDTH, D_MODEL), BF16), _fold_scratch(WIDTH)],
        compiler_params=pltpu.CompilerParams(dimension_semantics=("arbitrary",), vmem_limit_bytes=VMEM_LIMIT),
    )(x2, gain, w_bf, cos, sin_s, gqa, gka, gqb, gkb, bd256, bd128)


def _seq_pos(idx, dil):
    if dil == 4:
        return 4 * (idx % 32) + idx // 32
    return idx


SOFTMAX_ROWS = 64


def _upper_mask(dil, r0=0, rows=2 * BLOCK):
    qi = (lax.broadcasted_iota(jnp.int32, (rows, BLOCK), 0) + r0) % BLOCK
    kj = lax.broadcasted_iota(jnp.int32, (rows, BLOCK), 1)
    return _seq_pos(kj, dil) > _seq_pos(qi, dil)


def _eye_mask(r0=0, rows=2 * BLOCK):
    qi = (lax.broadcasted_iota(jnp.int32, (rows, BLOCK), 0) + r0) % BLOCK
    kj = lax.broadcasted_iota(jnp.int32, (rows, BLOCK), 1)
    return qi == kj


def _stack_heads(a2, c, gqa):
    lane = lax.broadcasted_iota(jnp.int32, (1, LANES), 1) // HEAD_DIM
    zero = jnp.zeros_like(a2)
    if gqa:
        keep = lane == (c // 2)
        return jnp.concatenate([jnp.where(keep, a2, zero), jnp.where(keep, _swap_heads(a2), zero)], axis=0)
    return jnp.concatenate([jnp.where(lane == 0, a2, zero), jnp.where(lane == 1, a2, zero)], axis=0)


def _unstack_heads(a, c, gqa):
    lane = lax.broadcasted_iota(jnp.int32, (1, LANES), 1) // HEAD_DIM
    if gqa:
        return jnp.where(lane == (c // 2), a[:BLOCK], _swap_heads(a[BLOCK:]))
    return jnp.where(lane == 0, a[:BLOCK], a[BLOCK:])


def _stacked_head_ids(c, gqa):
    if gqa:
        return 2 * c + c // 2, 2 * c + 1 - c // 2
    return 2 * c, 2 * c + 1


def _per_head_rows(blk, heads):
    return jnp.concatenate([blk[:, heads[0]:heads[0] + 1], blk[:, heads[1]:heads[1] + 1]], axis=0)


def _attn_view(a, dil):
    if dil == 1:
        return a[None]
    if dil == 4:
        return a.reshape(4, 4, a.shape[1], a.shape[2])
    return a


def _attn_unview(a, dil):
    if dil == 1:
        return a[0]
    if dil == 4:
        return a.reshape(FOLD, a.shape[2], a.shape[3])
    return a


def _attn_specs(dil, nb):
    if dil == 4:
        def spec(fn):
            return lambda w: pl.BlockSpec((4, None, BLOCK // 4, w), lambda r, i: (0, r, fn(i), 0))
    else:
        def spec(fn):
            return lambda w: pl.BlockSpec((None, BLOCK, w), lambda r, i: (r, fn(i), 0))
    return spec


def _blk_load(ref, sl, dil):
    if dil == 4:
        return ref[:, :, sl].reshape(BLOCK, sl.stop - sl.start)
    return ref[:, sl]


def _blk_store(ref, sl, val, dil):
    val = val.astype(ref.dtype)
    if dil == 4:
        ref[:, :, sl] = val.reshape(4, BLOCK // 4, sl.stop - sl.start)
    else:
        ref[:, sl] = val


def _swap_heads(a):
    return pltpu.roll(a.astype(F32), HEAD_DIM, 1).astype(a.dtype)


def _attn_fwd(q, k, v, sinks, *, dil, max_dist, name):
    q, k, v = _attn_view(q, dil), _attn_view(k, dil), _attn_view(v, dil)
    kw = k.shape[-1]
    gqa = kw == A_KV_WIDTH
    n_seq = dil
    nb = (q.shape[-2] * (4 if dil == 4 else 1)) // BLOCK
    with_sinks = sinks is not None
    all_lanes = slice(0, LANES)
    assert max_dist in (BLOCK - 1, BLOCK)
    diag = max_dist == BLOCK

    def body(*refs):
        if with_sinks:
            q_ref, kp_ref, kc_ref, vp_ref, vc_ref, sink_ref, o_ref, m_ref, l_ref = refs
        else:
            q_ref, kp_ref, kc_ref, vp_ref, vc_ref, o_ref, m_ref, l_ref = refs

        def block(has_prev):
            lane = lax.broadcasted_iota(jnp.int32, (1, LANES), 1)
            with_diag = diag and has_prev
            upper, eye = _upper_mask(dil), _eye_mask()
            first_rows = lax.broadcasted_iota(jnp.int32, (2 * BLOCK, 1), 0) < BLOCK
            m_blk = jnp.zeros((BLOCK, LANES), F32)
            l_blk = jnp.ones((BLOCK, LANES), F32)
            chunks = range(WIDTH // LANES)
            scores, values = [], []
            for c in chunks:
                sl = slice(c * LANES, (c + 1) * LANES)
                ksl = slice(0, LANES) if gqa else sl
                kcur, vcur = _blk_load(kc_ref, ksl, dil), _blk_load(vc_ref, ksl, dil)
                qs = _stack_heads(_blk_load(q_ref, sl, dil), c, gqa)
                if has_prev:
                    kcur = jnp.concatenate([_blk_load(kp_ref, ksl, dil), kcur], axis=0)
                    vcur = jnp.concatenate([_blk_load(vp_ref, ksl, dil), vcur], axis=0)
                scores.append(_dot_nt(qs, kcur))
                values.append(vcur)
            probs = []
            for c in chunks:
                heads = _stacked_head_ids(c, gqa)
                s = scores[c]
                if has_prev:
                    s_p = s[:, :BLOCK]
                    sc = jnp.where(upper, s_p, s[:, BLOCK:])
                else:
                    sc = jnp.where(upper, NEG, s)
                if with_diag:
                    sd = jnp.where(eye, s_p, NEG)
                    m = jnp.max(jnp.maximum(sc, sd), axis=-1, keepdims=True)
                else:
                    m = jnp.max(sc, axis=-1, keepdims=True)
                if with_sinks:
                    sk = jnp.where(first_rows, sink_ref[0, heads[0]], sink_ref[0, heads[1]])
                    m = jnp.maximum(m, sk)
                p = jnp.exp(sc - m)
                zero = jnp.zeros_like(p)
                if with_diag:
                    pd = jnp.exp(sd - m)
                    l = jnp.sum(p + pd, axis=-1, keepdims=True)
                else:
                    pd = zero
                    l = jnp.sum(p, axis=-1, keepdims=True)
                if with_sinks:
                    l = l + jnp.exp(sk - m)
                pf = jnp.where(upper, zero, p)
                if has_prev:
                    pf = jnp.concatenate([jnp.where(upper, p, pd), pf], axis=1)
                probs.append(pf.astype(BF16))
                for n, h in enumerate(heads):
                    rows = slice(n * BLOCK, (n + 1) * BLOCK)
                    m_blk = jnp.where(lane == h, m[rows], m_blk)
                    l_blk = jnp.where(lane == h, l[rows], l_blk)
            for c in chunks:
                sl = slice(c * LANES, (c + 1) * LANES)
                _blk_store(o_ref, sl, _unstack_heads(_dot(probs[c], values[c]), c, gqa), dil)
            _blk_store(m_ref, all_lanes, m_blk, dil)
            _blk_store(l_ref, all_lanes, l_blk, dil)

        @pl.when(pl.program_id(1) == 0)
        def _():
            block(False)

        @pl.when(pl.program_id(1) > 0)
        def _():
            block(True)

    spec = _attn_specs(dil, nb)
    cur = spec(lambda i: i)
    prev = spec(lambda i: jnp.maximum(i - 1, 0))
    in_specs = [cur(WIDTH), prev(kw), cur(kw), prev(kw), cur(kw)]
    args = [q, k, k, v, v]
    if with_sinks:
        in_specs.append(pl.BlockSpec(memory_space=pltpu.SMEM))
        args.append(sinks)
    stats = jax.ShapeDtypeStruct(q.shape[:-1] + (LANES,), F32)
    o, m, l = pl.pallas_call(
        body, name=name, grid=(n_seq, nb), in_specs=in_specs,
        out_specs=(cur(WIDTH), cur(LANES), cur(LANES)),
        out_shape=(jax.ShapeDtypeStruct(q.shape, BF16), stats, stats),
        compiler_params=pltpu.CompilerParams(dimension_semantics=("arbitrary", "arbitrary")),
    )(*args)
    return _attn_unview(o, dil), _attn_unview(m, dil), _attn_unview(l, dil)


def _attn_bwd(q, k, v, do, lse, delta, *, dil, max_dist, name):
    q, k, v, do, lse, delta = (_attn_view(a, dil) for a in (q, k, v, do, lse, delta))
    kw = k.shape[-1]
    gqa = kw == A_KV_WIDTH
    n_seq = dil
    nb = (q.shape[-2] * (4 if dil == 4 else 1)) // BLOCK
    n_kc = kw // LANES
    all_lanes = slice(0, LANES)
    assert max_dist in (BLOCK - 1, BLOCK)
    diag = max_dist == BLOCK

    def body(q_ref, kp_ref, kc_ref, vp_ref, vc_ref, do_ref, lse_ref, dl_ref, dq_ref, dk_ref, dv_ref, ck_ref, cv_ref):
        i = pl.program_id(1)

        def block(has_prev):
            upper, eye = _upper_mask(dil), _eye_mask()
            lse_blk = _blk_load(lse_ref, all_lanes, dil)
            dl_blk = _blk_load(dl_ref, all_lanes, dil)
            dk_acc = [None] * n_kc
            dv_acc = [None] * n_kc
            chunks = range(WIDTH // LANES)
            operands, products = [], []
            for c in chunks:
                sl = slice(c * LANES, (c + 1) * LANES)
                kc = 0 if gqa else c
                ksl = slice(kc * LANES, (kc + 1) * LANES)
                k2, v2 = _blk_load(kc_ref, ksl, dil), _blk_load(vc_ref, ksl, dil)
                if has_prev:
                    k2 = jnp.concatenate([_blk_load(kp_ref, ksl, dil), k2], axis=0)
                    v2 = jnp.concatenate([_blk_load(vp_ref, ksl, dil), v2], axis=0)
                qs = _stack_heads(_blk_load(q_ref, sl, dil), c, gqa)
                dos = _stack_heads(_blk_load(do_ref, sl, dil), c, gqa)
                operands.append((qs, dos, k2))
                products.append((_dot_nt(qs, k2), _dot_nt(dos, v2)))
            weights = []
            for c in chunks:
                heads = _stacked_head_ids(c, gqa)
                lse2 = _per_head_rows(lse_blk, heads)
                dl2 = _per_head_rows(dl_blk, heads)
                s, dp = products[c]
                if has_prev:
                    s_p, dp_p = s[:, :BLOCK], dp[:, :BLOCK]
                    sc = jnp.where(upper, s_p, s[:, BLOCK:])
                    dpc = jnp.where(upper, dp_p, dp[:, BLOCK:])
                else:
                    sc = jnp.where(upper, NEG, s)
                    dpc = dp
                p = jnp.exp(sc - lse2)
                ds = p * (dpc - dl2)
                zero = jnp.zeros_like(p)
                pf = jnp.where(upper, zero, p)
                dsf = jnp.where(upper, zero, ds)
                if has_prev:
                    if diag:
                        pd = jnp.exp(jnp.where(eye, s_p, NEG) - lse2)
                        dsd = pd * (dp_p - dl2)
                    else:
                        pd = dsd = zero
                    pf = jnp.concatenate([jnp.where(upper, p, pd), pf], axis=1)
                    dsf = jnp.concatenate([jnp.where(upper, ds, dsd), dsf], axis=1)
                weights.append((pf.astype(BF16), dsf.astype(BF16)))
            for c in chunks:
                sl = slice(c * LANES, (c + 1) * LANES)
                kc = 0 if gqa else c
                qs, dos, k2 = operands[c]
                pf, dsf = weights[c]
                _blk_store(dq_ref, sl, _unstack_heads(_dot(dsf, k2), c, gqa) * SCALE, dil)
                dk2 = _dot_tn(dsf, qs)
                dv2 = _dot_tn(pf, dos)
                dk_acc[kc] = dk2 if dk_acc[kc] is None else dk_acc[kc] + dk2
                dv_acc[kc] = dv2 if dv_acc[kc] is None else dv_acc[kc] + dv2
            for kc in range(n_kc):
                sl = slice(kc * LANES, (kc + 1) * LANES)
                if has_prev:
                    _blk_store(dk_ref, sl, ck_ref[:, sl] + dk_acc[kc][:BLOCK], dil)
                    _blk_store(dv_ref, sl, cv_ref[:, sl] + dv_acc[kc][:BLOCK], dil)
                    ck_ref[:, sl] = dk_acc[kc][BLOCK:]
                    cv_ref[:, sl] = dv_acc[kc][BLOCK:]
                else:
                    ck_ref[:, sl] = dk_acc[kc]
                    cv_ref[:, sl] = dv_acc[kc]

        @pl.when(i == 0)
        def _():
            block(False)

        @pl.when((i > 0) & (i < nb))
        def _():
            block(True)

        @pl.when(i == nb)
        def _():
            for kc in range(n_kc):
                sl = slice(kc * LANES, (kc + 1) * LANES)
                _blk_store(dk_ref, sl, ck_ref[:, sl], dil)
                _blk_store(dv_ref, sl, cv_ref[:, sl], dil)

    spec = _attn_specs(dil, nb)
    cur = spec(lambda i: jnp.minimum(i, nb - 1))
    prev = spec(lambda i: jnp.clip(i - 1, 0, nb - 1))
    lag = spec(lambda i: jnp.maximum(i - 1, 0))
    sds = jax.ShapeDtypeStruct
    dq, dk, dv = pl.pallas_call(
        body, name=name, grid=(n_seq, nb + 1),
        in_specs=[cur(WIDTH), prev(kw), cur(kw), prev(kw), cur(kw), cur(WIDTH), cur(LANES), cur(LANES)],
        out_specs=(cur(WIDTH), lag(kw), lag(kw)),
        out_shape=(sds(q.shape, BF16), sds(k.shape, BF16), sds(k.shape, BF16)),
        scratch_shapes=[pltpu.VMEM((BLOCK, kw), F32), pltpu.VMEM((BLOCK, kw), F32)],
        compiler_params=pltpu.CompilerParams(dimension_semantics=("arbitrary", "arbitrary")),
    )(q, k, k, v, v, do, lse, delta)
    return _attn_unview(dq, dil), _attn_unview(dk, dil), _attn_unview(dv, dil)


def _outproj(att_a, att_b1, att_b4, att_b16, g_a, g_b, x2, tgt2, w_out_bf, sink_row):
    s = x2.shape[0]
    tm = ROW_TILE

    def body(oa_ref, ma_ref, la_ref, ob1_ref, m1_ref, l1_ref, ob4_ref, m4_ref, l4_ref, ob16_ref, m16_ref, l16_ref,
             ga_ref, gb_ref, x_ref, t_ref, w_ref, sink_ref,
             dy_ref, doa_ref, dob_ref, dobf_ref, dga_ref, dgb_ref, lsea_ref, lseb_ref, lsebf_ref, dla_ref, dlb_ref,
             dlbf_ref, gw_ref, loss_ref, dsink_ref, scr, scr_st):
        i = pl.program_id(0)

        @pl.when(i == 0)
        def _():
            gw_ref[...] = jnp.zeros_like(gw_ref)
            loss_ref[...] = jnp.zeros_like(loss_ref)
            dsink_ref[...] = jnp.zeros_like(dsink_ref)

        ms = [m1_ref[...], _load_folded(m4_ref, scr_st), _load_folded(m16_ref, scr_st)]
        ls = [l1_ref[...], _load_folded(l4_ref, scr_st), _load_folded(l16_ref, scr_st)]
        mx = jnp.maximum(jnp.maximum(ms[0], ms[1]), ms[2])
        scale = [jnp.exp(mp - mx) for mp in ms]
        den = (ls[0] * scale[0] + ls[1] * scale[1]) + ls[2] * scale[2]
        lane = lax.broadcasted_iota(jnp.int32, (tm, LANES), 1)
        lse_b = jnp.where(lane < HEADS, mx + jnp.log(den), 0.0)
        lseb_ref[...] = lse_b
        _store_folded(lsebf_ref, lse_b, scr_st)
        inv_den = 1.0 / den
        o_b = _expand_heads(scale[0] * inv_den) * ob1_ref[...].astype(F32)
        o_b = o_b + _expand_heads(scale[1] * inv_den) * _load_folded(ob4_ref, scr)
        o_b = o_b + _expand_heads(scale[2] * inv_den) * _load_folded(ob16_ref, scr)
        l_a = la_ref[...]
        lse_a = jnp.where(lane < HEADS, ma_ref[...] + jnp.log(l_a), 0.0)
        lsea_ref[...] = lse_a
        o_a = _expand_heads(1.0 / l_a) * oa_ref[...].astype(F32)
        g_a = ga_ref[...].astype(F32)
        g_b = gb_ref[...].astype(F32)
        sg_a = _sigmoid(g_a)
        sg_b = _sigmoid(g_b)
        silu_a = g_a * sg_a
        silu_b = g_b * sg_b
        mixed = jnp.concatenate([o_a * silu_a, o_b * silu_b], axis=1).astype(BF16)
        w = w_ref[...]
        y = x_ref[...] + _dot(mixed, w)
        diff = y - t_ref[...]
        loss_ref[...] += (0.5 / D_MODEL) * jnp.sum(diff * diff)
        dy = diff * (1.0 / D_MODEL)
        dy_ref[...] = dy
        dyb = dy.astype(BF16)
        gw_ref[...] += _dot_tn(mixed, dyb)
        dmixed = _dot_nt(dyb, w)
        dm_a = dmixed[:, :WIDTH]
        dm_b = dmixed[:, WIDTH:]
        do_a = dm_a * silu_a
        do_b = dm_b * silu_b
        doa_ref[...] = do_a.astype(BF16)
        dob_ref[...] = do_b.astype(BF16)
        _store_folded(dobf_ref, do_b, scr)
        dga_ref[...] = (dm_a * o_a * (sg_a * (1.0 + g_a * (1.0 - sg_a)))).astype(BF16)
        dgb_ref[...] = (dm_b * o_b * (sg_b * (1.0 + g_b * (1.0 - sg_b)))).astype(BF16)
        dl_a = _reduce_heads(do_a * o_a)
        dla_ref[...] = dl_a
        dl_b = _reduce_heads(do_b * o_b)
        dlb_ref[...] = dl_b
        _store_folded(dlbf_ref, dl_b, scr_st)
        dsink_ref[...] -= jnp.sum(jnp.exp(sink_ref[...] - lse_a) * dl_a, axis=0, keepdims=True)

    sds = jax.ShapeDtypeStruct
    ln = s // FOLD
    natural = [_rows(WIDTH), _rows(LANES), _rows(LANES)]
    folded = [_folded_rows(WIDTH), _folded_rows(LANES), _folded_rows(LANES)]
    return pl.pallas_call(
        body, name="outproj_fwd_bwd", grid=(s // tm,),
        in_specs=natural + natural + folded + folded
                 + [_rows(WIDTH), _rows(WIDTH), _rows(D_MODEL), _rows(D_MODEL), _whole((D_MODEL, D_MODEL)),
                    _whole((1, LANES))],
        out_specs=(_rows(D_MODEL), _rows(WIDTH), _rows(WIDTH), _folded_rows(WIDTH), _rows(WIDTH), _rows(WIDTH),
                   _rows(LANES), _rows(LANES), _folded_rows(LANES), _rows(LANES), _rows(LANES), _folded_rows(LANES),
                   _whole((D_MODEL, D_MODEL)), _whole((1, LANES)), _whole((1, LANES))),
        out_shape=(sds((s, D_MODEL), F32), sds((s, WIDTH), BF16), sds((s, WIDTH), BF16),
                   sds((FOLD, ln, WIDTH), BF16), sds((s, WIDTH), BF16), sds((s, WIDTH), BF16),
                   sds((s, LANES), F32), sds((s, LANES), F32), sds((FOLD, ln, LANES), F32), sds((s, LANES), F32),
                   sds((s, LANES), F32), sds((FOLD, ln, LANES), F32),
                   sds((D_MODEL, D_MODEL), F32), sds((1, LANES), F32), sds((1, LANES), F32)),
        scratch_shapes=[_fold_scratch(WIDTH), _fold_scratch(LANES)],
        compiler_params=pltpu.CompilerParams(dimension_semantics=("arbitrary",), vmem_limit_bytes=VMEM_LIMIT),
    )(*att_a, *att_b1, *att_b4, *att_b16, g_a, g_b, x2, tgt2, w_out_bf, sink_row)


def _inproj_bwd(x2, dy, gain, w_bf, cos, sin_s, gqa, gka, gqb, gkb, bd256, bd128,
                qa_raw, ka_raw, qb_raw, kb_raw, dq_a, dk_a, dv_a, dqkv_b1, dqkv_b4, dqkv_b16, dg_a, dg_b):
    s = x2.shape[0]
    tm = ROW_TILE

    def body(x_ref, dy_ref, gain_ref, w_hbm, cos_ref, sin_ref, gqa_ref, gka_ref, gqb_ref, gkb_ref, bd256_ref,
             bd128_ref, qa_raw_ref, ka_raw_ref, qb_raw_ref, kb_raw_ref, dqa_ref, dka_ref, dva_ref,
             dq1_ref, dk1_ref, dv1_ref, dq4_ref, dk4_ref, dv4_ref, dq16_ref, dk16_ref, dv16_ref, dga_ref, dgb_ref,
             gx_ref, ht_ref, win_ref,
             dgain_ref, dgqa_ref, dgka_ref, dgqb_ref, dgkb_ref, w_vmem, dproj_ref, scr):
        i = pl.program_id(0)

        @pl.when(i == 0)
        def _():
            pltpu.sync_copy(w_hbm, w_vmem)
            dgain_ref[...] = jnp.zeros_like(dgain_ref)
            dgqa_ref[...] = jnp.zeros_like(dgqa_ref)
            dgka_ref[...] = jnp.zeros_like(dgka_ref)
            dgqb_ref[...] = jnp.zeros_like(dgqb_ref)
            dgkb_ref[...] = jnp.zeros_like(dgkb_ref)

        cos1 = cos_ref[...]
        sin1 = sin_ref[...]
        cos4 = jnp.tile(cos1, (1, 4))
        sin4 = jnp.tile(sin1, (1, 4))

        dt, dg = _qknorm_rope_bwd(dqa_ref[...], qa_raw_ref[...], gqa_ref[...], cos4, sin4, bd256_ref[...])
        dproj_ref[:, C_QA:C_KA] = dt.astype(BF16)
        dgqa_ref[...] += jnp.sum(dg, axis=0, keepdims=True)
        dt, dg = _qknorm_rope_bwd(dka_ref[...], ka_raw_ref[...], gka_ref[...], cos1, sin1, bd128_ref[...])
        dproj_ref[:, C_KA:C_VA] = dt.astype(BF16)
        dgka_ref[...] += jnp.sum(dg, axis=0, keepdims=True)
        dproj_ref[:, C_VA:C_GA] = dva_ref[...].astype(BF16)
        dproj_ref[:, C_GA:C_QB] = dga_ref[...]
        dq = (dq1_ref[...].astype(F32) + _load_folded(dq4_ref, scr)) + _load_folded(dq16_ref, scr)
        dt, dg = _qknorm_rope_bwd(dq, qb_raw_ref[...], gqb_ref[...], cos4, sin4, bd256_ref[...])
        dproj_ref[:, C_QB:C_KB] = dt.astype(BF16)
        dgqb_ref[...] += jnp.sum(dg, axis=0, keepdims=True)
        dk = (dk1_ref[...].astype(F32) + _load_folded(dk4_ref, scr)) + _load_folded(dk16_ref, scr)
        dt, dg = _qknorm_rope_bwd(dk, kb_raw_ref[...], gkb_ref[...], cos4, sin4, bd256_ref[...])
        dproj_ref[:, C_KB:C_VB] = dt.astype(BF16)
        dgkb_ref[...] += jnp.sum(dg, axis=0, keepdims=True)
        dv = (dv1_ref[...].astype(F32) + _load_folded(dv4_ref, scr)) + _load_folded(dv16_ref, scr)
        dproj_ref[:, C_VB:C_GB] = dv.astype(BF16)
        dproj_ref[:, C_GB:C_END] = dgb_ref[...]
        for k, start in enumerate(WIN_START):
            win_ref[k] = dproj_ref[:, start:start + WIN]

        xt = x_ref[...]
        gain_row = gain_ref[...]
        r = lax.rsqrt(jnp.mean(xt * xt, axis=-1, keepdims=True) + EPS)
        xr = xt * r
        ht_ref[...] = (xr * gain_row).T.astype(BF16)
        dh = _dot(dproj_ref[...], w_vmem[...])
        dgain_ref[...] += jnp.sum(dh * xr, axis=0, keepdims=True)
        u = dh * gain_row
        gx_ref[...] = dy_ref[...] + r * (u - xr * jnp.mean(u * xr, axis=-1, keepdims=True))

    def acc_row(w):
        return pl.BlockSpec((1, w), lambda i: (0, 0))

    sds = jax.ShapeDtypeStruct
    any_spec = pl.BlockSpec(memory_space=pl.ANY)
    win_spec = pl.BlockSpec((N_CHIP, tm, WIN), lambda i: (0, i, 0))
    return pl.pallas_call(
        body, name="inproj_bwd", grid=(s // tm,),
        in_specs=[_rows(D_MODEL), _rows(D_MODEL), _whole(gain.shape), any_spec, _rows(LANES), _rows(LANES),
                  _whole(gqa.shape), _whole(gka.shape), _whole(gqb.shape), _whole(gkb.shape), _whole(bd256.shape),
                  _whole(bd128.shape),
                  _rows(WIDTH), _rows(A_KV_WIDTH), _rows(WIDTH), _rows(WIDTH),
                  _rows(WIDTH), _rows(A_KV_WIDTH), _rows(A_KV_WIDTH)]
                 + [_rows(WIDTH)] * 3 + [_folded_rows(WIDTH)] * 6 + [_rows(WIDTH), _rows(WIDTH)],
        out_specs=(_rows(D_MODEL), pl.BlockSpec((D_MODEL, tm), lambda i: (0, i)), win_spec, acc_row(D_MODEL), acc_row(WIDTH), acc_row(A_KV_WIDTH), acc_row(WIDTH), acc_row(WIDTH)),
        out_shape=(sds((s, D_MODEL), F32), sds((D_MODEL, s), BF16), sds((N_CHIP, s, WIN), BF16),
                   sds((1, D_MODEL), F32),
                   sds((1, WIDTH), F32), sds((1, A_KV_WIDTH), F32), sds((1, WIDTH), F32), sds((1, WIDTH), F32)),
        scratch_shapes=[pltpu.VMEM((IN_WIDTH, D_MODEL), BF16), pltpu.VMEM((tm, IN_WIDTH), BF16),
                        _fold_scratch(WIDTH)],
        compiler_params=pltpu.CompilerParams(dimension_semantics=("arbitrary",), vmem_limit_bytes=VMEM_LIMIT),
    )(x2, dy, gain, w_bf, cos, sin_s, gqa, gka, gqb, gkb, bd256, bd128, qa_raw, ka_raw, qb_raw, kb_raw,
      dq_a, dk_a, dv_a, *dqkv_b1, *dqkv_b4, *dqkv_b16, dg_a, dg_b)


def _rope_tables(s):
    half = HEAD_DIM // 2
    inv = jnp.tile(ROPE_THETA ** (-jnp.arange(half, dtype=F32) / half), 4)
    sign = jnp.tile(jnp.concatenate([-jnp.ones((half,), F32), jnp.ones((half,), F32)]), 2)
    ang = jnp.arange(s).astype(F32)[:, None] * inv[None, :]
    return jnp.cos(ang), jnp.sin(ang) * sign[None, :]


def _block_diag_ones(w):
    idx = jnp.arange(w) // HEAD_DIM
    return (idx[:, None] == idx[None, :]).astype(BF16)


def _local_step(x2, tgt2, norm_gain, w_in_bf, q_norm_a, k_norm_a, sinks_a, q_norm_b, k_norm_b, w_out_bf):
    s = x2.shape[0]
    cos, sin_s = _rope_tables(s)
    bd256, bd128 = _block_diag_ones(2 * LANES), _block_diag_ones(A_KV_WIDTH)
    gqa = jnp.tile(q_norm_a, (1, HEADS))
    gka = jnp.tile(k_norm_a, (1, 2))
    gqb = jnp.tile(q_norm_b, (1, HEADS))
    gkb = jnp.tile(k_norm_b, (1, HEADS))
    sink_row = jnp.pad(sinks_a, ((0, 0), (0, LANES - HEADS)))

    (qa, ka, va, qb, kb, vb, qbf, kbf, vbf, qa_raw, ka_raw, g_a, qb_raw, kb_raw, g_b) = _inproj(
        x2, norm_gain, w_in_bf, cos, sin_s, gqa, gka, gqb, gkb, bd256, bd128)

    att_a = _attn_fwd(qa, ka, va, sinks_a, dil=1, max_dist=A_MAX_DIST, name="attn_a_fwd")
    att_b1 = _attn_fwd(qb, kb, vb, None, dil=1, max_dist=B_MAX_DIST, name="attn_b1_fwd")
    att_b4 = _attn_fwd(qbf, kbf, vbf, None, dil=4, max_dist=B_MAX_DIST, name="attn_b4_fwd")
    att_b16 = _attn_fwd(qbf, kbf, vbf, None, dil=16, max_dist=B_MAX_DIST, name="attn_b16_fwd")

    (dy, do_a, do_b, do_bf, dg_a, dg_b, lse_a, lse_b, lse_bf, dl_a, dl_b, dl_bf, gw_out, loss_part,
     dsink) = _outproj(att_a, att_b1, att_b4, att_b16, g_a, g_b, x2, tgt2, w_out_bf, sink_row)

    dq_a, dk_a, dv_a = _attn_bwd(qa, ka, va, do_a, lse_a, dl_a, dil=1, max_dist=A_MAX_DIST, name="attn_a_bwd")
    d_b1 = _attn_bwd(qb, kb, vb, do_b, lse_b, dl_b, dil=1, max_dist=B_MAX_DIST, name="attn_b1_bwd")
    d_b4 = _attn_bwd(qbf, kbf, vbf, do_bf, lse_bf, dl_bf, dil=4, max_dist=B_MAX_DIST, name="attn_b4_bwd")
    d_b16 = _attn_bwd(qbf, kbf, vbf, do_bf, lse_bf, dl_bf, dil=16, max_dist=B_MAX_DIST, name="attn_b16_bwd")

    gx, h_t, wins, dgain, dgqa, dgka, dgqb, dgkb = _inproj_bwd(
        x2, dy, norm_gain, w_in_bf, cos, sin_s, gqa, gka, gqb, gkb, bd256, bd128,
        qa_raw, ka_raw, qb_raw, kb_raw, dq_a, dk_a, dv_a, d_b1, d_b4, d_b16, dg_a, dg_b)
    return loss_part, gx, h_t, wins, gw_out, (dgain, dgqa, dgka, dsink, dgqb, dgkb)


def _position():
    return lax.axis_index("x"), lax.axis_index("y"), lax.axis_index("c")


GATHER_CHUNKS = 2


def _gather_weights(blocks, name):
    n = len(blocks)
    ch = GATHER_CHUNKS
    n_sems = n * (N_CHIP - 1) * ch

    def body(*refs):
        src_refs, dst_refs = refs[:n], refs[n:2 * n]
        ici_send, ici_recv, d2d_send, d2d_recv, local_sems = refs[2 * n:]
        x, y, c = _position()
        b = 2 * x + y
        copies = []
        for k in range(n):
            local = pltpu.make_async_copy(src_refs[k], dst_refs[k].at[b], local_sems.at[k])
            local.start()
            copies.append(local)

        def rows(k, core, j):
            half = blocks[k].shape[0] // 2
            return pl.ds(core * half + j * (half // ch), half // ch)

        plan = []
        for d in range(1, N_CHIP):
            px, py = x ^ (d >> 1), y ^ (d & 1)
            for j in range(ch):
                for k in range(n):
                    plan.append((px, py, 2 * px + py, k, j, ((d - 1) * ch + j) * n + k))
        sends = []
        for px, py, pb, k, j, sem in plan:
            send = pltpu.make_async_remote_copy(
                src_ref=src_refs[k].at[rows(k, c, j)], dst_ref=dst_refs[k].at[b, rows(k, c, j)],
                send_sem=ici_send.at[sem], recv_sem=ici_recv.at[sem], device_id=(px, py, c), device_id_type=MESH)
            send.start()
            sends.append(send)
        for px, py, pb, k, j, sem in plan:
            landed = dst_refs[k].at[pb, rows(k, c, j)]
            pltpu.make_async_remote_copy(
                src_ref=landed, dst_ref=landed, send_sem=ici_send.at[sem], recv_sem=ici_recv.at[sem],
                device_id=(px, py, c), device_id_type=MESH).wait_recv()
            forward = pltpu.make_async_remote_copy(
                src_ref=landed, dst_ref=landed, send_sem=d2d_send.at[sem], recv_sem=d2d_recv.at[sem],
                device_id=(x, y, 1 - c), device_id_type=MESH)
            forward.start()
            sends.append(forward)
        for px, py, pb, k, j, sem in plan:
            passed = dst_refs[k].at[pb, rows(k, 1 - c, j)]
            pltpu.make_async_remote_copy(
                src_ref=passed, dst_ref=passed, send_sem=d2d_send.at[sem], recv_sem=d2d_recv.at[sem],
                device_id=(x, y, 1 - c), device_id_type=MESH).wait_recv()
        for send in sends:
            send.wait_send()
        for local in copies:
            local.wait()

    vmem_spec = pl.BlockSpec(memory_space=pltpu.VMEM)
    out_shape = tuple(jax.ShapeDtypeStruct((N_CHIP,) + a.shape, a.dtype) for a in blocks)
    return pl.pallas_call(
        body, name=name, in_specs=[vmem_spec] * n, out_specs=tuple([vmem_spec] * n), out_shape=out_shape,
        scratch_shapes=[pltpu.SemaphoreType.DMA((n_sems,)) for _ in range(4)] + [pltpu.SemaphoreType.DMA((n,))],
        compiler_params=pltpu.CompilerParams(vmem_limit_bytes=VMEM_LIMIT),
    )(*blocks)


def _grad_reduce(order, h_t, wins, gw_out, small):
    s = h_t.shape[1]
    tk = GRAD_ROWS
    n_i = s // tk
    half = D_MODEL // 2
    o_half = OUT_ROWS // 2
    n_rel = N_CHIP - 1

    def body(order_ref, ht_ref, win_ref, gwo_ref, small_ref,
             win_out, wout_out, small_out,
             acc, mine, s1, r1, s2, r2, so1, ro1, so2, ro2, pair_in, pair_o, small_land,
             s1_send, s1_recv, s2_send, s2_recv, o1_send, o1_recv, o2_send, o2_recv,
             pair_send, pair_recv, small_send, small_recv):
        j = pl.program_id(0)
        i = pl.program_id(1)
        x, y, c = _position()
        me = 4 * x + 2 * y + c
        sibling = (x, y, 1 - c)
        my_rows = pl.ds(pl.multiple_of(c * half, half), half)
        sib_rows = pl.ds(pl.multiple_of((1 - c) * half, half), half)

        def chip_of(rel):
            return x ^ (rel >> 1), y ^ (rel & 1)

        def level1(k):
            return pltpu.make_async_remote_copy(src_ref=s1.at[k], dst_ref=r1.at[k], send_sem=s1_send.at[k],
                                                recv_sem=s1_recv.at[k], device_id=sibling, device_id_type=MESH)

        def level2(k):
            px, py = chip_of(RELATIONS[k])
            return pltpu.make_async_remote_copy(src_ref=s2.at[k], dst_ref=r2.at[k], send_sem=s2_send.at[k],
                                                recv_sem=s2_recv.at[k], device_id=(px, py, c), device_id_type=MESH)

        def out_level1(bk):
            return pltpu.make_async_remote_copy(src_ref=so1.at[bk], dst_ref=ro1.at[bk], send_sem=o1_send.at[bk],
                                                recv_sem=o1_recv.at[bk], device_id=sibling, device_id_type=MESH)

        def out_level2(k):
            px, py = chip_of(RELATIONS[k])
            return pltpu.make_async_remote_copy(src_ref=so2.at[k], dst_ref=ro2.at[k], send_sem=o2_send.at[k],
                                                recv_sem=o2_recv.at[k], device_id=(px, py, c), device_id_type=MESH)

        def small_copy(d):
            px, py, pc = x ^ (d >> 2), y ^ ((d >> 1) & 1), c ^ (d & 1)
            return pltpu.make_async_remote_copy(src_ref=small_ref, dst_ref=small_land.at[me],
                                                send_sem=small_send.at[d], recv_sem=small_recv.at[d],
                                                device_id=(px, py, pc), device_id_type=MESH)

        def pair_copy(k, buf):
            return pltpu.make_async_remote_copy(src_ref=buf.at[0], dst_ref=buf.at[1], send_sem=pair_send.at[k],
                                                recv_sem=pair_recv.at[k], device_id=sibling, device_id_type=MESH)

        def out_rows(bk, core):
            return pl.ds(pl.multiple_of(bk * OUT_ROWS + core * o_half, o_half), o_half)

        @pl.when((j == 0) & (i == 0))
        def _():
            for d in range(1, N_DEV):
                small_copy(d).start()
            small_land[me] = small_ref[...]
            for bk in range(N_CHIP):
                so1[bk] = gwo_ref[out_rows(bk, 1 - c), :].astype(BF16)
                out_level1(bk).start()

        @pl.when((j == 0) & (i == 1))
        def _():
            b = 2 * x + y
            for bk in range(N_CHIP):
                out_level1(bk).wait_recv()
            for k in range(n_rel):
                px, py = chip_of(RELATIONS[k])
                bk = 2 * px + py
                so2[k] = (gwo_ref[out_rows(bk, c), :] + ro1[bk].astype(F32)).astype(BF16)
                out_level2(k).start()

        @pl.when(i == 0)
        def _():
            acc[...] = jnp.zeros_like(acc)

        for n0 in range(0, WIN, ACC_COLS):
            n1 = min(n0 + ACC_COLS, WIN)
            acc[:, n0:n1] += _dot(ht_ref[...], win_ref[:, n0:n1])

        for k in range(N_CHIP):
            @pl.when((j == k) & (i == n_i - 1))
            def _(k=k):
                s1[k] = acc[sib_rows, :].astype(BF16)
                level1(k).start()
                mine[...] = acc[my_rows, :]

            if k < n_rel:
                @pl.when((j == k + 1) & (i == 1))
                def _(k=k):
                    level1(k).wait_recv()
                    s2[k] = (mine[...] + r1[k].astype(F32)).astype(BF16)
                    level2(k).start()

        @pl.when((j == N_CHIP - 1) & (i == n_i - 1))
        def _():
            b = 2 * x + y
            level1(N_CHIP - 1).wait_recv()
            total = mine[...] + r1[N_CHIP - 1].astype(F32)
            for k in range(n_rel):
                level2(k).wait_recv()
                total = total + r2[k].astype(F32)
            total = total.T
            pair_in[0] = total
            pair_copy(0, pair_in).start()
            total_o = gwo_ref[out_rows(b, c), :] + ro1[b].astype(F32)
            for k in range(n_rel):
                out_level2(k).wait_recv()
                total_o = total_o + ro2[k].astype(F32)
            pair_o[0] = total_o
            pair_copy(1, pair_o).start()
            win_out[c] = total
            wout_out[c] = total_o
            for d in range(1, N_DEV):
                small_copy(d).wait_recv()
            small_out[...] = small_land[...]
            pair_copy(0, pair_in).wait_recv()
            win_out[1 - c] = pair_in[1]
            pair_copy(1, pair_o).wait_recv()
            wout_out[1 - c] = pair_o[1]
            for d in range(1, N_DEV):
                small_copy(d).wait_send()
            for k in range(N_CHIP):
                level1(k).wait_send()
                out_level1(k).wait_send()
            for k in range(n_rel):
                level2(k).wait_send()
                out_level2(k).wait_send()
            pair_copy(0, pair_in).wait_send()
            pair_copy(1, pair_o).wait_send()

    vmem = pl.BlockSpec(memory_space=pltpu.VMEM)
    dma = pltpu.SemaphoreType.DMA
    sds = jax.ShapeDtypeStruct
    grid_spec = pltpu.PrefetchScalarGridSpec(
        num_scalar_prefetch=1, grid=(N_CHIP, n_i),
        in_specs=[pl.BlockSpec((D_MODEL, tk), lambda j, i, order: (0, i)),
                  pl.BlockSpec((None, tk, WIN), lambda j, i, order: (order[j], i, 0)), vmem, vmem],
        out_specs=(vmem, vmem, vmem),
        scratch_shapes=[
            pltpu.VMEM((D_MODEL, WIN), F32), pltpu.VMEM((half, WIN), F32),
            pltpu.VMEM((N_CHIP, half, WIN), BF16), pltpu.VMEM((N_CHIP, half, WIN), BF16),
            pltpu.VMEM((n_rel, half, WIN), BF16), pltpu.VMEM((n_rel, half, WIN), BF16),
            pltpu.VMEM((N_CHIP, o_half, D_MODEL), BF16), pltpu.VMEM((N_CHIP, o_half, D_MODEL), BF16),
            pltpu.VMEM((n_rel, o_half, D_MODEL), BF16), pltpu.VMEM((n_rel, o_half, D_MODEL), BF16),
            pltpu.VMEM((2, WIN, half), F32), pltpu.VMEM((2, o_half, D_MODEL), F32),
            pltpu.VMEM((N_DEV, PACK_ROWS, D_MODEL), F32),
            dma((N_CHIP,)), dma((N_CHIP,)), dma((n_rel,)), dma((n_rel,)),
            dma((N_CHIP,)), dma((N_CHIP,)), dma((n_rel,)), dma((n_rel,)),
            dma((2,)), dma((2,)), dma((N_DEV,)), dma((N_DEV,))])
    return pl.pallas_call(
        body, name="grad_w_in_reduce", grid_spec=grid_spec,
        out_shape=(sds((2, WIN, half), F32), sds((2, o_half, D_MODEL), F32), sds((N_DEV, PACK_ROWS, D_MODEL), F32)),
        compiler_params=pltpu.CompilerParams(dimension_semantics=("arbitrary", "arbitrary"),
                                             vmem_limit_bytes=VMEM_LIMIT),
    )(order, h_t, wins, gw_out, small)


ADAM_STEPS = 4


def _adamw_math(w, g, m, v):
    m = ADAM_B1 * m + (1.0 - ADAM_B1) * g
    v = ADAM_B2 * v + (1.0 - ADAM_B2) * (g * g)
    m_hat = m / (1.0 - ADAM_B1 ** ADAM_STEP)
    v_hat = v / (1.0 - ADAM_B2 ** ADAM_STEP)
    delta = -ADAM_LR * (m_hat / (jnp.sqrt(v_hat) + ADAM_EPS) + ADAM_WD * w)
    return delta, m, v


def _adamw(w, g, m, v, name):
    r, c = w.shape

    def body(w_ref, g_ref, m_ref, v_ref, d_ref, nm_ref, nv_ref):
        delta, nm, nv = _adamw_math(w_ref[...], g_ref[...], m_ref[...], v_ref[...])
        d_ref[...] = delta
        nm_ref[...] = nm
        nv_ref[...] = nv

    rows = r // ADAM_STEPS
    assert rows * ADAM_STEPS == r and rows % 8 == 0
    spec = pl.BlockSpec((rows, c), lambda i: (i, 0))
    shape = jax.ShapeDtypeStruct((r, c), F32)
    return pl.pallas_call(
        body, name=name, grid=(ADAM_STEPS,), in_specs=[spec] * 4, out_specs=(spec,) * 3,
        out_shape=(shape,) * 3, compiler_params=pltpu.CompilerParams(vmem_limit_bytes=VMEM_LIMIT),
    )(w, g, m, v)


PACK_ROWS = 8


def _fold_heads(v):
    y = v[:, 0:LANES]
    for j in range(1, v.shape[1] // LANES):
        y = y + v[:, j * LANES:(j + 1) * LANES]
    return y + pltpu.roll(y, HEAD_DIM, 1)


def _small_adamw(recv, w_p, m_p, v_p):
    def body(r_ref, w_ref, m_ref, v_ref, g_ref, d_ref, nm_ref, nv_ref, loss_ref):
        tot = r_ref[0]
        for j in range(1, N_DEV):
            tot = tot + r_ref[j]
        loss_ref[...] = tot[3:4, 0:LANES]
        row1 = tot[1:2, :]
        row2 = tot[2:3, :]
        pieces = [_fold_heads(row1[:, 0:WIDTH]), _fold_heads(row2[:, WIDTH:WIDTH + A_KV_WIDTH]),
                  _fold_heads(row1[:, WIDTH:2 * WIDTH]), _fold_heads(row2[:, 0:WIDTH]),
                  row2[:, WIDTH + A_KV_WIDTH:WIDTH + 2 * A_KV_WIDTH], jnp.zeros((1, 3 * LANES), F32)]
        g = jnp.concatenate([tot[0:1, :], jnp.concatenate(pieces, axis=1), jnp.zeros((PACK_ROWS - 2, D_MODEL), F32)],
                            axis=0)
        g_ref[...] = g
        delta, nm, nv = _adamw_math(w_ref[...], g, m_ref[...], v_ref[...])
        d_ref[...] = delta
        nm_ref[...] = nm
        nv_ref[...] = nv

    shape = jax.ShapeDtypeStruct((PACK_ROWS, D_MODEL), F32)
    return pl.pallas_call(body, name="small_adamw",
                          out_shape=(shape,) * 4 + (jax.ShapeDtypeStruct((1, LANES), F32),))(recv, w_p, m_p, v_p)


def _pack_small(norm_gain, q_a, k_a, q_b, k_b, sinks):
    def lane_pad(a):
        return jnp.pad(a, ((0, 0), (0, LANES - a.shape[1])))
    row1 = jnp.concatenate([lane_pad(q_a), lane_pad(k_a), lane_pad(q_b), lane_pad(k_b), lane_pad(sinks),
                            jnp.zeros((1, 3 * LANES), F32)], axis=1)
    return jnp.concatenate([norm_gain, row1, jnp.zeros((PACK_ROWS - 2, D_MODEL), F32)], axis=0)


def _unpack_small(p):
    return (p[0:1, :], p[1:2, 0:HEAD_DIM], p[1:2, LANES:LANES + HEAD_DIM], p[1:2, 2 * LANES:2 * LANES + HEAD_DIM],
            p[1:2, 3 * LANES:3 * LANES + HEAD_DIM], p[1:2, 4 * LANES:4 * LANES + HEADS])


def kernel(x, norm_gain, w_in, q_norm_a, k_norm_a, sinks_a, q_norm_b, k_norm_b, w_out, loss_target, m_norm_gain, m_w_in, m_q_norm_a, m_k_norm_a, m_sinks_a, m_q_norm_b, m_k_norm_b, m_w_out, v_norm_gain, v_w_in, v_q_norm_a, v_k_norm_a, v_sinks_a, v_q_norm_b, v_k_norm_b, v_w_out):
    chip = 2 * lax.axis_index("x") + lax.axis_index("y")

    w_in_t, m_w_in_t, v_w_in_t = w_in[0].T, m_w_in[0].T, v_w_in[0].T

    w_in_all, w_out_all = _gather_weights([w_in_t.astype(BF16), w_out[0].astype(BF16)], "gather_weights")
    w_in_bf = w_in_all.reshape(IN_WIDTH, D_MODEL)
    w_out_bf = w_out_all.reshape(D_MODEL, D_MODEL)

    loss_part, gx, h_t, wins, gw_out, (dgain, dgqa, dgka, dsink, dgqb, dgkb) = _local_step(
        x[0], loss_target[0], norm_gain, w_in_bf, q_norm_a, k_norm_a, sinks_a, q_norm_b, k_norm_b, w_out_bf)

    small = jnp.concatenate([
        dgain, jnp.concatenate([dgqa, dgqb], axis=1),
        jnp.concatenate([dgkb, dgka, dsink, jnp.zeros((1, D_MODEL - WIDTH - 2 * A_KV_WIDTH), F32)], axis=1),
        jnp.pad(loss_part, ((0, 0), (0, D_MODEL - LANES))),
        jnp.zeros((PACK_ROWS - 4, D_MODEL), F32)], axis=0)
    order = (chip ^ jnp.array(RELATIONS, jnp.int32)).astype(jnp.int32)
    win_sum, wout_sum, small_recv = _grad_reduce(order, h_t, wins, gw_out, small)
    shift = jnp.array(WIN_SHIFT, jnp.int32)[chip]
    g_w_in_t = lax.dynamic_slice_in_dim(win_sum.transpose(1, 0, 2).reshape(WIN, D_MODEL), shift, IN_COLS, axis=0)
    g_w_out = wout_sum.reshape(OUT_ROWS, D_MODEL)

    d_w_in, nm_w_in, nv_w_in = (a.T for a in _adamw(w_in_t, g_w_in_t, m_w_in_t, v_w_in_t, "adamw_w_in"))
    g_w_in = g_w_in_t.T
    d_w_out, nm_w_out, nv_w_out = _adamw(w_out[0], g_w_out, m_w_out[0], v_w_out[0], "adamw_w_out")
    g_s, d_s, nm_s, nv_s, loss_row = _small_adamw(
        small_recv,
        _pack_small(norm_gain, q_norm_a, k_norm_a, q_norm_b, k_norm_b, sinks_a),
        _pack_small(m_norm_gain, m_q_norm_a, m_k_norm_a, m_q_norm_b, m_k_norm_b, m_sinks_a),
        _pack_small(v_norm_gain, v_q_norm_a, v_k_norm_a, v_q_norm_b, v_k_norm_b, v_sinks_a))
    loss = loss_row[0, 0]

    def leaves(small_packed, big_in, big_out):
        gain, qa, ka, qb, kb, sk = _unpack_small(small_packed)
        return (gain, big_in[None], qa, ka, sk, qb, kb, big_out[None])

    return ((loss, gx[None]) + leaves(g_s, g_w_in, g_w_out) + leaves(d_s, d_w_in, d_w_out)
            + leaves(nm_s, nm_w_in, nm_w_out) + leaves(nv_s, nv_w_in, nv_w_out))
```

```python
import jax
import jax.numpy as jnp
from jax import lax
from jax.experimental import pallas as pl
from jax.experimental.pallas import tpu as pltpu

F32 = jnp.float32
BF16 = jnp.bfloat16

D_MODEL = 1024
HEAD_DIM = 64
HEADS = 8
WIDTH = HEADS * HEAD_DIM
A_KV_WIDTH = 2 * HEAD_DIM
BLOCK = 128
LANES = 128
FOLD = 16
A_MAX_DIST = 127
B_MAX_DIST = 128
ROPE_THETA = 10000.0
EPS = 1e-6
NEG = -1e30
SCALE = HEAD_DIM ** -0.5

IN_WIDTH = 3328
C_QA, C_KA, C_VA, C_GA, C_QB, C_KB, C_VB, C_GB, C_END = 0, 512, 640, 768, 1280, 1792, 2304, 2816, 3328

N_DEV = 8
N_CHIP = 4
MESH = pl.DeviceIdType.MESH
IN_COLS = IN_WIDTH // N_CHIP
WIN = 896
WIN_START = (0, 768, 1664, 2432)
WIN_SHIFT = (0, 64, 0, 64)
OUT_ROWS = D_MODEL // N_CHIP
RELATIONS = (3, 1, 2, 0)

ADAM_LR = 0.001
ADAM_B1 = 0.9
ADAM_B2 = 0.999
ADAM_EPS = 1e-08
ADAM_WD = 0.01
ADAM_STEP = 10

ROW_TILE = 256
FOLD_ROWS = ROW_TILE // FOLD
GRAD_ROWS = 256
ACC_COLS = 256
VMEM_LIMIT = 56 * 1024 * 1024


def _dot(a, b):
    return jnp.dot(a, b, preferred_element_type=F32)


def _dot_nt(a, b):
    return lax.dot_general(a, b, (((1,), (1,)), ((), ())), preferred_element_type=F32)


def _dot_tn(a, b):
    return lax.dot_general(a, b, (((0,), (0,)), ((), ())), preferred_element_type=F32)


def _head_sum(z, bd):
    w = bd.shape[0]
    zb = z.astype(BF16)
    parts = [_dot(zb[:, a:a + w], bd) for a in range(0, z.shape[1], w)]
    return parts[0] if len(parts) == 1 else jnp.concatenate(parts, axis=1)


def _swap_halves(t):
    w = t.shape[1]
    lane = lax.broadcasted_iota(jnp.int32, t.shape, 1)
    return jnp.where(lane % HEAD_DIM < HEAD_DIM // 2, pltpu.roll(t, w - 32, 1), pltpu.roll(t, 32, 1))


def _qknorm_rope(t, g, cos, sin_s, bd):
    r = lax.rsqrt(_head_sum(t * t, bd) * (1.0 / HEAD_DIM) + EPS)
    n = (t * r) * g
    return n * cos + _swap_halves(n) * sin_s


def _qknorm_rope_bwd(dout, t, g, cos, sin_s, bd):
    dout, t = dout.astype(F32), t.astype(F32)
    dn = dout * cos + _swap_halves(dout * sin_s)
    r = lax.rsqrt(_head_sum(t * t, bd) * (1.0 / HEAD_DIM) + EPS)
    tr = t * r
    u = dn * g
    dt = r * (u - tr * (_head_sum(u * tr, bd) * (1.0 / HEAD_DIM)))
    return dt, dn * tr


def _sigmoid(g):
    return 1.0 / (1.0 + jnp.exp(-g))


def _expand_heads(st):
    t = st.shape[0]
    lane = lax.broadcasted_iota(jnp.int32, (t, LANES), 1)
    chunks = []
    for c in range(WIDTH // LANES):
        chunks.append(jnp.where(lane < HEAD_DIM, st[:, 2 * c:2 * c + 1], st[:, 2 * c + 1:2 * c + 2]))
    return jnp.concatenate(chunks, axis=1)


def _reduce_heads(z):
    t = z.shape[0]
    lane = lax.broadcasted_iota(jnp.int32, (t, LANES), 1)
    out = jnp.zeros((t, LANES), F32)
    for c in range(WIDTH // LANES):
        zc = z[:, c * LANES:(c + 1) * LANES]
        for ph in range(2):
            s = jnp.sum(jnp.where((lane // HEAD_DIM) == ph, zc, 0.0), axis=-1, keepdims=True)
            out = jnp.where(lane == 2 * c + ph, s, out)
    return out


def _fold_scratch(w):
    return pltpu.VMEM((w // LANES, ROW_TILE, LANES), F32)


def _store_folded(out_ref, val, scr):
    n = val.shape[1] // LANES
    for c in range(n):
        scr[c] = val[:, c * LANES:(c + 1) * LANES]
    for r in range(FOLD):
        piece = [scr[c, pl.ds(r, FOLD_ROWS, stride=FOLD), :] for c in range(n)]
        out_ref[r] = (piece[0] if n == 1 else jnp.concatenate(piece, axis=1)).astype(out_ref.dtype)


def _load_folded(in_ref, scr):
    n = in_ref.shape[2] // LANES
    for r in range(FOLD):
        blk = in_ref[r].astype(F32)
        for c in range(n):
            scr[c, pl.ds(r, FOLD_ROWS, stride=FOLD), :] = blk[:, c * LANES:(c + 1) * LANES]
    return scr[0] if n == 1 else jnp.concatenate([scr[c] for c in range(n)], axis=1)


def _fold_matrix():
    f = jnp.arange(ROW_TILE)
    return (jnp.arange(ROW_TILE)[None, :] == (FOLD * (f % FOLD_ROWS) + f // FOLD_ROWS)[:, None]).astype(BF16)


def _store_folded_bf16(out_ref, val, perm):
    folded = _dot(perm, val.astype(BF16)).astype(out_ref.dtype)
    for r in range(FOLD):
        out_ref[r] = folded[r * FOLD_ROWS:(r + 1) * FOLD_ROWS]


def _load_folded_bf16(in_ref, perm):
    blk = jnp.concatenate([in_ref[r] for r in range(FOLD)], axis=0)
    return _dot(perm, blk)


def _rows(w, tm=ROW_TILE):
    return pl.BlockSpec((tm, w), lambda i: (i, 0))


def _folded_rows(w):
    return pl.BlockSpec((FOLD, FOLD_ROWS, w), lambda i: (0, i, 0))


def _whole(shape):
    return pl.BlockSpec(shape, lambda i: (0,) * len(shape))


def _inproj(x2, gain, w_bf, cos, sin_s, gqa, gka, gqb, gkb, bd256, bd128, perm):
    s = x2.shape[0]
    tm = ROW_TILE

    def body(x_ref, gain_ref, w_hbm, cos_ref, sin_ref, gqa_ref, gka_ref, gqb_ref, gkb_ref, bd256_ref, bd128_ref,
             perm_ref, qa_ref, ka_ref, va_ref, qb_ref, kb_ref, vb_ref, qbf_ref, kbf_ref, vbf_ref,
             qa_raw_ref, ka_raw_ref, ga_ref, qb_raw_ref, kb_raw_ref, gb_ref, w_vmem):
        @pl.when(pl.program_id(0) == 0)
        def _():
            pltpu.sync_copy(w_hbm, w_vmem)

        perm = perm_ref[...]
        xt = x_ref[...]
        r = lax.rsqrt(jnp.mean(xt * xt, axis=-1, keepdims=True) + EPS)
        h = ((xt * r) * gain_ref[...]).astype(BF16)
        cos1 = cos_ref[...]
        sin1 = sin_ref[...]
        cos4 = jnp.tile(cos1, (1, 4))
        sin4 = jnp.tile(sin1, (1, 4))

        def seg(a, b):
            return _dot_nt(h, w_vmem[a:b, :])

        t = seg(C_QA, C_KA)
        qa_raw_ref[...] = t.astype(BF16)
        qa_ref[...] = (_qknorm_rope(t, gqa_ref[...], cos4, sin4, bd256_ref[...]) * SCALE).astype(BF16)
        t = seg(C_KA, C_VA)
        ka_raw_ref[...] = t.astype(BF16)
        ka_ref[...] = _qknorm_rope(t, gka_ref[...], cos1, sin1, bd128_ref[...]).astype(BF16)
        va_ref[...] = seg(C_VA, C_GA).astype(BF16)
        ga_ref[...] = seg(C_GA, C_QB).astype(BF16)
        t = seg(C_QB, C_KB)
        qb_raw_ref[...] = t.astype(BF16)
        t = _qknorm_rope(t, gqb_ref[...], cos4, sin4, bd256_ref[...]) * SCALE
        qb_ref[...] = t.astype(BF16)
        _store_folded_bf16(qbf_ref, t, perm)
        t = seg(C_KB, C_VB)
        kb_raw_ref[...] = t.astype(BF16)
        t = _qknorm_rope(t, gkb_ref[...], cos4, sin4, bd256_ref[...])
        kb_ref[...] = t.astype(BF16)
        _store_folded_bf16(kbf_ref, t, perm)
        t = seg(C_VB, C_GB)
        vb_ref[...] = t.astype(BF16)
        _store_folded_bf16(vbf_ref, t, perm)
        gb_ref[...] = seg(C_GB, C_END).astype(BF16)

    sds = jax.ShapeDtypeStruct
    folded = sds((FOLD, s // FOLD, WIDTH), BF16)
    out_shape = (sds((s, WIDTH), BF16), sds((s, A_KV_WIDTH), BF16), sds((s, A_KV_WIDTH), BF16),
                 sds((s, WIDTH), BF16), sds((s, WIDTH), BF16), sds((s, WIDTH), BF16), folded, folded, folded,
                 sds((s, WIDTH), BF16), sds((s, A_KV_WIDTH), BF16), sds((s, WIDTH), BF16),
                 sds((s, WIDTH), BF16), sds((s, WIDTH), BF16), sds((s, WIDTH), BF16))
    out_specs = (_rows(WIDTH), _rows(A_KV_WIDTH), _rows(A_KV_WIDTH), _rows(WIDTH), _rows(WIDTH), _rows(WIDTH),
                 _folded_rows(WIDTH), _folded_rows(WIDTH), _folded_rows(WIDTH),
                 _rows(WIDTH), _rows(A_KV_WIDTH), _rows(WIDTH), _rows(WIDTH), _rows(WIDTH), _rows(WIDTH))
    return pl.pallas_call(
        body, name="inproj_fwd", grid=(s // tm,),
        in_specs=[_rows(D_MODEL), _whole(gain.shape), pl.BlockSpec(memory_space=pl.ANY), _rows(LANES), _rows(LANES),
                  _whole(gqa.shape), _whole(gka.shape), _whole(gqb.shape), _whole(gkb.shape), _whole(bd256.shape),
                  _whole(bd128.shape), _whole(perm.shape)],
        out_specs=out_specs, out_shape=out_shape,
        scratch_shapes=[pltpu.VMEM((IN_WIDTH, D_MODEL), BF16)],
        compiler_params=pltpu.CompilerParams(dimension_semantics=("arbitrary",), vmem_limit_bytes=VMEM_LIMIT),
    )(x2, gain, w_bf, cos, sin_s, gqa, gka, gqb, gkb, bd256, bd128, perm)


def _seq_pos(idx, dil):
    if dil == 4:
        return 4 * (idx % 32) + idx // 32
    return idx


SOFTMAX_ROWS = 64


def _upper_mask(dil, r0=0, rows=2 * BLOCK):
    qi = (lax.broadcasted_iota(jnp.int32, (rows, BLOCK), 0) + r0) % BLOCK
    kj = lax.broadcasted_iota(jnp.int32, (rows, BLOCK), 1)
    return _seq_pos(kj, dil) > _seq_pos(qi, dil)


def _eye_mask(r0=0, rows=2 * BLOCK):
    qi = (lax.broadcasted_iota(jnp.int32, (rows, BLOCK), 0) + r0) % BLOCK
    kj = lax.broadcasted_iota(jnp.int32, (rows, BLOCK), 1)
    return qi == kj


def _stack_heads(a2, c, gqa):
    lane = lax.broadcasted_iota(jnp.int32, (1, LANES), 1) // HEAD_DIM
    zero = jnp.zeros_like(a2)
    if gqa:
        keep = lane == (c // 2)
        return jnp.concatenate([jnp.where(keep, a2, zero), jnp.where(keep, _swap_heads(a2), zero)], axis=0)
    return jnp.concatenate([jnp.where(lane == 0, a2, zero), jnp.where(lane == 1, a2, zero)], axis=0)


def _unstack_heads(a, c, gqa):
    lane = lax.broadcasted_iota(jnp.int32, (1, LANES), 1) // HEAD_DIM
    if gqa:
        return jnp.where(lane == (c // 2), a[:BLOCK], _swap_heads(a[BLOCK:]))
    return jnp.where(lane == 0, a[:BLOCK], a[BLOCK:])


def _stacked_head_ids(c, gqa):
    if gqa:
        return 2 * c + c // 2, 2 * c + 1 - c // 2
    return 2 * c, 2 * c + 1


def _per_head_rows(blk, heads):
    return jnp.concatenate([blk[:, heads[0]:heads[0] + 1], blk[:, heads[1]:heads[1] + 1]], axis=0)


def _attn_view(a, dil):
    if dil == 1:
        return a[None]
    if dil == 4:
        return a.reshape(4, 4, a.shape[1], a.shape[2])
    return a


def _attn_unview(a, dil):
    if dil == 1:
        return a[0]
    if dil == 4:
        return a.reshape(FOLD, a.shape[2], a.shape[3])
    return a


def _attn_specs(dil, nb):
    if dil == 4:
        def spec(fn):
            return lambda w: pl.BlockSpec((4, None, BLOCK // 4, w), lambda r, i: (0, r, fn(i), 0))
    else:
        def spec(fn):
            return lambda w: pl.BlockSpec((None, BLOCK, w), lambda r, i: (r, fn(i), 0))
    return spec


def _blk_load(ref, sl, dil):
    if dil == 4:
        return ref[:, :, sl].reshape(BLOCK, sl.stop - sl.start)
    return ref[:, sl]


def _blk_store(ref, sl, val, dil):
    val = val.astype(ref.dtype)
    if dil == 4:
        ref[:, :, sl] = val.reshape(4, BLOCK // 4, sl.stop - sl.start)
    else:
        ref[:, sl] = val


def _swap_heads(a):
    return pltpu.roll(a.astype(F32), HEAD_DIM, 1).astype(a.dtype)


def _attn_fwd(q, k, v, sinks, *, dil, max_dist, name):
    q, k, v = _attn_view(q, dil), _attn_view(k, dil), _attn_view(v, dil)
    kw = k.shape[-1]
    gqa = kw == A_KV_WIDTH
    n_seq = dil
    nb = (q.shape[-2] * (4 if dil == 4 else 1)) // BLOCK
    with_sinks = sinks is not None
    all_lanes = slice(0, LANES)
    assert max_dist in (BLOCK - 1, BLOCK)
    diag = max_dist == BLOCK

    def body(*refs):
        if with_sinks:
            q_ref, kp_ref, kc_ref, vp_ref, vc_ref, sink_ref, o_ref, m_ref, l_ref = refs
        else:
            q_ref, kp_ref, kc_ref, vp_ref, vc_ref, o_ref, m_ref, l_ref = refs

        def block(has_prev):
            lane = lax.broadcasted_iota(jnp.int32, (1, LANES), 1)
            with_diag = diag and has_prev
            upper, eye = _upper_mask(dil), _eye_mask()
            first_rows = lax.broadcasted_iota(jnp.int32, (2 * BLOCK, 1), 0) < BLOCK
            m_blk = jnp.zeros((BLOCK, LANES), F32)
            l_blk = jnp.ones((BLOCK, LANES), F32)
            chunks = range(WIDTH // LANES)
            scores, values = [], []
            for c in chunks:
                sl = slice(c * LANES, (c + 1) * LANES)
                ksl = slice(0, LANES) if gqa else sl
                kcur, vcur = _blk_load(kc_ref, ksl, dil), _blk_load(vc_ref, ksl, dil)
                qs = _stack_heads(_blk_load(q_ref, sl, dil), c, gqa)
                if has_prev:
                    kcur = jnp.concatenate([_blk_load(kp_ref, ksl, dil), kcur], axis=0)
                    vcur = jnp.concatenate([_blk_load(vp_ref, ksl, dil), vcur], axis=0)
                scores.append(_dot_nt(qs, kcur))
                values.append(vcur)
            probs = []
            for c in chunks:
                heads = _stacked_head_ids(c, gqa)
                s = scores[c]
                if has_prev:
                    s_p = s[:, :BLOCK]
                    sc = jnp.where(upper, s_p, s[:, BLOCK:])
                else:
                    sc = jnp.where(upper, NEG, s)
                if with_diag:
                    sd = jnp.where(eye, s_p, NEG)
                    m = jnp.max(jnp.maximum(sc, sd), axis=-1, keepdims=True)
                else:
                    m = jnp.max(sc, axis=-1, keepdims=True)
                if with_sinks:
                    sk = jnp.where(first_rows, sink_ref[0, heads[0]], sink_ref[0, heads[1]])
                    m = jnp.maximum(m, sk)
                p = jnp.exp(sc - m)
                zero = jnp.zeros_like(p)
                if with_diag:
                    pd = jnp.exp(sd - m)
                    l = jnp.sum(p + pd, axis=-1, keepdims=True)
                else:
                    pd = zero
                    l = jnp.sum(p, axis=-1, keepdims=True)
                if with_sinks:
                    l = l + jnp.exp(sk - m)
                pf = jnp.where(upper, zero, p)
                if has_prev:
                    pf = jnp.concatenate([jnp.where(upper, p, pd), pf], axis=1)
                probs.append(pf.astype(BF16))
                for n, h in enumerate(heads):
                    rows = slice(n * BLOCK, (n + 1) * BLOCK)
                    m_blk = jnp.where(lane == h, m[rows], m_blk)
                    l_blk = jnp.where(lane == h, l[rows], l_blk)
            for c in chunks:
                sl = slice(c * LANES, (c + 1) * LANES)
                _blk_store(o_ref, sl, _unstack_heads(_dot(probs[c], values[c]), c, gqa), dil)
            _blk_store(m_ref, all_lanes, m_blk, dil)
            _blk_store(l_ref, all_lanes, l_blk, dil)

        @pl.when(pl.program_id(1) == 0)
        def _():
            block(False)

        @pl.when(pl.program_id(1) > 0)
        def _():
            block(True)

    spec = _attn_specs(dil, nb)
    cur = spec(lambda i: i)
    prev = spec(lambda i: jnp.maximum(i - 1, 0))
    in_specs = [cur(WIDTH), prev(kw), cur(kw), prev(kw), cur(kw)]
    args = [q, k, k, v, v]
    if with_sinks:
        in_specs.append(pl.BlockSpec(memory_space=pltpu.SMEM))
        args.append(sinks)
    stats = jax.ShapeDtypeStruct(q.shape[:-1] + (LANES,), F32)
    o, m, l = pl.pallas_call(
        body, name=name, grid=(n_seq, nb), in_specs=in_specs,
        out_specs=(cur(WIDTH), cur(LANES), cur(LANES)),
        out_shape=(jax.ShapeDtypeStruct(q.shape, BF16), stats, stats),
        compiler_params=pltpu.CompilerParams(dimension_semantics=("arbitrary", "arbitrary")),
    )(*args)
    return _attn_unview(o, dil), _attn_unview(m, dil), _attn_unview(l, dil)


def _attn_bwd(q, k, v, do, lse, delta, *, dil, max_dist, name):
    q, k, v, do, lse, delta = (_attn_view(a, dil) for a in (q, k, v, do, lse, delta))
    kw = k.shape[-1]
    gqa = kw == A_KV_WIDTH
    n_seq = dil
    nb = (q.shape[-2] * (4 if dil == 4 else 1)) // BLOCK
    n_kc = kw // LANES
    all_lanes = slice(0, LANES)
    assert max_dist in (BLOCK - 1, BLOCK)
    diag = max_dist == BLOCK

    def body(q_ref, kp_ref, kc_ref, vp_ref, vc_ref, do_ref, lse_ref, dl_ref, dq_ref, dk_ref, dv_ref, ck_ref, cv_ref):
        i = pl.program_id(1)

        def block(has_prev):
            upper, eye = _upper_mask(dil), _eye_mask()
            lse_blk = _blk_load(lse_ref, all_lanes, dil)
            dl_blk = _blk_load(dl_ref, all_lanes, dil)
            dk_acc = [None] * n_kc
            dv_acc = [None] * n_kc
            chunks = range(WIDTH // LANES)
            operands, products = [], []
            for c in chunks:
                sl = slice(c * LANES, (c + 1) * LANES)
                kc = 0 if gqa else c
                ksl = slice(kc * LANES, (kc + 1) * LANES)
                k2, v2 = _blk_load(kc_ref, ksl, dil), _blk_load(vc_ref, ksl, dil)
                if has_prev:
                    k2 = jnp.concatenate([_blk_load(kp_ref, ksl, dil), k2], axis=0)
                    v2 = jnp.concatenate([_blk_load(vp_ref, ksl, dil), v2], axis=0)
                qs = _stack_heads(_blk_load(q_ref, sl, dil), c, gqa)
                dos = _stack_heads(_blk_load(do_ref, sl, dil), c, gqa)
                operands.append((qs, dos, k2))
                products.append((_dot_nt(qs, k2), _dot_nt(dos, v2)))
            weights = []
            for c in chunks:
                heads = _stacked_head_ids(c, gqa)
                lse2 = _per_head_rows(lse_blk, heads)
                dl2 = _per_head_rows(dl_blk, heads)
                s, dp = products[c]
                if has_prev:
                    s_p, dp_p = s[:, :BLOCK], dp[:, :BLOCK]
                    sc = jnp.where(upper, s_p, s[:, BLOCK:])
                    dpc = jnp.where(upper, dp_p, dp[:, BLOCK:])
                else:
                    sc = jnp.where(upper, NEG, s)
                    dpc = dp
                p = jnp.exp(sc - lse2)
                ds = p * (dpc - dl2)
                zero = jnp.zeros_like(p)
                pf = jnp.where(upper, zero, p)
                dsf = jnp.where(upper, zero, ds)
                if has_prev:
                    if diag:
                        pd = jnp.exp(jnp.where(eye, s_p, NEG) - lse2)
                        dsd = pd * (dp_p - dl2)
                    else:
                        pd = dsd = zero
                    pf = jnp.concatenate([jnp.where(upper, p, pd), pf], axis=1)
                    dsf = jnp.concatenate([jnp.where(upper, ds, dsd), dsf], axis=1)
                weights.append((pf.astype(BF16), dsf.astype(BF16)))
            for c in chunks:
                sl = slice(c * LANES, (c + 1) * LANES)
                kc = 0 if gqa else c
                qs, dos, k2 = operands[c]
                pf, dsf = weights[c]
                _blk_store(dq_ref, sl, _unstack_heads(_dot(dsf, k2), c, gqa) * SCALE, dil)
                dk2 = _dot_tn(dsf, qs)
                dv2 = _dot_tn(pf, dos)
                dk_acc[kc] = dk2 if dk_acc[kc] is None else dk_acc[kc] + dk2
                dv_acc[kc] = dv2 if dv_acc[kc] is None else dv_acc[kc] + dv2
            for kc in range(n_kc):
                sl = slice(kc * LANES, (kc + 1) * LANES)
                if has_prev:
                    _blk_store(dk_ref, sl, ck_ref[:, sl] + dk_acc[kc][:BLOCK], dil)
                    _blk_store(dv_ref, sl, cv_ref[:, sl] + dv_acc[kc][:BLOCK], dil)
                    ck_ref[:, sl] = dk_acc[kc][BLOCK:]
                    cv_ref[:, sl] = dv_acc[kc][BLOCK:]
                else:
                    ck_ref[:, sl] = dk_acc[kc]
                    cv_ref[:, sl] = dv_acc[kc]

        @pl.when(i == 0)
        def _():
            block(False)

        @pl.when((i > 0) & (i < nb))
        def _():
            block(True)

        @pl.when(i == nb)
        def _():
            for kc in range(n_kc):
                sl = slice(kc * LANES, (kc + 1) * LANES)
                _blk_store(dk_ref, sl, ck_ref[:, sl], dil)
                _blk_store(dv_ref, sl, cv_ref[:, sl], dil)

    spec = _attn_specs(dil, nb)
    cur = spec(lambda i: jnp.minimum(i, nb - 1))
    prev = spec(lambda i: jnp.clip(i - 1, 0, nb - 1))
    lag = spec(lambda i: jnp.maximum(i - 1, 0))
    sds = jax.ShapeDtypeStruct
    dq, dk, dv = pl.pallas_call(
        body, name=name, grid=(n_seq, nb + 1),
        in_specs=[cur(WIDTH), prev(kw), cur(kw), prev(kw), cur(kw), cur(WIDTH), cur(LANES), cur(LANES)],
        out_specs=(cur(WIDTH), lag(kw), lag(kw)),
        out_shape=(sds(q.shape, BF16), sds(k.shape, BF16), sds(k.shape, BF16)),
        scratch_shapes=[pltpu.VMEM((BLOCK, kw), F32), pltpu.VMEM((BLOCK, kw), F32)],
        compiler_params=pltpu.CompilerParams(dimension_semantics=("arbitrary", "arbitrary")),
    )(q, k, k, v, v, do, lse, delta)
    return _attn_unview(dq, dil), _attn_unview(dk, dil), _attn_unview(dv, dil)


def _outproj(att_a, att_b1, att_b4, att_b16, g_a, g_b, x2, tgt2, w_out_bf, sink_row, perm):
    s = x2.shape[0]
    tm = ROW_TILE

    def body(oa_ref, ma_ref, la_ref, ob1_ref, m1_ref, l1_ref, ob4_ref, m4_ref, l4_ref, ob16_ref, m16_ref, l16_ref,
             ga_ref, gb_ref, x_ref, t_ref, w_ref, sink_ref, perm_ref,
             dy_ref, doa_ref, dob_ref, dobf_ref, dga_ref, dgb_ref, lsea_ref, lseb_ref, lsebf_ref, dla_ref, dlb_ref,
             dlbf_ref, gw_ref, loss_ref, dsink_ref, scr_st):
        i = pl.program_id(0)
        perm = perm_ref[...]

        @pl.when(i == 0)
        def _():
            gw_ref[...] = jnp.zeros_like(gw_ref)
            loss_ref[...] = jnp.zeros_like(loss_ref)
            dsink_ref[...] = jnp.zeros_like(dsink_ref)

        ms = [m1_ref[...], _load_folded(m4_ref, scr_st), _load_folded(m16_ref, scr_st)]
        ls = [l1_ref[...], _load_folded(l4_ref, scr_st), _load_folded(l16_ref, scr_st)]
        mx = jnp.maximum(jnp.maximum(ms[0], ms[1]), ms[2])
        scale = [jnp.exp(mp - mx) for mp in ms]
        den = (ls[0] * scale[0] + ls[1] * scale[1]) + ls[2] * scale[2]
        lane = lax.broadcasted_iota(jnp.int32, (tm, LANES), 1)
        lse_b = jnp.where(lane < HEADS, mx + jnp.log(den), 0.0)
        lseb_ref[...] = lse_b
        _store_folded(lsebf_ref, lse_b, scr_st)
        inv_den = 1.0 / den
        o_b = _expand_heads(scale[0] * inv_den) * ob1_ref[...].astype(F32)
        o_b = o_b + _expand_heads(scale[1] * inv_den) * _load_folded_bf16(ob4_ref, perm)
        o_b = o_b + _expand_heads(scale[2] * inv_den) * _load_folded_bf16(ob16_ref, perm)
        l_a = la_ref[...]
        lse_a = jnp.where(lane < HEADS, ma_ref[...] + jnp.log(l_a), 0.0)
        lsea_ref[...] = lse_a
        o_a = _expand_heads(1.0 / l_a) * oa_ref[...].astype(F32)
        g_a = ga_ref[...].astype(F32)
        g_b = gb_ref[...].astype(F32)
        sg_a = _sigmoid(g_a)
        sg_b = _sigmoid(g_b)
        silu_a = g_a * sg_a
        silu_b = g_b * sg_b
        mixed = jnp.concatenate([o_a * silu_a, o_b * silu_b], axis=1).astype(BF16)
        w = w_ref[...]
        y = x_ref[...] + _dot(mixed, w)
        diff = y - t_ref[...]
        loss_ref[...] += (0.5 / D_MODEL) * jnp.sum(diff * diff)
        dy = diff * (1.0 / D_MODEL)
        dy_ref[...] = dy
        dyb = dy.astype(BF16)
        gw_ref[...] += _dot_tn(mixed, dyb)
        dmixed = _dot_nt(dyb, w)
        dm_a = dmixed[:, :WIDTH]
        dm_b = dmixed[:, WIDTH:]
        do_a = dm_a * silu_a
        do_b = dm_b * silu_b
        doa_ref[...] = do_a.astype(BF16)
        dob_ref[...] = do_b.astype(BF16)
        _store_folded_bf16(dobf_ref, do_b, perm)
        dga_ref[...] = (dm_a * o_a * (sg_a * (1.0 + g_a * (1.0 - sg_a)))).astype(BF16)
        dgb_ref[...] = (dm_b * o_b * (sg_b * (1.0 + g_b * (1.0 - sg_b)))).astype(BF16)
        dl_a = _reduce_heads(do_a * o_a)
        dla_ref[...] = dl_a
        dl_b = _reduce_heads(do_b * o_b)
        dlb_ref[...] = dl_b
        _store_folded(dlbf_ref, dl_b, scr_st)
        dsink_ref[...] -= jnp.sum(jnp.exp(sink_ref[...] - lse_a) * dl_a, axis=0, keepdims=True)

    sds = jax.ShapeDtypeStruct
    ln = s // FOLD
    natural = [_rows(WIDTH), _rows(LANES), _rows(LANES)]
    folded = [_folded_rows(WIDTH), _folded_rows(LANES), _folded_rows(LANES)]
    return pl.pallas_call(
        body, name="outproj_fwd_bwd", grid=(s // tm,),
        in_specs=natural + natural + folded + folded
                 + [_rows(WIDTH), _rows(WIDTH), _rows(D_MODEL), _rows(D_MODEL), _whole((D_MODEL, D_MODEL)),
                    _whole((1, LANES)), _whole(perm.shape)],
        out_specs=(_rows(D_MODEL), _rows(WIDTH), _rows(WIDTH), _folded_rows(WIDTH), _rows(WIDTH), _rows(WIDTH),
                   _rows(LANES), _rows(LANES), _folded_rows(LANES), _rows(LANES), _rows(LANES), _folded_rows(LANES),
                   _whole((D_MODEL, D_MODEL)), _whole((1, LANES)), _whole((1, LANES))),
        out_shape=(sds((s, D_MODEL), F32), sds((s, WIDTH), BF16), sds((s, WIDTH), BF16),
                   sds((FOLD, ln, WIDTH), BF16), sds((s, WIDTH), BF16), sds((s, WIDTH), BF16),
                   sds((s, LANES), F32), sds((s, LANES), F32), sds((FOLD, ln, LANES), F32), sds((s, LANES), F32),
                   sds((s, LANES), F32), sds((FOLD, ln, LANES), F32),
                   sds((D_MODEL, D_MODEL), F32), sds((1, LANES), F32), sds((1, LANES), F32)),
        scratch_shapes=[_fold_scratch(LANES)],
        compiler_params=pltpu.CompilerParams(dimension_semantics=("arbitrary",), vmem_limit_bytes=VMEM_LIMIT),
    )(*att_a, *att_b1, *att_b4, *att_b16, g_a, g_b, x2, tgt2, w_out_bf, sink_row, perm)


def _inproj_bwd(x2, dy, gain, w_bf, cos, sin_s, gqa, gka, gqb, gkb, bd256, bd128, perm,
                qa_raw, ka_raw, qb_raw, kb_raw, dq_a, dk_a, dv_a, dqkv_b1, dqkv_b4, dqkv_b16, dg_a, dg_b):
    s = x2.shape[0]
    tm = ROW_TILE

    def body(x_ref, dy_ref, gain_ref, w_hbm, cos_ref, sin_ref, gqa_ref, gka_ref, gqb_ref, gkb_ref, bd256_ref,
             bd128_ref, perm_ref, qa_raw_ref, ka_raw_ref, qb_raw_ref, kb_raw_ref, dqa_ref, dka_ref, dva_ref,
             dq1_ref, dk1_ref, dv1_ref, dq4_ref, dk4_ref, dv4_ref, dq16_ref, dk16_ref, dv16_ref, dga_ref, dgb_ref,
             gx_ref, ht_ref, win_ref,
             dgain_ref, dgqa_ref, dgka_ref, dgqb_ref, dgkb_ref, w_vmem, dproj_ref):
        i = pl.program_id(0)
        perm = perm_ref[...]

        @pl.when(i == 0)
        def _():
            pltpu.sync_copy(w_hbm, w_vmem)
            dgain_ref[...] = jnp.zeros_like(dgain_ref)
            dgqa_ref[...] = jnp.zeros_like(dgqa_ref)
            dgka_ref[...] = jnp.zeros_like(dgka_ref)
            dgqb_ref[...] = jnp.zeros_like(dgqb_ref)
            dgkb_ref[...] = jnp.zeros_like(dgkb_ref)

        cos1 = cos_ref[...]
        sin1 = sin_ref[...]
        cos4 = jnp.tile(cos1, (1, 4))
        sin4 = jnp.tile(sin1, (1, 4))

        dt, dg = _qknorm_rope_bwd(dqa_ref[...], qa_raw_ref[...], gqa_ref[...], cos4, sin4, bd256_ref[...])
        dproj_ref[:, C_QA:C_KA] = dt.astype(BF16)
        dgqa_ref[...] += jnp.sum(dg, axis=0, keepdims=True)
        dt, dg = _qknorm_rope_bwd(dka_ref[...], ka_raw_ref[...], gka_ref[...], cos1, sin1, bd128_ref[...])
        dproj_ref[:, C_KA:C_VA] = dt.astype(BF16)
        dgka_ref[...] += jnp.sum(dg, axis=0, keepdims=True)
        dproj_ref[:, C_VA:C_GA] = dva_ref[...].astype(BF16)
        dproj_ref[:, C_GA:C_QB] = dga_ref[...]
        dq = (dq1_ref[...].astype(F32) + _load_folded_bf16(dq4_ref, perm)) + _load_folded_bf16(dq16_ref, perm)
        dt, dg = _qknorm_rope_bwd(dq, qb_raw_ref[...], gqb_ref[...], cos4, sin4, bd256_ref[...])
        dproj_ref[:, C_QB:C_KB] = dt.astype(BF16)
        dgqb_ref[...] += jnp.sum(dg, axis=0, keepdims=True)
        dk = (dk1_ref[...].astype(F32) + _load_folded_bf16(dk4_ref, perm)) + _load_folded_bf16(dk16_ref, perm)
        dt, dg = _qknorm_rope_bwd(dk, kb_raw_ref[...], gkb_ref[...], cos4, sin4, bd256_ref[...])
        dproj_ref[:, C_KB:C_VB] = dt.astype(BF16)
        dgkb_ref[...] += jnp.sum(dg, axis=0, keepdims=True)
        dv = (dv1_ref[...].astype(F32) + _load_folded_bf16(dv4_ref, perm)) + _load_folded_bf16(dv16_ref, perm)
        dproj_ref[:, C_VB:C_GB] = dv.astype(BF16)
        dproj_ref[:, C_GB:C_END] = dgb_ref[...]
        for k, start in enumerate(WIN_START):
            win_ref[k] = dproj_ref[:, start:start + WIN]

        xt = x_ref[...]
        gain_row = gain_ref[...]
        r = lax.rsqrt(jnp.mean(xt * xt, axis=-1, keepdims=True) + EPS)
        xr = xt * r
        ht_ref[...] = (xr * gain_row).T.astype(BF16)
        dh = _dot(dproj_ref[...], w_vmem[...])
        dgain_ref[...] += jnp.sum(dh * xr, axis=0, keepdims=True)
        u = dh * gain_row
        gx_ref[...] = dy_ref[...] + r * (u - xr * jnp.mean(u * xr, axis=-1, keepdims=True))

    def acc_row(w):
        return pl.BlockSpec((1, w), lambda i: (0, 0))

    sds = jax.ShapeDtypeStruct
    any_spec = pl.BlockSpec(memory_space=pl.ANY)
    win_spec = pl.BlockSpec((N_CHIP, tm, WIN), lambda i: (0, i, 0))
    return pl.pallas_call(
        body, name="inproj_bwd", grid=(s // tm,),
        in_specs=[_rows(D_MODEL), _rows(D_MODEL), _whole(gain.shape), any_spec, _rows(LANES), _rows(LANES),
                  _whole(gqa.shape), _whole(gka.shape), _whole(gqb.shape), _whole(gkb.shape), _whole(bd256.shape),
                  _whole(bd128.shape), _whole(perm.shape),
                  _rows(WIDTH), _rows(A_KV_WIDTH), _rows(WIDTH), _rows(WIDTH),
                  _rows(WIDTH), _rows(A_KV_WIDTH), _rows(A_KV_WIDTH)]
                 + [_rows(WIDTH)] * 3 + [_folded_rows(WIDTH)] * 6 + [_rows(WIDTH), _rows(WIDTH)],
        out_specs=(_rows(D_MODEL), pl.BlockSpec((D_MODEL, tm), lambda i: (0, i)), win_spec, acc_row(D_MODEL), acc_row(WIDTH), acc_row(A_KV_WIDTH), acc_row(WIDTH), acc_row(WIDTH)),
        out_shape=(sds((s, D_MODEL), F32), sds((D_MODEL, s), BF16), sds((N_CHIP, s, WIN), BF16),
                   sds((1, D_MODEL), F32),
                   sds((1, WIDTH), F32), sds((1, A_KV_WIDTH), F32), sds((1, WIDTH), F32), sds((1, WIDTH), F32)),
        scratch_shapes=[pltpu.VMEM((IN_WIDTH, D_MODEL), BF16), pltpu.VMEM((tm, IN_WIDTH), BF16)],
        compiler_params=pltpu.CompilerParams(dimension_semantics=("arbitrary",), vmem_limit_bytes=VMEM_LIMIT),
    )(x2, dy, gain, w_bf, cos, sin_s, gqa, gka, gqb, gkb, bd256, bd128, perm, qa_raw, ka_raw, qb_raw, kb_raw,
      dq_a, dk_a, dv_a, *dqkv_b1, *dqkv_b4, *dqkv_b16, dg_a, dg_b)


def _rope_tables(s):
    half = HEAD_DIM // 2
    inv = jnp.tile(ROPE_THETA ** (-jnp.arange(half, dtype=F32) / half), 4)
    sign = jnp.tile(jnp.concatenate([-jnp.ones((half,), F32), jnp.ones((half,), F32)]), 2)
    ang = jnp.arange(s).astype(F32)[:, None] * inv[None, :]
    return jnp.cos(ang), jnp.sin(ang) * sign[None, :]


def _block_diag_ones(w):
    idx = jnp.arange(w) // HEAD_DIM
    return (idx[:, None] == idx[None, :]).astype(BF16)


def _local_step(x2, tgt2, norm_gain, w_in_bf, q_norm_a, k_norm_a, sinks_a, q_norm_b, k_norm_b, w_out_bf):
    s = x2.shape[0]
    cos, sin_s = _rope_tables(s)
    bd256, bd128 = _block_diag_ones(2 * LANES), _block_diag_ones(A_KV_WIDTH)
    gqa = jnp.tile(q_norm_a, (1, HEADS))
    gka = jnp.tile(k_norm_a, (1, 2))
    gqb = jnp.tile(q_norm_b, (1, HEADS))
    gkb = jnp.tile(k_norm_b, (1, HEADS))
    sink_row = jnp.pad(sinks_a, ((0, 0), (0, LANES - HEADS)))
    perm = _fold_matrix()

    (qa, ka, va, qb, kb, vb, qbf, kbf, vbf, qa_raw, ka_raw, g_a, qb_raw, kb_raw, g_b) = _inproj(
        x2, norm_gain, w_in_bf, cos, sin_s, gqa, gka, gqb, gkb, bd256, bd128, perm)

    att_a = _attn_fwd(qa, ka, va, sinks_a, dil=1, max_dist=A_MAX_DIST, name="attn_a_fwd")
    att_b1 = _attn_fwd(qb, kb, vb, None, dil=1, max_dist=B_MAX_DIST, name="attn_b1_fwd")
    att_b4 = _attn_fwd(qbf, kbf, vbf, None, dil=4, max_dist=B_MAX_DIST, name="attn_b4_fwd")
    att_b16 = _attn_fwd(qbf, kbf, vbf, None, dil=16, max_dist=B_MAX_DIST, name="attn_b16_fwd")

    (dy, do_a, do_b, do_bf, dg_a, dg_b, lse_a, lse_b, lse_bf, dl_a, dl_b, dl_bf, gw_out, loss_part,
     dsink) = _outproj(att_a, att_b1, att_b4, att_b16, g_a, g_b, x2, tgt2, w_out_bf, sink_row, perm)

    dq_a, dk_a, dv_a = _attn_bwd(qa, ka, va, do_a, lse_a, dl_a, dil=1, max_dist=A_MAX_DIST, name="attn_a_bwd")
    d_b1 = _attn_bwd(qb, kb, vb, do_b, lse_b, dl_b, dil=1, max_dist=B_MAX_DIST, name="attn_b1_bwd")
    d_b4 = _attn_bwd(qbf, kbf, vbf, do_bf, lse_bf, dl_bf, dil=4, max_dist=B_MAX_DIST, name="attn_b4_bwd")
    d_b16 = _attn_bwd(qbf, kbf, vbf, do_bf, lse_bf, dl_bf, dil=16, max_dist=B_MAX_DIST, name="attn_b16_bwd")

    gx, h_t, wins, dgain, dgqa, dgka, dgqb, dgkb = _inproj_bwd(
        x2, dy, norm_gain, w_in_bf, cos, sin_s, gqa, gka, gqb, gkb, bd256, bd128, perm,
        qa_raw, ka_raw, qb_raw, kb_raw, dq_a, dk_a, dv_a, d_b1, d_b4, d_b16, dg_a, dg_b)
    return loss_part, gx, h_t, wins, gw_out, (dgain, dgqa, dgka, dsink, dgqb, dgkb)


def _position():
    return lax.axis_index("x"), lax.axis_index("y"), lax.axis_index("c")


GATHER_CHUNKS = 2


def _gather_weights(blocks, name):
    n = len(blocks)
    ch = GATHER_CHUNKS
    n_sems = n * (N_CHIP - 1) * ch

    def body(*refs):
        src_refs, dst_refs = refs[:n], refs[n:2 * n]
        ici_send, ici_recv, d2d_send, d2d_recv, local_sems = refs[2 * n:]
        x, y, c = _position()
        b = 2 * x + y
        copies = []
        for k in range(n):
            local = pltpu.make_async_copy(src_refs[k], dst_refs[k].at[b], local_sems.at[k])
            local.start()
            copies.append(local)

        def rows(k, core, j):
            half = blocks[k].shape[0] // 2
            return pl.ds(core * half + j * (half // ch), half // ch)

        plan = []
        for d in range(1, N_CHIP):
            px, py = x ^ (d >> 1), y ^ (d & 1)
            for j in range(ch):
                for k in range(n):
                    plan.append((px, py, 2 * px + py, k, j, ((d - 1) * ch + j) * n + k))
        sends = []
        for px, py, pb, k, j, sem in plan:
            send = pltpu.make_async_remote_copy(
                src_ref=src_refs[k].at[rows(k, c, j)], dst_ref=dst_refs[k].at[b, rows(k, c, j)],
                send_sem=ici_send.at[sem], recv_sem=ici_recv.at[sem], device_id=(px, py, c), device_id_type=MESH)
            send.start()
            sends.append(send)
        for px, py, pb, k, j, sem in plan:
            landed = dst_refs[k].at[pb, rows(k, c, j)]
            pltpu.make_async_remote_copy(
                src_ref=landed, dst_ref=landed, send_sem=ici_send.at[sem], recv_sem=ici_recv.at[sem],
                device_id=(px, py, c), device_id_type=MESH).wait_recv()
            forward = pltpu.make_async_remote_copy(
                src_ref=landed, dst_ref=landed, send_sem=d2d_send.at[sem], recv_sem=d2d_recv.at[sem],
                device_id=(x, y, 1 - c), device_id_type=MESH)
            forward.start()
            sends.append(forward)
        for px, py, pb, k, j, sem in plan:
            passed = dst_refs[k].at[pb, rows(k, 1 - c, j)]
            pltpu.make_async_remote_copy(
                src_ref=passed, dst_ref=passed, send_sem=d2d_send.at[sem], recv_sem=d2d_recv.at[sem],
                device_id=(x, y, 1 - c), device_id_type=MESH).wait_recv()
        for send in sends:
            send.wait_send()
        for local in copies:
            local.wait()

    vmem_spec = pl.BlockSpec(memory_space=pltpu.VMEM)
    out_shape = tuple(jax.ShapeDtypeStruct((N_CHIP,) + a.shape, a.dtype) for a in blocks)
    return pl.pallas_call(
        body, name=name, in_specs=[vmem_spec] * n, out_specs=tuple([vmem_spec] * n), out_shape=out_shape,
        scratch_shapes=[pltpu.SemaphoreType.DMA((n_sems,)) for _ in range(4)] + [pltpu.SemaphoreType.DMA((n,))],
        compiler_params=pltpu.CompilerParams(vmem_limit_bytes=VMEM_LIMIT),
    )(*blocks)


def _grad_reduce(order, h_t, wins, gw_out, small):
    s = h_t.shape[1]
    tm = GRAD_ROWS
    n_i = D_MODEL // tm
    half = D_MODEL // 2
    o_half = OUT_ROWS // 2
    n_rel = N_CHIP - 1

    def body(order_ref, ht_ref, win_ref, gwo_ref, small_ref,
             win_out, wout_out, small_out,
             acc, mine, s1, r1, s2, r2, so1, ro1, so2, ro2, pair_in, pair_o, small_land,
             s1_send, s1_recv, s2_send, s2_recv, o1_send, o1_recv, o2_send, o2_recv,
             pair_send, pair_recv, small_send, small_recv):
        j = pl.program_id(0)
        i = pl.program_id(1)
        x, y, c = _position()
        me = 4 * x + 2 * y + c
        sibling = (x, y, 1 - c)
        my_rows = pl.ds(pl.multiple_of(c * half, half), half)
        sib_rows = pl.ds(pl.multiple_of((1 - c) * half, half), half)

        def chip_of(rel):
            return x ^ (rel >> 1), y ^ (rel & 1)

        def level1(k):
            return pltpu.make_async_remote_copy(src_ref=s1.at[k], dst_ref=r1.at[k], send_sem=s1_send.at[k],
                                                recv_sem=s1_recv.at[k], device_id=sibling, device_id_type=MESH)

        def level2(k):
            px, py = chip_of(RELATIONS[k])
            return pltpu.make_async_remote_copy(src_ref=s2.at[k], dst_ref=r2.at[k], send_sem=s2_send.at[k],
                                                recv_sem=s2_recv.at[k], device_id=(px, py, c), device_id_type=MESH)

        def out_level1(bk):
            return pltpu.make_async_remote_copy(src_ref=so1.at[bk], dst_ref=ro1.at[bk], send_sem=o1_send.at[bk],
                                                recv_sem=o1_recv.at[bk], device_id=sibling, device_id_type=MESH)

        def out_level2(k):
            px, py = chip_of(RELATIONS[k])
            return pltpu.make_async_remote_copy(src_ref=so2.at[k], dst_ref=ro2.at[k], send_sem=o2_send.at[k],
                                                recv_sem=o2_recv.at[k], device_id=(px, py, c), device_id_type=MESH)

        def small_copy(d):
            px, py, pc = x ^ (d >> 2), y ^ ((d >> 1) & 1), c ^ (d & 1)
            return pltpu.make_async_remote_copy(src_ref=small_ref, dst_ref=small_land.at[me],
                                                send_sem=small_send.at[d], recv_sem=small_recv.at[d],
                                                device_id=(px, py, pc), device_id_type=MESH)

        def pair_copy(k, buf):
            return pltpu.make_async_remote_copy(src_ref=buf.at[0], dst_ref=buf.at[1], send_sem=pair_send.at[k],
                                                recv_sem=pair_recv.at[k], device_id=sibling, device_id_type=MESH)

        def out_rows(bk, core):
            return pl.ds(pl.multiple_of(bk * OUT_ROWS + core * o_half, o_half), o_half)

        @pl.when((j == 0) & (i == 0))
        def _():
            for d in range(1, N_DEV):
                small_copy(d).start()
            small_land[me] = small_ref[...]
            for bk in range(N_CHIP):
                so1[bk] = gwo_ref[out_rows(bk, 1 - c), :].astype(BF16)
                out_level1(bk).start()

        @pl.when((j == 0) & (i == 1))
        def _():
            b = 2 * x + y
            for bk in range(N_CHIP):
                out_level1(bk).wait_recv()
            for k in range(n_rel):
                px, py = chip_of(RELATIONS[k])
                bk = 2 * px + py
                so2[k] = (gwo_ref[out_rows(bk, c), :] + ro1[bk].astype(F32)).astype(BF16)
                out_level2(k).start()

        tile_rows = pl.ds(pl.multiple_of(i * tm, tm), tm)
        for n0 in range(0, WIN, ACC_COLS):
            n1 = min(n0 + ACC_COLS, WIN)
            acc[tile_rows, n0:n1] = _dot(ht_ref[...], win_ref[:, n0:n1])

        for k in range(N_CHIP):
            @pl.when((j == k) & (i == n_i - 1))
            def _(k=k):
                s1[k] = acc[sib_rows, :].astype(BF16)
                level1(k).start()
                mine[...] = acc[my_rows, :]

            if k < n_rel:
                @pl.when((j == k + 1) & (i == 1))
                def _(k=k):
                    level1(k).wait_recv()
                    s2[k] = (mine[...] + r1[k].astype(F32)).astype(BF16)
                    level2(k).start()

        @pl.when((j == N_CHIP - 1) & (i == n_i - 1))
        def _():
            b = 2 * x + y
            level1(N_CHIP - 1).wait_recv()
            total = mine[...] + r1[N_CHIP - 1].astype(F32)
            for k in range(n_rel):
                level2(k).wait_recv()
                total = total + r2[k].astype(F32)
            total = total.T
            pair_in[0] = total
            pair_copy(0, pair_in).start()
            total_o = gwo_ref[out_rows(b, c), :] + ro1[b].astype(F32)
            for k in range(n_rel):
                out_level2(k).wait_recv()
                total_o = total_o + ro2[k].astype(F32)
            pair_o[0] = total_o
            pair_copy(1, pair_o).start()
            win_out[c] = total
            wout_out[c] = total_o
            for d in range(1, N_DEV):
                small_copy(d).wait_recv()
            small_out[...] = small_land[...]
            pair_copy(0, pair_in).wait_recv()
            win_out[1 - c] = pair_in[1]
            pair_copy(1, pair_o).wait_recv()
            wout_out[1 - c] = pair_o[1]
            for d in range(1, N_DEV):
                small_copy(d).wait_send()
            for k in range(N_CHIP):
                level1(k).wait_send()
                out_level1(k).wait_send()
            for k in range(n_rel):
                level2(k).wait_send()
                out_level2(k).wait_send()
            pair_copy(0, pair_in).wait_send()
            pair_copy(1, pair_o).wait_send()

    vmem = pl.BlockSpec(memory_space=pltpu.VMEM)
    dma = pltpu.SemaphoreType.DMA
    sds = jax.ShapeDtypeStruct
    grid_spec = pltpu.PrefetchScalarGridSpec(
        num_scalar_prefetch=1, grid=(N_CHIP, n_i),
        in_specs=[pl.BlockSpec((tm, s), lambda j, i, order: (i, 0)),
                  pl.BlockSpec((None, s, WIN), lambda j, i, order: (order[j], 0, 0)), vmem, vmem],
        out_specs=(vmem, vmem, vmem),
        scratch_shapes=[
            pltpu.VMEM((D_MODEL, WIN), F32), pltpu.VMEM((half, WIN), F32),
            pltpu.VMEM((N_CHIP, half, WIN), BF16), pltpu.VMEM((N_CHIP, half, WIN), BF16),
            pltpu.VMEM((n_rel, half, WIN), BF16), pltpu.VMEM((n_rel, half, WIN), BF16),
            pltpu.VMEM((N_CHIP, o_half, D_MODEL), BF16), pltpu.VMEM((N_CHIP, o_half, D_MODEL), BF16),
            pltpu.VMEM((n_rel, o_half, D_MODEL), BF16), pltpu.VMEM((n_rel, o_half, D_MODEL), BF16),
            pltpu.VMEM((2, WIN, half), F32), pltpu.VMEM((2, o_half, D_MODEL), F32),
            pltpu.VMEM((N_DEV, PACK_ROWS, D_MODEL), F32),
            dma((N_CHIP,)), dma((N_CHIP,)), dma((n_rel,)), dma((n_rel,)),
            dma((N_CHIP,)), dma((N_CHIP,)), dma((n_rel,)), dma((n_rel,)),
            dma((2,)), dma((2,)), dma((N_DEV,)), dma((N_DEV,))])
    return pl.pallas_call(
        body, name="grad_w_in_reduce", grid_spec=grid_spec,
        out_shape=(sds((2, WIN, half), F32), sds((2, o_half, D_MODEL), F32), sds((N_DEV, PACK_ROWS, D_MODEL), F32)),
        compiler_params=pltpu.CompilerParams(dimension_semantics=("arbitrary", "arbitrary"),
                                             vmem_limit_bytes=VMEM_LIMIT),
    )(order, h_t, wins, gw_out, small)


ADAM_STEPS = 4


def _adamw_math(w, g, m, v):
    m = ADAM_B1 * m + (1.0 - ADAM_B1) * g
    v = ADAM_B2 * v + (1.0 - ADAM_B2) * (g * g)
    m_hat = m / (1.0 - ADAM_B1 ** ADAM_STEP)
    v_hat = v / (1.0 - ADAM_B2 ** ADAM_STEP)
    delta = -ADAM_LR * (m_hat / (jnp.sqrt(v_hat) + ADAM_EPS) + ADAM_WD * w)
    return delta, m, v


def _adamw(w, g, m, v, name):
    r, c = w.shape

    def body(w_ref, g_ref, m_ref, v_ref, d_ref, nm_ref, nv_ref):
        delta, nm, nv = _adamw_math(w_ref[...], g_ref[...], m_ref[...], v_ref[...])
        d_ref[...] = delta
        nm_ref[...] = nm
        nv_ref[...] = nv

    rows = r // ADAM_STEPS
    assert rows * ADAM_STEPS == r and rows % 8 == 0
    spec = pl.BlockSpec((rows, c), lambda i: (i, 0))
    shape = jax.ShapeDtypeStruct((r, c), F32)
    return pl.pallas_call(
        body, name=name, grid=(ADAM_STEPS,), in_specs=[spec] * 4, out_specs=(spec,) * 3,
        out_shape=(shape,) * 3, compiler_params=pltpu.CompilerParams(vmem_limit_bytes=VMEM_LIMIT),
    )(w, g, m, v)


PACK_ROWS = 8


def _fold_heads(v):
    y = v[:, 0:LANES]
    for j in range(1, v.shape[1] // LANES):
        y = y + v[:, j * LANES:(j + 1) * LANES]
    return y + pltpu.roll(y, HEAD_DIM, 1)


def _small_adamw(recv, w_p, m_p, v_p):
    def body(r_ref, w_ref, m_ref, v_ref, g_ref, d_ref, nm_ref, nv_ref, loss_ref):
        tot = r_ref[0]
        for j in range(1, N_DEV):
            tot = tot + r_ref[j]
        loss_ref[...] = tot[3:4, 0:LANES]
        row1 = tot[1:2, :]
        row2 = tot[2:3, :]
        pieces = [_fold_heads(row1[:, 0:WIDTH]), _fold_heads(row2[:, WIDTH:WIDTH + A_KV_WIDTH]),
                  _fold_heads(row1[:, WIDTH:2 * WIDTH]), _fold_heads(row2[:, 0:WIDTH]),
                  row2[:, WIDTH + A_KV_WIDTH:WIDTH + 2 * A_KV_WIDTH], jnp.zeros((1, 3 * LANES), F32)]
        g = jnp.concatenate([tot[0:1, :], jnp.concatenate(pieces, axis=1), jnp.zeros((PACK_ROWS - 2, D_MODEL), F32)],
                            axis=0)
        g_ref[...] = g
        delta, nm, nv = _adamw_math(w_ref[...], g, m_ref[...], v_ref[...])
        d_ref[...] = delta
        nm_ref[...] = nm
        nv_ref[...] = nv

    shape = jax.ShapeDtypeStruct((PACK_ROWS, D_MODEL), F32)
    return pl.pallas_call(body, name="small_adamw",
                          out_shape=(shape,) * 4 + (jax.ShapeDtypeStruct((1, LANES), F32),))(recv, w_p, m_p, v_p)


def _pack_small(norm_gain, q_a, k_a, q_b, k_b, sinks):
    def lane_pad(a):
        return jnp.pad(a, ((0, 0), (0, LANES - a.shape[1])))
    row1 = jnp.concatenate([lane_pad(q_a), lane_pad(k_a), lane_pad(q_b), lane_pad(k_b), lane_pad(sinks),
                            jnp.zeros((1, 3 * LANES), F32)], axis=1)
    return jnp.concatenate([norm_gain, row1, jnp.zeros((PACK_ROWS - 2, D_MODEL), F32)], axis=0)


def _unpack_small(p):
    return (p[0:1, :], p[1:2, 0:HEAD_DIM], p[1:2, LANES:LANES + HEAD_DIM], p[1:2, 2 * LANES:2 * LANES + HEAD_DIM],
            p[1:2, 3 * LANES:3 * LANES + HEAD_DIM], p[1:2, 4 * LANES:4 * LANES + HEADS])


def kernel(x, norm_gain, w_in, q_norm_a, k_norm_a, sinks_a, q_norm_b, k_norm_b, w_out, loss_target, m_norm_gain, m_w_in, m_q_norm_a, m_k_norm_a, m_sinks_a, m_q_norm_b, m_k_norm_b, m_w_out, v_norm_gain, v_w_in, v_q_norm_a, v_k_norm_a, v_sinks_a, v_q_norm_b, v_k_norm_b, v_w_out):
    chip = 2 * lax.axis_index("x") + lax.axis_index("y")

    w_in_t, m_w_in_t, v_w_in_t = w_in[0].T, m_w_in[0].T, v_w_in[0].T

    w_in_all, w_out_all = _gather_weights([w_in_t.astype(BF16), w_out[0].astype(BF16)], "gather_weights")
    w_in_bf = w_in_all.reshape(IN_WIDTH, D_MODEL)
    w_out_bf = w_out_all.reshape(D_MODEL, D_MODEL)

    loss_part, gx, h_t, wins, gw_out, (dgain, dgqa, dgka, dsink, dgqb, dgkb) = _local_step(
        x[0], loss_target[0], norm_gain, w_in_bf, q_norm_a, k_norm_a, sinks_a, q_norm_b, k_norm_b, w_out_bf)

    small = jnp.concatenate([
        dgain, jnp.concatenate([dgqa, dgqb], axis=1),
        jnp.concatenate([dgkb, dgka, dsink, jnp.zeros((1, D_MODEL - WIDTH - 2 * A_KV_WIDTH), F32)], axis=1),
        jnp.pad(loss_part, ((0, 0), (0, D_MODEL - LANES))),
        jnp.zeros((PACK_ROWS - 4, D_MODEL), F32)], axis=0)
    order = (chip ^ jnp.array(RELATIONS, jnp.int32)).astype(jnp.int32)
    win_sum, wout_sum, small_recv = _grad_reduce(order, h_t, wins, gw_out, small)
    shift = jnp.array(WIN_SHIFT, jnp.int32)[chip]
    g_w_in_t = lax.dynamic_slice_in_dim(win_sum.transpose(1, 0, 2).reshape(WIN, D_MODEL), shift, IN_COLS, axis=0)
    g_w_out = wout_sum.reshape(OUT_ROWS, D_MODEL)

    d_w_in, nm_w_in, nv_w_in = (a.T for a in _adamw(w_in_t, g_w_in_t, m_w_in_t, v_w_in_t, "adamw_w_in"))
    g_w_in = g_w_in_t.T
    d_w_out, nm_w_out, nv_w_out = _adamw(w_out[0], g_w_out, m_w_out[0], v_w_out[0], "adamw_w_out")
    g_s, d_s, nm_s, nv_s, loss_row = _small_adamw(
        small_recv,
        _pack_small(norm_gain, q_norm_a, k_norm_a, q_norm_b, k_norm_b, sinks_a),
        _pack_small(m_norm_gain, m_q_norm_a, m_k_norm_a, m_q_norm_b, m_k_norm_b, m_sinks_a),
        _pack_small(v_norm_gain, v_q_norm_a, v_k_norm_a, v_q_norm_b, v_k_norm_b, v_sinks_a))
    loss = loss_row[0, 0]

    def leaves(small_packed, big_in, big_out):
        gain, qa, ka, qb, kb, sk = _unpack_small(small_packed)
        return (gain, big_in[None], qa, ka, sk, qb, kb, big_out[None])

    return ((loss, gx[None]) + leaves(g_s, g_w_in, g_w_out) + leaves(d_s, d_w_in, d_w_out)
            + leaves(nm_s, nm_w_in, nm_w_out) + leaves(nv_s, nv_w_in, nv_w_out))
```

```python
import jax
import jax.numpy as jnp
from jax import lax
from jax.experimental import pallas as pl
from jax.experimental.pallas import tpu as pltpu

F32 = jnp.float32
BF16 = jnp.bfloat16

D_MODEL = 1024
HEAD_DIM = 64
HEADS = 8
WIDTH = HEADS * HEAD_DIM
A_KV_WIDTH = 2 * HEAD_DIM
BLOCK = 128
LANES = 128
FOLD = 16
A_MAX_DIST = 127
B_MAX_DIST = 128
ROPE_THETA = 10000.0
EPS = 1e-6
NEG = -1e30
SCALE = HEAD_DIM ** -0.5

IN_WIDTH = 3328
C_QA, C_KA, C_VA, C_GA, C_QB, C_KB, C_VB, C_GB, C_END = 0, 512, 640, 768, 1280, 1792, 2304, 2816, 3328

N_DEV = 8
N_CHIP = 4
MESH = pl.DeviceIdType.MESH
IN_COLS = IN_WIDTH // N_CHIP
WIN = 896
WIN_START = (0, 768, 1664, 2432)
WIN_SHIFT = (0, 64, 0, 64)
OUT_ROWS = D_MODEL // N_CHIP
RELATIONS = (3, 1, 2, 0)

ADAM_LR = 0.001
ADAM_B1 = 0.9
ADAM_B2 = 0.999
ADAM_EPS = 1e-08
ADAM_WD = 0.01
ADAM_STEP = 10

ROW_TILE = 256
FOLD_ROWS = ROW_TILE // FOLD
GRAD_ROWS = 1024
ACC_COLS = 256
VMEM_LIMIT = 56 * 1024 * 1024


def _dot(a, b):
    return jnp.dot(a, b, preferred_element_type=F32)


def _dot_nt(a, b):
    return lax.dot_general(a, b, (((1,), (1,)), ((), ())), preferred_element_type=F32)


def _dot_tn(a, b):
    return lax.dot_general(a, b, (((0,), (0,)), ((), ())), preferred_element_type=F32)


def _head_sum(z, bd):
    w = bd.shape[0]
    zb = z.astype(BF16)
    parts = [_dot(zb[:, a:a + w], bd) for a in range(0, z.shape[1], w)]
    return parts[0] if len(parts) == 1 else jnp.concatenate(parts, axis=1)


def _swap_halves(t):
    w = t.shape[1]
    lane = lax.broadcasted_iota(jnp.int32, t.shape, 1)
    return jnp.where(lane % HEAD_DIM < HEAD_DIM // 2, pltpu.roll(t, w - 32, 1), pltpu.roll(t, 32, 1))


def _qknorm_rope(t, g, cos, sin_s, bd):
    r = lax.rsqrt(_head_sum(t * t, bd) * (1.0 / HEAD_DIM) + EPS)
    n = (t * r) * g
    return n * cos + _swap_halves(n) * sin_s


def _qknorm_rope_bwd(dout, t, g, cos, sin_s, bd):
    dout, t = dout.astype(F32), t.astype(F32)
    dn = dout * cos + _swap_halves(dout * sin_s)
    r = lax.rsqrt(_head_sum(t * t, bd) * (1.0 / HEAD_DIM) + EPS)
    tr = t * r
    u = dn * g
    dt = r * (u - tr * (_head_sum(u * tr, bd) * (1.0 / HEAD_DIM)))
    return dt, dn * tr


def _sigmoid(g):
    return 1.0 / (1.0 + jnp.exp(-g))


def _expand_heads(st):
    t = st.shape[0]
    lane = lax.broadcasted_iota(jnp.int32, (t, LANES), 1)
    chunks = []
    for c in range(WIDTH // LANES):
        chunks.append(jnp.where(lane < HEAD_DIM, st[:, 2 * c:2 * c + 1], st[:, 2 * c + 1:2 * c + 2]))
    return jnp.concatenate(chunks, axis=1)


def _reduce_heads(z):
    t = z.shape[0]
    lane = lax.broadcasted_iota(jnp.int32, (t, LANES), 1)
    out = jnp.zeros((t, LANES), F32)
    for c in range(WIDTH // LANES):
        zc = z[:, c * LANES:(c + 1) * LANES]
        for ph in range(2):
            s = jnp.sum(jnp.where((lane // HEAD_DIM) == ph, zc, 0.0), axis=-1, keepdims=True)
            out = jnp.where(lane == 2 * c + ph, s, out)
    return out


def _fold_scratch(w):
    return pltpu.VMEM((w // LANES, ROW_TILE, LANES), F32)


def _store_folded(out_ref, val, scr):
    n = val.shape[1] // LANES
    for c in range(n):
        scr[c] = val[:, c * LANES:(c + 1) * LANES]
    for r in range(FOLD):
        piece = [scr[c, pl.ds(r, FOLD_ROWS, stride=FOLD), :] for c in range(n)]
        out_ref[r] = (piece[0] if n == 1 else jnp.concatenate(piece, axis=1)).astype(out_ref.dtype)


def _load_folded(in_ref, scr):
    n = in_ref.shape[2] // LANES
    for r in range(FOLD):
        blk = in_ref[r].astype(F32)
        for c in range(n):
            scr[c, pl.ds(r, FOLD_ROWS, stride=FOLD), :] = blk[:, c * LANES:(c + 1) * LANES]
    return scr[0] if n == 1 else jnp.concatenate([scr[c] for c in range(n)], axis=1)


def _fold_matrix():
    f = jnp.arange(ROW_TILE)
    return (jnp.arange(ROW_TILE)[None, :] == (FOLD * (f % FOLD_ROWS) + f // FOLD_ROWS)[:, None]).astype(BF16)


def _store_folded_bf16(out_ref, val, perm):
    folded = _dot(perm, val.astype(BF16)).astype(out_ref.dtype)
    for r in range(FOLD):
        out_ref[r] = folded[r * FOLD_ROWS:(r + 1) * FOLD_ROWS]


def _load_folded_bf16(in_ref, perm):
    blk = jnp.concatenate([in_ref[r] for r in range(FOLD)], axis=0)
    return _dot(perm, blk)


def _rows(w, tm=ROW_TILE):
    return pl.BlockSpec((tm, w), lambda i: (i, 0))


def _folded_rows(w):
    return pl.BlockSpec((FOLD, FOLD_ROWS, w), lambda i: (0, i, 0))


def _whole(shape):
    return pl.BlockSpec(shape, lambda i: (0,) * len(shape))


def _inproj(x2, gain, w_bf, cos, sin_s, gqa, gka, gqb, gkb, bd256, bd128):
    s = x2.shape[0]
    tm = ROW_TILE

    def body(x_ref, gain_ref, w_hbm, cos_ref, sin_ref, gqa_ref, gka_ref, gqb_ref, gkb_ref, bd256_ref, bd128_ref,
             qa_ref, ka_ref, va_ref, qb_ref, kb_ref, vb_ref, qbf_ref, kbf_ref, vbf_ref,
             qa_raw_ref, ka_raw_ref, ga_ref, qb_raw_ref, kb_raw_ref, gb_ref, w_vmem, scr):
        @pl.when(pl.program_id(0) == 0)
        def _():
            pltpu.sync_copy(w_hbm, w_vmem)

        xt = x_ref[...]
        r = lax.rsqrt(jnp.mean(xt * xt, axis=-1, keepdims=True) + EPS)
        h = ((xt * r) * gain_ref[...]).astype(BF16)
        cos1 = cos_ref[...]
        sin1 = sin_ref[...]
        cos4 = jnp.tile(cos1, (1, 4))
        sin4 = jnp.tile(sin1, (1, 4))

        def seg(a, b):
            return _dot_nt(h, w_vmem[a:b, :])

        t = seg(C_QA, C_KA)
        qa_raw_ref[...] = t.astype(BF16)
        qa_ref[...] = (_qknorm_rope(t, gqa_ref[...], cos4, sin4, bd256_ref[...]) * SCALE).astype(BF16)
        t = seg(C_KA, C_VA)
        ka_raw_ref[...] = t.astype(BF16)
        ka_ref[...] = _qknorm_rope(t, gka_ref[...], cos1, sin1, bd128_ref[...]).astype(BF16)
        va_ref[...] = seg(C_VA, C_GA).astype(BF16)
        ga_ref[...] = seg(C_GA, C_QB).astype(BF16)
        t = seg(C_QB, C_KB)
        qb_raw_ref[...] = t.astype(BF16)
        t = _qknorm_rope(t, gqb_ref[...], cos4, sin4, bd256_ref[...]) * SCALE
        qb_ref[...] = t.astype(BF16)
        _store_folded(qbf_ref, t, scr)
        t = seg(C_KB, C_VB)
        kb_raw_ref[...] = t.astype(BF16)
        t = _qknorm_rope(t, gkb_ref[...], cos4, sin4, bd256_ref[...])
        kb_ref[...] = t.astype(BF16)
        _store_folded(kbf_ref, t, scr)
        t = seg(C_VB, C_GB)
        vb_ref[...] = t.astype(BF16)
        _store_folded(vbf_ref, t, scr)
        gb_ref[...] = seg(C_GB, C_END).astype(BF16)

    sds = jax.ShapeDtypeStruct
    folded = sds((FOLD, s // FOLD, WIDTH), BF16)
    out_shape = (sds((s, WIDTH), BF16), sds((s, A_KV_WIDTH), BF16), sds((s, A_KV_WIDTH), BF16),
                 sds((s, WIDTH), BF16), sds((s, WIDTH), BF16), sds((s, WIDTH), BF16), folded, folded, folded,
                 sds((s, WIDTH), BF16), sds((s, A_KV_WIDTH), BF16), sds((s, WIDTH), BF16),
                 sds((s, WIDTH), BF16), sds((s, WIDTH), BF16), sds((s, WIDTH), BF16))
    out_specs = (_rows(WIDTH), _rows(A_KV_WIDTH), _rows(A_KV_WIDTH), _rows(WIDTH), _rows(WIDTH), _rows(WIDTH),
                 _folded_rows(WIDTH), _folded_rows(WIDTH), _folded_rows(WIDTH),
                 _rows(WIDTH), _rows(A_KV_WIDTH), _rows(WIDTH), _rows(WIDTH), _rows(WIDTH), _rows(WIDTH))
    return pl.pallas_call(
        body, name="inproj_fwd", grid=(s // tm,),
        in_specs=[_rows(D_MODEL), _whole(gain.shape), pl.BlockSpec(memory_space=pl.ANY), _rows(LANES), _rows(LANES),
                  _whole(gqa.shape), _whole(gka.shape), _whole(gqb.shape), _whole(gkb.shape), _whole(bd256.shape),
                  _whole(bd128.shape)],
        out_specs=out_specs, out_shape=out_shape,
        scratch_shapes=[pltpu.VMEM((IN_WIDTH, D_MODEL), BF16), _fold_scratch(WIDTH)],
        compiler_params=pltpu.CompilerParams(dimension_semantics=("arbitrary",), vmem_limit_bytes=VMEM_LIMIT),
    )(x2, gain, w_bf, cos, sin_s, gqa, gka, gqb, gkb, bd256, bd128)


def _seq_pos(idx, dil):
    if dil == 4:
        return 4 * (idx % 32) + idx // 32
    return idx


SOFTMAX_ROWS = 64


def _upper_mask(dil, r0=0, rows=2 * BLOCK):
    qi = (lax.broadcasted_iota(jnp.int32, (rows, BLOCK), 0) + r0) % BLOCK
    kj = lax.broadcasted_iota(jnp.int32, (rows, BLOCK), 1)
    return _seq_pos(kj, dil) > _seq_pos(qi, dil)


def _eye_mask(r0=0, rows=2 * BLOCK):
    qi = (lax.broadcasted_iota(jnp.int32, (rows, BLOCK), 0) + r0) % BLOCK
    kj = lax.broadcasted_iota(jnp.int32, (rows, BLOCK), 1)
    return qi == kj


def _stack_heads(a2, c, gqa):
    lane = lax.broadcasted_iota(jnp.int32, (1, LANES), 1) // HEAD_DIM
    zero = jnp.zeros_like(a2)
    if gqa:
        keep = lane == (c // 2)
        return jnp.concatenate([jnp.where(keep, a2, zero), jnp.where(keep, _swap_heads(a2), zero)], axis=0)
    return jnp.concatenate([jnp.where(lane == 0, a2, zero), jnp.where(lane == 1, a2, zero)], axis=0)


def _unstack_heads(a, c, gqa):
    lane = lax.broadcasted_iota(jnp.int32, (1, LANES), 1) // HEAD_DIM
    if gqa:
        return jnp.where(lane == (c // 2), a[:BLOCK], _swap_heads(a[BLOCK:]))
    return jnp.where(lane == 0, a[:BLOCK], a[BLOCK:])


def _stacked_head_ids(c, gqa):
    if gqa:
        return 2 * c + c // 2, 2 * c + 1 - c // 2
    return 2 * c, 2 * c + 1


def _per_head_rows(blk, heads):
    return jnp.concatenate([blk[:, heads[0]:heads[0] + 1], blk[:, heads[1]:heads[1] + 1]], axis=0)


def _attn_view(a, dil):
    if dil == 1:
        return a[None]
    if dil == 4:
        return a.reshape(4, 4, a.shape[1], a.shape[2])
    return a


def _attn_unview(a, dil):
    if dil == 1:
        return a[0]
    if dil == 4:
        return a.reshape(FOLD, a.shape[2], a.shape[3])
    return a


def _attn_specs(dil, nb):
    if dil == 4:
        def spec(fn):
            return lambda w: pl.BlockSpec((4, None, BLOCK // 4, w), lambda r, i: (0, r, fn(i), 0))
    else:
        def spec(fn):
            return lambda w: pl.BlockSpec((None, BLOCK, w), lambda r, i: (r, fn(i), 0))
    return spec


def _blk_load(ref, sl, dil):
    if dil == 4:
        return ref[:, :, sl].reshape(BLOCK, sl.stop - sl.start)
    return ref[:, sl]


def _blk_store(ref, sl, val, dil):
    val = val.astype(ref.dtype)
    if dil == 4:
        ref[:, :, sl] = val.reshape(4, BLOCK // 4, sl.stop - sl.start)
    else:
        ref[:, sl] = val


def _swap_heads(a):
    return pltpu.roll(a.astype(F32), HEAD_DIM, 1).astype(a.dtype)


def _attn_fwd(q, k, v, sinks, *, dil, max_dist, name):
    q, k, v = _attn_view(q, dil), _attn_view(k, dil), _attn_view(v, dil)
    kw = k.shape[-1]
    gqa = kw == A_KV_WIDTH
    n_seq = dil
    nb = (q.shape[-2] * (4 if dil == 4 else 1)) // BLOCK
    with_sinks = sinks is not None
    all_lanes = slice(0, LANES)
    assert max_dist in (BLOCK - 1, BLOCK)
    diag = max_dist == BLOCK

    def body(*refs):
        if with_sinks:
            q_ref, kp_ref, kc_ref, vp_ref, vc_ref, sink_ref, o_ref, m_ref, l_ref = refs
        else:
            q_ref, kp_ref, kc_ref, vp_ref, vc_ref, o_ref, m_ref, l_ref = refs

        def block(has_prev):
            lane = lax.broadcasted_iota(jnp.int32, (1, LANES), 1)
            with_diag = diag and has_prev
            upper, eye = _upper_mask(dil), _eye_mask()
            first_rows = lax.broadcasted_iota(jnp.int32, (2 * BLOCK, 1), 0) < BLOCK
            m_blk = jnp.zeros((BLOCK, LANES), F32)
            l_blk = jnp.ones((BLOCK, LANES), F32)
            chunks = range(WIDTH // LANES)
            scores, values = [], []
            for c in chunks:
                sl = slice(c * LANES, (c + 1) * LANES)
                ksl = slice(0, LANES) if gqa else sl
                kcur, vcur = _blk_load(kc_ref, ksl, dil), _blk_load(vc_ref, ksl, dil)
                qs = _stack_heads(_blk_load(q_ref, sl, dil), c, gqa)
                if has_prev:
                    kcur = jnp.concatenate([_blk_load(kp_ref, ksl, dil), kcur], axis=0)
                    vcur = jnp.concatenate([_blk_load(vp_ref, ksl, dil), vcur], axis=0)
                scores.append(_dot_nt(qs, kcur))
                values.append(vcur)
            probs = []
            for c in chunks:
                heads = _stacked_head_ids(c, gqa)
                s = scores[c]
                if has_prev:
                    s_p = s[:, :BLOCK]
                    sc = jnp.where(upper, s_p, s[:, BLOCK:])
                else:
                    sc = jnp.where(upper, NEG, s)
                if with_diag:
                    sd = jnp.where(eye, s_p, NEG)
                    m = jnp.max(jnp.maximum(sc, sd), axis=-1, keepdims=True)
                else:
                    m = jnp.max(sc, axis=-1, keepdims=True)
                if with_sinks:
                    sk = jnp.where(first_rows, sink_ref[0, heads[0]], sink_ref[0, heads[1]])
                    m = jnp.maximum(m, sk)
                p = jnp.exp(sc - m)
                zero = jnp.zeros_like(p)
                if with_diag:
                    pd = jnp.exp(sd - m)
                    l = jnp.sum(p + pd, axis=-1, keepdims=True)
                else:
                    pd = zero
                    l = jnp.sum(p, axis=-1, keepdims=True)
                if with_sinks:
                    l = l + jnp.exp(sk - m)
                pf = jnp.where(upper, zero, p)
                if has_prev:
                    pf = jnp.concatenate([jnp.where(upper, p, pd), pf], axis=1)
                probs.append(pf.astype(BF16))
                for n, h in enumerate(heads):
                    rows = slice(n * BLOCK, (n + 1) * BLOCK)
                    m_blk = jnp.where(lane == h, m[rows], m_blk)
                    l_blk = jnp.where(lane == h, l[rows], l_blk)
            for c in chunks:
                sl = slice(c * LANES, (c + 1) * LANES)
                _blk_store(o_ref, sl, _unstack_heads(_dot(probs[c], values[c]), c, gqa), dil)
            _blk_store(m_ref, all_lanes, m_blk, dil)
            _blk_store(l_ref, all_lanes, l_blk, dil)

        @pl.when(pl.program_id(1) == 0)
        def _():
            block(False)

        @pl.when(pl.program_id(1) > 0)
        def _():
            block(True)

    spec = _attn_specs(dil, nb)
    cur = spec(lambda i: i)
    prev = spec(lambda i: jnp.maximum(i - 1, 0))
    in_specs = [cur(WIDTH), prev(kw), cur(kw), prev(kw), cur(kw)]
    args = [q, k, k, v, v]
    if with_sinks:
        in_specs.append(pl.BlockSpec(memory_space=pltpu.SMEM))
        args.append(sinks)
    stats = jax.ShapeDtypeStruct(q.shape[:-1] + (LANES,), F32)
    o, m, l = pl.pallas_call(
        body, name=name, grid=(n_seq, nb), in_specs=in_specs,
        out_specs=(cur(WIDTH), cur(LANES), cur(LANES)),
        out_shape=(jax.ShapeDtypeStruct(q.shape, BF16), stats, stats),
        compiler_params=pltpu.CompilerParams(dimension_semantics=("arbitrary", "arbitrary")),
    )(*args)
    return _attn_unview(o, dil), _attn_unview(m, dil), _attn_unview(l, dil)


def _attn_bwd(q, k, v, do, lse, delta, *, dil, max_dist, name):
    q, k, v, do, lse, delta = (_attn_view(a, dil) for a in (q, k, v, do, lse, delta))
    kw = k.shape[-1]
    gqa = kw == A_KV_WIDTH
    n_seq = dil
    nb = (q.shape[-2] * (4 if dil == 4 else 1)) // BLOCK
    n_kc = kw // LANES
    all_lanes = slice(0, LANES)
    assert max_dist in (BLOCK - 1, BLOCK)
    diag = max_dist == BLOCK

    def body(q_ref, kp_ref, kc_ref, vp_ref, vc_ref, do_ref, lse_ref, dl_ref, dq_ref, dk_ref, dv_ref, ck_ref, cv_ref):
        i = pl.program_id(1)

        def block(has_prev):
            upper, eye = _upper_mask(dil), _eye_mask()
            lse_blk = _blk_load(lse_ref, all_lanes, dil)
            dl_blk = _blk_load(dl_ref, all_lanes, dil)
            dk_acc = [None] * n_kc
            dv_acc = [None] * n_kc
            chunks = range(WIDTH // LANES)
            operands, products = [], []
            for c in chunks:
                sl = slice(c * LANES, (c + 1) * LANES)
                kc = 0 if gqa else c
                ksl = slice(kc * LANES, (kc + 1) * LANES)
                k2, v2 = _blk_load(kc_ref, ksl, dil), _blk_load(vc_ref, ksl, dil)
                if has_prev:
                    k2 = jnp.concatenate([_blk_load(kp_ref, ksl, dil), k2], axis=0)
                    v2 = jnp.concatenate([_blk_load(vp_ref, ksl, dil), v2], axis=0)
                qs = _stack_heads(_blk_load(q_ref, sl, dil), c, gqa)
                dos = _stack_heads(_blk_load(do_ref, sl, dil), c, gqa)
                operands.append((qs, dos, k2))
                products.append((_dot_nt(qs, k2), _dot_nt(dos, v2)))
            weights = []
            for c in chunks:
                heads = _stacked_head_ids(c, gqa)
                lse2 = _per_head_rows(lse_blk, heads)
                dl2 = _per_head_rows(dl_blk, heads)
                s, dp = products[c]
                if has_prev:
                    s_p, dp_p = s[:, :BLOCK], dp[:, :BLOCK]
                    sc = jnp.where(upper, s_p, s[:, BLOCK:])
                    dpc = jnp.where(upper, dp_p, dp[:, BLOCK:])
                else:
                    sc = jnp.where(upper, NEG, s)
                    dpc = dp
                p = jnp.exp(sc - lse2)
                ds = p * (dpc - dl2)
                zero = jnp.zeros_like(p)
                pf = jnp.where(upper, zero, p)
                dsf = jnp.where(upper, zero, ds)
                if has_prev:
                    if diag:
                        pd = jnp.exp(jnp.where(eye, s_p, NEG) - lse2)
                        dsd = pd * (dp_p - dl2)
                    else:
                        pd = dsd = zero
                    pf = jnp.concatenate([jnp.where(upper, p, pd), pf], axis=1)
                    dsf = jnp.concatenate([jnp.where(upper, ds, dsd), dsf], axis=1)
                weights.append((pf.astype(BF16), dsf.astype(BF16)))
            for c in chunks:
                sl = slice(c * LANES, (c + 1) * LANES)
                kc = 0 if gqa else c
                qs, dos, k2 = operands[c]
                pf, dsf = weights[c]
                _blk_store(dq_ref, sl, _unstack_heads(_dot(dsf, k2), c, gqa) * SCALE, dil)
                dk2 = _dot_tn(dsf, qs)
                dv2 = _dot_tn(pf, dos)
                dk_acc[kc] = dk2 if dk_acc[kc] is None else dk_acc[kc] + dk2
                dv_acc[kc] = dv2 if dv_acc[kc] is None else dv_acc[kc] + dv2
            for kc in range(n_kc):
                sl = slice(kc * LANES, (kc + 1) * LANES)
                if has_prev:
                    _blk_store(dk_ref, sl, ck_ref[:, sl] + dk_acc[kc][:BLOCK], dil)
                    _blk_store(dv_ref, sl, cv_ref[:, sl] + dv_acc[kc][:BLOCK], dil)
                    ck_ref[:, sl] = dk_acc[kc][BLOCK:]
                    cv_ref[:, sl] = dv_acc[kc][BLOCK:]
                else:
                    ck_ref[:, sl] = dk_acc[kc]
                    cv_ref[:, sl] = dv_acc[kc]

        @pl.when(i == 0)
        def _():
            block(False)

        @pl.when((i > 0) & (i < nb))
        def _():
            block(True)

        @pl.when(i == nb)
        def _():
            for kc in range(n_kc):
                sl = slice(kc * LANES, (kc + 1) * LANES)
                _blk_store(dk_ref, sl, ck_ref[:, sl], dil)
                _blk_store(dv_ref, sl, cv_ref[:, sl], dil)

    spec = _attn_specs(dil, nb)
    cur = spec(lambda i: jnp.minimum(i, nb - 1))
    prev = spec(lambda i: jnp.clip(i - 1, 0, nb - 1))
    lag = spec(lambda i: jnp.maximum(i - 1, 0))
    sds = jax.ShapeDtypeStruct
    dq, dk, dv = pl.pallas_call(
        body, name=name, grid=(n_seq, nb + 1),
        in_specs=[cur(WIDTH), prev(kw), cur(kw), prev(kw), cur(kw), cur(WIDTH), cur(LANES), cur(LANES)],
        out_specs=(cur(WIDTH), lag(kw), lag(kw)),
        out_shape=(sds(q.shape, BF16), sds(k.shape, BF16), sds(k.shape, BF16)),
        scratch_shapes=[pltpu.VMEM((BLOCK, kw), F32), pltpu.VMEM((BLOCK, kw), F32)],
        compiler_params=pltpu.CompilerParams(dimension_semantics=("arbitrary", "arbitrary")),
    )(q, k, k, v, v, do, lse, delta)
    return _attn_unview(dq, dil), _attn_unview(dk, dil), _attn_unview(dv, dil)


def _outproj(att_a, att_b1, att_b4, att_b16, g_a, g_b, x2, tgt2, w_out_bf, sink_row, perm):
    s = x2.shape[0]
    tm = ROW_TILE

    def body(oa_ref, ma_ref, la_ref, ob1_ref, m1_ref, l1_ref, ob4_ref, m4_ref, l4_ref, ob16_ref, m16_ref, l16_ref,
             ga_ref, gb_ref, x_ref, t_ref, w_ref, sink_ref, perm_ref,
             dy_ref, doa_ref, dob_ref, dobf_ref, dga_ref, dgb_ref, lsea_ref, lseb_ref, lsebf_ref, dla_ref, dlb_ref,
             dlbf_ref, gw_ref, loss_ref, dsink_ref, scr_st):
        i = pl.program_id(0)
        perm = perm_ref[...]

        @pl.when(i == 0)
        def _():
            gw_ref[...] = jnp.zeros_like(gw_ref)
            loss_ref[...] = jnp.zeros_like(loss_ref)
            dsink_ref[...] = jnp.zeros_like(dsink_ref)

        ms = [m1_ref[...], _load_folded(m4_ref, scr_st), _load_folded(m16_ref, scr_st)]
        ls = [l1_ref[...], _load_folded(l4_ref, scr_st), _load_folded(l16_ref, scr_st)]
        mx = jnp.maximum(jnp.maximum(ms[0], ms[1]), ms[2])
        scale = [jnp.exp(mp - mx) for mp in ms]
        den = (ls[0] * scale[0] + ls[1] * scale[1]) + ls[2] * scale[2]
        lane = lax.broadcasted_iota(jnp.int32, (tm, LANES), 1)
        lse_b = jnp.where(lane < HEADS, mx + jnp.log(den), 0.0)
        lseb_ref[...] = lse_b
        _store_folded(lsebf_ref, lse_b, scr_st)
        inv_den = 1.0 / den
        o_b = _expand_heads(scale[0] * inv_den) * ob1_ref[...].astype(F32)
        o_b = o_b + _expand_heads(scale[1] * inv_den) * _load_folded_bf16(ob4_ref, perm)
        o_b = o_b + _expand_heads(scale[2] * inv_den) * _load_folded_bf16(ob16_ref, perm)
        l_a = la_ref[...]
        lse_a = jnp.where(lane < HEADS, ma_ref[...] + jnp.log(l_a), 0.0)
        lsea_ref[...] = lse_a
        o_a = _expand_heads(1.0 / l_a) * oa_ref[...].astype(F32)
        g_a = ga_ref[...].astype(F32)
        g_b = gb_ref[...].astype(F32)
        sg_a = _sigmoid(g_a)
        sg_b = _sigmoid(g_b)
        silu_a = g_a * sg_a
        silu_b = g_b * sg_b
        mixed = jnp.concatenate([o_a * silu_a, o_b * silu_b], axis=1).astype(BF16)
        w = w_ref[...]
        y = x_ref[...] + _dot(mixed, w)
        diff = y - t_ref[...]
        loss_ref[...] += (0.5 / D_MODEL) * jnp.sum(diff * diff)
        dy = diff * (1.0 / D_MODEL)
        dy_ref[...] = dy
        dyb = dy.astype(BF16)
        gw_ref[...] += _dot_tn(mixed, dyb)
        dmixed = _dot_nt(dyb, w)
        dm_a = dmixed[:, :WIDTH]
        dm_b = dmixed[:, WIDTH:]
        do_a = dm_a * silu_a
        do_b = dm_b * silu_b
        doa_ref[...] = do_a.astype(BF16)
        dob_ref[...] = do_b.astype(BF16)
        _store_folded_bf16(dobf_ref, do_b, perm)
        dga_ref[...] = (dm_a * o_a * (sg_a * (1.0 + g_a * (1.0 - sg_a)))).astype(BF16)
        dgb_ref[...] = (dm_b * o_b * (sg_b * (1.0 + g_b * (1.0 - sg_b)))).astype(BF16)
        dl_a = _reduce_heads(do_a * o_a)
        dla_ref[...] = dl_a
        dl_b = _reduce_heads(do_b * o_b)
        dlb_ref[...] = dl_b
        _store_folded(dlbf_ref, dl_b, scr_st)
        dsink_ref[...] -= jnp.sum(jnp.exp(sink_ref[...] - lse_a) * dl_a, axis=0, keepdims=True)

    sds = jax.ShapeDtypeStruct
    ln = s // FOLD
    natural = [_rows(WIDTH), _rows(LANES), _rows(LANES)]
    folded = [_folded_rows(WIDTH), _folded_rows(LANES), _folded_rows(LANES)]
    return pl.pallas_call(
        body, name="outproj_fwd_bwd", grid=(s // tm,),
        in_specs=natural + natural + folded + folded
                 + [_rows(WIDTH), _rows(WIDTH), _rows(D_MODEL), _rows(D_MODEL), _whole((D_MODEL, D_MODEL)),
                    _whole((1, LANES)), _whole(perm.shape)],
        out_specs=(_rows(D_MODEL), _rows(WIDTH), _rows(WIDTH), _folded_rows(WIDTH), _rows(WIDTH), _rows(WIDTH),
                   _rows(LANES), _rows(LANES), _folded_rows(LANES), _rows(LANES), _rows(LANES), _folded_rows(LANES),
                   _whole((D_MODEL, D_MODEL)), _whole((1, LANES)), _whole((1, LANES))),
        out_shape=(sds((s, D_MODEL), F32), sds((s, WIDTH), BF16), sds((s, WIDTH), BF16),
                   sds((FOLD, ln, WIDTH), BF16), sds((s, WIDTH), BF16), sds((s, WIDTH), BF16),
                   sds((s, LANES), F32), sds((s, LANES), F32), sds((FOLD, ln, LANES), F32), sds((s, LANES), F32),
                   sds((s, LANES), F32), sds((FOLD, ln, LANES), F32),
                   sds((D_MODEL, D_MODEL), F32), sds((1, LANES), F32), sds((1, LANES), F32)),
        scratch_shapes=[_fold_scratch(LANES)],
        compiler_params=pltpu.CompilerParams(dimension_semantics=("arbitrary",), vmem_limit_bytes=VMEM_LIMIT),
    )(*att_a, *att_b1, *att_b4, *att_b16, g_a, g_b, x2, tgt2, w_out_bf, sink_row, perm)


def _inproj_bwd(x2, dy, gain, w_bf, cos, sin_s, gqa, gka, gqb, gkb, bd256, bd128, perm,
                qa_raw, ka_raw, qb_raw, kb_raw, dq_a, dk_a, dv_a, dqkv_b1, dqkv_b4, dqkv_b16, dg_a, dg_b):
    s = x2.shape[0]
    tm = ROW_TILE

    def body(x_ref, dy_ref, gain_ref, w_hbm, cos_ref, sin_ref, gqa_ref, gka_ref, gqb_ref, gkb_ref, bd256_ref,
             bd128_ref, perm_ref, qa_raw_ref, ka_raw_ref, qb_raw_ref, kb_raw_ref, dqa_ref, dka_ref, dva_ref,
             dq1_ref, dk1_ref, dv1_ref, dq4_ref, dk4_ref, dv4_ref, dq16_ref, dk16_ref, dv16_ref, dga_ref, dgb_ref,
             gx_ref, ht_ref, win_ref,
             dgain_ref, dgqa_ref, dgka_ref, dgqb_ref, dgkb_ref, w_vmem, dproj_ref):
        i = pl.program_id(0)
        perm = perm_ref[...]

        @pl.when(i == 0)
        def _():
            pltpu.sync_copy(w_hbm, w_vmem)
            dgain_ref[...] = jnp.zeros_like(dgain_ref)
            dgqa_ref[...] = jnp.zeros_like(dgqa_ref)
            dgka_ref[...] = jnp.zeros_like(dgka_ref)
            dgqb_ref[...] = jnp.zeros_like(dgqb_ref)
            dgkb_ref[...] = jnp.zeros_like(dgkb_ref)

        cos1 = cos_ref[...]
        sin1 = sin_ref[...]
        cos4 = jnp.tile(cos1, (1, 4))
        sin4 = jnp.tile(sin1, (1, 4))

        dt, dg = _qknorm_rope_bwd(dqa_ref[...], qa_raw_ref[...], gqa_ref[...], cos4, sin4, bd256_ref[...])
        dproj_ref[:, C_QA:C_KA] = dt.astype(BF16)
        dgqa_ref[...] += jnp.sum(dg, axis=0, keepdims=True)
        dt, dg = _qknorm_rope_bwd(dka_ref[...], ka_raw_ref[...], gka_ref[...], cos1, sin1, bd128_ref[...])
        dproj_ref[:, C_KA:C_VA] = dt.astype(BF16)
        dgka_ref[...] += jnp.sum(dg, axis=0, keepdims=True)
        dproj_ref[:, C_VA:C_GA] = dva_ref[...].astype(BF16)
        dproj_ref[:, C_GA:C_QB] = dga_ref[...]
        dq = (dq1_ref[...].astype(F32) + _load_folded_bf16(dq4_ref, perm)) + _load_folded_bf16(dq16_ref, perm)
        dt, dg = _qknorm_rope_bwd(dq, qb_raw_ref[...], gqb_ref[...], cos4, sin4, bd256_ref[...])
        dproj_ref[:, C_QB:C_KB] = dt.astype(BF16)
        dgqb_ref[...] += jnp.sum(dg, axis=0, keepdims=True)
        dk = (dk1_ref[...].astype(F32) + _load_folded_bf16(dk4_ref, perm)) + _load_folded_bf16(dk16_ref, perm)
        dt, dg = _qknorm_rope_bwd(dk, kb_raw_ref[...], gkb_ref[...], cos4, sin4, bd256_ref[...])
        dproj_ref[:, C_KB:C_VB] = dt.astype(BF16)
        dgkb_ref[...] += jnp.sum(dg, axis=0, keepdims=True)
        dv = (dv1_ref[...].astype(F32) + _load_folded_bf16(dv4_ref, perm)) + _load_folded_bf16(dv16_ref, perm)
        dproj_ref[:, C_VB:C_GB] = dv.astype(BF16)
        dproj_ref[:, C_GB:C_END] = dgb_ref[...]
        for k, start in enumerate(WIN_START):
            win_ref[k] = dproj_ref[:, start:start + WIN]

        xt = x_ref[...]
        gain_row = gain_ref[...]
        r = lax.rsqrt(jnp.mean(xt * xt, axis=-1, keepdims=True) + EPS)
        xr = xt * r
        ht_ref[...] = (xr * gain_row).T.astype(BF16)
        dh = _dot(dproj_ref[...], w_vmem[...])
        dgain_ref[...] += jnp.sum(dh * xr, axis=0, keepdims=True)
        u = dh * gain_row
        gx_ref[...] = dy_ref[...] + r * (u - xr * jnp.mean(u * xr, axis=-1, keepdims=True))

    def acc_row(w):
        return pl.BlockSpec((1, w), lambda i: (0, 0))

    sds = jax.ShapeDtypeStruct
    any_spec = pl.BlockSpec(memory_space=pl.ANY)
    win_spec = pl.BlockSpec((N_CHIP, tm, WIN), lambda i: (0, i, 0))
    return pl.pallas_call(
        body, name="inproj_bwd", grid=(s // tm,),
        in_specs=[_rows(D_MODEL), _rows(D_MODEL), _whole(gain.shape), any_spec, _rows(LANES), _rows(LANES),
                  _whole(gqa.shape), _whole(gka.shape), _whole(gqb.shape), _whole(gkb.shape), _whole(bd256.shape),
                  _whole(bd128.shape), _whole(perm.shape),
                  _rows(WIDTH), _rows(A_KV_WIDTH), _rows(WIDTH), _rows(WIDTH),
                  _rows(WIDTH), _rows(A_KV_WIDTH), _rows(A_KV_WIDTH)]
                 + [_rows(WIDTH)] * 3 + [_folded_rows(WIDTH)] * 6 + [_rows(WIDTH), _rows(WIDTH)],
        out_specs=(_rows(D_MODEL), pl.BlockSpec((D_MODEL, tm), lambda i: (0, i)), win_spec, acc_row(D_MODEL), acc_row(WIDTH), acc_row(A_KV_WIDTH), acc_row(WIDTH), acc_row(WIDTH)),
        out_shape=(sds((s, D_MODEL), F32), sds((D_MODEL, s), BF16), sds((N_CHIP, s, WIN), BF16),
                   sds((1, D_MODEL), F32),
                   sds((1, WIDTH), F32), sds((1, A_KV_WIDTH), F32), sds((1, WIDTH), F32), sds((1, WIDTH), F32)),
        scratch_shapes=[pltpu.VMEM((IN_WIDTH, D_MODEL), BF16), pltpu.VMEM((tm, IN_WIDTH), BF16)],
        compiler_params=pltpu.CompilerParams(dimension_semantics=("arbitrary",), vmem_limit_bytes=VMEM_LIMIT),
    )(x2, dy, gain, w_bf, cos, sin_s, gqa, gka, gqb, gkb, bd256, bd128, perm, qa_raw, ka_raw, qb_raw, kb_raw,
      dq_a, dk_a, dv_a, *dqkv_b1, *dqkv_b4, *dqkv_b16, dg_a, dg_b)


def _rope_tables(s):
    half = HEAD_DIM // 2
    inv = jnp.tile(ROPE_THETA ** (-jnp.arange(half, dtype=F32) / half), 4)
    sign = jnp.tile(jnp.concatenate([-jnp.ones((half,), F32), jnp.ones((half,), F32)]), 2)
    ang = jnp.arange(s).astype(F32)[:, None] * inv[None, :]
    return jnp.cos(ang), jnp.sin(ang) * sign[None, :]


def _block_diag_ones(w):
    idx = jnp.arange(w) // HEAD_DIM
    return (idx[:, None] == idx[None, :]).astype(BF16)


def _local_step(x2, tgt2, norm_gain, w_in_bf, q_norm_a, k_norm_a, sinks_a, q_norm_b, k_norm_b, w_out_bf):
    s = x2.shape[0]
    cos, sin_s = _rope_tables(s)
    bd256, bd128 = _block_diag_ones(2 * LANES), _block_diag_ones(A_KV_WIDTH)
    gqa = jnp.tile(q_norm_a, (1, HEADS))
    gka = jnp.tile(k_norm_a, (1, 2))
    gqb = jnp.tile(q_norm_b, (1, HEADS))
    gkb = jnp.tile(k_norm_b, (1, HEADS))
    sink_row = jnp.pad(sinks_a, ((0, 0), (0, LANES - HEADS)))
    perm = _fold_matrix()

    (qa, ka, va, qb, kb, vb, qbf, kbf, vbf, qa_raw, ka_raw, g_a, qb_raw, kb_raw, g_b) = _inproj(
        x2, norm_gain, w_in_bf, cos, sin_s, gqa, gka, gqb, gkb, bd256, bd128)

    att_a = _attn_fwd(qa, ka, va, sinks_a, dil=1, max_dist=A_MAX_DIST, name="attn_a_fwd")
    att_b1 = _attn_fwd(qb, kb, vb, None, dil=1, max_dist=B_MAX_DIST, name="attn_b1_fwd")
    att_b4 = _attn_fwd(qbf, kbf, vbf, None, dil=4, max_dist=B_MAX_DIST, name="attn_b4_fwd")
    att_b16 = _attn_fwd(qbf, kbf, vbf, None, dil=16, max_dist=B_MAX_DIST, name="attn_b16_fwd")

    (dy, do_a, do_b, do_bf, dg_a, dg_b, lse_a, lse_b, lse_bf, dl_a, dl_b, dl_bf, gw_out, loss_part,
     dsink) = _outproj(att_a, att_b1, att_b4, att_b16, g_a, g_b, x2, tgt2, w_out_bf, sink_row, perm)

    dq_a, dk_a, dv_a = _attn_bwd(qa, ka, va, do_a, lse_a, dl_a, dil=1, max_dist=A_MAX_DIST, name="attn_a_bwd")
    d_b1 = _attn_bwd(qb, kb, vb, do_b, lse_b, dl_b, dil=1, max_dist=B_MAX_DIST, name="attn_b1_bwd")
    d_b4 = _attn_bwd(qbf, kbf, vbf, do_bf, lse_bf, dl_bf, dil=4, max_dist=B_MAX_DIST, name="attn_b4_bwd")
    d_b16 = _attn_bwd(qbf, kbf, vbf, do_bf, lse_bf, dl_bf, dil=16, max_dist=B_MAX_DIST, name="attn_b16_bwd")

    gx, h_t, wins, dgain, dgqa, dgka, dgqb, dgkb = _inproj_bwd(
        x2, dy, norm_gain, w_in_bf, cos, sin_s, gqa, gka, gqb, gkb, bd256, bd128, perm,
        qa_raw, ka_raw, qb_raw, kb_raw, dq_a, dk_a, dv_a, d_b1, d_b4, d_b16, dg_a, dg_b)
    return loss_part, gx, h_t, wins, gw_out, (dgain, dgqa, dgka, dsink, dgqb, dgkb)


def _position():
    return lax.axis_index("x"), lax.axis_index("y"), lax.axis_index("c")


GATHER_CHUNKS = 2


def _gather_weights(blocks, name):
    n = len(blocks)
    ch = GATHER_CHUNKS
    n_sems = n * (N_CHIP - 1) * ch

    def body(*refs):
        src_refs, dst_refs = refs[:n], refs[n:2 * n]
        ici_send, ici_recv, d2d_send, d2d_recv, local_sems = refs[2 * n:]
        x, y, c = _position()
        b = 2 * x + y
        copies = []
        for k in range(n):
            local = pltpu.make_async_copy(src_refs[k], dst_refs[k].at[b], local_sems.at[k])
            local.start()
            copies.append(local)

        def rows(k, core, j):
            half = blocks[k].shape[0] // 2
            return pl.ds(core * half + j * (half // ch), half // ch)

        plan = []
        for d in range(1, N_CHIP):
            px, py = x ^ (d >> 1), y ^ (d & 1)
            for j in range(ch):
                for k in range(n):
                    plan.append((px, py, 2 * px + py, k, j, ((d - 1) * ch + j) * n + k))
        sends = []
        for px, py, pb, k, j, sem in plan:
            send = pltpu.make_async_remote_copy(
                src_ref=src_refs[k].at[rows(k, c, j)], dst_ref=dst_refs[k].at[b, rows(k, c, j)],
                send_sem=ici_send.at[sem], recv_sem=ici_recv.at[sem], device_id=(px, py, c), device_id_type=MESH)
            send.start()
            sends.append(send)
        for px, py, pb, k, j, sem in plan:
            landed = dst_refs[k].at[pb, rows(k, c, j)]
            pltpu.make_async_remote_copy(
                src_ref=landed, dst_ref=landed, send_sem=ici_send.at[sem], recv_sem=ici_recv.at[sem],
                device_id=(px, py, c), device_id_type=MESH).wait_recv()
            forward = pltpu.make_async_remote_copy(
                src_ref=landed, dst_ref=landed, send_sem=d2d_send.at[sem], recv_sem=d2d_recv.at[sem],
                device_id=(x, y, 1 - c), device_id_type=MESH)
            forward.start()
            sends.append(forward)
        for px, py, pb, k, j, sem in plan:
            passed = dst_refs[k].at[pb, rows(k, 1 - c, j)]
            pltpu.make_async_remote_copy(
                src_ref=passed, dst_ref=passed, send_sem=d2d_send.at[sem], recv_sem=d2d_recv.at[sem],
                device_id=(x, y, 1 - c), device_id_type=MESH).wait_recv()
        for send in sends:
            send.wait_send()
        for local in copies:
            local.wait()

    vmem_spec = pl.BlockSpec(memory_space=pltpu.VMEM)
    out_shape = tuple(jax.ShapeDtypeStruct((N_CHIP,) + a.shape, a.dtype) for a in blocks)
    return pl.pallas_call(
        body, name=name, in_specs=[vmem_spec] * n, out_specs=tuple([vmem_spec] * n), out_shape=out_shape,
        scratch_shapes=[pltpu.SemaphoreType.DMA((n_sems,)) for _ in range(4)] + [pltpu.SemaphoreType.DMA((n,))],
        compiler_params=pltpu.CompilerParams(vmem_limit_bytes=VMEM_LIMIT),
    )(*blocks)


def _grad_reduce(order, h_t, wins, gw_out, small):
    s = h_t.shape[1]
    tk = GRAD_ROWS
    n_i = s // tk
    half = D_MODEL // 2
    o_half = OUT_ROWS // 2
    n_rel = N_CHIP - 1

    def body(order_ref, ht_ref, win_ref, gwo_ref, small_ref,
             win_out, wout_out, small_out,
             acc, mine, s1, r1, s2, r2, so1, ro1, so2, ro2, pair_in, pair_o, small_land,
             s1_send, s1_recv, s2_send, s2_recv, o1_send, o1_recv, o2_send, o2_recv,
             pair_send, pair_recv, small_send, small_recv):
        j = pl.program_id(0)
        i = pl.program_id(1)
        x, y, c = _position()
        me = 4 * x + 2 * y + c
        sibling = (x, y, 1 - c)
        my_rows = pl.ds(pl.multiple_of(c * half, half), half)
        sib_rows = pl.ds(pl.multiple_of((1 - c) * half, half), half)

        def chip_of(rel):
            return x ^ (rel >> 1), y ^ (rel & 1)

        def level1(k):
            return pltpu.make_async_remote_copy(src_ref=s1.at[k], dst_ref=r1.at[k], send_sem=s1_send.at[k],
                                                recv_sem=s1_recv.at[k], device_id=sibling, device_id_type=MESH)

        def level2(k):
            px, py = chip_of(RELATIONS[k])
            return pltpu.make_async_remote_copy(src_ref=s2.at[k], dst_ref=r2.at[k], send_sem=s2_send.at[k],
                                                recv_sem=s2_recv.at[k], device_id=(px, py, c), device_id_type=MESH)

        def out_level1(bk):
            return pltpu.make_async_remote_copy(src_ref=so1.at[bk], dst_ref=ro1.at[bk], send_sem=o1_send.at[bk],
                                                recv_sem=o1_recv.at[bk], device_id=sibling, device_id_type=MESH)

        def out_level2(k):
            px, py = chip_of(RELATIONS[k])
            return pltpu.make_async_remote_copy(src_ref=so2.at[k], dst_ref=ro2.at[k], send_sem=o2_send.at[k],
                                                recv_sem=o2_recv.at[k], device_id=(px, py, c), device_id_type=MESH)

        def small_copy(d):
            px, py, pc = x ^ (d >> 2), y ^ ((d >> 1) & 1), c ^ (d & 1)
            return pltpu.make_async_remote_copy(src_ref=small_ref, dst_ref=small_land.at[me],
                                                send_sem=small_send.at[d], recv_sem=small_recv.at[d],
                                                device_id=(px, py, pc), device_id_type=MESH)

        def pair_copy(k, buf):
            return pltpu.make_async_remote_copy(src_ref=buf.at[0], dst_ref=buf.at[1], send_sem=pair_send.at[k],
                                                recv_sem=pair_recv.at[k], device_id=sibling, device_id_type=MESH)

        def out_rows(bk, core):
            return pl.ds(pl.multiple_of(bk * OUT_ROWS + core * o_half, o_half), o_half)

        @pl.when((j == 0) & (i == 0))
        def _():
            for d in range(1, N_DEV):
                small_copy(d).start()
            small_land[me] = small_ref[...]
            for bk in range(N_CHIP):
                so1[bk] = gwo_ref[out_rows(bk, 1 - c), :].astype(BF16)
                out_level1(bk).start()

        @pl.when((j == 0) & (i == 1))
        def _():
            b = 2 * x + y
            for bk in range(N_CHIP):
                out_level1(bk).wait_recv()
            for k in range(n_rel):
                px, py = chip_of(RELATIONS[k])
                bk = 2 * px + py
                so2[k] = (gwo_ref[out_rows(bk, c), :] + ro1[bk].astype(F32)).astype(BF16)
                out_level2(k).start()

        @pl.when(i == 0)
        def _():
            acc[...] = jnp.zeros_like(acc)

        for n0 in range(0, WIN, ACC_COLS):
            n1 = min(n0 + ACC_COLS, WIN)
            acc[:, n0:n1] += _dot(ht_ref[...], win_ref[:, n0:n1])

        for k in range(N_CHIP):
            @pl.when((j == k) & (i == n_i - 1))
            def _(k=k):
                s1[k] = acc[sib_rows, :].astype(BF16)
                level1(k).start()
                mine[...] = acc[my_rows, :]

            if k < n_rel:
                @pl.when((j == k + 1) & (i == 1))
                def _(k=k):
                    level1(k).wait_recv()
                    s2[k] = (mine[...] + r1[k].astype(F32)).astype(BF16)
                    level2(k).start()

        @pl.when((j == N_CHIP - 1) & (i == n_i - 1))
        def _():
            b = 2 * x + y
            level1(N_CHIP - 1).wait_recv()
            total = mine[...] + r1[N_CHIP - 1].astype(F32)
            for k in range(n_rel):
                level2(k).wait_recv()
                total = total + r2[k].astype(F32)
            total = total.T
            pair_in[0] = total
            pair_copy(0, pair_in).start()
            total_o = gwo_ref[out_rows(b, c), :] + ro1[b].astype(F32)
            for k in range(n_rel):
                out_level2(k).wait_recv()
                total_o = total_o + ro2[k].astype(F32)
            pair_o[0] = total_o
            pair_copy(1, pair_o).start()
            win_out[c] = total
            wout_out[c] = total_o
            for d in range(1, N_DEV):
                small_copy(d).wait_recv()
            small_out[...] = small_land[...]
            pair_copy(0, pair_in).wait_recv()
            win_out[1 - c] = pair_in[1]
            pair_copy(1, pair_o).wait_recv()
            wout_out[1 - c] = pair_o[1]
            for d in range(1, N_DEV):
                small_copy(d).wait_send()
            for k in range(N_CHIP):
                level1(k).wait_send()
                out_level1(k).wait_send()
            for k in range(n_rel):
                level2(k).wait_send()
                out_level2(k).wait_send()
            pair_copy(0, pair_in).wait_send()
            pair_copy(1, pair_o).wait_send()

    vmem = pl.BlockSpec(memory_space=pltpu.VMEM)
    dma = pltpu.SemaphoreType.DMA
    sds = jax.ShapeDtypeStruct
    grid_spec = pltpu.PrefetchScalarGridSpec(
        num_scalar_prefetch=1, grid=(N_CHIP, n_i),
        in_specs=[pl.BlockSpec((D_MODEL, tk), lambda j, i, order: (0, i)),
                  pl.BlockSpec((None, tk, WIN), lambda j, i, order: (order[j], i, 0)), vmem, vmem],
        out_specs=(vmem, vmem, vmem),
        scratch_shapes=[
            pltpu.VMEM((D_MODEL, WIN), F32), pltpu.VMEM((half, WIN), F32),
            pltpu.VMEM((N_CHIP, half, WIN), BF16), pltpu.VMEM((N_CHIP, half, WIN), BF16),
            pltpu.VMEM((n_rel, half, WIN), BF16), pltpu.VMEM((n_rel, half, WIN), BF16),
            pltpu.VMEM((N_CHIP, o_half, D_MODEL), BF16), pltpu.VMEM((N_CHIP, o_half, D_MODEL), BF16),
            pltpu.VMEM((n_rel, o_half, D_MODEL), BF16), pltpu.VMEM((n_rel, o_half, D_MODEL), BF16),
            pltpu.VMEM((2, WIN, half), F32), pltpu.VMEM((2, o_half, D_MODEL), F32),
            pltpu.VMEM((N_DEV, PACK_ROWS, D_MODEL), F32),
            dma((N_CHIP,)), dma((N_CHIP,)), dma((n_rel,)), dma((n_rel,)),
            dma((N_CHIP,)), dma((N_CHIP,)), dma((n_rel,)), dma((n_rel,)),
            dma((2,)), dma((2,)), dma((N_DEV,)), dma((N_DEV,))])
    return pl.pallas_call(
        body, name="grad_w_in_reduce", grid_spec=grid_spec,
        out_shape=(sds((2, WIN, half), F32), sds((2, o_half, D_MODEL), F32), sds((N_DEV, PACK_ROWS, D_MODEL), F32)),
        compiler_params=pltpu.CompilerParams(dimension_semantics=("arbitrary", "arbitrary"),
                                             vmem_limit_bytes=VMEM_LIMIT),
    )(order, h_t, wins, gw_out, small)


ADAM_STEPS = 4


def _adamw_math(w, g, m, v):
    m = ADAM_B1 * m + (1.0 - ADAM_B1) * g
    v = ADAM_B2 * v + (1.0 - ADAM_B2) * (g * g)
    m_hat = m / (1.0 - ADAM_B1 ** ADAM_STEP)
    v_hat = v / (1.0 - ADAM_B2 ** ADAM_STEP)
    delta = -ADAM_LR * (m_hat / (jnp.sqrt(v_hat) + ADAM_EPS) + ADAM_WD * w)
    return delta, m, v


def _adamw(w, g, m, v, name):
    r, c = w.shape

    def body(w_ref, g_ref, m_ref, v_ref, d_ref, nm_ref, nv_ref):
        delta, nm, nv = _adamw_math(w_ref[...], g_ref[...], m_ref[...], v_ref[...])
        d_ref[...] = delta
        nm_ref[...] = nm
        nv_ref[...] = nv

    rows = r // ADAM_STEPS
    assert rows * ADAM_STEPS == r and rows % 8 == 0
    spec = pl.BlockSpec((rows, c), lambda i: (i, 0))
    shape = jax.ShapeDtypeStruct((r, c), F32)
    return pl.pallas_call(
        body, name=name, grid=(ADAM_STEPS,), in_specs=[spec] * 4, out_specs=(spec,) * 3,
        out_shape=(shape,) * 3, compiler_params=pltpu.CompilerParams(vmem_limit_bytes=VMEM_LIMIT),
    )(w, g, m, v)


PACK_ROWS = 8


def _fold_heads(v):
    y = v[:, 0:LANES]
    for j in range(1, v.shape[1] // LANES):
        y = y + v[:, j * LANES:(j + 1) * LANES]
    return y + pltpu.roll(y, HEAD_DIM, 1)


def _small_adamw(recv, w_p, m_p, v_p):
    def body(r_ref, w_ref, m_ref, v_ref, g_ref, d_ref, nm_ref, nv_ref, loss_ref):
        tot = r_ref[0]
        for j in range(1, N_DEV):
            tot = tot + r_ref[j]
        loss_ref[...] = tot[3:4, 0:LANES]
        row1 = tot[1:2, :]
        row2 = tot[2:3, :]
        pieces = [_fold_heads(row1[:, 0:WIDTH]), _fold_heads(row2[:, WIDTH:WIDTH + A_KV_WIDTH]),
                  _fold_heads(row1[:, WIDTH:2 * WIDTH]), _fold_heads(row2[:, 0:WIDTH]),
                  row2[:, WIDTH + A_KV_WIDTH:WIDTH + 2 * A_KV_WIDTH], jnp.zeros((1, 3 * LANES), F32)]
        g = jnp.concatenate([tot[0:1, :], jnp.concatenate(pieces, axis=1), jnp.zeros((PACK_ROWS - 2, D_MODEL), F32)],
                            axis=0)
        g_ref[...] = g
        delta, nm, nv = _adamw_math(w_ref[...], g, m_ref[...], v_ref[...])
        d_ref[...] = delta
        nm_ref[...] = nm
        nv_ref[...] = nv

    shape = jax.ShapeDtypeStruct((PACK_ROWS, D_MODEL), F32)
    return pl.pallas_call(body, name="small_adamw",
                          out_shape=(shape,) * 4 + (jax.ShapeDtypeStruct((1, LANES), F32),))(recv, w_p, m_p, v_p)


def _pack_small(norm_gain, q_a, k_a, q_b, k_b, sinks):
    def lane_pad(a):
        return jnp.pad(a, ((0, 0), (0, LANES - a.shape[1])))
    row1 = jnp.concatenate([lane_pad(q_a), lane_pad(k_a), lane_pad(q_b), lane_pad(k_b), lane_pad(sinks),
                            jnp.zeros((1, 3 * LANES), F32)], axis=1)
    return jnp.concatenate([norm_gain, row1, jnp.zeros((PACK_ROWS - 2, D_MODEL), F32)], axis=0)


def _unpack_small(p):
    return (p[0:1, :], p[1:2, 0:HEAD_DIM], p[1:2, LANES:LANES + HEAD_DIM], p[1:2, 2 * LANES:2 * LANES + HEAD_DIM],
            p[1:2, 3 * LANES:3 * LANES + HEAD_DIM], p[1:2, 4 * LANES:4 * LANES + HEADS])


def kernel(x, norm_gain, w_in, q_norm_a, k_norm_a, sinks_a, q_norm_b, k_norm_b, w_out, loss_target, m_norm_gain, m_w_in, m_q_norm_a, m_k_norm_a, m_sinks_a, m_q_norm_b, m_k_norm_b, m_w_out, v_norm_gain, v_w_in, v_q_norm_a, v_k_norm_a, v_sinks_a, v_q_norm_b, v_k_norm_b, v_w_out):
    chip = 2 * lax.axis_index("x") + lax.axis_index("y")

    w_in_t, m_w_in_t, v_w_in_t = w_in[0].T, m_w_in[0].T, v_w_in[0].T

    w_in_all, w_out_all = _gather_weights([w_in_t.astype(BF16), w_out[0].astype(BF16)], "gather_weights")
    w_in_bf = w_in_all.reshape(IN_WIDTH, D_MODEL)
    w_out_bf = w_out_all.reshape(D_MODEL, D_MODEL)

    loss_part, gx, h_t, wins, gw_out, (dgain, dgqa, dgka, dsink, dgqb, dgkb) = _local_step(
        x[0], loss_target[0], norm_gain, w_in_bf, q_norm_a, k_norm_a, sinks_a, q_norm_b, k_norm_b, w_out_bf)

    small = jnp.concatenate([
        dgain, jnp.concatenate([dgqa, dgqb], axis=1),
        jnp.concatenate([dgkb, dgka, dsink, jnp.zeros((1, D_MODEL - WIDTH - 2 * A_KV_WIDTH), F32)], axis=1),
        jnp.pad(loss_part, ((0, 0), (0, D_MODEL - LANES))),
        jnp.zeros((PACK_ROWS - 4, D_MODEL), F32)], axis=0)
    order = (chip ^ jnp.array(RELATIONS, jnp.int32)).astype(jnp.int32)
    win_sum, wout_sum, small_recv = _grad_reduce(order, h_t, wins, gw_out, small)
    shift = jnp.array(WIN_SHIFT, jnp.int32)[chip]
    g_w_in_t = lax.dynamic_slice_in_dim(win_sum.transpose(1, 0, 2).reshape(WIN, D_MODEL), shift, IN_COLS, axis=0)
    g_w_out = wout_sum.reshape(OUT_ROWS, D_MODEL)

    d_w_in, nm_w_in, nv_w_in = (a.T for a in _adamw(w_in_t, g_w_in_t, m_w_in_t, v_w_in_t, "adamw_w_in"))
    g_w_in = g_w_in_t.T
    d_w_out, nm_w_out, nv_w_out = _adamw(w_out[0], g_w_out, m_w_out[0], v_w_out[0], "adamw_w_out")
    g_s, d_s, nm_s, nv_s, loss_row = _small_adamw(
        small_recv,
        _pack_small(norm_gain, q_norm_a, k_norm_a, q_norm_b, k_norm_b, sinks_a),
        _pack_small(m_norm_gain, m_q_norm_a, m_k_norm_a, m_q_norm_b, m_k_norm_b, m_sinks_a),
        _pack_small(v_norm_gain, v_q_norm_a, v_k_norm_a, v_q_norm_b, v_k_norm_b, v_sinks_a))
    loss = loss_row[0, 0]

    def leaves(small_packed, big_in, big_out):
        gain, qa, ka, qb, kb, sk = _unpack_small(small_packed)
        return (gain, big_in[None], qa, ka, sk, qb, kb, big_out[None])

    return ((loss, gx[None]) + leaves(g_s, g_w_in, g_w_out) + leaves(d_s, d_w_in, d_w_out)
            + leaves(nm_s, nm_w_in, nm_w_out) + leaves(nv_s, nv_w_in, nv_w_out))
```

```python
import jax
import jax.numpy as jnp
from jax import lax
from jax.experimental import pallas as pl
from jax.experimental.pallas import tpu as pltpu

F32 = jnp.float32
BF16 = jnp.bfloat16

D_MODEL = 1024
HEAD_DIM = 64
HEADS = 8
WIDTH = HEADS * HEAD_DIM
A_KV_WIDTH = 2 * HEAD_DIM
BLOCK = 128
LANES = 128
FOLD = 16
A_MAX_DIST = 127
B_MAX_DIST = 128
ROPE_THETA = 10000.0
EPS = 1e-6
NEG = -1e30
SCALE = HEAD_DIM ** -0.5

IN_WIDTH = 3328
C_QA, C_KA, C_VA, C_GA, C_QB, C_KB, C_VB, C_GB, C_END = 0, 512, 640, 768, 1280, 1792, 2304, 2816, 3328

N_DEV = 8
N_CHIP = 4
MESH = pl.DeviceIdType.MESH
IN_COLS = IN_WIDTH // N_CHIP
WIN = 896
WIN_START = (0, 768, 1664, 2432)
WIN_SHIFT = (0, 64, 0, 64)
OUT_ROWS = D_MODEL // N_CHIP
RELATIONS = (3, 1, 2, 0)

ADAM_LR = 0.001
ADAM_B1 = 0.9
ADAM_B2 = 0.999
ADAM_EPS = 1e-08
ADAM_WD = 0.01
ADAM_STEP = 10

ROW_TILE = 256
FOLD_ROWS = ROW_TILE // FOLD
GRAD_ROWS = 1024
ACC_COLS = 256
VMEM_LIMIT = 56 * 1024 * 1024


def _dot(a, b):
    return jnp.dot(a, b, preferred_element_type=F32)


def _dot_nt(a, b):
    return lax.dot_general(a, b, (((1,), (1,)), ((), ())), preferred_element_type=F32)


def _dot_tn(a, b):
    return lax.dot_general(a, b, (((0,), (0,)), ((), ())), preferred_element_type=F32)


def _head_sum(z, bd):
    w = bd.shape[0]
    zb = z.astype(BF16)
    parts = [_dot(zb[:, a:a + w], bd) for a in range(0, z.shape[1], w)]
    return parts[0] if len(parts) == 1 else jnp.concatenate(parts, axis=1)


def _swap_halves(t):
    w = t.shape[1]
    lane = lax.broadcasted_iota(jnp.int32, t.shape, 1)
    return jnp.where(lane % HEAD_DIM < HEAD_DIM // 2, pltpu.roll(t, w - 32, 1), pltpu.roll(t, 32, 1))


def _qknorm_rope(t, g, cos, sin_s, bd):
    r = lax.rsqrt(_head_sum(t * t, bd) * (1.0 / HEAD_DIM) + EPS)
    n = (t * r) * g
    return n * cos + _swap_halves(n) * sin_s


def _qknorm_rope_bwd(dout, t, g, cos, sin_s, bd):
    dout, t = dout.astype(F32), t.astype(F32)
    dn = dout * cos + _swap_halves(dout * sin_s)
    r = lax.rsqrt(_head_sum(t * t, bd) * (1.0 / HEAD_DIM) + EPS)
    tr = t * r
    u = dn * g
    dt = r * (u - tr * (_head_sum(u * tr, bd) * (1.0 / HEAD_DIM)))
    return dt, dn * tr


def _sigmoid(g):
    return 1.0 / (1.0 + jnp.exp(-g))


def _expand_heads(st):
    t = st.shape[0]
    lane = lax.broadcasted_iota(jnp.int32, (t, LANES), 1)
    chunks = []
    for c in range(WIDTH // LANES):
        chunks.append(jnp.where(lane < HEAD_DIM, st[:, 2 * c:2 * c + 1], st[:, 2 * c + 1:2 * c + 2]))
    return jnp.concatenate(chunks, axis=1)


def _reduce_heads(z):
    t = z.shape[0]
    lane = lax.broadcasted_iota(jnp.int32, (t, LANES), 1)
    out = jnp.zeros((t, LANES), F32)
    for c in range(WIDTH // LANES):
        zc = z[:, c * LANES:(c + 1) * LANES]
        for ph in range(2):
            s = jnp.sum(jnp.where((lane // HEAD_DIM) == ph, zc, 0.0), axis=-1, keepdims=True)
            out = jnp.where(lane == 2 * c + ph, s, out)
    return out


def _fold_scratch(w):
    return pltpu.VMEM((w // LANES, ROW_TILE, LANES), F32)


def _store_folded(out_ref, val, scr, col0=0):
    w = val.shape[1]
    n = w // LANES
    for c in range(n):
        scr[c] = val[:, c * LANES:(c + 1) * LANES]
    for r in range(FOLD):
        piece = [scr[c, pl.ds(r, FOLD_ROWS, stride=FOLD), :] for c in range(n)]
        out_ref[r, :, col0:col0 + w] = (piece[0] if n == 1 else jnp.concatenate(piece, axis=1)).astype(out_ref.dtype)


def _load_folded(in_ref, scr):
    n = in_ref.shape[2] // LANES
    for r in range(FOLD):
        blk = in_ref[r].astype(F32)
        for c in range(n):
            scr[c, pl.ds(r, FOLD_ROWS, stride=FOLD), :] = blk[:, c * LANES:(c + 1) * LANES]
    return scr[0] if n == 1 else jnp.concatenate([scr[c] for c in range(n)], axis=1)


def _fold_matrix():
    f = jnp.arange(ROW_TILE)
    return (jnp.arange(ROW_TILE)[None, :] == (FOLD * (f % FOLD_ROWS) + f // FOLD_ROWS)[:, None]).astype(BF16)


def _store_folded_bf16(out_ref, val, perm):
    folded = _dot(perm, val.astype(BF16)).astype(out_ref.dtype)
    for r in range(FOLD):
        out_ref[r] = folded[r * FOLD_ROWS:(r + 1) * FOLD_ROWS]


def _load_folded_bf16(in_ref, perm):
    blk = jnp.concatenate([in_ref[r] for r in range(FOLD)], axis=0)
    return _dot(perm, blk)


def _rows(w, tm=ROW_TILE):
    return pl.BlockSpec((tm, w), lambda i: (i, 0))


def _folded_rows(w):
    return pl.BlockSpec((FOLD, FOLD_ROWS, w), lambda i: (0, i, 0))


def _whole(shape):
    return pl.BlockSpec(shape, lambda i: (0,) * len(shape))


def _inproj(x2, gain, w_bf, cos, sin_s, gqa, gka, gqb, gkb, bd256, bd128):
    s = x2.shape[0]
    tm = ROW_TILE

    def body(x_ref, gain_ref, w_hbm, cos_ref, sin_ref, gqa_ref, gka_ref, gqb_ref, gkb_ref, bd256_ref, bd128_ref,
             qa_ref, kva_ref, qb_ref, kvb_ref, qbf_ref, kvbf_ref,
             qa_raw_ref, ka_raw_ref, ga_ref, qb_raw_ref, kb_raw_ref, gb_ref, w_vmem, scr):
        @pl.when(pl.program_id(0) == 0)
        def _():
            pltpu.sync_copy(w_hbm, w_vmem)

        xt = x_ref[...]
        r = lax.rsqrt(jnp.mean(xt * xt, axis=-1, keepdims=True) + EPS)
        h = ((xt * r) * gain_ref[...]).astype(BF16)
        cos1 = cos_ref[...]
        sin1 = sin_ref[...]
        cos4 = jnp.tile(cos1, (1, 4))
        sin4 = jnp.tile(sin1, (1, 4))

        def seg(a, b):
            return _dot_nt(h, w_vmem[a:b, :])

        t = seg(C_QA, C_KA)
        qa_raw_ref[...] = t.astype(BF16)
        qa_ref[...] = (_qknorm_rope(t, gqa_ref[...], cos4, sin4, bd256_ref[...]) * SCALE).astype(BF16)
        t = seg(C_KA, C_VA)
        ka_raw_ref[...] = t.astype(BF16)
        kva_ref[:, :A_KV_WIDTH] = _qknorm_rope(t, gka_ref[...], cos1, sin1, bd128_ref[...]).astype(BF16)
        kva_ref[:, A_KV_WIDTH:] = seg(C_VA, C_GA).astype(BF16)
        ga_ref[...] = seg(C_GA, C_QB).astype(BF16)
        t = seg(C_QB, C_KB)
        qb_raw_ref[...] = t.astype(BF16)
        t = _qknorm_rope(t, gqb_ref[...], cos4, sin4, bd256_ref[...]) * SCALE
        qb_ref[...] = t.astype(BF16)
        _store_folded(qbf_ref, t, scr)
        t = seg(C_KB, C_VB)
        kb_raw_ref[...] = t.astype(BF16)
        t = _qknorm_rope(t, gkb_ref[...], cos4, sin4, bd256_ref[...])
        kvb_ref[:, :WIDTH] = t.astype(BF16)
        _store_folded(kvbf_ref, t, scr)
        t = seg(C_VB, C_GB)
        kvb_ref[:, WIDTH:] = t.astype(BF16)
        _store_folded(kvbf_ref, t, scr, WIDTH)
        gb_ref[...] = seg(C_GB, C_END).astype(BF16)

    sds = jax.ShapeDtypeStruct
    ln = s // FOLD
    out_shape = (sds((s, WIDTH), BF16), sds((s, 2 * A_KV_WIDTH), BF16), sds((s, WIDTH), BF16),
                 sds((s, 2 * WIDTH), BF16), sds((FOLD, ln, WIDTH), BF16), sds((FOLD, ln, 2 * WIDTH), BF16),
                 sds((s, WIDTH), BF16), sds((s, A_KV_WIDTH), BF16), sds((s, WIDTH), BF16),
                 sds((s, WIDTH), BF16), sds((s, WIDTH), BF16), sds((s, WIDTH), BF16))
    out_specs = (_rows(WIDTH), _rows(2 * A_KV_WIDTH), _rows(WIDTH), _rows(2 * WIDTH),
                 _folded_rows(WIDTH), _folded_rows(2 * WIDTH),
                 _rows(WIDTH), _rows(A_KV_WIDTH), _rows(WIDTH), _rows(WIDTH), _rows(WIDTH), _rows(WIDTH))
    return pl.pallas_call(
        body, name="inproj_fwd", grid=(s // tm,),
        in_specs=[_rows(D_MODEL), _whole(gain.shape), pl.BlockSpec(memory_space=pl.ANY), _rows(LANES), _rows(LANES),
                  _whole(gqa.shape), _whole(gka.shape), _whole(gqb.shape), _whole(gkb.shape), _whole(bd256.shape),
                  _whole(bd128.shape)],
        out_specs=out_specs, out_shape=out_shape,
        scratch_shapes=[pltpu.VMEM((IN_WIDTH, D_MODEL), BF16), _fold_scratch(WIDTH)],
        compiler_params=pltpu.CompilerParams(dimension_semantics=("arbitrary",), vmem_limit_bytes=VMEM_LIMIT),
    )(x2, gain, w_bf, cos, sin_s, gqa, gka, gqb, gkb, bd256, bd128)


def _seq_pos(idx, dil):
    if dil == 4:
        return 4 * (idx % 32) + idx // 32
    return idx


SOFTMAX_ROWS = 64


def _upper_mask(dil, r0=0, rows=2 * BLOCK):
    qi = (lax.broadcasted_iota(jnp.int32, (rows, BLOCK), 0) + r0) % BLOCK
    kj = lax.broadcasted_iota(jnp.int32, (rows, BLOCK), 1)
    return _seq_pos(kj, dil) > _seq_pos(qi, dil)


def _eye_mask(r0=0, rows=2 * BLOCK):
    qi = (lax.broadcasted_iota(jnp.int32, (rows, BLOCK), 0) + r0) % BLOCK
    kj = lax.broadcasted_iota(jnp.int32, (rows, BLOCK), 1)
    return qi == kj


def _stack_heads(a2, c, gqa):
    lane = lax.broadcasted_iota(jnp.int32, (1, LANES), 1) // HEAD_DIM
    zero = jnp.zeros_like(a2)
    if gqa:
        keep = lane == (c // 2)
        return jnp.concatenate([jnp.where(keep, a2, zero), jnp.where(keep, _swap_heads(a2), zero)], axis=0)
    return jnp.concatenate([jnp.where(lane == 0, a2, zero), jnp.where(lane == 1, a2, zero)], axis=0)


def _unstack_heads(a, c, gqa):
    lane = lax.broadcasted_iota(jnp.int32, (1, LANES), 1) // HEAD_DIM
    if gqa:
        return jnp.where(lane == (c // 2), a[:BLOCK], _swap_heads(a[BLOCK:]))
    return jnp.where(lane == 0, a[:BLOCK], a[BLOCK:])


def _stacked_head_ids(c, gqa):
    if gqa:
        return 2 * c + c // 2, 2 * c + 1 - c // 2
    return 2 * c, 2 * c + 1


def _per_head_rows(blk, heads):
    return jnp.concatenate([blk[:, heads[0]:heads[0] + 1], blk[:, heads[1]:heads[1] + 1]], axis=0)


def _attn_view(a, dil):
    if dil == 1:
        return a[None]
    if dil == 4:
        return a.reshape(4, 4, a.shape[1], a.shape[2])
    return a


def _attn_unview(a, dil):
    if dil == 1:
        return a[0]
    if dil == 4:
        return a.reshape(FOLD, a.shape[2], a.shape[3])
    return a


def _attn_specs(dil, nb):
    if dil == 4:
        def spec(fn):
            return lambda w: pl.BlockSpec((4, None, BLOCK // 4, w), lambda r, i: (0, r, fn(i), 0))
    else:
        def spec(fn):
            return lambda w: pl.BlockSpec((None, BLOCK, w), lambda r, i: (r, fn(i), 0))
    return spec


def _blk_load(ref, sl, dil):
    if dil == 4:
        return ref[:, :, sl].reshape(BLOCK, sl.stop - sl.start)
    return ref[:, sl]


def _blk_store(ref, sl, val, dil):
    val = val.astype(ref.dtype)
    if dil == 4:
        ref[:, :, sl] = val.reshape(4, BLOCK // 4, sl.stop - sl.start)
    else:
        ref[:, sl] = val


def _swap_heads(a):
    return pltpu.roll(a.astype(F32), HEAD_DIM, 1).astype(a.dtype)


STAT_SHIFT = 8


def _attn_fwd(q, kv, sinks, *, dil, max_dist, name):
    q, kv = _attn_view(q, dil), _attn_view(kv, dil)
    kw = kv.shape[-1] // 2
    gqa = kw == A_KV_WIDTH
    n_seq = dil
    nb = (q.shape[-2] * (4 if dil == 4 else 1)) // BLOCK
    with_sinks = sinks is not None
    all_lanes = slice(0, LANES)
    assert max_dist in (BLOCK - 1, BLOCK)
    diag = max_dist == BLOCK

    def body(*refs):
        if with_sinks:
            q_ref, kvp_ref, kvc_ref, sink_ref, o_ref, ml_ref = refs
        else:
            q_ref, kvp_ref, kvc_ref, o_ref, ml_ref = refs

        def block(has_prev):
            lane = lax.broadcasted_iota(jnp.int32, (1, LANES), 1)
            with_diag = diag and has_prev
            upper, eye = _upper_mask(dil), _eye_mask()
            first_rows = lax.broadcasted_iota(jnp.int32, (2 * BLOCK, 1), 0) < BLOCK
            ml_blk = jnp.zeros((BLOCK, LANES), F32)
            chunks = range(WIDTH // LANES)
            scores, values = [], []
            for c in chunks:
                sl = slice(c * LANES, (c + 1) * LANES)
                ksl = slice(0, LANES) if gqa else sl
                vsl = slice(ksl.start + kw, ksl.stop + kw)
                kcur, vcur = _blk_load(kvc_ref, ksl, dil), _blk_load(kvc_ref, vsl, dil)
                qs = _stack_heads(_blk_load(q_ref, sl, dil), c, gqa)
                if has_prev:
                    kcur = jnp.concatenate([_blk_load(kvp_ref, ksl, dil), kcur], axis=0)
                    vcur = jnp.concatenate([_blk_load(kvp_ref, vsl, dil), vcur], axis=0)
                scores.append(_dot_nt(qs, kcur))
                values.append(vcur)
            probs = []
            for c in chunks:
                heads = _stacked_head_ids(c, gqa)
                s = scores[c]
                if has_prev:
                    s_p = s[:, :BLOCK]
                    sc = jnp.where(upper, s_p, s[:, BLOCK:])
                else:
                    sc = jnp.where(upper, NEG, s)
                if with_diag:
                    sd = jnp.where(eye, s_p, NEG)
                    m = jnp.max(jnp.maximum(sc, sd), axis=-1, keepdims=True)
                else:
                    m = jnp.max(sc, axis=-1, keepdims=True)
                if with_sinks:
                    sk = jnp.where(first_rows, sink_ref[0, heads[0]], sink_ref[0, heads[1]])
                    m = jnp.maximum(m, sk)
                p = jnp.exp(sc - m)
                zero = jnp.zeros_like(p)
                if with_diag:
                    pd = jnp.exp(sd - m)
                    l = jnp.sum(p + pd, axis=-1, keepdims=True)
                else:
                    pd = zero
                    l = jnp.sum(p, axis=-1, keepdims=True)
                if with_sinks:
                    l = l + jnp.exp(sk - m)
                pf = jnp.where(upper, zero, p)
                if has_prev:
                    pf = jnp.concatenate([jnp.where(upper, p, pd), pf], axis=1)
                probs.append(pf.astype(BF16))
                for n, h in enumerate(heads):
                    rows = slice(n * BLOCK, (n + 1) * BLOCK)
                    ml_blk = jnp.where(lane == h, m[rows], ml_blk)
                    ml_blk = jnp.where(lane == h + STAT_SHIFT, l[rows], ml_blk)
            for c in chunks:
                sl = slice(c * LANES, (c + 1) * LANES)
                _blk_store(o_ref, sl, _unstack_heads(_dot(probs[c], values[c]), c, gqa), dil)
            _blk_store(ml_ref, all_lanes, ml_blk, dil)

        @pl.when(pl.program_id(1) == 0)
        def _():
            block(False)

        @pl.when(pl.program_id(1) > 0)
        def _():
            block(True)

    spec = _attn_specs(dil, nb)
    cur = spec(lambda i: i)
    prev = spec(lambda i: jnp.maximum(i - 1, 0))
    in_specs = [cur(WIDTH), prev(2 * kw), cur(2 * kw)]
    args = [q, kv, kv]
    if with_sinks:
        in_specs.append(pl.BlockSpec(memory_space=pltpu.SMEM))
        args.append(sinks)
    stats = jax.ShapeDtypeStruct(q.shape[:-1] + (LANES,), F32)
    o, ml = pl.pallas_call(
        body, name=name, grid=(n_seq, nb), in_specs=in_specs,
        out_specs=(cur(WIDTH), cur(LANES)),
        out_shape=(jax.ShapeDtypeStruct(q.shape, BF16), stats),
        compiler_params=pltpu.CompilerParams(dimension_semantics=("arbitrary", "arbitrary")),
    )(*args)
    return _attn_unview(o, dil), _attn_unview(ml, dil)


def _seq_block_store(ref, blk, sl, val, dil):
    val = val.astype(ref.dtype)
    if dil == 4:
        rows = pl.ds(pl.multiple_of(blk * (BLOCK // 4), BLOCK // 4), BLOCK // 4)
        ref[:, rows, sl] = val.reshape(4, BLOCK // 4, sl.stop - sl.start)
    else:
        ref[pl.ds(pl.multiple_of(blk * BLOCK, BLOCK), BLOCK), sl] = val


def _attn_bwd(q, kv, do, ld, *, dil, max_dist, name):
    q, kv, do, ld = (_attn_view(a, dil) for a in (q, kv, do, ld))
    kw = kv.shape[-1] // 2
    gqa = kw == A_KV_WIDTH
    n_seq = dil
    nb = (q.shape[-2] * (4 if dil == 4 else 1)) // BLOCK
    n_kc = kw // LANES
    all_lanes = slice(0, LANES)
    assert max_dist in (BLOCK - 1, BLOCK)
    diag = max_dist == BLOCK

    def body(q_ref, kvp_ref, kvc_ref, do_ref, ld_ref, dq_ref, dkv_ref, ck_ref, cv_ref):
        i = pl.program_id(1)

        def block(has_prev):
            upper, eye = _upper_mask(dil), _eye_mask()
            ld_blk = _blk_load(ld_ref, all_lanes, dil)
            dk_acc = [None] * n_kc
            dv_acc = [None] * n_kc
            chunks = range(WIDTH // LANES)
            operands, products = [], []
            for c in chunks:
                sl = slice(c * LANES, (c + 1) * LANES)
                kc = 0 if gqa else c
                ksl = slice(kc * LANES, (kc + 1) * LANES)
                vsl = slice(ksl.start + kw, ksl.stop + kw)
                k2, v2 = _blk_load(kvc_ref, ksl, dil), _blk_load(kvc_ref, vsl, dil)
                if has_prev:
                    k2 = jnp.concatenate([_blk_load(kvp_ref, ksl, dil), k2], axis=0)
                    v2 = jnp.concatenate([_blk_load(kvp_ref, vsl, dil), v2], axis=0)
                qs = _stack_heads(_blk_load(q_ref, sl, dil), c, gqa)
                dos = _stack_heads(_blk_load(do_ref, sl, dil), c, gqa)
                operands.append((qs, dos, k2))
                products.append((_dot_nt(qs, k2), _dot_nt(dos, v2)))
            weights = []
            for c in chunks:
                heads = _stacked_head_ids(c, gqa)
                lse2 = _per_head_rows(ld_blk, heads)
                dl2 = _per_head_rows(ld_blk, tuple(h + STAT_SHIFT for h in heads))
                s, dp = products[c]
                if has_prev:
                    s_p, dp_p = s[:, :BLOCK], dp[:, :BLOCK]
                    sc = jnp.where(upper, s_p, s[:, BLOCK:])
                    dpc = jnp.where(upper, dp_p, dp[:, BLOCK:])
                else:
                    sc = jnp.where(upper, NEG, s)
                    dpc = dp
                p = jnp.exp(sc - lse2)
                ds = p * (dpc - dl2)
                zero = jnp.zeros_like(p)
                pf = jnp.where(upper, zero, p)
                dsf = jnp.where(upper, zero, ds)
                if has_prev:
                    if diag:
                        pd = jnp.exp(jnp.where(eye, s_p, NEG) - lse2)
                        dsd = pd * (dp_p - dl2)
                    else:
                        pd = dsd = zero
                    pf = jnp.concatenate([jnp.where(upper, p, pd), pf], axis=1)
                    dsf = jnp.concatenate([jnp.where(upper, ds, dsd), dsf], axis=1)
                weights.append((pf.astype(BF16), dsf.astype(BF16)))
            for c in chunks:
                sl = slice(c * LANES, (c + 1) * LANES)
                kc = 0 if gqa else c
                qs, dos, k2 = operands[c]
                pf, dsf = weights[c]
                _blk_store(dq_ref, sl, _unstack_heads(_dot(dsf, k2), c, gqa) * SCALE, dil)
                dk2 = _dot_tn(dsf, qs)
                dv2 = _dot_tn(pf, dos)
                dk_acc[kc] = dk2 if dk_acc[kc] is None else dk_acc[kc] + dk2
                dv_acc[kc] = dv2 if dv_acc[kc] is None else dv_acc[kc] + dv2
            for kc in range(n_kc):
                sl = slice(kc * LANES, (kc + 1) * LANES)
                vsl = slice(sl.start + kw, sl.stop + kw)
                if has_prev:
                    _seq_block_store(dkv_ref, i - 1, sl, ck_ref[:, sl] + dk_acc[kc][:BLOCK], dil)
                    _seq_block_store(dkv_ref, i - 1, vsl, cv_ref[:, sl] + dv_acc[kc][:BLOCK], dil)
                    ck_ref[:, sl] = dk_acc[kc][BLOCK:]
                    cv_ref[:, sl] = dv_acc[kc][BLOCK:]
                else:
                    ck_ref[:, sl] = dk_acc[kc]
                    cv_ref[:, sl] = dv_acc[kc]

        @pl.when(i == 0)
        def _():
            block(False)

        @pl.when(i > 0)
        def _():
            block(True)

        @pl.when(i == nb - 1)
        def _():
            for kc in range(n_kc):
                sl = slice(kc * LANES, (kc + 1) * LANES)
                _seq_block_store(dkv_ref, i, sl, ck_ref[:, sl], dil)
                _seq_block_store(dkv_ref, i, slice(sl.start + kw, sl.stop + kw), cv_ref[:, sl], dil)

    spec = _attn_specs(dil, nb)
    cur = spec(lambda i: i)
    prev = spec(lambda i: jnp.maximum(i - 1, 0))
    if dil == 4:
        whole = pl.BlockSpec((4, None, kv.shape[2], 2 * kw), lambda r, i: (0, r, 0, 0))
    else:
        whole = pl.BlockSpec((None, kv.shape[1], 2 * kw), lambda r, i: (r, 0, 0))
    sds = jax.ShapeDtypeStruct
    dq, dkv = pl.pallas_call(
        body, name=name, grid=(n_seq, nb),
        in_specs=[cur(WIDTH), prev(2 * kw), cur(2 * kw), cur(WIDTH), cur(LANES)],
        out_specs=(cur(WIDTH), whole),
        out_shape=(sds(q.shape, BF16), sds(kv.shape, BF16)),
        scratch_shapes=[pltpu.VMEM((BLOCK, kw), F32), pltpu.VMEM((BLOCK, kw), F32)],
        compiler_params=pltpu.CompilerParams(dimension_semantics=("arbitrary", "arbitrary"),
                                             vmem_limit_bytes=VMEM_LIMIT),
    )(q, kv, kv, do, ld)
    return _attn_unview(dq, dil), _attn_unview(dkv, dil)


def _outproj(att_a, att_b1, att_b4, att_b16, g_a, g_b, x2, tgt2, w_out_bf, sink_row, perm):
    s = x2.shape[0]
    tm = ROW_TILE

    def body(oa_ref, mla_ref, ob1_ref, ml1_ref, ob4_ref, ml4_ref, ob16_ref, ml16_ref,
             ga_ref, gb_ref, x_ref, t_ref, w_ref, sink_ref, perm_ref,
             dy_ref, doa_ref, dob_ref, dobf_ref, dga_ref, dgb_ref, lda_ref, ldb_ref, ldbf_ref,
             gw_ref, loss_ref, dsink_ref, scr_st):
        i = pl.program_id(0)
        perm = perm_ref[...]
        lane = lax.broadcasted_iota(jnp.int32, (tm, LANES), 1)
        used = lane < HEADS

        def split(ml):
            return jnp.where(used, ml, 0.0), jnp.where(used, pltpu.roll(ml, LANES - STAT_SHIFT, 1), 1.0)

        @pl.when(i == 0)
        def _():
            gw_ref[...] = jnp.zeros_like(gw_ref)
            loss_ref[...] = jnp.zeros_like(loss_ref)
            dsink_ref[...] = jnp.zeros_like(dsink_ref)

        ms, ls = zip(split(ml1_ref[...]), split(_load_folded(ml4_ref, scr_st)), split(_load_folded(ml16_ref, scr_st)))
        mx = jnp.maximum(jnp.maximum(ms[0], ms[1]), ms[2])
        scale = [jnp.exp(mp - mx) for mp in ms]
        den = (ls[0] * scale[0] + ls[1] * scale[1]) + ls[2] * scale[2]
        lse_b = jnp.where(used, mx + jnp.log(den), 0.0)
        inv_den = 1.0 / den
        o_b = _expand_heads(scale[0] * inv_den) * ob1_ref[...].astype(F32)
        o_b = o_b + _expand_heads(scale[1] * inv_den) * _load_folded_bf16(ob4_ref, perm)
        o_b = o_b + _expand_heads(scale[2] * inv_den) * _load_folded_bf16(ob16_ref, perm)
        m_a, l_a = split(mla_ref[...])
        lse_a = jnp.where(used, m_a + jnp.log(l_a), 0.0)
        o_a = _expand_heads(1.0 / l_a) * oa_ref[...].astype(F32)
        g_a = ga_ref[...].astype(F32)
        g_b = gb_ref[...].astype(F32)
        sg_a = _sigmoid(g_a)
        sg_b = _sigmoid(g_b)
        silu_a = g_a * sg_a
        silu_b = g_b * sg_b
        mixed = jnp.concatenate([o_a * silu_a, o_b * silu_b], axis=1).astype(BF16)
        w = w_ref[...]
        y = x_ref[...] + _dot(mixed, w)
        diff = y - t_ref[...]
        loss_ref[...] += (0.5 / D_MODEL) * jnp.sum(diff * diff)
        dy = diff * (1.0 / D_MODEL)
        dy_ref[...] = dy
        dyb = dy.astype(BF16)
        gw_ref[...] += _dot_tn(mixed, dyb)
        dmixed = _dot_nt(dyb, w)
        dm_a = dmixed[:, :WIDTH]
        dm_b = dmixed[:, WIDTH:]
        do_a = dm_a * silu_a
        do_b = dm_b * silu_b
        doa_ref[...] = do_a.astype(BF16)
        dob_ref[...] = do_b.astype(BF16)
        _store_folded_bf16(dobf_ref, do_b, perm)
        dga_ref[...] = (dm_a * o_a * (sg_a * (1.0 + g_a * (1.0 - sg_a)))).astype(BF16)
        dgb_ref[...] = (dm_b * o_b * (sg_b * (1.0 + g_b * (1.0 - sg_b)))).astype(BF16)
        dl_a = _reduce_heads(do_a * o_a)
        dl_b = _reduce_heads(do_b * o_b)
        lda_ref[...] = lse_a + pltpu.roll(dl_a, STAT_SHIFT, 1)
        ld_b = lse_b + pltpu.roll(dl_b, STAT_SHIFT, 1)
        ldb_ref[...] = ld_b
        _store_folded(ldbf_ref, ld_b, scr_st)
        dsink_ref[...] -= jnp.sum(jnp.exp(sink_ref[...] - lse_a) * dl_a, axis=0, keepdims=True)

    sds = jax.ShapeDtypeStruct
    ln = s // FOLD
    natural = [_rows(WIDTH), _rows(LANES)]
    folded = [_folded_rows(WIDTH), _folded_rows(LANES)]
    return pl.pallas_call(
        body, name="outproj_fwd_bwd", grid=(s // tm,),
        in_specs=natural + natural + folded + folded
                 + [_rows(WIDTH), _rows(WIDTH), _rows(D_MODEL), _rows(D_MODEL), _whole((D_MODEL, D_MODEL)),
                    _whole((1, LANES)), _whole(perm.shape)],
        out_specs=(_rows(D_MODEL), _rows(WIDTH), _rows(WIDTH), _folded_rows(WIDTH), _rows(WIDTH), _rows(WIDTH),
                   _rows(LANES), _rows(LANES), _folded_rows(LANES),
                   _whole((D_MODEL, D_MODEL)), _whole((1, LANES)), _whole((1, LANES))),
        out_shape=(sds((s, D_MODEL), F32), sds((s, WIDTH), BF16), sds((s, WIDTH), BF16),
                   sds((FOLD, ln, WIDTH), BF16), sds((s, WIDTH), BF16), sds((s, WIDTH), BF16),
                   sds((s, LANES), F32), sds((s, LANES), F32), sds((FOLD, ln, LANES), F32),
                   sds((D_MODEL, D_MODEL), F32), sds((1, LANES), F32), sds((1, LANES), F32)),
        scratch_shapes=[_fold_scratch(LANES)],
        compiler_params=pltpu.CompilerParams(dimension_semantics=("arbitrary",), vmem_limit_bytes=VMEM_LIMIT),
    )(*att_a, *att_b1, *att_b4, *att_b16, g_a, g_b, x2, tgt2, w_out_bf, sink_row, perm)


def _inproj_bwd(x2, dy, gain, w_bf, cos, sin_s, gqa, gka, gqb, gkb, bd256, bd128, perm,
                qa_raw, ka_raw, qb_raw, kb_raw, d_a, d_b1, d_b4, d_b16, dg_a, dg_b):
    s = x2.shape[0]
    tm = ROW_TILE

    def body(x_ref, dy_ref, gain_ref, w_hbm, cos_ref, sin_ref, gqa_ref, gka_ref, gqb_ref, gkb_ref, bd256_ref,
             bd128_ref, perm_ref, qa_raw_ref, ka_raw_ref, qb_raw_ref, kb_raw_ref, dqa_ref, dkva_ref,
             dq1_ref, dkv1_ref, dq4_ref, dkv4_ref, dq16_ref, dkv16_ref, dga_ref, dgb_ref,
             gx_ref, ht_ref, win_ref,
             dgain_ref, dgqa_ref, dgka_ref, dgqb_ref, dgkb_ref, w_vmem, dproj_ref):
        i = pl.program_id(0)
        perm = perm_ref[...]

        @pl.when(i == 0)
        def _():
            pltpu.sync_copy(w_hbm, w_vmem)
            dgain_ref[...] = jnp.zeros_like(dgain_ref)
            dgqa_ref[...] = jnp.zeros_like(dgqa_ref)
            dgka_ref[...] = jnp.zeros_like(dgka_ref)
            dgqb_ref[...] = jnp.zeros_like(dgqb_ref)
            dgkb_ref[...] = jnp.zeros_like(dgkb_ref)

        cos1 = cos_ref[...]
        sin1 = sin_ref[...]
        cos4 = jnp.tile(cos1, (1, 4))
        sin4 = jnp.tile(sin1, (1, 4))

        dt, dg = _qknorm_rope_bwd(dqa_ref[...], qa_raw_ref[...], gqa_ref[...], cos4, sin4, bd256_ref[...])
        dproj_ref[:, C_QA:C_KA] = dt.astype(BF16)
        dgqa_ref[...] += jnp.sum(dg, axis=0, keepdims=True)
        dt, dg = _qknorm_rope_bwd(dkva_ref[:, :A_KV_WIDTH], ka_raw_ref[...], gka_ref[...], cos1, sin1,
                                  bd128_ref[...])
        dproj_ref[:, C_KA:C_VA] = dt.astype(BF16)
        dgka_ref[...] += jnp.sum(dg, axis=0, keepdims=True)
        dproj_ref[:, C_VA:C_GA] = dkva_ref[:, A_KV_WIDTH:]
        dproj_ref[:, C_GA:C_QB] = dga_ref[...]
        dq = (dq1_ref[...].astype(F32) + _load_folded_bf16(dq4_ref, perm)) + _load_folded_bf16(dq16_ref, perm)
        dt, dg = _qknorm_rope_bwd(dq, qb_raw_ref[...], gqb_ref[...], cos4, sin4, bd256_ref[...])
        dproj_ref[:, C_QB:C_KB] = dt.astype(BF16)
        dgqb_ref[...] += jnp.sum(dg, axis=0, keepdims=True)
        dkv = (dkv1_ref[...].astype(F32) + _load_folded_bf16(dkv4_ref, perm)) + _load_folded_bf16(dkv16_ref, perm)
        dt, dg = _qknorm_rope_bwd(dkv[:, :WIDTH], kb_raw_ref[...], gkb_ref[...], cos4, sin4, bd256_ref[...])
        dproj_ref[:, C_KB:C_VB] = dt.astype(BF16)
        dgkb_ref[...] += jnp.sum(dg, axis=0, keepdims=True)
        dproj_ref[:, C_VB:C_GB] = dkv[:, WIDTH:].astype(BF16)
        dproj_ref[:, C_GB:C_END] = dgb_ref[...]
        for k, start in enumerate(WIN_START):
            win_ref[k] = dproj_ref[:, start:start + WIN]

        xt = x_ref[...]
        gain_row = gain_ref[...]
        r = lax.rsqrt(jnp.mean(xt * xt, axis=-1, keepdims=True) + EPS)
        xr = xt * r
        ht_ref[...] = (xr * gain_row).T.astype(BF16)
        dh = _dot(dproj_ref[...], w_vmem[...])
        dgain_ref[...] += jnp.sum(dh * xr, axis=0, keepdims=True)
        u = dh * gain_row
        gx_ref[...] = dy_ref[...] + r * (u - xr * jnp.mean(u * xr, axis=-1, keepdims=True))

    def acc_row(w):
        return pl.BlockSpec((1, w), lambda i: (0, 0))

    sds = jax.ShapeDtypeStruct
    any_spec = pl.BlockSpec(memory_space=pl.ANY)
    win_spec = pl.BlockSpec((N_CHIP, tm, WIN), lambda i: (0, i, 0))
    return pl.pallas_call(
        body, name="inproj_bwd", grid=(s // tm,),
        in_specs=[_rows(D_MODEL), _rows(D_MODEL), _whole(gain.shape), any_spec, _rows(LANES), _rows(LANES),
                  _whole(gqa.shape), _whole(gka.shape), _whole(gqb.shape), _whole(gkb.shape), _whole(bd256.shape),
                  _whole(bd128.shape), _whole(perm.shape),
                  _rows(WIDTH), _rows(A_KV_WIDTH), _rows(WIDTH), _rows(WIDTH),
                  _rows(WIDTH), _rows(2 * A_KV_WIDTH), _rows(WIDTH), _rows(2 * WIDTH)]
                 + [_folded_rows(WIDTH), _folded_rows(2 * WIDTH)] * 2 + [_rows(WIDTH), _rows(WIDTH)],
        out_specs=(_rows(D_MODEL), pl.BlockSpec((D_MODEL, tm), lambda i: (0, i)), win_spec, acc_row(D_MODEL), acc_row(WIDTH), acc_row(A_KV_WIDTH), acc_row(WIDTH), acc_row(WIDTH)),
        out_shape=(sds((s, D_MODEL), F32), sds((D_MODEL, s), BF16), sds((N_CHIP, s, WIN), BF16),
                   sds((1, D_MODEL), F32),
                   sds((1, WIDTH), F32), sds((1, A_KV_WIDTH), F32), sds((1, WIDTH), F32), sds((1, WIDTH), F32)),
        scratch_shapes=[pltpu.VMEM((IN_WIDTH, D_MODEL), BF16), pltpu.VMEM((tm, IN_WIDTH), BF16)],
        compiler_params=pltpu.CompilerParams(dimension_semantics=("arbitrary",), vmem_limit_bytes=VMEM_LIMIT),
    )(x2, dy, gain, w_bf, cos, sin_s, gqa, gka, gqb, gkb, bd256, bd128, perm, qa_raw, ka_raw, qb_raw, kb_raw,
      *d_a, *d_b1, *d_b4, *d_b16, dg_a, dg_b)


def _rope_tables(s):
    half = HEAD_DIM // 2
    inv = jnp.tile(ROPE_THETA ** (-jnp.arange(half, dtype=F32) / half), 4)
    sign = jnp.tile(jnp.concatenate([-jnp.ones((half,), F32), jnp.ones((half,), F32)]), 2)
    ang = jnp.arange(s).astype(F32)[:, None] * inv[None, :]
    return jnp.cos(ang), jnp.sin(ang) * sign[None, :]


def _block_diag_ones(w):
    idx = jnp.arange(w) // HEAD_DIM
    return (idx[:, None] == idx[None, :]).astype(BF16)


def _local_step(x2, tgt2, norm_gain, w_in_bf, q_norm_a, k_norm_a, sinks_a, q_norm_b, k_norm_b, w_out_bf):
    s = x2.shape[0]
    cos, sin_s = _rope_tables(s)
    bd256, bd128 = _block_diag_ones(2 * LANES), _block_diag_ones(A_KV_WIDTH)
    gqa = jnp.tile(q_norm_a, (1, HEADS))
    gka = jnp.tile(k_norm_a, (1, 2))
    gqb = jnp.tile(q_norm_b, (1, HEADS))
    gkb = jnp.tile(k_norm_b, (1, HEADS))
    sink_row = jnp.pad(sinks_a, ((0, 0), (0, LANES - HEADS)))
    perm = _fold_matrix()

    (qa, kva, qb, kvb, qbf, kvbf, qa_raw, ka_raw, g_a, qb_raw, kb_raw, g_b) = _inproj(
        x2, norm_gain, w_in_bf, cos, sin_s, gqa, gka, gqb, gkb, bd256, bd128)

    att_a = _attn_fwd(qa, kva, sinks_a, dil=1, max_dist=A_MAX_DIST, name="attn_a_fwd")
    att_b1 = _attn_fwd(qb, kvb, None, dil=1, max_dist=B_MAX_DIST, name="attn_b1_fwd")
    att_b4 = _attn_fwd(qbf, kvbf, None, dil=4, max_dist=B_MAX_DIST, name="attn_b4_fwd")
    att_b16 = _attn_fwd(qbf, kvbf, None, dil=16, max_dist=B_MAX_DIST, name="attn_b16_fwd")

    (dy, do_a, do_b, do_bf, dg_a, dg_b, ld_a, ld_b, ld_bf, gw_out, loss_part, dsink) = _outproj(
        att_a, att_b1, att_b4, att_b16, g_a, g_b, x2, tgt2, w_out_bf, sink_row, perm)

    d_a = _attn_bwd(qa, kva, do_a, ld_a, dil=1, max_dist=A_MAX_DIST, name="attn_a_bwd")
    d_b1 = _attn_bwd(qb, kvb, do_b, ld_b, dil=1, max_dist=B_MAX_DIST, name="attn_b1_bwd")
    d_b4 = _attn_bwd(qbf, kvbf, do_bf, ld_bf, dil=4, max_dist=B_MAX_DIST, name="attn_b4_bwd")
    d_b16 = _attn_bwd(qbf, kvbf, do_bf, ld_bf, dil=16, max_dist=B_MAX_DIST, name="attn_b16_bwd")

    gx, h_t, wins, dgain, dgqa, dgka, dgqb, dgkb = _inproj_bwd(
        x2, dy, norm_gain, w_in_bf, cos, sin_s, gqa, gka, gqb, gkb, bd256, bd128, perm,
        qa_raw, ka_raw, qb_raw, kb_raw, d_a, d_b1, d_b4, d_b16, dg_a, dg_b)
    return loss_part, gx, h_t, wins, gw_out, (dgain, dgqa, dgka, dsink, dgqb, dgkb)


def _position():
    return lax.axis_index("x"), lax.axis_index("y"), lax.axis_index("c")


GATHER_CHUNKS = 2


def _gather_weights(blocks, name):
    n = len(blocks)
    ch = GATHER_CHUNKS
    n_sems = n * (N_CHIP - 1) * ch

    def body(*refs):
        src_refs, dst_refs = refs[:n], refs[n:2 * n]
        ici_send, ici_recv, d2d_send, d2d_recv, local_sems = refs[2 * n:]
        x, y, c = _position()
        b = 2 * x + y
        copies = []
        for k in range(n):
            local = pltpu.make_async_copy(src_refs[k], dst_refs[k].at[b], local_sems.at[k])
            local.start()
            copies.append(local)

        def rows(k, core, j):
            half = blocks[k].shape[0] // 2
            return pl.ds(core * half + j * (half // ch), half // ch)

        plan = []
        for d in range(1, N_CHIP):
            px, py = x ^ (d >> 1), y ^ (d & 1)
            for j in range(ch):
                for k in range(n):
                    plan.append((px, py, 2 * px + py, k, j, ((d - 1) * ch + j) * n + k))
        sends = []
        for px, py, pb, k, j, sem in plan:
            send = pltpu.make_async_remote_copy(
                src_ref=src_refs[k].at[rows(k, c, j)], dst_ref=dst_refs[k].at[b, rows(k, c, j)],
                send_sem=ici_send.at[sem], recv_sem=ici_recv.at[sem], device_id=(px, py, c), device_id_type=MESH)
            send.start()
            sends.append(send)
        for px, py, pb, k, j, sem in plan:
            landed = dst_refs[k].at[pb, rows(k, c, j)]
            pltpu.make_async_remote_copy(
                src_ref=landed, dst_ref=landed, send_sem=ici_send.at[sem], recv_sem=ici_recv.at[sem],
                device_id=(px, py, c), device_id_type=MESH).wait_recv()
            forward = pltpu.make_async_remote_copy(
                src_ref=landed, dst_ref=landed, send_sem=d2d_send.at[sem], recv_sem=d2d_recv.at[sem],
                device_id=(x, y, 1 - c), device_id_type=MESH)
            forward.start()
            sends.append(forward)
        for px, py, pb, k, j, sem in plan:
            passed = dst_refs[k].at[pb, rows(k, 1 - c, j)]
            pltpu.make_async_remote_copy(
                src_ref=passed, dst_ref=passed, send_sem=d2d_send.at[sem], recv_sem=d2d_recv.at[sem],
                device_id=(x, y, 1 - c), device_id_type=MESH).wait_recv()
        for send in sends:
            send.wait_send()
        for local in copies:
            local.wait()

    vmem_spec = pl.BlockSpec(memory_space=pltpu.VMEM)
    out_shape = tuple(jax.ShapeDtypeStruct((N_CHIP,) + a.shape, a.dtype) for a in blocks)
    return pl.pallas_call(
        body, name=name, in_specs=[vmem_spec] * n, out_specs=tuple([vmem_spec] * n), out_shape=out_shape,
        scratch_shapes=[pltpu.SemaphoreType.DMA((n_sems,)) for _ in range(4)] + [pltpu.SemaphoreType.DMA((n,))],
        compiler_params=pltpu.CompilerParams(vmem_limit_bytes=VMEM_LIMIT),
    )(*blocks)


def _grad_reduce(order, h_t, wins, gw_out, small):
    s = h_t.shape[1]
    tk = GRAD_ROWS
    n_i = s // tk
    half = D_MODEL // 2
    o_half = OUT_ROWS // 2
    n_rel = N_CHIP - 1

    def body(order_ref, ht_ref, win_ref, gwo_ref, small_ref,
             win_out, wout_out, small_out,
             acc, mine, s1, r1, s2, r2, so1, ro1, so2, ro2, pair_in, pair_o, small_land,
             s1_send, s1_recv, s2_send, s2_recv, o1_send, o1_recv, o2_send, o2_recv,
             pair_send, pair_recv, small_send, small_recv):
        j = pl.program_id(0)
        i = pl.program_id(1)
        x, y, c = _position()
        me = 4 * x + 2 * y + c
        sibling = (x, y, 1 - c)
        my_rows = pl.ds(pl.multiple_of(c * half, half), half)
        sib_rows = pl.ds(pl.multiple_of((1 - c) * half, half), half)

        def chip_of(rel):
            return x ^ (rel >> 1), y ^ (rel & 1)

        def level1(k):
            return pltpu.make_async_remote_copy(src_ref=s1.at[k], dst_ref=r1.at[k], send_sem=s1_send.at[k],
                                                recv_sem=s1_recv.at[k], device_id=sibling, device_id_type=MESH)

        def level2(k):
            px, py = chip_of(RELATIONS[k])
            return pltpu.make_async_remote_copy(src_ref=s2.at[k], dst_ref=r2.at[k], send_sem=s2_send.at[k],
                                                recv_sem=s2_recv.at[k], device_id=(px, py, c), device_id_type=MESH)

        def out_level1(bk):
            return pltpu.make_async_remote_copy(src_ref=so1.at[bk], dst_ref=ro1.at[bk], send_sem=o1_send.at[bk],
                                                recv_sem=o1_recv.at[bk], device_id=sibling, device_id_type=MESH)

        def out_level2(k):
            px, py = chip_of(RELATIONS[k])
            return pltpu.make_async_remote_copy(src_ref=so2.at[k], dst_ref=ro2.at[k], send_sem=o2_send.at[k],
                                                recv_sem=o2_recv.at[k], device_id=(px, py, c), device_id_type=MESH)

        def small_copy(d):
            px, py, pc = x ^ (d >> 2), y ^ ((d >> 1) & 1), c ^ (d & 1)
            return pltpu.make_async_remote_copy(src_ref=small_ref, dst_ref=small_land.at[me],
                                                send_sem=small_send.at[d], recv_sem=small_recv.at[d],
                                                device_id=(px, py, pc), device_id_type=MESH)

        def pair_copy(k, buf):
            return pltpu.make_async_remote_copy(src_ref=buf.at[0], dst_ref=buf.at[1], send_sem=pair_send.at[k],
                                                recv_sem=pair_recv.at[k], device_id=sibling, device_id_type=MESH)

        def out_rows(bk, core):
            return pl.ds(pl.multiple_of(bk * OUT_ROWS + core * o_half, o_half), o_half)

        @pl.when((j == 0) & (i == 0))
        def _():
            for d in range(1, N_DEV):
                small_copy(d).start()
            small_land[me] = small_ref[...]
            for bk in range(N_CHIP):
                so1[bk] = gwo_ref[out_rows(bk, 1 - c), :].astype(BF16)
                out_level1(bk).start()

        @pl.when((j == 0) & (i == 1))
        def _():
            b = 2 * x + y
            for bk in range(N_CHIP):
                out_level1(bk).wait_recv()
            for k in range(n_rel):
                px, py = chip_of(RELATIONS[k])
                bk = 2 * px + py
                so2[k] = (gwo_ref[out_rows(bk, c), :] + ro1[bk].astype(F32)).astype(BF16)
                out_level2(k).start()

        @pl.when(i == 0)
        def _():
            acc[...] = jnp.zeros_like(acc)

        for n0 in range(0, WIN, ACC_COLS):
            n1 = min(n0 + ACC_COLS, WIN)
            acc[:, n0:n1] += _dot(ht_ref[...], win_ref[:, n0:n1])

        for k in range(N_CHIP):
            @pl.when((j == k) & (i == n_i - 1))
            def _(k=k):
                s1[k] = acc[sib_rows, :].astype(BF16)
                level1(k).start()
                mine[...] = acc[my_rows, :]

            if k < n_rel:
                @pl.when((j == k + 1) & (i == 1))
                def _(k=k):
                    level1(k).wait_recv()
                    s2[k] = (mine[...] + r1[k].astype(F32)).astype(BF16)
                    level2(k).start()

        @pl.when((j == N_CHIP - 1) & (i == n_i - 1))
        def _():
            b = 2 * x + y
            level1(N_CHIP - 1).wait_recv()
            total = mine[...] + r1[N_CHIP - 1].astype(F32)
            for k in range(n_rel):
                level2(k).wait_recv()
                total = total + r2[k].astype(F32)
            total = total.T
            pair_in[0] = total
            pair_copy(0, pair_in).start()
            total_o = gwo_ref[out_rows(b, c), :] + ro1[b].astype(F32)
            for k in range(n_rel):
                out_level2(k).wait_recv()
                total_o = total_o + ro2[k].astype(F32)
            pair_o[0] = total_o
            pair_copy(1, pair_o).start()
            win_out[c] = total
            wout_out[c] = total_o
            for d in range(1, N_DEV):
                small_copy(d).wait_recv()
            small_out[...] = small_land[...]
            pair_copy(0, pair_in).wait_recv()
            win_out[1 - c] = pair_in[1]
            pair_copy(1, pair_o).wait_recv()
            wout_out[1 - c] = pair_o[1]
            for d in range(1, N_DEV):
                small_copy(d).wait_send()
            for k in range(N_CHIP):
                level1(k).wait_send()
                out_level1(k).wait_send()
            for k in range(n_rel):
                level2(k).wait_send()
                out_level2(k).wait_send()
            pair_copy(0, pair_in).wait_send()
            pair_copy(1, pair_o).wait_send()

    vmem = pl.BlockSpec(memory_space=pltpu.VMEM)
    dma = pltpu.SemaphoreType.DMA
    sds = jax.ShapeDtypeStruct
    grid_spec = pltpu.PrefetchScalarGridSpec(
        num_scalar_prefetch=1, grid=(N_CHIP, n_i),
        in_specs=[pl.BlockSpec((D_MODEL, tk), lambda j, i, order: (0, i)),
                  pl.BlockSpec((None, tk, WIN), lambda j, i, order: (order[j], i, 0)), vmem, vmem],
        out_specs=(vmem, vmem, vmem),
        scratch_shapes=[
            pltpu.VMEM((D_MODEL, WIN), F32), pltpu.VMEM((half, WIN), F32),
            pltpu.VMEM((N_CHIP, half, WIN), BF16), pltpu.VMEM((N_CHIP, half, WIN), BF16),
            pltpu.VMEM((n_rel, half, WIN), BF16), pltpu.VMEM((n_rel, half, WIN), BF16),
            pltpu.VMEM((N_CHIP, o_half, D_MODEL), BF16), pltpu.VMEM((N_CHIP, o_half, D_MODEL), BF16),
            pltpu.VMEM((n_rel, o_half, D_MODEL), BF16), pltpu.VMEM((n_rel, o_half, D_MODEL), BF16),
            pltpu.VMEM((2, WIN, half), F32), pltpu.VMEM((2, o_half, D_MODEL), F32),
            pltpu.VMEM((N_DEV, PACK_ROWS, D_MODEL), F32),
            dma((N_CHIP,)), dma((N_CHIP,)), dma((n_rel,)), dma((n_rel,)),
            dma((N_CHIP,)), dma((N_CHIP,)), dma((n_rel,)), dma((n_rel,)),
            dma((2,)), dma((2,)), dma((N_DEV,)), dma((N_DEV,))])
    return pl.pallas_call(
        body, name="grad_w_in_reduce", grid_spec=grid_spec,
        out_shape=(sds((2, WIN, half), F32), sds((2, o_half, D_MODEL), F32), sds((N_DEV, PACK_ROWS, D_MODEL), F32)),
        compiler_params=pltpu.CompilerParams(dimension_semantics=("arbitrary", "arbitrary"),
                                             vmem_limit_bytes=VMEM_LIMIT),
    )(order, h_t, wins, gw_out, small)


ADAM_STEPS = 4


def _adamw_math(w, g, m, v):
    m = ADAM_B1 * m + (1.0 - ADAM_B1) * g
    v = ADAM_B2 * v + (1.0 - ADAM_B2) * (g * g)
    m_hat = m / (1.0 - ADAM_B1 ** ADAM_STEP)
    v_hat = v / (1.0 - ADAM_B2 ** ADAM_STEP)
    delta = -ADAM_LR * (m_hat / (jnp.sqrt(v_hat) + ADAM_EPS) + ADAM_WD * w)
    return delta, m, v


def _adamw(w, g, m, v, name):
    r, c = w.shape

    def body(w_ref, g_ref, m_ref, v_ref, d_ref, nm_ref, nv_ref):
        delta, nm, nv = _adamw_math(w_ref[...], g_ref[...], m_ref[...], v_ref[...])
        d_ref[...] = delta
        nm_ref[...] = nm
        nv_ref[...] = nv

    rows = r // ADAM_STEPS
    assert rows * ADAM_STEPS == r and rows % 8 == 0
    spec = pl.BlockSpec((rows, c), lambda i: (i, 0))
    shape = jax.ShapeDtypeStruct((r, c), F32)
    return pl.pallas_call(
        body, name=name, grid=(ADAM_STEPS,), in_specs=[spec] * 4, out_specs=(spec,) * 3,
        out_shape=(shape,) * 3, compiler_params=pltpu.CompilerParams(vmem_limit_bytes=VMEM_LIMIT),
    )(w, g, m, v)


PACK_ROWS = 8


def _fold_heads(v):
    y = v[:, 0:LANES]
    for j in range(1, v.shape[1] // LANES):
        y = y + v[:, j * LANES:(j + 1) * LANES]
    return y + pltpu.roll(y, HEAD_DIM, 1)


def _small_adamw(recv, w_p, m_p, v_p):
    def body(r_ref, w_ref, m_ref, v_ref, g_ref, d_ref, nm_ref, nv_ref, loss_ref):
        tot = r_ref[0]
        for j in range(1, N_DEV):
            tot = tot + r_ref[j]
        loss_ref[...] = tot[3:4, 0:LANES]
        row1 = tot[1:2, :]
        row2 = tot[2:3, :]
        pieces = [_fold_heads(row1[:, 0:WIDTH]), _fold_heads(row2[:, WIDTH:WIDTH + A_KV_WIDTH]),
                  _fold_heads(row1[:, WIDTH:2 * WIDTH]), _fold_heads(row2[:, 0:WIDTH]),
                  row2[:, WIDTH + A_KV_WIDTH:WIDTH + 2 * A_KV_WIDTH], jnp.zeros((1, 3 * LANES), F32)]
        g = jnp.concatenate([tot[0:1, :], jnp.concatenate(pieces, axis=1), jnp.zeros((PACK_ROWS - 2, D_MODEL), F32)],
                            axis=0)
        g_ref[...] = g
        delta, nm, nv = _adamw_math(w_ref[...], g, m_ref[...], v_ref[...])
        d_ref[...] = delta
        nm_ref[...] = nm
        nv_ref[...] = nv

    shape = jax.ShapeDtypeStruct((PACK_ROWS, D_MODEL), F32)
    return pl.pallas_call(body, name="small_adamw",
                          out_shape=(shape,) * 4 + (jax.ShapeDtypeStruct((1, LANES), F32),))(recv, w_p, m_p, v_p)


def _pack_small(norm_gain, q_a, k_a, q_b, k_b, sinks):
    def lane_pad(a):
        return jnp.pad(a, ((0, 0), (0, LANES - a.shape[1])))
    row1 = jnp.concatenate([lane_pad(q_a), lane_pad(k_a), lane_pad(q_b), lane_pad(k_b), lane_pad(sinks),
                            jnp.zeros((1, 3 * LANES), F32)], axis=1)
    return jnp.concatenate([norm_gain, row1, jnp.zeros((PACK_ROWS - 2, D_MODEL), F32)], axis=0)


def _unpack_small(p):
    return (p[0:1, :], p[1:2, 0:HEAD_DIM], p[1:2, LANES:LANES + HEAD_DIM], p[1:2, 2 * LANES:2 * LANES + HEAD_DIM],
            p[1:2, 3 * LANES:3 * LANES + HEAD_DIM], p[1:2, 4 * LANES:4 * LANES + HEADS])


def kernel(x, norm_gain, w_in, q_norm_a, k_norm_a, sinks_a, q_norm_b, k_norm_b, w_out, loss_target, m_norm_gain, m_w_in, m_q_norm_a, m_k_norm_a, m_sinks_a, m_q_norm_b, m_k_norm_b, m_w_out, v_norm_gain, v_w_in, v_q_norm_a, v_k_norm_a, v_sinks_a, v_q_norm_b, v_k_norm_b, v_w_out):
    chip = 2 * lax.axis_index("x") + lax.axis_index("y")

    w_in_t, m_w_in_t, v_w_in_t = w_in[0].T, m_w_in[0].T, v_w_in[0].T

    w_in_all, w_out_all = _gather_weights([w_in_t.astype(BF16), w_out[0].astype(BF16)], "gather_weights")
    w_in_bf = w_in_all.reshape(IN_WIDTH, D_MODEL)
    w_out_bf = w_out_all.reshape(D_MODEL, D_MODEL)

    loss_part, gx, h_t, wins, gw_out, (dgain, dgqa, dgka, dsink, dgqb, dgkb) = _local_step(
        x[0], loss_target[0], norm_gain, w_in_bf, q_norm_a, k_norm_a, sinks_a, q_norm_b, k_norm_b, w_out_bf)

    small = jnp.concatenate([
        dgain, jnp.concatenate([dgqa, dgqb], axis=1),
        jnp.concatenate([dgkb, dgka, dsink, jnp.zeros((1, D_MODEL - WIDTH - 2 * A_KV_WIDTH), F32)], axis=1),
        jnp.pad(loss_part, ((0, 0), (0, D_MODEL - LANES))),
        jnp.zeros((PACK_ROWS - 4, D_MODEL), F32)], axis=0)
    order = (chip ^ jnp.array(RELATIONS, jnp.int32)).astype(jnp.int32)
    win_sum, wout_sum, small_recv = _grad_reduce(order, h_t, wins, gw_out, small)
    shift = jnp.array(WIN_SHIFT, jnp.int32)[chip]
    g_w_in_t = lax.dynamic_slice_in_dim(win_sum.transpose(1, 0, 2).reshape(WIN, D_MODEL), shift, IN_COLS, axis=0)
    g_w_out = wout_sum.reshape(OUT_ROWS, D_MODEL)

    d_w_in, nm_w_in, nv_w_in = (a.T for a in _adamw(w_in_t, g_w_in_t, m_w_in_t, v_w_in_t, "adamw_w_in"))
    g_w_in = g_w_in_t.T
    d_w_out, nm_w_out, nv_w_out = _adamw(w_out[0], g_w_out, m_w_out[0], v_w_out[0], "adamw_w_out")
    g_s, d_s, nm_s, nv_s, loss_row = _small_adamw(
        small_recv,
        _pack_small(norm_gain, q_norm_a, k_norm_a, q_norm_b, k_norm_b, sinks_a),
        _pack_small(m_norm_gain, m_q_norm_a, m_k_norm_a, m_q_norm_b, m_k_norm_b, m_sinks_a),
        _pack_small(v_norm_gain, v_q_norm_a, v_k_norm_a, v_q_norm_b, v_k_norm_b, v_sinks_a))
    loss = loss_row[0, 0]

    def leaves(small_packed, big_in, big_out):
        gain, qa, ka, qb, kb, sk = _unpack_small(small_packed)
        return (gain, big_in[None], qa, ka, sk, qb, kb, big_out[None])

    return ((loss, gx[None]) + leaves(g_s, g_w_in, g_w_out) + leaves(d_s, d_w_in, d_w_out)
            + leaves(nm_s, nm_w_in, nm_w_out) + leaves(nv_s, nv_w_in, nv_w_out))
```

```python
import jax
import jax.numpy as jnp
from jax import lax
from jax.experimental import pallas as pl
from jax.experimental.pallas import tpu as pltpu

F32 = jnp.float32
BF16 = jnp.bfloat16

D_MODEL = 1024
HEAD_DIM = 64
HEADS = 8
WIDTH = HEADS * HEAD_DIM
A_KV_WIDTH = 2 * HEAD_DIM
BLOCK = 128
LANES = 128
FOLD = 16
A_MAX_DIST = 127
B_MAX_DIST = 128
ROPE_THETA = 10000.0
EPS = 1e-6
NEG = -1e30
SCALE = HEAD_DIM ** -0.5

IN_WIDTH = 3328
C_QA, C_KA, C_VA, C_GA, C_QB, C_KB, C_VB, C_GB, C_END = 0, 512, 640, 768, 1280, 1792, 2304, 2816, 3328

N_DEV = 8
N_CHIP = 4
MESH = pl.DeviceIdType.MESH
IN_COLS = IN_WIDTH // N_CHIP
WIN = 896
WIN_START = (0, 768, 1664, 2432)
WIN_SHIFT = (0, 64, 0, 64)
OUT_ROWS = D_MODEL // N_CHIP
RELATIONS = (3, 1, 2, 0)

ADAM_LR = 0.001
ADAM_B1 = 0.9
ADAM_B2 = 0.999
ADAM_EPS = 1e-08
ADAM_WD = 0.01
ADAM_STEP = 10

ROW_TILE = 256
FOLD_ROWS = ROW_TILE // FOLD
GRAD_ROWS = 1024
ACC_COLS = 256
VMEM_LIMIT = 56 * 1024 * 1024


def _dot(a, b):
    return jnp.dot(a, b, preferred_element_type=F32)


def _dot_nt(a, b):
    return lax.dot_general(a, b, (((1,), (1,)), ((), ())), preferred_element_type=F32)


def _dot_tn(a, b):
    return lax.dot_general(a, b, (((0,), (0,)), ((), ())), preferred_element_type=F32)


def _head_sum(z, bd):
    w = bd.shape[0]
    zb = z.astype(BF16)
    parts = [_dot(zb[:, a:a + w], bd) for a in range(0, z.shape[1], w)]
    return parts[0] if len(parts) == 1 else jnp.concatenate(parts, axis=1)


def _swap_halves(t):
    w = t.shape[1]
    lane = lax.broadcasted_iota(jnp.int32, t.shape, 1)
    return jnp.where(lane % HEAD_DIM < HEAD_DIM // 2, pltpu.roll(t, w - 32, 1), pltpu.roll(t, 32, 1))


def _qknorm_rope(t, g, cos, sin_s, bd):
    r = lax.rsqrt(_head_sum(t * t, bd) * (1.0 / HEAD_DIM) + EPS)
    n = (t * r) * g
    return n * cos + _swap_halves(n) * sin_s


def _qknorm_rope_bwd(dout, t, g, cos, sin_s, bd):
    dout, t = dout.astype(F32), t.astype(F32)
    dn = dout * cos + _swap_halves(dout * sin_s)
    r = lax.rsqrt(_head_sum(t * t, bd) * (1.0 / HEAD_DIM) + EPS)
    tr = t * r
    u = dn * g
    dt = r * (u - tr * (_head_sum(u * tr, bd) * (1.0 / HEAD_DIM)))
    return dt, dn * tr


def _sigmoid(g):
    return 1.0 / (1.0 + jnp.exp(-g))


def _expand_heads(st):
    t = st.shape[0]
    lane = lax.broadcasted_iota(jnp.int32, (t, LANES), 1)
    chunks = []
    for c in range(WIDTH // LANES):
        chunks.append(jnp.where(lane < HEAD_DIM, st[:, 2 * c:2 * c + 1], st[:, 2 * c + 1:2 * c + 2]))
    return jnp.concatenate(chunks, axis=1)


def _reduce_heads(z):
    t = z.shape[0]
    lane = lax.broadcasted_iota(jnp.int32, (t, LANES), 1)
    out = jnp.zeros((t, LANES), F32)
    for c in range(WIDTH // LANES):
        zc = z[:, c * LANES:(c + 1) * LANES]
        for ph in range(2):
            s = jnp.sum(jnp.where((lane // HEAD_DIM) == ph, zc, 0.0), axis=-1, keepdims=True)
            out = jnp.where(lane == 2 * c + ph, s, out)
    return out


def _fold_scratch(w):
    return pltpu.VMEM((w // LANES, ROW_TILE, LANES), F32)


def _store_folded(out_ref, val, scr, col0=0):
    w = val.shape[1]
    n = w // LANES
    for c in range(n):
        scr[c] = val[:, c * LANES:(c + 1) * LANES]
    for r in range(FOLD):
        piece = [scr[c, pl.ds(r, FOLD_ROWS, stride=FOLD), :] for c in range(n)]
        out_ref[r, :, col0:col0 + w] = (piece[0] if n == 1 else jnp.concatenate(piece, axis=1)).astype(out_ref.dtype)


def _load_folded(in_ref, scr):
    n = in_ref.shape[2] // LANES
    for r in range(FOLD):
        blk = in_ref[r].astype(F32)
        for c in range(n):
            scr[c, pl.ds(r, FOLD_ROWS, stride=FOLD), :] = blk[:, c * LANES:(c + 1) * LANES]
    return scr[0] if n == 1 else jnp.concatenate([scr[c] for c in range(n)], axis=1)


def _fold_matrix():
    f = jnp.arange(ROW_TILE)
    return (jnp.arange(ROW_TILE)[None, :] == (FOLD * (f % FOLD_ROWS) + f // FOLD_ROWS)[:, None]).astype(BF16)


def _store_folded_bf16(out_ref, val, perm):
    folded = _dot(perm, val.astype(BF16)).astype(out_ref.dtype)
    for r in range(FOLD):
        out_ref[r] = folded[r * FOLD_ROWS:(r + 1) * FOLD_ROWS]


def _load_folded_bf16(in_ref, perm):
    blk = jnp.concatenate([in_ref[r] for r in range(FOLD)], axis=0)
    return _dot(perm, blk)


def _rows(w, tm=ROW_TILE):
    return pl.BlockSpec((tm, w), lambda i: (i, 0))


def _folded_rows(w):
    return pl.BlockSpec((FOLD, FOLD_ROWS, w), lambda i: (0, i, 0))


def _whole(shape):
    return pl.BlockSpec(shape, lambda i: (0,) * len(shape))


def _inproj(x2, gain, w_bf, cos, sin_s, gqa, gka, gqb, gkb, bd256, bd128):
    s = x2.shape[0]
    tm = ROW_TILE

    def body(x_ref, gain_ref, w_hbm, cos_ref, sin_ref, gqa_ref, gka_ref, gqb_ref, gkb_ref, bd256_ref, bd128_ref,
             qa_ref, kva_ref, qb_ref, kvb_ref, qbf_ref, kvbf_ref,
             qa_raw_ref, ka_raw_ref, ga_ref, qb_raw_ref, kb_raw_ref, gb_ref, w_vmem, scr):
        @pl.when(pl.program_id(0) == 0)
        def _():
            pltpu.sync_copy(w_hbm, w_vmem)

        xt = x_ref[...]
        r = lax.rsqrt(jnp.mean(xt * xt, axis=-1, keepdims=True) + EPS)
        h = ((xt * r) * gain_ref[...]).astype(BF16)
        cos1 = cos_ref[...]
        sin1 = sin_ref[...]
        cos4 = jnp.tile(cos1, (1, 4))
        sin4 = jnp.tile(sin1, (1, 4))

        def seg(a, b):
            return _dot_nt(h, w_vmem[a:b, :])

        t = seg(C_QA, C_KA)
        qa_raw_ref[...] = t.astype(BF16)
        qa_ref[...] = (_qknorm_rope(t, gqa_ref[...], cos4, sin4, bd256_ref[...]) * SCALE).astype(BF16)
        t = seg(C_KA, C_VA)
        ka_raw_ref[...] = t.astype(BF16)
        kva_ref[:, :A_KV_WIDTH] = _qknorm_rope(t, gka_ref[...], cos1, sin1, bd128_ref[...]).astype(BF16)
        kva_ref[:, A_KV_WIDTH:] = seg(C_VA, C_GA).astype(BF16)
        ga_ref[...] = seg(C_GA, C_QB).astype(BF16)
        t = seg(C_QB, C_KB)
        qb_raw_ref[...] = t.astype(BF16)
        t = _qknorm_rope(t, gqb_ref[...], cos4, sin4, bd256_ref[...]) * SCALE
        qb_ref[...] = t.astype(BF16)
        _store_folded(qbf_ref, t, scr)
        t = seg(C_KB, C_VB)
        kb_raw_ref[...] = t.astype(BF16)
        t = _qknorm_rope(t, gkb_ref[...], cos4, sin4, bd256_ref[...])
        kvb_ref[:, :WIDTH] = t.astype(BF16)
        _store_folded(kvbf_ref, t, scr)
        t = seg(C_VB, C_GB)
        kvb_ref[:, WIDTH:] = t.astype(BF16)
        _store_folded(kvbf_ref, t, scr, WIDTH)
        gb_ref[...] = seg(C_GB, C_END).astype(BF16)

    sds = jax.ShapeDtypeStruct
    ln = s // FOLD
    out_shape = (sds((s, WIDTH), BF16), sds((s, 2 * A_KV_WIDTH), BF16), sds((s, WIDTH), BF16),
                 sds((s, 2 * WIDTH), BF16), sds((FOLD, ln, WIDTH), BF16), sds((FOLD, ln, 2 * WIDTH), BF16),
                 sds((s, WIDTH), BF16), sds((s, A_KV_WIDTH), BF16), sds((s, WIDTH), BF16),
                 sds((s, WIDTH), BF16), sds((s, WIDTH), BF16), sds((s, WIDTH), BF16))
    out_specs = (_rows(WIDTH), _rows(2 * A_KV_WIDTH), _rows(WIDTH), _rows(2 * WIDTH),
                 _folded_rows(WIDTH), _folded_rows(2 * WIDTH),
                 _rows(WIDTH), _rows(A_KV_WIDTH), _rows(WIDTH), _rows(WIDTH), _rows(WIDTH), _rows(WIDTH))
    return pl.pallas_call(
        body, name="inproj_fwd", grid=(s // tm,),
        in_specs=[_rows(D_MODEL), _whole(gain.shape), pl.BlockSpec(memory_space=pl.ANY), _rows(LANES), _rows(LANES),
                  _whole(gqa.shape), _whole(gka.shape), _whole(gqb.shape), _whole(gkb.shape), _whole(bd256.shape),
                  _whole(bd128.shape)],
        out_specs=out_specs, out_shape=out_shape,
        scratch_shapes=[pltpu.VMEM((IN_WIDTH, D_MODEL), BF16), _fold_scratch(WIDTH)],
        compiler_params=pltpu.CompilerParams(dimension_semantics=("arbitrary",), vmem_limit_bytes=VMEM_LIMIT),
    )(x2, gain, w_bf, cos, sin_s, gqa, gka, gqb, gkb, bd256, bd128)


def _seq_pos(idx, dil):
    if dil == 4:
        return 4 * (idx % 32) + idx // 32
    return idx


SOFTMAX_ROWS = 64


def _upper_mask(dil, r0=0, rows=2 * BLOCK):
    qi = (lax.broadcasted_iota(jnp.int32, (rows, BLOCK), 0) + r0) % BLOCK
    kj = lax.broadcasted_iota(jnp.int32, (rows, BLOCK), 1)
    return _seq_pos(kj, dil) > _seq_pos(qi, dil)


def _eye_mask(r0=0, rows=2 * BLOCK):
    qi = (lax.broadcasted_iota(jnp.int32, (rows, BLOCK), 0) + r0) % BLOCK
    kj = lax.broadcasted_iota(jnp.int32, (rows, BLOCK), 1)
    return qi == kj


def _stack_heads(a2, c, gqa):
    lane = lax.broadcasted_iota(jnp.int32, (1, LANES), 1) // HEAD_DIM
    zero = jnp.zeros_like(a2)
    if gqa:
        keep = lane == (c // 2)
        return jnp.concatenate([jnp.where(keep, a2, zero), jnp.where(keep, _swap_heads(a2), zero)], axis=0)
    return jnp.concatenate([jnp.where(lane == 0, a2, zero), jnp.where(lane == 1, a2, zero)], axis=0)


def _unstack_heads(a, c, gqa):
    lane = lax.broadcasted_iota(jnp.int32, (1, LANES), 1) // HEAD_DIM
    if gqa:
        return jnp.where(lane == (c // 2), a[:BLOCK], _swap_heads(a[BLOCK:]))
    return jnp.where(lane == 0, a[:BLOCK], a[BLOCK:])


def _stacked_head_ids(c, gqa):
    if gqa:
        return 2 * c + c // 2, 2 * c + 1 - c // 2
    return 2 * c, 2 * c + 1


def _per_head_rows(blk, heads):
    return jnp.concatenate([blk[:, heads[0]:heads[0] + 1], blk[:, heads[1]:heads[1] + 1]], axis=0)


def _attn_view(a, dil):
    if dil == 1:
        return a[None]
    if dil == 4:
        return a.reshape(4, 4, a.shape[1], a.shape[2])
    return a


def _attn_unview(a, dil):
    if dil == 1:
        return a[0]
    if dil == 4:
        return a.reshape(FOLD, a.shape[2], a.shape[3])
    return a


ATTN_BLOCKS_PER_STEP = 8


def _attn_specs(dil):
    if dil == 4:
        def spec(n, fn):
            return lambda w: pl.BlockSpec((4, None, n * BLOCK // 4, w), lambda r, i: (0, r, fn(i), 0))
    else:
        def spec(n, fn):
            return lambda w: pl.BlockSpec((None, n * BLOCK, w), lambda r, i: (r, fn(i), 0))
    return spec


def _blk_rows(g, dil):
    n = BLOCK // 4 if dil == 4 else BLOCK
    if isinstance(g, int):
        return slice(g * n, (g + 1) * n)
    return pl.ds(pl.multiple_of(g * n, n), n)


def _blk_load(ref, sl, dil, g=0):
    if dil == 4:
        return ref[:, _blk_rows(g, dil), sl].reshape(BLOCK, sl.stop - sl.start)
    return ref[_blk_rows(g, dil), sl]


def _blk_store(ref, sl, val, dil, g=0):
    val = val.astype(ref.dtype)
    if dil == 4:
        ref[:, _blk_rows(g, dil), sl] = val.reshape(4, BLOCK // 4, sl.stop - sl.start)
    else:
        ref[_blk_rows(g, dil), sl] = val


def _swap_heads(a):
    return pltpu.roll(a.astype(F32), HEAD_DIM, 1).astype(a.dtype)


STAT_SHIFT = 8


def _attn_fwd(q, kv, sinks, *, dil, max_dist, name):
    q, kv = _attn_view(q, dil), _attn_view(kv, dil)
    kw = kv.shape[-1] // 2
    gqa = kw == A_KV_WIDTH
    n_seq = dil
    nb = (q.shape[-2] * (4 if dil == 4 else 1)) // BLOCK
    per_step = min(ATTN_BLOCKS_PER_STEP, nb)
    with_sinks = sinks is not None
    all_lanes = slice(0, LANES)
    assert max_dist in (BLOCK - 1, BLOCK) and nb % per_step == 0
    diag = max_dist == BLOCK

    def body(*refs):
        if with_sinks:
            q_ref, kvp_ref, kvc_ref, sink_ref, o_ref, ml_ref = refs
        else:
            q_ref, kvp_ref, kvc_ref, o_ref, ml_ref = refs

        def block(g, has_prev):
            prev_ref, prev_g = (kvp_ref, 0) if isinstance(g, int) else (kvc_ref, g - 1)
            lane = lax.broadcasted_iota(jnp.int32, (1, LANES), 1)
            with_diag = diag and has_prev
            upper, eye = _upper_mask(dil), _eye_mask()
            first_rows = lax.broadcasted_iota(jnp.int32, (2 * BLOCK, 1), 0) < BLOCK
            ml_blk = jnp.zeros((BLOCK, LANES), F32)
            chunks = range(WIDTH // LANES)
            scores, values = [], []
            for c in chunks:
                sl = slice(c * LANES, (c + 1) * LANES)
                ksl = slice(0, LANES) if gqa else sl
                vsl = slice(ksl.start + kw, ksl.stop + kw)
                kcur, vcur = _blk_load(kvc_ref, ksl, dil, g), _blk_load(kvc_ref, vsl, dil, g)
                qs = _stack_heads(_blk_load(q_ref, sl, dil, g), c, gqa)
                if has_prev:
                    kcur = jnp.concatenate([_blk_load(prev_ref, ksl, dil, prev_g), kcur], axis=0)
                    vcur = jnp.concatenate([_blk_load(prev_ref, vsl, dil, prev_g), vcur], axis=0)
                scores.append(_dot_nt(qs, kcur))
                values.append(vcur)
            probs = []
            for c in chunks:
                heads = _stacked_head_ids(c, gqa)
                s = scores[c]
                if has_prev:
                    s_p = s[:, :BLOCK]
                    sc = jnp.where(upper, s_p, s[:, BLOCK:])
                else:
                    sc = jnp.where(upper, NEG, s)
                if with_diag:
                    sd = jnp.where(eye, s_p, NEG)
                    m = jnp.max(jnp.maximum(sc, sd), axis=-1, keepdims=True)
                else:
                    m = jnp.max(sc, axis=-1, keepdims=True)
                if with_sinks:
                    sk = jnp.where(first_rows, sink_ref[0, heads[0]], sink_ref[0, heads[1]])
                    m = jnp.maximum(m, sk)
                p = jnp.exp(sc - m)
                zero = jnp.zeros_like(p)
                if with_diag:
                    pd = jnp.exp(sd - m)
                    l = jnp.sum(p + pd, axis=-1, keepdims=True)
                else:
                    pd = zero
                    l = jnp.sum(p, axis=-1, keepdims=True)
                if with_sinks:
                    l = l + jnp.exp(sk - m)
                pf = jnp.where(upper, zero, p)
                if has_prev:
                    pf = jnp.concatenate([jnp.where(upper, p, pd), pf], axis=1)
                probs.append(pf.astype(BF16))
                for n, h in enumerate(heads):
                    rows = slice(n * BLOCK, (n + 1) * BLOCK)
                    ml_blk = jnp.where(lane == h, m[rows], ml_blk)
                    ml_blk = jnp.where(lane == h + STAT_SHIFT, l[rows], ml_blk)
            for c in chunks:
                sl = slice(c * LANES, (c + 1) * LANES)
                _blk_store(o_ref, sl, _unstack_heads(_dot(probs[c], values[c]), c, gqa), dil, g)
            _blk_store(ml_ref, all_lanes, ml_blk, dil, g)

        @pl.when(pl.program_id(1) == 0)
        def _():
            block(0, False)

        @pl.when(pl.program_id(1) > 0)
        def _():
            block(0, True)

        if per_step > 1:
            def rest(g, carry):
                block(g, True)
                return carry

            lax.fori_loop(1, per_step, rest, 0)

    spec = _attn_specs(dil)
    cur = spec(per_step, lambda i: i)
    prev = spec(1, lambda i: jnp.maximum(i * per_step - 1, 0))
    in_specs = [cur(WIDTH), prev(2 * kw), cur(2 * kw)]
    args = [q, kv, kv]
    if with_sinks:
        in_specs.append(pl.BlockSpec(memory_space=pltpu.SMEM))
        args.append(sinks)
    stats = jax.ShapeDtypeStruct(q.shape[:-1] + (LANES,), F32)
    o, ml = pl.pallas_call(
        body, name=name, grid=(n_seq, nb // per_step), in_specs=in_specs,
        out_specs=(cur(WIDTH), cur(LANES)),
        out_shape=(jax.ShapeDtypeStruct(q.shape, BF16), stats),
        compiler_params=pltpu.CompilerParams(dimension_semantics=("arbitrary", "arbitrary"),
                                             vmem_limit_bytes=VMEM_LIMIT),
    )(*args)
    return _attn_unview(o, dil), _attn_unview(ml, dil)


def _attn_bwd(q, kv, do, ld, *, dil, max_dist, name):
    q, kv, do, ld = (_attn_view(a, dil) for a in (q, kv, do, ld))
    kw = kv.shape[-1] // 2
    gqa = kw == A_KV_WIDTH
    n_seq = dil
    nb = (q.shape[-2] * (4 if dil == 4 else 1)) // BLOCK
    n_kc = kw // LANES
    per_step = min(ATTN_BLOCKS_PER_STEP, nb)
    all_lanes = slice(0, LANES)
    assert max_dist in (BLOCK - 1, BLOCK) and nb % per_step == 0
    diag = max_dist == BLOCK

    def body(q_ref, kvp_ref, kvc_ref, do_ref, ld_ref, dq_ref, dkv_ref, ck_ref, cv_ref):
        i = pl.program_id(1)

        def block(g, has_prev):
            prev_ref, prev_g = (kvp_ref, 0) if isinstance(g, int) else (kvc_ref, g - 1)
            seq_blk = i * per_step + g
            upper, eye = _upper_mask(dil), _eye_mask()
            ld_blk = _blk_load(ld_ref, all_lanes, dil, g)
            dk_acc = [None] * n_kc
            dv_acc = [None] * n_kc
            chunks = range(WIDTH // LANES)
            operands, products = [], []
            for c in chunks:
                sl = slice(c * LANES, (c + 1) * LANES)
                kc = 0 if gqa else c
                ksl = slice(kc * LANES, (kc + 1) * LANES)
                vsl = slice(ksl.start + kw, ksl.stop + kw)
                k2, v2 = _blk_load(kvc_ref, ksl, dil, g), _blk_load(kvc_ref, vsl, dil, g)
                if has_prev:
                    k2 = jnp.concatenate([_blk_load(prev_ref, ksl, dil, prev_g), k2], axis=0)
                    v2 = jnp.concatenate([_blk_load(prev_ref, vsl, dil, prev_g), v2], axis=0)
                qs = _stack_heads(_blk_load(q_ref, sl, dil, g), c, gqa)
                dos = _stack_heads(_blk_load(do_ref, sl, dil, g), c, gqa)
                operands.append((qs, dos, k2))
                products.append((_dot_nt(qs, k2), _dot_nt(dos, v2)))
            weights = []
            for c in chunks:
                heads = _stacked_head_ids(c, gqa)
                lse2 = _per_head_rows(ld_blk, heads)
                dl2 = _per_head_rows(ld_blk, tuple(h + STAT_SHIFT for h in heads))
                s, dp = products[c]
                if has_prev:
                    s_p, dp_p = s[:, :BLOCK], dp[:, :BLOCK]
                    sc = jnp.where(upper, s_p, s[:, BLOCK:])
                    dpc = jnp.where(upper, dp_p, dp[:, BLOCK:])
                else:
                    sc = jnp.where(upper, NEG, s)
                    dpc = dp
                p = jnp.exp(sc - lse2)
                ds = p * (dpc - dl2)
                zero = jnp.zeros_like(p)
                pf = jnp.where(upper, zero, p)
                dsf = jnp.where(upper, zero, ds)
                if has_prev:
                    if diag:
                        pd = jnp.exp(jnp.where(eye, s_p, NEG) - lse2)
                        dsd = pd * (dp_p - dl2)
                    else:
                        pd = dsd = zero
                    pf = jnp.concatenate([jnp.where(upper, p, pd), pf], axis=1)
                    dsf = jnp.concatenate([jnp.where(upper, ds, dsd), dsf], axis=1)
                weights.append((pf.astype(BF16), dsf.astype(BF16)))
            for c in chunks:
                sl = slice(c * LANES, (c + 1) * LANES)
                kc = 0 if gqa else c
                qs, dos, k2 = operands[c]
                pf, dsf = weights[c]
                _blk_store(dq_ref, sl, _unstack_heads(_dot(dsf, k2), c, gqa) * SCALE, dil, g)
                dk2 = _dot_tn(dsf, qs)
                dv2 = _dot_tn(pf, dos)
                dk_acc[kc] = dk2 if dk_acc[kc] is None else dk_acc[kc] + dk2
                dv_acc[kc] = dv2 if dv_acc[kc] is None else dv_acc[kc] + dv2
            for kc in range(n_kc):
                sl = slice(kc * LANES, (kc + 1) * LANES)
                vsl = slice(sl.start + kw, sl.stop + kw)
                if has_prev:
                    _blk_store(dkv_ref, sl, ck_ref[:, sl] + dk_acc[kc][:BLOCK], dil, seq_blk - 1)
                    _blk_store(dkv_ref, vsl, cv_ref[:, sl] + dv_acc[kc][:BLOCK], dil, seq_blk - 1)
                    ck_ref[:, sl] = dk_acc[kc][BLOCK:]
                    cv_ref[:, sl] = dv_acc[kc][BLOCK:]
                else:
                    ck_ref[:, sl] = dk_acc[kc]
                    cv_ref[:, sl] = dv_acc[kc]

        @pl.when(i == 0)
        def _():
            block(0, False)

        @pl.when(i > 0)
        def _():
            block(0, True)

        if per_step > 1:
            def rest(g, carry):
                block(g, True)
                return carry

            lax.fori_loop(1, per_step, rest, 0)

        @pl.when(i == nb // per_step - 1)
        def _():
            for kc in range(n_kc):
                sl = slice(kc * LANES, (kc + 1) * LANES)
                _blk_store(dkv_ref, sl, ck_ref[:, sl], dil, nb - 1)
                _blk_store(dkv_ref, slice(sl.start + kw, sl.stop + kw), cv_ref[:, sl], dil, nb - 1)

    spec = _attn_specs(dil)
    cur = spec(per_step, lambda i: i)
    prev = spec(1, lambda i: jnp.maximum(i * per_step - 1, 0))
    if dil == 4:
        whole = pl.BlockSpec((4, None, kv.shape[2], 2 * kw), lambda r, i: (0, r, 0, 0))
    else:
        whole = pl.BlockSpec((None, kv.shape[1], 2 * kw), lambda r, i: (r, 0, 0))
    sds = jax.ShapeDtypeStruct
    dq, dkv = pl.pallas_call(
        body, name=name, grid=(n_seq, nb // per_step),
        in_specs=[cur(WIDTH), prev(2 * kw), cur(2 * kw), cur(WIDTH), cur(LANES)],
        out_specs=(cur(WIDTH), whole),
        out_shape=(sds(q.shape, BF16), sds(kv.shape, BF16)),
        scratch_shapes=[pltpu.VMEM((BLOCK, kw), F32), pltpu.VMEM((BLOCK, kw), F32)],
        compiler_params=pltpu.CompilerParams(dimension_semantics=("arbitrary", "arbitrary"),
                                             vmem_limit_bytes=VMEM_LIMIT),
    )(q, kv, kv, do, ld)
    return _attn_unview(dq, dil), _attn_unview(dkv, dil)


def _outproj(att_a, att_b1, att_b4, att_b16, g_a, g_b, x2, tgt2, w_out_bf, sink_row, perm):
    s = x2.shape[0]
    tm = ROW_TILE

    def body(oa_ref, mla_ref, ob1_ref, ml1_ref, ob4_ref, ml4_ref, ob16_ref, ml16_ref,
             ga_ref, gb_ref, x_ref, t_ref, w_ref, sink_ref, perm_ref,
             dy_ref, doa_ref, dob_ref, dobf_ref, dga_ref, dgb_ref, lda_ref, ldb_ref, ldbf_ref,
             gw_ref, loss_ref, dsink_ref, scr_st):
        i = pl.program_id(0)
        perm = perm_ref[...]
        lane = lax.broadcasted_iota(jnp.int32, (tm, LANES), 1)
        used = lane < HEADS

        def split(ml):
            return jnp.where(used, ml, 0.0), jnp.where(used, pltpu.roll(ml, LANES - STAT_SHIFT, 1), 1.0)

        @pl.when(i == 0)
        def _():
            gw_ref[...] = jnp.zeros_like(gw_ref)
            loss_ref[...] = jnp.zeros_like(loss_ref)
            dsink_ref[...] = jnp.zeros_like(dsink_ref)

        ms, ls = zip(split(ml1_ref[...]), split(_load_folded(ml4_ref, scr_st)), split(_load_folded(ml16_ref, scr_st)))
        mx = jnp.maximum(jnp.maximum(ms[0], ms[1]), ms[2])
        scale = [jnp.exp(mp - mx) for mp in ms]
        den = (ls[0] * scale[0] + ls[1] * scale[1]) + ls[2] * scale[2]
        lse_b = jnp.where(used, mx + jnp.log(den), 0.0)
        inv_den = 1.0 / den
        o_b = _expand_heads(scale[0] * inv_den) * ob1_ref[...].astype(F32)
        o_b = o_b + _expand_heads(scale[1] * inv_den) * _load_folded_bf16(ob4_ref, perm)
        o_b = o_b + _expand_heads(scale[2] * inv_den) * _load_folded_bf16(ob16_ref, perm)
        m_a, l_a = split(mla_ref[...])
        lse_a = jnp.where(used, m_a + jnp.log(l_a), 0.0)
        o_a = _expand_heads(1.0 / l_a) * oa_ref[...].astype(F32)
        g_a = ga_ref[...].astype(F32)
        g_b = gb_ref[...].astype(F32)
        sg_a = _sigmoid(g_a)
        sg_b = _sigmoid(g_b)
        silu_a = g_a * sg_a
        silu_b = g_b * sg_b
        mixed = jnp.concatenate([o_a * silu_a, o_b * silu_b], axis=1).astype(BF16)
        w = w_ref[...]
        y = x_ref[...] + _dot(mixed, w)
        diff = y - t_ref[...]
        loss_ref[...] += (0.5 / D_MODEL) * jnp.sum(diff * diff)
        dy = diff * (1.0 / D_MODEL)
        dy_ref[...] = dy
        dyb = dy.astype(BF16)
        gw_ref[...] += _dot_tn(mixed, dyb)
        dmixed = _dot_nt(dyb, w)
        dm_a = dmixed[:, :WIDTH]
        dm_b = dmixed[:, WIDTH:]
        do_a = dm_a * silu_a
        do_b = dm_b * silu_b
        doa_ref[...] = do_a.astype(BF16)
        dob_ref[...] = do_b.astype(BF16)
        _store_folded_bf16(dobf_ref, do_b, perm)
        dga_ref[...] = (dm_a * o_a * (sg_a * (1.0 + g_a * (1.0 - sg_a)))).astype(BF16)
        dgb_ref[...] = (dm_b * o_b * (sg_b * (1.0 + g_b * (1.0 - sg_b)))).astype(BF16)
        dl_a = _reduce_heads(do_a * o_a)
        dl_b = _reduce_heads(do_b * o_b)
        lda_ref[...] = lse_a + pltpu.roll(dl_a, STAT_SHIFT, 1)
        ld_b = lse_b + pltpu.roll(dl_b, STAT_SHIFT, 1)
        ldb_ref[...] = ld_b
        _store_folded(ldbf_ref, ld_b, scr_st)
        dsink_ref[...] -= jnp.sum(jnp.exp(sink_ref[...] - lse_a) * dl_a, axis=0, keepdims=True)

    sds = jax.ShapeDtypeStruct
    ln = s // FOLD
    natural = [_rows(WIDTH), _rows(LANES)]
    folded = [_folded_rows(WIDTH), _folded_rows(LANES)]
    return pl.pallas_call(
        body, name="outproj_fwd_bwd", grid=(s // tm,),
        in_specs=natural + natural + folded + folded
                 + [_rows(WIDTH), _rows(WIDTH), _rows(D_MODEL), _rows(D_MODEL), _whole((D_MODEL, D_MODEL)),
                    _whole((1, LANES)), _whole(perm.shape)],
        out_specs=(_rows(D_MODEL), _rows(WIDTH), _rows(WIDTH), _folded_rows(WIDTH), _rows(WIDTH), _rows(WIDTH),
                   _rows(LANES), _rows(LANES), _folded_rows(LANES),
                   _whole((D_MODEL, D_MODEL)), _whole((1, LANES)), _whole((1, LANES))),
        out_shape=(sds((s, D_MODEL), F32), sds((s, WIDTH), BF16), sds((s, WIDTH), BF16),
                   sds((FOLD, ln, WIDTH), BF16), sds((s, WIDTH), BF16), sds((s, WIDTH), BF16),
                   sds((s, LANES), F32), sds((s, LANES), F32), sds((FOLD, ln, LANES), F32),
                   sds((D_MODEL, D_MODEL), F32), sds((1, LANES), F32), sds((1, LANES), F32)),
        scratch_shapes=[_fold_scratch(LANES)],
        compiler_params=pltpu.CompilerParams(dimension_semantics=("arbitrary",), vmem_limit_bytes=VMEM_LIMIT),
    )(*att_a, *att_b1, *att_b4, *att_b16, g_a, g_b, x2, tgt2, w_out_bf, sink_row, perm)


def _inproj_bwd(x2, dy, gain, w_bf, cos, sin_s, gqa, gka, gqb, gkb, bd256, bd128, perm,
                qa_raw, ka_raw, qb_raw, kb_raw, d_a, d_b1, d_b4, d_b16, dg_a, dg_b):
    s = x2.shape[0]
    tm = ROW_TILE

    def body(x_ref, dy_ref, gain_ref, w_hbm, cos_ref, sin_ref, gqa_ref, gka_ref, gqb_ref, gkb_ref, bd256_ref,
             bd128_ref, perm_ref, qa_raw_ref, ka_raw_ref, qb_raw_ref, kb_raw_ref, dqa_ref, dkva_ref,
             dq1_ref, dkv1_ref, dq4_ref, dkv4_ref, dq16_ref, dkv16_ref, dga_ref, dgb_ref,
             gx_ref, ht_ref, win_ref,
             dgain_ref, dgqa_ref, dgka_ref, dgqb_ref, dgkb_ref, w_vmem, dproj_ref):
        i = pl.program_id(0)
        perm = perm_ref[...]

        @pl.when(i == 0)
        def _():
            pltpu.sync_copy(w_hbm, w_vmem)
            dgain_ref[...] = jnp.zeros_like(dgain_ref)
            dgqa_ref[...] = jnp.zeros_like(dgqa_ref)
            dgka_ref[...] = jnp.zeros_like(dgka_ref)
            dgqb_ref[...] = jnp.zeros_like(dgqb_ref)
            dgkb_ref[...] = jnp.zeros_like(dgkb_ref)

        cos1 = cos_ref[...]
        sin1 = sin_ref[...]
        cos4 = jnp.tile(cos1, (1, 4))
        sin4 = jnp.tile(sin1, (1, 4))

        dt, dg = _qknorm_rope_bwd(dqa_ref[...], qa_raw_ref[...], gqa_ref[...], cos4, sin4, bd256_ref[...])
        dproj_ref[:, C_QA:C_KA] = dt.astype(BF16)
        dgqa_ref[...] += jnp.sum(dg, axis=0, keepdims=True)
        dt, dg = _qknorm_rope_bwd(dkva_ref[:, :A_KV_WIDTH], ka_raw_ref[...], gka_ref[...], cos1, sin1,
                                  bd128_ref[...])
        dproj_ref[:, C_KA:C_VA] = dt.astype(BF16)
        dgka_ref[...] += jnp.sum(dg, axis=0, keepdims=True)
        dproj_ref[:, C_VA:C_GA] = dkva_ref[:, A_KV_WIDTH:]
        dproj_ref[:, C_GA:C_QB] = dga_ref[...]
        dq = (dq1_ref[...].astype(F32) + _load_folded_bf16(dq4_ref, perm)) + _load_folded_bf16(dq16_ref, perm)
        dt, dg = _qknorm_rope_bwd(dq, qb_raw_ref[...], gqb_ref[...], cos4, sin4, bd256_ref[...])
        dproj_ref[:, C_QB:C_KB] = dt.astype(BF16)
        dgqb_ref[...] += jnp.sum(dg, axis=0, keepdims=True)
        dkv = (dkv1_ref[...].astype(F32) + _load_folded_bf16(dkv4_ref, perm)) + _load_folded_bf16(dkv16_ref, perm)
        dt, dg = _qknorm_rope_bwd(dkv[:, :WIDTH], kb_raw_ref[...], gkb_ref[...], cos4, sin4, bd256_ref[...])
        dproj_ref[:, C_KB:C_VB] = dt.astype(BF16)
        dgkb_ref[...] += jnp.sum(dg, axis=0, keepdims=True)
        dproj_ref[:, C_VB:C_GB] = dkv[:, WIDTH:].astype(BF16)
        dproj_ref[:, C_GB:C_END] = dgb_ref[...]
        for k, start in enumerate(WIN_START):
            win_ref[k] = dproj_ref[:, start:start + WIN]

        xt = x_ref[...]
        gain_row = gain_ref[...]
        r = lax.rsqrt(jnp.mean(xt * xt, axis=-1, keepdims=True) + EPS)
        xr = xt * r
        ht_ref[...] = (xr * gain_row).T.astype(BF16)
        dh = _dot(dproj_ref[...], w_vmem[...])
        dgain_ref[...] += jnp.sum(dh * xr, axis=0, keepdims=True)
        u = dh * gain_row
        gx_ref[...] = dy_ref[...] + r * (u - xr * jnp.mean(u * xr, axis=-1, keepdims=True))

    def acc_row(w):
        return pl.BlockSpec((1, w), lambda i: (0, 0))

    sds = jax.ShapeDtypeStruct
    any_spec = pl.BlockSpec(memory_space=pl.ANY)
    win_spec = pl.BlockSpec((N_CHIP, tm, WIN), lambda i: (0, i, 0))
    return pl.pallas_call(
        body, name="inproj_bwd", grid=(s // tm,),
        in_specs=[_rows(D_MODEL), _rows(D_MODEL), _whole(gain.shape), any_spec, _rows(LANES), _rows(LANES),
                  _whole(gqa.shape), _whole(gka.shape), _whole(gqb.shape), _whole(gkb.shape), _whole(bd256.shape),
                  _whole(bd128.shape), _whole(perm.shape),
                  _rows(WIDTH), _rows(A_KV_WIDTH), _rows(WIDTH), _rows(WIDTH),
                  _rows(WIDTH), _rows(2 * A_KV_WIDTH), _rows(WIDTH), _rows(2 * WIDTH)]
                 + [_folded_rows(WIDTH), _folded_rows(2 * WIDTH)] * 2 + [_rows(WIDTH), _rows(WIDTH)],
        out_specs=(_rows(D_MODEL), pl.BlockSpec((D_MODEL, tm), lambda i: (0, i)), win_spec, acc_row(D_MODEL), acc_row(WIDTH), acc_row(A_KV_WIDTH), acc_row(WIDTH), acc_row(WIDTH)),
        out_shape=(sds((s, D_MODEL), F32), sds((D_MODEL, s), BF16), sds((N_CHIP, s, WIN), BF16),
                   sds((1, D_MODEL), F32),
                   sds((1, WIDTH), F32), sds((1, A_KV_WIDTH), F32), sds((1, WIDTH), F32), sds((1, WIDTH), F32)),
        scratch_shapes=[pltpu.VMEM((IN_WIDTH, D_MODEL), BF16), pltpu.VMEM((tm, IN_WIDTH), BF16)],
        compiler_params=pltpu.CompilerParams(dimension_semantics=("arbitrary",), vmem_limit_bytes=VMEM_LIMIT),
    )(x2, dy, gain, w_bf, cos, sin_s, gqa, gka, gqb, gkb, bd256, bd128, perm, qa_raw, ka_raw, qb_raw, kb_raw,
      *d_a, *d_b1, *d_b4, *d_b16, dg_a, dg_b)


def _rope_tables(s):
    half = HEAD_DIM // 2
    inv = jnp.tile(ROPE_THETA ** (-jnp.arange(half, dtype=F32) / half), 4)
    sign = jnp.tile(jnp.concatenate([-jnp.ones((half,), F32), jnp.ones((half,), F32)]), 2)
    ang = jnp.arange(s).astype(F32)[:, None] * inv[None, :]
    return jnp.cos(ang), jnp.sin(ang) * sign[None, :]


def _block_diag_ones(w):
    idx = jnp.arange(w) // HEAD_DIM
    return (idx[:, None] == idx[None, :]).astype(BF16)


def _local_step(x2, tgt2, norm_gain, w_in_bf, q_norm_a, k_norm_a, sinks_a, q_norm_b, k_norm_b, w_out_bf):
    s = x2.shape[0]
    cos, sin_s = _rope_tables(s)
    bd256, bd128 = _block_diag_ones(2 * LANES), _block_diag_ones(A_KV_WIDTH)
    gqa = jnp.tile(q_norm_a, (1, HEADS))
    gka = jnp.tile(k_norm_a, (1, 2))
    gqb = jnp.tile(q_norm_b, (1, HEADS))
    gkb = jnp.tile(k_norm_b, (1, HEADS))
    sink_row = jnp.pad(sinks_a, ((0, 0), (0, LANES - HEADS)))
    perm = _fold_matrix()

    (qa, kva, qb, kvb, qbf, kvbf, qa_raw, ka_raw, g_a, qb_raw, kb_raw, g_b) = _inproj(
        x2, norm_gain, w_in_bf, cos, sin_s, gqa, gka, gqb, gkb, bd256, bd128)

    att_a = _attn_fwd(qa, kva, sinks_a, dil=1, max_dist=A_MAX_DIST, name="attn_a_fwd")
    att_b1 = _attn_fwd(qb, kvb, None, dil=1, max_dist=B_MAX_DIST, name="attn_b1_fwd")
    att_b4 = _attn_fwd(qbf, kvbf, None, dil=4, max_dist=B_MAX_DIST, name="attn_b4_fwd")
    att_b16 = _attn_fwd(qbf, kvbf, None, dil=16, max_dist=B_MAX_DIST, name="attn_b16_fwd")

    (dy, do_a, do_b, do_bf, dg_a, dg_b, ld_a, ld_b, ld_bf, gw_out, loss_part, dsink) = _outproj(
        att_a, att_b1, att_b4, att_b16, g_a, g_b, x2, tgt2, w_out_bf, sink_row, perm)

    d_a = _attn_bwd(qa, kva, do_a, ld_a, dil=1, max_dist=A_MAX_DIST, name="attn_a_bwd")
    d_b1 = _attn_bwd(qb, kvb, do_b, ld_b, dil=1, max_dist=B_MAX_DIST, name="attn_b1_bwd")
    d_b4 = _attn_bwd(qbf, kvbf, do_bf, ld_bf, dil=4, max_dist=B_MAX_DIST, name="attn_b4_bwd")
    d_b16 = _attn_bwd(qbf, kvbf, do_bf, ld_bf, dil=16, max_dist=B_MAX_DIST, name="attn_b16_bwd")

    gx, h_t, wins, dgain, dgqa, dgka, dgqb, dgkb = _inproj_bwd(
        x2, dy, norm_gain, w_in_bf, cos, sin_s, gqa, gka, gqb, gkb, bd256, bd128, perm,
        qa_raw, ka_raw, qb_raw, kb_raw, d_a, d_b1, d_b4, d_b16, dg_a, dg_b)
    return loss_part, gx, h_t, wins, gw_out, (dgain, dgqa, dgka, dsink, dgqb, dgkb)


def _position():
    return lax.axis_index("x"), lax.axis_index("y"), lax.axis_index("c")


GATHER_CHUNKS = 2


def _gather_weights(blocks, name):
    n = len(blocks)
    ch = GATHER_CHUNKS
    n_sems = n * (N_CHIP - 1) * ch

    def body(*refs):
        src_refs, dst_refs = refs[:n], refs[n:2 * n]
        ici_send, ici_recv, d2d_send, d2d_recv, local_sems = refs[2 * n:]
        x, y, c = _position()
        b = 2 * x + y
        copies = []
        for k in range(n):
            local = pltpu.make_async_copy(src_refs[k], dst_refs[k].at[b], local_sems.at[k])
            local.start()
            copies.append(local)

        def rows(k, core, j):
            half = blocks[k].shape[0] // 2
            return pl.ds(core * half + j * (half // ch), half // ch)

        plan = []
        for d in range(1, N_CHIP):
            px, py = x ^ (d >> 1), y ^ (d & 1)
            for j in range(ch):
                for k in range(n):
                    plan.append((px, py, 2 * px + py, k, j, ((d - 1) * ch + j) * n + k))
        sends = []
        for px, py, pb, k, j, sem in plan:
            send = pltpu.make_async_remote_copy(
                src_ref=src_refs[k].at[rows(k, c, j)], dst_ref=dst_refs[k].at[b, rows(k, c, j)],
                send_sem=ici_send.at[sem], recv_sem=ici_recv.at[sem], device_id=(px, py, c), device_id_type=MESH)
            send.start()
            sends.append(send)
        for px, py, pb, k, j, sem in plan:
            landed = dst_refs[k].at[pb, rows(k, c, j)]
            pltpu.make_async_remote_copy(
                src_ref=landed, dst_ref=landed, send_sem=ici_send.at[sem], recv_sem=ici_recv.at[sem],
                device_id=(px, py, c), device_id_type=MESH).wait_recv()
            forward = pltpu.make_async_remote_copy(
                src_ref=landed, dst_ref=landed, send_sem=d2d_send.at[sem], recv_sem=d2d_recv.at[sem],
                device_id=(x, y, 1 - c), device_id_type=MESH)
            forward.start()
            sends.append(forward)
        for px, py, pb, k, j, sem in plan:
            passed = dst_refs[k].at[pb, rows(k, 1 - c, j)]
            pltpu.make_async_remote_copy(
                src_ref=passed, dst_ref=passed, send_sem=d2d_send.at[sem], recv_sem=d2d_recv.at[sem],
                device_id=(x, y, 1 - c), device_id_type=MESH).wait_recv()
        for send in sends:
            send.wait_send()
        for local in copies:
            local.wait()

    vmem_spec = pl.BlockSpec(memory_space=pltpu.VMEM)
    out_shape = tuple(jax.ShapeDtypeStruct((N_CHIP,) + a.shape, a.dtype) for a in blocks)
    return pl.pallas_call(
        body, name=name, in_specs=[vmem_spec] * n, out_specs=tuple([vmem_spec] * n), out_shape=out_shape,
        scratch_shapes=[pltpu.SemaphoreType.DMA((n_sems,)) for _ in range(4)] + [pltpu.SemaphoreType.DMA((n,))],
        compiler_params=pltpu.CompilerParams(vmem_limit_bytes=VMEM_LIMIT),
    )(*blocks)


def _grad_reduce(order, h_t, wins, gw_out, small):
    s = h_t.shape[1]
    tk = GRAD_ROWS
    n_i = s // tk
    half = D_MODEL // 2
    o_half = OUT_ROWS // 2
    n_rel = N_CHIP - 1

    def body(order_ref, ht_ref, win_ref, gwo_ref, small_ref,
             win_out, wout_out, small_out,
             acc, mine, s1, r1, s2, r2, so1, ro1, so2, ro2, pair_in, pair_o, small_land,
             s1_send, s1_recv, s2_send, s2_recv, o1_send, o1_recv, o2_send, o2_recv,
             pair_send, pair_recv, small_send, small_recv):
        j = pl.program_id(0)
        i = pl.program_id(1)
        x, y, c = _position()
        me = 4 * x + 2 * y + c
        sibling = (x, y, 1 - c)
        my_rows = pl.ds(pl.multiple_of(c * half, half), half)
        sib_rows = pl.ds(pl.multiple_of((1 - c) * half, half), half)

        def chip_of(rel):
            return x ^ (rel >> 1), y ^ (rel & 1)

        def level1(k):
            return pltpu.make_async_remote_copy(src_ref=s1.at[k], dst_ref=r1.at[k], send_sem=s1_send.at[k],
                                                recv_sem=s1_recv.at[k], device_id=sibling, device_id_type=MESH)

        def level2(k):
            px, py = chip_of(RELATIONS[k])
            return pltpu.make_async_remote_copy(src_ref=s2.at[k], dst_ref=r2.at[k], send_sem=s2_send.at[k],
                                                recv_sem=s2_recv.at[k], device_id=(px, py, c), device_id_type=MESH)

        def out_level1(bk):
            return pltpu.make_async_remote_copy(src_ref=so1.at[bk], dst_ref=ro1.at[bk], send_sem=o1_send.at[bk],
                                                recv_sem=o1_recv.at[bk], device_id=sibling, device_id_type=MESH)

        def out_level2(k):
            px, py = chip_of(RELATIONS[k])
            return pltpu.make_async_remote_copy(src_ref=so2.at[k], dst_ref=ro2.at[k], send_sem=o2_send.at[k],
                                                recv_sem=o2_recv.at[k], device_id=(px, py, c), device_id_type=MESH)

        def small_copy(d):
            px, py, pc = x ^ (d >> 2), y ^ ((d >> 1) & 1), c ^ (d & 1)
            return pltpu.make_async_remote_copy(src_ref=small_ref, dst_ref=small_land.at[me],
                                                send_sem=small_send.at[d], recv_sem=small_recv.at[d],
                                                device_id=(px, py, pc), device_id_type=MESH)

        def pair_copy(k, buf):
            return pltpu.make_async_remote_copy(src_ref=buf.at[0], dst_ref=buf.at[1], send_sem=pair_send.at[k],
                                                recv_sem=pair_recv.at[k], device_id=sibling, device_id_type=MESH)

        def out_rows(bk, core):
            return pl.ds(pl.multiple_of(bk * OUT_ROWS + core * o_half, o_half), o_half)

        @pl.when((j == 0) & (i == 0))
        def _():
            for d in range(1, N_DEV):
                small_copy(d).start()
            small_land[me] = small_ref[...]
            for bk in range(N_CHIP):
                so1[bk] = gwo_ref[out_rows(bk, 1 - c), :].astype(BF16)
                out_level1(bk).start()

        @pl.when((j == 0) & (i == 1))
        def _():
            b = 2 * x + y
            for bk in range(N_CHIP):
                out_level1(bk).wait_recv()
            for k in range(n_rel):
                px, py = chip_of(RELATIONS[k])
                bk = 2 * px + py
                so2[k] = (gwo_ref[out_rows(bk, c), :] + ro1[bk].astype(F32)).astype(BF16)
                out_level2(k).start()

        @pl.when(i == 0)
        def _():
            acc[...] = jnp.zeros_like(acc)

        for n0 in range(0, WIN, ACC_COLS):
            n1 = min(n0 + ACC_COLS, WIN)
            acc[:, n0:n1] += _dot(ht_ref[...], win_ref[:, n0:n1])

        for k in range(N_CHIP):
            @pl.when((j == k) & (i == n_i - 1))
            def _(k=k):
                s1[k] = acc[sib_rows, :].astype(BF16)
                level1(k).start()
                mine[...] = acc[my_rows, :]

            if k < n_rel:
                @pl.when((j == k + 1) & (i == 1))
                def _(k=k):
                    level1(k).wait_recv()
                    s2[k] = (mine[...] + r1[k].astype(F32)).astype(BF16)
                    level2(k).start()

        @pl.when((j == N_CHIP - 1) & (i == n_i - 1))
        def _():
            b = 2 * x + y
            level1(N_CHIP - 1).wait_recv()
            total = mine[...] + r1[N_CHIP - 1].astype(F32)
            for k in range(n_rel):
                level2(k).wait_recv()
                total = total + r2[k].astype(F32)
            total = total.T
            pair_in[0] = total
            pair_copy(0, pair_in).start()
            total_o = gwo_ref[out_rows(b, c), :] + ro1[b].astype(F32)
            for k in range(n_rel):
                out_level2(k).wait_recv()
                total_o = total_o + ro2[k].astype(F32)
            pair_o[0] = total_o
            pair_copy(1, pair_o).start()
            win_out[c] = total
            wout_out[c] = total_o
            for d in range(1, N_DEV):
                small_copy(d).wait_recv()
            small_out[...] = small_land[...]
            pair_copy(0, pair_in).wait_recv()
            win_out[1 - c] = pair_in[1]
            pair_copy(1, pair_o).wait_recv()
            wout_out[1 - c] = pair_o[1]
            for d in range(1, N_DEV):
                small_copy(d).wait_send()
            for k in range(N_CHIP):
                level1(k).wait_send()
                out_level1(k).wait_send()
            for k in range(n_rel):
                level2(k).wait_send()
                out_level2(k).wait_send()
            pair_copy(0, pair_in).wait_send()
            pair_copy(1, pair_o).wait_send()

    vmem = pl.BlockSpec(memory_space=pltpu.VMEM)
    dma = pltpu.SemaphoreType.DMA
    sds = jax.ShapeDtypeStruct
    grid_spec = pltpu.PrefetchScalarGridSpec(
        num_scalar_prefetch=1, grid=(N_CHIP, n_i),
        in_specs=[pl.BlockSpec((D_MODEL, tk), lambda j, i, order: (0, i)),
                  pl.BlockSpec((None, tk, WIN), lambda j, i, order: (order[j], i, 0)), vmem, vmem],
        out_specs=(vmem, vmem, vmem),
        scratch_shapes=[
            pltpu.VMEM((D_MODEL, WIN), F32), pltpu.VMEM((half, WIN), F32),
            pltpu.VMEM((N_CHIP, half, WIN), BF16), pltpu.VMEM((N_CHIP, half, WIN), BF16),
            pltpu.VMEM((n_rel, half, WIN), BF16), pltpu.VMEM((n_rel, half, WIN), BF16),
            pltpu.VMEM((N_CHIP, o_half, D_MODEL), BF16), pltpu.VMEM((N_CHIP, o_half, D_MODEL), BF16),
            pltpu.VMEM((n_rel, o_half, D_MODEL), BF16), pltpu.VMEM((n_rel, o_half, D_MODEL), BF16),
            pltpu.VMEM((2, WIN, half), F32), pltpu.VMEM((2, o_half, D_MODEL), F32),
            pltpu.VMEM((N_DEV, PACK_ROWS, D_MODEL), F32),
            dma((N_CHIP,)), dma((N_CHIP,)), dma((n_rel,)), dma((n_rel,)),
            dma((N_CHIP,)), dma((N_CHIP,)), dma((n_rel,)), dma((n_rel,)),
            dma((2,)), dma((2,)), dma((N_DEV,)), dma((N_DEV,))])
    return pl.pallas_call(
        body, name="grad_w_in_reduce", grid_spec=grid_spec,
        out_shape=(sds((2, WIN, half), F32), sds((2, o_half, D_MODEL), F32), sds((N_DEV, PACK_ROWS, D_MODEL), F32)),
        compiler_params=pltpu.CompilerParams(dimension_semantics=("arbitrary", "arbitrary"),
                                             vmem_limit_bytes=VMEM_LIMIT),
    )(order, h_t, wins, gw_out, small)


ADAM_STEPS = 4


def _adamw_math(w, g, m, v):
    m = ADAM_B1 * m + (1.0 - ADAM_B1) * g
    v = ADAM_B2 * v + (1.0 - ADAM_B2) * (g * g)
    m_hat = m / (1.0 - ADAM_B1 ** ADAM_STEP)
    v_hat = v / (1.0 - ADAM_B2 ** ADAM_STEP)
    delta = -ADAM_LR * (m_hat / (jnp.sqrt(v_hat) + ADAM_EPS) + ADAM_WD * w)
    return delta, m, v


def _adamw(w, g, m, v, name):
    r, c = w.shape

    def body(w_ref, g_ref, m_ref, v_ref, d_ref, nm_ref, nv_ref):
        delta, nm, nv = _adamw_math(w_ref[...], g_ref[...], m_ref[...], v_ref[...])
        d_ref[...] = delta
        nm_ref[...] = nm
        nv_ref[...] = nv

    rows = r // ADAM_STEPS
    assert rows * ADAM_STEPS == r and rows % 8 == 0
    spec = pl.BlockSpec((rows, c), lambda i: (i, 0))
    shape = jax.ShapeDtypeStruct((r, c), F32)
    return pl.pallas_call(
        body, name=name, grid=(ADAM_STEPS,), in_specs=[spec] * 4, out_specs=(spec,) * 3,
        out_shape=(shape,) * 3, compiler_params=pltpu.CompilerParams(vmem_limit_bytes=VMEM_LIMIT),
    )(w, g, m, v)


PACK_ROWS = 8


def _fold_heads(v):
    y = v[:, 0:LANES]
    for j in range(1, v.shape[1] // LANES):
        y = y + v[:, j * LANES:(j + 1) * LANES]
    return y + pltpu.roll(y, HEAD_DIM, 1)


def _small_adamw(recv, w_p, m_p, v_p):
    def body(r_ref, w_ref, m_ref, v_ref, g_ref, d_ref, nm_ref, nv_ref, loss_ref):
        tot = r_ref[0]
        for j in range(1, N_DEV):
            tot = tot + r_ref[j]
        loss_ref[...] = tot[3:4, 0:LANES]
        row1 = tot[1:2, :]
        row2 = tot[2:3, :]
        pieces = [_fold_heads(row1[:, 0:WIDTH]), _fold_heads(row2[:, WIDTH:WIDTH + A_KV_WIDTH]),
                  _fold_heads(row1[:, WIDTH:2 * WIDTH]), _fold_heads(row2[:, 0:WIDTH]),
                  row2[:, WIDTH + A_KV_WIDTH:WIDTH + 2 * A_KV_WIDTH], jnp.zeros((1, 3 * LANES), F32)]
        g = jnp.concatenate([tot[0:1, :], jnp.concatenate(pieces, axis=1), jnp.zeros((PACK_ROWS - 2, D_MODEL), F32)],
                            axis=0)
        g_ref[...] = g
        delta, nm, nv = _adamw_math(w_ref[...], g, m_ref[...], v_ref[...])
        d_ref[...] = delta
        nm_ref[...] = nm
        nv_ref[...] = nv

    shape = jax.ShapeDtypeStruct((PACK_ROWS, D_MODEL), F32)
    return pl.pallas_call(body, name="small_adamw",
                          out_shape=(shape,) * 4 + (jax.ShapeDtypeStruct((1, LANES), F32),))(recv, w_p, m_p, v_p)


def _pack_small(norm_gain, q_a, k_a, q_b, k_b, sinks):
    def lane_pad(a):
        return jnp.pad(a, ((0, 0), (0, LANES - a.shape[1])))
    row1 = jnp.concatenate([lane_pad(q_a), lane_pad(k_a), lane_pad(q_b), lane_pad(k_b), lane_pad(sinks),
                            jnp.zeros((1, 3 * LANES), F32)], axis=1)
    return jnp.concatenate([norm_gain, row1, jnp.zeros((PACK_ROWS - 2, D_MODEL), F32)], axis=0)


def _unpack_small(p):
    return (p[0:1, :], p[1:2, 0:HEAD_DIM], p[1:2, LANES:LANES + HEAD_DIM], p[1:2, 2 * LANES:2 * LANES + HEAD_DIM],
            p[1:2, 3 * LANES:3 * LANES + HEAD_DIM], p[1:2, 4 * LANES:4 * LANES + HEADS])


def kernel(x, norm_gain, w_in, q_norm_a, k_norm_a, sinks_a, q_norm_b, k_norm_b, w_out, loss_target, m_norm_gain, m_w_in, m_q_norm_a, m_k_norm_a, m_sinks_a, m_q_norm_b, m_k_norm_b, m_w_out, v_norm_gain, v_w_in, v_q_norm_a, v_k_norm_a, v_sinks_a, v_q_norm_b, v_k_norm_b, v_w_out):
    chip = 2 * lax.axis_index("x") + lax.axis_index("y")

    w_in_t, m_w_in_t, v_w_in_t = w_in[0].T, m_w_in[0].T, v_w_in[0].T

    w_in_all, w_out_all = _gather_weights([w_in_t.astype(BF16), w_out[0].astype(BF16)], "gather_weights")
    w_in_bf = w_in_all.reshape(IN_WIDTH, D_MODEL)
    w_out_bf = w_out_all.reshape(D_MODEL, D_MODEL)

    loss_part, gx, h_t, wins, gw_out, (dgain, dgqa, dgka, dsink, dgqb, dgkb) = _local_step(
        x[0], loss_target[0], norm_gain, w_in_bf, q_norm_a, k_norm_a, sinks_a, q_norm_b, k_norm_b, w_out_bf)

    small = jnp.concatenate([
        dgain, jnp.concatenate([dgqa, dgqb], axis=1),
        jnp.concatenate([dgkb, dgka, dsink, jnp.zeros((1, D_MODEL - WIDTH - 2 * A_KV_WIDTH), F32)], axis=1),
        jnp.pad(loss_part, ((0, 0), (0, D_MODEL - LANES))),
        jnp.zeros((PACK_ROWS - 4, D_MODEL), F32)], axis=0)
    order = (chip ^ jnp.array(RELATIONS, jnp.int32)).astype(jnp.int32)
    win_sum, wout_sum, small_recv = _grad_reduce(order, h_t, wins, gw_out, small)
    shift = jnp.array(WIN_SHIFT, jnp.int32)[chip]
    g_w_in_t = lax.dynamic_slice_in_dim(win_sum.transpose(1, 0, 2).reshape(WIN, D_MODEL), shift, IN_COLS, axis=0)
    g_w_out = wout_sum.reshape(OUT_ROWS, D_MODEL)

    d_w_in, nm_w_in, nv_w_in = (a.T for a in _adamw(w_in_t, g_w_in_t, m_w_in_t, v_w_in_t, "adamw_w_in"))
    g_w_in = g_w_in_t.T
    d_w_out, nm_w_out, nv_w_out = _adamw(w_out[0], g_w_out, m_w_out[0], v_w_out[0], "adamw_w_out")
    g_s, d_s, nm_s, nv_s, loss_row = _small_adamw(
        small_recv,
        _pack_small(norm_gain, q_norm_a, k_norm_a, q_norm_b, k_norm_b, sinks_a),
        _pack_small(m_norm_gain, m_q_norm_a, m_k_norm_a, m_q_norm_b, m_k_norm_b, m_sinks_a),
        _pack_small(v_norm_gain, v_q_norm_a, v_k_norm_a, v_q_norm_b, v_k_norm_b, v_sinks_a))
    loss = loss_row[0, 0]

    def leaves(small_packed, big_in, big_out):
        gain, qa, ka, qb, kb, sk = _unpack_small(small_packed)
        return (gain, big_in[None], qa, ka, sk, qb, kb, big_out[None])

    return ((loss, gx[None]) + leaves(g_s, g_w_in, g_w_out) + leaves(d_s, d_w_in, d_w_out)
            + leaves(nm_s, nm_w_in, nm_w_out) + leaves(nv_s, nv_w_in, nv_w_out))
```

```python
import jax
import jax.numpy as jnp
from jax import lax
from jax.experimental import pallas as pl
from jax.experimental.pallas import tpu as pltpu

F32 = jnp.float32
BF16 = jnp.bfloat16

D_MODEL = 1024
HEAD_DIM = 64
HEADS = 8
WIDTH = HEADS * HEAD_DIM
A_KV_WIDTH = 2 * HEAD_DIM
BLOCK = 128
LANES = 128
FOLD = 16
A_MAX_DIST = 127
B_MAX_DIST = 128
ROPE_THETA = 10000.0
ROPE_SPLIT = 64
EPS = 1e-6
NEG = -1e30
SCALE = HEAD_DIM ** -0.5

IN_WIDTH = 3328
C_QA, C_KA, C_VA, C_GA, C_QB, C_KB, C_VB, C_GB, C_END = 0, 512, 640, 768, 1280, 1792, 2304, 2816, 3328

N_DEV = 8
N_CHIP = 4
MESH = pl.DeviceIdType.MESH
IN_COLS = IN_WIDTH // N_CHIP
WIN = 896
WIN_START = (0, 768, 1664, 2432)
WIN_SHIFT = (0, 64, 0, 64)
OUT_ROWS = D_MODEL // N_CHIP
RELATIONS = (3, 1, 2, 0)

ADAM_LR = 0.001
ADAM_B1 = 0.9
ADAM_B2 = 0.999
ADAM_EPS = 1e-08
ADAM_WD = 0.01
ADAM_STEP = 10

ROW_TILE = 256
FOLD_ROWS = ROW_TILE // FOLD
GRAD_ROWS = 1024
ACC_COLS = 256
VMEM_LIMIT = 56 * 1024 * 1024


def _dot(a, b):
    return jnp.dot(a, b, preferred_element_type=F32)


def _dot_nt(a, b):
    return lax.dot_general(a, b, (((1,), (1,)), ((), ())), preferred_element_type=F32)


def _dot_tn(a, b):
    return lax.dot_general(a, b, (((0,), (0,)), ((), ())), preferred_element_type=F32)


def _head_sum(z, bd):
    w = bd.shape[0]
    zb = z.astype(BF16)
    parts = [_dot(zb[:, a:a + w], bd) for a in range(0, z.shape[1], w)]
    return parts[0] if len(parts) == 1 else jnp.concatenate(parts, axis=1)


def _swap_halves(t):
    w = t.shape[1]
    lane = lax.broadcasted_iota(jnp.int32, t.shape, 1)
    return jnp.where(lane % HEAD_DIM < HEAD_DIM // 2, pltpu.roll(t, w - 32, 1), pltpu.roll(t, 32, 1))


def _qknorm_rope(t, g, cos, sin_s, bd):
    r = lax.rsqrt(_head_sum(t * t, bd) * (1.0 / HEAD_DIM) + EPS)
    n = (t * r) * g
    return n * cos + _swap_halves(n) * sin_s


def _qknorm_rope_bwd(dout, t, g, cos, sin_s, bd):
    dout, t = dout.astype(F32), t.astype(F32)
    dn = dout * cos + _swap_halves(dout * sin_s)
    r = lax.rsqrt(_head_sum(t * t, bd) * (1.0 / HEAD_DIM) + EPS)
    tr = t * r
    u = dn * g
    dt = r * (u - tr * (_head_sum(u * tr, bd) * (1.0 / HEAD_DIM)))
    return dt, dn * tr


def _sigmoid(g):
    return 1.0 / (1.0 + jnp.exp(-g))


def _expand_heads(st):
    t = st.shape[0]
    lane = lax.broadcasted_iota(jnp.int32, (t, LANES), 1)
    chunks = []
    for c in range(WIDTH // LANES):
        chunks.append(jnp.where(lane < HEAD_DIM, st[:, 2 * c:2 * c + 1], st[:, 2 * c + 1:2 * c + 2]))
    return jnp.concatenate(chunks, axis=1)


def _reduce_heads(z):
    t = z.shape[0]
    lane = lax.broadcasted_iota(jnp.int32, (t, LANES), 1)
    out = jnp.zeros((t, LANES), F32)
    for c in range(WIDTH // LANES):
        zc = z[:, c * LANES:(c + 1) * LANES]
        for ph in range(2):
            s = jnp.sum(jnp.where((lane // HEAD_DIM) == ph, zc, 0.0), axis=-1, keepdims=True)
            out = jnp.where(lane == 2 * c + ph, s, out)
    return out


def _fold_scratch(w):
    return pltpu.VMEM((w // LANES, ROW_TILE, LANES), F32)


def _store_folded(out_ref, val, scr, col0=0):
    w = val.shape[1]
    n = w // LANES
    for c in range(n):
        scr[c] = val[:, c * LANES:(c + 1) * LANES]
    for r in range(FOLD):
        piece = [scr[c, pl.ds(r, FOLD_ROWS, stride=FOLD), :] for c in range(n)]
        out_ref[r, :, col0:col0 + w] = (piece[0] if n == 1 else jnp.concatenate(piece, axis=1)).astype(out_ref.dtype)


def _load_folded(in_ref, scr):
    n = in_ref.shape[2] // LANES
    for r in range(FOLD):
        blk = in_ref[r].astype(F32)
        for c in range(n):
            scr[c, pl.ds(r, FOLD_ROWS, stride=FOLD), :] = blk[:, c * LANES:(c + 1) * LANES]
    return scr[0] if n == 1 else jnp.concatenate([scr[c] for c in range(n)], axis=1)


def _fold_matrix():
    f = jnp.arange(ROW_TILE)
    return (jnp.arange(ROW_TILE)[None, :] == (FOLD * (f % FOLD_ROWS) + f // FOLD_ROWS)[:, None]).astype(BF16)


def _store_folded_bf16(out_ref, val, perm):
    folded = _dot(perm, val.astype(BF16)).astype(out_ref.dtype)
    for r in range(FOLD):
        out_ref[r] = folded[r * FOLD_ROWS:(r + 1) * FOLD_ROWS]


def _load_folded_bf16(in_ref, perm):
    blk = jnp.concatenate([in_ref[r] for r in range(FOLD)], axis=0)
    return _dot(perm, blk)


def _rows(w, tm=ROW_TILE):
    return pl.BlockSpec((tm, w), lambda i: (i, 0))


def _folded_rows(w):
    return pl.BlockSpec((FOLD, FOLD_ROWS, w), lambda i: (0, i, 0))


def _whole(shape):
    return pl.BlockSpec(shape, lambda i: (0,) * len(shape))


def _inproj(x2, gain, w_bf, cos, sin_s, gqa, gka, gqb, gkb, bd256, bd128):
    s = x2.shape[0]
    tm = ROW_TILE

    def body(x_ref, gain_ref, w_hbm, cos_ref, sin_ref, gqa_ref, gka_ref, gqb_ref, gkb_ref, bd256_ref, bd128_ref,
             qa_ref, kva_ref, qb_ref, kvb_ref, qbf_ref, kvbf_ref,
             qa_raw_ref, ka_raw_ref, ga_ref, qb_raw_ref, kb_raw_ref, gb_ref, w_vmem, scr):
        @pl.when(pl.program_id(0) == 0)
        def _():
            pltpu.sync_copy(w_hbm, w_vmem)

        xt = x_ref[...]
        r = lax.rsqrt(jnp.mean(xt * xt, axis=-1, keepdims=True) + EPS)
        h = ((xt * r) * gain_ref[...]).astype(BF16)
        cos1 = cos_ref[...]
        sin1 = sin_ref[...]
        cos4 = jnp.tile(cos1, (1, 4))
        sin4 = jnp.tile(sin1, (1, 4))

        def seg(a, b):
            return _dot_nt(h, w_vmem[a:b, :])

        t = seg(C_QA, C_KA)
        qa_raw_ref[...] = t.astype(BF16)
        qa_ref[...] = (_qknorm_rope(t, gqa_ref[...], cos4, sin4, bd256_ref[...]) * SCALE).astype(BF16)
        t = seg(C_KA, C_VA)
        ka_raw_ref[...] = t.astype(BF16)
        kva_ref[:, :A_KV_WIDTH] = _qknorm_rope(t, gka_ref[...], cos1, sin1, bd128_ref[...]).astype(BF16)
        kva_ref[:, A_KV_WIDTH:] = seg(C_VA, C_GA).astype(BF16)
        ga_ref[...] = seg(C_GA, C_QB).astype(BF16)
        t = seg(C_QB, C_KB)
        qb_raw_ref[...] = t.astype(BF16)
        t = _qknorm_rope(t, gqb_ref[...], cos4, sin4, bd256_ref[...]) * SCALE
        qb_ref[...] = t.astype(BF16)
        _store_folded(qbf_ref, t, scr)
        t = seg(C_KB, C_VB)
        kb_raw_ref[...] = t.astype(BF16)
        t = _qknorm_rope(t, gkb_ref[...], cos4, sin4, bd256_ref[...])
        kvb_ref[:, :WIDTH] = t.astype(BF16)
        _store_folded(kvbf_ref, t, scr)
        t = seg(C_VB, C_GB)
        kvb_ref[:, WIDTH:] = t.astype(BF16)
        _store_folded(kvbf_ref, t, scr, WIDTH)
        gb_ref[...] = seg(C_GB, C_END).astype(BF16)

    sds = jax.ShapeDtypeStruct
    ln = s // FOLD
    out_shape = (sds((s, WIDTH), BF16), sds((s, 2 * A_KV_WIDTH), BF16), sds((s, WIDTH), BF16),
                 sds((s, 2 * WIDTH), BF16), sds((FOLD, ln, WIDTH), BF16), sds((FOLD, ln, 2 * WIDTH), BF16),
                 sds((s, WIDTH), BF16), sds((s, A_KV_WIDTH), BF16), sds((s, WIDTH), BF16),
                 sds((s, WIDTH), BF16), sds((s, WIDTH), BF16), sds((s, WIDTH), BF16))
    out_specs = (_rows(WIDTH), _rows(2 * A_KV_WIDTH), _rows(WIDTH), _rows(2 * WIDTH),
                 _folded_rows(WIDTH), _folded_rows(2 * WIDTH),
                 _rows(WIDTH), _rows(A_KV_WIDTH), _rows(WIDTH), _rows(WIDTH), _rows(WIDTH), _rows(WIDTH))
    return pl.pallas_call(
        body, name="inproj_fwd", grid=(s // tm,),
        in_specs=[_rows(D_MODEL), _whole(gain.shape), pl.BlockSpec(memory_space=pl.ANY), _rows(LANES), _rows(LANES),
                  _whole(gqa.shape), _whole(gka.shape), _whole(gqb.shape), _whole(gkb.shape), _whole(bd256.shape),
                  _whole(bd128.shape)],
        out_specs=out_specs, out_shape=out_shape,
        scratch_shapes=[pltpu.VMEM((IN_WIDTH, D_MODEL), BF16), _fold_scratch(WIDTH)],
        compiler_params=pltpu.CompilerParams(dimension_semantics=("arbitrary",), vmem_limit_bytes=VMEM_LIMIT),
    )(x2, gain, w_bf, cos, sin_s, gqa, gka, gqb, gkb, bd256, bd128)


def _seq_pos(idx, dil):
    if dil == 4:
        return 4 * (idx % 32) + idx // 32
    return idx


SOFTMAX_ROWS = 64


def _upper_mask(dil, r0=0, rows=2 * BLOCK):
    qi = (lax.broadcasted_iota(jnp.int32, (rows, BLOCK), 0) + r0) % BLOCK
    kj = lax.broadcasted_iota(jnp.int32, (rows, BLOCK), 1)
    return _seq_pos(kj, dil) > _seq_pos(qi, dil)


def _eye_mask(r0=0, rows=2 * BLOCK):
    qi = (lax.broadcasted_iota(jnp.int32, (rows, BLOCK), 0) + r0) % BLOCK
    kj = lax.broadcasted_iota(jnp.int32, (rows, BLOCK), 1)
    return qi == kj


def _stack_heads(a2, c, gqa):
    lane = lax.broadcasted_iota(jnp.int32, (1, LANES), 1) // HEAD_DIM
    zero = jnp.zeros_like(a2)
    if gqa:
        keep = lane == (c // 2)
        return jnp.concatenate([jnp.where(keep, a2, zero), jnp.where(keep, _swap_heads(a2), zero)], axis=0)
    return jnp.concatenate([jnp.where(lane == 0, a2, zero), jnp.where(lane == 1, a2, zero)], axis=0)


def _unstack_heads(a, c, gqa):
    lane = lax.broadcasted_iota(jnp.int32, (1, LANES), 1) // HEAD_DIM
    if gqa:
        return jnp.where(lane == (c // 2), a[:BLOCK], _swap_heads(a[BLOCK:]))
    return jnp.where(lane == 0, a[:BLOCK], a[BLOCK:])


def _stacked_head_ids(c, gqa):
    if gqa:
        return 2 * c + c // 2, 2 * c + 1 - c // 2
    return 2 * c, 2 * c + 1


def _per_head_rows(blk, heads):
    return jnp.concatenate([blk[:, heads[0]:heads[0] + 1], blk[:, heads[1]:heads[1] + 1]], axis=0)


def _attn_view(a, dil):
    if dil == 1:
        return a[None]
    if dil == 4:
        return a.reshape(4, 4, a.shape[1], a.shape[2])
    return a


def _attn_unview(a, dil):
    if dil == 1:
        return a[0]
    if dil == 4:
        return a.reshape(FOLD, a.shape[2], a.shape[3])
    return a


ATTN_BLOCKS_PER_STEP = 8


def _attn_specs(dil):
    if dil == 4:
        def spec(n, fn):
            return lambda w: pl.BlockSpec((4, None, n * BLOCK // 4, w), lambda r, i: (0, r, fn(i), 0))
    else:
        def spec(n, fn):
            return lambda w: pl.BlockSpec((None, n * BLOCK, w), lambda r, i: (r, fn(i), 0))
    return spec


def _blk_rows(g, dil):
    n = BLOCK // 4 if dil == 4 else BLOCK
    if isinstance(g, int):
        return slice(g * n, (g + 1) * n)
    return pl.ds(pl.multiple_of(g * n, n), n)


def _blk_load(ref, sl, dil, g=0):
    if dil == 4:
        return ref[:, _blk_rows(g, dil), sl].reshape(BLOCK, sl.stop - sl.start)
    return ref[_blk_rows(g, dil), sl]


def _blk_store(ref, sl, val, dil, g=0):
    val = val.astype(ref.dtype)
    if dil == 4:
        ref[:, _blk_rows(g, dil), sl] = val.reshape(4, BLOCK // 4, sl.stop - sl.start)
    else:
        ref[_blk_rows(g, dil), sl] = val


def _swap_heads(a):
    return pltpu.roll(a.astype(F32), HEAD_DIM, 1).astype(a.dtype)


STAT_SHIFT = 8


def _attn_fwd(q, kv, sinks, *, dil, max_dist, name):
    q, kv = _attn_view(q, dil), _attn_view(kv, dil)
    kw = kv.shape[-1] // 2
    gqa = kw == A_KV_WIDTH
    n_seq = dil
    nb = (q.shape[-2] * (4 if dil == 4 else 1)) // BLOCK
    per_step = min(ATTN_BLOCKS_PER_STEP, nb)
    with_sinks = sinks is not None
    all_lanes = slice(0, LANES)
    assert max_dist in (BLOCK - 1, BLOCK) and nb % per_step == 0
    diag = max_dist == BLOCK

    def body(*refs):
        if with_sinks:
            q_ref, kvp_ref, kvc_ref, sink_ref, o_ref, ml_ref = refs
        else:
            q_ref, kvp_ref, kvc_ref, o_ref, ml_ref = refs

        def block(g, has_prev):
            prev_ref, prev_g = (kvp_ref, 0) if isinstance(g, int) else (kvc_ref, g - 1)
            lane = lax.broadcasted_iota(jnp.int32, (1, LANES), 1)
            with_diag = diag and has_prev
            upper, eye = _upper_mask(dil), _eye_mask()
            first_rows = lax.broadcasted_iota(jnp.int32, (2 * BLOCK, 1), 0) < BLOCK
            ml_blk = jnp.zeros((BLOCK, LANES), F32)
            chunks = range(WIDTH // LANES)
            scores, values = [], []
            for c in chunks:
                sl = slice(c * LANES, (c + 1) * LANES)
                ksl = slice(0, LANES) if gqa else sl
                vsl = slice(ksl.start + kw, ksl.stop + kw)
                kcur, vcur = _blk_load(kvc_ref, ksl, dil, g), _blk_load(kvc_ref, vsl, dil, g)
                qs = _stack_heads(_blk_load(q_ref, sl, dil, g), c, gqa)
                if has_prev:
                    kcur = jnp.concatenate([_blk_load(prev_ref, ksl, dil, prev_g), kcur], axis=0)
                    vcur = jnp.concatenate([_blk_load(prev_ref, vsl, dil, prev_g), vcur], axis=0)
                scores.append(_dot_nt(qs, kcur))
                values.append(vcur)
            probs = []
            for c in chunks:
                heads = _stacked_head_ids(c, gqa)
                s = scores[c]
                if has_prev:
                    s_p = s[:, :BLOCK]
                    sc = jnp.where(upper, s_p, s[:, BLOCK:])
                else:
                    sc = jnp.where(upper, NEG, s)
                if with_diag:
                    sd = jnp.where(eye, s_p, NEG)
                    m = jnp.max(jnp.maximum(sc, sd), axis=-1, keepdims=True)
                else:
                    m = jnp.max(sc, axis=-1, keepdims=True)
                if with_sinks:
                    sk = jnp.where(first_rows, sink_ref[0, heads[0]], sink_ref[0, heads[1]])
                    m = jnp.maximum(m, sk)
                p = jnp.exp(sc - m)
                zero = jnp.zeros_like(p)
                if with_diag:
                    pd = jnp.exp(sd - m)
                    l = jnp.sum(p + pd, axis=-1, keepdims=True)
                else:
                    pd = zero
                    l = jnp.sum(p, axis=-1, keepdims=True)
                if with_sinks:
                    l = l + jnp.exp(sk - m)
                pf = jnp.where(upper, zero, p)
                if has_prev:
                    pf = jnp.concatenate([jnp.where(upper, p, pd), pf], axis=1)
                probs.append(pf.astype(BF16))
                for n, h in enumerate(heads):
                    rows = slice(n * BLOCK, (n + 1) * BLOCK)
                    ml_blk = jnp.where(lane == h, m[rows], ml_blk)
                    ml_blk = jnp.where(lane == h + STAT_SHIFT, l[rows], ml_blk)
            for c in chunks:
                sl = slice(c * LANES, (c + 1) * LANES)
                _blk_store(o_ref, sl, _unstack_heads(_dot(probs[c], values[c]), c, gqa), dil, g)
            _blk_store(ml_ref, all_lanes, ml_blk, dil, g)

        @pl.when(pl.program_id(1) == 0)
        def _():
            block(0, False)

        @pl.when(pl.program_id(1) > 0)
        def _():
            block(0, True)

        if per_step > 1:
            def rest(g, carry):
                block(g, True)
                return carry

            lax.fori_loop(1, per_step, rest, 0)

    spec = _attn_specs(dil)
    cur = spec(per_step, lambda i: i)
    prev = spec(1, lambda i: jnp.maximum(i * per_step - 1, 0))
    in_specs = [cur(WIDTH), prev(2 * kw), cur(2 * kw)]
    args = [q, kv, kv]
    if with_sinks:
        in_specs.append(pl.BlockSpec(memory_space=pltpu.SMEM))
        args.append(sinks)
    stats = jax.ShapeDtypeStruct(q.shape[:-1] + (LANES,), F32)
    o, ml = pl.pallas_call(
        body, name=name, grid=(n_seq, nb // per_step), in_specs=in_specs,
        out_specs=(cur(WIDTH), cur(LANES)),
        out_shape=(jax.ShapeDtypeStruct(q.shape, BF16), stats),
        compiler_params=pltpu.CompilerParams(dimension_semantics=("arbitrary", "arbitrary"),
                                             vmem_limit_bytes=VMEM_LIMIT),
    )(*args)
    return _attn_unview(o, dil), _attn_unview(ml, dil)


def _attn_bwd(q, kv, do, ld, *, dil, max_dist, name):
    q, kv, do, ld = (_attn_view(a, dil) for a in (q, kv, do, ld))
    kw = kv.shape[-1] // 2
    gqa = kw == A_KV_WIDTH
    n_seq = dil
    nb = (q.shape[-2] * (4 if dil == 4 else 1)) // BLOCK
    n_kc = kw // LANES
    per_step = min(ATTN_BLOCKS_PER_STEP, nb)
    all_lanes = slice(0, LANES)
    assert max_dist in (BLOCK - 1, BLOCK) and nb % per_step == 0
    diag = max_dist == BLOCK

    def body(q_ref, kvp_ref, kvc_ref, do_ref, ld_ref, dq_ref, dkv_ref, ck_ref, cv_ref):
        i = pl.program_id(1)

        def block(g, has_prev):
            prev_ref, prev_g = (kvp_ref, 0) if isinstance(g, int) else (kvc_ref, g - 1)
            seq_blk = i * per_step + g
            upper, eye = _upper_mask(dil), _eye_mask()
            ld_blk = _blk_load(ld_ref, all_lanes, dil, g)
            dk_acc = [None] * n_kc
            dv_acc = [None] * n_kc
            chunks = range(WIDTH // LANES)
            operands, products = [], []
            for c in chunks:
                sl = slice(c * LANES, (c + 1) * LANES)
                kc = 0 if gqa else c
                ksl = slice(kc * LANES, (kc + 1) * LANES)
                vsl = slice(ksl.start + kw, ksl.stop + kw)
                k2, v2 = _blk_load(kvc_ref, ksl, dil, g), _blk_load(kvc_ref, vsl, dil, g)
                if has_prev:
                    k2 = jnp.concatenate([_blk_load(prev_ref, ksl, dil, prev_g), k2], axis=0)
                    v2 = jnp.concatenate([_blk_load(prev_ref, vsl, dil, prev_g), v2], axis=0)
                qs = _stack_heads(_blk_load(q_ref, sl, dil, g), c, gqa)
                dos = _stack_heads(_blk_load(do_ref, sl, dil, g), c, gqa)
                operands.append((qs, dos, k2))
                products.append((_dot_nt(qs, k2), _dot_nt(dos, v2)))
            weights = []
            for c in chunks:
                heads = _stacked_head_ids(c, gqa)
                lse2 = _per_head_rows(ld_blk, heads)
                dl2 = _per_head_rows(ld_blk, tuple(h + STAT_SHIFT for h in heads))
                s, dp = products[c]
                if has_prev:
                    s_p, dp_p = s[:, :BLOCK], dp[:, :BLOCK]
                    sc = jnp.where(upper, s_p, s[:, BLOCK:])
                    dpc = jnp.where(upper, dp_p, dp[:, BLOCK:])
                else:
                    sc = jnp.where(upper, NEG, s)
                    dpc = dp
                p = jnp.exp(sc - lse2)
                ds = p * (dpc - dl2)
                zero = jnp.zeros_like(p)
                pf = jnp.where(upper, zero, p)
                dsf = jnp.where(upper, zero, ds)
                if has_prev:
                    if diag:
                        pd = jnp.exp(jnp.where(eye, s_p, NEG) - lse2)
                        dsd = pd * (dp_p - dl2)
                    else:
                        pd = dsd = zero
                    pf = jnp.concatenate([jnp.where(upper, p, pd), pf], axis=1)
                    dsf = jnp.concatenate([jnp.where(upper, ds, dsd), dsf], axis=1)
                weights.append((pf.astype(BF16), dsf.astype(BF16)))
            for c in chunks:
                sl = slice(c * LANES, (c + 1) * LANES)
                kc = 0 if gqa else c
                qs, dos, k2 = operands[c]
                pf, dsf = weights[c]
                _blk_store(dq_ref, sl, _unstack_heads(_dot(dsf, k2), c, gqa) * SCALE, dil, g)
                dk2 = _dot_tn(dsf, qs)
                dv2 = _dot_tn(pf, dos)
                dk_acc[kc] = dk2 if dk_acc[kc] is None else dk_acc[kc] + dk2
                dv_acc[kc] = dv2 if dv_acc[kc] is None else dv_acc[kc] + dv2
            for kc in range(n_kc):
                sl = slice(kc * LANES, (kc + 1) * LANES)
                vsl = slice(sl.start + kw, sl.stop + kw)
                if has_prev:
                    _blk_store(dkv_ref, sl, ck_ref[:, sl] + dk_acc[kc][:BLOCK], dil, seq_blk - 1)
                    _blk_store(dkv_ref, vsl, cv_ref[:, sl] + dv_acc[kc][:BLOCK], dil, seq_blk - 1)
                    ck_ref[:, sl] = dk_acc[kc][BLOCK:]
                    cv_ref[:, sl] = dv_acc[kc][BLOCK:]
                else:
                    ck_ref[:, sl] = dk_acc[kc]
                    cv_ref[:, sl] = dv_acc[kc]

        @pl.when(i == 0)
        def _():
            block(0, False)

        @pl.when(i > 0)
        def _():
            block(0, True)

        if per_step > 1:
            def rest(g, carry):
                block(g, True)
                return carry

            lax.fori_loop(1, per_step, rest, 0)

        @pl.when(i == nb // per_step - 1)
        def _():
            for kc in range(n_kc):
                sl = slice(kc * LANES, (kc + 1) * LANES)
                _blk_store(dkv_ref, sl, ck_ref[:, sl], dil, nb - 1)
                _blk_store(dkv_ref, slice(sl.start + kw, sl.stop + kw), cv_ref[:, sl], dil, nb - 1)

    spec = _attn_specs(dil)
    cur = spec(per_step, lambda i: i)
    prev = spec(1, lambda i: jnp.maximum(i * per_step - 1, 0))
    if dil == 4:
        whole = pl.BlockSpec((4, None, kv.shape[2], 2 * kw), lambda r, i: (0, r, 0, 0))
    else:
        whole = pl.BlockSpec((None, kv.shape[1], 2 * kw), lambda r, i: (r, 0, 0))
    sds = jax.ShapeDtypeStruct
    dq, dkv = pl.pallas_call(
        body, name=name, grid=(n_seq, nb // per_step),
        in_specs=[cur(WIDTH), prev(2 * kw), cur(2 * kw), cur(WIDTH), cur(LANES)],
        out_specs=(cur(WIDTH), whole),
        out_shape=(sds(q.shape, BF16), sds(kv.shape, BF16)),
        scratch_shapes=[pltpu.VMEM((BLOCK, kw), F32), pltpu.VMEM((BLOCK, kw), F32)],
        compiler_params=pltpu.CompilerParams(dimension_semantics=("arbitrary", "arbitrary"),
                                             vmem_limit_bytes=VMEM_LIMIT),
    )(q, kv, kv, do, ld)
    return _attn_unview(dq, dil), _attn_unview(dkv, dil)


def _outproj(att_a, att_b1, att_b4, att_b16, g_a, g_b, x2, tgt2, w_out_bf, sink_row, perm):
    s = x2.shape[0]
    tm = ROW_TILE

    def body(oa_ref, mla_ref, ob1_ref, ml1_ref, ob4_ref, ml4_ref, ob16_ref, ml16_ref,
             ga_ref, gb_ref, x_ref, t_ref, w_ref, sink_ref, perm_ref,
             dy_ref, doa_ref, dob_ref, dobf_ref, dga_ref, dgb_ref, lda_ref, ldb_ref, ldbf_ref,
             gw_ref, loss_ref, dsink_ref, scr_st):
        i = pl.program_id(0)
        perm = perm_ref[...]
        lane = lax.broadcasted_iota(jnp.int32, (tm, LANES), 1)
        used = lane < HEADS

        def split(ml):
            return jnp.where(used, ml, 0.0), jnp.where(used, pltpu.roll(ml, LANES - STAT_SHIFT, 1), 1.0)

        @pl.when(i == 0)
        def _():
            gw_ref[...] = jnp.zeros_like(gw_ref)
            loss_ref[...] = jnp.zeros_like(loss_ref)
            dsink_ref[...] = jnp.zeros_like(dsink_ref)

        ms, ls = zip(split(ml1_ref[...]), split(_load_folded(ml4_ref, scr_st)), split(_load_folded(ml16_ref, scr_st)))
        mx = jnp.maximum(jnp.maximum(ms[0], ms[1]), ms[2])
        scale = [jnp.exp(mp - mx) for mp in ms]
        den = (ls[0] * scale[0] + ls[1] * scale[1]) + ls[2] * scale[2]
        lse_b = jnp.where(used, mx + jnp.log(den), 0.0)
        inv_den = 1.0 / den
        o_b = _expand_heads(scale[0] * inv_den) * ob1_ref[...].astype(F32)
        o_b = o_b + _expand_heads(scale[1] * inv_den) * _load_folded_bf16(ob4_ref, perm)
        o_b = o_b + _expand_heads(scale[2] * inv_den) * _load_folded_bf16(ob16_ref, perm)
        m_a, l_a = split(mla_ref[...])
        lse_a = jnp.where(used, m_a + jnp.log(l_a), 0.0)
        o_a = _expand_heads(1.0 / l_a) * oa_ref[...].astype(F32)
        g_a = ga_ref[...].astype(F32)
        g_b = gb_ref[...].astype(F32)
        sg_a = _sigmoid(g_a)
        sg_b = _sigmoid(g_b)
        silu_a = g_a * sg_a
        silu_b = g_b * sg_b
        mixed = jnp.concatenate([o_a * silu_a, o_b * silu_b], axis=1).astype(BF16)
        w = w_ref[...]
        y = x_ref[...] + _dot(mixed, w)
        diff = y - t_ref[...]
        loss_ref[...] += (0.5 / D_MODEL) * jnp.sum(diff * diff)
        dy = diff * (1.0 / D_MODEL)
        dy_ref[...] = dy
        dyb = dy.astype(BF16)
        gw_ref[...] += _dot_tn(mixed, dyb)
        dmixed = _dot_nt(dyb, w)
        dm_a = dmixed[:, :WIDTH]
        dm_b = dmixed[:, WIDTH:]
        do_a = dm_a * silu_a
        do_b = dm_b * silu_b
        doa_ref[...] = do_a.astype(BF16)
        dob_ref[...] = do_b.astype(BF16)
        _store_folded_bf16(dobf_ref, do_b, perm)
        dga_ref[...] = (dm_a * o_a * (sg_a * (1.0 + g_a * (1.0 - sg_a)))).astype(BF16)
        dgb_ref[...] = (dm_b * o_b * (sg_b * (1.0 + g_b * (1.0 - sg_b)))).astype(BF16)
        dl_a = _reduce_heads(do_a * o_a)
        dl_b = _reduce_heads(do_b * o_b)
        lda_ref[...] = lse_a + pltpu.roll(dl_a, STAT_SHIFT, 1)
        ld_b = lse_b + pltpu.roll(dl_b, STAT_SHIFT, 1)
        ldb_ref[...] = ld_b
        _store_folded(ldbf_ref, ld_b, scr_st)
        dsink_ref[...] -= jnp.sum(jnp.exp(sink_ref[...] - lse_a) * dl_a, axis=0, keepdims=True)

    sds = jax.ShapeDtypeStruct
    ln = s // FOLD
    natural = [_rows(WIDTH), _rows(LANES)]
    folded = [_folded_rows(WIDTH), _folded_rows(LANES)]
    return pl.pallas_call(
        body, name="outproj_fwd_bwd", grid=(s // tm,),
        in_specs=natural + natural + folded + folded
                 + [_rows(WIDTH), _rows(WIDTH), _rows(D_MODEL), _rows(D_MODEL), _whole((D_MODEL, D_MODEL)),
                    _whole((1, LANES)), _whole(perm.shape)],
        out_specs=(_rows(D_MODEL), _rows(WIDTH), _rows(WIDTH), _folded_rows(WIDTH), _rows(WIDTH), _rows(WIDTH),
                   _rows(LANES), _rows(LANES), _folded_rows(LANES),
                   _whole((D_MODEL, D_MODEL)), _whole((1, LANES)), _whole((1, LANES))),
        out_shape=(sds((s, D_MODEL), F32), sds((s, WIDTH), BF16), sds((s, WIDTH), BF16),
                   sds((FOLD, ln, WIDTH), BF16), sds((s, WIDTH), BF16), sds((s, WIDTH), BF16),
                   sds((s, LANES), F32), sds((s, LANES), F32), sds((FOLD, ln, LANES), F32),
                   sds((D_MODEL, D_MODEL), F32), sds((1, LANES), F32), sds((1, LANES), F32)),
        scratch_shapes=[_fold_scratch(LANES)],
        compiler_params=pltpu.CompilerParams(dimension_semantics=("arbitrary",), vmem_limit_bytes=VMEM_LIMIT),
    )(*att_a, *att_b1, *att_b4, *att_b16, g_a, g_b, x2, tgt2, w_out_bf, sink_row, perm)


def _inproj_bwd(x2, dy, gain, w_bf, cos, sin_s, gqa, gka, gqb, gkb, bd256, bd128, perm,
                qa_raw, ka_raw, qb_raw, kb_raw, d_a, d_b1, d_b4, d_b16, dg_a, dg_b):
    s = x2.shape[0]
    tm = ROW_TILE

    def body(x_ref, dy_ref, gain_ref, w_hbm, cos_ref, sin_ref, gqa_ref, gka_ref, gqb_ref, gkb_ref, bd256_ref,
             bd128_ref, perm_ref, qa_raw_ref, ka_raw_ref, qb_raw_ref, kb_raw_ref, dqa_ref, dkva_ref,
             dq1_ref, dkv1_ref, dq4_ref, dkv4_ref, dq16_ref, dkv16_ref, dga_ref, dgb_ref,
             gx_ref, ht_ref, win_ref,
             dgain_ref, dgqa_ref, dgka_ref, dgqb_ref, dgkb_ref, w_vmem, dproj_ref):
        i = pl.program_id(0)
        perm = perm_ref[...]

        @pl.when(i == 0)
        def _():
            pltpu.sync_copy(w_hbm, w_vmem)
            dgain_ref[...] = jnp.zeros_like(dgain_ref)
            dgqa_ref[...] = jnp.zeros_like(dgqa_ref)
            dgka_ref[...] = jnp.zeros_like(dgka_ref)
            dgqb_ref[...] = jnp.zeros_like(dgqb_ref)
            dgkb_ref[...] = jnp.zeros_like(dgkb_ref)

        cos1 = cos_ref[...]
        sin1 = sin_ref[...]
        cos4 = jnp.tile(cos1, (1, 4))
        sin4 = jnp.tile(sin1, (1, 4))

        dt, dg = _qknorm_rope_bwd(dqa_ref[...], qa_raw_ref[...], gqa_ref[...], cos4, sin4, bd256_ref[...])
        dproj_ref[:, C_QA:C_KA] = dt.astype(BF16)
        dgqa_ref[...] += jnp.sum(dg, axis=0, keepdims=True)
        dt, dg = _qknorm_rope_bwd(dkva_ref[:, :A_KV_WIDTH], ka_raw_ref[...], gka_ref[...], cos1, sin1,
                                  bd128_ref[...])
        dproj_ref[:, C_KA:C_VA] = dt.astype(BF16)
        dgka_ref[...] += jnp.sum(dg, axis=0, keepdims=True)
        dproj_ref[:, C_VA:C_GA] = dkva_ref[:, A_KV_WIDTH:]
        dproj_ref[:, C_GA:C_QB] = dga_ref[...]
        dq = (dq1_ref[...].astype(F32) + _load_folded_bf16(dq4_ref, perm)) + _load_folded_bf16(dq16_ref, perm)
        dt, dg = _qknorm_rope_bwd(dq, qb_raw_ref[...], gqb_ref[...], cos4, sin4, bd256_ref[...])
        dproj_ref[:, C_QB:C_KB] = dt.astype(BF16)
        dgqb_ref[...] += jnp.sum(dg, axis=0, keepdims=True)
        dkv = (dkv1_ref[...].astype(F32) + _load_folded_bf16(dkv4_ref, perm)) + _load_folded_bf16(dkv16_ref, perm)
        dt, dg = _qknorm_rope_bwd(dkv[:, :WIDTH], kb_raw_ref[...], gkb_ref[...], cos4, sin4, bd256_ref[...])
        dproj_ref[:, C_KB:C_VB] = dt.astype(BF16)
        dgkb_ref[...] += jnp.sum(dg, axis=0, keepdims=True)
        dproj_ref[:, C_VB:C_GB] = dkv[:, WIDTH:].astype(BF16)
        dproj_ref[:, C_GB:C_END] = dgb_ref[...]
        for k, start in enumerate(WIN_START):
            win_ref[k] = dproj_ref[:, start:start + WIN]

        xt = x_ref[...]
        gain_row = gain_ref[...]
        r = lax.rsqrt(jnp.mean(xt * xt, axis=-1, keepdims=True) + EPS)
        xr = xt * r
        ht_ref[...] = (xr * gain_row).T.astype(BF16)
        dh = _dot(dproj_ref[...], w_vmem[...])
        dgain_ref[...] += jnp.sum(dh * xr, axis=0, keepdims=True)
        u = dh * gain_row
        gx_ref[...] = dy_ref[...] + r * (u - xr * jnp.mean(u * xr, axis=-1, keepdims=True))

    def acc_row(w):
        return pl.BlockSpec((1, w), lambda i: (0, 0))

    sds = jax.ShapeDtypeStruct
    any_spec = pl.BlockSpec(memory_space=pl.ANY)
    win_spec = pl.BlockSpec((N_CHIP, tm, WIN), lambda i: (0, i, 0))
    return pl.pallas_call(
        body, name="inproj_bwd", grid=(s // tm,),
        in_specs=[_rows(D_MODEL), _rows(D_MODEL), _whole(gain.shape), any_spec, _rows(LANES), _rows(LANES),
                  _whole(gqa.shape), _whole(gka.shape), _whole(gqb.shape), _whole(gkb.shape), _whole(bd256.shape),
                  _whole(bd128.shape), _whole(perm.shape),
                  _rows(WIDTH), _rows(A_KV_WIDTH), _rows(WIDTH), _rows(WIDTH),
                  _rows(WIDTH), _rows(2 * A_KV_WIDTH), _rows(WIDTH), _rows(2 * WIDTH)]
                 + [_folded_rows(WIDTH), _folded_rows(2 * WIDTH)] * 2 + [_rows(WIDTH), _rows(WIDTH)],
        out_specs=(_rows(D_MODEL), pl.BlockSpec((D_MODEL, tm), lambda i: (0, i)), win_spec, acc_row(D_MODEL), acc_row(WIDTH), acc_row(A_KV_WIDTH), acc_row(WIDTH), acc_row(WIDTH)),
        out_shape=(sds((s, D_MODEL), F32), sds((D_MODEL, s), BF16), sds((N_CHIP, s, WIN), BF16),
                   sds((1, D_MODEL), F32),
                   sds((1, WIDTH), F32), sds((1, A_KV_WIDTH), F32), sds((1, WIDTH), F32), sds((1, WIDTH), F32)),
        scratch_shapes=[pltpu.VMEM((IN_WIDTH, D_MODEL), BF16), pltpu.VMEM((tm, IN_WIDTH), BF16)],
        compiler_params=pltpu.CompilerParams(dimension_semantics=("arbitrary",), vmem_limit_bytes=VMEM_LIMIT),
    )(x2, dy, gain, w_bf, cos, sin_s, gqa, gka, gqb, gkb, bd256, bd128, perm, qa_raw, ka_raw, qb_raw, kb_raw,
      *d_a, *d_b1, *d_b4, *d_b16, dg_a, dg_b)


def _rope_tables(s):
    half = HEAD_DIM // 2
    inv = jnp.tile(ROPE_THETA ** (-jnp.arange(half, dtype=F32) / half), 4)
    sign = jnp.tile(jnp.concatenate([-jnp.ones((half,), F32), jnp.ones((half,), F32)]), 2)
    hi = (jnp.arange(s // ROPE_SPLIT) * ROPE_SPLIT).astype(F32)[:, None] * inv[None, :]
    lo = jnp.arange(ROPE_SPLIT).astype(F32)[:, None] * inv[None, :]
    ch, sh, cl, sl = jnp.cos(hi)[:, None, :], jnp.sin(hi)[:, None, :], jnp.cos(lo)[None], jnp.sin(lo)[None]
    cos = (ch * cl - sh * sl).reshape(s, LANES)
    sin = (sh * cl + ch * sl).reshape(s, LANES)
    return cos, sin * sign[None, :]


def _block_diag_ones(w):
    idx = jnp.arange(w) // HEAD_DIM
    return (idx[:, None] == idx[None, :]).astype(BF16)


def _local_step(x2, tgt2, norm_gain, w_in_bf, q_norm_a, k_norm_a, sinks_a, q_norm_b, k_norm_b, w_out_bf):
    s = x2.shape[0]
    cos, sin_s = _rope_tables(s)
    bd256, bd128 = _block_diag_ones(2 * LANES), _block_diag_ones(A_KV_WIDTH)
    gqa = jnp.tile(q_norm_a, (1, HEADS))
    gka = jnp.tile(k_norm_a, (1, 2))
    gqb = jnp.tile(q_norm_b, (1, HEADS))
    gkb = jnp.tile(k_norm_b, (1, HEADS))
    sink_row = jnp.pad(sinks_a, ((0, 0), (0, LANES - HEADS)))
    perm = _fold_matrix()

    (qa, kva, qb, kvb, qbf, kvbf, qa_raw, ka_raw, g_a, qb_raw, kb_raw, g_b) = _inproj(
        x2, norm_gain, w_in_bf, cos, sin_s, gqa, gka, gqb, gkb, bd256, bd128)

    att_a = _attn_fwd(qa, kva, sinks_a, dil=1, max_dist=A_MAX_DIST, name="attn_a_fwd")
    att_b1 = _attn_fwd(qb, kvb, None, dil=1, max_dist=B_MAX_DIST, name="attn_b1_fwd")
    att_b4 = _attn_fwd(qbf, kvbf, None, dil=4, max_dist=B_MAX_DIST, name="attn_b4_fwd")
    att_b16 = _attn_fwd(qbf, kvbf, None, dil=16, max_dist=B_MAX_DIST, name="attn_b16_fwd")

    (dy, do_a, do_b, do_bf, dg_a, dg_b, ld_a, ld_b, ld_bf, gw_out, loss_part, dsink) = _outproj(
        att_a, att_b1, att_b4, att_b16, g_a, g_b, x2, tgt2, w_out_bf, sink_row, perm)

    d_a = _attn_bwd(qa, kva, do_a, ld_a, dil=1, max_dist=A_MAX_DIST, name="attn_a_bwd")
    d_b1 = _attn_bwd(qb, kvb, do_b, ld_b, dil=1, max_dist=B_MAX_DIST, name="attn_b1_bwd")
    d_b4 = _attn_bwd(qbf, kvbf, do_bf, ld_bf, dil=4, max_dist=B_MAX_DIST, name="attn_b4_bwd")
    d_b16 = _attn_bwd(qbf, kvbf, do_bf, ld_bf, dil=16, max_dist=B_MAX_DIST, name="attn_b16_bwd")

    gx, h_t, wins, dgain, dgqa, dgka, dgqb, dgkb = _inproj_bwd(
        x2, dy, norm_gain, w_in_bf, cos, sin_s, gqa, gka, gqb, gkb, bd256, bd128, perm,
        qa_raw, ka_raw, qb_raw, kb_raw, d_a, d_b1, d_b4, d_b16, dg_a, dg_b)
    return loss_part, gx, h_t, wins, gw_out, (dgain, dgqa, dgka, dsink, dgqb, dgkb)


def _position():
    return lax.axis_index("x"), lax.axis_index("y"), lax.axis_index("c")


GATHER_CHUNKS = 2


def _gather_weights(blocks, name):
    n = len(blocks)
    ch = GATHER_CHUNKS
    n_sems = n * (N_CHIP - 1) * ch

    def body(*refs):
        src_refs, dst_refs = refs[:n], refs[n:2 * n]
        ici_send, ici_recv, d2d_send, d2d_recv = refs[2 * n:]
        x, y, c = _position()
        b = 2 * x + y
        for k in range(n):
            dst_refs[k][b] = src_refs[k][...].astype(BF16)

        def rows(k, core, j):
            half = blocks[k].shape[0] // 2
            return pl.ds(core * half + j * (half // ch), half // ch)

        plan = []
        for d in range(1, N_CHIP):
            px, py = x ^ (d >> 1), y ^ (d & 1)
            for j in range(ch):
                for k in range(n):
                    plan.append((px, py, 2 * px + py, k, j, ((d - 1) * ch + j) * n + k))
        sends = []
        for px, py, pb, k, j, sem in plan:
            send = pltpu.make_async_remote_copy(
                src_ref=dst_refs[k].at[b, rows(k, c, j)], dst_ref=dst_refs[k].at[b, rows(k, c, j)],
                send_sem=ici_send.at[sem], recv_sem=ici_recv.at[sem], device_id=(px, py, c), device_id_type=MESH)
            send.start()
            sends.append(send)
        for px, py, pb, k, j, sem in plan:
            landed = dst_refs[k].at[pb, rows(k, c, j)]
            pltpu.make_async_remote_copy(
                src_ref=landed, dst_ref=landed, send_sem=ici_send.at[sem], recv_sem=ici_recv.at[sem],
                device_id=(px, py, c), device_id_type=MESH).wait_recv()
            forward = pltpu.make_async_remote_copy(
                src_ref=landed, dst_ref=landed, send_sem=d2d_send.at[sem], recv_sem=d2d_recv.at[sem],
                device_id=(x, y, 1 - c), device_id_type=MESH)
            forward.start()
            sends.append(forward)
        for px, py, pb, k, j, sem in plan:
            passed = dst_refs[k].at[pb, rows(k, 1 - c, j)]
            pltpu.make_async_remote_copy(
                src_ref=passed, dst_ref=passed, send_sem=d2d_send.at[sem], recv_sem=d2d_recv.at[sem],
                device_id=(x, y, 1 - c), device_id_type=MESH).wait_recv()
        for send in sends:
            send.wait_send()

    vmem_spec = pl.BlockSpec(memory_space=pltpu.VMEM)
    out_shape = tuple(jax.ShapeDtypeStruct((N_CHIP,) + a.shape, BF16) for a in blocks)
    return pl.pallas_call(
        body, name=name, in_specs=[vmem_spec] * n, out_specs=tuple([vmem_spec] * n), out_shape=out_shape,
        scratch_shapes=[pltpu.SemaphoreType.DMA((n_sems,)) for _ in range(4)],
        compiler_params=pltpu.CompilerParams(vmem_limit_bytes=VMEM_LIMIT),
    )(*blocks)


def _grad_reduce(order, h_t, wins, gw_out, small):
    s = h_t.shape[1]
    tk = GRAD_ROWS
    n_i = s // tk
    half = D_MODEL // 2
    o_half = OUT_ROWS // 2
    n_rel = N_CHIP - 1

    def body(order_ref, ht_ref, win_ref, gwo_ref, small_ref,
             win_out, wout_out, small_out,
             acc, mine, s1, r1, s2, r2, so1, ro1, so2, ro2, pair_in, pair_o, small_land,
             s1_send, s1_recv, s2_send, s2_recv, o1_send, o1_recv, o2_send, o2_recv,
             pair_send, pair_recv, small_send, small_recv):
        j = pl.program_id(0)
        i = pl.program_id(1)
        x, y, c = _position()
        me = 4 * x + 2 * y + c
        sibling = (x, y, 1 - c)
        my_rows = pl.ds(pl.multiple_of(c * half, half), half)
        sib_rows = pl.ds(pl.multiple_of((1 - c) * half, half), half)

        def chip_of(rel):
            return x ^ (rel >> 1), y ^ (rel & 1)

        def level1(k):
            return pltpu.make_async_remote_copy(src_ref=s1.at[k], dst_ref=r1.at[k], send_sem=s1_send.at[k],
                                                recv_sem=s1_recv.at[k], device_id=sibling, device_id_type=MESH)

        def level2(k):
            px, py = chip_of(RELATIONS[k])
            return pltpu.make_async_remote_copy(src_ref=s2.at[k], dst_ref=r2.at[k], send_sem=s2_send.at[k],
                                                recv_sem=s2_recv.at[k], device_id=(px, py, c), device_id_type=MESH)

        def out_level1(bk):
            return pltpu.make_async_remote_copy(src_ref=so1.at[bk], dst_ref=ro1.at[bk], send_sem=o1_send.at[bk],
                                                recv_sem=o1_recv.at[bk], device_id=sibling, device_id_type=MESH)

        def out_level2(k):
            px, py = chip_of(RELATIONS[k])
            return pltpu.make_async_remote_copy(src_ref=so2.at[k], dst_ref=ro2.at[k], send_sem=o2_send.at[k],
                                                recv_sem=o2_recv.at[k], device_id=(px, py, c), device_id_type=MESH)

        def small_copy(d):
            px, py, pc = x ^ (d >> 2), y ^ ((d >> 1) & 1), c ^ (d & 1)
            return pltpu.make_async_remote_copy(src_ref=small_ref, dst_ref=small_land.at[me],
                                                send_sem=small_send.at[d], recv_sem=small_recv.at[d],
                                                device_id=(px, py, pc), device_id_type=MESH)

        def pair_copy(k, buf):
            return pltpu.make_async_remote_copy(src_ref=buf.at[0], dst_ref=buf.at[1], send_sem=pair_send.at[k],
                                                recv_sem=pair_recv.at[k], device_id=sibling, device_id_type=MESH)

        def out_rows(bk, core):
            return pl.ds(pl.multiple_of(bk * OUT_ROWS + core * o_half, o_half), o_half)

        @pl.when((j == 0) & (i == 0))
        def _():
            for d in range(1, N_DEV):
                small_copy(d).start()
            small_land[me] = small_ref[...]
            for bk in range(N_CHIP):
                so1[bk] = gwo_ref[out_rows(bk, 1 - c), :].astype(BF16)
                out_level1(bk).start()

        @pl.when((j == 0) & (i == 1))
        def _():
            b = 2 * x + y
            for bk in range(N_CHIP):
                out_level1(bk).wait_recv()
            for k in range(n_rel):
                px, py = chip_of(RELATIONS[k])
                bk = 2 * px + py
                so2[k] = (gwo_ref[out_rows(bk, c), :] + ro1[bk].astype(F32)).astype(BF16)
                out_level2(k).start()

        @pl.when(i == 0)
        def _():
            acc[...] = jnp.zeros_like(acc)

        for n0 in range(0, WIN, ACC_COLS):
            n1 = min(n0 + ACC_COLS, WIN)
            acc[:, n0:n1] += _dot(ht_ref[...], win_ref[:, n0:n1])

        for k in range(N_CHIP):
            @pl.when((j == k) & (i == n_i - 1))
            def _(k=k):
                s1[k] = acc[sib_rows, :].astype(BF16)
                level1(k).start()
                mine[...] = acc[my_rows, :]

            if k < n_rel:
                @pl.when((j == k + 1) & (i == 1))
                def _(k=k):
                    level1(k).wait_recv()
                    s2[k] = (mine[...] + r1[k].astype(F32)).astype(BF16)
                    level2(k).start()

        @pl.when((j == N_CHIP - 1) & (i == n_i - 1))
        def _():
            b = 2 * x + y
            level1(N_CHIP - 1).wait_recv()
            total = mine[...] + r1[N_CHIP - 1].astype(F32)
            for k in range(n_rel):
                level2(k).wait_recv()
                total = total + r2[k].astype(F32)
            total = total.T
            pair_in[0] = total
            pair_copy(0, pair_in).start()
            total_o = gwo_ref[out_rows(b, c), :] + ro1[b].astype(F32)
            for k in range(n_rel):
                out_level2(k).wait_recv()
                total_o = total_o + ro2[k].astype(F32)
            pair_o[0] = total_o
            pair_copy(1, pair_o).start()
            for core in range(2):
                @pl.when(c == core)
                def _(core=core):
                    win_out[:, core * half:(core + 1) * half] = total
            wout_out[c] = total_o
            for d in range(1, N_DEV):
                small_copy(d).wait_recv()
            small_out[...] = small_land[...]
            pair_copy(0, pair_in).wait_recv()
            for core in range(2):
                @pl.when(c == core)
                def _(core=core):
                    win_out[:, (1 - core) * half:(2 - core) * half] = pair_in[1]
            pair_copy(1, pair_o).wait_recv()
            wout_out[1 - c] = pair_o[1]
            for d in range(1, N_DEV):
                small_copy(d).wait_send()
            for k in range(N_CHIP):
                level1(k).wait_send()
                out_level1(k).wait_send()
            for k in range(n_rel):
                level2(k).wait_send()
                out_level2(k).wait_send()
            pair_copy(0, pair_in).wait_send()
            pair_copy(1, pair_o).wait_send()

    vmem = pl.BlockSpec(memory_space=pltpu.VMEM)
    dma = pltpu.SemaphoreType.DMA
    sds = jax.ShapeDtypeStruct
    grid_spec = pltpu.PrefetchScalarGridSpec(
        num_scalar_prefetch=1, grid=(N_CHIP, n_i),
        in_specs=[pl.BlockSpec((D_MODEL, tk), lambda j, i, order: (0, i)),
                  pl.BlockSpec((None, tk, WIN), lambda j, i, order: (order[j], i, 0)), vmem, vmem],
        out_specs=(vmem, vmem, vmem),
        scratch_shapes=[
            pltpu.VMEM((D_MODEL, WIN), F32), pltpu.VMEM((half, WIN), F32),
            pltpu.VMEM((N_CHIP, half, WIN), BF16), pltpu.VMEM((N_CHIP, half, WIN), BF16),
            pltpu.VMEM((n_rel, half, WIN), BF16), pltpu.VMEM((n_rel, half, WIN), BF16),
            pltpu.VMEM((N_CHIP, o_half, D_MODEL), BF16), pltpu.VMEM((N_CHIP, o_half, D_MODEL), BF16),
            pltpu.VMEM((n_rel, o_half, D_MODEL), BF16), pltpu.VMEM((n_rel, o_half, D_MODEL), BF16),
            pltpu.VMEM((2, WIN, half), F32), pltpu.VMEM((2, o_half, D_MODEL), F32),
            pltpu.VMEM((N_DEV, PACK_ROWS, D_MODEL), F32),
            dma((N_CHIP,)), dma((N_CHIP,)), dma((n_rel,)), dma((n_rel,)),
            dma((N_CHIP,)), dma((N_CHIP,)), dma((n_rel,)), dma((n_rel,)),
            dma((2,)), dma((2,)), dma((N_DEV,)), dma((N_DEV,))])
    return pl.pallas_call(
        body, name="grad_w_in_reduce", grid_spec=grid_spec,
        out_shape=(sds((WIN, D_MODEL), F32), sds((2, o_half, D_MODEL), F32), sds((N_DEV, PACK_ROWS, D_MODEL), F32)),
        compiler_params=pltpu.CompilerParams(dimension_semantics=("arbitrary", "arbitrary"),
                                             vmem_limit_bytes=VMEM_LIMIT),
    )(order, h_t, wins, gw_out, small)


ADAM_STEPS = 4


def _adamw_math(w, g, m, v):
    m = ADAM_B1 * m + (1.0 - ADAM_B1) * g
    v = ADAM_B2 * v + (1.0 - ADAM_B2) * (g * g)
    m_hat = m / (1.0 - ADAM_B1 ** ADAM_STEP)
    v_hat = v / (1.0 - ADAM_B2 ** ADAM_STEP)
    delta = -ADAM_LR * (m_hat / (jnp.sqrt(v_hat) + ADAM_EPS) + ADAM_WD * w)
    return delta, m, v


def _adamw(w, g, m, v, name):
    r, c = w.shape

    def body(w_ref, g_ref, m_ref, v_ref, d_ref, nm_ref, nv_ref):
        delta, nm, nv = _adamw_math(w_ref[...], g_ref[...], m_ref[...], v_ref[...])
        d_ref[...] = delta
        nm_ref[...] = nm
        nv_ref[...] = nv

    rows = r // ADAM_STEPS
    assert rows * ADAM_STEPS == r and rows % 8 == 0
    spec = pl.BlockSpec((rows, c), lambda i: (i, 0))
    shape = jax.ShapeDtypeStruct((r, c), F32)
    return pl.pallas_call(
        body, name=name, grid=(ADAM_STEPS,), in_specs=[spec] * 4, out_specs=(spec,) * 3,
        out_shape=(shape,) * 3, compiler_params=pltpu.CompilerParams(vmem_limit_bytes=VMEM_LIMIT),
    )(w, g, m, v)


PACK_ROWS = 8


def _fold_heads(v):
    y = v[:, 0:LANES]
    for j in range(1, v.shape[1] // LANES):
        y = y + v[:, j * LANES:(j + 1) * LANES]
    return y + pltpu.roll(y, HEAD_DIM, 1)


N_SMALL = 6


def _small_adamw(recv, weights, m, v):
    def body(*refs):
        r_ref = refs[0]
        w_refs, m_refs, v_refs = (refs[1 + n * N_SMALL:1 + (n + 1) * N_SMALL] for n in range(3))
        outs = refs[1 + 3 * N_SMALL:]
        g_refs, d_refs, nm_refs, nv_refs = (outs[n * N_SMALL:(n + 1) * N_SMALL] for n in range(4))
        loss_ref = outs[4 * N_SMALL]
        tot = r_ref[0]
        for j in range(1, N_DEV):
            tot = tot + r_ref[j]
        loss_ref[...] = tot[3:4, 0:LANES]
        row1 = tot[1:2, :]
        row2 = tot[2:3, :]
        grads = [tot[0:1, :],
                 _fold_heads(row1[:, 0:WIDTH])[:, :HEAD_DIM],
                 _fold_heads(row2[:, WIDTH:WIDTH + A_KV_WIDTH])[:, :HEAD_DIM],
                 row2[:, WIDTH + A_KV_WIDTH:WIDTH + A_KV_WIDTH + HEADS],
                 _fold_heads(row1[:, WIDTH:2 * WIDTH])[:, :HEAD_DIM],
                 _fold_heads(row2[:, 0:WIDTH])[:, :HEAD_DIM]]
        for n, g in enumerate(grads):
            g_refs[n][...] = g
            delta, nm, nv = _adamw_math(w_refs[n][...], g, m_refs[n][...], v_refs[n][...])
            d_refs[n][...] = delta
            nm_refs[n][...] = nm
            nv_refs[n][...] = nv

    shapes = tuple(jax.ShapeDtypeStruct(a.shape, F32) for a in weights)
    outs = pl.pallas_call(body, name="small_adamw", out_shape=shapes * 4 + (jax.ShapeDtypeStruct((1, LANES), F32),)
                          )(recv, *weights, *m, *v)
    return tuple(outs[n * N_SMALL:(n + 1) * N_SMALL] for n in range(4)) + (outs[4 * N_SMALL],)


def kernel(x, norm_gain, w_in, q_norm_a, k_norm_a, sinks_a, q_norm_b, k_norm_b, w_out, loss_target, m_norm_gain, m_w_in, m_q_norm_a, m_k_norm_a, m_sinks_a, m_q_norm_b, m_k_norm_b, m_w_out, v_norm_gain, v_w_in, v_q_norm_a, v_k_norm_a, v_sinks_a, v_q_norm_b, v_k_norm_b, v_w_out):
    chip = 2 * lax.axis_index("x") + lax.axis_index("y")

    w_in_t, m_w_in_t, v_w_in_t = w_in[0].T, m_w_in[0].T, v_w_in[0].T

    w_in_all, w_out_all = _gather_weights([w_in_t, w_out[0]], "gather_weights")
    w_in_bf = w_in_all.reshape(IN_WIDTH, D_MODEL)
    w_out_bf = w_out_all.reshape(D_MODEL, D_MODEL)

    loss_part, gx, h_t, wins, gw_out, (dgain, dgqa, dgka, dsink, dgqb, dgkb) = _local_step(
        x[0], loss_target[0], norm_gain, w_in_bf, q_norm_a, k_norm_a, sinks_a, q_norm_b, k_norm_b, w_out_bf)

    small = jnp.concatenate([
        dgain, jnp.concatenate([dgqa, dgqb], axis=1),
        jnp.concatenate([dgkb, dgka, dsink, jnp.zeros((1, D_MODEL - WIDTH - 2 * A_KV_WIDTH), F32)], axis=1),
        jnp.pad(loss_part, ((0, 0), (0, D_MODEL - LANES))),
        jnp.zeros((PACK_ROWS - 4, D_MODEL), F32)], axis=0)
    order = (chip ^ jnp.array(RELATIONS, jnp.int32)).astype(jnp.int32)
    win_sum, wout_sum, small_recv = _grad_reduce(order, h_t, wins, gw_out, small)
    shift = jnp.array(WIN_SHIFT, jnp.int32)[chip]
    g_w_in_t = lax.dynamic_slice_in_dim(win_sum, shift, IN_COLS, axis=0)
    g_w_out = wout_sum.reshape(OUT_ROWS, D_MODEL)

    d_w_in, nm_w_in, nv_w_in = (a.T for a in _adamw(w_in_t, g_w_in_t, m_w_in_t, v_w_in_t, "adamw_w_in"))
    g_w_in = g_w_in_t.T
    d_w_out, nm_w_out, nv_w_out = _adamw(w_out[0], g_w_out, m_w_out[0], v_w_out[0], "adamw_w_out")
    g_s, d_s, nm_s, nv_s, loss_row = _small_adamw(
        small_recv,
        (norm_gain, q_norm_a, k_norm_a, sinks_a, q_norm_b, k_norm_b),
        (m_norm_gain, m_q_norm_a, m_k_norm_a, m_sinks_a, m_q_norm_b, m_k_norm_b),
        (v_norm_gain, v_q_norm_a, v_k_norm_a, v_sinks_a, v_q_norm_b, v_k_norm_b))
    loss = loss_row[0, 0]

    def leaves(small_ones, big_in, big_out):
        return (small_ones[0], big_in[None]) + tuple(small_ones[1:]) + (big_out[None],)

    return ((loss, gx[None]) + leaves(g_s, g_w_in, g_w_out) + leaves(d_s, d_w_in, d_w_out)
            + leaves(nm_s, nm_w_in, nm_w_out) + leaves(nv_s, nv_w_in, nv_w_out))
```

```python
import jax
import jax.numpy as jnp
from jax import lax
from jax.experimental import pallas as pl
from jax.experimental.pallas import tpu as pltpu

F32 = jnp.float32
BF16 = jnp.bfloat16

D_MODEL = 1024
HEAD_DIM = 64
HEADS = 8
WIDTH = HEADS * HEAD_DIM
A_KV_WIDTH = 2 * HEAD_DIM
BLOCK = 128
LANES = 128
FOLD = 16
A_MAX_DIST = 127
B_MAX_DIST = 128
ROPE_THETA = 10000.0
ROPE_SPLIT = 64
EPS = 1e-6
NEG = -1e30
SCALE = HEAD_DIM ** -0.5

IN_WIDTH = 3328
C_QA, C_KA, C_VA, C_GA, C_QB, C_KB, C_VB, C_GB, C_END = 0, 512, 640, 768, 1280, 1792, 2304, 2816, 3328

N_DEV = 8
N_CHIP = 4
MESH = pl.DeviceIdType.MESH
IN_COLS = IN_WIDTH // N_CHIP
WIN = 896
WIN_START = (0, 768, 1664, 2432)
WIN_SHIFT = (0, 64, 0, 64)
OUT_ROWS = D_MODEL // N_CHIP
RELATIONS = (3, 1, 2, 0)

ADAM_LR = 0.001
ADAM_B1 = 0.9
ADAM_B2 = 0.999
ADAM_EPS = 1e-08
ADAM_WD = 0.01
ADAM_STEP = 10

ROW_TILE = 256
FOLD_ROWS = ROW_TILE // FOLD
GRAD_ROWS = 1024
ACC_COLS = 256
VMEM_LIMIT = 56 * 1024 * 1024


def _dot(a, b):
    return jnp.dot(a, b, preferred_element_type=F32)


def _dot_nt(a, b):
    return lax.dot_general(a, b, (((1,), (1,)), ((), ())), preferred_element_type=F32)


def _dot_tn(a, b):
    return lax.dot_general(a, b, (((0,), (0,)), ((), ())), preferred_element_type=F32)


def _head_sum(z, bd):
    w = bd.shape[0]
    zb = z.astype(BF16)
    parts = [_dot(zb[:, a:a + w], bd) for a in range(0, z.shape[1], w)]
    return parts[0] if len(parts) == 1 else jnp.concatenate(parts, axis=1)


def _swap_halves(t):
    w = t.shape[1]
    lane = lax.broadcasted_iota(jnp.int32, t.shape, 1)
    return jnp.where(lane % HEAD_DIM < HEAD_DIM // 2, pltpu.roll(t, w - 32, 1), pltpu.roll(t, 32, 1))


def _qknorm_rope(t, g, cos, sin_s, bd):
    r = lax.rsqrt(_head_sum(t * t, bd) * (1.0 / HEAD_DIM) + EPS)
    n = (t * r) * g
    return n * cos + _swap_halves(n) * sin_s


def _qknorm_rope_bwd(dout, t, g, cos, sin_s, bd):
    dout, t = dout.astype(F32), t.astype(F32)
    dn = dout * cos + _swap_halves(dout * sin_s)
    r = lax.rsqrt(_head_sum(t * t, bd) * (1.0 / HEAD_DIM) + EPS)
    tr = t * r
    u = dn * g
    dt = r * (u - tr * (_head_sum(u * tr, bd) * (1.0 / HEAD_DIM)))
    return dt, dn * tr


def _sigmoid(g):
    return 1.0 / (1.0 + jnp.exp(-g))


def _expand_heads(st):
    t = st.shape[0]
    lane = lax.broadcasted_iota(jnp.int32, (t, LANES), 1)
    chunks = []
    for c in range(WIDTH // LANES):
        chunks.append(jnp.where(lane < HEAD_DIM, st[:, 2 * c:2 * c + 1], st[:, 2 * c + 1:2 * c + 2]))
    return jnp.concatenate(chunks, axis=1)


def _reduce_heads(z):
    t = z.shape[0]
    lane = lax.broadcasted_iota(jnp.int32, (t, LANES), 1)
    out = jnp.zeros((t, LANES), F32)
    for c in range(WIDTH // LANES):
        zc = z[:, c * LANES:(c + 1) * LANES]
        for ph in range(2):
            s = jnp.sum(jnp.where((lane // HEAD_DIM) == ph, zc, 0.0), axis=-1, keepdims=True)
            out = jnp.where(lane == 2 * c + ph, s, out)
    return out


def _fold_scratch(w):
    return pltpu.VMEM((w // LANES, ROW_TILE, LANES), F32)


def _store_folded(out_ref, val, scr, col0=0):
    w = val.shape[1]
    n = w // LANES
    for c in range(n):
        scr[c] = val[:, c * LANES:(c + 1) * LANES]
    for r in range(FOLD):
        piece = [scr[c, pl.ds(r, FOLD_ROWS, stride=FOLD), :] for c in range(n)]
        out_ref[r, :, col0:col0 + w] = (piece[0] if n == 1 else jnp.concatenate(piece, axis=1)).astype(out_ref.dtype)


def _load_folded(in_ref, scr):
    n = in_ref.shape[2] // LANES
    for r in range(FOLD):
        blk = in_ref[r].astype(F32)
        for c in range(n):
            scr[c, pl.ds(r, FOLD_ROWS, stride=FOLD), :] = blk[:, c * LANES:(c + 1) * LANES]
    return scr[0] if n == 1 else jnp.concatenate([scr[c] for c in range(n)], axis=1)


def _fold_matrix():
    f = jnp.arange(ROW_TILE)
    return (jnp.arange(ROW_TILE)[None, :] == (FOLD * (f % FOLD_ROWS) + f // FOLD_ROWS)[:, None]).astype(BF16)


def _store_folded_bf16(out_ref, val, perm):
    folded = _dot(perm, val.astype(BF16)).astype(out_ref.dtype)
    for r in range(FOLD):
        out_ref[r] = folded[r * FOLD_ROWS:(r + 1) * FOLD_ROWS]


def _load_folded_bf16(in_ref, perm):
    blk = jnp.concatenate([in_ref[r] for r in range(FOLD)], axis=0)
    return _dot(perm, blk)


def _rows(w, tm=ROW_TILE):
    return pl.BlockSpec((tm, w), lambda i: (i, 0))


def _folded_rows(w):
    return pl.BlockSpec((FOLD, FOLD_ROWS, w), lambda i: (0, i, 0))


def _whole(shape):
    return pl.BlockSpec(shape, lambda i: (0,) * len(shape))


def _inproj(x2, gain, w_bf, cos, sin_s, gqa, gka, gqb, gkb, bd256, bd128, w_out_blk):
    s = x2.shape[0]
    tm = ROW_TILE
    n_steps = s // tm
    n_rel = N_CHIP - 1
    o_half = OUT_ROWS // 2

    def body(x_ref, gain_ref, w_hbm, cos_ref, sin_ref, gqa_ref, gka_ref, gqb_ref, gkb_ref, bd256_ref, bd128_ref,
             wout_ref, qa_ref, kva_ref, qb_ref, kvb_ref, qbf_ref, kvbf_ref,
             qa_raw_ref, ka_raw_ref, ga_ref, qb_raw_ref, kb_raw_ref, gb_ref, wout_all_ref,
             w_vmem, scr, land, ici_send, ici_recv, d2d_send, d2d_recv):
        i = pl.program_id(0)
        px_, py_, c = _position()
        b = 2 * px_ + py_

        def piece(chip_idx, core):
            return land.at[chip_idx, pl.ds(pl.multiple_of(core * o_half, o_half), o_half)]

        def other_chip(d):
            ox, oy = px_ ^ (d >> 1), py_ ^ (d & 1)
            return ox, oy, 2 * ox + oy

        def ici_copy(d, chip_idx):
            ox, oy, _ = other_chip(d)
            return pltpu.make_async_remote_copy(
                src_ref=piece(chip_idx, c), dst_ref=piece(chip_idx, c), send_sem=ici_send.at[d - 1],
                recv_sem=ici_recv.at[d - 1], device_id=(ox, oy, c), device_id_type=MESH)

        def d2d_copy(d, core):
            return pltpu.make_async_remote_copy(
                src_ref=piece(other_chip(d)[2], core), dst_ref=piece(other_chip(d)[2], core),
                send_sem=d2d_send.at[d - 1], recv_sem=d2d_recv.at[d - 1], device_id=(px_, py_, 1 - c),
                device_id_type=MESH)

        @pl.when(i == 0)
        def _():
            pltpu.sync_copy(w_hbm, w_vmem)
            land[b] = wout_ref[...].astype(BF16)
            for d in range(1, N_CHIP):
                ici_copy(d, b).start()

        @pl.when(i == n_steps // 2)
        def _():
            for d in range(1, N_CHIP):
                ici_copy(d, other_chip(d)[2]).wait_recv()
                d2d_copy(d, c).start()

        @pl.when(i == n_steps - 1)
        def _():
            for d in range(1, N_CHIP):
                d2d_copy(d, 1 - c).wait_recv()
            for d in range(1, N_CHIP):
                ici_copy(d, b).wait_send()
                d2d_copy(d, c).wait_send()
            for k in range(N_CHIP):
                wout_all_ref[k * OUT_ROWS:(k + 1) * OUT_ROWS, :] = land[k]

        xt = x_ref[...]
        r = lax.rsqrt(jnp.mean(xt * xt, axis=-1, keepdims=True) + EPS)
        h = ((xt * r) * gain_ref[...]).astype(BF16)
        cos1 = cos_ref[...]
        sin1 = sin_ref[...]
        cos4 = jnp.tile(cos1, (1, 4))
        sin4 = jnp.tile(sin1, (1, 4))

        def seg(a, b):
            return _dot_nt(h, w_vmem[a:b, :])

        t = seg(C_QA, C_KA)
        qa_raw_ref[...] = t.astype(BF16)
        qa_ref[...] = (_qknorm_rope(t, gqa_ref[...], cos4, sin4, bd256_ref[...]) * SCALE).astype(BF16)
        t = seg(C_KA, C_VA)
        ka_raw_ref[...] = t.astype(BF16)
        kva_ref[:, :A_KV_WIDTH] = _qknorm_rope(t, gka_ref[...], cos1, sin1, bd128_ref[...]).astype(BF16)
        kva_ref[:, A_KV_WIDTH:] = seg(C_VA, C_GA).astype(BF16)
        ga_ref[...] = seg(C_GA, C_QB).astype(BF16)
        t = seg(C_QB, C_KB)
        qb_raw_ref[...] = t.astype(BF16)
        t = _qknorm_rope(t, gqb_ref[...], cos4, sin4, bd256_ref[...]) * SCALE
        qb_ref[...] = t.astype(BF16)
        _store_folded(qbf_ref, t, scr)
        t = seg(C_KB, C_VB)
        kb_raw_ref[...] = t.astype(BF16)
        t = _qknorm_rope(t, gkb_ref[...], cos4, sin4, bd256_ref[...])
        kvb_ref[:, :WIDTH] = t.astype(BF16)
        _store_folded(kvbf_ref, t, scr)
        t = seg(C_VB, C_GB)
        kvb_ref[:, WIDTH:] = t.astype(BF16)
        _store_folded(kvbf_ref, t, scr, WIDTH)
        gb_ref[...] = seg(C_GB, C_END).astype(BF16)

    sds = jax.ShapeDtypeStruct
    ln = s // FOLD
    out_shape = (sds((s, WIDTH), BF16), sds((s, 2 * A_KV_WIDTH), BF16), sds((s, WIDTH), BF16),
                 sds((s, 2 * WIDTH), BF16), sds((FOLD, ln, WIDTH), BF16), sds((FOLD, ln, 2 * WIDTH), BF16),
                 sds((s, WIDTH), BF16), sds((s, A_KV_WIDTH), BF16), sds((s, WIDTH), BF16),
                 sds((s, WIDTH), BF16), sds((s, WIDTH), BF16), sds((s, WIDTH), BF16),
                 sds((D_MODEL, D_MODEL), BF16))
    out_specs = (_rows(WIDTH), _rows(2 * A_KV_WIDTH), _rows(WIDTH), _rows(2 * WIDTH),
                 _folded_rows(WIDTH), _folded_rows(2 * WIDTH),
                 _rows(WIDTH), _rows(A_KV_WIDTH), _rows(WIDTH), _rows(WIDTH), _rows(WIDTH), _rows(WIDTH),
                 _whole((D_MODEL, D_MODEL)))
    dma = pltpu.SemaphoreType.DMA
    return pl.pallas_call(
        body, name="inproj_fwd", grid=(n_steps,),
        in_specs=[_rows(D_MODEL), _whole(gain.shape), pl.BlockSpec(memory_space=pl.ANY), _rows(LANES), _rows(LANES),
                  _whole(gqa.shape), _whole(gka.shape), _whole(gqb.shape), _whole(gkb.shape), _whole(bd256.shape),
                  _whole(bd128.shape), _whole(w_out_blk.shape)],
        out_specs=out_specs, out_shape=out_shape,
        scratch_shapes=[pltpu.VMEM((IN_WIDTH, D_MODEL), BF16), _fold_scratch(WIDTH),
                        pltpu.VMEM((N_CHIP, OUT_ROWS, D_MODEL), BF16),
                        dma((n_rel,)), dma((n_rel,)), dma((n_rel,)), dma((n_rel,))],
        compiler_params=pltpu.CompilerParams(dimension_semantics=("arbitrary",), vmem_limit_bytes=VMEM_LIMIT),
    )(x2, gain, w_bf, cos, sin_s, gqa, gka, gqb, gkb, bd256, bd128, w_out_blk)


def _seq_pos(idx, dil):
    if dil == 4:
        return 4 * (idx % 32) + idx // 32
    return idx


SOFTMAX_ROWS = 64


def _upper_mask(dil, r0=0, rows=2 * BLOCK):
    qi = (lax.broadcasted_iota(jnp.int32, (rows, BLOCK), 0) + r0) % BLOCK
    kj = lax.broadcasted_iota(jnp.int32, (rows, BLOCK), 1)
    return _seq_pos(kj, dil) > _seq_pos(qi, dil)


def _eye_mask(r0=0, rows=2 * BLOCK):
    qi = (lax.broadcasted_iota(jnp.int32, (rows, BLOCK), 0) + r0) % BLOCK
    kj = lax.broadcasted_iota(jnp.int32, (rows, BLOCK), 1)
    return qi == kj


def _stack_heads(a2, c, gqa):
    lane = lax.broadcasted_iota(jnp.int32, (1, LANES), 1) // HEAD_DIM
    zero = jnp.zeros_like(a2)
    if gqa:
        keep = lane == (c // 2)
        return jnp.concatenate([jnp.where(keep, a2, zero), jnp.where(keep, _swap_heads(a2), zero)], axis=0)
    return jnp.concatenate([jnp.where(lane == 0, a2, zero), jnp.where(lane == 1, a2, zero)], axis=0)


def _unstack_heads(a, c, gqa):
    lane = lax.broadcasted_iota(jnp.int32, (1, LANES), 1) // HEAD_DIM
    if gqa:
        return jnp.where(lane == (c // 2), a[:BLOCK], _swap_heads(a[BLOCK:]))
    return jnp.where(lane == 0, a[:BLOCK], a[BLOCK:])


def _stacked_head_ids(c, gqa):
    if gqa:
        return 2 * c + c // 2, 2 * c + 1 - c // 2
    return 2 * c, 2 * c + 1


def _per_head_rows(blk, heads):
    return jnp.concatenate([blk[:, heads[0]:heads[0] + 1], blk[:, heads[1]:heads[1] + 1]], axis=0)


def _attn_view(a, dil):
    if dil == 1:
        return a[None]
    if dil == 4:
        return a.reshape(4, 4, a.shape[1], a.shape[2])
    return a


def _attn_unview(a, dil):
    if dil == 1:
        return a[0]
    if dil == 4:
        return a.reshape(FOLD, a.shape[2], a.shape[3])
    return a


ATTN_BLOCKS_PER_STEP = 8


def _attn_specs(dil):
    if dil == 4:
        def spec(n, fn):
            return lambda w: pl.BlockSpec((4, None, n * BLOCK // 4, w), lambda r, i: (0, r, fn(i), 0))
    else:
        def spec(n, fn):
            return lambda w: pl.BlockSpec((None, n * BLOCK, w), lambda r, i: (r, fn(i), 0))
    return spec


def _blk_rows(g, dil):
    n = BLOCK // 4 if dil == 4 else BLOCK
    if isinstance(g, int):
        return slice(g * n, (g + 1) * n)
    return pl.ds(pl.multiple_of(g * n, n), n)


def _blk_load(ref, sl, dil, g=0):
    if dil == 4:
        return ref[:, _blk_rows(g, dil), sl].reshape(BLOCK, sl.stop - sl.start)
    return ref[_blk_rows(g, dil), sl]


def _blk_store(ref, sl, val, dil, g=0):
    val = val.astype(ref.dtype)
    if dil == 4:
        ref[:, _blk_rows(g, dil), sl] = val.reshape(4, BLOCK // 4, sl.stop - sl.start)
    else:
        ref[_blk_rows(g, dil), sl] = val


def _swap_heads(a):
    return pltpu.roll(a.astype(F32), HEAD_DIM, 1).astype(a.dtype)


STAT_SHIFT = 8


def _attn_fwd(q, kv, sinks, *, dil, max_dist, name):
    q, kv = _attn_view(q, dil), _attn_view(kv, dil)
    kw = kv.shape[-1] // 2
    gqa = kw == A_KV_WIDTH
    n_seq = dil
    nb = (q.shape[-2] * (4 if dil == 4 else 1)) // BLOCK
    per_step = min(ATTN_BLOCKS_PER_STEP, nb)
    with_sinks = sinks is not None
    all_lanes = slice(0, LANES)
    assert max_dist in (BLOCK - 1, BLOCK) and nb % per_step == 0
    diag = max_dist == BLOCK

    def body(*refs):
        if with_sinks:
            q_ref, kvp_ref, kvc_ref, sink_ref, o_ref, ml_ref = refs
        else:
            q_ref, kvp_ref, kvc_ref, o_ref, ml_ref = refs

        def block(g, has_prev):
            prev_ref, prev_g = (kvp_ref, 0) if isinstance(g, int) else (kvc_ref, g - 1)
            lane = lax.broadcasted_iota(jnp.int32, (1, LANES), 1)
            with_diag = diag and has_prev
            upper, eye = _upper_mask(dil), _eye_mask()
            first_rows = lax.broadcasted_iota(jnp.int32, (2 * BLOCK, 1), 0) < BLOCK
            ml_blk = jnp.zeros((BLOCK, LANES), F32)
            chunks = range(WIDTH // LANES)
            scores, values = [], []
            for c in chunks:
                sl = slice(c * LANES, (c + 1) * LANES)
                ksl = slice(0, LANES) if gqa else sl
                vsl = slice(ksl.start + kw, ksl.stop + kw)
                kcur, vcur = _blk_load(kvc_ref, ksl, dil, g), _blk_load(kvc_ref, vsl, dil, g)
                qs = _stack_heads(_blk_load(q_ref, sl, dil, g), c, gqa)
                if has_prev:
                    kcur = jnp.concatenate([_blk_load(prev_ref, ksl, dil, prev_g), kcur], axis=0)
                    vcur = jnp.concatenate([_blk_load(prev_ref, vsl, dil, prev_g), vcur], axis=0)
                scores.append(_dot_nt(qs, kcur))
                values.append(vcur)
            probs = []
            for c in chunks:
                heads = _stacked_head_ids(c, gqa)
                s = scores[c]
                if has_prev:
                    s_p = s[:, :BLOCK]
                    sc = jnp.where(upper, s_p, s[:, BLOCK:])
                else:
                    sc = jnp.where(upper, NEG, s)
                if with_diag:
                    sd = jnp.where(eye, s_p, NEG)
                    m = jnp.max(jnp.maximum(sc, sd), axis=-1, keepdims=True)
                else:
                    m = jnp.max(sc, axis=-1, keepdims=True)
                if with_sinks:
                    sk = jnp.where(first_rows, sink_ref[0, heads[0]], sink_ref[0, heads[1]])
                    m = jnp.maximum(m, sk)
                p = jnp.exp(sc - m)
                zero = jnp.zeros_like(p)
                if with_diag:
                    pd = jnp.exp(sd - m)
                    l = jnp.sum(p + pd, axis=-1, keepdims=True)
                else:
                    pd = zero
                    l = jnp.sum(p, axis=-1, keepdims=True)
                if with_sinks:
                    l = l + jnp.exp(sk - m)
                pf = jnp.where(upper, zero, p)
                if has_prev:
                    pf = jnp.concatenate([jnp.where(upper, p, pd), pf], axis=1)
                probs.append(pf.astype(BF16))
                for n, h in enumerate(heads):
                    rows = slice(n * BLOCK, (n + 1) * BLOCK)
                    ml_blk = jnp.where(lane == h, m[rows], ml_blk)
                    ml_blk = jnp.where(lane == h + STAT_SHIFT, l[rows], ml_blk)
            for c in chunks:
                sl = slice(c * LANES, (c + 1) * LANES)
                _blk_store(o_ref, sl, _unstack_heads(_dot(probs[c], values[c]), c, gqa), dil, g)
            _blk_store(ml_ref, all_lanes, ml_blk, dil, g)

        @pl.when(pl.program_id(1) == 0)
        def _():
            block(0, False)

        @pl.when(pl.program_id(1) > 0)
        def _():
            block(0, True)

        if per_step > 1:
            def rest(g, carry):
                block(g, True)
                return carry

            lax.fori_loop(1, per_step, rest, 0)

    spec = _attn_specs(dil)
    cur = spec(per_step, lambda i: i)
    prev = spec(1, lambda i: jnp.maximum(i * per_step - 1, 0))
    in_specs = [cur(WIDTH), prev(2 * kw), cur(2 * kw)]
    args = [q, kv, kv]
    if with_sinks:
        in_specs.append(pl.BlockSpec(memory_space=pltpu.SMEM))
        args.append(sinks)
    stats = jax.ShapeDtypeStruct(q.shape[:-1] + (LANES,), F32)
    o, ml = pl.pallas_call(
        body, name=name, grid=(n_seq, nb // per_step), in_specs=in_specs,
        out_specs=(cur(WIDTH), cur(LANES)),
        out_shape=(jax.ShapeDtypeStruct(q.shape, BF16), stats),
        compiler_params=pltpu.CompilerParams(dimension_semantics=("arbitrary", "arbitrary"),
                                             vmem_limit_bytes=VMEM_LIMIT),
    )(*args)
    return _attn_unview(o, dil), _attn_unview(ml, dil)


def _attn_bwd(q, kv, do, ld, *, dil, max_dist, name):
    q, kv, do, ld = (_attn_view(a, dil) for a in (q, kv, do, ld))
    kw = kv.shape[-1] // 2
    gqa = kw == A_KV_WIDTH
    n_seq = dil
    nb = (q.shape[-2] * (4 if dil == 4 else 1)) // BLOCK
    n_kc = kw // LANES
    per_step = min(ATTN_BLOCKS_PER_STEP, nb)
    all_lanes = slice(0, LANES)
    assert max_dist in (BLOCK - 1, BLOCK) and nb % per_step == 0
    diag = max_dist == BLOCK

    def body(q_ref, kvp_ref, kvc_ref, do_ref, ld_ref, dq_ref, dkv_ref, ck_ref, cv_ref):
        i = pl.program_id(1)

        def block(g, has_prev):
            prev_ref, prev_g = (kvp_ref, 0) if isinstance(g, int) else (kvc_ref, g - 1)
            seq_blk = i * per_step + g
            upper, eye = _upper_mask(dil), _eye_mask()
            ld_blk = _blk_load(ld_ref, all_lanes, dil, g)
            dk_acc = [None] * n_kc
            dv_acc = [None] * n_kc
            chunks = range(WIDTH // LANES)
            operands, products = [], []
            for c in chunks:
                sl = slice(c * LANES, (c + 1) * LANES)
                kc = 0 if gqa else c
                ksl = slice(kc * LANES, (kc + 1) * LANES)
                vsl = slice(ksl.start + kw, ksl.stop + kw)
                k2, v2 = _blk_load(kvc_ref, ksl, dil, g), _blk_load(kvc_ref, vsl, dil, g)
                if has_prev:
                    k2 = jnp.concatenate([_blk_load(prev_ref, ksl, dil, prev_g), k2], axis=0)
                    v2 = jnp.concatenate([_blk_load(prev_ref, vsl, dil, prev_g), v2], axis=0)
                qs = _stack_heads(_blk_load(q_ref, sl, dil, g), c, gqa)
                dos = _stack_heads(_blk_load(do_ref, sl, dil, g), c, gqa)
                operands.append((qs, dos, k2))
                products.append((_dot_nt(qs, k2), _dot_nt(dos, v2)))
            weights = []
            for c in chunks:
                heads = _stacked_head_ids(c, gqa)
                lse2 = _per_head_rows(ld_blk, heads)
                dl2 = _per_head_rows(ld_blk, tuple(h + STAT_SHIFT for h in heads))
                s, dp = products[c]
                if has_prev:
                    s_p, dp_p = s[:, :BLOCK], dp[:, :BLOCK]
                    sc = jnp.where(upper, s_p, s[:, BLOCK:])
                    dpc = jnp.where(upper, dp_p, dp[:, BLOCK:])
                else:
                    sc = jnp.where(upper, NEG, s)
                    dpc = dp
                p = jnp.exp(sc - lse2)
                ds = p * (dpc - dl2)
                zero = jnp.zeros_like(p)
                pf = jnp.where(upper, zero, p)
                dsf = jnp.where(upper, zero, ds)
                if has_prev:
                    if diag:
                        pd = jnp.exp(jnp.where(eye, s_p, NEG) - lse2)
                        dsd = pd * (dp_p - dl2)
                    else:
                        pd = dsd = zero
                    pf = jnp.concatenate([jnp.where(upper, p, pd), pf], axis=1)
                    dsf = jnp.concatenate([jnp.where(upper, ds, dsd), dsf], axis=1)
                weights.append((pf.astype(BF16), dsf.astype(BF16)))
            for c in chunks:
                sl = slice(c * LANES, (c + 1) * LANES)
                kc = 0 if gqa else c
                qs, dos, k2 = operands[c]
                pf, dsf = weights[c]
                _blk_store(dq_ref, sl, _unstack_heads(_dot(dsf, k2), c, gqa) * SCALE, dil, g)
                dk2 = _dot_tn(dsf, qs)
                dv2 = _dot_tn(pf, dos)
                dk_acc[kc] = dk2 if dk_acc[kc] is None else dk_acc[kc] + dk2
                dv_acc[kc] = dv2 if dv_acc[kc] is None else dv_acc[kc] + dv2
            for kc in range(n_kc):
                sl = slice(kc * LANES, (kc + 1) * LANES)
                vsl = slice(sl.start + kw, sl.stop + kw)
                if has_prev:
                    _blk_store(dkv_ref, sl, ck_ref[:, sl] + dk_acc[kc][:BLOCK], dil, seq_blk - 1)
                    _blk_store(dkv_ref, vsl, cv_ref[:, sl] + dv_acc[kc][:BLOCK], dil, seq_blk - 1)
                    ck_ref[:, sl] = dk_acc[kc][BLOCK:]
                    cv_ref[:, sl] = dv_acc[kc][BLOCK:]
                else:
                    ck_ref[:, sl] = dk_acc[kc]
                    cv_ref[:, sl] = dv_acc[kc]

        @pl.when(i == 0)
        def _():
            block(0, False)

        @pl.when(i > 0)
        def _():
            block(0, True)

        if per_step > 1:
            def rest(g, carry):
                block(g, True)
                return carry

            lax.fori_loop(1, per_step, rest, 0)

        @pl.when(i == nb // per_step - 1)
        def _():
            for kc in range(n_kc):
                sl = slice(kc * LANES, (kc + 1) * LANES)
                _blk_store(dkv_ref, sl, ck_ref[:, sl], dil, nb - 1)
                _blk_store(dkv_ref, slice(sl.start + kw, sl.stop + kw), cv_ref[:, sl], dil, nb - 1)

    spec = _attn_specs(dil)
    cur = spec(per_step, lambda i: i)
    prev = spec(1, lambda i: jnp.maximum(i * per_step - 1, 0))
    if dil == 4:
        whole = pl.BlockSpec((4, None, kv.shape[2], 2 * kw), lambda r, i: (0, r, 0, 0))
    else:
        whole = pl.BlockSpec((None, kv.shape[1], 2 * kw), lambda r, i: (r, 0, 0))
    sds = jax.ShapeDtypeStruct
    dq, dkv = pl.pallas_call(
        body, name=name, grid=(n_seq, nb // per_step),
        in_specs=[cur(WIDTH), prev(2 * kw), cur(2 * kw), cur(WIDTH), cur(LANES)],
        out_specs=(cur(WIDTH), whole),
        out_shape=(sds(q.shape, BF16), sds(kv.shape, BF16)),
        scratch_shapes=[pltpu.VMEM((BLOCK, kw), F32), pltpu.VMEM((BLOCK, kw), F32)],
        compiler_params=pltpu.CompilerParams(dimension_semantics=("arbitrary", "arbitrary"),
                                             vmem_limit_bytes=VMEM_LIMIT),
    )(q, kv, kv, do, ld)
    return _attn_unview(dq, dil), _attn_unview(dkv, dil)


def _outproj(att_a, att_b1, att_b4, att_b16, g_a, g_b, x2, tgt2, w_out_bf, sink_row, perm):
    s = x2.shape[0]
    tm = ROW_TILE

    def body(oa_ref, mla_ref, ob1_ref, ml1_ref, ob4_ref, ml4_ref, ob16_ref, ml16_ref,
             ga_ref, gb_ref, x_ref, t_ref, w_ref, sink_ref, perm_ref,
             dy_ref, doa_ref, dob_ref, dobf_ref, dga_ref, dgb_ref, lda_ref, ldb_ref, ldbf_ref,
             gw_ref, loss_ref, dsink_ref, scr_st):
        i = pl.program_id(0)
        perm = perm_ref[...]
        lane = lax.broadcasted_iota(jnp.int32, (tm, LANES), 1)
        used = lane < HEADS

        def split(ml):
            return jnp.where(used, ml, 0.0), jnp.where(used, pltpu.roll(ml, LANES - STAT_SHIFT, 1), 1.0)

        @pl.when(i == 0)
        def _():
            gw_ref[...] = jnp.zeros_like(gw_ref)
            loss_ref[...] = jnp.zeros_like(loss_ref)
            dsink_ref[...] = jnp.zeros_like(dsink_ref)

        ms, ls = zip(split(ml1_ref[...]), split(_load_folded(ml4_ref, scr_st)), split(_load_folded(ml16_ref, scr_st)))
        mx = jnp.maximum(jnp.maximum(ms[0], ms[1]), ms[2])
        scale = [jnp.exp(mp - mx) for mp in ms]
        den = (ls[0] * scale[0] + ls[1] * scale[1]) + ls[2] * scale[2]
        lse_b = jnp.where(used, mx + jnp.log(den), 0.0)
        inv_den = 1.0 / den
        o_b = _expand_heads(scale[0] * inv_den) * ob1_ref[...].astype(F32)
        o_b = o_b + _expand_heads(scale[1] * inv_den) * _load_folded_bf16(ob4_ref, perm)
        o_b = o_b + _expand_heads(scale[2] * inv_den) * _load_folded_bf16(ob16_ref, perm)
        m_a, l_a = split(mla_ref[...])
        lse_a = jnp.where(used, m_a + jnp.log(l_a), 0.0)
        o_a = _expand_heads(1.0 / l_a) * oa_ref[...].astype(F32)
        g_a = ga_ref[...].astype(F32)
        g_b = gb_ref[...].astype(F32)
        sg_a = _sigmoid(g_a)
        sg_b = _sigmoid(g_b)
        silu_a = g_a * sg_a
        silu_b = g_b * sg_b
        mixed = jnp.concatenate([o_a * silu_a, o_b * silu_b], axis=1).astype(BF16)
        w = w_ref[...]
        y = x_ref[...] + _dot(mixed, w)
        diff = y - t_ref[...]
        loss_ref[...] += (0.5 / D_MODEL) * jnp.sum(diff * diff)
        dy = diff * (1.0 / D_MODEL)
        dy_ref[...] = dy
        dyb = dy.astype(BF16)
        gw_ref[...] += _dot_tn(mixed, dyb)
        dmixed = _dot_nt(dyb, w)
        dm_a = dmixed[:, :WIDTH]
        dm_b = dmixed[:, WIDTH:]
        do_a = dm_a * silu_a
        do_b = dm_b * silu_b
        doa_ref[...] = do_a.astype(BF16)
        dob_ref[...] = do_b.astype(BF16)
        _store_folded_bf16(dobf_ref, do_b, perm)
        dga_ref[...] = (dm_a * o_a * (sg_a * (1.0 + g_a * (1.0 - sg_a)))).astype(BF16)
        dgb_ref[...] = (dm_b * o_b * (sg_b * (1.0 + g_b * (1.0 - sg_b)))).astype(BF16)
        dl_a = _reduce_heads(do_a * o_a)
        dl_b = _reduce_heads(do_b * o_b)
        lda_ref[...] = lse_a + pltpu.roll(dl_a, STAT_SHIFT, 1)
        ld_b = lse_b + pltpu.roll(dl_b, STAT_SHIFT, 1)
        ldb_ref[...] = ld_b
        _store_folded(ldbf_ref, ld_b, scr_st)
        dsink_ref[...] -= jnp.sum(jnp.exp(sink_ref[...] - lse_a) * dl_a, axis=0, keepdims=True)

    sds = jax.ShapeDtypeStruct
    ln = s // FOLD
    natural = [_rows(WIDTH), _rows(LANES)]
    folded = [_folded_rows(WIDTH), _folded_rows(LANES)]
    return pl.pallas_call(
        body, name="outproj_fwd_bwd", grid=(s // tm,),
        in_specs=natural + natural + folded + folded
                 + [_rows(WIDTH), _rows(WIDTH), _rows(D_MODEL), _rows(D_MODEL), _whole((D_MODEL, D_MODEL)),
                    _whole((1, LANES)), _whole(perm.shape)],
        out_specs=(_rows(D_MODEL), _rows(WIDTH), _rows(WIDTH), _folded_rows(WIDTH), _rows(WIDTH), _rows(WIDTH),
                   _rows(LANES), _rows(LANES), _folded_rows(LANES),
                   _whole((D_MODEL, D_MODEL)), _whole((1, LANES)), _whole((1, LANES))),
        out_shape=(sds((s, D_MODEL), F32), sds((s, WIDTH), BF16), sds((s, WIDTH), BF16),
                   sds((FOLD, ln, WIDTH), BF16), sds((s, WIDTH), BF16), sds((s, WIDTH), BF16),
                   sds((s, LANES), F32), sds((s, LANES), F32), sds((FOLD, ln, LANES), F32),
                   sds((D_MODEL, D_MODEL), F32), sds((1, LANES), F32), sds((1, LANES), F32)),
        scratch_shapes=[_fold_scratch(LANES)],
        compiler_params=pltpu.CompilerParams(dimension_semantics=("arbitrary",), vmem_limit_bytes=VMEM_LIMIT),
    )(*att_a, *att_b1, *att_b4, *att_b16, g_a, g_b, x2, tgt2, w_out_bf, sink_row, perm)


def _inproj_bwd(x2, dy, gain, w_bf, cos, sin_s, gqa, gka, gqb, gkb, bd256, bd128, perm,
                qa_raw, ka_raw, qb_raw, kb_raw, d_a, d_b1, d_b4, d_b16, dg_a, dg_b):
    s = x2.shape[0]
    tm = ROW_TILE

    def body(x_ref, dy_ref, gain_ref, w_hbm, cos_ref, sin_ref, gqa_ref, gka_ref, gqb_ref, gkb_ref, bd256_ref,
             bd128_ref, perm_ref, qa_raw_ref, ka_raw_ref, qb_raw_ref, kb_raw_ref, dqa_ref, dkva_ref,
             dq1_ref, dkv1_ref, dq4_ref, dkv4_ref, dq16_ref, dkv16_ref, dga_ref, dgb_ref,
             gx_ref, ht_ref, win_ref,
             dgain_ref, dgqa_ref, dgka_ref, dgqb_ref, dgkb_ref, w_vmem, dproj_ref):
        i = pl.program_id(0)
        perm = perm_ref[...]

        @pl.when(i == 0)
        def _():
            pltpu.sync_copy(w_hbm, w_vmem)
            dgain_ref[...] = jnp.zeros_like(dgain_ref)
            dgqa_ref[...] = jnp.zeros_like(dgqa_ref)
            dgka_ref[...] = jnp.zeros_like(dgka_ref)
            dgqb_ref[...] = jnp.zeros_like(dgqb_ref)
            dgkb_ref[...] = jnp.zeros_like(dgkb_ref)

        cos1 = cos_ref[...]
        sin1 = sin_ref[...]
        cos4 = jnp.tile(cos1, (1, 4))
        sin4 = jnp.tile(sin1, (1, 4))

        dt, dg = _qknorm_rope_bwd(dqa_ref[...], qa_raw_ref[...], gqa_ref[...], cos4, sin4, bd256_ref[...])
        dproj_ref[:, C_QA:C_KA] = dt.astype(BF16)
        dgqa_ref[...] += jnp.sum(dg, axis=0, keepdims=True)
        dt, dg = _qknorm_rope_bwd(dkva_ref[:, :A_KV_WIDTH], ka_raw_ref[...], gka_ref[...], cos1, sin1,
                                  bd128_ref[...])
        dproj_ref[:, C_KA:C_VA] = dt.astype(BF16)
        dgka_ref[...] += jnp.sum(dg, axis=0, keepdims=True)
        dproj_ref[:, C_VA:C_GA] = dkva_ref[:, A_KV_WIDTH:]
        dproj_ref[:, C_GA:C_QB] = dga_ref[...]
        dq = (dq1_ref[...].astype(F32) + _load_folded_bf16(dq4_ref, perm)) + _load_folded_bf16(dq16_ref, perm)
        dt, dg = _qknorm_rope_bwd(dq, qb_raw_ref[...], gqb_ref[...], cos4, sin4, bd256_ref[...])
        dproj_ref[:, C_QB:C_KB] = dt.astype(BF16)
        dgqb_ref[...] += jnp.sum(dg, axis=0, keepdims=True)
        dkv = (dkv1_ref[...].astype(F32) + _load_folded_bf16(dkv4_ref, perm)) + _load_folded_bf16(dkv16_ref, perm)
        dt, dg = _qknorm_rope_bwd(dkv[:, :WIDTH], kb_raw_ref[...], gkb_ref[...], cos4, sin4, bd256_ref[...])
        dproj_ref[:, C_KB:C_VB] = dt.astype(BF16)
        dgkb_ref[...] += jnp.sum(dg, axis=0, keepdims=True)
        dproj_ref[:, C_VB:C_GB] = dkv[:, WIDTH:].astype(BF16)
        dproj_ref[:, C_GB:C_END] = dgb_ref[...]
        for k, start in enumerate(WIN_START):
            win_ref[k] = dproj_ref[:, start:start + WIN]

        xt = x_ref[...]
        gain_row = gain_ref[...]
        r = lax.rsqrt(jnp.mean(xt * xt, axis=-1, keepdims=True) + EPS)
        xr = xt * r
        ht_ref[...] = (xr * gain_row).T.astype(BF16)
        dh = _dot(dproj_ref[...], w_vmem[...])
        dgain_ref[...] += jnp.sum(dh * xr, axis=0, keepdims=True)
        u = dh * gain_row
        gx_ref[...] = dy_ref[...] + r * (u - xr * jnp.mean(u * xr, axis=-1, keepdims=True))

    def acc_row(w):
        return pl.BlockSpec((1, w), lambda i: (0, 0))

    sds = jax.ShapeDtypeStruct
    any_spec = pl.BlockSpec(memory_space=pl.ANY)
    win_spec = pl.BlockSpec((N_CHIP, tm, WIN), lambda i: (0, i, 0))
    return pl.pallas_call(
        body, name="inproj_bwd", grid=(s // tm,),
        in_specs=[_rows(D_MODEL), _rows(D_MODEL), _whole(gain.shape), any_spec, _rows(LANES), _rows(LANES),
                  _whole(gqa.shape), _whole(gka.shape), _whole(gqb.shape), _whole(gkb.shape), _whole(bd256.shape),
                  _whole(bd128.shape), _whole(perm.shape),
                  _rows(WIDTH), _rows(A_KV_WIDTH), _rows(WIDTH), _rows(WIDTH),
                  _rows(WIDTH), _rows(2 * A_KV_WIDTH), _rows(WIDTH), _rows(2 * WIDTH)]
                 + [_folded_rows(WIDTH), _folded_rows(2 * WIDTH)] * 2 + [_rows(WIDTH), _rows(WIDTH)],
        out_specs=(_rows(D_MODEL), pl.BlockSpec((D_MODEL, tm), lambda i: (0, i)), win_spec, acc_row(D_MODEL), acc_row(WIDTH), acc_row(A_KV_WIDTH), acc_row(WIDTH), acc_row(WIDTH)),
        out_shape=(sds((s, D_MODEL), F32), sds((D_MODEL, s), BF16), sds((N_CHIP, s, WIN), BF16),
                   sds((1, D_MODEL), F32),
                   sds((1, WIDTH), F32), sds((1, A_KV_WIDTH), F32), sds((1, WIDTH), F32), sds((1, WIDTH), F32)),
        scratch_shapes=[pltpu.VMEM((IN_WIDTH, D_MODEL), BF16), pltpu.VMEM((tm, IN_WIDTH), BF16)],
        compiler_params=pltpu.CompilerParams(dimension_semantics=("arbitrary",), vmem_limit_bytes=VMEM_LIMIT),
    )(x2, dy, gain, w_bf, cos, sin_s, gqa, gka, gqb, gkb, bd256, bd128, perm, qa_raw, ka_raw, qb_raw, kb_raw,
      *d_a, *d_b1, *d_b4, *d_b16, dg_a, dg_b)


def _rope_tables(s):
    half = HEAD_DIM // 2
    inv = jnp.tile(ROPE_THETA ** (-jnp.arange(half, dtype=F32) / half), 4)
    sign = jnp.tile(jnp.concatenate([-jnp.ones((half,), F32), jnp.ones((half,), F32)]), 2)
    hi = (jnp.arange(s // ROPE_SPLIT) * ROPE_SPLIT).astype(F32)[:, None] * inv[None, :]
    lo = jnp.arange(ROPE_SPLIT).astype(F32)[:, None] * inv[None, :]
    ch, sh, cl, sl = jnp.cos(hi)[:, None, :], jnp.sin(hi)[:, None, :], jnp.cos(lo)[None], jnp.sin(lo)[None]
    cos = (ch * cl - sh * sl).reshape(s, LANES)
    sin = (sh * cl + ch * sl).reshape(s, LANES)
    return cos, sin * sign[None, :]


def _block_diag_ones(w):
    idx = jnp.arange(w) // HEAD_DIM
    return (idx[:, None] == idx[None, :]).astype(BF16)


def _local_step(x2, tgt2, norm_gain, w_in_bf, q_norm_a, k_norm_a, sinks_a, q_norm_b, k_norm_b, w_out_blk):
    s = x2.shape[0]
    cos, sin_s = _rope_tables(s)
    bd256, bd128 = _block_diag_ones(2 * LANES), _block_diag_ones(A_KV_WIDTH)
    gqa = jnp.tile(q_norm_a, (1, HEADS))
    gka = jnp.tile(k_norm_a, (1, 2))
    gqb = jnp.tile(q_norm_b, (1, HEADS))
    gkb = jnp.tile(k_norm_b, (1, HEADS))
    sink_row = jnp.pad(sinks_a, ((0, 0), (0, LANES - HEADS)))
    perm = _fold_matrix()

    (qa, kva, qb, kvb, qbf, kvbf, qa_raw, ka_raw, g_a, qb_raw, kb_raw, g_b, w_out_bf) = _inproj(
        x2, norm_gain, w_in_bf, cos, sin_s, gqa, gka, gqb, gkb, bd256, bd128, w_out_blk)

    att_a = _attn_fwd(qa, kva, sinks_a, dil=1, max_dist=A_MAX_DIST, name="attn_a_fwd")
    att_b1 = _attn_fwd(qb, kvb, None, dil=1, max_dist=B_MAX_DIST, name="attn_b1_fwd")
    att_b4 = _attn_fwd(qbf, kvbf, None, dil=4, max_dist=B_MAX_DIST, name="attn_b4_fwd")
    att_b16 = _attn_fwd(qbf, kvbf, None, dil=16, max_dist=B_MAX_DIST, name="attn_b16_fwd")

    (dy, do_a, do_b, do_bf, dg_a, dg_b, ld_a, ld_b, ld_bf, gw_out, loss_part, dsink) = _outproj(
        att_a, att_b1, att_b4, att_b16, g_a, g_b, x2, tgt2, w_out_bf, sink_row, perm)

    d_a = _attn_bwd(qa, kva, do_a, ld_a, dil=1, max_dist=A_MAX_DIST, name="attn_a_bwd")
    d_b1 = _attn_bwd(qb, kvb, do_b, ld_b, dil=1, max_dist=B_MAX_DIST, name="attn_b1_bwd")
    d_b4 = _attn_bwd(qbf, kvbf, do_bf, ld_bf, dil=4, max_dist=B_MAX_DIST, name="attn_b4_bwd")
    d_b16 = _attn_bwd(qbf, kvbf, do_bf, ld_bf, dil=16, max_dist=B_MAX_DIST, name="attn_b16_bwd")

    gx, h_t, wins, dgain, dgqa, dgka, dgqb, dgkb = _inproj_bwd(
        x2, dy, norm_gain, w_in_bf, cos, sin_s, gqa, gka, gqb, gkb, bd256, bd128, perm,
        qa_raw, ka_raw, qb_raw, kb_raw, d_a, d_b1, d_b4, d_b16, dg_a, dg_b)
    return loss_part, gx, h_t, wins, gw_out, (dgain, dgqa, dgka, dsink, dgqb, dgkb)


def _position():
    return lax.axis_index("x"), lax.axis_index("y"), lax.axis_index("c")


GATHER_CHUNKS = 2


def _gather_weights(blocks, name):
    n = len(blocks)
    ch = GATHER_CHUNKS
    n_sems = n * (N_CHIP - 1) * ch

    def body(*refs):
        src_refs, dst_refs = refs[:n], refs[n:2 * n]
        ici_send, ici_recv, d2d_send, d2d_recv = refs[2 * n:]
        x, y, c = _position()
        b = 2 * x + y
        for k in range(n):
            dst_refs[k][b] = src_refs[k][...].astype(BF16)

        def rows(k, core, j):
            half = blocks[k].shape[0] // 2
            return pl.ds(core * half + j * (half // ch), half // ch)

        plan = []
        for d in range(1, N_CHIP):
            px, py = x ^ (d >> 1), y ^ (d & 1)
            for j in range(ch):
                for k in range(n):
                    plan.append((px, py, 2 * px + py, k, j, ((d - 1) * ch + j) * n + k))
        sends = []
        for px, py, pb, k, j, sem in plan:
            send = pltpu.make_async_remote_copy(
                src_ref=dst_refs[k].at[b, rows(k, c, j)], dst_ref=dst_refs[k].at[b, rows(k, c, j)],
                send_sem=ici_send.at[sem], recv_sem=ici_recv.at[sem], device_id=(px, py, c), device_id_type=MESH)
            send.start()
            sends.append(send)
        for px, py, pb, k, j, sem in plan:
            landed = dst_refs[k].at[pb, rows(k, c, j)]
            pltpu.make_async_remote_copy(
                src_ref=landed, dst_ref=landed, send_sem=ici_send.at[sem], recv_sem=ici_recv.at[sem],
                device_id=(px, py, c), device_id_type=MESH).wait_recv()
            forward = pltpu.make_async_remote_copy(
                src_ref=landed, dst_ref=landed, send_sem=d2d_send.at[sem], recv_sem=d2d_recv.at[sem],
                device_id=(x, y, 1 - c), device_id_type=MESH)
            forward.start()
            sends.append(forward)
        for px, py, pb, k, j, sem in plan:
            passed = dst_refs[k].at[pb, rows(k, 1 - c, j)]
            pltpu.make_async_remote_copy(
                src_ref=passed, dst_ref=passed, send_sem=d2d_send.at[sem], recv_sem=d2d_recv.at[sem],
                device_id=(x, y, 1 - c), device_id_type=MESH).wait_recv()
        for send in sends:
            send.wait_send()

    vmem_spec = pl.BlockSpec(memory_space=pltpu.VMEM)
    out_shape = tuple(jax.ShapeDtypeStruct((N_CHIP,) + a.shape, BF16) for a in blocks)
    return pl.pallas_call(
        body, name=name, in_specs=[vmem_spec] * n, out_specs=tuple([vmem_spec] * n), out_shape=out_shape,
        scratch_shapes=[pltpu.SemaphoreType.DMA((n_sems,)) for _ in range(4)],
        compiler_params=pltpu.CompilerParams(vmem_limit_bytes=VMEM_LIMIT),
    )(*blocks)


def _grad_reduce(order, h_t, wins, gw_out, small):
    s = h_t.shape[1]
    tk = GRAD_ROWS
    n_i = s // tk
    half = D_MODEL // 2
    o_half = OUT_ROWS // 2
    n_rel = N_CHIP - 1

    def body(order_ref, ht_ref, win_ref, gwo_ref, small_ref,
             win_out, wout_out, small_out,
             acc, mine, s1, r1, s2, r2, so1, ro1, so2, ro2, pair_in, pair_o, small_land,
             s1_send, s1_recv, s2_send, s2_recv, o1_send, o1_recv, o2_send, o2_recv,
             pair_send, pair_recv, small_send, small_recv):
        j = pl.program_id(0)
        i = pl.program_id(1)
        x, y, c = _position()
        me = 4 * x + 2 * y + c
        sibling = (x, y, 1 - c)
        my_rows = pl.ds(pl.multiple_of(c * half, half), half)
        sib_rows = pl.ds(pl.multiple_of((1 - c) * half, half), half)

        def chip_of(rel):
            return x ^ (rel >> 1), y ^ (rel & 1)

        def level1(k):
            return pltpu.make_async_remote_copy(src_ref=s1.at[k], dst_ref=r1.at[k], send_sem=s1_send.at[k],
                                                recv_sem=s1_recv.at[k], device_id=sibling, device_id_type=MESH)

        def level2(k):
            px, py = chip_of(RELATIONS[k])
            return pltpu.make_async_remote_copy(src_ref=s2.at[k], dst_ref=r2.at[k], send_sem=s2_send.at[k],
                                                recv_sem=s2_recv.at[k], device_id=(px, py, c), device_id_type=MESH)

        def out_level1(bk):
            return pltpu.make_async_remote_copy(src_ref=so1.at[bk], dst_ref=ro1.at[bk], send_sem=o1_send.at[bk],
                                                recv_sem=o1_recv.at[bk], device_id=sibling, device_id_type=MESH)

        def out_level2(k):
            px, py = chip_of(RELATIONS[k])
            return pltpu.make_async_remote_copy(src_ref=so2.at[k], dst_ref=ro2.at[k], send_sem=o2_send.at[k],
                                                recv_sem=o2_recv.at[k], device_id=(px, py, c), device_id_type=MESH)

        def small_copy(d):
            px, py, pc = x ^ (d >> 2), y ^ ((d >> 1) & 1), c ^ (d & 1)
            return pltpu.make_async_remote_copy(src_ref=small_ref, dst_ref=small_land.at[me],
                                                send_sem=small_send.at[d], recv_sem=small_recv.at[d],
                                                device_id=(px, py, pc), device_id_type=MESH)

        def pair_copy(k, buf):
            return pltpu.make_async_remote_copy(src_ref=buf.at[0], dst_ref=buf.at[1], send_sem=pair_send.at[k],
                                                recv_sem=pair_recv.at[k], device_id=sibling, device_id_type=MESH)

        def out_rows(bk, core):
            return pl.ds(pl.multiple_of(bk * OUT_ROWS + core * o_half, o_half), o_half)

        @pl.when((j == 0) & (i == 0))
        def _():
            for d in range(1, N_DEV):
                small_copy(d).start()
            small_land[me] = small_ref[...]
            for bk in range(N_CHIP):
                so1[bk] = gwo_ref[out_rows(bk, 1 - c), :].astype(BF16)
                out_level1(bk).start()

        @pl.when((j == 0) & (i == 1))
        def _():
            b = 2 * x + y
            for bk in range(N_CHIP):
                out_level1(bk).wait_recv()
            for k in range(n_rel):
                px, py = chip_of(RELATIONS[k])
                bk = 2 * px + py
                so2[k] = (gwo_ref[out_rows(bk, c), :] + ro1[bk].astype(F32)).astype(BF16)
                out_level2(k).start()

        @pl.when(i == 0)
        def _():
            acc[...] = jnp.zeros_like(acc)

        for n0 in range(0, WIN, ACC_COLS):
            n1 = min(n0 + ACC_COLS, WIN)
            acc[:, n0:n1] += _dot(ht_ref[...], win_ref[:, n0:n1])

        for k in range(N_CHIP):
            @pl.when((j == k) & (i == n_i - 1))
            def _(k=k):
                s1[k] = acc[sib_rows, :].astype(BF16)
                level1(k).start()
                mine[...] = acc[my_rows, :]

            if k < n_rel:
                @pl.when((j == k + 1) & (i == 1))
                def _(k=k):
                    level1(k).wait_recv()
                    s2[k] = (mine[...] + r1[k].astype(F32)).astype(BF16)
                    level2(k).start()

        @pl.when((j == N_CHIP - 1) & (i == n_i - 1))
        def _():
            b = 2 * x + y
            level1(N_CHIP - 1).wait_recv()
            total = mine[...] + r1[N_CHIP - 1].astype(F32)
            for k in range(n_rel):
                level2(k).wait_recv()
                total = total + r2[k].astype(F32)
            total = total.T
            pair_in[0] = total
            pair_copy(0, pair_in).start()
            total_o = gwo_ref[out_rows(b, c), :] + ro1[b].astype(F32)
            for k in range(n_rel):
                out_level2(k).wait_recv()
                total_o = total_o + ro2[k].astype(F32)
            pair_o[0] = total_o
            pair_copy(1, pair_o).start()
            for core in range(2):
                @pl.when(c == core)
                def _(core=core):
                    win_out[:, core * half:(core + 1) * half] = total
            wout_out[c] = total_o
            for d in range(1, N_DEV):
                small_copy(d).wait_recv()
            small_out[...] = small_land[...]
            pair_copy(0, pair_in).wait_recv()
            for core in range(2):
                @pl.when(c == core)
                def _(core=core):
                    win_out[:, (1 - core) * half:(2 - core) * half] = pair_in[1]
            pair_copy(1, pair_o).wait_recv()
            wout_out[1 - c] = pair_o[1]
            for d in range(1, N_DEV):
                small_copy(d).wait_send()
            for k in range(N_CHIP):
                level1(k).wait_send()
                out_level1(k).wait_send()
            for k in range(n_rel):
                level2(k).wait_send()
                out_level2(k).wait_send()
            pair_copy(0, pair_in).wait_send()
            pair_copy(1, pair_o).wait_send()

    vmem = pl.BlockSpec(memory_space=pltpu.VMEM)
    dma = pltpu.SemaphoreType.DMA
    sds = jax.ShapeDtypeStruct
    grid_spec = pltpu.PrefetchScalarGridSpec(
        num_scalar_prefetch=1, grid=(N_CHIP, n_i),
        in_specs=[pl.BlockSpec((D_MODEL, tk), lambda j, i, order: (0, i)),
                  pl.BlockSpec((None, tk, WIN), lambda j, i, order: (order[j], i, 0)), vmem, vmem],
        out_specs=(vmem, vmem, vmem),
        scratch_shapes=[
            pltpu.VMEM((D_MODEL, WIN), F32), pltpu.VMEM((half, WIN), F32),
            pltpu.VMEM((N_CHIP, half, WIN), BF16), pltpu.VMEM((N_CHIP, half, WIN), BF16),
            pltpu.VMEM((n_rel, half, WIN), BF16), pltpu.VMEM((n_rel, half, WIN), BF16),
            pltpu.VMEM((N_CHIP, o_half, D_MODEL), BF16), pltpu.VMEM((N_CHIP, o_half, D_MODEL), BF16),
            pltpu.VMEM((n_rel, o_half, D_MODEL), BF16), pltpu.VMEM((n_rel, o_half, D_MODEL), BF16),
            pltpu.VMEM((2, WIN, half), F32), pltpu.VMEM((2, o_half, D_MODEL), F32),
            pltpu.VMEM((N_DEV, PACK_ROWS, D_MODEL), F32),
            dma((N_CHIP,)), dma((N_CHIP,)), dma((n_rel,)), dma((n_rel,)),
            dma((N_CHIP,)), dma((N_CHIP,)), dma((n_rel,)), dma((n_rel,)),
            dma((2,)), dma((2,)), dma((N_DEV,)), dma((N_DEV,))])
    return pl.pallas_call(
        body, name="grad_w_in_reduce", grid_spec=grid_spec,
        out_shape=(sds((WIN, D_MODEL), F32), sds((2, o_half, D_MODEL), F32), sds((N_DEV, PACK_ROWS, D_MODEL), F32)),
        compiler_params=pltpu.CompilerParams(dimension_semantics=("arbitrary", "arbitrary"),
                                             vmem_limit_bytes=VMEM_LIMIT),
    )(order, h_t, wins, gw_out, small)


ADAM_STEPS = 4


def _adamw_math(w, g, m, v):
    m = ADAM_B1 * m + (1.0 - ADAM_B1) * g
    v = ADAM_B2 * v + (1.0 - ADAM_B2) * (g * g)
    m_hat = m / (1.0 - ADAM_B1 ** ADAM_STEP)
    v_hat = v / (1.0 - ADAM_B2 ** ADAM_STEP)
    delta = -ADAM_LR * (m_hat / (jnp.sqrt(v_hat) + ADAM_EPS) + ADAM_WD * w)
    return delta, m, v


def _adamw(w, g, m, v, name):
    r, c = w.shape

    def body(w_ref, g_ref, m_ref, v_ref, d_ref, nm_ref, nv_ref):
        delta, nm, nv = _adamw_math(w_ref[...], g_ref[...], m_ref[...], v_ref[...])
        d_ref[...] = delta
        nm_ref[...] = nm
        nv_ref[...] = nv

    rows = r // ADAM_STEPS
    assert rows * ADAM_STEPS == r and rows % 8 == 0
    spec = pl.BlockSpec((rows, c), lambda i: (i, 0))
    shape = jax.ShapeDtypeStruct((r, c), F32)
    return pl.pallas_call(
        body, name=name, grid=(ADAM_STEPS,), in_specs=[spec] * 4, out_specs=(spec,) * 3,
        out_shape=(shape,) * 3, compiler_params=pltpu.CompilerParams(vmem_limit_bytes=VMEM_LIMIT),
    )(w, g, m, v)


PACK_ROWS = 8


def _fold_heads(v):
    y = v[:, 0:LANES]
    for j in range(1, v.shape[1] // LANES):
        y = y + v[:, j * LANES:(j + 1) * LANES]
    return y + pltpu.roll(y, HEAD_DIM, 1)


N_SMALL = 6


def _small_adamw(recv, weights, m, v):
    def body(*refs):
        r_ref = refs[0]
        w_refs, m_refs, v_refs = (refs[1 + n * N_SMALL:1 + (n + 1) * N_SMALL] for n in range(3))
        outs = refs[1 + 3 * N_SMALL:]
        g_refs, d_refs, nm_refs, nv_refs = (outs[n * N_SMALL:(n + 1) * N_SMALL] for n in range(4))
        loss_ref = outs[4 * N_SMALL]
        tot = r_ref[0]
        for j in range(1, N_DEV):
            tot = tot + r_ref[j]
        loss_ref[...] = tot[3:4, 0:LANES]
        row1 = tot[1:2, :]
        row2 = tot[2:3, :]
        grads = [tot[0:1, :],
                 _fold_heads(row1[:, 0:WIDTH])[:, :HEAD_DIM],
                 _fold_heads(row2[:, WIDTH:WIDTH + A_KV_WIDTH])[:, :HEAD_DIM],
                 row2[:, WIDTH + A_KV_WIDTH:WIDTH + A_KV_WIDTH + HEADS],
                 _fold_heads(row1[:, WIDTH:2 * WIDTH])[:, :HEAD_DIM],
                 _fold_heads(row2[:, 0:WIDTH])[:, :HEAD_DIM]]
        for n, g in enumerate(grads):
            g_refs[n][...] = g
            delta, nm, nv = _adamw_math(w_refs[n][...], g, m_refs[n][...], v_refs[n][...])
            d_refs[n][...] = delta
            nm_refs[n][...] = nm
            nv_refs[n][...] = nv

    shapes = tuple(jax.ShapeDtypeStruct(a.shape, F32) for a in weights)
    outs = pl.pallas_call(body, name="small_adamw", out_shape=shapes * 4 + (jax.ShapeDtypeStruct((1, LANES), F32),)
                          )(recv, *weights, *m, *v)
    return tuple(outs[n * N_SMALL:(n + 1) * N_SMALL] for n in range(4)) + (outs[4 * N_SMALL],)


def kernel(x, norm_gain, w_in, q_norm_a, k_norm_a, sinks_a, q_norm_b, k_norm_b, w_out, loss_target, m_norm_gain, m_w_in, m_q_norm_a, m_k_norm_a, m_sinks_a, m_q_norm_b, m_k_norm_b, m_w_out, v_norm_gain, v_w_in, v_q_norm_a, v_k_norm_a, v_sinks_a, v_q_norm_b, v_k_norm_b, v_w_out):
    chip = 2 * lax.axis_index("x") + lax.axis_index("y")

    w_in_t, m_w_in_t, v_w_in_t = w_in[0].T, m_w_in[0].T, v_w_in[0].T

    (w_in_all,) = _gather_weights([w_in_t], "gather_weights")
    w_in_bf = w_in_all.reshape(IN_WIDTH, D_MODEL)

    loss_part, gx, h_t, wins, gw_out, (dgain, dgqa, dgka, dsink, dgqb, dgkb) = _local_step(
        x[0], loss_target[0], norm_gain, w_in_bf, q_norm_a, k_norm_a, sinks_a, q_norm_b, k_norm_b, w_out[0])

    small = jnp.concatenate([
        dgain, jnp.concatenate([dgqa, dgqb], axis=1),
        jnp.concatenate([dgkb, dgka, dsink, jnp.zeros((1, D_MODEL - WIDTH - 2 * A_KV_WIDTH), F32)], axis=1),
        jnp.pad(loss_part, ((0, 0), (0, D_MODEL - LANES))),
        jnp.zeros((PACK_ROWS - 4, D_MODEL), F32)], axis=0)
    order = (chip ^ jnp.array(RELATIONS, jnp.int32)).astype(jnp.int32)
    win_sum, wout_sum, small_recv = _grad_reduce(order, h_t, wins, gw_out, small)
    shift = jnp.array(WIN_SHIFT, jnp.int32)[chip]
    g_w_in_t = lax.dynamic_slice_in_dim(win_sum, shift, IN_COLS, axis=0)
    g_w_out = wout_sum.reshape(OUT_ROWS, D_MODEL)

    d_w_in, nm_w_in, nv_w_in = (a.T for a in _adamw(w_in_t, g_w_in_t, m_w_in_t, v_w_in_t, "adamw_w_in"))
    g_w_in = g_w_in_t.T
    d_w_out, nm_w_out, nv_w_out = _adamw(w_out[0], g_w_out, m_w_out[0], v_w_out[0], "adamw_w_out")
    g_s, d_s, nm_s, nv_s, loss_row = _small_adamw(
        small_recv,
        (norm_gain, q_norm_a, k_norm_a, sinks_a, q_norm_b, k_norm_b),
        (m_norm_gain, m_q_norm_a, m_k_norm_a, m_sinks_a, m_q_norm_b, m_k_norm_b),
        (v_norm_gain, v_q_norm_a, v_k_norm_a, v_sinks_a, v_q_norm_b, v_k_norm_b))
    loss = loss_row[0, 0]

    def leaves(small_ones, big_in, big_out):
        return (small_ones[0], big_in[None]) + tuple(small_ones[1:]) + (big_out[None],)

    return ((loss, gx[None]) + leaves(g_s, g_w_in, g_w_out) + leaves(d_s, d_w_in, d_w_out)
            + leaves(nm_s, nm_w_in, nm_w_out) + leaves(nv_s, nv_w_in, nv_w_out))
```

```python
import jax
import jax.numpy as jnp
from jax import lax
from jax.experimental import pallas as pl
from jax.experimental.pallas import tpu as pltpu

F32 = jnp.float32
BF16 = jnp.bfloat16

D_MODEL = 1024
HEAD_DIM = 64
HEADS = 8
WIDTH = HEADS * HEAD_DIM
A_KV_WIDTH = 2 * HEAD_DIM
BLOCK = 128
LANES = 128
FOLD = 16
A_MAX_DIST = 127
B_MAX_DIST = 128
ROPE_THETA = 10000.0
ROPE_SPLIT = 64
EPS = 1e-6
NEG = -1e30
SCALE = HEAD_DIM ** -0.5

IN_WIDTH = 3328
C_QA, C_KA, C_VA, C_GA, C_QB, C_KB, C_VB, C_GB, C_END = 0, 512, 640, 768, 1280, 1792, 2304, 2816, 3328

N_DEV = 8
N_CHIP = 4
MESH = pl.DeviceIdType.MESH
IN_COLS = IN_WIDTH // N_CHIP
WIN = 896
WIN_START = (0, 768, 1664, 2432)
WIN_SHIFT = (0, 64, 0, 64)
OUT_ROWS = D_MODEL // N_CHIP
RELATIONS = (3, 1, 2, 0)

ADAM_LR = 0.001
ADAM_B1 = 0.9
ADAM_B2 = 0.999
ADAM_EPS = 1e-08
ADAM_WD = 0.01
ADAM_STEP = 10

ROW_TILE = 256
FOLD_ROWS = ROW_TILE // FOLD
GRAD_ROWS = 1024
ACC_COLS = 256
VMEM_LIMIT = 56 * 1024 * 1024


def _dot(a, b):
    return jnp.dot(a, b, preferred_element_type=F32)


def _dot_nt(a, b):
    return lax.dot_general(a, b, (((1,), (1,)), ((), ())), preferred_element_type=F32)


def _dot_tn(a, b):
    return lax.dot_general(a, b, (((0,), (0,)), ((), ())), preferred_element_type=F32)


def _head_sum(z, bd):
    w = bd.shape[0]
    zb = z.astype(BF16)
    parts = [_dot(zb[:, a:a + w], bd) for a in range(0, z.shape[1], w)]
    return parts[0] if len(parts) == 1 else jnp.concatenate(parts, axis=1)


def _swap_halves(t):
    w = t.shape[1]
    lane = lax.broadcasted_iota(jnp.int32, t.shape, 1)
    return jnp.where(lane % HEAD_DIM < HEAD_DIM // 2, pltpu.roll(t, w - 32, 1), pltpu.roll(t, 32, 1))


def _qknorm_rope(t, g, cos, sin_s, bd):
    r = lax.rsqrt(_head_sum(t * t, bd) * (1.0 / HEAD_DIM) + EPS)
    n = (t * r) * g
    return n * cos + _swap_halves(n) * sin_s


def _qknorm_rope_bwd(dout, t, g, cos, sin_s, bd):
    dout, t = dout.astype(F32), t.astype(F32)
    dn = dout * cos + _swap_halves(dout * sin_s)
    r = lax.rsqrt(_head_sum(t * t, bd) * (1.0 / HEAD_DIM) + EPS)
    tr = t * r
    u = dn * g
    dt = r * (u - tr * (_head_sum(u * tr, bd) * (1.0 / HEAD_DIM)))
    return dt, dn * tr


def _sigmoid(g):
    return 1.0 / (1.0 + jnp.exp(-g))


def _expand_heads(st):
    t = st.shape[0]
    lane = lax.broadcasted_iota(jnp.int32, (t, LANES), 1)
    chunks = []
    for c in range(WIDTH // LANES):
        chunks.append(jnp.where(lane < HEAD_DIM, st[:, 2 * c:2 * c + 1], st[:, 2 * c + 1:2 * c + 2]))
    return jnp.concatenate(chunks, axis=1)


def _reduce_heads(z):
    t = z.shape[0]
    lane = lax.broadcasted_iota(jnp.int32, (t, LANES), 1)
    out = jnp.zeros((t, LANES), F32)
    for c in range(WIDTH // LANES):
        zc = z[:, c * LANES:(c + 1) * LANES]
        for ph in range(2):
            s = jnp.sum(jnp.where((lane // HEAD_DIM) == ph, zc, 0.0), axis=-1, keepdims=True)
            out = jnp.where(lane == 2 * c + ph, s, out)
    return out


def _fold_scratch(w):
    return pltpu.VMEM((w // LANES, ROW_TILE, LANES), F32)


def _store_folded(out_ref, val, scr, col0=0):
    w = val.shape[1]
    n = w // LANES
    for c in range(n):
        scr[c] = val[:, c * LANES:(c + 1) * LANES]
    for r in range(FOLD):
        piece = [scr[c, pl.ds(r, FOLD_ROWS, stride=FOLD), :] for c in range(n)]
        out_ref[r, :, col0:col0 + w] = (piece[0] if n == 1 else jnp.concatenate(piece, axis=1)).astype(out_ref.dtype)


def _load_folded(in_ref, scr):
    n = in_ref.shape[2] // LANES
    for r in range(FOLD):
        blk = in_ref[r].astype(F32)
        for c in range(n):
            scr[c, pl.ds(r, FOLD_ROWS, stride=FOLD), :] = blk[:, c * LANES:(c + 1) * LANES]
    return scr[0] if n == 1 else jnp.concatenate([scr[c] for c in range(n)], axis=1)


def _fold_matrix():
    f = jnp.arange(ROW_TILE)
    return (jnp.arange(ROW_TILE)[None, :] == (FOLD * (f % FOLD_ROWS) + f // FOLD_ROWS)[:, None]).astype(BF16)


def _store_folded_bf16(out_ref, val, perm):
    folded = _dot(perm, val.astype(BF16)).astype(out_ref.dtype)
    for r in range(FOLD):
        out_ref[r] = folded[r * FOLD_ROWS:(r + 1) * FOLD_ROWS]


def _load_folded_bf16(in_ref, perm):
    blk = jnp.concatenate([in_ref[r] for r in range(FOLD)], axis=0)
    return _dot(perm, blk)


def _rows(w, tm=ROW_TILE):
    return pl.BlockSpec((tm, w), lambda i: (i, 0))


def _folded_rows(w):
    return pl.BlockSpec((FOLD, FOLD_ROWS, w), lambda i: (0, i, 0))


def _whole(shape):
    return pl.BlockSpec(shape, lambda i: (0,) * len(shape))


def _inproj(x2, gain, w_bf, cos, sin_s, gqa, gka, gqb, gkb, bd256, bd128, w_out_blk):
    s = x2.shape[0]
    tm = ROW_TILE
    n_steps = s // tm
    n_rel = N_CHIP - 1
    o_half = OUT_ROWS // 2

    def body(x_ref, gain_ref, w_hbm, cos_ref, sin_ref, gqa_ref, gka_ref, gqb_ref, gkb_ref, bd256_ref, bd128_ref,
             wout_ref, qa_ref, kva_ref, qb_ref, kvb_ref, qbf_ref, kvbf_ref,
             qa_raw_ref, ka_raw_ref, ga_ref, qb_raw_ref, kb_raw_ref, gb_ref, wout_all_ref,
             w_vmem, scr, land, ici_send, ici_recv, d2d_send, d2d_recv):
        i = pl.program_id(0)
        px_, py_, c = _position()
        b = 2 * px_ + py_

        def piece(chip_idx, core):
            return land.at[chip_idx, pl.ds(pl.multiple_of(core * o_half, o_half), o_half)]

        def other_chip(d):
            ox, oy = px_ ^ (d >> 1), py_ ^ (d & 1)
            return ox, oy, 2 * ox + oy

        def ici_copy(d, chip_idx):
            ox, oy, _ = other_chip(d)
            return pltpu.make_async_remote_copy(
                src_ref=piece(chip_idx, c), dst_ref=piece(chip_idx, c), send_sem=ici_send.at[d - 1],
                recv_sem=ici_recv.at[d - 1], device_id=(ox, oy, c), device_id_type=MESH)

        def d2d_copy(d, core):
            return pltpu.make_async_remote_copy(
                src_ref=piece(other_chip(d)[2], core), dst_ref=piece(other_chip(d)[2], core),
                send_sem=d2d_send.at[d - 1], recv_sem=d2d_recv.at[d - 1], device_id=(px_, py_, 1 - c),
                device_id_type=MESH)

        @pl.when(i == 0)
        def _():
            pltpu.sync_copy(w_hbm, w_vmem)
            land[b] = wout_ref[...].astype(BF16)
            for d in range(1, N_CHIP):
                ici_copy(d, b).start()

        @pl.when(i == n_steps // 2)
        def _():
            for d in range(1, N_CHIP):
                ici_copy(d, other_chip(d)[2]).wait_recv()
                d2d_copy(d, c).start()

        @pl.when(i == n_steps - 1)
        def _():
            for d in range(1, N_CHIP):
                d2d_copy(d, 1 - c).wait_recv()
            for d in range(1, N_CHIP):
                ici_copy(d, b).wait_send()
                d2d_copy(d, c).wait_send()
            for k in range(N_CHIP):
                wout_all_ref[k * OUT_ROWS:(k + 1) * OUT_ROWS, :] = land[k]

        xt = x_ref[...]
        r = lax.rsqrt(jnp.mean(xt * xt, axis=-1, keepdims=True) + EPS)
        h = ((xt * r) * gain_ref[...]).astype(BF16)
        cos1 = cos_ref[...]
        sin1 = sin_ref[...]
        cos4 = jnp.tile(cos1, (1, 4))
        sin4 = jnp.tile(sin1, (1, 4))

        def seg(a, b):
            return _dot_nt(h, w_vmem[a:b, :])

        t = seg(C_QA, C_KA)
        qa_raw_ref[...] = t.astype(BF16)
        qa_ref[...] = (_qknorm_rope(t, gqa_ref[...], cos4, sin4, bd256_ref[...]) * SCALE).astype(BF16)
        t = seg(C_KA, C_VA)
        ka_raw_ref[...] = t.astype(BF16)
        kva_ref[:, :A_KV_WIDTH] = _qknorm_rope(t, gka_ref[...], cos1, sin1, bd128_ref[...]).astype(BF16)
        kva_ref[:, A_KV_WIDTH:] = seg(C_VA, C_GA).astype(BF16)
        ga_ref[...] = seg(C_GA, C_QB).astype(BF16)
        t = seg(C_QB, C_KB)
        qb_raw_ref[...] = t.astype(BF16)
        t = _qknorm_rope(t, gqb_ref[...], cos4, sin4, bd256_ref[...]) * SCALE
        qb_ref[...] = t.astype(BF16)
        _store_folded(qbf_ref, t, scr)
        t = seg(C_KB, C_VB)
        kb_raw_ref[...] = t.astype(BF16)
        t = _qknorm_rope(t, gkb_ref[...], cos4, sin4, bd256_ref[...])
        kvb_ref[:, :WIDTH] = t.astype(BF16)
        _store_folded(kvbf_ref, t, scr)
        t = seg(C_VB, C_GB)
        kvb_ref[:, WIDTH:] = t.astype(BF16)
        _store_folded(kvbf_ref, t, scr, WIDTH)
        gb_ref[...] = seg(C_GB, C_END).astype(BF16)

    sds = jax.ShapeDtypeStruct
    ln = s // FOLD
    out_shape = (sds((s, WIDTH), BF16), sds((s, 2 * A_KV_WIDTH), BF16), sds((s, WIDTH), BF16),
                 sds((s, 2 * WIDTH), BF16), sds((FOLD, ln, WIDTH), BF16), sds((FOLD, ln, 2 * WIDTH), BF16),
                 sds((s, WIDTH), BF16), sds((s, A_KV_WIDTH), BF16), sds((s, WIDTH), BF16),
                 sds((s, WIDTH), BF16), sds((s, WIDTH), BF16), sds((s, WIDTH), BF16),
                 sds((D_MODEL, D_MODEL), BF16))
    out_specs = (_rows(WIDTH), _rows(2 * A_KV_WIDTH), _rows(WIDTH), _rows(2 * WIDTH),
                 _folded_rows(WIDTH), _folded_rows(2 * WIDTH),
                 _rows(WIDTH), _rows(A_KV_WIDTH), _rows(WIDTH), _rows(WIDTH), _rows(WIDTH), _rows(WIDTH),
                 _whole((D_MODEL, D_MODEL)))
    dma = pltpu.SemaphoreType.DMA
    return pl.pallas_call(
        body, name="inproj_fwd", grid=(n_steps,),
        in_specs=[_rows(D_MODEL), _whole(gain.shape), pl.BlockSpec(memory_space=pl.ANY), _rows(LANES), _rows(LANES),
                  _whole(gqa.shape), _whole(gka.shape), _whole(gqb.shape), _whole(gkb.shape), _whole(bd256.shape),
                  _whole(bd128.shape), _whole(w_out_blk.shape)],
        out_specs=out_specs, out_shape=out_shape,
        scratch_shapes=[pltpu.VMEM((IN_WIDTH, D_MODEL), BF16), _fold_scratch(WIDTH),
                        pltpu.VMEM((N_CHIP, OUT_ROWS, D_MODEL), BF16),
                        dma((n_rel,)), dma((n_rel,)), dma((n_rel,)), dma((n_rel,))],
        compiler_params=pltpu.CompilerParams(dimension_semantics=("arbitrary",), vmem_limit_bytes=VMEM_LIMIT),
    )(x2, gain, w_bf, cos, sin_s, gqa, gka, gqb, gkb, bd256, bd128, w_out_blk)


def _seq_pos(idx, dil):
    if dil == 4:
        return 4 * (idx % 32) + idx // 32
    return idx


SOFTMAX_ROWS = 64


def _upper_mask(dil, r0=0, rows=2 * BLOCK):
    qi = (lax.broadcasted_iota(jnp.int32, (rows, BLOCK), 0) + r0) % BLOCK
    kj = lax.broadcasted_iota(jnp.int32, (rows, BLOCK), 1)
    return _seq_pos(kj, dil) > _seq_pos(qi, dil)


def _eye_mask(r0=0, rows=2 * BLOCK):
    qi = (lax.broadcasted_iota(jnp.int32, (rows, BLOCK), 0) + r0) % BLOCK
    kj = lax.broadcasted_iota(jnp.int32, (rows, BLOCK), 1)
    return qi == kj


def _stack_heads(a2, c, gqa):
    lane = lax.broadcasted_iota(jnp.int32, (1, LANES), 1) // HEAD_DIM
    zero = jnp.zeros_like(a2)
    if gqa:
        keep = lane == (c // 2)
        return jnp.concatenate([jnp.where(keep, a2, zero), jnp.where(keep, _swap_heads(a2), zero)], axis=0)
    return jnp.concatenate([jnp.where(lane == 0, a2, zero), jnp.where(lane == 1, a2, zero)], axis=0)


def _unstack_heads(a, c, gqa):
    lane = lax.broadcasted_iota(jnp.int32, (1, LANES), 1) // HEAD_DIM
    if gqa:
        return jnp.where(lane == (c // 2), a[:BLOCK], _swap_heads(a[BLOCK:]))
    return jnp.where(lane == 0, a[:BLOCK], a[BLOCK:])


def _stacked_head_ids(c, gqa):
    if gqa:
        return 2 * c + c // 2, 2 * c + 1 - c // 2
    return 2 * c, 2 * c + 1


def _per_head_rows(blk, heads):
    return jnp.concatenate([blk[:, heads[0]:heads[0] + 1], blk[:, heads[1]:heads[1] + 1]], axis=0)


def _attn_view(a, dil):
    if dil == 1:
        return a[None]
    if dil == 4:
        return a.reshape(4, 4, a.shape[1], a.shape[2])
    return a


def _attn_unview(a, dil):
    if dil == 1:
        return a[0]
    if dil == 4:
        return a.reshape(FOLD, a.shape[2], a.shape[3])
    return a


ATTN_BLOCKS_PER_STEP = 8


def _attn_specs(dil):
    if dil == 4:
        def spec(n, fn):
            return lambda w: pl.BlockSpec((4, None, n * BLOCK // 4, w), lambda r, i: (0, r, fn(i), 0))
    else:
        def spec(n, fn):
            return lambda w: pl.BlockSpec((None, n * BLOCK, w), lambda r, i: (r, fn(i), 0))
    return spec


def _blk_rows(g, dil):
    n = BLOCK // 4 if dil == 4 else BLOCK
    if isinstance(g, int):
        return slice(g * n, (g + 1) * n)
    return pl.ds(pl.multiple_of(g * n, n), n)


def _blk_load(ref, sl, dil, g=0):
    if dil == 4:
        return ref[:, _blk_rows(g, dil), sl].reshape(BLOCK, sl.stop - sl.start)
    return ref[_blk_rows(g, dil), sl]


def _blk_store(ref, sl, val, dil, g=0):
    val = val.astype(ref.dtype)
    if dil == 4:
        ref[:, _blk_rows(g, dil), sl] = val.reshape(4, BLOCK // 4, sl.stop - sl.start)
    else:
        ref[_blk_rows(g, dil), sl] = val


def _swap_heads(a):
    return pltpu.roll(a.astype(F32), HEAD_DIM, 1).astype(a.dtype)


STAT_SHIFT = 8


def _attn_fwd(q, kv, sinks, *, dil, max_dist, name):
    q, kv = _attn_view(q, dil), _attn_view(kv, dil)
    kw = kv.shape[-1] // 2
    gqa = kw == A_KV_WIDTH
    n_seq = dil
    nb = (q.shape[-2] * (4 if dil == 4 else 1)) // BLOCK
    per_step = min(ATTN_BLOCKS_PER_STEP, nb)
    with_sinks = sinks is not None
    all_lanes = slice(0, LANES)
    assert max_dist in (BLOCK - 1, BLOCK) and nb % per_step == 0
    diag = max_dist == BLOCK

    def body(*refs):
        if with_sinks:
            q_ref, kvp_ref, kvc_ref, sink_ref, o_ref, ml_ref = refs
        else:
            q_ref, kvp_ref, kvc_ref, o_ref, ml_ref = refs

        def block(g, has_prev):
            prev_ref, prev_g = (kvp_ref, 0) if isinstance(g, int) else (kvc_ref, g - 1)
            lane = lax.broadcasted_iota(jnp.int32, (1, LANES), 1)
            with_diag = diag and has_prev
            upper, eye = _upper_mask(dil), _eye_mask()
            first_rows = lax.broadcasted_iota(jnp.int32, (2 * BLOCK, 1), 0) < BLOCK
            ml_blk = jnp.zeros((BLOCK, LANES), F32)
            chunks = range(WIDTH // LANES)
            scores, values = [], []
            for c in chunks:
                sl = slice(c * LANES, (c + 1) * LANES)
                ksl = slice(0, LANES) if gqa else sl
                vsl = slice(ksl.start + kw, ksl.stop + kw)
                kcur, vcur = _blk_load(kvc_ref, ksl, dil, g), _blk_load(kvc_ref, vsl, dil, g)
                qs = _stack_heads(_blk_load(q_ref, sl, dil, g), c, gqa)
                if has_prev:
                    kcur = jnp.concatenate([_blk_load(prev_ref, ksl, dil, prev_g), kcur], axis=0)
                    vcur = jnp.concatenate([_blk_load(prev_ref, vsl, dil, prev_g), vcur], axis=0)
                scores.append(_dot_nt(qs, kcur))
                values.append(vcur)
            probs = []
            for c in chunks:
                heads = _stacked_head_ids(c, gqa)
                s = scores[c]
                if has_prev:
                    s_p = s[:, :BLOCK]
                    sc = jnp.where(upper, s_p, s[:, BLOCK:])
                else:
                    sc = jnp.where(upper, NEG, s)
                if with_diag:
                    sd = jnp.where(eye, s_p, NEG)
                    m = jnp.max(jnp.maximum(sc, sd), axis=-1, keepdims=True)
                else:
                    m = jnp.max(sc, axis=-1, keepdims=True)
                if with_sinks:
                    sk = jnp.where(first_rows, sink_ref[0, heads[0]], sink_ref[0, heads[1]])
                    m = jnp.maximum(m, sk)
                p = jnp.exp(sc - m)
                zero = jnp.zeros_like(p)
                if with_diag:
                    pd = jnp.exp(sd - m)
                    l = jnp.sum(p + pd, axis=-1, keepdims=True)
                else:
                    pd = zero
                    l = jnp.sum(p, axis=-1, keepdims=True)
                if with_sinks:
                    l = l + jnp.exp(sk - m)
                pf = jnp.where(upper, zero, p)
                if has_prev:
                    pf = jnp.concatenate([jnp.where(upper, p, pd), pf], axis=1)
                probs.append(pf.astype(BF16))
                for n, h in enumerate(heads):
                    rows = slice(n * BLOCK, (n + 1) * BLOCK)
                    ml_blk = jnp.where(lane == h, m[rows], ml_blk)
                    ml_blk = jnp.where(lane == h + STAT_SHIFT, l[rows], ml_blk)
            for c in chunks:
                sl = slice(c * LANES, (c + 1) * LANES)
                _blk_store(o_ref, sl, _unstack_heads(_dot(probs[c], values[c]), c, gqa), dil, g)
            _blk_store(ml_ref, all_lanes, ml_blk, dil, g)

        @pl.when(pl.program_id(1) == 0)
        def _():
            block(0, False)

        @pl.when(pl.program_id(1) > 0)
        def _():
            block(0, True)

        if per_step > 1:
            def rest(g, carry):
                block(g, True)
                return carry

            lax.fori_loop(1, per_step, rest, 0)

    spec = _attn_specs(dil)
    cur = spec(per_step, lambda i: i)
    prev = spec(1, lambda i: jnp.maximum(i * per_step - 1, 0))
    in_specs = [cur(WIDTH), prev(2 * kw), cur(2 * kw)]
    args = [q, kv, kv]
    if with_sinks:
        in_specs.append(pl.BlockSpec(memory_space=pltpu.SMEM))
        args.append(sinks)
    stats = jax.ShapeDtypeStruct(q.shape[:-1] + (LANES,), F32)
    o, ml = pl.pallas_call(
        body, name=name, grid=(n_seq, nb // per_step), in_specs=in_specs,
        out_specs=(cur(WIDTH), cur(LANES)),
        out_shape=(jax.ShapeDtypeStruct(q.shape, BF16), stats),
        compiler_params=pltpu.CompilerParams(dimension_semantics=("arbitrary", "arbitrary"),
                                             vmem_limit_bytes=VMEM_LIMIT),
    )(*args)
    return _attn_unview(o, dil), _attn_unview(ml, dil)


def _attn_bwd(q, kv, do, ld, *, dil, max_dist, name):
    q, kv, do, ld = (_attn_view(a, dil) for a in (q, kv, do, ld))
    kw = kv.shape[-1] // 2
    gqa = kw == A_KV_WIDTH
    n_seq = dil
    nb = (q.shape[-2] * (4 if dil == 4 else 1)) // BLOCK
    n_kc = kw // LANES
    per_step = min(ATTN_BLOCKS_PER_STEP, nb)
    all_lanes = slice(0, LANES)
    assert max_dist in (BLOCK - 1, BLOCK) and nb % per_step == 0
    diag = max_dist == BLOCK

    def body(q_ref, kvp_ref, kvc_ref, do_ref, ld_ref, dq_ref, dkv_ref, ck_ref, cv_ref):
        i = pl.program_id(1)

        def block(g, has_prev):
            prev_ref, prev_g = (kvp_ref, 0) if isinstance(g, int) else (kvc_ref, g - 1)
            seq_blk = i * per_step + g
            upper, eye = _upper_mask(dil), _eye_mask()
            ld_blk = _blk_load(ld_ref, all_lanes, dil, g)
            dk_acc = [None] * n_kc
            dv_acc = [None] * n_kc
            chunks = range(WIDTH // LANES)
            operands, products = [], []
            for c in chunks:
                sl = slice(c * LANES, (c + 1) * LANES)
                kc = 0 if gqa else c
                ksl = slice(kc * LANES, (kc + 1) * LANES)
                vsl = slice(ksl.start + kw, ksl.stop + kw)
                k2, v2 = _blk_load(kvc_ref, ksl, dil, g), _blk_load(kvc_ref, vsl, dil, g)
                if has_prev:
                    k2 = jnp.concatenate([_blk_load(prev_ref, ksl, dil, prev_g), k2], axis=0)
                    v2 = jnp.concatenate([_blk_load(prev_ref, vsl, dil, prev_g), v2], axis=0)
                qs = _stack_heads(_blk_load(q_ref, sl, dil, g), c, gqa)
                dos = _stack_heads(_blk_load(do_ref, sl, dil, g), c, gqa)
                operands.append((qs, dos, k2))
                products.append((_dot_nt(qs, k2), _dot_nt(dos, v2)))
            weights = []
            for c in chunks:
                heads = _stacked_head_ids(c, gqa)
                lse2 = _per_head_rows(ld_blk, heads)
                dl2 = _per_head_rows(ld_blk, tuple(h + STAT_SHIFT for h in heads))
                s, dp = products[c]
                if has_prev:
                    s_p, dp_p = s[:, :BLOCK], dp[:, :BLOCK]
                    sc = jnp.where(upper, s_p, s[:, BLOCK:])
                    dpc = jnp.where(upper, dp_p, dp[:, BLOCK:])
                else:
                    sc = jnp.where(upper, NEG, s)
                    dpc = dp
                p = jnp.exp(sc - lse2)
                ds = p * (dpc - dl2)
                zero = jnp.zeros_like(p)
                pf = jnp.where(upper, zero, p)
                dsf = jnp.where(upper, zero, ds)
                if has_prev:
                    if diag:
                        pd = jnp.exp(jnp.where(eye, s_p, NEG) - lse2)
                        dsd = pd * (dp_p - dl2)
                    else:
                        pd = dsd = zero
                    pf = jnp.concatenate([jnp.where(upper, p, pd), pf], axis=1)
                    dsf = jnp.concatenate([jnp.where(upper, ds, dsd), dsf], axis=1)
                weights.append((pf.astype(BF16), dsf.astype(BF16)))
            for c in chunks:
                sl = slice(c * LANES, (c + 1) * LANES)
                kc = 0 if gqa else c
                qs, dos, k2 = operands[c]
                pf, dsf = weights[c]
                _blk_store(dq_ref, sl, _unstack_heads(_dot(dsf, k2), c, gqa) * SCALE, dil, g)
                dk2 = _dot_tn(dsf, qs)
                dv2 = _dot_tn(pf, dos)
                dk_acc[kc] = dk2 if dk_acc[kc] is None else dk_acc[kc] + dk2
                dv_acc[kc] = dv2 if dv_acc[kc] is None else dv_acc[kc] + dv2
            for kc in range(n_kc):
                sl = slice(kc * LANES, (kc + 1) * LANES)
                vsl = slice(sl.start + kw, sl.stop + kw)
                if has_prev:
                    _blk_store(dkv_ref, sl, ck_ref[:, sl] + dk_acc[kc][:BLOCK], dil, seq_blk - 1)
                    _blk_store(dkv_ref, vsl, cv_ref[:, sl] + dv_acc[kc][:BLOCK], dil, seq_blk - 1)
                    ck_ref[:, sl] = dk_acc[kc][BLOCK:]
                    cv_ref[:, sl] = dv_acc[kc][BLOCK:]
                else:
                    ck_ref[:, sl] = dk_acc[kc]
                    cv_ref[:, sl] = dv_acc[kc]

        @pl.when(i == 0)
        def _():
            block(0, False)

        @pl.when(i > 0)
        def _():
            block(0, True)

        if per_step > 1:
            def rest(g, carry):
                block(g, True)
                return carry

            lax.fori_loop(1, per_step, rest, 0)

        @pl.when(i == nb // per_step - 1)
        def _():
            for kc in range(n_kc):
                sl = slice(kc * LANES, (kc + 1) * LANES)
                _blk_store(dkv_ref, sl, ck_ref[:, sl], dil, nb - 1)
                _blk_store(dkv_ref, slice(sl.start + kw, sl.stop + kw), cv_ref[:, sl], dil, nb - 1)

    spec = _attn_specs(dil)
    cur = spec(per_step, lambda i: i)
    prev = spec(1, lambda i: jnp.maximum(i * per_step - 1, 0))
    if dil == 4:
        whole = pl.BlockSpec((4, None, kv.shape[2], 2 * kw), lambda r, i: (0, r, 0, 0))
    else:
        whole = pl.BlockSpec((None, kv.shape[1], 2 * kw), lambda r, i: (r, 0, 0))
    sds = jax.ShapeDtypeStruct
    dq, dkv = pl.pallas_call(
        body, name=name, grid=(n_seq, nb // per_step),
        in_specs=[cur(WIDTH), prev(2 * kw), cur(2 * kw), cur(WIDTH), cur(LANES)],
        out_specs=(cur(WIDTH), whole),
        out_shape=(sds(q.shape, BF16), sds(kv.shape, BF16)),
        scratch_shapes=[pltpu.VMEM((BLOCK, kw), F32), pltpu.VMEM((BLOCK, kw), F32)],
        compiler_params=pltpu.CompilerParams(dimension_semantics=("arbitrary", "arbitrary"),
                                             vmem_limit_bytes=VMEM_LIMIT),
    )(q, kv, kv, do, ld)
    return _attn_unview(dq, dil), _attn_unview(dkv, dil)


def _outproj(att_a, att_b1, att_b4, att_b16, g_a, g_b, x2, tgt2, w_out_bf, sink_row, perm):
    s = x2.shape[0]
    tm = ROW_TILE

    def body(oa_ref, mla_ref, ob1_ref, ml1_ref, ob4_ref, ml4_ref, ob16_ref, ml16_ref,
             ga_ref, gb_ref, x_ref, t_ref, w_ref, sink_ref, perm_ref,
             dy_ref, doa_ref, dob_ref, dobf_ref, dga_ref, dgb_ref, lda_ref, ldb_ref, ldbf_ref,
             gw_ref, loss_ref, dsink_ref, scr_st):
        i = pl.program_id(0)
        perm = perm_ref[...]
        lane = lax.broadcasted_iota(jnp.int32, (tm, LANES), 1)
        used = lane < HEADS

        def split(ml):
            return jnp.where(used, ml, 0.0), jnp.where(used, pltpu.roll(ml, LANES - STAT_SHIFT, 1), 1.0)

        @pl.when(i == 0)
        def _():
            gw_ref[...] = jnp.zeros_like(gw_ref)
            loss_ref[...] = jnp.zeros_like(loss_ref)
            dsink_ref[...] = jnp.zeros_like(dsink_ref)

        ms, ls = zip(split(ml1_ref[...]), split(_load_folded(ml4_ref, scr_st)), split(_load_folded(ml16_ref, scr_st)))
        mx = jnp.maximum(jnp.maximum(ms[0], ms[1]), ms[2])
        scale = [jnp.exp(mp - mx) for mp in ms]
        den = (ls[0] * scale[0] + ls[1] * scale[1]) + ls[2] * scale[2]
        lse_b = jnp.where(used, mx + jnp.log(den), 0.0)
        inv_den = 1.0 / den
        o_b = _expand_heads(scale[0] * inv_den) * ob1_ref[...].astype(F32)
        o_b = o_b + _expand_heads(scale[1] * inv_den) * _load_folded_bf16(ob4_ref, perm)
        o_b = o_b + _expand_heads(scale[2] * inv_den) * _load_folded_bf16(ob16_ref, perm)
        m_a, l_a = split(mla_ref[...])
        lse_a = jnp.where(used, m_a + jnp.log(l_a), 0.0)
        o_a = _expand_heads(1.0 / l_a) * oa_ref[...].astype(F32)
        g_a = ga_ref[...].astype(F32)
        g_b = gb_ref[...].astype(F32)
        sg_a = _sigmoid(g_a)
        sg_b = _sigmoid(g_b)
        silu_a = g_a * sg_a
        silu_b = g_b * sg_b
        mixed = jnp.concatenate([o_a * silu_a, o_b * silu_b], axis=1).astype(BF16)
        w = w_ref[...]
        y = x_ref[...] + _dot(mixed, w)
        diff = y - t_ref[...]
        loss_ref[...] += (0.5 / D_MODEL) * jnp.sum(diff * diff)
        dy = diff * (1.0 / D_MODEL)
        dy_ref[...] = dy
        dyb = dy.astype(BF16)
        gw_ref[...] += _dot_tn(mixed, dyb)
        dmixed = _dot_nt(dyb, w)
        dm_a = dmixed[:, :WIDTH]
        dm_b = dmixed[:, WIDTH:]
        do_a = dm_a * silu_a
        do_b = dm_b * silu_b
        doa_ref[...] = do_a.astype(BF16)
        dob_ref[...] = do_b.astype(BF16)
        _store_folded_bf16(dobf_ref, do_b, perm)
        dga_ref[...] = (dm_a * o_a * (sg_a * (1.0 + g_a * (1.0 - sg_a)))).astype(BF16)
        dgb_ref[...] = (dm_b * o_b * (sg_b * (1.0 + g_b * (1.0 - sg_b)))).astype(BF16)
        dl_a = _reduce_heads(do_a * o_a)
        dl_b = _reduce_heads(do_b * o_b)
        lda_ref[...] = lse_a + pltpu.roll(dl_a, STAT_SHIFT, 1)
        ld_b = lse_b + pltpu.roll(dl_b, STAT_SHIFT, 1)
        ldb_ref[...] = ld_b
        _store_folded(ldbf_ref, ld_b, scr_st)
        dsink_ref[...] -= jnp.sum(jnp.exp(sink_ref[...] - lse_a) * dl_a, axis=0, keepdims=True)

    sds = jax.ShapeDtypeStruct
    ln = s // FOLD
    natural = [_rows(WIDTH), _rows(LANES)]
    folded = [_folded_rows(WIDTH), _folded_rows(LANES)]
    return pl.pallas_call(
        body, name="outproj_fwd_bwd", grid=(s // tm,),
        in_specs=natural + natural + folded + folded
                 + [_rows(WIDTH), _rows(WIDTH), _rows(D_MODEL), _rows(D_MODEL), _whole((D_MODEL, D_MODEL)),
                    _whole((1, LANES)), _whole(perm.shape)],
        out_specs=(_rows(D_MODEL), _rows(WIDTH), _rows(WIDTH), _folded_rows(WIDTH), _rows(WIDTH), _rows(WIDTH),
                   _rows(LANES), _rows(LANES), _folded_rows(LANES),
                   _whole((D_MODEL, D_MODEL)), _whole((1, LANES)), _whole((1, LANES))),
        out_shape=(sds((s, D_MODEL), F32), sds((s, WIDTH), BF16), sds((s, WIDTH), BF16),
                   sds((FOLD, ln, WIDTH), BF16), sds((s, WIDTH), BF16), sds((s, WIDTH), BF16),
                   sds((s, LANES), F32), sds((s, LANES), F32), sds((FOLD, ln, LANES), F32),
                   sds((D_MODEL, D_MODEL), F32), sds((1, LANES), F32), sds((1, LANES), F32)),
        scratch_shapes=[_fold_scratch(LANES)],
        compiler_params=pltpu.CompilerParams(dimension_semantics=("arbitrary",), vmem_limit_bytes=VMEM_LIMIT),
    )(*att_a, *att_b1, *att_b4, *att_b16, g_a, g_b, x2, tgt2, w_out_bf, sink_row, perm)


def _inproj_bwd(x2, dy, gain, w_bf, cos, sin_s, gqa, gka, gqb, gkb, bd256, bd128, perm,
                qa_raw, ka_raw, qb_raw, kb_raw, d_a, d_b1, d_b4, d_b16, dg_a, dg_b):
    s = x2.shape[0]
    tm = ROW_TILE

    def body(x_ref, dy_ref, gain_ref, w_hbm, cos_ref, sin_ref, gqa_ref, gka_ref, gqb_ref, gkb_ref, bd256_ref,
             bd128_ref, perm_ref, qa_raw_ref, ka_raw_ref, qb_raw_ref, kb_raw_ref, dqa_ref, dkva_ref,
             dq1_ref, dkv1_ref, dq4_ref, dkv4_ref, dq16_ref, dkv16_ref, dga_ref, dgb_ref,
             gx_ref, ht_ref, win_ref,
             dgain_ref, dgqa_ref, dgka_ref, dgqb_ref, dgkb_ref, w_vmem, dproj_ref):
        i = pl.program_id(0)
        perm = perm_ref[...]

        @pl.when(i == 0)
        def _():
            pltpu.sync_copy(w_hbm, w_vmem)
            dgain_ref[...] = jnp.zeros_like(dgain_ref)
            dgqa_ref[...] = jnp.zeros_like(dgqa_ref)
            dgka_ref[...] = jnp.zeros_like(dgka_ref)
            dgqb_ref[...] = jnp.zeros_like(dgqb_ref)
            dgkb_ref[...] = jnp.zeros_like(dgkb_ref)

        cos1 = cos_ref[...]
        sin1 = sin_ref[...]
        cos4 = jnp.tile(cos1, (1, 4))
        sin4 = jnp.tile(sin1, (1, 4))

        dt, dg = _qknorm_rope_bwd(dqa_ref[...], qa_raw_ref[...], gqa_ref[...], cos4, sin4, bd256_ref[...])
        dproj_ref[:, C_QA:C_KA] = dt.astype(BF16)
        dgqa_ref[...] += jnp.sum(dg, axis=0, keepdims=True)
        dt, dg = _qknorm_rope_bwd(dkva_ref[:, :A_KV_WIDTH], ka_raw_ref[...], gka_ref[...], cos1, sin1,
                                  bd128_ref[...])
        dproj_ref[:, C_KA:C_VA] = dt.astype(BF16)
        dgka_ref[...] += jnp.sum(dg, axis=0, keepdims=True)
        dproj_ref[:, C_VA:C_GA] = dkva_ref[:, A_KV_WIDTH:]
        dproj_ref[:, C_GA:C_QB] = dga_ref[...]
        dq = (dq1_ref[...].astype(F32) + _load_folded_bf16(dq4_ref, perm)) + _load_folded_bf16(dq16_ref, perm)
        dt, dg = _qknorm_rope_bwd(dq, qb_raw_ref[...], gqb_ref[...], cos4, sin4, bd256_ref[...])
        dproj_ref[:, C_QB:C_KB] = dt.astype(BF16)
        dgqb_ref[...] += jnp.sum(dg, axis=0, keepdims=True)
        dkv = (dkv1_ref[...].astype(F32) + _load_folded_bf16(dkv4_ref, perm)) + _load_folded_bf16(dkv16_ref, perm)
        dt, dg = _qknorm_rope_bwd(dkv[:, :WIDTH], kb_raw_ref[...], gkb_ref[...], cos4, sin4, bd256_ref[...])
        dproj_ref[:, C_KB:C_VB] = dt.astype(BF16)
        dgkb_ref[...] += jnp.sum(dg, axis=0, keepdims=True)
        dproj_ref[:, C_VB:C_GB] = dkv[:, WIDTH:].astype(BF16)
        dproj_ref[:, C_GB:C_END] = dgb_ref[...]
        for k, start in enumerate(WIN_START):
            win_ref[k] = dproj_ref[:, start:start + WIN]

        xt = x_ref[...]
        gain_row = gain_ref[...]
        r = lax.rsqrt(jnp.mean(xt * xt, axis=-1, keepdims=True) + EPS)
        xr = xt * r
        ht_ref[...] = (xr * gain_row).T.astype(BF16)
        dh = _dot(dproj_ref[...], w_vmem[...])
        dgain_ref[...] += jnp.sum(dh * xr, axis=0, keepdims=True)
        u = dh * gain_row
        gx_ref[...] = dy_ref[...] + r * (u - xr * jnp.mean(u * xr, axis=-1, keepdims=True))

    def acc_row(w):
        return pl.BlockSpec((1, w), lambda i: (0, 0))

    sds = jax.ShapeDtypeStruct
    any_spec = pl.BlockSpec(memory_space=pl.ANY)
    win_spec = pl.BlockSpec((N_CHIP, tm, WIN), lambda i: (0, i, 0))
    return pl.pallas_call(
        body, name="inproj_bwd", grid=(s // tm,),
        in_specs=[_rows(D_MODEL), _rows(D_MODEL), _whole(gain.shape), any_spec, _rows(LANES), _rows(LANES),
                  _whole(gqa.shape), _whole(gka.shape), _whole(gqb.shape), _whole(gkb.shape), _whole(bd256.shape),
                  _whole(bd128.shape), _whole(perm.shape),
                  _rows(WIDTH), _rows(A_KV_WIDTH), _rows(WIDTH), _rows(WIDTH),
                  _rows(WIDTH), _rows(2 * A_KV_WIDTH), _rows(WIDTH), _rows(2 * WIDTH)]
                 + [_folded_rows(WIDTH), _folded_rows(2 * WIDTH)] * 2 + [_rows(WIDTH), _rows(WIDTH)],
        out_specs=(_rows(D_MODEL), pl.BlockSpec((D_MODEL, tm), lambda i: (0, i)), win_spec, acc_row(D_MODEL), acc_row(WIDTH), acc_row(A_KV_WIDTH), acc_row(WIDTH), acc_row(WIDTH)),
        out_shape=(sds((s, D_MODEL), F32), sds((D_MODEL, s), BF16), sds((N_CHIP, s, WIN), BF16),
                   sds((1, D_MODEL), F32),
                   sds((1, WIDTH), F32), sds((1, A_KV_WIDTH), F32), sds((1, WIDTH), F32), sds((1, WIDTH), F32)),
        scratch_shapes=[pltpu.VMEM((IN_WIDTH, D_MODEL), BF16), pltpu.VMEM((tm, IN_WIDTH), BF16)],
        compiler_params=pltpu.CompilerParams(dimension_semantics=("arbitrary",), vmem_limit_bytes=VMEM_LIMIT),
    )(x2, dy, gain, w_bf, cos, sin_s, gqa, gka, gqb, gkb, bd256, bd128, perm, qa_raw, ka_raw, qb_raw, kb_raw,
      *d_a, *d_b1, *d_b4, *d_b16, dg_a, dg_b)


def _rope_tables(s):
    half = HEAD_DIM // 2
    inv = jnp.tile(ROPE_THETA ** (-jnp.arange(half, dtype=F32) / half), 4)
    sign = jnp.tile(jnp.concatenate([-jnp.ones((half,), F32), jnp.ones((half,), F32)]), 2)
    hi = (jnp.arange(s // ROPE_SPLIT) * ROPE_SPLIT).astype(F32)[:, None] * inv[None, :]
    lo = jnp.arange(ROPE_SPLIT).astype(F32)[:, None] * inv[None, :]
    ch, sh, cl, sl = jnp.cos(hi)[:, None, :], jnp.sin(hi)[:, None, :], jnp.cos(lo)[None], jnp.sin(lo)[None]
    cos = (ch * cl - sh * sl).reshape(s, LANES)
    sin = (sh * cl + ch * sl).reshape(s, LANES)
    return cos, sin * sign[None, :]


def _block_diag_ones(w):
    idx = jnp.arange(w) // HEAD_DIM
    return (idx[:, None] == idx[None, :]).astype(BF16)


def _local_step(x2, tgt2, norm_gain, w_in_bf, q_norm_a, k_norm_a, sinks_a, q_norm_b, k_norm_b, w_out_blk):
    s = x2.shape[0]
    cos, sin_s = _rope_tables(s)
    bd256, bd128 = _block_diag_ones(2 * LANES), _block_diag_ones(A_KV_WIDTH)
    gqa = jnp.tile(q_norm_a, (1, HEADS))
    gka = jnp.tile(k_norm_a, (1, 2))
    gqb = jnp.tile(q_norm_b, (1, HEADS))
    gkb = jnp.tile(k_norm_b, (1, HEADS))
    sink_row = jnp.pad(sinks_a, ((0, 0), (0, LANES - HEADS)))
    perm = _fold_matrix()

    (qa, kva, qb, kvb, qbf, kvbf, qa_raw, ka_raw, g_a, qb_raw, kb_raw, g_b, w_out_bf) = _inproj(
        x2, norm_gain, w_in_bf, cos, sin_s, gqa, gka, gqb, gkb, bd256, bd128, w_out_blk)

    att_a = _attn_fwd(qa, kva, sinks_a, dil=1, max_dist=A_MAX_DIST, name="attn_a_fwd")
    att_b1 = _attn_fwd(qb, kvb, None, dil=1, max_dist=B_MAX_DIST, name="attn_b1_fwd")
    att_b4 = _attn_fwd(qbf, kvbf, None, dil=4, max_dist=B_MAX_DIST, name="attn_b4_fwd")
    att_b16 = _attn_fwd(qbf, kvbf, None, dil=16, max_dist=B_MAX_DIST, name="attn_b16_fwd")

    (dy, do_a, do_b, do_bf, dg_a, dg_b, ld_a, ld_b, ld_bf, gw_out, loss_part, dsink) = _outproj(
        att_a, att_b1, att_b4, att_b16, g_a, g_b, x2, tgt2, w_out_bf, sink_row, perm)

    d_a = _attn_bwd(qa, kva, do_a, ld_a, dil=1, max_dist=A_MAX_DIST, name="attn_a_bwd")
    d_b1 = _attn_bwd(qb, kvb, do_b, ld_b, dil=1, max_dist=B_MAX_DIST, name="attn_b1_bwd")
    d_b4 = _attn_bwd(qbf, kvbf, do_bf, ld_bf, dil=4, max_dist=B_MAX_DIST, name="attn_b4_bwd")
    d_b16 = _attn_bwd(qbf, kvbf, do_bf, ld_bf, dil=16, max_dist=B_MAX_DIST, name="attn_b16_bwd")

    gx, h_t, wins, dgain, dgqa, dgka, dgqb, dgkb = _inproj_bwd(
        x2, dy, norm_gain, w_in_bf, cos, sin_s, gqa, gka, gqb, gkb, bd256, bd128, perm,
        qa_raw, ka_raw, qb_raw, kb_raw, d_a, d_b1, d_b4, d_b16, dg_a, dg_b)
    return loss_part, gx, h_t, wins, gw_out, (dgain, dgqa, dgka, dsink, dgqb, dgkb)


def _position():
    return lax.axis_index("x"), lax.axis_index("y"), lax.axis_index("c")


GATHER_CHUNKS = 2


def _gather_weights(blocks, name):
    n = len(blocks)
    ch = GATHER_CHUNKS

    def body(*refs):
        src_refs, dst_refs = refs[:n], refs[n:2 * n]
        ici_send, ici_recv, hop_send, hop_recv, d2d_send, d2d_recv = refs[2 * n:]
        x, y, c = _position()
        b = 2 * x + y
        via = 2 - c
        out = 3 - via
        for k in range(n):
            dst_refs[k][b] = src_refs[k][...].astype(BF16)

        def rows(k, core, j):
            half = blocks[k].shape[0] // 2
            return pl.ds(pl.multiple_of(core * half + j * (half // ch), half // ch), half // ch)

        def chip(rel):
            return x ^ (rel >> 1), y ^ (rel & 1)

        def ici(k, j, slot, rel, send_sems, recv_sems, sem):
            px, py = chip(rel)
            piece = dst_refs[k].at[slot, rows(k, c, j)]
            return pltpu.make_async_remote_copy(src_ref=piece, dst_ref=piece, send_sem=send_sems.at[sem],
                                                recv_sem=recv_sems.at[sem], device_id=(px, py, c),
                                                device_id_type=MESH)

        def direct(k, j, slot, rel):
            return ici(k, j, slot, rel, ici_send, ici_recv, ((rel - 1) * ch + j) * n + k)

        def hop(k, j, slot, rel):
            return ici(k, j, slot, rel, hop_send, hop_recv, j * n + k)

        def d2d(k, j, rel, core):
            piece = dst_refs[k].at[b ^ rel, rows(k, core, j)]
            sem = ((rel - 1) * ch + j) * n + k
            return pltpu.make_async_remote_copy(src_ref=piece, dst_ref=piece, send_sem=d2d_send.at[sem],
                                                recv_sem=d2d_recv.at[sem], device_id=(x, y, 1 - c),
                                                device_id_type=MESH)

        pieces = [(k, j) for j in range(ch) for k in range(n)]
        for k, j in pieces:
            for rel in (1, 2):
                direct(k, j, b, rel).start()
        for k, j in pieces:
            direct(k, j, b ^ via, via).wait_recv()
            hop(k, j, b ^ via, out).start()
            d2d(k, j, via, c).start()
        for k, j in pieces:
            direct(k, j, b ^ out, out).wait_recv()
            d2d(k, j, out, c).start()
        for k, j in pieces:
            hop(k, j, b ^ 3, via).wait_recv()
            d2d(k, j, 3, c).start()
        for k, j in pieces:
            for rel in (1, 2, 3):
                d2d(k, j, rel, 1 - c).wait_recv()
        for k, j in pieces:
            for rel in (1, 2):
                direct(k, j, b, rel).wait_send()
            hop(k, j, b ^ via, out).wait_send()
            d2d(k, j, via, c).wait_send()
            d2d(k, j, out, c).wait_send()
            d2d(k, j, 3, c).wait_send()

    vmem_spec = pl.BlockSpec(memory_space=pltpu.VMEM)
    dma = pltpu.SemaphoreType.DMA
    out_shape = tuple(jax.ShapeDtypeStruct((N_CHIP,) + a.shape, BF16) for a in blocks)
    return pl.pallas_call(
        body, name=name, in_specs=[vmem_spec] * n, out_specs=tuple([vmem_spec] * n), out_shape=out_shape,
        scratch_shapes=[dma((2 * ch * n,)), dma((2 * ch * n,)), dma((ch * n,)), dma((ch * n,)),
                        dma((3 * ch * n,)), dma((3 * ch * n,))],
        compiler_params=pltpu.CompilerParams(vmem_limit_bytes=VMEM_LIMIT),
    )(*blocks)


def _grad_reduce(order, h_t, wins, gw_out, small):
    s = h_t.shape[1]
    tk = GRAD_ROWS
    n_i = s // tk
    half = D_MODEL // 2
    o_half = OUT_ROWS // 2
    n_rel = N_CHIP - 1

    def body(order_ref, ht_ref, win_ref, gwo_ref, small_ref,
             win_out, wout_out, small_out,
             acc, mine, s1, r1, s2, r2, so1, ro1, so2, ro2, pair_in, pair_o, small_land,
             s1_send, s1_recv, s2_send, s2_recv, o1_send, o1_recv, o2_send, o2_recv,
             pair_send, pair_recv, small_send, small_recv):
        j = pl.program_id(0)
        i = pl.program_id(1)
        x, y, c = _position()
        me = 4 * x + 2 * y + c
        sibling = (x, y, 1 - c)
        my_rows = pl.ds(pl.multiple_of(c * half, half), half)
        sib_rows = pl.ds(pl.multiple_of((1 - c) * half, half), half)

        def chip_of(rel):
            return x ^ (rel >> 1), y ^ (rel & 1)

        def level1(k):
            return pltpu.make_async_remote_copy(src_ref=s1.at[k], dst_ref=r1.at[k], send_sem=s1_send.at[k],
                                                recv_sem=s1_recv.at[k], device_id=sibling, device_id_type=MESH)

        def level2(k):
            px, py = chip_of(RELATIONS[k])
            return pltpu.make_async_remote_copy(src_ref=s2.at[k], dst_ref=r2.at[k], send_sem=s2_send.at[k],
                                                recv_sem=s2_recv.at[k], device_id=(px, py, c), device_id_type=MESH)

        def out_level1(bk):
            return pltpu.make_async_remote_copy(src_ref=so1.at[bk], dst_ref=ro1.at[bk], send_sem=o1_send.at[bk],
                                                recv_sem=o1_recv.at[bk], device_id=sibling, device_id_type=MESH)

        def out_level2(k):
            px, py = chip_of(RELATIONS[k])
            return pltpu.make_async_remote_copy(src_ref=so2.at[k], dst_ref=ro2.at[k], send_sem=o2_send.at[k],
                                                recv_sem=o2_recv.at[k], device_id=(px, py, c), device_id_type=MESH)

        def small_copy(d):
            px, py, pc = x ^ (d >> 2), y ^ ((d >> 1) & 1), c ^ (d & 1)
            return pltpu.make_async_remote_copy(src_ref=small_ref, dst_ref=small_land.at[me],
                                                send_sem=small_send.at[d], recv_sem=small_recv.at[d],
                                                device_id=(px, py, pc), device_id_type=MESH)

        def pair_copy(k, buf):
            return pltpu.make_async_remote_copy(src_ref=buf.at[0], dst_ref=buf.at[1], send_sem=pair_send.at[k],
                                                recv_sem=pair_recv.at[k], device_id=sibling, device_id_type=MESH)

        def out_rows(bk, core):
            return pl.ds(pl.multiple_of(bk * OUT_ROWS + core * o_half, o_half), o_half)

        @pl.when((j == 0) & (i == 0))
        def _():
            for d in range(1, N_DEV):
                small_copy(d).start()
            small_land[me] = small_ref[...]
            for bk in range(N_CHIP):
                so1[bk] = gwo_ref[out_rows(bk, 1 - c), :].astype(BF16)
                out_level1(bk).start()

        @pl.when((j == 0) & (i == 1))
        def _():
            b = 2 * x + y
            for bk in range(N_CHIP):
                out_level1(bk).wait_recv()
            for k in range(n_rel):
                px, py = chip_of(RELATIONS[k])
                bk = 2 * px + py
                so2[k] = (gwo_ref[out_rows(bk, c), :] + ro1[bk].astype(F32)).astype(BF16)
                out_level2(k).start()

        @pl.when(i == 0)
        def _():
            acc[...] = jnp.zeros_like(acc)

        for n0 in range(0, WIN, ACC_COLS):
            n1 = min(n0 + ACC_COLS, WIN)
            acc[:, n0:n1] += _dot(ht_ref[...], win_ref[:, n0:n1])

        for k in range(N_CHIP):
            @pl.when((j == k) & (i == n_i - 1))
            def _(k=k):
                s1[k] = acc[sib_rows, :].astype(BF16)
                level1(k).start()
                mine[...] = acc[my_rows, :]

            if k < n_rel:
                @pl.when((j == k + 1) & (i == 1))
                def _(k=k):
                    level1(k).wait_recv()
                    s2[k] = (mine[...] + r1[k].astype(F32)).astype(BF16)
                    level2(k).start()

        @pl.when((j == N_CHIP - 1) & (i == n_i - 1))
        def _():
            b = 2 * x + y
            level1(N_CHIP - 1).wait_recv()
            total = mine[...] + r1[N_CHIP - 1].astype(F32)
            for k in range(n_rel):
                level2(k).wait_recv()
                total = total + r2[k].astype(F32)
            total = total.T
            pair_in[0] = total
            pair_copy(0, pair_in).start()
            total_o = gwo_ref[out_rows(b, c), :] + ro1[b].astype(F32)
            for k in range(n_rel):
                out_level2(k).wait_recv()
                total_o = total_o + ro2[k].astype(F32)
            pair_o[0] = total_o
            pair_copy(1, pair_o).start()
            for core in range(2):
                @pl.when(c == core)
                def _(core=core):
                    win_out[:, core * half:(core + 1) * half] = total
            wout_out[c] = total_o
            for d in range(1, N_DEV):
                small_copy(d).wait_recv()
            small_out[...] = small_land[...]
            pair_copy(0, pair_in).wait_recv()
            for core in range(2):
                @pl.when(c == core)
                def _(core=core):
                    win_out[:, (1 - core) * half:(2 - core) * half] = pair_in[1]
            pair_copy(1, pair_o).wait_recv()
            wout_out[1 - c] = pair_o[1]
            for d in range(1, N_DEV):
                small_copy(d).wait_send()
            for k in range(N_CHIP):
                level1(k).wait_send()
                out_level1(k).wait_send()
            for k in range(n_rel):
                level2(k).wait_send()
                out_level2(k).wait_send()
            pair_copy(0, pair_in).wait_send()
            pair_copy(1, pair_o).wait_send()

    vmem = pl.BlockSpec(memory_space=pltpu.VMEM)
    dma = pltpu.SemaphoreType.DMA
    sds = jax.ShapeDtypeStruct
    grid_spec = pltpu.PrefetchScalarGridSpec(
        num_scalar_prefetch=1, grid=(N_CHIP, n_i),
        in_specs=[pl.BlockSpec((D_MODEL, tk), lambda j, i, order: (0, i)),
                  pl.BlockSpec((None, tk, WIN), lambda j, i, order: (order[j], i, 0)), vmem, vmem],
        out_specs=(vmem, vmem, vmem),
        scratch_shapes=[
            pltpu.VMEM((D_MODEL, WIN), F32), pltpu.VMEM((half, WIN), F32),
            pltpu.VMEM((N_CHIP, half, WIN), BF16), pltpu.VMEM((N_CHIP, half, WIN), BF16),
            pltpu.VMEM((n_rel, half, WIN), BF16), pltpu.VMEM((n_rel, half, WIN), BF16),
            pltpu.VMEM((N_CHIP, o_half, D_MODEL), BF16), pltpu.VMEM((N_CHIP, o_half, D_MODEL), BF16),
            pltpu.VMEM((n_rel, o_half, D_MODEL), BF16), pltpu.VMEM((n_rel, o_half, D_MODEL), BF16),
            pltpu.VMEM((2, WIN, half), F32), pltpu.VMEM((2, o_half, D_MODEL), F32),
            pltpu.VMEM((N_DEV, PACK_ROWS, D_MODEL), F32),
            dma((N_CHIP,)), dma((N_CHIP,)), dma((n_rel,)), dma((n_rel,)),
            dma((N_CHIP,)), dma((N_CHIP,)), dma((n_rel,)), dma((n_rel,)),
            dma((2,)), dma((2,)), dma((N_DEV,)), dma((N_DEV,))])
    return pl.pallas_call(
        body, name="grad_w_in_reduce", grid_spec=grid_spec,
        out_shape=(sds((WIN, D_MODEL), F32), sds((2, o_half, D_MODEL), F32), sds((N_DEV, PACK_ROWS, D_MODEL), F32)),
        compiler_params=pltpu.CompilerParams(dimension_semantics=("arbitrary", "arbitrary"),
                                             vmem_limit_bytes=VMEM_LIMIT),
    )(order, h_t, wins, gw_out, small)


ADAM_STEPS = 4


def _adamw_math(w, g, m, v):
    m = ADAM_B1 * m + (1.0 - ADAM_B1) * g
    v = ADAM_B2 * v + (1.0 - ADAM_B2) * (g * g)
    m_hat = m / (1.0 - ADAM_B1 ** ADAM_STEP)
    v_hat = v / (1.0 - ADAM_B2 ** ADAM_STEP)
    delta = -ADAM_LR * (m_hat / (jnp.sqrt(v_hat) + ADAM_EPS) + ADAM_WD * w)
    return delta, m, v


def _adamw(w, g, m, v, name):
    r, c = w.shape

    def body(w_ref, g_ref, m_ref, v_ref, d_ref, nm_ref, nv_ref):
        delta, nm, nv = _adamw_math(w_ref[...], g_ref[...], m_ref[...], v_ref[...])
        d_ref[...] = delta
        nm_ref[...] = nm
        nv_ref[...] = nv

    rows = r // ADAM_STEPS
    assert rows * ADAM_STEPS == r and rows % 8 == 0
    spec = pl.BlockSpec((rows, c), lambda i: (i, 0))
    shape = jax.ShapeDtypeStruct((r, c), F32)
    return pl.pallas_call(
        body, name=name, grid=(ADAM_STEPS,), in_specs=[spec] * 4, out_specs=(spec,) * 3,
        out_shape=(shape,) * 3, compiler_params=pltpu.CompilerParams(vmem_limit_bytes=VMEM_LIMIT),
    )(w, g, m, v)


PACK_ROWS = 8


def _fold_heads(v):
    y = v[:, 0:LANES]
    for j in range(1, v.shape[1] // LANES):
        y = y + v[:, j * LANES:(j + 1) * LANES]
    return y + pltpu.roll(y, HEAD_DIM, 1)


N_SMALL = 6


def _small_adamw(recv, weights, m, v):
    def body(*refs):
        r_ref = refs[0]
        w_refs, m_refs, v_refs = (refs[1 + n * N_SMALL:1 + (n + 1) * N_SMALL] for n in range(3))
        outs = refs[1 + 3 * N_SMALL:]
        g_refs, d_refs, nm_refs, nv_refs = (outs[n * N_SMALL:(n + 1) * N_SMALL] for n in range(4))
        loss_ref = outs[4 * N_SMALL]
        tot = r_ref[0]
        for j in range(1, N_DEV):
            tot = tot + r_ref[j]
        loss_ref[...] = tot[3:4, 0:LANES]
        row1 = tot[1:2, :]
        row2 = tot[2:3, :]
        grads = [tot[0:1, :],
                 _fold_heads(row1[:, 0:WIDTH])[:, :HEAD_DIM],
                 _fold_heads(row2[:, WIDTH:WIDTH + A_KV_WIDTH])[:, :HEAD_DIM],
                 row2[:, WIDTH + A_KV_WIDTH:WIDTH + A_KV_WIDTH + HEADS],
                 _fold_heads(row1[:, WIDTH:2 * WIDTH])[:, :HEAD_DIM],
                 _fold_heads(row2[:, 0:WIDTH])[:, :HEAD_DIM]]
        for n, g in enumerate(grads):
            g_refs[n][...] = g
            delta, nm, nv = _adamw_math(w_refs[n][...], g, m_refs[n][...], v_refs[n][...])
            d_refs[n][...] = delta
            nm_refs[n][...] = nm
            nv_refs[n][...] = nv

    shapes = tuple(jax.ShapeDtypeStruct(a.shape, F32) for a in weights)
    outs = pl.pallas_call(body, name="small_adamw", out_shape=shapes * 4 + (jax.ShapeDtypeStruct((1, LANES), F32),)
                          )(recv, *weights, *m, *v)
    return tuple(outs[n * N_SMALL:(n + 1) * N_SMALL] for n in range(4)) + (outs[4 * N_SMALL],)


def kernel(x, norm_gain, w_in, q_norm_a, k_norm_a, sinks_a, q_norm_b, k_norm_b, w_out, loss_target, m_norm_gain, m_w_in, m_q_norm_a, m_k_norm_a, m_sinks_a, m_q_norm_b, m_k_norm_b, m_w_out, v_norm_gain, v_w_in, v_q_norm_a, v_k_norm_a, v_sinks_a, v_q_norm_b, v_k_norm_b, v_w_out):
    chip = 2 * lax.axis_index("x") + lax.axis_index("y")

    w_in_t, m_w_in_t, v_w_in_t = w_in[0].T, m_w_in[0].T, v_w_in[0].T

    (w_in_all,) = _gather_weights([w_in_t], "gather_weights")
    w_in_bf = w_in_all.reshape(IN_WIDTH, D_MODEL)

    loss_part, gx, h_t, wins, gw_out, (dgain, dgqa, dgka, dsink, dgqb, dgkb) = _local_step(
        x[0], loss_target[0], norm_gain, w_in_bf, q_norm_a, k_norm_a, sinks_a, q_norm_b, k_norm_b, w_out[0])

    small = jnp.concatenate([
        dgain, jnp.concatenate([dgqa, dgqb], axis=1),
        jnp.concatenate([dgkb, dgka, dsink, jnp.zeros((1, D_MODEL - WIDTH - 2 * A_KV_WIDTH), F32)], axis=1),
        jnp.pad(loss_part, ((0, 0), (0, D_MODEL - LANES))),
        jnp.zeros((PACK_ROWS - 4, D_MODEL), F32)], axis=0)
    order = (chip ^ jnp.array(RELATIONS, jnp.int32)).astype(jnp.int32)
    win_sum, wout_sum, small_recv = _grad_reduce(order, h_t, wins, gw_out, small)
    shift = jnp.array(WIN_SHIFT, jnp.int32)[chip]
    g_w_in_t = lax.dynamic_slice_in_dim(win_sum, shift, IN_COLS, axis=0)
    g_w_out = wout_sum.reshape(OUT_ROWS, D_MODEL)

    d_w_in, nm_w_in, nv_w_in = (a.T for a in _adamw(w_in_t, g_w_in_t, m_w_in_t, v_w_in_t, "adamw_w_in"))
    g_w_in = g_w_in_t.T
    d_w_out, nm_w_out, nv_w_out = _adamw(w_out[0], g_w_out, m_w_out[0], v_w_out[0], "adamw_w_out")
    g_s, d_s, nm_s, nv_s, loss_row = _small_adamw(
        small_recv,
        (norm_gain, q_norm_a, k_norm_a, sinks_a, q_norm_b, k_norm_b),
        (m_norm_gain, m_q_norm_a, m_k_norm_a, m_sinks_a, m_q_norm_b, m_k_norm_b),
        (v_norm_gain, v_q_norm_a, v_k_norm_a, v_sinks_a, v_q_norm_b, v_k_norm_b))
    loss = loss_row[0, 0]

    def leaves(small_ones, big_in, big_out):
        return (small_ones[0], big_in[None]) + tuple(small_ones[1:]) + (big_out[None],)

    return ((loss, gx[None]) + leaves(g_s, g_w_in, g_w_out) + leaves(d_s, d_w_in, d_w_out)
            + leaves(nm_s, nm_w_in, nm_w_out) + leaves(nv_s, nv_w_in, nv_w_out))
```

```python
import jax
import jax.numpy as jnp
from jax import lax
from jax.experimental import pallas as pl
from jax.experimental.pallas import tpu as pltpu

F32 = jnp.float32
BF16 = jnp.bfloat16

D_MODEL = 1024
HEAD_DIM = 64
HEADS = 8
WIDTH = HEADS * HEAD_DIM
A_KV_WIDTH = 2 * HEAD_DIM
BLOCK = 128
LANES = 128
FOLD = 16
A_MAX_DIST = 127
B_MAX_DIST = 128
ROPE_THETA = 10000.0
ROPE_SPLIT = 64
EPS = 1e-6
NEG = -1e30
SCALE = HEAD_DIM ** -0.5

IN_WIDTH = 3328
C_QA, C_KA, C_VA, C_GA, C_QB, C_KB, C_VB, C_GB, C_END = 0, 512, 640, 768, 1280, 1792, 2304, 2816, 3328

N_DEV = 8
N_CHIP = 4
MESH = pl.DeviceIdType.MESH
IN_COLS = IN_WIDTH // N_CHIP
WIN = 896
WIN_START = (0, 768, 1664, 2432)
WIN_SHIFT = (0, 64, 0, 64)
OUT_ROWS = D_MODEL // N_CHIP
RELATIONS = (3, 1, 2, 0)

ADAM_LR = 0.001
ADAM_B1 = 0.9
ADAM_B2 = 0.999
ADAM_EPS = 1e-08
ADAM_WD = 0.01
ADAM_STEP = 10

ROW_TILE = 256
FOLD_ROWS = ROW_TILE // FOLD
GRAD_ROWS = 1024
ACC_COLS = 256
VMEM_LIMIT = 56 * 1024 * 1024


def _dot(a, b):
    return jnp.dot(a, b, preferred_element_type=F32)


def _dot_nt(a, b):
    return lax.dot_general(a, b, (((1,), (1,)), ((), ())), preferred_element_type=F32)


def _dot_tn(a, b):
    return lax.dot_general(a, b, (((0,), (0,)), ((), ())), preferred_element_type=F32)


def _head_sum(z, bd):
    w = bd.shape[0]
    zb = z.astype(BF16)
    parts = [_dot(zb[:, a:a + w], bd) for a in range(0, z.shape[1], w)]
    return parts[0] if len(parts) == 1 else jnp.concatenate(parts, axis=1)


def _swap_halves(t):
    w = t.shape[1]
    lane = lax.broadcasted_iota(jnp.int32, t.shape, 1)
    return jnp.where(lane % HEAD_DIM < HEAD_DIM // 2, pltpu.roll(t, w - 32, 1), pltpu.roll(t, 32, 1))


def _qknorm_rope(t, g, cos, sin_s, bd):
    r = lax.rsqrt(_head_sum(t * t, bd) * (1.0 / HEAD_DIM) + EPS)
    n = (t * r) * g
    return n * cos + _swap_halves(n) * sin_s


def _qknorm_rope_bwd(dout, t, g, cos, sin_s, bd):
    dout, t = dout.astype(F32), t.astype(F32)
    dn = dout * cos + _swap_halves(dout * sin_s)
    r = lax.rsqrt(_head_sum(t * t, bd) * (1.0 / HEAD_DIM) + EPS)
    tr = t * r
    u = dn * g
    dt = r * (u - tr * (_head_sum(u * tr, bd) * (1.0 / HEAD_DIM)))
    return dt, dn * tr


def _sigmoid(g):
    return 1.0 / (1.0 + jnp.exp(-g))


def _expand_heads(st):
    t = st.shape[0]
    lane = lax.broadcasted_iota(jnp.int32, (t, LANES), 1)
    chunks = []
    for c in range(WIDTH // LANES):
        chunks.append(jnp.where(lane < HEAD_DIM, st[:, 2 * c:2 * c + 1], st[:, 2 * c + 1:2 * c + 2]))
    return jnp.concatenate(chunks, axis=1)


def _reduce_heads(z):
    t = z.shape[0]
    lane = lax.broadcasted_iota(jnp.int32, (t, LANES), 1)
    out = jnp.zeros((t, LANES), F32)
    for c in range(WIDTH // LANES):
        zc = z[:, c * LANES:(c + 1) * LANES]
        for ph in range(2):
            s = jnp.sum(jnp.where((lane // HEAD_DIM) == ph, zc, 0.0), axis=-1, keepdims=True)
            out = jnp.where(lane == 2 * c + ph, s, out)
    return out


def _fold_scratch(w):
    return pltpu.VMEM((w // LANES, ROW_TILE, LANES), F32)


def _store_folded(out_ref, val, scr, col0=0):
    w = val.shape[1]
    n = w // LANES
    for c in range(n):
        scr[c] = val[:, c * LANES:(c + 1) * LANES]
    for r in range(FOLD):
        piece = [scr[c, pl.ds(r, FOLD_ROWS, stride=FOLD), :] for c in range(n)]
        out_ref[r, :, col0:col0 + w] = (piece[0] if n == 1 else jnp.concatenate(piece, axis=1)).astype(out_ref.dtype)


def _load_folded(in_ref, scr):
    n = in_ref.shape[2] // LANES
    for r in range(FOLD):
        blk = in_ref[r].astype(F32)
        for c in range(n):
            scr[c, pl.ds(r, FOLD_ROWS, stride=FOLD), :] = blk[:, c * LANES:(c + 1) * LANES]
    return scr[0] if n == 1 else jnp.concatenate([scr[c] for c in range(n)], axis=1)


def _fold_matrix():
    f = jnp.arange(ROW_TILE)
    return (jnp.arange(ROW_TILE)[None, :] == (FOLD * (f % FOLD_ROWS) + f // FOLD_ROWS)[:, None]).astype(BF16)


def _store_folded_bf16(out_ref, val, perm):
    folded = _dot(perm, val.astype(BF16)).astype(out_ref.dtype)
    for r in range(FOLD):
        out_ref[r] = folded[r * FOLD_ROWS:(r + 1) * FOLD_ROWS]


def _load_folded_bf16(in_ref, perm):
    blk = jnp.concatenate([in_ref[r] for r in range(FOLD)], axis=0)
    return _dot(perm, blk)


def _rows(w, tm=ROW_TILE):
    return pl.BlockSpec((tm, w), lambda i: (i, 0))


def _folded_rows(w):
    return pl.BlockSpec((FOLD, FOLD_ROWS, w), lambda i: (0, i, 0))


def _whole(shape):
    return pl.BlockSpec(shape, lambda i: (0,) * len(shape))


def _inproj(x2, gain, w_bf, cos, sin_s, gqa, gka, gqb, gkb, bd256, bd128, w_out_blk):
    s = x2.shape[0]
    tm = ROW_TILE
    n_steps = s // tm
    n_rel = N_CHIP - 1
    o_half = OUT_ROWS // 2

    def body(x_ref, gain_ref, w_hbm, cos_ref, sin_ref, gqa_ref, gka_ref, gqb_ref, gkb_ref, bd256_ref, bd128_ref,
             wout_ref, qa_ref, kva_ref, qb_ref, kvb_ref, qbf_ref, kvbf_ref,
             qa_raw_ref, ka_raw_ref, ga_ref, qb_raw_ref, kb_raw_ref, gb_ref, wout_all_ref,
             w_vmem, scr, land, ici_send, ici_recv, d2d_send, d2d_recv):
        i = pl.program_id(0)
        px_, py_, c = _position()
        b = 2 * px_ + py_

        def piece(chip_idx, core):
            return land.at[chip_idx, pl.ds(pl.multiple_of(core * o_half, o_half), o_half)]

        def other_chip(d):
            ox, oy = px_ ^ (d >> 1), py_ ^ (d & 1)
            return ox, oy, 2 * ox + oy

        def ici_copy(d, chip_idx):
            ox, oy, _ = other_chip(d)
            return pltpu.make_async_remote_copy(
                src_ref=piece(chip_idx, c), dst_ref=piece(chip_idx, c), send_sem=ici_send.at[d - 1],
                recv_sem=ici_recv.at[d - 1], device_id=(ox, oy, c), device_id_type=MESH)

        def d2d_copy(d, core):
            return pltpu.make_async_remote_copy(
                src_ref=piece(other_chip(d)[2], core), dst_ref=piece(other_chip(d)[2], core),
                send_sem=d2d_send.at[d - 1], recv_sem=d2d_recv.at[d - 1], device_id=(px_, py_, 1 - c),
                device_id_type=MESH)

        @pl.when(i == 0)
        def _():
            pltpu.sync_copy(w_hbm, w_vmem)
            land[b] = wout_ref[...].astype(BF16)
            for d in range(1, N_CHIP):
                ici_copy(d, b).start()

        @pl.when(i == n_steps // 2)
        def _():
            for d in range(1, N_CHIP):
                ici_copy(d, other_chip(d)[2]).wait_recv()
                d2d_copy(d, c).start()

        @pl.when(i == n_steps - 1)
        def _():
            for d in range(1, N_CHIP):
                d2d_copy(d, 1 - c).wait_recv()
            for d in range(1, N_CHIP):
                ici_copy(d, b).wait_send()
                d2d_copy(d, c).wait_send()
            for k in range(N_CHIP):
                wout_all_ref[k * OUT_ROWS:(k + 1) * OUT_ROWS, :] = land[k]

        xt = x_ref[...]
        r = lax.rsqrt(jnp.mean(xt * xt, axis=-1, keepdims=True) + EPS)
        h = ((xt * r) * gain_ref[...]).astype(BF16)
        cos1 = cos_ref[...]
        sin1 = sin_ref[...]
        cos4 = jnp.tile(cos1, (1, 4))
        sin4 = jnp.tile(sin1, (1, 4))

        def seg(a, b):
            return _dot_nt(h, w_vmem[a:b, :])

        t = seg(C_QA, C_KA)
        qa_raw_ref[...] = t.astype(BF16)
        qa_ref[...] = (_qknorm_rope(t, gqa_ref[...], cos4, sin4, bd256_ref[...]) * SCALE).astype(BF16)
        t = seg(C_KA, C_VA)
        ka_raw_ref[...] = t.astype(BF16)
        kva_ref[:, :A_KV_WIDTH] = _qknorm_rope(t, gka_ref[...], cos1, sin1, bd128_ref[...]).astype(BF16)
        kva_ref[:, A_KV_WIDTH:] = seg(C_VA, C_GA).astype(BF16)
        ga_ref[...] = seg(C_GA, C_QB).astype(BF16)
        t = seg(C_QB, C_KB)
        qb_raw_ref[...] = t.astype(BF16)
        t = _qknorm_rope(t, gqb_ref[...], cos4, sin4, bd256_ref[...]) * SCALE
        qb_ref[...] = t.astype(BF16)
        _store_folded(qbf_ref, t, scr)
        t = seg(C_KB, C_VB)
        kb_raw_ref[...] = t.astype(BF16)
        t = _qknorm_rope(t, gkb_ref[...], cos4, sin4, bd256_ref[...])
        kvb_ref[:, :WIDTH] = t.astype(BF16)
        _store_folded(kvbf_ref, t, scr)
        t = seg(C_VB, C_GB)
        kvb_ref[:, WIDTH:] = t.astype(BF16)
        _store_folded(kvbf_ref, t, scr, WIDTH)
        gb_ref[...] = seg(C_GB, C_END).astype(BF16)

    sds = jax.ShapeDtypeStruct
    ln = s // FOLD
    out_shape = (sds((s, WIDTH), BF16), sds((s, 2 * A_KV_WIDTH), BF16), sds((s, WIDTH), BF16),
                 sds((s, 2 * WIDTH), BF16), sds((FOLD, ln, WIDTH), BF16), sds((FOLD, ln, 2 * WIDTH), BF16),
                 sds((s, WIDTH), BF16), sds((s, A_KV_WIDTH), BF16), sds((s, WIDTH), BF16),
                 sds((s, WIDTH), BF16), sds((s, WIDTH), BF16), sds((s, WIDTH), BF16),
                 sds((D_MODEL, D_MODEL), BF16))
    out_specs = (_rows(WIDTH), _rows(2 * A_KV_WIDTH), _rows(WIDTH), _rows(2 * WIDTH),
                 _folded_rows(WIDTH), _folded_rows(2 * WIDTH),
                 _rows(WIDTH), _rows(A_KV_WIDTH), _rows(WIDTH), _rows(WIDTH), _rows(WIDTH), _rows(WIDTH),
                 _whole((D_MODEL, D_MODEL)))
    dma = pltpu.SemaphoreType.DMA
    return pl.pallas_call(
        body, name="inproj_fwd", grid=(n_steps,),
        in_specs=[_rows(D_MODEL), _whole(gain.shape), pl.BlockSpec(memory_space=pl.ANY), _rows(LANES), _rows(LANES),
                  _whole(gqa.shape), _whole(gka.shape), _whole(gqb.shape), _whole(gkb.shape), _whole(bd256.shape),
                  _whole(bd128.shape), _whole(w_out_blk.shape)],
        out_specs=out_specs, out_shape=out_shape,
        scratch_shapes=[pltpu.VMEM((IN_WIDTH, D_MODEL), BF16), _fold_scratch(WIDTH),
                        pltpu.VMEM((N_CHIP, OUT_ROWS, D_MODEL), BF16),
                        dma((n_rel,)), dma((n_rel,)), dma((n_rel,)), dma((n_rel,))],
        compiler_params=pltpu.CompilerParams(dimension_semantics=("arbitrary",), vmem_limit_bytes=VMEM_LIMIT),
    )(x2, gain, w_bf, cos, sin_s, gqa, gka, gqb, gkb, bd256, bd128, w_out_blk)


def _seq_pos(idx, dil):
    if dil == 4:
        return 4 * (idx % 32) + idx // 32
    return idx


SOFTMAX_ROWS = 64


def _upper_mask(dil, r0=0, rows=2 * BLOCK):
    qi = (lax.broadcasted_iota(jnp.int32, (rows, BLOCK), 0) + r0) % BLOCK
    kj = lax.broadcasted_iota(jnp.int32, (rows, BLOCK), 1)
    return _seq_pos(kj, dil) > _seq_pos(qi, dil)


def _eye_mask(r0=0, rows=2 * BLOCK):
    qi = (lax.broadcasted_iota(jnp.int32, (rows, BLOCK), 0) + r0) % BLOCK
    kj = lax.broadcasted_iota(jnp.int32, (rows, BLOCK), 1)
    return qi == kj


def _stack_heads(a2, c, gqa):
    lane = lax.broadcasted_iota(jnp.int32, (1, LANES), 1) // HEAD_DIM
    zero = jnp.zeros_like(a2)
    if gqa:
        keep = lane == (c // 2)
        return jnp.concatenate([jnp.where(keep, a2, zero), jnp.where(keep, _swap_heads(a2), zero)], axis=0)
    return jnp.concatenate([jnp.where(lane == 0, a2, zero), jnp.where(lane == 1, a2, zero)], axis=0)


def _unstack_heads(a, c, gqa):
    lane = lax.broadcasted_iota(jnp.int32, (1, LANES), 1) // HEAD_DIM
    if gqa:
        return jnp.where(lane == (c // 2), a[:BLOCK], _swap_heads(a[BLOCK:]))
    return jnp.where(lane == 0, a[:BLOCK], a[BLOCK:])


def _stacked_head_ids(c, gqa):
    if gqa:
        return 2 * c + c // 2, 2 * c + 1 - c // 2
    return 2 * c, 2 * c + 1


def _per_head_rows(blk, heads):
    return jnp.concatenate([blk[:, heads[0]:heads[0] + 1], blk[:, heads[1]:heads[1] + 1]], axis=0)


def _attn_view(a, dil):
    if dil == 1:
        return a[None]
    if dil == 4:
        return a.reshape(4, 4, a.shape[1], a.shape[2])
    return a


def _attn_unview(a, dil):
    if dil == 1:
        return a[0]
    if dil == 4:
        return a.reshape(FOLD, a.shape[2], a.shape[3])
    return a


ATTN_BLOCKS_PER_STEP = 8


def _attn_specs(dil):
    if dil == 4:
        def spec(n, fn):
            return lambda w: pl.BlockSpec((4, None, n * BLOCK // 4, w), lambda r, i: (0, r, fn(i), 0))
    else:
        def spec(n, fn):
            return lambda w: pl.BlockSpec((None, n * BLOCK, w), lambda r, i: (r, fn(i), 0))
    return spec


def _blk_rows(g, dil):
    n = BLOCK // 4 if dil == 4 else BLOCK
    if isinstance(g, int):
        return slice(g * n, (g + 1) * n)
    return pl.ds(pl.multiple_of(g * n, n), n)


def _blk_load(ref, sl, dil, g=0):
    if dil == 4:
        return ref[:, _blk_rows(g, dil), sl].reshape(BLOCK, sl.stop - sl.start)
    return ref[_blk_rows(g, dil), sl]


def _blk_store(ref, sl, val, dil, g=0):
    val = val.astype(ref.dtype)
    if dil == 4:
        ref[:, _blk_rows(g, dil), sl] = val.reshape(4, BLOCK // 4, sl.stop - sl.start)
    else:
        ref[_blk_rows(g, dil), sl] = val


def _swap_heads(a):
    return pltpu.roll(a.astype(F32), HEAD_DIM, 1).astype(a.dtype)


STAT_SHIFT = 8


def _attn_fwd(q, kv, sinks, *, dil, max_dist, name):
    q, kv = _attn_view(q, dil), _attn_view(kv, dil)
    kw = kv.shape[-1] // 2
    gqa = kw == A_KV_WIDTH
    n_seq = dil
    nb = (q.shape[-2] * (4 if dil == 4 else 1)) // BLOCK
    per_step = min(ATTN_BLOCKS_PER_STEP, nb)
    with_sinks = sinks is not None
    all_lanes = slice(0, LANES)
    assert max_dist in (BLOCK - 1, BLOCK) and nb % per_step == 0
    diag = max_dist == BLOCK

    def body(*refs):
        if with_sinks:
            q_ref, kvp_ref, kvc_ref, sink_ref, o_ref, ml_ref = refs
        else:
            q_ref, kvp_ref, kvc_ref, o_ref, ml_ref = refs

        def block(g, has_prev):
            prev_ref, prev_g = (kvp_ref, 0) if isinstance(g, int) else (kvc_ref, g - 1)
            lane = lax.broadcasted_iota(jnp.int32, (1, LANES), 1)
            with_diag = diag and has_prev
            upper, eye = _upper_mask(dil), _eye_mask()
            first_rows = lax.broadcasted_iota(jnp.int32, (2 * BLOCK, 1), 0) < BLOCK
            ml_blk = jnp.zeros((BLOCK, LANES), F32)
            chunks = range(WIDTH // LANES)
            scores, values = [], []
            for c in chunks:
                sl = slice(c * LANES, (c + 1) * LANES)
                ksl = slice(0, LANES) if gqa else sl
                vsl = slice(ksl.start + kw, ksl.stop + kw)
                kcur, vcur = _blk_load(kvc_ref, ksl, dil, g), _blk_load(kvc_ref, vsl, dil, g)
                qs = _stack_heads(_blk_load(q_ref, sl, dil, g), c, gqa)
                if has_prev:
                    kcur = jnp.concatenate([_blk_load(prev_ref, ksl, dil, prev_g), kcur], axis=0)
                    vcur = jnp.concatenate([_blk_load(prev_ref, vsl, dil, prev_g), vcur], axis=0)
                scores.append(_dot_nt(qs, kcur))
                values.append(vcur)
            probs = []
            for c in chunks:
                heads = _stacked_head_ids(c, gqa)
                s = scores[c]
                if has_prev:
                    s_p = s[:, :BLOCK]
                    sc = jnp.where(upper, s_p, s[:, BLOCK:])
                else:
                    sc = jnp.where(upper, NEG, s)
                if with_diag:
                    sd = jnp.where(eye, s_p, NEG)
                    m = jnp.max(jnp.maximum(sc, sd), axis=-1, keepdims=True)
                else:
                    m = jnp.max(sc, axis=-1, keepdims=True)
                if with_sinks:
                    sk = jnp.where(first_rows, sink_ref[0, heads[0]], sink_ref[0, heads[1]])
                    m = jnp.maximum(m, sk)
                p = jnp.exp(sc - m)
                zero = jnp.zeros_like(p)
                if with_diag:
                    pd = jnp.exp(sd - m)
                    l = jnp.sum(p + pd, axis=-1, keepdims=True)
                else:
                    pd = zero
                    l = jnp.sum(p, axis=-1, keepdims=True)
                if with_sinks:
                    l = l + jnp.exp(sk - m)
                pf = jnp.where(upper, zero, p)
                if has_prev:
                    pf = jnp.concatenate([jnp.where(upper, p, pd), pf], axis=1)
                probs.append(pf.astype(BF16))
                for n, h in enumerate(heads):
                    rows = slice(n * BLOCK, (n + 1) * BLOCK)
                    ml_blk = jnp.where(lane == h, m[rows], ml_blk)
                    ml_blk = jnp.where(lane == h + STAT_SHIFT, l[rows], ml_blk)
            for c in chunks:
                sl = slice(c * LANES, (c + 1) * LANES)
                _blk_store(o_ref, sl, _unstack_heads(_dot(probs[c], values[c]), c, gqa), dil, g)
            _blk_store(ml_ref, all_lanes, ml_blk, dil, g)

        @pl.when(pl.program_id(1) == 0)
        def _():
            block(0, False)

        @pl.when(pl.program_id(1) > 0)
        def _():
            block(0, True)

        if per_step > 1:
            def rest(g, carry):
                block(g, True)
                return carry

            lax.fori_loop(1, per_step, rest, 0)

    spec = _attn_specs(dil)
    cur = spec(per_step, lambda i: i)
    prev = spec(1, lambda i: jnp.maximum(i * per_step - 1, 0))
    in_specs = [cur(WIDTH), prev(2 * kw), cur(2 * kw)]
    args = [q, kv, kv]
    if with_sinks:
        in_specs.append(pl.BlockSpec(memory_space=pltpu.SMEM))
        args.append(sinks)
    stats = jax.ShapeDtypeStruct(q.shape[:-1] + (LANES,), F32)
    o, ml = pl.pallas_call(
        body, name=name, grid=(n_seq, nb // per_step), in_specs=in_specs,
        out_specs=(cur(WIDTH), cur(LANES)),
        out_shape=(jax.ShapeDtypeStruct(q.shape, BF16), stats),
        compiler_params=pltpu.CompilerParams(dimension_semantics=("arbitrary", "arbitrary"),
                                             vmem_limit_bytes=VMEM_LIMIT),
    )(*args)
    return _attn_unview(o, dil), _attn_unview(ml, dil)


def _attn_bwd(q, kv, do, ld, *, dil, max_dist, name):
    q, kv, do, ld = (_attn_view(a, dil) for a in (q, kv, do, ld))
    kw = kv.shape[-1] // 2
    gqa = kw == A_KV_WIDTH
    n_seq = dil
    nb = (q.shape[-2] * (4 if dil == 4 else 1)) // BLOCK
    n_kc = kw // LANES
    per_step = min(ATTN_BLOCKS_PER_STEP, nb)
    all_lanes = slice(0, LANES)
    assert max_dist in (BLOCK - 1, BLOCK) and nb % per_step == 0
    diag = max_dist == BLOCK

    def body(q_ref, kvp_ref, kvc_ref, do_ref, ld_ref, dq_ref, dkv_ref, ck_ref, cv_ref):
        i = pl.program_id(1)

        def block(g, has_prev):
            prev_ref, prev_g = (kvp_ref, 0) if isinstance(g, int) else (kvc_ref, g - 1)
            seq_blk = i * per_step + g
            upper, eye = _upper_mask(dil), _eye_mask()
            ld_blk = _blk_load(ld_ref, all_lanes, dil, g)
            dk_acc = [None] * n_kc
            dv_acc = [None] * n_kc
            chunks = range(WIDTH // LANES)
            operands, products = [], []
            for c in chunks:
                sl = slice(c * LANES, (c + 1) * LANES)
                kc = 0 if gqa else c
                ksl = slice(kc * LANES, (kc + 1) * LANES)
                vsl = slice(ksl.start + kw, ksl.stop + kw)
                k2, v2 = _blk_load(kvc_ref, ksl, dil, g), _blk_load(kvc_ref, vsl, dil, g)
                if has_prev:
                    k2 = jnp.concatenate([_blk_load(prev_ref, ksl, dil, prev_g), k2], axis=0)
                    v2 = jnp.concatenate([_blk_load(prev_ref, vsl, dil, prev_g), v2], axis=0)
                qs = _stack_heads(_blk_load(q_ref, sl, dil, g), c, gqa)
                dos = _stack_heads(_blk_load(do_ref, sl, dil, g), c, gqa)
                operands.append((qs, dos, k2))
                products.append((_dot_nt(qs, k2), _dot_nt(dos, v2)))
            weights = []
            for c in chunks:
                heads = _stacked_head_ids(c, gqa)
                lse2 = _per_head_rows(ld_blk, heads)
                dl2 = _per_head_rows(ld_blk, tuple(h + STAT_SHIFT for h in heads))
                s, dp = products[c]
                if has_prev:
                    s_p, dp_p = s[:, :BLOCK], dp[:, :BLOCK]
                    sc = jnp.where(upper, s_p, s[:, BLOCK:])
                    dpc = jnp.where(upper, dp_p, dp[:, BLOCK:])
                else:
                    sc = jnp.where(upper, NEG, s)
                    dpc = dp
                p = jnp.exp(sc - lse2)
                ds = p * (dpc - dl2)
                zero = jnp.zeros_like(p)
                pf = jnp.where(upper, zero, p)
                dsf = jnp.where(upper, zero, ds)
                if has_prev:
                    if diag:
                        pd = jnp.exp(jnp.where(eye, s_p, NEG) - lse2)
                        dsd = pd * (dp_p - dl2)
                    else:
                        pd = dsd = zero
                    pf = jnp.concatenate([jnp.where(upper, p, pd), pf], axis=1)
                    dsf = jnp.concatenate([jnp.where(upper, ds, dsd), dsf], axis=1)
                weights.append((pf.astype(BF16), dsf.astype(BF16)))
            for c in chunks:
                sl = slice(c * LANES, (c + 1) * LANES)
                kc = 0 if gqa else c
                qs, dos, k2 = operands[c]
                pf, dsf = weights[c]
                _blk_store(dq_ref, sl, _unstack_heads(_dot(dsf, k2), c, gqa) * SCALE, dil, g)
                dk2 = _dot_tn(dsf, qs)
                dv2 = _dot_tn(pf, dos)
                dk_acc[kc] = dk2 if dk_acc[kc] is None else dk_acc[kc] + dk2
                dv_acc[kc] = dv2 if dv_acc[kc] is None else dv_acc[kc] + dv2
            for kc in range(n_kc):
                sl = slice(kc * LANES, (kc + 1) * LANES)
                vsl = slice(sl.start + kw, sl.stop + kw)
                if has_prev:
                    _blk_store(dkv_ref, sl, ck_ref[:, sl] + dk_acc[kc][:BLOCK], dil, seq_blk - 1)
                    _blk_store(dkv_ref, vsl, cv_ref[:, sl] + dv_acc[kc][:BLOCK], dil, seq_blk - 1)
                    ck_ref[:, sl] = dk_acc[kc][BLOCK:]
                    cv_ref[:, sl] = dv_acc[kc][BLOCK:]
                else:
                    ck_ref[:, sl] = dk_acc[kc]
                    cv_ref[:, sl] = dv_acc[kc]

        @pl.when(i == 0)
        def _():
            block(0, False)

        @pl.when(i > 0)
        def _():
            block(0, True)

        if per_step > 1:
            def rest(g, carry):
                block(g, True)
                return carry

            lax.fori_loop(1, per_step, rest, 0)

        @pl.when(i == nb // per_step - 1)
        def _():
            for kc in range(n_kc):
                sl = slice(kc * LANES, (kc + 1) * LANES)
                _blk_store(dkv_ref, sl, ck_ref[:, sl], dil, nb - 1)
                _blk_store(dkv_ref, slice(sl.start + kw, sl.stop + kw), cv_ref[:, sl], dil, nb - 1)

    spec = _attn_specs(dil)
    cur = spec(per_step, lambda i: i)
    prev = spec(1, lambda i: jnp.maximum(i * per_step - 1, 0))
    if dil == 4:
        whole = pl.BlockSpec((4, None, kv.shape[2], 2 * kw), lambda r, i: (0, r, 0, 0))
    else:
        whole = pl.BlockSpec((None, kv.shape[1], 2 * kw), lambda r, i: (r, 0, 0))
    sds = jax.ShapeDtypeStruct
    dq, dkv = pl.pallas_call(
        body, name=name, grid=(n_seq, nb // per_step),
        in_specs=[cur(WIDTH), prev(2 * kw), cur(2 * kw), cur(WIDTH), cur(LANES)],
        out_specs=(cur(WIDTH), whole),
        out_shape=(sds(q.shape, BF16), sds(kv.shape, BF16)),
        scratch_shapes=[pltpu.VMEM((BLOCK, kw), F32), pltpu.VMEM((BLOCK, kw), F32)],
        compiler_params=pltpu.CompilerParams(dimension_semantics=("arbitrary", "arbitrary"),
                                             vmem_limit_bytes=VMEM_LIMIT),
    )(q, kv, kv, do, ld)
    return _attn_unview(dq, dil), _attn_unview(dkv, dil)


def _outproj(att_a, att_b1, att_b4, att_b16, g_a, g_b, x2, tgt2, w_out_bf, sink_row, perm):
    s = x2.shape[0]
    tm = ROW_TILE

    def body(oa_ref, mla_ref, ob1_ref, ml1_ref, ob4_ref, ml4_ref, ob16_ref, ml16_ref,
             ga_ref, gb_ref, x_ref, t_ref, w_ref, sink_ref, perm_ref,
             dy_ref, doa_ref, dob_ref, dobf_ref, dga_ref, dgb_ref, lda_ref, ldb_ref, ldbf_ref,
             gw_ref, loss_ref, dsink_ref, scr_st):
        i = pl.program_id(0)
        perm = perm_ref[...]
        lane = lax.broadcasted_iota(jnp.int32, (tm, LANES), 1)
        used = lane < HEADS

        def split(ml):
            return jnp.where(used, ml, 0.0), jnp.where(used, pltpu.roll(ml, LANES - STAT_SHIFT, 1), 1.0)

        @pl.when(i == 0)
        def _():
            gw_ref[...] = jnp.zeros_like(gw_ref)
            loss_ref[...] = jnp.zeros_like(loss_ref)
            dsink_ref[...] = jnp.zeros_like(dsink_ref)

        ms, ls = zip(split(ml1_ref[...]), split(_load_folded(ml4_ref, scr_st)), split(_load_folded(ml16_ref, scr_st)))
        mx = jnp.maximum(jnp.maximum(ms[0], ms[1]), ms[2])
        scale = [jnp.exp(mp - mx) for mp in ms]
        den = (ls[0] * scale[0] + ls[1] * scale[1]) + ls[2] * scale[2]
        lse_b = jnp.where(used, mx + jnp.log(den), 0.0)
        inv_den = 1.0 / den
        o_b = _expand_heads(scale[0] * inv_den) * ob1_ref[...].astype(F32)
        o_b = o_b + _expand_heads(scale[1] * inv_den) * _load_folded_bf16(ob4_ref, perm)
        o_b = o_b + _expand_heads(scale[2] * inv_den) * _load_folded_bf16(ob16_ref, perm)
        m_a, l_a = split(mla_ref[...])
        lse_a = jnp.where(used, m_a + jnp.log(l_a), 0.0)
        o_a = _expand_heads(1.0 / l_a) * oa_ref[...].astype(F32)
        g_a = ga_ref[...].astype(F32)
        g_b = gb_ref[...].astype(F32)
        sg_a = _sigmoid(g_a)
        sg_b = _sigmoid(g_b)
        silu_a = g_a * sg_a
        silu_b = g_b * sg_b
        mixed = jnp.concatenate([o_a * silu_a, o_b * silu_b], axis=1).astype(BF16)
        w = w_ref[...]
        y = x_ref[...] + _dot(mixed, w)
        diff = y - t_ref[...]
        loss_ref[...] += (0.5 / D_MODEL) * jnp.sum(diff * diff)
        dy = diff * (1.0 / D_MODEL)
        dy_ref[...] = dy
        dyb = dy.astype(BF16)
        gw_ref[...] += _dot_tn(mixed, dyb)
        dmixed = _dot_nt(dyb, w)
        dm_a = dmixed[:, :WIDTH]
        dm_b = dmixed[:, WIDTH:]
        do_a = dm_a * silu_a
        do_b = dm_b * silu_b
        doa_ref[...] = do_a.astype(BF16)
        dob_ref[...] = do_b.astype(BF16)
        _store_folded_bf16(dobf_ref, do_b, perm)
        dga_ref[...] = (dm_a * o_a * (sg_a * (1.0 + g_a * (1.0 - sg_a)))).astype(BF16)
        dgb_ref[...] = (dm_b * o_b * (sg_b * (1.0 + g_b * (1.0 - sg_b)))).astype(BF16)
        dl_a = _reduce_heads(do_a * o_a)
        dl_b = _reduce_heads(do_b * o_b)
        lda_ref[...] = lse_a + pltpu.roll(dl_a, STAT_SHIFT, 1)
        ld_b = lse_b + pltpu.roll(dl_b, STAT_SHIFT, 1)
        ldb_ref[...] = ld_b
        _store_folded(ldbf_ref, ld_b, scr_st)
        dsink_ref[...] -= jnp.sum(jnp.exp(sink_ref[...] - lse_a) * dl_a, axis=0, keepdims=True)

    sds = jax.ShapeDtypeStruct
    ln = s // FOLD
    natural = [_rows(WIDTH), _rows(LANES)]
    folded = [_folded_rows(WIDTH), _folded_rows(LANES)]
    return pl.pallas_call(
        body, name="outproj_fwd_bwd", grid=(s // tm,),
        in_specs=natural + natural + folded + folded
                 + [_rows(WIDTH), _rows(WIDTH), _rows(D_MODEL), _rows(D_MODEL), _whole((D_MODEL, D_MODEL)),
                    _whole((1, LANES)), _whole(perm.shape)],
        out_specs=(_rows(D_MODEL), _rows(WIDTH), _rows(WIDTH), _folded_rows(WIDTH), _rows(WIDTH), _rows(WIDTH),
                   _rows(LANES), _rows(LANES), _folded_rows(LANES),
                   _whole((D_MODEL, D_MODEL)), _whole((1, LANES)), _whole((1, LANES))),
        out_shape=(sds((s, D_MODEL), F32), sds((s, WIDTH), BF16), sds((s, WIDTH), BF16),
                   sds((FOLD, ln, WIDTH), BF16), sds((s, WIDTH), BF16), sds((s, WIDTH), BF16),
                   sds((s, LANES), F32), sds((s, LANES), F32), sds((FOLD, ln, LANES), F32),
                   sds((D_MODEL, D_MODEL), F32), sds((1, LANES), F32), sds((1, LANES), F32)),
        scratch_shapes=[_fold_scratch(LANES)],
        compiler_params=pltpu.CompilerParams(dimension_semantics=("arbitrary",), vmem_limit_bytes=VMEM_LIMIT),
    )(*att_a, *att_b1, *att_b4, *att_b16, g_a, g_b, x2, tgt2, w_out_bf, sink_row, perm)


def _inproj_bwd(x2, dy, gain, w_bf, cos, sin_s, gqa, gka, gqb, gkb, bd256, bd128, perm,
                qa_raw, ka_raw, qb_raw, kb_raw, d_a, d_b1, d_b4, d_b16, dg_a, dg_b):
    s = x2.shape[0]
    tm = ROW_TILE

    def body(x_ref, dy_ref, gain_ref, w_hbm, cos_ref, sin_ref, gqa_ref, gka_ref, gqb_ref, gkb_ref, bd256_ref,
             bd128_ref, perm_ref, qa_raw_ref, ka_raw_ref, qb_raw_ref, kb_raw_ref, dqa_ref, dkva_ref,
             dq1_ref, dkv1_ref, dq4_ref, dkv4_ref, dq16_ref, dkv16_ref, dga_ref, dgb_ref,
             gx_ref, ht_ref, win_ref,
             dgain_ref, dgqa_ref, dgka_ref, dgqb_ref, dgkb_ref, w_vmem, dproj_ref):
        i = pl.program_id(0)
        perm = perm_ref[...]

        @pl.when(i == 0)
        def _():
            pltpu.sync_copy(w_hbm, w_vmem)
            dgain_ref[...] = jnp.zeros_like(dgain_ref)
            dgqa_ref[...] = jnp.zeros_like(dgqa_ref)
            dgka_ref[...] = jnp.zeros_like(dgka_ref)
            dgqb_ref[...] = jnp.zeros_like(dgqb_ref)
            dgkb_ref[...] = jnp.zeros_like(dgkb_ref)

        cos1 = cos_ref[...]
        sin1 = sin_ref[...]
        cos4 = jnp.tile(cos1, (1, 4))
        sin4 = jnp.tile(sin1, (1, 4))

        dt, dg = _qknorm_rope_bwd(dqa_ref[...], qa_raw_ref[...], gqa_ref[...], cos4, sin4, bd256_ref[...])
        dproj_ref[:, C_QA:C_KA] = dt.astype(BF16)
        dgqa_ref[...] += jnp.sum(dg, axis=0, keepdims=True)
        dt, dg = _qknorm_rope_bwd(dkva_ref[:, :A_KV_WIDTH], ka_raw_ref[...], gka_ref[...], cos1, sin1,
                                  bd128_ref[...])
        dproj_ref[:, C_KA:C_VA] = dt.astype(BF16)
        dgka_ref[...] += jnp.sum(dg, axis=0, keepdims=True)
        dproj_ref[:, C_VA:C_GA] = dkva_ref[:, A_KV_WIDTH:]
        dproj_ref[:, C_GA:C_QB] = dga_ref[...]
        dq = (dq1_ref[...].astype(F32) + _load_folded_bf16(dq4_ref, perm)) + _load_folded_bf16(dq16_ref, perm)
        dt, dg = _qknorm_rope_bwd(dq, qb_raw_ref[...], gqb_ref[...], cos4, sin4, bd256_ref[...])
        dproj_ref[:, C_QB:C_KB] = dt.astype(BF16)
        dgqb_ref[...] += jnp.sum(dg, axis=0, keepdims=True)
        dkv = (dkv1_ref[...].astype(F32) + _load_folded_bf16(dkv4_ref, perm)) + _load_folded_bf16(dkv16_ref, perm)
        dt, dg = _qknorm_rope_bwd(dkv[:, :WIDTH], kb_raw_ref[...], gkb_ref[...], cos4, sin4, bd256_ref[...])
        dproj_ref[:, C_KB:C_VB] = dt.astype(BF16)
        dgkb_ref[...] += jnp.sum(dg, axis=0, keepdims=True)
        dproj_ref[:, C_VB:C_GB] = dkv[:, WIDTH:].astype(BF16)
        dproj_ref[:, C_GB:C_END] = dgb_ref[...]
        for k, start in enumerate(WIN_START):
            win_ref[k] = dproj_ref[:, start:start + WIN]

        xt = x_ref[...]
        gain_row = gain_ref[...]
        r = lax.rsqrt(jnp.mean(xt * xt, axis=-1, keepdims=True) + EPS)
        xr = xt * r
        ht_ref[...] = (xr * gain_row).T.astype(BF16)
        dh = _dot(dproj_ref[...], w_vmem[...])
        dgain_ref[...] += jnp.sum(dh * xr, axis=0, keepdims=True)
        u = dh * gain_row
        gx_ref[...] = dy_ref[...] + r * (u - xr * jnp.mean(u * xr, axis=-1, keepdims=True))

    def acc_row(w):
        return pl.BlockSpec((1, w), lambda i: (0, 0))

    sds = jax.ShapeDtypeStruct
    any_spec = pl.BlockSpec(memory_space=pl.ANY)
    win_spec = pl.BlockSpec((N_CHIP, tm, WIN), lambda i: (0, i, 0))
    return pl.pallas_call(
        body, name="inproj_bwd", grid=(s // tm,),
        in_specs=[_rows(D_MODEL), _rows(D_MODEL), _whole(gain.shape), any_spec, _rows(LANES), _rows(LANES),
                  _whole(gqa.shape), _whole(gka.shape), _whole(gqb.shape), _whole(gkb.shape), _whole(bd256.shape),
                  _whole(bd128.shape), _whole(perm.shape),
                  _rows(WIDTH), _rows(A_KV_WIDTH), _rows(WIDTH), _rows(WIDTH),
                  _rows(WIDTH), _rows(2 * A_KV_WIDTH), _rows(WIDTH), _rows(2 * WIDTH)]
                 + [_folded_rows(WIDTH), _folded_rows(2 * WIDTH)] * 2 + [_rows(WIDTH), _rows(WIDTH)],
        out_specs=(_rows(D_MODEL), pl.BlockSpec((D_MODEL, tm), lambda i: (0, i)), win_spec, acc_row(D_MODEL), acc_row(WIDTH), acc_row(A_KV_WIDTH), acc_row(WIDTH), acc_row(WIDTH)),
        out_shape=(sds((s, D_MODEL), F32), sds((D_MODEL, s), BF16), sds((N_CHIP, s, WIN), BF16),
                   sds((1, D_MODEL), F32),
                   sds((1, WIDTH), F32), sds((1, A_KV_WIDTH), F32), sds((1, WIDTH), F32), sds((1, WIDTH), F32)),
        scratch_shapes=[pltpu.VMEM((IN_WIDTH, D_MODEL), BF16), pltpu.VMEM((tm, IN_WIDTH), BF16)],
        compiler_params=pltpu.CompilerParams(dimension_semantics=("arbitrary",), vmem_limit_bytes=VMEM_LIMIT),
    )(x2, dy, gain, w_bf, cos, sin_s, gqa, gka, gqb, gkb, bd256, bd128, perm, qa_raw, ka_raw, qb_raw, kb_raw,
      *d_a, *d_b1, *d_b4, *d_b16, dg_a, dg_b)


def _rope_tables(s):
    half = HEAD_DIM // 2
    inv = jnp.tile(ROPE_THETA ** (-jnp.arange(half, dtype=F32) / half), 4)
    sign = jnp.tile(jnp.concatenate([-jnp.ones((half,), F32), jnp.ones((half,), F32)]), 2)
    hi = (jnp.arange(s // ROPE_SPLIT) * ROPE_SPLIT).astype(F32)[:, None] * inv[None, :]
    lo = jnp.arange(ROPE_SPLIT).astype(F32)[:, None] * inv[None, :]
    ch, sh, cl, sl = jnp.cos(hi)[:, None, :], jnp.sin(hi)[:, None, :], jnp.cos(lo)[None], jnp.sin(lo)[None]
    cos = (ch * cl - sh * sl).reshape(s, LANES)
    sin = (sh * cl + ch * sl).reshape(s, LANES)
    return cos, sin * sign[None, :]


def _block_diag_ones(w):
    idx = jnp.arange(w) // HEAD_DIM
    return (idx[:, None] == idx[None, :]).astype(BF16)


def _local_step(x2, tgt2, norm_gain, w_in_bf, q_norm_a, k_norm_a, sinks_a, q_norm_b, k_norm_b, w_out_blk):
    s = x2.shape[0]
    cos, sin_s = _rope_tables(s)
    bd256, bd128 = _block_diag_ones(2 * LANES), _block_diag_ones(A_KV_WIDTH)
    gqa = jnp.tile(q_norm_a, (1, HEADS))
    gka = jnp.tile(k_norm_a, (1, 2))
    gqb = jnp.tile(q_norm_b, (1, HEADS))
    gkb = jnp.tile(k_norm_b, (1, HEADS))
    sink_row = jnp.pad(sinks_a, ((0, 0), (0, LANES - HEADS)))
    perm = _fold_matrix()

    (qa, kva, qb, kvb, qbf, kvbf, qa_raw, ka_raw, g_a, qb_raw, kb_raw, g_b, w_out_bf) = _inproj(
        x2, norm_gain, w_in_bf, cos, sin_s, gqa, gka, gqb, gkb, bd256, bd128, w_out_blk)

    att_a = _attn_fwd(qa, kva, sinks_a, dil=1, max_dist=A_MAX_DIST, name="attn_a_fwd")
    att_b1 = _attn_fwd(qb, kvb, None, dil=1, max_dist=B_MAX_DIST, name="attn_b1_fwd")
    att_b4 = _attn_fwd(qbf, kvbf, None, dil=4, max_dist=B_MAX_DIST, name="attn_b4_fwd")
    att_b16 = _attn_fwd(qbf, kvbf, None, dil=16, max_dist=B_MAX_DIST, name="attn_b16_fwd")

    (dy, do_a, do_b, do_bf, dg_a, dg_b, ld_a, ld_b, ld_bf, gw_out, loss_part, dsink) = _outproj(
        att_a, att_b1, att_b4, att_b16, g_a, g_b, x2, tgt2, w_out_bf, sink_row, perm)

    d_a = _attn_bwd(qa, kva, do_a, ld_a, dil=1, max_dist=A_MAX_DIST, name="attn_a_bwd")
    d_b1 = _attn_bwd(qb, kvb, do_b, ld_b, dil=1, max_dist=B_MAX_DIST, name="attn_b1_bwd")
    d_b4 = _attn_bwd(qbf, kvbf, do_bf, ld_bf, dil=4, max_dist=B_MAX_DIST, name="attn_b4_bwd")
    d_b16 = _attn_bwd(qbf, kvbf, do_bf, ld_bf, dil=16, max_dist=B_MAX_DIST, name="attn_b16_bwd")

    gx, h_t, wins, dgain, dgqa, dgka, dgqb, dgkb = _inproj_bwd(
        x2, dy, norm_gain, w_in_bf, cos, sin_s, gqa, gka, gqb, gkb, bd256, bd128, perm,
        qa_raw, ka_raw, qb_raw, kb_raw, d_a, d_b1, d_b4, d_b16, dg_a, dg_b)
    return loss_part, gx, h_t, wins, gw_out, (dgain, dgqa, dgka, dsink, dgqb, dgkb)


def _position():
    return lax.axis_index("x"), lax.axis_index("y"), lax.axis_index("c")


GATHER_CHUNKS = 2


def _gather_weights(blocks, name):
    n = len(blocks)
    ch = GATHER_CHUNKS

    def body(*refs):
        src_refs, dst_refs = refs[:n], refs[n:2 * n]
        ici_send, ici_recv, hop_send, hop_recv, d2d_send, d2d_recv = refs[2 * n:]
        x, y, c = _position()
        b = 2 * x + y
        via = 2 - c
        out = 3 - via
        for k in range(n):
            dst_refs[k][b] = src_refs[k][...].astype(BF16)

        def rows(k, core, j):
            half = blocks[k].shape[0] // 2
            return pl.ds(pl.multiple_of(core * half + j * (half // ch), half // ch), half // ch)

        def chip(rel):
            return x ^ (rel >> 1), y ^ (rel & 1)

        def ici(k, j, slot, rel, send_sems, recv_sems, sem):
            px, py = chip(rel)
            piece = dst_refs[k].at[slot, rows(k, c, j)]
            return pltpu.make_async_remote_copy(src_ref=piece, dst_ref=piece, send_sem=send_sems.at[sem],
                                                recv_sem=recv_sems.at[sem], device_id=(px, py, c),
                                                device_id_type=MESH)

        def direct(k, j, slot, rel):
            return ici(k, j, slot, rel, ici_send, ici_recv, ((rel - 1) * ch + j) * n + k)

        def hop(k, j, slot, rel):
            return ici(k, j, slot, rel, hop_send, hop_recv, j * n + k)

        def d2d(k, j, rel, core):
            piece = dst_refs[k].at[b ^ rel, rows(k, core, j)]
            sem = ((rel - 1) * ch + j) * n + k
            return pltpu.make_async_remote_copy(src_ref=piece, dst_ref=piece, send_sem=d2d_send.at[sem],
                                                recv_sem=d2d_recv.at[sem], device_id=(x, y, 1 - c),
                                                device_id_type=MESH)

        pieces = [(k, j) for j in range(ch) for k in range(n)]
        for k, j in pieces:
            for rel in (1, 2):
                direct(k, j, b, rel).start()
        for k, j in pieces:
            direct(k, j, b ^ via, via).wait_recv()
            hop(k, j, b ^ via, out).start()
            d2d(k, j, via, c).start()
        for k, j in pieces:
            direct(k, j, b ^ out, out).wait_recv()
            d2d(k, j, out, c).start()
        for k, j in pieces:
            hop(k, j, b ^ 3, via).wait_recv()
            d2d(k, j, 3, c).start()
        for k, j in pieces:
            for rel in (1, 2, 3):
                d2d(k, j, rel, 1 - c).wait_recv()
        for k, j in pieces:
            for rel in (1, 2):
                direct(k, j, b, rel).wait_send()
            hop(k, j, b ^ via, out).wait_send()
            d2d(k, j, via, c).wait_send()
            d2d(k, j, out, c).wait_send()
            d2d(k, j, 3, c).wait_send()

    vmem_spec = pl.BlockSpec(memory_space=pltpu.VMEM)
    dma = pltpu.SemaphoreType.DMA
    out_shape = tuple(jax.ShapeDtypeStruct((N_CHIP,) + a.shape, BF16) for a in blocks)
    return pl.pallas_call(
        body, name=name, in_specs=[vmem_spec] * n, out_specs=tuple([vmem_spec] * n), out_shape=out_shape,
        scratch_shapes=[dma((2 * ch * n,)), dma((2 * ch * n,)), dma((ch * n,)), dma((ch * n,)),
                        dma((3 * ch * n,)), dma((3 * ch * n,))],
        compiler_params=pltpu.CompilerParams(vmem_limit_bytes=VMEM_LIMIT),
    )(*blocks)


def _grad_reduce(order, h_t, wins, gw_out, small):
    s = h_t.shape[1]
    tk = GRAD_ROWS
    n_i = s // tk
    half = D_MODEL // 2
    o_half = OUT_ROWS // 2
    n_rel = N_CHIP - 1

    def body(order_ref, ht_ref, win_ref, gwo_ref, small_ref,
             win_out, wout_out, small_out,
             acc, mine, s1, r1, s2, r2, so1, ro1, so2, ro2, pair_in, pair_o, small_land,
             s1_send, s1_recv, s2_send, s2_recv, o1_send, o1_recv, o2_send, o2_recv,
             pair_send, pair_recv, small_send, small_recv):
        j = pl.program_id(0)
        i = pl.program_id(1)
        x, y, c = _position()
        me = 4 * x + 2 * y + c
        sibling = (x, y, 1 - c)
        my_rows = pl.ds(pl.multiple_of(c * half, half), half)
        sib_rows = pl.ds(pl.multiple_of((1 - c) * half, half), half)

        def chip_of(rel):
            return x ^ (rel >> 1), y ^ (rel & 1)

        def level1(k):
            return pltpu.make_async_remote_copy(src_ref=s1.at[k], dst_ref=r1.at[k], send_sem=s1_send.at[k],
                                                recv_sem=s1_recv.at[k], device_id=sibling, device_id_type=MESH)

        def level2(k):
            px, py = chip_of(RELATIONS[k])
            return pltpu.make_async_remote_copy(src_ref=s2.at[k], dst_ref=r2.at[k], send_sem=s2_send.at[k],
                                                recv_sem=s2_recv.at[k], device_id=(px, py, c), device_id_type=MESH)

        def out_level1(bk):
            return pltpu.make_async_remote_copy(src_ref=so1.at[bk], dst_ref=ro1.at[bk], send_sem=o1_send.at[bk],
                                                recv_sem=o1_recv.at[bk], device_id=sibling, device_id_type=MESH)

        def out_level2(k):
            px, py = chip_of(RELATIONS[k])
            return pltpu.make_async_remote_copy(src_ref=so2.at[k], dst_ref=ro2.at[k], send_sem=o2_send.at[k],
                                                recv_sem=o2_recv.at[k], device_id=(px, py, c), device_id_type=MESH)

        def small_copy(d):
            px, py, pc = x ^ (d >> 2), y ^ ((d >> 1) & 1), c ^ (d & 1)
            return pltpu.make_async_remote_copy(src_ref=small_ref, dst_ref=small_land.at[me],
                                                send_sem=small_send.at[d], recv_sem=small_recv.at[d],
                                                device_id=(px, py, pc), device_id_type=MESH)

        def pair_copy(k, buf):
            return pltpu.make_async_remote_copy(src_ref=buf.at[0], dst_ref=buf.at[1], send_sem=pair_send.at[k],
                                                recv_sem=pair_recv.at[k], device_id=sibling, device_id_type=MESH)

        def out_rows(bk, core):
            return pl.ds(pl.multiple_of(bk * OUT_ROWS + core * o_half, o_half), o_half)

        @pl.when((j == 0) & (i == 0))
        def _():
            for d in range(1, N_DEV):
                small_copy(d).start()
            small_land[me] = small_ref[...]
            for bk in range(N_CHIP):
                so1[bk] = gwo_ref[out_rows(bk, 1 - c), :].astype(BF16)
                out_level1(bk).start()

        @pl.when((j == 0) & (i == 1))
        def _():
            b = 2 * x + y
            for bk in range(N_CHIP):
                out_level1(bk).wait_recv()
            for k in range(n_rel):
                px, py = chip_of(RELATIONS[k])
                bk = 2 * px + py
                so2[k] = (gwo_ref[out_rows(bk, c), :] + ro1[bk].astype(F32)).astype(BF16)
                out_level2(k).start()

        @pl.when(i == 0)
        def _():
            acc[...] = jnp.zeros_like(acc)

        for n0 in range(0, WIN, ACC_COLS):
            n1 = min(n0 + ACC_COLS, WIN)
            acc[:, n0:n1] += _dot(ht_ref[...], win_ref[:, n0:n1])

        for k in range(N_CHIP):
            @pl.when((j == k) & (i == n_i - 1))
            def _(k=k):
                s1[k] = acc[sib_rows, :].astype(BF16)
                level1(k).start()
                mine[...] = acc[my_rows, :]

            if k < n_rel:
                @pl.when((j == k + 1) & (i == 1))
                def _(k=k):
                    level1(k).wait_recv()
                    s2[k] = (mine[...] + r1[k].astype(F32)).astype(BF16)
                    level2(k).start()

        @pl.when((j == N_CHIP - 1) & (i == n_i - 1))
        def _():
            b = 2 * x + y
            level1(N_CHIP - 1).wait_recv()
            total = mine[...] + r1[N_CHIP - 1].astype(F32)
            for k in range(n_rel):
                level2(k).wait_recv()
                total = total + r2[k].astype(F32)
            total = total.T
            pair_in[0] = total
            pair_copy(0, pair_in).start()
            total_o = gwo_ref[out_rows(b, c), :] + ro1[b].astype(F32)
            for k in range(n_rel):
                out_level2(k).wait_recv()
                total_o = total_o + ro2[k].astype(F32)
            pair_o[0] = total_o
            pair_copy(1, pair_o).start()
            for core in range(2):
                @pl.when(c == core)
                def _(core=core):
                    win_out[:, core * half:(core + 1) * half] = total
            wout_out[c] = total_o
            for d in range(1, N_DEV):
                small_copy(d).wait_recv()
            small_out[...] = small_land[...]
            pair_copy(0, pair_in).wait_recv()
            for core in range(2):
                @pl.when(c == core)
                def _(core=core):
                    win_out[:, (1 - core) * half:(2 - core) * half] = pair_in[1]
            pair_copy(1, pair_o).wait_recv()
            wout_out[1 - c] = pair_o[1]
            for d in range(1, N_DEV):
                small_copy(d).wait_send()
            for k in range(N_CHIP):
                level1(k).wait_send()
                out_level1(k).wait_send()
            for k in range(n_rel):
                level2(k).wait_send()
                out_level2(k).wait_send()
            pair_copy(0, pair_in).wait_send()
            pair_copy(1, pair_o).wait_send()

    vmem = pl.BlockSpec(memory_space=pltpu.VMEM)
    dma = pltpu.SemaphoreType.DMA
    sds = jax.ShapeDtypeStruct
    grid_spec = pltpu.PrefetchScalarGridSpec(
        num_scalar_prefetch=1, grid=(N_CHIP, n_i),
        in_specs=[pl.BlockSpec((D_MODEL, tk), lambda j, i, order: (0, i)),
                  pl.BlockSpec((None, tk, WIN), lambda j, i, order: (order[j], i, 0)), vmem, vmem],
        out_specs=(vmem, vmem, vmem),
        scratch_shapes=[
            pltpu.VMEM((D_MODEL, WIN), F32), pltpu.VMEM((half, WIN), F32),
            pltpu.VMEM((N_CHIP, half, WIN), BF16), pltpu.VMEM((N_CHIP, half, WIN), BF16),
            pltpu.VMEM((n_rel, half, WIN), BF16), pltpu.VMEM((n_rel, half, WIN), BF16),
            pltpu.VMEM((N_CHIP, o_half, D_MODEL), BF16), pltpu.VMEM((N_CHIP, o_half, D_MODEL), BF16),
            pltpu.VMEM((n_rel, o_half, D_MODEL), BF16), pltpu.VMEM((n_rel, o_half, D_MODEL), BF16),
            pltpu.VMEM((2, WIN, half), F32), pltpu.VMEM((2, o_half, D_MODEL), F32),
            pltpu.VMEM((N_DEV, PACK_ROWS, D_MODEL), F32),
            dma((N_CHIP,)), dma((N_CHIP,)), dma((n_rel,)), dma((n_rel,)),
            dma((N_CHIP,)), dma((N_CHIP,)), dma((n_rel,)), dma((n_rel,)),
            dma((2,)), dma((2,)), dma((N_DEV,)), dma((N_DEV,))])
    return pl.pallas_call(
        body, name="grad_w_in_reduce", grid_spec=grid_spec,
        out_shape=(sds((WIN, D_MODEL), F32), sds((2, o_half, D_MODEL), F32), sds((N_DEV, PACK_ROWS, D_MODEL), F32)),
        compiler_params=pltpu.CompilerParams(dimension_semantics=("arbitrary", "arbitrary"),
                                             vmem_limit_bytes=VMEM_LIMIT),
    )(order, h_t, wins, gw_out, small)


ADAM_STEPS = 4


def _adamw_math(w, g, m, v):
    m = ADAM_B1 * m + (1.0 - ADAM_B1) * g
    v = ADAM_B2 * v + (1.0 - ADAM_B2) * (g * g)
    m_hat = m / (1.0 - ADAM_B1 ** ADAM_STEP)
    v_hat = v / (1.0 - ADAM_B2 ** ADAM_STEP)
    delta = -ADAM_LR * (m_hat / (jnp.sqrt(v_hat) + ADAM_EPS) + ADAM_WD * w)
    return delta, m, v


def _adamw(w, g, m, v, name):
    r, c = w.shape

    def body(w_ref, g_ref, m_ref, v_ref, d_ref, nm_ref, nv_ref):
        delta, nm, nv = _adamw_math(w_ref[...], g_ref[...], m_ref[...], v_ref[...])
        d_ref[...] = delta
        nm_ref[...] = nm
        nv_ref[...] = nv

    rows = r // ADAM_STEPS
    assert rows * ADAM_STEPS == r and rows % 8 == 0
    spec = pl.BlockSpec((rows, c), lambda i: (i, 0))
    shape = jax.ShapeDtypeStruct((r, c), F32)
    return pl.pallas_call(
        body, name=name, grid=(ADAM_STEPS,), in_specs=[spec] * 4, out_specs=(spec,) * 3,
        out_shape=(shape,) * 3, compiler_params=pltpu.CompilerParams(vmem_limit_bytes=VMEM_LIMIT),
    )(w, g, m, v)


def _adamw_window(w, window, shift, m, v, name):
    r, c = w.shape
    rows = r // ADAM_STEPS
    assert rows * ADAM_STEPS == r and rows % 8 == 0

    def body(shift_ref, w_ref, win_hbm, m_ref, v_ref, g_ref, d_ref, nm_ref, nv_ref, g_vmem):
        start = pl.multiple_of(shift_ref[0] + pl.program_id(0) * rows, 8)
        pltpu.sync_copy(win_hbm.at[pl.ds(start, rows)], g_vmem)
        g = g_vmem[...]
        g_ref[...] = g
        delta, nm, nv = _adamw_math(w_ref[...], g, m_ref[...], v_ref[...])
        d_ref[...] = delta
        nm_ref[...] = nm
        nv_ref[...] = nv

    spec = pl.BlockSpec((rows, c), lambda i, shift_ref: (i, 0))
    shape = jax.ShapeDtypeStruct((r, c), F32)
    grid_spec = pltpu.PrefetchScalarGridSpec(
        num_scalar_prefetch=1, grid=(ADAM_STEPS,),
        in_specs=[spec, pl.BlockSpec(memory_space=pl.ANY), spec, spec], out_specs=(spec,) * 4,
        scratch_shapes=[pltpu.VMEM((rows, c), F32)])
    return pl.pallas_call(
        body, name=name, grid_spec=grid_spec, out_shape=(shape,) * 4,
        compiler_params=pltpu.CompilerParams(vmem_limit_bytes=VMEM_LIMIT),
    )(shift, w, window, m, v)


PACK_ROWS = 8


def _fold_heads(v):
    y = v[:, 0:LANES]
    for j in range(1, v.shape[1] // LANES):
        y = y + v[:, j * LANES:(j + 1) * LANES]
    return y + pltpu.roll(y, HEAD_DIM, 1)


N_SMALL = 6


def _small_adamw(recv, weights, m, v):
    def body(*refs):
        r_ref = refs[0]
        w_refs, m_refs, v_refs = (refs[1 + n * N_SMALL:1 + (n + 1) * N_SMALL] for n in range(3))
        outs = refs[1 + 3 * N_SMALL:]
        g_refs, d_refs, nm_refs, nv_refs = (outs[n * N_SMALL:(n + 1) * N_SMALL] for n in range(4))
        loss_ref = outs[4 * N_SMALL]
        tot = r_ref[0]
        for j in range(1, N_DEV):
            tot = tot + r_ref[j]
        loss_ref[...] = tot[3:4, 0:LANES]
        row1 = tot[1:2, :]
        row2 = tot[2:3, :]
        grads = [tot[0:1, :],
                 _fold_heads(row1[:, 0:WIDTH])[:, :HEAD_DIM],
                 _fold_heads(row2[:, WIDTH:WIDTH + A_KV_WIDTH])[:, :HEAD_DIM],
                 row2[:, WIDTH + A_KV_WIDTH:WIDTH + A_KV_WIDTH + HEADS],
                 _fold_heads(row1[:, WIDTH:2 * WIDTH])[:, :HEAD_DIM],
                 _fold_heads(row2[:, 0:WIDTH])[:, :HEAD_DIM]]
        for n, g in enumerate(grads):
            g_refs[n][...] = g
            delta, nm, nv = _adamw_math(w_refs[n][...], g, m_refs[n][...], v_refs[n][...])
            d_refs[n][...] = delta
            nm_refs[n][...] = nm
            nv_refs[n][...] = nv

    shapes = tuple(jax.ShapeDtypeStruct(a.shape, F32) for a in weights)
    outs = pl.pallas_call(body, name="small_adamw", out_shape=shapes * 4 + (jax.ShapeDtypeStruct((1, LANES), F32),)
                          )(recv, *weights, *m, *v)
    return tuple(outs[n * N_SMALL:(n + 1) * N_SMALL] for n in range(4)) + (outs[4 * N_SMALL],)


def kernel(x, norm_gain, w_in, q_norm_a, k_norm_a, sinks_a, q_norm_b, k_norm_b, w_out, loss_target, m_norm_gain, m_w_in, m_q_norm_a, m_k_norm_a, m_sinks_a, m_q_norm_b, m_k_norm_b, m_w_out, v_norm_gain, v_w_in, v_q_norm_a, v_k_norm_a, v_sinks_a, v_q_norm_b, v_k_norm_b, v_w_out):
    chip = 2 * lax.axis_index("x") + lax.axis_index("y")

    w_in_t, m_w_in_t, v_w_in_t = w_in[0].T, m_w_in[0].T, v_w_in[0].T

    (w_in_all,) = _gather_weights([w_in_t], "gather_weights")
    w_in_bf = w_in_all.reshape(IN_WIDTH, D_MODEL)

    loss_part, gx, h_t, wins, gw_out, (dgain, dgqa, dgka, dsink, dgqb, dgkb) = _local_step(
        x[0], loss_target[0], norm_gain, w_in_bf, q_norm_a, k_norm_a, sinks_a, q_norm_b, k_norm_b, w_out[0])

    small = jnp.concatenate([
        dgain, jnp.concatenate([dgqa, dgqb], axis=1),
        jnp.concatenate([dgkb, dgka, dsink, jnp.zeros((1, D_MODEL - WIDTH - 2 * A_KV_WIDTH), F32)], axis=1),
        jnp.pad(loss_part, ((0, 0), (0, D_MODEL - LANES))),
        jnp.zeros((PACK_ROWS - 4, D_MODEL), F32)], axis=0)
    order = (chip ^ jnp.array(RELATIONS, jnp.int32)).astype(jnp.int32)
    win_sum, wout_sum, small_recv = _grad_reduce(order, h_t, wins, gw_out, small)
    shift = jnp.array(WIN_SHIFT, jnp.int32)[chip].reshape(1)
    g_w_out = wout_sum.reshape(OUT_ROWS, D_MODEL)

    g_w_in, d_w_in, nm_w_in, nv_w_in = (
        a.T for a in _adamw_window(w_in_t, win_sum, shift, m_w_in_t, v_w_in_t, "adamw_w_in"))
    d_w_out, nm_w_out, nv_w_out = _adamw(w_out[0], g_w_out, m_w_out[0], v_w_out[0], "adamw_w_out")
    g_s, d_s, nm_s, nv_s, loss_row = _small_adamw(
        small_recv,
        (norm_gain, q_norm_a, k_norm_a, sinks_a, q_norm_b, k_norm_b),
        (m_norm_gain, m_q_norm_a, m_k_norm_a, m_sinks_a, m_q_norm_b, m_k_norm_b),
        (v_norm_gain, v_q_norm_a, v_k_norm_a, v_sinks_a, v_q_norm_b, v_k_norm_b))
    loss = loss_row[0, 0]

    def leaves(small_ones, big_in, big_out):
        return (small_ones[0], big_in[None]) + tuple(small_ones[1:]) + (big_out[None],)

    return ((loss, gx[None]) + leaves(g_s, g_w_in, g_w_out) + leaves(d_s, d_w_in, d_w_out)
            + leaves(nm_s, nm_w_in, nm_w_out) + leaves(nv_s, nv_w_in, nv_w_out))
```

```python
import jax
import jax.numpy as jnp
from jax import lax
from jax.experimental import pallas as pl
from jax.experimental.pallas import tpu as pltpu

F32 = jnp.float32
BF16 = jnp.bfloat16

D_MODEL = 1024
HEAD_DIM = 64
HEADS = 8
WIDTH = HEADS * HEAD_DIM
A_KV_WIDTH = 2 * HEAD_DIM
BLOCK = 128
LANES = 128
FOLD = 16
A_MAX_DIST = 127
B_MAX_DIST = 128
ROPE_THETA = 10000.0
ROPE_SPLIT = 64
EPS = 1e-6
NEG = -1e30
SCALE = HEAD_DIM ** -0.5

IN_WIDTH = 3328
C_QA, C_KA, C_VA, C_GA, C_QB, C_KB, C_VB, C_GB, C_END = 0, 512, 640, 768, 1280, 1792, 2304, 2816, 3328

N_DEV = 8
N_CHIP = 4
MESH = pl.DeviceIdType.MESH
IN_COLS = IN_WIDTH // N_CHIP
WIN = 896
WIN_START = (0, 768, 1664, 2432)
WIN_SHIFT = (0, 64, 0, 64)
OUT_ROWS = D_MODEL // N_CHIP
RELATIONS = (3, 1, 2, 0)

ADAM_LR = 0.001
ADAM_B1 = 0.9
ADAM_B2 = 0.999
ADAM_EPS = 1e-08
ADAM_WD = 0.01
ADAM_STEP = 10

ROW_TILE = 256
FOLD_ROWS = ROW_TILE // FOLD
GRAD_ROWS = 1024
ACC_COLS = 256
VMEM_LIMIT = 56 * 1024 * 1024


def _dot(a, b):
    return jnp.dot(a, b, preferred_element_type=F32)


def _dot_nt(a, b):
    return lax.dot_general(a, b, (((1,), (1,)), ((), ())), preferred_element_type=F32)


def _dot_tn(a, b):
    return lax.dot_general(a, b, (((0,), (0,)), ((), ())), preferred_element_type=F32)


def _head_sum(z, bd):
    w = bd.shape[0]
    zb = z.astype(BF16)
    parts = [_dot(zb[:, a:a + w], bd) for a in range(0, z.shape[1], w)]
    return parts[0] if len(parts) == 1 else jnp.concatenate(parts, axis=1)


def _swap_halves(t):
    w = t.shape[1]
    lane = lax.broadcasted_iota(jnp.int32, t.shape, 1)
    return jnp.where(lane % HEAD_DIM < HEAD_DIM // 2, pltpu.roll(t, w - 32, 1), pltpu.roll(t, 32, 1))


def _qknorm_rope(t, g, cos, sin_s, bd):
    r = lax.rsqrt(_head_sum(t * t, bd) * (1.0 / HEAD_DIM) + EPS)
    n = (t * r) * g
    return n * cos + _swap_halves(n) * sin_s


def _qknorm_rope_bwd(dout, t, g, cos, sin_s, bd):
    dout, t = dout.astype(F32), t.astype(F32)
    dn = dout * cos + _swap_halves(dout * sin_s)
    r = lax.rsqrt(_head_sum(t * t, bd) * (1.0 / HEAD_DIM) + EPS)
    tr = t * r
    u = dn * g
    dt = r * (u - tr * (_head_sum(u * tr, bd) * (1.0 / HEAD_DIM)))
    return dt, dn * tr


def _sigmoid(g):
    return 1.0 / (1.0 + jnp.exp(-g))


def _expand_heads(st):
    t = st.shape[0]
    lane = lax.broadcasted_iota(jnp.int32, (t, LANES), 1)
    chunks = []
    for c in range(WIDTH // LANES):
        chunks.append(jnp.where(lane < HEAD_DIM, st[:, 2 * c:2 * c + 1], st[:, 2 * c + 1:2 * c + 2]))
    return jnp.concatenate(chunks, axis=1)


def _reduce_heads(z):
    t = z.shape[0]
    lane = lax.broadcasted_iota(jnp.int32, (t, LANES), 1)
    out = jnp.zeros((t, LANES), F32)
    for c in range(WIDTH // LANES):
        zc = z[:, c * LANES:(c + 1) * LANES]
        for ph in range(2):
            s = jnp.sum(jnp.where((lane // HEAD_DIM) == ph, zc, 0.0), axis=-1, keepdims=True)
            out = jnp.where(lane == 2 * c + ph, s, out)
    return out


def _fold_scratch(w):
    return pltpu.VMEM((w // LANES, ROW_TILE, LANES), F32)


def _store_folded(out_ref, val, scr, col0=0):
    w = val.shape[1]
    n = w // LANES
    for c in range(n):
        scr[c] = val[:, c * LANES:(c + 1) * LANES]
    for r in range(FOLD):
        piece = [scr[c, pl.ds(r, FOLD_ROWS, stride=FOLD), :] for c in range(n)]
        out_ref[r, :, col0:col0 + w] = (piece[0] if n == 1 else jnp.concatenate(piece, axis=1)).astype(out_ref.dtype)


def _load_folded(in_ref, scr):
    n = in_ref.shape[2] // LANES
    for r in range(FOLD):
        blk = in_ref[r].astype(F32)
        for c in range(n):
            scr[c, pl.ds(r, FOLD_ROWS, stride=FOLD), :] = blk[:, c * LANES:(c + 1) * LANES]
    return scr[0] if n == 1 else jnp.concatenate([scr[c] for c in range(n)], axis=1)


def _fold_matrix():
    f = jnp.arange(ROW_TILE)
    return (jnp.arange(ROW_TILE)[None, :] == (FOLD * (f % FOLD_ROWS) + f // FOLD_ROWS)[:, None]).astype(BF16)


def _store_folded_bf16(out_ref, val, perm):
    folded = _dot(perm, val.astype(BF16)).astype(out_ref.dtype)
    for r in range(FOLD):
        out_ref[r] = folded[r * FOLD_ROWS:(r + 1) * FOLD_ROWS]


def _load_folded_bf16(in_ref, perm):
    blk = jnp.concatenate([in_ref[r] for r in range(FOLD)], axis=0)
    return _dot(perm, blk)


def _rows(w, tm=ROW_TILE):
    return pl.BlockSpec((tm, w), lambda i: (i, 0))


def _folded_rows(w):
    return pl.BlockSpec((FOLD, FOLD_ROWS, w), lambda i: (0, i, 0))


def _whole(shape):
    return pl.BlockSpec(shape, lambda i: (0,) * len(shape))


def _inproj(x2, gain, w_bf, cos, sin_s, gqa, gka, gqb, gkb, bd256, bd128, w_out_blk):
    s = x2.shape[0]
    tm = ROW_TILE
    n_steps = s // tm
    n_rel = N_CHIP - 1
    o_half = OUT_ROWS // 2

    def body(x_ref, gain_ref, w_hbm, cos_ref, sin_ref, gqa_ref, gka_ref, gqb_ref, gkb_ref, bd256_ref, bd128_ref,
             wout_ref, qa_ref, kva_ref, qb_ref, kvb_ref, qbf_ref, kvbf_ref,
             qa_raw_ref, ka_raw_ref, ga_ref, qb_raw_ref, kb_raw_ref, gb_ref, wout_all_ref,
             w_vmem, scr, land, ici_send, ici_recv, d2d_send, d2d_recv):
        i = pl.program_id(0)
        px_, py_, c = _position()
        b = 2 * px_ + py_

        def piece(chip_idx, core):
            return land.at[chip_idx, pl.ds(pl.multiple_of(core * o_half, o_half), o_half)]

        def other_chip(d):
            ox, oy = px_ ^ (d >> 1), py_ ^ (d & 1)
            return ox, oy, 2 * ox + oy

        def ici_copy(d, chip_idx):
            ox, oy, _ = other_chip(d)
            return pltpu.make_async_remote_copy(
                src_ref=piece(chip_idx, c), dst_ref=piece(chip_idx, c), send_sem=ici_send.at[d - 1],
                recv_sem=ici_recv.at[d - 1], device_id=(ox, oy, c), device_id_type=MESH)

        def d2d_copy(d, core):
            return pltpu.make_async_remote_copy(
                src_ref=piece(other_chip(d)[2], core), dst_ref=piece(other_chip(d)[2], core),
                send_sem=d2d_send.at[d - 1], recv_sem=d2d_recv.at[d - 1], device_id=(px_, py_, 1 - c),
                device_id_type=MESH)

        @pl.when(i == 0)
        def _():
            pltpu.sync_copy(w_hbm, w_vmem)
            land[b] = wout_ref[...].astype(BF16)
            for d in range(1, N_CHIP):
                ici_copy(d, b).start()

        @pl.when(i == n_steps // 2)
        def _():
            for d in range(1, N_CHIP):
                ici_copy(d, other_chip(d)[2]).wait_recv()
                d2d_copy(d, c).start()

        @pl.when(i == n_steps - 1)
        def _():
            for d in range(1, N_CHIP):
                d2d_copy(d, 1 - c).wait_recv()
            for d in range(1, N_CHIP):
                ici_copy(d, b).wait_send()
                d2d_copy(d, c).wait_send()
            for k in range(N_CHIP):
                wout_all_ref[k * OUT_ROWS:(k + 1) * OUT_ROWS, :] = land[k]

        xt = x_ref[...]
        r = lax.rsqrt(jnp.mean(xt * xt, axis=-1, keepdims=True) + EPS)
        h = ((xt * r) * gain_ref[...]).astype(BF16)
        cos1 = cos_ref[...]
        sin1 = sin_ref[...]
        cos4 = jnp.tile(cos1, (1, 4))
        sin4 = jnp.tile(sin1, (1, 4))

        def seg(a, b):
            return _dot_nt(h, w_vmem[a:b, :])

        t = seg(C_QA, C_KA)
        qa_raw_ref[...] = t.astype(BF16)
        qa_ref[...] = (_qknorm_rope(t, gqa_ref[...], cos4, sin4, bd256_ref[...]) * SCALE).astype(BF16)
        t = seg(C_KA, C_VA)
        ka_raw_ref[...] = t.astype(BF16)
        kva_ref[:, :A_KV_WIDTH] = _qknorm_rope(t, gka_ref[...], cos1, sin1, bd128_ref[...]).astype(BF16)
        kva_ref[:, A_KV_WIDTH:] = seg(C_VA, C_GA).astype(BF16)
        ga_ref[...] = seg(C_GA, C_QB).astype(BF16)
        t = seg(C_QB, C_KB)
        qb_raw_ref[...] = t.astype(BF16)
        t = _qknorm_rope(t, gqb_ref[...], cos4, sin4, bd256_ref[...]) * SCALE
        qb_ref[...] = t.astype(BF16)
        _store_folded(qbf_ref, t, scr)
        t = seg(C_KB, C_VB)
        kb_raw_ref[...] = t.astype(BF16)
        t = _qknorm_rope(t, gkb_ref[...], cos4, sin4, bd256_ref[...])
        kvb_ref[:, :WIDTH] = t.astype(BF16)
        _store_folded(kvbf_ref, t, scr)
        t = seg(C_VB, C_GB)
        kvb_ref[:, WIDTH:] = t.astype(BF16)
        _store_folded(kvbf_ref, t, scr, WIDTH)
        gb_ref[...] = seg(C_GB, C_END).astype(BF16)

    sds = jax.ShapeDtypeStruct
    ln = s // FOLD
    out_shape = (sds((s, WIDTH), BF16), sds((s, 2 * A_KV_WIDTH), BF16), sds((s, WIDTH), BF16),
                 sds((s, 2 * WIDTH), BF16), sds((FOLD, ln, WIDTH), BF16), sds((FOLD, ln, 2 * WIDTH), BF16),
                 sds((s, WIDTH), BF16), sds((s, A_KV_WIDTH), BF16), sds((s, WIDTH), BF16),
                 sds((s, WIDTH), BF16), sds((s, WIDTH), BF16), sds((s, WIDTH), BF16),
                 sds((D_MODEL, D_MODEL), BF16))
    out_specs = (_rows(WIDTH), _rows(2 * A_KV_WIDTH), _rows(WIDTH), _rows(2 * WIDTH),
                 _folded_rows(WIDTH), _folded_rows(2 * WIDTH),
                 _rows(WIDTH), _rows(A_KV_WIDTH), _rows(WIDTH), _rows(WIDTH), _rows(WIDTH), _rows(WIDTH),
                 _whole((D_MODEL, D_MODEL)))
    dma = pltpu.SemaphoreType.DMA
    return pl.pallas_call(
        body, name="inproj_fwd", grid=(n_steps,),
        in_specs=[_rows(D_MODEL), _whole(gain.shape), pl.BlockSpec(memory_space=pl.ANY), _rows(LANES), _rows(LANES),
                  _whole(gqa.shape), _whole(gka.shape), _whole(gqb.shape), _whole(gkb.shape), _whole(bd256.shape),
                  _whole(bd128.shape), _whole(w_out_blk.shape)],
        out_specs=out_specs, out_shape=out_shape,
        scratch_shapes=[pltpu.VMEM((IN_WIDTH, D_MODEL), BF16), _fold_scratch(WIDTH),
                        pltpu.VMEM((N_CHIP, OUT_ROWS, D_MODEL), BF16),
                        dma((n_rel,)), dma((n_rel,)), dma((n_rel,)), dma((n_rel,))],
        compiler_params=pltpu.CompilerParams(dimension_semantics=("arbitrary",), vmem_limit_bytes=VMEM_LIMIT),
    )(x2, gain, w_bf, cos, sin_s, gqa, gka, gqb, gkb, bd256, bd128, w_out_blk)


def _seq_pos(idx, dil):
    if dil == 4:
        return 4 * (idx % 32) + idx // 32
    return idx


SOFTMAX_ROWS = 64


def _upper_mask(dil, r0=0, rows=2 * BLOCK):
    qi = (lax.broadcasted_iota(jnp.int32, (rows, BLOCK), 0) + r0) % BLOCK
    kj = lax.broadcasted_iota(jnp.int32, (rows, BLOCK), 1)
    return _seq_pos(kj, dil) > _seq_pos(qi, dil)


def _eye_mask(r0=0, rows=2 * BLOCK):
    qi = (lax.broadcasted_iota(jnp.int32, (rows, BLOCK), 0) + r0) % BLOCK
    kj = lax.broadcasted_iota(jnp.int32, (rows, BLOCK), 1)
    return qi == kj


def _stack_heads(a2, c, gqa):
    lane = lax.broadcasted_iota(jnp.int32, (1, LANES), 1) // HEAD_DIM
    zero = jnp.zeros_like(a2)
    if gqa:
        keep = lane == (c // 2)
        return jnp.concatenate([jnp.where(keep, a2, zero), jnp.where(keep, _swap_heads(a2), zero)], axis=0)
    return jnp.concatenate([jnp.where(lane == 0, a2, zero), jnp.where(lane == 1, a2, zero)], axis=0)


def _unstack_heads(a, c, gqa):
    lane = lax.broadcasted_iota(jnp.int32, (1, LANES), 1) // HEAD_DIM
    if gqa:
        return jnp.where(lane == (c // 2), a[:BLOCK], _swap_heads(a[BLOCK:]))
    return jnp.where(lane == 0, a[:BLOCK], a[BLOCK:])


def _stacked_head_ids(c, gqa):
    if gqa:
        return 2 * c + c // 2, 2 * c + 1 - c // 2
    return 2 * c, 2 * c + 1


def _per_head_rows(blk, heads):
    return jnp.concatenate([blk[:, heads[0]:heads[0] + 1], blk[:, heads[1]:heads[1] + 1]], axis=0)


def _attn_view(a, dil):
    if dil == 1:
        return a[None]
    if dil == 4:
        return a.reshape(4, 4, a.shape[1], a.shape[2])
    return a


def _attn_unview(a, dil):
    if dil == 1:
        return a[0]
    if dil == 4:
        return a.reshape(FOLD, a.shape[2], a.shape[3])
    return a


ATTN_BLOCKS_PER_STEP = 8


def _attn_specs(dil):
    if dil == 4:
        def spec(n, fn):
            return lambda w: pl.BlockSpec((4, None, n * BLOCK // 4, w), lambda r, i: (0, r, fn(i), 0))
    else:
        def spec(n, fn):
            return lambda w: pl.BlockSpec((None, n * BLOCK, w), lambda r, i: (r, fn(i), 0))
    return spec


def _blk_rows(g, dil):
    n = BLOCK // 4 if dil == 4 else BLOCK
    if isinstance(g, int):
        return slice(g * n, (g + 1) * n)
    return pl.ds(pl.multiple_of(g * n, n), n)


def _blk_load(ref, sl, dil, g=0):
    if dil == 4:
        return ref[:, _blk_rows(g, dil), sl].reshape(BLOCK, sl.stop - sl.start)
    return ref[_blk_rows(g, dil), sl]


def _blk_store(ref, sl, val, dil, g=0):
    val = val.astype(ref.dtype)
    if dil == 4:
        ref[:, _blk_rows(g, dil), sl] = val.reshape(4, BLOCK // 4, sl.stop - sl.start)
    else:
        ref[_blk_rows(g, dil), sl] = val


def _swap_heads(a):
    return pltpu.roll(a.astype(F32), HEAD_DIM, 1).astype(a.dtype)


STAT_SHIFT = 8
PROB_WIDTH = HEADS * 2 * BLOCK


def _attn_fwd(q, kv, sinks, *, dil, max_dist, name):
    q, kv = _attn_view(q, dil), _attn_view(kv, dil)
    kw = kv.shape[-1] // 2
    gqa = kw == A_KV_WIDTH
    n_seq = dil
    nb = (q.shape[-2] * (4 if dil == 4 else 1)) // BLOCK
    per_step = min(ATTN_BLOCKS_PER_STEP, nb)
    with_sinks = sinks is not None
    all_lanes = slice(0, LANES)
    assert max_dist in (BLOCK - 1, BLOCK) and nb % per_step == 0
    diag = max_dist == BLOCK

    def body(*refs):
        if with_sinks:
            q_ref, kvp_ref, kvc_ref, sink_ref, o_ref, ml_ref, p_ref = refs
        else:
            q_ref, kvp_ref, kvc_ref, o_ref, ml_ref, p_ref = refs

        def block(g, has_prev):
            prev_ref, prev_g = (kvp_ref, 0) if isinstance(g, int) else (kvc_ref, g - 1)
            lane = lax.broadcasted_iota(jnp.int32, (1, LANES), 1)
            with_diag = diag and has_prev
            upper, eye = _upper_mask(dil), _eye_mask()
            first_rows = lax.broadcasted_iota(jnp.int32, (2 * BLOCK, 1), 0) < BLOCK
            ml_blk = jnp.zeros((BLOCK, LANES), F32)
            chunks = range(WIDTH // LANES)
            scores, values = [], []
            for c in chunks:
                sl = slice(c * LANES, (c + 1) * LANES)
                ksl = slice(0, LANES) if gqa else sl
                vsl = slice(ksl.start + kw, ksl.stop + kw)
                kcur, vcur = _blk_load(kvc_ref, ksl, dil, g), _blk_load(kvc_ref, vsl, dil, g)
                qs = _stack_heads(_blk_load(q_ref, sl, dil, g), c, gqa)
                if has_prev:
                    kcur = jnp.concatenate([_blk_load(prev_ref, ksl, dil, prev_g), kcur], axis=0)
                    vcur = jnp.concatenate([_blk_load(prev_ref, vsl, dil, prev_g), vcur], axis=0)
                scores.append(_dot_nt(qs, kcur))
                values.append(vcur)
            probs = []
            for c in chunks:
                heads = _stacked_head_ids(c, gqa)
                s = scores[c]
                if has_prev:
                    s_p = s[:, :BLOCK]
                    sc = jnp.where(upper, s_p, s[:, BLOCK:])
                else:
                    sc = jnp.where(upper, NEG, s)
                if with_diag:
                    sd = jnp.where(eye, s_p, NEG)
                    m = jnp.max(jnp.maximum(sc, sd), axis=-1, keepdims=True)
                else:
                    m = jnp.max(sc, axis=-1, keepdims=True)
                if with_sinks:
                    sk = jnp.where(first_rows, sink_ref[0, heads[0]], sink_ref[0, heads[1]])
                    m = jnp.maximum(m, sk)
                p = jnp.exp(sc - m)
                zero = jnp.zeros_like(p)
                if with_diag:
                    pd = jnp.exp(sd - m)
                    l = jnp.sum(p + pd, axis=-1, keepdims=True)
                else:
                    pd = zero
                    l = jnp.sum(p, axis=-1, keepdims=True)
                if with_sinks:
                    l = l + jnp.exp(sk - m)
                pf = jnp.where(upper, zero, p)
                if has_prev:
                    pf = jnp.concatenate([jnp.where(upper, p, pd), pf], axis=1)
                probs.append(pf.astype(BF16))
                for n, h in enumerate(heads):
                    rows = slice(n * BLOCK, (n + 1) * BLOCK)
                    ml_blk = jnp.where(lane == h, m[rows], ml_blk)
                    ml_blk = jnp.where(lane == h + STAT_SHIFT, l[rows], ml_blk)
            for c in chunks:
                sl = slice(c * LANES, (c + 1) * LANES)
                _blk_store(o_ref, sl, _unstack_heads(_dot(probs[c], values[c]), c, gqa), dil, g)
                pf = probs[c]
                if not has_prev:
                    pf = jnp.concatenate([jnp.zeros_like(pf), pf], axis=1)
                for n in range(2):
                    cols = slice((2 * c + n) * 2 * BLOCK, (2 * c + n + 1) * 2 * BLOCK)
                    _blk_store(p_ref, cols, pf[n * BLOCK:(n + 1) * BLOCK], dil, g)
            _blk_store(ml_ref, all_lanes, ml_blk, dil, g)

        @pl.when(pl.program_id(1) == 0)
        def _():
            block(0, False)

        @pl.when(pl.program_id(1) > 0)
        def _():
            block(0, True)

        if per_step > 1:
            def rest(g, carry):
                block(g, True)
                return carry

            lax.fori_loop(1, per_step, rest, 0)

    spec = _attn_specs(dil)
    cur = spec(per_step, lambda i: i)
    prev = spec(1, lambda i: jnp.maximum(i * per_step - 1, 0))
    in_specs = [cur(WIDTH), prev(2 * kw), cur(2 * kw)]
    args = [q, kv, kv]
    if with_sinks:
        in_specs.append(pl.BlockSpec(memory_space=pltpu.SMEM))
        args.append(sinks)
    stats = jax.ShapeDtypeStruct(q.shape[:-1] + (LANES,), F32)
    o, ml, p = pl.pallas_call(
        body, name=name, grid=(n_seq, nb // per_step), in_specs=in_specs,
        out_specs=(cur(WIDTH), cur(LANES), cur(PROB_WIDTH)),
        out_shape=(jax.ShapeDtypeStruct(q.shape, BF16), stats,
                   jax.ShapeDtypeStruct(q.shape[:-1] + (PROB_WIDTH,), BF16)),
        compiler_params=pltpu.CompilerParams(dimension_semantics=("arbitrary", "arbitrary"),
                                             vmem_limit_bytes=VMEM_LIMIT),
    )(*args)
    return _attn_unview(o, dil), _attn_unview(ml, dil), _attn_unview(p, dil)


def _attn_bwd(q, kv, do, ld, ml, prob, *, dil, name):
    q, kv, do, ld, ml, prob = (_attn_view(a, dil) for a in (q, kv, do, ld, ml, prob))
    kw = kv.shape[-1] // 2
    gqa = kw == A_KV_WIDTH
    n_seq = dil
    nb = (q.shape[-2] * (4 if dil == 4 else 1)) // BLOCK
    n_kc = kw // LANES
    per_step = min(ATTN_BLOCKS_PER_STEP, nb)
    all_lanes = slice(0, LANES)
    assert nb % per_step == 0

    def body(q_ref, kvp_ref, kvc_ref, do_ref, ld_ref, ml_ref, p_ref, dq_ref, dkv_ref, ck_ref, cv_ref):
        i = pl.program_id(1)

        def block(g, has_prev):
            prev_ref, prev_g = (kvp_ref, 0) if isinstance(g, int) else (kvc_ref, g - 1)
            seq_blk = i * per_step + g
            ld_blk = _blk_load(ld_ref, all_lanes, dil, g)
            head_lanes = lax.broadcasted_iota(jnp.int32, (BLOCK, LANES), 1) < HEADS
            scale_blk = jnp.exp(jnp.where(head_lanes, _blk_load(ml_ref, all_lanes, dil, g) - ld_blk, 0.0))
            dk_acc = [None] * n_kc
            dv_acc = [None] * n_kc
            chunks = range(WIDTH // LANES)
            operands, products = [], []
            for c in chunks:
                sl = slice(c * LANES, (c + 1) * LANES)
                kc = 0 if gqa else c
                ksl = slice(kc * LANES, (kc + 1) * LANES)
                vsl = slice(ksl.start + kw, ksl.stop + kw)
                k2, v2 = _blk_load(kvc_ref, ksl, dil, g), _blk_load(kvc_ref, vsl, dil, g)
                if has_prev:
                    k2 = jnp.concatenate([_blk_load(prev_ref, ksl, dil, prev_g), k2], axis=0)
                    v2 = jnp.concatenate([_blk_load(prev_ref, vsl, dil, prev_g), v2], axis=0)
                qs = _stack_heads(_blk_load(q_ref, sl, dil, g), c, gqa)
                dos = _stack_heads(_blk_load(do_ref, sl, dil, g), c, gqa)
                operands.append((qs, dos, k2))
                products.append(_dot_nt(dos, v2))
            weights = []
            for c in chunks:
                heads = _stacked_head_ids(c, gqa)
                scale2 = _per_head_rows(scale_blk, heads)
                dl2 = _per_head_rows(ld_blk, tuple(h + STAT_SHIFT for h in heads))
                keys = slice(0, 2 * BLOCK) if has_prev else slice(BLOCK, 2 * BLOCK)
                pf = jnp.concatenate([_blk_load(p_ref, slice((2 * c + n) * 2 * BLOCK, (2 * c + n + 1) * 2 * BLOCK),
                                                dil, g)[:, keys] for n in range(2)], axis=0)
                dsf = pf.astype(F32) * (scale2 * (products[c] - dl2))
                dos_scaled = (operands[c][1].astype(F32) * scale2).astype(BF16)
                weights.append((pf, dsf.astype(BF16), dos_scaled))
            for c in chunks:
                sl = slice(c * LANES, (c + 1) * LANES)
                kc = 0 if gqa else c
                qs, dos, k2 = operands[c]
                pf, dsf, dos_scaled = weights[c]
                _blk_store(dq_ref, sl, _unstack_heads(_dot(dsf, k2), c, gqa) * SCALE, dil, g)
                dk2 = _dot_tn(dsf, qs)
                dv2 = _dot_tn(pf, dos_scaled)
                dk_acc[kc] = dk2 if dk_acc[kc] is None else dk_acc[kc] + dk2
                dv_acc[kc] = dv2 if dv_acc[kc] is None else dv_acc[kc] + dv2
            for kc in range(n_kc):
                sl = slice(kc * LANES, (kc + 1) * LANES)
                vsl = slice(sl.start + kw, sl.stop + kw)
                if has_prev:
                    _blk_store(dkv_ref, sl, ck_ref[:, sl] + dk_acc[kc][:BLOCK], dil, seq_blk - 1)
                    _blk_store(dkv_ref, vsl, cv_ref[:, sl] + dv_acc[kc][:BLOCK], dil, seq_blk - 1)
                    ck_ref[:, sl] = dk_acc[kc][BLOCK:]
                    cv_ref[:, sl] = dv_acc[kc][BLOCK:]
                else:
                    ck_ref[:, sl] = dk_acc[kc]
                    cv_ref[:, sl] = dv_acc[kc]

        @pl.when(i == 0)
        def _():
            block(0, False)

        @pl.when(i > 0)
        def _():
            block(0, True)

        if per_step > 1:
            def rest(g, carry):
                block(g, True)
                return carry

            lax.fori_loop(1, per_step, rest, 0)

        @pl.when(i == nb // per_step - 1)
        def _():
            for kc in range(n_kc):
                sl = slice(kc * LANES, (kc + 1) * LANES)
                _blk_store(dkv_ref, sl, ck_ref[:, sl], dil, nb - 1)
                _blk_store(dkv_ref, slice(sl.start + kw, sl.stop + kw), cv_ref[:, sl], dil, nb - 1)

    spec = _attn_specs(dil)
    cur = spec(per_step, lambda i: i)
    prev = spec(1, lambda i: jnp.maximum(i * per_step - 1, 0))
    if dil == 4:
        whole = pl.BlockSpec((4, None, kv.shape[2], 2 * kw), lambda r, i: (0, r, 0, 0))
    else:
        whole = pl.BlockSpec((None, kv.shape[1], 2 * kw), lambda r, i: (r, 0, 0))
    sds = jax.ShapeDtypeStruct
    dq, dkv = pl.pallas_call(
        body, name=name, grid=(n_seq, nb // per_step),
        in_specs=[cur(WIDTH), prev(2 * kw), cur(2 * kw), cur(WIDTH), cur(LANES), cur(LANES), cur(PROB_WIDTH)],
        out_specs=(cur(WIDTH), whole),
        out_shape=(sds(q.shape, BF16), sds(kv.shape, BF16)),
        scratch_shapes=[pltpu.VMEM((BLOCK, kw), F32), pltpu.VMEM((BLOCK, kw), F32)],
        compiler_params=pltpu.CompilerParams(dimension_semantics=("arbitrary", "arbitrary"),
                                             vmem_limit_bytes=VMEM_LIMIT),
    )(q, kv, kv, do, ld, ml, prob)
    return _attn_unview(dq, dil), _attn_unview(dkv, dil)


def _outproj(att_a, att_b1, att_b4, att_b16, g_a, g_b, x2, tgt2, w_out_bf, sink_row, perm):
    s = x2.shape[0]
    tm = ROW_TILE

    def body(oa_ref, mla_ref, ob1_ref, ml1_ref, ob4_ref, ml4_ref, ob16_ref, ml16_ref,
             ga_ref, gb_ref, x_ref, t_ref, w_ref, sink_ref, perm_ref,
             dy_ref, doa_ref, dob_ref, dobf_ref, dga_ref, dgb_ref, lda_ref, ldb_ref, ldbf_ref,
             gw_ref, loss_ref, dsink_ref, scr_st):
        i = pl.program_id(0)
        perm = perm_ref[...]
        lane = lax.broadcasted_iota(jnp.int32, (tm, LANES), 1)
        used = lane < HEADS

        def split(ml):
            return jnp.where(used, ml, 0.0), jnp.where(used, pltpu.roll(ml, LANES - STAT_SHIFT, 1), 1.0)

        @pl.when(i == 0)
        def _():
            gw_ref[...] = jnp.zeros_like(gw_ref)
            loss_ref[...] = jnp.zeros_like(loss_ref)
            dsink_ref[...] = jnp.zeros_like(dsink_ref)

        ms, ls = zip(split(ml1_ref[...]), split(_load_folded(ml4_ref, scr_st)), split(_load_folded(ml16_ref, scr_st)))
        mx = jnp.maximum(jnp.maximum(ms[0], ms[1]), ms[2])
        scale = [jnp.exp(mp - mx) for mp in ms]
        den = (ls[0] * scale[0] + ls[1] * scale[1]) + ls[2] * scale[2]
        lse_b = jnp.where(used, mx + jnp.log(den), 0.0)
        inv_den = 1.0 / den
        o_b = _expand_heads(scale[0] * inv_den) * ob1_ref[...].astype(F32)
        o_b = o_b + _expand_heads(scale[1] * inv_den) * _load_folded_bf16(ob4_ref, perm)
        o_b = o_b + _expand_heads(scale[2] * inv_den) * _load_folded_bf16(ob16_ref, perm)
        m_a, l_a = split(mla_ref[...])
        lse_a = jnp.where(used, m_a + jnp.log(l_a), 0.0)
        o_a = _expand_heads(1.0 / l_a) * oa_ref[...].astype(F32)
        g_a = ga_ref[...].astype(F32)
        g_b = gb_ref[...].astype(F32)
        sg_a = _sigmoid(g_a)
        sg_b = _sigmoid(g_b)
        silu_a = g_a * sg_a
        silu_b = g_b * sg_b
        mixed = jnp.concatenate([o_a * silu_a, o_b * silu_b], axis=1).astype(BF16)
        w = w_ref[...]
        y = x_ref[...] + _dot(mixed, w)
        diff = y - t_ref[...]
        loss_ref[...] += (0.5 / D_MODEL) * jnp.sum(diff * diff)
        dy = diff * (1.0 / D_MODEL)
        dy_ref[...] = dy
        dyb = dy.astype(BF16)
        gw_ref[...] += _dot_tn(mixed, dyb)
        dmixed = _dot_nt(dyb, w)
        dm_a = dmixed[:, :WIDTH]
        dm_b = dmixed[:, WIDTH:]
        do_a = dm_a * silu_a
        do_b = dm_b * silu_b
        doa_ref[...] = do_a.astype(BF16)
        dob_ref[...] = do_b.astype(BF16)
        _store_folded_bf16(dobf_ref, do_b, perm)
        dga_ref[...] = (dm_a * o_a * (sg_a * (1.0 + g_a * (1.0 - sg_a)))).astype(BF16)
        dgb_ref[...] = (dm_b * o_b * (sg_b * (1.0 + g_b * (1.0 - sg_b)))).astype(BF16)
        dl_a = _reduce_heads(do_a * o_a)
        dl_b = _reduce_heads(do_b * o_b)
        lda_ref[...] = lse_a + pltpu.roll(dl_a, STAT_SHIFT, 1)
        ld_b = lse_b + pltpu.roll(dl_b, STAT_SHIFT, 1)
        ldb_ref[...] = ld_b
        _store_folded(ldbf_ref, ld_b, scr_st)
        dsink_ref[...] -= jnp.sum(jnp.exp(sink_ref[...] - lse_a) * dl_a, axis=0, keepdims=True)

    sds = jax.ShapeDtypeStruct
    ln = s // FOLD
    natural = [_rows(WIDTH), _rows(LANES)]
    folded = [_folded_rows(WIDTH), _folded_rows(LANES)]
    return pl.pallas_call(
        body, name="outproj_fwd_bwd", grid=(s // tm,),
        in_specs=natural + natural + folded + folded
                 + [_rows(WIDTH), _rows(WIDTH), _rows(D_MODEL), _rows(D_MODEL), _whole((D_MODEL, D_MODEL)),
                    _whole((1, LANES)), _whole(perm.shape)],
        out_specs=(_rows(D_MODEL), _rows(WIDTH), _rows(WIDTH), _folded_rows(WIDTH), _rows(WIDTH), _rows(WIDTH),
                   _rows(LANES), _rows(LANES), _folded_rows(LANES),
                   _whole((D_MODEL, D_MODEL)), _whole((1, LANES)), _whole((1, LANES))),
        out_shape=(sds((s, D_MODEL), F32), sds((s, WIDTH), BF16), sds((s, WIDTH), BF16),
                   sds((FOLD, ln, WIDTH), BF16), sds((s, WIDTH), BF16), sds((s, WIDTH), BF16),
                   sds((s, LANES), F32), sds((s, LANES), F32), sds((FOLD, ln, LANES), F32),
                   sds((D_MODEL, D_MODEL), F32), sds((1, LANES), F32), sds((1, LANES), F32)),
        scratch_shapes=[_fold_scratch(LANES)],
        compiler_params=pltpu.CompilerParams(dimension_semantics=("arbitrary",), vmem_limit_bytes=VMEM_LIMIT),
    )(*att_a, *att_b1, *att_b4, *att_b16, g_a, g_b, x2, tgt2, w_out_bf, sink_row, perm)


def _inproj_bwd(x2, dy, gain, w_bf, cos, sin_s, gqa, gka, gqb, gkb, bd256, bd128, perm,
                qa_raw, ka_raw, qb_raw, kb_raw, d_a, d_b1, d_b4, d_b16, dg_a, dg_b):
    s = x2.shape[0]
    tm = ROW_TILE

    def body(x_ref, dy_ref, gain_ref, w_hbm, cos_ref, sin_ref, gqa_ref, gka_ref, gqb_ref, gkb_ref, bd256_ref,
             bd128_ref, perm_ref, qa_raw_ref, ka_raw_ref, qb_raw_ref, kb_raw_ref, dqa_ref, dkva_ref,
             dq1_ref, dkv1_ref, dq4_ref, dkv4_ref, dq16_ref, dkv16_ref, dga_ref, dgb_ref,
             gx_ref, ht_ref, win_ref,
             dgain_ref, dgqa_ref, dgka_ref, dgqb_ref, dgkb_ref, w_vmem, dproj_ref):
        i = pl.program_id(0)
        perm = perm_ref[...]

        @pl.when(i == 0)
        def _():
            pltpu.sync_copy(w_hbm, w_vmem)
            dgain_ref[...] = jnp.zeros_like(dgain_ref)
            dgqa_ref[...] = jnp.zeros_like(dgqa_ref)
            dgka_ref[...] = jnp.zeros_like(dgka_ref)
            dgqb_ref[...] = jnp.zeros_like(dgqb_ref)
            dgkb_ref[...] = jnp.zeros_like(dgkb_ref)

        cos1 = cos_ref[...]
        sin1 = sin_ref[...]
        cos4 = jnp.tile(cos1, (1, 4))
        sin4 = jnp.tile(sin1, (1, 4))

        dt, dg = _qknorm_rope_bwd(dqa_ref[...], qa_raw_ref[...], gqa_ref[...], cos4, sin4, bd256_ref[...])
        dproj_ref[:, C_QA:C_KA] = dt.astype(BF16)
        dgqa_ref[...] += jnp.sum(dg, axis=0, keepdims=True)
        dt, dg = _qknorm_rope_bwd(dkva_ref[:, :A_KV_WIDTH], ka_raw_ref[...], gka_ref[...], cos1, sin1,
                                  bd128_ref[...])
        dproj_ref[:, C_KA:C_VA] = dt.astype(BF16)
        dgka_ref[...] += jnp.sum(dg, axis=0, keepdims=True)
        dproj_ref[:, C_VA:C_GA] = dkva_ref[:, A_KV_WIDTH:]
        dproj_ref[:, C_GA:C_QB] = dga_ref[...]
        dq = (dq1_ref[...].astype(F32) + _load_folded_bf16(dq4_ref, perm)) + _load_folded_bf16(dq16_ref, perm)
        dt, dg = _qknorm_rope_bwd(dq, qb_raw_ref[...], gqb_ref[...], cos4, sin4, bd256_ref[...])
        dproj_ref[:, C_QB:C_KB] = dt.astype(BF16)
        dgqb_ref[...] += jnp.sum(dg, axis=0, keepdims=True)
        dkv = (dkv1_ref[...].astype(F32) + _load_folded_bf16(dkv4_ref, perm)) + _load_folded_bf16(dkv16_ref, perm)
        dt, dg = _qknorm_rope_bwd(dkv[:, :WIDTH], kb_raw_ref[...], gkb_ref[...], cos4, sin4, bd256_ref[...])
        dproj_ref[:, C_KB:C_VB] = dt.astype(BF16)
        dgkb_ref[...] += jnp.sum(dg, axis=0, keepdims=True)
        dproj_ref[:, C_VB:C_GB] = dkv[:, WIDTH:].astype(BF16)
        dproj_ref[:, C_GB:C_END] = dgb_ref[...]
        for k, start in enumerate(WIN_START):
            win_ref[k] = dproj_ref[:, start:start + WIN]

        xt = x_ref[...]
        gain_row = gain_ref[...]
        r = lax.rsqrt(jnp.mean(xt * xt, axis=-1, keepdims=True) + EPS)
        xr = xt * r
        ht_ref[...] = (xr * gain_row).T.astype(BF16)
        dh = _dot(dproj_ref[...], w_vmem[...])
        dgain_ref[...] += jnp.sum(dh * xr, axis=0, keepdims=True)
        u = dh * gain_row
        gx_ref[...] = dy_ref[...] + r * (u - xr * jnp.mean(u * xr, axis=-1, keepdims=True))

    def acc_row(w):
        return pl.BlockSpec((1, w), lambda i: (0, 0))

    sds = jax.ShapeDtypeStruct
    any_spec = pl.BlockSpec(memory_space=pl.ANY)
    win_spec = pl.BlockSpec((N_CHIP, tm, WIN), lambda i: (0, i, 0))
    return pl.pallas_call(
        body, name="inproj_bwd", grid=(s // tm,),
        in_specs=[_rows(D_MODEL), _rows(D_MODEL), _whole(gain.shape), any_spec, _rows(LANES), _rows(LANES),
                  _whole(gqa.shape), _whole(gka.shape), _whole(gqb.shape), _whole(gkb.shape), _whole(bd256.shape),
                  _whole(bd128.shape), _whole(perm.shape),
                  _rows(WIDTH), _rows(A_KV_WIDTH), _rows(WIDTH), _rows(WIDTH),
                  _rows(WIDTH), _rows(2 * A_KV_WIDTH), _rows(WIDTH), _rows(2 * WIDTH)]
                 + [_folded_rows(WIDTH), _folded_rows(2 * WIDTH)] * 2 + [_rows(WIDTH), _rows(WIDTH)],
        out_specs=(_rows(D_MODEL), pl.BlockSpec((D_MODEL, tm), lambda i: (0, i)), win_spec, acc_row(D_MODEL), acc_row(WIDTH), acc_row(A_KV_WIDTH), acc_row(WIDTH), acc_row(WIDTH)),
        out_shape=(sds((s, D_MODEL), F32), sds((D_MODEL, s), BF16), sds((N_CHIP, s, WIN), BF16),
                   sds((1, D_MODEL), F32),
                   sds((1, WIDTH), F32), sds((1, A_KV_WIDTH), F32), sds((1, WIDTH), F32), sds((1, WIDTH), F32)),
        scratch_shapes=[pltpu.VMEM((IN_WIDTH, D_MODEL), BF16), pltpu.VMEM((tm, IN_WIDTH), BF16)],
        compiler_params=pltpu.CompilerParams(dimension_semantics=("arbitrary",), vmem_limit_bytes=VMEM_LIMIT),
    )(x2, dy, gain, w_bf, cos, sin_s, gqa, gka, gqb, gkb, bd256, bd128, perm, qa_raw, ka_raw, qb_raw, kb_raw,
      *d_a, *d_b1, *d_b4, *d_b16, dg_a, dg_b)


def _rope_tables(s):
    half = HEAD_DIM // 2
    inv = jnp.tile(ROPE_THETA ** (-jnp.arange(half, dtype=F32) / half), 4)
    sign = jnp.tile(jnp.concatenate([-jnp.ones((half,), F32), jnp.ones((half,), F32)]), 2)
    hi = (jnp.arange(s // ROPE_SPLIT) * ROPE_SPLIT).astype(F32)[:, None] * inv[None, :]
    lo = jnp.arange(ROPE_SPLIT).astype(F32)[:, None] * inv[None, :]
    ch, sh, cl, sl = jnp.cos(hi)[:, None, :], jnp.sin(hi)[:, None, :], jnp.cos(lo)[None], jnp.sin(lo)[None]
    cos = (ch * cl - sh * sl).reshape(s, LANES)
    sin = (sh * cl + ch * sl).reshape(s, LANES)
    return cos, sin * sign[None, :]


def _block_diag_ones(w):
    idx = jnp.arange(w) // HEAD_DIM
    return (idx[:, None] == idx[None, :]).astype(BF16)


def _local_step(x2, tgt2, norm_gain, w_in_bf, q_norm_a, k_norm_a, sinks_a, q_norm_b, k_norm_b, w_out_blk):
    s = x2.shape[0]
    cos, sin_s = _rope_tables(s)
    bd256, bd128 = _block_diag_ones(2 * LANES), _block_diag_ones(A_KV_WIDTH)
    gqa = jnp.tile(q_norm_a, (1, HEADS))
    gka = jnp.tile(k_norm_a, (1, 2))
    gqb = jnp.tile(q_norm_b, (1, HEADS))
    gkb = jnp.tile(k_norm_b, (1, HEADS))
    sink_row = jnp.pad(sinks_a, ((0, 0), (0, LANES - HEADS)))
    perm = _fold_matrix()

    (qa, kva, qb, kvb, qbf, kvbf, qa_raw, ka_raw, g_a, qb_raw, kb_raw, g_b, w_out_bf) = _inproj(
        x2, norm_gain, w_in_bf, cos, sin_s, gqa, gka, gqb, gkb, bd256, bd128, w_out_blk)

    att_a = _attn_fwd(qa, kva, sinks_a, dil=1, max_dist=A_MAX_DIST, name="attn_a_fwd")
    att_b1 = _attn_fwd(qb, kvb, None, dil=1, max_dist=B_MAX_DIST, name="attn_b1_fwd")
    att_b4 = _attn_fwd(qbf, kvbf, None, dil=4, max_dist=B_MAX_DIST, name="attn_b4_fwd")
    att_b16 = _attn_fwd(qbf, kvbf, None, dil=16, max_dist=B_MAX_DIST, name="attn_b16_fwd")

    (dy, do_a, do_b, do_bf, dg_a, dg_b, ld_a, ld_b, ld_bf, gw_out, loss_part, dsink) = _outproj(
        att_a[:2], att_b1[:2], att_b4[:2], att_b16[:2], g_a, g_b, x2, tgt2, w_out_bf, sink_row, perm)

    d_a = _attn_bwd(qa, kva, do_a, ld_a, *att_a[1:], dil=1, name="attn_a_bwd")
    d_b1 = _attn_bwd(qb, kvb, do_b, ld_b, *att_b1[1:], dil=1, name="attn_b1_bwd")
    d_b4 = _attn_bwd(qbf, kvbf, do_bf, ld_bf, *att_b4[1:], dil=4, name="attn_b4_bwd")
    d_b16 = _attn_bwd(qbf, kvbf, do_bf, ld_bf, *att_b16[1:], dil=16, name="attn_b16_bwd")

    gx, h_t, wins, dgain, dgqa, dgka, dgqb, dgkb = _inproj_bwd(
        x2, dy, norm_gain, w_in_bf, cos, sin_s, gqa, gka, gqb, gkb, bd256, bd128, perm,
        qa_raw, ka_raw, qb_raw, kb_raw, d_a, d_b1, d_b4, d_b16, dg_a, dg_b)
    return loss_part, gx, h_t, wins, gw_out, (dgain, dgqa, dgka, dsink, dgqb, dgkb)


def _position():
    return lax.axis_index("x"), lax.axis_index("y"), lax.axis_index("c")


GATHER_CHUNKS = 2


def _gather_weights(blocks, name):
    n = len(blocks)
    ch = GATHER_CHUNKS

    def body(*refs):
        src_refs, dst_refs = refs[:n], refs[n:2 * n]
        ici_send, ici_recv, hop_send, hop_recv, d2d_send, d2d_recv = refs[2 * n:]
        x, y, c = _position()
        b = 2 * x + y
        via = 2 - c
        out = 3 - via
        for k in range(n):
            dst_refs[k][b] = src_refs[k][...].astype(BF16)

        def rows(k, core, j):
            half = blocks[k].shape[0] // 2
            return pl.ds(pl.multiple_of(core * half + j * (half // ch), half // ch), half // ch)

        def chip(rel):
            return x ^ (rel >> 1), y ^ (rel & 1)

        def ici(k, j, slot, rel, send_sems, recv_sems, sem):
            px, py = chip(rel)
            piece = dst_refs[k].at[slot, rows(k, c, j)]
            return pltpu.make_async_remote_copy(src_ref=piece, dst_ref=piece, send_sem=send_sems.at[sem],
                                                recv_sem=recv_sems.at[sem], device_id=(px, py, c),
                                                device_id_type=MESH)

        def direct(k, j, slot, rel):
            return ici(k, j, slot, rel, ici_send, ici_recv, ((rel - 1) * ch + j) * n + k)

        def hop(k, j, slot, rel):
            return ici(k, j, slot, rel, hop_send, hop_recv, j * n + k)

        def d2d(k, j, rel, core):
            piece = dst_refs[k].at[b ^ rel, rows(k, core, j)]
            sem = ((rel - 1) * ch + j) * n + k
            return pltpu.make_async_remote_copy(src_ref=piece, dst_ref=piece, send_sem=d2d_send.at[sem],
                                                recv_sem=d2d_recv.at[sem], device_id=(x, y, 1 - c),
                                                device_id_type=MESH)

        pieces = [(k, j) for j in range(ch) for k in range(n)]
        for k, j in pieces:
            for rel in (1, 2):
                direct(k, j, b, rel).start()
        for k, j in pieces:
            direct(k, j, b ^ via, via).wait_recv()
            hop(k, j, b ^ via, out).start()
            d2d(k, j, via, c).start()
        for k, j in pieces:
            direct(k, j, b ^ out, out).wait_recv()
            d2d(k, j, out, c).start()
        for k, j in pieces:
            hop(k, j, b ^ 3, via).wait_recv()
            d2d(k, j, 3, c).start()
        for k, j in pieces:
            for rel in (1, 2, 3):
                d2d(k, j, rel, 1 - c).wait_recv()
        for k, j in pieces:
            for rel in (1, 2):
                direct(k, j, b, rel).wait_send()
            hop(k, j, b ^ via, out).wait_send()
            d2d(k, j, via, c).wait_send()
            d2d(k, j, out, c).wait_send()
            d2d(k, j, 3, c).wait_send()

    vmem_spec = pl.BlockSpec(memory_space=pltpu.VMEM)
    dma = pltpu.SemaphoreType.DMA
    out_shape = tuple(jax.ShapeDtypeStruct((N_CHIP,) + a.shape, BF16) for a in blocks)
    return pl.pallas_call(
        body, name=name, in_specs=[vmem_spec] * n, out_specs=tuple([vmem_spec] * n), out_shape=out_shape,
        scratch_shapes=[dma((2 * ch * n,)), dma((2 * ch * n,)), dma((ch * n,)), dma((ch * n,)),
                        dma((3 * ch * n,)), dma((3 * ch * n,))],
        compiler_params=pltpu.CompilerParams(vmem_limit_bytes=VMEM_LIMIT),
    )(*blocks)


def _grad_reduce(order, h_t, wins, gw_out, small):
    s = h_t.shape[1]
    tk = GRAD_ROWS
    n_i = s // tk
    half = D_MODEL // 2
    o_half = OUT_ROWS // 2
    n_rel = N_CHIP - 1

    def body(order_ref, ht_ref, win_ref, gwo_ref, small_ref,
             win_out, wout_out, small_out,
             acc, mine, s1, r1, s2, r2, so1, ro1, so2, ro2, pair_in, pair_o, small_land,
             s1_send, s1_recv, s2_send, s2_recv, o1_send, o1_recv, o2_send, o2_recv,
             pair_send, pair_recv, small_send, small_recv):
        j = pl.program_id(0)
        i = pl.program_id(1)
        x, y, c = _position()
        me = 4 * x + 2 * y + c
        sibling = (x, y, 1 - c)
        my_rows = pl.ds(pl.multiple_of(c * half, half), half)
        sib_rows = pl.ds(pl.multiple_of((1 - c) * half, half), half)

        def chip_of(rel):
            return x ^ (rel >> 1), y ^ (rel & 1)

        def level1(k):
            return pltpu.make_async_remote_copy(src_ref=s1.at[k], dst_ref=r1.at[k], send_sem=s1_send.at[k],
                                                recv_sem=s1_recv.at[k], device_id=sibling, device_id_type=MESH)

        def level2(k):
            px, py = chip_of(RELATIONS[k])
            return pltpu.make_async_remote_copy(src_ref=s2.at[k], dst_ref=r2.at[k], send_sem=s2_send.at[k],
                                                recv_sem=s2_recv.at[k], device_id=(px, py, c), device_id_type=MESH)

        def out_level1(bk):
            return pltpu.make_async_remote_copy(src_ref=so1.at[bk], dst_ref=ro1.at[bk], send_sem=o1_send.at[bk],
                                                recv_sem=o1_recv.at[bk], device_id=sibling, device_id_type=MESH)

        def out_level2(k):
            px, py = chip_of(RELATIONS[k])
            return pltpu.make_async_remote_copy(src_ref=so2.at[k], dst_ref=ro2.at[k], send_sem=o2_send.at[k],
                                                recv_sem=o2_recv.at[k], device_id=(px, py, c), device_id_type=MESH)

        def small_copy(d):
            px, py, pc = x ^ (d >> 2), y ^ ((d >> 1) & 1), c ^ (d & 1)
            return pltpu.make_async_remote_copy(src_ref=small_ref, dst_ref=small_land.at[me],
                                                send_sem=small_send.at[d], recv_sem=small_recv.at[d],
                                                device_id=(px, py, pc), device_id_type=MESH)

        def pair_copy(k, buf):
            return pltpu.make_async_remote_copy(src_ref=buf.at[0], dst_ref=buf.at[1], send_sem=pair_send.at[k],
                                                recv_sem=pair_recv.at[k], device_id=sibling, device_id_type=MESH)

        def out_rows(bk, core):
            return pl.ds(pl.multiple_of(bk * OUT_ROWS + core * o_half, o_half), o_half)

        @pl.when((j == 0) & (i == 0))
        def _():
            for d in range(1, N_DEV):
                small_copy(d).start()
            small_land[me] = small_ref[...]
            for bk in range(N_CHIP):
                so1[bk] = gwo_ref[out_rows(bk, 1 - c), :].astype(BF16)
                out_level1(bk).start()

        @pl.when((j == 0) & (i == 1))
        def _():
            b = 2 * x + y
            for bk in range(N_CHIP):
                out_level1(bk).wait_recv()
            for k in range(n_rel):
                px, py = chip_of(RELATIONS[k])
                bk = 2 * px + py
                so2[k] = (gwo_ref[out_rows(bk, c), :] + ro1[bk].astype(F32)).astype(BF16)
                out_level2(k).start()

        @pl.when(i == 0)
        def _():
            acc[...] = jnp.zeros_like(acc)

        for n0 in range(0, WIN, ACC_COLS):
            n1 = min(n0 + ACC_COLS, WIN)
            acc[:, n0:n1] += _dot(ht_ref[...], win_ref[:, n0:n1])

        for k in range(N_CHIP):
            @pl.when((j == k) & (i == n_i - 1))
            def _(k=k):
                s1[k] = acc[sib_rows, :].astype(BF16)
                level1(k).start()
                mine[...] = acc[my_rows, :]

            if k < n_rel:
                @pl.when((j == k + 1) & (i == 1))
                def _(k=k):
                    level1(k).wait_recv()
                    s2[k] = (mine[...] + r1[k].astype(F32)).astype(BF16)
                    level2(k).start()

        @pl.when((j == N_CHIP - 1) & (i == n_i - 1))
        def _():
            b = 2 * x + y
            level1(N_CHIP - 1).wait_recv()
            total = mine[...] + r1[N_CHIP - 1].astype(F32)
            for k in range(n_rel):
                level2(k).wait_recv()
                total = total + r2[k].astype(F32)
            total = total.T
            pair_in[0] = total
            pair_copy(0, pair_in).start()
            total_o = gwo_ref[out_rows(b, c), :] + ro1[b].astype(F32)
            for k in range(n_rel):
                out_level2(k).wait_recv()
                total_o = total_o + ro2[k].astype(F32)
            pair_o[0] = total_o
            pair_copy(1, pair_o).start()
            for core in range(2):
                @pl.when(c == core)
                def _(core=core):
                    win_out[:, core * half:(core + 1) * half] = total
            wout_out[c] = total_o
            for d in range(1, N_DEV):
                small_copy(d).wait_recv()
            small_out[...] = small_land[...]
            pair_copy(0, pair_in).wait_recv()
            for core in range(2):
                @pl.when(c == core)
                def _(core=core):
                    win_out[:, (1 - core) * half:(2 - core) * half] = pair_in[1]
            pair_copy(1, pair_o).wait_recv()
            wout_out[1 - c] = pair_o[1]
            for d in range(1, N_DEV):
                small_copy(d).wait_send()
            for k in range(N_CHIP):
                level1(k).wait_send()
                out_level1(k).wait_send()
            for k in range(n_rel):
                level2(k).wait_send()
                out_level2(k).wait_send()
            pair_copy(0, pair_in).wait_send()
            pair_copy(1, pair_o).wait_send()

    vmem = pl.BlockSpec(memory_space=pltpu.VMEM)
    dma = pltpu.SemaphoreType.DMA
    sds = jax.ShapeDtypeStruct
    grid_spec = pltpu.PrefetchScalarGridSpec(
        num_scalar_prefetch=1, grid=(N_CHIP, n_i),
        in_specs=[pl.BlockSpec((D_MODEL, tk), lambda j, i, order: (0, i)),
                  pl.BlockSpec((None, tk, WIN), lambda j, i, order: (order[j], i, 0)), vmem, vmem],
        out_specs=(vmem, vmem, vmem),
        scratch_shapes=[
            pltpu.VMEM((D_MODEL, WIN), F32), pltpu.VMEM((half, WIN), F32),
            pltpu.VMEM((N_CHIP, half, WIN), BF16), pltpu.VMEM((N_CHIP, half, WIN), BF16),
            pltpu.VMEM((n_rel, half, WIN), BF16), pltpu.VMEM((n_rel, half, WIN), BF16),
            pltpu.VMEM((N_CHIP, o_half, D_MODEL), BF16), pltpu.VMEM((N_CHIP, o_half, D_MODEL), BF16),
            pltpu.VMEM((n_rel, o_half, D_MODEL), BF16), pltpu.VMEM((n_rel, o_half, D_MODEL), BF16),
            pltpu.VMEM((2, WIN, half), F32), pltpu.VMEM((2, o_half, D_MODEL), F32),
            pltpu.VMEM((N_DEV, PACK_ROWS, D_MODEL), F32),
            dma((N_CHIP,)), dma((N_CHIP,)), dma((n_rel,)), dma((n_rel,)),
            dma((N_CHIP,)), dma((N_CHIP,)), dma((n_rel,)), dma((n_rel,)),
            dma((2,)), dma((2,)), dma((N_DEV,)), dma((N_DEV,))])
    return pl.pallas_call(
        body, name="grad_w_in_reduce", grid_spec=grid_spec,
        out_shape=(sds((WIN, D_MODEL), F32), sds((2, o_half, D_MODEL), F32), sds((N_DEV, PACK_ROWS, D_MODEL), F32)),
        compiler_params=pltpu.CompilerParams(dimension_semantics=("arbitrary", "arbitrary"),
                                             vmem_limit_bytes=VMEM_LIMIT),
    )(order, h_t, wins, gw_out, small)


ADAM_STEPS = 4


def _adamw_math(w, g, m, v):
    m = ADAM_B1 * m + (1.0 - ADAM_B1) * g
    v = ADAM_B2 * v + (1.0 - ADAM_B2) * (g * g)
    m_hat = m / (1.0 - ADAM_B1 ** ADAM_STEP)
    v_hat = v / (1.0 - ADAM_B2 ** ADAM_STEP)
    delta = -ADAM_LR * (m_hat / (jnp.sqrt(v_hat) + ADAM_EPS) + ADAM_WD * w)
    return delta, m, v


def _adamw(w, g, m, v, name):
    r, c = w.shape

    def body(w_ref, g_ref, m_ref, v_ref, d_ref, nm_ref, nv_ref):
        delta, nm, nv = _adamw_math(w_ref[...], g_ref[...], m_ref[...], v_ref[...])
        d_ref[...] = delta
        nm_ref[...] = nm
        nv_ref[...] = nv

    rows = r // ADAM_STEPS
    assert rows * ADAM_STEPS == r and rows % 8 == 0
    spec = pl.BlockSpec((rows, c), lambda i: (i, 0))
    shape = jax.ShapeDtypeStruct((r, c), F32)
    return pl.pallas_call(
        body, name=name, grid=(ADAM_STEPS,), in_specs=[spec] * 4, out_specs=(spec,) * 3,
        out_shape=(shape,) * 3, compiler_params=pltpu.CompilerParams(vmem_limit_bytes=VMEM_LIMIT),
    )(w, g, m, v)


def _adamw_window(w, window, shift, m, v, name):
    r, c = w.shape
    rows = r // ADAM_STEPS
    assert rows * ADAM_STEPS == r and rows % 8 == 0

    def body(shift_ref, w_ref, win_hbm, m_ref, v_ref, g_ref, d_ref, nm_ref, nv_ref, g_vmem):
        start = pl.multiple_of(shift_ref[0] + pl.program_id(0) * rows, 8)
        pltpu.sync_copy(win_hbm.at[pl.ds(start, rows)], g_vmem)
        g = g_vmem[...]
        g_ref[...] = g
        delta, nm, nv = _adamw_math(w_ref[...], g, m_ref[...], v_ref[...])
        d_ref[...] = delta
        nm_ref[...] = nm
        nv_ref[...] = nv

    spec = pl.BlockSpec((rows, c), lambda i, shift_ref: (i, 0))
    shape = jax.ShapeDtypeStruct((r, c), F32)
    grid_spec = pltpu.PrefetchScalarGridSpec(
        num_scalar_prefetch=1, grid=(ADAM_STEPS,),
        in_specs=[spec, pl.BlockSpec(memory_space=pl.ANY), spec, spec], out_specs=(spec,) * 4,
        scratch_shapes=[pltpu.VMEM((rows, c), F32)])
    return pl.pallas_call(
        body, name=name, grid_spec=grid_spec, out_shape=(shape,) * 4,
        compiler_params=pltpu.CompilerParams(vmem_limit_bytes=VMEM_LIMIT),
    )(shift, w, window, m, v)


PACK_ROWS = 8


def _fold_heads(v):
    y = v[:, 0:LANES]
    for j in range(1, v.shape[1] // LANES):
        y = y + v[:, j * LANES:(j + 1) * LANES]
    return y + pltpu.roll(y, HEAD_DIM, 1)


N_SMALL = 6


def _small_adamw(recv, weights, m, v):
    def body(*refs):
        r_ref = refs[0]
        w_refs, m_refs, v_refs = (refs[1 + n * N_SMALL:1 + (n + 1) * N_SMALL] for n in range(3))
        outs = refs[1 + 3 * N_SMALL:]
        g_refs, d_refs, nm_refs, nv_refs = (outs[n * N_SMALL:(n + 1) * N_SMALL] for n in range(4))
        loss_ref = outs[4 * N_SMALL]
        tot = r_ref[0]
        for j in range(1, N_DEV):
            tot = tot + r_ref[j]
        loss_ref[...] = tot[3:4, 0:LANES]
        row1 = tot[1:2, :]
        row2 = tot[2:3, :]
        grads = [tot[0:1, :],
                 _fold_heads(row1[:, 0:WIDTH])[:, :HEAD_DIM],
                 _fold_heads(row2[:, WIDTH:WIDTH + A_KV_WIDTH])[:, :HEAD_DIM],
                 row2[:, WIDTH + A_KV_WIDTH:WIDTH + A_KV_WIDTH + HEADS],
                 _fold_heads(row1[:, WIDTH:2 * WIDTH])[:, :HEAD_DIM],
                 _fold_heads(row2[:, 0:WIDTH])[:, :HEAD_DIM]]
        for n, g in enumerate(grads):
            g_refs[n][...] = g
            delta, nm, nv = _adamw_math(w_refs[n][...], g, m_refs[n][...], v_refs[n][...])
            d_refs[n][...] = delta
            nm_refs[n][...] = nm
            nv_refs[n][...] = nv

    shapes = tuple(jax.ShapeDtypeStruct(a.shape, F32) for a in weights)
    outs = pl.pallas_call(body, name="small_adamw", out_shape=shapes * 4 + (jax.ShapeDtypeStruct((1, LANES), F32),)
                          )(recv, *weights, *m, *v)
    return tuple(outs[n * N_SMALL:(n + 1) * N_SMALL] for n in range(4)) + (outs[4 * N_SMALL],)


def kernel(x, norm_gain, w_in, q_norm_a, k_norm_a, sinks_a, q_norm_b, k_norm_b, w_out, loss_target, m_norm_gain, m_w_in, m_q_norm_a, m_k_norm_a, m_sinks_a, m_q_norm_b, m_k_norm_b, m_w_out, v_norm_gain, v_w_in, v_q_norm_a, v_k_norm_a, v_sinks_a, v_q_norm_b, v_k_norm_b, v_w_out):
    chip = 2 * lax.axis_index("x") + lax.axis_index("y")

    w_in_t, m_w_in_t, v_w_in_t = w_in[0].T, m_w_in[0].T, v_w_in[0].T

    (w_in_all,) = _gather_weights([w_in_t], "gather_weights")
    w_in_bf = w_in_all.reshape(IN_WIDTH, D_MODEL)

    loss_part, gx, h_t, wins, gw_out, (dgain, dgqa, dgka, dsink, dgqb, dgkb) = _local_step(
        x[0], loss_target[0], norm_gain, w_in_bf, q_norm_a, k_norm_a, sinks_a, q_norm_b, k_norm_b, w_out[0])

    small = jnp.concatenate([
        dgain, jnp.concatenate([dgqa, dgqb], axis=1),
        jnp.concatenate([dgkb, dgka, dsink, jnp.zeros((1, D_MODEL - WIDTH - 2 * A_KV_WIDTH), F32)], axis=1),
        jnp.pad(loss_part, ((0, 0), (0, D_MODEL - LANES))),
        jnp.zeros((PACK_ROWS - 4, D_MODEL), F32)], axis=0)
    order = (chip ^ jnp.array(RELATIONS, jnp.int32)).astype(jnp.int32)
    win_sum, wout_sum, small_recv = _grad_reduce(order, h_t, wins, gw_out, small)
    shift = jnp.array(WIN_SHIFT, jnp.int32)[chip].reshape(1)
    g_w_out = wout_sum.reshape(OUT_ROWS, D_MODEL)

    g_w_in, d_w_in, nm_w_in, nv_w_in = (
        a.T for a in _adamw_window(w_in_t, win_sum, shift, m_w_in_t, v_w_in_t, "adamw_w_in"))
    d_w_out, nm_w_out, nv_w_out = _adamw(w_out[0], g_w_out, m_w_out[0], v_w_out[0], "adamw_w_out")
    g_s, d_s, nm_s, nv_s, loss_row = _small_adamw(
        small_recv,
        (norm_gain, q_norm_a, k_norm_a, sinks_a, q_norm_b, k_norm_b),
        (m_norm_gain, m_q_norm_a, m_k_norm_a, m_sinks_a, m_q_norm_b, m_k_norm_b),
        (v_norm_gain, v_q_norm_a, v_k_norm_a, v_sinks_a, v_q_norm_b, v_k_norm_b))
    loss = loss_row[0, 0]

    def leaves(small_ones, big_in, big_out):
        return (small_ones[0], big_in[None]) + tuple(small_ones[1:]) + (big_out[None],)

    return ((loss, gx[None]) + leaves(g_s, g_w_in, g_w_out) + leaves(d_s, d_w_in, d_w_out)
            + leaves(nm_s, nm_w_in, nm_w_out) + leaves(nv_s, nv_w_in, nv_w_out))
```

```python
import jax
import jax.numpy as jnp
from jax import lax
from jax.experimental import pallas as pl
from jax.experimental.pallas import tpu as pltpu

F32 = jnp.float32
BF16 = jnp.bfloat16

D_MODEL = 1024
HEAD_DIM = 64
HEADS = 8
WIDTH = HEADS * HEAD_DIM
A_KV_WIDTH = 2 * HEAD_DIM
BLOCK = 128
LANES = 128
FOLD = 16
A_MAX_DIST = 127
B_MAX_DIST = 128
ROPE_THETA = 10000.0
ROPE_SPLIT = 64
EPS = 1e-6
NEG = -1e30
SCALE = HEAD_DIM ** -0.5

IN_WIDTH = 3328
C_QA, C_KA, C_VA, C_GA, C_QB, C_KB, C_VB, C_GB, C_END = 0, 512, 640, 768, 1280, 1792, 2304, 2816, 3328

N_DEV = 8
N_CHIP = 4
MESH = pl.DeviceIdType.MESH
IN_COLS = IN_WIDTH // N_CHIP
WIN = 896
WIN_START = (0, 768, 1664, 2432)
WIN_SHIFT = (0, 64, 0, 64)
OUT_ROWS = D_MODEL // N_CHIP
RELATIONS = (3, 1, 2, 0)

ADAM_LR = 0.001
ADAM_B1 = 0.9
ADAM_B2 = 0.999
ADAM_EPS = 1e-08
ADAM_WD = 0.01
ADAM_STEP = 10

ROW_TILE = 256
FOLD_ROWS = ROW_TILE // FOLD
GRAD_ROWS = 1024
ACC_COLS = 256
VMEM_LIMIT = 56 * 1024 * 1024


def _dot(a, b):
    return jnp.dot(a, b, preferred_element_type=F32)


def _dot_nt(a, b):
    return lax.dot_general(a, b, (((1,), (1,)), ((), ())), preferred_element_type=F32)


def _dot_tn(a, b):
    return lax.dot_general(a, b, (((0,), (0,)), ((), ())), preferred_element_type=F32)


def _head_sum(z, bd):
    w = bd.shape[0]
    zb = z.astype(BF16)
    parts = [_dot(zb[:, a:a + w], bd) for a in range(0, z.shape[1], w)]
    return parts[0] if len(parts) == 1 else jnp.concatenate(parts, axis=1)


def _swap_halves(t):
    w = t.shape[1]
    lane = lax.broadcasted_iota(jnp.int32, t.shape, 1)
    return jnp.where(lane % HEAD_DIM < HEAD_DIM // 2, pltpu.roll(t, w - 32, 1), pltpu.roll(t, 32, 1))


def _qknorm_rope(t, g, cos, sin_s, bd):
    r = lax.rsqrt(_head_sum(t * t, bd) * (1.0 / HEAD_DIM) + EPS)
    n = (t * r) * g
    return n * cos + _swap_halves(n) * sin_s


def _qknorm_rope_bwd(dout, t, g, cos, sin_s, bd):
    dout, t = dout.astype(F32), t.astype(F32)
    dn = dout * cos + _swap_halves(dout * sin_s)
    r = lax.rsqrt(_head_sum(t * t, bd) * (1.0 / HEAD_DIM) + EPS)
    tr = t * r
    u = dn * g
    dt = r * (u - tr * (_head_sum(u * tr, bd) * (1.0 / HEAD_DIM)))
    return dt, dn * tr


def _sigmoid(g):
    return 1.0 / (1.0 + jnp.exp(-g))


def _expand_heads(st):
    t = st.shape[0]
    lane = lax.broadcasted_iota(jnp.int32, (t, LANES), 1)
    chunks = []
    for c in range(WIDTH // LANES):
        chunks.append(jnp.where(lane < HEAD_DIM, st[:, 2 * c:2 * c + 1], st[:, 2 * c + 1:2 * c + 2]))
    return jnp.concatenate(chunks, axis=1)


def _reduce_heads(z):
    t = z.shape[0]
    lane = lax.broadcasted_iota(jnp.int32, (t, LANES), 1)
    out = jnp.zeros((t, LANES), F32)
    for c in range(WIDTH // LANES):
        zc = z[:, c * LANES:(c + 1) * LANES]
        for ph in range(2):
            s = jnp.sum(jnp.where((lane // HEAD_DIM) == ph, zc, 0.0), axis=-1, keepdims=True)
            out = jnp.where(lane == 2 * c + ph, s, out)
    return out


def _fold_scratch(w):
    return pltpu.VMEM((w // LANES, ROW_TILE, LANES), F32)


def _store_folded(out_ref, val, scr, col0=0):
    w = val.shape[1]
    n = w // LANES
    for c in range(n):
        scr[c] = val[:, c * LANES:(c + 1) * LANES]
    for r in range(FOLD):
        piece = [scr[c, pl.ds(r, FOLD_ROWS, stride=FOLD), :] for c in range(n)]
        out_ref[r, :, col0:col0 + w] = (piece[0] if n == 1 else jnp.concatenate(piece, axis=1)).astype(out_ref.dtype)


def _load_folded(in_ref, scr):
    n = in_ref.shape[2] // LANES
    for r in range(FOLD):
        blk = in_ref[r].astype(F32)
        for c in range(n):
            scr[c, pl.ds(r, FOLD_ROWS, stride=FOLD), :] = blk[:, c * LANES:(c + 1) * LANES]
    return scr[0] if n == 1 else jnp.concatenate([scr[c] for c in range(n)], axis=1)


def _fold_matrix():
    f = jnp.arange(ROW_TILE)
    return (jnp.arange(ROW_TILE)[None, :] == (FOLD * (f % FOLD_ROWS) + f // FOLD_ROWS)[:, None]).astype(BF16)


def _store_folded_bf16(out_ref, val, perm):
    folded = _dot(perm, val.astype(BF16)).astype(out_ref.dtype)
    for r in range(FOLD):
        out_ref[r] = folded[r * FOLD_ROWS:(r + 1) * FOLD_ROWS]


def _load_folded_bf16(in_ref, perm):
    blk = jnp.concatenate([in_ref[r] for r in range(FOLD)], axis=0)
    return _dot(perm, blk)


def _rows(w, tm=ROW_TILE):
    return pl.BlockSpec((tm, w), lambda i: (i, 0))


def _folded_rows(w):
    return pl.BlockSpec((FOLD, FOLD_ROWS, w), lambda i: (0, i, 0))


def _whole(shape):
    return pl.BlockSpec(shape, lambda i: (0,) * len(shape))


def _inproj(x2, gain, w_bf, cos, sin_s, gqa, gka, gqb, gkb, bd256, bd128, w_out_blk):
    s = x2.shape[0]
    tm = ROW_TILE
    n_steps = s // tm
    n_rel = N_CHIP - 1
    o_half = OUT_ROWS // 2

    def body(x_ref, gain_ref, w_hbm, cos_ref, sin_ref, gqa_ref, gka_ref, gqb_ref, gkb_ref, bd256_ref, bd128_ref,
             wout_ref, qa_ref, kva_ref, qb_ref, kvb_ref, qbf_ref, kvbf_ref,
             qa_raw_ref, ka_raw_ref, ga_ref, qb_raw_ref, kb_raw_ref, gb_ref, wout_all_ref,
             w_vmem, scr, land, ici_send, ici_recv, d2d_send, d2d_recv):
        i = pl.program_id(0)
        px_, py_, c = _position()
        b = 2 * px_ + py_

        def piece(chip_idx, core):
            return land.at[chip_idx, pl.ds(pl.multiple_of(core * o_half, o_half), o_half)]

        def other_chip(d):
            ox, oy = px_ ^ (d >> 1), py_ ^ (d & 1)
            return ox, oy, 2 * ox + oy

        def ici_copy(d, chip_idx):
            ox, oy, _ = other_chip(d)
            return pltpu.make_async_remote_copy(
                src_ref=piece(chip_idx, c), dst_ref=piece(chip_idx, c), send_sem=ici_send.at[d - 1],
                recv_sem=ici_recv.at[d - 1], device_id=(ox, oy, c), device_id_type=MESH)

        def d2d_copy(d, core):
            return pltpu.make_async_remote_copy(
                src_ref=piece(other_chip(d)[2], core), dst_ref=piece(other_chip(d)[2], core),
                send_sem=d2d_send.at[d - 1], recv_sem=d2d_recv.at[d - 1], device_id=(px_, py_, 1 - c),
                device_id_type=MESH)

        @pl.when(i == 0)
        def _():
            pltpu.sync_copy(w_hbm, w_vmem)
            land[b] = wout_ref[...].astype(BF16)
            for d in range(1, N_CHIP):
                ici_copy(d, b).start()

        @pl.when(i == n_steps // 2)
        def _():
            for d in range(1, N_CHIP):
                ici_copy(d, other_chip(d)[2]).wait_recv()
                d2d_copy(d, c).start()

        @pl.when(i == n_steps - 1)
        def _():
            for d in range(1, N_CHIP):
                d2d_copy(d, 1 - c).wait_recv()
            for d in range(1, N_CHIP):
                ici_copy(d, b).wait_send()
                d2d_copy(d, c).wait_send()
            for k in range(N_CHIP):
                wout_all_ref[k * OUT_ROWS:(k + 1) * OUT_ROWS, :] = land[k]

        xt = x_ref[...]
        r = lax.rsqrt(jnp.mean(xt * xt, axis=-1, keepdims=True) + EPS)
        h = ((xt * r) * gain_ref[...]).astype(BF16)
        cos1 = cos_ref[...]
        sin1 = sin_ref[...]
        cos4 = jnp.tile(cos1, (1, 4))
        sin4 = jnp.tile(sin1, (1, 4))

        def seg(a, b):
            return _dot_nt(h, w_vmem[a:b, :])

        t = seg(C_QA, C_KA)
        qa_raw_ref[...] = t.astype(BF16)
        qa_ref[...] = (_qknorm_rope(t, gqa_ref[...], cos4, sin4, bd256_ref[...]) * SCALE).astype(BF16)
        t = seg(C_KA, C_VA)
        ka_raw_ref[...] = t.astype(BF16)
        kva_ref[:, :A_KV_WIDTH] = _qknorm_rope(t, gka_ref[...], cos1, sin1, bd128_ref[...]).astype(BF16)
        kva_ref[:, A_KV_WIDTH:] = seg(C_VA, C_GA).astype(BF16)
        ga_ref[...] = seg(C_GA, C_QB).astype(BF16)
        t = seg(C_QB, C_KB)
        qb_raw_ref[...] = t.astype(BF16)
        t = _qknorm_rope(t, gqb_ref[...], cos4, sin4, bd256_ref[...]) * SCALE
        qb_ref[...] = t.astype(BF16)
        _store_folded(qbf_ref, t, scr)
        t = seg(C_KB, C_VB)
        kb_raw_ref[...] = t.astype(BF16)
        t = _qknorm_rope(t, gkb_ref[...], cos4, sin4, bd256_ref[...])
        kvb_ref[:, :WIDTH] = t.astype(BF16)
        _store_folded(kvbf_ref, t, scr)
        t = seg(C_VB, C_GB)
        kvb_ref[:, WIDTH:] = t.astype(BF16)
        _store_folded(kvbf_ref, t, scr, WIDTH)
        gb_ref[...] = seg(C_GB, C_END).astype(BF16)

    sds = jax.ShapeDtypeStruct
    ln = s // FOLD
    out_shape = (sds((s, WIDTH), BF16), sds((s, 2 * A_KV_WIDTH), BF16), sds((s, WIDTH), BF16),
                 sds((s, 2 * WIDTH), BF16), sds((FOLD, ln, WIDTH), BF16), sds((FOLD, ln, 2 * WIDTH), BF16),
                 sds((s, WIDTH), BF16), sds((s, A_KV_WIDTH), BF16), sds((s, WIDTH), BF16),
                 sds((s, WIDTH), BF16), sds((s, WIDTH), BF16), sds((s, WIDTH), BF16),
                 sds((D_MODEL, D_MODEL), BF16))
    out_specs = (_rows(WIDTH), _rows(2 * A_KV_WIDTH), _rows(WIDTH), _rows(2 * WIDTH),
                 _folded_rows(WIDTH), _folded_rows(2 * WIDTH),
                 _rows(WIDTH), _rows(A_KV_WIDTH), _rows(WIDTH), _rows(WIDTH), _rows(WIDTH), _rows(WIDTH),
                 _whole((D_MODEL, D_MODEL)))
    dma = pltpu.SemaphoreType.DMA
    return pl.pallas_call(
        body, name="inproj_fwd", grid=(n_steps,),
        in_specs=[_rows(D_MODEL), _whole(gain.shape), pl.BlockSpec(memory_space=pl.ANY), _rows(LANES), _rows(LANES),
                  _whole(gqa.shape), _whole(gka.shape), _whole(gqb.shape), _whole(gkb.shape), _whole(bd256.shape),
                  _whole(bd128.shape), _whole(w_out_blk.shape)],
        out_specs=out_specs, out_shape=out_shape,
        scratch_shapes=[pltpu.VMEM((IN_WIDTH, D_MODEL), BF16), _fold_scratch(WIDTH),
                        pltpu.VMEM((N_CHIP, OUT_ROWS, D_MODEL), BF16),
                        dma((n_rel,)), dma((n_rel,)), dma((n_rel,)), dma((n_rel,))],
        compiler_params=pltpu.CompilerParams(dimension_semantics=("arbitrary",), vmem_limit_bytes=VMEM_LIMIT),
    )(x2, gain, w_bf, cos, sin_s, gqa, gka, gqb, gkb, bd256, bd128, w_out_blk)


def _seq_pos(idx, dil):
    if dil == 4:
        return 4 * (idx % 32) + idx // 32
    return idx


SOFTMAX_ROWS = 64


def _upper_mask(dil, r0=0, rows=2 * BLOCK):
    qi = (lax.broadcasted_iota(jnp.int32, (rows, BLOCK), 0) + r0) % BLOCK
    kj = lax.broadcasted_iota(jnp.int32, (rows, BLOCK), 1)
    return _seq_pos(kj, dil) > _seq_pos(qi, dil)


def _eye_mask(r0=0, rows=2 * BLOCK):
    qi = (lax.broadcasted_iota(jnp.int32, (rows, BLOCK), 0) + r0) % BLOCK
    kj = lax.broadcasted_iota(jnp.int32, (rows, BLOCK), 1)
    return qi == kj


def _stack_heads(a2, c, gqa):
    lane = lax.broadcasted_iota(jnp.int32, (1, LANES), 1) // HEAD_DIM
    zero = jnp.zeros_like(a2)
    if gqa:
        keep = lane == (c // 2)
        return jnp.concatenate([jnp.where(keep, a2, zero), jnp.where(keep, _swap_heads(a2), zero)], axis=0)
    return jnp.concatenate([jnp.where(lane == 0, a2, zero), jnp.where(lane == 1, a2, zero)], axis=0)


def _unstack_heads(a, c, gqa):
    lane = lax.broadcasted_iota(jnp.int32, (1, LANES), 1) // HEAD_DIM
    if gqa:
        return jnp.where(lane == (c // 2), a[:BLOCK], _swap_heads(a[BLOCK:]))
    return jnp.where(lane == 0, a[:BLOCK], a[BLOCK:])


def _stacked_head_ids(c, gqa):
    if gqa:
        return 2 * c + c // 2, 2 * c + 1 - c // 2
    return 2 * c, 2 * c + 1


def _per_head_rows(blk, heads):
    return jnp.concatenate([blk[:, heads[0]:heads[0] + 1], blk[:, heads[1]:heads[1] + 1]], axis=0)


def _attn_view(a, dil):
    if dil == 1:
        return a[None]
    if dil == 4:
        return a.reshape(4, 4, a.shape[1], a.shape[2])
    return a


def _attn_unview(a, dil):
    if dil == 1:
        return a[0]
    if dil == 4:
        return a.reshape(FOLD, a.shape[2], a.shape[3])
    return a


ATTN_BLOCKS_PER_STEP = 8


def _attn_specs(dil):
    if dil == 4:
        def spec(n, fn):
            return lambda w: pl.BlockSpec((4, None, n * BLOCK // 4, w), lambda r, i: (0, r, fn(i), 0))
    else:
        def spec(n, fn):
            return lambda w: pl.BlockSpec((None, n * BLOCK, w), lambda r, i: (r, fn(i), 0))
    return spec


def _blk_rows(g, dil):
    n = BLOCK // 4 if dil == 4 else BLOCK
    if isinstance(g, int):
        return slice(g * n, (g + 1) * n)
    return pl.ds(pl.multiple_of(g * n, n), n)


def _blk_load(ref, sl, dil, g=0):
    if dil == 4:
        return ref[:, _blk_rows(g, dil), sl].reshape(BLOCK, sl.stop - sl.start)
    return ref[_blk_rows(g, dil), sl]


def _blk_store(ref, sl, val, dil, g=0):
    val = val.astype(ref.dtype)
    if dil == 4:
        ref[:, _blk_rows(g, dil), sl] = val.reshape(4, BLOCK // 4, sl.stop - sl.start)
    else:
        ref[_blk_rows(g, dil), sl] = val


def _swap_heads(a):
    return pltpu.roll(a.astype(F32), HEAD_DIM, 1).astype(a.dtype)


STAT_SHIFT = 8


def _attn_fwd(q, kv, sinks, *, dil, max_dist, name):
    q, kv = _attn_view(q, dil), _attn_view(kv, dil)
    kw = kv.shape[-1] // 2
    gqa = kw == A_KV_WIDTH
    n_seq = dil
    nb = (q.shape[-2] * (4 if dil == 4 else 1)) // BLOCK
    per_step = min(ATTN_BLOCKS_PER_STEP, nb)
    with_sinks = sinks is not None
    all_lanes = slice(0, LANES)
    assert max_dist in (BLOCK - 1, BLOCK) and nb % per_step == 0
    diag = max_dist == BLOCK

    def body(*refs):
        if with_sinks:
            q_ref, kvp_ref, kvc_ref, sink_ref, o_ref, ml_ref = refs
        else:
            q_ref, kvp_ref, kvc_ref, o_ref, ml_ref = refs

        def block(g, has_prev):
            prev_ref, prev_g = (kvp_ref, 0) if isinstance(g, int) else (kvc_ref, g - 1)
            lane = lax.broadcasted_iota(jnp.int32, (1, LANES), 1)
            with_diag = diag and has_prev
            upper, eye = _upper_mask(dil), _eye_mask()
            first_rows = lax.broadcasted_iota(jnp.int32, (2 * BLOCK, 1), 0) < BLOCK
            ml_blk = jnp.zeros((BLOCK, LANES), F32)
            chunks = range(WIDTH // LANES)
            scores, values = [], []
            for c in chunks:
                sl = slice(c * LANES, (c + 1) * LANES)
                ksl = slice(0, LANES) if gqa else sl
                vsl = slice(ksl.start + kw, ksl.stop + kw)
                kcur, vcur = _blk_load(kvc_ref, ksl, dil, g), _blk_load(kvc_ref, vsl, dil, g)
                qs = _stack_heads(_blk_load(q_ref, sl, dil, g), c, gqa)
                if has_prev:
                    kcur = jnp.concatenate([_blk_load(prev_ref, ksl, dil, prev_g), kcur], axis=0)
                    vcur = jnp.concatenate([_blk_load(prev_ref, vsl, dil, prev_g), vcur], axis=0)
                scores.append(_dot_nt(qs, kcur))
                values.append(vcur)
            probs = []
            for c in chunks:
                heads = _stacked_head_ids(c, gqa)
                s = scores[c]
                if has_prev:
                    s_p = s[:, :BLOCK]
                    sc = jnp.where(upper, s_p, s[:, BLOCK:])
                else:
                    sc = jnp.where(upper, NEG, s)
                if with_diag:
                    sd = jnp.where(eye, s_p, NEG)
                    m = jnp.max(jnp.maximum(sc, sd), axis=-1, keepdims=True)
                else:
                    m = jnp.max(sc, axis=-1, keepdims=True)
                if with_sinks:
                    sk = jnp.where(first_rows, sink_ref[0, heads[0]], sink_ref[0, heads[1]])
                    m = jnp.maximum(m, sk)
                p = jnp.exp(sc - m)
                zero = jnp.zeros_like(p)
                if with_diag:
                    pd = jnp.exp(sd - m)
                    l = jnp.sum(p + pd, axis=-1, keepdims=True)
                else:
                    pd = zero
                    l = jnp.sum(p, axis=-1, keepdims=True)
                if with_sinks:
                    l = l + jnp.exp(sk - m)
                pf = jnp.where(upper, zero, p)
                if has_prev:
                    pf = jnp.concatenate([jnp.where(upper, p, pd), pf], axis=1)
                probs.append(pf.astype(BF16))
                for n, h in enumerate(heads):
                    rows = slice(n * BLOCK, (n + 1) * BLOCK)
                    ml_blk = jnp.where(lane == h, m[rows], ml_blk)
                    ml_blk = jnp.where(lane == h + STAT_SHIFT, l[rows], ml_blk)
            for c in chunks:
                sl = slice(c * LANES, (c + 1) * LANES)
                _blk_store(o_ref, sl, _unstack_heads(_dot(probs[c], values[c]), c, gqa), dil, g)
            _blk_store(ml_ref, all_lanes, ml_blk, dil, g)

        @pl.when(pl.program_id(1) == 0)
        def _():
            block(0, False)

        @pl.when(pl.program_id(1) > 0)
        def _():
            block(0, True)

        if per_step > 1:
            def rest(g, carry):
                block(g, True)
                return carry

            lax.fori_loop(1, per_step, rest, 0)

    spec = _attn_specs(dil)
    cur = spec(per_step, lambda i: i)
    prev = spec(1, lambda i: jnp.maximum(i * per_step - 1, 0))
    in_specs = [cur(WIDTH), prev(2 * kw), cur(2 * kw)]
    args = [q, kv, kv]
    if with_sinks:
        in_specs.append(pl.BlockSpec(memory_space=pltpu.SMEM))
        args.append(sinks)
    stats = jax.ShapeDtypeStruct(q.shape[:-1] + (LANES,), F32)
    o, ml = pl.pallas_call(
        body, name=name, grid=(n_seq, nb // per_step), in_specs=in_specs,
        out_specs=(cur(WIDTH), cur(LANES)),
        out_shape=(jax.ShapeDtypeStruct(q.shape, BF16), stats),
        compiler_params=pltpu.CompilerParams(dimension_semantics=("arbitrary", "arbitrary"),
                                             vmem_limit_bytes=VMEM_LIMIT),
    )(*args)
    return _attn_unview(o, dil), _attn_unview(ml, dil)


def _attn_bwd(q, kv, do, ld, *, dil, max_dist, name):
    q, kv, do, ld = (_attn_view(a, dil) for a in (q, kv, do, ld))
    kw = kv.shape[-1] // 2
    gqa = kw == A_KV_WIDTH
    n_seq = dil
    nb = (q.shape[-2] * (4 if dil == 4 else 1)) // BLOCK
    n_kc = kw // LANES
    per_step = min(ATTN_BLOCKS_PER_STEP, nb)
    all_lanes = slice(0, LANES)
    assert max_dist in (BLOCK - 1, BLOCK) and nb % per_step == 0 and (per_step == 1 or per_step % 2 == 0)
    diag = max_dist == BLOCK

    def body(q_ref, kvp_ref, kvc_ref, do_ref, ld_ref, dq_ref, dkv_ref, ck_ref, cv_ref):
        i = pl.program_id(1)

        chunks = range(WIDTH // LANES)

        def matmuls_in(g, has_prev):
            prev_ref, prev_g = (kvp_ref, 0) if (isinstance(g, int) and g == 0) else (kvc_ref, g - 1)
            operands, products = [], []
            for c in chunks:
                sl = slice(c * LANES, (c + 1) * LANES)
                kc = 0 if gqa else c
                ksl = slice(kc * LANES, (kc + 1) * LANES)
                vsl = slice(ksl.start + kw, ksl.stop + kw)
                k2, v2 = _blk_load(kvc_ref, ksl, dil, g), _blk_load(kvc_ref, vsl, dil, g)
                if has_prev:
                    k2 = jnp.concatenate([_blk_load(prev_ref, ksl, dil, prev_g), k2], axis=0)
                    v2 = jnp.concatenate([_blk_load(prev_ref, vsl, dil, prev_g), v2], axis=0)
                qs = _stack_heads(_blk_load(q_ref, sl, dil, g), c, gqa)
                dos = _stack_heads(_blk_load(do_ref, sl, dil, g), c, gqa)
                operands.append((qs, dos, k2))
                products.append((_dot_nt(qs, k2), _dot_nt(dos, v2)))
            return operands, products

        def tile_ops(g, has_prev, products):
            upper, eye = _upper_mask(dil), _eye_mask()
            ld_blk = _blk_load(ld_ref, all_lanes, dil, g)
            weights = []
            for c in chunks:
                heads = _stacked_head_ids(c, gqa)
                lse2 = _per_head_rows(ld_blk, heads)
                dl2 = _per_head_rows(ld_blk, tuple(h + STAT_SHIFT for h in heads))
                s, dp = products[c]
                if has_prev:
                    s_p, dp_p = s[:, :BLOCK], dp[:, :BLOCK]
                    sc = jnp.where(upper, s_p, s[:, BLOCK:])
                    dpc = jnp.where(upper, dp_p, dp[:, BLOCK:])
                else:
                    sc = jnp.where(upper, NEG, s)
                    dpc = dp
                p = jnp.exp(sc - lse2)
                ds = p * (dpc - dl2)
                zero = jnp.zeros_like(p)
                pf = jnp.where(upper, zero, p)
                dsf = jnp.where(upper, zero, ds)
                if has_prev:
                    if diag:
                        pd = jnp.exp(jnp.where(eye, s_p, NEG) - lse2)
                        dsd = pd * (dp_p - dl2)
                    else:
                        pd = dsd = zero
                    pf = jnp.concatenate([jnp.where(upper, p, pd), pf], axis=1)
                    dsf = jnp.concatenate([jnp.where(upper, ds, dsd), dsf], axis=1)
                weights.append((pf.astype(BF16), dsf.astype(BF16)))
            return weights

        def matmuls_out(g, has_prev, operands, weights):
            seq_blk = i * per_step + g
            dk_acc = [None] * n_kc
            dv_acc = [None] * n_kc
            for c in chunks:
                sl = slice(c * LANES, (c + 1) * LANES)
                kc = 0 if gqa else c
                qs, dos, k2 = operands[c]
                pf, dsf = weights[c]
                _blk_store(dq_ref, sl, _unstack_heads(_dot(dsf, k2), c, gqa) * SCALE, dil, g)
                dk2 = _dot_tn(dsf, qs)
                dv2 = _dot_tn(pf, dos)
                dk_acc[kc] = dk2 if dk_acc[kc] is None else dk_acc[kc] + dk2
                dv_acc[kc] = dv2 if dv_acc[kc] is None else dv_acc[kc] + dv2
            for kc in range(n_kc):
                sl = slice(kc * LANES, (kc + 1) * LANES)
                vsl = slice(sl.start + kw, sl.stop + kw)
                if has_prev:
                    _blk_store(dkv_ref, sl, ck_ref[:, sl] + dk_acc[kc][:BLOCK], dil, seq_blk - 1)
                    _blk_store(dkv_ref, vsl, cv_ref[:, sl] + dv_acc[kc][:BLOCK], dil, seq_blk - 1)
                    ck_ref[:, sl] = dk_acc[kc][BLOCK:]
                    cv_ref[:, sl] = dv_acc[kc][BLOCK:]
                else:
                    ck_ref[:, sl] = dk_acc[kc]
                    cv_ref[:, sl] = dv_acc[kc]

        def run(blocks):
            ins = [matmuls_in(g, has_prev) for g, has_prev in blocks]
            mids = [tile_ops(g, has_prev, products) for (g, has_prev), (_, products) in zip(blocks, ins)]
            for (g, has_prev), (operands, _), weights in zip(blocks, ins, mids):
                matmuls_out(g, has_prev, operands, weights)

        second = [(1, True)] if per_step > 1 else []

        @pl.when(i == 0)
        def _():
            run([(0, False)] + second)

        @pl.when(i > 0)
        def _():
            run([(0, True)] + second)

        if per_step > 2:
            def rest(pair, carry):
                run([(2 * pair, True), (2 * pair + 1, True)])
                return carry

            lax.fori_loop(1, per_step // 2, rest, 0)

        @pl.when(i == nb // per_step - 1)
        def _():
            for kc in range(n_kc):
                sl = slice(kc * LANES, (kc + 1) * LANES)
                _blk_store(dkv_ref, sl, ck_ref[:, sl], dil, nb - 1)
                _blk_store(dkv_ref, slice(sl.start + kw, sl.stop + kw), cv_ref[:, sl], dil, nb - 1)

    spec = _attn_specs(dil)
    cur = spec(per_step, lambda i: i)
    prev = spec(1, lambda i: jnp.maximum(i * per_step - 1, 0))
    if dil == 4:
        whole = pl.BlockSpec((4, None, kv.shape[2], 2 * kw), lambda r, i: (0, r, 0, 0))
    else:
        whole = pl.BlockSpec((None, kv.shape[1], 2 * kw), lambda r, i: (r, 0, 0))
    sds = jax.ShapeDtypeStruct
    dq, dkv = pl.pallas_call(
        body, name=name, grid=(n_seq, nb // per_step),
        in_specs=[cur(WIDTH), prev(2 * kw), cur(2 * kw), cur(WIDTH), cur(LANES)],
        out_specs=(cur(WIDTH), whole),
        out_shape=(sds(q.shape, BF16), sds(kv.shape, BF16)),
        scratch_shapes=[pltpu.VMEM((BLOCK, kw), F32), pltpu.VMEM((BLOCK, kw), F32)],
        compiler_params=pltpu.CompilerParams(dimension_semantics=("arbitrary", "arbitrary"),
                                             vmem_limit_bytes=VMEM_LIMIT),
    )(q, kv, kv, do, ld)
    return _attn_unview(dq, dil), _attn_unview(dkv, dil)


def _outproj(att_a, att_b1, att_b4, att_b16, g_a, g_b, x2, tgt2, w_out_bf, sink_row, perm):
    s = x2.shape[0]
    tm = ROW_TILE

    def body(oa_ref, mla_ref, ob1_ref, ml1_ref, ob4_ref, ml4_ref, ob16_ref, ml16_ref,
             ga_ref, gb_ref, x_ref, t_ref, w_ref, sink_ref, perm_ref,
             dy_ref, doa_ref, dob_ref, dobf_ref, dga_ref, dgb_ref, lda_ref, ldb_ref, ldbf_ref,
             gw_ref, loss_ref, dsink_ref, scr_st):
        i = pl.program_id(0)
        perm = perm_ref[...]
        lane = lax.broadcasted_iota(jnp.int32, (tm, LANES), 1)
        used = lane < HEADS

        def split(ml):
            return jnp.where(used, ml, 0.0), jnp.where(used, pltpu.roll(ml, LANES - STAT_SHIFT, 1), 1.0)

        @pl.when(i == 0)
        def _():
            gw_ref[...] = jnp.zeros_like(gw_ref)
            loss_ref[...] = jnp.zeros_like(loss_ref)
            dsink_ref[...] = jnp.zeros_like(dsink_ref)

        ms, ls = zip(split(ml1_ref[...]), split(_load_folded(ml4_ref, scr_st)), split(_load_folded(ml16_ref, scr_st)))
        mx = jnp.maximum(jnp.maximum(ms[0], ms[1]), ms[2])
        scale = [jnp.exp(mp - mx) for mp in ms]
        den = (ls[0] * scale[0] + ls[1] * scale[1]) + ls[2] * scale[2]
        lse_b = jnp.where(used, mx + jnp.log(den), 0.0)
        inv_den = 1.0 / den
        o_b = _expand_heads(scale[0] * inv_den) * ob1_ref[...].astype(F32)
        o_b = o_b + _expand_heads(scale[1] * inv_den) * _load_folded_bf16(ob4_ref, perm)
        o_b = o_b + _expand_heads(scale[2] * inv_den) * _load_folded_bf16(ob16_ref, perm)
        m_a, l_a = split(mla_ref[...])
        lse_a = jnp.where(used, m_a + jnp.log(l_a), 0.0)
        o_a = _expand_heads(1.0 / l_a) * oa_ref[...].astype(F32)
        g_a = ga_ref[...].astype(F32)
        g_b = gb_ref[...].astype(F32)
        sg_a = _sigmoid(g_a)
        sg_b = _sigmoid(g_b)
        silu_a = g_a * sg_a
        silu_b = g_b * sg_b
        mixed = jnp.concatenate([o_a * silu_a, o_b * silu_b], axis=1).astype(BF16)
        w = w_ref[...]
        y = x_ref[...] + _dot(mixed, w)
        diff = y - t_ref[...]
        loss_ref[...] += (0.5 / D_MODEL) * jnp.sum(diff * diff)
        dy = diff * (1.0 / D_MODEL)
        dy_ref[...] = dy
        dyb = dy.astype(BF16)
        gw_ref[...] += _dot_tn(mixed, dyb)
        dmixed = _dot_nt(dyb, w)
        dm_a = dmixed[:, :WIDTH]
        dm_b = dmixed[:, WIDTH:]
        do_a = dm_a * silu_a
        do_b = dm_b * silu_b
        doa_ref[...] = do_a.astype(BF16)
        dob_ref[...] = do_b.astype(BF16)
        _store_folded_bf16(dobf_ref, do_b, perm)
        dga_ref[...] = (dm_a * o_a * (sg_a * (1.0 + g_a * (1.0 - sg_a)))).astype(BF16)
        dgb_ref[...] = (dm_b * o_b * (sg_b * (1.0 + g_b * (1.0 - sg_b)))).astype(BF16)
        dl_a = _reduce_heads(do_a * o_a)
        dl_b = _reduce_heads(do_b * o_b)
        lda_ref[...] = lse_a + pltpu.roll(dl_a, STAT_SHIFT, 1)
        ld_b = lse_b + pltpu.roll(dl_b, STAT_SHIFT, 1)
        ldb_ref[...] = ld_b
        _store_folded(ldbf_ref, ld_b, scr_st)
        dsink_ref[...] -= jnp.sum(jnp.exp(sink_ref[...] - lse_a) * dl_a, axis=0, keepdims=True)

    sds = jax.ShapeDtypeStruct
    ln = s // FOLD
    natural = [_rows(WIDTH), _rows(LANES)]
    folded = [_folded_rows(WIDTH), _folded_rows(LANES)]
    return pl.pallas_call(
        body, name="outproj_fwd_bwd", grid=(s // tm,),
        in_specs=natural + natural + folded + folded
                 + [_rows(WIDTH), _rows(WIDTH), _rows(D_MODEL), _rows(D_MODEL), _whole((D_MODEL, D_MODEL)),
                    _whole((1, LANES)), _whole(perm.shape)],
        out_specs=(_rows(D_MODEL), _rows(WIDTH), _rows(WIDTH), _folded_rows(WIDTH), _rows(WIDTH), _rows(WIDTH),
                   _rows(LANES), _rows(LANES), _folded_rows(LANES),
                   _whole((D_MODEL, D_MODEL)), _whole((1, LANES)), _whole((1, LANES))),
        out_shape=(sds((s, D_MODEL), F32), sds((s, WIDTH), BF16), sds((s, WIDTH), BF16),
                   sds((FOLD, ln, WIDTH), BF16), sds((s, WIDTH), BF16), sds((s, WIDTH), BF16),
                   sds((s, LANES), F32), sds((s, LANES), F32), sds((FOLD, ln, LANES), F32),
                   sds((D_MODEL, D_MODEL), F32), sds((1, LANES), F32), sds((1, LANES), F32)),
        scratch_shapes=[_fold_scratch(LANES)],
        compiler_params=pltpu.CompilerParams(dimension_semantics=("arbitrary",), vmem_limit_bytes=VMEM_LIMIT),
    )(*att_a, *att_b1, *att_b4, *att_b16, g_a, g_b, x2, tgt2, w_out_bf, sink_row, perm)


def _inproj_bwd(x2, dy, gain, w_bf, cos, sin_s, gqa, gka, gqb, gkb, bd256, bd128, perm,
                qa_raw, ka_raw, qb_raw, kb_raw, d_a, d_b1, d_b4, d_b16, dg_a, dg_b):
    s = x2.shape[0]
    tm = ROW_TILE

    def body(x_ref, dy_ref, gain_ref, w_hbm, cos_ref, sin_ref, gqa_ref, gka_ref, gqb_ref, gkb_ref, bd256_ref,
             bd128_ref, perm_ref, qa_raw_ref, ka_raw_ref, qb_raw_ref, kb_raw_ref, dqa_ref, dkva_ref,
             dq1_ref, dkv1_ref, dq4_ref, dkv4_ref, dq16_ref, dkv16_ref, dga_ref, dgb_ref,
             gx_ref, ht_ref, win_ref,
             dgain_ref, dgqa_ref, dgka_ref, dgqb_ref, dgkb_ref, w_vmem, dproj_ref):
        i = pl.program_id(0)
        perm = perm_ref[...]

        @pl.when(i == 0)
        def _():
            pltpu.sync_copy(w_hbm, w_vmem)
            dgain_ref[...] = jnp.zeros_like(dgain_ref)
            dgqa_ref[...] = jnp.zeros_like(dgqa_ref)
            dgka_ref[...] = jnp.zeros_like(dgka_ref)
            dgqb_ref[...] = jnp.zeros_like(dgqb_ref)
            dgkb_ref[...] = jnp.zeros_like(dgkb_ref)

        cos1 = cos_ref[...]
        sin1 = sin_ref[...]
        cos4 = jnp.tile(cos1, (1, 4))
        sin4 = jnp.tile(sin1, (1, 4))

        dt, dg = _qknorm_rope_bwd(dqa_ref[...], qa_raw_ref[...], gqa_ref[...], cos4, sin4, bd256_ref[...])
        dproj_ref[:, C_QA:C_KA] = dt.astype(BF16)
        dgqa_ref[...] += jnp.sum(dg, axis=0, keepdims=True)
        dt, dg = _qknorm_rope_bwd(dkva_ref[:, :A_KV_WIDTH], ka_raw_ref[...], gka_ref[...], cos1, sin1,
                                  bd128_ref[...])
        dproj_ref[:, C_KA:C_VA] = dt.astype(BF16)
        dgka_ref[...] += jnp.sum(dg, axis=0, keepdims=True)
        dproj_ref[:, C_VA:C_GA] = dkva_ref[:, A_KV_WIDTH:]
        dproj_ref[:, C_GA:C_QB] = dga_ref[...]
        dq = (dq1_ref[...].astype(F32) + _load_folded_bf16(dq4_ref, perm)) + _load_folded_bf16(dq16_ref, perm)
        dt, dg = _qknorm_rope_bwd(dq, qb_raw_ref[...], gqb_ref[...], cos4, sin4, bd256_ref[...])
        dproj_ref[:, C_QB:C_KB] = dt.astype(BF16)
        dgqb_ref[...] += jnp.sum(dg, axis=0, keepdims=True)
        dkv = (dkv1_ref[...].astype(F32) + _load_folded_bf16(dkv4_ref, perm)) + _load_folded_bf16(dkv16_ref, perm)
        dt, dg = _qknorm_rope_bwd(dkv[:, :WIDTH], kb_raw_ref[...], gkb_ref[...], cos4, sin4, bd256_ref[...])
        dproj_ref[:, C_KB:C_VB] = dt.astype(BF16)
        dgkb_ref[...] += jnp.sum(dg, axis=0, keepdims=True)
        dproj_ref[:, C_VB:C_GB] = dkv[:, WIDTH:].astype(BF16)
        dproj_ref[:, C_GB:C_END] = dgb_ref[...]
        for k, start in enumerate(WIN_START):
            win_ref[k] = dproj_ref[:, start:start + WIN]

        xt = x_ref[...]
        gain_row = gain_ref[...]
        r = lax.rsqrt(jnp.mean(xt * xt, axis=-1, keepdims=True) + EPS)
        xr = xt * r
        ht_ref[...] = (xr * gain_row).T.astype(BF16)
        dh = _dot(dproj_ref[...], w_vmem[...])
        dgain_ref[...] += jnp.sum(dh * xr, axis=0, keepdims=True)
        u = dh * gain_row
        gx_ref[...] = dy_ref[...] + r * (u - xr * jnp.mean(u * xr, axis=-1, keepdims=True))

    def acc_row(w):
        return pl.BlockSpec((1, w), lambda i: (0, 0))

    sds = jax.ShapeDtypeStruct
    any_spec = pl.BlockSpec(memory_space=pl.ANY)
    win_spec = pl.BlockSpec((N_CHIP, tm, WIN), lambda i: (0, i, 0))
    return pl.pallas_call(
        body, name="inproj_bwd", grid=(s // tm,),
        in_specs=[_rows(D_MODEL), _rows(D_MODEL), _whole(gain.shape), any_spec, _rows(LANES), _rows(LANES),
                  _whole(gqa.shape), _whole(gka.shape), _whole(gqb.shape), _whole(gkb.shape), _whole(bd256.shape),
                  _whole(bd128.shape), _whole(perm.shape),
                  _rows(WIDTH), _rows(A_KV_WIDTH), _rows(WIDTH), _rows(WIDTH),
                  _rows(WIDTH), _rows(2 * A_KV_WIDTH), _rows(WIDTH), _rows(2 * WIDTH)]
                 + [_folded_rows(WIDTH), _folded_rows(2 * WIDTH)] * 2 + [_rows(WIDTH), _rows(WIDTH)],
        out_specs=(_rows(D_MODEL), pl.BlockSpec((D_MODEL, tm), lambda i: (0, i)), win_spec, acc_row(D_MODEL), acc_row(WIDTH), acc_row(A_KV_WIDTH), acc_row(WIDTH), acc_row(WIDTH)),
        out_shape=(sds((s, D_MODEL), F32), sds((D_MODEL, s), BF16), sds((N_CHIP, s, WIN), BF16),
                   sds((1, D_MODEL), F32),
                   sds((1, WIDTH), F32), sds((1, A_KV_WIDTH), F32), sds((1, WIDTH), F32), sds((1, WIDTH), F32)),
        scratch_shapes=[pltpu.VMEM((IN_WIDTH, D_MODEL), BF16), pltpu.VMEM((tm, IN_WIDTH), BF16)],
        compiler_params=pltpu.CompilerParams(dimension_semantics=("arbitrary",), vmem_limit_bytes=VMEM_LIMIT),
    )(x2, dy, gain, w_bf, cos, sin_s, gqa, gka, gqb, gkb, bd256, bd128, perm, qa_raw, ka_raw, qb_raw, kb_raw,
      *d_a, *d_b1, *d_b4, *d_b16, dg_a, dg_b)


def _rope_tables(s):
    half = HEAD_DIM // 2
    inv = jnp.tile(ROPE_THETA ** (-jnp.arange(half, dtype=F32) / half), 4)
    sign = jnp.tile(jnp.concatenate([-jnp.ones((half,), F32), jnp.ones((half,), F32)]), 2)
    hi = (jnp.arange(s // ROPE_SPLIT) * ROPE_SPLIT).astype(F32)[:, None] * inv[None, :]
    lo = jnp.arange(ROPE_SPLIT).astype(F32)[:, None] * inv[None, :]
    ch, sh, cl, sl = jnp.cos(hi)[:, None, :], jnp.sin(hi)[:, None, :], jnp.cos(lo)[None], jnp.sin(lo)[None]
    cos = (ch * cl - sh * sl).reshape(s, LANES)
    sin = (sh * cl + ch * sl).reshape(s, LANES)
    return cos, sin * sign[None, :]


def _block_diag_ones(w):
    idx = jnp.arange(w) // HEAD_DIM
    return (idx[:, None] == idx[None, :]).astype(BF16)


def _local_step(x2, tgt2, norm_gain, w_in_bf, q_norm_a, k_norm_a, sinks_a, q_norm_b, k_norm_b, w_out_blk):
    s = x2.shape[0]
    cos, sin_s = _rope_tables(s)
    bd256, bd128 = _block_diag_ones(2 * LANES), _block_diag_ones(A_KV_WIDTH)
    gqa = jnp.tile(q_norm_a, (1, HEADS))
    gka = jnp.tile(k_norm_a, (1, 2))
    gqb = jnp.tile(q_norm_b, (1, HEADS))
    gkb = jnp.tile(k_norm_b, (1, HEADS))
    sink_row = jnp.pad(sinks_a, ((0, 0), (0, LANES - HEADS)))
    perm = _fold_matrix()

    (qa, kva, qb, kvb, qbf, kvbf, qa_raw, ka_raw, g_a, qb_raw, kb_raw, g_b, w_out_bf) = _inproj(
        x2, norm_gain, w_in_bf, cos, sin_s, gqa, gka, gqb, gkb, bd256, bd128, w_out_blk)

    att_a = _attn_fwd(qa, kva, sinks_a, dil=1, max_dist=A_MAX_DIST, name="attn_a_fwd")
    att_b1 = _attn_fwd(qb, kvb, None, dil=1, max_dist=B_MAX_DIST, name="attn_b1_fwd")
    att_b4 = _attn_fwd(qbf, kvbf, None, dil=4, max_dist=B_MAX_DIST, name="attn_b4_fwd")
    att_b16 = _attn_fwd(qbf, kvbf, None, dil=16, max_dist=B_MAX_DIST, name="attn_b16_fwd")

    (dy, do_a, do_b, do_bf, dg_a, dg_b, ld_a, ld_b, ld_bf, gw_out, loss_part, dsink) = _outproj(
        att_a, att_b1, att_b4, att_b16, g_a, g_b, x2, tgt2, w_out_bf, sink_row, perm)

    d_a = _attn_bwd(qa, kva, do_a, ld_a, dil=1, max_dist=A_MAX_DIST, name="attn_a_bwd")
    d_b1 = _attn_bwd(qb, kvb, do_b, ld_b, dil=1, max_dist=B_MAX_DIST, name="attn_b1_bwd")
    d_b4 = _attn_bwd(qbf, kvbf, do_bf, ld_bf, dil=4, max_dist=B_MAX_DIST, name="attn_b4_bwd")
    d_b16 = _attn_bwd(qbf, kvbf, do_bf, ld_bf, dil=16, max_dist=B_MAX_DIST, name="attn_b16_bwd")

    gx, h_t, wins, dgain, dgqa, dgka, dgqb, dgkb = _inproj_bwd(
        x2, dy, norm_gain, w_in_bf, cos, sin_s, gqa, gka, gqb, gkb, bd256, bd128, perm,
        qa_raw, ka_raw, qb_raw, kb_raw, d_a, d_b1, d_b4, d_b16, dg_a, dg_b)
    return loss_part, gx, h_t, wins, gw_out, (dgain, dgqa, dgka, dsink, dgqb, dgkb)


def _position():
    return lax.axis_index("x"), lax.axis_index("y"), lax.axis_index("c")


GATHER_CHUNKS = 2


def _gather_weights(blocks, name):
    n = len(blocks)
    ch = GATHER_CHUNKS

    def body(*refs):
        src_refs, dst_refs = refs[:n], refs[n:2 * n]
        ici_send, ici_recv, hop_send, hop_recv, d2d_send, d2d_recv = refs[2 * n:]
        x, y, c = _position()
        b = 2 * x + y
        via = 2 - c
        out = 3 - via
        for k in range(n):
            dst_refs[k][b] = src_refs[k][...].astype(BF16)

        def rows(k, core, j):
            half = blocks[k].shape[0] // 2
            return pl.ds(pl.multiple_of(core * half + j * (half // ch), half // ch), half // ch)

        def chip(rel):
            return x ^ (rel >> 1), y ^ (rel & 1)

        def ici(k, j, slot, rel, send_sems, recv_sems, sem):
            px, py = chip(rel)
            piece = dst_refs[k].at[slot, rows(k, c, j)]
            return pltpu.make_async_remote_copy(src_ref=piece, dst_ref=piece, send_sem=send_sems.at[sem],
                                                recv_sem=recv_sems.at[sem], device_id=(px, py, c),
                                                device_id_type=MESH)

        def direct(k, j, slot, rel):
            return ici(k, j, slot, rel, ici_send, ici_recv, ((rel - 1) * ch + j) * n + k)

        def hop(k, j, slot, rel):
            return ici(k, j, slot, rel, hop_send, hop_recv, j * n + k)

        def d2d(k, j, rel, core):
            piece = dst_refs[k].at[b ^ rel, rows(k, core, j)]
            sem = ((rel - 1) * ch + j) * n + k
            return pltpu.make_async_remote_copy(src_ref=piece, dst_ref=piece, send_sem=d2d_send.at[sem],
                                                recv_sem=d2d_recv.at[sem], device_id=(x, y, 1 - c),
                                                device_id_type=MESH)

        pieces = [(k, j) for j in range(ch) for k in range(n)]
        for k, j in pieces:
            for rel in (1, 2):
                direct(k, j, b, rel).start()
        for k, j in pieces:
            direct(k, j, b ^ via, via).wait_recv()
            hop(k, j, b ^ via, out).start()
            d2d(k, j, via, c).start()
        for k, j in pieces:
            direct(k, j, b ^ out, out).wait_recv()
            d2d(k, j, out, c).start()
        for k, j in pieces:
            hop(k, j, b ^ 3, via).wait_recv()
            d2d(k, j, 3, c).start()
        for k, j in pieces:
            for rel in (1, 2, 3):
                d2d(k, j, rel, 1 - c).wait_recv()
        for k, j in pieces:
            for rel in (1, 2):
                direct(k, j, b, rel).wait_send()
            hop(k, j, b ^ via, out).wait_send()
            d2d(k, j, via, c).wait_send()
            d2d(k, j, out, c).wait_send()
            d2d(k, j, 3, c).wait_send()

    vmem_spec = pl.BlockSpec(memory_space=pltpu.VMEM)
    dma = pltpu.SemaphoreType.DMA
    out_shape = tuple(jax.ShapeDtypeStruct((N_CHIP,) + a.shape, BF16) for a in blocks)
    return pl.pallas_call(
        body, name=name, in_specs=[vmem_spec] * n, out_specs=tuple([vmem_spec] * n), out_shape=out_shape,
        scratch_shapes=[dma((2 * ch * n,)), dma((2 * ch * n,)), dma((ch * n,)), dma((ch * n,)),
                        dma((3 * ch * n,)), dma((3 * ch * n,))],
        compiler_params=pltpu.CompilerParams(vmem_limit_bytes=VMEM_LIMIT),
    )(*blocks)


def _grad_reduce(order, h_t, wins, gw_out, small):
    s = h_t.shape[1]
    tk = GRAD_ROWS
    n_i = s // tk
    half = D_MODEL // 2
    o_half = OUT_ROWS // 2
    n_rel = N_CHIP - 1

    def body(order_ref, ht_ref, win_ref, gwo_ref, small_ref,
             win_out, wout_out, small_out,
             acc, mine, s1, r1, s2, r2, so1, ro1, so2, ro2, pair_in, pair_o, small_land,
             s1_send, s1_recv, s2_send, s2_recv, o1_send, o1_recv, o2_send, o2_recv,
             pair_send, pair_recv, small_send, small_recv):
        j = pl.program_id(0)
        i = pl.program_id(1)
        x, y, c = _position()
        me = 4 * x + 2 * y + c
        sibling = (x, y, 1 - c)
        my_rows = pl.ds(pl.multiple_of(c * half, half), half)
        sib_rows = pl.ds(pl.multiple_of((1 - c) * half, half), half)

        def chip_of(rel):
            return x ^ (rel >> 1), y ^ (rel & 1)

        def level1(k):
            return pltpu.make_async_remote_copy(src_ref=s1.at[k], dst_ref=r1.at[k], send_sem=s1_send.at[k],
                                                recv_sem=s1_recv.at[k], device_id=sibling, device_id_type=MESH)

        def level2(k):
            px, py = chip_of(RELATIONS[k])
            return pltpu.make_async_remote_copy(src_ref=s2.at[k], dst_ref=r2.at[k], send_sem=s2_send.at[k],
                                                recv_sem=s2_recv.at[k], device_id=(px, py, c), device_id_type=MESH)

        def out_level1(bk):
            return pltpu.make_async_remote_copy(src_ref=so1.at[bk], dst_ref=ro1.at[bk], send_sem=o1_send.at[bk],
                                                recv_sem=o1_recv.at[bk], device_id=sibling, device_id_type=MESH)

        def out_level2(k):
            px, py = chip_of(RELATIONS[k])
            return pltpu.make_async_remote_copy(src_ref=so2.at[k], dst_ref=ro2.at[k], send_sem=o2_send.at[k],
                                                recv_sem=o2_recv.at[k], device_id=(px, py, c), device_id_type=MESH)

        def small_copy(d):
            px, py, pc = x ^ (d >> 2), y ^ ((d >> 1) & 1), c ^ (d & 1)
            return pltpu.make_async_remote_copy(src_ref=small_ref, dst_ref=small_land.at[me],
                                                send_sem=small_send.at[d], recv_sem=small_recv.at[d],
                                                device_id=(px, py, pc), device_id_type=MESH)

        def pair_copy(k, buf):
            return pltpu.make_async_remote_copy(src_ref=buf.at[0], dst_ref=buf.at[1], send_sem=pair_send.at[k],
                                                recv_sem=pair_recv.at[k], device_id=sibling, device_id_type=MESH)

        def out_rows(bk, core):
            return pl.ds(pl.multiple_of(bk * OUT_ROWS + core * o_half, o_half), o_half)

        @pl.when((j == 0) & (i == 0))
        def _():
            for d in range(1, N_DEV):
                small_copy(d).start()
            small_land[me] = small_ref[...]
            for bk in range(N_CHIP):
                so1[bk] = gwo_ref[out_rows(bk, 1 - c), :].astype(BF16)
                out_level1(bk).start()

        @pl.when((j == 0) & (i == 1))
        def _():
            b = 2 * x + y
            for bk in range(N_CHIP):
                out_level1(bk).wait_recv()
            for k in range(n_rel):
                px, py = chip_of(RELATIONS[k])
                bk = 2 * px + py
                so2[k] = (gwo_ref[out_rows(bk, c), :] + ro1[bk].astype(F32)).astype(BF16)
                out_level2(k).start()

        @pl.when(i == 0)
        def _():
            acc[...] = jnp.zeros_like(acc)

        for n0 in range(0, WIN, ACC_COLS):
            n1 = min(n0 + ACC_COLS, WIN)
            acc[:, n0:n1] += _dot(ht_ref[...], win_ref[:, n0:n1])

        for k in range(N_CHIP):
            @pl.when((j == k) & (i == n_i - 1))
            def _(k=k):
                s1[k] = acc[sib_rows, :].astype(BF16)
                level1(k).start()
                mine[...] = acc[my_rows, :]

            if k < n_rel:
                @pl.when((j == k + 1) & (i == 1))
                def _(k=k):
                    level1(k).wait_recv()
                    s2[k] = (mine[...] + r1[k].astype(F32)).astype(BF16)
                    level2(k).start()

        @pl.when((j == N_CHIP - 1) & (i == n_i - 1))
        def _():
            b = 2 * x + y
            level1(N_CHIP - 1).wait_recv()
            total = mine[...] + r1[N_CHIP - 1].astype(F32)
            for k in range(n_rel):
                level2(k).wait_recv()
                total = total + r2[k].astype(F32)
            total = total.T
            pair_in[0] = total
            pair_copy(0, pair_in).start()
            total_o = gwo_ref[out_rows(b, c), :] + ro1[b].astype(F32)
            for k in range(n_rel):
                out_level2(k).wait_recv()
                total_o = total_o + ro2[k].astype(F32)
            pair_o[0] = total_o
            pair_copy(1, pair_o).start()
            for core in range(2):
                @pl.when(c == core)
                def _(core=core):
                    win_out[:, core * half:(core + 1) * half] = total
            wout_out[c] = total_o
            for d in range(1, N_DEV):
                small_copy(d).wait_recv()
            small_out[...] = small_land[...]
            pair_copy(0, pair_in).wait_recv()
            for core in range(2):
                @pl.when(c == core)
                def _(core=core):
                    win_out[:, (1 - core) * half:(2 - core) * half] = pair_in[1]
            pair_copy(1, pair_o).wait_recv()
            wout_out[1 - c] = pair_o[1]
            for d in range(1, N_DEV):
                small_copy(d).wait_send()
            for k in range(N_CHIP):
                level1(k).wait_send()
                out_level1(k).wait_send()
            for k in range(n_rel):
                level2(k).wait_send()
                out_level2(k).wait_send()
            pair_copy(0, pair_in).wait_send()
            pair_copy(1, pair_o).wait_send()

    vmem = pl.BlockSpec(memory_space=pltpu.VMEM)
    dma = pltpu.SemaphoreType.DMA
    sds = jax.ShapeDtypeStruct
    grid_spec = pltpu.PrefetchScalarGridSpec(
        num_scalar_prefetch=1, grid=(N_CHIP, n_i),
        in_specs=[pl.BlockSpec((D_MODEL, tk), lambda j, i, order: (0, i)),
                  pl.BlockSpec((None, tk, WIN), lambda j, i, order: (order[j], i, 0)), vmem, vmem],
        out_specs=(vmem, vmem, vmem),
        scratch_shapes=[
            pltpu.VMEM((D_MODEL, WIN), F32), pltpu.VMEM((half, WIN), F32),
            pltpu.VMEM((N_CHIP, half, WIN), BF16), pltpu.VMEM((N_CHIP, half, WIN), BF16),
            pltpu.VMEM((n_rel, half, WIN), BF16), pltpu.VMEM((n_rel, half, WIN), BF16),
            pltpu.VMEM((N_CHIP, o_half, D_MODEL), BF16), pltpu.VMEM((N_CHIP, o_half, D_MODEL), BF16),
            pltpu.VMEM((n_rel, o_half, D_MODEL), BF16), pltpu.VMEM((n_rel, o_half, D_MODEL), BF16),
            pltpu.VMEM((2, WIN, half), F32), pltpu.VMEM((2, o_half, D_MODEL), F32),
            pltpu.VMEM((N_DEV, PACK_ROWS, D_MODEL), F32),
            dma((N_CHIP,)), dma((N_CHIP,)), dma((n_rel,)), dma((n_rel,)),
            dma((N_CHIP,)), dma((N_CHIP,)), dma((n_rel,)), dma((n_rel,)),
            dma((2,)), dma((2,)), dma((N_DEV,)), dma((N_DEV,))])
    return pl.pallas_call(
        body, name="grad_w_in_reduce", grid_spec=grid_spec,
        out_shape=(sds((WIN, D_MODEL), F32), sds((2, o_half, D_MODEL), F32), sds((N_DEV, PACK_ROWS, D_MODEL), F32)),
        compiler_params=pltpu.CompilerParams(dimension_semantics=("arbitrary", "arbitrary"),
                                             vmem_limit_bytes=VMEM_LIMIT),
    )(order, h_t, wins, gw_out, small)


ADAM_STEPS = 4


def _adamw_math(w, g, m, v):
    m = ADAM_B1 * m + (1.0 - ADAM_B1) * g
    v = ADAM_B2 * v + (1.0 - ADAM_B2) * (g * g)
    m_hat = m / (1.0 - ADAM_B1 ** ADAM_STEP)
    v_hat = v / (1.0 - ADAM_B2 ** ADAM_STEP)
    delta = -ADAM_LR * (m_hat / (jnp.sqrt(v_hat) + ADAM_EPS) + ADAM_WD * w)
    return delta, m, v


def _adamw(w, g, m, v, name):
    r, c = w.shape

    def body(w_ref, g_ref, m_ref, v_ref, d_ref, nm_ref, nv_ref):
        delta, nm, nv = _adamw_math(w_ref[...], g_ref[...], m_ref[...], v_ref[...])
        d_ref[...] = delta
        nm_ref[...] = nm
        nv_ref[...] = nv

    rows = r // ADAM_STEPS
    assert rows * ADAM_STEPS == r and rows % 8 == 0
    spec = pl.BlockSpec((rows, c), lambda i: (i, 0))
    shape = jax.ShapeDtypeStruct((r, c), F32)
    return pl.pallas_call(
        body, name=name, grid=(ADAM_STEPS,), in_specs=[spec] * 4, out_specs=(spec,) * 3,
        out_shape=(shape,) * 3, compiler_params=pltpu.CompilerParams(vmem_limit_bytes=VMEM_LIMIT),
    )(w, g, m, v)


def _adamw_window(w, window, shift, m, v, name):
    r, c = w.shape
    rows = r // ADAM_STEPS
    assert rows * ADAM_STEPS == r and rows % 8 == 0

    def body(shift_ref, w_ref, win_hbm, m_ref, v_ref, g_ref, d_ref, nm_ref, nv_ref, g_vmem):
        start = pl.multiple_of(shift_ref[0] + pl.program_id(0) * rows, 8)
        pltpu.sync_copy(win_hbm.at[pl.ds(start, rows)], g_vmem)
        g = g_vmem[...]
        g_ref[...] = g
        delta, nm, nv = _adamw_math(w_ref[...], g, m_ref[...], v_ref[...])
        d_ref[...] = delta
        nm_ref[...] = nm
        nv_ref[...] = nv

    spec = pl.BlockSpec((rows, c), lambda i, shift_ref: (i, 0))
    shape = jax.ShapeDtypeStruct((r, c), F32)
    grid_spec = pltpu.PrefetchScalarGridSpec(
        num_scalar_prefetch=1, grid=(ADAM_STEPS,),
        in_specs=[spec, pl.BlockSpec(memory_space=pl.ANY), spec, spec], out_specs=(spec,) * 4,
        scratch_shapes=[pltpu.VMEM((rows, c), F32)])
    return pl.pallas_call(
        body, name=name, grid_spec=grid_spec, out_shape=(shape,) * 4,
        compiler_params=pltpu.CompilerParams(vmem_limit_bytes=VMEM_LIMIT),
    )(shift, w, window, m, v)


PACK_ROWS = 8


def _fold_heads(v):
    y = v[:, 0:LANES]
    for j in range(1, v.shape[1] // LANES):
        y = y + v[:, j * LANES:(j + 1) * LANES]
    return y + pltpu.roll(y, HEAD_DIM, 1)


N_SMALL = 6


def _small_adamw(recv, weights, m, v):
    def body(*refs):
        r_ref = refs[0]
        w_refs, m_refs, v_refs = (refs[1 + n * N_SMALL:1 + (n + 1) * N_SMALL] for n in range(3))
        outs = refs[1 + 3 * N_SMALL:]
        g_refs, d_refs, nm_refs, nv_refs = (outs[n * N_SMALL:(n + 1) * N_SMALL] for n in range(4))
        loss_ref = outs[4 * N_SMALL]
        tot = r_ref[0]
        for j in range(1, N_DEV):
            tot = tot + r_ref[j]
        loss_ref[...] = tot[3:4, 0:LANES]
        row1 = tot[1:2, :]
        row2 = tot[2:3, :]
        grads = [tot[0:1, :],
                 _fold_heads(row1[:, 0:WIDTH])[:, :HEAD_DIM],
                 _fold_heads(row2[:, WIDTH:WIDTH + A_KV_WIDTH])[:, :HEAD_DIM],
                 row2[:, WIDTH + A_KV_WIDTH:WIDTH + A_KV_WIDTH + HEADS],
                 _fold_heads(row1[:, WIDTH:2 * WIDTH])[:, :HEAD_DIM],
                 _fold_heads(row2[:, 0:WIDTH])[:, :HEAD_DIM]]
        for n, g in enumerate(grads):
            g_refs[n][...] = g
            delta, nm, nv = _adamw_math(w_refs[n][...], g, m_refs[n][...], v_refs[n][...])
            d_refs[n][...] = delta
            nm_refs[n][...] = nm
            nv_refs[n][...] = nv

    shapes = tuple(jax.ShapeDtypeStruct(a.shape, F32) for a in weights)
    outs = pl.pallas_call(body, name="small_adamw", out_shape=shapes * 4 + (jax.ShapeDtypeStruct((1, LANES), F32),)
                          )(recv, *weights, *m, *v)
    return tuple(outs[n * N_SMALL:(n + 1) * N_SMALL] for n in range(4)) + (outs[4 * N_SMALL],)


def kernel(x, norm_gain, w_in, q_norm_a, k_norm_a, sinks_a, q_norm_b, k_norm_b, w_out, loss_target, m_norm_gain, m_w_in, m_q_norm_a, m_k_norm_a, m_sinks_a, m_q_norm_b, m_k_norm_b, m_w_out, v_norm_gain, v_w_in, v_q_norm_a, v_k_norm_a, v_sinks_a, v_q_norm_b, v_k_norm_b, v_w_out):
    chip = 2 * lax.axis_index("x") + lax.axis_index("y")

    w_in_t, m_w_in_t, v_w_in_t = w_in[0].T, m_w_in[0].T, v_w_in[0].T

    (w_in_all,) = _gather_weights([w_in_t], "gather_weights")
    w_in_bf = w_in_all.reshape(IN_WIDTH, D_MODEL)

    loss_part, gx, h_t, wins, gw_out, (dgain, dgqa, dgka, dsink, dgqb, dgkb) = _local_step(
        x[0], loss_target[0], norm_gain, w_in_bf, q_norm_a, k_norm_a, sinks_a, q_norm_b, k_norm_b, w_out[0])

    small = jnp.concatenate([
        dgain, jnp.concatenate([dgqa, dgqb], axis=1),
        jnp.concatenate([dgkb, dgka, dsink, jnp.zeros((1, D_MODEL - WIDTH - 2 * A_KV_WIDTH), F32)], axis=1),
        jnp.pad(loss_part, ((0, 0), (0, D_MODEL - LANES))),
        jnp.zeros((PACK_ROWS - 4, D_MODEL), F32)], axis=0)
    order = (chip ^ jnp.array(RELATIONS, jnp.int32)).astype(jnp.int32)
    win_sum, wout_sum, small_recv = _grad_reduce(order, h_t, wins, gw_out, small)
    shift = jnp.array(WIN_SHIFT, jnp.int32)[chip].reshape(1)
    g_w_out = wout_sum.reshape(OUT_ROWS, D_MODEL)

    g_w_in, d_w_in, nm_w_in, nv_w_in = (
        a.T for a in _adamw_window(w_in_t, win_sum, shift, m_w_in_t, v_w_in_t, "adamw_w_in"))
    d_w_out, nm_w_out, nv_w_out = _adamw(w_out[0], g_w_out, m_w_out[0], v_w_out[0], "adamw_w_out")
    g_s, d_s, nm_s, nv_s, loss_row = _small_adamw(
        small_recv,
        (norm_gain, q_norm_a, k_norm_a, sinks_a, q_norm_b, k_norm_b),
        (m_norm_gain, m_q_norm_a, m_k_norm_a, m_sinks_a, m_q_norm_b, m_k_norm_b),
        (v_norm_gain, v_q_norm_a, v_k_norm_a, v_sinks_a, v_q_norm_b, v_k_norm_b))
    loss = loss_row[0, 0]

    def leaves(small_ones, big_in, big_out):
        return (small_ones[0], big_in[None]) + tuple(small_ones[1:]) + (big_out[None],)

    return ((loss, gx[None]) + leaves(g_s, g_w_in, g_w_out) + leaves(d_s, d_w_in, d_w_out)
            + leaves(nm_s, nm_w_in, nm_w_out) + leaves(nv_s, nv_w_in, nv_w_out))
```

```python
import jax
import jax.numpy as jnp
from jax import lax
from jax.experimental import pallas as pl
from jax.experimental.pallas import tpu as pltpu

F32 = jnp.float32
BF16 = jnp.bfloat16

D_MODEL = 1024
HEAD_DIM = 64
HEADS = 8
WIDTH = HEADS * HEAD_DIM
A_KV_WIDTH = 2 * HEAD_DIM
BLOCK = 128
LANES = 128
FOLD = 16
A_MAX_DIST = 127
B_MAX_DIST = 128
ROPE_THETA = 10000.0
ROPE_SPLIT = 64
EPS = 1e-6
NEG = -1e30
SCALE = HEAD_DIM ** -0.5

IN_WIDTH = 3328
C_QA, C_KA, C_VA, C_GA, C_QB, C_KB, C_VB, C_GB, C_END = 0, 512, 640, 768, 1280, 1792, 2304, 2816, 3328

N_DEV = 8
N_CHIP = 4
MESH = pl.DeviceIdType.MESH
IN_COLS = IN_WIDTH // N_CHIP
WIN = 896
WIN_START = (0, 768, 1664, 2432)
WIN_SHIFT = (0, 64, 0, 64)
OUT_ROWS = D_MODEL // N_CHIP
RELATIONS = (3, 1, 2, 0)

ADAM_LR = 0.001
ADAM_B1 = 0.9
ADAM_B2 = 0.999
ADAM_EPS = 1e-08
ADAM_WD = 0.01
ADAM_STEP = 10

ROW_TILE = 256
FOLD_ROWS = ROW_TILE // FOLD
GRAD_ROWS = 1024
ACC_COLS = 256
VMEM_LIMIT = 56 * 1024 * 1024


def _dot(a, b):
    return jnp.dot(a, b, preferred_element_type=F32)


def _dot_nt(a, b):
    return lax.dot_general(a, b, (((1,), (1,)), ((), ())), preferred_element_type=F32)


def _dot_tn(a, b):
    return lax.dot_general(a, b, (((0,), (0,)), ((), ())), preferred_element_type=F32)


def _head_sum(z, bd):
    w = bd.shape[0]
    zb = z.astype(BF16)
    parts = [_dot(zb[:, a:a + w], bd) for a in range(0, z.shape[1], w)]
    return parts[0] if len(parts) == 1 else jnp.concatenate(parts, axis=1)


def _swap_halves(t):
    w = t.shape[1]
    lane = lax.broadcasted_iota(jnp.int32, t.shape, 1)
    return jnp.where(lane % HEAD_DIM < HEAD_DIM // 2, pltpu.roll(t, w - 32, 1), pltpu.roll(t, 32, 1))


def _qknorm_rope(t, g, cos, sin_s, bd):
    r = lax.rsqrt(_head_sum(t * t, bd) * (1.0 / HEAD_DIM) + EPS)
    n = (t * r) * g
    return n * cos + _swap_halves(n) * sin_s


def _qknorm_rope_bwd(dout, t, g, cos, sin_s, bd):
    dout, t = dout.astype(F32), t.astype(F32)
    dn = dout * cos + _swap_halves(dout * sin_s)
    r = lax.rsqrt(_head_sum(t * t, bd) * (1.0 / HEAD_DIM) + EPS)
    tr = t * r
    u = dn * g
    dt = r * (u - tr * (_head_sum(u * tr, bd) * (1.0 / HEAD_DIM)))
    return dt, dn * tr


def _sigmoid(g):
    return 1.0 / (1.0 + jnp.exp(-g))


def _expand_heads(st):
    t = st.shape[0]
    lane = lax.broadcasted_iota(jnp.int32, (t, LANES), 1)
    chunks = []
    for c in range(WIDTH // LANES):
        chunks.append(jnp.where(lane < HEAD_DIM, st[:, 2 * c:2 * c + 1], st[:, 2 * c + 1:2 * c + 2]))
    return jnp.concatenate(chunks, axis=1)


def _reduce_heads(z):
    t = z.shape[0]
    lane = lax.broadcasted_iota(jnp.int32, (t, LANES), 1)
    out = jnp.zeros((t, LANES), F32)
    for c in range(WIDTH // LANES):
        zc = z[:, c * LANES:(c + 1) * LANES]
        for ph in range(2):
            s = jnp.sum(jnp.where((lane // HEAD_DIM) == ph, zc, 0.0), axis=-1, keepdims=True)
            out = jnp.where(lane == 2 * c + ph, s, out)
    return out


def _fold_scratch(w):
    return pltpu.VMEM((w // LANES, ROW_TILE, LANES), F32)


def _store_folded(out_ref, val, scr, col0=0):
    w = val.shape[1]
    n = w // LANES
    for c in range(n):
        scr[c] = val[:, c * LANES:(c + 1) * LANES]
    for r in range(FOLD):
        piece = [scr[c, pl.ds(r, FOLD_ROWS, stride=FOLD), :] for c in range(n)]
        out_ref[r, :, col0:col0 + w] = (piece[0] if n == 1 else jnp.concatenate(piece, axis=1)).astype(out_ref.dtype)


def _load_folded(in_ref, scr):
    n = in_ref.shape[2] // LANES
    for r in range(FOLD):
        blk = in_ref[r].astype(F32)
        for c in range(n):
            scr[c, pl.ds(r, FOLD_ROWS, stride=FOLD), :] = blk[:, c * LANES:(c + 1) * LANES]
    return scr[0] if n == 1 else jnp.concatenate([scr[c] for c in range(n)], axis=1)


def _fold_matrix():
    f = jnp.arange(ROW_TILE)
    return (jnp.arange(ROW_TILE)[None, :] == (FOLD * (f % FOLD_ROWS) + f // FOLD_ROWS)[:, None]).astype(BF16)


def _store_folded_bf16(out_ref, val, perm):
    folded = _dot(perm, val.astype(BF16)).astype(out_ref.dtype)
    for r in range(FOLD):
        out_ref[r] = folded[r * FOLD_ROWS:(r + 1) * FOLD_ROWS]


def _load_folded_bf16(in_ref, perm):
    blk = jnp.concatenate([in_ref[r] for r in range(FOLD)], axis=0)
    return _dot(perm, blk)


def _rows(w, tm=ROW_TILE):
    return pl.BlockSpec((tm, w), lambda i: (i, 0))


def _folded_rows(w):
    return pl.BlockSpec((FOLD, FOLD_ROWS, w), lambda i: (0, i, 0))


def _whole(shape):
    return pl.BlockSpec(shape, lambda i: (0,) * len(shape))


def _inproj(x2, gain, w_bf, cos, sin_s, gqa, gka, gqb, gkb, bd256, bd128, w_out_blk):
    s = x2.shape[0]
    tm = ROW_TILE
    n_steps = s // tm
    n_rel = N_CHIP - 1
    o_half = OUT_ROWS // 2

    def body(x_ref, gain_ref, w_hbm, cos_ref, sin_ref, gqa_ref, gka_ref, gqb_ref, gkb_ref, bd256_ref, bd128_ref,
             wout_ref, qa_ref, kva_ref, qb_ref, kvb_ref, qbf_ref, kvbf_ref,
             qa_raw_ref, ka_raw_ref, ga_ref, qb_raw_ref, kb_raw_ref, gb_ref, wout_all_ref,
             w_vmem, scr, land, ici_send, ici_recv, d2d_send, d2d_recv):
        i = pl.program_id(0)
        px_, py_, c = _position()
        b = 2 * px_ + py_

        def piece(chip_idx, core):
            return land.at[chip_idx, pl.ds(pl.multiple_of(core * o_half, o_half), o_half)]

        def other_chip(d):
            ox, oy = px_ ^ (d >> 1), py_ ^ (d & 1)
            return ox, oy, 2 * ox + oy

        def ici_copy(d, chip_idx):
            ox, oy, _ = other_chip(d)
            return pltpu.make_async_remote_copy(
                src_ref=piece(chip_idx, c), dst_ref=piece(chip_idx, c), send_sem=ici_send.at[d - 1],
                recv_sem=ici_recv.at[d - 1], device_id=(ox, oy, c), device_id_type=MESH)

        def d2d_copy(d, core):
            return pltpu.make_async_remote_copy(
                src_ref=piece(other_chip(d)[2], core), dst_ref=piece(other_chip(d)[2], core),
                send_sem=d2d_send.at[d - 1], recv_sem=d2d_recv.at[d - 1], device_id=(px_, py_, 1 - c),
                device_id_type=MESH)

        @pl.when(i == 0)
        def _():
            pltpu.sync_copy(w_hbm, w_vmem)
            land[b] = wout_ref[...].astype(BF16)
            for d in range(1, N_CHIP):
                ici_copy(d, b).start()

        @pl.when(i == n_steps // 2)
        def _():
            for d in range(1, N_CHIP):
                ici_copy(d, other_chip(d)[2]).wait_recv()
                d2d_copy(d, c).start()

        @pl.when(i == n_steps - 1)
        def _():
            for d in range(1, N_CHIP):
                d2d_copy(d, 1 - c).wait_recv()
            for d in range(1, N_CHIP):
                ici_copy(d, b).wait_send()
                d2d_copy(d, c).wait_send()
            for k in range(N_CHIP):
                wout_all_ref[k * OUT_ROWS:(k + 1) * OUT_ROWS, :] = land[k]

        xt = x_ref[...]
        r = lax.rsqrt(jnp.mean(xt * xt, axis=-1, keepdims=True) + EPS)
        h = ((xt * r) * gain_ref[...]).astype(BF16)
        cos1 = cos_ref[...]
        sin1 = sin_ref[...]
        cos4 = jnp.tile(cos1, (1, 4))
        sin4 = jnp.tile(sin1, (1, 4))

        def seg(a, b):
            return _dot_nt(h, w_vmem[a:b, :])

        t = seg(C_QA, C_KA)
        qa_raw_ref[...] = t.astype(BF16)
        qa_ref[...] = (_qknorm_rope(t, gqa_ref[...], cos4, sin4, bd256_ref[...]) * SCALE).astype(BF16)
        t = seg(C_KA, C_VA)
        ka_raw_ref[...] = t.astype(BF16)
        kva_ref[:, :A_KV_WIDTH] = _qknorm_rope(t, gka_ref[...], cos1, sin1, bd128_ref[...]).astype(BF16)
        kva_ref[:, A_KV_WIDTH:] = seg(C_VA, C_GA).astype(BF16)
        ga_ref[...] = seg(C_GA, C_QB).astype(BF16)
        t = seg(C_QB, C_KB)
        qb_raw_ref[...] = t.astype(BF16)
        t = _qknorm_rope(t, gqb_ref[...], cos4, sin4, bd256_ref[...]) * SCALE
        qb_ref[...] = t.astype(BF16)
        _store_folded(qbf_ref, t, scr)
        t = seg(C_KB, C_VB)
        kb_raw_ref[...] = t.astype(BF16)
        t = _qknorm_rope(t, gkb_ref[...], cos4, sin4, bd256_ref[...])
        kvb_ref[:, :WIDTH] = t.astype(BF16)
        _store_folded(kvbf_ref, t, scr)
        t = seg(C_VB, C_GB)
        kvb_ref[:, WIDTH:] = t.astype(BF16)
        _store_folded(kvbf_ref, t, scr, WIDTH)
        gb_ref[...] = seg(C_GB, C_END).astype(BF16)

    sds = jax.ShapeDtypeStruct
    ln = s // FOLD
    out_shape = (sds((s, WIDTH), BF16), sds((s, 2 * A_KV_WIDTH), BF16), sds((s, WIDTH), BF16),
                 sds((s, 2 * WIDTH), BF16), sds((FOLD, ln, WIDTH), BF16), sds((FOLD, ln, 2 * WIDTH), BF16),
                 sds((s, WIDTH), BF16), sds((s, A_KV_WIDTH), BF16), sds((s, WIDTH), BF16),
                 sds((s, WIDTH), BF16), sds((s, WIDTH), BF16), sds((s, WIDTH), BF16),
                 sds((D_MODEL, D_MODEL), BF16))
    out_specs = (_rows(WIDTH), _rows(2 * A_KV_WIDTH), _rows(WIDTH), _rows(2 * WIDTH),
                 _folded_rows(WIDTH), _folded_rows(2 * WIDTH),
                 _rows(WIDTH), _rows(A_KV_WIDTH), _rows(WIDTH), _rows(WIDTH), _rows(WIDTH), _rows(WIDTH),
                 _whole((D_MODEL, D_MODEL)))
    dma = pltpu.SemaphoreType.DMA
    return pl.pallas_call(
        body, name="inproj_fwd", grid=(n_steps,),
        in_specs=[_rows(D_MODEL), _whole(gain.shape), pl.BlockSpec(memory_space=pl.ANY), _rows(LANES), _rows(LANES),
                  _whole(gqa.shape), _whole(gka.shape), _whole(gqb.shape), _whole(gkb.shape), _whole(bd256.shape),
                  _whole(bd128.shape), _whole(w_out_blk.shape)],
        out_specs=out_specs, out_shape=out_shape,
        scratch_shapes=[pltpu.VMEM((IN_WIDTH, D_MODEL), BF16), _fold_scratch(WIDTH),
                        pltpu.VMEM((N_CHIP, OUT_ROWS, D_MODEL), BF16),
                        dma((n_rel,)), dma((n_rel,)), dma((n_rel,)), dma((n_rel,))],
        compiler_params=pltpu.CompilerParams(dimension_semantics=("arbitrary",), vmem_limit_bytes=VMEM_LIMIT),
    )(x2, gain, w_bf, cos, sin_s, gqa, gka, gqb, gkb, bd256, bd128, w_out_blk)


def _seq_pos(idx, dil):
    if dil == 4:
        return 4 * (idx % 32) + idx // 32
    return idx


SOFTMAX_ROWS = 64


def _upper_mask(dil, r0=0, rows=2 * BLOCK):
    qi = (lax.broadcasted_iota(jnp.int32, (rows, BLOCK), 0) + r0) % BLOCK
    kj = lax.broadcasted_iota(jnp.int32, (rows, BLOCK), 1)
    return _seq_pos(kj, dil) > _seq_pos(qi, dil)


def _eye_mask(r0=0, rows=2 * BLOCK):
    qi = (lax.broadcasted_iota(jnp.int32, (rows, BLOCK), 0) + r0) % BLOCK
    kj = lax.broadcasted_iota(jnp.int32, (rows, BLOCK), 1)
    return qi == kj


def _stack_heads(a2, c, gqa):
    lane = lax.broadcasted_iota(jnp.int32, (1, LANES), 1) // HEAD_DIM
    zero = jnp.zeros_like(a2)
    if gqa:
        keep = lane == (c // 2)
        return jnp.concatenate([jnp.where(keep, a2, zero), jnp.where(keep, _swap_heads(a2), zero)], axis=0)
    return jnp.concatenate([jnp.where(lane == 0, a2, zero), jnp.where(lane == 1, a2, zero)], axis=0)


def _unstack_heads(a, c, gqa):
    lane = lax.broadcasted_iota(jnp.int32, (1, LANES), 1) // HEAD_DIM
    if gqa:
        return jnp.where(lane == (c // 2), a[:BLOCK], _swap_heads(a[BLOCK:]))
    return jnp.where(lane == 0, a[:BLOCK], a[BLOCK:])


def _stacked_head_ids(c, gqa):
    if gqa:
        return 2 * c + c // 2, 2 * c + 1 - c // 2
    return 2 * c, 2 * c + 1


def _per_head_rows(blk, heads):
    return jnp.concatenate([blk[:, heads[0]:heads[0] + 1], blk[:, heads[1]:heads[1] + 1]], axis=0)


def _attn_view(a, dil):
    if dil == 1:
        return a[None]
    if dil == 4:
        return a.reshape(4, 4, a.shape[1], a.shape[2])
    return a


def _attn_unview(a, dil):
    if dil == 1:
        return a[0]
    if dil == 4:
        return a.reshape(FOLD, a.shape[2], a.shape[3])
    return a


ATTN_BLOCKS_PER_STEP = 8


def _attn_specs(dil):
    if dil == 4:
        def spec(n, fn):
            return lambda w: pl.BlockSpec((4, None, n * BLOCK // 4, w), lambda r, i: (0, r, fn(i), 0))
    else:
        def spec(n, fn):
            return lambda w: pl.BlockSpec((None, n * BLOCK, w), lambda r, i: (r, fn(i), 0))
    return spec


def _blk_rows(g, dil):
    n = BLOCK // 4 if dil == 4 else BLOCK
    if isinstance(g, int):
        return slice(g * n, (g + 1) * n)
    return pl.ds(pl.multiple_of(g * n, n), n)


def _blk_load(ref, sl, dil, g=0):
    if dil == 4:
        return ref[:, _blk_rows(g, dil), sl].reshape(BLOCK, sl.stop - sl.start)
    return ref[_blk_rows(g, dil), sl]


def _blk_store(ref, sl, val, dil, g=0):
    val = val.astype(ref.dtype)
    if dil == 4:
        ref[:, _blk_rows(g, dil), sl] = val.reshape(4, BLOCK // 4, sl.stop - sl.start)
    else:
        ref[_blk_rows(g, dil), sl] = val


def _swap_heads(a):
    return pltpu.roll(a.astype(F32), HEAD_DIM, 1).astype(a.dtype)


STAT_SHIFT = 8


def _attn_fwd(q, kv, sinks, *, dil, max_dist, name):
    q, kv = _attn_view(q, dil), _attn_view(kv, dil)
    kw = kv.shape[-1] // 2
    gqa = kw == A_KV_WIDTH
    n_seq = dil
    nb = (q.shape[-2] * (4 if dil == 4 else 1)) // BLOCK
    per_step = min(ATTN_BLOCKS_PER_STEP, nb)
    with_sinks = sinks is not None
    all_lanes = slice(0, LANES)
    assert max_dist in (BLOCK - 1, BLOCK) and nb % per_step == 0 and (per_step == 1 or per_step % 2 == 0)
    diag = max_dist == BLOCK

    def body(*refs):
        if with_sinks:
            q_ref, kvp_ref, kvc_ref, sink_ref, o_ref, ml_ref = refs
        else:
            q_ref, kvp_ref, kvc_ref, o_ref, ml_ref = refs

        chunks = range(WIDTH // LANES)

        def matmuls_in(g, has_prev):
            prev_ref, prev_g = (kvp_ref, 0) if (isinstance(g, int) and g == 0) else (kvc_ref, g - 1)
            scores, values = [], []
            for c in chunks:
                sl = slice(c * LANES, (c + 1) * LANES)
                ksl = slice(0, LANES) if gqa else sl
                vsl = slice(ksl.start + kw, ksl.stop + kw)
                kcur, vcur = _blk_load(kvc_ref, ksl, dil, g), _blk_load(kvc_ref, vsl, dil, g)
                qs = _stack_heads(_blk_load(q_ref, sl, dil, g), c, gqa)
                if has_prev:
                    kcur = jnp.concatenate([_blk_load(prev_ref, ksl, dil, prev_g), kcur], axis=0)
                    vcur = jnp.concatenate([_blk_load(prev_ref, vsl, dil, prev_g), vcur], axis=0)
                scores.append(_dot_nt(qs, kcur))
                values.append(vcur)
            return scores, values

        def tile_ops(has_prev, scores):
            lane = lax.broadcasted_iota(jnp.int32, (1, LANES), 1)
            with_diag = diag and has_prev
            upper, eye = _upper_mask(dil), _eye_mask()
            first_rows = lax.broadcasted_iota(jnp.int32, (2 * BLOCK, 1), 0) < BLOCK
            ml_blk = jnp.zeros((BLOCK, LANES), F32)
            probs = []
            for c in chunks:
                heads = _stacked_head_ids(c, gqa)
                s = scores[c]
                if has_prev:
                    s_p = s[:, :BLOCK]
                    sc = jnp.where(upper, s_p, s[:, BLOCK:])
                else:
                    sc = jnp.where(upper, NEG, s)
                if with_diag:
                    sd = jnp.where(eye, s_p, NEG)
                    m = jnp.max(jnp.maximum(sc, sd), axis=-1, keepdims=True)
                else:
                    m = jnp.max(sc, axis=-1, keepdims=True)
                if with_sinks:
                    sk = jnp.where(first_rows, sink_ref[0, heads[0]], sink_ref[0, heads[1]])
                    m = jnp.maximum(m, sk)
                p = jnp.exp(sc - m)
                zero = jnp.zeros_like(p)
                if with_diag:
                    pd = jnp.exp(sd - m)
                    l = jnp.sum(p + pd, axis=-1, keepdims=True)
                else:
                    pd = zero
                    l = jnp.sum(p, axis=-1, keepdims=True)
                if with_sinks:
                    l = l + jnp.exp(sk - m)
                pf = jnp.where(upper, zero, p)
                if has_prev:
                    pf = jnp.concatenate([jnp.where(upper, p, pd), pf], axis=1)
                probs.append(pf.astype(BF16))
                for n, h in enumerate(heads):
                    rows = slice(n * BLOCK, (n + 1) * BLOCK)
                    ml_blk = jnp.where(lane == h, m[rows], ml_blk)
                    ml_blk = jnp.where(lane == h + STAT_SHIFT, l[rows], ml_blk)
            return probs, ml_blk

        def matmuls_out(g, values, probs, ml_blk):
            for c in chunks:
                sl = slice(c * LANES, (c + 1) * LANES)
                _blk_store(o_ref, sl, _unstack_heads(_dot(probs[c], values[c]), c, gqa), dil, g)
            _blk_store(ml_ref, all_lanes, ml_blk, dil, g)

        def run(blocks):
            ins = [matmuls_in(g, has_prev) for g, has_prev in blocks]
            mids = [tile_ops(has_prev, scores) for (_, has_prev), (scores, _) in zip(blocks, ins)]
            for (g, _), (_, values), (probs, ml_blk) in zip(blocks, ins, mids):
                matmuls_out(g, values, probs, ml_blk)

        second = [(1, True)] if per_step > 1 else []

        @pl.when(pl.program_id(1) == 0)
        def _():
            run([(0, False)] + second)

        @pl.when(pl.program_id(1) > 0)
        def _():
            run([(0, True)] + second)

        if per_step > 2:
            def rest(pair, carry):
                run([(2 * pair, True), (2 * pair + 1, True)])
                return carry

            lax.fori_loop(1, per_step // 2, rest, 0)

    spec = _attn_specs(dil)
    cur = spec(per_step, lambda i: i)
    prev = spec(1, lambda i: jnp.maximum(i * per_step - 1, 0))
    in_specs = [cur(WIDTH), prev(2 * kw), cur(2 * kw)]
    args = [q, kv, kv]
    if with_sinks:
        in_specs.append(pl.BlockSpec(memory_space=pltpu.SMEM))
        args.append(sinks)
    stats = jax.ShapeDtypeStruct(q.shape[:-1] + (LANES,), F32)
    o, ml = pl.pallas_call(
        body, name=name, grid=(n_seq, nb // per_step), in_specs=in_specs,
        out_specs=(cur(WIDTH), cur(LANES)),
        out_shape=(jax.ShapeDtypeStruct(q.shape, BF16), stats),
        compiler_params=pltpu.CompilerParams(dimension_semantics=("arbitrary", "arbitrary"),
                                             vmem_limit_bytes=VMEM_LIMIT),
    )(*args)
    return _attn_unview(o, dil), _attn_unview(ml, dil)


def _attn_bwd(q, kv, do, ld, *, dil, max_dist, name):
    q, kv, do, ld = (_attn_view(a, dil) for a in (q, kv, do, ld))
    kw = kv.shape[-1] // 2
    gqa = kw == A_KV_WIDTH
    n_seq = dil
    nb = (q.shape[-2] * (4 if dil == 4 else 1)) // BLOCK
    n_kc = kw // LANES
    per_step = min(ATTN_BLOCKS_PER_STEP, nb)
    all_lanes = slice(0, LANES)
    assert max_dist in (BLOCK - 1, BLOCK) and nb % per_step == 0 and (per_step == 1 or per_step % 2 == 0)
    diag = max_dist == BLOCK

    def body(q_ref, kvp_ref, kvc_ref, do_ref, ld_ref, dq_ref, dkv_ref, ck_ref, cv_ref):
        i = pl.program_id(1)

        chunks = range(WIDTH // LANES)

        def matmuls_in(g, has_prev):
            prev_ref, prev_g = (kvp_ref, 0) if (isinstance(g, int) and g == 0) else (kvc_ref, g - 1)
            operands, products = [], []
            for c in chunks:
                sl = slice(c * LANES, (c + 1) * LANES)
                kc = 0 if gqa else c
                ksl = slice(kc * LANES, (kc + 1) * LANES)
                vsl = slice(ksl.start + kw, ksl.stop + kw)
                k2, v2 = _blk_load(kvc_ref, ksl, dil, g), _blk_load(kvc_ref, vsl, dil, g)
                if has_prev:
                    k2 = jnp.concatenate([_blk_load(prev_ref, ksl, dil, prev_g), k2], axis=0)
                    v2 = jnp.concatenate([_blk_load(prev_ref, vsl, dil, prev_g), v2], axis=0)
                qs = _stack_heads(_blk_load(q_ref, sl, dil, g), c, gqa)
                dos = _stack_heads(_blk_load(do_ref, sl, dil, g), c, gqa)
                operands.append((qs, dos, k2))
                products.append((_dot_nt(qs, k2), _dot_nt(dos, v2)))
            return operands, products

        def tile_ops(g, has_prev, products):
            upper, eye = _upper_mask(dil), _eye_mask()
            ld_blk = _blk_load(ld_ref, all_lanes, dil, g)
            weights = []
            for c in chunks:
                heads = _stacked_head_ids(c, gqa)
                lse2 = _per_head_rows(ld_blk, heads)
                dl2 = _per_head_rows(ld_blk, tuple(h + STAT_SHIFT for h in heads))
                s, dp = products[c]
                if has_prev:
                    s_p, dp_p = s[:, :BLOCK], dp[:, :BLOCK]
                    sc = jnp.where(upper, s_p, s[:, BLOCK:])
                    dpc = jnp.where(upper, dp_p, dp[:, BLOCK:])
                else:
                    sc = jnp.where(upper, NEG, s)
                    dpc = dp
                p = jnp.exp(sc - lse2)
                ds = p * (dpc - dl2)
                zero = jnp.zeros_like(p)
                pf = jnp.where(upper, zero, p)
                dsf = jnp.where(upper, zero, ds)
                if has_prev:
                    if diag:
                        pd = jnp.exp(jnp.where(eye, s_p, NEG) - lse2)
                        dsd = pd * (dp_p - dl2)
                    else:
                        pd = dsd = zero
                    pf = jnp.concatenate([jnp.where(upper, p, pd), pf], axis=1)
                    dsf = jnp.concatenate([jnp.where(upper, ds, dsd), dsf], axis=1)
                weights.append((pf.astype(BF16), dsf.astype(BF16)))
            return weights

        def matmuls_out(g, has_prev, operands, weights):
            seq_blk = i * per_step + g
            dk_acc = [None] * n_kc
            dv_acc = [None] * n_kc
            for c in chunks:
                sl = slice(c * LANES, (c + 1) * LANES)
                kc = 0 if gqa else c
                qs, dos, k2 = operands[c]
                pf, dsf = weights[c]
                _blk_store(dq_ref, sl, _unstack_heads(_dot(dsf, k2), c, gqa) * SCALE, dil, g)
                dk2 = _dot_tn(dsf, qs)
                dv2 = _dot_tn(pf, dos)
                dk_acc[kc] = dk2 if dk_acc[kc] is None else dk_acc[kc] + dk2
                dv_acc[kc] = dv2 if dv_acc[kc] is None else dv_acc[kc] + dv2
            for kc in range(n_kc):
                sl = slice(kc * LANES, (kc + 1) * LANES)
                vsl = slice(sl.start + kw, sl.stop + kw)
                if has_prev:
                    _blk_store(dkv_ref, sl, ck_ref[:, sl] + dk_acc[kc][:BLOCK], dil, seq_blk - 1)
                    _blk_store(dkv_ref, vsl, cv_ref[:, sl] + dv_acc[kc][:BLOCK], dil, seq_blk - 1)
                    ck_ref[:, sl] = dk_acc[kc][BLOCK:]
                    cv_ref[:, sl] = dv_acc[kc][BLOCK:]
                else:
                    ck_ref[:, sl] = dk_acc[kc]
                    cv_ref[:, sl] = dv_acc[kc]

        def run(blocks):
            ins = [matmuls_in(g, has_prev) for g, has_prev in blocks]
            mids = [tile_ops(g, has_prev, products) for (g, has_prev), (_, products) in zip(blocks, ins)]
            for (g, has_prev), (operands, _), weights in zip(blocks, ins, mids):
                matmuls_out(g, has_prev, operands, weights)

        second = [(1, True)] if per_step > 1 else []

        @pl.when(i == 0)
        def _():
            run([(0, False)] + second)

        @pl.when(i > 0)
        def _():
            run([(0, True)] + second)

        if per_step > 2:
            def rest(pair, carry):
                run([(2 * pair, True), (2 * pair + 1, True)])
                return carry

            lax.fori_loop(1, per_step // 2, rest, 0)

        @pl.when(i == nb // per_step - 1)
        def _():
            for kc in range(n_kc):
                sl = slice(kc * LANES, (kc + 1) * LANES)
                _blk_store(dkv_ref, sl, ck_ref[:, sl], dil, nb - 1)
                _blk_store(dkv_ref, slice(sl.start + kw, sl.stop + kw), cv_ref[:, sl], dil, nb - 1)

    spec = _attn_specs(dil)
    cur = spec(per_step, lambda i: i)
    prev = spec(1, lambda i: jnp.maximum(i * per_step - 1, 0))
    if dil == 4:
        whole = pl.BlockSpec((4, None, kv.shape[2], 2 * kw), lambda r, i: (0, r, 0, 0))
    else:
        whole = pl.BlockSpec((None, kv.shape[1], 2 * kw), lambda r, i: (r, 0, 0))
    sds = jax.ShapeDtypeStruct
    dq, dkv = pl.pallas_call(
        body, name=name, grid=(n_seq, nb // per_step),
        in_specs=[cur(WIDTH), prev(2 * kw), cur(2 * kw), cur(WIDTH), cur(LANES)],
        out_specs=(cur(WIDTH), whole),
        out_shape=(sds(q.shape, BF16), sds(kv.shape, BF16)),
        scratch_shapes=[pltpu.VMEM((BLOCK, kw), F32), pltpu.VMEM((BLOCK, kw), F32)],
        compiler_params=pltpu.CompilerParams(dimension_semantics=("arbitrary", "arbitrary"),
                                             vmem_limit_bytes=VMEM_LIMIT),
    )(q, kv, kv, do, ld)
    return _attn_unview(dq, dil), _attn_unview(dkv, dil)


def _outproj(att_a, att_b1, att_b4, att_b16, g_a, g_b, x2, tgt2, w_out_bf, sink_row, perm):
    s = x2.shape[0]
    tm = ROW_TILE

    def body(oa_ref, mla_ref, ob1_ref, ml1_ref, ob4_ref, ml4_ref, ob16_ref, ml16_ref,
             ga_ref, gb_ref, x_ref, t_ref, w_ref, sink_ref, perm_ref,
             dy_ref, doa_ref, dob_ref, dobf_ref, dga_ref, dgb_ref, lda_ref, ldb_ref, ldbf_ref,
             gw_ref, loss_ref, dsink_ref, scr_st):
        i = pl.program_id(0)
        perm = perm_ref[...]
        lane = lax.broadcasted_iota(jnp.int32, (tm, LANES), 1)
        used = lane < HEADS

        def split(ml):
            return jnp.where(used, ml, 0.0), jnp.where(used, pltpu.roll(ml, LANES - STAT_SHIFT, 1), 1.0)

        @pl.when(i == 0)
        def _():
            gw_ref[...] = jnp.zeros_like(gw_ref)
            loss_ref[...] = jnp.zeros_like(loss_ref)
            dsink_ref[...] = jnp.zeros_like(dsink_ref)

        ms, ls = zip(split(ml1_ref[...]), split(_load_folded(ml4_ref, scr_st)), split(_load_folded(ml16_ref, scr_st)))
        mx = jnp.maximum(jnp.maximum(ms[0], ms[1]), ms[2])
        scale = [jnp.exp(mp - mx) for mp in ms]
        den = (ls[0] * scale[0] + ls[1] * scale[1]) + ls[2] * scale[2]
        lse_b = jnp.where(used, mx + jnp.log(den), 0.0)
        inv_den = 1.0 / den
        o_b = _expand_heads(scale[0] * inv_den) * ob1_ref[...].astype(F32)
        o_b = o_b + _expand_heads(scale[1] * inv_den) * _load_folded_bf16(ob4_ref, perm)
        o_b = o_b + _expand_heads(scale[2] * inv_den) * _load_folded_bf16(ob16_ref, perm)
        m_a, l_a = split(mla_ref[...])
        lse_a = jnp.where(used, m_a + jnp.log(l_a), 0.0)
        o_a = _expand_heads(1.0 / l_a) * oa_ref[...].astype(F32)
        g_a = ga_ref[...].astype(F32)
        g_b = gb_ref[...].astype(F32)
        sg_a = _sigmoid(g_a)
        sg_b = _sigmoid(g_b)
        silu_a = g_a * sg_a
        silu_b = g_b * sg_b
        mixed = jnp.concatenate([o_a * silu_a, o_b * silu_b], axis=1).astype(BF16)
        w = w_ref[...]
        y = x_ref[...] + _dot(mixed, w)
        diff = y - t_ref[...]
        loss_ref[...] += (0.5 / D_MODEL) * jnp.sum(diff * diff)
        dy = diff * (1.0 / D_MODEL)
        dy_ref[...] = dy
        dyb = dy.astype(BF16)
        gw_ref[...] += _dot_tn(mixed, dyb)
        dmixed = _dot_nt(dyb, w)
        dm_a = dmixed[:, :WIDTH]
        dm_b = dmixed[:, WIDTH:]
        do_a = dm_a * silu_a
        do_b = dm_b * silu_b
        doa_ref[...] = do_a.astype(BF16)
        dob_ref[...] = do_b.astype(BF16)
        _store_folded_bf16(dobf_ref, do_b, perm)
        dga_ref[...] = (dm_a * o_a * (sg_a * (1.0 + g_a * (1.0 - sg_a)))).astype(BF16)
        dgb_ref[...] = (dm_b * o_b * (sg_b * (1.0 + g_b * (1.0 - sg_b)))).astype(BF16)
        dl_a = _reduce_heads(do_a * o_a)
        dl_b = _reduce_heads(do_b * o_b)
        lda_ref[...] = lse_a + pltpu.roll(dl_a, STAT_SHIFT, 1)
        ld_b = lse_b + pltpu.roll(dl_b, STAT_SHIFT, 1)
        ldb_ref[...] = ld_b
        _store_folded(ldbf_ref, ld_b, scr_st)
        dsink_ref[...] -= jnp.sum(jnp.exp(sink_ref[...] - lse_a) * dl_a, axis=0, keepdims=True)

    sds = jax.ShapeDtypeStruct
    ln = s // FOLD
    natural = [_rows(WIDTH), _rows(LANES)]
    folded = [_folded_rows(WIDTH), _folded_rows(LANES)]
    return pl.pallas_call(
        body, name="outproj_fwd_bwd", grid=(s // tm,),
        in_specs=natural + natural + folded + folded
                 + [_rows(WIDTH), _rows(WIDTH), _rows(D_MODEL), _rows(D_MODEL), _whole((D_MODEL, D_MODEL)),
                    _whole((1, LANES)), _whole(perm.shape)],
        out_specs=(_rows(D_MODEL), _rows(WIDTH), _rows(WIDTH), _folded_rows(WIDTH), _rows(WIDTH), _rows(WIDTH),
                   _rows(LANES), _rows(LANES), _folded_rows(LANES),
                   _whole((D_MODEL, D_MODEL)), _whole((1, LANES)), _whole((1, LANES))),
        out_shape=(sds((s, D_MODEL), F32), sds((s, WIDTH), BF16), sds((s, WIDTH), BF16),
                   sds((FOLD, ln, WIDTH), BF16), sds((s, WIDTH), BF16), sds((s, WIDTH), BF16),
                   sds((s, LANES), F32), sds((s, LANES), F32), sds((FOLD, ln, LANES), F32),
                   sds((D_MODEL, D_MODEL), F32), sds((1, LANES), F32), sds((1, LANES), F32)),
        scratch_shapes=[_fold_scratch(LANES)],
        compiler_params=pltpu.CompilerParams(dimension_semantics=("arbitrary",), vmem_limit_bytes=VMEM_LIMIT),
    )(*att_a, *att_b1, *att_b4, *att_b16, g_a, g_b, x2, tgt2, w_out_bf, sink_row, perm)


def _inproj_bwd(x2, dy, gain, w_bf, cos, sin_s, gqa, gka, gqb, gkb, bd256, bd128, perm,
                qa_raw, ka_raw, qb_raw, kb_raw, d_a, d_b1, d_b4, d_b16, dg_a, dg_b):
    s = x2.shape[0]
    tm = ROW_TILE

    def body(x_ref, dy_ref, gain_ref, w_hbm, cos_ref, sin_ref, gqa_ref, gka_ref, gqb_ref, gkb_ref, bd256_ref,
             bd128_ref, perm_ref, qa_raw_ref, ka_raw_ref, qb_raw_ref, kb_raw_ref, dqa_ref, dkva_ref,
             dq1_ref, dkv1_ref, dq4_ref, dkv4_ref, dq16_ref, dkv16_ref, dga_ref, dgb_ref,
             gx_ref, ht_ref, win_ref,
             dgain_ref, dgqa_ref, dgka_ref, dgqb_ref, dgkb_ref, w_vmem, dproj_ref):
        i = pl.program_id(0)
        perm = perm_ref[...]

        @pl.when(i == 0)
        def _():
            pltpu.sync_copy(w_hbm, w_vmem)
            dgain_ref[...] = jnp.zeros_like(dgain_ref)
            dgqa_ref[...] = jnp.zeros_like(dgqa_ref)
            dgka_ref[...] = jnp.zeros_like(dgka_ref)
            dgqb_ref[...] = jnp.zeros_like(dgqb_ref)
            dgkb_ref[...] = jnp.zeros_like(dgkb_ref)

        cos1 = cos_ref[...]
        sin1 = sin_ref[...]
        cos4 = jnp.tile(cos1, (1, 4))
        sin4 = jnp.tile(sin1, (1, 4))

        dt, dg = _qknorm_rope_bwd(dqa_ref[...], qa_raw_ref[...], gqa_ref[...], cos4, sin4, bd256_ref[...])
        dproj_ref[:, C_QA:C_KA] = dt.astype(BF16)
        dgqa_ref[...] += jnp.sum(dg, axis=0, keepdims=True)
        dt, dg = _qknorm_rope_bwd(dkva_ref[:, :A_KV_WIDTH], ka_raw_ref[...], gka_ref[...], cos1, sin1,
                                  bd128_ref[...])
        dproj_ref[:, C_KA:C_VA] = dt.astype(BF16)
        dgka_ref[...] += jnp.sum(dg, axis=0, keepdims=True)
        dproj_ref[:, C_VA:C_GA] = dkva_ref[:, A_KV_WIDTH:]
        dproj_ref[:, C_GA:C_QB] = dga_ref[...]
        dq = (dq1_ref[...].astype(F32) + _load_folded_bf16(dq4_ref, perm)) + _load_folded_bf16(dq16_ref, perm)
        dt, dg = _qknorm_rope_bwd(dq, qb_raw_ref[...], gqb_ref[...], cos4, sin4, bd256_ref[...])
        dproj_ref[:, C_QB:C_KB] = dt.astype(BF16)
        dgqb_ref[...] += jnp.sum(dg, axis=0, keepdims=True)
        dkv = (dkv1_ref[...].astype(F32) + _load_folded_bf16(dkv4_ref, perm)) + _load_folded_bf16(dkv16_ref, perm)
        dt, dg = _qknorm_rope_bwd(dkv[:, :WIDTH], kb_raw_ref[...], gkb_ref[...], cos4, sin4, bd256_ref[...])
        dproj_ref[:, C_KB:C_VB] = dt.astype(BF16)
        dgkb_ref[...] += jnp.sum(dg, axis=0, keepdims=True)
        dproj_ref[:, C_VB:C_GB] = dkv[:, WIDTH:].astype(BF16)
        dproj_ref[:, C_GB:C_END] = dgb_ref[...]
        for k, start in enumerate(WIN_START):
            win_ref[k] = dproj_ref[:, start:start + WIN]

        xt = x_ref[...]
        gain_row = gain_ref[...]
        r = lax.rsqrt(jnp.mean(xt * xt, axis=-1, keepdims=True) + EPS)
        xr = xt * r
        ht_ref[...] = (xr * gain_row).T.astype(BF16)
        dh = _dot(dproj_ref[...], w_vmem[...])
        dgain_ref[...] += jnp.sum(dh * xr, axis=0, keepdims=True)
        u = dh * gain_row
        gx_ref[...] = dy_ref[...] + r * (u - xr * jnp.mean(u * xr, axis=-1, keepdims=True))

    def acc_row(w):
        return pl.BlockSpec((1, w), lambda i: (0, 0))

    sds = jax.ShapeDtypeStruct
    any_spec = pl.BlockSpec(memory_space=pl.ANY)
    win_spec = pl.BlockSpec((N_CHIP, tm, WIN), lambda i: (0, i, 0))
    return pl.pallas_call(
        body, name="inproj_bwd", grid=(s // tm,),
        in_specs=[_rows(D_MODEL), _rows(D_MODEL), _whole(gain.shape), any_spec, _rows(LANES), _rows(LANES),
                  _whole(gqa.shape), _whole(gka.shape), _whole(gqb.shape), _whole(gkb.shape), _whole(bd256.shape),
                  _whole(bd128.shape), _whole(perm.shape),
                  _rows(WIDTH), _rows(A_KV_WIDTH), _rows(WIDTH), _rows(WIDTH),
                  _rows(WIDTH), _rows(2 * A_KV_WIDTH), _rows(WIDTH), _rows(2 * WIDTH)]
                 + [_folded_rows(WIDTH), _folded_rows(2 * WIDTH)] * 2 + [_rows(WIDTH), _rows(WIDTH)],
        out_specs=(_rows(D_MODEL), pl.BlockSpec((D_MODEL, tm), lambda i: (0, i)), win_spec, acc_row(D_MODEL), acc_row(WIDTH), acc_row(A_KV_WIDTH), acc_row(WIDTH), acc_row(WIDTH)),
        out_shape=(sds((s, D_MODEL), F32), sds((D_MODEL, s), BF16), sds((N_CHIP, s, WIN), BF16),
                   sds((1, D_MODEL), F32),
                   sds((1, WIDTH), F32), sds((1, A_KV_WIDTH), F32), sds((1, WIDTH), F32), sds((1, WIDTH), F32)),
        scratch_shapes=[pltpu.VMEM((IN_WIDTH, D_MODEL), BF16), pltpu.VMEM((tm, IN_WIDTH), BF16)],
        compiler_params=pltpu.CompilerParams(dimension_semantics=("arbitrary",), vmem_limit_bytes=VMEM_LIMIT),
    )(x2, dy, gain, w_bf, cos, sin_s, gqa, gka, gqb, gkb, bd256, bd128, perm, qa_raw, ka_raw, qb_raw, kb_raw,
      *d_a, *d_b1, *d_b4, *d_b16, dg_a, dg_b)


def _rope_tables(s):
    half = HEAD_DIM // 2
    inv = jnp.tile(ROPE_THETA ** (-jnp.arange(half, dtype=F32) / half), 4)
    sign = jnp.tile(jnp.concatenate([-jnp.ones((half,), F32), jnp.ones((half,), F32)]), 2)
    hi = (jnp.arange(s // ROPE_SPLIT) * ROPE_SPLIT).astype(F32)[:, None] * inv[None, :]
    lo = jnp.arange(ROPE_SPLIT).astype(F32)[:, None] * inv[None, :]
    ch, sh, cl, sl = jnp.cos(hi)[:, None, :], jnp.sin(hi)[:, None, :], jnp.cos(lo)[None], jnp.sin(lo)[None]
    cos = (ch * cl - sh * sl).reshape(s, LANES)
    sin = (sh * cl + ch * sl).reshape(s, LANES)
    return cos, sin * sign[None, :]


def _block_diag_ones(w):
    idx = jnp.arange(w) // HEAD_DIM
    return (idx[:, None] == idx[None, :]).astype(BF16)


def _local_step(x2, tgt2, norm_gain, w_in_bf, q_norm_a, k_norm_a, sinks_a, q_norm_b, k_norm_b, w_out_blk):
    s = x2.shape[0]
    cos, sin_s = _rope_tables(s)
    bd256, bd128 = _block_diag_ones(2 * LANES), _block_diag_ones(A_KV_WIDTH)
    gqa = jnp.tile(q_norm_a, (1, HEADS))
    gka = jnp.tile(k_norm_a, (1, 2))
    gqb = jnp.tile(q_norm_b, (1, HEADS))
    gkb = jnp.tile(k_norm_b, (1, HEADS))
    sink_row = jnp.pad(sinks_a, ((0, 0), (0, LANES - HEADS)))
    perm = _fold_matrix()

    (qa, kva, qb, kvb, qbf, kvbf, qa_raw, ka_raw, g_a, qb_raw, kb_raw, g_b, w_out_bf) = _inproj(
        x2, norm_gain, w_in_bf, cos, sin_s, gqa, gka, gqb, gkb, bd256, bd128, w_out_blk)

    att_a = _attn_fwd(qa, kva, sinks_a, dil=1, max_dist=A_MAX_DIST, name="attn_a_fwd")
    att_b1 = _attn_fwd(qb, kvb, None, dil=1, max_dist=B_MAX_DIST, name="attn_b1_fwd")
    att_b4 = _attn_fwd(qbf, kvbf, None, dil=4, max_dist=B_MAX_DIST, name="attn_b4_fwd")
    att_b16 = _attn_fwd(qbf, kvbf, None, dil=16, max_dist=B_MAX_DIST, name="attn_b16_fwd")

    (dy, do_a, do_b, do_bf, dg_a, dg_b, ld_a, ld_b, ld_bf, gw_out, loss_part, dsink) = _outproj(
        att_a, att_b1, att_b4, att_b16, g_a, g_b, x2, tgt2, w_out_bf, sink_row, perm)

    d_a = _attn_bwd(qa, kva, do_a, ld_a, dil=1, max_dist=A_MAX_DIST, name="attn_a_bwd")
    d_b1 = _attn_bwd(qb, kvb, do_b, ld_b, dil=1, max_dist=B_MAX_DIST, name="attn_b1_bwd")
    d_b4 = _attn_bwd(qbf, kvbf, do_bf, ld_bf, dil=4, max_dist=B_MAX_DIST, name="attn_b4_bwd")
    d_b16 = _attn_bwd(qbf, kvbf, do_bf, ld_bf, dil=16, max_dist=B_MAX_DIST, name="attn_b16_bwd")

    gx, h_t, wins, dgain, dgqa, dgka, dgqb, dgkb = _inproj_bwd(
        x2, dy, norm_gain, w_in_bf, cos, sin_s, gqa, gka, gqb, gkb, bd256, bd128, perm,
        qa_raw, ka_raw, qb_raw, kb_raw, d_a, d_b1, d_b4, d_b16, dg_a, dg_b)
    return loss_part, gx, h_t, wins, gw_out, (dgain, dgqa, dgka, dsink, dgqb, dgkb)


def _position():
    return lax.axis_index("x"), lax.axis_index("y"), lax.axis_index("c")


GATHER_CHUNKS = 2


def _gather_weights(blocks, name):
    n = len(blocks)
    ch = GATHER_CHUNKS

    def body(*refs):
        src_refs, dst_refs = refs[:n], refs[n:2 * n]
        ici_send, ici_recv, hop_send, hop_recv, d2d_send, d2d_recv = refs[2 * n:]
        x, y, c = _position()
        b = 2 * x + y
        via = 2 - c
        out = 3 - via
        for k in range(n):
            dst_refs[k][b] = src_refs[k][...].astype(BF16)

        def rows(k, core, j):
            half = blocks[k].shape[0] // 2
            return pl.ds(pl.multiple_of(core * half + j * (half // ch), half // ch), half // ch)

        def chip(rel):
            return x ^ (rel >> 1), y ^ (rel & 1)

        def ici(k, j, slot, rel, send_sems, recv_sems, sem):
            px, py = chip(rel)
            piece = dst_refs[k].at[slot, rows(k, c, j)]
            return pltpu.make_async_remote_copy(src_ref=piece, dst_ref=piece, send_sem=send_sems.at[sem],
                                                recv_sem=recv_sems.at[sem], device_id=(px, py, c),
                                                device_id_type=MESH)

        def direct(k, j, slot, rel):
            return ici(k, j, slot, rel, ici_send, ici_recv, ((rel - 1) * ch + j) * n + k)

        def hop(k, j, slot, rel):
            return ici(k, j, slot, rel, hop_send, hop_recv, j * n + k)

        def d2d(k, j, rel, core):
            piece = dst_refs[k].at[b ^ rel, rows(k, core, j)]
            sem = ((rel - 1) * ch + j) * n + k
            return pltpu.make_async_remote_copy(src_ref=piece, dst_ref=piece, send_sem=d2d_send.at[sem],
                                                recv_sem=d2d_recv.at[sem], device_id=(x, y, 1 - c),
                                                device_id_type=MESH)

        pieces = [(k, j) for j in range(ch) for k in range(n)]
        for k, j in pieces:
            for rel in (1, 2):
                direct(k, j, b, rel).start()
        for k, j in pieces:
            direct(k, j, b ^ via, via).wait_recv()
            hop(k, j, b ^ via, out).start()
            d2d(k, j, via, c).start()
        for k, j in pieces:
            direct(k, j, b ^ out, out).wait_recv()
            d2d(k, j, out, c).start()
        for k, j in pieces:
            hop(k, j, b ^ 3, via).wait_recv()
            d2d(k, j, 3, c).start()
        for k, j in pieces:
            for rel in (1, 2, 3):
                d2d(k, j, rel, 1 - c).wait_recv()
        for k, j in pieces:
            for rel in (1, 2):
                direct(k, j, b, rel).wait_send()
            hop(k, j, b ^ via, out).wait_send()
            d2d(k, j, via, c).wait_send()
            d2d(k, j, out, c).wait_send()
            d2d(k, j, 3, c).wait_send()

    vmem_spec = pl.BlockSpec(memory_space=pltpu.VMEM)
    dma = pltpu.SemaphoreType.DMA
    out_shape = tuple(jax.ShapeDtypeStruct((N_CHIP,) + a.shape, BF16) for a in blocks)
    return pl.pallas_call(
        body, name=name, in_specs=[vmem_spec] * n, out_specs=tuple([vmem_spec] * n), out_shape=out_shape,
        scratch_shapes=[dma((2 * ch * n,)), dma((2 * ch * n,)), dma((ch * n,)), dma((ch * n,)),
                        dma((3 * ch * n,)), dma((3 * ch * n,))],
        compiler_params=pltpu.CompilerParams(vmem_limit_bytes=VMEM_LIMIT),
    )(*blocks)


def _grad_reduce(order, h_t, wins, gw_out, small):
    s = h_t.shape[1]
    tk = GRAD_ROWS
    n_i = s // tk
    half = D_MODEL // 2
    o_half = OUT_ROWS // 2
    n_rel = N_CHIP - 1

    def body(order_ref, ht_ref, win_ref, gwo_ref, small_ref,
             win_out, wout_out, small_out,
             acc, mine, s1, r1, s2, r2, so1, ro1, so2, ro2, pair_in, pair_o, small_land,
             s1_send, s1_recv, s2_send, s2_recv, o1_send, o1_recv, o2_send, o2_recv,
             pair_send, pair_recv, small_send, small_recv):
        j = pl.program_id(0)
        i = pl.program_id(1)
        x, y, c = _position()
        me = 4 * x + 2 * y + c
        sibling = (x, y, 1 - c)
        my_rows = pl.ds(pl.multiple_of(c * half, half), half)
        sib_rows = pl.ds(pl.multiple_of((1 - c) * half, half), half)

        def chip_of(rel):
            return x ^ (rel >> 1), y ^ (rel & 1)

        def level1(k):
            return pltpu.make_async_remote_copy(src_ref=s1.at[k], dst_ref=r1.at[k], send_sem=s1_send.at[k],
                                                recv_sem=s1_recv.at[k], device_id=sibling, device_id_type=MESH)

        def level2(k):
            px, py = chip_of(RELATIONS[k])
            return pltpu.make_async_remote_copy(src_ref=s2.at[k], dst_ref=r2.at[k], send_sem=s2_send.at[k],
                                                recv_sem=s2_recv.at[k], device_id=(px, py, c), device_id_type=MESH)

        def out_level1(bk):
            return pltpu.make_async_remote_copy(src_ref=so1.at[bk], dst_ref=ro1.at[bk], send_sem=o1_send.at[bk],
                                                recv_sem=o1_recv.at[bk], device_id=sibling, device_id_type=MESH)

        def out_level2(k):
            px, py = chip_of(RELATIONS[k])
            return pltpu.make_async_remote_copy(src_ref=so2.at[k], dst_ref=ro2.at[k], send_sem=o2_send.at[k],
                                                recv_sem=o2_recv.at[k], device_id=(px, py, c), device_id_type=MESH)

        def small_copy(d):
            px, py, pc = x ^ (d >> 2), y ^ ((d >> 1) & 1), c ^ (d & 1)
            return pltpu.make_async_remote_copy(src_ref=small_ref, dst_ref=small_land.at[me],
                                                send_sem=small_send.at[d], recv_sem=small_recv.at[d],
                                                device_id=(px, py, pc), device_id_type=MESH)

        def pair_copy(k, buf):
            return pltpu.make_async_remote_copy(src_ref=buf.at[0], dst_ref=buf.at[1], send_sem=pair_send.at[k],
                                                recv_sem=pair_recv.at[k], device_id=sibling, device_id_type=MESH)

        def out_rows(bk, core):
            return pl.ds(pl.multiple_of(bk * OUT_ROWS + core * o_half, o_half), o_half)

        @pl.when((j == 0) & (i == 0))
        def _():
            for d in range(1, N_DEV):
                small_copy(d).start()
            small_land[me] = small_ref[...]
            for bk in range(N_CHIP):
                so1[bk] = gwo_ref[out_rows(bk, 1 - c), :].astype(BF16)
                out_level1(bk).start()

        @pl.when((j == 0) & (i == 1))
        def _():
            b = 2 * x + y
            for bk in range(N_CHIP):
                out_level1(bk).wait_recv()
            for k in range(n_rel):
                px, py = chip_of(RELATIONS[k])
                bk = 2 * px + py
                so2[k] = (gwo_ref[out_rows(bk, c), :] + ro1[bk].astype(F32)).astype(BF16)
                out_level2(k).start()

        @pl.when(i == 0)
        def _():
            acc[...] = jnp.zeros_like(acc)

        for n0 in range(0, WIN, ACC_COLS):
            n1 = min(n0 + ACC_COLS, WIN)
            acc[:, n0:n1] += _dot(ht_ref[...], win_ref[:, n0:n1])

        for k in range(N_CHIP):
            @pl.when((j == k) & (i == n_i - 1))
            def _(k=k):
                s1[k] = acc[sib_rows, :].astype(BF16)
                level1(k).start()
                mine[...] = acc[my_rows, :]

            if k < n_rel:
                @pl.when((j == k + 1) & (i == 1))
                def _(k=k):
                    level1(k).wait_recv()
                    s2[k] = (mine[...] + r1[k].astype(F32)).astype(BF16)
                    level2(k).start()

        @pl.when((j == N_CHIP - 1) & (i == n_i - 1))
        def _():
            b = 2 * x + y
            level1(N_CHIP - 1).wait_recv()
            total = mine[...] + r1[N_CHIP - 1].astype(F32)
            for k in range(n_rel):
                level2(k).wait_recv()
                total = total + r2[k].astype(F32)
            total = total.T
            pair_in[0] = total
            pair_copy(0, pair_in).start()
            total_o = gwo_ref[out_rows(b, c), :] + ro1[b].astype(F32)
            for k in range(n_rel):
                out_level2(k).wait_recv()
                total_o = total_o + ro2[k].astype(F32)
            pair_o[0] = total_o
            pair_copy(1, pair_o).start()
            for core in range(2):
                @pl.when(c == core)
                def _(core=core):
                    win_out[:, core * half:(core + 1) * half] = total
            wout_out[c] = total_o
            for d in range(1, N_DEV):
                small_copy(d).wait_recv()
            small_out[...] = small_land[...]
            pair_copy(0, pair_in).wait_recv()
            for core in range(2):
                @pl.when(c == core)
                def _(core=core):
                    win_out[:, (1 - core) * half:(2 - core) * half] = pair_in[1]
            pair_copy(1, pair_o).wait_recv()
            wout_out[1 - c] = pair_o[1]
            for d in range(1, N_DEV):
                small_copy(d).wait_send()
            for k in range(N_CHIP):
                level1(k).wait_send()
                out_level1(k).wait_send()
            for k in range(n_rel):
                level2(k).wait_send()
                out_level2(k).wait_send()
            pair_copy(0, pair_in).wait_send()
            pair_copy(1, pair_o).wait_send()

    vmem = pl.BlockSpec(memory_space=pltpu.VMEM)
    dma = pltpu.SemaphoreType.DMA
    sds = jax.ShapeDtypeStruct
    grid_spec = pltpu.PrefetchScalarGridSpec(
        num_scalar_prefetch=1, grid=(N_CHIP, n_i),
        in_specs=[pl.BlockSpec((D_MODEL, tk), lambda j, i, order: (0, i)),
                  pl.BlockSpec((None, tk, WIN), lambda j, i, order: (order[j], i, 0)), vmem, vmem],
        out_specs=(vmem, vmem, vmem),
        scratch_shapes=[
            pltpu.VMEM((D_MODEL, WIN), F32), pltpu.VMEM((half, WIN), F32),
            pltpu.VMEM((N_CHIP, half, WIN), BF16), pltpu.VMEM((N_CHIP, half, WIN), BF16),
            pltpu.VMEM((n_rel, half, WIN), BF16), pltpu.VMEM((n_rel, half, WIN), BF16),
            pltpu.VMEM((N_CHIP, o_half, D_MODEL), BF16), pltpu.VMEM((N_CHIP, o_half, D_MODEL), BF16),
            pltpu.VMEM((n_rel, o_half, D_MODEL), BF16), pltpu.VMEM((n_rel, o_half, D_MODEL), BF16),
            pltpu.VMEM((2, WIN, half), F32), pltpu.VMEM((2, o_half, D_MODEL), F32),
            pltpu.VMEM((N_DEV, PACK_ROWS, D_MODEL), F32),
            dma((N_CHIP,)), dma((N_CHIP,)), dma((n_rel,)), dma((n_rel,)),
            dma((N_CHIP,)), dma((N_CHIP,)), dma((n_rel,)), dma((n_rel,)),
            dma((2,)), dma((2,)), dma((N_DEV,)), dma((N_DEV,))])
    return pl.pallas_call(
        body, name="grad_w_in_reduce", grid_spec=grid_spec,
        out_shape=(sds((WIN, D_MODEL), F32), sds((2, o_half, D_MODEL), F32), sds((N_DEV, PACK_ROWS, D_MODEL), F32)),
        compiler_params=pltpu.CompilerParams(dimension_semantics=("arbitrary", "arbitrary"),
                                             vmem_limit_bytes=VMEM_LIMIT),
    )(order, h_t, wins, gw_out, small)


ADAM_STEPS = 4


def _adamw_math(w, g, m, v):
    m = ADAM_B1 * m + (1.0 - ADAM_B1) * g
    v = ADAM_B2 * v + (1.0 - ADAM_B2) * (g * g)
    m_hat = m / (1.0 - ADAM_B1 ** ADAM_STEP)
    v_hat = v / (1.0 - ADAM_B2 ** ADAM_STEP)
    delta = -ADAM_LR * (m_hat / (jnp.sqrt(v_hat) + ADAM_EPS) + ADAM_WD * w)
    return delta, m, v


def _adamw(w, g, m, v, name):
    r, c = w.shape

    def body(w_ref, g_ref, m_ref, v_ref, d_ref, nm_ref, nv_ref):
        delta, nm, nv = _adamw_math(w_ref[...], g_ref[...], m_ref[...], v_ref[...])
        d_ref[...] = delta
        nm_ref[...] = nm
        nv_ref[...] = nv

    rows = r // ADAM_STEPS
    assert rows * ADAM_STEPS == r and rows % 8 == 0
    spec = pl.BlockSpec((rows, c), lambda i: (i, 0))
    shape = jax.ShapeDtypeStruct((r, c), F32)
    return pl.pallas_call(
        body, name=name, grid=(ADAM_STEPS,), in_specs=[spec] * 4, out_specs=(spec,) * 3,
        out_shape=(shape,) * 3, compiler_params=pltpu.CompilerParams(vmem_limit_bytes=VMEM_LIMIT),
    )(w, g, m, v)


def _adamw_window(w, window, shift, m, v, name):
    r, c = w.shape
    rows = r // ADAM_STEPS
    assert rows * ADAM_STEPS == r and rows % 8 == 0

    def body(shift_ref, w_ref, win_hbm, m_ref, v_ref, g_ref, d_ref, nm_ref, nv_ref, g_vmem):
        start = pl.multiple_of(shift_ref[0] + pl.program_id(0) * rows, 8)
        pltpu.sync_copy(win_hbm.at[pl.ds(start, rows)], g_vmem)
        g = g_vmem[...]
        g_ref[...] = g
        delta, nm, nv = _adamw_math(w_ref[...], g, m_ref[...], v_ref[...])
        d_ref[...] = delta
        nm_ref[...] = nm
        nv_ref[...] = nv

    spec = pl.BlockSpec((rows, c), lambda i, shift_ref: (i, 0))
    shape = jax.ShapeDtypeStruct((r, c), F32)
    grid_spec = pltpu.PrefetchScalarGridSpec(
        num_scalar_prefetch=1, grid=(ADAM_STEPS,),
        in_specs=[spec, pl.BlockSpec(memory_space=pl.ANY), spec, spec], out_specs=(spec,) * 4,
        scratch_shapes=[pltpu.VMEM((rows, c), F32)])
    return pl.pallas_call(
        body, name=name, grid_spec=grid_spec, out_shape=(shape,) * 4,
        compiler_params=pltpu.CompilerParams(vmem_limit_bytes=VMEM_LIMIT),
    )(shift, w, window, m, v)


PACK_ROWS = 8


def _fold_heads(v):
    y = v[:, 0:LANES]
    for j in range(1, v.shape[1] // LANES):
        y = y + v[:, j * LANES:(j + 1) * LANES]
    return y + pltpu.roll(y, HEAD_DIM, 1)


N_SMALL = 6


def _small_adamw(recv, weights, m, v):
    def body(*refs):
        r_ref = refs[0]
        w_refs, m_refs, v_refs = (refs[1 + n * N_SMALL:1 + (n + 1) * N_SMALL] for n in range(3))
        outs = refs[1 + 3 * N_SMALL:]
        g_refs, d_refs, nm_refs, nv_refs = (outs[n * N_SMALL:(n + 1) * N_SMALL] for n in range(4))
        loss_ref = outs[4 * N_SMALL]
        tot = r_ref[0]
        for j in range(1, N_DEV):
            tot = tot + r_ref[j]
        loss_ref[...] = tot[3:4, 0:LANES]
        row1 = tot[1:2, :]
        row2 = tot[2:3, :]
        grads = [tot[0:1, :],
                 _fold_heads(row1[:, 0:WIDTH])[:, :HEAD_DIM],
                 _fold_heads(row2[:, WIDTH:WIDTH + A_KV_WIDTH])[:, :HEAD_DIM],
                 row2[:, WIDTH + A_KV_WIDTH:WIDTH + A_KV_WIDTH + HEADS],
                 _fold_heads(row1[:, WIDTH:2 * WIDTH])[:, :HEAD_DIM],
                 _fold_heads(row2[:, 0:WIDTH])[:, :HEAD_DIM]]
        for n, g in enumerate(grads):
            g_refs[n][...] = g
            delta, nm, nv = _adamw_math(w_refs[n][...], g, m_refs[n][...], v_refs[n][...])
            d_refs[n][...] = delta
            nm_refs[n][...] = nm
            nv_refs[n][...] = nv

    shapes = tuple(jax.ShapeDtypeStruct(a.shape, F32) for a in weights)
    outs = pl.pallas_call(body, name="small_adamw", out_shape=shapes * 4 + (jax.ShapeDtypeStruct((1, LANES), F32),)
                          )(recv, *weights, *m, *v)
    return tuple(outs[n * N_SMALL:(n + 1) * N_SMALL] for n in range(4)) + (outs[4 * N_SMALL],)


def kernel(x, norm_gain, w_in, q_norm_a, k_norm_a, sinks_a, q_norm_b, k_norm_b, w_out, loss_target, m_norm_gain, m_w_in, m_q_norm_a, m_k_norm_a, m_sinks_a, m_q_norm_b, m_k_norm_b, m_w_out, v_norm_gain, v_w_in, v_q_norm_a, v_k_norm_a, v_sinks_a, v_q_norm_b, v_k_norm_b, v_w_out):
    chip = 2 * lax.axis_index("x") + lax.axis_index("y")

    w_in_t, m_w_in_t, v_w_in_t = w_in[0].T, m_w_in[0].T, v_w_in[0].T

    (w_in_all,) = _gather_weights([w_in_t], "gather_weights")
    w_in_bf = w_in_all.reshape(IN_WIDTH, D_MODEL)

    loss_part, gx, h_t, wins, gw_out, (dgain, dgqa, dgka, dsink, dgqb, dgkb) = _local_step(
        x[0], loss_target[0], norm_gain, w_in_bf, q_norm_a, k_norm_a, sinks_a, q_norm_b, k_norm_b, w_out[0])

    small = jnp.concatenate([
        dgain, jnp.concatenate([dgqa, dgqb], axis=1),
        jnp.concatenate([dgkb, dgka, dsink, jnp.zeros((1, D_MODEL - WIDTH - 2 * A_KV_WIDTH), F32)], axis=1),
        jnp.pad(loss_part, ((0, 0), (0, D_MODEL - LANES))),
        jnp.zeros((PACK_ROWS - 4, D_MODEL), F32)], axis=0)
    order = (chip ^ jnp.array(RELATIONS, jnp.int32)).astype(jnp.int32)
    win_sum, wout_sum, small_recv = _grad_reduce(order, h_t, wins, gw_out, small)
    shift = jnp.array(WIN_SHIFT, jnp.int32)[chip].reshape(1)
    g_w_out = wout_sum.reshape(OUT_ROWS, D_MODEL)

    g_w_in, d_w_in, nm_w_in, nv_w_in = (
        a.T for a in _adamw_window(w_in_t, win_sum, shift, m_w_in_t, v_w_in_t, "adamw_w_in"))
    d_w_out, nm_w_out, nv_w_out = _adamw(w_out[0], g_w_out, m_w_out[0], v_w_out[0], "adamw_w_out")
    g_s, d_s, nm_s, nv_s, loss_row = _small_adamw(
        small_recv,
        (norm_gain, q_norm_a, k_norm_a, sinks_a, q_norm_b, k_norm_b),
        (m_norm_gain, m_q_norm_a, m_k_norm_a, m_sinks_a, m_q_norm_b, m_k_norm_b),
        (v_norm_gain, v_q_norm_a, v_k_norm_a, v_sinks_a, v_q_norm_b, v_k_norm_b))
    loss = loss_row[0, 0]

    def leaves(small_ones, big_in, big_out):
        return (small_ones[0], big_in[None]) + tuple(small_ones[1:]) + (big_out[None],)

    return ((loss, gx[None]) + leaves(g_s, g_w_in, g_w_out) + leaves(d_s, d_w_in, d_w_out)
            + leaves(nm_s, nm_w_in, nm_w_out) + leaves(nv_s, nv_w_in, nv_w_out))
```

```python
import jax
import jax.numpy as jnp
from jax import lax
from jax.experimental import pallas as pl
from jax.experimental.pallas import tpu as pltpu

F32 = jnp.float32
BF16 = jnp.bfloat16

D_MODEL = 1024
HEAD_DIM = 64
HEADS = 8
WIDTH = HEADS * HEAD_DIM
A_KV_WIDTH = 2 * HEAD_DIM
BLOCK = 128
LANES = 128
FOLD = 16
A_MAX_DIST = 127
B_MAX_DIST = 128
ROPE_THETA = 10000.0
ROPE_SPLIT = 64
EPS = 1e-6
NEG = -1e30
SCALE = HEAD_DIM ** -0.5

IN_WIDTH = 3328
C_QA, C_KA, C_VA, C_GA, C_QB, C_KB, C_VB, C_GB, C_END = 0, 512, 640, 768, 1280, 1792, 2304, 2816, 3328

N_DEV = 8
N_CHIP = 4
MESH = pl.DeviceIdType.MESH
IN_COLS = IN_WIDTH // N_CHIP
WIN = 896
WIN_START = (0, 768, 1664, 2432)
WIN_SHIFT = (0, 64, 0, 64)
OUT_ROWS = D_MODEL // N_CHIP
RELATIONS = (3, 1, 2, 0)

ADAM_LR = 0.001
ADAM_B1 = 0.9
ADAM_B2 = 0.999
ADAM_EPS = 1e-08
ADAM_WD = 0.01
ADAM_STEP = 10

ROW_TILE = 256
FOLD_ROWS = ROW_TILE // FOLD
GRAD_ROWS = 1024
ACC_COLS = 256
VMEM_LIMIT = 56 * 1024 * 1024


def _dot(a, b):
    return jnp.dot(a, b, preferred_element_type=F32)


def _dot_nt(a, b):
    return lax.dot_general(a, b, (((1,), (1,)), ((), ())), preferred_element_type=F32)


def _dot_tn(a, b):
    return lax.dot_general(a, b, (((0,), (0,)), ((), ())), preferred_element_type=F32)


def _head_sum(z, bd):
    w = bd.shape[0]
    zb = z.astype(BF16)
    parts = [_dot(zb[:, a:a + w], bd) for a in range(0, z.shape[1], w)]
    return parts[0] if len(parts) == 1 else jnp.concatenate(parts, axis=1)


def _swap_halves(t):
    w = t.shape[1]
    lane = lax.broadcasted_iota(jnp.int32, t.shape, 1)
    return jnp.where(lane % HEAD_DIM < HEAD_DIM // 2, pltpu.roll(t, w - 32, 1), pltpu.roll(t, 32, 1))


def _qknorm_rope(t, g, cos, sin_s, bd):
    r = lax.rsqrt(_head_sum(t * t, bd) * (1.0 / HEAD_DIM) + EPS)
    n = (t * r) * g
    return n * cos + _swap_halves(n) * sin_s


def _qknorm_rope_bwd(dout, t, g, cos, sin_s, bd):
    dout, t = dout.astype(F32), t.astype(F32)
    dn = dout * cos + _swap_halves(dout * sin_s)
    r = lax.rsqrt(_head_sum(t * t, bd) * (1.0 / HEAD_DIM) + EPS)
    tr = t * r
    u = dn * g
    dt = r * (u - tr * (_head_sum(u * tr, bd) * (1.0 / HEAD_DIM)))
    return dt, dn * tr


def _sigmoid(g):
    return 1.0 / (1.0 + jnp.exp(-g))


def _expand_heads(st):
    t = st.shape[0]
    lane = lax.broadcasted_iota(jnp.int32, (t, LANES), 1)
    chunks = []
    for c in range(WIDTH // LANES):
        chunks.append(jnp.where(lane < HEAD_DIM, st[:, 2 * c:2 * c + 1], st[:, 2 * c + 1:2 * c + 2]))
    return jnp.concatenate(chunks, axis=1)


def _reduce_heads(z):
    t = z.shape[0]
    lane = lax.broadcasted_iota(jnp.int32, (t, LANES), 1)
    out = jnp.zeros((t, LANES), F32)
    for c in range(WIDTH // LANES):
        zc = z[:, c * LANES:(c + 1) * LANES]
        for ph in range(2):
            s = jnp.sum(jnp.where((lane // HEAD_DIM) == ph, zc, 0.0), axis=-1, keepdims=True)
            out = jnp.where(lane == 2 * c + ph, s, out)
    return out


def _fold_scratch(w):
    return pltpu.VMEM((w // LANES, ROW_TILE, LANES), F32)


def _store_folded(out_ref, val, scr, col0=0):
    w = val.shape[1]
    n = w // LANES
    for c in range(n):
        scr[c] = val[:, c * LANES:(c + 1) * LANES]
    for r in range(FOLD):
        piece = [scr[c, pl.ds(r, FOLD_ROWS, stride=FOLD), :] for c in range(n)]
        out_ref[r, :, col0:col0 + w] = (piece[0] if n == 1 else jnp.concatenate(piece, axis=1)).astype(out_ref.dtype)


def _load_folded(in_ref, scr):
    n = in_ref.shape[2] // LANES
    for r in range(FOLD):
        blk = in_ref[r].astype(F32)
        for c in range(n):
            scr[c, pl.ds(r, FOLD_ROWS, stride=FOLD), :] = blk[:, c * LANES:(c + 1) * LANES]
    return scr[0] if n == 1 else jnp.concatenate([scr[c] for c in range(n)], axis=1)


def _fold_matrix():
    f = jnp.arange(ROW_TILE)
    return (jnp.arange(ROW_TILE)[None, :] == (FOLD * (f % FOLD_ROWS) + f // FOLD_ROWS)[:, None]).astype(BF16)


def _store_folded_bf16(out_ref, val, perm):
    folded = _dot(perm, val.astype(BF16)).astype(out_ref.dtype)
    for r in range(FOLD):
        out_ref[r] = folded[r * FOLD_ROWS:(r + 1) * FOLD_ROWS]


def _load_folded_bf16(in_ref, perm):
    blk = jnp.concatenate([in_ref[r] for r in range(FOLD)], axis=0)
    return _dot(perm, blk)


def _rows(w, tm=ROW_TILE):
    return pl.BlockSpec((tm, w), lambda i: (i, 0))


def _folded_rows(w):
    return pl.BlockSpec((FOLD, FOLD_ROWS, w), lambda i: (0, i, 0))


def _whole(shape):
    return pl.BlockSpec(shape, lambda i: (0,) * len(shape))


def _inproj(x2, gain, w_bf, cos, sin_s, gqa, gka, gqb, gkb, bd256, bd128, w_out_blk):
    s = x2.shape[0]
    tm = ROW_TILE
    n_steps = s // tm
    n_rel = N_CHIP - 1
    o_half = OUT_ROWS // 2

    def body(x_ref, gain_ref, w_hbm, cos_ref, sin_ref, gqa_ref, gka_ref, gqb_ref, gkb_ref, bd256_ref, bd128_ref,
             wout_ref, qa_ref, kva_ref, qb_ref, kvb_ref, qbf_ref, kvbf_ref,
             qa_raw_ref, ka_raw_ref, ga_ref, qb_raw_ref, kb_raw_ref, gb_ref, wout_all_ref,
             w_vmem, scr, land, ici_send, ici_recv, d2d_send, d2d_recv):
        i = pl.program_id(0)
        px_, py_, c = _position()
        b = 2 * px_ + py_

        def piece(chip_idx, core):
            return land.at[chip_idx, pl.ds(pl.multiple_of(core * o_half, o_half), o_half)]

        def other_chip(d):
            ox, oy = px_ ^ (d >> 1), py_ ^ (d & 1)
            return ox, oy, 2 * ox + oy

        def ici_copy(d, chip_idx):
            ox, oy, _ = other_chip(d)
            return pltpu.make_async_remote_copy(
                src_ref=piece(chip_idx, c), dst_ref=piece(chip_idx, c), send_sem=ici_send.at[d - 1],
                recv_sem=ici_recv.at[d - 1], device_id=(ox, oy, c), device_id_type=MESH)

        def d2d_copy(d, core):
            return pltpu.make_async_remote_copy(
                src_ref=piece(other_chip(d)[2], core), dst_ref=piece(other_chip(d)[2], core),
                send_sem=d2d_send.at[d - 1], recv_sem=d2d_recv.at[d - 1], device_id=(px_, py_, 1 - c),
                device_id_type=MESH)

        @pl.when(i == 0)
        def _():
            pltpu.sync_copy(w_hbm, w_vmem)
            land[b] = wout_ref[...].astype(BF16)
            for d in range(1, N_CHIP):
                ici_copy(d, b).start()

        @pl.when(i == n_steps // 2)
        def _():
            for d in range(1, N_CHIP):
                ici_copy(d, other_chip(d)[2]).wait_recv()
                d2d_copy(d, c).start()

        @pl.when(i == n_steps - 1)
        def _():
            for d in range(1, N_CHIP):
                d2d_copy(d, 1 - c).wait_recv()
            for d in range(1, N_CHIP):
                ici_copy(d, b).wait_send()
                d2d_copy(d, c).wait_send()
            for k in range(N_CHIP):
                wout_all_ref[k * OUT_ROWS:(k + 1) * OUT_ROWS, :] = land[k]

        xt = x_ref[...]
        r = lax.rsqrt(jnp.mean(xt * xt, axis=-1, keepdims=True) + EPS)
        h = ((xt * r) * gain_ref[...]).astype(BF16)
        cos1 = cos_ref[...]
        sin1 = sin_ref[...]
        cos4 = jnp.tile(cos1, (1, 4))
        sin4 = jnp.tile(sin1, (1, 4))

        def seg(a, b):
            return _dot_nt(h, w_vmem[a:b, :])

        t = seg(C_QA, C_KA)
        qa_raw_ref[...] = t.astype(BF16)
        qa_ref[...] = (_qknorm_rope(t, gqa_ref[...], cos4, sin4, bd256_ref[...]) * SCALE).astype(BF16)
        t = seg(C_KA, C_VA)
        ka_raw_ref[...] = t.astype(BF16)
        kva_ref[:, :A_KV_WIDTH] = _qknorm_rope(t, gka_ref[...], cos1, sin1, bd128_ref[...]).astype(BF16)
        kva_ref[:, A_KV_WIDTH:] = seg(C_VA, C_GA).astype(BF16)
        ga_ref[...] = seg(C_GA, C_QB).astype(BF16)
        t = seg(C_QB, C_KB)
        qb_raw_ref[...] = t.astype(BF16)
        t = _qknorm_rope(t, gqb_ref[...], cos4, sin4, bd256_ref[...]) * SCALE
        qb_ref[...] = t.astype(BF16)
        _store_folded(qbf_ref, t, scr)
        t = seg(C_KB, C_VB)
        kb_raw_ref[...] = t.astype(BF16)
        t = _qknorm_rope(t, gkb_ref[...], cos4, sin4, bd256_ref[...])
        kvb_ref[:, :WIDTH] = t.astype(BF16)
        _store_folded(kvbf_ref, t, scr)
        t = seg(C_VB, C_GB)
        kvb_ref[:, WIDTH:] = t.astype(BF16)
        _store_folded(kvbf_ref, t, scr, WIDTH)
        gb_ref[...] = seg(C_GB, C_END).astype(BF16)

    sds = jax.ShapeDtypeStruct
    ln = s // FOLD
    out_shape = (sds((s, WIDTH), BF16), sds((s, 2 * A_KV_WIDTH), BF16), sds((s, WIDTH), BF16),
                 sds((s, 2 * WIDTH), BF16), sds((FOLD, ln, WIDTH), BF16), sds((FOLD, ln, 2 * WIDTH), BF16),
                 sds((s, WIDTH), BF16), sds((s, A_KV_WIDTH), BF16), sds((s, WIDTH), BF16),
                 sds((s, WIDTH), BF16), sds((s, WIDTH), BF16), sds((s, WIDTH), BF16),
                 sds((D_MODEL, D_MODEL), BF16))
    out_specs = (_rows(WIDTH), _rows(2 * A_KV_WIDTH), _rows(WIDTH), _rows(2 * WIDTH),
                 _folded_rows(WIDTH), _folded_rows(2 * WIDTH),
                 _rows(WIDTH), _rows(A_KV_WIDTH), _rows(WIDTH), _rows(WIDTH), _rows(WIDTH), _rows(WIDTH),
                 _whole((D_MODEL, D_MODEL)))
    dma = pltpu.SemaphoreType.DMA
    return pl.pallas_call(
        body, name="inproj_fwd", grid=(n_steps,),
        in_specs=[_rows(D_MODEL), _whole(gain.shape), pl.BlockSpec(memory_space=pl.ANY), _rows(LANES), _rows(LANES),
                  _whole(gqa.shape), _whole(gka.shape), _whole(gqb.shape), _whole(gkb.shape), _whole(bd256.shape),
                  _whole(bd128.shape), _whole(w_out_blk.shape)],
        out_specs=out_specs, out_shape=out_shape,
        scratch_shapes=[pltpu.VMEM((IN_WIDTH, D_MODEL), BF16), _fold_scratch(WIDTH),
                        pltpu.VMEM((N_CHIP, OUT_ROWS, D_MODEL), BF16),
                        dma((n_rel,)), dma((n_rel,)), dma((n_rel,)), dma((n_rel,))],
        compiler_params=pltpu.CompilerParams(dimension_semantics=("arbitrary",), vmem_limit_bytes=VMEM_LIMIT),
    )(x2, gain, w_bf, cos, sin_s, gqa, gka, gqb, gkb, bd256, bd128, w_out_blk)


def _seq_pos(idx, dil):
    if dil == 4:
        return 4 * (idx % 32) + idx // 32
    return idx


SOFTMAX_ROWS = 64


def _upper_mask(dil, r0=0, rows=2 * BLOCK):
    qi = (lax.broadcasted_iota(jnp.int32, (rows, BLOCK), 0) + r0) % BLOCK
    kj = lax.broadcasted_iota(jnp.int32, (rows, BLOCK), 1)
    return _seq_pos(kj, dil) > _seq_pos(qi, dil)


def _eye_mask(r0=0, rows=2 * BLOCK):
    qi = (lax.broadcasted_iota(jnp.int32, (rows, BLOCK), 0) + r0) % BLOCK
    kj = lax.broadcasted_iota(jnp.int32, (rows, BLOCK), 1)
    return qi == kj


def _stack_heads(a2, c, gqa):
    lane = lax.broadcasted_iota(jnp.int32, (1, LANES), 1) // HEAD_DIM
    zero = jnp.zeros_like(a2)
    if gqa:
        keep = lane == (c // 2)
        return jnp.concatenate([jnp.where(keep, a2, zero), jnp.where(keep, _swap_heads(a2), zero)], axis=0)
    return jnp.concatenate([jnp.where(lane == 0, a2, zero), jnp.where(lane == 1, a2, zero)], axis=0)


def _unstack_heads(a, c, gqa):
    lane = lax.broadcasted_iota(jnp.int32, (1, LANES), 1) // HEAD_DIM
    if gqa:
        return jnp.where(lane == (c // 2), a[:BLOCK], _swap_heads(a[BLOCK:]))
    return jnp.where(lane == 0, a[:BLOCK], a[BLOCK:])


def _stacked_head_ids(c, gqa):
    if gqa:
        return 2 * c + c // 2, 2 * c + 1 - c // 2
    return 2 * c, 2 * c + 1


def _per_head_rows(blk, heads):
    return jnp.concatenate([blk[:, heads[0]:heads[0] + 1], blk[:, heads[1]:heads[1] + 1]], axis=0)


def _attn_view(a, dil):
    if dil == 1:
        return a[None]
    if dil == 4:
        return a.reshape(4, 4, a.shape[1], a.shape[2])
    return a


def _attn_unview(a, dil):
    if dil == 1:
        return a[0]
    if dil == 4:
        return a.reshape(FOLD, a.shape[2], a.shape[3])
    return a


ATTN_BLOCKS_PER_STEP = 8


def _attn_specs(dil):
    if dil == 4:
        def spec(n, fn):
            return lambda w: pl.BlockSpec((4, None, n * BLOCK // 4, w), lambda r, i: (0, r, fn(i), 0))
    else:
        def spec(n, fn):
            return lambda w: pl.BlockSpec((None, n * BLOCK, w), lambda r, i: (r, fn(i), 0))
    return spec


def _blk_rows(g, dil):
    n = BLOCK // 4 if dil == 4 else BLOCK
    if isinstance(g, int):
        return slice(g * n, (g + 1) * n)
    return pl.ds(pl.multiple_of(g * n, n), n)


def _blk_load(ref, sl, dil, g=0):
    if dil == 4:
        return ref[:, _blk_rows(g, dil), sl].reshape(BLOCK, sl.stop - sl.start)
    return ref[_blk_rows(g, dil), sl]


def _blk_store(ref, sl, val, dil, g=0):
    val = val.astype(ref.dtype)
    if dil == 4:
        ref[:, _blk_rows(g, dil), sl] = val.reshape(4, BLOCK // 4, sl.stop - sl.start)
    else:
        ref[_blk_rows(g, dil), sl] = val


def _swap_heads(a):
    return pltpu.roll(a.astype(F32), HEAD_DIM, 1).astype(a.dtype)


STAT_SHIFT = 8


def _attn_fwd(q, kv, sinks, *, dil, max_dist, name):
    q, kv = _attn_view(q, dil), _attn_view(kv, dil)
    kw = kv.shape[-1] // 2
    gqa = kw == A_KV_WIDTH
    n_seq = dil
    nb = (q.shape[-2] * (4 if dil == 4 else 1)) // BLOCK
    per_step = min(ATTN_BLOCKS_PER_STEP, nb)
    with_sinks = sinks is not None
    all_lanes = slice(0, LANES)
    assert max_dist in (BLOCK - 1, BLOCK) and nb % per_step == 0 and (per_step == 1 or per_step % 2 == 0)
    diag = max_dist == BLOCK

    def body(*refs):
        if with_sinks:
            q_ref, kvp_ref, kvc_ref, sink_ref, o_ref, ml_ref = refs
        else:
            q_ref, kvp_ref, kvc_ref, o_ref, ml_ref = refs

        chunks = range(WIDTH // LANES)

        def matmuls_in(g, has_prev):
            prev_ref, prev_g = (kvp_ref, 0) if (isinstance(g, int) and g == 0) else (kvc_ref, g - 1)
            scores, values = [], []
            for c in chunks:
                sl = slice(c * LANES, (c + 1) * LANES)
                ksl = slice(0, LANES) if gqa else sl
                vsl = slice(ksl.start + kw, ksl.stop + kw)
                kcur, vcur = _blk_load(kvc_ref, ksl, dil, g), _blk_load(kvc_ref, vsl, dil, g)
                qs = _stack_heads(_blk_load(q_ref, sl, dil, g), c, gqa)
                if has_prev:
                    kcur = jnp.concatenate([_blk_load(prev_ref, ksl, dil, prev_g), kcur], axis=0)
                    vcur = jnp.concatenate([_blk_load(prev_ref, vsl, dil, prev_g), vcur], axis=0)
                scores.append(_dot_nt(qs, kcur))
                values.append(vcur)
            return scores, values

        def tile_ops(has_prev, scores):
            lane = lax.broadcasted_iota(jnp.int32, (1, LANES), 1)
            with_diag = diag and has_prev
            upper, eye = _upper_mask(dil), _eye_mask()
            first_rows = lax.broadcasted_iota(jnp.int32, (2 * BLOCK, 1), 0) < BLOCK
            ml_blk = jnp.zeros((BLOCK, LANES), F32)
            probs = []
            for c in chunks:
                heads = _stacked_head_ids(c, gqa)
                s = scores[c]
                if has_prev:
                    s_p = s[:, :BLOCK]
                    sc = jnp.where(upper, s_p, s[:, BLOCK:])
                else:
                    sc = jnp.where(upper, NEG, s)
                if with_diag:
                    sd = jnp.where(eye, s_p, NEG)
                    m = jnp.max(jnp.maximum(sc, sd), axis=-1, keepdims=True)
                else:
                    m = jnp.max(sc, axis=-1, keepdims=True)
                if with_sinks:
                    sk = jnp.where(first_rows, sink_ref[0, heads[0]], sink_ref[0, heads[1]])
                    m = jnp.maximum(m, sk)
                p = jnp.exp(sc - m)
                zero = jnp.zeros_like(p)
                if with_diag:
                    pd = jnp.exp(sd - m)
                    l = jnp.sum(p + pd, axis=-1, keepdims=True)
                else:
                    pd = zero
                    l = jnp.sum(p, axis=-1, keepdims=True)
                if with_sinks:
                    l = l + jnp.exp(sk - m)
                pf = jnp.where(upper, zero, p)
                if has_prev:
                    pf = jnp.concatenate([jnp.where(upper, p, pd), pf], axis=1)
                probs.append(pf.astype(BF16))
                for n, h in enumerate(heads):
                    rows = slice(n * BLOCK, (n + 1) * BLOCK)
                    ml_blk = jnp.where(lane == h, m[rows], ml_blk)
                    ml_blk = jnp.where(lane == h + STAT_SHIFT, l[rows], ml_blk)
            return probs, ml_blk

        def matmuls_out(g, values, probs, ml_blk):
            for c in chunks:
                sl = slice(c * LANES, (c + 1) * LANES)
                _blk_store(o_ref, sl, _unstack_heads(_dot(probs[c], values[c]), c, gqa), dil, g)
            _blk_store(ml_ref, all_lanes, ml_blk, dil, g)

        def run(blocks):
            ins = [matmuls_in(g, has_prev) for g, has_prev in blocks]
            mids = [tile_ops(has_prev, scores) for (_, has_prev), (scores, _) in zip(blocks, ins)]
            for (g, _), (_, values), (probs, ml_blk) in zip(blocks, ins, mids):
                matmuls_out(g, values, probs, ml_blk)

        second = [(1, True)] if per_step > 1 else []

        @pl.when(pl.program_id(1) == 0)
        def _():
            run([(0, False)] + second)

        @pl.when(pl.program_id(1) > 0)
        def _():
            run([(0, True)] + second)

        if per_step > 2:
            def rest(pair, carry):
                run([(2 * pair, True), (2 * pair + 1, True)])
                return carry

            lax.fori_loop(1, per_step // 2, rest, 0)

    spec = _attn_specs(dil)
    cur = spec(per_step, lambda i: i)
    prev = spec(1, lambda i: jnp.maximum(i * per_step - 1, 0))
    in_specs = [cur(WIDTH), prev(2 * kw), cur(2 * kw)]
    args = [q, kv, kv]
    if with_sinks:
        in_specs.append(pl.BlockSpec(memory_space=pltpu.SMEM))
        args.append(sinks)
    stats = jax.ShapeDtypeStruct(q.shape[:-1] + (LANES,), F32)
    o, ml = pl.pallas_call(
        body, name=name, grid=(n_seq, nb // per_step), in_specs=in_specs,
        out_specs=(cur(WIDTH), cur(LANES)),
        out_shape=(jax.ShapeDtypeStruct(q.shape, BF16), stats),
        compiler_params=pltpu.CompilerParams(dimension_semantics=("arbitrary", "arbitrary"),
                                             vmem_limit_bytes=VMEM_LIMIT),
    )(*args)
    return _attn_unview(o, dil), _attn_unview(ml, dil)


def _attn_bwd(q, kv, do, ld, *, dil, max_dist, name, onto=None):
    q, kv, do, ld = (_attn_view(a, dil) for a in (q, kv, do, ld))
    onto = () if onto is None else tuple(_attn_view(a, dil) for a in onto)
    kw = kv.shape[-1] // 2
    gqa = kw == A_KV_WIDTH
    n_seq = dil
    nb = (q.shape[-2] * (4 if dil == 4 else 1)) // BLOCK
    n_kc = kw // LANES
    per_step = min(ATTN_BLOCKS_PER_STEP, nb)
    all_lanes = slice(0, LANES)
    assert max_dist in (BLOCK - 1, BLOCK) and nb % per_step == 0 and (per_step == 1 or per_step % 2 == 0)
    diag = max_dist == BLOCK

    def body(q_ref, kvp_ref, kvc_ref, do_ref, ld_ref, *rest_refs):
        dq_ref, dkv_ref, ck_ref, cv_ref = rest_refs[len(onto):]
        i = pl.program_id(1)

        def store(ref, sl, val, blk):
            if onto:
                val = val + _blk_load(rest_refs[0 if ref is dq_ref else 1], sl, dil, blk).astype(F32)
            _blk_store(ref, sl, val, dil, blk)

        chunks = range(WIDTH // LANES)

        def matmuls_in(g, has_prev):
            prev_ref, prev_g = (kvp_ref, 0) if (isinstance(g, int) and g == 0) else (kvc_ref, g - 1)
            operands, products = [], []
            for c in chunks:
                sl = slice(c * LANES, (c + 1) * LANES)
                kc = 0 if gqa else c
                ksl = slice(kc * LANES, (kc + 1) * LANES)
                vsl = slice(ksl.start + kw, ksl.stop + kw)
                k2, v2 = _blk_load(kvc_ref, ksl, dil, g), _blk_load(kvc_ref, vsl, dil, g)
                if has_prev:
                    k2 = jnp.concatenate([_blk_load(prev_ref, ksl, dil, prev_g), k2], axis=0)
                    v2 = jnp.concatenate([_blk_load(prev_ref, vsl, dil, prev_g), v2], axis=0)
                qs = _stack_heads(_blk_load(q_ref, sl, dil, g), c, gqa)
                dos = _stack_heads(_blk_load(do_ref, sl, dil, g), c, gqa)
                operands.append((qs, dos, k2))
                products.append((_dot_nt(qs, k2), _dot_nt(dos, v2)))
            return operands, products

        def tile_ops(g, has_prev, products):
            upper, eye = _upper_mask(dil), _eye_mask()
            ld_blk = _blk_load(ld_ref, all_lanes, dil, g)
            weights = []
            for c in chunks:
                heads = _stacked_head_ids(c, gqa)
                lse2 = _per_head_rows(ld_blk, heads)
                dl2 = _per_head_rows(ld_blk, tuple(h + STAT_SHIFT for h in heads))
                s, dp = products[c]
                if has_prev:
                    s_p, dp_p = s[:, :BLOCK], dp[:, :BLOCK]
                    sc = jnp.where(upper, s_p, s[:, BLOCK:])
                    dpc = jnp.where(upper, dp_p, dp[:, BLOCK:])
                else:
                    sc = jnp.where(upper, NEG, s)
                    dpc = dp
                p = jnp.exp(sc - lse2)
                ds = p * (dpc - dl2)
                zero = jnp.zeros_like(p)
                pf = jnp.where(upper, zero, p)
                dsf = jnp.where(upper, zero, ds)
                if has_prev:
                    if diag:
                        pd = jnp.exp(jnp.where(eye, s_p, NEG) - lse2)
                        dsd = pd * (dp_p - dl2)
                    else:
                        pd = dsd = zero
                    pf = jnp.concatenate([jnp.where(upper, p, pd), pf], axis=1)
                    dsf = jnp.concatenate([jnp.where(upper, ds, dsd), dsf], axis=1)
                weights.append((pf.astype(BF16), dsf.astype(BF16)))
            return weights

        def matmuls_out(g, has_prev, operands, weights):
            seq_blk = i * per_step + g
            dk_acc = [None] * n_kc
            dv_acc = [None] * n_kc
            for c in chunks:
                sl = slice(c * LANES, (c + 1) * LANES)
                kc = 0 if gqa else c
                qs, dos, k2 = operands[c]
                pf, dsf = weights[c]
                store(dq_ref, sl, _unstack_heads(_dot(dsf, k2), c, gqa) * SCALE, g)
                dk2 = _dot_tn(dsf, qs)
                dv2 = _dot_tn(pf, dos)
                dk_acc[kc] = dk2 if dk_acc[kc] is None else dk_acc[kc] + dk2
                dv_acc[kc] = dv2 if dv_acc[kc] is None else dv_acc[kc] + dv2
            for kc in range(n_kc):
                sl = slice(kc * LANES, (kc + 1) * LANES)
                vsl = slice(sl.start + kw, sl.stop + kw)
                if has_prev:
                    store(dkv_ref, sl, ck_ref[:, sl] + dk_acc[kc][:BLOCK], seq_blk - 1)
                    store(dkv_ref, vsl, cv_ref[:, sl] + dv_acc[kc][:BLOCK], seq_blk - 1)
                    ck_ref[:, sl] = dk_acc[kc][BLOCK:]
                    cv_ref[:, sl] = dv_acc[kc][BLOCK:]
                else:
                    ck_ref[:, sl] = dk_acc[kc]
                    cv_ref[:, sl] = dv_acc[kc]

        def run(blocks):
            ins = [matmuls_in(g, has_prev) for g, has_prev in blocks]
            mids = [tile_ops(g, has_prev, products) for (g, has_prev), (_, products) in zip(blocks, ins)]
            for (g, has_prev), (operands, _), weights in zip(blocks, ins, mids):
                matmuls_out(g, has_prev, operands, weights)

        second = [(1, True)] if per_step > 1 else []

        @pl.when(i == 0)
        def _():
            run([(0, False)] + second)

        @pl.when(i > 0)
        def _():
            run([(0, True)] + second)

        if per_step > 2:
            def rest(pair, carry):
                run([(2 * pair, True), (2 * pair + 1, True)])
                return carry

            lax.fori_loop(1, per_step // 2, rest, 0)

        @pl.when(i == nb // per_step - 1)
        def _():
            for kc in range(n_kc):
                sl = slice(kc * LANES, (kc + 1) * LANES)
                store(dkv_ref, sl, ck_ref[:, sl], nb - 1)
                store(dkv_ref, slice(sl.start + kw, sl.stop + kw), cv_ref[:, sl], nb - 1)

    spec = _attn_specs(dil)
    cur = spec(per_step, lambda i: i)
    prev = spec(1, lambda i: jnp.maximum(i * per_step - 1, 0))
    if dil == 4:
        whole = pl.BlockSpec((4, None, kv.shape[2], 2 * kw), lambda r, i: (0, r, 0, 0))
    else:
        whole = pl.BlockSpec((None, kv.shape[1], 2 * kw), lambda r, i: (r, 0, 0))
    sds = jax.ShapeDtypeStruct
    dq, dkv = pl.pallas_call(
        body, name=name, grid=(n_seq, nb // per_step),
        in_specs=[cur(WIDTH), prev(2 * kw), cur(2 * kw), cur(WIDTH), cur(LANES)] + [cur(WIDTH), whole][:len(onto)],
        out_specs=(cur(WIDTH), whole),
        out_shape=(sds(q.shape, BF16), sds(kv.shape, BF16)),
        scratch_shapes=[pltpu.VMEM((BLOCK, kw), F32), pltpu.VMEM((BLOCK, kw), F32)],
        compiler_params=pltpu.CompilerParams(dimension_semantics=("arbitrary", "arbitrary"),
                                             vmem_limit_bytes=VMEM_LIMIT),
    )(q, kv, kv, do, ld, *onto)
    return _attn_unview(dq, dil), _attn_unview(dkv, dil)


def _outproj(att_a, att_b1, att_b4, att_b16, g_a, g_b, x2, tgt2, w_out_bf, sink_row, perm):
    s = x2.shape[0]
    tm = ROW_TILE

    def body(oa_ref, mla_ref, ob1_ref, ml1_ref, ob4_ref, ml4_ref, ob16_ref, ml16_ref,
             ga_ref, gb_ref, x_ref, t_ref, w_ref, sink_ref, perm_ref,
             dy_ref, doa_ref, dob_ref, dobf_ref, dga_ref, dgb_ref, lda_ref, ldb_ref, ldbf_ref,
             gw_ref, loss_ref, dsink_ref, scr_st):
        i = pl.program_id(0)
        perm = perm_ref[...]
        lane = lax.broadcasted_iota(jnp.int32, (tm, LANES), 1)
        used = lane < HEADS

        def split(ml):
            return jnp.where(used, ml, 0.0), jnp.where(used, pltpu.roll(ml, LANES - STAT_SHIFT, 1), 1.0)

        @pl.when(i == 0)
        def _():
            gw_ref[...] = jnp.zeros_like(gw_ref)
            loss_ref[...] = jnp.zeros_like(loss_ref)
            dsink_ref[...] = jnp.zeros_like(dsink_ref)

        ms, ls = zip(split(ml1_ref[...]), split(_load_folded(ml4_ref, scr_st)), split(_load_folded(ml16_ref, scr_st)))
        mx = jnp.maximum(jnp.maximum(ms[0], ms[1]), ms[2])
        scale = [jnp.exp(mp - mx) for mp in ms]
        den = (ls[0] * scale[0] + ls[1] * scale[1]) + ls[2] * scale[2]
        lse_b = jnp.where(used, mx + jnp.log(den), 0.0)
        inv_den = 1.0 / den
        o_b = _expand_heads(scale[0] * inv_den) * ob1_ref[...].astype(F32)
        o_b = o_b + _expand_heads(scale[1] * inv_den) * _load_folded_bf16(ob4_ref, perm)
        o_b = o_b + _expand_heads(scale[2] * inv_den) * _load_folded_bf16(ob16_ref, perm)
        m_a, l_a = split(mla_ref[...])
        lse_a = jnp.where(used, m_a + jnp.log(l_a), 0.0)
        o_a = _expand_heads(1.0 / l_a) * oa_ref[...].astype(F32)
        g_a = ga_ref[...].astype(F32)
        g_b = gb_ref[...].astype(F32)
        sg_a = _sigmoid(g_a)
        sg_b = _sigmoid(g_b)
        silu_a = g_a * sg_a
        silu_b = g_b * sg_b
        mixed = jnp.concatenate([o_a * silu_a, o_b * silu_b], axis=1).astype(BF16)
        w = w_ref[...]
        y = x_ref[...] + _dot(mixed, w)
        diff = y - t_ref[...]
        loss_ref[...] += (0.5 / D_MODEL) * jnp.sum(diff * diff)
        dy = diff * (1.0 / D_MODEL)
        dy_ref[...] = dy
        dyb = dy.astype(BF16)
        gw_ref[...] += _dot_tn(mixed, dyb)
        dmixed = _dot_nt(dyb, w)
        dm_a = dmixed[:, :WIDTH]
        dm_b = dmixed[:, WIDTH:]
        do_a = dm_a * silu_a
        do_b = dm_b * silu_b
        doa_ref[...] = do_a.astype(BF16)
        dob_ref[...] = do_b.astype(BF16)
        _store_folded_bf16(dobf_ref, do_b, perm)
        dga_ref[...] = (dm_a * o_a * (sg_a * (1.0 + g_a * (1.0 - sg_a)))).astype(BF16)
        dgb_ref[...] = (dm_b * o_b * (sg_b * (1.0 + g_b * (1.0 - sg_b)))).astype(BF16)
        dl_a = _reduce_heads(do_a * o_a)
        dl_b = _reduce_heads(do_b * o_b)
        lda_ref[...] = lse_a + pltpu.roll(dl_a, STAT_SHIFT, 1)
        ld_b = lse_b + pltpu.roll(dl_b, STAT_SHIFT, 1)
        ldb_ref[...] = ld_b
        _store_folded(ldbf_ref, ld_b, scr_st)
        dsink_ref[...] -= jnp.sum(jnp.exp(sink_ref[...] - lse_a) * dl_a, axis=0, keepdims=True)

    sds = jax.ShapeDtypeStruct
    ln = s // FOLD
    natural = [_rows(WIDTH), _rows(LANES)]
    folded = [_folded_rows(WIDTH), _folded_rows(LANES)]
    return pl.pallas_call(
        body, name="outproj_fwd_bwd", grid=(s // tm,),
        in_specs=natural + natural + folded + folded
                 + [_rows(WIDTH), _rows(WIDTH), _rows(D_MODEL), _rows(D_MODEL), _whole((D_MODEL, D_MODEL)),
                    _whole((1, LANES)), _whole(perm.shape)],
        out_specs=(_rows(D_MODEL), _rows(WIDTH), _rows(WIDTH), _folded_rows(WIDTH), _rows(WIDTH), _rows(WIDTH),
                   _rows(LANES), _rows(LANES), _folded_rows(LANES),
                   _whole((D_MODEL, D_MODEL)), _whole((1, LANES)), _whole((1, LANES))),
        out_shape=(sds((s, D_MODEL), F32), sds((s, WIDTH), BF16), sds((s, WIDTH), BF16),
                   sds((FOLD, ln, WIDTH), BF16), sds((s, WIDTH), BF16), sds((s, WIDTH), BF16),
                   sds((s, LANES), F32), sds((s, LANES), F32), sds((FOLD, ln, LANES), F32),
                   sds((D_MODEL, D_MODEL), F32), sds((1, LANES), F32), sds((1, LANES), F32)),
        scratch_shapes=[_fold_scratch(LANES)],
        compiler_params=pltpu.CompilerParams(dimension_semantics=("arbitrary",), vmem_limit_bytes=VMEM_LIMIT),
    )(*att_a, *att_b1, *att_b4, *att_b16, g_a, g_b, x2, tgt2, w_out_bf, sink_row, perm)


def _inproj_bwd(x2, dy, gain, w_bf, cos, sin_s, gqa, gka, gqb, gkb, bd256, bd128, perm,
                qa_raw, ka_raw, qb_raw, kb_raw, d_a, d_b1, d_bf, dg_a, dg_b):
    s = x2.shape[0]
    tm = ROW_TILE

    def body(x_ref, dy_ref, gain_ref, w_hbm, cos_ref, sin_ref, gqa_ref, gka_ref, gqb_ref, gkb_ref, bd256_ref,
             bd128_ref, perm_ref, qa_raw_ref, ka_raw_ref, qb_raw_ref, kb_raw_ref, dqa_ref, dkva_ref,
             dq1_ref, dkv1_ref, dqf_ref, dkvf_ref, dga_ref, dgb_ref,
             gx_ref, ht_ref, win_ref,
             dgain_ref, dgqa_ref, dgka_ref, dgqb_ref, dgkb_ref, w_vmem, dproj_ref):
        i = pl.program_id(0)
        perm = perm_ref[...]

        @pl.when(i == 0)
        def _():
            pltpu.sync_copy(w_hbm, w_vmem)
            dgain_ref[...] = jnp.zeros_like(dgain_ref)
            dgqa_ref[...] = jnp.zeros_like(dgqa_ref)
            dgka_ref[...] = jnp.zeros_like(dgka_ref)
            dgqb_ref[...] = jnp.zeros_like(dgqb_ref)
            dgkb_ref[...] = jnp.zeros_like(dgkb_ref)

        cos1 = cos_ref[...]
        sin1 = sin_ref[...]
        cos4 = jnp.tile(cos1, (1, 4))
        sin4 = jnp.tile(sin1, (1, 4))

        dt, dg = _qknorm_rope_bwd(dqa_ref[...], qa_raw_ref[...], gqa_ref[...], cos4, sin4, bd256_ref[...])
        dproj_ref[:, C_QA:C_KA] = dt.astype(BF16)
        dgqa_ref[...] += jnp.sum(dg, axis=0, keepdims=True)
        dt, dg = _qknorm_rope_bwd(dkva_ref[:, :A_KV_WIDTH], ka_raw_ref[...], gka_ref[...], cos1, sin1,
                                  bd128_ref[...])
        dproj_ref[:, C_KA:C_VA] = dt.astype(BF16)
        dgka_ref[...] += jnp.sum(dg, axis=0, keepdims=True)
        dproj_ref[:, C_VA:C_GA] = dkva_ref[:, A_KV_WIDTH:]
        dproj_ref[:, C_GA:C_QB] = dga_ref[...]
        dq = dq1_ref[...].astype(F32) + _load_folded_bf16(dqf_ref, perm)
        dt, dg = _qknorm_rope_bwd(dq, qb_raw_ref[...], gqb_ref[...], cos4, sin4, bd256_ref[...])
        dproj_ref[:, C_QB:C_KB] = dt.astype(BF16)
        dgqb_ref[...] += jnp.sum(dg, axis=0, keepdims=True)
        dkv = dkv1_ref[...].astype(F32) + _load_folded_bf16(dkvf_ref, perm)
        dt, dg = _qknorm_rope_bwd(dkv[:, :WIDTH], kb_raw_ref[...], gkb_ref[...], cos4, sin4, bd256_ref[...])
        dproj_ref[:, C_KB:C_VB] = dt.astype(BF16)
        dgkb_ref[...] += jnp.sum(dg, axis=0, keepdims=True)
        dproj_ref[:, C_VB:C_GB] = dkv[:, WIDTH:].astype(BF16)
        dproj_ref[:, C_GB:C_END] = dgb_ref[...]
        for k, start in enumerate(WIN_START):
            win_ref[k] = dproj_ref[:, start:start + WIN]

        xt = x_ref[...]
        gain_row = gain_ref[...]
        r = lax.rsqrt(jnp.mean(xt * xt, axis=-1, keepdims=True) + EPS)
        xr = xt * r
        ht_ref[...] = (xr * gain_row).T.astype(BF16)
        dh = _dot(dproj_ref[...], w_vmem[...])
        dgain_ref[...] += jnp.sum(dh * xr, axis=0, keepdims=True)
        u = dh * gain_row
        gx_ref[...] = dy_ref[...] + r * (u - xr * jnp.mean(u * xr, axis=-1, keepdims=True))

    def acc_row(w):
        return pl.BlockSpec((1, w), lambda i: (0, 0))

    sds = jax.ShapeDtypeStruct
    any_spec = pl.BlockSpec(memory_space=pl.ANY)
    win_spec = pl.BlockSpec((N_CHIP, tm, WIN), lambda i: (0, i, 0))
    return pl.pallas_call(
        body, name="inproj_bwd", grid=(s // tm,),
        in_specs=[_rows(D_MODEL), _rows(D_MODEL), _whole(gain.shape), any_spec, _rows(LANES), _rows(LANES),
                  _whole(gqa.shape), _whole(gka.shape), _whole(gqb.shape), _whole(gkb.shape), _whole(bd256.shape),
                  _whole(bd128.shape), _whole(perm.shape),
                  _rows(WIDTH), _rows(A_KV_WIDTH), _rows(WIDTH), _rows(WIDTH),
                  _rows(WIDTH), _rows(2 * A_KV_WIDTH), _rows(WIDTH), _rows(2 * WIDTH)]
                 + [_folded_rows(WIDTH), _folded_rows(2 * WIDTH), _rows(WIDTH), _rows(WIDTH)],
        out_specs=(_rows(D_MODEL), pl.BlockSpec((D_MODEL, tm), lambda i: (0, i)), win_spec, acc_row(D_MODEL), acc_row(WIDTH), acc_row(A_KV_WIDTH), acc_row(WIDTH), acc_row(WIDTH)),
        out_shape=(sds((s, D_MODEL), F32), sds((D_MODEL, s), BF16), sds((N_CHIP, s, WIN), BF16),
                   sds((1, D_MODEL), F32),
                   sds((1, WIDTH), F32), sds((1, A_KV_WIDTH), F32), sds((1, WIDTH), F32), sds((1, WIDTH), F32)),
        scratch_shapes=[pltpu.VMEM((IN_WIDTH, D_MODEL), BF16), pltpu.VMEM((tm, IN_WIDTH), BF16)],
        compiler_params=pltpu.CompilerParams(dimension_semantics=("arbitrary",), vmem_limit_bytes=VMEM_LIMIT),
    )(x2, dy, gain, w_bf, cos, sin_s, gqa, gka, gqb, gkb, bd256, bd128, perm, qa_raw, ka_raw, qb_raw, kb_raw,
      *d_a, *d_b1, *d_bf, dg_a, dg_b)


def _rope_tables(s):
    half = HEAD_DIM // 2
    inv = jnp.tile(ROPE_THETA ** (-jnp.arange(half, dtype=F32) / half), 4)
    sign = jnp.tile(jnp.concatenate([-jnp.ones((half,), F32), jnp.ones((half,), F32)]), 2)
    hi = (jnp.arange(s // ROPE_SPLIT) * ROPE_SPLIT).astype(F32)[:, None] * inv[None, :]
    lo = jnp.arange(ROPE_SPLIT).astype(F32)[:, None] * inv[None, :]
    ch, sh, cl, sl = jnp.cos(hi)[:, None, :], jnp.sin(hi)[:, None, :], jnp.cos(lo)[None], jnp.sin(lo)[None]
    cos = (ch * cl - sh * sl).reshape(s, LANES)
    sin = (sh * cl + ch * sl).reshape(s, LANES)
    return cos, sin * sign[None, :]


def _block_diag_ones(w):
    idx = jnp.arange(w) // HEAD_DIM
    return (idx[:, None] == idx[None, :]).astype(BF16)


def _local_step(x2, tgt2, norm_gain, w_in_bf, q_norm_a, k_norm_a, sinks_a, q_norm_b, k_norm_b, w_out_blk):
    s = x2.shape[0]
    cos, sin_s = _rope_tables(s)
    bd256, bd128 = _block_diag_ones(2 * LANES), _block_diag_ones(A_KV_WIDTH)
    gqa = jnp.tile(q_norm_a, (1, HEADS))
    gka = jnp.tile(k_norm_a, (1, 2))
    gqb = jnp.tile(q_norm_b, (1, HEADS))
    gkb = jnp.tile(k_norm_b, (1, HEADS))
    sink_row = jnp.pad(sinks_a, ((0, 0), (0, LANES - HEADS)))
    perm = _fold_matrix()

    (qa, kva, qb, kvb, qbf, kvbf, qa_raw, ka_raw, g_a, qb_raw, kb_raw, g_b, w_out_bf) = _inproj(
        x2, norm_gain, w_in_bf, cos, sin_s, gqa, gka, gqb, gkb, bd256, bd128, w_out_blk)

    att_a = _attn_fwd(qa, kva, sinks_a, dil=1, max_dist=A_MAX_DIST, name="attn_a_fwd")
    att_b1 = _attn_fwd(qb, kvb, None, dil=1, max_dist=B_MAX_DIST, name="attn_b1_fwd")
    att_b4 = _attn_fwd(qbf, kvbf, None, dil=4, max_dist=B_MAX_DIST, name="attn_b4_fwd")
    att_b16 = _attn_fwd(qbf, kvbf, None, dil=16, max_dist=B_MAX_DIST, name="attn_b16_fwd")

    (dy, do_a, do_b, do_bf, dg_a, dg_b, ld_a, ld_b, ld_bf, gw_out, loss_part, dsink) = _outproj(
        att_a, att_b1, att_b4, att_b16, g_a, g_b, x2, tgt2, w_out_bf, sink_row, perm)

    d_a = _attn_bwd(qa, kva, do_a, ld_a, dil=1, max_dist=A_MAX_DIST, name="attn_a_bwd")
    d_b1 = _attn_bwd(qb, kvb, do_b, ld_b, dil=1, max_dist=B_MAX_DIST, name="attn_b1_bwd")
    d_b4 = _attn_bwd(qbf, kvbf, do_bf, ld_bf, dil=4, max_dist=B_MAX_DIST, name="attn_b4_bwd")
    d_b16 = _attn_bwd(qbf, kvbf, do_bf, ld_bf, dil=16, max_dist=B_MAX_DIST, name="attn_b16_bwd", onto=d_b4)

    gx, h_t, wins, dgain, dgqa, dgka, dgqb, dgkb = _inproj_bwd(
        x2, dy, norm_gain, w_in_bf, cos, sin_s, gqa, gka, gqb, gkb, bd256, bd128, perm,
        qa_raw, ka_raw, qb_raw, kb_raw, d_a, d_b1, d_b16, dg_a, dg_b)
    return loss_part, gx, h_t, wins, gw_out, (dgain, dgqa, dgka, dsink, dgqb, dgkb)


def _position():
    return lax.axis_index("x"), lax.axis_index("y"), lax.axis_index("c")


GATHER_CHUNKS = 2


def _gather_weights(blocks, name):
    n = len(blocks)
    ch = GATHER_CHUNKS

    def body(*refs):
        src_refs, dst_refs = refs[:n], refs[n:2 * n]
        ici_send, ici_recv, hop_send, hop_recv, d2d_send, d2d_recv = refs[2 * n:]
        x, y, c = _position()
        b = 2 * x + y
        via = 2 - c
        out = 3 - via
        for k in range(n):
            dst_refs[k][b] = src_refs[k][...].astype(BF16)

        def rows(k, core, j):
            half = blocks[k].shape[0] // 2
            return pl.ds(pl.multiple_of(core * half + j * (half // ch), half // ch), half // ch)

        def chip(rel):
            return x ^ (rel >> 1), y ^ (rel & 1)

        def ici(k, j, slot, rel, send_sems, recv_sems, sem):
            px, py = chip(rel)
            piece = dst_refs[k].at[slot, rows(k, c, j)]
            return pltpu.make_async_remote_copy(src_ref=piece, dst_ref=piece, send_sem=send_sems.at[sem],
                                                recv_sem=recv_sems.at[sem], device_id=(px, py, c),
                                                device_id_type=MESH)

        def direct(k, j, slot, rel):
            return ici(k, j, slot, rel, ici_send, ici_recv, ((rel - 1) * ch + j) * n + k)

        def hop(k, j, slot, rel):
            return ici(k, j, slot, rel, hop_send, hop_recv, j * n + k)

        def d2d(k, j, rel, core):
            piece = dst_refs[k].at[b ^ rel, rows(k, core, j)]
            sem = ((rel - 1) * ch + j) * n + k
            return pltpu.make_async_remote_copy(src_ref=piece, dst_ref=piece, send_sem=d2d_send.at[sem],
                                                recv_sem=d2d_recv.at[sem], device_id=(x, y, 1 - c),
                                                device_id_type=MESH)

        pieces = [(k, j) for j in range(ch) for k in range(n)]
        for k, j in pieces:
            for rel in (1, 2):
                direct(k, j, b, rel).start()
        for k, j in pieces:
            direct(k, j, b ^ via, via).wait_recv()
            hop(k, j, b ^ via, out).start()
            d2d(k, j, via, c).start()
        for k, j in pieces:
            direct(k, j, b ^ out, out).wait_recv()
            d2d(k, j, out, c).start()
        for k, j in pieces:
            hop(k, j, b ^ 3, via).wait_recv()
            d2d(k, j, 3, c).start()
        for k, j in pieces:
            for rel in (1, 2, 3):
                d2d(k, j, rel, 1 - c).wait_recv()
        for k, j in pieces:
            for rel in (1, 2):
                direct(k, j, b, rel).wait_send()
            hop(k, j, b ^ via, out).wait_send()
            d2d(k, j, via, c).wait_send()
            d2d(k, j, out, c).wait_send()
            d2d(k, j, 3, c).wait_send()

    vmem_spec = pl.BlockSpec(memory_space=pltpu.VMEM)
    dma = pltpu.SemaphoreType.DMA
    out_shape = tuple(jax.ShapeDtypeStruct((N_CHIP,) + a.shape, BF16) for a in blocks)
    return pl.pallas_call(
        body, name=name, in_specs=[vmem_spec] * n, out_specs=tuple([vmem_spec] * n), out_shape=out_shape,
        scratch_shapes=[dma((2 * ch * n,)), dma((2 * ch * n,)), dma((ch * n,)), dma((ch * n,)),
                        dma((3 * ch * n,)), dma((3 * ch * n,))],
        compiler_params=pltpu.CompilerParams(vmem_limit_bytes=VMEM_LIMIT),
    )(*blocks)


def _grad_reduce(order, h_t, wins, gw_out, small):
    s = h_t.shape[1]
    tk = GRAD_ROWS
    n_i = s // tk
    half = D_MODEL // 2
    o_half = OUT_ROWS // 2
    n_rel = N_CHIP - 1

    def body(order_ref, ht_ref, win_ref, gwo_ref, small_ref,
             win_out, wout_out, small_out,
             acc, mine, s1, r1, s2, r2, so1, ro1, so2, ro2, pair_in, pair_o, small_land,
             s1_send, s1_recv, s2_send, s2_recv, o1_send, o1_recv, o2_send, o2_recv,
             pair_send, pair_recv, small_send, small_recv):
        j = pl.program_id(0)
        i = pl.program_id(1)
        x, y, c = _position()
        me = 4 * x + 2 * y + c
        sibling = (x, y, 1 - c)
        my_rows = pl.ds(pl.multiple_of(c * half, half), half)
        sib_rows = pl.ds(pl.multiple_of((1 - c) * half, half), half)

        def chip_of(rel):
            return x ^ (rel >> 1), y ^ (rel & 1)

        def level1(k):
            return pltpu.make_async_remote_copy(src_ref=s1.at[k], dst_ref=r1.at[k], send_sem=s1_send.at[k],
                                                recv_sem=s1_recv.at[k], device_id=sibling, device_id_type=MESH)

        def level2(k):
            px, py = chip_of(RELATIONS[k])
            return pltpu.make_async_remote_copy(src_ref=s2.at[k], dst_ref=r2.at[k], send_sem=s2_send.at[k],
                                                recv_sem=s2_recv.at[k], device_id=(px, py, c), device_id_type=MESH)

        def out_level1(bk):
            return pltpu.make_async_remote_copy(src_ref=so1.at[bk], dst_ref=ro1.at[bk], send_sem=o1_send.at[bk],
                                                recv_sem=o1_recv.at[bk], device_id=sibling, device_id_type=MESH)

        def out_level2(k):
            px, py = chip_of(RELATIONS[k])
            return pltpu.make_async_remote_copy(src_ref=so2.at[k], dst_ref=ro2.at[k], send_sem=o2_send.at[k],
                                                recv_sem=o2_recv.at[k], device_id=(px, py, c), device_id_type=MESH)

        def small_copy(d):
            px, py, pc = x ^ (d >> 2), y ^ ((d >> 1) & 1), c ^ (d & 1)
            return pltpu.make_async_remote_copy(src_ref=small_ref, dst_ref=small_land.at[me],
                                                send_sem=small_send.at[d], recv_sem=small_recv.at[d],
                                                device_id=(px, py, pc), device_id_type=MESH)

        def pair_copy(k, buf):
            return pltpu.make_async_remote_copy(src_ref=buf.at[0], dst_ref=buf.at[1], send_sem=pair_send.at[k],
                                                recv_sem=pair_recv.at[k], device_id=sibling, device_id_type=MESH)

        def out_rows(bk, core):
            return pl.ds(pl.multiple_of(bk * OUT_ROWS + core * o_half, o_half), o_half)

        @pl.when((j == 0) & (i == 0))
        def _():
            for d in range(1, N_DEV):
                small_copy(d).start()
            small_land[me] = small_ref[...]
            for bk in range(N_CHIP):
                so1[bk] = gwo_ref[out_rows(bk, 1 - c), :].astype(BF16)
                out_level1(bk).start()

        @pl.when((j == 0) & (i == 1))
        def _():
            b = 2 * x + y
            for bk in range(N_CHIP):
                out_level1(bk).wait_recv()
            for k in range(n_rel):
                px, py = chip_of(RELATIONS[k])
                bk = 2 * px + py
                so2[k] = (gwo_ref[out_rows(bk, c), :] + ro1[bk].astype(F32)).astype(BF16)
                out_level2(k).start()

        @pl.when(i == 0)
        def _():
            acc[...] = jnp.zeros_like(acc)

        for n0 in range(0, WIN, ACC_COLS):
            n1 = min(n0 + ACC_COLS, WIN)
            acc[:, n0:n1] += _dot(ht_ref[...], win_ref[:, n0:n1])

        for k in range(N_CHIP):
            @pl.when((j == k) & (i == n_i - 1))
            def _(k=k):
                s1[k] = acc[sib_rows, :].astype(BF16)
                level1(k).start()
                mine[...] = acc[my_rows, :]

            if k < n_rel:
                @pl.when((j == k + 1) & (i == 1))
                def _(k=k):
                    level1(k).wait_recv()
                    s2[k] = (mine[...] + r1[k].astype(F32)).astype(BF16)
                    level2(k).start()

        @pl.when((j == N_CHIP - 1) & (i == n_i - 1))
        def _():
            b = 2 * x + y
            level1(N_CHIP - 1).wait_recv()
            total = mine[...] + r1[N_CHIP - 1].astype(F32)
            for k in range(n_rel):
                level2(k).wait_recv()
                total = total + r2[k].astype(F32)
            total = total.T
            pair_in[0] = total
            pair_copy(0, pair_in).start()
            total_o = gwo_ref[out_rows(b, c), :] + ro1[b].astype(F32)
            for k in range(n_rel):
                out_level2(k).wait_recv()
                total_o = total_o + ro2[k].astype(F32)
            pair_o[0] = total_o
            pair_copy(1, pair_o).start()
            for core in range(2):
                @pl.when(c == core)
                def _(core=core):
                    win_out[:, core * half:(core + 1) * half] = total
            wout_out[c] = total_o
            for d in range(1, N_DEV):
                small_copy(d).wait_recv()
            small_out[...] = small_land[...]
            pair_copy(0, pair_in).wait_recv()
            for core in range(2):
                @pl.when(c == core)
                def _(core=core):
                    win_out[:, (1 - core) * half:(2 - core) * half] = pair_in[1]
            pair_copy(1, pair_o).wait_recv()
            wout_out[1 - c] = pair_o[1]
            for d in range(1, N_DEV):
                small_copy(d).wait_send()
            for k in range(N_CHIP):
                level1(k).wait_send()
                out_level1(k).wait_send()
            for k in range(n_rel):
                level2(k).wait_send()
                out_level2(k).wait_send()
            pair_copy(0, pair_in).wait_send()
            pair_copy(1, pair_o).wait_send()

    vmem = pl.BlockSpec(memory_space=pltpu.VMEM)
    dma = pltpu.SemaphoreType.DMA
    sds = jax.ShapeDtypeStruct
    grid_spec = pltpu.PrefetchScalarGridSpec(
        num_scalar_prefetch=1, grid=(N_CHIP, n_i),
        in_specs=[pl.BlockSpec((D_MODEL, tk), lambda j, i, order: (0, i)),
                  pl.BlockSpec((None, tk, WIN), lambda j, i, order: (order[j], i, 0)), vmem, vmem],
        out_specs=(vmem, vmem, vmem),
        scratch_shapes=[
            pltpu.VMEM((D_MODEL, WIN), F32), pltpu.VMEM((half, WIN), F32),
            pltpu.VMEM((N_CHIP, half, WIN), BF16), pltpu.VMEM((N_CHIP, half, WIN), BF16),
            pltpu.VMEM((n_rel, half, WIN), BF16), pltpu.VMEM((n_rel, half, WIN), BF16),
            pltpu.VMEM((N_CHIP, o_half, D_MODEL), BF16), pltpu.VMEM((N_CHIP, o_half, D_MODEL), BF16),
            pltpu.VMEM((n_rel, o_half, D_MODEL), BF16), pltpu.VMEM((n_rel, o_half, D_MODEL), BF16),
            pltpu.VMEM((2, WIN, half), F32), pltpu.VMEM((2, o_half, D_MODEL), F32),
            pltpu.VMEM((N_DEV, PACK_ROWS, D_MODEL), F32),
            dma((N_CHIP,)), dma((N_CHIP,)), dma((n_rel,)), dma((n_rel,)),
            dma((N_CHIP,)), dma((N_CHIP,)), dma((n_rel,)), dma((n_rel,)),
            dma((2,)), dma((2,)), dma((N_DEV,)), dma((N_DEV,))])
    return pl.pallas_call(
        body, name="grad_w_in_reduce", grid_spec=grid_spec,
        out_shape=(sds((WIN, D_MODEL), F32), sds((2, o_half, D_MODEL), F32), sds((N_DEV, PACK_ROWS, D_MODEL), F32)),
        compiler_params=pltpu.CompilerParams(dimension_semantics=("arbitrary", "arbitrary"),
                                             vmem_limit_bytes=VMEM_LIMIT),
    )(order, h_t, wins, gw_out, small)


ADAM_STEPS = 4


def _adamw_math(w, g, m, v):
    m = ADAM_B1 * m + (1.0 - ADAM_B1) * g
    v = ADAM_B2 * v + (1.0 - ADAM_B2) * (g * g)
    m_hat = m / (1.0 - ADAM_B1 ** ADAM_STEP)
    v_hat = v / (1.0 - ADAM_B2 ** ADAM_STEP)
    delta = -ADAM_LR * (m_hat / (jnp.sqrt(v_hat) + ADAM_EPS) + ADAM_WD * w)
    return delta, m, v


def _adamw(w, g, m, v, name):
    r, c = w.shape

    def body(w_ref, g_ref, m_ref, v_ref, d_ref, nm_ref, nv_ref):
        delta, nm, nv = _adamw_math(w_ref[...], g_ref[...], m_ref[...], v_ref[...])
        d_ref[...] = delta
        nm_ref[...] = nm
        nv_ref[...] = nv

    rows = r // ADAM_STEPS
    assert rows * ADAM_STEPS == r and rows % 8 == 0
    spec = pl.BlockSpec((rows, c), lambda i: (i, 0))
    shape = jax.ShapeDtypeStruct((r, c), F32)
    return pl.pallas_call(
        body, name=name, grid=(ADAM_STEPS,), in_specs=[spec] * 4, out_specs=(spec,) * 3,
        out_shape=(shape,) * 3, compiler_params=pltpu.CompilerParams(vmem_limit_bytes=VMEM_LIMIT),
    )(w, g, m, v)


def _adamw_window(w, window, shift, m, v, name):
    r, c = w.shape
    rows = r // ADAM_STEPS
    assert rows * ADAM_STEPS == r and rows % 8 == 0

    def body(shift_ref, w_ref, win_hbm, m_ref, v_ref, g_ref, d_ref, nm_ref, nv_ref, g_vmem):
        start = pl.multiple_of(shift_ref[0] + pl.program_id(0) * rows, 8)
        pltpu.sync_copy(win_hbm.at[pl.ds(start, rows)], g_vmem)
        g = g_vmem[...]
        g_ref[...] = g
        delta, nm, nv = _adamw_math(w_ref[...], g, m_ref[...], v_ref[...])
        d_ref[...] = delta
        nm_ref[...] = nm
        nv_ref[...] = nv

    spec = pl.BlockSpec((rows, c), lambda i, shift_ref: (i, 0))
    shape = jax.ShapeDtypeStruct((r, c), F32)
    grid_spec = pltpu.PrefetchScalarGridSpec(
        num_scalar_prefetch=1, grid=(ADAM_STEPS,),
        in_specs=[spec, pl.BlockSpec(memory_space=pl.ANY), spec, spec], out_specs=(spec,) * 4,
        scratch_shapes=[pltpu.VMEM((rows, c), F32)])
    return pl.pallas_call(
        body, name=name, grid_spec=grid_spec, out_shape=(shape,) * 4,
        compiler_params=pltpu.CompilerParams(vmem_limit_bytes=VMEM_LIMIT),
    )(shift, w, window, m, v)


PACK_ROWS = 8


def _fold_heads(v):
    y = v[:, 0:LANES]
    for j in range(1, v.shape[1] // LANES):
        y = y + v[:, j * LANES:(j + 1) * LANES]
    return y + pltpu.roll(y, HEAD_DIM, 1)


N_SMALL = 6


def _small_adamw(recv, weights, m, v):
    def body(*refs):
        r_ref = refs[0]
        w_refs, m_refs, v_refs = (refs[1 + n * N_SMALL:1 + (n + 1) * N_SMALL] for n in range(3))
        outs = refs[1 + 3 * N_SMALL:]
        g_refs, d_refs, nm_refs, nv_refs = (outs[n * N_SMALL:(n + 1) * N_SMALL] for n in range(4))
        loss_ref = outs[4 * N_SMALL]
        tot = r_ref[0]
        for j in range(1, N_DEV):
            tot = tot + r_ref[j]
        loss_ref[...] = tot[3:4, 0:LANES]
        row1 = tot[1:2, :]
        row2 = tot[2:3, :]
        grads = [tot[0:1, :],
                 _fold_heads(row1[:, 0:WIDTH])[:, :HEAD_DIM],
                 _fold_heads(row2[:, WIDTH:WIDTH + A_KV_WIDTH])[:, :HEAD_DIM],
                 row2[:, WIDTH + A_KV_WIDTH:WIDTH + A_KV_WIDTH + HEADS],
                 _fold_heads(row1[:, WIDTH:2 * WIDTH])[:, :HEAD_DIM],
                 _fold_heads(row2[:, 0:WIDTH])[:, :HEAD_DIM]]
        for n, g in enumerate(grads):
            g_refs[n][...] = g
            delta, nm, nv = _adamw_math(w_refs[n][...], g, m_refs[n][...], v_refs[n][...])
            d_refs[n][...] = delta
            nm_refs[n][...] = nm
            nv_refs[n][...] = nv

    shapes = tuple(jax.ShapeDtypeStruct(a.shape, F32) for a in weights)
    outs = pl.pallas_call(body, name="small_adamw", out_shape=shapes * 4 + (jax.ShapeDtypeStruct((1, LANES), F32),)
                          )(recv, *weights, *m, *v)
    return tuple(outs[n * N_SMALL:(n + 1) * N_SMALL] for n in range(4)) + (outs[4 * N_SMALL],)


def kernel(x, norm_gain, w_in, q_norm_a, k_norm_a, sinks_a, q_norm_b, k_norm_b, w_out, loss_target, m_norm_gain, m_w_in, m_q_norm_a, m_k_norm_a, m_sinks_a, m_q_norm_b, m_k_norm_b, m_w_out, v_norm_gain, v_w_in, v_q_norm_a, v_k_norm_a, v_sinks_a, v_q_norm_b, v_k_norm_b, v_w_out):
    chip = 2 * lax.axis_index("x") + lax.axis_index("y")

    w_in_t, m_w_in_t, v_w_in_t = w_in[0].T, m_w_in[0].T, v_w_in[0].T

    (w_in_all,) = _gather_weights([w_in_t], "gather_weights")
    w_in_bf = w_in_all.reshape(IN_WIDTH, D_MODEL)

    loss_part, gx, h_t, wins, gw_out, (dgain, dgqa, dgka, dsink, dgqb, dgkb) = _local_step(
        x[0], loss_target[0], norm_gain, w_in_bf, q_norm_a, k_norm_a, sinks_a, q_norm_b, k_norm_b, w_out[0])

    small = jnp.concatenate([
        dgain, jnp.concatenate([dgqa, dgqb], axis=1),
        jnp.concatenate([dgkb, dgka, dsink, jnp.zeros((1, D_MODEL - WIDTH - 2 * A_KV_WIDTH), F32)], axis=1),
        jnp.pad(loss_part, ((0, 0), (0, D_MODEL - LANES))),
        jnp.zeros((PACK_ROWS - 4, D_MODEL), F32)], axis=0)
    order = (chip ^ jnp.array(RELATIONS, jnp.int32)).astype(jnp.int32)
    win_sum, wout_sum, small_recv = _grad_reduce(order, h_t, wins, gw_out, small)
    shift = jnp.array(WIN_SHIFT, jnp.int32)[chip].reshape(1)
    g_w_out = wout_sum.reshape(OUT_ROWS, D_MODEL)

    g_w_in, d_w_in, nm_w_in, nv_w_in = (
        a.T for a in _adamw_window(w_in_t, win_sum, shift, m_w_in_t, v_w_in_t, "adamw_w_in"))
    d_w_out, nm_w_out, nv_w_out = _adamw(w_out[0], g_w_out, m_w_out[0], v_w_out[0], "adamw_w_out")
    g_s, d_s, nm_s, nv_s, loss_row = _small_adamw(
        small_recv,
        (norm_gain, q_norm_a, k_norm_a, sinks_a, q_norm_b, k_norm_b),
        (m_norm_gain, m_q_norm_a, m_k_norm_a, m_sinks_a, m_q_norm_b, m_k_norm_b),
        (v_norm_gain, v_q_norm_a, v_k_norm_a, v_sinks_a, v_q_norm_b, v_k_norm_b))
    loss = loss_row[0, 0]

    def leaves(small_ones, big_in, big_out):
        return (small_ones[0], big_in[None]) + tuple(small_ones[1:]) + (big_out[None],)

    return ((loss, gx[None]) + leaves(g_s, g_w_in, g_w_out) + leaves(d_s, d_w_in, d_w_out)
            + leaves(nm_s, nm_w_in, nm_w_out) + leaves(nv_s, nv_w_in, nv_w_out))
```

```python
import jax
import jax.numpy as jnp
from jax import lax
from jax.experimental import pallas as pl
from jax.experimental.pallas import tpu as pltpu

F32 = jnp.float32
BF16 = jnp.bfloat16

D_MODEL = 1024
HEAD_DIM = 64
HEADS = 8
WIDTH = HEADS * HEAD_DIM
A_KV_WIDTH = 2 * HEAD_DIM
BLOCK = 128
LANES = 128
FOLD = 16
A_MAX_DIST = 127
B_MAX_DIST = 128
ROPE_THETA = 10000.0
ROPE_SPLIT = 64
EPS = 1e-6
NEG = -1e30
SCALE = HEAD_DIM ** -0.5

IN_WIDTH = 3328
C_QA, C_KA, C_VA, C_GA, C_QB, C_KB, C_VB, C_GB, C_END = 0, 512, 640, 768, 1280, 1792, 2304, 2816, 3328

N_DEV = 8
N_CHIP = 4
MESH = pl.DeviceIdType.MESH
WIN = 896
WIN_START = (0, 768, 1664, 2432)
WIN_SHIFT = (0, 64, 0, 64)
OUT_ROWS = D_MODEL // N_CHIP
RELATIONS = (3, 1, 2, 0)

ADAM_LR = 0.001
ADAM_B1 = 0.9
ADAM_B2 = 0.999
ADAM_EPS = 1e-08
ADAM_WD = 0.01
ADAM_STEP = 10

ROW_TILE = 256
FOLD_ROWS = ROW_TILE // FOLD
GRAD_ROWS = 1024
ACC_COLS = 256
VMEM_LIMIT = 56 * 1024 * 1024


def _dot(a, b):
    return jnp.dot(a, b, preferred_element_type=F32)


def _dot_nt(a, b):
    return lax.dot_general(a, b, (((1,), (1,)), ((), ())), preferred_element_type=F32)


def _dot_tn(a, b):
    return lax.dot_general(a, b, (((0,), (0,)), ((), ())), preferred_element_type=F32)


def _head_sum(z, bd):
    w = bd.shape[0]
    zb = z.astype(BF16)
    parts = [_dot(zb[:, a:a + w], bd) for a in range(0, z.shape[1], w)]
    return parts[0] if len(parts) == 1 else jnp.concatenate(parts, axis=1)


def _swap_halves(t):
    w = t.shape[1]
    lane = lax.broadcasted_iota(jnp.int32, t.shape, 1)
    return jnp.where(lane % HEAD_DIM < HEAD_DIM // 2, pltpu.roll(t, w - 32, 1), pltpu.roll(t, 32, 1))


def _qknorm_rope(t, g, cos, sin_s, bd):
    r = lax.rsqrt(_head_sum(t * t, bd) * (1.0 / HEAD_DIM) + EPS)
    n = (t * r) * g
    return n * cos + _swap_halves(n) * sin_s


def _qknorm_rope_bwd(dout, t, g, cos, sin_s, bd):
    dout, t = dout.astype(F32), t.astype(F32)
    dn = dout * cos + _swap_halves(dout * sin_s)
    r = lax.rsqrt(_head_sum(t * t, bd) * (1.0 / HEAD_DIM) + EPS)
    tr = t * r
    u = dn * g
    dt = r * (u - tr * (_head_sum(u * tr, bd) * (1.0 / HEAD_DIM)))
    return dt, dn * tr


def _sigmoid(g):
    return 1.0 / (1.0 + jnp.exp(-g))


def _expand_heads(st):
    t = st.shape[0]
    lane = lax.broadcasted_iota(jnp.int32, (t, LANES), 1)
    chunks = []
    for c in range(WIDTH // LANES):
        chunks.append(jnp.where(lane < HEAD_DIM, st[:, 2 * c:2 * c + 1], st[:, 2 * c + 1:2 * c + 2]))
    return jnp.concatenate(chunks, axis=1)


def _reduce_heads(z):
    t = z.shape[0]
    lane = lax.broadcasted_iota(jnp.int32, (t, LANES), 1)
    out = jnp.zeros((t, LANES), F32)
    for c in range(WIDTH // LANES):
        zc = z[:, c * LANES:(c + 1) * LANES]
        for ph in range(2):
            s = jnp.sum(jnp.where((lane // HEAD_DIM) == ph, zc, 0.0), axis=-1, keepdims=True)
            out = jnp.where(lane == 2 * c + ph, s, out)
    return out


def _fold_scratch(w):
    return pltpu.VMEM((w // LANES, ROW_TILE, LANES), F32)


def _store_folded(out_ref, val, scr, col0=0):
    w = val.shape[1]
    n = w // LANES
    for c in range(n):
        scr[c] = val[:, c * LANES:(c + 1) * LANES]
    for r in range(FOLD):
        piece = [scr[c, pl.ds(r, FOLD_ROWS, stride=FOLD), :] for c in range(n)]
        out_ref[r, :, col0:col0 + w] = (piece[0] if n == 1 else jnp.concatenate(piece, axis=1)).astype(out_ref.dtype)


def _load_folded(in_ref, scr):
    n = in_ref.shape[2] // LANES
    for r in range(FOLD):
        blk = in_ref[r].astype(F32)
        for c in range(n):
            scr[c, pl.ds(r, FOLD_ROWS, stride=FOLD), :] = blk[:, c * LANES:(c + 1) * LANES]
    return scr[0] if n == 1 else jnp.concatenate([scr[c] for c in range(n)], axis=1)


def _fold_matrix():
    f = jnp.arange(ROW_TILE)
    return (jnp.arange(ROW_TILE)[None, :] == (FOLD * (f % FOLD_ROWS) + f // FOLD_ROWS)[:, None]).astype(BF16)


def _store_folded_bf16(out_ref, val, perm):
    folded = _dot(perm, val.astype(BF16)).astype(out_ref.dtype)
    for r in range(FOLD):
        out_ref[r] = folded[r * FOLD_ROWS:(r + 1) * FOLD_ROWS]


def _load_folded_bf16(in_ref, perm):
    blk = jnp.concatenate([in_ref[r] for r in range(FOLD)], axis=0)
    return _dot(perm, blk)


def _rows(w, tm=ROW_TILE):
    return pl.BlockSpec((tm, w), lambda i: (i, 0))


def _folded_rows(w):
    return pl.BlockSpec((FOLD, FOLD_ROWS, w), lambda i: (0, i, 0))


def _whole(shape):
    return pl.BlockSpec(shape, lambda i: (0,) * len(shape))


def _inproj(x2, gain, w_bf, cos, sin_s, gqa, gka, gqb, gkb, bd256, bd128, w_out_blk):
    s = x2.shape[0]
    tm = ROW_TILE
    n_steps = s // tm
    n_rel = N_CHIP - 1
    o_half = OUT_ROWS // 2

    def body(x_ref, gain_ref, w_hbm, cos_ref, sin_ref, gqa_ref, gka_ref, gqb_ref, gkb_ref, bd256_ref, bd128_ref,
             wout_ref, qa_ref, kva_ref, qb_ref, kvb_ref, qbf_ref, kvbf_ref,
             qa_raw_ref, ka_raw_ref, ga_ref, qb_raw_ref, kb_raw_ref, gb_ref, wout_all_ref,
             w_vmem, scr, land, ici_send, ici_recv, d2d_send, d2d_recv):
        i = pl.program_id(0)
        px_, py_, c = _position()
        b = 2 * px_ + py_

        def piece(chip_idx, core):
            return land.at[chip_idx, pl.ds(pl.multiple_of(core * o_half, o_half), o_half)]

        def other_chip(d):
            ox, oy = px_ ^ (d >> 1), py_ ^ (d & 1)
            return ox, oy, 2 * ox + oy

        def ici_copy(d, chip_idx):
            ox, oy, _ = other_chip(d)
            return pltpu.make_async_remote_copy(
                src_ref=piece(chip_idx, c), dst_ref=piece(chip_idx, c), send_sem=ici_send.at[d - 1],
                recv_sem=ici_recv.at[d - 1], device_id=(ox, oy, c), device_id_type=MESH)

        def d2d_copy(d, core):
            return pltpu.make_async_remote_copy(
                src_ref=piece(other_chip(d)[2], core), dst_ref=piece(other_chip(d)[2], core),
                send_sem=d2d_send.at[d - 1], recv_sem=d2d_recv.at[d - 1], device_id=(px_, py_, 1 - c),
                device_id_type=MESH)

        @pl.when(i == 0)
        def _():
            pltpu.sync_copy(w_hbm, w_vmem)
            land[b] = wout_ref[...].astype(BF16)
            for d in range(1, N_CHIP):
                ici_copy(d, b).start()

        @pl.when(i == n_steps // 2)
        def _():
            for d in range(1, N_CHIP):
                ici_copy(d, other_chip(d)[2]).wait_recv()
                d2d_copy(d, c).start()

        @pl.when(i == n_steps - 1)
        def _():
            for d in range(1, N_CHIP):
                d2d_copy(d, 1 - c).wait_recv()
            for d in range(1, N_CHIP):
                ici_copy(d, b).wait_send()
                d2d_copy(d, c).wait_send()
            for k in range(N_CHIP):
                wout_all_ref[k * OUT_ROWS:(k + 1) * OUT_ROWS, :] = land[k]

        xt = x_ref[...]
        r = lax.rsqrt(jnp.mean(xt * xt, axis=-1, keepdims=True) + EPS)
        h = ((xt * r) * gain_ref[...]).astype(BF16)
        cos1 = cos_ref[...]
        sin1 = sin_ref[...]
        cos4 = jnp.tile(cos1, (1, 4))
        sin4 = jnp.tile(sin1, (1, 4))

        def seg(a, b):
            return _dot_nt(h, w_vmem[a:b, :])

        t = seg(C_QA, C_KA)
        qa_raw_ref[...] = t.astype(BF16)
        qa_ref[...] = (_qknorm_rope(t, gqa_ref[...], cos4, sin4, bd256_ref[...]) * SCALE).astype(BF16)
        t = seg(C_KA, C_VA)
        ka_raw_ref[...] = t.astype(BF16)
        kva_ref[:, :A_KV_WIDTH] = _qknorm_rope(t, gka_ref[...], cos1, sin1, bd128_ref[...]).astype(BF16)
        kva_ref[:, A_KV_WIDTH:] = seg(C_VA, C_GA).astype(BF16)
        ga_ref[...] = seg(C_GA, C_QB).astype(BF16)
        t = seg(C_QB, C_KB)
        qb_raw_ref[...] = t.astype(BF16)
        t = _qknorm_rope(t, gqb_ref[...], cos4, sin4, bd256_ref[...]) * SCALE
        qb_ref[...] = t.astype(BF16)
        _store_folded(qbf_ref, t, scr)
        t = seg(C_KB, C_VB)
        kb_raw_ref[...] = t.astype(BF16)
        t = _qknorm_rope(t, gkb_ref[...], cos4, sin4, bd256_ref[...])
        kvb_ref[:, :WIDTH] = t.astype(BF16)
        _store_folded(kvbf_ref, t, scr)
        t = seg(C_VB, C_GB)
        kvb_ref[:, WIDTH:] = t.astype(BF16)
        _store_folded(kvbf_ref, t, scr, WIDTH)
        gb_ref[...] = seg(C_GB, C_END).astype(BF16)

    sds = jax.ShapeDtypeStruct
    ln = s // FOLD
    out_shape = (sds((s, WIDTH), BF16), sds((s, 2 * A_KV_WIDTH), BF16), sds((s, WIDTH), BF16),
                 sds((s, 2 * WIDTH), BF16), sds((FOLD, ln, WIDTH), BF16), sds((FOLD, ln, 2 * WIDTH), BF16),
                 sds((s, WIDTH), BF16), sds((s, A_KV_WIDTH), BF16), sds((s, WIDTH), BF16),
                 sds((s, WIDTH), BF16), sds((s, WIDTH), BF16), sds((s, WIDTH), BF16),
                 sds((D_MODEL, D_MODEL), BF16))
    out_specs = (_rows(WIDTH), _rows(2 * A_KV_WIDTH), _rows(WIDTH), _rows(2 * WIDTH),
                 _folded_rows(WIDTH), _folded_rows(2 * WIDTH),
                 _rows(WIDTH), _rows(A_KV_WIDTH), _rows(WIDTH), _rows(WIDTH), _rows(WIDTH), _rows(WIDTH),
                 _whole((D_MODEL, D_MODEL)))
    dma = pltpu.SemaphoreType.DMA
    return pl.pallas_call(
        body, name="inproj_fwd", grid=(n_steps,),
        in_specs=[_rows(D_MODEL), _whole(gain.shape), pl.BlockSpec(memory_space=pl.ANY), _rows(LANES), _rows(LANES),
                  _whole(gqa.shape), _whole(gka.shape), _whole(gqb.shape), _whole(gkb.shape), _whole(bd256.shape),
                  _whole(bd128.shape), _whole(w_out_blk.shape)],
        out_specs=out_specs, out_shape=out_shape,
        scratch_shapes=[pltpu.VMEM((IN_WIDTH, D_MODEL), BF16), _fold_scratch(WIDTH),
                        pltpu.VMEM((N_CHIP, OUT_ROWS, D_MODEL), BF16),
                        dma((n_rel,)), dma((n_rel,)), dma((n_rel,)), dma((n_rel,))],
        compiler_params=pltpu.CompilerParams(dimension_semantics=("arbitrary",), vmem_limit_bytes=VMEM_LIMIT),
    )(x2, gain, w_bf, cos, sin_s, gqa, gka, gqb, gkb, bd256, bd128, w_out_blk)


def _seq_pos(idx, dil):
    if dil == 4:
        return 4 * (idx % 32) + idx // 32
    return idx


def _upper_mask(dil, r0=0, rows=2 * BLOCK):
    qi = (lax.broadcasted_iota(jnp.int32, (rows, BLOCK), 0) + r0) % BLOCK
    kj = lax.broadcasted_iota(jnp.int32, (rows, BLOCK), 1)
    return _seq_pos(kj, dil) > _seq_pos(qi, dil)


def _eye_mask(r0=0, rows=2 * BLOCK):
    qi = (lax.broadcasted_iota(jnp.int32, (rows, BLOCK), 0) + r0) % BLOCK
    kj = lax.broadcasted_iota(jnp.int32, (rows, BLOCK), 1)
    return qi == kj


def _stack_heads(a2, c, gqa):
    lane = lax.broadcasted_iota(jnp.int32, (1, LANES), 1) // HEAD_DIM
    zero = jnp.zeros_like(a2)
    if gqa:
        keep = lane == (c // 2)
        return jnp.concatenate([jnp.where(keep, a2, zero), jnp.where(keep, _swap_heads(a2), zero)], axis=0)
    return jnp.concatenate([jnp.where(lane == 0, a2, zero), jnp.where(lane == 1, a2, zero)], axis=0)


def _unstack_heads(a, c, gqa):
    lane = lax.broadcasted_iota(jnp.int32, (1, LANES), 1) // HEAD_DIM
    if gqa:
        return jnp.where(lane == (c // 2), a[:BLOCK], _swap_heads(a[BLOCK:]))
    return jnp.where(lane == 0, a[:BLOCK], a[BLOCK:])


def _stacked_head_ids(c, gqa):
    if gqa:
        return 2 * c + c // 2, 2 * c + 1 - c // 2
    return 2 * c, 2 * c + 1


def _per_head_rows(blk, heads):
    return jnp.concatenate([blk[:, heads[0]:heads[0] + 1], blk[:, heads[1]:heads[1] + 1]], axis=0)


def _attn_view(a, dil):
    if dil == 1:
        return a[None]
    if dil == 4:
        return a.reshape(4, 4, a.shape[1], a.shape[2])
    return a


def _attn_unview(a, dil):
    if dil == 1:
        return a[0]
    if dil == 4:
        return a.reshape(FOLD, a.shape[2], a.shape[3])
    return a


ATTN_BLOCKS_PER_STEP = 16


def _attn_specs(dil):
    if dil == 4:
        def spec(n, fn):
            return lambda w: pl.BlockSpec((4, None, n * BLOCK // 4, w), lambda r, i: (0, r, fn(i), 0))
    else:
        def spec(n, fn):
            return lambda w: pl.BlockSpec((None, n * BLOCK, w), lambda r, i: (r, fn(i), 0))
    return spec


def _blk_rows(g, dil):
    n = BLOCK // 4 if dil == 4 else BLOCK
    if isinstance(g, int):
        return slice(g * n, (g + 1) * n)
    return pl.ds(pl.multiple_of(g * n, n), n)


def _blk_load(ref, sl, dil, g=0):
    if dil == 4:
        return ref[:, _blk_rows(g, dil), sl].reshape(BLOCK, sl.stop - sl.start)
    return ref[_blk_rows(g, dil), sl]


def _blk_store(ref, sl, val, dil, g=0):
    val = val.astype(ref.dtype)
    if dil == 4:
        ref[:, _blk_rows(g, dil), sl] = val.reshape(4, BLOCK // 4, sl.stop - sl.start)
    else:
        ref[_blk_rows(g, dil), sl] = val


def _swap_heads(a):
    return pltpu.roll(a.astype(F32), HEAD_DIM, 1).astype(a.dtype)


STAT_SHIFT = 8


def _attn_fwd(q, kv, sinks, *, dil, max_dist, name):
    q, kv = _attn_view(q, dil), _attn_view(kv, dil)
    kw = kv.shape[-1] // 2
    gqa = kw == A_KV_WIDTH
    n_seq = dil
    nb = (q.shape[-2] * (4 if dil == 4 else 1)) // BLOCK
    per_step = min(ATTN_BLOCKS_PER_STEP, nb)
    with_sinks = sinks is not None
    all_lanes = slice(0, LANES)
    assert max_dist in (BLOCK - 1, BLOCK) and nb % per_step == 0 and (per_step == 1 or per_step % 2 == 0)
    diag = max_dist == BLOCK

    def body(*refs):
        if with_sinks:
            q_ref, kvp_ref, kvc_ref, sink_ref, o_ref, ml_ref = refs
        else:
            q_ref, kvp_ref, kvc_ref, o_ref, ml_ref = refs

        chunks = range(WIDTH // LANES)

        def matmuls_in(g, has_prev):
            prev_ref, prev_g = (kvp_ref, 0) if (isinstance(g, int) and g == 0) else (kvc_ref, g - 1)
            scores, values = [], []
            for c in chunks:
                sl = slice(c * LANES, (c + 1) * LANES)
                ksl = slice(0, LANES) if gqa else sl
                vsl = slice(ksl.start + kw, ksl.stop + kw)
                kcur, vcur = _blk_load(kvc_ref, ksl, dil, g), _blk_load(kvc_ref, vsl, dil, g)
                qs = _stack_heads(_blk_load(q_ref, sl, dil, g), c, gqa)
                if has_prev:
                    kcur = jnp.concatenate([_blk_load(prev_ref, ksl, dil, prev_g), kcur], axis=0)
                    vcur = jnp.concatenate([_blk_load(prev_ref, vsl, dil, prev_g), vcur], axis=0)
                scores.append(_dot_nt(qs, kcur))
                values.append(vcur)
            return scores, values

        def tile_ops(has_prev, scores):
            lane = lax.broadcasted_iota(jnp.int32, (1, LANES), 1)
            with_diag = diag and has_prev
            upper, eye = _upper_mask(dil), _eye_mask()
            first_rows = lax.broadcasted_iota(jnp.int32, (2 * BLOCK, 1), 0) < BLOCK
            ml_blk = jnp.zeros((BLOCK, LANES), F32)
            probs = []
            for c in chunks:
                heads = _stacked_head_ids(c, gqa)
                s = scores[c]
                if has_prev:
                    s_p = s[:, :BLOCK]
                    sc = jnp.where(upper, s_p, s[:, BLOCK:])
                else:
                    sc = jnp.where(upper, NEG, s)
                if with_diag:
                    sd = jnp.where(eye, s_p, NEG)
                    m = jnp.max(jnp.maximum(sc, sd), axis=-1, keepdims=True)
                else:
                    m = jnp.max(sc, axis=-1, keepdims=True)
                if with_sinks:
                    sk = jnp.where(first_rows, sink_ref[0, heads[0]], sink_ref[0, heads[1]])
                    m = jnp.maximum(m, sk)
                p = jnp.exp(sc - m)
                zero = jnp.zeros_like(p)
                if with_diag:
                    pd = jnp.exp(sd - m)
                    l = jnp.sum(p + pd, axis=-1, keepdims=True)
                else:
                    pd = zero
                    l = jnp.sum(p, axis=-1, keepdims=True)
                if with_sinks:
                    l = l + jnp.exp(sk - m)
                pf = jnp.where(upper, zero, p)
                if has_prev:
                    pf = jnp.concatenate([jnp.where(upper, p, pd), pf], axis=1)
                probs.append(pf.astype(BF16))
                for n, h in enumerate(heads):
                    rows = slice(n * BLOCK, (n + 1) * BLOCK)
                    ml_blk = jnp.where(lane == h, m[rows], ml_blk)
                    ml_blk = jnp.where(lane == h + STAT_SHIFT, l[rows], ml_blk)
            return probs, ml_blk

        def matmuls_out(g, values, probs, ml_blk):
            for c in chunks:
                sl = slice(c * LANES, (c + 1) * LANES)
                _blk_store(o_ref, sl, _unstack_heads(_dot(probs[c], values[c]), c, gqa), dil, g)
            _blk_store(ml_ref, all_lanes, ml_blk, dil, g)

        def run(blocks):
            ins = [matmuls_in(g, has_prev) for g, has_prev in blocks]
            mids = [tile_ops(has_prev, scores) for (_, has_prev), (scores, _) in zip(blocks, ins)]
            for (g, _), (_, values), (probs, ml_blk) in zip(blocks, ins, mids):
                matmuls_out(g, values, probs, ml_blk)

        second = [(1, True)] if per_step > 1 else []

        @pl.when(pl.program_id(1) == 0)
        def _():
            run([(0, False)] + second)

        @pl.when(pl.program_id(1) > 0)
        def _():
            run([(0, True)] + second)

        if per_step > 2:
            def rest(pair, carry):
                run([(2 * pair, True), (2 * pair + 1, True)])
                return carry

            lax.fori_loop(1, per_step // 2, rest, 0)

    spec = _attn_specs(dil)
    cur = spec(per_step, lambda i: i)
    prev = spec(1, lambda i: jnp.maximum(i * per_step - 1, 0))
    in_specs = [cur(WIDTH), prev(2 * kw), cur(2 * kw)]
    args = [q, kv, kv]
    if with_sinks:
        in_specs.append(pl.BlockSpec(memory_space=pltpu.SMEM))
        args.append(sinks)
    stats = jax.ShapeDtypeStruct(q.shape[:-1] + (LANES,), F32)
    o, ml = pl.pallas_call(
        body, name=name, grid=(n_seq, nb // per_step), in_specs=in_specs,
        out_specs=(cur(WIDTH), cur(LANES)),
        out_shape=(jax.ShapeDtypeStruct(q.shape, BF16), stats),
        compiler_params=pltpu.CompilerParams(dimension_semantics=("arbitrary", "arbitrary"),
                                             vmem_limit_bytes=VMEM_LIMIT),
    )(*args)
    return _attn_unview(o, dil), _attn_unview(ml, dil)


def _attn_bwd(q, kv, do, ld, *, dil, max_dist, name, onto=None):
    q, kv, do, ld = (_attn_view(a, dil) for a in (q, kv, do, ld))
    onto = () if onto is None else tuple(_attn_view(a, dil) for a in onto)
    kw = kv.shape[-1] // 2
    gqa = kw == A_KV_WIDTH
    n_seq = dil
    nb = (q.shape[-2] * (4 if dil == 4 else 1)) // BLOCK
    n_kc = kw // LANES
    per_step = min(ATTN_BLOCKS_PER_STEP, nb)
    all_lanes = slice(0, LANES)
    assert max_dist in (BLOCK - 1, BLOCK) and nb % per_step == 0 and (per_step == 1 or per_step % 2 == 0)
    diag = max_dist == BLOCK

    def body(q_ref, kvp_ref, kvc_ref, do_ref, ld_ref, *rest_refs):
        dq_ref, dkv_ref, ck_ref, cv_ref = rest_refs[len(onto):]
        i = pl.program_id(1)

        def store(ref, sl, val, blk):
            if onto:
                val = val + _blk_load(rest_refs[0 if ref is dq_ref else 1], sl, dil, blk).astype(F32)
            _blk_store(ref, sl, val, dil, blk)

        chunks = range(WIDTH // LANES)

        def matmuls_in(g, has_prev):
            prev_ref, prev_g = (kvp_ref, 0) if (isinstance(g, int) and g == 0) else (kvc_ref, g - 1)
            operands, products = [], []
            for c in chunks:
                sl = slice(c * LANES, (c + 1) * LANES)
                kc = 0 if gqa else c
                ksl = slice(kc * LANES, (kc + 1) * LANES)
                vsl = slice(ksl.start + kw, ksl.stop + kw)
                k2, v2 = _blk_load(kvc_ref, ksl, dil, g), _blk_load(kvc_ref, vsl, dil, g)
                if has_prev:
                    k2 = jnp.concatenate([_blk_load(prev_ref, ksl, dil, prev_g), k2], axis=0)
                    v2 = jnp.concatenate([_blk_load(prev_ref, vsl, dil, prev_g), v2], axis=0)
                qs = _stack_heads(_blk_load(q_ref, sl, dil, g), c, gqa)
                dos = _stack_heads(_blk_load(do_ref, sl, dil, g), c, gqa)
                operands.append((qs, dos, k2))
                products.append((_dot_nt(qs, k2), _dot_nt(dos, v2)))
            return operands, products

        def tile_ops(g, has_prev, products):
            upper, eye = _upper_mask(dil), _eye_mask()
            ld_blk = _blk_load(ld_ref, all_lanes, dil, g)
            weights = []
            for c in chunks:
                heads = _stacked_head_ids(c, gqa)
                lse2 = _per_head_rows(ld_blk, heads)
                dl2 = _per_head_rows(ld_blk, tuple(h + STAT_SHIFT for h in heads))
                s, dp = products[c]
                if has_prev:
                    s_p, dp_p = s[:, :BLOCK], dp[:, :BLOCK]
                    sc = jnp.where(upper, s_p, s[:, BLOCK:])
                    dpc = jnp.where(upper, dp_p, dp[:, BLOCK:])
                else:
                    sc = jnp.where(upper, NEG, s)
                    dpc = dp
                p = jnp.exp(sc - lse2)
                ds = p * (dpc - dl2)
                zero = jnp.zeros_like(p)
                pf = jnp.where(upper, zero, p)
                dsf = jnp.where(upper, zero, ds)
                if has_prev:
                    if diag:
                        pd = jnp.exp(jnp.where(eye, s_p, NEG) - lse2)
                        dsd = pd * (dp_p - dl2)
                    else:
                        pd = dsd = zero
                    pf = jnp.concatenate([jnp.where(upper, p, pd), pf], axis=1)
                    dsf = jnp.concatenate([jnp.where(upper, ds, dsd), dsf], axis=1)
                weights.append((pf.astype(BF16), dsf.astype(BF16)))
            return weights

        def matmuls_out(g, has_prev, operands, weights):
            seq_blk = i * per_step + g
            dk_acc = [None] * n_kc
            dv_acc = [None] * n_kc
            for c in chunks:
                sl = slice(c * LANES, (c + 1) * LANES)
                kc = 0 if gqa else c
                qs, dos, k2 = operands[c]
                pf, dsf = weights[c]
                store(dq_ref, sl, _unstack_heads(_dot(dsf, k2), c, gqa) * SCALE, g)
                dk2 = _dot_tn(dsf, qs)
                dv2 = _dot_tn(pf, dos)
                dk_acc[kc] = dk2 if dk_acc[kc] is None else dk_acc[kc] + dk2
                dv_acc[kc] = dv2 if dv_acc[kc] is None else dv_acc[kc] + dv2
            for kc in range(n_kc):
                sl = slice(kc * LANES, (kc + 1) * LANES)
                vsl = slice(sl.start + kw, sl.stop + kw)
                if has_prev:
                    store(dkv_ref, sl, ck_ref[:, sl] + dk_acc[kc][:BLOCK], seq_blk - 1)
                    store(dkv_ref, vsl, cv_ref[:, sl] + dv_acc[kc][:BLOCK], seq_blk - 1)
                    ck_ref[:, sl] = dk_acc[kc][BLOCK:]
                    cv_ref[:, sl] = dv_acc[kc][BLOCK:]
                else:
                    ck_ref[:, sl] = dk_acc[kc]
                    cv_ref[:, sl] = dv_acc[kc]

        def run(blocks):
            ins = [matmuls_in(g, has_prev) for g, has_prev in blocks]
            mids = [tile_ops(g, has_prev, products) for (g, has_prev), (_, products) in zip(blocks, ins)]
            for (g, has_prev), (operands, _), weights in zip(blocks, ins, mids):
                matmuls_out(g, has_prev, operands, weights)

        second = [(1, True)] if per_step > 1 else []

        @pl.when(i == 0)
        def _():
            run([(0, False)] + second)

        @pl.when(i > 0)
        def _():
            run([(0, True)] + second)

        if per_step > 2:
            def rest(pair, carry):
                run([(2 * pair, True), (2 * pair + 1, True)])
                return carry

            lax.fori_loop(1, per_step // 2, rest, 0)

        @pl.when(i == nb // per_step - 1)
        def _():
            for kc in range(n_kc):
                sl = slice(kc * LANES, (kc + 1) * LANES)
                store(dkv_ref, sl, ck_ref[:, sl], nb - 1)
                store(dkv_ref, slice(sl.start + kw, sl.stop + kw), cv_ref[:, sl], nb - 1)

    spec = _attn_specs(dil)
    cur = spec(per_step, lambda i: i)
    prev = spec(1, lambda i: jnp.maximum(i * per_step - 1, 0))
    if dil == 4:
        whole = pl.BlockSpec((4, None, kv.shape[2], 2 * kw), lambda r, i: (0, r, 0, 0))
    else:
        whole = pl.BlockSpec((None, kv.shape[1], 2 * kw), lambda r, i: (r, 0, 0))
    sds = jax.ShapeDtypeStruct
    dq, dkv = pl.pallas_call(
        body, name=name, grid=(n_seq, nb // per_step),
        in_specs=[cur(WIDTH), prev(2 * kw), cur(2 * kw), cur(WIDTH), cur(LANES)] + [cur(WIDTH), whole][:len(onto)],
        out_specs=(cur(WIDTH), whole),
        out_shape=(sds(q.shape, BF16), sds(kv.shape, BF16)),
        scratch_shapes=[pltpu.VMEM((BLOCK, kw), F32), pltpu.VMEM((BLOCK, kw), F32)],
        compiler_params=pltpu.CompilerParams(dimension_semantics=("arbitrary", "arbitrary"),
                                             vmem_limit_bytes=VMEM_LIMIT),
    )(q, kv, kv, do, ld, *onto)
    return _attn_unview(dq, dil), _attn_unview(dkv, dil)


def _outproj(att_a, att_b1, att_b4, att_b16, g_a, g_b, x2, tgt2, w_out_bf, sink_row, perm):
    s = x2.shape[0]
    tm = ROW_TILE

    def body(oa_ref, mla_ref, ob1_ref, ml1_ref, ob4_ref, ml4_ref, ob16_ref, ml16_ref,
             ga_ref, gb_ref, x_ref, t_ref, w_ref, sink_ref, perm_ref,
             dy_ref, doa_ref, dob_ref, dobf_ref, dga_ref, dgb_ref, lda_ref, ldb_ref, ldbf_ref,
             gw_ref, loss_ref, dsink_ref, scr_st):
        i = pl.program_id(0)
        perm = perm_ref[...]
        lane = lax.broadcasted_iota(jnp.int32, (tm, LANES), 1)
        used = lane < HEADS

        def split(ml):
            return jnp.where(used, ml, 0.0), jnp.where(used, pltpu.roll(ml, LANES - STAT_SHIFT, 1), 1.0)

        @pl.when(i == 0)
        def _():
            gw_ref[...] = jnp.zeros_like(gw_ref)
            loss_ref[...] = jnp.zeros_like(loss_ref)
            dsink_ref[...] = jnp.zeros_like(dsink_ref)

        ms, ls = zip(split(ml1_ref[...]), split(_load_folded(ml4_ref, scr_st)), split(_load_folded(ml16_ref, scr_st)))
        mx = jnp.maximum(jnp.maximum(ms[0], ms[1]), ms[2])
        scale = [jnp.exp(mp - mx) for mp in ms]
        den = (ls[0] * scale[0] + ls[1] * scale[1]) + ls[2] * scale[2]
        lse_b = jnp.where(used, mx + jnp.log(den), 0.0)
        inv_den = 1.0 / den
        o_b = _expand_heads(scale[0] * inv_den) * ob1_ref[...].astype(F32)
        o_b = o_b + _expand_heads(scale[1] * inv_den) * _load_folded_bf16(ob4_ref, perm)
        o_b = o_b + _expand_heads(scale[2] * inv_den) * _load_folded_bf16(ob16_ref, perm)
        m_a, l_a = split(mla_ref[...])
        lse_a = jnp.where(used, m_a + jnp.log(l_a), 0.0)
        o_a = _expand_heads(1.0 / l_a) * oa_ref[...].astype(F32)
        g_a = ga_ref[...].astype(F32)
        g_b = gb_ref[...].astype(F32)
        sg_a = _sigmoid(g_a)
        sg_b = _sigmoid(g_b)
        silu_a = g_a * sg_a
        silu_b = g_b * sg_b
        mixed = jnp.concatenate([o_a * silu_a, o_b * silu_b], axis=1).astype(BF16)
        w = w_ref[...]
        y = x_ref[...] + _dot(mixed, w)
        diff = y - t_ref[...]
        loss_ref[...] += (0.5 / D_MODEL) * jnp.sum(diff * diff)
        dy = diff * (1.0 / D_MODEL)
        dy_ref[...] = dy
        dyb = dy.astype(BF16)
        gw_ref[...] += _dot_tn(mixed, dyb)
        dmixed = _dot_nt(dyb, w)
        dm_a = dmixed[:, :WIDTH]
        dm_b = dmixed[:, WIDTH:]
        do_a = dm_a * silu_a
        do_b = dm_b * silu_b
        doa_ref[...] = do_a.astype(BF16)
        dob_ref[...] = do_b.astype(BF16)
        _store_folded_bf16(dobf_ref, do_b, perm)
        dga_ref[...] = (dm_a * o_a * (sg_a * (1.0 + g_a * (1.0 - sg_a)))).astype(BF16)
        dgb_ref[...] = (dm_b * o_b * (sg_b * (1.0 + g_b * (1.0 - sg_b)))).astype(BF16)
        dl_a = _reduce_heads(do_a * o_a)
        dl_b = _reduce_heads(do_b * o_b)
        lda_ref[...] = lse_a + pltpu.roll(dl_a, STAT_SHIFT, 1)
        ld_b = lse_b + pltpu.roll(dl_b, STAT_SHIFT, 1)
        ldb_ref[...] = ld_b
        _store_folded(ldbf_ref, ld_b, scr_st)
        dsink_ref[...] -= jnp.sum(jnp.exp(sink_ref[...] - lse_a) * dl_a, axis=0, keepdims=True)

    sds = jax.ShapeDtypeStruct
    ln = s // FOLD
    natural = [_rows(WIDTH), _rows(LANES)]
    folded = [_folded_rows(WIDTH), _folded_rows(LANES)]
    return pl.pallas_call(
        body, name="outproj_fwd_bwd", grid=(s // tm,),
        in_specs=natural + natural + folded + folded
                 + [_rows(WIDTH), _rows(WIDTH), _rows(D_MODEL), _rows(D_MODEL), _whole((D_MODEL, D_MODEL)),
                    _whole((1, LANES)), _whole(perm.shape)],
        out_specs=(_rows(D_MODEL), _rows(WIDTH), _rows(WIDTH), _folded_rows(WIDTH), _rows(WIDTH), _rows(WIDTH),
                   _rows(LANES), _rows(LANES), _folded_rows(LANES),
                   _whole((D_MODEL, D_MODEL)), _whole((1, LANES)), _whole((1, LANES))),
        out_shape=(sds((s, D_MODEL), F32), sds((s, WIDTH), BF16), sds((s, WIDTH), BF16),
                   sds((FOLD, ln, WIDTH), BF16), sds((s, WIDTH), BF16), sds((s, WIDTH), BF16),
                   sds((s, LANES), F32), sds((s, LANES), F32), sds((FOLD, ln, LANES), F32),
                   sds((D_MODEL, D_MODEL), F32), sds((1, LANES), F32), sds((1, LANES), F32)),
        scratch_shapes=[_fold_scratch(LANES)],
        compiler_params=pltpu.CompilerParams(dimension_semantics=("arbitrary",), vmem_limit_bytes=VMEM_LIMIT),
    )(*att_a, *att_b1, *att_b4, *att_b16, g_a, g_b, x2, tgt2, w_out_bf, sink_row, perm)


def _inproj_bwd(x2, dy, gain, w_bf, cos, sin_s, gqa, gka, gqb, gkb, bd256, bd128, perm,
                qa_raw, ka_raw, qb_raw, kb_raw, d_a, d_b1, d_bf, dg_a, dg_b):
    s = x2.shape[0]
    tm = ROW_TILE

    def body(x_ref, dy_ref, gain_ref, w_hbm, cos_ref, sin_ref, gqa_ref, gka_ref, gqb_ref, gkb_ref, bd256_ref,
             bd128_ref, perm_ref, qa_raw_ref, ka_raw_ref, qb_raw_ref, kb_raw_ref, dqa_ref, dkva_ref,
             dq1_ref, dkv1_ref, dqf_ref, dkvf_ref, dga_ref, dgb_ref,
             gx_ref, ht_ref, win_ref,
             dgain_ref, dgqa_ref, dgka_ref, dgqb_ref, dgkb_ref, w_vmem, dproj_ref):
        i = pl.program_id(0)
        perm = perm_ref[...]

        @pl.when(i == 0)
        def _():
            pltpu.sync_copy(w_hbm, w_vmem)
            dgain_ref[...] = jnp.zeros_like(dgain_ref)
            dgqa_ref[...] = jnp.zeros_like(dgqa_ref)
            dgka_ref[...] = jnp.zeros_like(dgka_ref)
            dgqb_ref[...] = jnp.zeros_like(dgqb_ref)
            dgkb_ref[...] = jnp.zeros_like(dgkb_ref)

        cos1 = cos_ref[...]
        sin1 = sin_ref[...]
        cos4 = jnp.tile(cos1, (1, 4))
        sin4 = jnp.tile(sin1, (1, 4))

        dt, dg = _qknorm_rope_bwd(dqa_ref[...], qa_raw_ref[...], gqa_ref[...], cos4, sin4, bd256_ref[...])
        dproj_ref[:, C_QA:C_KA] = dt.astype(BF16)
        dgqa_ref[...] += jnp.sum(dg, axis=0, keepdims=True)
        dt, dg = _qknorm_rope_bwd(dkva_ref[:, :A_KV_WIDTH], ka_raw_ref[...], gka_ref[...], cos1, sin1,
                                  bd128_ref[...])
        dproj_ref[:, C_KA:C_VA] = dt.astype(BF16)
        dgka_ref[...] += jnp.sum(dg, axis=0, keepdims=True)
        dproj_ref[:, C_VA:C_GA] = dkva_ref[:, A_KV_WIDTH:]
        dproj_ref[:, C_GA:C_QB] = dga_ref[...]
        dq = dq1_ref[...].astype(F32) + _load_folded_bf16(dqf_ref, perm)
        dt, dg = _qknorm_rope_bwd(dq, qb_raw_ref[...], gqb_ref[...], cos4, sin4, bd256_ref[...])
        dproj_ref[:, C_QB:C_KB] = dt.astype(BF16)
        dgqb_ref[...] += jnp.sum(dg, axis=0, keepdims=True)
        dkv = dkv1_ref[...].astype(F32) + _load_folded_bf16(dkvf_ref, perm)
        dt, dg = _qknorm_rope_bwd(dkv[:, :WIDTH], kb_raw_ref[...], gkb_ref[...], cos4, sin4, bd256_ref[...])
        dproj_ref[:, C_KB:C_VB] = dt.astype(BF16)
        dgkb_ref[...] += jnp.sum(dg, axis=0, keepdims=True)
        dproj_ref[:, C_VB:C_GB] = dkv[:, WIDTH:].astype(BF16)
        dproj_ref[:, C_GB:C_END] = dgb_ref[...]
        for k, start in enumerate(WIN_START):
            win_ref[k] = dproj_ref[:, start:start + WIN]

        xt = x_ref[...]
        gain_row = gain_ref[...]
        r = lax.rsqrt(jnp.mean(xt * xt, axis=-1, keepdims=True) + EPS)
        xr = xt * r
        ht_ref[...] = (xr * gain_row).T.astype(BF16)
        dh = _dot(dproj_ref[...], w_vmem[...])
        dgain_ref[...] += jnp.sum(dh * xr, axis=0, keepdims=True)
        u = dh * gain_row
        gx_ref[...] = dy_ref[...] + r * (u - xr * jnp.mean(u * xr, axis=-1, keepdims=True))

    def acc_row(w):
        return pl.BlockSpec((1, w), lambda i: (0, 0))

    sds = jax.ShapeDtypeStruct
    any_spec = pl.BlockSpec(memory_space=pl.ANY)
    win_spec = pl.BlockSpec((N_CHIP, tm, WIN), lambda i: (0, i, 0))
    return pl.pallas_call(
        body, name="inproj_bwd", grid=(s // tm,),
        in_specs=[_rows(D_MODEL), _rows(D_MODEL), _whole(gain.shape), any_spec, _rows(LANES), _rows(LANES),
                  _whole(gqa.shape), _whole(gka.shape), _whole(gqb.shape), _whole(gkb.shape), _whole(bd256.shape),
                  _whole(bd128.shape), _whole(perm.shape),
                  _rows(WIDTH), _rows(A_KV_WIDTH), _rows(WIDTH), _rows(WIDTH),
                  _rows(WIDTH), _rows(2 * A_KV_WIDTH), _rows(WIDTH), _rows(2 * WIDTH)]
                 + [_folded_rows(WIDTH), _folded_rows(2 * WIDTH), _rows(WIDTH), _rows(WIDTH)],
        out_specs=(_rows(D_MODEL), pl.BlockSpec((D_MODEL, tm), lambda i: (0, i)), win_spec, acc_row(D_MODEL), acc_row(WIDTH), acc_row(A_KV_WIDTH), acc_row(WIDTH), acc_row(WIDTH)),
        out_shape=(sds((s, D_MODEL), F32), sds((D_MODEL, s), BF16), sds((N_CHIP, s, WIN), BF16),
                   sds((1, D_MODEL), F32),
                   sds((1, WIDTH), F32), sds((1, A_KV_WIDTH), F32), sds((1, WIDTH), F32), sds((1, WIDTH), F32)),
        scratch_shapes=[pltpu.VMEM((IN_WIDTH, D_MODEL), BF16), pltpu.VMEM((tm, IN_WIDTH), BF16)],
        compiler_params=pltpu.CompilerParams(dimension_semantics=("arbitrary",), vmem_limit_bytes=VMEM_LIMIT),
    )(x2, dy, gain, w_bf, cos, sin_s, gqa, gka, gqb, gkb, bd256, bd128, perm, qa_raw, ka_raw, qb_raw, kb_raw,
      *d_a, *d_b1, *d_bf, dg_a, dg_b)


def _rope_tables(s):
    half = HEAD_DIM // 2
    inv = jnp.tile(ROPE_THETA ** (-jnp.arange(half, dtype=F32) / half), 4)
    sign = jnp.tile(jnp.concatenate([-jnp.ones((half,), F32), jnp.ones((half,), F32)]), 2)
    hi = (jnp.arange(s // ROPE_SPLIT) * ROPE_SPLIT).astype(F32)[:, None] * inv[None, :]
    lo = jnp.arange(ROPE_SPLIT).astype(F32)[:, None] * inv[None, :]
    ch, sh, cl, sl = jnp.cos(hi)[:, None, :], jnp.sin(hi)[:, None, :], jnp.cos(lo)[None], jnp.sin(lo)[None]
    cos = (ch * cl - sh * sl).reshape(s, LANES)
    sin = (sh * cl + ch * sl).reshape(s, LANES)
    return cos, sin * sign[None, :]


def _block_diag_ones(w):
    idx = jnp.arange(w) // HEAD_DIM
    return (idx[:, None] == idx[None, :]).astype(BF16)


def _local_step(x2, tgt2, norm_gain, w_in_bf, q_norm_a, k_norm_a, sinks_a, q_norm_b, k_norm_b, w_out_blk):
    s = x2.shape[0]
    cos, sin_s = _rope_tables(s)
    bd256, bd128 = _block_diag_ones(2 * LANES), _block_diag_ones(A_KV_WIDTH)
    gqa = jnp.tile(q_norm_a, (1, HEADS))
    gka = jnp.tile(k_norm_a, (1, 2))
    gqb = jnp.tile(q_norm_b, (1, HEADS))
    gkb = jnp.tile(k_norm_b, (1, HEADS))
    sink_row = jnp.pad(sinks_a, ((0, 0), (0, LANES - HEADS)))
    perm = _fold_matrix()

    (qa, kva, qb, kvb, qbf, kvbf, qa_raw, ka_raw, g_a, qb_raw, kb_raw, g_b, w_out_bf) = _inproj(
        x2, norm_gain, w_in_bf, cos, sin_s, gqa, gka, gqb, gkb, bd256, bd128, w_out_blk)

    att_a = _attn_fwd(qa, kva, sinks_a, dil=1, max_dist=A_MAX_DIST, name="attn_a_fwd")
    att_b1 = _attn_fwd(qb, kvb, None, dil=1, max_dist=B_MAX_DIST, name="attn_b1_fwd")
    att_b4 = _attn_fwd(qbf, kvbf, None, dil=4, max_dist=B_MAX_DIST, name="attn_b4_fwd")
    att_b16 = _attn_fwd(qbf, kvbf, None, dil=16, max_dist=B_MAX_DIST, name="attn_b16_fwd")

    (dy, do_a, do_b, do_bf, dg_a, dg_b, ld_a, ld_b, ld_bf, gw_out, loss_part, dsink) = _outproj(
        att_a, att_b1, att_b4, att_b16, g_a, g_b, x2, tgt2, w_out_bf, sink_row, perm)

    d_a = _attn_bwd(qa, kva, do_a, ld_a, dil=1, max_dist=A_MAX_DIST, name="attn_a_bwd")
    d_b1 = _attn_bwd(qb, kvb, do_b, ld_b, dil=1, max_dist=B_MAX_DIST, name="attn_b1_bwd")
    d_b4 = _attn_bwd(qbf, kvbf, do_bf, ld_bf, dil=4, max_dist=B_MAX_DIST, name="attn_b4_bwd")
    d_b16 = _attn_bwd(qbf, kvbf, do_bf, ld_bf, dil=16, max_dist=B_MAX_DIST, name="attn_b16_bwd", onto=d_b4)

    gx, h_t, wins, dgain, dgqa, dgka, dgqb, dgkb = _inproj_bwd(
        x2, dy, norm_gain, w_in_bf, cos, sin_s, gqa, gka, gqb, gkb, bd256, bd128, perm,
        qa_raw, ka_raw, qb_raw, kb_raw, d_a, d_b1, d_b16, dg_a, dg_b)
    return loss_part, gx, h_t, wins, gw_out, (dgain, dgqa, dgka, dsink, dgqb, dgkb)


def _position():
    return lax.axis_index("x"), lax.axis_index("y"), lax.axis_index("c")


GATHER_CHUNKS = 2


def _gather_weights(blocks, name):
    n = len(blocks)
    ch = GATHER_CHUNKS

    def body(*refs):
        src_refs, dst_refs = refs[:n], refs[n:2 * n]
        ici_send, ici_recv, hop_send, hop_recv, d2d_send, d2d_recv = refs[2 * n:]
        x, y, c = _position()
        b = 2 * x + y
        via = 2 - c
        out = 3 - via
        for k in range(n):
            dst_refs[k][b] = src_refs[k][...].astype(BF16)

        def rows(k, core, j):
            half = blocks[k].shape[0] // 2
            return pl.ds(pl.multiple_of(core * half + j * (half // ch), half // ch), half // ch)

        def chip(rel):
            return x ^ (rel >> 1), y ^ (rel & 1)

        def ici(k, j, slot, rel, send_sems, recv_sems, sem):
            px, py = chip(rel)
            piece = dst_refs[k].at[slot, rows(k, c, j)]
            return pltpu.make_async_remote_copy(src_ref=piece, dst_ref=piece, send_sem=send_sems.at[sem],
                                                recv_sem=recv_sems.at[sem], device_id=(px, py, c),
                                                device_id_type=MESH)

        def direct(k, j, slot, rel):
            return ici(k, j, slot, rel, ici_send, ici_recv, ((rel - 1) * ch + j) * n + k)

        def hop(k, j, slot, rel):
            return ici(k, j, slot, rel, hop_send, hop_recv, j * n + k)

        def d2d(k, j, rel, core):
            piece = dst_refs[k].at[b ^ rel, rows(k, core, j)]
            sem = ((rel - 1) * ch + j) * n + k
            return pltpu.make_async_remote_copy(src_ref=piece, dst_ref=piece, send_sem=d2d_send.at[sem],
                                                recv_sem=d2d_recv.at[sem], device_id=(x, y, 1 - c),
                                                device_id_type=MESH)

        pieces = [(k, j) for j in range(ch) for k in range(n)]
        for k, j in pieces:
            for rel in (1, 2):
                direct(k, j, b, rel).start()
        for k, j in pieces:
            direct(k, j, b ^ via, via).wait_recv()
            hop(k, j, b ^ via, out).start()
            d2d(k, j, via, c).start()
        for k, j in pieces:
            direct(k, j, b ^ out, out).wait_recv()
            d2d(k, j, out, c).start()
        for k, j in pieces:
            hop(k, j, b ^ 3, via).wait_recv()
            d2d(k, j, 3, c).start()
        for k, j in pieces:
            for rel in (1, 2, 3):
                d2d(k, j, rel, 1 - c).wait_recv()
        for k, j in pieces:
            for rel in (1, 2):
                direct(k, j, b, rel).wait_send()
            hop(k, j, b ^ via, out).wait_send()
            d2d(k, j, via, c).wait_send()
            d2d(k, j, out, c).wait_send()
            d2d(k, j, 3, c).wait_send()

    vmem_spec = pl.BlockSpec(memory_space=pltpu.VMEM)
    dma = pltpu.SemaphoreType.DMA
    out_shape = tuple(jax.ShapeDtypeStruct((N_CHIP,) + a.shape, BF16) for a in blocks)
    return pl.pallas_call(
        body, name=name, in_specs=[vmem_spec] * n, out_specs=tuple([vmem_spec] * n), out_shape=out_shape,
        scratch_shapes=[dma((2 * ch * n,)), dma((2 * ch * n,)), dma((ch * n,)), dma((ch * n,)),
                        dma((3 * ch * n,)), dma((3 * ch * n,))],
        compiler_params=pltpu.CompilerParams(vmem_limit_bytes=VMEM_LIMIT),
    )(*blocks)


def _grad_reduce(order, h_t, wins, gw_out, small):
    s = h_t.shape[1]
    tk = GRAD_ROWS
    n_i = s // tk
    half = D_MODEL // 2
    o_half = OUT_ROWS // 2
    n_rel = N_CHIP - 1

    def body(order_ref, ht_ref, win_ref, gwo_ref, small_ref,
             win_out, wout_out, small_out,
             acc, mine, s1, r1, s2, r2, so1, ro1, so2, ro2, pair_in, pair_o, small_land,
             s1_send, s1_recv, s2_send, s2_recv, o1_send, o1_recv, o2_send, o2_recv,
             pair_send, pair_recv, small_send, small_recv):
        j = pl.program_id(0)
        i = pl.program_id(1)
        x, y, c = _position()
        me = 4 * x + 2 * y + c
        sibling = (x, y, 1 - c)
        my_rows = pl.ds(pl.multiple_of(c * half, half), half)
        sib_rows = pl.ds(pl.multiple_of((1 - c) * half, half), half)

        def chip_of(rel):
            return x ^ (rel >> 1), y ^ (rel & 1)

        def level1(k):
            return pltpu.make_async_remote_copy(src_ref=s1.at[k], dst_ref=r1.at[k], send_sem=s1_send.at[k],
                                                recv_sem=s1_recv.at[k], device_id=sibling, device_id_type=MESH)

        def level2(k):
            px, py = chip_of(RELATIONS[k])
            return pltpu.make_async_remote_copy(src_ref=s2.at[k], dst_ref=r2.at[k], send_sem=s2_send.at[k],
                                                recv_sem=s2_recv.at[k], device_id=(px, py, c), device_id_type=MESH)

        def out_level1(bk):
            return pltpu.make_async_remote_copy(src_ref=so1.at[bk], dst_ref=ro1.at[bk], send_sem=o1_send.at[bk],
                                                recv_sem=o1_recv.at[bk], device_id=sibling, device_id_type=MESH)

        def out_level2(k):
            px, py = chip_of(RELATIONS[k])
            return pltpu.make_async_remote_copy(src_ref=so2.at[k], dst_ref=ro2.at[k], send_sem=o2_send.at[k],
                                                recv_sem=o2_recv.at[k], device_id=(px, py, c), device_id_type=MESH)

        def small_copy(d):
            px, py, pc = x ^ (d >> 2), y ^ ((d >> 1) & 1), c ^ (d & 1)
            return pltpu.make_async_remote_copy(src_ref=small_ref, dst_ref=small_land.at[me],
                                                send_sem=small_send.at[d], recv_sem=small_recv.at[d],
                                                device_id=(px, py, pc), device_id_type=MESH)

        def pair_copy(k, buf):
            return pltpu.make_async_remote_copy(src_ref=buf.at[0], dst_ref=buf.at[1], send_sem=pair_send.at[k],
                                                recv_sem=pair_recv.at[k], device_id=sibling, device_id_type=MESH)

        def out_rows(bk, core):
            return pl.ds(pl.multiple_of(bk * OUT_ROWS + core * o_half, o_half), o_half)

        @pl.when((j == 0) & (i == 0))
        def _():
            for d in range(1, N_DEV):
                small_copy(d).start()
            small_land[me] = small_ref[...]
            for bk in range(N_CHIP):
                so1[bk] = gwo_ref[out_rows(bk, 1 - c), :].astype(BF16)
                out_level1(bk).start()

        @pl.when((j == 0) & (i == 1))
        def _():
            b = 2 * x + y
            for bk in range(N_CHIP):
                out_level1(bk).wait_recv()
            for k in range(n_rel):
                px, py = chip_of(RELATIONS[k])
                bk = 2 * px + py
                so2[k] = (gwo_ref[out_rows(bk, c), :] + ro1[bk].astype(F32)).astype(BF16)
                out_level2(k).start()

        @pl.when(i == 0)
        def _():
            acc[...] = jnp.zeros_like(acc)

        for n0 in range(0, WIN, ACC_COLS):
            n1 = min(n0 + ACC_COLS, WIN)
            acc[:, n0:n1] += _dot(ht_ref[...], win_ref[:, n0:n1])

        for k in range(N_CHIP):
            @pl.when((j == k) & (i == n_i - 1))
            def _(k=k):
                s1[k] = acc[sib_rows, :].astype(BF16)
                level1(k).start()
                mine[...] = acc[my_rows, :]

            if k < n_rel:
                @pl.when((j == k + 1) & (i == 1))
                def _(k=k):
                    level1(k).wait_recv()
                    s2[k] = (mine[...] + r1[k].astype(F32)).astype(BF16)
                    level2(k).start()

        @pl.when((j == N_CHIP - 1) & (i == n_i - 1))
        def _():
            b = 2 * x + y
            level1(N_CHIP - 1).wait_recv()
            total = mine[...] + r1[N_CHIP - 1].astype(F32)
            for k in range(n_rel):
                level2(k).wait_recv()
                total = total + r2[k].astype(F32)
            total = total.T
            pair_in[0] = total
            pair_copy(0, pair_in).start()
            total_o = gwo_ref[out_rows(b, c), :] + ro1[b].astype(F32)
            for k in range(n_rel):
                out_level2(k).wait_recv()
                total_o = total_o + ro2[k].astype(F32)
            pair_o[0] = total_o
            pair_copy(1, pair_o).start()
            for core in range(2):
                @pl.when(c == core)
                def _(core=core):
                    win_out[:, core * half:(core + 1) * half] = total
            wout_out[c] = total_o
            for d in range(1, N_DEV):
                small_copy(d).wait_recv()
            small_out[...] = small_land[...]
            pair_copy(0, pair_in).wait_recv()
            for core in range(2):
                @pl.when(c == core)
                def _(core=core):
                    win_out[:, (1 - core) * half:(2 - core) * half] = pair_in[1]
            pair_copy(1, pair_o).wait_recv()
            wout_out[1 - c] = pair_o[1]
            for d in range(1, N_DEV):
                small_copy(d).wait_send()
            for k in range(N_CHIP):
                level1(k).wait_send()
                out_level1(k).wait_send()
            for k in range(n_rel):
                level2(k).wait_send()
                out_level2(k).wait_send()
            pair_copy(0, pair_in).wait_send()
            pair_copy(1, pair_o).wait_send()

    vmem = pl.BlockSpec(memory_space=pltpu.VMEM)
    dma = pltpu.SemaphoreType.DMA
    sds = jax.ShapeDtypeStruct
    grid_spec = pltpu.PrefetchScalarGridSpec(
        num_scalar_prefetch=1, grid=(N_CHIP, n_i),
        in_specs=[pl.BlockSpec((D_MODEL, tk), lambda j, i, order: (0, i)),
                  pl.BlockSpec((None, tk, WIN), lambda j, i, order: (order[j], i, 0)), vmem, vmem],
        out_specs=(vmem, vmem, vmem),
        scratch_shapes=[
            pltpu.VMEM((D_MODEL, WIN), F32), pltpu.VMEM((half, WIN), F32),
            pltpu.VMEM((N_CHIP, half, WIN), BF16), pltpu.VMEM((N_CHIP, half, WIN), BF16),
            pltpu.VMEM((n_rel, half, WIN), BF16), pltpu.VMEM((n_rel, half, WIN), BF16),
            pltpu.VMEM((N_CHIP, o_half, D_MODEL), BF16), pltpu.VMEM((N_CHIP, o_half, D_MODEL), BF16),
            pltpu.VMEM((n_rel, o_half, D_MODEL), BF16), pltpu.VMEM((n_rel, o_half, D_MODEL), BF16),
            pltpu.VMEM((2, WIN, half), F32), pltpu.VMEM((2, o_half, D_MODEL), F32),
            pltpu.VMEM((N_DEV, PACK_ROWS, D_MODEL), F32),
            dma((N_CHIP,)), dma((N_CHIP,)), dma((n_rel,)), dma((n_rel,)),
            dma((N_CHIP,)), dma((N_CHIP,)), dma((n_rel,)), dma((n_rel,)),
            dma((2,)), dma((2,)), dma((N_DEV,)), dma((N_DEV,))])
    return pl.pallas_call(
        body, name="grad_w_in_reduce", grid_spec=grid_spec,
        out_shape=(sds((WIN, D_MODEL), F32), sds((2, o_half, D_MODEL), F32), sds((N_DEV, PACK_ROWS, D_MODEL), F32)),
        compiler_params=pltpu.CompilerParams(dimension_semantics=("arbitrary", "arbitrary"),
                                             vmem_limit_bytes=VMEM_LIMIT),
    )(order, h_t, wins, gw_out, small)


ADAM_STEPS = 4


def _adamw_math(w, g, m, v):
    m = ADAM_B1 * m + (1.0 - ADAM_B1) * g
    v = ADAM_B2 * v + (1.0 - ADAM_B2) * (g * g)
    m_hat = m / (1.0 - ADAM_B1 ** ADAM_STEP)
    v_hat = v / (1.0 - ADAM_B2 ** ADAM_STEP)
    delta = -ADAM_LR * (m_hat / (jnp.sqrt(v_hat) + ADAM_EPS) + ADAM_WD * w)
    return delta, m, v


def _adamw(w, g, m, v, name):
    r, c = w.shape

    def body(w_ref, g_ref, m_ref, v_ref, d_ref, nm_ref, nv_ref):
        delta, nm, nv = _adamw_math(w_ref[...], g_ref[...], m_ref[...], v_ref[...])
        d_ref[...] = delta
        nm_ref[...] = nm
        nv_ref[...] = nv

    rows = r // ADAM_STEPS
    assert rows * ADAM_STEPS == r and rows % 8 == 0
    spec = pl.BlockSpec((rows, c), lambda i: (i, 0))
    shape = jax.ShapeDtypeStruct((r, c), F32)
    return pl.pallas_call(
        body, name=name, grid=(ADAM_STEPS,), in_specs=[spec] * 4, out_specs=(spec,) * 3,
        out_shape=(shape,) * 3, compiler_params=pltpu.CompilerParams(vmem_limit_bytes=VMEM_LIMIT),
    )(w, g, m, v)


def _adamw_window(w, window, shift, m, v, name):
    r, c = w.shape
    rows = r // ADAM_STEPS
    assert rows * ADAM_STEPS == r and rows % 8 == 0

    def body(shift_ref, w_ref, win_hbm, m_ref, v_ref, g_ref, d_ref, nm_ref, nv_ref, g_vmem):
        start = pl.multiple_of(shift_ref[0] + pl.program_id(0) * rows, 8)
        pltpu.sync_copy(win_hbm.at[pl.ds(start, rows)], g_vmem)
        g = g_vmem[...]
        g_ref[...] = g
        delta, nm, nv = _adamw_math(w_ref[...], g, m_ref[...], v_ref[...])
        d_ref[...] = delta
        nm_ref[...] = nm
        nv_ref[...] = nv

    spec = pl.BlockSpec((rows, c), lambda i, shift_ref: (i, 0))
    shape = jax.ShapeDtypeStruct((r, c), F32)
    grid_spec = pltpu.PrefetchScalarGridSpec(
        num_scalar_prefetch=1, grid=(ADAM_STEPS,),
        in_specs=[spec, pl.BlockSpec(memory_space=pl.ANY), spec, spec], out_specs=(spec,) * 4,
        scratch_shapes=[pltpu.VMEM((rows, c), F32)])
    return pl.pallas_call(
        body, name=name, grid_spec=grid_spec, out_shape=(shape,) * 4,
        compiler_params=pltpu.CompilerParams(vmem_limit_bytes=VMEM_LIMIT),
    )(shift, w, window, m, v)


PACK_ROWS = 8


def _fold_heads(v):
    y = v[:, 0:LANES]
    for j in range(1, v.shape[1] // LANES):
        y = y + v[:, j * LANES:(j + 1) * LANES]
    return y + pltpu.roll(y, HEAD_DIM, 1)


N_SMALL = 6


def _small_adamw(recv, weights, m, v):
    def body(*refs):
        r_ref = refs[0]
        w_refs, m_refs, v_refs = (refs[1 + n * N_SMALL:1 + (n + 1) * N_SMALL] for n in range(3))
        outs = refs[1 + 3 * N_SMALL:]
        g_refs, d_refs, nm_refs, nv_refs = (outs[n * N_SMALL:(n + 1) * N_SMALL] for n in range(4))
        loss_ref = outs[4 * N_SMALL]
        tot = r_ref[0]
        for j in range(1, N_DEV):
            tot = tot + r_ref[j]
        loss_ref[...] = tot[3:4, 0:LANES]
        row1 = tot[1:2, :]
        row2 = tot[2:3, :]
        grads = [tot[0:1, :],
                 _fold_heads(row1[:, 0:WIDTH])[:, :HEAD_DIM],
                 _fold_heads(row2[:, WIDTH:WIDTH + A_KV_WIDTH])[:, :HEAD_DIM],
                 row2[:, WIDTH + A_KV_WIDTH:WIDTH + A_KV_WIDTH + HEADS],
                 _fold_heads(row1[:, WIDTH:2 * WIDTH])[:, :HEAD_DIM],
                 _fold_heads(row2[:, 0:WIDTH])[:, :HEAD_DIM]]
        for n, g in enumerate(grads):
            g_refs[n][...] = g
            delta, nm, nv = _adamw_math(w_refs[n][...], g, m_refs[n][...], v_refs[n][...])
            d_refs[n][...] = delta
            nm_refs[n][...] = nm
            nv_refs[n][...] = nv

    shapes = tuple(jax.ShapeDtypeStruct(a.shape, F32) for a in weights)
    outs = pl.pallas_call(body, name="small_adamw", out_shape=shapes * 4 + (jax.ShapeDtypeStruct((1, LANES), F32),)
                          )(recv, *weights, *m, *v)
    return tuple(outs[n * N_SMALL:(n + 1) * N_SMALL] for n in range(4)) + (outs[4 * N_SMALL],)


def kernel(x, norm_gain, w_in, q_norm_a, k_norm_a, sinks_a, q_norm_b, k_norm_b, w_out, loss_target, m_norm_gain, m_w_in, m_q_norm_a, m_k_norm_a, m_sinks_a, m_q_norm_b, m_k_norm_b, m_w_out, v_norm_gain, v_w_in, v_q_norm_a, v_k_norm_a, v_sinks_a, v_q_norm_b, v_k_norm_b, v_w_out):
    chip = 2 * lax.axis_index("x") + lax.axis_index("y")

    w_in_t, m_w_in_t, v_w_in_t = w_in[0].T, m_w_in[0].T, v_w_in[0].T

    (w_in_all,) = _gather_weights([w_in_t], "gather_weights")
    w_in_bf = w_in_all.reshape(IN_WIDTH, D_MODEL)

    loss_part, gx, h_t, wins, gw_out, (dgain, dgqa, dgka, dsink, dgqb, dgkb) = _local_step(
        x[0], loss_target[0], norm_gain, w_in_bf, q_norm_a, k_norm_a, sinks_a, q_norm_b, k_norm_b, w_out[0])

    small = jnp.concatenate([
        dgain, jnp.concatenate([dgqa, dgqb], axis=1),
        jnp.concatenate([dgkb, dgka, dsink, jnp.zeros((1, D_MODEL - WIDTH - 2 * A_KV_WIDTH), F32)], axis=1),
        jnp.pad(loss_part, ((0, 0), (0, D_MODEL - LANES))),
        jnp.zeros((PACK_ROWS - 4, D_MODEL), F32)], axis=0)
    order = (chip ^ jnp.array(RELATIONS, jnp.int32)).astype(jnp.int32)
    win_sum, wout_sum, small_recv = _grad_reduce(order, h_t, wins, gw_out, small)
    shift = jnp.array(WIN_SHIFT, jnp.int32)[chip].reshape(1)
    g_w_out = wout_sum.reshape(OUT_ROWS, D_MODEL)

    g_w_in, d_w_in, nm_w_in, nv_w_in = (
        a.T for a in _adamw_window(w_in_t, win_sum, shift, m_w_in_t, v_w_in_t, "adamw_w_in"))
    d_w_out, nm_w_out, nv_w_out = _adamw(w_out[0], g_w_out, m_w_out[0], v_w_out[0], "adamw_w_out")
    g_s, d_s, nm_s, nv_s, loss_row = _small_adamw(
        small_recv,
        (norm_gain, q_norm_a, k_norm_a, sinks_a, q_norm_b, k_norm_b),
        (m_norm_gain, m_q_norm_a, m_k_norm_a, m_sinks_a, m_q_norm_b, m_k_norm_b),
        (v_norm_gain, v_q_norm_a, v_k_norm_a, v_sinks_a, v_q_norm_b, v_k_norm_b))
    loss = loss_row[0, 0]

    def leaves(small_ones, big_in, big_out):
        return (small_ones[0], big_in[None]) + tuple(small_ones[1:]) + (big_out[None],)

    return ((loss, gx[None]) + leaves(g_s, g_w_in, g_w_out) + leaves(d_s, d_w_in, d_w_out)
            + leaves(nm_s, nm_w_in, nm_w_out) + leaves(nv_s, nv_w_in, nv_w_out))
```

```python
import jax
import jax.numpy as jnp
from jax import lax
from jax.experimental import pallas as pl
from jax.experimental.pallas import tpu as pltpu

F32 = jnp.float32
BF16 = jnp.bfloat16

D_MODEL = 1024
HEAD_DIM = 64
HEADS = 8
WIDTH = HEADS * HEAD_DIM
A_KV_WIDTH = 2 * HEAD_DIM
BLOCK = 128
LANES = 128
FOLD = 16
A_MAX_DIST = 127
B_MAX_DIST = 128
ROPE_THETA = 10000.0
ROPE_SPLIT = 64
EPS = 1e-6
NEG = -1e30
SCALE = HEAD_DIM ** -0.5

IN_WIDTH = 3328
C_QA, C_KA, C_VA, C_GA, C_QB, C_KB, C_VB, C_GB, C_END = 0, 512, 640, 768, 1280, 1792, 2304, 2816, 3328

N_DEV = 8
N_CHIP = 4
MESH = pl.DeviceIdType.MESH
WIN = 896
WIN_START = (0, 768, 1664, 2432)
WIN_SHIFT = (0, 64, 0, 64)
OUT_ROWS = D_MODEL // N_CHIP
RELATIONS = (3, 1, 2, 0)

ADAM_LR = 0.001
ADAM_B1 = 0.9
ADAM_B2 = 0.999
ADAM_EPS = 1e-08
ADAM_WD = 0.01
ADAM_STEP = 10

ROW_TILE = 256
FOLD_ROWS = ROW_TILE // FOLD
GRAD_ROWS = 1024
ACC_COLS = 256
VMEM_LIMIT = 56 * 1024 * 1024


def _dot(a, b):
    return jnp.dot(a, b, preferred_element_type=F32)


def _dot_nt(a, b):
    return lax.dot_general(a, b, (((1,), (1,)), ((), ())), preferred_element_type=F32)


def _dot_tn(a, b):
    return lax.dot_general(a, b, (((0,), (0,)), ((), ())), preferred_element_type=F32)


def _head_sum(z, bd):
    w = bd.shape[0]
    zb = z.astype(BF16)
    parts = [_dot(zb[:, a:a + w], bd) for a in range(0, z.shape[1], w)]
    return parts[0] if len(parts) == 1 else jnp.concatenate(parts, axis=1)


def _swap_halves(t):
    w = t.shape[1]
    lane = lax.broadcasted_iota(jnp.int32, t.shape, 1)
    return jnp.where(lane % HEAD_DIM < HEAD_DIM // 2, pltpu.roll(t, w - 32, 1), pltpu.roll(t, 32, 1))


def _qknorm_rope(t, g, cos, sin_s, bd):
    r = lax.rsqrt(_head_sum(t * t, bd) * (1.0 / HEAD_DIM) + EPS)
    n = (t * r) * g
    return n * cos + _swap_halves(n) * sin_s


def _qknorm_rope_bwd(dout, t, g, cos, sin_s, bd):
    dout, t = dout.astype(F32), t.astype(F32)
    dn = dout * cos + _swap_halves(dout * sin_s)
    r = lax.rsqrt(_head_sum(t * t, bd) * (1.0 / HEAD_DIM) + EPS)
    tr = t * r
    u = dn * g
    dt = r * (u - tr * (_head_sum(u * tr, bd) * (1.0 / HEAD_DIM)))
    return dt, dn * tr


def _sigmoid(g):
    return 1.0 / (1.0 + jnp.exp(-g))


def _expand_heads(st):
    t = st.shape[0]
    lane = lax.broadcasted_iota(jnp.int32, (t, LANES), 1)
    chunks = []
    for c in range(WIDTH // LANES):
        chunks.append(jnp.where(lane < HEAD_DIM, st[:, 2 * c:2 * c + 1], st[:, 2 * c + 1:2 * c + 2]))
    return jnp.concatenate(chunks, axis=1)


def _reduce_heads(z):
    t = z.shape[0]
    lane = lax.broadcasted_iota(jnp.int32, (t, LANES), 1)
    out = jnp.zeros((t, LANES), F32)
    for c in range(WIDTH // LANES):
        zc = z[:, c * LANES:(c + 1) * LANES]
        for ph in range(2):
            s = jnp.sum(jnp.where((lane // HEAD_DIM) == ph, zc, 0.0), axis=-1, keepdims=True)
            out = jnp.where(lane == 2 * c + ph, s, out)
    return out


def _fold_scratch(w):
    return pltpu.VMEM((w // LANES, ROW_TILE, LANES), F32)


def _store_folded(out_ref, val, scr, col0=0):
    w = val.shape[1]
    n = w // LANES
    for c in range(n):
        scr[c] = val[:, c * LANES:(c + 1) * LANES]
    for r in range(FOLD):
        piece = [scr[c, pl.ds(r, FOLD_ROWS, stride=FOLD), :] for c in range(n)]
        out_ref[r, :, col0:col0 + w] = (piece[0] if n == 1 else jnp.concatenate(piece, axis=1)).astype(out_ref.dtype)


def _load_folded(in_ref, scr):
    n = in_ref.shape[2] // LANES
    for r in range(FOLD):
        blk = in_ref[r].astype(F32)
        for c in range(n):
            scr[c, pl.ds(r, FOLD_ROWS, stride=FOLD), :] = blk[:, c * LANES:(c + 1) * LANES]
    return scr[0] if n == 1 else jnp.concatenate([scr[c] for c in range(n)], axis=1)


def _fold_matrix():
    f = jnp.arange(ROW_TILE)
    return (jnp.arange(ROW_TILE)[None, :] == (FOLD * (f % FOLD_ROWS) + f // FOLD_ROWS)[:, None]).astype(BF16)


def _store_folded_bf16(out_ref, val, perm):
    folded = _dot(perm, val.astype(BF16)).astype(out_ref.dtype)
    for r in range(FOLD):
        out_ref[r] = folded[r * FOLD_ROWS:(r + 1) * FOLD_ROWS]


def _load_folded_bf16(in_ref, perm):
    blk = jnp.concatenate([in_ref[r] for r in range(FOLD)], axis=0)
    return _dot(perm, blk)


def _rows(w, tm=ROW_TILE):
    return pl.BlockSpec((tm, w), lambda i: (i, 0))


def _folded_rows(w):
    return pl.BlockSpec((FOLD, FOLD_ROWS, w), lambda i: (0, i, 0))


def _whole(shape):
    return pl.BlockSpec(shape, lambda i: (0,) * len(shape))


def _inproj(x2, gain, w_bf, cos, sin_s, gqa, gka, gqb, gkb, bd256, bd128, w_out_blk):
    s = x2.shape[0]
    tm = ROW_TILE
    n_steps = s // tm
    n_rel = N_CHIP - 1
    o_half = OUT_ROWS // 2

    def body(x_ref, gain_ref, w_hbm, cos_ref, sin_ref, gqa_ref, gka_ref, gqb_ref, gkb_ref, bd256_ref, bd128_ref,
             wout_ref, qa_ref, kva_ref, qb_ref, kvb_ref, qbf_ref, kvbf_ref,
             qa_raw_ref, ka_raw_ref, ga_ref, qb_raw_ref, kb_raw_ref, gb_ref, wout_all_ref,
             w_vmem, scr, land, ici_send, ici_recv, d2d_send, d2d_recv):
        i = pl.program_id(0)
        px_, py_, c = _position()
        b = 2 * px_ + py_

        def piece(chip_idx, core):
            return land.at[chip_idx, pl.ds(pl.multiple_of(core * o_half, o_half), o_half)]

        def other_chip(d):
            ox, oy = px_ ^ (d >> 1), py_ ^ (d & 1)
            return ox, oy, 2 * ox + oy

        def ici_copy(d, chip_idx):
            ox, oy, _ = other_chip(d)
            return pltpu.make_async_remote_copy(
                src_ref=piece(chip_idx, c), dst_ref=piece(chip_idx, c), send_sem=ici_send.at[d - 1],
                recv_sem=ici_recv.at[d - 1], device_id=(ox, oy, c), device_id_type=MESH)

        def d2d_copy(d, core):
            return pltpu.make_async_remote_copy(
                src_ref=piece(other_chip(d)[2], core), dst_ref=piece(other_chip(d)[2], core),
                send_sem=d2d_send.at[d - 1], recv_sem=d2d_recv.at[d - 1], device_id=(px_, py_, 1 - c),
                device_id_type=MESH)

        @pl.when(i == 0)
        def _():
            pltpu.sync_copy(w_hbm, w_vmem)
            land[b] = wout_ref[...].astype(BF16)
            for d in range(1, N_CHIP):
                ici_copy(d, b).start()

        @pl.when(i == n_steps // 2)
        def _():
            for d in range(1, N_CHIP):
                ici_copy(d, other_chip(d)[2]).wait_recv()
                d2d_copy(d, c).start()

        @pl.when(i == n_steps - 1)
        def _():
            for d in range(1, N_CHIP):
                d2d_copy(d, 1 - c).wait_recv()
            for d in range(1, N_CHIP):
                ici_copy(d, b).wait_send()
                d2d_copy(d, c).wait_send()
            for k in range(N_CHIP):
                wout_all_ref[k * OUT_ROWS:(k + 1) * OUT_ROWS, :] = land[k]

        xt = x_ref[...]
        r = lax.rsqrt(jnp.mean(xt * xt, axis=-1, keepdims=True) + EPS)
        h = ((xt * r) * gain_ref[...]).astype(BF16)
        cos1 = cos_ref[...]
        sin1 = sin_ref[...]
        cos4 = jnp.tile(cos1, (1, 4))
        sin4 = jnp.tile(sin1, (1, 4))

        def seg(a, b):
            return _dot_nt(h, w_vmem[a:b, :])

        t = seg(C_QA, C_KA)
        qa_raw_ref[...] = t.astype(BF16)
        qa_ref[...] = (_qknorm_rope(t, gqa_ref[...], cos4, sin4, bd256_ref[...]) * SCALE).astype(BF16)
        t = seg(C_KA, C_VA)
        ka_raw_ref[...] = t.astype(BF16)
        kva_ref[:, :A_KV_WIDTH] = _qknorm_rope(t, gka_ref[...], cos1, sin1, bd128_ref[...]).astype(BF16)
        kva_ref[:, A_KV_WIDTH:] = seg(C_VA, C_GA).astype(BF16)
        ga_ref[...] = seg(C_GA, C_QB).astype(BF16)
        t = seg(C_QB, C_KB)
        qb_raw_ref[...] = t.astype(BF16)
        t = _qknorm_rope(t, gqb_ref[...], cos4, sin4, bd256_ref[...]) * SCALE
        qb_ref[...] = t.astype(BF16)
        _store_folded(qbf_ref, t, scr)
        t = seg(C_KB, C_VB)
        kb_raw_ref[...] = t.astype(BF16)
        t = _qknorm_rope(t, gkb_ref[...], cos4, sin4, bd256_ref[...])
        kvb_ref[:, :WIDTH] = t.astype(BF16)
        _store_folded(kvbf_ref, t, scr)
        t = seg(C_VB, C_GB)
        kvb_ref[:, WIDTH:] = t.astype(BF16)
        _store_folded(kvbf_ref, t, scr, WIDTH)
        gb_ref[...] = seg(C_GB, C_END).astype(BF16)

    sds = jax.ShapeDtypeStruct
    ln = s // FOLD
    out_shape = (sds((s, WIDTH), BF16), sds((s, 2 * A_KV_WIDTH), BF16), sds((s, WIDTH), BF16),
                 sds((s, 2 * WIDTH), BF16), sds((FOLD, ln, WIDTH), BF16), sds((FOLD, ln, 2 * WIDTH), BF16),
                 sds((s, WIDTH), BF16), sds((s, A_KV_WIDTH), BF16), sds((s, WIDTH), BF16),
                 sds((s, WIDTH), BF16), sds((s, WIDTH), BF16), sds((s, WIDTH), BF16),
                 sds((D_MODEL, D_MODEL), BF16))
    out_specs = (_rows(WIDTH), _rows(2 * A_KV_WIDTH), _rows(WIDTH), _rows(2 * WIDTH),
                 _folded_rows(WIDTH), _folded_rows(2 * WIDTH),
                 _rows(WIDTH), _rows(A_KV_WIDTH), _rows(WIDTH), _rows(WIDTH), _rows(WIDTH), _rows(WIDTH),
                 _whole((D_MODEL, D_MODEL)))
    dma = pltpu.SemaphoreType.DMA
    return pl.pallas_call(
        body, name="inproj_fwd", grid=(n_steps,),
        in_specs=[_rows(D_MODEL), _whole(gain.shape), pl.BlockSpec(memory_space=pl.ANY), _rows(LANES), _rows(LANES),
                  _whole(gqa.shape), _whole(gka.shape), _whole(gqb.shape), _whole(gkb.shape), _whole(bd256.shape),
                  _whole(bd128.shape), _whole(w_out_blk.shape)],
        out_specs=out_specs, out_shape=out_shape,
        scratch_shapes=[pltpu.VMEM((IN_WIDTH, D_MODEL), BF16), _fold_scratch(WIDTH),
                        pltpu.VMEM((N_CHIP, OUT_ROWS, D_MODEL), BF16),
                        dma((n_rel,)), dma((n_rel,)), dma((n_rel,)), dma((n_rel,))],
        compiler_params=pltpu.CompilerParams(dimension_semantics=("arbitrary",), vmem_limit_bytes=VMEM_LIMIT),
    )(x2, gain, w_bf, cos, sin_s, gqa, gka, gqb, gkb, bd256, bd128, w_out_blk)


def _seq_pos(idx, dil):
    if dil == 4:
        return 4 * (idx % 32) + idx // 32
    return idx


def _upper_mask(dil, r0=0, rows=2 * BLOCK):
    qi = (lax.broadcasted_iota(jnp.int32, (rows, BLOCK), 0) + r0) % BLOCK
    kj = lax.broadcasted_iota(jnp.int32, (rows, BLOCK), 1)
    return _seq_pos(kj, dil) > _seq_pos(qi, dil)


def _eye_mask(r0=0, rows=2 * BLOCK):
    qi = (lax.broadcasted_iota(jnp.int32, (rows, BLOCK), 0) + r0) % BLOCK
    kj = lax.broadcasted_iota(jnp.int32, (rows, BLOCK), 1)
    return qi == kj


def _stack_heads(a2, c, gqa):
    lane = lax.broadcasted_iota(jnp.int32, (1, LANES), 1) // HEAD_DIM
    zero = jnp.zeros_like(a2)
    if gqa:
        keep = lane == (c // 2)
        return jnp.concatenate([jnp.where(keep, a2, zero), jnp.where(keep, _swap_heads(a2), zero)], axis=0)
    return jnp.concatenate([jnp.where(lane == 0, a2, zero), jnp.where(lane == 1, a2, zero)], axis=0)


def _unstack_heads(a, c, gqa):
    lane = lax.broadcasted_iota(jnp.int32, (1, LANES), 1) // HEAD_DIM
    if gqa:
        return jnp.where(lane == (c // 2), a[:BLOCK], _swap_heads(a[BLOCK:]))
    return jnp.where(lane == 0, a[:BLOCK], a[BLOCK:])


def _stacked_head_ids(c, gqa):
    if gqa:
        return 2 * c + c // 2, 2 * c + 1 - c // 2
    return 2 * c, 2 * c + 1


def _per_head_rows(blk, heads):
    return jnp.concatenate([blk[:, heads[0]:heads[0] + 1], blk[:, heads[1]:heads[1] + 1]], axis=0)


def _attn_view(a, dil):
    if dil == 1:
        return a[None]
    if dil == 4:
        return a.reshape(4, 4, a.shape[1], a.shape[2])
    return a


def _attn_unview(a, dil):
    if dil == 1:
        return a[0]
    if dil == 4:
        return a.reshape(FOLD, a.shape[2], a.shape[3])
    return a


ATTN_BLOCKS_PER_STEP = 8


def _attn_specs(dil):
    if dil == 4:
        def spec(n, fn):
            return lambda w: pl.BlockSpec((4, None, n * BLOCK // 4, w), lambda r, i: (0, r, fn(i), 0))
    else:
        def spec(n, fn):
            return lambda w: pl.BlockSpec((None, n * BLOCK, w), lambda r, i: (r, fn(i), 0))
    return spec


def _blk_rows(g, dil):
    n = BLOCK // 4 if dil == 4 else BLOCK
    if isinstance(g, int):
        return slice(g * n, (g + 1) * n)
    return pl.ds(pl.multiple_of(g * n, n), n)


def _blk_load(ref, sl, dil, g=0):
    if dil == 4:
        return ref[:, _blk_rows(g, dil), sl].reshape(BLOCK, sl.stop - sl.start)
    return ref[_blk_rows(g, dil), sl]


def _blk_store(ref, sl, val, dil, g=0):
    val = val.astype(ref.dtype)
    if dil == 4:
        ref[:, _blk_rows(g, dil), sl] = val.reshape(4, BLOCK // 4, sl.stop - sl.start)
    else:
        ref[_blk_rows(g, dil), sl] = val


def _swap_heads(a):
    return pltpu.roll(a.astype(F32), HEAD_DIM, 1).astype(a.dtype)


STAT_SHIFT = 8


def _attn_fwd(q, kv, sinks, *, dil, max_dist, name):
    q, kv = _attn_view(q, dil), _attn_view(kv, dil)
    kw = kv.shape[-1] // 2
    gqa = kw == A_KV_WIDTH
    n_seq = dil
    nb = (q.shape[-2] * (4 if dil == 4 else 1)) // BLOCK
    per_step = min(ATTN_BLOCKS_PER_STEP, nb)
    with_sinks = sinks is not None
    all_lanes = slice(0, LANES)
    assert max_dist in (BLOCK - 1, BLOCK) and nb % per_step == 0 and (per_step == 1 or per_step % 2 == 0)
    diag = max_dist == BLOCK

    def body(*refs):
        if with_sinks:
            q_ref, kvp_ref, kvc_ref, sink_ref, o_ref, ml_ref = refs
        else:
            q_ref, kvp_ref, kvc_ref, o_ref, ml_ref = refs

        chunks = range(WIDTH // LANES)

        def matmuls_in(g, has_prev):
            prev_ref, prev_g = (kvp_ref, 0) if (isinstance(g, int) and g == 0) else (kvc_ref, g - 1)
            scores, values = [], []
            for c in chunks:
                sl = slice(c * LANES, (c + 1) * LANES)
                ksl = slice(0, LANES) if gqa else sl
                vsl = slice(ksl.start + kw, ksl.stop + kw)
                kcur, vcur = _blk_load(kvc_ref, ksl, dil, g), _blk_load(kvc_ref, vsl, dil, g)
                qs = _stack_heads(_blk_load(q_ref, sl, dil, g), c, gqa)
                if has_prev:
                    kcur = jnp.concatenate([_blk_load(prev_ref, ksl, dil, prev_g), kcur], axis=0)
                    vcur = jnp.concatenate([_blk_load(prev_ref, vsl, dil, prev_g), vcur], axis=0)
                scores.append(_dot_nt(qs, kcur))
                values.append(vcur)
            return scores, values

        def tile_ops(has_prev, scores):
            lane = lax.broadcasted_iota(jnp.int32, (1, LANES), 1)
            with_diag = diag and has_prev
            upper, eye = _upper_mask(dil), _eye_mask()
            first_rows = lax.broadcasted_iota(jnp.int32, (2 * BLOCK, 1), 0) < BLOCK
            ml_blk = jnp.zeros((BLOCK, LANES), F32)
            probs = []
            for c in chunks:
                heads = _stacked_head_ids(c, gqa)
                s = scores[c]
                if has_prev:
                    s_p = s[:, :BLOCK]
                    sc = jnp.where(upper, s_p, s[:, BLOCK:])
                else:
                    sc = jnp.where(upper, NEG, s)
                if with_diag:
                    sd = jnp.where(eye, s_p, NEG)
                    m = jnp.max(jnp.maximum(sc, sd), axis=-1, keepdims=True)
                else:
                    m = jnp.max(sc, axis=-1, keepdims=True)
                if with_sinks:
                    sk = jnp.where(first_rows, sink_ref[0, heads[0]], sink_ref[0, heads[1]])
                    m = jnp.maximum(m, sk)
                p = jnp.exp(sc - m)
                zero = jnp.zeros_like(p)
                if with_diag:
                    pd = jnp.exp(sd - m)
                    l = jnp.sum(p + pd, axis=-1, keepdims=True)
                else:
                    pd = zero
                    l = jnp.sum(p, axis=-1, keepdims=True)
                if with_sinks:
                    l = l + jnp.exp(sk - m)
                pf = jnp.where(upper, zero, p)
                if has_prev:
                    pf = jnp.concatenate([jnp.where(upper, p, pd), pf], axis=1)
                probs.append(pf.astype(BF16))
                for n, h in enumerate(heads):
                    rows = slice(n * BLOCK, (n + 1) * BLOCK)
                    ml_blk = jnp.where(lane == h, m[rows], ml_blk)
                    ml_blk = jnp.where(lane == h + STAT_SHIFT, l[rows], ml_blk)
            return probs, ml_blk

        def matmuls_out(g, values, probs, ml_blk):
            for c in chunks:
                sl = slice(c * LANES, (c + 1) * LANES)
                _blk_store(o_ref, sl, _unstack_heads(_dot(probs[c], values[c]), c, gqa), dil, g)
            _blk_store(ml_ref, all_lanes, ml_blk, dil, g)

        def run(blocks):
            ins = [matmuls_in(g, has_prev) for g, has_prev in blocks]
            mids = [tile_ops(has_prev, scores) for (_, has_prev), (scores, _) in zip(blocks, ins)]
            for (g, _), (_, values), (probs, ml_blk) in zip(blocks, ins, mids):
                matmuls_out(g, values, probs, ml_blk)

        second = [(1, True)] if per_step > 1 else []

        @pl.when(pl.program_id(1) == 0)
        def _():
            run([(0, False)] + second)

        @pl.when(pl.program_id(1) > 0)
        def _():
            run([(0, True)] + second)

        if per_step > 2:
            def rest(pair, carry):
                run([(2 * pair, True), (2 * pair + 1, True)])
                return carry

            lax.fori_loop(1, per_step // 2, rest, 0)

    spec = _attn_specs(dil)
    cur = spec(per_step, lambda i: i)
    prev = spec(1, lambda i: jnp.maximum(i * per_step - 1, 0))
    in_specs = [cur(WIDTH), prev(2 * kw), cur(2 * kw)]
    args = [q, kv, kv]
    if with_sinks:
        in_specs.append(pl.BlockSpec(memory_space=pltpu.SMEM))
        args.append(sinks)
    stats = jax.ShapeDtypeStruct(q.shape[:-1] + (LANES,), F32)
    o, ml = pl.pallas_call(
        body, name=name, grid=(n_seq, nb // per_step), in_specs=in_specs,
        out_specs=(cur(WIDTH), cur(LANES)),
        out_shape=(jax.ShapeDtypeStruct(q.shape, BF16), stats),
        compiler_params=pltpu.CompilerParams(dimension_semantics=("arbitrary", "arbitrary"),
                                             vmem_limit_bytes=VMEM_LIMIT),
    )(*args)
    return _attn_unview(o, dil), _attn_unview(ml, dil)


def _attn_bwd(q, kv, do, ld, *, dil, max_dist, name, onto=None):
    q, kv, do, ld = (_attn_view(a, dil) for a in (q, kv, do, ld))
    onto = () if onto is None else tuple(_attn_view(a, dil) for a in onto)
    kw = kv.shape[-1] // 2
    gqa = kw == A_KV_WIDTH
    n_seq = dil
    nb = (q.shape[-2] * (4 if dil == 4 else 1)) // BLOCK
    n_kc = kw // LANES
    per_step = min(ATTN_BLOCKS_PER_STEP, nb)
    all_lanes = slice(0, LANES)
    assert max_dist in (BLOCK - 1, BLOCK) and nb % per_step == 0 and (per_step == 1 or per_step % 2 == 0)
    diag = max_dist == BLOCK

    def body(q_ref, kvp_ref, kvc_ref, do_ref, ld_ref, *rest_refs):
        dq_ref, dkv_ref, ck_ref, cv_ref = rest_refs[len(onto):]
        i = pl.program_id(1)

        def store(ref, sl, val, blk):
            if onto:
                val = val + _blk_load(rest_refs[0 if ref is dq_ref else 1], sl, dil, blk).astype(F32)
            _blk_store(ref, sl, val, dil, blk)

        chunks = range(WIDTH // LANES)

        def matmuls_in(g, has_prev):
            prev_ref, prev_g = (kvp_ref, 0) if (isinstance(g, int) and g == 0) else (kvc_ref, g - 1)
            operands, products = [], []
            for c in chunks:
                sl = slice(c * LANES, (c + 1) * LANES)
                kc = 0 if gqa else c
                ksl = slice(kc * LANES, (kc + 1) * LANES)
                vsl = slice(ksl.start + kw, ksl.stop + kw)
                k2, v2 = _blk_load(kvc_ref, ksl, dil, g), _blk_load(kvc_ref, vsl, dil, g)
                if has_prev:
                    k2 = jnp.concatenate([_blk_load(prev_ref, ksl, dil, prev_g), k2], axis=0)
                    v2 = jnp.concatenate([_blk_load(prev_ref, vsl, dil, prev_g), v2], axis=0)
                qs = _stack_heads(_blk_load(q_ref, sl, dil, g), c, gqa)
                dos = _stack_heads(_blk_load(do_ref, sl, dil, g), c, gqa)
                operands.append((qs, dos, k2))
                products.append((_dot_nt(qs, k2), _dot_nt(dos, v2)))
            return operands, products

        def tile_ops(g, has_prev, products):
            upper, eye = _upper_mask(dil), _eye_mask()
            ld_blk = _blk_load(ld_ref, all_lanes, dil, g)
            weights = []
            for c in chunks:
                heads = _stacked_head_ids(c, gqa)
                lse2 = _per_head_rows(ld_blk, heads)
                dl2 = _per_head_rows(ld_blk, tuple(h + STAT_SHIFT for h in heads))
                s, dp = products[c]
                if has_prev:
                    s_p, dp_p = s[:, :BLOCK], dp[:, :BLOCK]
                    sc = jnp.where(upper, s_p, s[:, BLOCK:])
                    dpc = jnp.where(upper, dp_p, dp[:, BLOCK:])
                else:
                    sc = jnp.where(upper, NEG, s)
                    dpc = dp
                p = jnp.exp(sc - lse2)
                ds = p * (dpc - dl2)
                zero = jnp.zeros_like(p)
                pf = jnp.where(upper, zero, p)
                dsf = jnp.where(upper, zero, ds)
                if has_prev:
                    if diag:
                        pd = jnp.exp(jnp.where(eye, s_p, NEG) - lse2)
                        dsd = pd * (dp_p - dl2)
                    else:
                        pd = dsd = zero
                    pf = jnp.concatenate([jnp.where(upper, p, pd), pf], axis=1)
                    dsf = jnp.concatenate([jnp.where(upper, ds, dsd), dsf], axis=1)
                weights.append((pf.astype(BF16), dsf.astype(BF16)))
            return weights

        def matmuls_out(g, has_prev, operands, weights):
            seq_blk = i * per_step + g
            dk_acc = [None] * n_kc
            dv_acc = [None] * n_kc
            for c in chunks:
                sl = slice(c * LANES, (c + 1) * LANES)
                kc = 0 if gqa else c
                qs, dos, k2 = operands[c]
                pf, dsf = weights[c]
                store(dq_ref, sl, _unstack_heads(_dot(dsf, k2), c, gqa) * SCALE, g)
                dk2 = _dot_tn(dsf, qs)
                dv2 = _dot_tn(pf, dos)
                dk_acc[kc] = dk2 if dk_acc[kc] is None else dk_acc[kc] + dk2
                dv_acc[kc] = dv2 if dv_acc[kc] is None else dv_acc[kc] + dv2
            for kc in range(n_kc):
                sl = slice(kc * LANES, (kc + 1) * LANES)
                vsl = slice(sl.start + kw, sl.stop + kw)
                if has_prev:
                    store(dkv_ref, sl, ck_ref[:, sl] + dk_acc[kc][:BLOCK], seq_blk - 1)
                    store(dkv_ref, vsl, cv_ref[:, sl] + dv_acc[kc][:BLOCK], seq_blk - 1)
                    ck_ref[:, sl] = dk_acc[kc][BLOCK:]
                    cv_ref[:, sl] = dv_acc[kc][BLOCK:]
                else:
                    ck_ref[:, sl] = dk_acc[kc]
                    cv_ref[:, sl] = dv_acc[kc]

        def run(blocks):
            ins = [matmuls_in(g, has_prev) for g, has_prev in blocks]
            mids = [tile_ops(g, has_prev, products) for (g, has_prev), (_, products) in zip(blocks, ins)]
            for (g, has_prev), (operands, _), weights in zip(blocks, ins, mids):
                matmuls_out(g, has_prev, operands, weights)

        second = [(1, True)] if per_step > 1 else []

        @pl.when(i == 0)
        def _():
            run([(0, False)] + second)

        @pl.when(i > 0)
        def _():
            run([(0, True)] + second)

        if per_step > 2:
            def rest(pair, carry):
                run([(2 * pair, True), (2 * pair + 1, True)])
                return carry

            lax.fori_loop(1, per_step // 2, rest, 0)

        @pl.when(i == nb // per_step - 1)
        def _():
            for kc in range(n_kc):
                sl = slice(kc * LANES, (kc + 1) * LANES)
                store(dkv_ref, sl, ck_ref[:, sl], nb - 1)
                store(dkv_ref, slice(sl.start + kw, sl.stop + kw), cv_ref[:, sl], nb - 1)

    spec = _attn_specs(dil)
    cur = spec(per_step, lambda i: i)
    prev = spec(1, lambda i: jnp.maximum(i * per_step - 1, 0))
    if dil == 4:
        whole = pl.BlockSpec((4, None, kv.shape[2], 2 * kw), lambda r, i: (0, r, 0, 0))
    else:
        whole = pl.BlockSpec((None, kv.shape[1], 2 * kw), lambda r, i: (r, 0, 0))
    sds = jax.ShapeDtypeStruct
    dq, dkv = pl.pallas_call(
        body, name=name, grid=(n_seq, nb // per_step),
        in_specs=[cur(WIDTH), prev(2 * kw), cur(2 * kw), cur(WIDTH), cur(LANES)] + [cur(WIDTH), whole][:len(onto)],
        out_specs=(cur(WIDTH), whole),
        out_shape=(sds(q.shape, BF16), sds(kv.shape, BF16)),
        scratch_shapes=[pltpu.VMEM((BLOCK, kw), F32), pltpu.VMEM((BLOCK, kw), F32)],
        compiler_params=pltpu.CompilerParams(dimension_semantics=("arbitrary", "arbitrary"),
                                             vmem_limit_bytes=VMEM_LIMIT),
    )(q, kv, kv, do, ld, *onto)
    return _attn_unview(dq, dil), _attn_unview(dkv, dil)


def _outproj(att_a, att_b1, att_b4, att_b16, g_a, g_b, x2, tgt2, w_out_bf, sink_row, perm):
    s = x2.shape[0]
    tm = ROW_TILE

    def body(oa_ref, mla_ref, ob1_ref, ml1_ref, ob4_ref, ml4_ref, ob16_ref, ml16_ref,
             ga_ref, gb_ref, x_ref, t_ref, w_ref, sink_ref, perm_ref,
             dy_ref, doa_ref, dob_ref, dobf_ref, dga_ref, dgb_ref, lda_ref, ldb_ref, ldbf_ref,
             gw_ref, loss_ref, dsink_ref, scr_st):
        i = pl.program_id(0)
        perm = perm_ref[...]
        lane = lax.broadcasted_iota(jnp.int32, (tm, LANES), 1)
        used = lane < HEADS

        def split(ml):
            return jnp.where(used, ml, 0.0), jnp.where(used, pltpu.roll(ml, LANES - STAT_SHIFT, 1), 1.0)

        @pl.when(i == 0)
        def _():
            gw_ref[...] = jnp.zeros_like(gw_ref)
            loss_ref[...] = jnp.zeros_like(loss_ref)
            dsink_ref[...] = jnp.zeros_like(dsink_ref)

        ms, ls = zip(split(ml1_ref[...]), split(_load_folded(ml4_ref, scr_st)), split(_load_folded(ml16_ref, scr_st)))
        mx = jnp.maximum(jnp.maximum(ms[0], ms[1]), ms[2])
        scale = [jnp.exp(mp - mx) for mp in ms]
        den = (ls[0] * scale[0] + ls[1] * scale[1]) + ls[2] * scale[2]
        lse_b = jnp.where(used, mx + jnp.log(den), 0.0)
        inv_den = 1.0 / den
        o_b = _expand_heads(scale[0] * inv_den) * ob1_ref[...].astype(F32)
        o_b = o_b + _expand_heads(scale[1] * inv_den) * _load_folded_bf16(ob4_ref, perm)
        o_b = o_b + _expand_heads(scale[2] * inv_den) * _load_folded_bf16(ob16_ref, perm)
        m_a, l_a = split(mla_ref[...])
        lse_a = jnp.where(used, m_a + jnp.log(l_a), 0.0)
        o_a = _expand_heads(1.0 / l_a) * oa_ref[...].astype(F32)
        g_a = ga_ref[...].astype(F32)
        g_b = gb_ref[...].astype(F32)
        sg_a = _sigmoid(g_a)
        sg_b = _sigmoid(g_b)
        silu_a = g_a * sg_a
        silu_b = g_b * sg_b
        mixed = jnp.concatenate([o_a * silu_a, o_b * silu_b], axis=1).astype(BF16)
        w = w_ref[...]
        y = x_ref[...] + _dot(mixed, w)
        diff = y - t_ref[...]
        loss_ref[...] += (0.5 / D_MODEL) * jnp.sum(diff * diff)
        dy = diff * (1.0 / D_MODEL)
        dy_ref[...] = dy
        dyb = dy.astype(BF16)
        gw_ref[...] += _dot_tn(mixed, dyb)
        dmixed = _dot_nt(dyb, w)
        dm_a = dmixed[:, :WIDTH]
        dm_b = dmixed[:, WIDTH:]
        do_a = dm_a * silu_a
        do_b = dm_b * silu_b
        doa_ref[...] = do_a.astype(BF16)
        dob_ref[...] = do_b.astype(BF16)
        _store_folded_bf16(dobf_ref, do_b, perm)
        dga_ref[...] = (dm_a * o_a * (sg_a * (1.0 + g_a * (1.0 - sg_a)))).astype(BF16)
        dgb_ref[...] = (dm_b * o_b * (sg_b * (1.0 + g_b * (1.0 - sg_b)))).astype(BF16)
        dl_a = _reduce_heads(do_a * o_a)
        dl_b = _reduce_heads(do_b * o_b)
        lda_ref[...] = lse_a + pltpu.roll(dl_a, STAT_SHIFT, 1)
        ld_b = lse_b + pltpu.roll(dl_b, STAT_SHIFT, 1)
        ldb_ref[...] = ld_b
        _store_folded(ldbf_ref, ld_b, scr_st)
        dsink_ref[...] -= jnp.sum(jnp.exp(sink_ref[...] - lse_a) * dl_a, axis=0, keepdims=True)

    sds = jax.ShapeDtypeStruct
    ln = s // FOLD
    natural = [_rows(WIDTH), _rows(LANES)]
    folded = [_folded_rows(WIDTH), _folded_rows(LANES)]
    return pl.pallas_call(
        body, name="outproj_fwd_bwd", grid=(s // tm,),
        in_specs=natural + natural + folded + folded
                 + [_rows(WIDTH), _rows(WIDTH), _rows(D_MODEL), _rows(D_MODEL), _whole((D_MODEL, D_MODEL)),
                    _whole((1, LANES)), _whole(perm.shape)],
        out_specs=(_rows(D_MODEL), _rows(WIDTH), _rows(WIDTH), _folded_rows(WIDTH), _rows(WIDTH), _rows(WIDTH),
                   _rows(LANES), _rows(LANES), _folded_rows(LANES),
                   _whole((D_MODEL, D_MODEL)), _whole((1, LANES)), _whole((1, LANES))),
        out_shape=(sds((s, D_MODEL), F32), sds((s, WIDTH), BF16), sds((s, WIDTH), BF16),
                   sds((FOLD, ln, WIDTH), BF16), sds((s, WIDTH), BF16), sds((s, WIDTH), BF16),
                   sds((s, LANES), F32), sds((s, LANES), F32), sds((FOLD, ln, LANES), F32),
                   sds((D_MODEL, D_MODEL), F32), sds((1, LANES), F32), sds((1, LANES), F32)),
        scratch_shapes=[_fold_scratch(LANES)],
        compiler_params=pltpu.CompilerParams(dimension_semantics=("arbitrary",), vmem_limit_bytes=VMEM_LIMIT),
    )(*att_a, *att_b1, *att_b4, *att_b16, g_a, g_b, x2, tgt2, w_out_bf, sink_row, perm)


def _inproj_bwd(x2, dy, gain, w_bf, cos, sin_s, gqa, gka, gqb, gkb, bd256, bd128, perm,
                qa_raw, ka_raw, qb_raw, kb_raw, d_a, d_b1, d_bf, dg_a, dg_b):
    s = x2.shape[0]
    tm = ROW_TILE

    def body(x_ref, dy_ref, gain_ref, w_hbm, cos_ref, sin_ref, gqa_ref, gka_ref, gqb_ref, gkb_ref, bd256_ref,
             bd128_ref, perm_ref, qa_raw_ref, ka_raw_ref, qb_raw_ref, kb_raw_ref, dqa_ref, dkva_ref,
             dq1_ref, dkv1_ref, dqf_ref, dkvf_ref, dga_ref, dgb_ref,
             gx_ref, ht_ref, win_ref,
             dgain_ref, dgqa_ref, dgka_ref, dgqb_ref, dgkb_ref, w_vmem, dproj_ref):
        i = pl.program_id(0)
        perm = perm_ref[...]

        @pl.when(i == 0)
        def _():
            pltpu.sync_copy(w_hbm, w_vmem)
            dgain_ref[...] = jnp.zeros_like(dgain_ref)
            dgqa_ref[...] = jnp.zeros_like(dgqa_ref)
            dgka_ref[...] = jnp.zeros_like(dgka_ref)
            dgqb_ref[...] = jnp.zeros_like(dgqb_ref)
            dgkb_ref[...] = jnp.zeros_like(dgkb_ref)

        cos1 = cos_ref[...]
        sin1 = sin_ref[...]
        cos4 = jnp.tile(cos1, (1, 4))
        sin4 = jnp.tile(sin1, (1, 4))

        dt, dg = _qknorm_rope_bwd(dqa_ref[...], qa_raw_ref[...], gqa_ref[...], cos4, sin4, bd256_ref[...])
        dproj_ref[:, C_QA:C_KA] = dt.astype(BF16)
        dgqa_ref[...] += jnp.sum(dg, axis=0, keepdims=True)
        dt, dg = _qknorm_rope_bwd(dkva_ref[:, :A_KV_WIDTH], ka_raw_ref[...], gka_ref[...], cos1, sin1,
                                  bd128_ref[...])
        dproj_ref[:, C_KA:C_VA] = dt.astype(BF16)
        dgka_ref[...] += jnp.sum(dg, axis=0, keepdims=True)
        dproj_ref[:, C_VA:C_GA] = dkva_ref[:, A_KV_WIDTH:]
        dproj_ref[:, C_GA:C_QB] = dga_ref[...]
        dq = dq1_ref[...].astype(F32) + _load_folded_bf16(dqf_ref, perm)
        dt, dg = _qknorm_rope_bwd(dq, qb_raw_ref[...], gqb_ref[...], cos4, sin4, bd256_ref[...])
        dproj_ref[:, C_QB:C_KB] = dt.astype(BF16)
        dgqb_ref[...] += jnp.sum(dg, axis=0, keepdims=True)
        dkv = dkv1_ref[...].astype(F32) + _load_folded_bf16(dkvf_ref, perm)
        dt, dg = _qknorm_rope_bwd(dkv[:, :WIDTH], kb_raw_ref[...], gkb_ref[...], cos4, sin4, bd256_ref[...])
        dproj_ref[:, C_KB:C_VB] = dt.astype(BF16)
        dgkb_ref[...] += jnp.sum(dg, axis=0, keepdims=True)
        dproj_ref[:, C_VB:C_GB] = dkv[:, WIDTH:].astype(BF16)
        dproj_ref[:, C_GB:C_END] = dgb_ref[...]
        for k, start in enumerate(WIN_START):
            win_ref[k] = dproj_ref[:, start:start + WIN]

        xt = x_ref[...]
        gain_row = gain_ref[...]
        r = lax.rsqrt(jnp.mean(xt * xt, axis=-1, keepdims=True) + EPS)
        xr = xt * r
        ht_ref[...] = (xr * gain_row).T.astype(BF16)
        dh = _dot(dproj_ref[...], w_vmem[...])
        dgain_ref[...] += jnp.sum(dh * xr, axis=0, keepdims=True)
        u = dh * gain_row
        gx_ref[...] = dy_ref[...] + r * (u - xr * jnp.mean(u * xr, axis=-1, keepdims=True))

    def acc_row(w):
        return pl.BlockSpec((1, w), lambda i: (0, 0))

    sds = jax.ShapeDtypeStruct
    any_spec = pl.BlockSpec(memory_space=pl.ANY)
    win_spec = pl.BlockSpec((N_CHIP, tm, WIN), lambda i: (0, i, 0))
    return pl.pallas_call(
        body, name="inproj_bwd", grid=(s // tm,),
        in_specs=[_rows(D_MODEL), _rows(D_MODEL), _whole(gain.shape), any_spec, _rows(LANES), _rows(LANES),
                  _whole(gqa.shape), _whole(gka.shape), _whole(gqb.shape), _whole(gkb.shape), _whole(bd256.shape),
                  _whole(bd128.shape), _whole(perm.shape),
                  _rows(WIDTH), _rows(A_KV_WIDTH), _rows(WIDTH), _rows(WIDTH),
                  _rows(WIDTH), _rows(2 * A_KV_WIDTH), _rows(WIDTH), _rows(2 * WIDTH)]
                 + [_folded_rows(WIDTH), _folded_rows(2 * WIDTH), _rows(WIDTH), _rows(WIDTH)],
        out_specs=(_rows(D_MODEL), pl.BlockSpec((D_MODEL, tm), lambda i: (0, i)), win_spec, acc_row(D_MODEL), acc_row(WIDTH), acc_row(A_KV_WIDTH), acc_row(WIDTH), acc_row(WIDTH)),
        out_shape=(sds((s, D_MODEL), F32), sds((D_MODEL, s), BF16), sds((N_CHIP, s, WIN), BF16),
                   sds((1, D_MODEL), F32),
                   sds((1, WIDTH), F32), sds((1, A_KV_WIDTH), F32), sds((1, WIDTH), F32), sds((1, WIDTH), F32)),
        scratch_shapes=[pltpu.VMEM((IN_WIDTH, D_MODEL), BF16), pltpu.VMEM((tm, IN_WIDTH), BF16)],
        compiler_params=pltpu.CompilerParams(dimension_semantics=("arbitrary",), vmem_limit_bytes=VMEM_LIMIT),
    )(x2, dy, gain, w_bf, cos, sin_s, gqa, gka, gqb, gkb, bd256, bd128, perm, qa_raw, ka_raw, qb_raw, kb_raw,
      *d_a, *d_b1, *d_bf, dg_a, dg_b)


def _rope_tables(s):
    half = HEAD_DIM // 2
    inv = jnp.tile(ROPE_THETA ** (-jnp.arange(half, dtype=F32) / half), 4)
    sign = jnp.tile(jnp.concatenate([-jnp.ones((half,), F32), jnp.ones((half,), F32)]), 2)
    hi = (jnp.arange(s // ROPE_SPLIT) * ROPE_SPLIT).astype(F32)[:, None] * inv[None, :]
    lo = jnp.arange(ROPE_SPLIT).astype(F32)[:, None] * inv[None, :]
    ch, sh, cl, sl = jnp.cos(hi)[:, None, :], jnp.sin(hi)[:, None, :], jnp.cos(lo)[None], jnp.sin(lo)[None]
    cos = (ch * cl - sh * sl).reshape(s, LANES)
    sin = (sh * cl + ch * sl).reshape(s, LANES)
    return cos, sin * sign[None, :]


def _block_diag_ones(w):
    idx = jnp.arange(w) // HEAD_DIM
    return (idx[:, None] == idx[None, :]).astype(BF16)


def _local_step(x2, tgt2, norm_gain, w_in_bf, q_norm_a, k_norm_a, sinks_a, q_norm_b, k_norm_b, w_out_blk):
    s = x2.shape[0]
    cos, sin_s = _rope_tables(s)
    bd256, bd128 = _block_diag_ones(2 * LANES), _block_diag_ones(A_KV_WIDTH)
    gqa = jnp.tile(q_norm_a, (1, HEADS))
    gka = jnp.tile(k_norm_a, (1, 2))
    gqb = jnp.tile(q_norm_b, (1, HEADS))
    gkb = jnp.tile(k_norm_b, (1, HEADS))
    sink_row = jnp.pad(sinks_a, ((0, 0), (0, LANES - HEADS)))
    perm = _fold_matrix()

    (qa, kva, qb, kvb, qbf, kvbf, qa_raw, ka_raw, g_a, qb_raw, kb_raw, g_b, w_out_bf) = _inproj(
        x2, norm_gain, w_in_bf, cos, sin_s, gqa, gka, gqb, gkb, bd256, bd128, w_out_blk)

    att_a = _attn_fwd(qa, kva, sinks_a, dil=1, max_dist=A_MAX_DIST, name="attn_a_fwd")
    att_b1 = _attn_fwd(qb, kvb, None, dil=1, max_dist=B_MAX_DIST, name="attn_b1_fwd")
    att_b4 = _attn_fwd(qbf, kvbf, None, dil=4, max_dist=B_MAX_DIST, name="attn_b4_fwd")
    att_b16 = _attn_fwd(qbf, kvbf, None, dil=16, max_dist=B_MAX_DIST, name="attn_b16_fwd")

    (dy, do_a, do_b, do_bf, dg_a, dg_b, ld_a, ld_b, ld_bf, gw_out, loss_part, dsink) = _outproj(
        att_a, att_b1, att_b4, att_b16, g_a, g_b, x2, tgt2, w_out_bf, sink_row, perm)

    d_a = _attn_bwd(qa, kva, do_a, ld_a, dil=1, max_dist=A_MAX_DIST, name="attn_a_bwd")
    d_b1 = _attn_bwd(qb, kvb, do_b, ld_b, dil=1, max_dist=B_MAX_DIST, name="attn_b1_bwd")
    d_b4 = _attn_bwd(qbf, kvbf, do_bf, ld_bf, dil=4, max_dist=B_MAX_DIST, name="attn_b4_bwd")
    d_b16 = _attn_bwd(qbf, kvbf, do_bf, ld_bf, dil=16, max_dist=B_MAX_DIST, name="attn_b16_bwd", onto=d_b4)

    gx, h_t, wins, dgain, dgqa, dgka, dgqb, dgkb = _inproj_bwd(
        x2, dy, norm_gain, w_in_bf, cos, sin_s, gqa, gka, gqb, gkb, bd256, bd128, perm,
        qa_raw, ka_raw, qb_raw, kb_raw, d_a, d_b1, d_b16, dg_a, dg_b)
    return loss_part, gx, h_t, wins, gw_out, (dgain, dgqa, dgka, dsink, dgqb, dgkb)


def _position():
    return lax.axis_index("x"), lax.axis_index("y"), lax.axis_index("c")


GATHER_CHUNKS = 13


def _gather_weights(blocks, name):
    n = len(blocks)
    ch = GATHER_CHUNKS

    def body(*refs):
        src_refs, dst_refs = refs[:n], refs[n:2 * n]
        ici_send, ici_recv, hop_send, hop_recv, d2d_send, d2d_recv = refs[2 * n:]
        x, y, c = _position()
        b = 2 * x + y
        via = 2 - c
        out = 3 - via
        for k in range(n):
            dst_refs[k][b] = src_refs[k][...].astype(BF16)

        def rows(k, core, j):
            half = blocks[k].shape[0] // 2
            return pl.ds(pl.multiple_of(core * half + j * (half // ch), half // ch), half // ch)

        def chip(rel):
            return x ^ (rel >> 1), y ^ (rel & 1)

        def ici(k, j, slot, rel, send_sems, recv_sems, sem):
            px, py = chip(rel)
            piece = dst_refs[k].at[slot, rows(k, c, j)]
            return pltpu.make_async_remote_copy(src_ref=piece, dst_ref=piece, send_sem=send_sems.at[sem],
                                                recv_sem=recv_sems.at[sem], device_id=(px, py, c),
                                                device_id_type=MESH)

        def direct(k, j, slot, rel):
            return ici(k, j, slot, rel, ici_send, ici_recv, ((rel - 1) * ch + j) * n + k)

        def hop(k, j, slot, rel):
            return ici(k, j, slot, rel, hop_send, hop_recv, j * n + k)

        def d2d(k, j, rel, core):
            piece = dst_refs[k].at[b ^ rel, rows(k, core, j)]
            sem = ((rel - 1) * ch + j) * n + k
            return pltpu.make_async_remote_copy(src_ref=piece, dst_ref=piece, send_sem=d2d_send.at[sem],
                                                recv_sem=d2d_recv.at[sem], device_id=(x, y, 1 - c),
                                                device_id_type=MESH)

        pieces = [(k, j) for j in range(ch) for k in range(n)]
        for k, j in pieces:
            for rel in (1, 2):
                direct(k, j, b, rel).start()
        for k, j in pieces:
            direct(k, j, b ^ via, via).wait_recv()
            hop(k, j, b ^ via, out).start()
            d2d(k, j, via, c).start()
        for k, j in pieces:
            direct(k, j, b ^ out, out).wait_recv()
            d2d(k, j, out, c).start()
        for k, j in pieces:
            hop(k, j, b ^ 3, via).wait_recv()
            d2d(k, j, 3, c).start()
        for k, j in pieces:
            for rel in (1, 2, 3):
                d2d(k, j, rel, 1 - c).wait_recv()
        for k, j in pieces:
            for rel in (1, 2):
                direct(k, j, b, rel).wait_send()
            hop(k, j, b ^ via, out).wait_send()
            d2d(k, j, via, c).wait_send()
            d2d(k, j, out, c).wait_send()
            d2d(k, j, 3, c).wait_send()

    vmem_spec = pl.BlockSpec(memory_space=pltpu.VMEM)
    dma = pltpu.SemaphoreType.DMA
    out_shape = tuple(jax.ShapeDtypeStruct((N_CHIP,) + a.shape, BF16) for a in blocks)
    return pl.pallas_call(
        body, name=name, in_specs=[vmem_spec] * n, out_specs=tuple([vmem_spec] * n), out_shape=out_shape,
        scratch_shapes=[dma((2 * ch * n,)), dma((2 * ch * n,)), dma((ch * n,)), dma((ch * n,)),
                        dma((3 * ch * n,)), dma((3 * ch * n,))],
        compiler_params=pltpu.CompilerParams(vmem_limit_bytes=VMEM_LIMIT),
    )(*blocks)


def _grad_reduce(order, h_t, wins, gw_out, small):
    s = h_t.shape[1]
    tk = GRAD_ROWS
    n_i = s // tk
    half = D_MODEL // 2
    o_half = OUT_ROWS // 2
    n_rel = N_CHIP - 1

    def body(order_ref, ht_ref, win_ref, gwo_ref, small_ref,
             win_out, wout_out, small_out,
             acc, mine, s1, r1, s2, r2, so1, ro1, so2, ro2, pair_in, pair_o, small_land,
             s1_send, s1_recv, s2_send, s2_recv, o1_send, o1_recv, o2_send, o2_recv,
             pair_send, pair_recv, small_send, small_recv):
        j = pl.program_id(0)
        i = pl.program_id(1)
        x, y, c = _position()
        me = 4 * x + 2 * y + c
        sibling = (x, y, 1 - c)
        my_rows = pl.ds(pl.multiple_of(c * half, half), half)
        sib_rows = pl.ds(pl.multiple_of((1 - c) * half, half), half)

        def chip_of(rel):
            return x ^ (rel >> 1), y ^ (rel & 1)

        def level1(k):
            return pltpu.make_async_remote_copy(src_ref=s1.at[k], dst_ref=r1.at[k], send_sem=s1_send.at[k],
                                                recv_sem=s1_recv.at[k], device_id=sibling, device_id_type=MESH)

        def level2(k):
            px, py = chip_of(RELATIONS[k])
            return pltpu.make_async_remote_copy(src_ref=s2.at[k], dst_ref=r2.at[k], send_sem=s2_send.at[k],
                                                recv_sem=s2_recv.at[k], device_id=(px, py, c), device_id_type=MESH)

        def out_level1(bk):
            return pltpu.make_async_remote_copy(src_ref=so1.at[bk], dst_ref=ro1.at[bk], send_sem=o1_send.at[bk],
                                                recv_sem=o1_recv.at[bk], device_id=sibling, device_id_type=MESH)

        def out_level2(k):
            px, py = chip_of(RELATIONS[k])
            return pltpu.make_async_remote_copy(src_ref=so2.at[k], dst_ref=ro2.at[k], send_sem=o2_send.at[k],
                                                recv_sem=o2_recv.at[k], device_id=(px, py, c), device_id_type=MESH)

        def small_copy(d):
            px, py, pc = x ^ (d >> 2), y ^ ((d >> 1) & 1), c ^ (d & 1)
            return pltpu.make_async_remote_copy(src_ref=small_ref, dst_ref=small_land.at[me],
                                                send_sem=small_send.at[d], recv_sem=small_recv.at[d],
                                                device_id=(px, py, pc), device_id_type=MESH)

        def pair_copy(k, buf):
            return pltpu.make_async_remote_copy(src_ref=buf.at[0], dst_ref=buf.at[1], send_sem=pair_send.at[k],
                                                recv_sem=pair_recv.at[k], device_id=sibling, device_id_type=MESH)

        def out_rows(bk, core):
            return pl.ds(pl.multiple_of(bk * OUT_ROWS + core * o_half, o_half), o_half)

        @pl.when((j == 0) & (i == 0))
        def _():
            for d in range(1, N_DEV):
                small_copy(d).start()
            small_land[me] = small_ref[...]
            for bk in range(N_CHIP):
                so1[bk] = gwo_ref[out_rows(bk, 1 - c), :].astype(BF16)
                out_level1(bk).start()

        @pl.when((j == 0) & (i == 1))
        def _():
            b = 2 * x + y
            for bk in range(N_CHIP):
                out_level1(bk).wait_recv()
            for k in range(n_rel):
                px, py = chip_of(RELATIONS[k])
                bk = 2 * px + py
                so2[k] = (gwo_ref[out_rows(bk, c), :] + ro1[bk].astype(F32)).astype(BF16)
                out_level2(k).start()

        @pl.when(i == 0)
        def _():
            acc[...] = jnp.zeros_like(acc)

        for n0 in range(0, WIN, ACC_COLS):
            n1 = min(n0 + ACC_COLS, WIN)
            acc[:, n0:n1] += _dot(ht_ref[...], win_ref[:, n0:n1])

        for k in range(N_CHIP):
            @pl.when((j == k) & (i == n_i - 1))
            def _(k=k):
                s1[k] = acc[sib_rows, :].astype(BF16)
                level1(k).start()
                mine[...] = acc[my_rows, :]

            if k < n_rel:
                @pl.when((j == k + 1) & (i == 1))
                def _(k=k):
                    level1(k).wait_recv()
                    s2[k] = (mine[...] + r1[k].astype(F32)).astype(BF16)
                    level2(k).start()

        @pl.when((j == N_CHIP - 1) & (i == n_i - 1))
        def _():
            b = 2 * x + y
            level1(N_CHIP - 1).wait_recv()
            total = mine[...] + r1[N_CHIP - 1].astype(F32)
            for k in range(n_rel):
                level2(k).wait_recv()
                total = total + r2[k].astype(F32)
            total = total.T
            pair_in[0] = total
            pair_copy(0, pair_in).start()
            total_o = gwo_ref[out_rows(b, c), :] + ro1[b].astype(F32)
            for k in range(n_rel):
                out_level2(k).wait_recv()
                total_o = total_o + ro2[k].astype(F32)
            pair_o[0] = total_o
            pair_copy(1, pair_o).start()
            for core in range(2):
                @pl.when(c == core)
                def _(core=core):
                    win_out[:, core * half:(core + 1) * half] = total
            wout_out[c] = total_o
            for d in range(1, N_DEV):
                small_copy(d).wait_recv()
            small_out[...] = small_land[...]
            pair_copy(0, pair_in).wait_recv()
            for core in range(2):
                @pl.when(c == core)
                def _(core=core):
                    win_out[:, (1 - core) * half:(2 - core) * half] = pair_in[1]
            pair_copy(1, pair_o).wait_recv()
            wout_out[1 - c] = pair_o[1]
            for d in range(1, N_DEV):
                small_copy(d).wait_send()
            for k in range(N_CHIP):
                level1(k).wait_send()
                out_level1(k).wait_send()
            for k in range(n_rel):
                level2(k).wait_send()
                out_level2(k).wait_send()
            pair_copy(0, pair_in).wait_send()
            pair_copy(1, pair_o).wait_send()

    vmem = pl.BlockSpec(memory_space=pltpu.VMEM)
    dma = pltpu.SemaphoreType.DMA
    sds = jax.ShapeDtypeStruct
    grid_spec = pltpu.PrefetchScalarGridSpec(
        num_scalar_prefetch=1, grid=(N_CHIP, n_i),
        in_specs=[pl.BlockSpec((D_MODEL, tk), lambda j, i, order: (0, i)),
                  pl.BlockSpec((None, tk, WIN), lambda j, i, order: (order[j], i, 0)), vmem, vmem],
        out_specs=(vmem, vmem, vmem),
        scratch_shapes=[
            pltpu.VMEM((D_MODEL, WIN), F32), pltpu.VMEM((half, WIN), F32),
            pltpu.VMEM((N_CHIP, half, WIN), BF16), pltpu.VMEM((N_CHIP, half, WIN), BF16),
            pltpu.VMEM((n_rel, half, WIN), BF16), pltpu.VMEM((n_rel, half, WIN), BF16),
            pltpu.VMEM((N_CHIP, o_half, D_MODEL), BF16), pltpu.VMEM((N_CHIP, o_half, D_MODEL), BF16),
            pltpu.VMEM((n_rel, o_half, D_MODEL), BF16), pltpu.VMEM((n_rel, o_half, D_MODEL), BF16),
            pltpu.VMEM((2, WIN, half), F32), pltpu.VMEM((2, o_half, D_MODEL), F32),
            pltpu.VMEM((N_DEV, PACK_ROWS, D_MODEL), F32),
            dma((N_CHIP,)), dma((N_CHIP,)), dma((n_rel,)), dma((n_rel,)),
            dma((N_CHIP,)), dma((N_CHIP,)), dma((n_rel,)), dma((n_rel,)),
            dma((2,)), dma((2,)), dma((N_DEV,)), dma((N_DEV,))])
    return pl.pallas_call(
        body, name="grad_w_in_reduce", grid_spec=grid_spec,
        out_shape=(sds((WIN, D_MODEL), F32), sds((2, o_half, D_MODEL), F32), sds((N_DEV, PACK_ROWS, D_MODEL), F32)),
        compiler_params=pltpu.CompilerParams(dimension_semantics=("arbitrary", "arbitrary"),
                                             vmem_limit_bytes=VMEM_LIMIT),
    )(order, h_t, wins, gw_out, small)


ADAM_STEPS = 4


def _adamw_math(w, g, m, v):
    m = ADAM_B1 * m + (1.0 - ADAM_B1) * g
    v = ADAM_B2 * v + (1.0 - ADAM_B2) * (g * g)
    m_hat = m / (1.0 - ADAM_B1 ** ADAM_STEP)
    v_hat = v / (1.0 - ADAM_B2 ** ADAM_STEP)
    delta = -ADAM_LR * (m_hat / (jnp.sqrt(v_hat) + ADAM_EPS) + ADAM_WD * w)
    return delta, m, v


def _adamw(w, g, m, v, name):
    r, c = w.shape

    def body(w_ref, g_ref, m_ref, v_ref, d_ref, nm_ref, nv_ref):
        delta, nm, nv = _adamw_math(w_ref[...], g_ref[...], m_ref[...], v_ref[...])
        d_ref[...] = delta
        nm_ref[...] = nm
        nv_ref[...] = nv

    rows = r // ADAM_STEPS
    assert rows * ADAM_STEPS == r and rows % 8 == 0
    spec = pl.BlockSpec((rows, c), lambda i: (i, 0))
    shape = jax.ShapeDtypeStruct((r, c), F32)
    return pl.pallas_call(
        body, name=name, grid=(ADAM_STEPS,), in_specs=[spec] * 4, out_specs=(spec,) * 3,
        out_shape=(shape,) * 3, compiler_params=pltpu.CompilerParams(vmem_limit_bytes=VMEM_LIMIT),
    )(w, g, m, v)


def _adamw_window(w, window, shift, m, v, name):
    r, c = w.shape
    rows = r // ADAM_STEPS
    assert rows * ADAM_STEPS == r and rows % 8 == 0

    def body(shift_ref, w_ref, win_hbm, m_ref, v_ref, g_ref, d_ref, nm_ref, nv_ref, g_vmem):
        start = pl.multiple_of(shift_ref[0] + pl.program_id(0) * rows, 8)
        pltpu.sync_copy(win_hbm.at[pl.ds(start, rows)], g_vmem)
        g = g_vmem[...]
        g_ref[...] = g
        delta, nm, nv = _adamw_math(w_ref[...], g, m_ref[...], v_ref[...])
        d_ref[...] = delta
        nm_ref[...] = nm
        nv_ref[...] = nv

    spec = pl.BlockSpec((rows, c), lambda i, shift_ref: (i, 0))
    shape = jax.ShapeDtypeStruct((r, c), F32)
    grid_spec = pltpu.PrefetchScalarGridSpec(
        num_scalar_prefetch=1, grid=(ADAM_STEPS,),
        in_specs=[spec, pl.BlockSpec(memory_space=pl.ANY), spec, spec], out_specs=(spec,) * 4,
        scratch_shapes=[pltpu.VMEM((rows, c), F32)])
    return pl.pallas_call(
        body, name=name, grid_spec=grid_spec, out_shape=(shape,) * 4,
        compiler_params=pltpu.CompilerParams(vmem_limit_bytes=VMEM_LIMIT),
    )(shift, w, window, m, v)


PACK_ROWS = 8


def _fold_heads(v):
    y = v[:, 0:LANES]
    for j in range(1, v.shape[1] // LANES):
        y = y + v[:, j * LANES:(j + 1) * LANES]
    return y + pltpu.roll(y, HEAD_DIM, 1)


N_SMALL = 6


def _small_adamw(recv, weights, m, v):
    def body(*refs):
        r_ref = refs[0]
        w_refs, m_refs, v_refs = (refs[1 + n * N_SMALL:1 + (n + 1) * N_SMALL] for n in range(3))
        outs = refs[1 + 3 * N_SMALL:]
        g_refs, d_refs, nm_refs, nv_refs = (outs[n * N_SMALL:(n + 1) * N_SMALL] for n in range(4))
        loss_ref = outs[4 * N_SMALL]
        tot = r_ref[0]
        for j in range(1, N_DEV):
            tot = tot + r_ref[j]
        loss_ref[...] = tot[3:4, 0:LANES]
        row1 = tot[1:2, :]
        row2 = tot[2:3, :]
        grads = [tot[0:1, :],
                 _fold_heads(row1[:, 0:WIDTH])[:, :HEAD_DIM],
                 _fold_heads(row2[:, WIDTH:WIDTH + A_KV_WIDTH])[:, :HEAD_DIM],
                 row2[:, WIDTH + A_KV_WIDTH:WIDTH + A_KV_WIDTH + HEADS],
                 _fold_heads(row1[:, WIDTH:2 * WIDTH])[:, :HEAD_DIM],
                 _fold_heads(row2[:, 0:WIDTH])[:, :HEAD_DIM]]
        for n, g in enumerate(grads):
            g_refs[n][...] = g
            delta, nm, nv = _adamw_math(w_refs[n][...], g, m_refs[n][...], v_refs[n][...])
            d_refs[n][...] = delta
            nm_refs[n][...] = nm
            nv_refs[n][...] = nv

    shapes = tuple(jax.ShapeDtypeStruct(a.shape, F32) for a in weights)
    outs = pl.pallas_call(body, name="small_adamw", out_shape=shapes * 4 + (jax.ShapeDtypeStruct((1, LANES), F32),)
                          )(recv, *weights, *m, *v)
    return tuple(outs[n * N_SMALL:(n + 1) * N_SMALL] for n in range(4)) + (outs[4 * N_SMALL],)


def kernel(x, norm_gain, w_in, q_norm_a, k_norm_a, sinks_a, q_norm_b, k_norm_b, w_out, loss_target, m_norm_gain, m_w_in, m_q_norm_a, m_k_norm_a, m_sinks_a, m_q_norm_b, m_k_norm_b, m_w_out, v_norm_gain, v_w_in, v_q_norm_a, v_k_norm_a, v_sinks_a, v_q_norm_b, v_k_norm_b, v_w_out):
    chip = 2 * lax.axis_index("x") + lax.axis_index("y")

    w_in_t, m_w_in_t, v_w_in_t = w_in[0].T, m_w_in[0].T, v_w_in[0].T

    (w_in_all,) = _gather_weights([w_in_t], "gather_weights")
    w_in_bf = w_in_all.reshape(IN_WIDTH, D_MODEL)

    loss_part, gx, h_t, wins, gw_out, (dgain, dgqa, dgka, dsink, dgqb, dgkb) = _local_step(
        x[0], loss_target[0], norm_gain, w_in_bf, q_norm_a, k_norm_a, sinks_a, q_norm_b, k_norm_b, w_out[0])

    small = jnp.concatenate([
        dgain, jnp.concatenate([dgqa, dgqb], axis=1),
        jnp.concatenate([dgkb, dgka, dsink, jnp.zeros((1, D_MODEL - WIDTH - 2 * A_KV_WIDTH), F32)], axis=1),
        jnp.pad(loss_part, ((0, 0), (0, D_MODEL - LANES))),
        jnp.zeros((PACK_ROWS - 4, D_MODEL), F32)], axis=0)
    order = (chip ^ jnp.array(RELATIONS, jnp.int32)).astype(jnp.int32)
    win_sum, wout_sum, small_recv = _grad_reduce(order, h_t, wins, gw_out, small)
    shift = jnp.array(WIN_SHIFT, jnp.int32)[chip].reshape(1)
    g_w_out = wout_sum.reshape(OUT_ROWS, D_MODEL)

    g_w_in, d_w_in, nm_w_in, nv_w_in = (
        a.T for a in _adamw_window(w_in_t, win_sum, shift, m_w_in_t, v_w_in_t, "adamw_w_in"))
    d_w_out, nm_w_out, nv_w_out = _adamw(w_out[0], g_w_out, m_w_out[0], v_w_out[0], "adamw_w_out")
    g_s, d_s, nm_s, nv_s, loss_row = _small_adamw(
        small_recv,
        (norm_gain, q_norm_a, k_norm_a, sinks_a, q_norm_b, k_norm_b),
        (m_norm_gain, m_q_norm_a, m_k_norm_a, m_sinks_a, m_q_norm_b, m_k_norm_b),
        (v_norm_gain, v_q_norm_a, v_k_norm_a, v_sinks_a, v_q_norm_b, v_k_norm_b))
    loss = loss_row[0, 0]

    def leaves(small_ones, big_in, big_out):
        return (small_ones[0], big_in[None]) + tuple(small_ones[1:]) + (big_out[None],)

    return ((loss, gx[None]) + leaves(g_s, g_w_in, g_w_out) + leaves(d_s, d_w_in, d_w_out)
            + leaves(nm_s, nm_w_in, nm_w_out) + leaves(nv_s, nv_w_in, nv_w_out))
```

```python
import jax
import jax.numpy as jnp
from jax import lax
from jax.experimental import pallas as pl
from jax.experimental.pallas import tpu as pltpu

F32 = jnp.float32
BF16 = jnp.bfloat16

D_MODEL = 1024
HEAD_DIM = 64
HEADS = 8
WIDTH = HEADS * HEAD_DIM
A_KV_WIDTH = 2 * HEAD_DIM
BLOCK = 128
LANES = 128
FOLD = 16
A_MAX_DIST = 127
B_MAX_DIST = 128
ROPE_THETA = 10000.0
ROPE_SPLIT = 64
EPS = 1e-6
NEG = -1e30
SCALE = HEAD_DIM ** -0.5

IN_WIDTH = 3328
C_QA, C_KA, C_VA, C_GA, C_QB, C_KB, C_VB, C_GB, C_END = 0, 512, 640, 768, 1280, 1792, 2304, 2816, 3328

N_DEV = 8
N_CHIP = 4
MESH = pl.DeviceIdType.MESH
WIN = 896
WIN_START = (0, 768, 1664, 2432)
WIN_SHIFT = (0, 64, 0, 64)
OUT_ROWS = D_MODEL // N_CHIP
RELATIONS = (3, 1, 2, 0)

ADAM_LR = 0.001
ADAM_B1 = 0.9
ADAM_B2 = 0.999
ADAM_EPS = 1e-08
ADAM_WD = 0.01
ADAM_STEP = 10

ROW_TILE = 256
FOLD_ROWS = ROW_TILE // FOLD
GRAD_ROWS = 1024
ACC_COLS = 256
VMEM_LIMIT = 56 * 1024 * 1024


def _dot(a, b):
    return jnp.dot(a, b, preferred_element_type=F32)


def _dot_nt(a, b):
    return lax.dot_general(a, b, (((1,), (1,)), ((), ())), preferred_element_type=F32)


def _dot_tn(a, b):
    return lax.dot_general(a, b, (((0,), (0,)), ((), ())), preferred_element_type=F32)


def _head_sum(z, bd):
    w = bd.shape[0]
    zb = z.astype(BF16)
    parts = [_dot(zb[:, a:a + w], bd) for a in range(0, z.shape[1], w)]
    return parts[0] if len(parts) == 1 else jnp.concatenate(parts, axis=1)


def _swap_halves(t):
    w = t.shape[1]
    lane = lax.broadcasted_iota(jnp.int32, t.shape, 1)
    return jnp.where(lane % HEAD_DIM < HEAD_DIM // 2, pltpu.roll(t, w - 32, 1), pltpu.roll(t, 32, 1))


def _qknorm_rope(t, g, cos, sin_s, bd):
    r = lax.rsqrt(_head_sum(t * t, bd) * (1.0 / HEAD_DIM) + EPS)
    n = (t * r) * g
    return n * cos + _swap_halves(n) * sin_s


def _qknorm_rope_bwd(dout, t, g, cos, sin_s, bd):
    dout, t = dout.astype(F32), t.astype(F32)
    dn = dout * cos + _swap_halves(dout * sin_s)
    r = lax.rsqrt(_head_sum(t * t, bd) * (1.0 / HEAD_DIM) + EPS)
    tr = t * r
    u = dn * g
    dt = r * (u - tr * (_head_sum(u * tr, bd) * (1.0 / HEAD_DIM)))
    return dt, dn * tr


def _sigmoid(g):
    return 1.0 / (1.0 + jnp.exp(-g))


def _expand_heads(st):
    t = st.shape[0]
    lane = lax.broadcasted_iota(jnp.int32, (t, LANES), 1)
    chunks = []
    for c in range(WIDTH // LANES):
        chunks.append(jnp.where(lane < HEAD_DIM, st[:, 2 * c:2 * c + 1], st[:, 2 * c + 1:2 * c + 2]))
    return jnp.concatenate(chunks, axis=1)


def _reduce_heads(z):
    t = z.shape[0]
    lane = lax.broadcasted_iota(jnp.int32, (t, LANES), 1)
    out = jnp.zeros((t, LANES), F32)
    for c in range(WIDTH // LANES):
        zc = z[:, c * LANES:(c + 1) * LANES]
        for ph in range(2):
            s = jnp.sum(jnp.where((lane // HEAD_DIM) == ph, zc, 0.0), axis=-1, keepdims=True)
            out = jnp.where(lane == 2 * c + ph, s, out)
    return out


def _fold_scratch(w):
    return pltpu.VMEM((w // LANES, ROW_TILE, LANES), F32)


def _store_folded(out_ref, val, scr, col0=0):
    w = val.shape[1]
    n = w // LANES
    for c in range(n):
        scr[c] = val[:, c * LANES:(c + 1) * LANES]
    for r in range(FOLD):
        piece = [scr[c, pl.ds(r, FOLD_ROWS, stride=FOLD), :] for c in range(n)]
        out_ref[r, :, col0:col0 + w] = (piece[0] if n == 1 else jnp.concatenate(piece, axis=1)).astype(out_ref.dtype)


def _load_folded(in_ref, scr):
    n = in_ref.shape[2] // LANES
    for r in range(FOLD):
        blk = in_ref[r].astype(F32)
        for c in range(n):
            scr[c, pl.ds(r, FOLD_ROWS, stride=FOLD), :] = blk[:, c * LANES:(c + 1) * LANES]
    return scr[0] if n == 1 else jnp.concatenate([scr[c] for c in range(n)], axis=1)


def _store_folded_bf16(out_ref, val, perm):
    folded = _dot(perm, val.astype(BF16)).astype(out_ref.dtype)
    for r in range(FOLD):
        out_ref[r] = folded[r * FOLD_ROWS:(r + 1) * FOLD_ROWS]


def _load_folded_bf16(in_ref, perm):
    blk = jnp.concatenate([in_ref[r] for r in range(FOLD)], axis=0)
    return _dot(perm, blk)


def _rows(w, tm=ROW_TILE):
    return pl.BlockSpec((tm, w), lambda i: (i, 0))


def _folded_rows(w):
    return pl.BlockSpec((FOLD, FOLD_ROWS, w), lambda i: (0, i, 0))


def _whole(shape):
    return pl.BlockSpec(shape, lambda i: (0,) * len(shape))


def _inproj(x2, gain, w_bf, cos, sin_s, gqa, gka, gqb, gkb, bd256, bd128, w_out_blk):
    s = x2.shape[0]
    tm = ROW_TILE
    n_steps = s // tm
    n_rel = N_CHIP - 1
    o_half = OUT_ROWS // 2

    def body(x_ref, gain_ref, w_hbm, cos_ref, sin_ref, gqa_ref, gka_ref, gqb_ref, gkb_ref, bd256_ref, bd128_ref,
             wout_ref, qa_ref, kva_ref, qb_ref, kvb_ref, qbf_ref, kvbf_ref,
             qa_raw_ref, ka_raw_ref, ga_ref, qb_raw_ref, kb_raw_ref, gb_ref, wout_all_ref,
             w_vmem, scr, land, ici_send, ici_recv, d2d_send, d2d_recv):
        i = pl.program_id(0)
        px_, py_, c = _position()
        b = 2 * px_ + py_

        def piece(chip_idx, core):
            return land.at[chip_idx, pl.ds(pl.multiple_of(core * o_half, o_half), o_half)]

        def other_chip(d):
            ox, oy = px_ ^ (d >> 1), py_ ^ (d & 1)
            return ox, oy, 2 * ox + oy

        def ici_copy(d, chip_idx):
            ox, oy, _ = other_chip(d)
            return pltpu.make_async_remote_copy(
                src_ref=piece(chip_idx, c), dst_ref=piece(chip_idx, c), send_sem=ici_send.at[d - 1],
                recv_sem=ici_recv.at[d - 1], device_id=(ox, oy, c), device_id_type=MESH)

        def d2d_copy(d, core):
            return pltpu.make_async_remote_copy(
                src_ref=piece(other_chip(d)[2], core), dst_ref=piece(other_chip(d)[2], core),
                send_sem=d2d_send.at[d - 1], recv_sem=d2d_recv.at[d - 1], device_id=(px_, py_, 1 - c),
                device_id_type=MESH)

        @pl.when(i == 0)
        def _():
            pltpu.sync_copy(w_hbm, w_vmem)
            land[b] = wout_ref[...].astype(BF16)
            for d in range(1, N_CHIP):
                ici_copy(d, b).start()

        @pl.when(i == n_steps // 2)
        def _():
            for d in range(1, N_CHIP):
                ici_copy(d, other_chip(d)[2]).wait_recv()
                d2d_copy(d, c).start()

        @pl.when(i == n_steps - 1)
        def _():
            for d in range(1, N_CHIP):
                d2d_copy(d, 1 - c).wait_recv()
            for d in range(1, N_CHIP):
                ici_copy(d, b).wait_send()
                d2d_copy(d, c).wait_send()
            for k in range(N_CHIP):
                wout_all_ref[k * OUT_ROWS:(k + 1) * OUT_ROWS, :] = land[k]

        xt = x_ref[...]
        r = lax.rsqrt(jnp.mean(xt * xt, axis=-1, keepdims=True) + EPS)
        h = ((xt * r) * gain_ref[...]).astype(BF16)
        cos1 = cos_ref[...]
        sin1 = sin_ref[...]
        cos4 = jnp.tile(cos1, (1, 4))
        sin4 = jnp.tile(sin1, (1, 4))

        def seg(a, b):
            return _dot_nt(h, w_vmem[a:b, :])

        t = seg(C_QA, C_KA)
        qa_raw_ref[...] = t.astype(BF16)
        qa_ref[...] = (_qknorm_rope(t, gqa_ref[...], cos4, sin4, bd256_ref[...]) * SCALE).astype(BF16)
        t = seg(C_KA, C_VA)
        ka_raw_ref[...] = t.astype(BF16)
        kva_ref[:, :A_KV_WIDTH] = _qknorm_rope(t, gka_ref[...], cos1, sin1, bd128_ref[...]).astype(BF16)
        kva_ref[:, A_KV_WIDTH:] = seg(C_VA, C_GA).astype(BF16)
        ga_ref[...] = seg(C_GA, C_QB).astype(BF16)
        t = seg(C_QB, C_KB)
        qb_raw_ref[...] = t.astype(BF16)
        t = _qknorm_rope(t, gqb_ref[...], cos4, sin4, bd256_ref[...]) * SCALE
        qb_ref[...] = t.astype(BF16)
        _store_folded(qbf_ref, t, scr)
        t = seg(C_KB, C_VB)
        kb_raw_ref[...] = t.astype(BF16)
        t = _qknorm_rope(t, gkb_ref[...], cos4, sin4, bd256_ref[...])
        kvb_ref[:, :WIDTH] = t.astype(BF16)
        _store_folded(kvbf_ref, t, scr)
        t = seg(C_VB, C_GB)
        kvb_ref[:, WIDTH:] = t.astype(BF16)
        _store_folded(kvbf_ref, t, scr, WIDTH)
        gb_ref[...] = seg(C_GB, C_END).astype(BF16)

    sds = jax.ShapeDtypeStruct
    ln = s // FOLD
    out_shape = (sds((s, WIDTH), BF16), sds((s, 2 * A_KV_WIDTH), BF16), sds((s, WIDTH), BF16),
                 sds((s, 2 * WIDTH), BF16), sds((FOLD, ln, WIDTH), BF16), sds((FOLD, ln, 2 * WIDTH), BF16),
                 sds((s, WIDTH), BF16), sds((s, A_KV_WIDTH), BF16), sds((s, WIDTH), BF16),
                 sds((s, WIDTH), BF16), sds((s, WIDTH), BF16), sds((s, WIDTH), BF16),
                 sds((D_MODEL, D_MODEL), BF16))
    out_specs = (_rows(WIDTH), _rows(2 * A_KV_WIDTH), _rows(WIDTH), _rows(2 * WIDTH),
                 _folded_rows(WIDTH), _folded_rows(2 * WIDTH),
                 _rows(WIDTH), _rows(A_KV_WIDTH), _rows(WIDTH), _rows(WIDTH), _rows(WIDTH), _rows(WIDTH),
                 _whole((D_MODEL, D_MODEL)))
    dma = pltpu.SemaphoreType.DMA
    return pl.pallas_call(
        body, name="inproj_fwd", grid=(n_steps,),
        in_specs=[_rows(D_MODEL), _whole(gain.shape), pl.BlockSpec(memory_space=pl.ANY), _rows(LANES), _rows(LANES),
                  _whole(gqa.shape), _whole(gka.shape), _whole(gqb.shape), _whole(gkb.shape), _whole(bd256.shape),
                  _whole(bd128.shape), _whole(w_out_blk.shape)],
        out_specs=out_specs, out_shape=out_shape,
        scratch_shapes=[pltpu.VMEM((IN_WIDTH, D_MODEL), BF16), _fold_scratch(WIDTH),
                        pltpu.VMEM((N_CHIP, OUT_ROWS, D_MODEL), BF16),
                        dma((n_rel,)), dma((n_rel,)), dma((n_rel,)), dma((n_rel,))],
        compiler_params=pltpu.CompilerParams(dimension_semantics=("arbitrary",), vmem_limit_bytes=VMEM_LIMIT),
    )(x2, gain, w_bf, cos, sin_s, gqa, gka, gqb, gkb, bd256, bd128, w_out_blk)


def _seq_pos(idx, dil):
    if dil == 4:
        return 4 * (idx % 32) + idx // 32
    return idx


def _upper_mask(dil, r0=0, rows=2 * BLOCK):
    qi = (lax.broadcasted_iota(jnp.int32, (rows, BLOCK), 0) + r0) % BLOCK
    kj = lax.broadcasted_iota(jnp.int32, (rows, BLOCK), 1)
    return _seq_pos(kj, dil) > _seq_pos(qi, dil)


def _eye_mask(r0=0, rows=2 * BLOCK):
    qi = (lax.broadcasted_iota(jnp.int32, (rows, BLOCK), 0) + r0) % BLOCK
    kj = lax.broadcasted_iota(jnp.int32, (rows, BLOCK), 1)
    return qi == kj


def _stack_heads(a2, c, gqa):
    lane = lax.broadcasted_iota(jnp.int32, (1, LANES), 1) // HEAD_DIM
    zero = jnp.zeros_like(a2)
    if gqa:
        keep = lane == (c // 2)
        return jnp.concatenate([jnp.where(keep, a2, zero), jnp.where(keep, _swap_heads(a2), zero)], axis=0)
    return jnp.concatenate([jnp.where(lane == 0, a2, zero), jnp.where(lane == 1, a2, zero)], axis=0)


def _unstack_heads(a, c, gqa):
    lane = lax.broadcasted_iota(jnp.int32, (1, LANES), 1) // HEAD_DIM
    if gqa:
        return jnp.where(lane == (c // 2), a[:BLOCK], _swap_heads(a[BLOCK:]))
    return jnp.where(lane == 0, a[:BLOCK], a[BLOCK:])


def _stacked_head_ids(c, gqa):
    if gqa:
        return 2 * c + c // 2, 2 * c + 1 - c // 2
    return 2 * c, 2 * c + 1


def _per_head_rows(blk, heads):
    return jnp.concatenate([blk[:, heads[0]:heads[0] + 1], blk[:, heads[1]:heads[1] + 1]], axis=0)


def _attn_view(a, dil):
    if dil == 1:
        return a[None]
    if dil == 4:
        return a.reshape(4, 4, a.shape[1], a.shape[2])
    return a


def _attn_unview(a, dil):
    if dil == 1:
        return a[0]
    if dil == 4:
        return a.reshape(FOLD, a.shape[2], a.shape[3])
    return a


ATTN_BLOCKS_PER_STEP = 8


def _attn_specs(dil):
    if dil == 4:
        def spec(n, fn):
            return lambda w: pl.BlockSpec((4, None, n * BLOCK // 4, w), lambda r, i: (0, r, fn(i), 0))
    else:
        def spec(n, fn):
            return lambda w: pl.BlockSpec((None, n * BLOCK, w), lambda r, i: (r, fn(i), 0))
    return spec


def _blk_rows(g, dil):
    n = BLOCK // 4 if dil == 4 else BLOCK
    if isinstance(g, int):
        return slice(g * n, (g + 1) * n)
    return pl.ds(pl.multiple_of(g * n, n), n)


def _blk_load(ref, sl, dil, g=0):
    if dil == 4:
        return ref[:, _blk_rows(g, dil), sl].reshape(BLOCK, sl.stop - sl.start)
    return ref[_blk_rows(g, dil), sl]


def _blk_store(ref, sl, val, dil, g=0):
    val = val.astype(ref.dtype)
    if dil == 4:
        ref[:, _blk_rows(g, dil), sl] = val.reshape(4, BLOCK // 4, sl.stop - sl.start)
    else:
        ref[_blk_rows(g, dil), sl] = val


def _swap_heads(a):
    return pltpu.roll(a.astype(F32), HEAD_DIM, 1).astype(a.dtype)


STAT_SHIFT = 8


def _attn_fwd(q, kv, sinks, *, dil, max_dist, name):
    q, kv = _attn_view(q, dil), _attn_view(kv, dil)
    kw = kv.shape[-1] // 2
    gqa = kw == A_KV_WIDTH
    n_seq = dil
    nb = (q.shape[-2] * (4 if dil == 4 else 1)) // BLOCK
    per_step = min(ATTN_BLOCKS_PER_STEP, nb)
    with_sinks = sinks is not None
    all_lanes = slice(0, LANES)
    assert max_dist in (BLOCK - 1, BLOCK) and nb % per_step == 0 and (per_step == 1 or per_step % 2 == 0)
    diag = max_dist == BLOCK

    def body(*refs):
        if with_sinks:
            q_ref, kvp_ref, kvc_ref, sink_ref, o_ref, ml_ref = refs
        else:
            q_ref, kvp_ref, kvc_ref, o_ref, ml_ref = refs

        chunks = range(WIDTH // LANES)

        def matmuls_in(g, has_prev):
            prev_ref, prev_g = (kvp_ref, 0) if (isinstance(g, int) and g == 0) else (kvc_ref, g - 1)
            scores, values = [], []
            for c in chunks:
                sl = slice(c * LANES, (c + 1) * LANES)
                ksl = slice(0, LANES) if gqa else sl
                vsl = slice(ksl.start + kw, ksl.stop + kw)
                kcur, vcur = _blk_load(kvc_ref, ksl, dil, g), _blk_load(kvc_ref, vsl, dil, g)
                qs = _stack_heads(_blk_load(q_ref, sl, dil, g), c, gqa)
                if has_prev:
                    kcur = jnp.concatenate([_blk_load(prev_ref, ksl, dil, prev_g), kcur], axis=0)
                    vcur = jnp.concatenate([_blk_load(prev_ref, vsl, dil, prev_g), vcur], axis=0)
                scores.append(_dot_nt(qs, kcur))
                values.append(vcur)
            return scores, values

        def tile_ops(has_prev, scores):
            lane = lax.broadcasted_iota(jnp.int32, (1, LANES), 1)
            with_diag = diag and has_prev
            upper, eye = _upper_mask(dil), _eye_mask()
            first_rows = lax.broadcasted_iota(jnp.int32, (2 * BLOCK, 1), 0) < BLOCK
            ml_blk = jnp.zeros((BLOCK, LANES), F32)
            probs = []
            for c in chunks:
                heads = _stacked_head_ids(c, gqa)
                s = scores[c]
                if has_prev:
                    s_p = s[:, :BLOCK]
                    sc = jnp.where(upper, s_p, s[:, BLOCK:])
                else:
                    sc = jnp.where(upper, NEG, s)
                if with_diag:
                    sd = jnp.where(eye, s_p, NEG)
                    m = jnp.max(jnp.maximum(sc, sd), axis=-1, keepdims=True)
                else:
                    m = jnp.max(sc, axis=-1, keepdims=True)
                if with_sinks:
                    sk = jnp.where(first_rows, sink_ref[0, heads[0]], sink_ref[0, heads[1]])
                    m = jnp.maximum(m, sk)
                p = jnp.exp(sc - m)
                zero = jnp.zeros_like(p)
                if with_diag:
                    pd = jnp.exp(sd - m)
                    l = jnp.sum(p + pd, axis=-1, keepdims=True)
                else:
                    pd = zero
                    l = jnp.sum(p, axis=-1, keepdims=True)
                if with_sinks:
                    l = l + jnp.exp(sk - m)
                pf = jnp.where(upper, zero, p)
                if has_prev:
                    pf = jnp.concatenate([jnp.where(upper, p, pd), pf], axis=1)
                probs.append(pf.astype(BF16))
                for n, h in enumerate(heads):
                    rows = slice(n * BLOCK, (n + 1) * BLOCK)
                    ml_blk = jnp.where(lane == h, m[rows], ml_blk)
                    ml_blk = jnp.where(lane == h + STAT_SHIFT, l[rows], ml_blk)
            return probs, ml_blk

        def matmuls_out(g, values, probs, ml_blk):
            for c in chunks:
                sl = slice(c * LANES, (c + 1) * LANES)
                _blk_store(o_ref, sl, _unstack_heads(_dot(probs[c], values[c]), c, gqa), dil, g)
            _blk_store(ml_ref, all_lanes, ml_blk, dil, g)

        def run(blocks):
            ins = [matmuls_in(g, has_prev) for g, has_prev in blocks]
            mids = [tile_ops(has_prev, scores) for (_, has_prev), (scores, _) in zip(blocks, ins)]
            for (g, _), (_, values), (probs, ml_blk) in zip(blocks, ins, mids):
                matmuls_out(g, values, probs, ml_blk)

        second = [(1, True)] if per_step > 1 else []

        @pl.when(pl.program_id(1) == 0)
        def _():
            run([(0, False)] + second)

        @pl.when(pl.program_id(1) > 0)
        def _():
            run([(0, True)] + second)

        if per_step > 2:
            def rest(pair, carry):
                run([(2 * pair, True), (2 * pair + 1, True)])
                return carry

            lax.fori_loop(1, per_step // 2, rest, 0)

    spec = _attn_specs(dil)
    cur = spec(per_step, lambda i: i)
    prev = spec(1, lambda i: jnp.maximum(i * per_step - 1, 0))
    in_specs = [cur(WIDTH), prev(2 * kw), cur(2 * kw)]
    args = [q, kv, kv]
    if with_sinks:
        in_specs.append(pl.BlockSpec(memory_space=pltpu.SMEM))
        args.append(sinks)
    stats = jax.ShapeDtypeStruct(q.shape[:-1] + (LANES,), F32)
    o, ml = pl.pallas_call(
        body, name=name, grid=(n_seq, nb // per_step), in_specs=in_specs,
        out_specs=(cur(WIDTH), cur(LANES)),
        out_shape=(jax.ShapeDtypeStruct(q.shape, BF16), stats),
        compiler_params=pltpu.CompilerParams(dimension_semantics=("arbitrary", "arbitrary"),
                                             vmem_limit_bytes=VMEM_LIMIT),
    )(*args)
    return _attn_unview(o, dil), _attn_unview(ml, dil)


def _attn_bwd(q, kv, do, ld, *, dil, max_dist, name, onto=None):
    q, kv, do, ld = (_attn_view(a, dil) for a in (q, kv, do, ld))
    onto = () if onto is None else tuple(_attn_view(a, dil) for a in onto)
    kw = kv.shape[-1] // 2
    gqa = kw == A_KV_WIDTH
    n_seq = dil
    nb = (q.shape[-2] * (4 if dil == 4 else 1)) // BLOCK
    n_kc = kw // LANES
    per_step = min(ATTN_BLOCKS_PER_STEP, nb)
    all_lanes = slice(0, LANES)
    assert max_dist in (BLOCK - 1, BLOCK) and nb % per_step == 0 and (per_step == 1 or per_step % 2 == 0)
    diag = max_dist == BLOCK

    def body(q_ref, kvp_ref, kvc_ref, do_ref, ld_ref, *rest_refs):
        dq_ref, dkv_ref, ck_ref, cv_ref = rest_refs[len(onto):]
        i = pl.program_id(1)

        def store(ref, sl, val, blk):
            if onto:
                val = val + _blk_load(rest_refs[0 if ref is dq_ref else 1], sl, dil, blk).astype(F32)
            _blk_store(ref, sl, val, dil, blk)

        chunks = range(WIDTH // LANES)

        def matmuls_in(g, has_prev):
            prev_ref, prev_g = (kvp_ref, 0) if (isinstance(g, int) and g == 0) else (kvc_ref, g - 1)
            operands, products = [], []
            for c in chunks:
                sl = slice(c * LANES, (c + 1) * LANES)
                kc = 0 if gqa else c
                ksl = slice(kc * LANES, (kc + 1) * LANES)
                vsl = slice(ksl.start + kw, ksl.stop + kw)
                k2, v2 = _blk_load(kvc_ref, ksl, dil, g), _blk_load(kvc_ref, vsl, dil, g)
                if has_prev:
                    k2 = jnp.concatenate([_blk_load(prev_ref, ksl, dil, prev_g), k2], axis=0)
                    v2 = jnp.concatenate([_blk_load(prev_ref, vsl, dil, prev_g), v2], axis=0)
                qs = _stack_heads(_blk_load(q_ref, sl, dil, g), c, gqa)
                dos = _stack_heads(_blk_load(do_ref, sl, dil, g), c, gqa)
                operands.append((qs, dos, k2))
                products.append((_dot_nt(qs, k2), _dot_nt(dos, v2)))
            return operands, products

        def tile_ops(g, has_prev, products):
            upper, eye = _upper_mask(dil), _eye_mask()
            ld_blk = _blk_load(ld_ref, all_lanes, dil, g)
            weights = []
            for c in chunks:
                heads = _stacked_head_ids(c, gqa)
                lse2 = _per_head_rows(ld_blk, heads)
                dl2 = _per_head_rows(ld_blk, tuple(h + STAT_SHIFT for h in heads))
                s, dp = products[c]
                if has_prev:
                    s_p, dp_p = s[:, :BLOCK], dp[:, :BLOCK]
                    sc = jnp.where(upper, s_p, s[:, BLOCK:])
                    dpc = jnp.where(upper, dp_p, dp[:, BLOCK:])
                else:
                    sc = jnp.where(upper, NEG, s)
                    dpc = dp
                p = jnp.exp(sc - lse2)
                ds = p * (dpc - dl2)
                zero = jnp.zeros_like(p)
                pf = jnp.where(upper, zero, p)
                dsf = jnp.where(upper, zero, ds)
                if has_prev:
                    if diag:
                        pd = jnp.exp(jnp.where(eye, s_p, NEG) - lse2)
                        dsd = pd * (dp_p - dl2)
                    else:
                        pd = dsd = zero
                    pf = jnp.concatenate([jnp.where(upper, p, pd), pf], axis=1)
                    dsf = jnp.concatenate([jnp.where(upper, ds, dsd), dsf], axis=1)
                weights.append((pf.astype(BF16), dsf.astype(BF16)))
            return weights

        def matmuls_out(g, has_prev, operands, weights):
            seq_blk = i * per_step + g
            dk_acc = [None] * n_kc
            dv_acc = [None] * n_kc
            for c in chunks:
                sl = slice(c * LANES, (c + 1) * LANES)
                kc = 0 if gqa else c
                qs, dos, k2 = operands[c]
                pf, dsf = weights[c]
                store(dq_ref, sl, _unstack_heads(_dot(dsf, k2), c, gqa) * SCALE, g)
                dk2 = _dot_tn(dsf, qs)
                dv2 = _dot_tn(pf, dos)
                dk_acc[kc] = dk2 if dk_acc[kc] is None else dk_acc[kc] + dk2
                dv_acc[kc] = dv2 if dv_acc[kc] is None else dv_acc[kc] + dv2
            for kc in range(n_kc):
                sl = slice(kc * LANES, (kc + 1) * LANES)
                vsl = slice(sl.start + kw, sl.stop + kw)
                if has_prev:
                    store(dkv_ref, sl, ck_ref[:, sl] + dk_acc[kc][:BLOCK], seq_blk - 1)
                    store(dkv_ref, vsl, cv_ref[:, sl] + dv_acc[kc][:BLOCK], seq_blk - 1)
                    ck_ref[:, sl] = dk_acc[kc][BLOCK:]
                    cv_ref[:, sl] = dv_acc[kc][BLOCK:]
                else:
                    ck_ref[:, sl] = dk_acc[kc]
                    cv_ref[:, sl] = dv_acc[kc]

        def run(blocks):
            ins = [matmuls_in(g, has_prev) for g, has_prev in blocks]
            mids = [tile_ops(g, has_prev, products) for (g, has_prev), (_, products) in zip(blocks, ins)]
            for (g, has_prev), (operands, _), weights in zip(blocks, ins, mids):
                matmuls_out(g, has_prev, operands, weights)

        second = [(1, True)] if per_step > 1 else []

        @pl.when(i == 0)
        def _():
            run([(0, False)] + second)

        @pl.when(i > 0)
        def _():
            run([(0, True)] + second)

        if per_step > 2:
            def rest(pair, carry):
                run([(2 * pair, True), (2 * pair + 1, True)])
                return carry

            lax.fori_loop(1, per_step // 2, rest, 0)

        @pl.when(i == nb // per_step - 1)
        def _():
            for kc in range(n_kc):
                sl = slice(kc * LANES, (kc + 1) * LANES)
                store(dkv_ref, sl, ck_ref[:, sl], nb - 1)
                store(dkv_ref, slice(sl.start + kw, sl.stop + kw), cv_ref[:, sl], nb - 1)

    spec = _attn_specs(dil)
    cur = spec(per_step, lambda i: i)
    prev = spec(1, lambda i: jnp.maximum(i * per_step - 1, 0))
    if dil == 4:
        whole = pl.BlockSpec((4, None, kv.shape[2], 2 * kw), lambda r, i: (0, r, 0, 0))
    else:
        whole = pl.BlockSpec((None, kv.shape[1], 2 * kw), lambda r, i: (r, 0, 0))
    sds = jax.ShapeDtypeStruct
    dq, dkv = pl.pallas_call(
        body, name=name, grid=(n_seq, nb // per_step),
        in_specs=[cur(WIDTH), prev(2 * kw), cur(2 * kw), cur(WIDTH), cur(LANES)] + [cur(WIDTH), whole][:len(onto)],
        out_specs=(cur(WIDTH), whole),
        out_shape=(sds(q.shape, BF16), sds(kv.shape, BF16)),
        scratch_shapes=[pltpu.VMEM((BLOCK, kw), F32), pltpu.VMEM((BLOCK, kw), F32)],
        compiler_params=pltpu.CompilerParams(dimension_semantics=("arbitrary", "arbitrary"),
                                             vmem_limit_bytes=VMEM_LIMIT),
    )(q, kv, kv, do, ld, *onto)
    return _attn_unview(dq, dil), _attn_unview(dkv, dil)


def _outproj(att_a, att_b1, att_b4, att_b16, g_a, g_b, x2, tgt2, w_out_bf, sink_row, perm):
    s = x2.shape[0]
    tm = ROW_TILE

    def body(oa_ref, mla_ref, ob1_ref, ml1_ref, ob4_ref, ml4_ref, ob16_ref, ml16_ref,
             ga_ref, gb_ref, x_ref, t_ref, w_ref, sink_ref, perm_ref,
             dy_ref, doa_ref, dob_ref, dobf_ref, dga_ref, dgb_ref, lda_ref, ldb_ref, ldbf_ref,
             gw_ref, loss_ref, dsink_ref, scr_st):
        i = pl.program_id(0)
        perm = perm_ref[...]
        lane = lax.broadcasted_iota(jnp.int32, (tm, LANES), 1)
        used = lane < HEADS

        def split(ml):
            return jnp.where(used, ml, 0.0), jnp.where(used, pltpu.roll(ml, LANES - STAT_SHIFT, 1), 1.0)

        @pl.when(i == 0)
        def _():
            gw_ref[...] = jnp.zeros_like(gw_ref)
            loss_ref[...] = jnp.zeros_like(loss_ref)
            dsink_ref[...] = jnp.zeros_like(dsink_ref)

        ms, ls = zip(split(ml1_ref[...]), split(_load_folded(ml4_ref, scr_st)), split(_load_folded(ml16_ref, scr_st)))
        mx = jnp.maximum(jnp.maximum(ms[0], ms[1]), ms[2])
        scale = [jnp.exp(mp - mx) for mp in ms]
        den = (ls[0] * scale[0] + ls[1] * scale[1]) + ls[2] * scale[2]
        lse_b = jnp.where(used, mx + jnp.log(den), 0.0)
        inv_den = 1.0 / den
        o_b = _expand_heads(scale[0] * inv_den) * ob1_ref[...].astype(F32)
        o_b = o_b + _expand_heads(scale[1] * inv_den) * _load_folded_bf16(ob4_ref, perm)
        o_b = o_b + _expand_heads(scale[2] * inv_den) * _load_folded_bf16(ob16_ref, perm)
        m_a, l_a = split(mla_ref[...])
        lse_a = jnp.where(used, m_a + jnp.log(l_a), 0.0)
        o_a = _expand_heads(1.0 / l_a) * oa_ref[...].astype(F32)
        g_a = ga_ref[...].astype(F32)
        g_b = gb_ref[...].astype(F32)
        sg_a = _sigmoid(g_a)
        sg_b = _sigmoid(g_b)
        silu_a = g_a * sg_a
        silu_b = g_b * sg_b
        mixed = jnp.concatenate([o_a * silu_a, o_b * silu_b], axis=1).astype(BF16)
        w = w_ref[...]
        y = x_ref[...] + _dot(mixed, w)
        diff = y - t_ref[...]
        loss_ref[...] += (0.5 / D_MODEL) * jnp.sum(diff * diff)
        dy = diff * (1.0 / D_MODEL)
        dy_ref[...] = dy
        dyb = dy.astype(BF16)
        gw_ref[...] += _dot_tn(mixed, dyb)
        dmixed = _dot_nt(dyb, w)
        dm_a = dmixed[:, :WIDTH]
        dm_b = dmixed[:, WIDTH:]
        do_a = dm_a * silu_a
        do_b = dm_b * silu_b
        doa_ref[...] = do_a.astype(BF16)
        dob_ref[...] = do_b.astype(BF16)
        _store_folded_bf16(dobf_ref, do_b, perm)
        dga_ref[...] = (dm_a * o_a * (sg_a * (1.0 + g_a * (1.0 - sg_a)))).astype(BF16)
        dgb_ref[...] = (dm_b * o_b * (sg_b * (1.0 + g_b * (1.0 - sg_b)))).astype(BF16)
        dl_a = _reduce_heads(do_a * o_a)
        dl_b = _reduce_heads(do_b * o_b)
        lda_ref[...] = lse_a + pltpu.roll(dl_a, STAT_SHIFT, 1)
        ld_b = lse_b + pltpu.roll(dl_b, STAT_SHIFT, 1)
        ldb_ref[...] = ld_b
        _store_folded(ldbf_ref, ld_b, scr_st)
        dsink_ref[...] -= jnp.sum(jnp.exp(sink_ref[...] - lse_a) * dl_a, axis=0, keepdims=True)

    sds = jax.ShapeDtypeStruct
    ln = s // FOLD
    natural = [_rows(WIDTH), _rows(LANES)]
    folded = [_folded_rows(WIDTH), _folded_rows(LANES)]
    return pl.pallas_call(
        body, name="outproj_fwd_bwd", grid=(s // tm,),
        in_specs=natural + natural + folded + folded
                 + [_rows(WIDTH), _rows(WIDTH), _rows(D_MODEL), _rows(D_MODEL), _whole((D_MODEL, D_MODEL)),
                    _whole((1, LANES)), _whole(perm.shape)],
        out_specs=(_rows(D_MODEL), _rows(WIDTH), _rows(WIDTH), _folded_rows(WIDTH), _rows(WIDTH), _rows(WIDTH),
                   _rows(LANES), _rows(LANES), _folded_rows(LANES),
                   _whole((D_MODEL, D_MODEL)), _whole((1, LANES)), _whole((1, LANES))),
        out_shape=(sds((s, D_MODEL), F32), sds((s, WIDTH), BF16), sds((s, WIDTH), BF16),
                   sds((FOLD, ln, WIDTH), BF16), sds((s, WIDTH), BF16), sds((s, WIDTH), BF16),
                   sds((s, LANES), F32), sds((s, LANES), F32), sds((FOLD, ln, LANES), F32),
                   sds((D_MODEL, D_MODEL), F32), sds((1, LANES), F32), sds((1, LANES), F32)),
        scratch_shapes=[_fold_scratch(LANES)],
        compiler_params=pltpu.CompilerParams(dimension_semantics=("arbitrary",), vmem_limit_bytes=VMEM_LIMIT),
    )(*att_a, *att_b1, *att_b4, *att_b16, g_a, g_b, x2, tgt2, w_out_bf, sink_row, perm)


def _inproj_bwd(x2, dy, gain, w_bf, cos, sin_s, gqa, gka, gqb, gkb, bd256, bd128, perm,
                qa_raw, ka_raw, qb_raw, kb_raw, d_a, d_b1, d_bf, dg_a, dg_b):
    s = x2.shape[0]
    tm = ROW_TILE

    def body(x_ref, dy_ref, gain_ref, w_hbm, cos_ref, sin_ref, gqa_ref, gka_ref, gqb_ref, gkb_ref, bd256_ref,
             bd128_ref, perm_ref, qa_raw_ref, ka_raw_ref, qb_raw_ref, kb_raw_ref, dqa_ref, dkva_ref,
             dq1_ref, dkv1_ref, dqf_ref, dkvf_ref, dga_ref, dgb_ref,
             gx_ref, ht_ref, win_ref,
             dgain_ref, dgqa_ref, dgka_ref, dgqb_ref, dgkb_ref, w_vmem, dproj_ref):
        i = pl.program_id(0)
        perm = perm_ref[...]

        @pl.when(i == 0)
        def _():
            pltpu.sync_copy(w_hbm, w_vmem)
            dgain_ref[...] = jnp.zeros_like(dgain_ref)
            dgqa_ref[...] = jnp.zeros_like(dgqa_ref)
            dgka_ref[...] = jnp.zeros_like(dgka_ref)
            dgqb_ref[...] = jnp.zeros_like(dgqb_ref)
            dgkb_ref[...] = jnp.zeros_like(dgkb_ref)

        cos1 = cos_ref[...]
        sin1 = sin_ref[...]
        cos4 = jnp.tile(cos1, (1, 4))
        sin4 = jnp.tile(sin1, (1, 4))

        dt, dg = _qknorm_rope_bwd(dqa_ref[...], qa_raw_ref[...], gqa_ref[...], cos4, sin4, bd256_ref[...])
        dproj_ref[:, C_QA:C_KA] = dt.astype(BF16)
        dgqa_ref[...] += jnp.sum(dg, axis=0, keepdims=True)
        dt, dg = _qknorm_rope_bwd(dkva_ref[:, :A_KV_WIDTH], ka_raw_ref[...], gka_ref[...], cos1, sin1,
                                  bd128_ref[...])
        dproj_ref[:, C_KA:C_VA] = dt.astype(BF16)
        dgka_ref[...] += jnp.sum(dg, axis=0, keepdims=True)
        dproj_ref[:, C_VA:C_GA] = dkva_ref[:, A_KV_WIDTH:]
        dproj_ref[:, C_GA:C_QB] = dga_ref[...]
        dq = dq1_ref[...].astype(F32) + _load_folded_bf16(dqf_ref, perm)
        dt, dg = _qknorm_rope_bwd(dq, qb_raw_ref[...], gqb_ref[...], cos4, sin4, bd256_ref[...])
        dproj_ref[:, C_QB:C_KB] = dt.astype(BF16)
        dgqb_ref[...] += jnp.sum(dg, axis=0, keepdims=True)
        dkv = dkv1_ref[...].astype(F32) + _load_folded_bf16(dkvf_ref, perm)
        dt, dg = _qknorm_rope_bwd(dkv[:, :WIDTH], kb_raw_ref[...], gkb_ref[...], cos4, sin4, bd256_ref[...])
        dproj_ref[:, C_KB:C_VB] = dt.astype(BF16)
        dgkb_ref[...] += jnp.sum(dg, axis=0, keepdims=True)
        dproj_ref[:, C_VB:C_GB] = dkv[:, WIDTH:].astype(BF16)
        dproj_ref[:, C_GB:C_END] = dgb_ref[...]
        for k, start in enumerate(WIN_START):
            win_ref[k] = dproj_ref[:, start:start + WIN]

        xt = x_ref[...]
        gain_row = gain_ref[...]
        r = lax.rsqrt(jnp.mean(xt * xt, axis=-1, keepdims=True) + EPS)
        xr = xt * r
        ht_ref[...] = (xr * gain_row).T.astype(BF16)
        dh = _dot(dproj_ref[...], w_vmem[...])
        dgain_ref[...] += jnp.sum(dh * xr, axis=0, keepdims=True)
        u = dh * gain_row
        gx_ref[...] = dy_ref[...] + r * (u - xr * jnp.mean(u * xr, axis=-1, keepdims=True))

    def acc_row(w):
        return pl.BlockSpec((1, w), lambda i: (0, 0))

    sds = jax.ShapeDtypeStruct
    any_spec = pl.BlockSpec(memory_space=pl.ANY)
    win_spec = pl.BlockSpec((N_CHIP, tm, WIN), lambda i: (0, i, 0))
    return pl.pallas_call(
        body, name="inproj_bwd", grid=(s // tm,),
        in_specs=[_rows(D_MODEL), _rows(D_MODEL), _whole(gain.shape), any_spec, _rows(LANES), _rows(LANES),
                  _whole(gqa.shape), _whole(gka.shape), _whole(gqb.shape), _whole(gkb.shape), _whole(bd256.shape),
                  _whole(bd128.shape), _whole(perm.shape),
                  _rows(WIDTH), _rows(A_KV_WIDTH), _rows(WIDTH), _rows(WIDTH),
                  _rows(WIDTH), _rows(2 * A_KV_WIDTH), _rows(WIDTH), _rows(2 * WIDTH)]
                 + [_folded_rows(WIDTH), _folded_rows(2 * WIDTH), _rows(WIDTH), _rows(WIDTH)],
        out_specs=(_rows(D_MODEL), pl.BlockSpec((D_MODEL, tm), lambda i: (0, i)), win_spec, acc_row(D_MODEL), acc_row(WIDTH), acc_row(A_KV_WIDTH), acc_row(WIDTH), acc_row(WIDTH)),
        out_shape=(sds((s, D_MODEL), F32), sds((D_MODEL, s), BF16), sds((N_CHIP, s, WIN), BF16),
                   sds((1, D_MODEL), F32),
                   sds((1, WIDTH), F32), sds((1, A_KV_WIDTH), F32), sds((1, WIDTH), F32), sds((1, WIDTH), F32)),
        scratch_shapes=[pltpu.VMEM((IN_WIDTH, D_MODEL), BF16), pltpu.VMEM((tm, IN_WIDTH), BF16)],
        compiler_params=pltpu.CompilerParams(dimension_semantics=("arbitrary",), vmem_limit_bytes=VMEM_LIMIT),
    )(x2, dy, gain, w_bf, cos, sin_s, gqa, gka, gqb, gkb, bd256, bd128, perm, qa_raw, ka_raw, qb_raw, kb_raw,
      *d_a, *d_b1, *d_bf, dg_a, dg_b)


def _rope_angles(s):
    half = HEAD_DIM // 2
    inv = jnp.tile(ROPE_THETA ** (-jnp.arange(half, dtype=F32) / half), 4)
    sign = jnp.tile(jnp.concatenate([-jnp.ones((half,), F32), jnp.ones((half,), F32)]), 2)
    hi = (jnp.arange(s // ROPE_SPLIT) * ROPE_SPLIT).astype(F32)[:, None] * inv[None, :]
    lo = jnp.arange(ROPE_SPLIT).astype(F32)[:, None] * inv[None, :]
    return jnp.cos(hi), jnp.sin(hi), jnp.cos(lo), jnp.sin(lo), sign[None, :]


def _table_shapes(s):
    return (jax.ShapeDtypeStruct((s, LANES), F32), jax.ShapeDtypeStruct((s, LANES), F32),
            jax.ShapeDtypeStruct((2 * LANES, 2 * LANES), BF16), jax.ShapeDtypeStruct((A_KV_WIDTH, A_KV_WIDTH), BF16),
            jax.ShapeDtypeStruct((ROW_TILE, ROW_TILE), BF16))


def _tables(in_refs, out_refs):
    ch_ref, sh_ref, cl_ref, sl_ref, sign_ref = in_refs
    cos_ref, sin_ref, bd256_ref, bd128_ref, perm_ref = out_refs
    cl, sl, sign = cl_ref[...], sl_ref[...], sign_ref[...]

    def tile(a, carry):
        rows = pl.ds(pl.multiple_of(a * ROPE_SPLIT, ROPE_SPLIT), ROPE_SPLIT)
        ch, sh = ch_ref[pl.ds(a, 1), :], sh_ref[pl.ds(a, 1), :]
        cos_ref[rows, :] = ch * cl - sh * sl
        sin_ref[rows, :] = (sh * cl + ch * sl) * sign
        return carry

    lax.fori_loop(0, cos_ref.shape[0] // ROPE_SPLIT, tile, 0)
    for ref in (bd256_ref, bd128_ref):
        head = [lax.broadcasted_iota(jnp.int32, ref.shape, d) // HEAD_DIM for d in (0, 1)]
        ref[...] = jnp.where(head[0] == head[1], 1.0, 0.0).astype(BF16)
    f = lax.broadcasted_iota(jnp.int32, perm_ref.shape, 0)
    col = lax.broadcasted_iota(jnp.int32, perm_ref.shape, 1)
    perm_ref[...] = jnp.where(col == FOLD * (f % FOLD_ROWS) + f // FOLD_ROWS, 1.0, 0.0).astype(BF16)


def _local_step(x2, tgt2, norm_gain, w_in_bf, q_norm_a, k_norm_a, sinks_a, q_norm_b, k_norm_b, w_out_blk, tables):
    cos, sin_s, bd256, bd128, perm = tables
    gqa = jnp.tile(q_norm_a, (1, HEADS))
    gka = jnp.tile(k_norm_a, (1, 2))
    gqb = jnp.tile(q_norm_b, (1, HEADS))
    gkb = jnp.tile(k_norm_b, (1, HEADS))
    sink_row = jnp.pad(sinks_a, ((0, 0), (0, LANES - HEADS)))

    (qa, kva, qb, kvb, qbf, kvbf, qa_raw, ka_raw, g_a, qb_raw, kb_raw, g_b, w_out_bf) = _inproj(
        x2, norm_gain, w_in_bf, cos, sin_s, gqa, gka, gqb, gkb, bd256, bd128, w_out_blk)

    att_a = _attn_fwd(qa, kva, sinks_a, dil=1, max_dist=A_MAX_DIST, name="attn_a_fwd")
    att_b1 = _attn_fwd(qb, kvb, None, dil=1, max_dist=B_MAX_DIST, name="attn_b1_fwd")
    att_b4 = _attn_fwd(qbf, kvbf, None, dil=4, max_dist=B_MAX_DIST, name="attn_b4_fwd")
    att_b16 = _attn_fwd(qbf, kvbf, None, dil=16, max_dist=B_MAX_DIST, name="attn_b16_fwd")

    (dy, do_a, do_b, do_bf, dg_a, dg_b, ld_a, ld_b, ld_bf, gw_out, loss_part, dsink) = _outproj(
        att_a, att_b1, att_b4, att_b16, g_a, g_b, x2, tgt2, w_out_bf, sink_row, perm)

    d_a = _attn_bwd(qa, kva, do_a, ld_a, dil=1, max_dist=A_MAX_DIST, name="attn_a_bwd")
    d_b1 = _attn_bwd(qb, kvb, do_b, ld_b, dil=1, max_dist=B_MAX_DIST, name="attn_b1_bwd")
    d_b4 = _attn_bwd(qbf, kvbf, do_bf, ld_bf, dil=4, max_dist=B_MAX_DIST, name="attn_b4_bwd")
    d_b16 = _attn_bwd(qbf, kvbf, do_bf, ld_bf, dil=16, max_dist=B_MAX_DIST, name="attn_b16_bwd", onto=d_b4)

    gx, h_t, wins, dgain, dgqa, dgka, dgqb, dgkb = _inproj_bwd(
        x2, dy, norm_gain, w_in_bf, cos, sin_s, gqa, gka, gqb, gkb, bd256, bd128, perm,
        qa_raw, ka_raw, qb_raw, kb_raw, d_a, d_b1, d_b16, dg_a, dg_b)
    return loss_part, gx, h_t, wins, gw_out, (dgain, dgqa, dgka, dsink, dgqb, dgkb)


def _position():
    return lax.axis_index("x"), lax.axis_index("y"), lax.axis_index("c")


GATHER_CHUNKS = 2


def _gather_weights(blocks, name, side_inputs, side_shapes, side_work):
    n = len(blocks)
    n_in, n_out = len(side_inputs), len(side_shapes)
    ch = GATHER_CHUNKS

    def body(*refs):
        src_refs, side_in = refs[:n], refs[n:n + n_in]
        dst_refs, side_out = refs[n + n_in:2 * n + n_in], refs[2 * n + n_in:2 * n + n_in + n_out]
        ici_send, ici_recv, hop_send, hop_recv, d2d_send, d2d_recv = refs[2 * n + n_in + n_out:]
        x, y, c = _position()
        b = 2 * x + y
        via = 2 - c
        out = 3 - via
        for k in range(n):
            dst_refs[k][b] = src_refs[k][...].astype(BF16)

        def rows(k, core, j):
            half = blocks[k].shape[0] // 2
            return pl.ds(pl.multiple_of(core * half + j * (half // ch), half // ch), half // ch)

        def chip(rel):
            return x ^ (rel >> 1), y ^ (rel & 1)

        def ici(k, j, slot, rel, send_sems, recv_sems, sem):
            px, py = chip(rel)
            piece = dst_refs[k].at[slot, rows(k, c, j)]
            return pltpu.make_async_remote_copy(src_ref=piece, dst_ref=piece, send_sem=send_sems.at[sem],
                                                recv_sem=recv_sems.at[sem], device_id=(px, py, c),
                                                device_id_type=MESH)

        def direct(k, j, slot, rel):
            return ici(k, j, slot, rel, ici_send, ici_recv, ((rel - 1) * ch + j) * n + k)

        def hop(k, j, slot, rel):
            return ici(k, j, slot, rel, hop_send, hop_recv, j * n + k)

        def d2d(k, j, rel, core):
            piece = dst_refs[k].at[b ^ rel, rows(k, core, j)]
            sem = ((rel - 1) * ch + j) * n + k
            return pltpu.make_async_remote_copy(src_ref=piece, dst_ref=piece, send_sem=d2d_send.at[sem],
                                                recv_sem=d2d_recv.at[sem], device_id=(x, y, 1 - c),
                                                device_id_type=MESH)

        pieces = [(k, j) for j in range(ch) for k in range(n)]
        for k, j in pieces:
            for rel in (1, 2):
                direct(k, j, b, rel).start()
        side_work(side_in, side_out)
        for k, j in pieces:
            direct(k, j, b ^ via, via).wait_recv()
            hop(k, j, b ^ via, out).start()
            d2d(k, j, via, c).start()
        for k, j in pieces:
            direct(k, j, b ^ out, out).wait_recv()
            d2d(k, j, out, c).start()
        for k, j in pieces:
            hop(k, j, b ^ 3, via).wait_recv()
            d2d(k, j, 3, c).start()
        for k, j in pieces:
            for rel in (1, 2, 3):
                d2d(k, j, rel, 1 - c).wait_recv()
        for k, j in pieces:
            for rel in (1, 2):
                direct(k, j, b, rel).wait_send()
            hop(k, j, b ^ via, out).wait_send()
            d2d(k, j, via, c).wait_send()
            d2d(k, j, out, c).wait_send()
            d2d(k, j, 3, c).wait_send()

    vmem_spec = pl.BlockSpec(memory_space=pltpu.VMEM)
    dma = pltpu.SemaphoreType.DMA
    out_shape = tuple(jax.ShapeDtypeStruct((N_CHIP,) + a.shape, BF16) for a in blocks) + tuple(side_shapes)
    outs = pl.pallas_call(
        body, name=name, in_specs=[vmem_spec] * (n + n_in), out_specs=tuple([vmem_spec] * (n + n_out)),
        out_shape=out_shape,
        scratch_shapes=[dma((2 * ch * n,)), dma((2 * ch * n,)), dma((ch * n,)), dma((ch * n,)),
                        dma((3 * ch * n,)), dma((3 * ch * n,))],
        compiler_params=pltpu.CompilerParams(vmem_limit_bytes=VMEM_LIMIT),
    )(*blocks, *side_inputs)
    return outs[:n], outs[n:]


def _grad_reduce(order, h_t, wins, gw_out, small):
    s = h_t.shape[1]
    tk = GRAD_ROWS
    n_i = s // tk
    half = D_MODEL // 2
    o_half = OUT_ROWS // 2
    n_rel = N_CHIP - 1

    def body(order_ref, ht_ref, win_ref, gwo_ref, small_ref,
             win_out, wout_out, small_out,
             acc, mine, s1, r1, s2, r2, so1, ro1, so2, ro2, pair_in, pair_o, small_land,
             s1_send, s1_recv, s2_send, s2_recv, o1_send, o1_recv, o2_send, o2_recv,
             pair_send, pair_recv, small_send, small_recv):
        j = pl.program_id(0)
        i = pl.program_id(1)
        x, y, c = _position()
        me = 4 * x + 2 * y + c
        sibling = (x, y, 1 - c)
        my_rows = pl.ds(pl.multiple_of(c * half, half), half)
        sib_rows = pl.ds(pl.multiple_of((1 - c) * half, half), half)

        def chip_of(rel):
            return x ^ (rel >> 1), y ^ (rel & 1)

        def level1(k):
            return pltpu.make_async_remote_copy(src_ref=s1.at[k], dst_ref=r1.at[k], send_sem=s1_send.at[k],
                                                recv_sem=s1_recv.at[k], device_id=sibling, device_id_type=MESH)

        def level2(k):
            px, py = chip_of(RELATIONS[k])
            return pltpu.make_async_remote_copy(src_ref=s2.at[k], dst_ref=r2.at[k], send_sem=s2_send.at[k],
                                                recv_sem=s2_recv.at[k], device_id=(px, py, c), device_id_type=MESH)

        def out_level1(bk):
            return pltpu.make_async_remote_copy(src_ref=so1.at[bk], dst_ref=ro1.at[bk], send_sem=o1_send.at[bk],
                                                recv_sem=o1_recv.at[bk], device_id=sibling, device_id_type=MESH)

        def out_level2(k):
            px, py = chip_of(RELATIONS[k])
            return pltpu.make_async_remote_copy(src_ref=so2.at[k], dst_ref=ro2.at[k], send_sem=o2_send.at[k],
                                                recv_sem=o2_recv.at[k], device_id=(px, py, c), device_id_type=MESH)

        def small_copy(d):
            px, py, pc = x ^ (d >> 2), y ^ ((d >> 1) & 1), c ^ (d & 1)
            return pltpu.make_async_remote_copy(src_ref=small_ref, dst_ref=small_land.at[me],
                                                send_sem=small_send.at[d], recv_sem=small_recv.at[d],
                                                device_id=(px, py, pc), device_id_type=MESH)

        def pair_copy(k, buf):
            return pltpu.make_async_remote_copy(src_ref=buf.at[0], dst_ref=buf.at[1], send_sem=pair_send.at[k],
                                                recv_sem=pair_recv.at[k], device_id=sibling, device_id_type=MESH)

        def out_rows(bk, core):
            return pl.ds(pl.multiple_of(bk * OUT_ROWS + core * o_half, o_half), o_half)

        @pl.when((j == 0) & (i == 0))
        def _():
            for d in range(1, N_DEV):
                small_copy(d).start()
            small_land[me] = small_ref[...]
            for bk in range(N_CHIP):
                so1[bk] = gwo_ref[out_rows(bk, 1 - c), :].astype(BF16)
                out_level1(bk).start()

        @pl.when((j == 0) & (i == 1))
        def _():
            b = 2 * x + y
            for bk in range(N_CHIP):
                out_level1(bk).wait_recv()
            for k in range(n_rel):
                px, py = chip_of(RELATIONS[k])
                bk = 2 * px + py
                so2[k] = (gwo_ref[out_rows(bk, c), :] + ro1[bk].astype(F32)).astype(BF16)
                out_level2(k).start()

        @pl.when(i == 0)
        def _():
            acc[...] = jnp.zeros_like(acc)

        for n0 in range(0, WIN, ACC_COLS):
            n1 = min(n0 + ACC_COLS, WIN)
            acc[:, n0:n1] += _dot(ht_ref[...], win_ref[:, n0:n1])

        for k in range(N_CHIP):
            @pl.when((j == k) & (i == n_i - 1))
            def _(k=k):
                s1[k] = acc[sib_rows, :].astype(BF16)
                level1(k).start()
                mine[...] = acc[my_rows, :]

            if k < n_rel:
                @pl.when((j == k + 1) & (i == 1))
                def _(k=k):
                    level1(k).wait_recv()
                    s2[k] = (mine[...] + r1[k].astype(F32)).astype(BF16)
                    level2(k).start()

        @pl.when((j == N_CHIP - 1) & (i == n_i - 1))
        def _():
            b = 2 * x + y
            level1(N_CHIP - 1).wait_recv()
            total = mine[...] + r1[N_CHIP - 1].astype(F32)
            for k in range(n_rel):
                level2(k).wait_recv()
                total = total + r2[k].astype(F32)
            total = total.T
            pair_in[0] = total
            pair_copy(0, pair_in).start()
            total_o = gwo_ref[out_rows(b, c), :] + ro1[b].astype(F32)
            for k in range(n_rel):
                out_level2(k).wait_recv()
                total_o = total_o + ro2[k].astype(F32)
            pair_o[0] = total_o
            pair_copy(1, pair_o).start()
            for core in range(2):
                @pl.when(c == core)
                def _(core=core):
                    win_out[:, core * half:(core + 1) * half] = total
            wout_out[c] = total_o
            for d in range(1, N_DEV):
                small_copy(d).wait_recv()
            small_out[...] = small_land[...]
            pair_copy(0, pair_in).wait_recv()
            for core in range(2):
                @pl.when(c == core)
                def _(core=core):
                    win_out[:, (1 - core) * half:(2 - core) * half] = pair_in[1]
            pair_copy(1, pair_o).wait_recv()
            wout_out[1 - c] = pair_o[1]
            for d in range(1, N_DEV):
                small_copy(d).wait_send()
            for k in range(N_CHIP):
                level1(k).wait_send()
                out_level1(k).wait_send()
            for k in range(n_rel):
                level2(k).wait_send()
                out_level2(k).wait_send()
            pair_copy(0, pair_in).wait_send()
            pair_copy(1, pair_o).wait_send()

    vmem = pl.BlockSpec(memory_space=pltpu.VMEM)
    dma = pltpu.SemaphoreType.DMA
    sds = jax.ShapeDtypeStruct
    grid_spec = pltpu.PrefetchScalarGridSpec(
        num_scalar_prefetch=1, grid=(N_CHIP, n_i),
        in_specs=[pl.BlockSpec((D_MODEL, tk), lambda j, i, order: (0, i)),
                  pl.BlockSpec((None, tk, WIN), lambda j, i, order: (order[j], i, 0)), vmem, vmem],
        out_specs=(vmem, vmem, vmem),
        scratch_shapes=[
            pltpu.VMEM((D_MODEL, WIN), F32), pltpu.VMEM((half, WIN), F32),
            pltpu.VMEM((N_CHIP, half, WIN), BF16), pltpu.VMEM((N_CHIP, half, WIN), BF16),
            pltpu.VMEM((n_rel, half, WIN), BF16), pltpu.VMEM((n_rel, half, WIN), BF16),
            pltpu.VMEM((N_CHIP, o_half, D_MODEL), BF16), pltpu.VMEM((N_CHIP, o_half, D_MODEL), BF16),
            pltpu.VMEM((n_rel, o_half, D_MODEL), BF16), pltpu.VMEM((n_rel, o_half, D_MODEL), BF16),
            pltpu.VMEM((2, WIN, half), F32), pltpu.VMEM((2, o_half, D_MODEL), F32),
            pltpu.VMEM((N_DEV, PACK_ROWS, D_MODEL), F32),
            dma((N_CHIP,)), dma((N_CHIP,)), dma((n_rel,)), dma((n_rel,)),
            dma((N_CHIP,)), dma((N_CHIP,)), dma((n_rel,)), dma((n_rel,)),
            dma((2,)), dma((2,)), dma((N_DEV,)), dma((N_DEV,))])
    return pl.pallas_call(
        body, name="grad_w_in_reduce", grid_spec=grid_spec,
        out_shape=(sds((WIN, D_MODEL), F32), sds((2, o_half, D_MODEL), F32), sds((N_DEV, PACK_ROWS, D_MODEL), F32)),
        compiler_params=pltpu.CompilerParams(dimension_semantics=("arbitrary", "arbitrary"),
                                             vmem_limit_bytes=VMEM_LIMIT),
    )(order, h_t, wins, gw_out, small)


ADAM_STEPS = 4


def _adamw_math(w, g, m, v):
    m = ADAM_B1 * m + (1.0 - ADAM_B1) * g
    v = ADAM_B2 * v + (1.0 - ADAM_B2) * (g * g)
    m_hat = m / (1.0 - ADAM_B1 ** ADAM_STEP)
    v_hat = v / (1.0 - ADAM_B2 ** ADAM_STEP)
    delta = -ADAM_LR * (m_hat / (jnp.sqrt(v_hat) + ADAM_EPS) + ADAM_WD * w)
    return delta, m, v


def _adamw(w, g, m, v, name):
    r, c = w.shape

    def body(w_ref, g_ref, m_ref, v_ref, d_ref, nm_ref, nv_ref):
        delta, nm, nv = _adamw_math(w_ref[...], g_ref[...], m_ref[...], v_ref[...])
        d_ref[...] = delta
        nm_ref[...] = nm
        nv_ref[...] = nv

    rows = r // ADAM_STEPS
    assert rows * ADAM_STEPS == r and rows % 8 == 0
    spec = pl.BlockSpec((rows, c), lambda i: (i, 0))
    shape = jax.ShapeDtypeStruct((r, c), F32)
    return pl.pallas_call(
        body, name=name, grid=(ADAM_STEPS,), in_specs=[spec] * 4, out_specs=(spec,) * 3,
        out_shape=(shape,) * 3, compiler_params=pltpu.CompilerParams(vmem_limit_bytes=VMEM_LIMIT),
    )(w, g, m, v)


def _adamw_window(w, window, shift, m, v, name):
    r, c = w.shape
    rows = r // ADAM_STEPS
    assert rows * ADAM_STEPS == r and rows % 8 == 0

    def body(shift_ref, w_ref, win_hbm, m_ref, v_ref, g_ref, d_ref, nm_ref, nv_ref, g_vmem):
        start = pl.multiple_of(shift_ref[0] + pl.program_id(0) * rows, 8)
        pltpu.sync_copy(win_hbm.at[pl.ds(start, rows)], g_vmem)
        g = g_vmem[...]
        g_ref[...] = g
        delta, nm, nv = _adamw_math(w_ref[...], g, m_ref[...], v_ref[...])
        d_ref[...] = delta
        nm_ref[...] = nm
        nv_ref[...] = nv

    spec = pl.BlockSpec((rows, c), lambda i, shift_ref: (i, 0))
    shape = jax.ShapeDtypeStruct((r, c), F32)
    grid_spec = pltpu.PrefetchScalarGridSpec(
        num_scalar_prefetch=1, grid=(ADAM_STEPS,),
        in_specs=[spec, pl.BlockSpec(memory_space=pl.ANY), spec, spec], out_specs=(spec,) * 4,
        scratch_shapes=[pltpu.VMEM((rows, c), F32)])
    return pl.pallas_call(
        body, name=name, grid_spec=grid_spec, out_shape=(shape,) * 4,
        compiler_params=pltpu.CompilerParams(vmem_limit_bytes=VMEM_LIMIT),
    )(shift, w, window, m, v)


PACK_ROWS = 8


def _fold_heads(v):
    y = v[:, 0:LANES]
    for j in range(1, v.shape[1] // LANES):
        y = y + v[:, j * LANES:(j + 1) * LANES]
    return y + pltpu.roll(y, HEAD_DIM, 1)


N_SMALL = 6


def _small_adamw(recv, weights, m, v):
    def body(*refs):
        r_ref = refs[0]
        w_refs, m_refs, v_refs = (refs[1 + n * N_SMALL:1 + (n + 1) * N_SMALL] for n in range(3))
        outs = refs[1 + 3 * N_SMALL:]
        g_refs, d_refs, nm_refs, nv_refs = (outs[n * N_SMALL:(n + 1) * N_SMALL] for n in range(4))
        loss_ref = outs[4 * N_SMALL]
        tot = r_ref[0]
        for j in range(1, N_DEV):
            tot = tot + r_ref[j]
        loss_ref[...] = tot[3:4, 0:LANES]
        row1 = tot[1:2, :]
        row2 = tot[2:3, :]
        grads = [tot[0:1, :],
                 _fold_heads(row1[:, 0:WIDTH])[:, :HEAD_DIM],
                 _fold_heads(row2[:, WIDTH:WIDTH + A_KV_WIDTH])[:, :HEAD_DIM],
                 row2[:, WIDTH + A_KV_WIDTH:WIDTH + A_KV_WIDTH + HEADS],
                 _fold_heads(row1[:, WIDTH:2 * WIDTH])[:, :HEAD_DIM],
                 _fold_heads(row2[:, 0:WIDTH])[:, :HEAD_DIM]]
        for n, g in enumerate(grads):
            g_refs[n][...] = g
            delta, nm, nv = _adamw_math(w_refs[n][...], g, m_refs[n][...], v_refs[n][...])
            d_refs[n][...] = delta
            nm_refs[n][...] = nm
            nv_refs[n][...] = nv

    shapes = tuple(jax.ShapeDtypeStruct(a.shape, F32) for a in weights)
    outs = pl.pallas_call(body, name="small_adamw", out_shape=shapes * 4 + (jax.ShapeDtypeStruct((1, LANES), F32),)
                          )(recv, *weights, *m, *v)
    return tuple(outs[n * N_SMALL:(n + 1) * N_SMALL] for n in range(4)) + (outs[4 * N_SMALL],)


def kernel(x, norm_gain, w_in, q_norm_a, k_norm_a, sinks_a, q_norm_b, k_norm_b, w_out, loss_target, m_norm_gain, m_w_in, m_q_norm_a, m_k_norm_a, m_sinks_a, m_q_norm_b, m_k_norm_b, m_w_out, v_norm_gain, v_w_in, v_q_norm_a, v_k_norm_a, v_sinks_a, v_q_norm_b, v_k_norm_b, v_w_out):
    chip = 2 * lax.axis_index("x") + lax.axis_index("y")

    w_in_t, m_w_in_t, v_w_in_t = w_in[0].T, m_w_in[0].T, v_w_in[0].T

    s = x.shape[1]
    (w_in_all,), tables = _gather_weights([w_in_t], "gather_weights", _rope_angles(s), _table_shapes(s), _tables)
    w_in_bf = w_in_all.reshape(IN_WIDTH, D_MODEL)

    loss_part, gx, h_t, wins, gw_out, (dgain, dgqa, dgka, dsink, dgqb, dgkb) = _local_step(
        x[0], loss_target[0], norm_gain, w_in_bf, q_norm_a, k_norm_a, sinks_a, q_norm_b, k_norm_b, w_out[0], tables)

    small = jnp.concatenate([
        dgain, jnp.concatenate([dgqa, dgqb], axis=1),
        jnp.concatenate([dgkb, dgka, dsink, jnp.zeros((1, D_MODEL - WIDTH - 2 * A_KV_WIDTH), F32)], axis=1),
        jnp.pad(loss_part, ((0, 0), (0, D_MODEL - LANES))),
        jnp.zeros((PACK_ROWS - 4, D_MODEL), F32)], axis=0)
    order = (chip ^ jnp.array(RELATIONS, jnp.int32)).astype(jnp.int32)
    win_sum, wout_sum, small_recv = _grad_reduce(order, h_t, wins, gw_out, small)
    shift = jnp.array(WIN_SHIFT, jnp.int32)[chip].reshape(1)
    g_w_out = wout_sum.reshape(OUT_ROWS, D_MODEL)

    g_w_in, d_w_in, nm_w_in, nv_w_in = (
        a.T for a in _adamw_window(w_in_t, win_sum, shift, m_w_in_t, v_w_in_t, "adamw_w_in"))
    d_w_out, nm_w_out, nv_w_out = _adamw(w_out[0], g_w_out, m_w_out[0], v_w_out[0], "adamw_w_out")
    g_s, d_s, nm_s, nv_s, loss_row = _small_adamw(
        small_recv,
        (norm_gain, q_norm_a, k_norm_a, sinks_a, q_norm_b, k_norm_b),
        (m_norm_gain, m_q_norm_a, m_k_norm_a, m_sinks_a, m_q_norm_b, m_k_norm_b),
        (v_norm_gain, v_q_norm_a, v_k_norm_a, v_sinks_a, v_q_norm_b, v_k_norm_b))
    loss = loss_row[0, 0]

    def leaves(small_ones, big_in, big_out):
        return (small_ones[0], big_in[None]) + tuple(small_ones[1:]) + (big_out[None],)

    return ((loss, gx[None]) + leaves(g_s, g_w_in, g_w_out) + leaves(d_s, d_w_in, d_w_out)
            + leaves(nm_s, nm_w_in, nm_w_out) + leaves(nv_s, nv_w_in, nv_w_out))
```

```python
import numpy as np
import jax
import jax.numpy as jnp
from jax import lax
from jax.experimental import pallas as pl
from jax.experimental.pallas import tpu as pltpu

F32 = jnp.float32
BF16 = jnp.bfloat16

D_MODEL = 1024
HEAD_DIM = 64
HEADS = 8
WIDTH = HEADS * HEAD_DIM
A_KV_WIDTH = 2 * HEAD_DIM
BLOCK = 128
LANES = 128
FOLD = 16
A_MAX_DIST = 127
B_MAX_DIST = 128
ROPE_THETA = 10000.0
ROPE_SPLIT = 64
EPS = 1e-6
NEG = -1e30
SCALE = HEAD_DIM ** -0.5

IN_WIDTH = 3328
C_QA, C_KA, C_VA, C_GA, C_QB, C_KB, C_VB, C_GB, C_END = 0, 512, 640, 768, 1280, 1792, 2304, 2816, 3328

N_DEV = 8
N_CHIP = 4
MESH = pl.DeviceIdType.MESH
WIN = 896
WIN_START = (0, 768, 1664, 2432)
WIN_SHIFT = (0, 64, 0, 64)
OUT_ROWS = D_MODEL // N_CHIP
RELATIONS = (3, 1, 2, 0)

ADAM_LR = 0.001
ADAM_B1 = 0.9
ADAM_B2 = 0.999
ADAM_EPS = 1e-08
ADAM_WD = 0.01
ADAM_STEP = 10

ROW_TILE = 256
FOLD_ROWS = ROW_TILE // FOLD
GRAD_ROWS = 1024
ACC_COLS = 256
VMEM_LIMIT = 56 * 1024 * 1024


def _dot(a, b):
    return jnp.dot(a, b, preferred_element_type=F32)


def _dot_nt(a, b):
    return lax.dot_general(a, b, (((1,), (1,)), ((), ())), preferred_element_type=F32)


def _dot_tn(a, b):
    return lax.dot_general(a, b, (((0,), (0,)), ((), ())), preferred_element_type=F32)


def _head_sum(z, bd):
    w = bd.shape[0]
    zb = z.astype(BF16)
    parts = [_dot(zb[:, a:a + w], bd) for a in range(0, z.shape[1], w)]
    return parts[0] if len(parts) == 1 else jnp.concatenate(parts, axis=1)


def _swap_halves(t):
    w = t.shape[1]
    lane = lax.broadcasted_iota(jnp.int32, t.shape, 1)
    return jnp.where(lane % HEAD_DIM < HEAD_DIM // 2, pltpu.roll(t, w - 32, 1), pltpu.roll(t, 32, 1))


def _qknorm_rope(t, g, cos, sin_s, bd):
    r = lax.rsqrt(_head_sum(t * t, bd) * (1.0 / HEAD_DIM) + EPS)
    n = (t * r) * g
    return n * cos + _swap_halves(n) * sin_s


def _qknorm_rope_bwd(dout, t, g, cos, sin_s, bd):
    dout, t = dout.astype(F32), t.astype(F32)
    dn = dout * cos + _swap_halves(dout * sin_s)
    r = lax.rsqrt(_head_sum(t * t, bd) * (1.0 / HEAD_DIM) + EPS)
    tr = t * r
    u = dn * g
    dt = r * (u - tr * (_head_sum(u * tr, bd) * (1.0 / HEAD_DIM)))
    return dt, dn * tr


def _sigmoid(g):
    return 1.0 / (1.0 + jnp.exp(-g))


def _expand_heads(st):
    t = st.shape[0]
    lane = lax.broadcasted_iota(jnp.int32, (t, LANES), 1)
    chunks = []
    for c in range(WIDTH // LANES):
        chunks.append(jnp.where(lane < HEAD_DIM, st[:, 2 * c:2 * c + 1], st[:, 2 * c + 1:2 * c + 2]))
    return jnp.concatenate(chunks, axis=1)


def _reduce_heads(z):
    t = z.shape[0]
    lane = lax.broadcasted_iota(jnp.int32, (t, LANES), 1)
    out = jnp.zeros((t, LANES), F32)
    for c in range(WIDTH // LANES):
        zc = z[:, c * LANES:(c + 1) * LANES]
        for ph in range(2):
            s = jnp.sum(jnp.where((lane // HEAD_DIM) == ph, zc, 0.0), axis=-1, keepdims=True)
            out = jnp.where(lane == 2 * c + ph, s, out)
    return out


def _fold_scratch(w):
    return pltpu.VMEM((w // LANES, ROW_TILE, LANES), F32)


def _store_folded(out_ref, val, scr, col0=0):
    w = val.shape[1]
    n = w // LANES
    for c in range(n):
        scr[c] = val[:, c * LANES:(c + 1) * LANES]
    for r in range(FOLD):
        piece = [scr[c, pl.ds(r, FOLD_ROWS, stride=FOLD), :] for c in range(n)]
        out_ref[r, :, col0:col0 + w] = (piece[0] if n == 1 else jnp.concatenate(piece, axis=1)).astype(out_ref.dtype)


def _load_folded(in_ref, scr):
    n = in_ref.shape[2] // LANES
    for r in range(FOLD):
        blk = in_ref[r].astype(F32)
        for c in range(n):
            scr[c, pl.ds(r, FOLD_ROWS, stride=FOLD), :] = blk[:, c * LANES:(c + 1) * LANES]
    return scr[0] if n == 1 else jnp.concatenate([scr[c] for c in range(n)], axis=1)


def _store_folded_bf16(out_ref, val, perm):
    folded = _dot(perm, val.astype(BF16)).astype(out_ref.dtype)
    for r in range(FOLD):
        out_ref[r] = folded[r * FOLD_ROWS:(r + 1) * FOLD_ROWS]


def _load_folded_bf16(in_ref, perm):
    blk = jnp.concatenate([in_ref[r] for r in range(FOLD)], axis=0)
    return _dot(perm, blk)


def _rows(w, tm=ROW_TILE):
    return pl.BlockSpec((tm, w), lambda i: (i, 0))


def _folded_rows(w):
    return pl.BlockSpec((FOLD, FOLD_ROWS, w), lambda i: (0, i, 0))


def _whole(shape):
    return pl.BlockSpec(shape, lambda i: (0,) * len(shape))


def _inproj(x2, gain, w_bf, cos, sin_s, gqa, gka, gqb, gkb, bd256, bd128, w_out_blk):
    s = x2.shape[0]
    tm = ROW_TILE
    n_steps = s // tm
    n_rel = N_CHIP - 1
    o_half = OUT_ROWS // 2

    def body(x_ref, gain_ref, w_hbm, cos_ref, sin_ref, gqa_ref, gka_ref, gqb_ref, gkb_ref, bd256_ref, bd128_ref,
             wout_ref, qa_ref, kva_ref, qb_ref, kvb_ref, qbf_ref, kvbf_ref,
             qa_raw_ref, ka_raw_ref, ga_ref, qb_raw_ref, kb_raw_ref, gb_ref, wout_all_ref,
             w_vmem, scr, land, ici_send, ici_recv, d2d_send, d2d_recv):
        i = pl.program_id(0)
        px_, py_, c = _position()
        b = 2 * px_ + py_

        def piece(chip_idx, core):
            return land.at[chip_idx, pl.ds(pl.multiple_of(core * o_half, o_half), o_half)]

        def other_chip(d):
            ox, oy = px_ ^ (d >> 1), py_ ^ (d & 1)
            return ox, oy, 2 * ox + oy

        def ici_copy(d, chip_idx):
            ox, oy, _ = other_chip(d)
            return pltpu.make_async_remote_copy(
                src_ref=piece(chip_idx, c), dst_ref=piece(chip_idx, c), send_sem=ici_send.at[d - 1],
                recv_sem=ici_recv.at[d - 1], device_id=(ox, oy, c), device_id_type=MESH)

        def d2d_copy(d, core):
            return pltpu.make_async_remote_copy(
                src_ref=piece(other_chip(d)[2], core), dst_ref=piece(other_chip(d)[2], core),
                send_sem=d2d_send.at[d - 1], recv_sem=d2d_recv.at[d - 1], device_id=(px_, py_, 1 - c),
                device_id_type=MESH)

        @pl.when(i == 0)
        def _():
            pltpu.sync_copy(w_hbm, w_vmem)
            land[b] = wout_ref[...].astype(BF16)
            for d in range(1, N_CHIP):
                ici_copy(d, b).start()

        @pl.when(i == n_steps // 2)
        def _():
            for d in range(1, N_CHIP):
                ici_copy(d, other_chip(d)[2]).wait_recv()
                d2d_copy(d, c).start()

        @pl.when(i == n_steps - 1)
        def _():
            for d in range(1, N_CHIP):
                d2d_copy(d, 1 - c).wait_recv()
            for d in range(1, N_CHIP):
                ici_copy(d, b).wait_send()
                d2d_copy(d, c).wait_send()
            for k in range(N_CHIP):
                wout_all_ref[k * OUT_ROWS:(k + 1) * OUT_ROWS, :] = land[k]

        xt = x_ref[...]
        r = lax.rsqrt(jnp.mean(xt * xt, axis=-1, keepdims=True) + EPS)
        h = ((xt * r) * gain_ref[...]).astype(BF16)
        cos1 = cos_ref[...]
        sin1 = sin_ref[...]
        cos4 = jnp.tile(cos1, (1, 4))
        sin4 = jnp.tile(sin1, (1, 4))

        def seg(a, b):
            return _dot_nt(h, w_vmem[a:b, :])

        t = seg(C_QA, C_KA)
        qa_raw_ref[...] = t.astype(BF16)
        qa_ref[...] = (_qknorm_rope(t, gqa_ref[...], cos4, sin4, bd256_ref[...]) * SCALE).astype(BF16)
        t = seg(C_KA, C_VA)
        ka_raw_ref[...] = t.astype(BF16)
        kva_ref[:, :A_KV_WIDTH] = _qknorm_rope(t, gka_ref[...], cos1, sin1, bd128_ref[...]).astype(BF16)
        kva_ref[:, A_KV_WIDTH:] = seg(C_VA, C_GA).astype(BF16)
        ga_ref[...] = seg(C_GA, C_QB).astype(BF16)
        t = seg(C_QB, C_KB)
        qb_raw_ref[...] = t.astype(BF16)
        t = _qknorm_rope(t, gqb_ref[...], cos4, sin4, bd256_ref[...]) * SCALE
        qb_ref[...] = t.astype(BF16)
        _store_folded(qbf_ref, t, scr)
        t = seg(C_KB, C_VB)
        kb_raw_ref[...] = t.astype(BF16)
        t = _qknorm_rope(t, gkb_ref[...], cos4, sin4, bd256_ref[...])
        kvb_ref[:, :WIDTH] = t.astype(BF16)
        _store_folded(kvbf_ref, t, scr)
        t = seg(C_VB, C_GB)
        kvb_ref[:, WIDTH:] = t.astype(BF16)
        _store_folded(kvbf_ref, t, scr, WIDTH)
        gb_ref[...] = seg(C_GB, C_END).astype(BF16)

    sds = jax.ShapeDtypeStruct
    ln = s // FOLD
    out_shape = (sds((s, WIDTH), BF16), sds((s, 2 * A_KV_WIDTH), BF16), sds((s, WIDTH), BF16),
                 sds((s, 2 * WIDTH), BF16), sds((FOLD, ln, WIDTH), BF16), sds((FOLD, ln, 2 * WIDTH), BF16),
                 sds((s, WIDTH), BF16), sds((s, A_KV_WIDTH), BF16), sds((s, WIDTH), BF16),
                 sds((s, WIDTH), BF16), sds((s, WIDTH), BF16), sds((s, WIDTH), BF16),
                 sds((D_MODEL, D_MODEL), BF16))
    out_specs = (_rows(WIDTH), _rows(2 * A_KV_WIDTH), _rows(WIDTH), _rows(2 * WIDTH),
                 _folded_rows(WIDTH), _folded_rows(2 * WIDTH),
                 _rows(WIDTH), _rows(A_KV_WIDTH), _rows(WIDTH), _rows(WIDTH), _rows(WIDTH), _rows(WIDTH),
                 _whole((D_MODEL, D_MODEL)))
    dma = pltpu.SemaphoreType.DMA
    return pl.pallas_call(
        body, name="inproj_fwd", grid=(n_steps,),
        in_specs=[_rows(D_MODEL), _whole(gain.shape), pl.BlockSpec(memory_space=pl.ANY), _rows(LANES), _rows(LANES),
                  _whole(gqa.shape), _whole(gka.shape), _whole(gqb.shape), _whole(gkb.shape), _whole(bd256.shape),
                  _whole(bd128.shape), _whole(w_out_blk.shape)],
        out_specs=out_specs, out_shape=out_shape,
        scratch_shapes=[pltpu.VMEM((IN_WIDTH, D_MODEL), BF16), _fold_scratch(WIDTH),
                        pltpu.VMEM((N_CHIP, OUT_ROWS, D_MODEL), BF16),
                        dma((n_rel,)), dma((n_rel,)), dma((n_rel,)), dma((n_rel,))],
        compiler_params=pltpu.CompilerParams(dimension_semantics=("arbitrary",), vmem_limit_bytes=VMEM_LIMIT),
    )(x2, gain, w_bf, cos, sin_s, gqa, gka, gqb, gkb, bd256, bd128, w_out_blk)


def _seq_pos(idx, dil):
    if dil == 4:
        return 4 * (idx % 32) + idx // 32
    return idx


def _upper_mask(dil, r0=0, rows=2 * BLOCK):
    qi = (lax.broadcasted_iota(jnp.int32, (rows, BLOCK), 0) + r0) % BLOCK
    kj = lax.broadcasted_iota(jnp.int32, (rows, BLOCK), 1)
    return _seq_pos(kj, dil) > _seq_pos(qi, dil)


def _eye_mask(r0=0, rows=2 * BLOCK):
    qi = (lax.broadcasted_iota(jnp.int32, (rows, BLOCK), 0) + r0) % BLOCK
    kj = lax.broadcasted_iota(jnp.int32, (rows, BLOCK), 1)
    return qi == kj


def _stack_heads(a2, c, gqa):
    lane = lax.broadcasted_iota(jnp.int32, (1, LANES), 1) // HEAD_DIM
    zero = jnp.zeros_like(a2)
    if gqa:
        keep = lane == (c // 2)
        return jnp.concatenate([jnp.where(keep, a2, zero), jnp.where(keep, _swap_heads(a2), zero)], axis=0)
    return jnp.concatenate([jnp.where(lane == 0, a2, zero), jnp.where(lane == 1, a2, zero)], axis=0)


def _unstack_heads(a, c, gqa):
    lane = lax.broadcasted_iota(jnp.int32, (1, LANES), 1) // HEAD_DIM
    if gqa:
        return jnp.where(lane == (c // 2), a[:BLOCK], _swap_heads(a[BLOCK:]))
    return jnp.where(lane == 0, a[:BLOCK], a[BLOCK:])


def _stacked_head_ids(c, gqa):
    if gqa:
        return 2 * c + c // 2, 2 * c + 1 - c // 2
    return 2 * c, 2 * c + 1


def _per_head_rows(blk, heads):
    return jnp.concatenate([blk[:, heads[0]:heads[0] + 1], blk[:, heads[1]:heads[1] + 1]], axis=0)


def _attn_view(a, dil):
    if dil == 1:
        return a[None]
    if dil == 4:
        return a.reshape(4, 4, a.shape[1], a.shape[2])
    return a


def _attn_unview(a, dil):
    if dil == 1:
        return a[0]
    if dil == 4:
        return a.reshape(FOLD, a.shape[2], a.shape[3])
    return a


ATTN_BLOCKS_PER_STEP = 8


def _attn_specs(dil):
    if dil == 4:
        def spec(n, fn):
            return lambda w: pl.BlockSpec((4, None, n * BLOCK // 4, w), lambda r, i: (0, r, fn(i), 0))
    else:
        def spec(n, fn):
            return lambda w: pl.BlockSpec((None, n * BLOCK, w), lambda r, i: (r, fn(i), 0))
    return spec


def _blk_rows(g, dil):
    n = BLOCK // 4 if dil == 4 else BLOCK
    if isinstance(g, int):
        return slice(g * n, (g + 1) * n)
    return pl.ds(pl.multiple_of(g * n, n), n)


def _blk_load(ref, sl, dil, g=0):
    if dil == 4:
        return ref[:, _blk_rows(g, dil), sl].reshape(BLOCK, sl.stop - sl.start)
    return ref[_blk_rows(g, dil), sl]


def _blk_store(ref, sl, val, dil, g=0):
    val = val.astype(ref.dtype)
    if dil == 4:
        ref[:, _blk_rows(g, dil), sl] = val.reshape(4, BLOCK // 4, sl.stop - sl.start)
    else:
        ref[_blk_rows(g, dil), sl] = val


def _swap_heads(a):
    return pltpu.roll(a.astype(F32), HEAD_DIM, 1).astype(a.dtype)


STAT_SHIFT = 8


def _attn_fwd(q, kv, sinks, *, dil, max_dist, name):
    q, kv = _attn_view(q, dil), _attn_view(kv, dil)
    kw = kv.shape[-1] // 2
    gqa = kw == A_KV_WIDTH
    n_seq = dil
    nb = (q.shape[-2] * (4 if dil == 4 else 1)) // BLOCK
    per_step = min(ATTN_BLOCKS_PER_STEP, nb)
    with_sinks = sinks is not None
    all_lanes = slice(0, LANES)
    assert max_dist in (BLOCK - 1, BLOCK) and nb % per_step == 0 and (per_step == 1 or per_step % 2 == 0)
    diag = max_dist == BLOCK

    def body(*refs):
        if with_sinks:
            q_ref, kvp_ref, kvc_ref, sink_ref, o_ref, ml_ref = refs
        else:
            q_ref, kvp_ref, kvc_ref, o_ref, ml_ref = refs

        chunks = range(WIDTH // LANES)

        def matmuls_in(g, has_prev):
            prev_ref, prev_g = (kvp_ref, 0) if (isinstance(g, int) and g == 0) else (kvc_ref, g - 1)
            scores, values = [], []
            for c in chunks:
                sl = slice(c * LANES, (c + 1) * LANES)
                ksl = slice(0, LANES) if gqa else sl
                vsl = slice(ksl.start + kw, ksl.stop + kw)
                kcur, vcur = _blk_load(kvc_ref, ksl, dil, g), _blk_load(kvc_ref, vsl, dil, g)
                qs = _stack_heads(_blk_load(q_ref, sl, dil, g), c, gqa)
                if has_prev:
                    kcur = jnp.concatenate([_blk_load(prev_ref, ksl, dil, prev_g), kcur], axis=0)
                    vcur = jnp.concatenate([_blk_load(prev_ref, vsl, dil, prev_g), vcur], axis=0)
                scores.append(_dot_nt(qs, kcur))
                values.append(vcur)
            return scores, values

        def tile_ops(has_prev, scores):
            lane = lax.broadcasted_iota(jnp.int32, (1, LANES), 1)
            with_diag = diag and has_prev
            upper, eye = _upper_mask(dil), _eye_mask()
            first_rows = lax.broadcasted_iota(jnp.int32, (2 * BLOCK, 1), 0) < BLOCK
            ml_blk = jnp.zeros((BLOCK, LANES), F32)
            probs = []
            for c in chunks:
                heads = _stacked_head_ids(c, gqa)
                s = scores[c]
                if has_prev:
                    s_p = s[:, :BLOCK]
                    sc = jnp.where(upper, s_p, s[:, BLOCK:])
                else:
                    sc = jnp.where(upper, NEG, s)
                if with_diag:
                    sd = jnp.where(eye, s_p, NEG)
                    m = jnp.max(jnp.maximum(sc, sd), axis=-1, keepdims=True)
                else:
                    m = jnp.max(sc, axis=-1, keepdims=True)
                if with_sinks:
                    sk = jnp.where(first_rows, sink_ref[0, heads[0]], sink_ref[0, heads[1]])
                    m = jnp.maximum(m, sk)
                p = jnp.exp(sc - m)
                zero = jnp.zeros_like(p)
                if with_diag:
                    pd = jnp.exp(sd - m)
                    l = jnp.sum(p + pd, axis=-1, keepdims=True)
                else:
                    pd = zero
                    l = jnp.sum(p, axis=-1, keepdims=True)
                if with_sinks:
                    l = l + jnp.exp(sk - m)
                pf = jnp.where(upper, zero, p)
                if has_prev:
                    pf = jnp.concatenate([jnp.where(upper, p, pd), pf], axis=1)
                probs.append(pf.astype(BF16))
                for n, h in enumerate(heads):
                    rows = slice(n * BLOCK, (n + 1) * BLOCK)
                    ml_blk = jnp.where(lane == h, m[rows], ml_blk)
                    ml_blk = jnp.where(lane == h + STAT_SHIFT, l[rows], ml_blk)
            return probs, ml_blk

        def matmuls_out(g, values, probs, ml_blk):
            for c in chunks:
                sl = slice(c * LANES, (c + 1) * LANES)
                _blk_store(o_ref, sl, _unstack_heads(_dot(probs[c], values[c]), c, gqa), dil, g)
            _blk_store(ml_ref, all_lanes, ml_blk, dil, g)

        def run(blocks):
            ins = [matmuls_in(g, has_prev) for g, has_prev in blocks]
            mids = [tile_ops(has_prev, scores) for (_, has_prev), (scores, _) in zip(blocks, ins)]
            for (g, _), (_, values), (probs, ml_blk) in zip(blocks, ins, mids):
                matmuls_out(g, values, probs, ml_blk)

        second = [(1, True)] if per_step > 1 else []

        @pl.when(pl.program_id(1) == 0)
        def _():
            run([(0, False)] + second)

        @pl.when(pl.program_id(1) > 0)
        def _():
            run([(0, True)] + second)

        if per_step > 2:
            def rest(pair, carry):
                run([(2 * pair, True), (2 * pair + 1, True)])
                return carry

            lax.fori_loop(1, per_step // 2, rest, 0)

    spec = _attn_specs(dil)
    cur = spec(per_step, lambda i: i)
    prev = spec(1, lambda i: jnp.maximum(i * per_step - 1, 0))
    in_specs = [cur(WIDTH), prev(2 * kw), cur(2 * kw)]
    args = [q, kv, kv]
    if with_sinks:
        in_specs.append(pl.BlockSpec(memory_space=pltpu.SMEM))
        args.append(sinks)
    stats = jax.ShapeDtypeStruct(q.shape[:-1] + (LANES,), F32)
    o, ml = pl.pallas_call(
        body, name=name, grid=(n_seq, nb // per_step), in_specs=in_specs,
        out_specs=(cur(WIDTH), cur(LANES)),
        out_shape=(jax.ShapeDtypeStruct(q.shape, BF16), stats),
        compiler_params=pltpu.CompilerParams(dimension_semantics=("arbitrary", "arbitrary"),
                                             vmem_limit_bytes=VMEM_LIMIT),
    )(*args)
    return _attn_unview(o, dil), _attn_unview(ml, dil)


def _attn_bwd(q, kv, do, ld, *, dil, max_dist, name, onto=None):
    q, kv, do, ld = (_attn_view(a, dil) for a in (q, kv, do, ld))
    onto = () if onto is None else tuple(_attn_view(a, dil) for a in onto)
    kw = kv.shape[-1] // 2
    gqa = kw == A_KV_WIDTH
    n_seq = dil
    nb = (q.shape[-2] * (4 if dil == 4 else 1)) // BLOCK
    n_kc = kw // LANES
    per_step = min(ATTN_BLOCKS_PER_STEP, nb)
    all_lanes = slice(0, LANES)
    assert max_dist in (BLOCK - 1, BLOCK) and nb % per_step == 0 and (per_step == 1 or per_step % 2 == 0)
    diag = max_dist == BLOCK

    def body(q_ref, kvp_ref, kvc_ref, do_ref, ld_ref, *rest_refs):
        dq_ref, dkv_ref, ck_ref, cv_ref = rest_refs[len(onto):]
        i = pl.program_id(1)

        def store(ref, sl, val, blk):
            if onto:
                val = val + _blk_load(rest_refs[0 if ref is dq_ref else 1], sl, dil, blk).astype(F32)
            _blk_store(ref, sl, val, dil, blk)

        chunks = range(WIDTH // LANES)

        def matmuls_in(g, has_prev):
            prev_ref, prev_g = (kvp_ref, 0) if (isinstance(g, int) and g == 0) else (kvc_ref, g - 1)
            operands, products = [], []
            for c in chunks:
                sl = slice(c * LANES, (c + 1) * LANES)
                kc = 0 if gqa else c
                ksl = slice(kc * LANES, (kc + 1) * LANES)
                vsl = slice(ksl.start + kw, ksl.stop + kw)
                k2, v2 = _blk_load(kvc_ref, ksl, dil, g), _blk_load(kvc_ref, vsl, dil, g)
                if has_prev:
                    k2 = jnp.concatenate([_blk_load(prev_ref, ksl, dil, prev_g), k2], axis=0)
                    v2 = jnp.concatenate([_blk_load(prev_ref, vsl, dil, prev_g), v2], axis=0)
                qs = _stack_heads(_blk_load(q_ref, sl, dil, g), c, gqa)
                dos = _stack_heads(_blk_load(do_ref, sl, dil, g), c, gqa)
                operands.append((qs, dos, k2))
                products.append((_dot_nt(qs, k2), _dot_nt(dos, v2)))
            return operands, products

        def tile_ops(g, has_prev, products):
            upper, eye = _upper_mask(dil), _eye_mask()
            ld_blk = _blk_load(ld_ref, all_lanes, dil, g)
            weights = []
            for c in chunks:
                heads = _stacked_head_ids(c, gqa)
                lse2 = _per_head_rows(ld_blk, heads)
                dl2 = _per_head_rows(ld_blk, tuple(h + STAT_SHIFT for h in heads))
                s, dp = products[c]
                if has_prev:
                    s_p, dp_p = s[:, :BLOCK], dp[:, :BLOCK]
                    sc = jnp.where(upper, s_p, s[:, BLOCK:])
                    dpc = jnp.where(upper, dp_p, dp[:, BLOCK:])
                else:
                    sc = jnp.where(upper, NEG, s)
                    dpc = dp
                p = jnp.exp(sc - lse2)
                ds = p * (dpc - dl2)
                zero = jnp.zeros_like(p)
                pf = jnp.where(upper, zero, p)
                dsf = jnp.where(upper, zero, ds)
                if has_prev:
                    if diag:
                        pd = jnp.exp(jnp.where(eye, s_p, NEG) - lse2)
                        dsd = pd * (dp_p - dl2)
                    else:
                        pd = dsd = zero
                    pf = jnp.concatenate([jnp.where(upper, p, pd), pf], axis=1)
                    dsf = jnp.concatenate([jnp.where(upper, ds, dsd), dsf], axis=1)
                weights.append((pf.astype(BF16), dsf.astype(BF16)))
            return weights

        def matmuls_out(g, has_prev, operands, weights):
            seq_blk = i * per_step + g
            dk_acc = [None] * n_kc
            dv_acc = [None] * n_kc
            for c in chunks:
                sl = slice(c * LANES, (c + 1) * LANES)
                kc = 0 if gqa else c
                qs, dos, k2 = operands[c]
                pf, dsf = weights[c]
                store(dq_ref, sl, _unstack_heads(_dot(dsf, k2), c, gqa) * SCALE, g)
                dk2 = _dot_tn(dsf, qs)
                dv2 = _dot_tn(pf, dos)
                dk_acc[kc] = dk2 if dk_acc[kc] is None else dk_acc[kc] + dk2
                dv_acc[kc] = dv2 if dv_acc[kc] is None else dv_acc[kc] + dv2
            for kc in range(n_kc):
                sl = slice(kc * LANES, (kc + 1) * LANES)
                vsl = slice(sl.start + kw, sl.stop + kw)
                if has_prev:
                    store(dkv_ref, sl, ck_ref[:, sl] + dk_acc[kc][:BLOCK], seq_blk - 1)
                    store(dkv_ref, vsl, cv_ref[:, sl] + dv_acc[kc][:BLOCK], seq_blk - 1)
                    ck_ref[:, sl] = dk_acc[kc][BLOCK:]
                    cv_ref[:, sl] = dv_acc[kc][BLOCK:]
                else:
                    ck_ref[:, sl] = dk_acc[kc]
                    cv_ref[:, sl] = dv_acc[kc]

        def run(blocks):
            ins = [matmuls_in(g, has_prev) for g, has_prev in blocks]
            mids = [tile_ops(g, has_prev, products) for (g, has_prev), (_, products) in zip(blocks, ins)]
            for (g, has_prev), (operands, _), weights in zip(blocks, ins, mids):
                matmuls_out(g, has_prev, operands, weights)

        second = [(1, True)] if per_step > 1 else []

        @pl.when(i == 0)
        def _():
            run([(0, False)] + second)

        @pl.when(i > 0)
        def _():
            run([(0, True)] + second)

        if per_step > 2:
            def rest(pair, carry):
                run([(2 * pair, True), (2 * pair + 1, True)])
                return carry

            lax.fori_loop(1, per_step // 2, rest, 0)

        @pl.when(i == nb // per_step - 1)
        def _():
            for kc in range(n_kc):
                sl = slice(kc * LANES, (kc + 1) * LANES)
                store(dkv_ref, sl, ck_ref[:, sl], nb - 1)
                store(dkv_ref, slice(sl.start + kw, sl.stop + kw), cv_ref[:, sl], nb - 1)

    spec = _attn_specs(dil)
    cur = spec(per_step, lambda i: i)
    prev = spec(1, lambda i: jnp.maximum(i * per_step - 1, 0))
    if dil == 4:
        whole = pl.BlockSpec((4, None, kv.shape[2], 2 * kw), lambda r, i: (0, r, 0, 0))
    else:
        whole = pl.BlockSpec((None, kv.shape[1], 2 * kw), lambda r, i: (r, 0, 0))
    sds = jax.ShapeDtypeStruct
    dq, dkv = pl.pallas_call(
        body, name=name, grid=(n_seq, nb // per_step),
        in_specs=[cur(WIDTH), prev(2 * kw), cur(2 * kw), cur(WIDTH), cur(LANES)] + [cur(WIDTH), whole][:len(onto)],
        out_specs=(cur(WIDTH), whole),
        out_shape=(sds(q.shape, BF16), sds(kv.shape, BF16)),
        scratch_shapes=[pltpu.VMEM((BLOCK, kw), F32), pltpu.VMEM((BLOCK, kw), F32)],
        compiler_params=pltpu.CompilerParams(dimension_semantics=("arbitrary", "arbitrary"),
                                             vmem_limit_bytes=VMEM_LIMIT),
    )(q, kv, kv, do, ld, *onto)
    return _attn_unview(dq, dil), _attn_unview(dkv, dil)


def _outproj(att_a, att_b1, att_b4, att_b16, g_a, g_b, x2, tgt2, w_out_bf, sink_row, perm):
    s = x2.shape[0]
    tm = ROW_TILE

    def body(oa_ref, mla_ref, ob1_ref, ml1_ref, ob4_ref, ml4_ref, ob16_ref, ml16_ref,
             ga_ref, gb_ref, x_ref, t_ref, w_ref, sink_ref, perm_ref,
             dy_ref, doa_ref, dob_ref, dobf_ref, dga_ref, dgb_ref, lda_ref, ldb_ref, ldbf_ref,
             gw_ref, loss_ref, dsink_ref, scr_st):
        i = pl.program_id(0)
        perm = perm_ref[...]
        lane = lax.broadcasted_iota(jnp.int32, (tm, LANES), 1)
        used = lane < HEADS

        def split(ml):
            return jnp.where(used, ml, 0.0), jnp.where(used, pltpu.roll(ml, LANES - STAT_SHIFT, 1), 1.0)

        @pl.when(i == 0)
        def _():
            gw_ref[...] = jnp.zeros_like(gw_ref)
            loss_ref[...] = jnp.zeros_like(loss_ref)
            dsink_ref[...] = jnp.zeros_like(dsink_ref)

        ms, ls = zip(split(ml1_ref[...]), split(_load_folded(ml4_ref, scr_st)), split(_load_folded(ml16_ref, scr_st)))
        mx = jnp.maximum(jnp.maximum(ms[0], ms[1]), ms[2])
        scale = [jnp.exp(mp - mx) for mp in ms]
        den = (ls[0] * scale[0] + ls[1] * scale[1]) + ls[2] * scale[2]
        lse_b = jnp.where(used, mx + jnp.log(den), 0.0)
        inv_den = 1.0 / den
        o_b = _expand_heads(scale[0] * inv_den) * ob1_ref[...].astype(F32)
        o_b = o_b + _expand_heads(scale[1] * inv_den) * _load_folded_bf16(ob4_ref, perm)
        o_b = o_b + _expand_heads(scale[2] * inv_den) * _load_folded_bf16(ob16_ref, perm)
        m_a, l_a = split(mla_ref[...])
        lse_a = jnp.where(used, m_a + jnp.log(l_a), 0.0)
        o_a = _expand_heads(1.0 / l_a) * oa_ref[...].astype(F32)
        g_a = ga_ref[...].astype(F32)
        g_b = gb_ref[...].astype(F32)
        sg_a = _sigmoid(g_a)
        sg_b = _sigmoid(g_b)
        silu_a = g_a * sg_a
        silu_b = g_b * sg_b
        mixed = jnp.concatenate([o_a * silu_a, o_b * silu_b], axis=1).astype(BF16)
        w = w_ref[...]
        y = x_ref[...] + _dot(mixed, w)
        diff = y - t_ref[...]
        loss_ref[...] += (0.5 / D_MODEL) * jnp.sum(diff * diff)
        dy = diff * (1.0 / D_MODEL)
        dy_ref[...] = dy
        dyb = dy.astype(BF16)
        gw_ref[...] += _dot_tn(mixed, dyb)
        dmixed = _dot_nt(dyb, w)
        dm_a = dmixed[:, :WIDTH]
        dm_b = dmixed[:, WIDTH:]
        do_a = dm_a * silu_a
        do_b = dm_b * silu_b
        doa_ref[...] = do_a.astype(BF16)
        dob_ref[...] = do_b.astype(BF16)
        _store_folded_bf16(dobf_ref, do_b, perm)
        dga_ref[...] = (dm_a * o_a * (sg_a * (1.0 + g_a * (1.0 - sg_a)))).astype(BF16)
        dgb_ref[...] = (dm_b * o_b * (sg_b * (1.0 + g_b * (1.0 - sg_b)))).astype(BF16)
        dl_a = _reduce_heads(do_a * o_a)
        dl_b = _reduce_heads(do_b * o_b)
        lda_ref[...] = lse_a + pltpu.roll(dl_a, STAT_SHIFT, 1)
        ld_b = lse_b + pltpu.roll(dl_b, STAT_SHIFT, 1)
        ldb_ref[...] = ld_b
        _store_folded(ldbf_ref, ld_b, scr_st)
        dsink_ref[...] -= jnp.sum(jnp.exp(sink_ref[...] - lse_a) * dl_a, axis=0, keepdims=True)

    sds = jax.ShapeDtypeStruct
    ln = s // FOLD
    natural = [_rows(WIDTH), _rows(LANES)]
    folded = [_folded_rows(WIDTH), _folded_rows(LANES)]
    return pl.pallas_call(
        body, name="outproj_fwd_bwd", grid=(s // tm,),
        in_specs=natural + natural + folded + folded
                 + [_rows(WIDTH), _rows(WIDTH), _rows(D_MODEL), _rows(D_MODEL), _whole((D_MODEL, D_MODEL)),
                    _whole((1, LANES)), _whole(perm.shape)],
        out_specs=(_rows(D_MODEL), _rows(WIDTH), _rows(WIDTH), _folded_rows(WIDTH), _rows(WIDTH), _rows(WIDTH),
                   _rows(LANES), _rows(LANES), _folded_rows(LANES),
                   _whole((D_MODEL, D_MODEL)), _whole((1, LANES)), _whole((1, LANES))),
        out_shape=(sds((s, D_MODEL), F32), sds((s, WIDTH), BF16), sds((s, WIDTH), BF16),
                   sds((FOLD, ln, WIDTH), BF16), sds((s, WIDTH), BF16), sds((s, WIDTH), BF16),
                   sds((s, LANES), F32), sds((s, LANES), F32), sds((FOLD, ln, LANES), F32),
                   sds((D_MODEL, D_MODEL), F32), sds((1, LANES), F32), sds((1, LANES), F32)),
        scratch_shapes=[_fold_scratch(LANES)],
        compiler_params=pltpu.CompilerParams(dimension_semantics=("arbitrary",), vmem_limit_bytes=VMEM_LIMIT),
    )(*att_a, *att_b1, *att_b4, *att_b16, g_a, g_b, x2, tgt2, w_out_bf, sink_row, perm)


def _inproj_bwd(x2, dy, gain, w_bf, cos, sin_s, gqa, gka, gqb, gkb, bd256, bd128, perm,
                qa_raw, ka_raw, qb_raw, kb_raw, d_a, d_b1, d_bf, dg_a, dg_b):
    s = x2.shape[0]
    tm = ROW_TILE

    def body(x_ref, dy_ref, gain_ref, w_hbm, cos_ref, sin_ref, gqa_ref, gka_ref, gqb_ref, gkb_ref, bd256_ref,
             bd128_ref, perm_ref, qa_raw_ref, ka_raw_ref, qb_raw_ref, kb_raw_ref, dqa_ref, dkva_ref,
             dq1_ref, dkv1_ref, dqf_ref, dkvf_ref, dga_ref, dgb_ref,
             gx_ref, ht_ref, win_ref,
             dgain_ref, dgqa_ref, dgka_ref, dgqb_ref, dgkb_ref, w_vmem, dproj_ref):
        i = pl.program_id(0)
        perm = perm_ref[...]

        @pl.when(i == 0)
        def _():
            pltpu.sync_copy(w_hbm, w_vmem)
            dgain_ref[...] = jnp.zeros_like(dgain_ref)
            dgqa_ref[...] = jnp.zeros_like(dgqa_ref)
            dgka_ref[...] = jnp.zeros_like(dgka_ref)
            dgqb_ref[...] = jnp.zeros_like(dgqb_ref)
            dgkb_ref[...] = jnp.zeros_like(dgkb_ref)

        cos1 = cos_ref[...]
        sin1 = sin_ref[...]
        cos4 = jnp.tile(cos1, (1, 4))
        sin4 = jnp.tile(sin1, (1, 4))

        dt, dg = _qknorm_rope_bwd(dqa_ref[...], qa_raw_ref[...], gqa_ref[...], cos4, sin4, bd256_ref[...])
        dproj_ref[:, C_QA:C_KA] = dt.astype(BF16)
        dgqa_ref[...] += jnp.sum(dg, axis=0, keepdims=True)
        dt, dg = _qknorm_rope_bwd(dkva_ref[:, :A_KV_WIDTH], ka_raw_ref[...], gka_ref[...], cos1, sin1,
                                  bd128_ref[...])
        dproj_ref[:, C_KA:C_VA] = dt.astype(BF16)
        dgka_ref[...] += jnp.sum(dg, axis=0, keepdims=True)
        dproj_ref[:, C_VA:C_GA] = dkva_ref[:, A_KV_WIDTH:]
        dproj_ref[:, C_GA:C_QB] = dga_ref[...]
        dq = dq1_ref[...].astype(F32) + _load_folded_bf16(dqf_ref, perm)
        dt, dg = _qknorm_rope_bwd(dq, qb_raw_ref[...], gqb_ref[...], cos4, sin4, bd256_ref[...])
        dproj_ref[:, C_QB:C_KB] = dt.astype(BF16)
        dgqb_ref[...] += jnp.sum(dg, axis=0, keepdims=True)
        dkv = dkv1_ref[...].astype(F32) + _load_folded_bf16(dkvf_ref, perm)
        dt, dg = _qknorm_rope_bwd(dkv[:, :WIDTH], kb_raw_ref[...], gkb_ref[...], cos4, sin4, bd256_ref[...])
        dproj_ref[:, C_KB:C_VB] = dt.astype(BF16)
        dgkb_ref[...] += jnp.sum(dg, axis=0, keepdims=True)
        dproj_ref[:, C_VB:C_GB] = dkv[:, WIDTH:].astype(BF16)
        dproj_ref[:, C_GB:C_END] = dgb_ref[...]
        for k, start in enumerate(WIN_START):
            win_ref[k] = dproj_ref[:, start:start + WIN]

        xt = x_ref[...]
        gain_row = gain_ref[...]
        r = lax.rsqrt(jnp.mean(xt * xt, axis=-1, keepdims=True) + EPS)
        xr = xt * r
        ht_ref[...] = (xr * gain_row).T.astype(BF16)
        dh = _dot(dproj_ref[...], w_vmem[...])
        dgain_ref[...] += jnp.sum(dh * xr, axis=0, keepdims=True)
        u = dh * gain_row
        gx_ref[...] = dy_ref[...] + r * (u - xr * jnp.mean(u * xr, axis=-1, keepdims=True))

    def acc_row(w):
        return pl.BlockSpec((1, w), lambda i: (0, 0))

    sds = jax.ShapeDtypeStruct
    any_spec = pl.BlockSpec(memory_space=pl.ANY)
    win_spec = pl.BlockSpec((N_CHIP, tm, WIN), lambda i: (0, i, 0))
    return pl.pallas_call(
        body, name="inproj_bwd", grid=(s // tm,),
        in_specs=[_rows(D_MODEL), _rows(D_MODEL), _whole(gain.shape), any_spec, _rows(LANES), _rows(LANES),
                  _whole(gqa.shape), _whole(gka.shape), _whole(gqb.shape), _whole(gkb.shape), _whole(bd256.shape),
                  _whole(bd128.shape), _whole(perm.shape),
                  _rows(WIDTH), _rows(A_KV_WIDTH), _rows(WIDTH), _rows(WIDTH),
                  _rows(WIDTH), _rows(2 * A_KV_WIDTH), _rows(WIDTH), _rows(2 * WIDTH)]
                 + [_folded_rows(WIDTH), _folded_rows(2 * WIDTH), _rows(WIDTH), _rows(WIDTH)],
        out_specs=(_rows(D_MODEL), pl.BlockSpec((D_MODEL, tm), lambda i: (0, i)), win_spec, acc_row(D_MODEL), acc_row(WIDTH), acc_row(A_KV_WIDTH), acc_row(WIDTH), acc_row(WIDTH)),
        out_shape=(sds((s, D_MODEL), F32), sds((D_MODEL, s), BF16), sds((N_CHIP, s, WIN), BF16),
                   sds((1, D_MODEL), F32),
                   sds((1, WIDTH), F32), sds((1, A_KV_WIDTH), F32), sds((1, WIDTH), F32), sds((1, WIDTH), F32)),
        scratch_shapes=[pltpu.VMEM((IN_WIDTH, D_MODEL), BF16), pltpu.VMEM((tm, IN_WIDTH), BF16)],
        compiler_params=pltpu.CompilerParams(dimension_semantics=("arbitrary",), vmem_limit_bytes=VMEM_LIMIT),
    )(x2, dy, gain, w_bf, cos, sin_s, gqa, gka, gqb, gkb, bd256, bd128, perm, qa_raw, ka_raw, qb_raw, kb_raw,
      *d_a, *d_b1, *d_bf, dg_a, dg_b)


def _rope_angles(s):
    half = HEAD_DIM // 2
    f32 = np.float32
    inv = np.tile((f32(ROPE_THETA) ** (-np.arange(half, dtype=f32) / f32(half))).astype(f32), 4)
    sign = np.tile(np.concatenate([-np.ones((half,), f32), np.ones((half,), f32)]), 2)
    hi = (np.arange(s // ROPE_SPLIT) * ROPE_SPLIT).astype(f32)[:, None] * inv[None, :]
    lo = np.arange(ROPE_SPLIT).astype(f32)[:, None] * inv[None, :]
    return tuple(jnp.asarray(t.astype(f32)) for t in (np.cos(hi), np.sin(hi), np.cos(lo), np.sin(lo), sign[None, :]))


def _table_shapes(s):
    return (jax.ShapeDtypeStruct((s, LANES), F32), jax.ShapeDtypeStruct((s, LANES), F32),
            jax.ShapeDtypeStruct((2 * LANES, 2 * LANES), BF16), jax.ShapeDtypeStruct((A_KV_WIDTH, A_KV_WIDTH), BF16),
            jax.ShapeDtypeStruct((ROW_TILE, ROW_TILE), BF16))


def _tables(in_refs, out_refs):
    ch_ref, sh_ref, cl_ref, sl_ref, sign_ref = in_refs
    cos_ref, sin_ref, bd256_ref, bd128_ref, perm_ref = out_refs
    cl, sl, sign = cl_ref[...], sl_ref[...], sign_ref[...]

    def tile(a, carry):
        rows = pl.ds(pl.multiple_of(a * ROPE_SPLIT, ROPE_SPLIT), ROPE_SPLIT)
        ch, sh = ch_ref[pl.ds(a, 1), :], sh_ref[pl.ds(a, 1), :]
        cos_ref[rows, :] = ch * cl - sh * sl
        sin_ref[rows, :] = (sh * cl + ch * sl) * sign
        return carry

    lax.fori_loop(0, cos_ref.shape[0] // ROPE_SPLIT, tile, 0)
    for ref in (bd256_ref, bd128_ref):
        head = [lax.broadcasted_iota(jnp.int32, ref.shape, d) // HEAD_DIM for d in (0, 1)]
        ref[...] = jnp.where(head[0] == head[1], 1.0, 0.0).astype(BF16)
    f = lax.broadcasted_iota(jnp.int32, perm_ref.shape, 0)
    col = lax.broadcasted_iota(jnp.int32, perm_ref.shape, 1)
    perm_ref[...] = jnp.where(col == FOLD * (f % FOLD_ROWS) + f // FOLD_ROWS, 1.0, 0.0).astype(BF16)


def _local_step(x2, tgt2, norm_gain, w_in_bf, q_norm_a, k_norm_a, sinks_a, q_norm_b, k_norm_b, w_out_blk, tables):
    cos, sin_s, bd256, bd128, perm = tables
    gqa = jnp.tile(q_norm_a, (1, HEADS))
    gka = jnp.tile(k_norm_a, (1, 2))
    gqb = jnp.tile(q_norm_b, (1, HEADS))
    gkb = jnp.tile(k_norm_b, (1, HEADS))
    sink_row = jnp.pad(sinks_a, ((0, 0), (0, LANES - HEADS)))

    (qa, kva, qb, kvb, qbf, kvbf, qa_raw, ka_raw, g_a, qb_raw, kb_raw, g_b, w_out_bf) = _inproj(
        x2, norm_gain, w_in_bf, cos, sin_s, gqa, gka, gqb, gkb, bd256, bd128, w_out_blk)

    att_a = _attn_fwd(qa, kva, sinks_a, dil=1, max_dist=A_MAX_DIST, name="attn_a_fwd")
    att_b1 = _attn_fwd(qb, kvb, None, dil=1, max_dist=B_MAX_DIST, name="attn_b1_fwd")
    att_b4 = _attn_fwd(qbf, kvbf, None, dil=4, max_dist=B_MAX_DIST, name="attn_b4_fwd")
    att_b16 = _attn_fwd(qbf, kvbf, None, dil=16, max_dist=B_MAX_DIST, name="attn_b16_fwd")

    (dy, do_a, do_b, do_bf, dg_a, dg_b, ld_a, ld_b, ld_bf, gw_out, loss_part, dsink) = _outproj(
        att_a, att_b1, att_b4, att_b16, g_a, g_b, x2, tgt2, w_out_bf, sink_row, perm)

    d_a = _attn_bwd(qa, kva, do_a, ld_a, dil=1, max_dist=A_MAX_DIST, name="attn_a_bwd")
    d_b1 = _attn_bwd(qb, kvb, do_b, ld_b, dil=1, max_dist=B_MAX_DIST, name="attn_b1_bwd")
    d_b4 = _attn_bwd(qbf, kvbf, do_bf, ld_bf, dil=4, max_dist=B_MAX_DIST, name="attn_b4_bwd")
    d_b16 = _attn_bwd(qbf, kvbf, do_bf, ld_bf, dil=16, max_dist=B_MAX_DIST, name="attn_b16_bwd", onto=d_b4)

    gx, h_t, wins, dgain, dgqa, dgka, dgqb, dgkb = _inproj_bwd(
        x2, dy, norm_gain, w_in_bf, cos, sin_s, gqa, gka, gqb, gkb, bd256, bd128, perm,
        qa_raw, ka_raw, qb_raw, kb_raw, d_a, d_b1, d_b16, dg_a, dg_b)
    return loss_part, gx, h_t, wins, gw_out, (dgain, dgqa, dgka, dsink, dgqb, dgkb)


def _position():
    return lax.axis_index("x"), lax.axis_index("y"), lax.axis_index("c")


GATHER_CHUNKS = 2


def _gather_weights(blocks, name, side_inputs, side_shapes, side_work):
    n = len(blocks)
    n_in, n_out = len(side_inputs), len(side_shapes)
    ch = GATHER_CHUNKS

    def body(*refs):
        src_refs, side_in = refs[:n], refs[n:n + n_in]
        dst_refs, side_out = refs[n + n_in:2 * n + n_in], refs[2 * n + n_in:2 * n + n_in + n_out]
        ici_send, ici_recv, hop_send, hop_recv, d2d_send, d2d_recv = refs[2 * n + n_in + n_out:]
        x, y, c = _position()
        b = 2 * x + y
        via = 2 - c
        out = 3 - via
        for k in range(n):
            dst_refs[k][b] = src_refs[k][...].astype(BF16)

        def rows(k, core, j):
            half = blocks[k].shape[0] // 2
            return pl.ds(pl.multiple_of(core * half + j * (half // ch), half // ch), half // ch)

        def chip(rel):
            return x ^ (rel >> 1), y ^ (rel & 1)

        def ici(k, j, slot, rel, send_sems, recv_sems, sem):
            px, py = chip(rel)
            piece = dst_refs[k].at[slot, rows(k, c, j)]
            return pltpu.make_async_remote_copy(src_ref=piece, dst_ref=piece, send_sem=send_sems.at[sem],
                                                recv_sem=recv_sems.at[sem], device_id=(px, py, c),
                                                device_id_type=MESH)

        def direct(k, j, slot, rel):
            return ici(k, j, slot, rel, ici_send, ici_recv, ((rel - 1) * ch + j) * n + k)

        def hop(k, j, slot, rel):
            return ici(k, j, slot, rel, hop_send, hop_recv, j * n + k)

        def d2d(k, j, rel, core):
            piece = dst_refs[k].at[b ^ rel, rows(k, core, j)]
            sem = ((rel - 1) * ch + j) * n + k
            return pltpu.make_async_remote_copy(src_ref=piece, dst_ref=piece, send_sem=d2d_send.at[sem],
                                                recv_sem=d2d_recv.at[sem], device_id=(x, y, 1 - c),
                                                device_id_type=MESH)

        pieces = [(k, j) for j in range(ch) for k in range(n)]
        for k, j in pieces:
            for rel in (1, 2):
                direct(k, j, b, rel).start()
        side_work(side_in, side_out)
        for k, j in pieces:
            direct(k, j, b ^ via, via).wait_recv()
            hop(k, j, b ^ via, out).start()
            d2d(k, j, via, c).start()
        for k, j in pieces:
            direct(k, j, b ^ out, out).wait_recv()
            d2d(k, j, out, c).start()
        for k, j in pieces:
            hop(k, j, b ^ 3, via).wait_recv()
            d2d(k, j, 3, c).start()
        for k, j in pieces:
            for rel in (1, 2, 3):
                d2d(k, j, rel, 1 - c).wait_recv()
        for k, j in pieces:
            for rel in (1, 2):
                direct(k, j, b, rel).wait_send()
            hop(k, j, b ^ via, out).wait_send()
            d2d(k, j, via, c).wait_send()
            d2d(k, j, out, c).wait_send()
            d2d(k, j, 3, c).wait_send()

    vmem_spec = pl.BlockSpec(memory_space=pltpu.VMEM)
    dma = pltpu.SemaphoreType.DMA
    out_shape = tuple(jax.ShapeDtypeStruct((N_CHIP,) + a.shape, BF16) for a in blocks) + tuple(side_shapes)
    outs = pl.pallas_call(
        body, name=name, in_specs=[vmem_spec] * (n + n_in), out_specs=tuple([vmem_spec] * (n + n_out)),
        out_shape=out_shape,
        scratch_shapes=[dma((2 * ch * n,)), dma((2 * ch * n,)), dma((ch * n,)), dma((ch * n,)),
                        dma((3 * ch * n,)), dma((3 * ch * n,))],
        compiler_params=pltpu.CompilerParams(vmem_limit_bytes=VMEM_LIMIT),
    )(*blocks, *side_inputs)
    return outs[:n], outs[n:]


def _grad_reduce(order, h_t, wins, gw_out, small):
    s = h_t.shape[1]
    tk = GRAD_ROWS
    n_i = s // tk
    half = D_MODEL // 2
    o_half = OUT_ROWS // 2
    n_rel = N_CHIP - 1

    def body(order_ref, ht_ref, win_ref, gwo_ref, small_ref,
             win_out, wout_out, small_out,
             acc, mine, s1, r1, s2, r2, so1, ro1, so2, ro2, pair_in, pair_o, small_land,
             s1_send, s1_recv, s2_send, s2_recv, o1_send, o1_recv, o2_send, o2_recv,
             pair_send, pair_recv, small_send, small_recv):
        j = pl.program_id(0)
        i = pl.program_id(1)
        x, y, c = _position()
        me = 4 * x + 2 * y + c
        sibling = (x, y, 1 - c)
        my_rows = pl.ds(pl.multiple_of(c * half, half), half)
        sib_rows = pl.ds(pl.multiple_of((1 - c) * half, half), half)

        def chip_of(rel):
            return x ^ (rel >> 1), y ^ (rel & 1)

        def level1(k):
            return pltpu.make_async_remote_copy(src_ref=s1.at[k], dst_ref=r1.at[k], send_sem=s1_send.at[k],
                                                recv_sem=s1_recv.at[k], device_id=sibling, device_id_type=MESH)

        def level2(k):
            px, py = chip_of(RELATIONS[k])
            return pltpu.make_async_remote_copy(src_ref=s2.at[k], dst_ref=r2.at[k], send_sem=s2_send.at[k],
                                                recv_sem=s2_recv.at[k], device_id=(px, py, c), device_id_type=MESH)

        def out_level1(bk):
            return pltpu.make_async_remote_copy(src_ref=so1.at[bk], dst_ref=ro1.at[bk], send_sem=o1_send.at[bk],
                                                recv_sem=o1_recv.at[bk], device_id=sibling, device_id_type=MESH)

        def out_level2(k):
            px, py = chip_of(RELATIONS[k])
            return pltpu.make_async_remote_copy(src_ref=so2.at[k], dst_ref=ro2.at[k], send_sem=o2_send.at[k],
                                                recv_sem=o2_recv.at[k], device_id=(px, py, c), device_id_type=MESH)

        def small_copy(d):
            px, py, pc = x ^ (d >> 2), y ^ ((d >> 1) & 1), c ^ (d & 1)
            return pltpu.make_async_remote_copy(src_ref=small_ref, dst_ref=small_land.at[me],
                                                send_sem=small_send.at[d], recv_sem=small_recv.at[d],
                                                device_id=(px, py, pc), device_id_type=MESH)

        def pair_copy(k, buf):
            return pltpu.make_async_remote_copy(src_ref=buf.at[0], dst_ref=buf.at[1], send_sem=pair_send.at[k],
                                                recv_sem=pair_recv.at[k], device_id=sibling, device_id_type=MESH)

        def out_rows(bk, core):
            return pl.ds(pl.multiple_of(bk * OUT_ROWS + core * o_half, o_half), o_half)

        @pl.when((j == 0) & (i == 0))
        def _():
            for d in range(1, N_DEV):
                small_copy(d).start()
            small_land[me] = small_ref[...]
            for bk in range(N_CHIP):
                so1[bk] = gwo_ref[out_rows(bk, 1 - c), :].astype(BF16)
                out_level1(bk).start()

        @pl.when((j == 0) & (i == 1))
        def _():
            b = 2 * x + y
            for bk in range(N_CHIP):
                out_level1(bk).wait_recv()
            for k in range(n_rel):
                px, py = chip_of(RELATIONS[k])
                bk = 2 * px + py
                so2[k] = (gwo_ref[out_rows(bk, c), :] + ro1[bk].astype(F32)).astype(BF16)
                out_level2(k).start()

        @pl.when(i == 0)
        def _():
            acc[...] = jnp.zeros_like(acc)

        for n0 in range(0, WIN, ACC_COLS):
            n1 = min(n0 + ACC_COLS, WIN)
            acc[:, n0:n1] += _dot(ht_ref[...], win_ref[:, n0:n1])

        for k in range(N_CHIP):
            @pl.when((j == k) & (i == n_i - 1))
            def _(k=k):
                s1[k] = acc[sib_rows, :].astype(BF16)
                level1(k).start()
                mine[...] = acc[my_rows, :]

            if k < n_rel:
                @pl.when((j == k + 1) & (i == 1))
                def _(k=k):
                    level1(k).wait_recv()
                    s2[k] = (mine[...] + r1[k].astype(F32)).astype(BF16)
                    level2(k).start()

        @pl.when((j == N_CHIP - 1) & (i == n_i - 1))
        def _():
            b = 2 * x + y
            level1(N_CHIP - 1).wait_recv()
            total = mine[...] + r1[N_CHIP - 1].astype(F32)
            for k in range(n_rel):
                level2(k).wait_recv()
                total = total + r2[k].astype(F32)
            total = total.T
            pair_in[0] = total
            pair_copy(0, pair_in).start()
            total_o = gwo_ref[out_rows(b, c), :] + ro1[b].astype(F32)
            for k in range(n_rel):
                out_level2(k).wait_recv()
                total_o = total_o + ro2[k].astype(F32)
            pair_o[0] = total_o
            pair_copy(1, pair_o).start()
            for core in range(2):
                @pl.when(c == core)
                def _(core=core):
                    win_out[:, core * half:(core + 1) * half] = total
            wout_out[c] = total_o
            for d in range(1, N_DEV):
                small_copy(d).wait_recv()
            small_out[...] = small_land[...]
            pair_copy(0, pair_in).wait_recv()
            for core in range(2):
                @pl.when(c == core)
                def _(core=core):
                    win_out[:, (1 - core) * half:(2 - core) * half] = pair_in[1]
            pair_copy(1, pair_o).wait_recv()
            wout_out[1 - c] = pair_o[1]
            for d in range(1, N_DEV):
                small_copy(d).wait_send()
            for k in range(N_CHIP):
                level1(k).wait_send()
                out_level1(k).wait_send()
            for k in range(n_rel):
                level2(k).wait_send()
                out_level2(k).wait_send()
            pair_copy(0, pair_in).wait_send()
            pair_copy(1, pair_o).wait_send()

    vmem = pl.BlockSpec(memory_space=pltpu.VMEM)
    dma = pltpu.SemaphoreType.DMA
    sds = jax.ShapeDtypeStruct
    grid_spec = pltpu.PrefetchScalarGridSpec(
        num_scalar_prefetch=1, grid=(N_CHIP, n_i),
        in_specs=[pl.BlockSpec((D_MODEL, tk), lambda j, i, order: (0, i)),
                  pl.BlockSpec((None, tk, WIN), lambda j, i, order: (order[j], i, 0)), vmem, vmem],
        out_specs=(vmem, vmem, vmem),
        scratch_shapes=[
            pltpu.VMEM((D_MODEL, WIN), F32), pltpu.VMEM((half, WIN), F32),
            pltpu.VMEM((N_CHIP, half, WIN), BF16), pltpu.VMEM((N_CHIP, half, WIN), BF16),
            pltpu.VMEM((n_rel, half, WIN), BF16), pltpu.VMEM((n_rel, half, WIN), BF16),
            pltpu.VMEM((N_CHIP, o_half, D_MODEL), BF16), pltpu.VMEM((N_CHIP, o_half, D_MODEL), BF16),
            pltpu.VMEM((n_rel, o_half, D_MODEL), BF16), pltpu.VMEM((n_rel, o_half, D_MODEL), BF16),
            pltpu.VMEM((2, WIN, half), F32), pltpu.VMEM((2, o_half, D_MODEL), F32),
            pltpu.VMEM((N_DEV, PACK_ROWS, D_MODEL), F32),
            dma((N_CHIP,)), dma((N_CHIP,)), dma((n_rel,)), dma((n_rel,)),
            dma((N_CHIP,)), dma((N_CHIP,)), dma((n_rel,)), dma((n_rel,)),
            dma((2,)), dma((2,)), dma((N_DEV,)), dma((N_DEV,))])
    return pl.pallas_call(
        body, name="grad_w_in_reduce", grid_spec=grid_spec,
        out_shape=(sds((WIN, D_MODEL), F32), sds((2, o_half, D_MODEL), F32), sds((N_DEV, PACK_ROWS, D_MODEL), F32)),
        compiler_params=pltpu.CompilerParams(dimension_semantics=("arbitrary", "arbitrary"),
                                             vmem_limit_bytes=VMEM_LIMIT),
    )(order, h_t, wins, gw_out, small)


ADAM_STEPS = 4


def _adamw_math(w, g, m, v):
    m = ADAM_B1 * m + (1.0 - ADAM_B1) * g
    v = ADAM_B2 * v + (1.0 - ADAM_B2) * (g * g)
    m_hat = m / (1.0 - ADAM_B1 ** ADAM_STEP)
    v_hat = v / (1.0 - ADAM_B2 ** ADAM_STEP)
    delta = -ADAM_LR * (m_hat / (jnp.sqrt(v_hat) + ADAM_EPS) + ADAM_WD * w)
    return delta, m, v


def _adamw(w, g, m, v, name):
    r, c = w.shape

    def body(w_ref, g_ref, m_ref, v_ref, d_ref, nm_ref, nv_ref):
        delta, nm, nv = _adamw_math(w_ref[...], g_ref[...], m_ref[...], v_ref[...])
        d_ref[...] = delta
        nm_ref[...] = nm
        nv_ref[...] = nv

    rows = r // ADAM_STEPS
    assert rows * ADAM_STEPS == r and rows % 8 == 0
    spec = pl.BlockSpec((rows, c), lambda i: (i, 0))
    shape = jax.ShapeDtypeStruct((r, c), F32)
    return pl.pallas_call(
        body, name=name, grid=(ADAM_STEPS,), in_specs=[spec] * 4, out_specs=(spec,) * 3,
        out_shape=(shape,) * 3, compiler_params=pltpu.CompilerParams(vmem_limit_bytes=VMEM_LIMIT),
    )(w, g, m, v)


def _adamw_window(w, window, shift, m, v, name):
    r, c = w.shape
    rows = r // ADAM_STEPS
    assert rows * ADAM_STEPS == r and rows % 8 == 0

    def body(shift_ref, w_ref, win_hbm, m_ref, v_ref, g_ref, d_ref, nm_ref, nv_ref, g_vmem):
        start = pl.multiple_of(shift_ref[0] + pl.program_id(0) * rows, 8)
        pltpu.sync_copy(win_hbm.at[pl.ds(start, rows)], g_vmem)
        g = g_vmem[...]
        g_ref[...] = g
        delta, nm, nv = _adamw_math(w_ref[...], g, m_ref[...], v_ref[...])
        d_ref[...] = delta
        nm_ref[...] = nm
        nv_ref[...] = nv

    spec = pl.BlockSpec((rows, c), lambda i, shift_ref: (i, 0))
    shape = jax.ShapeDtypeStruct((r, c), F32)
    grid_spec = pltpu.PrefetchScalarGridSpec(
        num_scalar_prefetch=1, grid=(ADAM_STEPS,),
        in_specs=[spec, pl.BlockSpec(memory_space=pl.ANY), spec, spec], out_specs=(spec,) * 4,
        scratch_shapes=[pltpu.VMEM((rows, c), F32)])
    return pl.pallas_call(
        body, name=name, grid_spec=grid_spec, out_shape=(shape,) * 4,
        compiler_params=pltpu.CompilerParams(vmem_limit_bytes=VMEM_LIMIT),
    )(shift, w, window, m, v)


PACK_ROWS = 8


def _fold_heads(v):
    y = v[:, 0:LANES]
    for j in range(1, v.shape[1] // LANES):
        y = y + v[:, j * LANES:(j + 1) * LANES]
    return y + pltpu.roll(y, HEAD_DIM, 1)


N_SMALL = 6


def _small_adamw(recv, weights, m, v):
    def body(*refs):
        r_ref = refs[0]
        w_refs, m_refs, v_refs = (refs[1 + n * N_SMALL:1 + (n + 1) * N_SMALL] for n in range(3))
        outs = refs[1 + 3 * N_SMALL:]
        g_refs, d_refs, nm_refs, nv_refs = (outs[n * N_SMALL:(n + 1) * N_SMALL] for n in range(4))
        loss_ref = outs[4 * N_SMALL]
        tot = r_ref[0]
        for j in range(1, N_DEV):
            tot = tot + r_ref[j]
        loss_ref[...] = tot[3:4, 0:LANES]
        row1 = tot[1:2, :]
        row2 = tot[2:3, :]
        grads = [tot[0:1, :],
                 _fold_heads(row1[:, 0:WIDTH])[:, :HEAD_DIM],
                 _fold_heads(row2[:, WIDTH:WIDTH + A_KV_WIDTH])[:, :HEAD_DIM],
                 row2[:, WIDTH + A_KV_WIDTH:WIDTH + A_KV_WIDTH + HEADS],
                 _fold_heads(row1[:, WIDTH:2 * WIDTH])[:, :HEAD_DIM],
                 _fold_heads(row2[:, 0:WIDTH])[:, :HEAD_DIM]]
        for n, g in enumerate(grads):
            g_refs[n][...] = g
            delta, nm, nv = _adamw_math(w_refs[n][...], g, m_refs[n][...], v_refs[n][...])
            d_refs[n][...] = delta
            nm_refs[n][...] = nm
            nv_refs[n][...] = nv

    shapes = tuple(jax.ShapeDtypeStruct(a.shape, F32) for a in weights)
    outs = pl.pallas_call(body, name="small_adamw", out_shape=shapes * 4 + (jax.ShapeDtypeStruct((1, LANES), F32),)
                          )(recv, *weights, *m, *v)
    return tuple(outs[n * N_SMALL:(n + 1) * N_SMALL] for n in range(4)) + (outs[4 * N_SMALL],)


def kernel(x, norm_gain, w_in, q_norm_a, k_norm_a, sinks_a, q_norm_b, k_norm_b, w_out, loss_target, m_norm_gain, m_w_in, m_q_norm_a, m_k_norm_a, m_sinks_a, m_q_norm_b, m_k_norm_b, m_w_out, v_norm_gain, v_w_in, v_q_norm_a, v_k_norm_a, v_sinks_a, v_q_norm_b, v_k_norm_b, v_w_out):
    chip = 2 * lax.axis_index("x") + lax.axis_index("y")

    w_in_t, m_w_in_t, v_w_in_t = w_in[0].T, m_w_in[0].T, v_w_in[0].T

    s = x.shape[1]
    (w_in_all,), tables = _gather_weights([w_in_t], "gather_weights", _rope_angles(s), _table_shapes(s), _tables)
    w_in_bf = w_in_all.reshape(IN_WIDTH, D_MODEL)

    loss_part, gx, h_t, wins, gw_out, (dgain, dgqa, dgka, dsink, dgqb, dgkb) = _local_step(
        x[0], loss_target[0], norm_gain, w_in_bf, q_norm_a, k_norm_a, sinks_a, q_norm_b, k_norm_b, w_out[0], tables)

    small = jnp.concatenate([
        dgain, jnp.concatenate([dgqa, dgqb], axis=1),
        jnp.concatenate([dgkb, dgka, dsink, jnp.zeros((1, D_MODEL - WIDTH - 2 * A_KV_WIDTH), F32)], axis=1),
        jnp.pad(loss_part, ((0, 0), (0, D_MODEL - LANES))),
        jnp.zeros((PACK_ROWS - 4, D_MODEL), F32)], axis=0)
    order = (chip ^ jnp.array(RELATIONS, jnp.int32)).astype(jnp.int32)
    win_sum, wout_sum, small_recv = _grad_reduce(order, h_t, wins, gw_out, small)
    shift = jnp.array(WIN_SHIFT, jnp.int32)[chip].reshape(1)
    g_w_out = wout_sum.reshape(OUT_ROWS, D_MODEL)

    g_w_in, d_w_in, nm_w_in, nv_w_in = (
        a.T for a in _adamw_window(w_in_t, win_sum, shift, m_w_in_t, v_w_in_t, "adamw_w_in"))
    d_w_out, nm_w_out, nv_w_out = _adamw(w_out[0], g_w_out, m_w_out[0], v_w_out[0], "adamw_w_out")
    g_s, d_s, nm_s, nv_s, loss_row = _small_adamw(
        small_recv,
        (norm_gain, q_norm_a, k_norm_a, sinks_a, q_norm_b, k_norm_b),
        (m_norm_gain, m_q_norm_a, m_k_norm_a, m_sinks_a, m_q_norm_b, m_k_norm_b),
        (v_norm_gain, v_q_norm_a, v_k_norm_a, v_sinks_a, v_q_norm_b, v_k_norm_b))
    loss = loss_row[0, 0]

    def leaves(small_ones, big_in, big_out):
        return (small_ones[0], big_in[None]) + tuple(small_ones[1:]) + (big_out[None],)

    return ((loss, gx[None]) + leaves(g_s, g_w_in, g_w_out) + leaves(d_s, d_w_in, d_w_out)
            + leaves(nm_s, nm_w_in, nm_w_out) + leaves(nv_s, nv_w_in, nv_w_out))
```

```python
import numpy as np
import jax
import jax.numpy as jnp
from jax import lax
from jax.experimental import pallas as pl
from jax.experimental.pallas import tpu as pltpu

F32 = jnp.float32
BF16 = jnp.bfloat16

D_MODEL = 1024
HEAD_DIM = 64
HEADS = 8
WIDTH = HEADS * HEAD_DIM
A_KV_WIDTH = 2 * HEAD_DIM
BLOCK = 128
LANES = 128
FOLD = 16
A_MAX_DIST = 127
B_MAX_DIST = 128
ROPE_THETA = 10000.0
ROPE_SPLIT = 64
EPS = 1e-6
NEG = -1e30
SCALE = HEAD_DIM ** -0.5

IN_WIDTH = 3328
C_QA, C_KA, C_VA, C_GA, C_QB, C_KB, C_VB, C_GB, C_END = 0, 512, 640, 768, 1280, 1792, 2304, 2816, 3328

N_DEV = 8
N_CHIP = 4
MESH = pl.DeviceIdType.MESH
WIN = 896
WIN_START = (0, 768, 1664, 2432)
WIN_SHIFT = (0, 64, 0, 64)
OUT_ROWS = D_MODEL // N_CHIP
RELATIONS = (3, 1, 2, 0)

ADAM_LR = 0.001
ADAM_B1 = 0.9
ADAM_B2 = 0.999
ADAM_EPS = 1e-08
ADAM_WD = 0.01
ADAM_STEP = 10

ROW_TILE = 256
FOLD_ROWS = ROW_TILE // FOLD
GRAD_ROWS = 1024
ACC_COLS = 256
VMEM_LIMIT = 56 * 1024 * 1024


def _dot(a, b):
    return jnp.dot(a, b, preferred_element_type=F32)


def _dot_nt(a, b):
    return lax.dot_general(a, b, (((1,), (1,)), ((), ())), preferred_element_type=F32)


def _dot_tn(a, b):
    return lax.dot_general(a, b, (((0,), (0,)), ((), ())), preferred_element_type=F32)


def _head_sum(z, bd):
    w = bd.shape[0]
    zb = z.astype(BF16)
    parts = [_dot(zb[:, a:a + w], bd) for a in range(0, z.shape[1], w)]
    return parts[0] if len(parts) == 1 else jnp.concatenate(parts, axis=1)


def _swap_halves(t):
    w = t.shape[1]
    lane = lax.broadcasted_iota(jnp.int32, t.shape, 1)
    return jnp.where(lane % HEAD_DIM < HEAD_DIM // 2, pltpu.roll(t, w - 32, 1), pltpu.roll(t, 32, 1))


def _qknorm_rope(t, g, cos, sin_s, bd):
    r = lax.rsqrt(_head_sum(t * t, bd) * (1.0 / HEAD_DIM) + EPS)
    n = (t * r) * g
    return n * cos + _swap_halves(n) * sin_s


def _qknorm_rope_bwd(dout, t, g, cos, sin_s, bd):
    dout, t = dout.astype(F32), t.astype(F32)
    dn = dout * cos + _swap_halves(dout * sin_s)
    r = lax.rsqrt(_head_sum(t * t, bd) * (1.0 / HEAD_DIM) + EPS)
    tr = t * r
    u = dn * g
    dt = r * (u - tr * (_head_sum(u * tr, bd) * (1.0 / HEAD_DIM)))
    return dt, dn * tr


def _sigmoid(g):
    return 1.0 / (1.0 + jnp.exp(-g))


def _expand_heads(st):
    t = st.shape[0]
    lane = lax.broadcasted_iota(jnp.int32, (t, LANES), 1)
    chunks = []
    for c in range(WIDTH // LANES):
        chunks.append(jnp.where(lane < HEAD_DIM, st[:, 2 * c:2 * c + 1], st[:, 2 * c + 1:2 * c + 2]))
    return jnp.concatenate(chunks, axis=1)


def _reduce_heads(z):
    t = z.shape[0]
    lane = lax.broadcasted_iota(jnp.int32, (t, LANES), 1)
    out = jnp.zeros((t, LANES), F32)
    for c in range(WIDTH // LANES):
        zc = z[:, c * LANES:(c + 1) * LANES]
        for ph in range(2):
            s = jnp.sum(jnp.where((lane // HEAD_DIM) == ph, zc, 0.0), axis=-1, keepdims=True)
            out = jnp.where(lane == 2 * c + ph, s, out)
    return out


def _fold_scratch(w):
    return pltpu.VMEM((w // LANES, ROW_TILE, LANES), F32)


def _store_folded(out_ref, val, scr, col0=0):
    w = val.shape[1]
    n = w // LANES
    for c in range(n):
        scr[c] = val[:, c * LANES:(c + 1) * LANES]
    for r in range(FOLD):
        piece = [scr[c, pl.ds(r, FOLD_ROWS, stride=FOLD), :] for c in range(n)]
        out_ref[r, :, col0:col0 + w] = (piece[0] if n == 1 else jnp.concatenate(piece, axis=1)).astype(out_ref.dtype)


def _load_folded(in_ref, scr):
    n = in_ref.shape[2] // LANES
    for r in range(FOLD):
        blk = in_ref[r].astype(F32)
        for c in range(n):
            scr[c, pl.ds(r, FOLD_ROWS, stride=FOLD), :] = blk[:, c * LANES:(c + 1) * LANES]
    return scr[0] if n == 1 else jnp.concatenate([scr[c] for c in range(n)], axis=1)


def _store_folded_bf16(out_ref, val, perm):
    folded = _dot(perm, val.astype(BF16)).astype(out_ref.dtype)
    for r in range(FOLD):
        out_ref[r] = folded[r * FOLD_ROWS:(r + 1) * FOLD_ROWS]


def _load_folded_bf16(in_ref, perm):
    blk = jnp.concatenate([in_ref[r] for r in range(FOLD)], axis=0)
    return _dot(perm, blk)


def _rows(w, tm=ROW_TILE):
    return pl.BlockSpec((tm, w), lambda i: (i, 0))


def _folded_rows(w):
    return pl.BlockSpec((FOLD, FOLD_ROWS, w), lambda i: (0, i, 0))


def _whole(shape):
    return pl.BlockSpec(shape, lambda i: (0,) * len(shape))


def _inproj(x2, gain, w_bf, cos, sin_s, gqa, gka, gqb, gkb, bd256, bd128, w_out_blk):
    s = x2.shape[0]
    tm = ROW_TILE
    n_steps = s // tm
    n_rel = N_CHIP - 1
    o_half = OUT_ROWS // 2

    def body(x_ref, gain_ref, w_hbm, cos_ref, sin_ref, gqa_ref, gka_ref, gqb_ref, gkb_ref, bd256_ref, bd128_ref,
             wout_ref, qa_ref, kva_ref, qb_ref, kvb_ref, qbf_ref, kvbf_ref,
             qa_raw_ref, ka_raw_ref, ga_ref, qb_raw_ref, kb_raw_ref, gb_ref, wout_all_ref,
             w_vmem, scr, land, ici_send, ici_recv, d2d_send, d2d_recv):
        i = pl.program_id(0)
        px_, py_, c = _position()
        b = 2 * px_ + py_

        def piece(chip_idx, core):
            return land.at[chip_idx, pl.ds(pl.multiple_of(core * o_half, o_half), o_half)]

        def other_chip(d):
            ox, oy = px_ ^ (d >> 1), py_ ^ (d & 1)
            return ox, oy, 2 * ox + oy

        def ici_copy(d, chip_idx):
            ox, oy, _ = other_chip(d)
            return pltpu.make_async_remote_copy(
                src_ref=piece(chip_idx, c), dst_ref=piece(chip_idx, c), send_sem=ici_send.at[d - 1],
                recv_sem=ici_recv.at[d - 1], device_id=(ox, oy, c), device_id_type=MESH)

        def d2d_copy(d, core):
            return pltpu.make_async_remote_copy(
                src_ref=piece(other_chip(d)[2], core), dst_ref=piece(other_chip(d)[2], core),
                send_sem=d2d_send.at[d - 1], recv_sem=d2d_recv.at[d - 1], device_id=(px_, py_, 1 - c),
                device_id_type=MESH)

        @pl.when(i == 0)
        def _():
            pltpu.sync_copy(w_hbm, w_vmem)
            land[b] = wout_ref[...].astype(BF16)
            for d in range(1, N_CHIP):
                ici_copy(d, b).start()

        @pl.when(i == n_steps // 2)
        def _():
            for d in range(1, N_CHIP):
                ici_copy(d, other_chip(d)[2]).wait_recv()
                d2d_copy(d, c).start()

        @pl.when(i == n_steps - 1)
        def _():
            for d in range(1, N_CHIP):
                d2d_copy(d, 1 - c).wait_recv()
            for d in range(1, N_CHIP):
                ici_copy(d, b).wait_send()
                d2d_copy(d, c).wait_send()
            for k in range(N_CHIP):
                wout_all_ref[k * OUT_ROWS:(k + 1) * OUT_ROWS, :] = land[k]

        xt = x_ref[...]
        r = lax.rsqrt(jnp.mean(xt * xt, axis=-1, keepdims=True) + EPS)
        h = ((xt * r) * gain_ref[...]).astype(BF16)
        cos1 = cos_ref[...]
        sin1 = sin_ref[...]
        cos4 = jnp.tile(cos1, (1, 4))
        sin4 = jnp.tile(sin1, (1, 4))

        def seg(a, b):
            return _dot_nt(h, w_vmem[a:b, :])

        t = seg(C_QA, C_KA)
        qa_raw_ref[...] = t.astype(BF16)
        qa_ref[...] = (_qknorm_rope(t, gqa_ref[...], cos4, sin4, bd256_ref[...]) * SCALE).astype(BF16)
        t = seg(C_KA, C_VA)
        ka_raw_ref[...] = t.astype(BF16)
        kva_ref[:, :A_KV_WIDTH] = _qknorm_rope(t, gka_ref[...], cos1, sin1, bd128_ref[...]).astype(BF16)
        kva_ref[:, A_KV_WIDTH:] = seg(C_VA, C_GA).astype(BF16)
        ga_ref[...] = seg(C_GA, C_QB).astype(BF16)
        t = seg(C_QB, C_KB)
        qb_raw_ref[...] = t.astype(BF16)
        t = _qknorm_rope(t, gqb_ref[...], cos4, sin4, bd256_ref[...]) * SCALE
        qb_ref[...] = t.astype(BF16)
        _store_folded(qbf_ref, t, scr)
        t = seg(C_KB, C_VB)
        kb_raw_ref[...] = t.astype(BF16)
        t = _qknorm_rope(t, gkb_ref[...], cos4, sin4, bd256_ref[...])
        kvb_ref[:, :WIDTH] = t.astype(BF16)
        _store_folded(kvbf_ref, t, scr)
        t = seg(C_VB, C_GB)
        kvb_ref[:, WIDTH:] = t.astype(BF16)
        _store_folded(kvbf_ref, t, scr, WIDTH)
        gb_ref[...] = seg(C_GB, C_END).astype(BF16)

    sds = jax.ShapeDtypeStruct
    ln = s // FOLD
    out_shape = (sds((s, WIDTH), BF16), sds((s, 2 * A_KV_WIDTH), BF16), sds((s, WIDTH), BF16),
                 sds((s, 2 * WIDTH), BF16), sds((FOLD, ln, WIDTH), BF16), sds((FOLD, ln, 2 * WIDTH), BF16),
                 sds((s, WIDTH), BF16), sds((s, A_KV_WIDTH), BF16), sds((s, WIDTH), BF16),
                 sds((s, WIDTH), BF16), sds((s, WIDTH), BF16), sds((s, WIDTH), BF16),
                 sds((D_MODEL, D_MODEL), BF16))
    out_specs = (_rows(WIDTH), _rows(2 * A_KV_WIDTH), _rows(WIDTH), _rows(2 * WIDTH),
                 _folded_rows(WIDTH), _folded_rows(2 * WIDTH),
                 _rows(WIDTH), _rows(A_KV_WIDTH), _rows(WIDTH), _rows(WIDTH), _rows(WIDTH), _rows(WIDTH),
                 _whole((D_MODEL, D_MODEL)))
    dma = pltpu.SemaphoreType.DMA
    return pl.pallas_call(
        body, name="inproj_fwd", grid=(n_steps,),
        in_specs=[_rows(D_MODEL), _whole(gain.shape), pl.BlockSpec(memory_space=pl.ANY), _rows(LANES), _rows(LANES),
                  _whole(gqa.shape), _whole(gka.shape), _whole(gqb.shape), _whole(gkb.shape), _whole(bd256.shape),
                  _whole(bd128.shape), _whole(w_out_blk.shape)],
        out_specs=out_specs, out_shape=out_shape,
        scratch_shapes=[pltpu.VMEM((IN_WIDTH, D_MODEL), BF16), _fold_scratch(WIDTH),
                        pltpu.VMEM((N_CHIP, OUT_ROWS, D_MODEL), BF16),
                        dma((n_rel,)), dma((n_rel,)), dma((n_rel,)), dma((n_rel,))],
        compiler_params=pltpu.CompilerParams(dimension_semantics=("arbitrary",), vmem_limit_bytes=VMEM_LIMIT),
    )(x2, gain, w_bf, cos, sin_s, gqa, gka, gqb, gkb, bd256, bd128, w_out_blk)


def _seq_pos(idx, dil):
    if dil == 4:
        return 4 * (idx % 32) + idx // 32
    return idx


def _upper_mask(dil, r0=0, rows=2 * BLOCK):
    qi = (lax.broadcasted_iota(jnp.int32, (rows, BLOCK), 0) + r0) % BLOCK
    kj = lax.broadcasted_iota(jnp.int32, (rows, BLOCK), 1)
    return _seq_pos(kj, dil) > _seq_pos(qi, dil)


def _eye_mask(r0=0, rows=2 * BLOCK):
    qi = (lax.broadcasted_iota(jnp.int32, (rows, BLOCK), 0) + r0) % BLOCK
    kj = lax.broadcasted_iota(jnp.int32, (rows, BLOCK), 1)
    return qi == kj


def _stack_heads(a2, c, gqa):
    lane = lax.broadcasted_iota(jnp.int32, (1, LANES), 1) // HEAD_DIM
    zero = jnp.zeros_like(a2)
    if gqa:
        keep = lane == (c // 2)
        return jnp.concatenate([jnp.where(keep, a2, zero), jnp.where(keep, _swap_heads(a2), zero)], axis=0)
    return jnp.concatenate([jnp.where(lane == 0, a2, zero), jnp.where(lane == 1, a2, zero)], axis=0)


def _unstack_heads(a, c, gqa):
    lane = lax.broadcasted_iota(jnp.int32, (1, LANES), 1) // HEAD_DIM
    if gqa:
        return jnp.where(lane == (c // 2), a[:BLOCK], _swap_heads(a[BLOCK:]))
    return jnp.where(lane == 0, a[:BLOCK], a[BLOCK:])


def _stacked_head_ids(c, gqa):
    if gqa:
        return 2 * c + c // 2, 2 * c + 1 - c // 2
    return 2 * c, 2 * c + 1


def _per_head_rows(blk, heads):
    return jnp.concatenate([blk[:, heads[0]:heads[0] + 1], blk[:, heads[1]:heads[1] + 1]], axis=0)


def _attn_view(a, dil):
    if dil == 1:
        return a[None]
    if dil == 4:
        return a.reshape(4, 4, a.shape[1], a.shape[2])
    return a


def _attn_unview(a, dil):
    if dil == 1:
        return a[0]
    if dil == 4:
        return a.reshape(FOLD, a.shape[2], a.shape[3])
    return a


ATTN_BLOCKS_PER_STEP = 8


def _attn_specs(dil):
    if dil == 4:
        def spec(n, fn):
            return lambda w: pl.BlockSpec((4, None, n * BLOCK // 4, w), lambda r, i: (0, r, fn(i), 0))
    else:
        def spec(n, fn):
            return lambda w: pl.BlockSpec((None, n * BLOCK, w), lambda r, i: (r, fn(i), 0))
    return spec


def _blk_rows(g, dil):
    n = BLOCK // 4 if dil == 4 else BLOCK
    if isinstance(g, int):
        return slice(g * n, (g + 1) * n)
    return pl.ds(pl.multiple_of(g * n, n), n)


def _blk_load(ref, sl, dil, g=0):
    if dil == 4:
        return ref[:, _blk_rows(g, dil), sl].reshape(BLOCK, sl.stop - sl.start)
    return ref[_blk_rows(g, dil), sl]


def _blk_store(ref, sl, val, dil, g=0):
    val = val.astype(ref.dtype)
    if dil == 4:
        ref[:, _blk_rows(g, dil), sl] = val.reshape(4, BLOCK // 4, sl.stop - sl.start)
    else:
        ref[_blk_rows(g, dil), sl] = val


def _swap_heads(a):
    return pltpu.roll(a.astype(F32), HEAD_DIM, 1).astype(a.dtype)


STAT_SHIFT = 8


def _attn_fwd(q, kv, sinks, *, dil, max_dist, name):
    q, kv = _attn_view(q, dil), _attn_view(kv, dil)
    kw = kv.shape[-1] // 2
    gqa = kw == A_KV_WIDTH
    n_seq = dil
    nb = (q.shape[-2] * (4 if dil == 4 else 1)) // BLOCK
    per_step = min(ATTN_BLOCKS_PER_STEP, nb)
    with_sinks = sinks is not None
    all_lanes = slice(0, LANES)
    assert max_dist in (BLOCK - 1, BLOCK) and nb % per_step == 0 and (per_step == 1 or per_step % 2 == 0)
    diag = max_dist == BLOCK

    def body(*refs):
        if with_sinks:
            q_ref, kvp_ref, kvc_ref, sink_ref, o_ref, ml_ref = refs
        else:
            q_ref, kvp_ref, kvc_ref, o_ref, ml_ref = refs

        chunks = range(WIDTH // LANES)

        def matmuls_in(g, has_prev):
            prev_ref, prev_g = (kvp_ref, 0) if (isinstance(g, int) and g == 0) else (kvc_ref, g - 1)
            scores, values = [], []
            for c in chunks:
                sl = slice(c * LANES, (c + 1) * LANES)
                ksl = slice(0, LANES) if gqa else sl
                vsl = slice(ksl.start + kw, ksl.stop + kw)
                kcur, vcur = _blk_load(kvc_ref, ksl, dil, g), _blk_load(kvc_ref, vsl, dil, g)
                qs = _stack_heads(_blk_load(q_ref, sl, dil, g), c, gqa)
                if has_prev:
                    kcur = jnp.concatenate([_blk_load(prev_ref, ksl, dil, prev_g), kcur], axis=0)
                    vcur = jnp.concatenate([_blk_load(prev_ref, vsl, dil, prev_g), vcur], axis=0)
                scores.append(_dot_nt(qs, kcur))
                values.append(vcur)
            return scores, values

        def tile_ops(has_prev, scores):
            lane = lax.broadcasted_iota(jnp.int32, (1, LANES), 1)
            with_diag = diag and has_prev
            upper, eye = _upper_mask(dil), _eye_mask()
            first_rows = lax.broadcasted_iota(jnp.int32, (2 * BLOCK, 1), 0) < BLOCK
            ml_blk = jnp.zeros((BLOCK, LANES), F32)
            probs = []
            for c in chunks:
                heads = _stacked_head_ids(c, gqa)
                s = scores[c]
                if has_prev:
                    s_p = s[:, :BLOCK]
                    sc = jnp.where(upper, s_p, s[:, BLOCK:])
                else:
                    sc = jnp.where(upper, NEG, s)
                if with_diag:
                    sd = jnp.where(eye, s_p, NEG)
                    m = jnp.max(jnp.maximum(sc, sd), axis=-1, keepdims=True)
                else:
                    m = jnp.max(sc, axis=-1, keepdims=True)
                if with_sinks:
                    sk = jnp.where(first_rows, sink_ref[0, heads[0]], sink_ref[0, heads[1]])
                    m = jnp.maximum(m, sk)
                p = jnp.exp(sc - m)
                zero = jnp.zeros_like(p)
                if with_diag:
                    pd = jnp.exp(sd - m)
                    l = jnp.sum(p + pd, axis=-1, keepdims=True)
                else:
                    pd = zero
                    l = jnp.sum(p, axis=-1, keepdims=True)
                if with_sinks:
                    l = l + jnp.exp(sk - m)
                pf = jnp.where(upper, zero, p)
                if has_prev:
                    pf = jnp.concatenate([jnp.where(upper, p, pd), pf], axis=1)
                probs.append(pf.astype(BF16))
                for n, h in enumerate(heads):
                    rows = slice(n * BLOCK, (n + 1) * BLOCK)
                    ml_blk = jnp.where(lane == h, m[rows], ml_blk)
                    ml_blk = jnp.where(lane == h + STAT_SHIFT, l[rows], ml_blk)
            return probs, ml_blk

        def matmuls_out(g, values, probs, ml_blk):
            for c in chunks:
                sl = slice(c * LANES, (c + 1) * LANES)
                _blk_store(o_ref, sl, _unstack_heads(_dot(probs[c], values[c]), c, gqa), dil, g)
            _blk_store(ml_ref, all_lanes, ml_blk, dil, g)

        def run(blocks):
            ins = [matmuls_in(g, has_prev) for g, has_prev in blocks]
            mids = [tile_ops(has_prev, scores) for (_, has_prev), (scores, _) in zip(blocks, ins)]
            for (g, _), (_, values), (probs, ml_blk) in zip(blocks, ins, mids):
                matmuls_out(g, values, probs, ml_blk)

        second = [(1, True)] if per_step > 1 else []

        @pl.when(pl.program_id(1) == 0)
        def _():
            run([(0, False)] + second)

        @pl.when(pl.program_id(1) > 0)
        def _():
            run([(0, True)] + second)

        if per_step > 2:
            def rest(pair, carry):
                run([(2 * pair, True), (2 * pair + 1, True)])
                return carry

            lax.fori_loop(1, per_step // 2, rest, 0)

    spec = _attn_specs(dil)
    cur = spec(per_step, lambda i: i)
    prev = spec(1, lambda i: jnp.maximum(i * per_step - 1, 0))
    in_specs = [cur(WIDTH), prev(2 * kw), cur(2 * kw)]
    args = [q, kv, kv]
    if with_sinks:
        in_specs.append(pl.BlockSpec(memory_space=pltpu.SMEM))
        args.append(sinks)
    stats = jax.ShapeDtypeStruct(q.shape[:-1] + (LANES,), F32)
    o, ml = pl.pallas_call(
        body, name=name, grid=(n_seq, nb // per_step), in_specs=in_specs,
        out_specs=(cur(WIDTH), cur(LANES)),
        out_shape=(jax.ShapeDtypeStruct(q.shape, BF16), stats),
        compiler_params=pltpu.CompilerParams(dimension_semantics=("arbitrary", "arbitrary"),
                                             vmem_limit_bytes=VMEM_LIMIT),
    )(*args)
    return _attn_unview(o, dil), _attn_unview(ml, dil)


def _attn_bwd(q, kv, do, ld, *, dil, max_dist, name, onto=None):
    q, kv, do, ld = (_attn_view(a, dil) for a in (q, kv, do, ld))
    onto = () if onto is None else tuple(_attn_view(a, dil) for a in onto)
    kw = kv.shape[-1] // 2
    gqa = kw == A_KV_WIDTH
    n_seq = dil
    nb = (q.shape[-2] * (4 if dil == 4 else 1)) // BLOCK
    n_kc = kw // LANES
    per_step = min(ATTN_BLOCKS_PER_STEP, nb)
    all_lanes = slice(0, LANES)
    assert max_dist in (BLOCK - 1, BLOCK) and nb % per_step == 0 and (per_step == 1 or per_step % 2 == 0)
    diag = max_dist == BLOCK

    def body(q_ref, kvp_ref, kvc_ref, do_ref, ld_ref, *rest_refs):
        dq_ref, dkv_ref, ck_ref, cv_ref = rest_refs[len(onto):]
        i = pl.program_id(1)

        def store(ref, sl, val, blk):
            if onto:
                val = val + _blk_load(rest_refs[0 if ref is dq_ref else 1], sl, dil, blk).astype(F32)
            _blk_store(ref, sl, val, dil, blk)

        chunks = range(WIDTH // LANES)

        def matmuls_in(g, has_prev):
            prev_ref, prev_g = (kvp_ref, 0) if (isinstance(g, int) and g == 0) else (kvc_ref, g - 1)
            operands, products = [], []
            for c in chunks:
                sl = slice(c * LANES, (c + 1) * LANES)
                kc = 0 if gqa else c
                ksl = slice(kc * LANES, (kc + 1) * LANES)
                vsl = slice(ksl.start + kw, ksl.stop + kw)
                k2, v2 = _blk_load(kvc_ref, ksl, dil, g), _blk_load(kvc_ref, vsl, dil, g)
                if has_prev:
                    k2 = jnp.concatenate([_blk_load(prev_ref, ksl, dil, prev_g), k2], axis=0)
                    v2 = jnp.concatenate([_blk_load(prev_ref, vsl, dil, prev_g), v2], axis=0)
                qs = _stack_heads(_blk_load(q_ref, sl, dil, g), c, gqa)
                dos = _stack_heads(_blk_load(do_ref, sl, dil, g), c, gqa)
                operands.append((qs, dos, k2))
                products.append((_dot_nt(qs, k2), _dot_nt(dos, v2)))
            return operands, products

        def tile_ops(g, has_prev, products):
            upper, eye = _upper_mask(dil), _eye_mask()
            ld_blk = _blk_load(ld_ref, all_lanes, dil, g)
            weights = []
            for c in chunks:
                heads = _stacked_head_ids(c, gqa)
                lse2 = _per_head_rows(ld_blk, heads)
                dl2 = _per_head_rows(ld_blk, tuple(h + STAT_SHIFT for h in heads))
                s, dp = products[c]
                if has_prev:
                    s_p, dp_p = s[:, :BLOCK], dp[:, :BLOCK]
                    sc = jnp.where(upper, s_p, s[:, BLOCK:])
                    dpc = jnp.where(upper, dp_p, dp[:, BLOCK:])
                else:
                    sc = jnp.where(upper, NEG, s)
                    dpc = dp
                p = jnp.exp(sc - lse2)
                ds = p * (dpc - dl2)
                zero = jnp.zeros_like(p)
                pf = jnp.where(upper, zero, p)
                dsf = jnp.where(upper, zero, ds)
                if has_prev:
                    if diag:
                        pd = jnp.exp(jnp.where(eye, s_p, NEG) - lse2)
                        dsd = pd * (dp_p - dl2)
                    else:
                        pd = dsd = zero
                    pf = jnp.concatenate([jnp.where(upper, p, pd), pf], axis=1)
                    dsf = jnp.concatenate([jnp.where(upper, ds, dsd), dsf], axis=1)
                weights.append((pf.astype(BF16), dsf.astype(BF16)))
            return weights

        def matmuls_out(g, has_prev, operands, weights):
            seq_blk = i * per_step + g
            dk_acc = [None] * n_kc
            dv_acc = [None] * n_kc
            for c in chunks:
                sl = slice(c * LANES, (c + 1) * LANES)
                kc = 0 if gqa else c
                qs, dos, k2 = operands[c]
                pf, dsf = weights[c]
                store(dq_ref, sl, _unstack_heads(_dot(dsf, k2), c, gqa) * SCALE, g)
                dk2 = _dot_tn(dsf, qs)
                dv2 = _dot_tn(pf, dos)
                dk_acc[kc] = dk2 if dk_acc[kc] is None else dk_acc[kc] + dk2
                dv_acc[kc] = dv2 if dv_acc[kc] is None else dv_acc[kc] + dv2
            for kc in range(n_kc):
                sl = slice(kc * LANES, (kc + 1) * LANES)
                vsl = slice(sl.start + kw, sl.stop + kw)
                if has_prev:
                    store(dkv_ref, sl, ck_ref[:, sl] + dk_acc[kc][:BLOCK], seq_blk - 1)
                    store(dkv_ref, vsl, cv_ref[:, sl] + dv_acc[kc][:BLOCK], seq_blk - 1)
                    ck_ref[:, sl] = dk_acc[kc][BLOCK:]
                    cv_ref[:, sl] = dv_acc[kc][BLOCK:]
                else:
                    ck_ref[:, sl] = dk_acc[kc]
                    cv_ref[:, sl] = dv_acc[kc]

        def run(blocks):
            ins = [matmuls_in(g, has_prev) for g, has_prev in blocks]
            mids = [tile_ops(g, has_prev, products) for (g, has_prev), (_, products) in zip(blocks, ins)]
            for (g, has_prev), (operands, _), weights in zip(blocks, ins, mids):
                matmuls_out(g, has_prev, operands, weights)

        second = [(1, True)] if per_step > 1 else []

        @pl.when(i == 0)
        def _():
            run([(0, False)] + second)

        @pl.when(i > 0)
        def _():
            run([(0, True)] + second)

        if per_step > 2:
            def rest(pair, carry):
                run([(2 * pair, True), (2 * pair + 1, True)])
                return carry

            lax.fori_loop(1, per_step // 2, rest, 0)

        @pl.when(i == nb // per_step - 1)
        def _():
            for kc in range(n_kc):
                sl = slice(kc * LANES, (kc + 1) * LANES)
                store(dkv_ref, sl, ck_ref[:, sl], nb - 1)
                store(dkv_ref, slice(sl.start + kw, sl.stop + kw), cv_ref[:, sl], nb - 1)

    spec = _attn_specs(dil)
    cur = spec(per_step, lambda i: i)
    prev = spec(1, lambda i: jnp.maximum(i * per_step - 1, 0))
    if dil == 4:
        whole = pl.BlockSpec((4, None, kv.shape[2], 2 * kw), lambda r, i: (0, r, 0, 0))
    else:
        whole = pl.BlockSpec((None, kv.shape[1], 2 * kw), lambda r, i: (r, 0, 0))
    sds = jax.ShapeDtypeStruct
    dq, dkv = pl.pallas_call(
        body, name=name, grid=(n_seq, nb // per_step),
        in_specs=[cur(WIDTH), prev(2 * kw), cur(2 * kw), cur(WIDTH), cur(LANES)] + [cur(WIDTH), whole][:len(onto)],
        out_specs=(cur(WIDTH), whole),
        out_shape=(sds(q.shape, BF16), sds(kv.shape, BF16)),
        scratch_shapes=[pltpu.VMEM((BLOCK, kw), F32), pltpu.VMEM((BLOCK, kw), F32)],
        compiler_params=pltpu.CompilerParams(dimension_semantics=("arbitrary", "arbitrary"),
                                             vmem_limit_bytes=VMEM_LIMIT),
    )(q, kv, kv, do, ld, *onto)
    return _attn_unview(dq, dil), _attn_unview(dkv, dil)


def _outproj(att_a, att_b1, att_b4, att_b16, g_a, g_b, x2, tgt2, w_out_bf, sink_row, perm):
    s = x2.shape[0]
    tm = ROW_TILE

    def body(oa_ref, mla_ref, ob1_ref, ml1_ref, ob4_ref, ml4_ref, ob16_ref, ml16_ref,
             ga_ref, gb_ref, x_ref, t_ref, w_ref, sink_ref, perm_ref,
             dy_ref, doa_ref, dob_ref, dobf_ref, dga_ref, dgb_ref, lda_ref, ldb_ref, ldbf_ref,
             gw_ref, loss_ref, dsink_ref, scr_st):
        i = pl.program_id(0)
        perm = perm_ref[...]
        lane = lax.broadcasted_iota(jnp.int32, (tm, LANES), 1)
        used = lane < HEADS

        def split(ml):
            return jnp.where(used, ml, 0.0), jnp.where(used, pltpu.roll(ml, LANES - STAT_SHIFT, 1), 1.0)

        @pl.when(i == 0)
        def _():
            gw_ref[...] = jnp.zeros_like(gw_ref)
            loss_ref[...] = jnp.zeros_like(loss_ref)
            dsink_ref[...] = jnp.zeros_like(dsink_ref)

        ms, ls = zip(split(ml1_ref[...]), split(_load_folded(ml4_ref, scr_st)), split(_load_folded(ml16_ref, scr_st)))
        mx = jnp.maximum(jnp.maximum(ms[0], ms[1]), ms[2])
        scale = [jnp.exp(mp - mx) for mp in ms]
        den = (ls[0] * scale[0] + ls[1] * scale[1]) + ls[2] * scale[2]
        lse_b = jnp.where(used, mx + jnp.log(den), 0.0)
        inv_den = 1.0 / den
        o_b = _expand_heads(scale[0] * inv_den) * ob1_ref[...].astype(F32)
        o_b = o_b + _expand_heads(scale[1] * inv_den) * _load_folded_bf16(ob4_ref, perm)
        o_b = o_b + _expand_heads(scale[2] * inv_den) * _load_folded_bf16(ob16_ref, perm)
        m_a, l_a = split(mla_ref[...])
        lse_a = jnp.where(used, m_a + jnp.log(l_a), 0.0)
        o_a = _expand_heads(1.0 / l_a) * oa_ref[...].astype(F32)
        g_a = ga_ref[...].astype(F32)
        g_b = gb_ref[...].astype(F32)
        sg_a = _sigmoid(g_a)
        sg_b = _sigmoid(g_b)
        silu_a = g_a * sg_a
        silu_b = g_b * sg_b
        mixed = jnp.concatenate([o_a * silu_a, o_b * silu_b], axis=1).astype(BF16)
        w = w_ref[...]
        y = x_ref[...] + _dot(mixed, w)
        diff = y - t_ref[...]
        loss_ref[...] += (0.5 / D_MODEL) * jnp.sum(diff * diff)
        dy = diff * (1.0 / D_MODEL)
        dy_ref[...] = dy
        dyb = dy.astype(BF16)
        gw_ref[...] += _dot_tn(mixed, dyb)
        dmixed = _dot_nt(dyb, w)
        dm_a = dmixed[:, :WIDTH]
        dm_b = dmixed[:, WIDTH:]
        do_a = dm_a * silu_a
        do_b = dm_b * silu_b
        doa_ref[...] = do_a.astype(BF16)
        dob_ref[...] = do_b.astype(BF16)
        _store_folded_bf16(dobf_ref, do_b, perm)
        dga_ref[...] = (dm_a * o_a * (sg_a * (1.0 + g_a * (1.0 - sg_a)))).astype(BF16)
        dgb_ref[...] = (dm_b * o_b * (sg_b * (1.0 + g_b * (1.0 - sg_b)))).astype(BF16)
        dl_a = _reduce_heads(do_a * o_a)
        dl_b = _reduce_heads(do_b * o_b)
        lda_ref[...] = lse_a + pltpu.roll(dl_a, STAT_SHIFT, 1)
        ld_b = lse_b + pltpu.roll(dl_b, STAT_SHIFT, 1)
        ldb_ref[...] = ld_b
        _store_folded(ldbf_ref, ld_b, scr_st)
        dsink_ref[...] -= jnp.sum(jnp.exp(sink_ref[...] - lse_a) * dl_a, axis=0, keepdims=True)

    sds = jax.ShapeDtypeStruct
    ln = s // FOLD
    natural = [_rows(WIDTH), _rows(LANES)]
    folded = [_folded_rows(WIDTH), _folded_rows(LANES)]
    return pl.pallas_call(
        body, name="outproj_fwd_bwd", grid=(s // tm,),
        in_specs=natural + natural + folded + folded
                 + [_rows(WIDTH), _rows(WIDTH), _rows(D_MODEL), _rows(D_MODEL), _whole((D_MODEL, D_MODEL)),
                    _whole((1, LANES)), _whole(perm.shape)],
        out_specs=(_rows(D_MODEL), _rows(WIDTH), _rows(WIDTH), _folded_rows(WIDTH), _rows(WIDTH), _rows(WIDTH),
                   _rows(LANES), _rows(LANES), _folded_rows(LANES),
                   _whole((D_MODEL, D_MODEL)), _whole((1, LANES)), _whole((1, LANES))),
        out_shape=(sds((s, D_MODEL), F32), sds((s, WIDTH), BF16), sds((s, WIDTH), BF16),
                   sds((FOLD, ln, WIDTH), BF16), sds((s, WIDTH), BF16), sds((s, WIDTH), BF16),
                   sds((s, LANES), F32), sds((s, LANES), F32), sds((FOLD, ln, LANES), F32),
                   sds((D_MODEL, D_MODEL), F32), sds((1, LANES), F32), sds((1, LANES), F32)),
        scratch_shapes=[_fold_scratch(LANES)],
        compiler_params=pltpu.CompilerParams(dimension_semantics=("arbitrary",), vmem_limit_bytes=VMEM_LIMIT),
    )(*att_a, *att_b1, *att_b4, *att_b16, g_a, g_b, x2, tgt2, w_out_bf, sink_row, perm)


def _inproj_bwd(x2, dy, gain, w_bf, cos, sin_s, gqa, gka, gqb, gkb, bd256, bd128, perm,
                qa_raw, ka_raw, qb_raw, kb_raw, d_a, d_b1, d_bf, dg_a, dg_b):
    s = x2.shape[0]
    tm = ROW_TILE

    def body(x_ref, dy_ref, gain_ref, w_hbm, cos_ref, sin_ref, gqa_ref, gka_ref, gqb_ref, gkb_ref, bd256_ref,
             bd128_ref, perm_ref, qa_raw_ref, ka_raw_ref, qb_raw_ref, kb_raw_ref, dqa_ref, dkva_ref,
             dq1_ref, dkv1_ref, dqf_ref, dkvf_ref, dga_ref, dgb_ref,
             gx_ref, ht_ref, win_ref,
             dgain_ref, dgqa_ref, dgka_ref, dgqb_ref, dgkb_ref, w_vmem, dproj_ref):
        i = pl.program_id(0)
        perm = perm_ref[...]

        @pl.when(i == 0)
        def _():
            pltpu.sync_copy(w_hbm, w_vmem)
            dgain_ref[...] = jnp.zeros_like(dgain_ref)
            dgqa_ref[...] = jnp.zeros_like(dgqa_ref)
            dgka_ref[...] = jnp.zeros_like(dgka_ref)
            dgqb_ref[...] = jnp.zeros_like(dgqb_ref)
            dgkb_ref[...] = jnp.zeros_like(dgkb_ref)

        cos1 = cos_ref[...]
        sin1 = sin_ref[...]
        cos4 = jnp.tile(cos1, (1, 4))
        sin4 = jnp.tile(sin1, (1, 4))

        dt, dg = _qknorm_rope_bwd(dqa_ref[...], qa_raw_ref[...], gqa_ref[...], cos4, sin4, bd256_ref[...])
        dproj_ref[:, C_QA:C_KA] = dt.astype(BF16)
        dgqa_ref[...] += jnp.sum(dg, axis=0, keepdims=True)
        dt, dg = _qknorm_rope_bwd(dkva_ref[:, :A_KV_WIDTH], ka_raw_ref[...], gka_ref[...], cos1, sin1,
                                  bd128_ref[...])
        dproj_ref[:, C_KA:C_VA] = dt.astype(BF16)
        dgka_ref[...] += jnp.sum(dg, axis=0, keepdims=True)
        dproj_ref[:, C_VA:C_GA] = dkva_ref[:, A_KV_WIDTH:]
        dproj_ref[:, C_GA:C_QB] = dga_ref[...]
        dq = dq1_ref[...].astype(F32) + _load_folded_bf16(dqf_ref, perm)
        dt, dg = _qknorm_rope_bwd(dq, qb_raw_ref[...], gqb_ref[...], cos4, sin4, bd256_ref[...])
        dproj_ref[:, C_QB:C_KB] = dt.astype(BF16)
        dgqb_ref[...] += jnp.sum(dg, axis=0, keepdims=True)
        dkv = dkv1_ref[...].astype(F32) + _load_folded_bf16(dkvf_ref, perm)
        dt, dg = _qknorm_rope_bwd(dkv[:, :WIDTH], kb_raw_ref[...], gkb_ref[...], cos4, sin4, bd256_ref[...])
        dproj_ref[:, C_KB:C_VB] = dt.astype(BF16)
        dgkb_ref[...] += jnp.sum(dg, axis=0, keepdims=True)
        dproj_ref[:, C_VB:C_GB] = dkv[:, WIDTH:].astype(BF16)
        dproj_ref[:, C_GB:C_END] = dgb_ref[...]
        for k, start in enumerate(WIN_START):
            win_ref[k] = dproj_ref[:, start:start + WIN]

        xt = x_ref[...]
        gain_row = gain_ref[...]
        r = lax.rsqrt(jnp.mean(xt * xt, axis=-1, keepdims=True) + EPS)
        xr = xt * r
        ht_ref[...] = (xr * gain_row).T.astype(BF16)
        dh = _dot(dproj_ref[...], w_vmem[...])
        dgain_ref[...] += jnp.sum(dh * xr, axis=0, keepdims=True)
        u = dh * gain_row
        gx_ref[...] = dy_ref[...] + r * (u - xr * jnp.mean(u * xr, axis=-1, keepdims=True))

    def acc_row(w):
        return pl.BlockSpec((1, w), lambda i: (0, 0))

    sds = jax.ShapeDtypeStruct
    any_spec = pl.BlockSpec(memory_space=pl.ANY)
    win_spec = pl.BlockSpec((N_CHIP, tm, WIN), lambda i: (0, i, 0))
    return pl.pallas_call(
        body, name="inproj_bwd", grid=(s // tm,),
        in_specs=[_rows(D_MODEL), _rows(D_MODEL), _whole(gain.shape), any_spec, _rows(LANES), _rows(LANES),
                  _whole(gqa.shape), _whole(gka.shape), _whole(gqb.shape), _whole(gkb.shape), _whole(bd256.shape),
                  _whole(bd128.shape), _whole(perm.shape),
                  _rows(WIDTH), _rows(A_KV_WIDTH), _rows(WIDTH), _rows(WIDTH),
                  _rows(WIDTH), _rows(2 * A_KV_WIDTH), _rows(WIDTH), _rows(2 * WIDTH)]
                 + [_folded_rows(WIDTH), _folded_rows(2 * WIDTH), _rows(WIDTH), _rows(WIDTH)],
        out_specs=(_rows(D_MODEL), pl.BlockSpec((D_MODEL, tm), lambda i: (0, i)), win_spec, acc_row(D_MODEL), acc_row(WIDTH), acc_row(A_KV_WIDTH), acc_row(WIDTH), acc_row(WIDTH)),
        out_shape=(sds((s, D_MODEL), F32), sds((D_MODEL, s), BF16), sds((N_CHIP, s, WIN), BF16),
                   sds((1, D_MODEL), F32),
                   sds((1, WIDTH), F32), sds((1, A_KV_WIDTH), F32), sds((1, WIDTH), F32), sds((1, WIDTH), F32)),
        scratch_shapes=[pltpu.VMEM((IN_WIDTH, D_MODEL), BF16), pltpu.VMEM((tm, IN_WIDTH), BF16)],
        compiler_params=pltpu.CompilerParams(dimension_semantics=("arbitrary",), vmem_limit_bytes=VMEM_LIMIT),
    )(x2, dy, gain, w_bf, cos, sin_s, gqa, gka, gqb, gkb, bd256, bd128, perm, qa_raw, ka_raw, qb_raw, kb_raw,
      *d_a, *d_b1, *d_bf, dg_a, dg_b)


def _rope_angles(s):
    half = HEAD_DIM // 2
    f32 = np.float32
    inv = np.tile((f32(ROPE_THETA) ** (-np.arange(half, dtype=f32) / f32(half))).astype(f32), 4)
    sign = np.tile(np.concatenate([-np.ones((half,), f32), np.ones((half,), f32)]), 2)
    hi = (np.arange(s // ROPE_SPLIT) * ROPE_SPLIT).astype(f32)[:, None] * inv[None, :]
    lo = np.arange(ROPE_SPLIT).astype(f32)[:, None] * inv[None, :]
    return tuple(jnp.asarray(t.astype(f32)) for t in (np.cos(hi), np.sin(hi), np.cos(lo), np.sin(lo), sign[None, :]))


def _table_shapes(s):
    return (jax.ShapeDtypeStruct((s, LANES), F32), jax.ShapeDtypeStruct((s, LANES), F32),
            jax.ShapeDtypeStruct((2 * LANES, 2 * LANES), BF16), jax.ShapeDtypeStruct((A_KV_WIDTH, A_KV_WIDTH), BF16),
            jax.ShapeDtypeStruct((ROW_TILE, ROW_TILE), BF16)) + tuple(
                jax.ShapeDtypeStruct((1, w), F32) for w in GAIN_WIDTHS)


GAIN_WIDTHS = (WIDTH, A_KV_WIDTH, WIDTH, WIDTH, LANES)


def _gain_rows(q_norm_a, k_norm_a, q_norm_b, k_norm_b, sinks_a):
    rows = [jnp.concatenate([g] * (D_MODEL // HEAD_DIM), axis=1) for g in (q_norm_a, k_norm_a, q_norm_b, k_norm_b)]
    rows.append(jnp.concatenate([sinks_a, jnp.zeros((1, D_MODEL - HEADS), F32)], axis=1))
    return jnp.concatenate(rows + [jnp.zeros((PACK_ROWS - len(rows), D_MODEL), F32)], axis=0)


def _tables(in_refs, out_refs):
    ch_ref, sh_ref, cl_ref, sl_ref, sign_ref, gains_ref = in_refs
    cos_ref, sin_ref, bd256_ref, bd128_ref, perm_ref = out_refs[:5]
    for row, ref in enumerate(out_refs[5:]):
        ref[...] = gains_ref[row:row + 1, 0:ref.shape[1]]
    cl, sl, sign = cl_ref[...], sl_ref[...], sign_ref[...]

    def tile(a, carry):
        rows = pl.ds(pl.multiple_of(a * ROPE_SPLIT, ROPE_SPLIT), ROPE_SPLIT)
        ch, sh = ch_ref[pl.ds(a, 1), :], sh_ref[pl.ds(a, 1), :]
        cos_ref[rows, :] = ch * cl - sh * sl
        sin_ref[rows, :] = (sh * cl + ch * sl) * sign
        return carry

    lax.fori_loop(0, cos_ref.shape[0] // ROPE_SPLIT, tile, 0)
    for ref in (bd256_ref, bd128_ref):
        head = [lax.broadcasted_iota(jnp.int32, ref.shape, d) // HEAD_DIM for d in (0, 1)]
        ref[...] = jnp.where(head[0] == head[1], 1.0, 0.0).astype(BF16)
    f = lax.broadcasted_iota(jnp.int32, perm_ref.shape, 0)
    col = lax.broadcasted_iota(jnp.int32, perm_ref.shape, 1)
    perm_ref[...] = jnp.where(col == FOLD * (f % FOLD_ROWS) + f // FOLD_ROWS, 1.0, 0.0).astype(BF16)


def _local_step(x2, tgt2, norm_gain, w_in_bf, q_norm_a, k_norm_a, sinks_a, q_norm_b, k_norm_b, w_out_blk, tables):
    cos, sin_s, bd256, bd128, perm, gqa, gka, gqb, gkb, sink_row = tables

    (qa, kva, qb, kvb, qbf, kvbf, qa_raw, ka_raw, g_a, qb_raw, kb_raw, g_b, w_out_bf) = _inproj(
        x2, norm_gain, w_in_bf, cos, sin_s, gqa, gka, gqb, gkb, bd256, bd128, w_out_blk)

    att_a = _attn_fwd(qa, kva, sinks_a, dil=1, max_dist=A_MAX_DIST, name="attn_a_fwd")
    att_b1 = _attn_fwd(qb, kvb, None, dil=1, max_dist=B_MAX_DIST, name="attn_b1_fwd")
    att_b4 = _attn_fwd(qbf, kvbf, None, dil=4, max_dist=B_MAX_DIST, name="attn_b4_fwd")
    att_b16 = _attn_fwd(qbf, kvbf, None, dil=16, max_dist=B_MAX_DIST, name="attn_b16_fwd")

    (dy, do_a, do_b, do_bf, dg_a, dg_b, ld_a, ld_b, ld_bf, gw_out, loss_part, dsink) = _outproj(
        att_a, att_b1, att_b4, att_b16, g_a, g_b, x2, tgt2, w_out_bf, sink_row, perm)

    d_a = _attn_bwd(qa, kva, do_a, ld_a, dil=1, max_dist=A_MAX_DIST, name="attn_a_bwd")
    d_b1 = _attn_bwd(qb, kvb, do_b, ld_b, dil=1, max_dist=B_MAX_DIST, name="attn_b1_bwd")
    d_b4 = _attn_bwd(qbf, kvbf, do_bf, ld_bf, dil=4, max_dist=B_MAX_DIST, name="attn_b4_bwd")
    d_b16 = _attn_bwd(qbf, kvbf, do_bf, ld_bf, dil=16, max_dist=B_MAX_DIST, name="attn_b16_bwd", onto=d_b4)

    gx, h_t, wins, dgain, dgqa, dgka, dgqb, dgkb = _inproj_bwd(
        x2, dy, norm_gain, w_in_bf, cos, sin_s, gqa, gka, gqb, gkb, bd256, bd128, perm,
        qa_raw, ka_raw, qb_raw, kb_raw, d_a, d_b1, d_b16, dg_a, dg_b)
    return loss_part, gx, h_t, wins, gw_out, (dgain, dgqa, dgka, dsink, dgqb, dgkb)


def _position():
    return lax.axis_index("x"), lax.axis_index("y"), lax.axis_index("c")


GATHER_CHUNKS = 2


def _gather_weights(blocks, name, side_inputs, side_shapes, side_work):
    n = len(blocks)
    n_in, n_out = len(side_inputs), len(side_shapes)
    ch = GATHER_CHUNKS

    def body(*refs):
        src_refs, side_in = refs[:n], refs[n:n + n_in]
        dst_refs, side_out = refs[n + n_in:2 * n + n_in], refs[2 * n + n_in:2 * n + n_in + n_out]
        ici_send, ici_recv, hop_send, hop_recv, d2d_send, d2d_recv = refs[2 * n + n_in + n_out:]
        x, y, c = _position()
        b = 2 * x + y
        via = 2 - c
        out = 3 - via
        for k in range(n):
            dst_refs[k][b] = src_refs[k][...].astype(BF16)

        def rows(k, core, j):
            half = blocks[k].shape[0] // 2
            return pl.ds(pl.multiple_of(core * half + j * (half // ch), half // ch), half // ch)

        def chip(rel):
            return x ^ (rel >> 1), y ^ (rel & 1)

        def ici(k, j, slot, rel, send_sems, recv_sems, sem):
            px, py = chip(rel)
            piece = dst_refs[k].at[slot, rows(k, c, j)]
            return pltpu.make_async_remote_copy(src_ref=piece, dst_ref=piece, send_sem=send_sems.at[sem],
                                                recv_sem=recv_sems.at[sem], device_id=(px, py, c),
                                                device_id_type=MESH)

        def direct(k, j, slot, rel):
            return ici(k, j, slot, rel, ici_send, ici_recv, ((rel - 1) * ch + j) * n + k)

        def hop(k, j, slot, rel):
            return ici(k, j, slot, rel, hop_send, hop_recv, j * n + k)

        def d2d(k, j, rel, core):
            piece = dst_refs[k].at[b ^ rel, rows(k, core, j)]
            sem = ((rel - 1) * ch + j) * n + k
            return pltpu.make_async_remote_copy(src_ref=piece, dst_ref=piece, send_sem=d2d_send.at[sem],
                                                recv_sem=d2d_recv.at[sem], device_id=(x, y, 1 - c),
                                                device_id_type=MESH)

        pieces = [(k, j) for j in range(ch) for k in range(n)]
        for k, j in pieces:
            for rel in (1, 2):
                direct(k, j, b, rel).start()
        side_work(side_in, side_out)
        for k, j in pieces:
            direct(k, j, b ^ via, via).wait_recv()
            hop(k, j, b ^ via, out).start()
            d2d(k, j, via, c).start()
        for k, j in pieces:
            direct(k, j, b ^ out, out).wait_recv()
            d2d(k, j, out, c).start()
        for k, j in pieces:
            hop(k, j, b ^ 3, via).wait_recv()
            d2d(k, j, 3, c).start()
        for k, j in pieces:
            for rel in (1, 2, 3):
                d2d(k, j, rel, 1 - c).wait_recv()
        for k, j in pieces:
            for rel in (1, 2):
                direct(k, j, b, rel).wait_send()
            hop(k, j, b ^ via, out).wait_send()
            d2d(k, j, via, c).wait_send()
            d2d(k, j, out, c).wait_send()
            d2d(k, j, 3, c).wait_send()

    vmem_spec = pl.BlockSpec(memory_space=pltpu.VMEM)
    dma = pltpu.SemaphoreType.DMA
    out_shape = tuple(jax.ShapeDtypeStruct((N_CHIP,) + a.shape, BF16) for a in blocks) + tuple(side_shapes)
    outs = pl.pallas_call(
        body, name=name, in_specs=[vmem_spec] * (n + n_in), out_specs=tuple([vmem_spec] * (n + n_out)),
        out_shape=out_shape,
        scratch_shapes=[dma((2 * ch * n,)), dma((2 * ch * n,)), dma((ch * n,)), dma((ch * n,)),
                        dma((3 * ch * n,)), dma((3 * ch * n,))],
        compiler_params=pltpu.CompilerParams(vmem_limit_bytes=VMEM_LIMIT),
    )(*blocks, *side_inputs)
    return outs[:n], outs[n:]


def _grad_reduce(order, h_t, wins, gw_out, small):
    s = h_t.shape[1]
    tk = GRAD_ROWS
    n_i = s // tk
    half = D_MODEL // 2
    o_half = OUT_ROWS // 2
    n_rel = N_CHIP - 1

    def body(order_ref, ht_ref, win_ref, gwo_ref, small_ref,
             win_out, wout_out, small_out,
             acc, mine, s1, r1, s2, r2, so1, ro1, so2, ro2, pair_in, pair_o, small_land,
             s1_send, s1_recv, s2_send, s2_recv, o1_send, o1_recv, o2_send, o2_recv,
             pair_send, pair_recv, small_send, small_recv):
        j = pl.program_id(0)
        i = pl.program_id(1)
        x, y, c = _position()
        me = 4 * x + 2 * y + c
        sibling = (x, y, 1 - c)
        my_rows = pl.ds(pl.multiple_of(c * half, half), half)
        sib_rows = pl.ds(pl.multiple_of((1 - c) * half, half), half)

        def chip_of(rel):
            return x ^ (rel >> 1), y ^ (rel & 1)

        def level1(k):
            return pltpu.make_async_remote_copy(src_ref=s1.at[k], dst_ref=r1.at[k], send_sem=s1_send.at[k],
                                                recv_sem=s1_recv.at[k], device_id=sibling, device_id_type=MESH)

        def level2(k):
            px, py = chip_of(RELATIONS[k])
            return pltpu.make_async_remote_copy(src_ref=s2.at[k], dst_ref=r2.at[k], send_sem=s2_send.at[k],
                                                recv_sem=s2_recv.at[k], device_id=(px, py, c), device_id_type=MESH)

        def out_level1(bk):
            return pltpu.make_async_remote_copy(src_ref=so1.at[bk], dst_ref=ro1.at[bk], send_sem=o1_send.at[bk],
                                                recv_sem=o1_recv.at[bk], device_id=sibling, device_id_type=MESH)

        def out_level2(k):
            px, py = chip_of(RELATIONS[k])
            return pltpu.make_async_remote_copy(src_ref=so2.at[k], dst_ref=ro2.at[k], send_sem=o2_send.at[k],
                                                recv_sem=o2_recv.at[k], device_id=(px, py, c), device_id_type=MESH)

        def small_copy(d):
            px, py, pc = x ^ (d >> 2), y ^ ((d >> 1) & 1), c ^ (d & 1)
            return pltpu.make_async_remote_copy(src_ref=small_ref, dst_ref=small_land.at[me],
                                                send_sem=small_send.at[d], recv_sem=small_recv.at[d],
                                                device_id=(px, py, pc), device_id_type=MESH)

        def pair_copy(k, buf):
            return pltpu.make_async_remote_copy(src_ref=buf.at[0], dst_ref=buf.at[1], send_sem=pair_send.at[k],
                                                recv_sem=pair_recv.at[k], device_id=sibling, device_id_type=MESH)

        def out_rows(bk, core):
            return pl.ds(pl.multiple_of(bk * OUT_ROWS + core * o_half, o_half), o_half)

        @pl.when((j == 0) & (i == 0))
        def _():
            for d in range(1, N_DEV):
                small_copy(d).start()
            small_land[me] = small_ref[...]
            for bk in range(N_CHIP):
                so1[bk] = gwo_ref[out_rows(bk, 1 - c), :].astype(BF16)
                out_level1(bk).start()

        @pl.when((j == 0) & (i == 1))
        def _():
            b = 2 * x + y
            for bk in range(N_CHIP):
                out_level1(bk).wait_recv()
            for k in range(n_rel):
                px, py = chip_of(RELATIONS[k])
                bk = 2 * px + py
                so2[k] = (gwo_ref[out_rows(bk, c), :] + ro1[bk].astype(F32)).astype(BF16)
                out_level2(k).start()

        @pl.when(i == 0)
        def _():
            acc[...] = jnp.zeros_like(acc)

        for n0 in range(0, WIN, ACC_COLS):
            n1 = min(n0 + ACC_COLS, WIN)
            acc[:, n0:n1] += _dot(ht_ref[...], win_ref[:, n0:n1])

        for k in range(N_CHIP):
            @pl.when((j == k) & (i == n_i - 1))
            def _(k=k):
                s1[k] = acc[sib_rows, :].astype(BF16)
                level1(k).start()
                mine[...] = acc[my_rows, :]

            if k < n_rel:
                @pl.when((j == k + 1) & (i == 1))
                def _(k=k):
                    level1(k).wait_recv()
                    s2[k] = (mine[...] + r1[k].astype(F32)).astype(BF16)
                    level2(k).start()

        @pl.when((j == N_CHIP - 1) & (i == n_i - 1))
        def _():
            b = 2 * x + y
            level1(N_CHIP - 1).wait_recv()
            total = mine[...] + r1[N_CHIP - 1].astype(F32)
            for k in range(n_rel):
                level2(k).wait_recv()
                total = total + r2[k].astype(F32)
            total = total.T
            pair_in[0] = total
            pair_copy(0, pair_in).start()
            total_o = gwo_ref[out_rows(b, c), :] + ro1[b].astype(F32)
            for k in range(n_rel):
                out_level2(k).wait_recv()
                total_o = total_o + ro2[k].astype(F32)
            pair_o[0] = total_o
            pair_copy(1, pair_o).start()
            for core in range(2):
                @pl.when(c == core)
                def _(core=core):
                    win_out[:, core * half:(core + 1) * half] = total
            wout_out[c] = total_o
            for d in range(1, N_DEV):
                small_copy(d).wait_recv()
            small_out[...] = small_land[...]
            pair_copy(0, pair_in).wait_recv()
            for core in range(2):
                @pl.when(c == core)
                def _(core=core):
                    win_out[:, (1 - core) * half:(2 - core) * half] = pair_in[1]
            pair_copy(1, pair_o).wait_recv()
            wout_out[1 - c] = pair_o[1]
            for d in range(1, N_DEV):
                small_copy(d).wait_send()
            for k in range(N_CHIP):
                level1(k).wait_send()
                out_level1(k).wait_send()
            for k in range(n_rel):
                level2(k).wait_send()
                out_level2(k).wait_send()
            pair_copy(0, pair_in).wait_send()
            pair_copy(1, pair_o).wait_send()

    vmem = pl.BlockSpec(memory_space=pltpu.VMEM)
    dma = pltpu.SemaphoreType.DMA
    sds = jax.ShapeDtypeStruct
    grid_spec = pltpu.PrefetchScalarGridSpec(
        num_scalar_prefetch=1, grid=(N_CHIP, n_i),
        in_specs=[pl.BlockSpec((D_MODEL, tk), lambda j, i, order: (0, i)),
                  pl.BlockSpec((None, tk, WIN), lambda j, i, order: (order[j], i, 0)), vmem, vmem],
        out_specs=(vmem, vmem, vmem),
        scratch_shapes=[
            pltpu.VMEM((D_MODEL, WIN), F32), pltpu.VMEM((half, WIN), F32),
            pltpu.VMEM((N_CHIP, half, WIN), BF16), pltpu.VMEM((N_CHIP, half, WIN), BF16),
            pltpu.VMEM((n_rel, half, WIN), BF16), pltpu.VMEM((n_rel, half, WIN), BF16),
            pltpu.VMEM((N_CHIP, o_half, D_MODEL), BF16), pltpu.VMEM((N_CHIP, o_half, D_MODEL), BF16),
            pltpu.VMEM((n_rel, o_half, D_MODEL), BF16), pltpu.VMEM((n_rel, o_half, D_MODEL), BF16),
            pltpu.VMEM((2, WIN, half), F32), pltpu.VMEM((2, o_half, D_MODEL), F32),
            pltpu.VMEM((N_DEV, PACK_ROWS, D_MODEL), F32),
            dma((N_CHIP,)), dma((N_CHIP,)), dma((n_rel,)), dma((n_rel,)),
            dma((N_CHIP,)), dma((N_CHIP,)), dma((n_rel,)), dma((n_rel,)),
            dma((2,)), dma((2,)), dma((N_DEV,)), dma((N_DEV,))])
    return pl.pallas_call(
        body, name="grad_w_in_reduce", grid_spec=grid_spec,
        out_shape=(sds((WIN, D_MODEL), F32), sds((2, o_half, D_MODEL), F32), sds((N_DEV, PACK_ROWS, D_MODEL), F32)),
        compiler_params=pltpu.CompilerParams(dimension_semantics=("arbitrary", "arbitrary"),
                                             vmem_limit_bytes=VMEM_LIMIT),
    )(order, h_t, wins, gw_out, small)


ADAM_STEPS = 4


def _adamw_math(w, g, m, v):
    m = ADAM_B1 * m + (1.0 - ADAM_B1) * g
    v = ADAM_B2 * v + (1.0 - ADAM_B2) * (g * g)
    m_hat = m / (1.0 - ADAM_B1 ** ADAM_STEP)
    v_hat = v / (1.0 - ADAM_B2 ** ADAM_STEP)
    delta = -ADAM_LR * (m_hat / (jnp.sqrt(v_hat) + ADAM_EPS) + ADAM_WD * w)
    return delta, m, v


def _adamw(w, g, m, v, name):
    r, c = w.shape

    def body(w_ref, g_ref, m_ref, v_ref, d_ref, nm_ref, nv_ref):
        delta, nm, nv = _adamw_math(w_ref[...], g_ref[...], m_ref[...], v_ref[...])
        d_ref[...] = delta
        nm_ref[...] = nm
        nv_ref[...] = nv

    rows = r // ADAM_STEPS
    assert rows * ADAM_STEPS == r and rows % 8 == 0
    spec = pl.BlockSpec((rows, c), lambda i: (i, 0))
    shape = jax.ShapeDtypeStruct((r, c), F32)
    return pl.pallas_call(
        body, name=name, grid=(ADAM_STEPS,), in_specs=[spec] * 4, out_specs=(spec,) * 3,
        out_shape=(shape,) * 3, compiler_params=pltpu.CompilerParams(vmem_limit_bytes=VMEM_LIMIT),
    )(w, g, m, v)


def _adamw_window(w, window, shift, m, v, name):
    r, c = w.shape
    rows = r // ADAM_STEPS
    assert rows * ADAM_STEPS == r and rows % 8 == 0

    def body(shift_ref, w_ref, win_hbm, m_ref, v_ref, g_ref, d_ref, nm_ref, nv_ref, g_vmem):
        start = pl.multiple_of(shift_ref[0] + pl.program_id(0) * rows, 8)
        pltpu.sync_copy(win_hbm.at[pl.ds(start, rows)], g_vmem)
        g = g_vmem[...]
        g_ref[...] = g
        delta, nm, nv = _adamw_math(w_ref[...], g, m_ref[...], v_ref[...])
        d_ref[...] = delta
        nm_ref[...] = nm
        nv_ref[...] = nv

    spec = pl.BlockSpec((rows, c), lambda i, shift_ref: (i, 0))
    shape = jax.ShapeDtypeStruct((r, c), F32)
    grid_spec = pltpu.PrefetchScalarGridSpec(
        num_scalar_prefetch=1, grid=(ADAM_STEPS,),
        in_specs=[spec, pl.BlockSpec(memory_space=pl.ANY), spec, spec], out_specs=(spec,) * 4,
        scratch_shapes=[pltpu.VMEM((rows, c), F32)])
    return pl.pallas_call(
        body, name=name, grid_spec=grid_spec, out_shape=(shape,) * 4,
        compiler_params=pltpu.CompilerParams(vmem_limit_bytes=VMEM_LIMIT),
    )(shift, w, window, m, v)


PACK_ROWS = 8


def _fold_heads(v):
    y = v[:, 0:LANES]
    for j in range(1, v.shape[1] // LANES):
        y = y + v[:, j * LANES:(j + 1) * LANES]
    return y + pltpu.roll(y, HEAD_DIM, 1)


N_SMALL = 6


def _small_adamw(recv, weights, m, v):
    def body(*refs):
        r_ref = refs[0]
        w_refs, m_refs, v_refs = (refs[1 + n * N_SMALL:1 + (n + 1) * N_SMALL] for n in range(3))
        outs = refs[1 + 3 * N_SMALL:]
        g_refs, d_refs, nm_refs, nv_refs = (outs[n * N_SMALL:(n + 1) * N_SMALL] for n in range(4))
        loss_ref = outs[4 * N_SMALL]
        tot = r_ref[0]
        for j in range(1, N_DEV):
            tot = tot + r_ref[j]
        loss_ref[...] = tot[3:4, 0:LANES]
        row1 = tot[1:2, :]
        row2 = tot[2:3, :]
        grads = [tot[0:1, :],
                 _fold_heads(row1[:, 0:WIDTH])[:, :HEAD_DIM],
                 _fold_heads(row2[:, WIDTH:WIDTH + A_KV_WIDTH])[:, :HEAD_DIM],
                 row2[:, WIDTH + A_KV_WIDTH:WIDTH + A_KV_WIDTH + HEADS],
                 _fold_heads(row1[:, WIDTH:2 * WIDTH])[:, :HEAD_DIM],
                 _fold_heads(row2[:, 0:WIDTH])[:, :HEAD_DIM]]
        for n, g in enumerate(grads):
            g_refs[n][...] = g
            delta, nm, nv = _adamw_math(w_refs[n][...], g, m_refs[n][...], v_refs[n][...])
            d_refs[n][...] = delta
            nm_refs[n][...] = nm
            nv_refs[n][...] = nv

    shapes = tuple(jax.ShapeDtypeStruct(a.shape, F32) for a in weights)
    outs = pl.pallas_call(body, name="small_adamw", out_shape=shapes * 4 + (jax.ShapeDtypeStruct((1, LANES), F32),)
                          )(recv, *weights, *m, *v)
    return tuple(outs[n * N_SMALL:(n + 1) * N_SMALL] for n in range(4)) + (outs[4 * N_SMALL],)


def kernel(x, norm_gain, w_in, q_norm_a, k_norm_a, sinks_a, q_norm_b, k_norm_b, w_out, loss_target, m_norm_gain, m_w_in, m_q_norm_a, m_k_norm_a, m_sinks_a, m_q_norm_b, m_k_norm_b, m_w_out, v_norm_gain, v_w_in, v_q_norm_a, v_k_norm_a, v_sinks_a, v_q_norm_b, v_k_norm_b, v_w_out):
    chip = 2 * lax.axis_index("x") + lax.axis_index("y")

    w_in_t, m_w_in_t, v_w_in_t = w_in[0].T, m_w_in[0].T, v_w_in[0].T

    s = x.shape[1]
    side_inputs = _rope_angles(s) + (_gain_rows(q_norm_a, k_norm_a, q_norm_b, k_norm_b, sinks_a),)
    (w_in_all,), tables = _gather_weights([w_in_t], "gather_weights", side_inputs, _table_shapes(s), _tables)
    w_in_bf = w_in_all.reshape(IN_WIDTH, D_MODEL)

    loss_part, gx, h_t, wins, gw_out, (dgain, dgqa, dgka, dsink, dgqb, dgkb) = _local_step(
        x[0], loss_target[0], norm_gain, w_in_bf, q_norm_a, k_norm_a, sinks_a, q_norm_b, k_norm_b, w_out[0], tables)

    small = jnp.concatenate([
        dgain, jnp.concatenate([dgqa, dgqb], axis=1),
        jnp.concatenate([dgkb, dgka, dsink, jnp.zeros((1, D_MODEL - WIDTH - 2 * A_KV_WIDTH), F32)], axis=1),
        jnp.pad(loss_part, ((0, 0), (0, D_MODEL - LANES))),
        jnp.zeros((PACK_ROWS - 4, D_MODEL), F32)], axis=0)
    order = (chip ^ jnp.array(RELATIONS, jnp.int32)).astype(jnp.int32)
    win_sum, wout_sum, small_recv = _grad_reduce(order, h_t, wins, gw_out, small)
    shift = jnp.array(WIN_SHIFT, jnp.int32)[chip].reshape(1)
    g_w_out = wout_sum.reshape(OUT_ROWS, D_MODEL)

    g_w_in, d_w_in, nm_w_in, nv_w_in = (
        a.T for a in _adamw_window(w_in_t, win_sum, shift, m_w_in_t, v_w_in_t, "adamw_w_in"))
    d_w_out, nm_w_out, nv_w_out = _adamw(w_out[0], g_w_out, m_w_out[0], v_w_out[0], "adamw_w_out")
    g_s, d_s, nm_s, nv_s, loss_row = _small_adamw(
        small_recv,
        (norm_gain, q_norm_a, k_norm_a, sinks_a, q_norm_b, k_norm_b),
        (m_norm_gain, m_q_norm_a, m_k_norm_a, m_sinks_a, m_q_norm_b, m_k_norm_b),
        (v_norm_gain, v_q_norm_a, v_k_norm_a, v_sinks_a, v_q_norm_b, v_k_norm_b))
    loss = loss_row[0, 0]

    def leaves(small_ones, big_in, big_out):
        return (small_ones[0], big_in[None]) + tuple(small_ones[1:]) + (big_out[None],)

    return ((loss, gx[None]) + leaves(g_s, g_w_in, g_w_out) + leaves(d_s, d_w_in, d_w_out)
            + leaves(nm_s, nm_w_in, nm_w_out) + leaves(nv_s, nv_w_in, nv_w_out))
```

```python
import numpy as np
import jax
import jax.numpy as jnp
from jax import lax
from jax.experimental import pallas as pl
from jax.experimental.pallas import tpu as pltpu

F32 = jnp.float32
BF16 = jnp.bfloat16

D_MODEL = 1024
HEAD_DIM = 64
HEADS = 8
WIDTH = HEADS * HEAD_DIM
A_KV_WIDTH = 2 * HEAD_DIM
BLOCK = 128
LANES = 128
FOLD = 16
A_MAX_DIST = 127
B_MAX_DIST = 128
ROPE_THETA = 10000.0
ROPE_SPLIT = 64
EPS = 1e-6
NEG = -1e30
SCALE = HEAD_DIM ** -0.5

IN_WIDTH = 3328
C_QA, C_KA, C_VA, C_GA, C_QB, C_KB, C_VB, C_GB, C_END = 0, 512, 640, 768, 1280, 1792, 2304, 2816, 3328

N_DEV = 8
N_CHIP = 4
MESH = pl.DeviceIdType.MESH
WIN = 896
WIN_START = (0, 768, 1664, 2432)
WIN_SHIFT = (0, 64, 0, 64)
OUT_ROWS = D_MODEL // N_CHIP
RELATIONS = (3, 1, 2, 0)

ADAM_LR = 0.001
ADAM_B1 = 0.9
ADAM_B2 = 0.999
ADAM_EPS = 1e-08
ADAM_WD = 0.01
ADAM_STEP = 10

ROW_TILE = 256
FOLD_ROWS = ROW_TILE // FOLD
GRAD_ROWS = 1024
ACC_COLS = 256
VMEM_LIMIT = 56 * 1024 * 1024


def _dot(a, b):
    return jnp.dot(a, b, preferred_element_type=F32)


def _dot_nt(a, b):
    return lax.dot_general(a, b, (((1,), (1,)), ((), ())), preferred_element_type=F32)


def _dot_tn(a, b):
    return lax.dot_general(a, b, (((0,), (0,)), ((), ())), preferred_element_type=F32)


def _head_sum(z, bd):
    w = bd.shape[0]
    zb = z.astype(BF16)
    parts = [_dot(zb[:, a:a + w], bd) for a in range(0, z.shape[1], w)]
    return parts[0] if len(parts) == 1 else jnp.concatenate(parts, axis=1)


def _swap_halves(t):
    w = t.shape[1]
    lane = lax.broadcasted_iota(jnp.int32, t.shape, 1)
    return jnp.where(lane % HEAD_DIM < HEAD_DIM // 2, pltpu.roll(t, w - 32, 1), pltpu.roll(t, 32, 1))


def _qknorm_rope(t, g, cos, sin_s, bd):
    r = lax.rsqrt(_head_sum(t * t, bd) * (1.0 / HEAD_DIM) + EPS)
    n = (t * r) * g
    return n * cos + _swap_halves(n) * sin_s


def _qknorm_rope_bwd(dout, t, g, cos, sin_s, bd):
    dout, t = dout.astype(F32), t.astype(F32)
    dn = dout * cos + _swap_halves(dout * sin_s)
    r = lax.rsqrt(_head_sum(t * t, bd) * (1.0 / HEAD_DIM) + EPS)
    tr = t * r
    u = dn * g
    dt = r * (u - tr * (_head_sum(u * tr, bd) * (1.0 / HEAD_DIM)))
    return dt, dn * tr


def _sigmoid(g):
    return 1.0 / (1.0 + jnp.exp(-g))


def _expand_heads(st):
    t = st.shape[0]
    lane = lax.broadcasted_iota(jnp.int32, (t, LANES), 1)
    chunks = []
    for c in range(WIDTH // LANES):
        chunks.append(jnp.where(lane < HEAD_DIM, st[:, 2 * c:2 * c + 1], st[:, 2 * c + 1:2 * c + 2]))
    return jnp.concatenate(chunks, axis=1)


def _reduce_heads(z):
    t = z.shape[0]
    lane = lax.broadcasted_iota(jnp.int32, (t, LANES), 1)
    out = jnp.zeros((t, LANES), F32)
    for c in range(WIDTH // LANES):
        zc = z[:, c * LANES:(c + 1) * LANES]
        for ph in range(2):
            s = jnp.sum(jnp.where((lane // HEAD_DIM) == ph, zc, 0.0), axis=-1, keepdims=True)
            out = jnp.where(lane == 2 * c + ph, s, out)
    return out


def _fold_scratch(w):
    return pltpu.VMEM((w // LANES, ROW_TILE, LANES), F32)


def _store_folded(out_ref, val, scr, col0=0):
    w = val.shape[1]
    n = w // LANES
    for c in range(n):
        scr[c] = val[:, c * LANES:(c + 1) * LANES]
    for r in range(FOLD):
        piece = [scr[c, pl.ds(r, FOLD_ROWS, stride=FOLD), :] for c in range(n)]
        out_ref[r, :, col0:col0 + w] = (piece[0] if n == 1 else jnp.concatenate(piece, axis=1)).astype(out_ref.dtype)


def _load_folded(in_ref, scr):
    n = in_ref.shape[2] // LANES
    for r in range(FOLD):
        blk = in_ref[r].astype(F32)
        for c in range(n):
            scr[c, pl.ds(r, FOLD_ROWS, stride=FOLD), :] = blk[:, c * LANES:(c + 1) * LANES]
    return scr[0] if n == 1 else jnp.concatenate([scr[c] for c in range(n)], axis=1)


def _store_folded_bf16(out_ref, val, perm):
    folded = _dot(perm, val.astype(BF16)).astype(out_ref.dtype)
    for r in range(FOLD):
        out_ref[r] = folded[r * FOLD_ROWS:(r + 1) * FOLD_ROWS]


def _load_folded_bf16(in_ref, perm):
    blk = jnp.concatenate([in_ref[r] for r in range(FOLD)], axis=0)
    return _dot(perm, blk)


def _rows(w, tm=ROW_TILE):
    return pl.BlockSpec((tm, w), lambda i: (i, 0))


def _folded_rows(w):
    return pl.BlockSpec((FOLD, FOLD_ROWS, w), lambda i: (0, i, 0))


def _whole(shape):
    return pl.BlockSpec(shape, lambda i: (0,) * len(shape))


def _inproj(x2, gain, w_bf, cos, sin_s, gqa, gka, gqb, gkb, bd256, bd128, w_out_blk):
    s = x2.shape[0]
    tm = ROW_TILE
    n_steps = s // tm
    n_rel = N_CHIP - 1
    o_half = OUT_ROWS // 2

    def body(x_ref, gain_ref, w_hbm, cos_ref, sin_ref, gqa_ref, gka_ref, gqb_ref, gkb_ref, bd256_ref, bd128_ref,
             wout_ref, qa_ref, kva_ref, qb_ref, kvb_ref, qbf_ref, kvbf_ref,
             qa_raw_ref, ka_raw_ref, ga_ref, qb_raw_ref, kb_raw_ref, gb_ref, wout_all_ref,
             w_vmem, scr, land, ici_send, ici_recv, d2d_send, d2d_recv):
        i = pl.program_id(0)
        px_, py_, c = _position()
        b = 2 * px_ + py_

        def piece(chip_idx, core):
            return land.at[chip_idx, pl.ds(pl.multiple_of(core * o_half, o_half), o_half)]

        def other_chip(d):
            ox, oy = px_ ^ (d >> 1), py_ ^ (d & 1)
            return ox, oy, 2 * ox + oy

        def ici_copy(d, chip_idx):
            ox, oy, _ = other_chip(d)
            return pltpu.make_async_remote_copy(
                src_ref=piece(chip_idx, c), dst_ref=piece(chip_idx, c), send_sem=ici_send.at[d - 1],
                recv_sem=ici_recv.at[d - 1], device_id=(ox, oy, c), device_id_type=MESH)

        def d2d_copy(d, core):
            return pltpu.make_async_remote_copy(
                src_ref=piece(other_chip(d)[2], core), dst_ref=piece(other_chip(d)[2], core),
                send_sem=d2d_send.at[d - 1], recv_sem=d2d_recv.at[d - 1], device_id=(px_, py_, 1 - c),
                device_id_type=MESH)

        @pl.when(i == 0)
        def _():
            pltpu.sync_copy(w_hbm, w_vmem)
            land[b] = wout_ref[...].astype(BF16)
            for d in range(1, N_CHIP):
                ici_copy(d, b).start()

        @pl.when(i == n_steps // 2)
        def _():
            for d in range(1, N_CHIP):
                ici_copy(d, other_chip(d)[2]).wait_recv()
                d2d_copy(d, c).start()

        @pl.when(i == n_steps - 1)
        def _():
            for d in range(1, N_CHIP):
                d2d_copy(d, 1 - c).wait_recv()
            for d in range(1, N_CHIP):
                ici_copy(d, b).wait_send()
                d2d_copy(d, c).wait_send()
            for k in range(N_CHIP):
                wout_all_ref[k * OUT_ROWS:(k + 1) * OUT_ROWS, :] = land[k]

        xt = x_ref[...]
        r = lax.rsqrt(jnp.mean(xt * xt, axis=-1, keepdims=True) + EPS)
        h = ((xt * r) * gain_ref[...]).astype(BF16)
        cos1 = cos_ref[...]
        sin1 = sin_ref[...]
        cos4 = jnp.tile(cos1, (1, 4))
        sin4 = jnp.tile(sin1, (1, 4))

        def seg(a, b):
            return _dot_nt(h, w_vmem[a:b, :])

        t = seg(C_QA, C_KA)
        qa_raw_ref[...] = t.astype(BF16)
        qa_ref[...] = (_qknorm_rope(t, gqa_ref[...], cos4, sin4, bd256_ref[...]) * SCALE).astype(BF16)
        t = seg(C_KA, C_VA)
        ka_raw_ref[...] = t.astype(BF16)
        kva_ref[:, :A_KV_WIDTH] = _qknorm_rope(t, gka_ref[...], cos1, sin1, bd128_ref[...]).astype(BF16)
        kva_ref[:, A_KV_WIDTH:] = seg(C_VA, C_GA).astype(BF16)
        ga_ref[...] = seg(C_GA, C_QB).astype(BF16)
        t = seg(C_QB, C_KB)
        qb_raw_ref[...] = t.astype(BF16)
        t = _qknorm_rope(t, gqb_ref[...], cos4, sin4, bd256_ref[...]) * SCALE
        qb_ref[...] = t.astype(BF16)
        _store_folded(qbf_ref, t, scr)
        t = seg(C_KB, C_VB)
        kb_raw_ref[...] = t.astype(BF16)
        t = _qknorm_rope(t, gkb_ref[...], cos4, sin4, bd256_ref[...])
        kvb_ref[:, :WIDTH] = t.astype(BF16)
        _store_folded(kvbf_ref, t, scr)
        t = seg(C_VB, C_GB)
        kvb_ref[:, WIDTH:] = t.astype(BF16)
        _store_folded(kvbf_ref, t, scr, WIDTH)
        gb_ref[...] = seg(C_GB, C_END).astype(BF16)

    sds = jax.ShapeDtypeStruct
    ln = s // FOLD
    out_shape = (sds((s, WIDTH), BF16), sds((s, 2 * A_KV_WIDTH), BF16), sds((s, WIDTH), BF16),
                 sds((s, 2 * WIDTH), BF16), sds((FOLD, ln, WIDTH), BF16), sds((FOLD, ln, 2 * WIDTH), BF16),
                 sds((s, WIDTH), BF16), sds((s, A_KV_WIDTH), BF16), sds((s, WIDTH), BF16),
                 sds((s, WIDTH), BF16), sds((s, WIDTH), BF16), sds((s, WIDTH), BF16),
                 sds((D_MODEL, D_MODEL), BF16))
    out_specs = (_rows(WIDTH), _rows(2 * A_KV_WIDTH), _rows(WIDTH), _rows(2 * WIDTH),
                 _folded_rows(WIDTH), _folded_rows(2 * WIDTH),
                 _rows(WIDTH), _rows(A_KV_WIDTH), _rows(WIDTH), _rows(WIDTH), _rows(WIDTH), _rows(WIDTH),
                 _whole((D_MODEL, D_MODEL)))
    dma = pltpu.SemaphoreType.DMA
    return pl.pallas_call(
        body, name="inproj_fwd", grid=(n_steps,),
        in_specs=[_rows(D_MODEL), _whole(gain.shape), pl.BlockSpec(memory_space=pl.ANY), _rows(LANES), _rows(LANES),
                  _whole(gqa.shape), _whole(gka.shape), _whole(gqb.shape), _whole(gkb.shape), _whole(bd256.shape),
                  _whole(bd128.shape), _whole(w_out_blk.shape)],
        out_specs=out_specs, out_shape=out_shape,
        scratch_shapes=[pltpu.VMEM((IN_WIDTH, D_MODEL), BF16), _fold_scratch(WIDTH),
                        pltpu.VMEM((N_CHIP, OUT_ROWS, D_MODEL), BF16),
                        dma((n_rel,)), dma((n_rel,)), dma((n_rel,)), dma((n_rel,))],
        compiler_params=pltpu.CompilerParams(dimension_semantics=("arbitrary",), vmem_limit_bytes=VMEM_LIMIT),
    )(x2, gain, w_bf, cos, sin_s, gqa, gka, gqb, gkb, bd256, bd128, w_out_blk)


def _seq_pos(idx, dil):
    if dil == 4:
        return 4 * (idx % 32) + idx // 32
    return idx


def _upper_mask(dil, r0=0, rows=2 * BLOCK):
    qi = (lax.broadcasted_iota(jnp.int32, (rows, BLOCK), 0) + r0) % BLOCK
    kj = lax.broadcasted_iota(jnp.int32, (rows, BLOCK), 1)
    return _seq_pos(kj, dil) > _seq_pos(qi, dil)


def _eye_mask(r0=0, rows=2 * BLOCK):
    qi = (lax.broadcasted_iota(jnp.int32, (rows, BLOCK), 0) + r0) % BLOCK
    kj = lax.broadcasted_iota(jnp.int32, (rows, BLOCK), 1)
    return qi == kj


def _stack_heads(a2, c, gqa):
    lane = lax.broadcasted_iota(jnp.int32, (1, LANES), 1) // HEAD_DIM
    zero = jnp.zeros_like(a2)
    if gqa:
        keep = lane == (c // 2)
        return jnp.concatenate([jnp.where(keep, a2, zero), jnp.where(keep, _swap_heads(a2), zero)], axis=0)
    return jnp.concatenate([jnp.where(lane == 0, a2, zero), jnp.where(lane == 1, a2, zero)], axis=0)


def _unstack_heads(a, c, gqa):
    lane = lax.broadcasted_iota(jnp.int32, (1, LANES), 1) // HEAD_DIM
    if gqa:
        return jnp.where(lane == (c // 2), a[:BLOCK], _swap_heads(a[BLOCK:]))
    return jnp.where(lane == 0, a[:BLOCK], a[BLOCK:])


def _stacked_head_ids(c, gqa):
    if gqa:
        return 2 * c + c // 2, 2 * c + 1 - c // 2
    return 2 * c, 2 * c + 1


def _per_head_rows(blk, heads):
    return jnp.concatenate([blk[:, heads[0]:heads[0] + 1], blk[:, heads[1]:heads[1] + 1]], axis=0)


def _attn_view(a, dil):
    if dil == 1:
        return a[None]
    if dil == 4:
        return a.reshape(4, 4, a.shape[1], a.shape[2])
    return a


def _attn_unview(a, dil):
    if dil == 1:
        return a[0]
    if dil == 4:
        return a.reshape(FOLD, a.shape[2], a.shape[3])
    return a


ATTN_BLOCKS_PER_STEP = 8


def _attn_specs(dil):
    if dil == 4:
        def spec(n, fn):
            return lambda w: pl.BlockSpec((4, None, n * BLOCK // 4, w), lambda r, i: (0, r, fn(i), 0))
    else:
        def spec(n, fn):
            return lambda w: pl.BlockSpec((None, n * BLOCK, w), lambda r, i: (r, fn(i), 0))
    return spec


def _blk_rows(g, dil):
    n = BLOCK // 4 if dil == 4 else BLOCK
    if isinstance(g, int):
        return slice(g * n, (g + 1) * n)
    return pl.ds(pl.multiple_of(g * n, n), n)


def _blk_load(ref, sl, dil, g=0):
    if dil == 4:
        return ref[:, _blk_rows(g, dil), sl].reshape(BLOCK, sl.stop - sl.start)
    return ref[_blk_rows(g, dil), sl]


def _blk_store(ref, sl, val, dil, g=0):
    val = val.astype(ref.dtype)
    if dil == 4:
        ref[:, _blk_rows(g, dil), sl] = val.reshape(4, BLOCK // 4, sl.stop - sl.start)
    else:
        ref[_blk_rows(g, dil), sl] = val


def _swap_heads(a):
    return pltpu.roll(a.astype(F32), HEAD_DIM, 1).astype(a.dtype)


STAT_SHIFT = 8


def _attn_fwd(q, kv, sinks, *, dil, max_dist, name):
    q, kv = _attn_view(q, dil), _attn_view(kv, dil)
    kw = kv.shape[-1] // 2
    gqa = kw == A_KV_WIDTH
    n_seq = dil
    nb = (q.shape[-2] * (4 if dil == 4 else 1)) // BLOCK
    per_step = min(ATTN_BLOCKS_PER_STEP, nb)
    with_sinks = sinks is not None
    all_lanes = slice(0, LANES)
    assert max_dist in (BLOCK - 1, BLOCK) and nb % per_step == 0 and (per_step == 1 or per_step % 2 == 0)
    diag = max_dist == BLOCK

    def body(*refs):
        if with_sinks:
            q_ref, kvp_ref, kvc_ref, sink_ref, o_ref, ml_ref = refs
        else:
            q_ref, kvp_ref, kvc_ref, o_ref, ml_ref = refs

        chunks = range(WIDTH // LANES)

        def matmuls_in(g, has_prev):
            prev_ref, prev_g = (kvp_ref, 0) if (isinstance(g, int) and g == 0) else (kvc_ref, g - 1)
            scores, values = [], []
            for c in chunks:
                sl = slice(c * LANES, (c + 1) * LANES)
                ksl = slice(0, LANES) if gqa else sl
                vsl = slice(ksl.start + kw, ksl.stop + kw)
                kcur, vcur = _blk_load(kvc_ref, ksl, dil, g), _blk_load(kvc_ref, vsl, dil, g)
                qs = _stack_heads(_blk_load(q_ref, sl, dil, g), c, gqa)
                if has_prev:
                    kcur = jnp.concatenate([_blk_load(prev_ref, ksl, dil, prev_g), kcur], axis=0)
                    vcur = jnp.concatenate([_blk_load(prev_ref, vsl, dil, prev_g), vcur], axis=0)
                scores.append(_dot_nt(qs, kcur))
                values.append(vcur)
            return scores, values

        def tile_ops(has_prev, scores):
            lane = lax.broadcasted_iota(jnp.int32, (1, LANES), 1)
            with_diag = diag and has_prev
            upper, eye = _upper_mask(dil), _eye_mask()
            first_rows = lax.broadcasted_iota(jnp.int32, (2 * BLOCK, 1), 0) < BLOCK
            ml_blk = jnp.zeros((BLOCK, LANES), F32)
            probs = []
            for c in chunks:
                heads = _stacked_head_ids(c, gqa)
                s = scores[c]
                if has_prev:
                    s_p = s[:, :BLOCK]
                    sc = jnp.where(upper, s_p, s[:, BLOCK:])
                else:
                    sc = jnp.where(upper, NEG, s)
                if with_diag:
                    sd = jnp.where(eye, s_p, NEG)
                    m = jnp.max(jnp.maximum(sc, sd), axis=-1, keepdims=True)
                else:
                    m = jnp.max(sc, axis=-1, keepdims=True)
                if with_sinks:
                    sk = jnp.where(first_rows, sink_ref[0, heads[0]], sink_ref[0, heads[1]])
                    m = jnp.maximum(m, sk)
                p = jnp.exp(sc - m)
                zero = jnp.zeros_like(p)
                if with_diag:
                    pd = jnp.exp(sd - m)
                    l = jnp.sum(p + pd, axis=-1, keepdims=True)
                else:
                    pd = zero
                    l = jnp.sum(p, axis=-1, keepdims=True)
                if with_sinks:
                    l = l + jnp.exp(sk - m)
                pf = jnp.where(upper, zero, p)
                if has_prev:
                    pf = jnp.concatenate([jnp.where(upper, p, pd), pf], axis=1)
                probs.append(pf.astype(BF16))
                for n, h in enumerate(heads):
                    rows = slice(n * BLOCK, (n + 1) * BLOCK)
                    ml_blk = jnp.where(lane == h, m[rows], ml_blk)
                    ml_blk = jnp.where(lane == h + STAT_SHIFT, l[rows], ml_blk)
            return probs, ml_blk

        def matmuls_out(g, values, probs, ml_blk):
            for c in chunks:
                sl = slice(c * LANES, (c + 1) * LANES)
                _blk_store(o_ref, sl, _unstack_heads(_dot(probs[c], values[c]), c, gqa), dil, g)
            _blk_store(ml_ref, all_lanes, ml_blk, dil, g)

        def run(blocks):
            ins = [matmuls_in(g, has_prev) for g, has_prev in blocks]
            mids = [tile_ops(has_prev, scores) for (_, has_prev), (scores, _) in zip(blocks, ins)]
            for (g, _), (_, values), (probs, ml_blk) in zip(blocks, ins, mids):
                matmuls_out(g, values, probs, ml_blk)

        second = [(1, True)] if per_step > 1 else []

        @pl.when(pl.program_id(1) == 0)
        def _():
            run([(0, False)] + second)

        @pl.when(pl.program_id(1) > 0)
        def _():
            run([(0, True)] + second)

        if per_step > 2:
            def rest(pair, carry):
                run([(2 * pair, True), (2 * pair + 1, True)])
                return carry

            lax.fori_loop(1, per_step // 2, rest, 0)

    spec = _attn_specs(dil)
    cur = spec(per_step, lambda i: i)
    prev = spec(1, lambda i: jnp.maximum(i * per_step - 1, 0))
    in_specs = [cur(WIDTH), prev(2 * kw), cur(2 * kw)]
    args = [q, kv, kv]
    if with_sinks:
        in_specs.append(pl.BlockSpec(memory_space=pltpu.SMEM))
        args.append(sinks)
    stats = jax.ShapeDtypeStruct(q.shape[:-1] + (LANES,), F32)
    o, ml = pl.pallas_call(
        body, name=name, grid=(n_seq, nb // per_step), in_specs=in_specs,
        out_specs=(cur(WIDTH), cur(LANES)),
        out_shape=(jax.ShapeDtypeStruct(q.shape, BF16), stats),
        compiler_params=pltpu.CompilerParams(dimension_semantics=("arbitrary", "arbitrary"),
                                             vmem_limit_bytes=VMEM_LIMIT),
    )(*args)
    return _attn_unview(o, dil), _attn_unview(ml, dil)


def _attn_bwd(q, kv, do, ld, *, dil, max_dist, name, onto=None):
    q, kv, do, ld = (_attn_view(a, dil) for a in (q, kv, do, ld))
    onto = () if onto is None else tuple(_attn_view(a, dil) for a in onto)
    kw = kv.shape[-1] // 2
    gqa = kw == A_KV_WIDTH
    n_seq = dil
    nb = (q.shape[-2] * (4 if dil == 4 else 1)) // BLOCK
    n_kc = kw // LANES
    per_step = min(ATTN_BLOCKS_PER_STEP, nb)
    all_lanes = slice(0, LANES)
    assert max_dist in (BLOCK - 1, BLOCK) and nb % per_step == 0 and (per_step == 1 or per_step % 2 == 0)
    diag = max_dist == BLOCK

    def body(q_ref, kvp_ref, kvc_ref, do_ref, ld_ref, *rest_refs):
        dq_ref, dkv_ref, ck_ref, cv_ref = rest_refs[len(onto):]
        i = pl.program_id(1)

        def store(ref, sl, val, blk):
            if onto:
                val = val + _blk_load(rest_refs[0 if ref is dq_ref else 1], sl, dil, blk).astype(F32)
            _blk_store(ref, sl, val, dil, blk)

        chunks = range(WIDTH // LANES)

        def matmuls_in(g, has_prev):
            prev_ref, prev_g = (kvp_ref, 0) if (isinstance(g, int) and g == 0) else (kvc_ref, g - 1)
            operands, products = [], []
            for c in chunks:
                sl = slice(c * LANES, (c + 1) * LANES)
                kc = 0 if gqa else c
                ksl = slice(kc * LANES, (kc + 1) * LANES)
                vsl = slice(ksl.start + kw, ksl.stop + kw)
                k2, v2 = _blk_load(kvc_ref, ksl, dil, g), _blk_load(kvc_ref, vsl, dil, g)
                if has_prev:
                    k2 = jnp.concatenate([_blk_load(prev_ref, ksl, dil, prev_g), k2], axis=0)
                    v2 = jnp.concatenate([_blk_load(prev_ref, vsl, dil, prev_g), v2], axis=0)
                qs = _stack_heads(_blk_load(q_ref, sl, dil, g), c, gqa)
                dos = _stack_heads(_blk_load(do_ref, sl, dil, g), c, gqa)
                operands.append((qs, dos, k2))
                products.append((_dot_nt(qs, k2), _dot_nt(dos, v2)))
            return operands, products

        def tile_ops(g, has_prev, products):
            upper, eye = _upper_mask(dil), _eye_mask()
            ld_blk = _blk_load(ld_ref, all_lanes, dil, g)
            weights = []
            for c in chunks:
                heads = _stacked_head_ids(c, gqa)
                lse2 = _per_head_rows(ld_blk, heads)
                dl2 = _per_head_rows(ld_blk, tuple(h + STAT_SHIFT for h in heads))
                s, dp = products[c]
                if has_prev:
                    s_p, dp_p = s[:, :BLOCK], dp[:, :BLOCK]
                    sc = jnp.where(upper, s_p, s[:, BLOCK:])
                    dpc = jnp.where(upper, dp_p, dp[:, BLOCK:])
                else:
                    sc = jnp.where(upper, NEG, s)
                    dpc = dp
                p = jnp.exp(sc - lse2)
                ds = p * (dpc - dl2)
                zero = jnp.zeros_like(p)
                pf = jnp.where(upper, zero, p)
                dsf = jnp.where(upper, zero, ds)
                if has_prev:
                    if diag:
                        pd = jnp.exp(jnp.where(eye, s_p, NEG) - lse2)
                        dsd = pd * (dp_p - dl2)
                    else:
                        pd = dsd = zero
                    pf = jnp.concatenate([jnp.where(upper, p, pd), pf], axis=1)
                    dsf = jnp.concatenate([jnp.where(upper, ds, dsd), dsf], axis=1)
                weights.append((pf.astype(BF16), dsf.astype(BF16)))
            return weights

        def matmuls_out(g, has_prev, operands, weights):
            seq_blk = i * per_step + g
            dk_acc = [None] * n_kc
            dv_acc = [None] * n_kc
            for c in chunks:
                sl = slice(c * LANES, (c + 1) * LANES)
                kc = 0 if gqa else c
                qs, dos, k2 = operands[c]
                pf, dsf = weights[c]
                store(dq_ref, sl, _unstack_heads(_dot(dsf, k2), c, gqa) * SCALE, g)
                dk2 = _dot_tn(dsf, qs)
                dv2 = _dot_tn(pf, dos)
                dk_acc[kc] = dk2 if dk_acc[kc] is None else dk_acc[kc] + dk2
                dv_acc[kc] = dv2 if dv_acc[kc] is None else dv_acc[kc] + dv2
            for kc in range(n_kc):
                sl = slice(kc * LANES, (kc + 1) * LANES)
                vsl = slice(sl.start + kw, sl.stop + kw)
                if has_prev:
                    store(dkv_ref, sl, ck_ref[:, sl] + dk_acc[kc][:BLOCK], seq_blk - 1)
                    store(dkv_ref, vsl, cv_ref[:, sl] + dv_acc[kc][:BLOCK], seq_blk - 1)
                    ck_ref[:, sl] = dk_acc[kc][BLOCK:]
                    cv_ref[:, sl] = dv_acc[kc][BLOCK:]
                else:
                    ck_ref[:, sl] = dk_acc[kc]
                    cv_ref[:, sl] = dv_acc[kc]

        def run(blocks):
            ins = [matmuls_in(g, has_prev) for g, has_prev in blocks]
            mids = [tile_ops(g, has_prev, products) for (g, has_prev), (_, products) in zip(blocks, ins)]
            for (g, has_prev), (operands, _), weights in zip(blocks, ins, mids):
                matmuls_out(g, has_prev, operands, weights)

        second = [(1, True)] if per_step > 1 else []

        @pl.when(i == 0)
        def _():
            run([(0, False)] + second)

        @pl.when(i > 0)
        def _():
            run([(0, True)] + second)

        if per_step > 2:
            def rest(pair, carry):
                run([(2 * pair, True), (2 * pair + 1, True)])
                return carry

            lax.fori_loop(1, per_step // 2, rest, 0)

        @pl.when(i == nb // per_step - 1)
        def _():
            for kc in range(n_kc):
                sl = slice(kc * LANES, (kc + 1) * LANES)
                store(dkv_ref, sl, ck_ref[:, sl], nb - 1)
                store(dkv_ref, slice(sl.start + kw, sl.stop + kw), cv_ref[:, sl], nb - 1)

    spec = _attn_specs(dil)
    cur = spec(per_step, lambda i: i)
    prev = spec(1, lambda i: jnp.maximum(i * per_step - 1, 0))
    if dil == 4:
        whole = pl.BlockSpec((4, None, kv.shape[2], 2 * kw), lambda r, i: (0, r, 0, 0))
    else:
        whole = pl.BlockSpec((None, kv.shape[1], 2 * kw), lambda r, i: (r, 0, 0))
    sds = jax.ShapeDtypeStruct
    dq, dkv = pl.pallas_call(
        body, name=name, grid=(n_seq, nb // per_step),
        in_specs=[cur(WIDTH), prev(2 * kw), cur(2 * kw), cur(WIDTH), cur(LANES)] + [cur(WIDTH), whole][:len(onto)],
        out_specs=(cur(WIDTH), whole),
        out_shape=(sds(q.shape, BF16), sds(kv.shape, BF16)),
        scratch_shapes=[pltpu.VMEM((BLOCK, kw), F32), pltpu.VMEM((BLOCK, kw), F32)],
        compiler_params=pltpu.CompilerParams(dimension_semantics=("arbitrary", "arbitrary"),
                                             vmem_limit_bytes=VMEM_LIMIT),
    )(q, kv, kv, do, ld, *onto)
    return _attn_unview(dq, dil), _attn_unview(dkv, dil)


def _outproj(att_a, att_b1, att_b4, att_b16, g_a, g_b, x2, tgt2, w_out_bf, sink_row, perm):
    s = x2.shape[0]
    tm = ROW_TILE

    def body(oa_ref, mla_ref, ob1_ref, ml1_ref, ob4_ref, ml4_ref, ob16_ref, ml16_ref,
             ga_ref, gb_ref, x_ref, t_ref, w_ref, sink_ref, perm_ref,
             dy_ref, doa_ref, dob_ref, dobf_ref, dga_ref, dgb_ref, lda_ref, ldb_ref, ldbf_ref,
             gw_ref, loss_ref, dsink_ref, scr_st):
        i = pl.program_id(0)
        perm = perm_ref[...]
        lane = lax.broadcasted_iota(jnp.int32, (tm, LANES), 1)
        used = lane < HEADS

        def split(ml):
            return jnp.where(used, ml, 0.0), jnp.where(used, pltpu.roll(ml, LANES - STAT_SHIFT, 1), 1.0)

        @pl.when(i == 0)
        def _():
            gw_ref[...] = jnp.zeros_like(gw_ref)
            loss_ref[...] = jnp.zeros_like(loss_ref)
            dsink_ref[...] = jnp.zeros_like(dsink_ref)

        ms, ls = zip(split(ml1_ref[...]), split(_load_folded(ml4_ref, scr_st)), split(_load_folded(ml16_ref, scr_st)))
        mx = jnp.maximum(jnp.maximum(ms[0], ms[1]), ms[2])
        scale = [jnp.exp(mp - mx) for mp in ms]
        den = (ls[0] * scale[0] + ls[1] * scale[1]) + ls[2] * scale[2]
        lse_b = jnp.where(used, mx + jnp.log(den), 0.0)
        inv_den = 1.0 / den
        o_b = _expand_heads(scale[0] * inv_den) * ob1_ref[...].astype(F32)
        o_b = o_b + _expand_heads(scale[1] * inv_den) * _load_folded_bf16(ob4_ref, perm)
        o_b = o_b + _expand_heads(scale[2] * inv_den) * _load_folded_bf16(ob16_ref, perm)
        m_a, l_a = split(mla_ref[...])
        lse_a = jnp.where(used, m_a + jnp.log(l_a), 0.0)
        o_a = _expand_heads(1.0 / l_a) * oa_ref[...].astype(F32)
        g_a = ga_ref[...].astype(F32)
        g_b = gb_ref[...].astype(F32)
        sg_a = _sigmoid(g_a)
        sg_b = _sigmoid(g_b)
        silu_a = g_a * sg_a
        silu_b = g_b * sg_b
        mixed = jnp.concatenate([o_a * silu_a, o_b * silu_b], axis=1).astype(BF16)
        w = w_ref[...]
        y = x_ref[...] + _dot(mixed, w)
        diff = y - t_ref[...]
        loss_ref[...] += (0.5 / D_MODEL) * jnp.sum(diff * diff)
        dy = diff * (1.0 / D_MODEL)
        dy_ref[...] = dy
        dyb = dy.astype(BF16)
        gw_ref[...] += _dot_tn(mixed, dyb)
        dmixed = _dot_nt(dyb, w)
        dm_a = dmixed[:, :WIDTH]
        dm_b = dmixed[:, WIDTH:]
        do_a = dm_a * silu_a
        do_b = dm_b * silu_b
        doa_ref[...] = do_a.astype(BF16)
        dob_ref[...] = do_b.astype(BF16)
        _store_folded_bf16(dobf_ref, do_b, perm)
        dga_ref[...] = (dm_a * o_a * (sg_a * (1.0 + g_a * (1.0 - sg_a)))).astype(BF16)
        dgb_ref[...] = (dm_b * o_b * (sg_b * (1.0 + g_b * (1.0 - sg_b)))).astype(BF16)
        dl_a = _reduce_heads(do_a * o_a)
        dl_b = _reduce_heads(do_b * o_b)
        lda_ref[...] = lse_a + pltpu.roll(dl_a, STAT_SHIFT, 1)
        ld_b = lse_b + pltpu.roll(dl_b, STAT_SHIFT, 1)
        ldb_ref[...] = ld_b
        _store_folded(ldbf_ref, ld_b, scr_st)
        dsink_ref[...] -= jnp.sum(jnp.exp(sink_ref[...] - lse_a) * dl_a, axis=0, keepdims=True)

    sds = jax.ShapeDtypeStruct
    ln = s // FOLD
    natural = [_rows(WIDTH), _rows(LANES)]
    folded = [_folded_rows(WIDTH), _folded_rows(LANES)]
    return pl.pallas_call(
        body, name="outproj_fwd_bwd", grid=(s // tm,),
        in_specs=natural + natural + folded + folded
                 + [_rows(WIDTH), _rows(WIDTH), _rows(D_MODEL), _rows(D_MODEL), _whole((D_MODEL, D_MODEL)),
                    _whole((1, LANES)), _whole(perm.shape)],
        out_specs=(_rows(D_MODEL), _rows(WIDTH), _rows(WIDTH), _folded_rows(WIDTH), _rows(WIDTH), _rows(WIDTH),
                   _rows(LANES), _rows(LANES), _folded_rows(LANES),
                   _whole((D_MODEL, D_MODEL)), _whole((1, LANES)), _whole((1, LANES))),
        out_shape=(sds((s, D_MODEL), F32), sds((s, WIDTH), BF16), sds((s, WIDTH), BF16),
                   sds((FOLD, ln, WIDTH), BF16), sds((s, WIDTH), BF16), sds((s, WIDTH), BF16),
                   sds((s, LANES), F32), sds((s, LANES), F32), sds((FOLD, ln, LANES), F32),
                   sds((D_MODEL, D_MODEL), F32), sds((1, LANES), F32), sds((1, LANES), F32)),
        scratch_shapes=[_fold_scratch(LANES)],
        compiler_params=pltpu.CompilerParams(dimension_semantics=("arbitrary",), vmem_limit_bytes=VMEM_LIMIT),
    )(*att_a, *att_b1, *att_b4, *att_b16, g_a, g_b, x2, tgt2, w_out_bf, sink_row, perm)


def _inproj_bwd(x2, dy, gain, w_bf, cos, sin_s, gqa, gka, gqb, gkb, bd256, bd128, perm,
                qa_raw, ka_raw, qb_raw, kb_raw, d_a, d_b1, d_bf, dg_a, dg_b):
    s = x2.shape[0]
    tm = ROW_TILE

    def body(x_ref, dy_ref, gain_ref, w_hbm, cos_ref, sin_ref, gqa_ref, gka_ref, gqb_ref, gkb_ref, bd256_ref,
             bd128_ref, perm_ref, qa_raw_ref, ka_raw_ref, qb_raw_ref, kb_raw_ref, dqa_ref, dkva_ref,
             dq1_ref, dkv1_ref, dqf_ref, dkvf_ref, dga_ref, dgb_ref,
             gx_ref, ht_ref, win_ref,
             dgain_ref, dgqa_ref, dgka_ref, dgqb_ref, dgkb_ref, w_vmem, dproj_ref):
        i = pl.program_id(0)
        perm = perm_ref[...]

        @pl.when(i == 0)
        def _():
            pltpu.sync_copy(w_hbm, w_vmem)
            dgain_ref[...] = jnp.zeros_like(dgain_ref)
            dgqa_ref[...] = jnp.zeros_like(dgqa_ref)
            dgka_ref[...] = jnp.zeros_like(dgka_ref)
            dgqb_ref[...] = jnp.zeros_like(dgqb_ref)
            dgkb_ref[...] = jnp.zeros_like(dgkb_ref)

        cos1 = cos_ref[...]
        sin1 = sin_ref[...]
        cos4 = jnp.tile(cos1, (1, 4))
        sin4 = jnp.tile(sin1, (1, 4))

        dt, dg = _qknorm_rope_bwd(dqa_ref[...], qa_raw_ref[...], gqa_ref[...], cos4, sin4, bd256_ref[...])
        dproj_ref[:, C_QA:C_KA] = dt.astype(BF16)
        dgqa_ref[...] += jnp.sum(dg, axis=0, keepdims=True)
        dt, dg = _qknorm_rope_bwd(dkva_ref[:, :A_KV_WIDTH], ka_raw_ref[...], gka_ref[...], cos1, sin1,
                                  bd128_ref[...])
        dproj_ref[:, C_KA:C_VA] = dt.astype(BF16)
        dgka_ref[...] += jnp.sum(dg, axis=0, keepdims=True)
        dproj_ref[:, C_VA:C_GA] = dkva_ref[:, A_KV_WIDTH:]
        dproj_ref[:, C_GA:C_QB] = dga_ref[...]
        dq = dq1_ref[...].astype(F32) + _load_folded_bf16(dqf_ref, perm)
        dt, dg = _qknorm_rope_bwd(dq, qb_raw_ref[...], gqb_ref[...], cos4, sin4, bd256_ref[...])
        dproj_ref[:, C_QB:C_KB] = dt.astype(BF16)
        dgqb_ref[...] += jnp.sum(dg, axis=0, keepdims=True)
        dkv = dkv1_ref[...].astype(F32) + _load_folded_bf16(dkvf_ref, perm)
        dt, dg = _qknorm_rope_bwd(dkv[:, :WIDTH], kb_raw_ref[...], gkb_ref[...], cos4, sin4, bd256_ref[...])
        dproj_ref[:, C_KB:C_VB] = dt.astype(BF16)
        dgkb_ref[...] += jnp.sum(dg, axis=0, keepdims=True)
        dproj_ref[:, C_VB:C_GB] = dkv[:, WIDTH:].astype(BF16)
        dproj_ref[:, C_GB:C_END] = dgb_ref[...]
        for k, start in enumerate(WIN_START):
            win_ref[k] = dproj_ref[:, start:start + WIN]

        xt = x_ref[...]
        gain_row = gain_ref[...]
        r = lax.rsqrt(jnp.mean(xt * xt, axis=-1, keepdims=True) + EPS)
        xr = xt * r
        ht_ref[...] = (xr * gain_row).T.astype(BF16)
        dh = _dot(dproj_ref[...], w_vmem[...])
        dgain_ref[...] += jnp.sum(dh * xr, axis=0, keepdims=True)
        u = dh * gain_row
        gx_ref[...] = dy_ref[...] + r * (u - xr * jnp.mean(u * xr, axis=-1, keepdims=True))

    def acc_row(w):
        return pl.BlockSpec((1, w), lambda i: (0, 0))

    sds = jax.ShapeDtypeStruct
    any_spec = pl.BlockSpec(memory_space=pl.ANY)
    win_spec = pl.BlockSpec((N_CHIP, tm, WIN), lambda i: (0, i, 0))
    return pl.pallas_call(
        body, name="inproj_bwd", grid=(s // tm,),
        in_specs=[_rows(D_MODEL), _rows(D_MODEL), _whole(gain.shape), any_spec, _rows(LANES), _rows(LANES),
                  _whole(gqa.shape), _whole(gka.shape), _whole(gqb.shape), _whole(gkb.shape), _whole(bd256.shape),
                  _whole(bd128.shape), _whole(perm.shape),
                  _rows(WIDTH), _rows(A_KV_WIDTH), _rows(WIDTH), _rows(WIDTH),
                  _rows(WIDTH), _rows(2 * A_KV_WIDTH), _rows(WIDTH), _rows(2 * WIDTH)]
                 + [_folded_rows(WIDTH), _folded_rows(2 * WIDTH), _rows(WIDTH), _rows(WIDTH)],
        out_specs=(_rows(D_MODEL), pl.BlockSpec((D_MODEL, tm), lambda i: (0, i)), win_spec, acc_row(D_MODEL), acc_row(WIDTH), acc_row(A_KV_WIDTH), acc_row(WIDTH), acc_row(WIDTH)),
        out_shape=(sds((s, D_MODEL), F32), sds((D_MODEL, s), BF16), sds((N_CHIP, s, WIN), BF16),
                   sds((1, D_MODEL), F32),
                   sds((1, WIDTH), F32), sds((1, A_KV_WIDTH), F32), sds((1, WIDTH), F32), sds((1, WIDTH), F32)),
        scratch_shapes=[pltpu.VMEM((IN_WIDTH, D_MODEL), BF16), pltpu.VMEM((tm, IN_WIDTH), BF16)],
        compiler_params=pltpu.CompilerParams(dimension_semantics=("arbitrary",), vmem_limit_bytes=VMEM_LIMIT),
    )(x2, dy, gain, w_bf, cos, sin_s, gqa, gka, gqb, gkb, bd256, bd128, perm, qa_raw, ka_raw, qb_raw, kb_raw,
      *d_a, *d_b1, *d_bf, dg_a, dg_b)


def _rope_angles(s):
    half = HEAD_DIM // 2
    f32 = np.float32
    inv = np.tile((f32(ROPE_THETA) ** (-np.arange(half, dtype=f32) / f32(half))).astype(f32), 4)
    sign = np.tile(np.concatenate([-np.ones((half,), f32), np.ones((half,), f32)]), 2)
    hi = (np.arange(s // ROPE_SPLIT) * ROPE_SPLIT).astype(f32)[:, None] * inv[None, :]
    lo = np.arange(ROPE_SPLIT).astype(f32)[:, None] * inv[None, :]
    return tuple(jnp.asarray(t.astype(f32)) for t in (np.cos(hi), np.sin(hi), np.cos(lo), np.sin(lo), sign[None, :]))


def _table_shapes(s):
    return (jax.ShapeDtypeStruct((s, LANES), F32), jax.ShapeDtypeStruct((s, LANES), F32),
            jax.ShapeDtypeStruct((2 * LANES, 2 * LANES), BF16), jax.ShapeDtypeStruct((A_KV_WIDTH, A_KV_WIDTH), BF16),
            jax.ShapeDtypeStruct((ROW_TILE, ROW_TILE), BF16)) + tuple(
                jax.ShapeDtypeStruct((1, w), F32) for w in GAIN_WIDTHS)


GAIN_WIDTHS = (WIDTH, A_KV_WIDTH, WIDTH, WIDTH, LANES)


def _tables(in_refs, out_refs):
    ch_ref, sh_ref, cl_ref, sl_ref, sign_ref = in_refs[:5]
    cos_ref, sin_ref, bd256_ref, bd128_ref, perm_ref = out_refs[:5]
    for g_ref, ref in zip(in_refs[5:], out_refs[5:]):
        g = g_ref[...]
        fill = g if g.shape[1] == HEAD_DIM else jnp.zeros((1, LANES - g.shape[1]), F32)
        ref[...] = jnp.concatenate([jnp.concatenate([g, fill], axis=1)] * (ref.shape[1] // LANES), axis=1)
    cl, sl, sign = cl_ref[...], sl_ref[...], sign_ref[...]

    def tile(a, carry):
        rows = pl.ds(pl.multiple_of(a * ROPE_SPLIT, ROPE_SPLIT), ROPE_SPLIT)
        ch, sh = ch_ref[pl.ds(a, 1), :], sh_ref[pl.ds(a, 1), :]
        cos_ref[rows, :] = ch * cl - sh * sl
        sin_ref[rows, :] = (sh * cl + ch * sl) * sign
        return carry

    lax.fori_loop(0, cos_ref.shape[0] // ROPE_SPLIT, tile, 0)
    for ref in (bd256_ref, bd128_ref):
        head = [lax.broadcasted_iota(jnp.int32, ref.shape, d) // HEAD_DIM for d in (0, 1)]
        ref[...] = jnp.where(head[0] == head[1], 1.0, 0.0).astype(BF16)
    f = lax.broadcasted_iota(jnp.int32, perm_ref.shape, 0)
    col = lax.broadcasted_iota(jnp.int32, perm_ref.shape, 1)
    perm_ref[...] = jnp.where(col == FOLD * (f % FOLD_ROWS) + f // FOLD_ROWS, 1.0, 0.0).astype(BF16)


def _local_step(x2, tgt2, norm_gain, w_in_bf, q_norm_a, k_norm_a, sinks_a, q_norm_b, k_norm_b, w_out_blk, tables):
    cos, sin_s, bd256, bd128, perm, gqa, gka, gqb, gkb, sink_row = tables

    (qa, kva, qb, kvb, qbf, kvbf, qa_raw, ka_raw, g_a, qb_raw, kb_raw, g_b, w_out_bf) = _inproj(
        x2, norm_gain, w_in_bf, cos, sin_s, gqa, gka, gqb, gkb, bd256, bd128, w_out_blk)

    att_a = _attn_fwd(qa, kva, sinks_a, dil=1, max_dist=A_MAX_DIST, name="attn_a_fwd")
    att_b1 = _attn_fwd(qb, kvb, None, dil=1, max_dist=B_MAX_DIST, name="attn_b1_fwd")
    att_b4 = _attn_fwd(qbf, kvbf, None, dil=4, max_dist=B_MAX_DIST, name="attn_b4_fwd")
    att_b16 = _attn_fwd(qbf, kvbf, None, dil=16, max_dist=B_MAX_DIST, name="attn_b16_fwd")

    (dy, do_a, do_b, do_bf, dg_a, dg_b, ld_a, ld_b, ld_bf, gw_out, loss_part, dsink) = _outproj(
        att_a, att_b1, att_b4, att_b16, g_a, g_b, x2, tgt2, w_out_bf, sink_row, perm)

    d_a = _attn_bwd(qa, kva, do_a, ld_a, dil=1, max_dist=A_MAX_DIST, name="attn_a_bwd")
    d_b1 = _attn_bwd(qb, kvb, do_b, ld_b, dil=1, max_dist=B_MAX_DIST, name="attn_b1_bwd")
    d_b4 = _attn_bwd(qbf, kvbf, do_bf, ld_bf, dil=4, max_dist=B_MAX_DIST, name="attn_b4_bwd")
    d_b16 = _attn_bwd(qbf, kvbf, do_bf, ld_bf, dil=16, max_dist=B_MAX_DIST, name="attn_b16_bwd", onto=d_b4)

    gx, h_t, wins, dgain, dgqa, dgka, dgqb, dgkb = _inproj_bwd(
        x2, dy, norm_gain, w_in_bf, cos, sin_s, gqa, gka, gqb, gkb, bd256, bd128, perm,
        qa_raw, ka_raw, qb_raw, kb_raw, d_a, d_b1, d_b16, dg_a, dg_b)
    return loss_part, gx, h_t, wins, gw_out, (dgain, dgqa, dgka, dsink, dgqb, dgkb)


def _position():
    return lax.axis_index("x"), lax.axis_index("y"), lax.axis_index("c")


GATHER_CHUNKS = 2


def _gather_weights(blocks, name, side_inputs, side_shapes, side_work):
    n = len(blocks)
    n_in, n_out = len(side_inputs), len(side_shapes)
    ch = GATHER_CHUNKS

    def body(*refs):
        src_refs, side_in = refs[:n], refs[n:n + n_in]
        dst_refs, side_out = refs[n + n_in:2 * n + n_in], refs[2 * n + n_in:2 * n + n_in + n_out]
        ici_send, ici_recv, hop_send, hop_recv, d2d_send, d2d_recv = refs[2 * n + n_in + n_out:]
        x, y, c = _position()
        b = 2 * x + y
        via = 2 - c
        out = 3 - via
        for k in range(n):
            dst_refs[k][b] = src_refs[k][...].astype(BF16)

        def rows(k, core, j):
            half = blocks[k].shape[0] // 2
            return pl.ds(pl.multiple_of(core * half + j * (half // ch), half // ch), half // ch)

        def chip(rel):
            return x ^ (rel >> 1), y ^ (rel & 1)

        def ici(k, j, slot, rel, send_sems, recv_sems, sem):
            px, py = chip(rel)
            piece = dst_refs[k].at[slot, rows(k, c, j)]
            return pltpu.make_async_remote_copy(src_ref=piece, dst_ref=piece, send_sem=send_sems.at[sem],
                                                recv_sem=recv_sems.at[sem], device_id=(px, py, c),
                                                device_id_type=MESH)

        def direct(k, j, slot, rel):
            return ici(k, j, slot, rel, ici_send, ici_recv, ((rel - 1) * ch + j) * n + k)

        def hop(k, j, slot, rel):
            return ici(k, j, slot, rel, hop_send, hop_recv, j * n + k)

        def d2d(k, j, rel, core):
            piece = dst_refs[k].at[b ^ rel, rows(k, core, j)]
            sem = ((rel - 1) * ch + j) * n + k
            return pltpu.make_async_remote_copy(src_ref=piece, dst_ref=piece, send_sem=d2d_send.at[sem],
                                                recv_sem=d2d_recv.at[sem], device_id=(x, y, 1 - c),
                                                device_id_type=MESH)

        pieces = [(k, j) for j in range(ch) for k in range(n)]
        for k, j in pieces:
            for rel in (1, 2):
                direct(k, j, b, rel).start()
        side_work(side_in, side_out)
        for k, j in pieces:
            direct(k, j, b ^ via, via).wait_recv()
            hop(k, j, b ^ via, out).start()
            d2d(k, j, via, c).start()
        for k, j in pieces:
            direct(k, j, b ^ out, out).wait_recv()
            d2d(k, j, out, c).start()
        for k, j in pieces:
            hop(k, j, b ^ 3, via).wait_recv()
            d2d(k, j, 3, c).start()
        for k, j in pieces:
            for rel in (1, 2, 3):
                d2d(k, j, rel, 1 - c).wait_recv()
        for k, j in pieces:
            for rel in (1, 2):
                direct(k, j, b, rel).wait_send()
            hop(k, j, b ^ via, out).wait_send()
            d2d(k, j, via, c).wait_send()
            d2d(k, j, out, c).wait_send()
            d2d(k, j, 3, c).wait_send()

    vmem_spec = pl.BlockSpec(memory_space=pltpu.VMEM)
    dma = pltpu.SemaphoreType.DMA
    out_shape = tuple(jax.ShapeDtypeStruct((N_CHIP,) + a.shape, BF16) for a in blocks) + tuple(side_shapes)
    outs = pl.pallas_call(
        body, name=name, in_specs=[vmem_spec] * (n + n_in), out_specs=tuple([vmem_spec] * (n + n_out)),
        out_shape=out_shape,
        scratch_shapes=[dma((2 * ch * n,)), dma((2 * ch * n,)), dma((ch * n,)), dma((ch * n,)),
                        dma((3 * ch * n,)), dma((3 * ch * n,))],
        compiler_params=pltpu.CompilerParams(vmem_limit_bytes=VMEM_LIMIT),
    )(*blocks, *side_inputs)
    return outs[:n], outs[n:]


def _grad_reduce(order, h_t, wins, gw_out, small):
    s = h_t.shape[1]
    tk = GRAD_ROWS
    n_i = s // tk
    half = D_MODEL // 2
    o_half = OUT_ROWS // 2
    n_rel = N_CHIP - 1

    def body(order_ref, ht_ref, win_ref, gwo_ref, small_ref,
             win_out, wout_out, small_out,
             acc, mine, s1, r1, s2, r2, so1, ro1, so2, ro2, pair_in, pair_o, small_land,
             s1_send, s1_recv, s2_send, s2_recv, o1_send, o1_recv, o2_send, o2_recv,
             pair_send, pair_recv, small_send, small_recv):
        j = pl.program_id(0)
        i = pl.program_id(1)
        x, y, c = _position()
        me = 4 * x + 2 * y + c
        sibling = (x, y, 1 - c)
        my_rows = pl.ds(pl.multiple_of(c * half, half), half)
        sib_rows = pl.ds(pl.multiple_of((1 - c) * half, half), half)

        def chip_of(rel):
            return x ^ (rel >> 1), y ^ (rel & 1)

        def level1(k):
            return pltpu.make_async_remote_copy(src_ref=s1.at[k], dst_ref=r1.at[k], send_sem=s1_send.at[k],
                                                recv_sem=s1_recv.at[k], device_id=sibling, device_id_type=MESH)

        def level2(k):
            px, py = chip_of(RELATIONS[k])
            return pltpu.make_async_remote_copy(src_ref=s2.at[k], dst_ref=r2.at[k], send_sem=s2_send.at[k],
                                                recv_sem=s2_recv.at[k], device_id=(px, py, c), device_id_type=MESH)

        def out_level1(bk):
            return pltpu.make_async_remote_copy(src_ref=so1.at[bk], dst_ref=ro1.at[bk], send_sem=o1_send.at[bk],
                                                recv_sem=o1_recv.at[bk], device_id=sibling, device_id_type=MESH)

        def out_level2(k):
            px, py = chip_of(RELATIONS[k])
            return pltpu.make_async_remote_copy(src_ref=so2.at[k], dst_ref=ro2.at[k], send_sem=o2_send.at[k],
                                                recv_sem=o2_recv.at[k], device_id=(px, py, c), device_id_type=MESH)

        def small_copy(d):
            px, py, pc = x ^ (d >> 2), y ^ ((d >> 1) & 1), c ^ (d & 1)
            return pltpu.make_async_remote_copy(src_ref=small_ref, dst_ref=small_land.at[me],
                                                send_sem=small_send.at[d], recv_sem=small_recv.at[d],
                                                device_id=(px, py, pc), device_id_type=MESH)

        def pair_copy(k, buf):
            return pltpu.make_async_remote_copy(src_ref=buf.at[0], dst_ref=buf.at[1], send_sem=pair_send.at[k],
                                                recv_sem=pair_recv.at[k], device_id=sibling, device_id_type=MESH)

        def out_rows(bk, core):
            return pl.ds(pl.multiple_of(bk * OUT_ROWS + core * o_half, o_half), o_half)

        @pl.when((j == 0) & (i == 0))
        def _():
            for d in range(1, N_DEV):
                small_copy(d).start()
            small_land[me] = small_ref[...]
            for bk in range(N_CHIP):
                so1[bk] = gwo_ref[out_rows(bk, 1 - c), :].astype(BF16)
                out_level1(bk).start()

        @pl.when((j == 0) & (i == 1))
        def _():
            b = 2 * x + y
            for bk in range(N_CHIP):
                out_level1(bk).wait_recv()
            for k in range(n_rel):
                px, py = chip_of(RELATIONS[k])
                bk = 2 * px + py
                so2[k] = (gwo_ref[out_rows(bk, c), :] + ro1[bk].astype(F32)).astype(BF16)
                out_level2(k).start()

        @pl.when(i == 0)
        def _():
            acc[...] = jnp.zeros_like(acc)

        for n0 in range(0, WIN, ACC_COLS):
            n1 = min(n0 + ACC_COLS, WIN)
            acc[:, n0:n1] += _dot(ht_ref[...], win_ref[:, n0:n1])

        for k in range(N_CHIP):
            @pl.when((j == k) & (i == n_i - 1))
            def _(k=k):
                s1[k] = acc[sib_rows, :].astype(BF16)
                level1(k).start()
                mine[...] = acc[my_rows, :]

            if k < n_rel:
                @pl.when((j == k + 1) & (i == 1))
                def _(k=k):
                    level1(k).wait_recv()
                    s2[k] = (mine[...] + r1[k].astype(F32)).astype(BF16)
                    level2(k).start()

        @pl.when((j == N_CHIP - 1) & (i == n_i - 1))
        def _():
            b = 2 * x + y
            level1(N_CHIP - 1).wait_recv()
            total = mine[...] + r1[N_CHIP - 1].astype(F32)
            for k in range(n_rel):
                level2(k).wait_recv()
                total = total + r2[k].astype(F32)
            total = total.T
            pair_in[0] = total
            pair_copy(0, pair_in).start()
            total_o = gwo_ref[out_rows(b, c), :] + ro1[b].astype(F32)
            for k in range(n_rel):
                out_level2(k).wait_recv()
                total_o = total_o + ro2[k].astype(F32)
            pair_o[0] = total_o
            pair_copy(1, pair_o).start()
            for core in range(2):
                @pl.when(c == core)
                def _(core=core):
                    win_out[:, core * half:(core + 1) * half] = total
            wout_out[c] = total_o
            for d in range(1, N_DEV):
                small_copy(d).wait_recv()
            small_out[...] = small_land[...]
            pair_copy(0, pair_in).wait_recv()
            for core in range(2):
                @pl.when(c == core)
                def _(core=core):
                    win_out[:, (1 - core) * half:(2 - core) * half] = pair_in[1]
            pair_copy(1, pair_o).wait_recv()
            wout_out[1 - c] = pair_o[1]
            for d in range(1, N_DEV):
                small_copy(d).wait_send()
            for k in range(N_CHIP):
                level1(k).wait_send()
                out_level1(k).wait_send()
            for k in range(n_rel):
                level2(k).wait_send()
                out_level2(k).wait_send()
            pair_copy(0, pair_in).wait_send()
            pair_copy(1, pair_o).wait_send()

    vmem = pl.BlockSpec(memory_space=pltpu.VMEM)
    dma = pltpu.SemaphoreType.DMA
    sds = jax.ShapeDtypeStruct
    grid_spec = pltpu.PrefetchScalarGridSpec(
        num_scalar_prefetch=1, grid=(N_CHIP, n_i),
        in_specs=[pl.BlockSpec((D_MODEL, tk), lambda j, i, order: (0, i)),
                  pl.BlockSpec((None, tk, WIN), lambda j, i, order: (order[j], i, 0)), vmem, vmem],
        out_specs=(vmem, vmem, vmem),
        scratch_shapes=[
            pltpu.VMEM((D_MODEL, WIN), F32), pltpu.VMEM((half, WIN), F32),
            pltpu.VMEM((N_CHIP, half, WIN), BF16), pltpu.VMEM((N_CHIP, half, WIN), BF16),
            pltpu.VMEM((n_rel, half, WIN), BF16), pltpu.VMEM((n_rel, half, WIN), BF16),
            pltpu.VMEM((N_CHIP, o_half, D_MODEL), BF16), pltpu.VMEM((N_CHIP, o_half, D_MODEL), BF16),
            pltpu.VMEM((n_rel, o_half, D_MODEL), BF16), pltpu.VMEM((n_rel, o_half, D_MODEL), BF16),
            pltpu.VMEM((2, WIN, half), F32), pltpu.VMEM((2, o_half, D_MODEL), F32),
            pltpu.VMEM((N_DEV, PACK_ROWS, D_MODEL), F32),
            dma((N_CHIP,)), dma((N_CHIP,)), dma((n_rel,)), dma((n_rel,)),
            dma((N_CHIP,)), dma((N_CHIP,)), dma((n_rel,)), dma((n_rel,)),
            dma((2,)), dma((2,)), dma((N_DEV,)), dma((N_DEV,))])
    return pl.pallas_call(
        body, name="grad_w_in_reduce", grid_spec=grid_spec,
        out_shape=(sds((WIN, D_MODEL), F32), sds((2, o_half, D_MODEL), F32), sds((N_DEV, PACK_ROWS, D_MODEL), F32)),
        compiler_params=pltpu.CompilerParams(dimension_semantics=("arbitrary", "arbitrary"),
                                             vmem_limit_bytes=VMEM_LIMIT),
    )(order, h_t, wins, gw_out, small)


ADAM_STEPS = 4


def _adamw_math(w, g, m, v):
    m = ADAM_B1 * m + (1.0 - ADAM_B1) * g
    v = ADAM_B2 * v + (1.0 - ADAM_B2) * (g * g)
    m_hat = m / (1.0 - ADAM_B1 ** ADAM_STEP)
    v_hat = v / (1.0 - ADAM_B2 ** ADAM_STEP)
    delta = -ADAM_LR * (m_hat / (jnp.sqrt(v_hat) + ADAM_EPS) + ADAM_WD * w)
    return delta, m, v


def _adamw(w, g, m, v, name):
    r, c = w.shape

    def body(w_ref, g_ref, m_ref, v_ref, d_ref, nm_ref, nv_ref):
        delta, nm, nv = _adamw_math(w_ref[...], g_ref[...], m_ref[...], v_ref[...])
        d_ref[...] = delta
        nm_ref[...] = nm
        nv_ref[...] = nv

    rows = r // ADAM_STEPS
    assert rows * ADAM_STEPS == r and rows % 8 == 0
    spec = pl.BlockSpec((rows, c), lambda i: (i, 0))
    shape = jax.ShapeDtypeStruct((r, c), F32)
    return pl.pallas_call(
        body, name=name, grid=(ADAM_STEPS,), in_specs=[spec] * 4, out_specs=(spec,) * 3,
        out_shape=(shape,) * 3, compiler_params=pltpu.CompilerParams(vmem_limit_bytes=VMEM_LIMIT),
    )(w, g, m, v)


def _adamw_window(w, window, shift, m, v, name):
    r, c = w.shape
    rows = r // ADAM_STEPS
    assert rows * ADAM_STEPS == r and rows % 8 == 0

    def body(shift_ref, w_ref, win_hbm, m_ref, v_ref, g_ref, d_ref, nm_ref, nv_ref, g_vmem):
        start = pl.multiple_of(shift_ref[0] + pl.program_id(0) * rows, 8)
        pltpu.sync_copy(win_hbm.at[pl.ds(start, rows)], g_vmem)
        g = g_vmem[...]
        g_ref[...] = g
        delta, nm, nv = _adamw_math(w_ref[...], g, m_ref[...], v_ref[...])
        d_ref[...] = delta
        nm_ref[...] = nm
        nv_ref[...] = nv

    spec = pl.BlockSpec((rows, c), lambda i, shift_ref: (i, 0))
    shape = jax.ShapeDtypeStruct((r, c), F32)
    grid_spec = pltpu.PrefetchScalarGridSpec(
        num_scalar_prefetch=1, grid=(ADAM_STEPS,),
        in_specs=[spec, pl.BlockSpec(memory_space=pl.ANY), spec, spec], out_specs=(spec,) * 4,
        scratch_shapes=[pltpu.VMEM((rows, c), F32)])
    return pl.pallas_call(
        body, name=name, grid_spec=grid_spec, out_shape=(shape,) * 4,
        compiler_params=pltpu.CompilerParams(vmem_limit_bytes=VMEM_LIMIT),
    )(shift, w, window, m, v)


PACK_ROWS = 8


def _fold_heads(v):
    y = v[:, 0:LANES]
    for j in range(1, v.shape[1] // LANES):
        y = y + v[:, j * LANES:(j + 1) * LANES]
    return y + pltpu.roll(y, HEAD_DIM, 1)


N_SMALL = 6


def _small_adamw(recv, weights, m, v):
    def body(*refs):
        r_ref = refs[0]
        w_refs, m_refs, v_refs = (refs[1 + n * N_SMALL:1 + (n + 1) * N_SMALL] for n in range(3))
        outs = refs[1 + 3 * N_SMALL:]
        g_refs, d_refs, nm_refs, nv_refs = (outs[n * N_SMALL:(n + 1) * N_SMALL] for n in range(4))
        loss_ref = outs[4 * N_SMALL]
        tot = r_ref[0]
        for j in range(1, N_DEV):
            tot = tot + r_ref[j]
        loss_ref[...] = tot[3:4, 0:LANES]
        row1 = tot[1:2, :]
        row2 = tot[2:3, :]
        grads = [tot[0:1, :],
                 _fold_heads(row1[:, 0:WIDTH])[:, :HEAD_DIM],
                 _fold_heads(row2[:, WIDTH:WIDTH + A_KV_WIDTH])[:, :HEAD_DIM],
                 row2[:, WIDTH + A_KV_WIDTH:WIDTH + A_KV_WIDTH + HEADS],
                 _fold_heads(row1[:, WIDTH:2 * WIDTH])[:, :HEAD_DIM],
                 _fold_heads(row2[:, 0:WIDTH])[:, :HEAD_DIM]]
        for n, g in enumerate(grads):
            g_refs[n][...] = g
            delta, nm, nv = _adamw_math(w_refs[n][...], g, m_refs[n][...], v_refs[n][...])
            d_refs[n][...] = delta
            nm_refs[n][...] = nm
            nv_refs[n][...] = nv

    shapes = tuple(jax.ShapeDtypeStruct(a.shape, F32) for a in weights)
    outs = pl.pallas_call(body, name="small_adamw", out_shape=shapes * 4 + (jax.ShapeDtypeStruct((1, LANES), F32),)
                          )(recv, *weights, *m, *v)
    return tuple(outs[n * N_SMALL:(n + 1) * N_SMALL] for n in range(4)) + (outs[4 * N_SMALL],)


def kernel(x, norm_gain, w_in, q_norm_a, k_norm_a, sinks_a, q_norm_b, k_norm_b, w_out, loss_target, m_norm_gain, m_w_in, m_q_norm_a, m_k_norm_a, m_sinks_a, m_q_norm_b, m_k_norm_b, m_w_out, v_norm_gain, v_w_in, v_q_norm_a, v_k_norm_a, v_sinks_a, v_q_norm_b, v_k_norm_b, v_w_out):
    chip = 2 * lax.axis_index("x") + lax.axis_index("y")

    w_in_t, m_w_in_t, v_w_in_t = w_in[0].T, m_w_in[0].T, v_w_in[0].T

    s = x.shape[1]
    side_inputs = _rope_angles(s) + (q_norm_a, k_norm_a, q_norm_b, k_norm_b, sinks_a)
    (w_in_all,), tables = _gather_weights([w_in_t], "gather_weights", side_inputs, _table_shapes(s), _tables)
    w_in_bf = w_in_all.reshape(IN_WIDTH, D_MODEL)

    loss_part, gx, h_t, wins, gw_out, (dgain, dgqa, dgka, dsink, dgqb, dgkb) = _local_step(
        x[0], loss_target[0], norm_gain, w_in_bf, q_norm_a, k_norm_a, sinks_a, q_norm_b, k_norm_b, w_out[0], tables)

    small = jnp.concatenate([
        dgain, jnp.concatenate([dgqa, dgqb], axis=1),
        jnp.concatenate([dgkb, dgka, dsink, jnp.zeros((1, D_MODEL - WIDTH - 2 * A_KV_WIDTH), F32)], axis=1),
        jnp.pad(loss_part, ((0, 0), (0, D_MODEL - LANES))),
        jnp.zeros((PACK_ROWS - 4, D_MODEL), F32)], axis=0)
    order = (chip ^ jnp.array(RELATIONS, jnp.int32)).astype(jnp.int32)
    win_sum, wout_sum, small_recv = _grad_reduce(order, h_t, wins, gw_out, small)
    shift = jnp.array(WIN_SHIFT, jnp.int32)[chip].reshape(1)
    g_w_out = wout_sum.reshape(OUT_ROWS, D_MODEL)

    g_w_in, d_w_in, nm_w_in, nv_w_in = (
        a.T for a in _adamw_window(w_in_t, win_sum, shift, m_w_in_t, v_w_in_t, "adamw_w_in"))
    d_w_out, nm_w_out, nv_w_out = _adamw(w_out[0], g_w_out, m_w_out[0], v_w_out[0], "adamw_w_out")
    g_s, d_s, nm_s, nv_s, loss_row = _small_adamw(
        small_recv,
        (norm_gain, q_norm_a, k_norm_a, sinks_a, q_norm_b, k_norm_b),
        (m_norm_gain, m_q_norm_a, m_k_norm_a, m_sinks_a, m_q_norm_b, m_k_norm_b),
        (v_norm_gain, v_q_norm_a, v_k_norm_a, v_sinks_a, v_q_norm_b, v_k_norm_b))
    loss = loss_row[0, 0]

    def leaves(small_ones, big_in, big_out):
        return (small_ones[0], big_in[None]) + tuple(small_ones[1:]) + (big_out[None],)

    return ((loss, gx[None]) + leaves(g_s, g_w_in, g_w_out) + leaves(d_s, d_w_in, d_w_out)
            + leaves(nm_s, nm_w_in, nm_w_out) + leaves(nv_s, nv_w_in, nv_w_out))
```

```python
import numpy as np
import jax
import jax.numpy as jnp
from jax import lax
from jax.experimental import pallas as pl
from jax.experimental.pallas import tpu as pltpu

F32 = jnp.float32
BF16 = jnp.bfloat16

D_MODEL = 1024
HEAD_DIM = 64
HEADS = 8
WIDTH = HEADS * HEAD_DIM
A_KV_WIDTH = 2 * HEAD_DIM
BLOCK = 128
LANES = 128
FOLD = 16
A_MAX_DIST = 127
B_MAX_DIST = 128
ROPE_THETA = 10000.0
ROPE_SPLIT = 64
EPS = 1e-6
NEG = -1e30
SCALE = HEAD_DIM ** -0.5

IN_WIDTH = 3328
C_QA, C_KA, C_VA, C_GA, C_QB, C_KB, C_VB, C_GB, C_END = 0, 512, 640, 768, 1280, 1792, 2304, 2816, 3328

N_DEV = 8
N_CHIP = 4
MESH = pl.DeviceIdType.MESH
WIN = 896
WIN_START = (0, 768, 1664, 2432)
WIN_SHIFT = (0, 64, 0, 64)
OUT_ROWS = D_MODEL // N_CHIP
RELATIONS = (3, 1, 2, 0)

ADAM_LR = 0.001
ADAM_B1 = 0.9
ADAM_B2 = 0.999
ADAM_EPS = 1e-08
ADAM_WD = 0.01
ADAM_STEP = 10

ROW_TILE = 256
FOLD_ROWS = ROW_TILE // FOLD
GRAD_ROWS = 1024
ACC_COLS = 256
VMEM_LIMIT = 56 * 1024 * 1024


def _dot(a, b):
    return jnp.dot(a, b, preferred_element_type=F32)


def _dot_nt(a, b):
    return lax.dot_general(a, b, (((1,), (1,)), ((), ())), preferred_element_type=F32)


def _dot_tn(a, b):
    return lax.dot_general(a, b, (((0,), (0,)), ((), ())), preferred_element_type=F32)


def _head_sum(z, bd):
    w = bd.shape[0]
    zb = z.astype(BF16)
    parts = [_dot(zb[:, a:a + w], bd) for a in range(0, z.shape[1], w)]
    return parts[0] if len(parts) == 1 else jnp.concatenate(parts, axis=1)


def _swap_halves(t):
    w = t.shape[1]
    lane = lax.broadcasted_iota(jnp.int32, t.shape, 1)
    return jnp.where(lane % HEAD_DIM < HEAD_DIM // 2, pltpu.roll(t, w - 32, 1), pltpu.roll(t, 32, 1))


def _qknorm_rope(t, g, cos, sin_s, bd):
    r = lax.rsqrt(_head_sum(t * t, bd) * (1.0 / HEAD_DIM) + EPS)
    n = (t * r) * g
    return n * cos + _swap_halves(n) * sin_s


def _qknorm_rope_bwd(dout, t, g, cos, sin_s, bd):
    dout, t = dout.astype(F32), t.astype(F32)
    dn = dout * cos + _swap_halves(dout * sin_s)
    r = lax.rsqrt(_head_sum(t * t, bd) * (1.0 / HEAD_DIM) + EPS)
    tr = t * r
    u = dn * g
    dt = r * (u - tr * (_head_sum(u * tr, bd) * (1.0 / HEAD_DIM)))
    return dt, dn * tr


def _sigmoid(g):
    return 1.0 / (1.0 + jnp.exp(-g))


def _expand_heads(st):
    t = st.shape[0]
    lane = lax.broadcasted_iota(jnp.int32, (t, LANES), 1)
    chunks = []
    for c in range(WIDTH // LANES):
        chunks.append(jnp.where(lane < HEAD_DIM, st[:, 2 * c:2 * c + 1], st[:, 2 * c + 1:2 * c + 2]))
    return jnp.concatenate(chunks, axis=1)


def _reduce_heads(z):
    t = z.shape[0]
    lane = lax.broadcasted_iota(jnp.int32, (t, LANES), 1)
    out = jnp.zeros((t, LANES), F32)
    for c in range(WIDTH // LANES):
        zc = z[:, c * LANES:(c + 1) * LANES]
        for ph in range(2):
            s = jnp.sum(jnp.where((lane // HEAD_DIM) == ph, zc, 0.0), axis=-1, keepdims=True)
            out = jnp.where(lane == 2 * c + ph, s, out)
    return out


def _fold_scratch(w):
    return pltpu.VMEM((w // LANES, ROW_TILE, LANES), F32)


def _store_folded(out_ref, val, scr, col0=0):
    w = val.shape[1]
    n = w // LANES
    for c in range(n):
        scr[c] = val[:, c * LANES:(c + 1) * LANES]
    for r in range(FOLD):
        piece = [scr[c, pl.ds(r, FOLD_ROWS, stride=FOLD), :] for c in range(n)]
        out_ref[r, :, col0:col0 + w] = (piece[0] if n == 1 else jnp.concatenate(piece, axis=1)).astype(out_ref.dtype)


def _load_folded(in_ref, scr):
    n = in_ref.shape[2] // LANES
    for r in range(FOLD):
        blk = in_ref[r].astype(F32)
        for c in range(n):
            scr[c, pl.ds(r, FOLD_ROWS, stride=FOLD), :] = blk[:, c * LANES:(c + 1) * LANES]
    return scr[0] if n == 1 else jnp.concatenate([scr[c] for c in range(n)], axis=1)


def _store_folded_bf16(out_ref, val, perm):
    folded = _dot(perm, val.astype(BF16)).astype(out_ref.dtype)
    for r in range(FOLD):
        out_ref[r] = folded[r * FOLD_ROWS:(r + 1) * FOLD_ROWS]


def _load_folded_bf16(in_ref, perm):
    blk = jnp.concatenate([in_ref[r] for r in range(FOLD)], axis=0)
    return _dot(perm, blk)


def _rows(w, tm=ROW_TILE):
    return pl.BlockSpec((tm, w), lambda i: (i, 0))


def _folded_rows(w):
    return pl.BlockSpec((FOLD, FOLD_ROWS, w), lambda i: (0, i, 0))


def _whole(shape):
    return pl.BlockSpec(shape, lambda i: (0,) * len(shape))


def _inproj(x2, gain, w_bf, cos, sin_s, gqa, gka, gqb, gkb, bd256, bd128, w_out_blk):
    s = x2.shape[0]
    tm = ROW_TILE
    n_steps = s // tm
    n_rel = N_CHIP - 1
    o_half = OUT_ROWS // 2

    def body(x_ref, gain_ref, w_hbm, cos_ref, sin_ref, gqa_ref, gka_ref, gqb_ref, gkb_ref, bd256_ref, bd128_ref,
             wout_ref, qa_ref, kva_ref, qb_ref, kvb_ref, qbf_ref, kvbf_ref,
             qa_raw_ref, ka_raw_ref, ga_ref, qb_raw_ref, kb_raw_ref, gb_ref, wout_all_ref,
             w_vmem, scr, land, ici_send, ici_recv, d2d_send, d2d_recv):
        i = pl.program_id(0)
        px_, py_, c = _position()
        b = 2 * px_ + py_

        def piece(chip_idx, core):
            return land.at[chip_idx, pl.ds(pl.multiple_of(core * o_half, o_half), o_half)]

        def other_chip(d):
            ox, oy = px_ ^ (d >> 1), py_ ^ (d & 1)
            return ox, oy, 2 * ox + oy

        def ici_copy(d, chip_idx):
            ox, oy, _ = other_chip(d)
            return pltpu.make_async_remote_copy(
                src_ref=piece(chip_idx, c), dst_ref=piece(chip_idx, c), send_sem=ici_send.at[d - 1],
                recv_sem=ici_recv.at[d - 1], device_id=(ox, oy, c), device_id_type=MESH)

        def d2d_copy(d, core):
            return pltpu.make_async_remote_copy(
                src_ref=piece(other_chip(d)[2], core), dst_ref=piece(other_chip(d)[2], core),
                send_sem=d2d_send.at[d - 1], recv_sem=d2d_recv.at[d - 1], device_id=(px_, py_, 1 - c),
                device_id_type=MESH)

        @pl.when(i == 0)
        def _():
            pltpu.sync_copy(w_hbm, w_vmem)
            land[b] = wout_ref[...].astype(BF16)
            for d in range(1, N_CHIP):
                ici_copy(d, b).start()

        @pl.when(i == n_steps // 2)
        def _():
            for d in range(1, N_CHIP):
                ici_copy(d, other_chip(d)[2]).wait_recv()
                d2d_copy(d, c).start()

        @pl.when(i == n_steps - 1)
        def _():
            for d in range(1, N_CHIP):
                d2d_copy(d, 1 - c).wait_recv()
            for d in range(1, N_CHIP):
                ici_copy(d, b).wait_send()
                d2d_copy(d, c).wait_send()
            for k in range(N_CHIP):
                wout_all_ref[k * OUT_ROWS:(k + 1) * OUT_ROWS, :] = land[k]

        xt = x_ref[...]
        r = lax.rsqrt(jnp.mean(xt * xt, axis=-1, keepdims=True) + EPS)
        h = ((xt * r) * gain_ref[...]).astype(BF16)
        cos1 = cos_ref[...]
        sin1 = sin_ref[...]
        cos4 = jnp.tile(cos1, (1, 4))
        sin4 = jnp.tile(sin1, (1, 4))

        def seg(a, b):
            return _dot_nt(h, w_vmem[a:b, :])

        t = seg(C_QA, C_KA)
        qa_raw_ref[...] = t.astype(BF16)
        qa_ref[...] = (_qknorm_rope(t, gqa_ref[...], cos4, sin4, bd256_ref[...]) * SCALE).astype(BF16)
        t = seg(C_KA, C_VA)
        ka_raw_ref[...] = t.astype(BF16)
        kva_ref[:, :A_KV_WIDTH] = _qknorm_rope(t, gka_ref[...], cos1, sin1, bd128_ref[...]).astype(BF16)
        kva_ref[:, A_KV_WIDTH:] = seg(C_VA, C_GA).astype(BF16)
        ga_ref[...] = seg(C_GA, C_QB).astype(BF16)
        t = seg(C_QB, C_KB)
        qb_raw_ref[...] = t.astype(BF16)
        t = _qknorm_rope(t, gqb_ref[...], cos4, sin4, bd256_ref[...]) * SCALE
        qb_ref[...] = t.astype(BF16)
        _store_folded(qbf_ref, t, scr)
        t = seg(C_KB, C_VB)
        kb_raw_ref[...] = t.astype(BF16)
        t = _qknorm_rope(t, gkb_ref[...], cos4, sin4, bd256_ref[...])
        kvb_ref[:, :WIDTH] = t.astype(BF16)
        _store_folded(kvbf_ref, t, scr)
        t = seg(C_VB, C_GB)
        kvb_ref[:, WIDTH:] = t.astype(BF16)
        _store_folded(kvbf_ref, t, scr, WIDTH)
        gb_ref[...] = seg(C_GB, C_END).astype(BF16)

    sds = jax.ShapeDtypeStruct
    ln = s // FOLD
    out_shape = (sds((s, WIDTH), BF16), sds((s, 2 * A_KV_WIDTH), BF16), sds((s, WIDTH), BF16),
                 sds((s, 2 * WIDTH), BF16), sds((FOLD, ln, WIDTH), BF16), sds((FOLD, ln, 2 * WIDTH), BF16),
                 sds((s, WIDTH), BF16), sds((s, A_KV_WIDTH), BF16), sds((s, WIDTH), BF16),
                 sds((s, WIDTH), BF16), sds((s, WIDTH), BF16), sds((s, WIDTH), BF16),
                 sds((D_MODEL, D_MODEL), BF16))
    out_specs = (_rows(WIDTH), _rows(2 * A_KV_WIDTH), _rows(WIDTH), _rows(2 * WIDTH),
                 _folded_rows(WIDTH), _folded_rows(2 * WIDTH),
                 _rows(WIDTH), _rows(A_KV_WIDTH), _rows(WIDTH), _rows(WIDTH), _rows(WIDTH), _rows(WIDTH),
                 _whole((D_MODEL, D_MODEL)))
    dma = pltpu.SemaphoreType.DMA
    return pl.pallas_call(
        body, name="inproj_fwd", grid=(n_steps,),
        in_specs=[_rows(D_MODEL), _whole(gain.shape), pl.BlockSpec(memory_space=pl.ANY), _rows(LANES), _rows(LANES),
                  _whole(gqa.shape), _whole(gka.shape), _whole(gqb.shape), _whole(gkb.shape), _whole(bd256.shape),
                  _whole(bd128.shape), _whole(w_out_blk.shape)],
        out_specs=out_specs, out_shape=out_shape,
        scratch_shapes=[pltpu.VMEM((IN_WIDTH, D_MODEL), BF16), _fold_scratch(WIDTH),
                        pltpu.VMEM((N_CHIP, OUT_ROWS, D_MODEL), BF16),
                        dma((n_rel,)), dma((n_rel,)), dma((n_rel,)), dma((n_rel,))],
        compiler_params=pltpu.CompilerParams(dimension_semantics=("arbitrary",), vmem_limit_bytes=VMEM_LIMIT),
    )(x2, gain, w_bf, cos, sin_s, gqa, gka, gqb, gkb, bd256, bd128, w_out_blk)


def _seq_pos(idx, dil):
    if dil == 4:
        return 4 * (idx % 32) + idx // 32
    return idx


def _upper_mask(dil, r0=0, rows=2 * BLOCK):
    qi = (lax.broadcasted_iota(jnp.int32, (rows, BLOCK), 0) + r0) % BLOCK
    kj = lax.broadcasted_iota(jnp.int32, (rows, BLOCK), 1)
    return _seq_pos(kj, dil) > _seq_pos(qi, dil)


def _eye_mask(r0=0, rows=2 * BLOCK):
    qi = (lax.broadcasted_iota(jnp.int32, (rows, BLOCK), 0) + r0) % BLOCK
    kj = lax.broadcasted_iota(jnp.int32, (rows, BLOCK), 1)
    return qi == kj


def _stack_heads(a2, c, gqa):
    lane = lax.broadcasted_iota(jnp.int32, (1, LANES), 1) // HEAD_DIM
    zero = jnp.zeros_like(a2)
    if gqa:
        keep = lane == (c // 2)
        return jnp.concatenate([jnp.where(keep, a2, zero), jnp.where(keep, _swap_heads(a2), zero)], axis=0)
    return jnp.concatenate([jnp.where(lane == 0, a2, zero), jnp.where(lane == 1, a2, zero)], axis=0)


def _unstack_heads(a, c, gqa):
    lane = lax.broadcasted_iota(jnp.int32, (1, LANES), 1) // HEAD_DIM
    if gqa:
        return jnp.where(lane == (c // 2), a[:BLOCK], _swap_heads(a[BLOCK:]))
    return jnp.where(lane == 0, a[:BLOCK], a[BLOCK:])


def _stacked_head_ids(c, gqa):
    if gqa:
        return 2 * c + c // 2, 2 * c + 1 - c // 2
    return 2 * c, 2 * c + 1


def _per_head_rows(blk, heads):
    return jnp.concatenate([blk[:, heads[0]:heads[0] + 1], blk[:, heads[1]:heads[1] + 1]], axis=0)


def _attn_view(a, dil):
    if dil == 1:
        return a[None]
    if dil == 4:
        return a.reshape(4, 4, a.shape[1], a.shape[2])
    return a


def _attn_unview(a, dil):
    if dil == 1:
        return a[0]
    if dil == 4:
        return a.reshape(FOLD, a.shape[2], a.shape[3])
    return a


ATTN_BLOCKS_PER_STEP = 8


def _attn_specs(dil):
    if dil == 4:
        def spec(n, fn):
            return lambda w: pl.BlockSpec((4, None, n * BLOCK // 4, w), lambda r, i: (0, r, fn(i), 0))
    else:
        def spec(n, fn):
            return lambda w: pl.BlockSpec((None, n * BLOCK, w), lambda r, i: (r, fn(i), 0))
    return spec


def _blk_rows(g, dil):
    n = BLOCK // 4 if dil == 4 else BLOCK
    if isinstance(g, int):
        return slice(g * n, (g + 1) * n)
    return pl.ds(pl.multiple_of(g * n, n), n)


def _blk_load(ref, sl, dil, g=0):
    if dil == 4:
        return ref[:, _blk_rows(g, dil), sl].reshape(BLOCK, sl.stop - sl.start)
    return ref[_blk_rows(g, dil), sl]


def _blk_store(ref, sl, val, dil, g=0):
    val = val.astype(ref.dtype)
    if dil == 4:
        ref[:, _blk_rows(g, dil), sl] = val.reshape(4, BLOCK // 4, sl.stop - sl.start)
    else:
        ref[_blk_rows(g, dil), sl] = val


def _swap_heads(a):
    return pltpu.roll(a.astype(F32), HEAD_DIM, 1).astype(a.dtype)


STAT_SHIFT = 8


def _attn_fwd(q, kv, sinks, *, dil, max_dist, name):
    q, kv = _attn_view(q, dil), _attn_view(kv, dil)
    kw = kv.shape[-1] // 2
    gqa = kw == A_KV_WIDTH
    n_seq = dil
    nb = (q.shape[-2] * (4 if dil == 4 else 1)) // BLOCK
    per_step = min(ATTN_BLOCKS_PER_STEP, nb)
    with_sinks = sinks is not None
    all_lanes = slice(0, LANES)
    assert max_dist in (BLOCK - 1, BLOCK) and nb % per_step == 0 and (per_step == 1 or per_step % 2 == 0)
    diag = max_dist == BLOCK

    def body(*refs):
        if with_sinks:
            q_ref, kvp_ref, kvc_ref, sink_ref, o_ref, ml_ref = refs
        else:
            q_ref, kvp_ref, kvc_ref, o_ref, ml_ref = refs

        chunks = range(WIDTH // LANES)

        def matmuls_in(g, has_prev):
            prev_ref, prev_g = (kvp_ref, 0) if (isinstance(g, int) and g == 0) else (kvc_ref, g - 1)
            scores, values = [], []
            for c in chunks:
                sl = slice(c * LANES, (c + 1) * LANES)
                ksl = slice(0, LANES) if gqa else sl
                vsl = slice(ksl.start + kw, ksl.stop + kw)
                kcur, vcur = _blk_load(kvc_ref, ksl, dil, g), _blk_load(kvc_ref, vsl, dil, g)
                qs = _stack_heads(_blk_load(q_ref, sl, dil, g), c, gqa)
                if has_prev:
                    kcur = jnp.concatenate([_blk_load(prev_ref, ksl, dil, prev_g), kcur], axis=0)
                    vcur = jnp.concatenate([_blk_load(prev_ref, vsl, dil, prev_g), vcur], axis=0)
                scores.append(_dot_nt(qs, kcur))
                values.append(vcur)
            return scores, values

        def tile_ops(has_prev, scores):
            lane = lax.broadcasted_iota(jnp.int32, (1, LANES), 1)
            with_diag = diag and has_prev
            upper, eye = _upper_mask(dil), _eye_mask()
            first_rows = lax.broadcasted_iota(jnp.int32, (2 * BLOCK, 1), 0) < BLOCK
            ml_blk = jnp.zeros((BLOCK, LANES), F32)
            probs = []
            for c in chunks:
                heads = _stacked_head_ids(c, gqa)
                s = scores[c]
                if has_prev:
                    s_p = s[:, :BLOCK]
                    sc = jnp.where(upper, s_p, s[:, BLOCK:])
                else:
                    sc = jnp.where(upper, NEG, s)
                if with_diag:
                    sd = jnp.where(eye, s_p, NEG)
                    m = jnp.max(jnp.maximum(sc, sd), axis=-1, keepdims=True)
                else:
                    m = jnp.max(sc, axis=-1, keepdims=True)
                if with_sinks:
                    sk = jnp.where(first_rows, sink_ref[0, heads[0]], sink_ref[0, heads[1]])
                    m = jnp.maximum(m, sk)
                p = jnp.exp(sc - m)
                zero = jnp.zeros_like(p)
                if with_diag:
                    pd = jnp.exp(sd - m)
                    l = jnp.sum(p + pd, axis=-1, keepdims=True)
                else:
                    pd = zero
                    l = jnp.sum(p, axis=-1, keepdims=True)
                if with_sinks:
                    l = l + jnp.exp(sk - m)
                pf = jnp.where(upper, zero, p)
                if has_prev:
                    pf = jnp.concatenate([jnp.where(upper, p, pd), pf], axis=1)
                probs.append(pf.astype(BF16))
                for n, h in enumerate(heads):
                    rows = slice(n * BLOCK, (n + 1) * BLOCK)
                    ml_blk = jnp.where(lane == h, m[rows], ml_blk)
                    ml_blk = jnp.where(lane == h + STAT_SHIFT, l[rows], ml_blk)
            return probs, ml_blk

        def matmuls_out(g, values, probs, ml_blk):
            for c in chunks:
                sl = slice(c * LANES, (c + 1) * LANES)
                _blk_store(o_ref, sl, _unstack_heads(_dot(probs[c], values[c]), c, gqa), dil, g)
            _blk_store(ml_ref, all_lanes, ml_blk, dil, g)

        def run(blocks):
            ins = [matmuls_in(g, has_prev) for g, has_prev in blocks]
            mids = [tile_ops(has_prev, scores) for (_, has_prev), (scores, _) in zip(blocks, ins)]
            for (g, _), (_, values), (probs, ml_blk) in zip(blocks, ins, mids):
                matmuls_out(g, values, probs, ml_blk)

        second = [(1, True)] if per_step > 1 else []

        @pl.when(pl.program_id(1) == 0)
        def _():
            run([(0, False)] + second)

        @pl.when(pl.program_id(1) > 0)
        def _():
            run([(0, True)] + second)

        if per_step > 2:
            def rest(pair, carry):
                run([(2 * pair, True), (2 * pair + 1, True)])
                return carry

            lax.fori_loop(1, per_step // 2, rest, 0)

    spec = _attn_specs(dil)
    cur = spec(per_step, lambda i: i)
    prev = spec(1, lambda i: jnp.maximum(i * per_step - 1, 0))
    in_specs = [cur(WIDTH), prev(2 * kw), cur(2 * kw)]
    args = [q, kv, kv]
    if with_sinks:
        in_specs.append(pl.BlockSpec(memory_space=pltpu.SMEM))
        args.append(sinks)
    stats = jax.ShapeDtypeStruct(q.shape[:-1] + (LANES,), F32)
    o, ml = pl.pallas_call(
        body, name=name, grid=(n_seq, nb // per_step), in_specs=in_specs,
        out_specs=(cur(WIDTH), cur(LANES)),
        out_shape=(jax.ShapeDtypeStruct(q.shape, BF16), stats),
        compiler_params=pltpu.CompilerParams(dimension_semantics=("arbitrary", "arbitrary"),
                                             vmem_limit_bytes=VMEM_LIMIT),
    )(*args)
    return _attn_unview(o, dil), _attn_unview(ml, dil)


def _attn_bwd(q, kv, do, ld, *, dil, max_dist, name, onto=None):
    q, kv, do, ld = (_attn_view(a, dil) for a in (q, kv, do, ld))
    onto = () if onto is None else tuple(_attn_view(a, dil) for a in onto)
    kw = kv.shape[-1] // 2
    gqa = kw == A_KV_WIDTH
    n_seq = dil
    nb = (q.shape[-2] * (4 if dil == 4 else 1)) // BLOCK
    n_kc = kw // LANES
    per_step = min(ATTN_BLOCKS_PER_STEP, nb)
    all_lanes = slice(0, LANES)
    assert max_dist in (BLOCK - 1, BLOCK) and nb % per_step == 0 and (per_step == 1 or per_step % 2 == 0)
    diag = max_dist == BLOCK

    def body(q_ref, kvp_ref, kvc_ref, do_ref, ld_ref, *rest_refs):
        dq_ref, dkv_ref, ck_ref, cv_ref = rest_refs[len(onto):]
        i = pl.program_id(1)

        def store(ref, sl, val, blk):
            if onto:
                val = val + _blk_load(rest_refs[0 if ref is dq_ref else 1], sl, dil, blk).astype(F32)
            _blk_store(ref, sl, val, dil, blk)

        chunks = range(WIDTH // LANES)

        def matmuls_in(g, has_prev):
            prev_ref, prev_g = (kvp_ref, 0) if (isinstance(g, int) and g == 0) else (kvc_ref, g - 1)
            operands, products = [], []
            for c in chunks:
                sl = slice(c * LANES, (c + 1) * LANES)
                kc = 0 if gqa else c
                ksl = slice(kc * LANES, (kc + 1) * LANES)
                vsl = slice(ksl.start + kw, ksl.stop + kw)
                k2, v2 = _blk_load(kvc_ref, ksl, dil, g), _blk_load(kvc_ref, vsl, dil, g)
                if has_prev:
                    k2 = jnp.concatenate([_blk_load(prev_ref, ksl, dil, prev_g), k2], axis=0)
                    v2 = jnp.concatenate([_blk_load(prev_ref, vsl, dil, prev_g), v2], axis=0)
                qs = _stack_heads(_blk_load(q_ref, sl, dil, g), c, gqa)
                dos = _stack_heads(_blk_load(do_ref, sl, dil, g), c, gqa)
                operands.append((qs, dos, k2))
                products.append((_dot_nt(qs, k2), _dot_nt(dos, v2)))
            return operands, products

        def tile_ops(g, has_prev, products):
            upper, eye = _upper_mask(dil), _eye_mask()
            ld_blk = _blk_load(ld_ref, all_lanes, dil, g)
            weights = []
            for c in chunks:
                heads = _stacked_head_ids(c, gqa)
                lse2 = _per_head_rows(ld_blk, heads)
                dl2 = _per_head_rows(ld_blk, tuple(h + STAT_SHIFT for h in heads))
                s, dp = products[c]
                if has_prev:
                    s_p, dp_p = s[:, :BLOCK], dp[:, :BLOCK]
                    sc = jnp.where(upper, s_p, s[:, BLOCK:])
                    dpc = jnp.where(upper, dp_p, dp[:, BLOCK:])
                else:
                    sc = jnp.where(upper, NEG, s)
                    dpc = dp
                p = jnp.exp(sc - lse2)
                ds = p * (dpc - dl2)
                zero = jnp.zeros_like(p)
                pf = jnp.where(upper, zero, p)
                dsf = jnp.where(upper, zero, ds)
                if has_prev:
                    if diag:
                        pd = jnp.exp(jnp.where(eye, s_p, NEG) - lse2)
                        dsd = pd * (dp_p - dl2)
                    else:
                        pd = dsd = zero
                    pf = jnp.concatenate([jnp.where(upper, p, pd), pf], axis=1)
                    dsf = jnp.concatenate([jnp.where(upper, ds, dsd), dsf], axis=1)
                weights.append((pf.astype(BF16), dsf.astype(BF16)))
            return weights

        def matmuls_out(g, has_prev, operands, weights):
            seq_blk = i * per_step + g
            dk_acc = [None] * n_kc
            dv_acc = [None] * n_kc
            for c in chunks:
                sl = slice(c * LANES, (c + 1) * LANES)
                kc = 0 if gqa else c
                qs, dos, k2 = operands[c]
                pf, dsf = weights[c]
                store(dq_ref, sl, _unstack_heads(_dot(dsf, k2), c, gqa) * SCALE, g)
                dk2 = _dot_tn(dsf, qs)
                dv2 = _dot_tn(pf, dos)
                dk_acc[kc] = dk2 if dk_acc[kc] is None else dk_acc[kc] + dk2
                dv_acc[kc] = dv2 if dv_acc[kc] is None else dv_acc[kc] + dv2
            for kc in range(n_kc):
                sl = slice(kc * LANES, (kc + 1) * LANES)
                vsl = slice(sl.start + kw, sl.stop + kw)
                if has_prev:
                    store(dkv_ref, sl, ck_ref[:, sl] + dk_acc[kc][:BLOCK], seq_blk - 1)
                    store(dkv_ref, vsl, cv_ref[:, sl] + dv_acc[kc][:BLOCK], seq_blk - 1)
                    ck_ref[:, sl] = dk_acc[kc][BLOCK:]
                    cv_ref[:, sl] = dv_acc[kc][BLOCK:]
                else:
                    ck_ref[:, sl] = dk_acc[kc]
                    cv_ref[:, sl] = dv_acc[kc]

        def run(blocks):
            ins = [matmuls_in(g, has_prev) for g, has_prev in blocks]
            mids = [tile_ops(g, has_prev, products) for (g, has_prev), (_, products) in zip(blocks, ins)]
            for (g, has_prev), (operands, _), weights in zip(blocks, ins, mids):
                matmuls_out(g, has_prev, operands, weights)

        second = [(1, True)] if per_step > 1 else []

        @pl.when(i == 0)
        def _():
            run([(0, False)] + second)

        @pl.when(i > 0)
        def _():
            run([(0, True)] + second)

        if per_step > 2:
            def rest(pair, carry):
                run([(2 * pair, True), (2 * pair + 1, True)])
                return carry

            lax.fori_loop(1, per_step // 2, rest, 0)

        @pl.when(i == nb // per_step - 1)
        def _():
            for kc in range(n_kc):
                sl = slice(kc * LANES, (kc + 1) * LANES)
                store(dkv_ref, sl, ck_ref[:, sl], nb - 1)
                store(dkv_ref, slice(sl.start + kw, sl.stop + kw), cv_ref[:, sl], nb - 1)

    spec = _attn_specs(dil)
    cur = spec(per_step, lambda i: i)
    prev = spec(1, lambda i: jnp.maximum(i * per_step - 1, 0))
    if dil == 4:
        whole = pl.BlockSpec((4, None, kv.shape[2], 2 * kw), lambda r, i: (0, r, 0, 0))
    else:
        whole = pl.BlockSpec((None, kv.shape[1], 2 * kw), lambda r, i: (r, 0, 0))
    sds = jax.ShapeDtypeStruct
    dq, dkv = pl.pallas_call(
        body, name=name, grid=(n_seq, nb // per_step),
        in_specs=[cur(WIDTH), prev(2 * kw), cur(2 * kw), cur(WIDTH), cur(LANES)] + [cur(WIDTH), whole][:len(onto)],
        out_specs=(cur(WIDTH), whole),
        out_shape=(sds(q.shape, BF16), sds(kv.shape, BF16)),
        scratch_shapes=[pltpu.VMEM((BLOCK, kw), F32), pltpu.VMEM((BLOCK, kw), F32)],
        compiler_params=pltpu.CompilerParams(dimension_semantics=("arbitrary", "arbitrary"),
                                             vmem_limit_bytes=VMEM_LIMIT),
    )(q, kv, kv, do, ld, *onto)
    return _attn_unview(dq, dil), _attn_unview(dkv, dil)


def _outproj(att_a, att_b1, att_b4, att_b16, g_a, g_b, x2, tgt2, w_out_bf, sink_row, perm):
    s = x2.shape[0]
    tm = ROW_TILE

    def body(oa_ref, mla_ref, ob1_ref, ml1_ref, ob4_ref, ml4_ref, ob16_ref, ml16_ref,
             ga_ref, gb_ref, x_ref, t_ref, w_ref, sink_ref, perm_ref,
             dy_ref, doa_ref, dob_ref, dobf_ref, dga_ref, dgb_ref, lda_ref, ldb_ref, ldbf_ref,
             gw_ref, loss_ref, dsink_ref, scr_st):
        i = pl.program_id(0)
        perm = perm_ref[...]
        lane = lax.broadcasted_iota(jnp.int32, (tm, LANES), 1)
        used = lane < HEADS

        def split(ml):
            return jnp.where(used, ml, 0.0), jnp.where(used, pltpu.roll(ml, LANES - STAT_SHIFT, 1), 1.0)

        @pl.when(i == 0)
        def _():
            gw_ref[...] = jnp.zeros_like(gw_ref)
            loss_ref[...] = jnp.zeros_like(loss_ref)
            dsink_ref[...] = jnp.zeros_like(dsink_ref)

        ms, ls = zip(split(ml1_ref[...]), split(_load_folded(ml4_ref, scr_st)), split(_load_folded(ml16_ref, scr_st)))
        mx = jnp.maximum(jnp.maximum(ms[0], ms[1]), ms[2])
        scale = [jnp.exp(mp - mx) for mp in ms]
        den = (ls[0] * scale[0] + ls[1] * scale[1]) + ls[2] * scale[2]
        lse_b = jnp.where(used, mx + jnp.log(den), 0.0)
        inv_den = 1.0 / den
        o_b = _expand_heads(scale[0] * inv_den) * ob1_ref[...].astype(F32)
        o_b = o_b + _expand_heads(scale[1] * inv_den) * _load_folded_bf16(ob4_ref, perm)
        o_b = o_b + _expand_heads(scale[2] * inv_den) * _load_folded_bf16(ob16_ref, perm)
        m_a, l_a = split(mla_ref[...])
        lse_a = jnp.where(used, m_a + jnp.log(l_a), 0.0)
        o_a = _expand_heads(1.0 / l_a) * oa_ref[...].astype(F32)
        g_a = ga_ref[...].astype(F32)
        g_b = gb_ref[...].astype(F32)
        sg_a = _sigmoid(g_a)
        sg_b = _sigmoid(g_b)
        silu_a = g_a * sg_a
        silu_b = g_b * sg_b
        mixed = jnp.concatenate([o_a * silu_a, o_b * silu_b], axis=1).astype(BF16)
        w = w_ref[...]
        y = x_ref[...] + _dot(mixed, w)
        diff = y - t_ref[...]
        loss_ref[...] += (0.5 / D_MODEL) * jnp.sum(diff * diff)
        dy = diff * (1.0 / D_MODEL)
        dy_ref[...] = dy
        dyb = dy.astype(BF16)
        gw_ref[...] += _dot_tn(mixed, dyb)
        dmixed = _dot_nt(dyb, w)
        dm_a = dmixed[:, :WIDTH]
        dm_b = dmixed[:, WIDTH:]
        do_a = dm_a * silu_a
        do_b = dm_b * silu_b
        doa_ref[...] = do_a.astype(BF16)
        dob_ref[...] = do_b.astype(BF16)
        _store_folded_bf16(dobf_ref, do_b, perm)
        dga_ref[...] = (dm_a * o_a * (sg_a * (1.0 + g_a * (1.0 - sg_a)))).astype(BF16)
        dgb_ref[...] = (dm_b * o_b * (sg_b * (1.0 + g_b * (1.0 - sg_b)))).astype(BF16)
        dl_a = _reduce_heads(do_a * o_a)
        dl_b = _reduce_heads(do_b * o_b)
        lda_ref[...] = lse_a + pltpu.roll(dl_a, STAT_SHIFT, 1)
        ld_b = lse_b + pltpu.roll(dl_b, STAT_SHIFT, 1)
        ldb_ref[...] = ld_b
        _store_folded(ldbf_ref, ld_b, scr_st)
        dsink_ref[...] -= jnp.sum(jnp.exp(sink_ref[...] - lse_a) * dl_a, axis=0, keepdims=True)

    sds = jax.ShapeDtypeStruct
    ln = s // FOLD
    natural = [_rows(WIDTH), _rows(LANES)]
    folded = [_folded_rows(WIDTH), _folded_rows(LANES)]
    return pl.pallas_call(
        body, name="outproj_fwd_bwd", grid=(s // tm,),
        in_specs=natural + natural + folded + folded
                 + [_rows(WIDTH), _rows(WIDTH), _rows(D_MODEL), _rows(D_MODEL), _whole((D_MODEL, D_MODEL)),
                    _whole((1, LANES)), _whole(perm.shape)],
        out_specs=(_rows(D_MODEL), _rows(WIDTH), _rows(WIDTH), _folded_rows(WIDTH), _rows(WIDTH), _rows(WIDTH),
                   _rows(LANES), _rows(LANES), _folded_rows(LANES),
                   _whole((D_MODEL, D_MODEL)), _whole((1, LANES)), _whole((1, LANES))),
        out_shape=(sds((s, D_MODEL), F32), sds((s, WIDTH), BF16), sds((s, WIDTH), BF16),
                   sds((FOLD, ln, WIDTH), BF16), sds((s, WIDTH), BF16), sds((s, WIDTH), BF16),
                   sds((s, LANES), F32), sds((s, LANES), F32), sds((FOLD, ln, LANES), F32),
                   sds((D_MODEL, D_MODEL), F32), sds((1, LANES), F32), sds((1, LANES), F32)),
        scratch_shapes=[_fold_scratch(LANES)],
        compiler_params=pltpu.CompilerParams(dimension_semantics=("arbitrary",), vmem_limit_bytes=VMEM_LIMIT),
    )(*att_a, *att_b1, *att_b4, *att_b16, g_a, g_b, x2, tgt2, w_out_bf, sink_row, perm)


def _inproj_bwd(x2, dy, gain, w_bf, cos, sin_s, gqa, gka, gqb, gkb, bd256, bd128, perm,
                qa_raw, ka_raw, qb_raw, kb_raw, d_a, d_b1, d_bf, dg_a, dg_b):
    s = x2.shape[0]
    tm = ROW_TILE

    def body(x_ref, dy_ref, gain_ref, w_hbm, cos_ref, sin_ref, gqa_ref, gka_ref, gqb_ref, gkb_ref, bd256_ref,
             bd128_ref, perm_ref, qa_raw_ref, ka_raw_ref, qb_raw_ref, kb_raw_ref, dqa_ref, dkva_ref,
             dq1_ref, dkv1_ref, dqf_ref, dkvf_ref, dga_ref, dgb_ref,
             gx_ref, ht_ref, win_ref,
             dgain_ref, dgqa_ref, dgka_ref, dgqb_ref, dgkb_ref, w_vmem, dproj_ref):
        i = pl.program_id(0)
        perm = perm_ref[...]

        @pl.when(i == 0)
        def _():
            pltpu.sync_copy(w_hbm, w_vmem)
            dgain_ref[...] = jnp.zeros_like(dgain_ref)
            dgqa_ref[...] = jnp.zeros_like(dgqa_ref)
            dgka_ref[...] = jnp.zeros_like(dgka_ref)
            dgqb_ref[...] = jnp.zeros_like(dgqb_ref)
            dgkb_ref[...] = jnp.zeros_like(dgkb_ref)

        cos1 = cos_ref[...]
        sin1 = sin_ref[...]
        cos4 = jnp.tile(cos1, (1, 4))
        sin4 = jnp.tile(sin1, (1, 4))

        dt, dg = _qknorm_rope_bwd(dqa_ref[...], qa_raw_ref[...], gqa_ref[...], cos4, sin4, bd256_ref[...])
        dproj_ref[:, C_QA:C_KA] = dt.astype(BF16)
        dgqa_ref[...] += jnp.sum(dg, axis=0, keepdims=True)
        dt, dg = _qknorm_rope_bwd(dkva_ref[:, :A_KV_WIDTH], ka_raw_ref[...], gka_ref[...], cos1, sin1,
                                  bd128_ref[...])
        dproj_ref[:, C_KA:C_VA] = dt.astype(BF16)
        dgka_ref[...] += jnp.sum(dg, axis=0, keepdims=True)
        dproj_ref[:, C_VA:C_GA] = dkva_ref[:, A_KV_WIDTH:]
        dproj_ref[:, C_GA:C_QB] = dga_ref[...]
        dq = dq1_ref[...].astype(F32) + _load_folded_bf16(dqf_ref, perm)
        dt, dg = _qknorm_rope_bwd(dq, qb_raw_ref[...], gqb_ref[...], cos4, sin4, bd256_ref[...])
        dproj_ref[:, C_QB:C_KB] = dt.astype(BF16)
        dgqb_ref[...] += jnp.sum(dg, axis=0, keepdims=True)
        dkv = dkv1_ref[...].astype(F32) + _load_folded_bf16(dkvf_ref, perm)
        dt, dg = _qknorm_rope_bwd(dkv[:, :WIDTH], kb_raw_ref[...], gkb_ref[...], cos4, sin4, bd256_ref[...])
        dproj_ref[:, C_KB:C_VB] = dt.astype(BF16)
        dgkb_ref[...] += jnp.sum(dg, axis=0, keepdims=True)
        dproj_ref[:, C_VB:C_GB] = dkv[:, WIDTH:].astype(BF16)
        dproj_ref[:, C_GB:C_END] = dgb_ref[...]
        for k, start in enumerate(WIN_START):
            win_ref[k] = dproj_ref[:, start:start + WIN]

        xt = x_ref[...]
        gain_row = gain_ref[...]
        r = lax.rsqrt(jnp.mean(xt * xt, axis=-1, keepdims=True) + EPS)
        xr = xt * r
        ht_ref[...] = (xr * gain_row).T.astype(BF16)
        dh = _dot(dproj_ref[...], w_vmem[...])
        dgain_ref[...] += jnp.sum(dh * xr, axis=0, keepdims=True)
        u = dh * gain_row
        gx_ref[...] = dy_ref[...] + r * (u - xr * jnp.mean(u * xr, axis=-1, keepdims=True))

    def acc_row(w):
        return pl.BlockSpec((1, w), lambda i: (0, 0))

    sds = jax.ShapeDtypeStruct
    any_spec = pl.BlockSpec(memory_space=pl.ANY)
    win_spec = pl.BlockSpec((N_CHIP, tm, WIN), lambda i: (0, i, 0))
    return pl.pallas_call(
        body, name="inproj_bwd", grid=(s // tm,),
        in_specs=[_rows(D_MODEL), _rows(D_MODEL), _whole(gain.shape), any_spec, _rows(LANES), _rows(LANES),
                  _whole(gqa.shape), _whole(gka.shape), _whole(gqb.shape), _whole(gkb.shape), _whole(bd256.shape),
                  _whole(bd128.shape), _whole(perm.shape),
                  _rows(WIDTH), _rows(A_KV_WIDTH), _rows(WIDTH), _rows(WIDTH),
                  _rows(WIDTH), _rows(2 * A_KV_WIDTH), _rows(WIDTH), _rows(2 * WIDTH)]
                 + [_folded_rows(WIDTH), _folded_rows(2 * WIDTH), _rows(WIDTH), _rows(WIDTH)],
        out_specs=(_rows(D_MODEL), pl.BlockSpec((D_MODEL, tm), lambda i: (0, i)), win_spec, acc_row(D_MODEL), acc_row(WIDTH), acc_row(A_KV_WIDTH), acc_row(WIDTH), acc_row(WIDTH)),
        out_shape=(sds((s, D_MODEL), F32), sds((D_MODEL, s), BF16), sds((N_CHIP, s, WIN), BF16),
                   sds((1, D_MODEL), F32),
                   sds((1, WIDTH), F32), sds((1, A_KV_WIDTH), F32), sds((1, WIDTH), F32), sds((1, WIDTH), F32)),
        scratch_shapes=[pltpu.VMEM((IN_WIDTH, D_MODEL), BF16), pltpu.VMEM((tm, IN_WIDTH), BF16)],
        compiler_params=pltpu.CompilerParams(dimension_semantics=("arbitrary",), vmem_limit_bytes=VMEM_LIMIT),
    )(x2, dy, gain, w_bf, cos, sin_s, gqa, gka, gqb, gkb, bd256, bd128, perm, qa_raw, ka_raw, qb_raw, kb_raw,
      *d_a, *d_b1, *d_bf, dg_a, dg_b)


def _rope_angles(s):
    half = HEAD_DIM // 2
    f32 = np.float32
    inv = np.tile((f32(ROPE_THETA) ** (-np.arange(half, dtype=f32) / f32(half))).astype(f32), 4)
    sign = np.tile(np.concatenate([-np.ones((half,), f32), np.ones((half,), f32)]), 2)
    hi = (np.arange(s // ROPE_SPLIT) * ROPE_SPLIT).astype(f32)[:, None] * inv[None, :]
    lo = np.arange(ROPE_SPLIT).astype(f32)[:, None] * inv[None, :]
    return tuple(jnp.asarray(t.astype(f32)) for t in (np.cos(hi), np.sin(hi), np.cos(lo), np.sin(lo), sign[None, :]))


def _table_shapes(s):
    return (jax.ShapeDtypeStruct((s, LANES), F32), jax.ShapeDtypeStruct((s, LANES), F32),
            jax.ShapeDtypeStruct((2 * LANES, 2 * LANES), BF16), jax.ShapeDtypeStruct((A_KV_WIDTH, A_KV_WIDTH), BF16),
            jax.ShapeDtypeStruct((ROW_TILE, ROW_TILE), BF16)) + tuple(
                jax.ShapeDtypeStruct((1, w), F32) for w in GAIN_WIDTHS)


GAIN_WIDTHS = (WIDTH, A_KV_WIDTH, WIDTH, WIDTH, LANES)


def _tables(in_refs, out_refs):
    ch_ref, sh_ref, cl_ref, sl_ref, sign_ref = in_refs[:5]
    cos_ref, sin_ref, bd256_ref, bd128_ref, perm_ref = out_refs[:5]
    for g_ref, ref in zip(in_refs[5:], out_refs[5:]):
        g = g_ref[...]
        fill = g if g.shape[1] == HEAD_DIM else jnp.zeros((1, LANES - g.shape[1]), F32)
        ref[...] = jnp.concatenate([jnp.concatenate([g, fill], axis=1)] * (ref.shape[1] // LANES), axis=1)
    cl, sl, sign = cl_ref[...], sl_ref[...], sign_ref[...]

    def tile(a, carry):
        rows = pl.ds(pl.multiple_of(a * ROPE_SPLIT, ROPE_SPLIT), ROPE_SPLIT)
        ch, sh = ch_ref[pl.ds(a, 1), :], sh_ref[pl.ds(a, 1), :]
        cos_ref[rows, :] = ch * cl - sh * sl
        sin_ref[rows, :] = (sh * cl + ch * sl) * sign
        return carry

    lax.fori_loop(0, cos_ref.shape[0] // ROPE_SPLIT, tile, 0)
    for ref in (bd256_ref, bd128_ref):
        head = [lax.broadcasted_iota(jnp.int32, ref.shape, d) // HEAD_DIM for d in (0, 1)]
        ref[...] = jnp.where(head[0] == head[1], 1.0, 0.0).astype(BF16)
    f = lax.broadcasted_iota(jnp.int32, perm_ref.shape, 0)
    col = lax.broadcasted_iota(jnp.int32, perm_ref.shape, 1)
    perm_ref[...] = jnp.where(col == FOLD * (f % FOLD_ROWS) + f // FOLD_ROWS, 1.0, 0.0).astype(BF16)


def _local_step(x2, tgt2, norm_gain, w_in_bf, q_norm_a, k_norm_a, sinks_a, q_norm_b, k_norm_b, w_out_blk, tables):
    cos, sin_s, bd256, bd128, perm, gqa, gka, gqb, gkb, sink_row = tables

    (qa, kva, qb, kvb, qbf, kvbf, qa_raw, ka_raw, g_a, qb_raw, kb_raw, g_b, w_out_bf) = _inproj(
        x2, norm_gain, w_in_bf, cos, sin_s, gqa, gka, gqb, gkb, bd256, bd128, w_out_blk)

    att_a = _attn_fwd(qa, kva, sinks_a, dil=1, max_dist=A_MAX_DIST, name="attn_a_fwd")
    att_b1 = _attn_fwd(qb, kvb, None, dil=1, max_dist=B_MAX_DIST, name="attn_b1_fwd")
    att_b4 = _attn_fwd(qbf, kvbf, None, dil=4, max_dist=B_MAX_DIST, name="attn_b4_fwd")
    att_b16 = _attn_fwd(qbf, kvbf, None, dil=16, max_dist=B_MAX_DIST, name="attn_b16_fwd")

    (dy, do_a, do_b, do_bf, dg_a, dg_b, ld_a, ld_b, ld_bf, gw_out, loss_part, dsink) = _outproj(
        att_a, att_b1, att_b4, att_b16, g_a, g_b, x2, tgt2, w_out_bf, sink_row, perm)

    d_a = _attn_bwd(qa, kva, do_a, ld_a, dil=1, max_dist=A_MAX_DIST, name="attn_a_bwd")
    d_b1 = _attn_bwd(qb, kvb, do_b, ld_b, dil=1, max_dist=B_MAX_DIST, name="attn_b1_bwd")
    d_b4 = _attn_bwd(qbf, kvbf, do_bf, ld_bf, dil=4, max_dist=B_MAX_DIST, name="attn_b4_bwd")
    d_b16 = _attn_bwd(qbf, kvbf, do_bf, ld_bf, dil=16, max_dist=B_MAX_DIST, name="attn_b16_bwd", onto=d_b4)

    gx, h_t, wins, dgain, dgqa, dgka, dgqb, dgkb = _inproj_bwd(
        x2, dy, norm_gain, w_in_bf, cos, sin_s, gqa, gka, gqb, gkb, bd256, bd128, perm,
        qa_raw, ka_raw, qb_raw, kb_raw, d_a, d_b1, d_b16, dg_a, dg_b)
    return loss_part, gx, h_t, wins, gw_out, (dgain, dgqa, dgka, dsink, dgqb, dgkb)


def _position():
    return lax.axis_index("x"), lax.axis_index("y"), lax.axis_index("c")


GATHER_CHUNKS = 2


def _gather_weights(blocks, name, side_inputs, side_shapes, side_work):
    n = len(blocks)
    n_in, n_out = len(side_inputs), len(side_shapes)
    ch = GATHER_CHUNKS

    def body(*refs):
        src_refs, side_in = refs[:n], refs[n:n + n_in]
        dst_refs, side_out = refs[n + n_in:2 * n + n_in], refs[2 * n + n_in:2 * n + n_in + n_out]
        ici_send, ici_recv, hop_send, hop_recv, d2d_send, d2d_recv = refs[2 * n + n_in + n_out:]
        x, y, c = _position()
        b = 2 * x + y
        via = 2 - c
        out = 3 - via
        for k in range(n):
            dst_refs[k][b] = src_refs[k][...].astype(BF16)

        def rows(k, core, j):
            half = blocks[k].shape[0] // 2
            return pl.ds(pl.multiple_of(core * half + j * (half // ch), half // ch), half // ch)

        def chip(rel):
            return x ^ (rel >> 1), y ^ (rel & 1)

        def ici(k, j, slot, rel, send_sems, recv_sems, sem):
            px, py = chip(rel)
            piece = dst_refs[k].at[slot, rows(k, c, j)]
            return pltpu.make_async_remote_copy(src_ref=piece, dst_ref=piece, send_sem=send_sems.at[sem],
                                                recv_sem=recv_sems.at[sem], device_id=(px, py, c),
                                                device_id_type=MESH)

        def direct(k, j, slot, rel):
            return ici(k, j, slot, rel, ici_send, ici_recv, ((rel - 1) * ch + j) * n + k)

        def hop(k, j, slot, rel):
            return ici(k, j, slot, rel, hop_send, hop_recv, j * n + k)

        def d2d(k, j, rel, core):
            piece = dst_refs[k].at[b ^ rel, rows(k, core, j)]
            sem = ((rel - 1) * ch + j) * n + k
            return pltpu.make_async_remote_copy(src_ref=piece, dst_ref=piece, send_sem=d2d_send.at[sem],
                                                recv_sem=d2d_recv.at[sem], device_id=(x, y, 1 - c),
                                                device_id_type=MESH)

        pieces = [(k, j) for j in range(ch) for k in range(n)]
        for k, j in pieces:
            for rel in (1, 2):
                direct(k, j, b, rel).start()
        for k, j in pieces:
            direct(k, j, b ^ via, via).wait_recv()
            hop(k, j, b ^ via, out).start()
            d2d(k, j, via, c).start()
        side_work(side_in, side_out)
        for k, j in pieces:
            direct(k, j, b ^ out, out).wait_recv()
            d2d(k, j, out, c).start()
        for k, j in pieces:
            hop(k, j, b ^ 3, via).wait_recv()
            d2d(k, j, 3, c).start()
        for k, j in pieces:
            for rel in (1, 2, 3):
                d2d(k, j, rel, 1 - c).wait_recv()
        for k, j in pieces:
            for rel in (1, 2):
                direct(k, j, b, rel).wait_send()
            hop(k, j, b ^ via, out).wait_send()
            d2d(k, j, via, c).wait_send()
            d2d(k, j, out, c).wait_send()
            d2d(k, j, 3, c).wait_send()

    vmem_spec = pl.BlockSpec(memory_space=pltpu.VMEM)
    dma = pltpu.SemaphoreType.DMA
    out_shape = tuple(jax.ShapeDtypeStruct((N_CHIP,) + a.shape, BF16) for a in blocks) + tuple(side_shapes)
    outs = pl.pallas_call(
        body, name=name, in_specs=[vmem_spec] * (n + n_in), out_specs=tuple([vmem_spec] * (n + n_out)),
        out_shape=out_shape,
        scratch_shapes=[dma((2 * ch * n,)), dma((2 * ch * n,)), dma((ch * n,)), dma((ch * n,)),
                        dma((3 * ch * n,)), dma((3 * ch * n,))],
        compiler_params=pltpu.CompilerParams(vmem_limit_bytes=VMEM_LIMIT),
    )(*blocks, *side_inputs)
    return outs[:n], outs[n:]


def _grad_reduce(order, h_t, wins, gw_out, small):
    s = h_t.shape[1]
    tk = GRAD_ROWS
    n_i = s // tk
    half = D_MODEL // 2
    o_half = OUT_ROWS // 2
    n_rel = N_CHIP - 1

    def body(order_ref, ht_ref, win_ref, gwo_ref, small_ref,
             win_out, wout_out, small_out,
             acc, mine, s1, r1, s2, r2, so1, ro1, so2, ro2, pair_in, pair_o, small_land,
             s1_send, s1_recv, s2_send, s2_recv, o1_send, o1_recv, o2_send, o2_recv,
             pair_send, pair_recv, small_send, small_recv):
        j = pl.program_id(0)
        i = pl.program_id(1)
        x, y, c = _position()
        me = 4 * x + 2 * y + c
        sibling = (x, y, 1 - c)
        my_rows = pl.ds(pl.multiple_of(c * half, half), half)
        sib_rows = pl.ds(pl.multiple_of((1 - c) * half, half), half)

        def chip_of(rel):
            return x ^ (rel >> 1), y ^ (rel & 1)

        def level1(k):
            return pltpu.make_async_remote_copy(src_ref=s1.at[k], dst_ref=r1.at[k], send_sem=s1_send.at[k],
                                                recv_sem=s1_recv.at[k], device_id=sibling, device_id_type=MESH)

        def level2(k):
            px, py = chip_of(RELATIONS[k])
            return pltpu.make_async_remote_copy(src_ref=s2.at[k], dst_ref=r2.at[k], send_sem=s2_send.at[k],
                                                recv_sem=s2_recv.at[k], device_id=(px, py, c), device_id_type=MESH)

        def out_level1(bk):
            return pltpu.make_async_remote_copy(src_ref=so1.at[bk], dst_ref=ro1.at[bk], send_sem=o1_send.at[bk],
                                                recv_sem=o1_recv.at[bk], device_id=sibling, device_id_type=MESH)

        def out_level2(k):
            px, py = chip_of(RELATIONS[k])
            return pltpu.make_async_remote_copy(src_ref=so2.at[k], dst_ref=ro2.at[k], send_sem=o2_send.at[k],
                                                recv_sem=o2_recv.at[k], device_id=(px, py, c), device_id_type=MESH)

        def small_copy(d):
            px, py, pc = x ^ (d >> 2), y ^ ((d >> 1) & 1), c ^ (d & 1)
            return pltpu.make_async_remote_copy(src_ref=small_ref, dst_ref=small_land.at[me],
                                                send_sem=small_send.at[d], recv_sem=small_recv.at[d],
                                                device_id=(px, py, pc), device_id_type=MESH)

        def pair_copy(k, buf):
            return pltpu.make_async_remote_copy(src_ref=buf.at[0], dst_ref=buf.at[1], send_sem=pair_send.at[k],
                                                recv_sem=pair_recv.at[k], device_id=sibling, device_id_type=MESH)

        def out_rows(bk, core):
            return pl.ds(pl.multiple_of(bk * OUT_ROWS + core * o_half, o_half), o_half)

        @pl.when((j == 0) & (i == 0))
        def _():
            for d in range(1, N_DEV):
                small_copy(d).start()
            small_land[me] = small_ref[...]
            for bk in range(N_CHIP):
                so1[bk] = gwo_ref[out_rows(bk, 1 - c), :].astype(BF16)
                out_level1(bk).start()

        @pl.when((j == 0) & (i == 1))
        def _():
            b = 2 * x + y
            for bk in range(N_CHIP):
                out_level1(bk).wait_recv()
            for k in range(n_rel):
                px, py = chip_of(RELATIONS[k])
                bk = 2 * px + py
                so2[k] = (gwo_ref[out_rows(bk, c), :] + ro1[bk].astype(F32)).astype(BF16)
                out_level2(k).start()

        @pl.when(i == 0)
        def _():
            acc[...] = jnp.zeros_like(acc)

        for n0 in range(0, WIN, ACC_COLS):
            n1 = min(n0 + ACC_COLS, WIN)
            acc[:, n0:n1] += _dot(ht_ref[...], win_ref[:, n0:n1])

        for k in range(N_CHIP):
            @pl.when((j == k) & (i == n_i - 1))
            def _(k=k):
                s1[k] = acc[sib_rows, :].astype(BF16)
                level1(k).start()
                mine[...] = acc[my_rows, :]

            if k < n_rel:
                @pl.when((j == k + 1) & (i == 1))
                def _(k=k):
                    level1(k).wait_recv()
                    s2[k] = (mine[...] + r1[k].astype(F32)).astype(BF16)
                    level2(k).start()

        @pl.when((j == N_CHIP - 1) & (i == n_i - 1))
        def _():
            b = 2 * x + y
            level1(N_CHIP - 1).wait_recv()
            total = mine[...] + r1[N_CHIP - 1].astype(F32)
            for k in range(n_rel):
                level2(k).wait_recv()
                total = total + r2[k].astype(F32)
            total = total.T
            pair_in[0] = total
            pair_copy(0, pair_in).start()
            total_o = gwo_ref[out_rows(b, c), :] + ro1[b].astype(F32)
            for k in range(n_rel):
                out_level2(k).wait_recv()
                total_o = total_o + ro2[k].astype(F32)
            pair_o[0] = total_o
            pair_copy(1, pair_o).start()
            for core in range(2):
                @pl.when(c == core)
                def _(core=core):
                    win_out[:, core * half:(core + 1) * half] = total
            wout_out[c] = total_o
            for d in range(1, N_DEV):
                small_copy(d).wait_recv()
            small_out[...] = small_land[...]
            pair_copy(0, pair_in).wait_recv()
            for core in range(2):
                @pl.when(c == core)
                def _(core=core):
                    win_out[:, (1 - core) * half:(2 - core) * half] = pair_in[1]
            pair_copy(1, pair_o).wait_recv()
            wout_out[1 - c] = pair_o[1]
            for d in range(1, N_DEV):
                small_copy(d).wait_send()
            for k in range(N_CHIP):
                level1(k).wait_send()
                out_level1(k).wait_send()
            for k in range(n_rel):
                level2(k).wait_send()
                out_level2(k).wait_send()
            pair_copy(0, pair_in).wait_send()
            pair_copy(1, pair_o).wait_send()

    vmem = pl.BlockSpec(memory_space=pltpu.VMEM)
    dma = pltpu.SemaphoreType.DMA
    sds = jax.ShapeDtypeStruct
    grid_spec = pltpu.PrefetchScalarGridSpec(
        num_scalar_prefetch=1, grid=(N_CHIP, n_i),
        in_specs=[pl.BlockSpec((D_MODEL, tk), lambda j, i, order: (0, i)),
                  pl.BlockSpec((None, tk, WIN), lambda j, i, order: (order[j], i, 0)), vmem, vmem],
        out_specs=(vmem, vmem, vmem),
        scratch_shapes=[
            pltpu.VMEM((D_MODEL, WIN), F32), pltpu.VMEM((half, WIN), F32),
            pltpu.VMEM((N_CHIP, half, WIN), BF16), pltpu.VMEM((N_CHIP, half, WIN), BF16),
            pltpu.VMEM((n_rel, half, WIN), BF16), pltpu.VMEM((n_rel, half, WIN), BF16),
            pltpu.VMEM((N_CHIP, o_half, D_MODEL), BF16), pltpu.VMEM((N_CHIP, o_half, D_MODEL), BF16),
            pltpu.VMEM((n_rel, o_half, D_MODEL), BF16), pltpu.VMEM((n_rel, o_half, D_MODEL), BF16),
            pltpu.VMEM((2, WIN, half), F32), pltpu.VMEM((2, o_half, D_MODEL), F32),
            pltpu.VMEM((N_DEV, PACK_ROWS, D_MODEL), F32),
            dma((N_CHIP,)), dma((N_CHIP,)), dma((n_rel,)), dma((n_rel,)),
            dma((N_CHIP,)), dma((N_CHIP,)), dma((n_rel,)), dma((n_rel,)),
            dma((2,)), dma((2,)), dma((N_DEV,)), dma((N_DEV,))])
    return pl.pallas_call(
        body, name="grad_w_in_reduce", grid_spec=grid_spec,
        out_shape=(sds((WIN, D_MODEL), F32), sds((2, o_half, D_MODEL), F32), sds((N_DEV, PACK_ROWS, D_MODEL), F32)),
        compiler_params=pltpu.CompilerParams(dimension_semantics=("arbitrary", "arbitrary"),
                                             vmem_limit_bytes=VMEM_LIMIT),
    )(order, h_t, wins, gw_out, small)


ADAM_STEPS = 4


def _adamw_math(w, g, m, v):
    m = ADAM_B1 * m + (1.0 - ADAM_B1) * g
    v = ADAM_B2 * v + (1.0 - ADAM_B2) * (g * g)
    m_hat = m / (1.0 - ADAM_B1 ** ADAM_STEP)
    v_hat = v / (1.0 - ADAM_B2 ** ADAM_STEP)
    delta = -ADAM_LR * (m_hat / (jnp.sqrt(v_hat) + ADAM_EPS) + ADAM_WD * w)
    return delta, m, v


def _adamw(w, g, m, v, name):
    r, c = w.shape

    def body(w_ref, g_ref, m_ref, v_ref, d_ref, nm_ref, nv_ref):
        delta, nm, nv = _adamw_math(w_ref[...], g_ref[...], m_ref[...], v_ref[...])
        d_ref[...] = delta
        nm_ref[...] = nm
        nv_ref[...] = nv

    rows = r // ADAM_STEPS
    assert rows * ADAM_STEPS == r and rows % 8 == 0
    spec = pl.BlockSpec((rows, c), lambda i: (i, 0))
    shape = jax.ShapeDtypeStruct((r, c), F32)
    return pl.pallas_call(
        body, name=name, grid=(ADAM_STEPS,), in_specs=[spec] * 4, out_specs=(spec,) * 3,
        out_shape=(shape,) * 3, compiler_params=pltpu.CompilerParams(vmem_limit_bytes=VMEM_LIMIT),
    )(w, g, m, v)


def _adamw_window(w, window, shift, m, v, name):
    r, c = w.shape
    rows = r // ADAM_STEPS
    assert rows * ADAM_STEPS == r and rows % 8 == 0

    def body(shift_ref, w_ref, win_hbm, m_ref, v_ref, g_ref, d_ref, nm_ref, nv_ref, g_vmem):
        start = pl.multiple_of(shift_ref[0] + pl.program_id(0) * rows, 8)
        pltpu.sync_copy(win_hbm.at[pl.ds(start, rows)], g_vmem)
        g = g_vmem[...]
        g_ref[...] = g
        delta, nm, nv = _adamw_math(w_ref[...], g, m_ref[...], v_ref[...])
        d_ref[...] = delta
        nm_ref[...] = nm
        nv_ref[...] = nv

    spec = pl.BlockSpec((rows, c), lambda i, shift_ref: (i, 0))
    shape = jax.ShapeDtypeStruct((r, c), F32)
    grid_spec = pltpu.PrefetchScalarGridSpec(
        num_scalar_prefetch=1, grid=(ADAM_STEPS,),
        in_specs=[spec, pl.BlockSpec(memory_space=pl.ANY), spec, spec], out_specs=(spec,) * 4,
        scratch_shapes=[pltpu.VMEM((rows, c), F32)])
    return pl.pallas_call(
        body, name=name, grid_spec=grid_spec, out_shape=(shape,) * 4,
        compiler_params=pltpu.CompilerParams(vmem_limit_bytes=VMEM_LIMIT),
    )(shift, w, window, m, v)


PACK_ROWS = 8


def _fold_heads(v):
    y = v[:, 0:LANES]
    for j in range(1, v.shape[1] // LANES):
        y = y + v[:, j * LANES:(j + 1) * LANES]
    return y + pltpu.roll(y, HEAD_DIM, 1)


N_SMALL = 6


def _small_adamw(recv, weights, m, v):
    def body(*refs):
        r_ref = refs[0]
        w_refs, m_refs, v_refs = (refs[1 + n * N_SMALL:1 + (n + 1) * N_SMALL] for n in range(3))
        outs = refs[1 + 3 * N_SMALL:]
        g_refs, d_refs, nm_refs, nv_refs = (outs[n * N_SMALL:(n + 1) * N_SMALL] for n in range(4))
        loss_ref = outs[4 * N_SMALL]
        tot = r_ref[0]
        for j in range(1, N_DEV):
            tot = tot + r_ref[j]
        loss_ref[...] = tot[3:4, 0:LANES]
        row1 = tot[1:2, :]
        row2 = tot[2:3, :]
        grads = [tot[0:1, :],
                 _fold_heads(row1[:, 0:WIDTH])[:, :HEAD_DIM],
                 _fold_heads(row2[:, WIDTH:WIDTH + A_KV_WIDTH])[:, :HEAD_DIM],
                 row2[:, WIDTH + A_KV_WIDTH:WIDTH + A_KV_WIDTH + HEADS],
                 _fold_heads(row1[:, WIDTH:2 * WIDTH])[:, :HEAD_DIM],
                 _fold_heads(row2[:, 0:WIDTH])[:, :HEAD_DIM]]
        for n, g in enumerate(grads):
            g_refs[n][...] = g
            delta, nm, nv = _adamw_math(w_refs[n][...], g, m_refs[n][...], v_refs[n][...])
            d_refs[n][...] = delta
            nm_refs[n][...] = nm
            nv_refs[n][...] = nv

    shapes = tuple(jax.ShapeDtypeStruct(a.shape, F32) for a in weights)
    outs = pl.pallas_call(body, name="small_adamw", out_shape=shapes * 4 + (jax.ShapeDtypeStruct((1, LANES), F32),)
                          )(recv, *weights, *m, *v)
    return tuple(outs[n * N_SMALL:(n + 1) * N_SMALL] for n in range(4)) + (outs[4 * N_SMALL],)


def kernel(x, norm_gain, w_in, q_norm_a, k_norm_a, sinks_a, q_norm_b, k_norm_b, w_out, loss_target, m_norm_gain, m_w_in, m_q_norm_a, m_k_norm_a, m_sinks_a, m_q_norm_b, m_k_norm_b, m_w_out, v_norm_gain, v_w_in, v_q_norm_a, v_k_norm_a, v_sinks_a, v_q_norm_b, v_k_norm_b, v_w_out):
    chip = 2 * lax.axis_index("x") + lax.axis_index("y")

    w_in_t, m_w_in_t, v_w_in_t = w_in[0].T, m_w_in[0].T, v_w_in[0].T

    s = x.shape[1]
    side_inputs = _rope_angles(s) + (q_norm_a, k_norm_a, q_norm_b, k_norm_b, sinks_a)
    (w_in_all,), tables = _gather_weights([w_in_t], "gather_weights", side_inputs, _table_shapes(s), _tables)
    w_in_bf = w_in_all.reshape(IN_WIDTH, D_MODEL)

    loss_part, gx, h_t, wins, gw_out, (dgain, dgqa, dgka, dsink, dgqb, dgkb) = _local_step(
        x[0], loss_target[0], norm_gain, w_in_bf, q_norm_a, k_norm_a, sinks_a, q_norm_b, k_norm_b, w_out[0], tables)

    small = jnp.concatenate([
        dgain, jnp.concatenate([dgqa, dgqb], axis=1),
        jnp.concatenate([dgkb, dgka, dsink, jnp.zeros((1, D_MODEL - WIDTH - 2 * A_KV_WIDTH), F32)], axis=1),
        jnp.pad(loss_part, ((0, 0), (0, D_MODEL - LANES))),
        jnp.zeros((PACK_ROWS - 4, D_MODEL), F32)], axis=0)
    order = (chip ^ jnp.array(RELATIONS, jnp.int32)).astype(jnp.int32)
    win_sum, wout_sum, small_recv = _grad_reduce(order, h_t, wins, gw_out, small)
    shift = jnp.array(WIN_SHIFT, jnp.int32)[chip].reshape(1)
    g_w_out = wout_sum.reshape(OUT_ROWS, D_MODEL)

    g_w_in, d_w_in, nm_w_in, nv_w_in = (
        a.T for a in _adamw_window(w_in_t, win_sum, shift, m_w_in_t, v_w_in_t, "adamw_w_in"))
    d_w_out, nm_w_out, nv_w_out = _adamw(w_out[0], g_w_out, m_w_out[0], v_w_out[0], "adamw_w_out")
    g_s, d_s, nm_s, nv_s, loss_row = _small_adamw(
        small_recv,
        (norm_gain, q_norm_a, k_norm_a, sinks_a, q_norm_b, k_norm_b),
        (m_norm_gain, m_q_norm_a, m_k_norm_a, m_sinks_a, m_q_norm_b, m_k_norm_b),
        (v_norm_gain, v_q_norm_a, v_k_norm_a, v_sinks_a, v_q_norm_b, v_k_norm_b))
    loss = loss_row[0, 0]

    def leaves(small_ones, big_in, big_out):
        return (small_ones[0], big_in[None]) + tuple(small_ones[1:]) + (big_out[None],)

    return ((loss, gx[None]) + leaves(g_s, g_w_in, g_w_out) + leaves(d_s, d_w_in, d_w_out)
            + leaves(nm_s, nm_w_in, nm_w_out) + leaves(nv_s, nv_w_in, nv_w_out))
```

```python
import numpy as np
import jax
import jax.numpy as jnp
from jax import lax
from jax.experimental import pallas as pl
from jax.experimental.pallas import tpu as pltpu

F32 = jnp.float32
BF16 = jnp.bfloat16

D_MODEL = 1024
HEAD_DIM = 64
HEADS = 8
WIDTH = HEADS * HEAD_DIM
A_KV_WIDTH = 2 * HEAD_DIM
BLOCK = 128
LANES = 128
FOLD = 16
A_MAX_DIST = 127
B_MAX_DIST = 128
ROPE_THETA = 10000.0
ROPE_SPLIT = 64
EPS = 1e-6
NEG = -1e30
SCALE = HEAD_DIM ** -0.5

IN_WIDTH = 3328
C_QA, C_KA, C_VA, C_GA, C_QB, C_KB, C_VB, C_GB, C_END = 0, 512, 640, 768, 1280, 1792, 2304, 2816, 3328

N_DEV = 8
N_CHIP = 4
MESH = pl.DeviceIdType.MESH
WIN = 896
WIN_START = (0, 768, 1664, 2432)
WIN_SHIFT = (0, 64, 0, 64)
OUT_ROWS = D_MODEL // N_CHIP
RELATIONS = (3, 1, 2, 0)

ADAM_LR = 0.001
ADAM_B1 = 0.9
ADAM_B2 = 0.999
ADAM_EPS = 1e-08
ADAM_WD = 0.01
ADAM_STEP = 10

ROW_TILE = 256
FOLD_ROWS = ROW_TILE // FOLD
GRAD_ROWS = 1024
ACC_COLS = 256
VMEM_LIMIT = 56 * 1024 * 1024


def _dot(a, b):
    return jnp.dot(a, b, preferred_element_type=F32)


def _dot_nt(a, b):
    return lax.dot_general(a, b, (((1,), (1,)), ((), ())), preferred_element_type=F32)


def _dot_tn(a, b):
    return lax.dot_general(a, b, (((0,), (0,)), ((), ())), preferred_element_type=F32)


def _head_sum(z, bd):
    w = bd.shape[0]
    zb = z.astype(BF16)
    parts = [_dot(zb[:, a:a + w], bd) for a in range(0, z.shape[1], w)]
    return parts[0] if len(parts) == 1 else jnp.concatenate(parts, axis=1)


def _swap_halves(t):
    w = t.shape[1]
    lane = lax.broadcasted_iota(jnp.int32, t.shape, 1)
    return jnp.where(lane % HEAD_DIM < HEAD_DIM // 2, pltpu.roll(t, w - 32, 1), pltpu.roll(t, 32, 1))


def _qknorm_rope(t, g, cos, sin_s, bd):
    r = lax.rsqrt(_head_sum(t * t, bd) * (1.0 / HEAD_DIM) + EPS)
    n = (t * r) * g
    return n * cos + _swap_halves(n) * sin_s


def _qknorm_rope_bwd(dout, t, g, cos, sin_s, bd):
    dout, t = dout.astype(F32), t.astype(F32)
    dn = dout * cos + _swap_halves(dout * sin_s)
    r = lax.rsqrt(_head_sum(t * t, bd) * (1.0 / HEAD_DIM) + EPS)
    tr = t * r
    u = dn * g
    dt = r * (u - tr * (_head_sum(u * tr, bd) * (1.0 / HEAD_DIM)))
    return dt, dn * tr


def _sigmoid(g):
    return 1.0 / (1.0 + jnp.exp(-g))


def _expand_heads(st):
    t = st.shape[0]
    lane = lax.broadcasted_iota(jnp.int32, (t, LANES), 1)
    chunks = []
    for c in range(WIDTH // LANES):
        chunks.append(jnp.where(lane < HEAD_DIM, st[:, 2 * c:2 * c + 1], st[:, 2 * c + 1:2 * c + 2]))
    return jnp.concatenate(chunks, axis=1)


def _reduce_heads(z):
    t = z.shape[0]
    lane = lax.broadcasted_iota(jnp.int32, (t, LANES), 1)
    out = jnp.zeros((t, LANES), F32)
    for c in range(WIDTH // LANES):
        zc = z[:, c * LANES:(c + 1) * LANES]
        for ph in range(2):
            s = jnp.sum(jnp.where((lane // HEAD_DIM) == ph, zc, 0.0), axis=-1, keepdims=True)
            out = jnp.where(lane == 2 * c + ph, s, out)
    return out


def _fold_scratch(w):
    return pltpu.VMEM((w // LANES, ROW_TILE, LANES), F32)


def _store_folded(out_ref, val, scr, col0=0):
    w = val.shape[1]
    n = w // LANES
    for c in range(n):
        scr[c] = val[:, c * LANES:(c + 1) * LANES]
    for r in range(FOLD):
        piece = [scr[c, pl.ds(r, FOLD_ROWS, stride=FOLD), :] for c in range(n)]
        out_ref[r, :, col0:col0 + w] = (piece[0] if n == 1 else jnp.concatenate(piece, axis=1)).astype(out_ref.dtype)


def _load_folded(in_ref, scr):
    n = in_ref.shape[2] // LANES
    for r in range(FOLD):
        blk = in_ref[r].astype(F32)
        for c in range(n):
            scr[c, pl.ds(r, FOLD_ROWS, stride=FOLD), :] = blk[:, c * LANES:(c + 1) * LANES]
    return scr[0] if n == 1 else jnp.concatenate([scr[c] for c in range(n)], axis=1)


def _store_folded_bf16(out_ref, val, perm):
    folded = _dot(perm, val.astype(BF16)).astype(out_ref.dtype)
    for r in range(FOLD):
        out_ref[r] = folded[r * FOLD_ROWS:(r + 1) * FOLD_ROWS]


def _load_folded_bf16(in_ref, perm):
    blk = jnp.concatenate([in_ref[r] for r in range(FOLD)], axis=0)
    return _dot(perm, blk)


def _rows(w, tm=ROW_TILE):
    return pl.BlockSpec((tm, w), lambda i: (i, 0))


def _folded_rows(w):
    return pl.BlockSpec((FOLD, FOLD_ROWS, w), lambda i: (0, i, 0))


def _whole(shape):
    return pl.BlockSpec(shape, lambda i: (0,) * len(shape))


def _inproj(x2, gain, w_bf, cos, sin_s, gqa, gka, gqb, gkb, bd256, bd128, w_out_blk):
    s = x2.shape[0]
    tm = ROW_TILE
    n_steps = s // tm
    n_rel = N_CHIP - 1
    o_half = OUT_ROWS // 2

    def body(x_ref, gain_ref, w_hbm, cos_ref, sin_ref, gqa_ref, gka_ref, gqb_ref, gkb_ref, bd256_ref, bd128_ref,
             wout_ref, qa_ref, kva_ref, qb_ref, kvb_ref, qbf_ref, kvbf_ref,
             qa_raw_ref, ka_raw_ref, ga_ref, qb_raw_ref, kb_raw_ref, gb_ref, wout_all_ref,
             w_vmem, scr, land, ici_send, ici_recv, d2d_send, d2d_recv):
        i = pl.program_id(0)
        px_, py_, c = _position()
        b = 2 * px_ + py_

        def piece(chip_idx, core):
            return land.at[chip_idx, pl.ds(pl.multiple_of(core * o_half, o_half), o_half)]

        def other_chip(d):
            ox, oy = px_ ^ (d >> 1), py_ ^ (d & 1)
            return ox, oy, 2 * ox + oy

        def ici_copy(d, chip_idx):
            ox, oy, _ = other_chip(d)
            return pltpu.make_async_remote_copy(
                src_ref=piece(chip_idx, c), dst_ref=piece(chip_idx, c), send_sem=ici_send.at[d - 1],
                recv_sem=ici_recv.at[d - 1], device_id=(ox, oy, c), device_id_type=MESH)

        def d2d_copy(d, core):
            return pltpu.make_async_remote_copy(
                src_ref=piece(other_chip(d)[2], core), dst_ref=piece(other_chip(d)[2], core),
                send_sem=d2d_send.at[d - 1], recv_sem=d2d_recv.at[d - 1], device_id=(px_, py_, 1 - c),
                device_id_type=MESH)

        @pl.when(i == 0)
        def _():
            pltpu.sync_copy(w_hbm, w_vmem)
            land[b] = wout_ref[...].astype(BF16)
            for d in range(1, N_CHIP):
                ici_copy(d, b).start()

        @pl.when(i == n_steps // 2)
        def _():
            for d in range(1, N_CHIP):
                ici_copy(d, other_chip(d)[2]).wait_recv()
                d2d_copy(d, c).start()

        @pl.when(i == n_steps - 1)
        def _():
            for d in range(1, N_CHIP):
                d2d_copy(d, 1 - c).wait_recv()
            for d in range(1, N_CHIP):
                ici_copy(d, b).wait_send()
                d2d_copy(d, c).wait_send()
            for k in range(N_CHIP):
                wout_all_ref[k * OUT_ROWS:(k + 1) * OUT_ROWS, :] = land[k]

        xt = x_ref[...]
        r = lax.rsqrt(jnp.mean(xt * xt, axis=-1, keepdims=True) + EPS)
        h = ((xt * r) * gain_ref[...]).astype(BF16)
        cos1 = cos_ref[...]
        sin1 = sin_ref[...]
        cos4 = jnp.tile(cos1, (1, 4))
        sin4 = jnp.tile(sin1, (1, 4))

        def seg(a, b):
            return _dot_nt(h, w_vmem[a:b, :])

        t = seg(C_QA, C_KA)
        qa_raw_ref[...] = t.astype(BF16)
        qa_ref[...] = (_qknorm_rope(t, gqa_ref[...], cos4, sin4, bd256_ref[...]) * SCALE).astype(BF16)
        t = seg(C_KA, C_VA)
        ka_raw_ref[...] = t.astype(BF16)
        kva_ref[:, :A_KV_WIDTH] = _qknorm_rope(t, gka_ref[...], cos1, sin1, bd128_ref[...]).astype(BF16)
        kva_ref[:, A_KV_WIDTH:] = seg(C_VA, C_GA).astype(BF16)
        ga_ref[...] = seg(C_GA, C_QB).astype(BF16)
        t = seg(C_QB, C_KB)
        qb_raw_ref[...] = t.astype(BF16)
        t = _qknorm_rope(t, gqb_ref[...], cos4, sin4, bd256_ref[...]) * SCALE
        qb_ref[...] = t.astype(BF16)
        _store_folded(qbf_ref, t, scr)
        t = seg(C_KB, C_VB)
        kb_raw_ref[...] = t.astype(BF16)
        t = _qknorm_rope(t, gkb_ref[...], cos4, sin4, bd256_ref[...])
        kvb_ref[:, :WIDTH] = t.astype(BF16)
        _store_folded(kvbf_ref, t, scr)
        t = seg(C_VB, C_GB)
        kvb_ref[:, WIDTH:] = t.astype(BF16)
        _store_folded(kvbf_ref, t, scr, WIDTH)
        gb_ref[...] = seg(C_GB, C_END).astype(BF16)

    sds = jax.ShapeDtypeStruct
    ln = s // FOLD
    out_shape = (sds((s, WIDTH), BF16), sds((s, 2 * A_KV_WIDTH), BF16), sds((s, WIDTH), BF16),
                 sds((s, 2 * WIDTH), BF16), sds((FOLD, ln, WIDTH), BF16), sds((FOLD, ln, 2 * WIDTH), BF16),
                 sds((s, WIDTH), BF16), sds((s, A_KV_WIDTH), BF16), sds((s, WIDTH), BF16),
                 sds((s, WIDTH), BF16), sds((s, WIDTH), BF16), sds((s, WIDTH), BF16),
                 sds((D_MODEL, D_MODEL), BF16))
    out_specs = (_rows(WIDTH), _rows(2 * A_KV_WIDTH), _rows(WIDTH), _rows(2 * WIDTH),
                 _folded_rows(WIDTH), _folded_rows(2 * WIDTH),
                 _rows(WIDTH), _rows(A_KV_WIDTH), _rows(WIDTH), _rows(WIDTH), _rows(WIDTH), _rows(WIDTH),
                 _whole((D_MODEL, D_MODEL)))
    dma = pltpu.SemaphoreType.DMA
    return pl.pallas_call(
        body, name="inproj_fwd", grid=(n_steps,),
        in_specs=[_rows(D_MODEL), _whole(gain.shape), pl.BlockSpec(memory_space=pl.ANY), _rows(LANES), _rows(LANES),
                  _whole(gqa.shape), _whole(gka.shape), _whole(gqb.shape), _whole(gkb.shape), _whole(bd256.shape),
                  _whole(bd128.shape), _whole(w_out_blk.shape)],
        out_specs=out_specs, out_shape=out_shape,
        scratch_shapes=[pltpu.VMEM((IN_WIDTH, D_MODEL), BF16), _fold_scratch(WIDTH),
                        pltpu.VMEM((N_CHIP, OUT_ROWS, D_MODEL), BF16),
                        dma((n_rel,)), dma((n_rel,)), dma((n_rel,)), dma((n_rel,))],
        compiler_params=pltpu.CompilerParams(dimension_semantics=("arbitrary",), vmem_limit_bytes=VMEM_LIMIT),
    )(x2, gain, w_bf, cos, sin_s, gqa, gka, gqb, gkb, bd256, bd128, w_out_blk)


def _seq_pos(idx, dil):
    if dil == 4:
        return 4 * (idx % 32) + idx // 32
    return idx


def _upper_mask(dil, r0=0, rows=2 * BLOCK):
    qi = (lax.broadcasted_iota(jnp.int32, (rows, BLOCK), 0) + r0) % BLOCK
    kj = lax.broadcasted_iota(jnp.int32, (rows, BLOCK), 1)
    return _seq_pos(kj, dil) > _seq_pos(qi, dil)


def _eye_mask(r0=0, rows=2 * BLOCK):
    qi = (lax.broadcasted_iota(jnp.int32, (rows, BLOCK), 0) + r0) % BLOCK
    kj = lax.broadcasted_iota(jnp.int32, (rows, BLOCK), 1)
    return qi == kj


def _stack_heads(a2, c, gqa):
    lane = lax.broadcasted_iota(jnp.int32, (1, LANES), 1) // HEAD_DIM
    zero = jnp.zeros_like(a2)
    if gqa:
        keep = lane == (c // 2)
        return jnp.concatenate([jnp.where(keep, a2, zero), jnp.where(keep, _swap_heads(a2), zero)], axis=0)
    return jnp.concatenate([jnp.where(lane == 0, a2, zero), jnp.where(lane == 1, a2, zero)], axis=0)


def _unstack_heads(a, c, gqa):
    lane = lax.broadcasted_iota(jnp.int32, (1, LANES), 1) // HEAD_DIM
    if gqa:
        return jnp.where(lane == (c // 2), a[:BLOCK], _swap_heads(a[BLOCK:]))
    return jnp.where(lane == 0, a[:BLOCK], a[BLOCK:])


def _stacked_head_ids(c, gqa):
    if gqa:
        return 2 * c + c // 2, 2 * c + 1 - c // 2
    return 2 * c, 2 * c + 1


def _per_head_rows(blk, heads):
    return jnp.concatenate([blk[:, heads[0]:heads[0] + 1], blk[:, heads[1]:heads[1] + 1]], axis=0)


def _attn_view(a, dil):
    if dil == 1:
        return a[None]
    if dil == 4:
        return a.reshape(4, 4, a.shape[1], a.shape[2])
    return a


def _attn_unview(a, dil):
    if dil == 1:
        return a[0]
    if dil == 4:
        return a.reshape(FOLD, a.shape[2], a.shape[3])
    return a


ATTN_BLOCKS_PER_STEP = 8


def _attn_specs(dil):
    if dil == 4:
        def spec(n, fn):
            return lambda w: pl.BlockSpec((4, None, n * BLOCK // 4, w), lambda r, i: (0, r, fn(i), 0))
    else:
        def spec(n, fn):
            return lambda w: pl.BlockSpec((None, n * BLOCK, w), lambda r, i: (r, fn(i), 0))
    return spec


def _blk_rows(g, dil):
    n = BLOCK // 4 if dil == 4 else BLOCK
    if isinstance(g, int):
        return slice(g * n, (g + 1) * n)
    return pl.ds(pl.multiple_of(g * n, n), n)


def _blk_load(ref, sl, dil, g=0):
    if dil == 4:
        return ref[:, _blk_rows(g, dil), sl].reshape(BLOCK, sl.stop - sl.start)
    return ref[_blk_rows(g, dil), sl]


def _blk_store(ref, sl, val, dil, g=0):
    val = val.astype(ref.dtype)
    if dil == 4:
        ref[:, _blk_rows(g, dil), sl] = val.reshape(4, BLOCK // 4, sl.stop - sl.start)
    else:
        ref[_blk_rows(g, dil), sl] = val


def _swap_heads(a):
    return pltpu.roll(a.astype(F32), HEAD_DIM, 1).astype(a.dtype)


STAT_SHIFT = 8


def _attn_fwd(q, kv, sinks, *, dil, max_dist, name):
    q, kv = _attn_view(q, dil), _attn_view(kv, dil)
    kw = kv.shape[-1] // 2
    gqa = kw == A_KV_WIDTH
    n_seq = dil
    nb = (q.shape[-2] * (4 if dil == 4 else 1)) // BLOCK
    per_step = min(ATTN_BLOCKS_PER_STEP, nb)
    with_sinks = sinks is not None
    all_lanes = slice(0, LANES)
    assert max_dist in (BLOCK - 1, BLOCK) and nb % per_step == 0 and (per_step == 1 or per_step % 2 == 0)
    diag = max_dist == BLOCK

    def body(*refs):
        if with_sinks:
            q_ref, kvp_ref, kvc_ref, sink_ref, o_ref, ml_ref = refs
        else:
            q_ref, kvp_ref, kvc_ref, o_ref, ml_ref = refs

        chunks = range(WIDTH // LANES)

        def matmuls_in(g, has_prev):
            prev_ref, prev_g = (kvp_ref, 0) if (isinstance(g, int) and g == 0) else (kvc_ref, g - 1)
            scores, values = [], []
            for c in chunks:
                sl = slice(c * LANES, (c + 1) * LANES)
                ksl = slice(0, LANES) if gqa else sl
                vsl = slice(ksl.start + kw, ksl.stop + kw)
                kcur, vcur = _blk_load(kvc_ref, ksl, dil, g), _blk_load(kvc_ref, vsl, dil, g)
                qs = _stack_heads(_blk_load(q_ref, sl, dil, g), c, gqa)
                if has_prev:
                    kcur = jnp.concatenate([_blk_load(prev_ref, ksl, dil, prev_g), kcur], axis=0)
                    vcur = jnp.concatenate([_blk_load(prev_ref, vsl, dil, prev_g), vcur], axis=0)
                scores.append(_dot_nt(qs, kcur))
                values.append(vcur)
            return scores, values

        def tile_ops(has_prev, scores):
            lane = lax.broadcasted_iota(jnp.int32, (1, LANES), 1)
            with_diag = diag and has_prev
            upper, eye = _upper_mask(dil), _eye_mask()
            first_rows = lax.broadcasted_iota(jnp.int32, (2 * BLOCK, 1), 0) < BLOCK
            ml_blk = jnp.zeros((BLOCK, LANES), F32)
            probs = []
            for c in chunks:
                heads = _stacked_head_ids(c, gqa)
                s = scores[c]
                if has_prev:
                    s_p = s[:, :BLOCK]
                    sc = jnp.where(upper, s_p, s[:, BLOCK:])
                else:
                    sc = jnp.where(upper, NEG, s)
                if with_diag:
                    sd = jnp.where(eye, s_p, NEG)
                    m = jnp.max(jnp.maximum(sc, sd), axis=-1, keepdims=True)
                else:
                    m = jnp.max(sc, axis=-1, keepdims=True)
                if with_sinks:
                    sk = jnp.where(first_rows, sink_ref[0, heads[0]], sink_ref[0, heads[1]])
                    m = jnp.maximum(m, sk)
                p = jnp.exp(sc - m)
                zero = jnp.zeros_like(p)
                if with_diag:
                    pd = jnp.exp(sd - m)
                    l = jnp.sum(p + pd, axis=-1, keepdims=True)
                else:
                    pd = zero
                    l = jnp.sum(p, axis=-1, keepdims=True)
                if with_sinks:
                    l = l + jnp.exp(sk - m)
                pf = jnp.where(upper, zero, p)
                if has_prev:
                    pf = jnp.concatenate([jnp.where(upper, p, pd), pf], axis=1)
                probs.append(pf.astype(BF16))
                for n, h in enumerate(heads):
                    rows = slice(n * BLOCK, (n + 1) * BLOCK)
                    ml_blk = jnp.where(lane == h, m[rows], ml_blk)
                    ml_blk = jnp.where(lane == h + STAT_SHIFT, l[rows], ml_blk)
            return probs, ml_blk

        def matmuls_out(g, values, probs, ml_blk):
            for c in chunks:
                sl = slice(c * LANES, (c + 1) * LANES)
                _blk_store(o_ref, sl, _unstack_heads(_dot(probs[c], values[c]), c, gqa), dil, g)
            _blk_store(ml_ref, all_lanes, ml_blk, dil, g)

        def run(blocks):
            ins = [matmuls_in(g, has_prev) for g, has_prev in blocks]
            mids = [tile_ops(has_prev, scores) for (_, has_prev), (scores, _) in zip(blocks, ins)]
            for (g, _), (_, values), (probs, ml_blk) in zip(blocks, ins, mids):
                matmuls_out(g, values, probs, ml_blk)

        second = [(1, True)] if per_step > 1 else []

        @pl.when(pl.program_id(1) == 0)
        def _():
            run([(0, False)] + second)

        @pl.when(pl.program_id(1) > 0)
        def _():
            run([(0, True)] + second)

        if per_step > 2:
            def rest(pair, carry):
                run([(2 * pair, True), (2 * pair + 1, True)])
                return carry

            lax.fori_loop(1, per_step // 2, rest, 0)

    spec = _attn_specs(dil)
    cur = spec(per_step, lambda i: i)
    prev = spec(1, lambda i: jnp.maximum(i * per_step - 1, 0))
    in_specs = [cur(WIDTH), prev(2 * kw), cur(2 * kw)]
    args = [q, kv, kv]
    if with_sinks:
        in_specs.append(pl.BlockSpec(memory_space=pltpu.SMEM))
        args.append(sinks)
    stats = jax.ShapeDtypeStruct(q.shape[:-1] + (LANES,), F32)
    o, ml = pl.pallas_call(
        body, name=name, grid=(n_seq, nb // per_step), in_specs=in_specs,
        out_specs=(cur(WIDTH), cur(LANES)),
        out_shape=(jax.ShapeDtypeStruct(q.shape, BF16), stats),
        compiler_params=pltpu.CompilerParams(dimension_semantics=("arbitrary", "arbitrary"),
                                             vmem_limit_bytes=VMEM_LIMIT),
    )(*args)
    return _attn_unview(o, dil), _attn_unview(ml, dil)


def _attn_bwd(q, kv, do, ld, *, dil, max_dist, name, onto=None):
    q, kv, do, ld = (_attn_view(a, dil) for a in (q, kv, do, ld))
    onto = () if onto is None else tuple(_attn_view(a, dil) for a in onto)
    kw = kv.shape[-1] // 2
    gqa = kw == A_KV_WIDTH
    n_seq = dil
    nb = (q.shape[-2] * (4 if dil == 4 else 1)) // BLOCK
    n_kc = kw // LANES
    per_step = min(ATTN_BLOCKS_PER_STEP, nb)
    all_lanes = slice(0, LANES)
    assert max_dist in (BLOCK - 1, BLOCK) and nb % per_step == 0 and (per_step == 1 or per_step % 2 == 0)
    diag = max_dist == BLOCK

    def body(q_ref, kvp_ref, kvc_ref, do_ref, ld_ref, *rest_refs):
        dq_ref, dkv_ref, ck_ref, cv_ref = rest_refs[len(onto):]
        i = pl.program_id(1)

        def store(ref, sl, val, blk):
            if onto:
                val = val + _blk_load(rest_refs[0 if ref is dq_ref else 1], sl, dil, blk).astype(F32)
            _blk_store(ref, sl, val, dil, blk)

        chunks = range(WIDTH // LANES)

        def matmuls_in(g, has_prev):
            prev_ref, prev_g = (kvp_ref, 0) if (isinstance(g, int) and g == 0) else (kvc_ref, g - 1)
            operands, products = [], []
            for c in chunks:
                sl = slice(c * LANES, (c + 1) * LANES)
                kc = 0 if gqa else c
                ksl = slice(kc * LANES, (kc + 1) * LANES)
                vsl = slice(ksl.start + kw, ksl.stop + kw)
                k2, v2 = _blk_load(kvc_ref, ksl, dil, g), _blk_load(kvc_ref, vsl, dil, g)
                if has_prev:
                    k2 = jnp.concatenate([_blk_load(prev_ref, ksl, dil, prev_g), k2], axis=0)
                    v2 = jnp.concatenate([_blk_load(prev_ref, vsl, dil, prev_g), v2], axis=0)
                qs = _stack_heads(_blk_load(q_ref, sl, dil, g), c, gqa)
                dos = _stack_heads(_blk_load(do_ref, sl, dil, g), c, gqa)
                operands.append((qs, dos, k2))
                products.append((_dot_nt(qs, k2), _dot_nt(dos, v2)))
            return operands, products

        def tile_ops(g, has_prev, products):
            upper, eye = _upper_mask(dil), _eye_mask()
            ld_blk = _blk_load(ld_ref, all_lanes, dil, g)
            weights = []
            for c in chunks:
                heads = _stacked_head_ids(c, gqa)
                lse2 = _per_head_rows(ld_blk, heads)
                dl2 = _per_head_rows(ld_blk, tuple(h + STAT_SHIFT for h in heads))
                s, dp = products[c]
                if has_prev:
                    s_p, dp_p = s[:, :BLOCK], dp[:, :BLOCK]
                    sc = jnp.where(upper, s_p, s[:, BLOCK:])
                    dpc = jnp.where(upper, dp_p, dp[:, BLOCK:])
                else:
                    sc = jnp.where(upper, NEG, s)
                    dpc = dp
                p = jnp.exp(sc - lse2)
                ds = p * (dpc - dl2)
                zero = jnp.zeros_like(p)
                pf = jnp.where(upper, zero, p)
                dsf = jnp.where(upper, zero, ds)
                if has_prev:
                    if diag:
                        pd = jnp.exp(jnp.where(eye, s_p, NEG) - lse2)
                        dsd = pd * (dp_p - dl2)
                    else:
                        pd = dsd = zero
                    pf = jnp.concatenate([jnp.where(upper, p, pd), pf], axis=1)
                    dsf = jnp.concatenate([jnp.where(upper, ds, dsd), dsf], axis=1)
                weights.append((pf.astype(BF16), dsf.astype(BF16)))
            return weights

        def matmuls_out(g, has_prev, operands, weights):
            seq_blk = i * per_step + g
            dk_acc = [None] * n_kc
            dv_acc = [None] * n_kc
            for c in chunks:
                sl = slice(c * LANES, (c + 1) * LANES)
                kc = 0 if gqa else c
                qs, dos, k2 = operands[c]
                pf, dsf = weights[c]
                store(dq_ref, sl, _unstack_heads(_dot(dsf, k2), c, gqa) * SCALE, g)
                dk2 = _dot_tn(dsf, qs)
                dv2 = _dot_tn(pf, dos)
                dk_acc[kc] = dk2 if dk_acc[kc] is None else dk_acc[kc] + dk2
                dv_acc[kc] = dv2 if dv_acc[kc] is None else dv_acc[kc] + dv2
            for kc in range(n_kc):
                sl = slice(kc * LANES, (kc + 1) * LANES)
                vsl = slice(sl.start + kw, sl.stop + kw)
                if has_prev:
                    store(dkv_ref, sl, ck_ref[:, sl] + dk_acc[kc][:BLOCK], seq_blk - 1)
                    store(dkv_ref, vsl, cv_ref[:, sl] + dv_acc[kc][:BLOCK], seq_blk - 1)
                    ck_ref[:, sl] = dk_acc[kc][BLOCK:]
                    cv_ref[:, sl] = dv_acc[kc][BLOCK:]
                else:
                    ck_ref[:, sl] = dk_acc[kc]
                    cv_ref[:, sl] = dv_acc[kc]

        def run(blocks):
            ins = [matmuls_in(g, has_prev) for g, has_prev in blocks]
            mids = [tile_ops(g, has_prev, products) for (g, has_prev), (_, products) in zip(blocks, ins)]
            for (g, has_prev), (operands, _), weights in zip(blocks, ins, mids):
                matmuls_out(g, has_prev, operands, weights)

        second = [(1, True)] if per_step > 1 else []

        @pl.when(i == 0)
        def _():
            run([(0, False)] + second)

        @pl.when(i > 0)
        def _():
            run([(0, True)] + second)

        if per_step > 2:
            def rest(pair, carry):
                run([(2 * pair, True), (2 * pair + 1, True)])
                return carry

            lax.fori_loop(1, per_step // 2, rest, 0)

        @pl.when(i == nb // per_step - 1)
        def _():
            for kc in range(n_kc):
                sl = slice(kc * LANES, (kc + 1) * LANES)
                store(dkv_ref, sl, ck_ref[:, sl], nb - 1)
                store(dkv_ref, slice(sl.start + kw, sl.stop + kw), cv_ref[:, sl], nb - 1)

    spec = _attn_specs(dil)
    cur = spec(per_step, lambda i: i)
    prev = spec(1, lambda i: jnp.maximum(i * per_step - 1, 0))
    if dil == 4:
        whole = pl.BlockSpec((4, None, kv.shape[2], 2 * kw), lambda r, i: (0, r, 0, 0))
    else:
        whole = pl.BlockSpec((None, kv.shape[1], 2 * kw), lambda r, i: (r, 0, 0))
    sds = jax.ShapeDtypeStruct
    dq, dkv = pl.pallas_call(
        body, name=name, grid=(n_seq, nb // per_step),
        in_specs=[cur(WIDTH), prev(2 * kw), cur(2 * kw), cur(WIDTH), cur(LANES)] + [cur(WIDTH), whole][:len(onto)],
        out_specs=(cur(WIDTH), whole),
        out_shape=(sds(q.shape, BF16), sds(kv.shape, BF16)),
        scratch_shapes=[pltpu.VMEM((BLOCK, kw), F32), pltpu.VMEM((BLOCK, kw), F32)],
        compiler_params=pltpu.CompilerParams(dimension_semantics=("arbitrary", "arbitrary"),
                                             vmem_limit_bytes=VMEM_LIMIT),
    )(q, kv, kv, do, ld, *onto)
    return _attn_unview(dq, dil), _attn_unview(dkv, dil)


def _outproj(att_a, att_b1, att_b4, att_b16, g_a, g_b, x2, tgt2, w_out_bf, sink_row, perm):
    s = x2.shape[0]
    tm = ROW_TILE

    def body(oa_ref, mla_ref, ob1_ref, ml1_ref, ob4_ref, ml4_ref, ob16_ref, ml16_ref,
             ga_ref, gb_ref, x_ref, t_ref, w_ref, sink_ref, perm_ref,
             dy_ref, doa_ref, dob_ref, dobf_ref, dga_ref, dgb_ref, lda_ref, ldb_ref, ldbf_ref,
             gw_ref, loss_ref, dsink_ref, scr_st):
        i = pl.program_id(0)
        perm = perm_ref[...]
        lane = lax.broadcasted_iota(jnp.int32, (tm, LANES), 1)
        used = lane < HEADS

        def split(ml):
            return jnp.where(used, ml, 0.0), jnp.where(used, pltpu.roll(ml, LANES - STAT_SHIFT, 1), 1.0)

        @pl.when(i == 0)
        def _():
            gw_ref[...] = jnp.zeros_like(gw_ref)
            loss_ref[...] = jnp.zeros_like(loss_ref)
            dsink_ref[...] = jnp.zeros_like(dsink_ref)

        ms, ls = zip(split(ml1_ref[...]), split(_load_folded(ml4_ref, scr_st)), split(_load_folded(ml16_ref, scr_st)))
        mx = jnp.maximum(jnp.maximum(ms[0], ms[1]), ms[2])
        scale = [jnp.exp(mp - mx) for mp in ms]
        den = (ls[0] * scale[0] + ls[1] * scale[1]) + ls[2] * scale[2]
        lse_b = jnp.where(used, mx + jnp.log(den), 0.0)
        inv_den = 1.0 / den
        o_b = _expand_heads(scale[0] * inv_den) * ob1_ref[...].astype(F32)
        o_b = o_b + _expand_heads(scale[1] * inv_den) * _load_folded_bf16(ob4_ref, perm)
        o_b = o_b + _expand_heads(scale[2] * inv_den) * _load_folded_bf16(ob16_ref, perm)
        m_a, l_a = split(mla_ref[...])
        lse_a = jnp.where(used, m_a + jnp.log(l_a), 0.0)
        o_a = _expand_heads(1.0 / l_a) * oa_ref[...].astype(F32)
        g_a = ga_ref[...].astype(F32)
        g_b = gb_ref[...].astype(F32)
        sg_a = _sigmoid(g_a)
        sg_b = _sigmoid(g_b)
        silu_a = g_a * sg_a
        silu_b = g_b * sg_b
        mixed = jnp.concatenate([o_a * silu_a, o_b * silu_b], axis=1).astype(BF16)
        w = w_ref[...]
        y = x_ref[...] + _dot(mixed, w)
        diff = y - t_ref[...]
        loss_ref[...] += (0.5 / D_MODEL) * jnp.sum(diff * diff)
        dy = diff * (1.0 / D_MODEL)
        dy_ref[...] = dy
        dyb = dy.astype(BF16)
        gw_ref[...] += _dot_tn(mixed, dyb)
        dmixed = _dot_nt(dyb, w)
        dm_a = dmixed[:, :WIDTH]
        dm_b = dmixed[:, WIDTH:]
        do_a = dm_a * silu_a
        do_b = dm_b * silu_b
        doa_ref[...] = do_a.astype(BF16)
        dob_ref[...] = do_b.astype(BF16)
        _store_folded_bf16(dobf_ref, do_b, perm)
        dga_ref[...] = (dm_a * o_a * (sg_a * (1.0 + g_a * (1.0 - sg_a)))).astype(BF16)
        dgb_ref[...] = (dm_b * o_b * (sg_b * (1.0 + g_b * (1.0 - sg_b)))).astype(BF16)
        dl_a = _reduce_heads(do_a * o_a)
        dl_b = _reduce_heads(do_b * o_b)
        lda_ref[...] = lse_a + pltpu.roll(dl_a, STAT_SHIFT, 1)
        ld_b = lse_b + pltpu.roll(dl_b, STAT_SHIFT, 1)
        ldb_ref[...] = ld_b
        _store_folded(ldbf_ref, ld_b, scr_st)
        dsink_ref[...] -= jnp.sum(jnp.exp(sink_ref[...] - lse_a) * dl_a, axis=0, keepdims=True)

    sds = jax.ShapeDtypeStruct
    ln = s // FOLD
    natural = [_rows(WIDTH), _rows(LANES)]
    folded = [_folded_rows(WIDTH), _folded_rows(LANES)]
    return pl.pallas_call(
        body, name="outproj_fwd_bwd", grid=(s // tm,),
        in_specs=natural + natural + folded + folded
                 + [_rows(WIDTH), _rows(WIDTH), _rows(D_MODEL), _rows(D_MODEL), _whole((D_MODEL, D_MODEL)),
                    _whole((1, LANES)), _whole(perm.shape)],
        out_specs=(_rows(D_MODEL), _rows(WIDTH), _rows(WIDTH), _folded_rows(WIDTH), _rows(WIDTH), _rows(WIDTH),
                   _rows(LANES), _rows(LANES), _folded_rows(LANES),
                   _whole((D_MODEL, D_MODEL)), _whole((1, LANES)), _whole((1, LANES))),
        out_shape=(sds((s, D_MODEL), F32), sds((s, WIDTH), BF16), sds((s, WIDTH), BF16),
                   sds((FOLD, ln, WIDTH), BF16), sds((s, WIDTH), BF16), sds((s, WIDTH), BF16),
                   sds((s, LANES), F32), sds((s, LANES), F32), sds((FOLD, ln, LANES), F32),
                   sds((D_MODEL, D_MODEL), F32), sds((1, LANES), F32), sds((1, LANES), F32)),
        scratch_shapes=[_fold_scratch(LANES)],
        compiler_params=pltpu.CompilerParams(dimension_semantics=("arbitrary",), vmem_limit_bytes=VMEM_LIMIT),
    )(*att_a, *att_b1, *att_b4, *att_b16, g_a, g_b, x2, tgt2, w_out_bf, sink_row, perm)


def _inproj_bwd(x2, dy, gain, w_bf, cos, sin_s, gqa, gka, gqb, gkb, bd256, bd128, perm,
                qa_raw, ka_raw, qb_raw, kb_raw, d_a, d_b1, d_bf, dg_a, dg_b):
    s = x2.shape[0]
    tm = ROW_TILE

    def body(x_ref, dy_ref, gain_ref, w_hbm, cos_ref, sin_ref, gqa_ref, gka_ref, gqb_ref, gkb_ref, bd256_ref,
             bd128_ref, perm_ref, qa_raw_ref, ka_raw_ref, qb_raw_ref, kb_raw_ref, dqa_ref, dkva_ref,
             dq1_ref, dkv1_ref, dqf_ref, dkvf_ref, dga_ref, dgb_ref,
             gx_ref, ht_ref, win_ref,
             dgain_ref, dgqa_ref, dgka_ref, dgqb_ref, dgkb_ref, w_vmem, dproj_ref):
        i = pl.program_id(0)
        perm = perm_ref[...]

        @pl.when(i == 0)
        def _():
            pltpu.sync_copy(w_hbm, w_vmem)
            dgain_ref[...] = jnp.zeros_like(dgain_ref)
            dgqa_ref[...] = jnp.zeros_like(dgqa_ref)
            dgka_ref[...] = jnp.zeros_like(dgka_ref)
            dgqb_ref[...] = jnp.zeros_like(dgqb_ref)
            dgkb_ref[...] = jnp.zeros_like(dgkb_ref)

        cos1 = cos_ref[...]
        sin1 = sin_ref[...]
        cos4 = jnp.tile(cos1, (1, 4))
        sin4 = jnp.tile(sin1, (1, 4))

        dt, dg = _qknorm_rope_bwd(dqa_ref[...], qa_raw_ref[...], gqa_ref[...], cos4, sin4, bd256_ref[...])
        dproj_ref[:, C_QA:C_KA] = dt.astype(BF16)
        dgqa_ref[...] += jnp.sum(dg, axis=0, keepdims=True)
        dt, dg = _qknorm_rope_bwd(dkva_ref[:, :A_KV_WIDTH], ka_raw_ref[...], gka_ref[...], cos1, sin1,
                                  bd128_ref[...])
        dproj_ref[:, C_KA:C_VA] = dt.astype(BF16)
        dgka_ref[...] += jnp.sum(dg, axis=0, keepdims=True)
        dproj_ref[:, C_VA:C_GA] = dkva_ref[:, A_KV_WIDTH:]
        dproj_ref[:, C_GA:C_QB] = dga_ref[...]
        dq = dq1_ref[...].astype(F32) + _load_folded_bf16(dqf_ref, perm)
        dt, dg = _qknorm_rope_bwd(dq, qb_raw_ref[...], gqb_ref[...], cos4, sin4, bd256_ref[...])
        dproj_ref[:, C_QB:C_KB] = dt.astype(BF16)
        dgqb_ref[...] += jnp.sum(dg, axis=0, keepdims=True)
        dkv = dkv1_ref[...].astype(F32) + _load_folded_bf16(dkvf_ref, perm)
        dt, dg = _qknorm_rope_bwd(dkv[:, :WIDTH], kb_raw_ref[...], gkb_ref[...], cos4, sin4, bd256_ref[...])
        dproj_ref[:, C_KB:C_VB] = dt.astype(BF16)
        dgkb_ref[...] += jnp.sum(dg, axis=0, keepdims=True)
        dproj_ref[:, C_VB:C_GB] = dkv[:, WIDTH:].astype(BF16)
        dproj_ref[:, C_GB:C_END] = dgb_ref[...]
        for k, start in enumerate(WIN_START):
            win_ref[k] = dproj_ref[:, start:start + WIN]

        xt = x_ref[...]
        gain_row = gain_ref[...]
        r = lax.rsqrt(jnp.mean(xt * xt, axis=-1, keepdims=True) + EPS)
        xr = xt * r
        ht_ref[...] = (xr * gain_row).T.astype(BF16)
        dh = _dot(dproj_ref[...], w_vmem[...])
        dgain_ref[...] += jnp.sum(dh * xr, axis=0, keepdims=True)
        u = dh * gain_row
        gx_ref[...] = dy_ref[...] + r * (u - xr * jnp.mean(u * xr, axis=-1, keepdims=True))

    def acc_row(w):
        return pl.BlockSpec((1, w), lambda i: (0, 0))

    sds = jax.ShapeDtypeStruct
    any_spec = pl.BlockSpec(memory_space=pl.ANY)
    win_spec = pl.BlockSpec((N_CHIP, tm, WIN), lambda i: (0, i, 0))
    return pl.pallas_call(
        body, name="inproj_bwd", grid=(s // tm,),
        in_specs=[_rows(D_MODEL), _rows(D_MODEL), _whole(gain.shape), any_spec, _rows(LANES), _rows(LANES),
                  _whole(gqa.shape), _whole(gka.shape), _whole(gqb.shape), _whole(gkb.shape), _whole(bd256.shape),
                  _whole(bd128.shape), _whole(perm.shape),
                  _rows(WIDTH), _rows(A_KV_WIDTH), _rows(WIDTH), _rows(WIDTH),
                  _rows(WIDTH), _rows(2 * A_KV_WIDTH), _rows(WIDTH), _rows(2 * WIDTH)]
                 + [_folded_rows(WIDTH), _folded_rows(2 * WIDTH), _rows(WIDTH), _rows(WIDTH)],
        out_specs=(_rows(D_MODEL), pl.BlockSpec((D_MODEL, tm), lambda i: (0, i)), win_spec, acc_row(D_MODEL), acc_row(WIDTH), acc_row(A_KV_WIDTH), acc_row(WIDTH), acc_row(WIDTH)),
        out_shape=(sds((s, D_MODEL), F32), sds((D_MODEL, s), BF16), sds((N_CHIP, s, WIN), BF16),
                   sds((1, D_MODEL), F32),
                   sds((1, WIDTH), F32), sds((1, A_KV_WIDTH), F32), sds((1, WIDTH), F32), sds((1, WIDTH), F32)),
        scratch_shapes=[pltpu.VMEM((IN_WIDTH, D_MODEL), BF16), pltpu.VMEM((tm, IN_WIDTH), BF16)],
        compiler_params=pltpu.CompilerParams(dimension_semantics=("arbitrary",), vmem_limit_bytes=VMEM_LIMIT),
    )(x2, dy, gain, w_bf, cos, sin_s, gqa, gka, gqb, gkb, bd256, bd128, perm, qa_raw, ka_raw, qb_raw, kb_raw,
      *d_a, *d_b1, *d_bf, dg_a, dg_b)


def _rope_angles(s):
    half = HEAD_DIM // 2
    f32 = np.float32
    inv = np.tile((f32(ROPE_THETA) ** (-np.arange(half, dtype=f32) / f32(half))).astype(f32), 4)
    sign = np.tile(np.concatenate([-np.ones((half,), f32), np.ones((half,), f32)]), 2)
    hi = (np.arange(s // ROPE_SPLIT) * ROPE_SPLIT).astype(f32)[:, None] * inv[None, :]
    lo = np.arange(ROPE_SPLIT).astype(f32)[:, None] * inv[None, :]
    return tuple(jnp.asarray(t.astype(f32)) for t in (np.cos(hi), np.sin(hi), np.cos(lo), np.sin(lo), sign[None, :]))


def _table_shapes(s):
    return (jax.ShapeDtypeStruct((s, LANES), F32), jax.ShapeDtypeStruct((s, LANES), F32),
            jax.ShapeDtypeStruct((2 * LANES, 2 * LANES), BF16), jax.ShapeDtypeStruct((A_KV_WIDTH, A_KV_WIDTH), BF16),
            jax.ShapeDtypeStruct((ROW_TILE, ROW_TILE), BF16)) + tuple(
                jax.ShapeDtypeStruct((1, w), F32) for w in GAIN_WIDTHS)


GAIN_WIDTHS = (WIDTH, A_KV_WIDTH, WIDTH, WIDTH, LANES)


def _tables(in_refs, out_refs):
    ch_ref, sh_ref, cl_ref, sl_ref, sign_ref = in_refs[:5]
    cos_ref, sin_ref, bd256_ref, bd128_ref, perm_ref = out_refs[:5]
    for g_ref, ref in zip(in_refs[5:], out_refs[5:]):
        g = g_ref[...]
        fill = g if g.shape[1] == HEAD_DIM else jnp.zeros((1, LANES - g.shape[1]), F32)
        ref[...] = jnp.concatenate([jnp.concatenate([g, fill], axis=1)] * (ref.shape[1] // LANES), axis=1)
    cl, sl, sign = cl_ref[...], sl_ref[...], sign_ref[...]

    def tile(a, carry):
        rows = pl.ds(pl.multiple_of(a * ROPE_SPLIT, ROPE_SPLIT), ROPE_SPLIT)
        ch, sh = ch_ref[pl.ds(a, 1), :], sh_ref[pl.ds(a, 1), :]
        cos_ref[rows, :] = ch * cl - sh * sl
        sin_ref[rows, :] = (sh * cl + ch * sl) * sign
        return carry

    lax.fori_loop(0, cos_ref.shape[0] // ROPE_SPLIT, tile, 0)
    for ref in (bd256_ref, bd128_ref):
        head = [lax.broadcasted_iota(jnp.int32, ref.shape, d) // HEAD_DIM for d in (0, 1)]
        ref[...] = jnp.where(head[0] == head[1], 1.0, 0.0).astype(BF16)
    f = lax.broadcasted_iota(jnp.int32, perm_ref.shape, 0)
    col = lax.broadcasted_iota(jnp.int32, perm_ref.shape, 1)
    perm_ref[...] = jnp.where(col == FOLD * (f % FOLD_ROWS) + f // FOLD_ROWS, 1.0, 0.0).astype(BF16)


def _local_step(x2, tgt2, norm_gain, w_in_bf, q_norm_a, k_norm_a, sinks_a, q_norm_b, k_norm_b, w_out_blk, tables):
    cos, sin_s, bd256, bd128, perm, gqa, gka, gqb, gkb, sink_row = tables

    (qa, kva, qb, kvb, qbf, kvbf, qa_raw, ka_raw, g_a, qb_raw, kb_raw, g_b, w_out_bf) = _inproj(
        x2, norm_gain, w_in_bf, cos, sin_s, gqa, gka, gqb, gkb, bd256, bd128, w_out_blk)

    att_a = _attn_fwd(qa, kva, sinks_a, dil=1, max_dist=A_MAX_DIST, name="attn_a_fwd")
    att_b1 = _attn_fwd(qb, kvb, None, dil=1, max_dist=B_MAX_DIST, name="attn_b1_fwd")
    att_b4 = _attn_fwd(qbf, kvbf, None, dil=4, max_dist=B_MAX_DIST, name="attn_b4_fwd")
    att_b16 = _attn_fwd(qbf, kvbf, None, dil=16, max_dist=B_MAX_DIST, name="attn_b16_fwd")

    (dy, do_a, do_b, do_bf, dg_a, dg_b, ld_a, ld_b, ld_bf, gw_out, loss_part, dsink) = _outproj(
        att_a, att_b1, att_b4, att_b16, g_a, g_b, x2, tgt2, w_out_bf, sink_row, perm)

    d_a = _attn_bwd(qa, kva, do_a, ld_a, dil=1, max_dist=A_MAX_DIST, name="attn_a_bwd")
    d_b1 = _attn_bwd(qb, kvb, do_b, ld_b, dil=1, max_dist=B_MAX_DIST, name="attn_b1_bwd")
    d_b4 = _attn_bwd(qbf, kvbf, do_bf, ld_bf, dil=4, max_dist=B_MAX_DIST, name="attn_b4_bwd")
    d_b16 = _attn_bwd(qbf, kvbf, do_bf, ld_bf, dil=16, max_dist=B_MAX_DIST, name="attn_b16_bwd", onto=d_b4)

    gx, h_t, wins, dgain, dgqa, dgka, dgqb, dgkb = _inproj_bwd(
        x2, dy, norm_gain, w_in_bf, cos, sin_s, gqa, gka, gqb, gkb, bd256, bd128, perm,
        qa_raw, ka_raw, qb_raw, kb_raw, d_a, d_b1, d_b16, dg_a, dg_b)
    return loss_part, gx, h_t, wins, gw_out, (dgain, dgqa, dgka, dsink, dgqb, dgkb)


def _position():
    return lax.axis_index("x"), lax.axis_index("y"), lax.axis_index("c")


GATHER_CHUNKS = 2


def _gather_weights(blocks, name, side_inputs, side_shapes, side_work):
    n = len(blocks)
    n_in, n_out = len(side_inputs), len(side_shapes)
    ch = GATHER_CHUNKS

    def body(*refs):
        src_refs, side_in = refs[:n], refs[n:n + n_in]
        dst_refs, side_out = refs[n + n_in:2 * n + n_in], refs[2 * n + n_in:2 * n + n_in + n_out]
        ici_send, ici_recv, hop_send, hop_recv, d2d_send, d2d_recv = refs[2 * n + n_in + n_out:]
        x, y, c = _position()
        b = 2 * x + y
        via = 2 - c
        out = 3 - via
        for k in range(n):
            dst_refs[k][b] = src_refs[k][...].astype(BF16)

        def rows(k, core, j):
            half = blocks[k].shape[0] // 2
            return pl.ds(pl.multiple_of(core * half + j * (half // ch), half // ch), half // ch)

        def chip(rel):
            return x ^ (rel >> 1), y ^ (rel & 1)

        def ici(k, j, slot, rel, send_sems, recv_sems, sem):
            px, py = chip(rel)
            piece = dst_refs[k].at[slot, rows(k, c, j)]
            return pltpu.make_async_remote_copy(src_ref=piece, dst_ref=piece, send_sem=send_sems.at[sem],
                                                recv_sem=recv_sems.at[sem], device_id=(px, py, c),
                                                device_id_type=MESH)

        def direct(k, j, slot, rel):
            return ici(k, j, slot, rel, ici_send, ici_recv, ((rel - 1) * ch + j) * n + k)

        def hop(k, j, slot, rel):
            return ici(k, j, slot, rel, hop_send, hop_recv, j * n + k)

        def d2d(k, j, rel, core):
            piece = dst_refs[k].at[b ^ rel, rows(k, core, j)]
            sem = ((rel - 1) * ch + j) * n + k
            return pltpu.make_async_remote_copy(src_ref=piece, dst_ref=piece, send_sem=d2d_send.at[sem],
                                                recv_sem=d2d_recv.at[sem], device_id=(x, y, 1 - c),
                                                device_id_type=MESH)

        pieces = [(k, j) for j in range(ch) for k in range(n)]
        for k, j in pieces:
            for rel in (1, 2):
                direct(k, j, b, rel).start()
        side_work(side_in, side_out)
        for k, j in pieces:
            direct(k, j, b ^ via, via).wait_recv()
            hop(k, j, b ^ via, out).start()
            d2d(k, j, via, c).start()
        for k, j in pieces:
            direct(k, j, b ^ out, out).wait_recv()
            d2d(k, j, out, c).start()
        for k, j in pieces:
            hop(k, j, b ^ 3, via).wait_recv()
            d2d(k, j, 3, c).start()
        for k, j in pieces:
            for rel in (1, 2, 3):
                d2d(k, j, rel, 1 - c).wait_recv()
        for k, j in pieces:
            for rel in (1, 2):
                direct(k, j, b, rel).wait_send()
            hop(k, j, b ^ via, out).wait_send()
            d2d(k, j, via, c).wait_send()
            d2d(k, j, out, c).wait_send()
            d2d(k, j, 3, c).wait_send()

    vmem_spec = pl.BlockSpec(memory_space=pltpu.VMEM)
    dma = pltpu.SemaphoreType.DMA
    out_shape = tuple(jax.ShapeDtypeStruct((N_CHIP,) + a.shape, BF16) for a in blocks) + tuple(side_shapes)
    outs = pl.pallas_call(
        body, name=name, in_specs=[vmem_spec] * (n + n_in), out_specs=tuple([vmem_spec] * (n + n_out)),
        out_shape=out_shape,
        scratch_shapes=[dma((2 * ch * n,)), dma((2 * ch * n,)), dma((ch * n,)), dma((ch * n,)),
                        dma((3 * ch * n,)), dma((3 * ch * n,))],
        compiler_params=pltpu.CompilerParams(vmem_limit_bytes=VMEM_LIMIT),
    )(*blocks, *side_inputs)
    return outs[:n], outs[n:]


def _grad_reduce(order, h_t, wins, gw_out, small):
    s = h_t.shape[1]
    tk = GRAD_ROWS
    n_i = s // tk
    half = D_MODEL // 2
    o_half = OUT_ROWS // 2
    n_rel = N_CHIP - 1

    def body(order_ref, ht_ref, win_ref, gwo_ref,
             dgain_ref, dgqa_ref, dgqb_ref, dgkb_ref, dgka_ref, dsink_ref, loss_ref,
             win_out, wout_out, small_out,
             acc, mine, s1, r1, s2, r2, so1, ro1, so2, ro2, pair_in, pair_o, small_land, small_ref,
             s1_send, s1_recv, s2_send, s2_recv, o1_send, o1_recv, o2_send, o2_recv,
             pair_send, pair_recv, small_send, small_recv):
        j = pl.program_id(0)
        i = pl.program_id(1)
        x, y, c = _position()
        me = 4 * x + 2 * y + c
        sibling = (x, y, 1 - c)
        my_rows = pl.ds(pl.multiple_of(c * half, half), half)
        sib_rows = pl.ds(pl.multiple_of((1 - c) * half, half), half)

        def chip_of(rel):
            return x ^ (rel >> 1), y ^ (rel & 1)

        def level1(k):
            return pltpu.make_async_remote_copy(src_ref=s1.at[k], dst_ref=r1.at[k], send_sem=s1_send.at[k],
                                                recv_sem=s1_recv.at[k], device_id=sibling, device_id_type=MESH)

        def level2(k):
            px, py = chip_of(RELATIONS[k])
            return pltpu.make_async_remote_copy(src_ref=s2.at[k], dst_ref=r2.at[k], send_sem=s2_send.at[k],
                                                recv_sem=s2_recv.at[k], device_id=(px, py, c), device_id_type=MESH)

        def out_level1(bk):
            return pltpu.make_async_remote_copy(src_ref=so1.at[bk], dst_ref=ro1.at[bk], send_sem=o1_send.at[bk],
                                                recv_sem=o1_recv.at[bk], device_id=sibling, device_id_type=MESH)

        def out_level2(k):
            px, py = chip_of(RELATIONS[k])
            return pltpu.make_async_remote_copy(src_ref=so2.at[k], dst_ref=ro2.at[k], send_sem=o2_send.at[k],
                                                recv_sem=o2_recv.at[k], device_id=(px, py, c), device_id_type=MESH)

        def small_copy(d):
            px, py, pc = x ^ (d >> 2), y ^ ((d >> 1) & 1), c ^ (d & 1)
            return pltpu.make_async_remote_copy(src_ref=small_ref, dst_ref=small_land.at[me],
                                                send_sem=small_send.at[d], recv_sem=small_recv.at[d],
                                                device_id=(px, py, pc), device_id_type=MESH)

        def pair_copy(k, buf):
            return pltpu.make_async_remote_copy(src_ref=buf.at[0], dst_ref=buf.at[1], send_sem=pair_send.at[k],
                                                recv_sem=pair_recv.at[k], device_id=sibling, device_id_type=MESH)

        def out_rows(bk, core):
            return pl.ds(pl.multiple_of(bk * OUT_ROWS + core * o_half, o_half), o_half)

        @pl.when((j == 0) & (i == 0))
        def _():
            small_ref[...] = jnp.zeros(small_ref.shape, F32)
            for row, col, ref in ((0, 0, dgain_ref), (1, 0, dgqa_ref), (1, WIDTH, dgqb_ref), (2, 0, dgkb_ref),
                                  (2, WIDTH, dgka_ref), (2, WIDTH + A_KV_WIDTH, dsink_ref), (3, 0, loss_ref)):
                small_ref[row:row + 1, col:col + ref.shape[1]] = ref[...]
            for d in range(1, N_DEV):
                small_copy(d).start()
            small_land[me] = small_ref[...]
            for bk in range(N_CHIP):
                so1[bk] = gwo_ref[out_rows(bk, 1 - c), :].astype(BF16)
                out_level1(bk).start()

        @pl.when((j == 0) & (i == 1))
        def _():
            b = 2 * x + y
            for bk in range(N_CHIP):
                out_level1(bk).wait_recv()
            for k in range(n_rel):
                px, py = chip_of(RELATIONS[k])
                bk = 2 * px + py
                so2[k] = (gwo_ref[out_rows(bk, c), :] + ro1[bk].astype(F32)).astype(BF16)
                out_level2(k).start()

        @pl.when(i == 0)
        def _():
            acc[...] = jnp.zeros_like(acc)

        for n0 in range(0, WIN, ACC_COLS):
            n1 = min(n0 + ACC_COLS, WIN)
            acc[:, n0:n1] += _dot(ht_ref[...], win_ref[:, n0:n1])

        for k in range(N_CHIP):
            @pl.when((j == k) & (i == n_i - 1))
            def _(k=k):
                s1[k] = acc[sib_rows, :].astype(BF16)
                level1(k).start()
                mine[...] = acc[my_rows, :]

            if k < n_rel:
                @pl.when((j == k + 1) & (i == 1))
                def _(k=k):
                    level1(k).wait_recv()
                    s2[k] = (mine[...] + r1[k].astype(F32)).astype(BF16)
                    level2(k).start()

        @pl.when((j == N_CHIP - 1) & (i == n_i - 1))
        def _():
            b = 2 * x + y
            level1(N_CHIP - 1).wait_recv()
            total = mine[...] + r1[N_CHIP - 1].astype(F32)
            for k in range(n_rel):
                level2(k).wait_recv()
                total = total + r2[k].astype(F32)
            total = total.T
            pair_in[0] = total
            pair_copy(0, pair_in).start()
            total_o = gwo_ref[out_rows(b, c), :] + ro1[b].astype(F32)
            for k in range(n_rel):
                out_level2(k).wait_recv()
                total_o = total_o + ro2[k].astype(F32)
            pair_o[0] = total_o
            pair_copy(1, pair_o).start()
            for core in range(2):
                @pl.when(c == core)
                def _(core=core):
                    win_out[:, core * half:(core + 1) * half] = total
            wout_out[c] = total_o
            for d in range(1, N_DEV):
                small_copy(d).wait_recv()
            small_out[...] = small_land[...]
            pair_copy(0, pair_in).wait_recv()
            for core in range(2):
                @pl.when(c == core)
                def _(core=core):
                    win_out[:, (1 - core) * half:(2 - core) * half] = pair_in[1]
            pair_copy(1, pair_o).wait_recv()
            wout_out[1 - c] = pair_o[1]
            for d in range(1, N_DEV):
                small_copy(d).wait_send()
            for k in range(N_CHIP):
                level1(k).wait_send()
                out_level1(k).wait_send()
            for k in range(n_rel):
                level2(k).wait_send()
                out_level2(k).wait_send()
            pair_copy(0, pair_in).wait_send()
            pair_copy(1, pair_o).wait_send()

    vmem = pl.BlockSpec(memory_space=pltpu.VMEM)
    dma = pltpu.SemaphoreType.DMA
    sds = jax.ShapeDtypeStruct
    grid_spec = pltpu.PrefetchScalarGridSpec(
        num_scalar_prefetch=1, grid=(N_CHIP, n_i),
        in_specs=[pl.BlockSpec((D_MODEL, tk), lambda j, i, order: (0, i)),
                  pl.BlockSpec((None, tk, WIN), lambda j, i, order: (order[j], i, 0)), vmem] + [vmem] * len(small),
        out_specs=(vmem, vmem, vmem),
        scratch_shapes=[
            pltpu.VMEM((D_MODEL, WIN), F32), pltpu.VMEM((half, WIN), F32),
            pltpu.VMEM((N_CHIP, half, WIN), BF16), pltpu.VMEM((N_CHIP, half, WIN), BF16),
            pltpu.VMEM((n_rel, half, WIN), BF16), pltpu.VMEM((n_rel, half, WIN), BF16),
            pltpu.VMEM((N_CHIP, o_half, D_MODEL), BF16), pltpu.VMEM((N_CHIP, o_half, D_MODEL), BF16),
            pltpu.VMEM((n_rel, o_half, D_MODEL), BF16), pltpu.VMEM((n_rel, o_half, D_MODEL), BF16),
            pltpu.VMEM((2, WIN, half), F32), pltpu.VMEM((2, o_half, D_MODEL), F32),
            pltpu.VMEM((N_DEV, PACK_ROWS, D_MODEL), F32), pltpu.VMEM((PACK_ROWS, D_MODEL), F32),
            dma((N_CHIP,)), dma((N_CHIP,)), dma((n_rel,)), dma((n_rel,)),
            dma((N_CHIP,)), dma((N_CHIP,)), dma((n_rel,)), dma((n_rel,)),
            dma((2,)), dma((2,)), dma((N_DEV,)), dma((N_DEV,))])
    return pl.pallas_call(
        body, name="grad_w_in_reduce", grid_spec=grid_spec,
        out_shape=(sds((WIN, D_MODEL), F32), sds((2, o_half, D_MODEL), F32), sds((N_DEV, PACK_ROWS, D_MODEL), F32)),
        compiler_params=pltpu.CompilerParams(dimension_semantics=("arbitrary", "arbitrary"),
                                             vmem_limit_bytes=VMEM_LIMIT),
    )(order, h_t, wins, gw_out, *small)


ADAM_STEPS = 4


def _adamw_math(w, g, m, v):
    m = ADAM_B1 * m + (1.0 - ADAM_B1) * g
    v = ADAM_B2 * v + (1.0 - ADAM_B2) * (g * g)
    m_hat = m / (1.0 - ADAM_B1 ** ADAM_STEP)
    v_hat = v / (1.0 - ADAM_B2 ** ADAM_STEP)
    delta = -ADAM_LR * (m_hat / (jnp.sqrt(v_hat) + ADAM_EPS) + ADAM_WD * w)
    return delta, m, v


def _adamw(w, g, m, v, name):
    r, c = w.shape

    def body(w_ref, g_ref, m_ref, v_ref, d_ref, nm_ref, nv_ref):
        delta, nm, nv = _adamw_math(w_ref[...], g_ref[...], m_ref[...], v_ref[...])
        d_ref[...] = delta
        nm_ref[...] = nm
        nv_ref[...] = nv

    rows = r // ADAM_STEPS
    assert rows * ADAM_STEPS == r and rows % 8 == 0
    spec = pl.BlockSpec((rows, c), lambda i: (i, 0))
    shape = jax.ShapeDtypeStruct((r, c), F32)
    return pl.pallas_call(
        body, name=name, grid=(ADAM_STEPS,), in_specs=[spec] * 4, out_specs=(spec,) * 3,
        out_shape=(shape,) * 3, compiler_params=pltpu.CompilerParams(vmem_limit_bytes=VMEM_LIMIT),
    )(w, g, m, v)


def _adamw_window(w, window, shift, m, v, name):
    r, c = w.shape
    rows = r // ADAM_STEPS
    assert rows * ADAM_STEPS == r and rows % 8 == 0

    def body(shift_ref, w_ref, win_hbm, m_ref, v_ref, g_ref, d_ref, nm_ref, nv_ref, g_vmem):
        start = pl.multiple_of(shift_ref[0] + pl.program_id(0) * rows, 8)
        pltpu.sync_copy(win_hbm.at[pl.ds(start, rows)], g_vmem)
        g = g_vmem[...]
        g_ref[...] = g
        delta, nm, nv = _adamw_math(w_ref[...], g, m_ref[...], v_ref[...])
        d_ref[...] = delta
        nm_ref[...] = nm
        nv_ref[...] = nv

    spec = pl.BlockSpec((rows, c), lambda i, shift_ref: (i, 0))
    shape = jax.ShapeDtypeStruct((r, c), F32)
    grid_spec = pltpu.PrefetchScalarGridSpec(
        num_scalar_prefetch=1, grid=(ADAM_STEPS,),
        in_specs=[spec, pl.BlockSpec(memory_space=pl.ANY), spec, spec], out_specs=(spec,) * 4,
        scratch_shapes=[pltpu.VMEM((rows, c), F32)])
    return pl.pallas_call(
        body, name=name, grid_spec=grid_spec, out_shape=(shape,) * 4,
        compiler_params=pltpu.CompilerParams(vmem_limit_bytes=VMEM_LIMIT),
    )(shift, w, window, m, v)


PACK_ROWS = 8


def _fold_heads(v):
    y = v[:, 0:LANES]
    for j in range(1, v.shape[1] // LANES):
        y = y + v[:, j * LANES:(j + 1) * LANES]
    return y + pltpu.roll(y, HEAD_DIM, 1)


N_SMALL = 6


def _small_adamw(recv, weights, m, v):
    def body(*refs):
        r_ref = refs[0]
        w_refs, m_refs, v_refs = (refs[1 + n * N_SMALL:1 + (n + 1) * N_SMALL] for n in range(3))
        outs = refs[1 + 3 * N_SMALL:]
        g_refs, d_refs, nm_refs, nv_refs = (outs[n * N_SMALL:(n + 1) * N_SMALL] for n in range(4))
        loss_ref = outs[4 * N_SMALL]
        tot = r_ref[0]
        for j in range(1, N_DEV):
            tot = tot + r_ref[j]
        loss_ref[...] = tot[3:4, 0:LANES]
        row1 = tot[1:2, :]
        row2 = tot[2:3, :]
        grads = [tot[0:1, :],
                 _fold_heads(row1[:, 0:WIDTH])[:, :HEAD_DIM],
                 _fold_heads(row2[:, WIDTH:WIDTH + A_KV_WIDTH])[:, :HEAD_DIM],
                 row2[:, WIDTH + A_KV_WIDTH:WIDTH + A_KV_WIDTH + HEADS],
                 _fold_heads(row1[:, WIDTH:2 * WIDTH])[:, :HEAD_DIM],
                 _fold_heads(row2[:, 0:WIDTH])[:, :HEAD_DIM]]
        for n, g in enumerate(grads):
            g_refs[n][...] = g
            delta, nm, nv = _adamw_math(w_refs[n][...], g, m_refs[n][...], v_refs[n][...])
            d_refs[n][...] = delta
            nm_refs[n][...] = nm
            nv_refs[n][...] = nv

    shapes = tuple(jax.ShapeDtypeStruct(a.shape, F32) for a in weights)
    outs = pl.pallas_call(body, name="small_adamw", out_shape=shapes * 4 + (jax.ShapeDtypeStruct((1, LANES), F32),)
                          )(recv, *weights, *m, *v)
    return tuple(outs[n * N_SMALL:(n + 1) * N_SMALL] for n in range(4)) + (outs[4 * N_SMALL],)


def kernel(x, norm_gain, w_in, q_norm_a, k_norm_a, sinks_a, q_norm_b, k_norm_b, w_out, loss_target, m_norm_gain, m_w_in, m_q_norm_a, m_k_norm_a, m_sinks_a, m_q_norm_b, m_k_norm_b, m_w_out, v_norm_gain, v_w_in, v_q_norm_a, v_k_norm_a, v_sinks_a, v_q_norm_b, v_k_norm_b, v_w_out):
    chip = 2 * lax.axis_index("x") + lax.axis_index("y")

    w_in_t, m_w_in_t, v_w_in_t = w_in[0].T, m_w_in[0].T, v_w_in[0].T

    s = x.shape[1]
    side_inputs = _rope_angles(s) + (q_norm_a, k_norm_a, q_norm_b, k_norm_b, sinks_a)
    (w_in_all,), tables = _gather_weights([w_in_t], "gather_weights", side_inputs, _table_shapes(s), _tables)
    w_in_bf = w_in_all.reshape(IN_WIDTH, D_MODEL)

    loss_part, gx, h_t, wins, gw_out, (dgain, dgqa, dgka, dsink, dgqb, dgkb) = _local_step(
        x[0], loss_target[0], norm_gain, w_in_bf, q_norm_a, k_norm_a, sinks_a, q_norm_b, k_norm_b, w_out[0], tables)

    small = (dgain, dgqa, dgqb, dgkb, dgka, dsink, loss_part)
    order = (chip ^ jnp.array(RELATIONS, jnp.int32)).astype(jnp.int32)
    win_sum, wout_sum, small_recv = _grad_reduce(order, h_t, wins, gw_out, small)
    shift = jnp.array(WIN_SHIFT, jnp.int32)[chip].reshape(1)
    g_w_out = wout_sum.reshape(OUT_ROWS, D_MODEL)

    g_w_in, d_w_in, nm_w_in, nv_w_in = (
        a.T for a in _adamw_window(w_in_t, win_sum, shift, m_w_in_t, v_w_in_t, "adamw_w_in"))
    d_w_out, nm_w_out, nv_w_out = _adamw(w_out[0], g_w_out, m_w_out[0], v_w_out[0], "adamw_w_out")
    g_s, d_s, nm_s, nv_s, loss_row = _small_adamw(
        small_recv,
        (norm_gain, q_norm_a, k_norm_a, sinks_a, q_norm_b, k_norm_b),
        (m_norm_gain, m_q_norm_a, m_k_norm_a, m_sinks_a, m_q_norm_b, m_k_norm_b),
        (v_norm_gain, v_q_norm_a, v_k_norm_a, v_sinks_a, v_q_norm_b, v_k_norm_b))
    loss = loss_row[0, 0]

    def leaves(small_ones, big_in, big_out):
        return (small_ones[0], big_in[None]) + tuple(small_ones[1:]) + (big_out[None],)

    return ((loss, gx[None]) + leaves(g_s, g_w_in, g_w_out) + leaves(d_s, d_w_in, d_w_out)
            + leaves(nm_s, nm_w_in, nm_w_out) + leaves(nv_s, nv_w_in, nv_w_out))
```
